```python
import jax, jax.numpy as jnp
from jax import lax
import numpy as np

D_MODEL = 1024
BATCH = 8
SEQ = 4096
DEPTH = 2

GRID_W = 64
CTX_LEN = 256
D_A = D_MODEL
SGU_GROUPS = 8
SGU_GROUP_DIM = D_A // SGU_GROUPS
SGU_CHUNK = 128
D_B = D_MODEL
HGRN_HEAD_DIM = 128
HGRN_HEADS = D_B // HGRN_HEAD_DIM
HGRN_CHUNK = 64
D_FF = 2816
CONV_W = 3
N_MOD = 6
RMS_EPS = 1e-6
LN_EPS = 1e-5
IN_SPLITS = (D_B, 2 * D_B, 3 * D_B, 4 * D_B, 4 * D_B + D_A, 4 * D_B + 2 * D_A, 5 * D_B + 2 * D_A, 5 * D_B + 2 * D_A + D_MODEL)
D_IN = 5 * D_B + 2 * D_A + 2 * D_MODEL

kernel_name = 'hybrid_sgu_hgrn2_convffn_prefix_dit'


def rms_norm(x, w):
    xf = x.astype(jnp.float32)
    y = xf * lax.rsqrt(jnp.mean(xf * xf, axis=-1, keepdims=True) + RMS_EPS)
    return (y * w.astype(jnp.float32)).astype(x.dtype)


def layer_norm(x, w, b):
    xf = x.astype(jnp.float32)
    mu = jnp.mean(xf, axis=-1, keepdims=True)
    var = jnp.mean(jnp.square(xf - mu), axis=-1, keepdims=True)
    y = (xf - mu) * lax.rsqrt(var + LN_EPS)
    return (y * w.astype(jnp.float32) + b.astype(jnp.float32)).astype(x.dtype)


def modulate(h, shift, scale):
    return h * (1 + scale) + shift


def to_heads(t):
    bsz, length, _ = t.shape
    return t.reshape(bsz, length, HGRN_HEADS, HGRN_HEAD_DIM).transpose(0, 2, 1, 3)


def hgrn_forget(f_logit, lb):
    z = f_logit.astype(jnp.float32)
    f = lb + (1 - lb) * jax.nn.sigmoid(z)
    return to_heads((1 - lb) * jax.nn.sigmoid(-z)), to_heads(jnp.log(f))


def gla_chunked(q, k, v, g, s0):
    bsz, heads, length, _ = q.shape
    n = length // HGRN_CHUNK
    split = lambda t: t.reshape(bsz, heads, n, HGRN_CHUNK, t.shape[-1])
    q, k, v, g = split(q), split(k), split(v), split(g)
    b = jnp.cumsum(g, axis=3)
    b_last = b[:, :, :, -1:, :]
    ref = b[:, :, :, HGRN_CHUNK // 2 - 1:HGRN_CHUNK // 2, :]
    scores = jnp.einsum('bhntk,bhnsk->bhnts', q * jnp.exp(b - ref), k * jnp.exp(ref - b))
    lower = jnp.tril(jnp.ones((HGRN_CHUNK, HGRN_CHUNK), dtype=bool))
    o_intra = jnp.einsum('bhnts,bhnsv->bhntv', jnp.where(lower, scores, 0.0), v)
    q_inter = q * jnp.exp(b)
    kv = jnp.einsum('bhnsk,bhnsv->bhnkv', k * jnp.exp(b_last - b), v)
    decay = jnp.exp(b_last[:, :, :, 0, :])

    def step(state, xs):
        q_n, kv_n, d_n = xs
        o_n = jnp.einsum('bhtk,bhkv->bhtv', q_n, state)
        return d_n[..., None] * state + kv_n, o_n

    move = lambda t: jnp.moveaxis(t, 2, 0)
    s_final, o_inter = lax.scan(step, s0, (move(q_inter), move(kv), move(decay)))
    o = o_intra + jnp.moveaxis(o_inter, 0, 2)
    return o.reshape(bsz, heads, length, -1), s_final


def hgrn_bidir(q, f_fwd, f_bwd, i, lb_fwd, lb_bwd, s0_fwd, s0_bwd):
    qh = to_heads(jax.nn.silu(q.astype(jnp.float32)))
    ih = to_heads(i.astype(jnp.float32))
    k_f, g_f = hgrn_forget(f_fwd, lb_fwd)
    k_b, g_b = hgrn_forget(f_bwd, lb_bwd)
    o_f, s_f = gla_chunked(qh, k_f, ih, g_f, s0_fwd)
    rev = lambda t: jnp.flip(t, axis=2)
    o_b, s_b = gla_chunked(rev(qh), rev(k_b), rev(ih), rev(g_b), s0_bwd)
    return o_f + rev(o_b), s_f, s_b


def hgrn_readout(o, og, norm_w):
    o = o * lax.rsqrt(jnp.mean(o * o, axis=-1, keepdims=True) + RMS_EPS) * norm_w.astype(jnp.float32)
    bsz, _, length, _ = o.shape
    o = o.transpose(0, 2, 1, 3).reshape(bsz, length, D_B).astype(og.dtype)
    return o * jax.nn.silu(og)


def sgu(u, v, ln_w, ln_b, w_s, b_s):
    bsz, length, _ = v.shape
    vn = layer_norm(v, ln_w, ln_b).reshape(bsz, length // SGU_CHUNK, SGU_CHUNK, SGU_GROUPS, SGU_GROUP_DIM)
    mixed = jnp.einsum('gts,bnsgd->bntgd', w_s, vn) + b_s.T[:, :, None]
    return u * mixed.reshape(bsz, length, D_A)


def token_mixer_out(parts, o_b, sgu_ln_w, sgu_ln_b, sgu_w, sgu_b, hgrn_norm_w, w_a, w_b, w_o):
    u, v, og, gate_a, gate_b = parts
    y_a = sgu(jax.nn.gelu(u), jax.nn.gelu(v), sgu_ln_w, sgu_ln_b, sgu_w, sgu_b)
    y_b = hgrn_readout(o_b, og, hgrn_norm_w)
    merged = jax.nn.sigmoid(gate_a) * (y_a @ w_a) + jax.nn.sigmoid(gate_b) * (y_b @ w_b)
    return merged @ w_o


def dwconv_grid(a, conv_w, conv_b):
    bsz, length, ch = a.shape
    rows = length // GRID_W
    y = lax.conv_general_dilated(a.reshape(bsz, rows, GRID_W, ch), conv_w[:, :, None, :].astype(a.dtype),
                                 window_strides=(1, 1), padding='SAME',
                                 dimension_numbers=('NHWC', 'HWIO', 'NHWC'), feature_group_count=ch)
    return y.reshape(bsz, length, ch) + conv_b


def dwconv_seq(a, conv_w, conv_b):
    y = lax.conv_general_dilated(a, conv_w[CONV_W // 2][:, None, :].astype(a.dtype),
                                 window_strides=(1,), padding='SAME',
                                 dimension_numbers=('NWC', 'WIO', 'NWC'), feature_group_count=a.shape[-1])
    return y + conv_b


def conv_ffn(h, w_up, conv_w, conv_b, w_down, on_grid):
    a, v = jnp.split(h @ w_up, 2, axis=-1)
    a = dwconv_grid(a, conv_w, conv_b) if on_grid else dwconv_seq(a, conv_w, conv_b)
    return (jax.nn.gelu(a) * v) @ w_down


def _fwd_setup_inputs(seed: int = 0) -> dict:
    key = jax.random.key(seed)
    ks = jax.random.split(key, 24)
    nrm = lambda k, shape, s: s * jax.random.normal(k, shape, jnp.float32)
    gain = lambda k, shape: 1.0 + nrm(k, shape, 0.02)
    return {
        'x': nrm(ks[0], (BATCH, SEQ, D_MODEL), 1.0),
        'c': nrm(ks[1], (BATCH, D_MODEL), 1.0),
        'ctx': nrm(ks[2], (BATCH, CTX_LEN, D_MODEL), 1.0),
        'c_ctx': nrm(ks[3], (D_MODEL,), 1.0),
        'ada_w': nrm(ks[4], (DEPTH, D_MODEL, N_MOD * D_MODEL), 0.5 * D_MODEL ** -0.5),
        'ada_b': nrm(ks[5], (DEPTH, N_MOD * D_MODEL), 0.02),
        'norm1_w': gain(ks[6], (DEPTH, D_MODEL)),
        'w_in': nrm(ks[7], (DEPTH, D_MODEL, D_IN), D_MODEL ** -0.5),
        'sgu_ln_w': gain(ks[8], (DEPTH, D_A)),
        'sgu_ln_b': nrm(ks[9], (DEPTH, D_A), 0.02),
        'sgu_w': nrm(ks[10], (DEPTH, SGU_GROUPS, SGU_CHUNK, SGU_CHUNK), SGU_CHUNK ** -0.5),
        'sgu_b': gain(ks[11], (DEPTH, SGU_GROUPS, SGU_CHUNK)),
        'hgrn_lower_bounds': nrm(ks[12], (DEPTH, 2 * D_B), 0.1),
        'hgrn_norm_w': gain(ks[13], (DEPTH, HGRN_HEAD_DIM)),
        'w_branch_a': nrm(ks[14], (DEPTH, D_A, D_MODEL), D_A ** -0.5),
        'w_branch_b': nrm(ks[15], (DEPTH, D_B, D_MODEL), D_B ** -0.5),
        'w_out': nrm(ks[16], (DEPTH, D_MODEL, D_MODEL), D_MODEL ** -0.5),
        'norm2_w': gain(ks[17], (DEPTH, D_MODEL)),
        'ffn_w_up': nrm(ks[18], (DEPTH, D_MODEL, 2 * D_FF), D_MODEL ** -0.5),
        'ffn_conv_w': nrm(ks[19], (DEPTH, CONV_W, CONV_W, D_FF), 1.0 / CONV_W),
        'ffn_conv_b': nrm(ks[20], (DEPTH, D_FF), 0.02),
        'ffn_w_down': nrm(ks[21], (DEPTH, D_FF, D_MODEL), D_FF ** -0.5),
        'final_norm_w': gain(ks[22], (D_MODEL,)),
    }


def _fwd_reference(x, c, ctx, c_ctx, ada_w, ada_b, norm1_w, w_in, sgu_ln_w, sgu_ln_b, sgu_w, sgu_b,
              hgrn_lower_bounds, hgrn_norm_w, w_branch_a, w_branch_b, w_out, norm2_w,
              ffn_w_up, ffn_conv_w, ffn_conv_b, ffn_w_down, final_norm_w):
    lb = jax.nn.softmax(hgrn_lower_bounds.astype(jnp.float32), axis=0)
    lb = jnp.cumsum(lb, axis=0) - lb[0]
    zero_state = jnp.zeros((ctx.shape[0], HGRN_HEADS, HGRN_HEAD_DIM, HGRN_HEAD_DIM), jnp.float32)
    for l in range(DEPTH):
        last = l == DEPTH - 1
        mod_x = (jax.nn.silu(c) @ ada_w[l] + ada_b[l])[:, None, :]
        mod_c = jax.nn.silu(c_ctx) @ ada_w[l] + ada_b[l]
        sh1, sc1, g1, sh2, sc2, g2 = jnp.split(mod_x, N_MOD, axis=-1)
        csh1, csc1, cg1, csh2, csc2, cg2 = jnp.split(mod_c, N_MOD, axis=-1)
        lb_f, lb_b = lb[l, :D_B], lb[l, D_B:]

        h_c = modulate(rms_norm(ctx, norm1_w[l]), csh1, csc1)
        n_cols = IN_SPLITS[3] if last else D_IN
        parts_c = jnp.split(h_c @ w_in[l, :, :n_cols], IN_SPLITS[:3] if last else IN_SPLITS, axis=-1)
        o_c, s_f, s_b = hgrn_bidir(*parts_c[:4], lb_f, lb_b, zero_state, zero_state)

        h_x = modulate(rms_norm(x, norm1_w[l]), sh1, sc1)
        parts_x = jnp.split(h_x @ w_in[l], IN_SPLITS, axis=-1)
        o_x, _, _ = hgrn_bidir(*parts_x[:4], lb_f, lb_b, s_f, s_b)
        x = x + g1 * token_mixer_out(parts_x[4:], o_x, sgu_ln_w[l], sgu_ln_b[l], sgu_w[l], sgu_b[l],
                                     hgrn_norm_w[l], w_branch_a[l], w_branch_b[l], w_out[l])
        h2 = modulate(rms_norm(x, norm2_w[l]), sh2, sc2)
        x = x + g2 * conv_ffn(h2, ffn_w_up[l], ffn_conv_w[l], ffn_conv_b[l], ffn_w_down[l], True)

        if not last:
            ctx = ctx + cg1 * token_mixer_out(parts_c[4:], o_c, sgu_ln_w[l], sgu_ln_b[l], sgu_w[l], sgu_b[l],
                                              hgrn_norm_w[l], w_branch_a[l], w_branch_b[l], w_out[l])
            hc2 = modulate(rms_norm(ctx, norm2_w[l]), csh2, csc2)
            ctx = ctx + cg2 * conv_ffn(hc2, ffn_w_up[l], ffn_conv_w[l], ffn_conv_b[l], ffn_w_down[l], False)
    return rms_norm(x, final_norm_w)


import jax as _jax
import jax.numpy as _jnp

TWIN_FORMAT = 'train_step'
FWD_PARAMS = ['x', 'c', 'ctx', 'c_ctx', 'ada_w', 'ada_b', 'norm1_w', 'w_in', 'sgu_ln_w', 'sgu_ln_b', 'sgu_w', 'sgu_b', 'hgrn_lower_bounds', 'hgrn_norm_w', 'w_branch_a', 'w_branch_b', 'w_out', 'norm2_w', 'ffn_w_up', 'ffn_conv_w', 'ffn_conv_b', 'ffn_w_down', 'final_norm_w']
TWIN_WEIGHTS = ['c_ctx', 'ada_w', 'ada_b', 'norm1_w', 'w_in', 'sgu_ln_w', 'sgu_ln_b', 'sgu_w', 'sgu_b', 'hgrn_lower_bounds', 'hgrn_norm_w', 'w_branch_a', 'w_branch_b', 'w_out', 'norm2_w', 'ffn_w_up', 'ffn_conv_w', 'ffn_conv_b', 'ffn_w_down', 'final_norm_w']
TWIN_DIFF_INPUT = 'x'
TWIN_INPUTS = ['x', 'c', 'ctx', 'c_ctx', 'ada_w', 'ada_b', 'norm1_w', 'w_in', 'sgu_ln_w', 'sgu_ln_b', 'sgu_w', 'sgu_b', 'hgrn_lower_bounds', 'hgrn_norm_w', 'w_branch_a', 'w_branch_b', 'w_out', 'norm2_w', 'ffn_w_up', 'ffn_conv_w', 'ffn_conv_b', 'ffn_w_down', 'final_norm_w', 'loss_target', 'm_c_ctx', 'm_ada_w', 'm_ada_b', 'm_norm1_w', 'm_w_in', 'm_sgu_ln_w', 'm_sgu_ln_b', 'm_sgu_w', 'm_sgu_b', 'm_hgrn_lower_bounds', 'm_hgrn_norm_w', 'm_w_branch_a', 'm_w_branch_b', 'm_w_out', 'm_norm2_w', 'm_ffn_w_up', 'm_ffn_conv_w', 'm_ffn_conv_b', 'm_ffn_w_down', 'm_final_norm_w', 'v_c_ctx', 'v_ada_w', 'v_ada_b', 'v_norm1_w', 'v_w_in', 'v_sgu_ln_w', 'v_sgu_ln_b', 'v_sgu_w', 'v_sgu_b', 'v_hgrn_lower_bounds', 'v_hgrn_norm_w', 'v_w_branch_a', 'v_w_branch_b', 'v_w_out', 'v_norm2_w', 'v_ffn_w_up', 'v_ffn_conv_w', 'v_ffn_conv_b', 'v_ffn_w_down', 'v_final_norm_w']
TWIN_OUTPUTS = ['loss', 'grad_x', 'grad_c_ctx', 'grad_ada_w', 'grad_ada_b', 'grad_norm1_w', 'grad_w_in', 'grad_sgu_ln_w', 'grad_sgu_ln_b', 'grad_sgu_w', 'grad_sgu_b', 'grad_hgrn_lower_bounds', 'grad_hgrn_norm_w', 'grad_w_branch_a', 'grad_w_branch_b', 'grad_w_out', 'grad_norm2_w', 'grad_ffn_w_up', 'grad_ffn_conv_w', 'grad_ffn_conv_b', 'grad_ffn_w_down', 'grad_final_norm_w', 'delta_c_ctx', 'delta_ada_w', 'delta_ada_b', 'delta_norm1_w', 'delta_w_in', 'delta_sgu_ln_w', 'delta_sgu_ln_b', 'delta_sgu_w', 'delta_sgu_b', 'delta_hgrn_lower_bounds', 'delta_hgrn_norm_w', 'delta_w_branch_a', 'delta_w_branch_b', 'delta_w_out', 'delta_norm2_w', 'delta_ffn_w_up', 'delta_ffn_conv_w', 'delta_ffn_conv_b', 'delta_ffn_w_down', 'delta_final_norm_w', 'new_m_c_ctx', 'new_m_ada_w', 'new_m_ada_b', 'new_m_norm1_w', 'new_m_w_in', 'new_m_sgu_ln_w', 'new_m_sgu_ln_b', 'new_m_sgu_w', 'new_m_sgu_b', 'new_m_hgrn_lower_bounds', 'new_m_hgrn_norm_w', 'new_m_w_branch_a', 'new_m_w_branch_b', 'new_m_w_out', 'new_m_norm2_w', 'new_m_ffn_w_up', 'new_m_ffn_conv_w', 'new_m_ffn_conv_b', 'new_m_ffn_w_down', 'new_m_final_norm_w', 'new_v_c_ctx', 'new_v_ada_w', 'new_v_ada_b', 'new_v_norm1_w', 'new_v_w_in', 'new_v_sgu_ln_w', 'new_v_sgu_ln_b', 'new_v_sgu_w', 'new_v_sgu_b', 'new_v_hgrn_lower_bounds', 'new_v_hgrn_norm_w', 'new_v_w_branch_a', 'new_v_w_branch_b', 'new_v_w_out', 'new_v_norm2_w', 'new_v_ffn_w_up', 'new_v_ffn_conv_w', 'new_v_ffn_conv_b', 'new_v_ffn_w_down', 'new_v_final_norm_w']
TWIN_LEAF_KINDS = {'loss': 'loss', 'grad_x': 'grad_x', 'grad_c_ctx': 'grad_w', 'grad_ada_w': 'grad_w', 'grad_ada_b': 'grad_w', 'grad_norm1_w': 'grad_w', 'grad_w_in': 'grad_w', 'grad_sgu_ln_w': 'grad_w', 'grad_sgu_ln_b': 'grad_w', 'grad_sgu_w': 'grad_w', 'grad_sgu_b': 'grad_w', 'grad_hgrn_lower_bounds': 'grad_w', 'grad_hgrn_norm_w': 'grad_w', 'grad_w_branch_a': 'grad_w', 'grad_w_branch_b': 'grad_w', 'grad_w_out': 'grad_w', 'grad_norm2_w': 'grad_w', 'grad_ffn_w_up': 'grad_w', 'grad_ffn_conv_w': 'grad_w', 'grad_ffn_conv_b': 'grad_w', 'grad_ffn_w_down': 'grad_w', 'grad_final_norm_w': 'grad_w', 'delta_c_ctx': 'delta_w', 'delta_ada_w': 'delta_w', 'delta_ada_b': 'delta_w', 'delta_norm1_w': 'delta_w', 'delta_w_in': 'delta_w', 'delta_sgu_ln_w': 'delta_w', 'delta_sgu_ln_b': 'delta_w', 'delta_sgu_w': 'delta_w', 'delta_sgu_b': 'delta_w', 'delta_hgrn_lower_bounds': 'delta_w', 'delta_hgrn_norm_w': 'delta_w', 'delta_w_branch_a': 'delta_w', 'delta_w_branch_b': 'delta_w', 'delta_w_out': 'delta_w', 'delta_norm2_w': 'delta_w', 'delta_ffn_w_up': 'delta_w', 'delta_ffn_conv_w': 'delta_w', 'delta_ffn_conv_b': 'delta_w', 'delta_ffn_w_down': 'delta_w', 'delta_final_norm_w': 'delta_w', 'new_m_c_ctx': 'new_m', 'new_m_ada_w': 'new_m', 'new_m_ada_b': 'new_m', 'new_m_norm1_w': 'new_m', 'new_m_w_in': 'new_m', 'new_m_sgu_ln_w': 'new_m', 'new_m_sgu_ln_b': 'new_m', 'new_m_sgu_w': 'new_m', 'new_m_sgu_b': 'new_m', 'new_m_hgrn_lower_bounds': 'new_m', 'new_m_hgrn_norm_w': 'new_m', 'new_m_w_branch_a': 'new_m', 'new_m_w_branch_b': 'new_m', 'new_m_w_out': 'new_m', 'new_m_norm2_w': 'new_m', 'new_m_ffn_w_up': 'new_m', 'new_m_ffn_conv_w': 'new_m', 'new_m_ffn_conv_b': 'new_m', 'new_m_ffn_w_down': 'new_m', 'new_m_final_norm_w': 'new_m', 'new_v_c_ctx': 'new_v', 'new_v_ada_w': 'new_v', 'new_v_ada_b': 'new_v', 'new_v_norm1_w': 'new_v', 'new_v_w_in': 'new_v', 'new_v_sgu_ln_w': 'new_v', 'new_v_sgu_ln_b': 'new_v', 'new_v_sgu_w': 'new_v', 'new_v_sgu_b': 'new_v', 'new_v_hgrn_lower_bounds': 'new_v', 'new_v_hgrn_norm_w': 'new_v', 'new_v_w_branch_a': 'new_v', 'new_v_w_branch_b': 'new_v', 'new_v_w_out': 'new_v', 'new_v_norm2_w': 'new_v', 'new_v_ffn_w_up': 'new_v', 'new_v_ffn_conv_w': 'new_v', 'new_v_ffn_conv_b': 'new_v', 'new_v_ffn_w_down': 'new_v', 'new_v_final_norm_w': 'new_v'}


def _forward(args):
    return _fwd_reference(*[args[k] for k in FWD_PARAMS])


def _output_shape():
    def fwd():
        inp = _fwd_setup_inputs(0)
        return _fwd_reference(*[inp[k] for k in FWD_PARAMS])
    out = _jax.eval_shape(fwd)
    return out.shape, out.dtype

N_MICROBATCH = 1
ADAM_LR = 0.001
ADAM_B1 = 0.9
ADAM_B2 = 0.999
ADAM_EPS = 1e-08
ADAM_WD = 0.01
ADAM_STEP = 10
PER_EXAMPLE_BATCH_AXIS = {'x': 0, 'c': 0, 'ctx': 0, 'loss_target': 0}
SHARED_INPUTS = []
_WEIGHT_DTYPES = {'c_ctx': _jnp.float32, 'ada_w': _jnp.float32, 'ada_b': _jnp.float32, 'norm1_w': _jnp.float32, 'w_in': _jnp.float32, 'sgu_ln_w': _jnp.float32, 'sgu_ln_b': _jnp.float32, 'sgu_w': _jnp.float32, 'sgu_b': _jnp.float32, 'hgrn_lower_bounds': _jnp.float32, 'hgrn_norm_w': _jnp.float32, 'w_branch_a': _jnp.float32, 'w_branch_b': _jnp.float32, 'w_out': _jnp.float32, 'norm2_w': _jnp.float32, 'ffn_w_up': _jnp.float32, 'ffn_conv_w': _jnp.float32, 'ffn_conv_b': _jnp.float32, 'ffn_w_down': _jnp.float32, 'final_norm_w': _jnp.float32}
MOMENT_SCALE = {'c_ctx': 7.875976e-04, 'ada_w': 5.424564e-02, 'ada_b': 9.080758e-02, 'norm1_w': 4.675171e-02, 'w_in': 1.672667e-02, 'sgu_ln_w': 2.103698e-02, 'sgu_ln_b': 2.064220e-02, 'sgu_w': 2.096786e-02, 'sgu_b': 2.204034e-02, 'hgrn_lower_bounds': 7.980737e-04, 'hgrn_norm_w': 5.691365e-02, 'w_branch_a': 2.965287e-02, 'w_branch_b': 1.940433e-02, 'w_out': 3.557474e-02, 'norm2_w': 5.981081e-02, 'ffn_w_up': 2.580702e-02, 'ffn_conv_w': 2.624837e-02, 'ffn_conv_b': 2.313453e-02, 'ffn_w_down': 4.211277e-02, 'final_norm_w': 3.201746e+01}


def _to_microbatches(a, axis):
    t = _jnp.moveaxis(a, axis, 0)
    t = t.reshape((N_MICROBATCH, t.shape[0] // N_MICROBATCH) + t.shape[1:])
    return _jnp.moveaxis(t, 1, axis + 1)


def setup_inputs(seed: int = 0) -> dict:
    inp = _fwd_setup_inputs(seed)
    key = _jax.random.fold_in(_jax.random.key(seed), 7919)
    shape, _ = _output_shape()
    out = dict(inp)
    out["loss_target"] = _jax.random.normal(_jax.random.fold_in(key, 0), shape, _jnp.float32)
    for i, name in enumerate(TWIN_WEIGHTS):
        w = inp[name].astype(_jnp.float32)
        if MOMENT_SCALE is None:
            s = _jnp.sqrt(_jnp.mean(_jnp.square(w)) + 1e-30)
        else:
            s = MOMENT_SCALE[name]
        km, kv = _jax.random.split(_jax.random.fold_in(key, i + 1))
        out[name] = w
        out["m_" + name] = s * _jax.random.normal(km, w.shape, _jnp.float32)
        out["v_" + name] = (s * s) * _jax.random.uniform(kv, w.shape, _jnp.float32, 0.5, 1.5)
    if N_MICROBATCH > 1:
        for name, axis in PER_EXAMPLE_BATCH_AXIS.items():
            out[name] = _to_microbatches(out[name], axis)
    return {'x': out['x'], 'c': out['c'], 'ctx': out['ctx'], 'c_ctx': out['c_ctx'], 'ada_w': out['ada_w'], 'ada_b': out['ada_b'], 'norm1_w': out['norm1_w'], 'w_in': out['w_in'], 'sgu_ln_w': out['sgu_ln_w'], 'sgu_ln_b': out['sgu_ln_b'], 'sgu_w': out['sgu_w'], 'sgu_b': out['sgu_b'], 'hgrn_lower_bounds': out['hgrn_lower_bounds'], 'hgrn_norm_w': out['hgrn_norm_w'], 'w_branch_a': out['w_branch_a'], 'w_branch_b': out['w_branch_b'], 'w_out': out['w_out'], 'norm2_w': out['norm2_w'], 'ffn_w_up': out['ffn_w_up'], 'ffn_conv_w': out['ffn_conv_w'], 'ffn_conv_b': out['ffn_conv_b'], 'ffn_w_down': out['ffn_w_down'], 'final_norm_w': out['final_norm_w'], 'loss_target': out['loss_target'], 'm_c_ctx': out['m_c_ctx'], 'm_ada_w': out['m_ada_w'], 'm_ada_b': out['m_ada_b'], 'm_norm1_w': out['m_norm1_w'], 'm_w_in': out['m_w_in'], 'm_sgu_ln_w': out['m_sgu_ln_w'], 'm_sgu_ln_b': out['m_sgu_ln_b'], 'm_sgu_w': out['m_sgu_w'], 'm_sgu_b': out['m_sgu_b'], 'm_hgrn_lower_bounds': out['m_hgrn_lower_bounds'], 'm_hgrn_norm_w': out['m_hgrn_norm_w'], 'm_w_branch_a': out['m_w_branch_a'], 'm_w_branch_b': out['m_w_branch_b'], 'm_w_out': out['m_w_out'], 'm_norm2_w': out['m_norm2_w'], 'm_ffn_w_up': out['m_ffn_w_up'], 'm_ffn_conv_w': out['m_ffn_conv_w'], 'm_ffn_conv_b': out['m_ffn_conv_b'], 'm_ffn_w_down': out['m_ffn_w_down'], 'm_final_norm_w': out['m_final_norm_w'], 'v_c_ctx': out['v_c_ctx'], 'v_ada_w': out['v_ada_w'], 'v_ada_b': out['v_ada_b'], 'v_norm1_w': out['v_norm1_w'], 'v_w_in': out['v_w_in'], 'v_sgu_ln_w': out['v_sgu_ln_w'], 'v_sgu_ln_b': out['v_sgu_ln_b'], 'v_sgu_w': out['v_sgu_w'], 'v_sgu_b': out['v_sgu_b'], 'v_hgrn_lower_bounds': out['v_hgrn_lower_bounds'], 'v_hgrn_norm_w': out['v_hgrn_norm_w'], 'v_w_branch_a': out['v_w_branch_a'], 'v_w_branch_b': out['v_w_branch_b'], 'v_w_out': out['v_w_out'], 'v_norm2_w': out['v_norm2_w'], 'v_ffn_w_up': out['v_ffn_w_up'], 'v_ffn_conv_w': out['v_ffn_conv_w'], 'v_ffn_conv_b': out['v_ffn_conv_b'], 'v_ffn_w_down': out['v_ffn_w_down'], 'v_final_norm_w': out['v_final_norm_w']}


def _loss(weights, diff, rest, loss_target):
    with _jax.named_scope("forward"):
        args = {**rest, TWIN_DIFF_INPUT: diff, **{k: w.astype(_WEIGHT_DTYPES[k]) for k, w in weights.items()}}
        y = _forward(args)
    with _jax.named_scope("loss_head"):
        err = _jnp.square(y.astype(_jnp.float32) - loss_target)
        return 0.5 * _jnp.sum(_jnp.mean(err, axis=-1)) if err.ndim else 0.5 * err


def _adamw(w, g, m, v):
    m = ADAM_B1 * m + (1.0 - ADAM_B1) * g
    v = ADAM_B2 * v + (1.0 - ADAM_B2) * _jnp.square(g)
    m_hat = m / (1.0 - ADAM_B1 ** ADAM_STEP)
    v_hat = v / (1.0 - ADAM_B2 ** ADAM_STEP)
    delta = -ADAM_LR * (m_hat / (_jnp.sqrt(v_hat) + ADAM_EPS) + ADAM_WD * w)
    return delta, m, v


def reference(x, c, ctx, c_ctx, ada_w, ada_b, norm1_w, w_in, sgu_ln_w, sgu_ln_b, sgu_w, sgu_b, hgrn_lower_bounds, hgrn_norm_w, w_branch_a, w_branch_b, w_out, norm2_w, ffn_w_up, ffn_conv_w, ffn_conv_b, ffn_w_down, final_norm_w, loss_target, m_c_ctx, m_ada_w, m_ada_b, m_norm1_w, m_w_in, m_sgu_ln_w, m_sgu_ln_b, m_sgu_w, m_sgu_b, m_hgrn_lower_bounds, m_hgrn_norm_w, m_w_branch_a, m_w_branch_b, m_w_out, m_norm2_w, m_ffn_w_up, m_ffn_conv_w, m_ffn_conv_b, m_ffn_w_down, m_final_norm_w, v_c_ctx, v_ada_w, v_ada_b, v_norm1_w, v_w_in, v_sgu_ln_w, v_sgu_ln_b, v_sgu_w, v_sgu_b, v_hgrn_lower_bounds, v_hgrn_norm_w, v_w_branch_a, v_w_branch_b, v_w_out, v_norm2_w, v_ffn_w_up, v_ffn_conv_w, v_ffn_conv_b, v_ffn_w_down, v_final_norm_w):
    given = dict(x=x, c=c, ctx=ctx, c_ctx=c_ctx, ada_w=ada_w, ada_b=ada_b, norm1_w=norm1_w, w_in=w_in, sgu_ln_w=sgu_ln_w, sgu_ln_b=sgu_ln_b, sgu_w=sgu_w, sgu_b=sgu_b, hgrn_lower_bounds=hgrn_lower_bounds, hgrn_norm_w=hgrn_norm_w, w_branch_a=w_branch_a, w_branch_b=w_branch_b, w_out=w_out, norm2_w=norm2_w, ffn_w_up=ffn_w_up, ffn_conv_w=ffn_conv_w, ffn_conv_b=ffn_conv_b, ffn_w_down=ffn_w_down, final_norm_w=final_norm_w, loss_target=loss_target, m_c_ctx=m_c_ctx, m_ada_w=m_ada_w, m_ada_b=m_ada_b, m_norm1_w=m_norm1_w, m_w_in=m_w_in, m_sgu_ln_w=m_sgu_ln_w, m_sgu_ln_b=m_sgu_ln_b, m_sgu_w=m_sgu_w, m_sgu_b=m_sgu_b, m_hgrn_lower_bounds=m_hgrn_lower_bounds, m_hgrn_norm_w=m_hgrn_norm_w, m_w_branch_a=m_w_branch_a, m_w_branch_b=m_w_branch_b, m_w_out=m_w_out, m_norm2_w=m_norm2_w, m_ffn_w_up=m_ffn_w_up, m_ffn_conv_w=m_ffn_conv_w, m_ffn_conv_b=m_ffn_conv_b, m_ffn_w_down=m_ffn_w_down, m_final_norm_w=m_final_norm_w, v_c_ctx=v_c_ctx, v_ada_w=v_ada_w, v_ada_b=v_ada_b, v_norm1_w=v_norm1_w, v_w_in=v_w_in, v_sgu_ln_w=v_sgu_ln_w, v_sgu_ln_b=v_sgu_ln_b, v_sgu_w=v_sgu_w, v_sgu_b=v_sgu_b, v_hgrn_lower_bounds=v_hgrn_lower_bounds, v_hgrn_norm_w=v_hgrn_norm_w, v_w_branch_a=v_w_branch_a, v_w_branch_b=v_w_branch_b, v_w_out=v_w_out, v_norm2_w=v_norm2_w, v_ffn_w_up=v_ffn_w_up, v_ffn_conv_w=v_ffn_conv_w, v_ffn_conv_b=v_ffn_conv_b, v_ffn_w_down=v_ffn_w_down, v_final_norm_w=v_final_norm_w)
    weights = {n: given[n] for n in TWIN_WEIGHTS}
    shared = {n: given[n] for n in SHARED_INPUTS}
    per_example = {n: given[n] for n in ['x', 'c', 'ctx']}
    grad_fn = _jax.value_and_grad(_loss, argnums=(0, 1))

    def one_microbatch(ex, loss_target):
        ex = dict(ex)
        diff = ex.pop(TWIN_DIFF_INPUT)
        return grad_fn(weights, diff, {**shared, **ex}, loss_target)

    if N_MICROBATCH == 1:
        loss, (grad_w, grad_x) = one_microbatch(per_example, given["loss_target"])
    else:
        def body(carry, xs):
            loss_sum, grad_sum = carry
            l_k, (gw_k, gx_k) = one_microbatch(xs[0], xs[1])
            with _jax.named_scope("update"):
                return (loss_sum + l_k, _jax.tree.map(_jnp.add, grad_sum, gw_k)), gx_k

        init = (_jnp.zeros((), _jnp.float32), _jax.tree.map(_jnp.zeros_like, weights))
        (loss, grad_w), grad_x = _jax.lax.scan(body, init, (per_example, given["loss_target"]))
    with _jax.named_scope("update"):
        delta_w, new_m, new_v = {}, {}, {}
        for n in TWIN_WEIGHTS:
            delta_w[n], new_m[n], new_v[n] = _adamw(weights[n], grad_w[n], given["m_" + n], given["v_" + n])
    return (loss, grad_x, *[grad_w[n] for n in TWIN_WEIGHTS], *[delta_w[n] for n in TWIN_WEIGHTS],
            *[new_m[n] for n in TWIN_WEIGHTS], *[new_v[n] for n in TWIN_WEIGHTS])
```

```python
import functools
import math

import jax
import jax.numpy as jnp
from jax import lax
from jax.experimental import pallas as pl
from jax.experimental.pallas import tpu as pltpu

F32 = jnp.float32
BF16 = jnp.bfloat16
HIGHEST = lax.Precision.HIGHEST

N_DEV = 8
AXES = ("x", "y", "c")
D = 1024
CTX = 256
TM = 256
TM_SMALL = 128
CH = 64
SGU_CH = 128
HEADS = 8
HD = 128
GRID_W = 64
D_IN = 9 * D
IN_SLOT = D_IN // N_DEV
D_FF = 2816
FF_SLOT = 2 * D_FF // N_DEV
N_FFK = D_FF // FF_SLOT
RMS_EPS = 1e-6
LN_EPS = 1e-5
ADAM_LR, ADAM_B1, ADAM_B2, ADAM_EPS, ADAM_WD, ADAM_STEP = 0.001, 0.9, 0.999, 1e-08, 0.01, 10
VMEM_LIMIT_V7X = 56 * 2 ** 20

VMEM_WHOLE = pl.BlockSpec(memory_space=pltpu.VMEM)
ANY = pl.BlockSpec(memory_space=pl.ANY)


def _cp(n_axes):
    return pltpu.CompilerParams(dimension_semantics=("arbitrary",) * n_axes, vmem_limit_bytes=VMEM_LIMIT_V7X)


def _dot(a, b, dims):
    return lax.dot_general(a.astype(BF16), b.astype(BF16), (dims, ((), ())), preferred_element_type=F32)


@jax.custom_vjp
def mm(a, b):
    return _dot(a, b, ((1,), (0,)))


mm.defvjp(lambda a, b: (mm(a, b), (a, b)),
          lambda r, g: (_dot(g, r[1], ((1,), (1,))).astype(r[0].dtype), _dot(r[0], g, ((0,), (0,))).astype(r[1].dtype)))


@jax.custom_vjp
def mm_nt(a, b):
    return _dot(a, b, ((1,), (1,)))


mm_nt.defvjp(lambda a, b: (mm_nt(a, b), (a, b)),
             lambda r, g: (_dot(g, r[1], ((1,), (0,))).astype(r[0].dtype), _dot(g, r[0], ((0,), (0,))).astype(r[1].dtype)))


@jax.custom_vjp
def mm_tn(a, b):
    return _dot(a, b, ((0,), (0,)))


mm_tn.defvjp(lambda a, b: (mm_tn(a, b), (a, b)),
             lambda r, g: (_dot(r[1], g, ((1,), (1,))).astype(r[0].dtype), _dot(r[0], g, ((1,), (0,))).astype(r[1].dtype)))


@jax.custom_vjp
def _cum(m, mt, g):
    return jnp.dot(m, g, precision=HIGHEST, preferred_element_type=F32)


_cum.defvjp(lambda m, mt, g: (_cum(m, mt, g), (m, mt)),
            lambda r, d: (jnp.zeros_like(r[0]), jnp.zeros_like(r[1]),
                          jnp.dot(r[1], d, precision=HIGHEST, preferred_element_type=F32)))


def _silu(x):
    return x * jax.nn.sigmoid(x)


def _gelu(x):
    return 0.5 * x * (1.0 + jnp.tanh(math.sqrt(2.0 / math.pi) * (x + 0.044715 * (x * x * x))))


def _rms(x, w):
    return x * lax.rsqrt(jnp.mean(x * x, axis=-1, keepdims=True) + RMS_EPS) * w


def _norm_mod(x, w, shift, scale):
    return _rms(x, w) * (1.0 + scale) + shift


def _hsl(h):
    return slice(h * HD, (h + 1) * HD)


def _hgrn_chunk(st, qz, fz, iv, lb, m, mt, mref):
    outs, news = [], []
    for h in range(HEADS):
        q = _silu(qz[h])
        keep = 1.0 - lb[h]
        f = lb[h] + keep * jax.nn.sigmoid(fz[h])
        k = keep * jax.nn.sigmoid(-fz[h])
        g = jnp.log(f)
        b = _cum(m, mt, g)
        ref = jnp.sum(mref * g, axis=0, keepdims=True)
        last = jnp.sum(g, axis=0, keepdims=True)
        scores = jnp.where(m > 0.5, mm_nt(q * jnp.exp(b - ref), k * jnp.exp(ref - b)), 0.0)
        outs.append(mm(scores, iv[h]) + mm_nt(q * jnp.exp(b), st[h]))
        news.append(jnp.exp(last) * st[h] + mm_tn(iv[h], k * jnp.exp(last - b)))
    return outs, news


def _sgu_fn(ub, vb, lnw, lnb, sw, sb):
    gv = [_gelu(v) for v in vb]
    mu = sum(jnp.sum(t, axis=-1, keepdims=True) for t in gv) / D
    var = sum(jnp.sum((t - mu) * (t - mu), axis=-1, keepdims=True) for t in gv) / D
    inv = lax.rsqrt(var + LN_EPS)
    cols = []
    for g in range(HEADS):
        vn = (gv[g] - mu) * inv * lnw[g] + lnb[g]
        cols.append(_gelu(ub[g]) * (mm(sw[g], vn) + sb[g]))
    return jnp.concatenate(cols, axis=1)


def _readout_fn(ob, og, hnw):
    r = [o * lax.rsqrt(jnp.mean(o * o, axis=-1, keepdims=True) + RMS_EPS) * hnw for o in ob]
    return jnp.concatenate(r, axis=1) * _silu(og)


def _glu_fn(ac, v):
    return _gelu(ac) * v


def _stream_row(tm):
    n_ctx = CTX // tm
    return lambda i: (jnp.where(i < n_ctx, 0, 1), 0, 0, 0)


def in_proj_fwd(x, mod, nw, wg):
    t = x.shape[0]

    def body(x_ref, mod_ref, nw_ref, w_ref, out_ref):
        h = _norm_mod(x_ref[...], nw_ref[...], mod_ref[0, 0], mod_ref[0, 1]).astype(BF16)
        for j in range(N_DEV):
            out_ref[:, j * IN_SLOT:(j + 1) * IN_SLOT] = jnp.dot(h, w_ref[j], preferred_element_type=F32)

    return pl.pallas_call(
        body, name="in_proj_fwd", grid=(t // TM,),
        in_specs=[pl.BlockSpec((TM, D), lambda i: (i, 0)), pl.BlockSpec((1, 6, 1, D), _stream_row(TM)),
                  pl.BlockSpec((1, D), lambda i: (0, 0)), VMEM_WHOLE],
        out_specs=pl.BlockSpec((TM, D_IN), lambda i: (i, 0)),
        out_shape=jax.ShapeDtypeStruct((t, D_IN), F32), compiler_params=_cp(1))(x, mod, nw, wg)


def _scan_chunk(nc):
    ncc = CTX // CH

    def chunk(d, s):
        bwd = jnp.where(s < ncc, ncc - 1 - s, nc + ncc - 1 - s)
        return jnp.where(d == 0, s, bwd)
    return chunk


def hgrn_fwd(parts, lb, mc, mtc, mrefc):
    t = parts.shape[0]
    nc = t // CH
    chunk = _scan_chunk(nc)

    def body(q_ref, f_ref, i_ref, lb_ref, m_ref, mt_ref, mr_ref, o_ref, ck_ref, st):
        @pl.when(pl.program_id(1) == 0)
        def _():
            st[...] = jnp.zeros_like(st)
        ck_ref[0, 0] = st[...]
        outs, news = _hgrn_chunk([st[h] for h in range(HEADS)], [q_ref[:, _hsl(h)] for h in range(HEADS)],
                                 [f_ref[:, _hsl(h)] for h in range(HEADS)], [i_ref[:, _hsl(h)] for h in range(HEADS)],
                                 [lb_ref[0, :, _hsl(h)] for h in range(HEADS)], m_ref[0], mt_ref[0], mr_ref[0])
        for h in range(HEADS):
            o_ref[0, :, _hsl(h)] = outs[h]
            st[h] = news[h]

    const = lambda d, s: (d, 0, 0)
    return pl.pallas_call(
        body, name="hgrn_fwd", grid=(2, nc),
        in_specs=[pl.BlockSpec((CH, D), lambda d, s: (chunk(d, s), 0)), pl.BlockSpec((CH, D), lambda d, s: (chunk(d, s), 1 + d)),
                  pl.BlockSpec((CH, D), lambda d, s: (chunk(d, s), 3)), pl.BlockSpec((1, 1, D), const),
                  pl.BlockSpec((1, CH, CH), const), pl.BlockSpec((1, CH, CH), const), pl.BlockSpec((1, CH, 1), const)],
        out_specs=[pl.BlockSpec((1, CH, D), lambda d, s: (d, chunk(d, s), 0)),
                   pl.BlockSpec((1, 1, HEADS, HD, HD), lambda d, s: (d, s, 0, 0, 0))],
        out_shape=[jax.ShapeDtypeStruct((2, t, D), F32), jax.ShapeDtypeStruct((2, nc, HEADS, HD, HD), F32)],
        scratch_shapes=[pltpu.VMEM((HEADS, HD, HD), F32)], compiler_params=_cp(2))(parts, parts, parts, lb, mc, mtc, mrefc)


def _mixer_tile(rows, u_ref, v_ref, og_ref, o_ref, lnw_ref, lnb_ref, sw_ref, sb_ref, hnw_ref):
    n = (rows.stop - rows.start) // SGU_CH
    yas, vjps = [], []
    for c in range(n):
        r = slice(rows.start + c * SGU_CH, rows.start + (c + 1) * SGU_CH)
        ya, vjp_a = jax.vjp(_sgu_fn, [u_ref[r, _hsl(g)] for g in range(HEADS)], [v_ref[r, _hsl(g)] for g in range(HEADS)],
                            [lnw_ref[:, _hsl(g)] for g in range(HEADS)], [lnb_ref[:, _hsl(g)] for g in range(HEADS)],
                            [sw_ref[g] for g in range(HEADS)], [sb_ref[g] for g in range(HEADS)])
        yas.append(ya)
        vjps.append(vjp_a)
    yb, vjp_b = jax.vjp(_readout_fn, [o_ref[0, rows, _hsl(h)] + o_ref[1, rows, _hsl(h)] for h in range(HEADS)],
                        og_ref[rows, :], hnw_ref[...])
    return (yas[0] if n == 1 else jnp.concatenate(yas, axis=0)), yb, vjps, vjp_b


def _part_specs(tm, first, n):
    return [pl.BlockSpec((tm, D), functools.partial(lambda k, i: (i, k), first + k)) for k in range(n)]


def mixer_fwd(x, parts, o, mod, lnw, lnb, sw, sb, hnw, wa, wb, wo):
    t = x.shape[0]

    def body(x_ref, u_ref, v_ref, og_ref, ga_ref, gb_ref, o_ref, mod_ref, lnw_ref, lnb_ref, sw_ref, sb_ref, hnw_ref,
             wa_ref, wb_ref, wo_ref, out_ref):
        ya, yb, _, _ = _mixer_tile(slice(0, TM), u_ref, v_ref, og_ref, o_ref, lnw_ref, lnb_ref, sw_ref, sb_ref, hnw_ref)
        merged = (jax.nn.sigmoid(ga_ref[...]) * mm(ya, wa_ref[...]) + jax.nn.sigmoid(gb_ref[...]) * mm(yb, wb_ref[...]))
        out_ref[...] = x_ref[...] + mod_ref[0, 2] * mm(merged, wo_ref[...])

    vec = lambda n: pl.BlockSpec((1, n), lambda i: (0, 0))
    return pl.pallas_call(
        body, name="mixer_fwd", grid=(t // TM,),
        in_specs=[pl.BlockSpec((TM, D), lambda i: (i, 0))] + _part_specs(TM, 4, 5)
        + [pl.BlockSpec((2, TM, D), lambda i: (0, i, 0)), pl.BlockSpec((1, 6, 1, D), _stream_row(TM)), vec(D), vec(D),
           VMEM_WHOLE, VMEM_WHOLE, vec(HD), VMEM_WHOLE, VMEM_WHOLE, VMEM_WHOLE],
        out_specs=pl.BlockSpec((TM, D), lambda i: (i, 0)),
        out_shape=jax.ShapeDtypeStruct((t, D), F32), compiler_params=_cp(1),
    )(x, parts, parts, parts, parts, parts, o, mod, lnw, lnb, sw, sb, hnw, wa, wb, wo)


def ffn_up_fwd(x, mod, nw, wg):
    t = x.shape[0]

    def body(x_ref, mod_ref, nw_ref, w_ref, out_ref):
        h = _norm_mod(x_ref[...], nw_ref[...], mod_ref[0, 3], mod_ref[0, 4]).astype(BF16)
        for j in range(N_DEV):
            out_ref[j] = jnp.dot(h, w_ref[j], preferred_element_type=F32)

    return pl.pallas_call(
        body, name="ffn_up_fwd", grid=(t // TM,),
        in_specs=[pl.BlockSpec((TM, D), lambda i: (i, 0)), pl.BlockSpec((1, 6, 1, D), _stream_row(TM)),
                  pl.BlockSpec((1, D), lambda i: (0, 0)), VMEM_WHOLE],
        out_specs=pl.BlockSpec((N_DEV, TM, FF_SLOT), lambda i: (0, i, 0)),
        out_shape=jax.ShapeDtypeStruct((N_DEV, t, FF_SLOT), F32), compiler_params=_cp(1))(x, mod, nw, wg)


def _halo_specs(nt, k_of, i_of):
    per = TM // GRID_W
    last = nt * per - 1
    return [pl.BlockSpec((1, GRID_W, FF_SLOT), lambda *g: (k_of(*g), jnp.maximum(i_of(*g) * per - 1, 0), 0)),
            pl.BlockSpec((1, TM, FF_SLOT), lambda *g: (k_of(*g), i_of(*g), 0)),
            pl.BlockSpec((1, GRID_W, FF_SLOT), lambda *g: (k_of(*g), jnp.minimum(i_of(*g) * per + per, last), 0))]


def _with_halo(prev_ref, main_ref, next_ref, i, nt):
    prev = jnp.where(i >= 2, prev_ref[0], 0.0)
    nxt = jnp.where((i >= 1) & (i <= nt - 2), next_ref[0], 0.0)
    return jnp.concatenate([prev, main_ref[0], nxt], axis=0)


def _tap_valid(dc, i, n_rows, offset):
    r = lax.broadcasted_iota(jnp.int32, (n_rows, 1), 0) - offset
    col = jnp.bitwise_and(r, GRID_W - 1)
    pos = jnp.where(i == 0, r, col) + dc
    return (pos >= 0) & (pos < jnp.where(i == 0, TM, GRID_W))


def _row_weight(cw_ref, dr, dc, i):
    w = cw_ref[0, 3 * (dr + 1) + dc + 1:3 * (dr + 1) + dc + 2, :]
    return w if dr == 0 else jnp.where(i == 0, 0.0, w)


def ffn_down_fwd(x, av, mod, cw, cb, wd):
    t = x.shape[0]
    nt = t // TM
    ext = TM + 2 * GRID_W

    def body(x_ref, ap_ref, am_ref, an_ref, v_ref, mod_ref, cw_ref, cb_ref, wd_ref, out_ref, ac_ref, y_ref, acc):
        i, k = pl.program_id(0), pl.program_id(1)
        a_ext = _with_halo(ap_ref, am_ref, an_ref, i, nt)
        conv = jnp.zeros((TM, FF_SLOT), F32) + cb_ref[0]
        for dc in (-1, 0, 1):
            rolled = a_ext if dc == 0 else pltpu.roll(a_ext, (-dc) % ext, 0)
            valid = _tap_valid(dc, i, TM, 0)
            for dr in (-1, 0, 1):
                lo = GRID_W + GRID_W * dr
                conv = conv + jnp.where(valid, rolled[lo:lo + TM], 0.0) * _row_weight(cw_ref, dr, dc, i)
        ac_ref[0] = conv
        part = mm(_glu_fn(conv, v_ref[0]), wd_ref[0])

        @pl.when(k == 0)
        def _():
            acc[...] = part

        @pl.when(k > 0)
        def _():
            acc[...] += part

        @pl.when(k == N_FFK - 1)
        def _():
            y_ref[...] = acc[...]
            out_ref[...] = x_ref[...] + mod_ref[0, 5] * acc[...]

    tile = pl.BlockSpec((TM, D), lambda i, k: (i, 0))
    return pl.pallas_call(
        body, name="ffn_down_fwd", grid=(nt, N_FFK),
        in_specs=[tile] + _halo_specs(nt, lambda i, k: k, lambda i, k: i)
        + [pl.BlockSpec((1, TM, FF_SLOT), lambda i, k: (N_FFK + k, i, 0)),
           pl.BlockSpec((1, 6, 1, D), lambda i, k: (jnp.where(i < 1, 0, 1), 0, 0, 0)),
           pl.BlockSpec((1, 9, FF_SLOT), lambda i, k: (k, 0, 0)), pl.BlockSpec((1, 1, FF_SLOT), lambda i, k: (k, 0, 0)),
           pl.BlockSpec((1, FF_SLOT, D), lambda i, k: (k, 0, 0))],
        out_specs=[tile, pl.BlockSpec((1, TM, FF_SLOT), lambda i, k: (k, i, 0)), tile],
        out_shape=[jax.ShapeDtypeStruct((t, D), F32), jax.ShapeDtypeStruct((N_FFK, t, FF_SLOT), F32),
                   jax.ShapeDtypeStruct((t, D), F32)],
        scratch_shapes=[pltpu.VMEM((TM, D), F32)], compiler_params=_cp(2))(x, av, av, av, av, mod, cw, cb, wd)


def loss_fwd_bwd(x, target, fw):
    t = x.shape[0]

    def body(x_ref, t_ref, w_ref, loss_ref, dx_ref, dw_ref):
        i = pl.program_id(0)

        @pl.when(i == 0)
        def _():
            loss_ref[...] = jnp.zeros_like(loss_ref)
            dw_ref[...] = jnp.zeros_like(dw_ref)
            dx_ref[...] = jnp.zeros_like(dx_ref)

        @pl.when(i > 0)
        def _():
            y, vjp = jax.vjp(_rms, x_ref[...], w_ref[...])
            err = y - t_ref[...]
            loss_ref[...] += 0.5 * jnp.sum(jnp.sum(err * err, axis=-1, keepdims=True) / D)
            dx, dw = vjp(err / D)
            dx_ref[...] = dx
            dw_ref[...] += dw

    return pl.pallas_call(
        body, name="loss_fwd_bwd", grid=(t // TM,),
        in_specs=[pl.BlockSpec((TM, D), lambda i: (i, 0)), pl.BlockSpec((TM, D), lambda i: (jnp.maximum(i - 1, 0), 0)),
                  pl.BlockSpec((1, D), lambda i: (0, 0))],
        out_specs=[pl.BlockSpec((8, 128), lambda i: (0, 0)), pl.BlockSpec((TM, D), lambda i: (i, 0)),
                   pl.BlockSpec((1, D), lambda i: (0, 0))],
        out_shape=[jax.ShapeDtypeStruct((8, 128), F32), jax.ShapeDtypeStruct((t, D), F32), jax.ShapeDtypeStruct((1, D), F32)],
        compiler_params=_cp(1))(x, target, fw)


def _stream_add(ref, k, is_ctx, val):
    ref[0, k] += jnp.where(is_ctx, val, 0.0)
    ref[1, k] += jnp.where(is_ctx, 0.0, val)


def ffn_down_bwd(dx, ac, av, y, mod, wd):
    t = dx.shape[0]
    nt = t // TM

    def body(dx_ref, ac_ref, v_ref, y_ref, mod_ref, wd_ref, dav_ref, dac_ref, dwd_ref, dg_ref):
        k, i = pl.program_id(0), pl.program_id(1)

        @pl.when((k == 0) & (i == 0))
        def _():
            dg_ref[...] = jnp.zeros_like(dg_ref)

        @pl.when(i == 0)
        def _():
            dwd_ref[...] = jnp.zeros_like(dwd_ref)

        @pl.when(k == 0)
        def _():
            _stream_add(dg_ref, 0, i == 0, jnp.sum(dx_ref[...] * y_ref[...], axis=0, keepdims=True))

        dout = mod_ref[0, 5] * dx_ref[...]
        z, vjp = jax.vjp(_glu_fn, ac_ref[0], v_ref[0])
        dac, dv = vjp(mm_nt(dout, wd_ref[0]))
        dac_ref[0] = dac
        dav_ref[0] = dv
        dwd_ref[0] += mm_tn(z, dout)

    tile = pl.BlockSpec((TM, D), lambda k, i: (i, 0))
    return pl.pallas_call(
        body, name="ffn_down_bwd", grid=(N_FFK, nt),
        in_specs=[tile, pl.BlockSpec((1, TM, FF_SLOT), lambda k, i: (k, i, 0)),
                  pl.BlockSpec((1, TM, FF_SLOT), lambda k, i: (N_FFK + k, i, 0)), tile,
                  pl.BlockSpec((1, 6, 1, D), lambda k, i: (jnp.where(i < 1, 0, 1), 0, 0, 0)),
                  pl.BlockSpec((1, FF_SLOT, D), lambda k, i: (k, 0, 0))],
        out_specs=[pl.BlockSpec((1, TM, FF_SLOT), lambda k, i: (N_FFK + k, i, 0)),
                   pl.BlockSpec((1, TM, FF_SLOT), lambda k, i: (k, i, 0)),
                   pl.BlockSpec((1, FF_SLOT, D), lambda k, i: (k, 0, 0)),
                   pl.BlockSpec((2, 1, 1, D), lambda k, i: (0, 0, 0, 0))],
        out_shape=[jax.ShapeDtypeStruct((N_DEV, t, FF_SLOT), F32), jax.ShapeDtypeStruct((N_FFK, t, FF_SLOT), F32),
                   jax.ShapeDtypeStruct((N_FFK, FF_SLOT, D), F32), jax.ShapeDtypeStruct((2, 1, 1, D), F32)],
        compiler_params=_cp(2))(dx, ac, av, y, mod, wd)


def conv_bwd(dav, dac, av, cw):
    t = dac.shape[1]
    nt = t // TM
    ext = TM + 2 * GRID_W

    def body(dav_in, gp_ref, gm_ref, gn_ref, ap_ref, am_ref, an_ref, cw_ref, dav_ref, dcw_ref, dcb_ref):
        k, i = pl.program_id(0), pl.program_id(1)

        @pl.when(i == 0)
        def _():
            dcw_ref[...] = jnp.zeros_like(dcw_ref)
            dcb_ref[...] = jnp.zeros_like(dcb_ref)

        g_ext = _with_halo(gp_ref, gm_ref, gn_ref, i, nt)
        a_ext = _with_halo(ap_ref, am_ref, an_ref, i, nt)
        g_main = gm_ref[0]
        dcb_ref[0] += jnp.sum(g_main, axis=0, keepdims=True)
        da = jnp.zeros((TM, FF_SLOT), F32)
        for dc in (-1, 0, 1):
            gv = jnp.where(_tap_valid(dc, i, ext, GRID_W), g_ext, 0.0)
            g_rolled = gv if dc == 0 else pltpu.roll(gv, dc % ext, 0)
            a_rolled = a_ext if dc == 0 else pltpu.roll(a_ext, (-dc) % ext, 0)
            g_valid = jnp.where(_tap_valid(dc, i, TM, 0), g_main, 0.0)
            for dr in (-1, 0, 1):
                lo = GRID_W - GRID_W * dr
                da = da + g_rolled[lo:lo + TM] * _row_weight(cw_ref, dr, dc, i)
                lo = GRID_W + GRID_W * dr
                tap = 3 * (dr + 1) + dc + 1
                dw = jnp.sum(g_valid * a_rolled[lo:lo + TM], axis=0, keepdims=True)
                dcw_ref[0, tap:tap + 1, :] += dw if dr == 0 else jnp.where(i == 0, 0.0, dw)
        dav_ref[0] = da

    return pl.pallas_call(
        body, name="conv_bwd", grid=(N_FFK, nt),
        in_specs=[ANY] + _halo_specs(nt, lambda k, i: k, lambda k, i: i) + _halo_specs(nt, lambda k, i: k, lambda k, i: i)
        + [pl.BlockSpec((1, 9, FF_SLOT), lambda k, i: (k, 0, 0))],
        out_specs=[pl.BlockSpec((1, TM, FF_SLOT), lambda k, i: (k, i, 0)), pl.BlockSpec((1, 9, FF_SLOT), lambda k, i: (k, 0, 0)),
                   pl.BlockSpec((1, 1, FF_SLOT), lambda k, i: (k, 0, 0))],
        out_shape=[jax.ShapeDtypeStruct(dav.shape, F32), jax.ShapeDtypeStruct((N_FFK, 9, FF_SLOT), F32),
                   jax.ShapeDtypeStruct((N_FFK, 1, FF_SLOT), F32)],
        input_output_aliases={0: 0}, compiler_params=_cp(2))(dav, dac, dac, dac, av, av, av, cw)


def _norm_mod_bwd(x_ref, nw_ref, mod_ref, k_shift, dh, dx_in, dx_ref, dnw_ref, dmod_ref, is_ctx):
    _, vjp = jax.vjp(_norm_mod, x_ref[...], nw_ref[...], mod_ref[0, k_shift], mod_ref[0, k_shift + 1])
    dx, dnw, dshift, dscale = vjp(dh)
    dx_ref[...] = dx_in + dx
    dnw_ref[...] += dnw
    _stream_add(dmod_ref, 0, is_ctx, dshift)
    _stream_add(dmod_ref, 1, is_ctx, dscale)


def ffn_up_bwd_x(dx2, x, dav, mod, nw, wg):
    t = x.shape[0]

    def body(dx2_ref, x_ref, dav_ref, mod_ref, nw_ref, w_ref, dx_ref, dnw_ref, dmod_ref):
        i = pl.program_id(0)

        @pl.when(i == 0)
        def _():
            dnw_ref[...] = jnp.zeros_like(dnw_ref)
            dmod_ref[...] = jnp.zeros_like(dmod_ref)

        dh = mm_nt(dav_ref[0], w_ref[0])
        for j in range(1, N_DEV):
            dh = dh + mm_nt(dav_ref[j], w_ref[j])
        _norm_mod_bwd(x_ref, nw_ref, mod_ref, 3, dh, dx2_ref[...], dx_ref, dnw_ref, dmod_ref, i == 0)

    tile = pl.BlockSpec((TM, D), lambda i: (i, 0))
    return pl.pallas_call(
        body, name="ffn_up_bwd_x", grid=(t // TM,),
        in_specs=[tile, tile, pl.BlockSpec((N_DEV, TM, FF_SLOT), lambda i: (0, i, 0)), pl.BlockSpec((1, 6, 1, D), _stream_row(TM)),
                  pl.BlockSpec((1, D), lambda i: (0, 0)), VMEM_WHOLE],
        out_specs=[tile, pl.BlockSpec((1, D), lambda i: (0, 0)), pl.BlockSpec((2, 2, 1, D), lambda i: (0, 0, 0, 0))],
        out_shape=[jax.ShapeDtypeStruct((t, D), F32), jax.ShapeDtypeStruct((1, D), F32), jax.ShapeDtypeStruct((2, 2, 1, D), F32)],
        compiler_params=_cp(1))(dx2, x, dav, mod, nw, wg)


def proj_bwd_w(x, dout, mod, nw, k_shift, slot, name):
    t = x.shape[0]
    stacked = dout.ndim == 3

    def body(x_ref, d_ref, mod_ref, nw_ref, dw_ref):
        @pl.when(pl.program_id(1) == 0)
        def _():
            dw_ref[...] = jnp.zeros_like(dw_ref)
        h = _norm_mod(x_ref[...], nw_ref[...], mod_ref[0, k_shift], mod_ref[0, k_shift + 1])
        dw_ref[0] += mm_tn(h, d_ref[0] if stacked else d_ref[...])

    d_spec = (pl.BlockSpec((1, TM, slot), lambda j, i: (j, i, 0)) if stacked else pl.BlockSpec((TM, slot), lambda j, i: (i, j)))
    return pl.pallas_call(
        body, name=name, grid=(N_DEV, t // TM),
        in_specs=[pl.BlockSpec((TM, D), lambda j, i: (i, 0)), d_spec,
                  pl.BlockSpec((1, 6, 1, D), lambda j, i: (jnp.where(i < 1, 0, 1), 0, 0, 0)), pl.BlockSpec((1, D), lambda j, i: (0, 0))],
        out_specs=pl.BlockSpec((1, D, slot), lambda j, i: (j, 0, 0)),
        out_shape=jax.ShapeDtypeStruct((N_DEV, D, slot), F32), compiler_params=_cp(2))(x, dout, mod, nw)


def mixer_bwd(dx, parts, o, mod, lnw, lnb, sw, sb, hnw, wa, wb, wo):
    t = dx.shape[0]
    tm = TM_SMALL
    n_ctx = CTX // tm

    def body(dx_ref, u_ref, v_ref, og_ref, ga_ref, gb_ref, o_ref, mod_ref, lnw_ref, lnb_ref, sw_ref, sb_ref, hnw_ref,
             wa_ref, wb_ref, wo_ref, d5_ref, do_ref, dwa_ref, dwb_ref, dwo_ref, dlnw_ref, dlnb_ref, dsw_ref, dsb_ref,
             dhnw_ref, dg_ref):
        i = pl.program_id(0)

        @pl.when(i == 0)
        def _():
            for r in (dwa_ref, dwb_ref, dwo_ref, dlnw_ref, dlnb_ref, dsw_ref, dsb_ref, dhnw_ref, dg_ref):
                r[...] = jnp.zeros_like(r)

        ya, yb, vjps, vjp_b = _mixer_tile(slice(0, tm), u_ref, v_ref, og_ref, o_ref, lnw_ref, lnb_ref, sw_ref, sb_ref, hnw_ref)
        pa, pb = mm(ya, wa_ref[...]), mm(yb, wb_ref[...])
        sa, sbg = jax.nn.sigmoid(ga_ref[...]), jax.nn.sigmoid(gb_ref[...])
        merged = sa * pa + sbg * pb
        dxv = dx_ref[...]
        _stream_add(dg_ref, 0, i < n_ctx, jnp.sum(dxv * mm(merged, wo_ref[...]), axis=0, keepdims=True))
        dy = mod_ref[0, 2] * dxv
        dmerged = mm_nt(dy, wo_ref[...])
        dwo_ref[...] += mm_tn(merged, dy)
        dpa, dpb = sa * dmerged, sbg * dmerged
        d5_ref[:, 3 * D:4 * D] = dmerged * pa * sa * (1.0 - sa)
        d5_ref[:, 4 * D:5 * D] = dmerged * pb * sbg * (1.0 - sbg)
        dwa_ref[...] += mm_tn(ya, dpa)
        dwb_ref[...] += mm_tn(yb, dpb)
        dub, dvb, dlnw, dlnb, dsw, dsb = vjps[0](mm_nt(dpa, wa_ref[...]))
        dob, dog, dhnw = vjp_b(mm_nt(dpb, wb_ref[...]))
        d5_ref[:, 2 * D:3 * D] = dog
        dhnw_ref[...] += dhnw
        for g in range(HEADS):
            d5_ref[:, g * HD:(g + 1) * HD] = dub[g]
            d5_ref[:, D + g * HD:D + (g + 1) * HD] = dvb[g]
            do_ref[:, _hsl(g)] = dob[g]
            dlnw_ref[:, _hsl(g)] += dlnw[g]
            dlnb_ref[:, _hsl(g)] += dlnb[g]
            dsw_ref[g] += dsw[g]
            dsb_ref[g] += dsb[g]

    vec = lambda n: pl.BlockSpec((1, n), lambda i: (0, 0))
    tile = pl.BlockSpec((tm, D), lambda i: (i, 0))
    sds = jax.ShapeDtypeStruct
    return pl.pallas_call(
        body, name="mixer_bwd", grid=(t // tm,),
        in_specs=[tile] + _part_specs(tm, 4, 5)
        + [pl.BlockSpec((2, tm, D), lambda i: (0, i, 0)), pl.BlockSpec((1, 6, 1, D), _stream_row(tm)), vec(D), vec(D),
           VMEM_WHOLE, VMEM_WHOLE, vec(HD), VMEM_WHOLE, VMEM_WHOLE, VMEM_WHOLE],
        out_specs=[pl.BlockSpec((tm, 5 * D), lambda i: (i, 0)), tile, VMEM_WHOLE, VMEM_WHOLE, VMEM_WHOLE, vec(D), vec(D),
                   VMEM_WHOLE, VMEM_WHOLE, vec(HD), pl.BlockSpec((2, 1, 1, D), lambda i: (0, 0, 0, 0))],
        out_shape=[sds((t, 5 * D), F32), sds((t, D), F32), sds((D, D), F32), sds((D, D), F32), sds((D, D), F32),
                   sds((1, D), F32), sds((1, D), F32), sds((HEADS, SGU_CH, SGU_CH), F32), sds((HEADS, SGU_CH, 1), F32),
                   sds((1, HD), F32), sds((2, 1, 1, D), F32)],
        compiler_params=_cp(1))(dx, parts, parts, parts, parts, parts, o, mod, lnw, lnb, sw, sb, hnw, wa, wb, wo)


def hgrn_bwd(parts, lb, mc, mtc, mrefc, ck, do):
    t = parts.shape[0]
    nc = t // CH
    chunk = _scan_chunk(nc)
    rev = lambda d, s: chunk(d, nc - 1 - s)

    def body(q_ref, f_ref, i_ref, lb_ref, m_ref, mt_ref, mr_ref, ck_ref, do_ref, dq_ref, df_ref, di_ref, dlb_ref, dst):
        @pl.when(pl.program_id(1) == 0)
        def _():
            dst[...] = jnp.zeros_like(dst)
            dlb_ref[...] = jnp.zeros_like(dlb_ref)

        heads = range(HEADS)
        fn = functools.partial(_hgrn_chunk, m=m_ref[0], mt=mt_ref[0], mref=mr_ref[0])
        _, vjp = jax.vjp(fn, [ck_ref[0, 0, h] for h in heads], [q_ref[:, _hsl(h)] for h in heads],
                         [f_ref[:, _hsl(h)] for h in heads], [i_ref[:, _hsl(h)] for h in heads],
                         [lb_ref[0, :, _hsl(h)] for h in heads])
        dstl, dq, df, di, dlb = vjp(([do_ref[:, _hsl(h)] for h in heads], [dst[h] for h in heads]))
        for h in heads:
            dst[h] = dstl[h]
            dq_ref[0, :, _hsl(h)] = dq[h]
            df_ref[0, :, _hsl(h)] = df[h]
            di_ref[0, :, _hsl(h)] = di[h]
            dlb_ref[0, :, _hsl(h)] += dlb[h]

    const = lambda d, s: (d, 0, 0)
    out = pl.BlockSpec((1, CH, D), lambda d, s: (d, rev(d, s), 0))
    return pl.pallas_call(
        body, name="hgrn_bwd", grid=(2, nc),
        in_specs=[pl.BlockSpec((CH, D), lambda d, s: (rev(d, s), 0)), pl.BlockSpec((CH, D), lambda d, s: (rev(d, s), 1 + d)),
                  pl.BlockSpec((CH, D), lambda d, s: (rev(d, s), 3)), pl.BlockSpec((1, 1, D), const),
                  pl.BlockSpec((1, CH, CH), const), pl.BlockSpec((1, CH, CH), const), pl.BlockSpec((1, CH, 1), const),
                  pl.BlockSpec((1, 1, HEADS, HD, HD), lambda d, s: (d, nc - 1 - s, 0, 0, 0)),
                  pl.BlockSpec((CH, D), lambda d, s: (rev(d, s), 0))],
        out_specs=[out, out, out, pl.BlockSpec((1, 1, D), const)],
        out_shape=[jax.ShapeDtypeStruct((2, t, D), F32)] * 3 + [jax.ShapeDtypeStruct((2, 1, D), F32)],
        scratch_shapes=[pltpu.VMEM((HEADS, HD, HD), F32)], compiler_params=_cp(2),
    )(parts, parts, parts, lb, mc, mtc, mrefc, ck, do)


def in_proj_bwd_x(dx1, x, dq, df, di, d5, mod, nw, wg):
    t = x.shape[0]
    tm = TM_SMALL
    n_ctx = CTX // tm

    def body(dx1_ref, x_ref, dq_ref, df_ref, di_ref, d5_ref, mod_ref, nw_ref, w_ref, dx_ref, dp_ref, dnw_ref, dmod_ref):
        i = pl.program_id(0)

        @pl.when(i == 0)
        def _():
            dnw_ref[...] = jnp.zeros_like(dnw_ref)
            dmod_ref[...] = jnp.zeros_like(dmod_ref)

        dp_ref[:, 0:D] = dq_ref[0] + dq_ref[1]
        dp_ref[:, D:2 * D] = df_ref[0]
        dp_ref[:, 2 * D:3 * D] = df_ref[1]
        dp_ref[:, 3 * D:4 * D] = di_ref[0] + di_ref[1]
        dp_ref[:, 4 * D:] = d5_ref[...]
        dh = mm_nt(dp_ref[:, 0:IN_SLOT], w_ref[0])
        for j in range(1, N_DEV):
            dh = dh + mm_nt(dp_ref[:, j * IN_SLOT:(j + 1) * IN_SLOT], w_ref[j])
        _norm_mod_bwd(x_ref, nw_ref, mod_ref, 0, dh, dx1_ref[...], dx_ref, dnw_ref, dmod_ref, i < n_ctx)

    tile = pl.BlockSpec((tm, D), lambda i: (i, 0))
    pair = pl.BlockSpec((2, tm, D), lambda i: (0, i, 0))
    return pl.pallas_call(
        body, name="in_proj_bwd_x", grid=(t // tm,),
        in_specs=[tile, tile, pair, pair, pair, pl.BlockSpec((tm, 5 * D), lambda i: (i, 0)),
                  pl.BlockSpec((1, 6, 1, D), _stream_row(tm)), pl.BlockSpec((1, D), lambda i: (0, 0)), VMEM_WHOLE],
        out_specs=[tile, pl.BlockSpec((tm, D_IN), lambda i: (i, 0)), pl.BlockSpec((1, D), lambda i: (0, 0)),
                   pl.BlockSpec((2, 2, 1, D), lambda i: (0, 0, 0, 0))],
        out_shape=[jax.ShapeDtypeStruct((t, D), F32), jax.ShapeDtypeStruct((t, D_IN), F32), jax.ShapeDtypeStruct((1, D), F32),
                   jax.ShapeDtypeStruct((2, 2, 1, D), F32)],
        compiler_params=_cp(1))(dx1, x, dq, df, di, d5, mod, nw, wg)


def _lb_fn(h0, h1):
    m = jnp.maximum(h0, h1)
    e0, e1 = jnp.exp(h0 - m), jnp.exp(h1 - m)
    return e1 / (e0 + e1)


def lower_bounds(hlb):
    def body(h_ref, out_ref):
        out_ref[...] = _lb_fn(h_ref[0:1, :], h_ref[1:2, :])
    return pl.pallas_call(body, name="lower_bounds", out_shape=jax.ShapeDtypeStruct((1, 2 * D), F32))(hlb)


def lower_bounds_bwd(hlb, dlb1):
    def body(h_ref, d_ref, out_ref):
        _, vjp = jax.vjp(_lb_fn, h_ref[0:1, :], h_ref[1:2, :])
        d0, d1 = vjp(d_ref[...])
        out_ref[0:1, :] = d0
        out_ref[1:2, :] = d1
    return pl.pallas_call(body, name="lower_bounds_bwd", out_shape=jax.ShapeDtypeStruct((2, 2 * D), F32))(hlb, dlb1)


def _ada_fn(c_all, cctx8, w, b):
    dot = lambda a, l: jnp.dot(_silu(a), w[l], precision=HIGHEST, preferred_element_type=F32) + b[l]
    return [dot(c_all, l) for l in range(2)], [dot(cctx8, l) for l in range(2)]


def ada_fwd(c_all, cctx8, w, b):
    cols = w.shape[-1]

    def body(c_ref, cc_ref, w_ref, b_ref, out_ref):
        ox, oc = _ada_fn(c_ref[...], cc_ref[...], [w_ref[0], w_ref[1]], [b_ref[0], b_ref[1]])
        for l in range(2):
            out_ref[l, 0] = ox[l]
            out_ref[l, 1] = oc[l]
    return pl.pallas_call(body, name="ada_fwd", out_shape=jax.ShapeDtypeStruct((2, 2, N_DEV, cols), F32),
                          compiler_params=_cp(0))(c_all, cctx8, w, b)


def ada_bwd(c_all, cctx8, w, b, dmx, dmc):
    cols = w.shape[-1]

    def body(c_ref, cc_ref, w_ref, b_ref, dmx_ref, dmc_ref, dw_ref, dc_ref):
        fn = lambda cc, w0, w1: _ada_fn(c_ref[...], cc, [w0, w1], [b_ref[0], b_ref[1]])
        _, vjp = jax.vjp(fn, cc_ref[...], w_ref[0], w_ref[1])
        dcc, dw0, dw1 = vjp(([dmx_ref[0], dmx_ref[1]], [dmc_ref[0], dmc_ref[1]]))
        dw_ref[0] = dw0
        dw_ref[1] = dw1
        dc_ref[...] = jnp.sum(dcc, axis=0, keepdims=True)
    return pl.pallas_call(body, name="ada_bwd", out_shape=[jax.ShapeDtypeStruct((2, D, cols), F32), jax.ShapeDtypeStruct((1, D), F32)],
                          compiler_params=_cp(0))(c_all, cctx8, w, b, dmx, dmc)


def adamw(w, m, v, gparts, name):
    r, c = w.shape
    p = gparts.shape[0]
    rt = r
    while rt % 16 == 0 and (p + 7) * rt * c * 4 * 2 > 24 * 2 ** 20:
        rt //= 2

    def body(w_ref, m_ref, v_ref, g_ref, go_ref, d_ref, mo_ref, vo_ref):
        g = g_ref[0]
        for k in range(1, p):
            g = g + g_ref[k]
        m2 = ADAM_B1 * m_ref[...] + (1.0 - ADAM_B1) * g
        v2 = ADAM_B2 * v_ref[...] + (1.0 - ADAM_B2) * (g * g)
        m_hat = m2 / (1.0 - ADAM_B1 ** ADAM_STEP)
        v_hat = v2 / (1.0 - ADAM_B2 ** ADAM_STEP)
        go_ref[...] = g
        d_ref[...] = -ADAM_LR * (m_hat / (jnp.sqrt(v_hat) + ADAM_EPS) + ADAM_WD * w_ref[...])
        mo_ref[...] = m2
        vo_ref[...] = v2

    tile = pl.BlockSpec((rt, c), lambda i: (i, 0))
    return pl.pallas_call(
        body, name=name, grid=(r // rt,),
        in_specs=[tile, tile, tile, pl.BlockSpec((p, rt, c), lambda i: (0, i, 0))], out_specs=[tile] * 4,
        out_shape=[jax.ShapeDtypeStruct((r, c), F32)] * 4, compiler_params=_cp(1))(w, m, v, gparts)


def _me():
    x, y, c = lax.axis_index("x"), lax.axis_index("y"), lax.axis_index("c")
    return x, y, c, 4 * x + 2 * y + c


def _peer(x, y, c, p):
    fx, fy, fc = (p >> 2) & 1, (p >> 1) & 1, p & 1
    return (1 - x if fx else x, 1 - y if fy else y, 1 - c if fc else c)


def all_gather(arrs, name):
    n = len(arrs)

    def body(*refs):
        ins, outs = refs[:n], refs[n:2 * n]
        send, recv, local = refs[2 * n:]
        x, y, c, me = _me()
        copies = []
        for a in range(n):
            lc = pltpu.make_async_copy(ins[a], outs[a].at[me], local.at[a])
            lc.start()
            copies.append(lc)
            for p in range(1, N_DEV):
                cp = pltpu.make_async_remote_copy(src_ref=ins[a], dst_ref=outs[a].at[me], send_sem=send.at[a, p - 1],
                                                  recv_sem=recv.at[a, p - 1], device_id=_peer(x, y, c, p),
                                                  device_id_type=pl.DeviceIdType.MESH)
                cp.start()
                copies.append(cp)
        for cp in copies:
            cp.wait()

    return pl.pallas_call(
        body, name=name, in_specs=[ANY] * n, out_specs=[ANY] * n,
        out_shape=[jax.ShapeDtypeStruct((N_DEV,) + a.shape, a.dtype) for a in arrs],
        scratch_shapes=[pltpu.SemaphoreType.DMA((n, N_DEV - 1)), pltpu.SemaphoreType.DMA((n, N_DEV - 1)),
                        pltpu.SemaphoreType.DMA((n,))])(*arrs)


def scatter_slots(grads, name):
    n = len(grads)
    n_l = len(grads[0])

    def body(*refs):
        ins = [refs[a * n_l:(a + 1) * n_l] for a in range(n)]
        outs = refs[n * n_l:n * n_l + n]
        send, recv, local = refs[n * n_l + n:]
        x, y, c, me = _me()
        copies = []
        for a in range(n):
            for l in range(n_l):
                lc = pltpu.make_async_copy(ins[a][l].at[me], outs[a].at[me, l], local.at[a, l])
                lc.start()
                copies.append(lc)
                for p in range(1, N_DEV):
                    px, py, pc = _peer(x, y, c, p)
                    cp = pltpu.make_async_remote_copy(src_ref=ins[a][l].at[4 * px + 2 * py + pc], dst_ref=outs[a].at[me, l],
                                                      send_sem=send.at[a, l, p - 1], recv_sem=recv.at[a, l, p - 1],
                                                      device_id=(px, py, pc), device_id_type=pl.DeviceIdType.MESH)
                    cp.start()
                    copies.append(cp)
        for cp in copies:
            cp.wait()

    flat = [g for ga in grads for g in ga]
    return pl.pallas_call(
        body, name=name, in_specs=[ANY] * len(flat), out_specs=[ANY] * n,
        out_shape=[jax.ShapeDtypeStruct((N_DEV, n_l) + ga[0].shape[1:], ga[0].dtype) for ga in grads],
        scratch_shapes=[pltpu.SemaphoreType.DMA((n, n_l, N_DEV - 1)), pltpu.SemaphoreType.DMA((n, n_l, N_DEV - 1)),
                        pltpu.SemaphoreType.DMA((n, n_l))])(*flat)


def _scan_constants():
    r = lax.broadcasted_iota(jnp.int32, (CH, CH), 0)
    s = lax.broadcasted_iota(jnp.int32, (CH, CH), 1)
    lower = (s <= r).astype(F32)
    t = jnp.arange(CH)[:, None]
    mc = jnp.stack([lower, lower.T])
    mref = jnp.stack([(t <= CH // 2 - 1).astype(F32), (t >= CH // 2).astype(F32)])
    return mc, jnp.stack([lower.T, lower]), mref


def local_step(x, ctx, target, mod, lb, w):
    n_layers = len(mod)
    mc, mtc, mrefc = _scan_constants()
    xs = jnp.concatenate([ctx, x], axis=0)
    saved = []
    for l in range(n_layers):
        parts = in_proj_fwd(xs, mod[l], w["nw1"][l], w["win"][l])
        o, ck = hgrn_fwd(parts, lb[l], mc, mtc, mrefc)
        x1 = mixer_fwd(xs, parts, o, mod[l], w["lnw"][l], w["lnb"][l], w["sw"][l], w["sb"][l], w["hnw"][l],
                       w["wa"][l], w["wb"][l], w["wo"][l])
        av = ffn_up_fwd(x1, mod[l], w["nw2"][l], w["wup"][l])
        x2, ac, y = ffn_down_fwd(x1, av, mod[l], w["cw"][l], w["cb"][l], w["wd"][l])
        saved.append((xs, parts, o, ck, x1, av, ac, y))
        xs = x2
    loss, dx, dfw = loss_fwd_bwd(xs, target, w["fw"])
    g = {k: [None] * n_layers for k in ("win", "wup", "wa", "wb", "wo", "wd", "nw1", "nw2", "lnw", "lnb", "sw", "sb", "hnw",
                                         "cw", "cb")}
    dmod, dlb = [None] * n_layers, [None] * n_layers
    for l in reversed(range(n_layers)):
        x0, parts, o, ck, x1, av, ac, y = saved[l]
        dav, dac, g["wd"][l], dg2 = ffn_down_bwd(dx, ac, av, y, mod[l], w["wd"][l])
        dav, g["cw"][l], g["cb"][l] = conv_bwd(dav, dac, av, w["cw"][l])
        dx1, g["nw2"][l], dmod2 = ffn_up_bwd_x(dx, x1, dav, mod[l], w["nw2"][l], w["wup"][l])
        g["wup"][l] = proj_bwd_w(x1, dav, mod[l], w["nw2"][l], 3, FF_SLOT, "ffn_up_bwd_w")
        (d5, do, g["wa"][l], g["wb"][l], g["wo"][l], g["lnw"][l], g["lnb"][l], g["sw"][l], g["sb"][l], g["hnw"][l],
         dg1) = mixer_bwd(dx1, parts, o, mod[l], w["lnw"][l], w["lnb"][l], w["sw"][l], w["sb"][l], w["hnw"][l],
                          w["wa"][l], w["wb"][l], w["wo"][l])
        dq, df, di, dlb[l] = hgrn_bwd(parts, lb[l], mc, mtc, mrefc, ck, do)
        dx, dparts, g["nw1"][l], dmod1 = in_proj_bwd_x(dx1, x0, dq, df, di, d5, mod[l], w["nw1"][l], w["win"][l])
        g["win"][l] = proj_bwd_w(x0, dparts, mod[l], w["nw1"][l], 0, IN_SLOT, "in_proj_bwd_w")
        dmod[l] = jnp.concatenate([dmod1, dg1, dmod2, dg2], axis=1)
    g["fw"] = dfw
    return loss[0, 0], dx[CTX:], g, dmod, dlb


ROW = 1024
REPLICATED = ("norm1_w", "sgu_ln_w", "sgu_ln_b", "sgu_w", "sgu_b", "hgrn_lower_bounds", "hgrn_norm_w", "norm2_w",
              "ffn_conv_b", "final_norm_w")
WEIGHT_ORDER = ("c_ctx", "ada_w", "ada_b", "norm1_w", "w_in", "sgu_ln_w", "sgu_ln_b", "sgu_w", "sgu_b", "hgrn_lower_bounds",
                "hgrn_norm_w", "w_branch_a", "w_branch_b", "w_out", "norm2_w", "ffn_w_up", "ffn_conv_w", "ffn_conv_b",
                "ffn_w_down", "final_norm_w")


def _rows_of(n):
    return -(-n // (8 * ROW)) * 8


def _pack(arrs, total_rows=None):
    parts = []
    for a in arrs:
        flat = a.reshape(-1).astype(F32)
        rows = _rows_of(flat.shape[0])
        parts.append(jnp.pad(flat, (0, rows * ROW - flat.shape[0])).reshape(rows, ROW))
    have = sum(p.shape[0] for p in parts)
    if total_rows is not None and total_rows > have:
        parts.append(jnp.zeros((total_rows - have, ROW), F32))
    return jnp.concatenate(parts, axis=0)


def _unpack(packed, shapes):
    lead = packed.shape[:-2]
    out, r0 = [], 0
    for s in shapes:
        n = math.prod(s)
        rows = _rows_of(n)
        out.append(packed[..., r0:r0 + rows, :].reshape(lead + (rows * ROW,))[..., :n].reshape(lead + tuple(s)))
        r0 += rows
    return out


def kernel(x, c, ctx, c_ctx, ada_w, ada_b, norm1_w, w_in, sgu_ln_w, sgu_ln_b, sgu_w, sgu_b, hgrn_lower_bounds, hgrn_norm_w, w_branch_a, w_branch_b, w_out, norm2_w, ffn_w_up, ffn_conv_w, ffn_conv_b, ffn_w_down, final_norm_w, loss_target, m_c_ctx, m_ada_w, m_ada_b, m_norm1_w, m_w_in, m_sgu_ln_w, m_sgu_ln_b, m_sgu_w, m_sgu_b, m_hgrn_lower_bounds, m_hgrn_norm_w, m_w_branch_a, m_w_branch_b, m_w_out, m_norm2_w, m_ffn_w_up, m_ffn_conv_w, m_ffn_conv_b, m_ffn_w_down, m_final_norm_w, v_c_ctx, v_ada_w, v_ada_b, v_norm1_w, v_w_in, v_sgu_ln_w, v_sgu_ln_b, v_sgu_w, v_sgu_b, v_hgrn_lower_bounds, v_hgrn_norm_w, v_w_branch_a, v_w_branch_b, v_w_out, v_norm2_w, v_ffn_w_up, v_ffn_conv_w, v_ffn_conv_b, v_ffn_w_down, v_final_norm_w):
    wts = dict(c_ctx=c_ctx, ada_w=ada_w, ada_b=ada_b, norm1_w=norm1_w, w_in=w_in, sgu_ln_w=sgu_ln_w, sgu_ln_b=sgu_ln_b,
               sgu_w=sgu_w, sgu_b=sgu_b, hgrn_lower_bounds=hgrn_lower_bounds, hgrn_norm_w=hgrn_norm_w, w_branch_a=w_branch_a,
               w_branch_b=w_branch_b, w_out=w_out, norm2_w=norm2_w, ffn_w_up=ffn_w_up, ffn_conv_w=ffn_conv_w,
               ffn_conv_b=ffn_conv_b, ffn_w_down=ffn_w_down, final_norm_w=final_norm_w)
    mom1 = dict(c_ctx=m_c_ctx, ada_w=m_ada_w, ada_b=m_ada_b, norm1_w=m_norm1_w, w_in=m_w_in, sgu_ln_w=m_sgu_ln_w,
                sgu_ln_b=m_sgu_ln_b, sgu_w=m_sgu_w, sgu_b=m_sgu_b, hgrn_lower_bounds=m_hgrn_lower_bounds,
                hgrn_norm_w=m_hgrn_norm_w, w_branch_a=m_w_branch_a, w_branch_b=m_w_branch_b, w_out=m_w_out, norm2_w=m_norm2_w,
                ffn_w_up=m_ffn_w_up, ffn_conv_w=m_ffn_conv_w, ffn_conv_b=m_ffn_conv_b, ffn_w_down=m_ffn_w_down,
                final_norm_w=m_final_norm_w)
    mom2 = dict(c_ctx=v_c_ctx, ada_w=v_ada_w, ada_b=v_ada_b, norm1_w=v_norm1_w, w_in=v_w_in, sgu_ln_w=v_sgu_ln_w,
                sgu_ln_b=v_sgu_ln_b, sgu_w=v_sgu_w, sgu_b=v_sgu_b, hgrn_lower_bounds=v_hgrn_lower_bounds,
                hgrn_norm_w=v_hgrn_norm_w, w_branch_a=v_w_branch_a, w_branch_b=v_w_branch_b, w_out=v_w_out, norm2_w=v_norm2_w,
                ffn_w_up=v_ffn_w_up, ffn_conv_w=v_ffn_conv_w, ffn_conv_b=v_ffn_conv_b, ffn_w_down=v_ffn_w_down,
                final_norm_w=v_final_norm_w)
    n_layers = w_in.shape[0]
    layers = range(n_layers)
    me = 4 * lax.axis_index("x") + 2 * lax.axis_index("y") + lax.axis_index("c")
    ada_cols = ada_w.shape[-1]

    big = ("w_in", "ffn_w_up", "w_branch_a", "w_branch_b", "w_out", "ffn_w_down")
    shards = [wts[k][l].astype(BF16) for k in big for l in layers]
    gathered = all_gather(shards + [ffn_conv_w.reshape(n_layers, 9, -1), c], "gather_weights")
    full = {k: [gathered[a * n_layers + l] for l in layers] for a, k in enumerate(big)}
    conv_all, c_all = gathered[-2], gathered[-1].reshape(N_DEV, D)
    conv_full = [conv_all[:, l].transpose(1, 0, 2).reshape(9, N_FFK, FF_SLOT).transpose(1, 0, 2) for l in layers]

    cctx8 = jnp.broadcast_to(c_ctx[None, :], (N_DEV, D))
    ada_b_cols = lax.dynamic_slice_in_dim(ada_b, me * ada_cols, ada_cols, axis=1)[:, None, :]
    mod_cols = ada_fwd(c_all, cctx8, ada_w, ada_b_cols)
    (mod_all,) = all_gather([mod_cols], "gather_mod")
    mod_x = lax.dynamic_index_in_dim(mod_all[:, :, 0], me, axis=2, keepdims=False)
    mod_c = mod_all[:, :, 1, 0]
    mod = [jnp.stack([mod_c[:, l].reshape(6, 1, D), mod_x[:, l].reshape(6, 1, D)]) for l in layers]

    lb1 = lower_bounds(hgrn_lower_bounds)
    lb = [jnp.zeros((2, 1, D), F32), lb1.reshape(2, 1, D)]

    w = {
        "win": full["w_in"], "wup": full["ffn_w_up"],
        "wa": [a.reshape(D, D) for a in full["w_branch_a"]], "wb": [a.reshape(D, D) for a in full["w_branch_b"]],
        "wo": [a.reshape(D, D) for a in full["w_out"]], "wd": [a.reshape(N_FFK, FF_SLOT, D) for a in full["ffn_w_down"]],
        "nw1": [norm1_w[l][None] for l in layers], "nw2": [norm2_w[l][None] for l in layers],
        "lnw": [sgu_ln_w[l][None] for l in layers], "lnb": [sgu_ln_b[l][None] for l in layers],
        "sw": [sgu_w[l] for l in layers], "sb": [sgu_b[l][:, :, None] for l in layers],
        "hnw": [hgrn_norm_w[l][None] for l in layers], "cw": conv_full,
        "cb": [ffn_conv_b[l].reshape(N_FFK, 1, FF_SLOT) for l in layers], "fw": final_norm_w[None],
    }
    loss, grad_x, g, dmod, dlb = local_step(x[0], ctx[0], loss_target[0], mod, lb, w)
    loss = lax.psum(loss, AXES)

    d_hlb = lower_bounds_bwd(hgrn_lower_bounds, dlb[1].reshape(1, 2 * D))
    st = lambda k: jnp.stack(g[k])
    rep_grads = {"norm1_w": st("nw1"), "sgu_ln_w": st("lnw"), "sgu_ln_b": st("lnb"), "sgu_w": st("sw"), "sgu_b": st("sb"),
                 "hgrn_lower_bounds": d_hlb, "hgrn_norm_w": st("hnw"), "norm2_w": st("nw2"), "ffn_conv_b": st("cb"),
                 "final_norm_w": g["fw"]}
    rep_rows = -(-sum(_rows_of(wts[k].size) for k in REPLICATED) // 64) * 64
    d_conv = jnp.stack([g["cw"][l].transpose(1, 0, 2).reshape(9, D_FF) for l in layers])
    dmod_x = jnp.stack([dmod[l][1].reshape(6 * D) for l in layers])
    dmod_c = jnp.stack([dmod[l][0].reshape(6 * D) for l in layers])
    small = jnp.concatenate([_pack([rep_grads[k] for k in REPLICATED], rep_rows), _pack([d_conv, dmod_x, dmod_c])], axis=0)
    (small_all,) = all_gather([small], "gather_small_grads")
    conv_g, dmx_all, dmc_all = _unpack(small_all[:, rep_rows:], [d_conv.shape, dmod_x.shape, dmod_c.shape])

    out = {}
    rep = adamw(_pack([wts[k] for k in REPLICATED], rep_rows), _pack([mom1[k] for k in REPLICATED], rep_rows),
                _pack([mom2[k] for k in REPLICATED], rep_rows), small_all[:, :rep_rows], "adamw_replicated")
    rep = [_unpack(r, [wts[k].shape for k in REPLICATED]) for r in rep]
    for n, k in enumerate(REPLICATED):
        out[k] = tuple(r[n] for r in rep)

    conv_mine = lax.dynamic_index_in_dim(conv_g.reshape(N_DEV, n_layers, 9, N_DEV, -1), me, axis=3, keepdims=False)
    flat2 = lambda a: a.reshape(-1, a.shape[-1])
    res = adamw(flat2(ffn_conv_w), flat2(m_ffn_conv_w), flat2(v_ffn_conv_w), conv_mine.reshape(N_DEV, -1, conv_mine.shape[-1]),
                "adamw_conv_w")
    out["ffn_conv_w"] = tuple(r.reshape(ffn_conv_w.shape) for r in res)

    out["ada_b"] = tuple(adamw(ada_b, m_ada_b, v_ada_b, jnp.concatenate([dmx_all, dmc_all], axis=0), "adamw_ada_b"))

    cols_of = lambda a: lax.dynamic_slice_in_dim(a, me * ada_cols, ada_cols, axis=2).transpose(1, 0, 2)
    d_ada_w, d_cctx = ada_bwd(c_all, cctx8, ada_w, ada_b_cols, cols_of(dmx_all), cols_of(dmc_all))
    res = adamw(flat2(ada_w), flat2(m_ada_w), flat2(v_ada_w), flat2(d_ada_w)[None], "adamw_ada_w")
    out["ada_w"] = tuple(r.reshape(ada_w.shape) for r in res)
    (d_cctx_all,) = all_gather([d_cctx], "gather_c_ctx_grad")
    res = adamw(c_ctx[None], m_c_ctx[None], v_c_ctx[None], d_cctx_all, "adamw_c_ctx")
    out["c_ctx"] = tuple(r[0] for r in res)

    slots = {"w_in": g["win"], "ffn_w_up": g["wup"], "w_branch_a": [a.reshape(N_DEV, -1, D) for a in g["wa"]],
             "w_branch_b": [a.reshape(N_DEV, -1, D) for a in g["wb"]], "w_out": [a.reshape(N_DEV, -1, D) for a in g["wo"]],
             "ffn_w_down": [a.reshape(N_DEV, -1, D) for a in g["wd"]]}
    received = scatter_slots([slots[k] for k in big], "scatter_weight_grads")
    for k, r in zip(big, received):
        res = adamw(flat2(wts[k]), flat2(mom1[k]), flat2(mom2[k]), r.reshape(N_DEV, -1, r.shape[-1]), "adamw_" + k)
        out[k] = tuple(a.reshape(wts[k].shape) for a in res)

    return (loss, grad_x[None]) + tuple(out[k][n] for n in range(4) for k in WEIGHT_ORDER)
```

```python
import functools
import math

import jax
import jax.numpy as jnp
from jax import lax
from jax.experimental import pallas as pl
from jax.experimental.pallas import tpu as pltpu

F32 = jnp.float32
BF16 = jnp.bfloat16
HIGHEST = lax.Precision.HIGHEST

N_DEV = 8
AXES = ("x", "y", "c")
D = 1024
CTX = 256
TM = 256
TM_SMALL = 128
CH = 64
SGU_CH = 128
HEADS = 8
HD = 128
GRID_W = 64
D_IN = 9 * D
IN_SLOT = D_IN // N_DEV
D_FF = 2816
FF_SLOT = 2 * D_FF // N_DEV
N_FFK = D_FF // FF_SLOT
RMS_EPS = 1e-6
LN_EPS = 1e-5
ADAM_LR, ADAM_B1, ADAM_B2, ADAM_EPS, ADAM_WD, ADAM_STEP = 0.001, 0.9, 0.999, 1e-08, 0.01, 10
VMEM_LIMIT_V7X = 56 * 2 ** 20

VMEM_WHOLE = pl.BlockSpec(memory_space=pltpu.VMEM)
ANY = pl.BlockSpec(memory_space=pl.ANY)


def _cp(n_axes):
    return pltpu.CompilerParams(dimension_semantics=("arbitrary",) * n_axes, vmem_limit_bytes=VMEM_LIMIT_V7X)


def _dot(a, b, dims):
    return lax.dot_general(a.astype(BF16), b.astype(BF16), (dims, ((), ())), preferred_element_type=F32)


@jax.custom_vjp
def mm(a, b):
    return _dot(a, b, ((1,), (0,)))


mm.defvjp(lambda a, b: (mm(a, b), (a, b)),
          lambda r, g: (_dot(g, r[1], ((1,), (1,))).astype(r[0].dtype), _dot(r[0], g, ((0,), (0,))).astype(r[1].dtype)))


@jax.custom_vjp
def mm_nt(a, b):
    return _dot(a, b, ((1,), (1,)))


mm_nt.defvjp(lambda a, b: (mm_nt(a, b), (a, b)),
             lambda r, g: (_dot(g, r[1], ((1,), (0,))).astype(r[0].dtype), _dot(g, r[0], ((0,), (0,))).astype(r[1].dtype)))


@jax.custom_vjp
def mm_tn(a, b):
    return _dot(a, b, ((0,), (0,)))


mm_tn.defvjp(lambda a, b: (mm_tn(a, b), (a, b)),
             lambda r, g: (_dot(r[1], g, ((1,), (1,))).astype(r[0].dtype), _dot(r[0], g, ((1,), (0,))).astype(r[1].dtype)))


@jax.custom_vjp
def _cum(m, mt, g):
    return jnp.dot(m, g, precision=HIGHEST, preferred_element_type=F32)


_cum.defvjp(lambda m, mt, g: (_cum(m, mt, g), (m, mt)),
            lambda r, d: (jnp.zeros_like(r[0]), jnp.zeros_like(r[1]),
                          jnp.dot(r[1], d, precision=HIGHEST, preferred_element_type=F32)))


def _silu(x):
    return x * jax.nn.sigmoid(x)


def _gelu(x):
    return 0.5 * x * (1.0 + jnp.tanh(math.sqrt(2.0 / math.pi) * (x + 0.044715 * (x * x * x))))


def _rms(x, w):
    return x * lax.rsqrt(jnp.mean(x * x, axis=-1, keepdims=True) + RMS_EPS) * w


def _norm_mod(x, w, shift, scale):
    return _rms(x, w) * (1.0 + scale) + shift


def _hsl(h):
    return slice(h * HD, (h + 1) * HD)


def _hgrn_chunk(st, qz, fz, iv, lb, m, mt, mref):
    outs, news = [], []
    for h in range(HEADS):
        q = _silu(qz[h])
        keep = 1.0 - lb[h]
        f = lb[h] + keep * jax.nn.sigmoid(fz[h])
        k = keep * jax.nn.sigmoid(-fz[h])
        g = jnp.log(f)
        b = _cum(m, mt, g)
        ref = jnp.sum(mref * g, axis=0, keepdims=True)
        last = jnp.sum(g, axis=0, keepdims=True)
        scores = jnp.where(m > 0.5, mm_nt(q * jnp.exp(b - ref), k * jnp.exp(ref - b)), 0.0)
        outs.append(mm(scores, iv[h]) + mm_nt(q * jnp.exp(b), st[h]))
        news.append(jnp.exp(last) * st[h] + mm_tn(iv[h], k * jnp.exp(last - b)))
    return outs, news


def _sgu_fn(ub, vb, lnw, lnb, sw, sb):
    gv = [_gelu(v) for v in vb]
    mu = sum(jnp.sum(t, axis=-1, keepdims=True) for t in gv) / D
    var = sum(jnp.sum((t - mu) * (t - mu), axis=-1, keepdims=True) for t in gv) / D
    inv = lax.rsqrt(var + LN_EPS)
    cols = []
    for g in range(HEADS):
        vn = (gv[g] - mu) * inv * lnw[g] + lnb[g]
        cols.append(_gelu(ub[g]) * (mm(sw[g], vn) + sb[g]))
    return jnp.concatenate(cols, axis=1)


def _readout_fn(ob, og, hnw):
    r = [o * lax.rsqrt(jnp.mean(o * o, axis=-1, keepdims=True) + RMS_EPS) * hnw for o in ob]
    return jnp.concatenate(r, axis=1) * _silu(og)


def _glu_fn(ac, v):
    return _gelu(ac) * v


def _stream_row(tm):
    n_ctx = CTX // tm
    return lambda i: (jnp.where(i < n_ctx, 0, 1), 0, 0, 0)


def in_proj_fwd(x, mod, nw, wg):
    t = x.shape[0]

    def body(x_ref, mod_ref, nw_ref, w_ref, out_ref):
        h = _norm_mod(x_ref[...], nw_ref[...], mod_ref[0, 0], mod_ref[0, 1]).astype(BF16)
        for j in range(N_DEV):
            out_ref[:, j * IN_SLOT:(j + 1) * IN_SLOT] = jnp.dot(h, w_ref[j], preferred_element_type=F32)

    return pl.pallas_call(
        body, name="in_proj_fwd", grid=(t // TM,),
        in_specs=[pl.BlockSpec((TM, D), lambda i: (i, 0)), pl.BlockSpec((1, 6, 1, D), _stream_row(TM)),
                  pl.BlockSpec((1, D), lambda i: (0, 0)), VMEM_WHOLE],
        out_specs=pl.BlockSpec((TM, D_IN), lambda i: (i, 0)),
        out_shape=jax.ShapeDtypeStruct((t, D_IN), F32), compiler_params=_cp(1))(x, mod, nw, wg)


def _scan_chunk(nc):
    ncc = CTX // CH

    def chunk(d, s):
        bwd = jnp.where(s < ncc, ncc - 1 - s, nc + ncc - 1 - s)
        return jnp.where(d == 0, s, bwd)
    return chunk


def hgrn_fwd(parts, lb, mc, mtc, mrefc):
    t = parts.shape[0]
    nc = t // CH
    chunk = _scan_chunk(nc)

    def body(q_ref, f_ref, i_ref, lb_ref, m_ref, mt_ref, mr_ref, o_ref, ck_ref, st):
        @pl.when(pl.program_id(1) == 0)
        def _():
            st[...] = jnp.zeros_like(st)
        ck_ref[0, 0] = st[...]
        outs, news = _hgrn_chunk([st[h] for h in range(HEADS)], [q_ref[:, _hsl(h)] for h in range(HEADS)],
                                 [f_ref[:, _hsl(h)] for h in range(HEADS)], [i_ref[:, _hsl(h)] for h in range(HEADS)],
                                 [lb_ref[0, :, _hsl(h)] for h in range(HEADS)], m_ref[0], mt_ref[0], mr_ref[0])
        for h in range(HEADS):
            o_ref[0, :, _hsl(h)] = outs[h]
            st[h] = news[h]

    const = lambda d, s: (d, 0, 0)
    return pl.pallas_call(
        body, name="hgrn_fwd", grid=(2, nc),
        in_specs=[pl.BlockSpec((CH, D), lambda d, s: (chunk(d, s), 0)), pl.BlockSpec((CH, D), lambda d, s: (chunk(d, s), 1 + d)),
                  pl.BlockSpec((CH, D), lambda d, s: (chunk(d, s), 3)), pl.BlockSpec((1, 1, D), const),
                  pl.BlockSpec((1, CH, CH), const), pl.BlockSpec((1, CH, CH), const), pl.BlockSpec((1, CH, 1), const)],
        out_specs=[pl.BlockSpec((1, CH, D), lambda d, s: (d, chunk(d, s), 0)),
                   pl.BlockSpec((1, 1, HEADS, HD, HD), lambda d, s: (d, s, 0, 0, 0))],
        out_shape=[jax.ShapeDtypeStruct((2, t, D), F32), jax.ShapeDtypeStruct((2, nc, HEADS, HD, HD), F32)],
        scratch_shapes=[pltpu.VMEM((HEADS, HD, HD), F32)], compiler_params=_cp(2))(parts, parts, parts, lb, mc, mtc, mrefc)


def _mixer_tile(rows, u_ref, v_ref, og_ref, o_ref, lnw_ref, lnb_ref, sw_ref, sb_ref, hnw_ref):
    n = (rows.stop - rows.start) // SGU_CH
    yas, vjps = [], []
    for c in range(n):
        r = slice(rows.start + c * SGU_CH, rows.start + (c + 1) * SGU_CH)
        ya, vjp_a = jax.vjp(_sgu_fn, [u_ref[r, _hsl(g)] for g in range(HEADS)], [v_ref[r, _hsl(g)] for g in range(HEADS)],
                            [lnw_ref[:, _hsl(g)] for g in range(HEADS)], [lnb_ref[:, _hsl(g)] for g in range(HEADS)],
                            [sw_ref[g] for g in range(HEADS)], [sb_ref[g] for g in range(HEADS)])
        yas.append(ya)
        vjps.append(vjp_a)
    yb, vjp_b = jax.vjp(_readout_fn, [o_ref[0, rows, _hsl(h)] + o_ref[1, rows, _hsl(h)] for h in range(HEADS)],
                        og_ref[rows, :], hnw_ref[...])
    return (yas[0] if n == 1 else jnp.concatenate(yas, axis=0)), yb, vjps, vjp_b


def _part_specs(tm, first, n):
    return [pl.BlockSpec((tm, D), functools.partial(lambda k, i: (i, k), first + k)) for k in range(n)]


def mixer_fwd(x, parts, o, mod, lnw, lnb, sw, sb, hnw, wa, wb, wo):
    t = x.shape[0]

    def body(x_ref, u_ref, v_ref, og_ref, ga_ref, gb_ref, o_ref, mod_ref, lnw_ref, lnb_ref, sw_ref, sb_ref, hnw_ref,
             wa_ref, wb_ref, wo_ref, out_ref):
        ya, yb, _, _ = _mixer_tile(slice(0, TM), u_ref, v_ref, og_ref, o_ref, lnw_ref, lnb_ref, sw_ref, sb_ref, hnw_ref)
        merged = (jax.nn.sigmoid(ga_ref[...]) * mm(ya, wa_ref[...]) + jax.nn.sigmoid(gb_ref[...]) * mm(yb, wb_ref[...]))
        out_ref[...] = x_ref[...] + mod_ref[0, 2] * mm(merged, wo_ref[...])

    vec = lambda n: pl.BlockSpec((1, n), lambda i: (0, 0))
    return pl.pallas_call(
        body, name="mixer_fwd", grid=(t // TM,),
        in_specs=[pl.BlockSpec((TM, D), lambda i: (i, 0))] + _part_specs(TM, 4, 5)
        + [pl.BlockSpec((2, TM, D), lambda i: (0, i, 0)), pl.BlockSpec((1, 6, 1, D), _stream_row(TM)), vec(D), vec(D),
           VMEM_WHOLE, VMEM_WHOLE, vec(HD), VMEM_WHOLE, VMEM_WHOLE, VMEM_WHOLE],
        out_specs=pl.BlockSpec((TM, D), lambda i: (i, 0)),
        out_shape=jax.ShapeDtypeStruct((t, D), F32), compiler_params=_cp(1),
    )(x, parts, parts, parts, parts, parts, o, mod, lnw, lnb, sw, sb, hnw, wa, wb, wo)


def ffn_up_fwd(x, mod, nw, wg):
    t = x.shape[0]

    def body(x_ref, mod_ref, nw_ref, w_ref, out_ref):
        h = _norm_mod(x_ref[...], nw_ref[...], mod_ref[0, 3], mod_ref[0, 4]).astype(BF16)
        for j in range(N_DEV):
            out_ref[j] = jnp.dot(h, w_ref[j], preferred_element_type=F32)

    return pl.pallas_call(
        body, name="ffn_up_fwd", grid=(t // TM,),
        in_specs=[pl.BlockSpec((TM, D), lambda i: (i, 0)), pl.BlockSpec((1, 6, 1, D), _stream_row(TM)),
                  pl.BlockSpec((1, D), lambda i: (0, 0)), VMEM_WHOLE],
        out_specs=pl.BlockSpec((N_DEV, TM, FF_SLOT), lambda i: (0, i, 0)),
        out_shape=jax.ShapeDtypeStruct((N_DEV, t, FF_SLOT), F32), compiler_params=_cp(1))(x, mod, nw, wg)


def _halo_specs(nt, k_of, i_of):
    per = TM // GRID_W
    last = nt * per - 1
    return [pl.BlockSpec((1, GRID_W, FF_SLOT), lambda *g: (k_of(*g), jnp.maximum(i_of(*g) * per - 1, 0), 0)),
            pl.BlockSpec((1, TM, FF_SLOT), lambda *g: (k_of(*g), i_of(*g), 0)),
            pl.BlockSpec((1, GRID_W, FF_SLOT), lambda *g: (k_of(*g), jnp.minimum(i_of(*g) * per + per, last), 0))]


def _with_halo(prev_ref, main_ref, next_ref, i, nt):
    prev = jnp.where(i >= 2, prev_ref[0], 0.0)
    nxt = jnp.where((i >= 1) & (i <= nt - 2), next_ref[0], 0.0)
    return jnp.concatenate([prev, main_ref[0], nxt], axis=0)


def _tap_valid(dc, i, n_rows, offset):
    r = lax.broadcasted_iota(jnp.int32, (n_rows, 1), 0) - offset
    col = jnp.bitwise_and(r, GRID_W - 1)
    pos = jnp.where(i == 0, r, col) + dc
    return (pos >= 0) & (pos < jnp.where(i == 0, TM, GRID_W))


def _row_weight(cw_ref, dr, dc, i):
    w = cw_ref[0, 3 * (dr + 1) + dc + 1:3 * (dr + 1) + dc + 2, :]
    return w if dr == 0 else jnp.where(i == 0, 0.0, w)


def ffn_down_fwd(x, av, mod, cw, cb, wd):
    t = x.shape[0]
    nt = t // TM
    ext = TM + 2 * GRID_W

    def body(x_ref, ap_ref, am_ref, an_ref, v_ref, mod_ref, cw_ref, cb_ref, wd_ref, out_ref, ac_ref, y_ref, acc):
        i, k = pl.program_id(0), pl.program_id(1)
        a_ext = _with_halo(ap_ref, am_ref, an_ref, i, nt)
        conv = jnp.zeros((TM, FF_SLOT), F32) + cb_ref[0]
        for dc in (-1, 0, 1):
            rolled = a_ext if dc == 0 else pltpu.roll(a_ext, (-dc) % ext, 0)
            valid = _tap_valid(dc, i, TM, 0)
            for dr in (-1, 0, 1):
                lo = GRID_W + GRID_W * dr
                conv = conv + jnp.where(valid, rolled[lo:lo + TM], 0.0) * _row_weight(cw_ref, dr, dc, i)
        ac_ref[0] = conv
        part = mm(_glu_fn(conv, v_ref[0]), wd_ref[0])

        @pl.when(k == 0)
        def _():
            acc[...] = part

        @pl.when(k > 0)
        def _():
            acc[...] += part

        @pl.when(k == N_FFK - 1)
        def _():
            y_ref[...] = acc[...]
            out_ref[...] = x_ref[...] + mod_ref[0, 5] * acc[...]

    tile = pl.BlockSpec((TM, D), lambda i, k: (i, 0))
    return pl.pallas_call(
        body, name="ffn_down_fwd", grid=(nt, N_FFK),
        in_specs=[tile] + _halo_specs(nt, lambda i, k: k, lambda i, k: i)
        + [pl.BlockSpec((1, TM, FF_SLOT), lambda i, k: (N_FFK + k, i, 0)),
           pl.BlockSpec((1, 6, 1, D), lambda i, k: (jnp.where(i < 1, 0, 1), 0, 0, 0)),
           pl.BlockSpec((1, 9, FF_SLOT), lambda i, k: (k, 0, 0)), pl.BlockSpec((1, 1, FF_SLOT), lambda i, k: (k, 0, 0)),
           pl.BlockSpec((1, FF_SLOT, D), lambda i, k: (k, 0, 0))],
        out_specs=[tile, pl.BlockSpec((1, TM, FF_SLOT), lambda i, k: (k, i, 0)), tile],
        out_shape=[jax.ShapeDtypeStruct((t, D), F32), jax.ShapeDtypeStruct((N_FFK, t, FF_SLOT), F32),
                   jax.ShapeDtypeStruct((t, D), F32)],
        scratch_shapes=[pltpu.VMEM((TM, D), F32)], compiler_params=_cp(2))(x, av, av, av, av, mod, cw, cb, wd)


def loss_fwd_bwd(x, target, fw):
    t = x.shape[0]

    def body(x_ref, t_ref, w_ref, loss_ref, dx_ref, dw_ref):
        i = pl.program_id(0)

        @pl.when(i == 0)
        def _():
            loss_ref[...] = jnp.zeros_like(loss_ref)
            dw_ref[...] = jnp.zeros_like(dw_ref)
            dx_ref[...] = jnp.zeros_like(dx_ref)

        @pl.when(i > 0)
        def _():
            y, vjp = jax.vjp(_rms, x_ref[...], w_ref[...])
            err = y - t_ref[...]
            loss_ref[...] += 0.5 * jnp.sum(jnp.sum(err * err, axis=-1, keepdims=True) / D)
            dx, dw = vjp(err / D)
            dx_ref[...] = dx
            dw_ref[...] += dw

    return pl.pallas_call(
        body, name="loss_fwd_bwd", grid=(t // TM,),
        in_specs=[pl.BlockSpec((TM, D), lambda i: (i, 0)), pl.BlockSpec((TM, D), lambda i: (jnp.maximum(i - 1, 0), 0)),
                  pl.BlockSpec((1, D), lambda i: (0, 0))],
        out_specs=[pl.BlockSpec((8, 128), lambda i: (0, 0)), pl.BlockSpec((TM, D), lambda i: (i, 0)),
                   pl.BlockSpec((1, D), lambda i: (0, 0))],
        out_shape=[jax.ShapeDtypeStruct((8, 128), F32), jax.ShapeDtypeStruct((t, D), F32), jax.ShapeDtypeStruct((1, D), F32)],
        compiler_params=_cp(1))(x, target, fw)


def _stream_add(ref, k, is_ctx, val):
    ref[0, k] += jnp.where(is_ctx, val, 0.0)
    ref[1, k] += jnp.where(is_ctx, 0.0, val)


def ffn_down_bwd(dx, ac, av, y, mod, wd):
    t = dx.shape[0]
    nt = t // TM

    def body(dx_ref, ac_ref, v_ref, y_ref, mod_ref, wd_ref, dav_ref, dac_ref, dwd_ref, dg_ref):
        k, i = pl.program_id(0), pl.program_id(1)

        @pl.when((k == 0) & (i == 0))
        def _():
            dg_ref[...] = jnp.zeros_like(dg_ref)

        @pl.when(i == 0)
        def _():
            dwd_ref[...] = jnp.zeros_like(dwd_ref)

        @pl.when(k == 0)
        def _():
            _stream_add(dg_ref, 0, i == 0, jnp.sum(dx_ref[...] * y_ref[...], axis=0, keepdims=True))

        dout = mod_ref[0, 5] * dx_ref[...]
        z, vjp = jax.vjp(_glu_fn, ac_ref[0], v_ref[0])
        dac, dv = vjp(mm_nt(dout, wd_ref[0]))
        dac_ref[0] = dac
        dav_ref[0] = dv
        dwd_ref[0] += mm_tn(z, dout)

    tile = pl.BlockSpec((TM, D), lambda k, i: (i, 0))
    return pl.pallas_call(
        body, name="ffn_down_bwd", grid=(N_FFK, nt),
        in_specs=[tile, pl.BlockSpec((1, TM, FF_SLOT), lambda k, i: (k, i, 0)),
                  pl.BlockSpec((1, TM, FF_SLOT), lambda k, i: (N_FFK + k, i, 0)), tile,
                  pl.BlockSpec((1, 6, 1, D), lambda k, i: (jnp.where(i < 1, 0, 1), 0, 0, 0)),
                  pl.BlockSpec((1, FF_SLOT, D), lambda k, i: (k, 0, 0))],
        out_specs=[pl.BlockSpec((1, TM, FF_SLOT), lambda k, i: (N_FFK + k, i, 0)),
                   pl.BlockSpec((1, TM, FF_SLOT), lambda k, i: (k, i, 0)),
                   pl.BlockSpec((1, FF_SLOT, D), lambda k, i: (k, 0, 0)),
                   pl.BlockSpec((2, 1, 1, D), lambda k, i: (0, 0, 0, 0))],
        out_shape=[jax.ShapeDtypeStruct((N_DEV, t, FF_SLOT), F32), jax.ShapeDtypeStruct((N_FFK, t, FF_SLOT), F32),
                   jax.ShapeDtypeStruct((N_FFK, FF_SLOT, D), F32), jax.ShapeDtypeStruct((2, 1, 1, D), F32)],
        compiler_params=_cp(2))(dx, ac, av, y, mod, wd)


def conv_bwd(dav, dac, av, cw):
    t = dac.shape[1]
    nt = t // TM
    ext = TM + 2 * GRID_W

    def body(dav_in, gp_ref, gm_ref, gn_ref, ap_ref, am_ref, an_ref, cw_ref, dav_ref, dcw_ref, dcb_ref):
        k, i = pl.program_id(0), pl.program_id(1)

        @pl.when(i == 0)
        def _():
            dcw_ref[...] = jnp.zeros_like(dcw_ref)
            dcb_ref[...] = jnp.zeros_like(dcb_ref)

        g_ext = _with_halo(gp_ref, gm_ref, gn_ref, i, nt)
        a_ext = _with_halo(ap_ref, am_ref, an_ref, i, nt)
        g_main = gm_ref[0]
        dcb_ref[0] += jnp.sum(g_main, axis=0, keepdims=True)
        da = jnp.zeros((TM, FF_SLOT), F32)
        for dc in (-1, 0, 1):
            gv = jnp.where(_tap_valid(dc, i, ext, GRID_W), g_ext, 0.0)
            g_rolled = gv if dc == 0 else pltpu.roll(gv, dc % ext, 0)
            a_rolled = a_ext if dc == 0 else pltpu.roll(a_ext, (-dc) % ext, 0)
            g_valid = jnp.where(_tap_valid(dc, i, TM, 0), g_main, 0.0)
            for dr in (-1, 0, 1):
                lo = GRID_W - GRID_W * dr
                da = da + g_rolled[lo:lo + TM] * _row_weight(cw_ref, dr, dc, i)
                lo = GRID_W + GRID_W * dr
                tap = 3 * (dr + 1) + dc + 1
                dw = jnp.sum(g_valid * a_rolled[lo:lo + TM], axis=0, keepdims=True)
                dcw_ref[0, tap:tap + 1, :] += dw if dr == 0 else jnp.where(i == 0, 0.0, dw)
        dav_ref[0] = da

    return pl.pallas_call(
        body, name="conv_bwd", grid=(N_FFK, nt),
        in_specs=[ANY] + _halo_specs(nt, lambda k, i: k, lambda k, i: i) + _halo_specs(nt, lambda k, i: k, lambda k, i: i)
        + [pl.BlockSpec((1, 9, FF_SLOT), lambda k, i: (k, 0, 0))],
        out_specs=[pl.BlockSpec((1, TM, FF_SLOT), lambda k, i: (k, i, 0)), pl.BlockSpec((1, 9, FF_SLOT), lambda k, i: (k, 0, 0)),
                   pl.BlockSpec((1, 1, FF_SLOT), lambda k, i: (k, 0, 0))],
        out_shape=[jax.ShapeDtypeStruct(dav.shape, F32), jax.ShapeDtypeStruct((N_FFK, 9, FF_SLOT), F32),
                   jax.ShapeDtypeStruct((N_FFK, 1, FF_SLOT), F32)],
        input_output_aliases={0: 0}, compiler_params=_cp(2))(dav, dac, dac, dac, av, av, av, cw)


def _norm_mod_bwd(x_ref, nw_ref, mod_ref, k_shift, dh, dx_in, dx_ref, dnw_ref, dmod_ref, is_ctx):
    _, vjp = jax.vjp(_norm_mod, x_ref[...], nw_ref[...], mod_ref[0, k_shift], mod_ref[0, k_shift + 1])
    dx, dnw, dshift, dscale = vjp(dh)
    dx_ref[...] = dx_in + dx
    dnw_ref[...] += dnw
    _stream_add(dmod_ref, 0, is_ctx, dshift)
    _stream_add(dmod_ref, 1, is_ctx, dscale)


def ffn_up_bwd_x(dx2, x, dav, mod, nw, wg):
    t = x.shape[0]

    def body(dx2_ref, x_ref, dav_ref, mod_ref, nw_ref, w_ref, dx_ref, dnw_ref, dmod_ref):
        i = pl.program_id(0)

        @pl.when(i == 0)
        def _():
            dnw_ref[...] = jnp.zeros_like(dnw_ref)
            dmod_ref[...] = jnp.zeros_like(dmod_ref)

        dh = mm_nt(dav_ref[0], w_ref[0])
        for j in range(1, N_DEV):
            dh = dh + mm_nt(dav_ref[j], w_ref[j])
        _norm_mod_bwd(x_ref, nw_ref, mod_ref, 3, dh, dx2_ref[...], dx_ref, dnw_ref, dmod_ref, i == 0)

    tile = pl.BlockSpec((TM, D), lambda i: (i, 0))
    return pl.pallas_call(
        body, name="ffn_up_bwd_x", grid=(t // TM,),
        in_specs=[tile, tile, pl.BlockSpec((N_DEV, TM, FF_SLOT), lambda i: (0, i, 0)), pl.BlockSpec((1, 6, 1, D), _stream_row(TM)),
                  pl.BlockSpec((1, D), lambda i: (0, 0)), VMEM_WHOLE],
        out_specs=[tile, pl.BlockSpec((1, D), lambda i: (0, 0)), pl.BlockSpec((2, 2, 1, D), lambda i: (0, 0, 0, 0))],
        out_shape=[jax.ShapeDtypeStruct((t, D), F32), jax.ShapeDtypeStruct((1, D), F32), jax.ShapeDtypeStruct((2, 2, 1, D), F32)],
        compiler_params=_cp(1))(dx2, x, dav, mod, nw, wg)


def proj_bwd_w(x, dout, mod, nw, k_shift, slot, name):
    t = x.shape[0]
    stacked = dout.ndim == 3

    def body(x_ref, d_ref, mod_ref, nw_ref, dw_ref):
        @pl.when(pl.program_id(1) == 0)
        def _():
            dw_ref[...] = jnp.zeros_like(dw_ref)
        h = _norm_mod(x_ref[...], nw_ref[...], mod_ref[0, k_shift], mod_ref[0, k_shift + 1])
        dw_ref[0] += mm_tn(h, d_ref[0] if stacked else d_ref[...])

    d_spec = (pl.BlockSpec((1, TM, slot), lambda j, i: (j, i, 0)) if stacked else pl.BlockSpec((TM, slot), lambda j, i: (i, j)))
    return pl.pallas_call(
        body, name=name, grid=(N_DEV, t // TM),
        in_specs=[pl.BlockSpec((TM, D), lambda j, i: (i, 0)), d_spec,
                  pl.BlockSpec((1, 6, 1, D), lambda j, i: (jnp.where(i < 1, 0, 1), 0, 0, 0)), pl.BlockSpec((1, D), lambda j, i: (0, 0))],
        out_specs=pl.BlockSpec((1, D, slot), lambda j, i: (j, 0, 0)),
        out_shape=jax.ShapeDtypeStruct((N_DEV, D, slot), F32), compiler_params=_cp(2))(x, dout, mod, nw)


def mixer_bwd(dx, parts, o, mod, lnw, lnb, sw, sb, hnw, wa, wb, wo):
    t = dx.shape[0]
    tm = TM_SMALL
    n_ctx = CTX // tm

    def body(dx_ref, u_ref, v_ref, og_ref, ga_ref, gb_ref, o_ref, mod_ref, lnw_ref, lnb_ref, sw_ref, sb_ref, hnw_ref,
             wa_ref, wb_ref, wo_ref, d5_ref, do_ref, dwa_ref, dwb_ref, dwo_ref, dlnw_ref, dlnb_ref, dsw_ref, dsb_ref,
             dhnw_ref, dg_ref):
        i = pl.program_id(0)

        @pl.when(i == 0)
        def _():
            for r in (dwa_ref, dwb_ref, dwo_ref, dlnw_ref, dlnb_ref, dsw_ref, dsb_ref, dhnw_ref, dg_ref):
                r[...] = jnp.zeros_like(r)

        ya, yb, vjps, vjp_b = _mixer_tile(slice(0, tm), u_ref, v_ref, og_ref, o_ref, lnw_ref, lnb_ref, sw_ref, sb_ref, hnw_ref)
        pa, pb = mm(ya, wa_ref[...]), mm(yb, wb_ref[...])
        sa, sbg = jax.nn.sigmoid(ga_ref[...]), jax.nn.sigmoid(gb_ref[...])
        merged = sa * pa + sbg * pb
        dxv = dx_ref[...]
        _stream_add(dg_ref, 0, i < n_ctx, jnp.sum(dxv * mm(merged, wo_ref[...]), axis=0, keepdims=True))
        dy = mod_ref[0, 2] * dxv
        dmerged = mm_nt(dy, wo_ref[...])
        dwo_ref[...] += mm_tn(merged, dy)
        dpa, dpb = sa * dmerged, sbg * dmerged
        d5_ref[:, 3 * D:4 * D] = dmerged * pa * sa * (1.0 - sa)
        d5_ref[:, 4 * D:5 * D] = dmerged * pb * sbg * (1.0 - sbg)
        dwa_ref[...] += mm_tn(ya, dpa)
        dwb_ref[...] += mm_tn(yb, dpb)
        dub, dvb, dlnw, dlnb, dsw, dsb = vjps[0](mm_nt(dpa, wa_ref[...]))
        dob, dog, dhnw = vjp_b(mm_nt(dpb, wb_ref[...]))
        d5_ref[:, 2 * D:3 * D] = dog
        dhnw_ref[...] += dhnw
        for g in range(HEADS):
            d5_ref[:, g * HD:(g + 1) * HD] = dub[g]
            d5_ref[:, D + g * HD:D + (g + 1) * HD] = dvb[g]
            do_ref[:, _hsl(g)] = dob[g]
            dlnw_ref[:, _hsl(g)] += dlnw[g]
            dlnb_ref[:, _hsl(g)] += dlnb[g]
            dsw_ref[g] += dsw[g]
            dsb_ref[g] += dsb[g]

    vec = lambda n: pl.BlockSpec((1, n), lambda i: (0, 0))
    tile = pl.BlockSpec((tm, D), lambda i: (i, 0))
    sds = jax.ShapeDtypeStruct
    return pl.pallas_call(
        body, name="mixer_bwd", grid=(t // tm,),
        in_specs=[tile] + _part_specs(tm, 4, 5)
        + [pl.BlockSpec((2, tm, D), lambda i: (0, i, 0)), pl.BlockSpec((1, 6, 1, D), _stream_row(tm)), vec(D), vec(D),
           VMEM_WHOLE, VMEM_WHOLE, vec(HD), VMEM_WHOLE, VMEM_WHOLE, VMEM_WHOLE],
        out_specs=[pl.BlockSpec((tm, 5 * D), lambda i: (i, 0)), tile, VMEM_WHOLE, VMEM_WHOLE, VMEM_WHOLE, vec(D), vec(D),
                   VMEM_WHOLE, VMEM_WHOLE, vec(HD), pl.BlockSpec((2, 1, 1, D), lambda i: (0, 0, 0, 0))],
        out_shape=[sds((t, 5 * D), F32), sds((t, D), F32), sds((D, D), F32), sds((D, D), F32), sds((D, D), F32),
                   sds((1, D), F32), sds((1, D), F32), sds((HEADS, SGU_CH, SGU_CH), F32), sds((HEADS, SGU_CH, 1), F32),
                   sds((1, HD), F32), sds((2, 1, 1, D), F32)],
        compiler_params=_cp(1))(dx, parts, parts, parts, parts, parts, o, mod, lnw, lnb, sw, sb, hnw, wa, wb, wo)


def hgrn_bwd(parts, lb, mc, mtc, mrefc, ck, do):
    t = parts.shape[0]
    nc = t // CH
    chunk = _scan_chunk(nc)
    rev = lambda d, s: chunk(d, nc - 1 - s)

    def body(q_ref, f_ref, i_ref, lb_ref, m_ref, mt_ref, mr_ref, ck_ref, do_ref, dq_ref, df_ref, di_ref, dlb_ref, dst):
        @pl.when(pl.program_id(1) == 0)
        def _():
            dst[...] = jnp.zeros_like(dst)
            dlb_ref[...] = jnp.zeros_like(dlb_ref)

        heads = range(HEADS)
        fn = functools.partial(_hgrn_chunk, m=m_ref[0], mt=mt_ref[0], mref=mr_ref[0])
        _, vjp = jax.vjp(fn, [ck_ref[0, 0, h] for h in heads], [q_ref[:, _hsl(h)] for h in heads],
                         [f_ref[:, _hsl(h)] for h in heads], [i_ref[:, _hsl(h)] for h in heads],
                         [lb_ref[0, :, _hsl(h)] for h in heads])
        dstl, dq, df, di, dlb = vjp(([do_ref[:, _hsl(h)] for h in heads], [dst[h] for h in heads]))
        for h in heads:
            dst[h] = dstl[h]
            dq_ref[0, :, _hsl(h)] = dq[h]
            df_ref[0, :, _hsl(h)] = df[h]
            di_ref[0, :, _hsl(h)] = di[h]
            dlb_ref[0, :, _hsl(h)] += dlb[h]

    const = lambda d, s: (d, 0, 0)
    out = pl.BlockSpec((1, CH, D), lambda d, s: (d, rev(d, s), 0))
    return pl.pallas_call(
        body, name="hgrn_bwd", grid=(2, nc),
        in_specs=[pl.BlockSpec((CH, D), lambda d, s: (rev(d, s), 0)), pl.BlockSpec((CH, D), lambda d, s: (rev(d, s), 1 + d)),
                  pl.BlockSpec((CH, D), lambda d, s: (rev(d, s), 3)), pl.BlockSpec((1, 1, D), const),
                  pl.BlockSpec((1, CH, CH), const), pl.BlockSpec((1, CH, CH), const), pl.BlockSpec((1, CH, 1), const),
                  pl.BlockSpec((1, 1, HEADS, HD, HD), lambda d, s: (d, nc - 1 - s, 0, 0, 0)),
                  pl.BlockSpec((CH, D), lambda d, s: (rev(d, s), 0))],
        out_specs=[out, out, out, pl.BlockSpec((1, 1, D), const)],
        out_shape=[jax.ShapeDtypeStruct((2, t, D), F32)] * 3 + [jax.ShapeDtypeStruct((2, 1, D), F32)],
        scratch_shapes=[pltpu.VMEM((HEADS, HD, HD), F32)], compiler_params=_cp(2),
    )(parts, parts, parts, lb, mc, mtc, mrefc, ck, do)


def in_proj_bwd_x(dx1, x, dq, df, di, d5, mod, nw, wg):
    t = x.shape[0]
    tm = TM_SMALL
    n_ctx = CTX // tm

    def body(dx1_ref, x_ref, dq_ref, df_ref, di_ref, d5_ref, mod_ref, nw_ref, w_ref, dx_ref, dp_ref, dnw_ref, dmod_ref):
        i = pl.program_id(0)

        @pl.when(i == 0)
        def _():
            dnw_ref[...] = jnp.zeros_like(dnw_ref)
            dmod_ref[...] = jnp.zeros_like(dmod_ref)

        dp_ref[:, 0:D] = dq_ref[0] + dq_ref[1]
        dp_ref[:, D:2 * D] = df_ref[0]
        dp_ref[:, 2 * D:3 * D] = df_ref[1]
        dp_ref[:, 3 * D:4 * D] = di_ref[0] + di_ref[1]
        dp_ref[:, 4 * D:] = d5_ref[...]
        dh = mm_nt(dp_ref[:, 0:IN_SLOT], w_ref[0])
        for j in range(1, N_DEV):
            dh = dh + mm_nt(dp_ref[:, j * IN_SLOT:(j + 1) * IN_SLOT], w_ref[j])
        _norm_mod_bwd(x_ref, nw_ref, mod_ref, 0, dh, dx1_ref[...], dx_ref, dnw_ref, dmod_ref, i < n_ctx)

    tile = pl.BlockSpec((tm, D), lambda i: (i, 0))
    pair = pl.BlockSpec((2, tm, D), lambda i: (0, i, 0))
    return pl.pallas_call(
        body, name="in_proj_bwd_x", grid=(t // tm,),
        in_specs=[tile, tile, pair, pair, pair, pl.BlockSpec((tm, 5 * D), lambda i: (i, 0)),
                  pl.BlockSpec((1, 6, 1, D), _stream_row(tm)), pl.BlockSpec((1, D), lambda i: (0, 0)), VMEM_WHOLE],
        out_specs=[tile, pl.BlockSpec((tm, D_IN), lambda i: (i, 0)), pl.BlockSpec((1, D), lambda i: (0, 0)),
                   pl.BlockSpec((2, 2, 1, D), lambda i: (0, 0, 0, 0))],
        out_shape=[jax.ShapeDtypeStruct((t, D), F32), jax.ShapeDtypeStruct((t, D_IN), F32), jax.ShapeDtypeStruct((1, D), F32),
                   jax.ShapeDtypeStruct((2, 2, 1, D), F32)],
        compiler_params=_cp(1))(dx1, x, dq, df, di, d5, mod, nw, wg)


def _lb_fn(h0, h1):
    m = jnp.maximum(h0, h1)
    e0, e1 = jnp.exp(h0 - m), jnp.exp(h1 - m)
    return e1 / (e0 + e1)


def lower_bounds(hlb):
    def body(h_ref, out_ref):
        out_ref[...] = _lb_fn(h_ref[0:1, :], h_ref[1:2, :])
    return pl.pallas_call(body, name="lower_bounds", out_shape=jax.ShapeDtypeStruct((1, 2 * D), F32))(hlb)


def lower_bounds_bwd(hlb, dlb1):
    def body(h_ref, d_ref, out_ref):
        _, vjp = jax.vjp(_lb_fn, h_ref[0:1, :], h_ref[1:2, :])
        d0, d1 = vjp(d_ref[...])
        out_ref[0:1, :] = d0
        out_ref[1:2, :] = d1
    return pl.pallas_call(body, name="lower_bounds_bwd", out_shape=jax.ShapeDtypeStruct((2, 2 * D), F32))(hlb, dlb1)


def _ada_fn(c_all, cctx8, w, b):
    dot = lambda a, l: jnp.dot(_silu(a), w[l], precision=HIGHEST, preferred_element_type=F32) + b[l]
    return [dot(c_all, l) for l in range(2)], [dot(cctx8, l) for l in range(2)]


def ada_fwd(c_all, cctx8, w, b):
    cols = w.shape[-1]

    def body(c_ref, cc_ref, w_ref, b_ref, out_ref):
        ox, oc = _ada_fn(c_ref[...], cc_ref[...], [w_ref[0], w_ref[1]], [b_ref[0], b_ref[1]])
        for l in range(2):
            out_ref[l, 0] = ox[l]
            out_ref[l, 1] = oc[l]
    return pl.pallas_call(body, name="ada_fwd", out_shape=jax.ShapeDtypeStruct((2, 2, N_DEV, cols), F32),
                          compiler_params=_cp(0))(c_all, cctx8, w, b)


def ada_bwd(c_all, cctx8, w, b, dmx, dmc):
    cols = w.shape[-1]

    def body(c_ref, cc_ref, w_ref, b_ref, dmx_ref, dmc_ref, dw_ref, dc_ref):
        fn = lambda cc, w0, w1: _ada_fn(c_ref[...], cc, [w0, w1], [b_ref[0], b_ref[1]])
        _, vjp = jax.vjp(fn, cc_ref[...], w_ref[0], w_ref[1])
        dcc, dw0, dw1 = vjp(([dmx_ref[0], dmx_ref[1]], [dmc_ref[0], dmc_ref[1]]))
        dw_ref[0] = dw0
        dw_ref[1] = dw1
        dc_ref[...] = jnp.sum(dcc, axis=0, keepdims=True)
    return pl.pallas_call(body, name="ada_bwd", out_shape=[jax.ShapeDtypeStruct((2, D, cols), F32), jax.ShapeDtypeStruct((1, D), F32)],
                          compiler_params=_cp(0))(c_all, cctx8, w, b, dmx, dmc)


def adamw(w, m, v, gparts, name):
    r, c = w.shape
    p = gparts.shape[0]
    rt = r
    while rt % 16 == 0 and (p + 7) * rt * c * 4 * 2 > 24 * 2 ** 20:
        rt //= 2

    def body(w_ref, m_ref, v_ref, g_ref, go_ref, d_ref, mo_ref, vo_ref):
        g = g_ref[0]
        for k in range(1, p):
            g = g + g_ref[k]
        m2 = ADAM_B1 * m_ref[...] + (1.0 - ADAM_B1) * g
        v2 = ADAM_B2 * v_ref[...] + (1.0 - ADAM_B2) * (g * g)
        m_hat = m2 / (1.0 - ADAM_B1 ** ADAM_STEP)
        v_hat = v2 / (1.0 - ADAM_B2 ** ADAM_STEP)
        go_ref[...] = g
        d_ref[...] = -ADAM_LR * (m_hat / (jnp.sqrt(v_hat) + ADAM_EPS) + ADAM_WD * w_ref[...])
        mo_ref[...] = m2
        vo_ref[...] = v2

    tile = pl.BlockSpec((rt, c), lambda i: (i, 0))
    return pl.pallas_call(
        body, name=name, grid=(r // rt,),
        in_specs=[tile, tile, tile, pl.BlockSpec((p, rt, c), lambda i: (0, i, 0))], out_specs=[tile] * 4,
        out_shape=[jax.ShapeDtypeStruct((r, c), F32)] * 4, compiler_params=_cp(1))(w, m, v, gparts)


def _me():
    x, y, c = lax.axis_index("x"), lax.axis_index("y"), lax.axis_index("c")
    return x, y, c, 4 * x + 2 * y + c


def _peer(x, y, c, p):
    fx, fy, fc = (p >> 2) & 1, (p >> 1) & 1, p & 1
    return (1 - x if fx else x, 1 - y if fy else y, 1 - c if fc else c)


def all_gather(arrs, name):
    n = len(arrs)

    def body(*refs):
        ins, outs = refs[:n], refs[n:2 * n]
        send, recv, local = refs[2 * n:]
        x, y, c, me = _me()
        copies = []
        for a in range(n):
            lc = pltpu.make_async_copy(ins[a], outs[a].at[me], local.at[a])
            lc.start()
            copies.append(lc)
            for p in range(1, N_DEV):
                cp = pltpu.make_async_remote_copy(src_ref=ins[a], dst_ref=outs[a].at[me], send_sem=send.at[a, p - 1],
                                                  recv_sem=recv.at[a, p - 1], device_id=_peer(x, y, c, p),
                                                  device_id_type=pl.DeviceIdType.MESH)
                cp.start()
                copies.append(cp)
        for cp in copies:
            cp.wait()

    return pl.pallas_call(
        body, name=name, in_specs=[ANY] * n, out_specs=[ANY] * n,
        out_shape=[jax.ShapeDtypeStruct((N_DEV,) + a.shape, a.dtype) for a in arrs],
        scratch_shapes=[pltpu.SemaphoreType.DMA((n, N_DEV - 1)), pltpu.SemaphoreType.DMA((n, N_DEV - 1)),
                        pltpu.SemaphoreType.DMA((n,))])(*arrs)


HBM = pl.BlockSpec(memory_space=pltpu.HBM)
SEM = pl.BlockSpec(memory_space=pltpu.SEMAPHORE)


def _in_hbm(a):
    return pltpu.with_memory_space_constraint(a, pltpu.HBM)


def _exchange_refs(srcs, lands, layer, scatter, a, x, y, c, p):
    me = 4 * x + 2 * y + c
    px, py, pc = _peer(x, y, c, p) if p else (x, y, c)
    src = srcs[a].at[4 * px + 2 * py + pc] if scatter else srcs[a]
    dst = lands[a].at[me] if layer is None else lands[a].at[me, layer]
    return src, dst, (px, py, pc)


def exchange_start(srcs, lands, layer, scatter, name):
    n = len(srcs)

    def body(*refs):
        ins, lz = refs[:n], refs[n:2 * n]
        send, recv = refs[2 * n], refs[2 * n + 1]
        token = refs[-1]
        x, y, c, _ = _me()
        for a in range(n):
            for p in range(1, N_DEV):
                src, dst, peer = _exchange_refs(ins, lz, layer, scatter, a, x, y, c, p)
                k = a * (N_DEV - 1) + p - 1
                pltpu.make_async_remote_copy(src_ref=src, dst_ref=dst, send_sem=send.at[k], recv_sem=recv.at[k],
                                             device_id=peer, device_id_type=pl.DeviceIdType.MESH).start()
        token[...] = jnp.zeros_like(token)

    thru = [pltpu.HBM(a.shape, a.dtype) for a in list(srcs) + list(lands)]
    out = pl.pallas_call(
        body, name=name, in_specs=[HBM] * (2 * n),
        out_specs=[SEM, SEM] + [HBM] * (2 * n) + [pl.BlockSpec(memory_space=pltpu.VMEM)],
        out_shape=[pltpu.SemaphoreType.DMA((n * (N_DEV - 1),)), pltpu.SemaphoreType.DMA((n * (N_DEV - 1),))] + thru
        + [jax.ShapeDtypeStruct((8, 128), F32)],
        input_output_aliases={i: 2 + i for i in range(2 * n)},
        compiler_params=pltpu.CompilerParams(has_side_effects=pltpu.SideEffectType.DATAFLOW_SIDE_EFFECTING),
    )(*[_in_hbm(a) for a in list(srcs) + list(lands)])
    return out[0], out[1], out[2:2 + n], out[2 + n:2 + 2 * n], out[-1]


def exchange_wait(send, recv, srcs, lands, layer, scatter, after, name):
    n = len(srcs)

    def body(*refs):
        ins, lz = refs[:n], refs[n:2 * n]
        send_ref, recv_ref = refs[2 * n], refs[2 * n + 1]
        x, y, c, _ = _me()
        for a in range(n):
            for p in range(1, N_DEV):
                src, dst, peer = _exchange_refs(ins, lz, layer, scatter, a, x, y, c, 0)
                k = a * (N_DEV - 1) + p - 1
                cp = pltpu.make_async_remote_copy(src_ref=src, dst_ref=dst, send_sem=send_ref.at[k],
                                                  recv_sem=recv_ref.at[k], device_id=peer,
                                                  device_id_type=pl.DeviceIdType.MESH)
                cp.wait_send()
                cp.wait_recv()

    thru = [pltpu.HBM(a.shape, a.dtype) for a in list(srcs) + list(lands)]
    out = pl.pallas_call(
        body, name=name, in_specs=[HBM] * (2 * n) + [SEM, SEM, ANY], out_specs=[HBM] * (2 * n), out_shape=thru,
        input_output_aliases={i: i for i in range(2 * n)},
        compiler_params=pltpu.CompilerParams(has_side_effects=pltpu.SideEffectType.DATAFLOW_SIDE_EFFECTING),
    )(*srcs, *lands, send, recv, after)
    return out[n:]


def place_own(srcs, lands_like, layer, scatter, name):
    n = len(srcs)
    create = isinstance(lands_like[0], jax.ShapeDtypeStruct)

    def body(*refs):
        ins = refs[:n]
        lz = refs[-n - 1:-1]
        sem = refs[-1]
        x, y, c, me = _me()
        copies = []
        for a in range(n):
            src = ins[a].at[me] if scatter else ins[a]
            dst = lz[a].at[me] if layer is None else lz[a].at[me, layer]
            cp = pltpu.make_async_copy(src, dst, sem.at[a])
            cp.start()
            copies.append(cp)
        for cp in copies:
            cp.wait()

    shapes = [jax.ShapeDtypeStruct(l.shape, l.dtype) for l in lands_like]
    if create:
        return pl.pallas_call(body, name=name, in_specs=[ANY] * n, out_specs=[ANY] * n, out_shape=shapes,
                              scratch_shapes=[pltpu.SemaphoreType.DMA((n,))])(*srcs)
    return pl.pallas_call(body, name=name, in_specs=[ANY] * (2 * n), out_specs=[ANY] * n, out_shape=shapes,
                          input_output_aliases={n + i: i for i in range(n)},
                          scratch_shapes=[pltpu.SemaphoreType.DMA((n,))])(*srcs, *lands_like)


def _scan_constants():
    r = lax.broadcasted_iota(jnp.int32, (CH, CH), 0)
    s = lax.broadcasted_iota(jnp.int32, (CH, CH), 1)
    lower = (s <= r).astype(F32)
    t = jnp.arange(CH)[:, None]
    mc = jnp.stack([lower, lower.T])
    mref = jnp.stack([(t <= CH // 2 - 1).astype(F32), (t >= CH // 2).astype(F32)])
    return mc, jnp.stack([lower.T, lower]), mref


def local_step(x, ctx, target, mod, lb, w, fetch=None, publish=None):
    kept = {}

    def keep(l, part, grads):
        kept[(l, part)] = grads
        return 0.0

    fetch = fetch or (lambda l, part, after: w)
    publish = publish or keep
    n_layers = len(mod)
    mc, mtc, mrefc = _scan_constants()
    xs = jnp.concatenate([ctx, x], axis=0)
    saved, big = [], []
    for l in range(n_layers):
        wl = dict(fetch(l, "in", xs))
        parts = in_proj_fwd(xs, mod[l], w["nw1"][l], wl["win"][l])
        o, ck = hgrn_fwd(parts, lb[l], mc, mtc, mrefc)
        wl.update(fetch(l, "rest", o))
        x1 = mixer_fwd(xs, parts, o, mod[l], w["lnw"][l], w["lnb"][l], w["sw"][l], w["sb"][l], w["hnw"][l],
                       wl["wa"][l], wl["wb"][l], wl["wo"][l])
        av = ffn_up_fwd(x1, mod[l], w["nw2"][l], wl["wup"][l])
        x2, ac, y = ffn_down_fwd(x1, av, mod[l], w["cw"][l], w["cb"][l], wl["wd"][l])
        saved.append((xs, parts, o, ck, x1, av, ac, y))
        big.append(wl)
        xs = x2
    loss, dx, dfw = loss_fwd_bwd(xs, target, w["fw"])
    g = {k: [None] * n_layers for k in ("nw1", "nw2", "lnw", "lnb", "sw", "sb", "hnw", "cw", "cb")}
    dmod, dlb = [None] * n_layers, [None] * n_layers
    tok = 0.0
    for l in reversed(range(n_layers)):
        x0, parts, o, ck, x1, av, ac, y = saved[l]
        wl = big[l]
        dav, dac, dwd, dg2 = ffn_down_bwd(dx, ac, av, y, mod[l] + tok, wl["wd"][l])
        dav, g["cw"][l], g["cb"][l] = conv_bwd(dav, dac, av, w["cw"][l])
        dx1, g["nw2"][l], dmod2 = ffn_up_bwd_x(dx, x1, dav, mod[l], w["nw2"][l], wl["wup"][l])
        dwup = proj_bwd_w(x1, dav, mod[l], w["nw2"][l], 3, FF_SLOT, "ffn_up_bwd_w")
        tok = publish(l, "ffn", {"wd": dwd, "wup": dwup})
        (d5, do, dwa, dwb, dwo, g["lnw"][l], g["lnb"][l], g["sw"][l], g["sb"][l], g["hnw"][l],
         dg1) = mixer_bwd(dx1, parts, o, mod[l] + tok, w["lnw"][l], w["lnb"][l], w["sw"][l], w["sb"][l], w["hnw"][l],
                          wl["wa"][l], wl["wb"][l], wl["wo"][l])
        tok = publish(l, "mix", {"wa": dwa, "wb": dwb, "wo": dwo})
        dq, df, di, dlb[l] = hgrn_bwd(parts, lb[l] + tok, mc, mtc, mrefc, ck, do)
        dx, dparts, g["nw1"][l], dmod1 = in_proj_bwd_x(dx1, x0, dq, df, di, d5, mod[l], w["nw1"][l], wl["win"][l])
        dwin = proj_bwd_w(x0, dparts, mod[l], w["nw1"][l], 0, IN_SLOT, "in_proj_bwd_w")
        tok = publish(l, "in", {"win": dwin})
        dmod[l] = jnp.concatenate([dmod1, dg1, dmod2, dg2], axis=1)
    g["fw"] = dfw
    for (l, part), grads in kept.items():
        for k, v in grads.items():
            g.setdefault(k, [None] * n_layers)[l] = v
    return loss[0, 0], dx[CTX:], g, dmod, dlb


ROW = 1024
REPLICATED = ("norm1_w", "sgu_ln_w", "sgu_ln_b", "sgu_w", "sgu_b", "hgrn_lower_bounds", "hgrn_norm_w", "norm2_w",
              "ffn_conv_b", "final_norm_w")
WEIGHT_ORDER = ("c_ctx", "ada_w", "ada_b", "norm1_w", "w_in", "sgu_ln_w", "sgu_ln_b", "sgu_w", "sgu_b", "hgrn_lower_bounds",
                "hgrn_norm_w", "w_branch_a", "w_branch_b", "w_out", "norm2_w", "ffn_w_up", "ffn_conv_w", "ffn_conv_b",
                "ffn_w_down", "final_norm_w")


def _rows_of(n):
    return -(-n // (8 * ROW)) * 8


def _pack(arrs, total_rows=None):
    parts = []
    for a in arrs:
        flat = a.reshape(-1).astype(F32)
        rows = _rows_of(flat.shape[0])
        parts.append(jnp.pad(flat, (0, rows * ROW - flat.shape[0])).reshape(rows, ROW))
    have = sum(p.shape[0] for p in parts)
    if total_rows is not None and total_rows > have:
        parts.append(jnp.zeros((total_rows - have, ROW), F32))
    return jnp.concatenate(parts, axis=0)


def _unpack(packed, shapes):
    lead = packed.shape[:-2]
    out, r0 = [], 0
    for s in shapes:
        n = math.prod(s)
        rows = _rows_of(n)
        out.append(packed[..., r0:r0 + rows, :].reshape(lead + (rows * ROW,))[..., :n].reshape(lead + tuple(s)))
        r0 += rows
    return out


def kernel(x, c, ctx, c_ctx, ada_w, ada_b, norm1_w, w_in, sgu_ln_w, sgu_ln_b, sgu_w, sgu_b, hgrn_lower_bounds, hgrn_norm_w, w_branch_a, w_branch_b, w_out, norm2_w, ffn_w_up, ffn_conv_w, ffn_conv_b, ffn_w_down, final_norm_w, loss_target, m_c_ctx, m_ada_w, m_ada_b, m_norm1_w, m_w_in, m_sgu_ln_w, m_sgu_ln_b, m_sgu_w, m_sgu_b, m_hgrn_lower_bounds, m_hgrn_norm_w, m_w_branch_a, m_w_branch_b, m_w_out, m_norm2_w, m_ffn_w_up, m_ffn_conv_w, m_ffn_conv_b, m_ffn_w_down, m_final_norm_w, v_c_ctx, v_ada_w, v_ada_b, v_norm1_w, v_w_in, v_sgu_ln_w, v_sgu_ln_b, v_sgu_w, v_sgu_b, v_hgrn_lower_bounds, v_hgrn_norm_w, v_w_branch_a, v_w_branch_b, v_w_out, v_norm2_w, v_ffn_w_up, v_ffn_conv_w, v_ffn_conv_b, v_ffn_w_down, v_final_norm_w):
    wts = dict(c_ctx=c_ctx, ada_w=ada_w, ada_b=ada_b, norm1_w=norm1_w, w_in=w_in, sgu_ln_w=sgu_ln_w, sgu_ln_b=sgu_ln_b,
               sgu_w=sgu_w, sgu_b=sgu_b, hgrn_lower_bounds=hgrn_lower_bounds, hgrn_norm_w=hgrn_norm_w, w_branch_a=w_branch_a,
               w_branch_b=w_branch_b, w_out=w_out, norm2_w=norm2_w, ffn_w_up=ffn_w_up, ffn_conv_w=ffn_conv_w,
               ffn_conv_b=ffn_conv_b, ffn_w_down=ffn_w_down, final_norm_w=final_norm_w)
    mom1 = dict(c_ctx=m_c_ctx, ada_w=m_ada_w, ada_b=m_ada_b, norm1_w=m_norm1_w, w_in=m_w_in, sgu_ln_w=m_sgu_ln_w,
                sgu_ln_b=m_sgu_ln_b, sgu_w=m_sgu_w, sgu_b=m_sgu_b, hgrn_lower_bounds=m_hgrn_lower_bounds,
                hgrn_norm_w=m_hgrn_norm_w, w_branch_a=m_w_branch_a, w_branch_b=m_w_branch_b, w_out=m_w_out, norm2_w=m_norm2_w,
                ffn_w_up=m_ffn_w_up, ffn_conv_w=m_ffn_conv_w, ffn_conv_b=m_ffn_conv_b, ffn_w_down=m_ffn_w_down,
                final_norm_w=m_final_norm_w)
    mom2 = dict(c_ctx=v_c_ctx, ada_w=v_ada_w, ada_b=v_ada_b, norm1_w=v_norm1_w, w_in=v_w_in, sgu_ln_w=v_sgu_ln_w,
                sgu_ln_b=v_sgu_ln_b, sgu_w=v_sgu_w, sgu_b=v_sgu_b, hgrn_lower_bounds=v_hgrn_lower_bounds,
                hgrn_norm_w=v_hgrn_norm_w, w_branch_a=v_w_branch_a, w_branch_b=v_w_branch_b, w_out=v_w_out, norm2_w=v_norm2_w,
                ffn_w_up=v_ffn_w_up, ffn_conv_w=v_ffn_conv_w, ffn_conv_b=v_ffn_conv_b, ffn_w_down=v_ffn_w_down,
                final_norm_w=v_final_norm_w)
    n_layers = w_in.shape[0]
    layers = range(n_layers)
    me = 4 * lax.axis_index("x") + 2 * lax.axis_index("y") + lax.axis_index("c")
    ada_cols = ada_w.shape[-1]

    big = ("w_in", "ffn_w_up", "w_branch_a", "w_branch_b", "w_out", "ffn_w_down")
    short = {"w_in": "win", "ffn_w_up": "wup", "w_branch_a": "wa", "w_branch_b": "wb", "w_out": "wo", "ffn_w_down": "wd"}
    groups = [[("w_in", 0)], [(k, 0) for k in big[1:]], [(k, 1) for k in big]]
    in_flight = []
    for n, group in enumerate(groups):
        shards = [wts[k][l].astype(BF16) for k, l in group]
        lands = place_own(shards, [jax.ShapeDtypeStruct((N_DEV,) + s.shape, BF16) for s in shards], None, False,
                          f"gather_weights_own_{n}")
        in_flight.append(exchange_start(shards, lands, None, False, f"gather_weights_start_{n}"))

    def as_used(k, a):
        return a if k in ("w_in", "ffn_w_up") else a.reshape(N_FFK, FF_SLOT, D) if k == "ffn_w_down" else a.reshape(D, D)

    arrived = {}

    def fetch(l, part, after):
        n = {(0, "in"): 0, (0, "rest"): 1, (1, "in"): 2}.get((l, part))
        if n is not None:
            send, recv, srcs, lands, _ = in_flight[n]
            got = exchange_wait(send, recv, srcs, lands, None, False, after, f"gather_weights_wait_{n}")
            for (k, ll), a in zip(groups[n], got):
                arrived.setdefault(short[k], [None] * n_layers)[ll] = as_used(k, a)
        return arrived

    gathered = all_gather([ffn_conv_w.reshape(n_layers, 9, -1), c], "gather_conv_c")
    conv_all, c_all = gathered[0], gathered[1].reshape(N_DEV, D)
    conv_full = [conv_all[:, l].transpose(1, 0, 2).reshape(9, N_FFK, FF_SLOT).transpose(1, 0, 2) for l in layers]

    cctx8 = jnp.broadcast_to(c_ctx[None, :], (N_DEV, D))
    ada_b_cols = lax.dynamic_slice_in_dim(ada_b, me * ada_cols, ada_cols, axis=1)[:, None, :]
    mod_cols = ada_fwd(c_all, cctx8, ada_w, ada_b_cols)
    (mod_all,) = all_gather([mod_cols], "gather_mod")
    mod_x = lax.dynamic_index_in_dim(mod_all[:, :, 0], me, axis=2, keepdims=False)
    mod_c = mod_all[:, :, 1, 0]
    mod = [jnp.stack([mod_c[:, l].reshape(6, 1, D), mod_x[:, l].reshape(6, 1, D)]) for l in layers]

    lb1 = lower_bounds(hgrn_lower_bounds)
    lb = [jnp.zeros((2, 1, D), F32), lb1.reshape(2, 1, D)]

    w = {
        "nw1": [norm1_w[l][None] for l in layers], "nw2": [norm2_w[l][None] for l in layers],
        "lnw": [sgu_ln_w[l][None] for l in layers], "lnb": [sgu_ln_b[l][None] for l in layers],
        "sw": [sgu_w[l] for l in layers], "sb": [sgu_b[l][:, :, None] for l in layers],
        "hnw": [hgrn_norm_w[l][None] for l in layers], "cw": conv_full,
        "cb": [ffn_conv_b[l].reshape(N_FFK, 1, FF_SLOT) for l in layers], "fw": final_norm_w[None],
    }
    long = {v: k for k, v in short.items()}
    landing, sent = {}, []

    def publish(l, part, grads):
        keys = [long[k] for k in grads]
        slots = [a.reshape((N_DEV, -1, a.shape[-1])) for a in grads.values()]
        zones = [landing.get(k, jax.ShapeDtypeStruct((N_DEV, n_layers) + s.shape[1:], F32)) for k, s in zip(keys, slots)]
        zones = place_own(slots, zones, l, True, f"scatter_grads_own_{part}_{l}")
        send, recv, srcs, zones, token = exchange_start(slots, zones, l, True, f"scatter_grads_start_{part}_{l}")
        landing.update(zip(keys, zones))
        sent.append((keys, l, part, send, recv, srcs, token))
        return token[0, 0]

    loss, grad_x, g, dmod, dlb = local_step(x[0], ctx[0], loss_target[0], mod, lb, w, fetch, publish)
    loss = lax.psum(loss, AXES)

    d_hlb = lower_bounds_bwd(hgrn_lower_bounds, dlb[1].reshape(1, 2 * D))
    st = lambda k: jnp.stack(g[k])
    rep_grads = {"norm1_w": st("nw1"), "sgu_ln_w": st("lnw"), "sgu_ln_b": st("lnb"), "sgu_w": st("sw"), "sgu_b": st("sb"),
                 "hgrn_lower_bounds": d_hlb, "hgrn_norm_w": st("hnw"), "norm2_w": st("nw2"), "ffn_conv_b": st("cb"),
                 "final_norm_w": g["fw"]}
    rep_rows = -(-sum(_rows_of(wts[k].size) for k in REPLICATED) // 64) * 64
    d_conv = jnp.stack([g["cw"][l].transpose(1, 0, 2).reshape(9, D_FF) for l in layers])
    dmod_x = jnp.stack([dmod[l][1].reshape(6 * D) for l in layers])
    dmod_c = jnp.stack([dmod[l][0].reshape(6 * D) for l in layers])
    small = jnp.concatenate([_pack([rep_grads[k] for k in REPLICATED], rep_rows), _pack([d_conv, dmod_x, dmod_c])], axis=0)
    (small_all,) = all_gather([small], "gather_small_grads")
    conv_g, dmx_all, dmc_all = _unpack(small_all[:, rep_rows:], [d_conv.shape, dmod_x.shape, dmod_c.shape])

    out = {}
    rep = adamw(_pack([wts[k] for k in REPLICATED], rep_rows), _pack([mom1[k] for k in REPLICATED], rep_rows),
                _pack([mom2[k] for k in REPLICATED], rep_rows), small_all[:, :rep_rows], "adamw_replicated")
    rep = [_unpack(r, [wts[k].shape for k in REPLICATED]) for r in rep]
    for n, k in enumerate(REPLICATED):
        out[k] = tuple(r[n] for r in rep)

    conv_mine = lax.dynamic_index_in_dim(conv_g.reshape(N_DEV, n_layers, 9, N_DEV, -1), me, axis=3, keepdims=False)
    flat2 = lambda a: a.reshape(-1, a.shape[-1])
    res = adamw(flat2(ffn_conv_w), flat2(m_ffn_conv_w), flat2(v_ffn_conv_w), conv_mine.reshape(N_DEV, -1, conv_mine.shape[-1]),
                "adamw_conv_w")
    out["ffn_conv_w"] = tuple(r.reshape(ffn_conv_w.shape) for r in res)

    out["ada_b"] = tuple(adamw(ada_b, m_ada_b, v_ada_b, jnp.concatenate([dmx_all, dmc_all], axis=0), "adamw_ada_b"))

    cols_of = lambda a: lax.dynamic_slice_in_dim(a, me * ada_cols, ada_cols, axis=2).transpose(1, 0, 2)
    d_ada_w, d_cctx = ada_bwd(c_all, cctx8, ada_w, ada_b_cols, cols_of(dmx_all), cols_of(dmc_all))
    res = adamw(flat2(ada_w), flat2(m_ada_w), flat2(v_ada_w), flat2(d_ada_w)[None], "adamw_ada_w")
    out["ada_w"] = tuple(r.reshape(ada_w.shape) for r in res)
    (d_cctx_all,) = all_gather([d_cctx], "gather_c_ctx_grad")
    res = adamw(c_ctx[None], m_c_ctx[None], v_c_ctx[None], d_cctx_all, "adamw_c_ctx")
    out["c_ctx"] = tuple(r[0] for r in res)

    after = sent[-1][-1]
    for part in ("ffn", "mix", "in"):
        for keys, l, p, send, recv, srcs, _ in sent:
            if p == part:
                zones = exchange_wait(send, recv, srcs, [landing[k] for k in keys], l, True, after, f"scatter_grads_wait_{part}_{l}")
                landing.update(zip(keys, zones))
    for k in big:
        r = landing[k]
        res = adamw(flat2(wts[k]), flat2(mom1[k]), flat2(mom2[k]), r.reshape(N_DEV, -1, r.shape[-1]), "adamw_" + k)
        out[k] = tuple(a.reshape(wts[k].shape) for a in res)

    return (loss, grad_x[None]) + tuple(out[k][n] for n in range(4) for k in WEIGHT_ORDER)
```

```python
import functools
import math

import jax
import jax.numpy as jnp
from jax import lax
from jax.experimental import pallas as pl
from jax.experimental.pallas import tpu as pltpu

F32 = jnp.float32
BF16 = jnp.bfloat16
HIGHEST = lax.Precision.HIGHEST

N_DEV = 8
AXES = ("x", "y", "c")
D = 1024
CTX = 256
TM = 256
TM_SMALL = 128
CH = 64
SGU_CH = 128
HEADS = 8
HD = 128
GRID_W = 64
D_IN = 9 * D
IN_SLOT = D_IN // N_DEV
D_FF = 2816
FF_SLOT = 2 * D_FF // N_DEV
N_FFK = D_FF // FF_SLOT
RMS_EPS = 1e-6
LN_EPS = 1e-5
ADAM_LR, ADAM_B1, ADAM_B2, ADAM_EPS, ADAM_WD, ADAM_STEP = 0.001, 0.9, 0.999, 1e-08, 0.01, 10
VMEM_LIMIT_V7X = 56 * 2 ** 20
GRAD_WIRE = jnp.bfloat16

VMEM_WHOLE = pl.BlockSpec(memory_space=pltpu.VMEM)
ANY = pl.BlockSpec(memory_space=pl.ANY)


def _cp(n_axes):
    return pltpu.CompilerParams(dimension_semantics=("arbitrary",) * n_axes, vmem_limit_bytes=VMEM_LIMIT_V7X)


def _dot(a, b, dims):
    return lax.dot_general(a.astype(BF16), b.astype(BF16), (dims, ((), ())), preferred_element_type=F32)


@jax.custom_vjp
def mm(a, b):
    return _dot(a, b, ((1,), (0,)))


mm.defvjp(lambda a, b: (mm(a, b), (a, b)),
          lambda r, g: (_dot(g, r[1], ((1,), (1,))).astype(r[0].dtype), _dot(r[0], g, ((0,), (0,))).astype(r[1].dtype)))


@jax.custom_vjp
def mm_nt(a, b):
    return _dot(a, b, ((1,), (1,)))


mm_nt.defvjp(lambda a, b: (mm_nt(a, b), (a, b)),
             lambda r, g: (_dot(g, r[1], ((1,), (0,))).astype(r[0].dtype), _dot(g, r[0], ((0,), (0,))).astype(r[1].dtype)))


@jax.custom_vjp
def mm_tn(a, b):
    return _dot(a, b, ((0,), (0,)))


mm_tn.defvjp(lambda a, b: (mm_tn(a, b), (a, b)),
             lambda r, g: (_dot(r[1], g, ((1,), (1,))).astype(r[0].dtype), _dot(r[0], g, ((1,), (0,))).astype(r[1].dtype)))


@jax.custom_vjp
def _cum(m, mt, g):
    return jnp.dot(m, g, precision=HIGHEST, preferred_element_type=F32)


_cum.defvjp(lambda m, mt, g: (_cum(m, mt, g), (m, mt)),
            lambda r, d: (jnp.zeros_like(r[0]), jnp.zeros_like(r[1]),
                          jnp.dot(r[1], d, precision=HIGHEST, preferred_element_type=F32)))


def _silu(x):
    return x * jax.nn.sigmoid(x)


def _gelu(x):
    return 0.5 * x * (1.0 + jnp.tanh(math.sqrt(2.0 / math.pi) * (x + 0.044715 * (x * x * x))))


def _rms(x, w):
    return x * lax.rsqrt(jnp.mean(x * x, axis=-1, keepdims=True) + RMS_EPS) * w


def _norm_mod(x, w, shift, scale):
    return _rms(x, w) * (1.0 + scale) + shift


def _hsl(h):
    return slice(h * HD, (h + 1) * HD)


def _hgrn_chunk(st, qz, fz, iv, lb, m, mt, mref):
    outs, news = [], []
    for h in range(HEADS):
        q = _silu(qz[h])
        keep = 1.0 - lb[h]
        f = lb[h] + keep * jax.nn.sigmoid(fz[h])
        k = keep * jax.nn.sigmoid(-fz[h])
        g = jnp.log(f)
        b = _cum(m, mt, g)
        ref = jnp.sum(mref * g, axis=0, keepdims=True)
        last = jnp.sum(g, axis=0, keepdims=True)
        scores = jnp.where(m > 0.5, mm_nt(q * jnp.exp(b - ref), k * jnp.exp(ref - b)), 0.0)
        outs.append(mm(scores, iv[h]) + mm_nt(q * jnp.exp(b), st[h]))
        news.append(jnp.exp(last) * st[h] + mm_tn(iv[h], k * jnp.exp(last - b)))
    return outs, news


def _sgu_fn(ub, vb, lnw, lnb, sw, sb):
    gv = [_gelu(v) for v in vb]
    mu = sum(jnp.sum(t, axis=-1, keepdims=True) for t in gv) / D
    var = sum(jnp.sum((t - mu) * (t - mu), axis=-1, keepdims=True) for t in gv) / D
    inv = lax.rsqrt(var + LN_EPS)
    cols = []
    for g in range(HEADS):
        vn = (gv[g] - mu) * inv * lnw[g] + lnb[g]
        cols.append(_gelu(ub[g]) * (mm(sw[g], vn) + sb[g]))
    return jnp.concatenate(cols, axis=1)


def _readout_fn(ob, og, hnw):
    r = [o * lax.rsqrt(jnp.mean(o * o, axis=-1, keepdims=True) + RMS_EPS) * hnw for o in ob]
    return jnp.concatenate(r, axis=1) * _silu(og)


def _glu_fn(ac, v):
    return _gelu(ac) * v


def _stream_row(tm):
    n_ctx = CTX // tm
    return lambda i: (jnp.where(i < n_ctx, 0, 1), 0, 0, 0)


def in_proj_fwd(x, mod, nw, wg):
    t = x.shape[0]

    def body(x_ref, mod_ref, nw_ref, w_ref, out_ref):
        h = _norm_mod(x_ref[...], nw_ref[...], mod_ref[0, 0], mod_ref[0, 1]).astype(BF16)
        for j in range(N_DEV):
            out_ref[:, j * IN_SLOT:(j + 1) * IN_SLOT] = jnp.dot(h, w_ref[j], preferred_element_type=F32)

    return pl.pallas_call(
        body, name="in_proj_fwd", grid=(t // TM,),
        in_specs=[pl.BlockSpec((TM, D), lambda i: (i, 0)), pl.BlockSpec((1, 6, 1, D), _stream_row(TM)),
                  pl.BlockSpec((1, D), lambda i: (0, 0)), VMEM_WHOLE],
        out_specs=pl.BlockSpec((TM, D_IN), lambda i: (i, 0)),
        out_shape=jax.ShapeDtypeStruct((t, D_IN), F32), compiler_params=_cp(1))(x, mod, nw, wg)


def _scan_chunk(nc):
    ncc = CTX // CH

    def chunk(d, s):
        bwd = jnp.where(s < ncc, ncc - 1 - s, nc + ncc - 1 - s)
        return jnp.where(d == 0, s, bwd)
    return chunk


def hgrn_fwd(parts, lb, mc, mtc, mrefc):
    t = parts.shape[0]
    nc = t // CH
    chunk = _scan_chunk(nc)

    def body(q_ref, f_ref, i_ref, lb_ref, m_ref, mt_ref, mr_ref, o_ref, ck_ref, st):
        @pl.when(pl.program_id(1) == 0)
        def _():
            st[...] = jnp.zeros_like(st)
        ck_ref[0, 0] = st[...]
        outs, news = _hgrn_chunk([st[h] for h in range(HEADS)], [q_ref[:, _hsl(h)] for h in range(HEADS)],
                                 [f_ref[:, _hsl(h)] for h in range(HEADS)], [i_ref[:, _hsl(h)] for h in range(HEADS)],
                                 [lb_ref[0, :, _hsl(h)] for h in range(HEADS)], m_ref[0], mt_ref[0], mr_ref[0])
        for h in range(HEADS):
            o_ref[0, :, _hsl(h)] = outs[h]
            st[h] = news[h]

    const = lambda d, s: (d, 0, 0)
    return pl.pallas_call(
        body, name="hgrn_fwd", grid=(2, nc),
        in_specs=[pl.BlockSpec((CH, D), lambda d, s: (chunk(d, s), 0)), pl.BlockSpec((CH, D), lambda d, s: (chunk(d, s), 1 + d)),
                  pl.BlockSpec((CH, D), lambda d, s: (chunk(d, s), 3)), pl.BlockSpec((1, 1, D), const),
                  pl.BlockSpec((1, CH, CH), const), pl.BlockSpec((1, CH, CH), const), pl.BlockSpec((1, CH, 1), const)],
        out_specs=[pl.BlockSpec((1, CH, D), lambda d, s: (d, chunk(d, s), 0)),
                   pl.BlockSpec((1, 1, HEADS, HD, HD), lambda d, s: (d, s, 0, 0, 0))],
        out_shape=[jax.ShapeDtypeStruct((2, t, D), F32), jax.ShapeDtypeStruct((2, nc, HEADS, HD, HD), F32)],
        scratch_shapes=[pltpu.VMEM((HEADS, HD, HD), F32)], compiler_params=_cp(2))(parts, parts, parts, lb, mc, mtc, mrefc)


def _mixer_tile(rows, u_ref, v_ref, og_ref, o_ref, lnw_ref, lnb_ref, sw_ref, sb_ref, hnw_ref):
    n = (rows.stop - rows.start) // SGU_CH
    yas, vjps = [], []
    for c in range(n):
        r = slice(rows.start + c * SGU_CH, rows.start + (c + 1) * SGU_CH)
        ya, vjp_a = jax.vjp(_sgu_fn, [u_ref[r, _hsl(g)] for g in range(HEADS)], [v_ref[r, _hsl(g)] for g in range(HEADS)],
                            [lnw_ref[:, _hsl(g)] for g in range(HEADS)], [lnb_ref[:, _hsl(g)] for g in range(HEADS)],
                            [sw_ref[g] for g in range(HEADS)], [sb_ref[g] for g in range(HEADS)])
        yas.append(ya)
        vjps.append(vjp_a)
    yb, vjp_b = jax.vjp(_readout_fn, [o_ref[0, rows, _hsl(h)] + o_ref[1, rows, _hsl(h)] for h in range(HEADS)],
                        og_ref[rows, :], hnw_ref[...])
    return (yas[0] if n == 1 else jnp.concatenate(yas, axis=0)), yb, vjps, vjp_b


def _part_specs(tm, first, n):
    return [pl.BlockSpec((tm, D), functools.partial(lambda k, i: (i, k), first + k)) for k in range(n)]


def mixer_fwd(x, parts, o, mod, lnw, lnb, sw, sb, hnw, wa, wb, wo):
    t = x.shape[0]

    def body(x_ref, u_ref, v_ref, og_ref, ga_ref, gb_ref, o_ref, mod_ref, lnw_ref, lnb_ref, sw_ref, sb_ref, hnw_ref,
             wa_ref, wb_ref, wo_ref, out_ref):
        ya, yb, _, _ = _mixer_tile(slice(0, TM), u_ref, v_ref, og_ref, o_ref, lnw_ref, lnb_ref, sw_ref, sb_ref, hnw_ref)
        merged = (jax.nn.sigmoid(ga_ref[...]) * mm(ya, wa_ref[...]) + jax.nn.sigmoid(gb_ref[...]) * mm(yb, wb_ref[...]))
        out_ref[...] = x_ref[...] + mod_ref[0, 2] * mm(merged, wo_ref[...])

    vec = lambda n: pl.BlockSpec((1, n), lambda i: (0, 0))
    return pl.pallas_call(
        body, name="mixer_fwd", grid=(t // TM,),
        in_specs=[pl.BlockSpec((TM, D), lambda i: (i, 0))] + _part_specs(TM, 4, 5)
        + [pl.BlockSpec((2, TM, D), lambda i: (0, i, 0)), pl.BlockSpec((1, 6, 1, D), _stream_row(TM)), vec(D), vec(D),
           VMEM_WHOLE, VMEM_WHOLE, vec(HD), VMEM_WHOLE, VMEM_WHOLE, VMEM_WHOLE],
        out_specs=pl.BlockSpec((TM, D), lambda i: (i, 0)),
        out_shape=jax.ShapeDtypeStruct((t, D), F32), compiler_params=_cp(1),
    )(x, parts, parts, parts, parts, parts, o, mod, lnw, lnb, sw, sb, hnw, wa, wb, wo)


def ffn_up_fwd(x, mod, nw, wg):
    t = x.shape[0]

    def body(x_ref, mod_ref, nw_ref, w_ref, out_ref):
        h = _norm_mod(x_ref[...], nw_ref[...], mod_ref[0, 3], mod_ref[0, 4]).astype(BF16)
        for j in range(N_DEV):
            out_ref[j] = jnp.dot(h, w_ref[j], preferred_element_type=F32)

    return pl.pallas_call(
        body, name="ffn_up_fwd", grid=(t // TM,),
        in_specs=[pl.BlockSpec((TM, D), lambda i: (i, 0)), pl.BlockSpec((1, 6, 1, D), _stream_row(TM)),
                  pl.BlockSpec((1, D), lambda i: (0, 0)), VMEM_WHOLE],
        out_specs=pl.BlockSpec((N_DEV, TM, FF_SLOT), lambda i: (0, i, 0)),
        out_shape=jax.ShapeDtypeStruct((N_DEV, t, FF_SLOT), F32), compiler_params=_cp(1))(x, mod, nw, wg)


def _halo_specs(nt, k_of, i_of):
    per = TM // GRID_W
    last = nt * per - 1
    return [pl.BlockSpec((1, GRID_W, FF_SLOT), lambda *g: (k_of(*g), jnp.maximum(i_of(*g) * per - 1, 0), 0)),
            pl.BlockSpec((1, TM, FF_SLOT), lambda *g: (k_of(*g), i_of(*g), 0)),
            pl.BlockSpec((1, GRID_W, FF_SLOT), lambda *g: (k_of(*g), jnp.minimum(i_of(*g) * per + per, last), 0))]


def _with_halo(prev_ref, main_ref, next_ref, i, nt):
    prev = jnp.where(i >= 2, prev_ref[0], 0.0)
    nxt = jnp.where((i >= 1) & (i <= nt - 2), next_ref[0], 0.0)
    return jnp.concatenate([prev, main_ref[0], nxt], axis=0)


def _tap_valid(dc, i, n_rows, offset):
    r = lax.broadcasted_iota(jnp.int32, (n_rows, 1), 0) - offset
    col = jnp.bitwise_and(r, GRID_W - 1)
    pos = jnp.where(i == 0, r, col) + dc
    return (pos >= 0) & (pos < jnp.where(i == 0, TM, GRID_W))


def _row_weight(cw_ref, dr, dc, i):
    w = cw_ref[0, 3 * (dr + 1) + dc + 1:3 * (dr + 1) + dc + 2, :]
    return w if dr == 0 else jnp.where(i == 0, 0.0, w)


def ffn_down_fwd(x, av, mod, cw, cb, wd):
    t = x.shape[0]
    nt = t // TM
    ext = TM + 2 * GRID_W

    def body(x_ref, ap_ref, am_ref, an_ref, v_ref, mod_ref, cw_ref, cb_ref, wd_ref, out_ref, ac_ref, y_ref, acc):
        i, k = pl.program_id(0), pl.program_id(1)
        a_ext = _with_halo(ap_ref, am_ref, an_ref, i, nt)
        conv = jnp.zeros((TM, FF_SLOT), F32) + cb_ref[0]
        for dc in (-1, 0, 1):
            rolled = a_ext if dc == 0 else pltpu.roll(a_ext, (-dc) % ext, 0)
            valid = _tap_valid(dc, i, TM, 0)
            for dr in (-1, 0, 1):
                lo = GRID_W + GRID_W * dr
                conv = conv + jnp.where(valid, rolled[lo:lo + TM], 0.0) * _row_weight(cw_ref, dr, dc, i)
        ac_ref[0] = conv
        part = mm(_glu_fn(conv, v_ref[0]), wd_ref[0])

        @pl.when(k == 0)
        def _():
            acc[...] = part

        @pl.when(k > 0)
        def _():
            acc[...] += part

        @pl.when(k == N_FFK - 1)
        def _():
            y_ref[...] = acc[...]
            out_ref[...] = x_ref[...] + mod_ref[0, 5] * acc[...]

    tile = pl.BlockSpec((TM, D), lambda i, k: (i, 0))
    return pl.pallas_call(
        body, name="ffn_down_fwd", grid=(nt, N_FFK),
        in_specs=[tile] + _halo_specs(nt, lambda i, k: k, lambda i, k: i)
        + [pl.BlockSpec((1, TM, FF_SLOT), lambda i, k: (N_FFK + k, i, 0)),
           pl.BlockSpec((1, 6, 1, D), lambda i, k: (jnp.where(i < 1, 0, 1), 0, 0, 0)),
           pl.BlockSpec((1, 9, FF_SLOT), lambda i, k: (k, 0, 0)), pl.BlockSpec((1, 1, FF_SLOT), lambda i, k: (k, 0, 0)),
           pl.BlockSpec((1, FF_SLOT, D), lambda i, k: (k, 0, 0))],
        out_specs=[tile, pl.BlockSpec((1, TM, FF_SLOT), lambda i, k: (k, i, 0)), tile],
        out_shape=[jax.ShapeDtypeStruct((t, D), F32), jax.ShapeDtypeStruct((N_FFK, t, FF_SLOT), F32),
                   jax.ShapeDtypeStruct((t, D), F32)],
        scratch_shapes=[pltpu.VMEM((TM, D), F32)], compiler_params=_cp(2))(x, av, av, av, av, mod, cw, cb, wd)


def loss_fwd_bwd(x, target, fw):
    t = x.shape[0]

    def body(x_ref, t_ref, w_ref, loss_ref, dx_ref, dw_ref):
        i = pl.program_id(0)

        @pl.when(i == 0)
        def _():
            loss_ref[...] = jnp.zeros_like(loss_ref)
            dw_ref[...] = jnp.zeros_like(dw_ref)
            dx_ref[...] = jnp.zeros_like(dx_ref)

        @pl.when(i > 0)
        def _():
            y, vjp = jax.vjp(_rms, x_ref[...], w_ref[...])
            err = y - t_ref[...]
            loss_ref[...] += 0.5 * jnp.sum(jnp.sum(err * err, axis=-1, keepdims=True) / D)
            dx, dw = vjp(err / D)
            dx_ref[...] = dx
            dw_ref[...] += dw

    return pl.pallas_call(
        body, name="loss_fwd_bwd", grid=(t // TM,),
        in_specs=[pl.BlockSpec((TM, D), lambda i: (i, 0)), pl.BlockSpec((TM, D), lambda i: (jnp.maximum(i - 1, 0), 0)),
                  pl.BlockSpec((1, D), lambda i: (0, 0))],
        out_specs=[pl.BlockSpec((8, 128), lambda i: (0, 0)), pl.BlockSpec((TM, D), lambda i: (i, 0)),
                   pl.BlockSpec((1, D), lambda i: (0, 0))],
        out_shape=[jax.ShapeDtypeStruct((8, 128), F32), jax.ShapeDtypeStruct((t, D), F32), jax.ShapeDtypeStruct((1, D), F32)],
        compiler_params=_cp(1))(x, target, fw)


def _stream_add(ref, k, is_ctx, val):
    ref[0, k] += jnp.where(is_ctx, val, 0.0)
    ref[1, k] += jnp.where(is_ctx, 0.0, val)


def ffn_down_bwd(dx, ac, av, y, mod, wd):
    t = dx.shape[0]
    nt = t // TM

    def body(dx_ref, ac_ref, v_ref, y_ref, mod_ref, wd_ref, dav_ref, dac_ref, dwd_ref, dg_ref, acc):
        k, i = pl.program_id(0), pl.program_id(1)

        @pl.when((k == 0) & (i == 0))
        def _():
            dg_ref[...] = jnp.zeros_like(dg_ref)

        @pl.when(i == 0)
        def _():
            acc[...] = jnp.zeros_like(acc)

        @pl.when(k == 0)
        def _():
            _stream_add(dg_ref, 0, i == 0, jnp.sum(dx_ref[...] * y_ref[...], axis=0, keepdims=True))

        dout = mod_ref[0, 5] * dx_ref[...]
        z, vjp = jax.vjp(_glu_fn, ac_ref[0], v_ref[0])
        dac, dv = vjp(mm_nt(dout, wd_ref[0]))
        dac_ref[0] = dac
        dav_ref[0] = dv
        acc[...] += mm_tn(z, dout)

        @pl.when(i == nt - 1)
        def _():
            dwd_ref[0] = acc[...].astype(dwd_ref.dtype)

    tile = pl.BlockSpec((TM, D), lambda k, i: (i, 0))
    return pl.pallas_call(
        body, name="ffn_down_bwd", grid=(N_FFK, nt),
        in_specs=[tile, pl.BlockSpec((1, TM, FF_SLOT), lambda k, i: (k, i, 0)),
                  pl.BlockSpec((1, TM, FF_SLOT), lambda k, i: (N_FFK + k, i, 0)), tile,
                  pl.BlockSpec((1, 6, 1, D), lambda k, i: (jnp.where(i < 1, 0, 1), 0, 0, 0)),
                  pl.BlockSpec((1, FF_SLOT, D), lambda k, i: (k, 0, 0))],
        out_specs=[pl.BlockSpec((1, TM, FF_SLOT), lambda k, i: (N_FFK + k, i, 0)),
                   pl.BlockSpec((1, TM, FF_SLOT), lambda k, i: (k, i, 0)),
                   pl.BlockSpec((1, FF_SLOT, D), lambda k, i: (k, 0, 0)),
                   pl.BlockSpec((2, 1, 1, D), lambda k, i: (0, 0, 0, 0))],
        out_shape=[jax.ShapeDtypeStruct((N_DEV, t, FF_SLOT), F32), jax.ShapeDtypeStruct((N_FFK, t, FF_SLOT), F32),
                   jax.ShapeDtypeStruct((N_FFK, FF_SLOT, D), GRAD_WIRE), jax.ShapeDtypeStruct((2, 1, 1, D), F32)],
        scratch_shapes=[pltpu.VMEM((FF_SLOT, D), F32)], compiler_params=_cp(2))(dx, ac, av, y, mod, wd)


def conv_bwd(dav, dac, av, cw):
    t = dac.shape[1]
    nt = t // TM
    ext = TM + 2 * GRID_W

    def body(dav_in, gp_ref, gm_ref, gn_ref, ap_ref, am_ref, an_ref, cw_ref, dav_ref, dcw_ref, dcb_ref):
        k, i = pl.program_id(0), pl.program_id(1)

        @pl.when(i == 0)
        def _():
            dcw_ref[...] = jnp.zeros_like(dcw_ref)
            dcb_ref[...] = jnp.zeros_like(dcb_ref)

        g_ext = _with_halo(gp_ref, gm_ref, gn_ref, i, nt)
        a_ext = _with_halo(ap_ref, am_ref, an_ref, i, nt)
        g_main = gm_ref[0]
        dcb_ref[0] += jnp.sum(g_main, axis=0, keepdims=True)
        da = jnp.zeros((TM, FF_SLOT), F32)
        for dc in (-1, 0, 1):
            gv = jnp.where(_tap_valid(dc, i, ext, GRID_W), g_ext, 0.0)
            g_rolled = gv if dc == 0 else pltpu.roll(gv, dc % ext, 0)
            a_rolled = a_ext if dc == 0 else pltpu.roll(a_ext, (-dc) % ext, 0)
            g_valid = jnp.where(_tap_valid(dc, i, TM, 0), g_main, 0.0)
            for dr in (-1, 0, 1):
                lo = GRID_W - GRID_W * dr
                da = da + g_rolled[lo:lo + TM] * _row_weight(cw_ref, dr, dc, i)
                lo = GRID_W + GRID_W * dr
                tap = 3 * (dr + 1) + dc + 1
                dw = jnp.sum(g_valid * a_rolled[lo:lo + TM], axis=0, keepdims=True)
                dcw_ref[0, tap:tap + 1, :] += dw if dr == 0 else jnp.where(i == 0, 0.0, dw)
        dav_ref[0] = da

    return pl.pallas_call(
        body, name="conv_bwd", grid=(N_FFK, nt),
        in_specs=[ANY] + _halo_specs(nt, lambda k, i: k, lambda k, i: i) + _halo_specs(nt, lambda k, i: k, lambda k, i: i)
        + [pl.BlockSpec((1, 9, FF_SLOT), lambda k, i: (k, 0, 0))],
        out_specs=[pl.BlockSpec((1, TM, FF_SLOT), lambda k, i: (k, i, 0)), pl.BlockSpec((1, 9, FF_SLOT), lambda k, i: (k, 0, 0)),
                   pl.BlockSpec((1, 1, FF_SLOT), lambda k, i: (k, 0, 0))],
        out_shape=[jax.ShapeDtypeStruct(dav.shape, F32), jax.ShapeDtypeStruct((N_FFK, 9, FF_SLOT), F32),
                   jax.ShapeDtypeStruct((N_FFK, 1, FF_SLOT), F32)],
        input_output_aliases={0: 0}, compiler_params=_cp(2))(dav, dac, dac, dac, av, av, av, cw)


def _norm_mod_bwd(x_ref, nw_ref, mod_ref, k_shift, dh, dx_in, dx_ref, dnw_ref, dmod_ref, is_ctx):
    _, vjp = jax.vjp(_norm_mod, x_ref[...], nw_ref[...], mod_ref[0, k_shift], mod_ref[0, k_shift + 1])
    dx, dnw, dshift, dscale = vjp(dh)
    dx_ref[...] = dx_in + dx
    dnw_ref[...] += dnw
    _stream_add(dmod_ref, 0, is_ctx, dshift)
    _stream_add(dmod_ref, 1, is_ctx, dscale)


def ffn_up_bwd_x(dx2, x, dav, mod, nw, wg):
    t = x.shape[0]

    def body(dx2_ref, x_ref, dav_ref, mod_ref, nw_ref, w_ref, dx_ref, dnw_ref, dmod_ref):
        i = pl.program_id(0)

        @pl.when(i == 0)
        def _():
            dnw_ref[...] = jnp.zeros_like(dnw_ref)
            dmod_ref[...] = jnp.zeros_like(dmod_ref)

        dh = mm_nt(dav_ref[0], w_ref[0])
        for j in range(1, N_DEV):
            dh = dh + mm_nt(dav_ref[j], w_ref[j])
        _norm_mod_bwd(x_ref, nw_ref, mod_ref, 3, dh, dx2_ref[...], dx_ref, dnw_ref, dmod_ref, i == 0)

    tile = pl.BlockSpec((TM, D), lambda i: (i, 0))
    return pl.pallas_call(
        body, name="ffn_up_bwd_x", grid=(t // TM,),
        in_specs=[tile, tile, pl.BlockSpec((N_DEV, TM, FF_SLOT), lambda i: (0, i, 0)), pl.BlockSpec((1, 6, 1, D), _stream_row(TM)),
                  pl.BlockSpec((1, D), lambda i: (0, 0)), VMEM_WHOLE],
        out_specs=[tile, pl.BlockSpec((1, D), lambda i: (0, 0)), pl.BlockSpec((2, 2, 1, D), lambda i: (0, 0, 0, 0))],
        out_shape=[jax.ShapeDtypeStruct((t, D), F32), jax.ShapeDtypeStruct((1, D), F32), jax.ShapeDtypeStruct((2, 2, 1, D), F32)],
        compiler_params=_cp(1))(dx2, x, dav, mod, nw, wg)


def proj_bwd_w(x, dout, mod, nw, k_shift, slot, name):
    t = x.shape[0]
    stacked = dout.ndim == 3

    nt = t // TM

    def body(x_ref, d_ref, mod_ref, nw_ref, dw_ref, acc):
        i = pl.program_id(1)

        @pl.when(i == 0)
        def _():
            acc[...] = jnp.zeros_like(acc)
        h = _norm_mod(x_ref[...], nw_ref[...], mod_ref[0, k_shift], mod_ref[0, k_shift + 1])
        acc[...] += mm_tn(h, d_ref[0] if stacked else d_ref[...])

        @pl.when(i == nt - 1)
        def _():
            dw_ref[0] = acc[...].astype(dw_ref.dtype)

    d_spec = (pl.BlockSpec((1, TM, slot), lambda j, i: (j, i, 0)) if stacked else pl.BlockSpec((TM, slot), lambda j, i: (i, j)))
    return pl.pallas_call(
        body, name=name, grid=(N_DEV, nt),
        in_specs=[pl.BlockSpec((TM, D), lambda j, i: (i, 0)), d_spec,
                  pl.BlockSpec((1, 6, 1, D), lambda j, i: (jnp.where(i < 1, 0, 1), 0, 0, 0)), pl.BlockSpec((1, D), lambda j, i: (0, 0))],
        out_specs=pl.BlockSpec((1, D, slot), lambda j, i: (j, 0, 0)),
        out_shape=jax.ShapeDtypeStruct((N_DEV, D, slot), GRAD_WIRE), scratch_shapes=[pltpu.VMEM((D, slot), F32)],
        compiler_params=_cp(2))(x, dout, mod, nw)


def mixer_bwd(dx, parts, o, mod, lnw, lnb, sw, sb, hnw, wa, wb, wo):
    t = dx.shape[0]
    tm = TM_SMALL
    n_ctx = CTX // tm

    def body(dx_ref, u_ref, v_ref, og_ref, ga_ref, gb_ref, o_ref, mod_ref, lnw_ref, lnb_ref, sw_ref, sb_ref, hnw_ref,
             wa_ref, wb_ref, wo_ref, d5_ref, do_ref, dwa_out, dwb_out, dwo_out, dlnw_ref, dlnb_ref, dsw_ref, dsb_ref,
             dhnw_ref, dg_ref, dwa_ref, dwb_ref, dwo_ref):
        i = pl.program_id(0)

        @pl.when(i == 0)
        def _():
            for r in (dwa_ref, dwb_ref, dwo_ref, dlnw_ref, dlnb_ref, dsw_ref, dsb_ref, dhnw_ref, dg_ref):
                r[...] = jnp.zeros_like(r)

        ya, yb, vjps, vjp_b = _mixer_tile(slice(0, tm), u_ref, v_ref, og_ref, o_ref, lnw_ref, lnb_ref, sw_ref, sb_ref, hnw_ref)
        pa, pb = mm(ya, wa_ref[...]), mm(yb, wb_ref[...])
        sa, sbg = jax.nn.sigmoid(ga_ref[...]), jax.nn.sigmoid(gb_ref[...])
        merged = sa * pa + sbg * pb
        dxv = dx_ref[...]
        _stream_add(dg_ref, 0, i < n_ctx, jnp.sum(dxv * mm(merged, wo_ref[...]), axis=0, keepdims=True))
        dy = mod_ref[0, 2] * dxv
        dmerged = mm_nt(dy, wo_ref[...])
        dwo_ref[...] += mm_tn(merged, dy)
        dpa, dpb = sa * dmerged, sbg * dmerged
        d5_ref[:, 3 * D:4 * D] = dmerged * pa * sa * (1.0 - sa)
        d5_ref[:, 4 * D:5 * D] = dmerged * pb * sbg * (1.0 - sbg)
        dwa_ref[...] += mm_tn(ya, dpa)
        dwb_ref[...] += mm_tn(yb, dpb)
        dub, dvb, dlnw, dlnb, dsw, dsb = vjps[0](mm_nt(dpa, wa_ref[...]))
        dob, dog, dhnw = vjp_b(mm_nt(dpb, wb_ref[...]))
        d5_ref[:, 2 * D:3 * D] = dog
        dhnw_ref[...] += dhnw
        for g in range(HEADS):
            d5_ref[:, g * HD:(g + 1) * HD] = dub[g]
            d5_ref[:, D + g * HD:D + (g + 1) * HD] = dvb[g]
            do_ref[:, _hsl(g)] = dob[g]
            dlnw_ref[:, _hsl(g)] += dlnw[g]
            dlnb_ref[:, _hsl(g)] += dlnb[g]
            dsw_ref[g] += dsw[g]
            dsb_ref[g] += dsb[g]

        @pl.when(i == t // tm - 1)
        def _():
            for acc, out in ((dwa_ref, dwa_out), (dwb_ref, dwb_out), (dwo_ref, dwo_out)):
                out[...] = acc[...].astype(out.dtype)

    vec = lambda n: pl.BlockSpec((1, n), lambda i: (0, 0))
    tile = pl.BlockSpec((tm, D), lambda i: (i, 0))
    sds = jax.ShapeDtypeStruct
    return pl.pallas_call(
        body, name="mixer_bwd", grid=(t // tm,),
        in_specs=[tile] + _part_specs(tm, 4, 5)
        + [pl.BlockSpec((2, tm, D), lambda i: (0, i, 0)), pl.BlockSpec((1, 6, 1, D), _stream_row(tm)), vec(D), vec(D),
           VMEM_WHOLE, VMEM_WHOLE, vec(HD), VMEM_WHOLE, VMEM_WHOLE, VMEM_WHOLE],
        out_specs=[pl.BlockSpec((tm, 5 * D), lambda i: (i, 0)), tile, VMEM_WHOLE, VMEM_WHOLE, VMEM_WHOLE, vec(D), vec(D),
                   VMEM_WHOLE, VMEM_WHOLE, vec(HD), pl.BlockSpec((2, 1, 1, D), lambda i: (0, 0, 0, 0))],
        out_shape=[sds((t, 5 * D), F32), sds((t, D), F32), sds((D, D), GRAD_WIRE), sds((D, D), GRAD_WIRE), sds((D, D), GRAD_WIRE),
                   sds((1, D), F32), sds((1, D), F32), sds((HEADS, SGU_CH, SGU_CH), F32), sds((HEADS, SGU_CH, 1), F32),
                   sds((1, HD), F32), sds((2, 1, 1, D), F32)],
        scratch_shapes=[pltpu.VMEM((D, D), F32)] * 3,
        compiler_params=_cp(1))(dx, parts, parts, parts, parts, parts, o, mod, lnw, lnb, sw, sb, hnw, wa, wb, wo)


def hgrn_bwd(parts, lb, mc, mtc, mrefc, ck, do):
    t = parts.shape[0]
    nc = t // CH
    chunk = _scan_chunk(nc)
    rev = lambda d, s: chunk(d, nc - 1 - s)

    def body(q_ref, f_ref, i_ref, lb_ref, m_ref, mt_ref, mr_ref, ck_ref, do_ref, dq_ref, df_ref, di_ref, dlb_ref, dst):
        @pl.when(pl.program_id(1) == 0)
        def _():
            dst[...] = jnp.zeros_like(dst)
            dlb_ref[...] = jnp.zeros_like(dlb_ref)

        heads = range(HEADS)
        fn = functools.partial(_hgrn_chunk, m=m_ref[0], mt=mt_ref[0], mref=mr_ref[0])
        _, vjp = jax.vjp(fn, [ck_ref[0, 0, h] for h in heads], [q_ref[:, _hsl(h)] for h in heads],
                         [f_ref[:, _hsl(h)] for h in heads], [i_ref[:, _hsl(h)] for h in heads],
                         [lb_ref[0, :, _hsl(h)] for h in heads])
        dstl, dq, df, di, dlb = vjp(([do_ref[:, _hsl(h)] for h in heads], [dst[h] for h in heads]))
        for h in heads:
            dst[h] = dstl[h]
            dq_ref[0, :, _hsl(h)] = dq[h]
            df_ref[0, :, _hsl(h)] = df[h]
            di_ref[0, :, _hsl(h)] = di[h]
            dlb_ref[0, :, _hsl(h)] += dlb[h]

    const = lambda d, s: (d, 0, 0)
    out = pl.BlockSpec((1, CH, D), lambda d, s: (d, rev(d, s), 0))
    return pl.pallas_call(
        body, name="hgrn_bwd", grid=(2, nc),
        in_specs=[pl.BlockSpec((CH, D), lambda d, s: (rev(d, s), 0)), pl.BlockSpec((CH, D), lambda d, s: (rev(d, s), 1 + d)),
                  pl.BlockSpec((CH, D), lambda d, s: (rev(d, s), 3)), pl.BlockSpec((1, 1, D), const),
                  pl.BlockSpec((1, CH, CH), const), pl.BlockSpec((1, CH, CH), const), pl.BlockSpec((1, CH, 1), const),
                  pl.BlockSpec((1, 1, HEADS, HD, HD), lambda d, s: (d, nc - 1 - s, 0, 0, 0)),
                  pl.BlockSpec((CH, D), lambda d, s: (rev(d, s), 0))],
        out_specs=[out, out, out, pl.BlockSpec((1, 1, D), const)],
        out_shape=[jax.ShapeDtypeStruct((2, t, D), F32)] * 3 + [jax.ShapeDtypeStruct((2, 1, D), F32)],
        scratch_shapes=[pltpu.VMEM((HEADS, HD, HD), F32)], compiler_params=_cp(2),
    )(parts, parts, parts, lb, mc, mtc, mrefc, ck, do)


def in_proj_bwd_x(dx1, x, dq, df, di, d5, mod, nw, wg):
    t = x.shape[0]
    tm = TM_SMALL
    n_ctx = CTX // tm

    def body(dx1_ref, x_ref, dq_ref, df_ref, di_ref, d5_ref, mod_ref, nw_ref, w_ref, dx_ref, dp_ref, dnw_ref, dmod_ref):
        i = pl.program_id(0)

        @pl.when(i == 0)
        def _():
            dnw_ref[...] = jnp.zeros_like(dnw_ref)
            dmod_ref[...] = jnp.zeros_like(dmod_ref)

        dp_ref[:, 0:D] = dq_ref[0] + dq_ref[1]
        dp_ref[:, D:2 * D] = df_ref[0]
        dp_ref[:, 2 * D:3 * D] = df_ref[1]
        dp_ref[:, 3 * D:4 * D] = di_ref[0] + di_ref[1]
        dp_ref[:, 4 * D:] = d5_ref[...]
        dh = mm_nt(dp_ref[:, 0:IN_SLOT], w_ref[0])
        for j in range(1, N_DEV):
            dh = dh + mm_nt(dp_ref[:, j * IN_SLOT:(j + 1) * IN_SLOT], w_ref[j])
        _norm_mod_bwd(x_ref, nw_ref, mod_ref, 0, dh, dx1_ref[...], dx_ref, dnw_ref, dmod_ref, i < n_ctx)

    tile = pl.BlockSpec((tm, D), lambda i: (i, 0))
    pair = pl.BlockSpec((2, tm, D), lambda i: (0, i, 0))
    return pl.pallas_call(
        body, name="in_proj_bwd_x", grid=(t // tm,),
        in_specs=[tile, tile, pair, pair, pair, pl.BlockSpec((tm, 5 * D), lambda i: (i, 0)),
                  pl.BlockSpec((1, 6, 1, D), _stream_row(tm)), pl.BlockSpec((1, D), lambda i: (0, 0)), VMEM_WHOLE],
        out_specs=[tile, pl.BlockSpec((tm, D_IN), lambda i: (i, 0)), pl.BlockSpec((1, D), lambda i: (0, 0)),
                   pl.BlockSpec((2, 2, 1, D), lambda i: (0, 0, 0, 0))],
        out_shape=[jax.ShapeDtypeStruct((t, D), F32), jax.ShapeDtypeStruct((t, D_IN), F32), jax.ShapeDtypeStruct((1, D), F32),
                   jax.ShapeDtypeStruct((2, 2, 1, D), F32)],
        compiler_params=_cp(1))(dx1, x, dq, df, di, d5, mod, nw, wg)


def _lb_fn(h0, h1):
    m = jnp.maximum(h0, h1)
    e0, e1 = jnp.exp(h0 - m), jnp.exp(h1 - m)
    return e1 / (e0 + e1)


def lower_bounds(hlb):
    def body(h_ref, out_ref):
        out_ref[...] = _lb_fn(h_ref[0:1, :], h_ref[1:2, :])
    return pl.pallas_call(body, name="lower_bounds", out_shape=jax.ShapeDtypeStruct((1, 2 * D), F32))(hlb)


def lower_bounds_bwd(hlb, dlb1):
    def body(h_ref, d_ref, out_ref):
        _, vjp = jax.vjp(_lb_fn, h_ref[0:1, :], h_ref[1:2, :])
        d0, d1 = vjp(d_ref[...])
        out_ref[0:1, :] = d0
        out_ref[1:2, :] = d1
    return pl.pallas_call(body, name="lower_bounds_bwd", out_shape=jax.ShapeDtypeStruct((2, 2 * D), F32))(hlb, dlb1)


def _ada_fn(c_all, cctx8, w, b):
    dot = lambda a, l: jnp.dot(_silu(a), w[l], precision=HIGHEST, preferred_element_type=F32) + b[l]
    return [dot(c_all, l) for l in range(2)], [dot(cctx8, l) for l in range(2)]


def ada_fwd(c_all, cctx8, w, b):
    cols = w.shape[-1]

    def body(c_ref, cc_ref, w_ref, b_ref, out_ref):
        ox, oc = _ada_fn(c_ref[...], cc_ref[...], [w_ref[0], w_ref[1]], [b_ref[0], b_ref[1]])
        for l in range(2):
            out_ref[l, 0] = ox[l]
            out_ref[l, 1] = oc[l]
    return pl.pallas_call(body, name="ada_fwd", out_shape=jax.ShapeDtypeStruct((2, 2, N_DEV, cols), F32),
                          compiler_params=_cp(0))(c_all, cctx8, w, b)


def ada_bwd(c_all, cctx8, w, b, dmx, dmc):
    cols = w.shape[-1]

    def body(c_ref, cc_ref, w_ref, b_ref, dmx_ref, dmc_ref, dw_ref, dc_ref):
        fn = lambda cc, w0, w1: _ada_fn(c_ref[...], cc, [w0, w1], [b_ref[0], b_ref[1]])
        _, vjp = jax.vjp(fn, cc_ref[...], w_ref[0], w_ref[1])
        dcc, dw0, dw1 = vjp(([dmx_ref[0], dmx_ref[1]], [dmc_ref[0], dmc_ref[1]]))
        dw_ref[0] = dw0
        dw_ref[1] = dw1
        dc_ref[...] = jnp.sum(dcc, axis=0, keepdims=True)
    return pl.pallas_call(body, name="ada_bwd", out_shape=[jax.ShapeDtypeStruct((2, D, cols), F32), jax.ShapeDtypeStruct((1, D), F32)],
                          compiler_params=_cp(0))(c_all, cctx8, w, b, dmx, dmc)


def adamw(w, m, v, gparts, name):
    r, c = w.shape
    p = gparts.shape[0]
    rt = r
    while rt % 16 == 0 and (p + 7) * rt * c * 4 * 2 > 24 * 2 ** 20:
        rt //= 2

    def body(w_ref, m_ref, v_ref, g_ref, go_ref, d_ref, mo_ref, vo_ref):
        g = g_ref[0].astype(F32)
        for k in range(1, p):
            g = g + g_ref[k].astype(F32)
        m2 = ADAM_B1 * m_ref[...] + (1.0 - ADAM_B1) * g
        v2 = ADAM_B2 * v_ref[...] + (1.0 - ADAM_B2) * (g * g)
        m_hat = m2 / (1.0 - ADAM_B1 ** ADAM_STEP)
        v_hat = v2 / (1.0 - ADAM_B2 ** ADAM_STEP)
        go_ref[...] = g
        d_ref[...] = -ADAM_LR * (m_hat / (jnp.sqrt(v_hat) + ADAM_EPS) + ADAM_WD * w_ref[...])
        mo_ref[...] = m2
        vo_ref[...] = v2

    tile = pl.BlockSpec((rt, c), lambda i: (i, 0))
    return pl.pallas_call(
        body, name=name, grid=(r // rt,),
        in_specs=[tile, tile, tile, pl.BlockSpec((p, rt, c), lambda i: (0, i, 0))], out_specs=[tile] * 4,
        out_shape=[jax.ShapeDtypeStruct((r, c), F32)] * 4, compiler_params=_cp(1))(w, m, v, gparts)


def _me():
    x, y, c = lax.axis_index("x"), lax.axis_index("y"), lax.axis_index("c")
    return x, y, c, 4 * x + 2 * y + c


def _peer(x, y, c, p):
    fx, fy, fc = (p >> 2) & 1, (p >> 1) & 1, p & 1
    return (1 - x if fx else x, 1 - y if fy else y, 1 - c if fc else c)


def all_gather(arrs, name):
    n = len(arrs)

    def body(*refs):
        ins, outs = refs[:n], refs[n:2 * n]
        send, recv, local = refs[2 * n:]
        x, y, c, me = _me()
        copies = []
        for a in range(n):
            lc = pltpu.make_async_copy(ins[a], outs[a].at[me], local.at[a])
            lc.start()
            copies.append(lc)
            for p in range(1, N_DEV):
                cp = pltpu.make_async_remote_copy(src_ref=ins[a], dst_ref=outs[a].at[me], send_sem=send.at[a, p - 1],
                                                  recv_sem=recv.at[a, p - 1], device_id=_peer(x, y, c, p),
                                                  device_id_type=pl.DeviceIdType.MESH)
                cp.start()
                copies.append(cp)
        for cp in copies:
            cp.wait()

    return pl.pallas_call(
        body, name=name, in_specs=[ANY] * n, out_specs=[ANY] * n,
        out_shape=[jax.ShapeDtypeStruct((N_DEV,) + a.shape, a.dtype) for a in arrs],
        scratch_shapes=[pltpu.SemaphoreType.DMA((n, N_DEV - 1)), pltpu.SemaphoreType.DMA((n, N_DEV - 1)),
                        pltpu.SemaphoreType.DMA((n,))])(*arrs)


HBM = pl.BlockSpec(memory_space=pltpu.HBM)
SEM = pl.BlockSpec(memory_space=pltpu.SEMAPHORE)


def _in_hbm(a):
    return pltpu.with_memory_space_constraint(a, pltpu.HBM)


def _exchange_refs(srcs, lands, layer, scatter, a, x, y, c, p):
    me = 4 * x + 2 * y + c
    px, py, pc = _peer(x, y, c, p) if p else (x, y, c)
    dst = lands[a].at[me] if layer is None else lands[a].at[me, layer]
    src = srcs[a].at[4 * px + 2 * py + pc] if scatter else dst
    return src, dst, (px, py, pc)


def exchange_start(srcs, lands, layer, scatter, name):
    n, ns = len(lands), len(srcs)

    def body(*refs):
        ins, lz = refs[:ns], refs[ns:ns + n]
        send, recv = refs[ns + n], refs[ns + n + 1]
        token = refs[-1]
        x, y, c, _ = _me()
        for a in range(n):
            for p in range(1, N_DEV):
                src, dst, peer = _exchange_refs(ins, lz, layer, scatter, a, x, y, c, p)
                k = a * (N_DEV - 1) + p - 1
                pltpu.make_async_remote_copy(src_ref=src, dst_ref=dst, send_sem=send.at[k], recv_sem=recv.at[k],
                                             device_id=peer, device_id_type=pl.DeviceIdType.MESH).start()
        token[...] = jnp.zeros_like(token)

    thru = [pltpu.HBM(a.shape, a.dtype) for a in list(srcs) + list(lands)]
    out = pl.pallas_call(
        body, name=name, in_specs=[HBM] * (ns + n),
        out_specs=[SEM, SEM] + [HBM] * (ns + n) + [pl.BlockSpec(memory_space=pltpu.VMEM)],
        out_shape=[pltpu.SemaphoreType.DMA((n * (N_DEV - 1),)), pltpu.SemaphoreType.DMA((n * (N_DEV - 1),))] + thru
        + [jax.ShapeDtypeStruct((8, 128), F32)],
        input_output_aliases={i: 2 + i for i in range(ns + n)},
        compiler_params=pltpu.CompilerParams(has_side_effects=pltpu.SideEffectType.DATAFLOW_SIDE_EFFECTING),
    )(*[_in_hbm(a) for a in list(srcs) + list(lands)])
    return out[0], out[1], out[2:2 + ns], out[2 + ns:2 + ns + n], out[-1]


def exchange_wait(send, recv, srcs, lands, layer, scatter, after, name):
    n, ns = len(lands), len(srcs)

    def body(*refs):
        ins, lz = refs[:ns], refs[ns:ns + n]
        send_ref, recv_ref = refs[ns + n], refs[ns + n + 1]
        x, y, c, _ = _me()
        for a in range(n):
            for p in range(1, N_DEV):
                src, dst, peer = _exchange_refs(ins, lz, layer, scatter, a, x, y, c, 0)
                k = a * (N_DEV - 1) + p - 1
                cp = pltpu.make_async_remote_copy(src_ref=src, dst_ref=dst, send_sem=send_ref.at[k],
                                                  recv_sem=recv_ref.at[k], device_id=peer,
                                                  device_id_type=pl.DeviceIdType.MESH)
                cp.wait_send()
                cp.wait_recv()

    thru = [pltpu.HBM(a.shape, a.dtype) for a in list(srcs) + list(lands)]
    out = pl.pallas_call(
        body, name=name, in_specs=[HBM] * (ns + n) + [SEM, SEM, ANY], out_specs=[HBM] * (ns + n), out_shape=thru,
        input_output_aliases={i: i for i in range(ns + n)},
        compiler_params=pltpu.CompilerParams(has_side_effects=pltpu.SideEffectType.DATAFLOW_SIDE_EFFECTING),
    )(*srcs, *lands, send, recv, after)
    return out[ns:]


def place_own(src, land, me, layer, scatter, name):
    create = isinstance(land, jax.ShapeDtypeStruct)
    r, c = src.shape[-2:]
    rt = r
    while rt % 32 == 0 and rt * c * 4 > 2 ** 21:
        rt //= 2

    def body(me_ref, src_ref, *rest):
        out_ref = rest[-1]
        out_ref[...] = src_ref[...].reshape(out_ref.shape).astype(out_ref.dtype)

    src_spec = (pl.BlockSpec((1, rt, c), lambda i, m: (m[0], i, 0)) if scatter else pl.BlockSpec((rt, c), lambda i, m: (i, 0)))
    out_spec = (pl.BlockSpec((1, rt, c), lambda i, m: (m[0], i, 0)) if layer is None
                else pl.BlockSpec((1, 1, rt, c), lambda i, m: (m[0], layer, i, 0)))
    grid_spec = pltpu.PrefetchScalarGridSpec(num_scalar_prefetch=1, grid=(r // rt,),
                                             in_specs=[src_spec] + ([] if create else [ANY]), out_specs=out_spec)
    return pl.pallas_call(body, name=name, grid_spec=grid_spec, out_shape=jax.ShapeDtypeStruct(land.shape, land.dtype),
                          input_output_aliases={} if create else {2: 0}, compiler_params=_cp(1),
                          )(*((me, src) if create else (me, src, land)))


def _scan_constants():
    r = lax.broadcasted_iota(jnp.int32, (CH, CH), 0)
    s = lax.broadcasted_iota(jnp.int32, (CH, CH), 1)
    lower = (s <= r).astype(F32)
    t = jnp.arange(CH)[:, None]
    mc = jnp.stack([lower, lower.T])
    mref = jnp.stack([(t <= CH // 2 - 1).astype(F32), (t >= CH // 2).astype(F32)])
    return mc, jnp.stack([lower.T, lower]), mref


def local_step(x, ctx, target, mod, lb, w, fetch=None, publish=None, small_ready=None):
    kept = {}

    def keep(l, part, grads):
        kept[(l, part)] = grads
        return 0.0

    fetch = fetch or (lambda l, part, after: w)
    publish = publish or keep
    n_layers = len(mod)
    mc, mtc, mrefc = _scan_constants()
    xs = jnp.concatenate([ctx, x], axis=0)
    saved, big = [], []
    for l in range(n_layers):
        wl = dict(fetch(l, "in", xs))
        parts = in_proj_fwd(xs, mod[l], w["nw1"][l], wl["win"][l])
        o, ck = hgrn_fwd(parts, lb[l], mc, mtc, mrefc)
        wl.update(fetch(l, "rest", o))
        x1 = mixer_fwd(xs, parts, o, mod[l], w["lnw"][l], w["lnb"][l], w["sw"][l], w["sb"][l], w["hnw"][l],
                       wl["wa"][l], wl["wb"][l], wl["wo"][l])
        av = ffn_up_fwd(x1, mod[l], w["nw2"][l], wl["wup"][l])
        x2, ac, y = ffn_down_fwd(x1, av, mod[l], w["cw"][l], w["cb"][l], wl["wd"][l])
        saved.append((xs, parts, o, ck, x1, av, ac, y))
        big.append(wl)
        xs = x2
    loss, dx, dfw = loss_fwd_bwd(xs, target, w["fw"])
    g = {k: [None] * n_layers for k in ("nw1", "nw2", "lnw", "lnb", "sw", "sb", "hnw", "cw", "cb")}
    g["fw"] = dfw
    dmod, dlb = [None] * n_layers, [None] * n_layers
    tok = 0.0
    for l in reversed(range(n_layers)):
        x0, parts, o, ck, x1, av, ac, y = saved[l]
        wl = big[l]
        dav, dac, dwd, dg2 = ffn_down_bwd(dx, ac, av, y, mod[l] + tok, wl["wd"][l])
        dav, g["cw"][l], g["cb"][l] = conv_bwd(dav, dac, av, w["cw"][l])
        dx1, g["nw2"][l], dmod2 = ffn_up_bwd_x(dx, x1, dav, mod[l], w["nw2"][l], wl["wup"][l])
        dwup = proj_bwd_w(x1, dav, mod[l], w["nw2"][l], 3, FF_SLOT, "ffn_up_bwd_w")
        tok = publish(l, "ffn", {"wd": dwd, "wup": dwup})
        (d5, do, dwa, dwb, dwo, g["lnw"][l], g["lnb"][l], g["sw"][l], g["sb"][l], g["hnw"][l],
         dg1) = mixer_bwd(dx1, parts, o, mod[l] + tok, w["lnw"][l], w["lnb"][l], w["sw"][l], w["sb"][l], w["hnw"][l],
                          wl["wa"][l], wl["wb"][l], wl["wo"][l])
        tok = publish(l, "mix", {"wa": dwa, "wb": dwb, "wo": dwo})
        dq, df, di, dlb[l] = hgrn_bwd(parts, lb[l] + tok, mc, mtc, mrefc, ck, do)
        dx, dparts, g["nw1"][l], dmod1 = in_proj_bwd_x(dx1, x0, dq, df, di, d5, mod[l], w["nw1"][l], wl["win"][l])
        dmod[l] = jnp.concatenate([dmod1, dg1, dmod2, dg2], axis=1)
        tok = small_ready(g, dmod, dlb) if l == 0 and small_ready else 0.0
        dwin = proj_bwd_w(x0, dparts, mod[l] + tok, w["nw1"][l], 0, IN_SLOT, "in_proj_bwd_w")
        tok = publish(l, "in", {"win": dwin})
    for (l, part), grads in kept.items():
        for k, v in grads.items():
            g.setdefault(k, [None] * n_layers)[l] = v
    return loss[0, 0], dx[CTX:], g, dmod, dlb


ROW = 1024
REPLICATED = ("norm1_w", "sgu_ln_w", "sgu_ln_b", "sgu_w", "sgu_b", "hgrn_lower_bounds", "hgrn_norm_w", "norm2_w",
              "ffn_conv_b", "final_norm_w")
WEIGHT_ORDER = ("c_ctx", "ada_w", "ada_b", "norm1_w", "w_in", "sgu_ln_w", "sgu_ln_b", "sgu_w", "sgu_b", "hgrn_lower_bounds",
                "hgrn_norm_w", "w_branch_a", "w_branch_b", "w_out", "norm2_w", "ffn_w_up", "ffn_conv_w", "ffn_conv_b",
                "ffn_w_down", "final_norm_w")


def _rows_of(n):
    return -(-n // (8 * ROW)) * 8


def _pack(arrs, total_rows=None):
    parts = []
    for a in arrs:
        flat = a.reshape(-1).astype(F32)
        rows = _rows_of(flat.shape[0])
        parts.append(jnp.pad(flat, (0, rows * ROW - flat.shape[0])).reshape(rows, ROW))
    have = sum(p.shape[0] for p in parts)
    if total_rows is not None and total_rows > have:
        parts.append(jnp.zeros((total_rows - have, ROW), F32))
    return jnp.concatenate(parts, axis=0)


def _unpack(packed, shapes):
    lead = packed.shape[:-2]
    out, r0 = [], 0
    for s in shapes:
        n = math.prod(s)
        rows = _rows_of(n)
        out.append(packed[..., r0:r0 + rows, :].reshape(lead + (rows * ROW,))[..., :n].reshape(lead + tuple(s)))
        r0 += rows
    return out


def kernel(x, c, ctx, c_ctx, ada_w, ada_b, norm1_w, w_in, sgu_ln_w, sgu_ln_b, sgu_w, sgu_b, hgrn_lower_bounds, hgrn_norm_w, w_branch_a, w_branch_b, w_out, norm2_w, ffn_w_up, ffn_conv_w, ffn_conv_b, ffn_w_down, final_norm_w, loss_target, m_c_ctx, m_ada_w, m_ada_b, m_norm1_w, m_w_in, m_sgu_ln_w, m_sgu_ln_b, m_sgu_w, m_sgu_b, m_hgrn_lower_bounds, m_hgrn_norm_w, m_w_branch_a, m_w_branch_b, m_w_out, m_norm2_w, m_ffn_w_up, m_ffn_conv_w, m_ffn_conv_b, m_ffn_w_down, m_final_norm_w, v_c_ctx, v_ada_w, v_ada_b, v_norm1_w, v_w_in, v_sgu_ln_w, v_sgu_ln_b, v_sgu_w, v_sgu_b, v_hgrn_lower_bounds, v_hgrn_norm_w, v_w_branch_a, v_w_branch_b, v_w_out, v_norm2_w, v_ffn_w_up, v_ffn_conv_w, v_ffn_conv_b, v_ffn_w_down, v_final_norm_w):
    wts = dict(c_ctx=c_ctx, ada_w=ada_w, ada_b=ada_b, norm1_w=norm1_w, w_in=w_in, sgu_ln_w=sgu_ln_w, sgu_ln_b=sgu_ln_b,
               sgu_w=sgu_w, sgu_b=sgu_b, hgrn_lower_bounds=hgrn_lower_bounds, hgrn_norm_w=hgrn_norm_w, w_branch_a=w_branch_a,
               w_branch_b=w_branch_b, w_out=w_out, norm2_w=norm2_w, ffn_w_up=ffn_w_up, ffn_conv_w=ffn_conv_w,
               ffn_conv_b=ffn_conv_b, ffn_w_down=ffn_w_down, final_norm_w=final_norm_w)
    mom1 = dict(c_ctx=m_c_ctx, ada_w=m_ada_w, ada_b=m_ada_b, norm1_w=m_norm1_w, w_in=m_w_in, sgu_ln_w=m_sgu_ln_w,
                sgu_ln_b=m_sgu_ln_b, sgu_w=m_sgu_w, sgu_b=m_sgu_b, hgrn_lower_bounds=m_hgrn_lower_bounds,
                hgrn_norm_w=m_hgrn_norm_w, w_branch_a=m_w_branch_a, w_branch_b=m_w_branch_b, w_out=m_w_out, norm2_w=m_norm2_w,
                ffn_w_up=m_ffn_w_up, ffn_conv_w=m_ffn_conv_w, ffn_conv_b=m_ffn_conv_b, ffn_w_down=m_ffn_w_down,
                final_norm_w=m_final_norm_w)
    mom2 = dict(c_ctx=v_c_ctx, ada_w=v_ada_w, ada_b=v_ada_b, norm1_w=v_norm1_w, w_in=v_w_in, sgu_ln_w=v_sgu_ln_w,
                sgu_ln_b=v_sgu_ln_b, sgu_w=v_sgu_w, sgu_b=v_sgu_b, hgrn_lower_bounds=v_hgrn_lower_bounds,
                hgrn_norm_w=v_hgrn_norm_w, w_branch_a=v_w_branch_a, w_branch_b=v_w_branch_b, w_out=v_w_out, norm2_w=v_norm2_w,
                ffn_w_up=v_ffn_w_up, ffn_conv_w=v_ffn_conv_w, ffn_conv_b=v_ffn_conv_b, ffn_w_down=v_ffn_w_down,
                final_norm_w=v_final_norm_w)
    n_layers = w_in.shape[0]
    layers = range(n_layers)
    me = 4 * lax.axis_index("x") + 2 * lax.axis_index("y") + lax.axis_index("c")
    ada_cols = ada_w.shape[-1]

    big = ("w_in", "ffn_w_up", "w_branch_a", "w_branch_b", "w_out", "ffn_w_down")
    short = {"w_in": "win", "ffn_w_up": "wup", "w_branch_a": "wa", "w_branch_b": "wb", "w_out": "wo", "ffn_w_down": "wd"}
    me1 = me.reshape(1).astype(jnp.int32)
    groups = [[("w_in", 0)], [(k, 0) for k in big[1:]], [(k, 1) for k in big]]
    in_flight, started = [], 0.0
    for n, group in enumerate(groups):
        lands = [place_own(wts[k][l], jax.ShapeDtypeStruct((N_DEV,) + wts[k].shape[1:], BF16), me1, None, False,
                           f"gather_own_{short[k]}_{l}") for k, l in group]
        in_flight.append(exchange_start([], lands, None, False, f"gather_weights_start_{n}"))
        started = started + in_flight[-1][-1][0, 0]

    def as_used(k, a):
        return a if k in ("w_in", "ffn_w_up") else a.reshape(N_FFK, FF_SLOT, D) if k == "ffn_w_down" else a.reshape(D, D)

    arrived = {}

    def fetch(l, part, after):
        n = {(0, "in"): 0, (0, "rest"): 1, (1, "in"): 2}.get((l, part))
        if n is not None:
            send, recv, _, lands, _ = in_flight[n]
            got = exchange_wait(send, recv, [], lands, None, False, after, f"gather_weights_wait_{n}")
            for (k, ll), a in zip(groups[n], got):
                arrived.setdefault(short[k], [None] * n_layers)[ll] = as_used(k, a)
        return arrived

    gathered = all_gather([ffn_conv_w.reshape(n_layers, 9, -1), c], "gather_conv_c")
    conv_all, c_all = gathered[0], gathered[1].reshape(N_DEV, D)
    conv_full = [conv_all[:, l].transpose(1, 0, 2).reshape(9, N_FFK, FF_SLOT).transpose(1, 0, 2) for l in layers]

    cctx8 = jnp.broadcast_to(c_ctx[None, :], (N_DEV, D))
    ada_b_cols = lax.dynamic_slice_in_dim(ada_b, me * ada_cols, ada_cols, axis=1)[:, None, :]
    mod_cols = ada_fwd(c_all, cctx8, ada_w, ada_b_cols)
    (mod_all,) = all_gather([mod_cols], "gather_mod")
    mod_x = lax.dynamic_index_in_dim(mod_all[:, :, 0], me, axis=2, keepdims=False)
    mod_c = mod_all[:, :, 1, 0]
    mod = [jnp.stack([mod_c[:, l].reshape(6, 1, D), mod_x[:, l].reshape(6, 1, D)]) for l in layers]
    mod[0] = mod[0] + started

    lb1 = lower_bounds(hgrn_lower_bounds)
    lb = [jnp.zeros((2, 1, D), F32), lb1.reshape(2, 1, D)]

    w = {
        "nw1": [norm1_w[l][None] for l in layers], "nw2": [norm2_w[l][None] for l in layers],
        "lnw": [sgu_ln_w[l][None] for l in layers], "lnb": [sgu_ln_b[l][None] for l in layers],
        "sw": [sgu_w[l] for l in layers], "sb": [sgu_b[l][:, :, None] for l in layers],
        "hnw": [hgrn_norm_w[l][None] for l in layers], "cw": conv_full,
        "cb": [ffn_conv_b[l].reshape(N_FFK, 1, FF_SLOT) for l in layers], "fw": final_norm_w[None],
    }
    long = {v: k for k, v in short.items()}
    landing, sent = {}, []

    def publish(l, part, grads):
        keys = [long[k] for k in grads]
        slots = [a.reshape((N_DEV, -1, a.shape[-1])) for a in grads.values()]
        zones = [place_own(s, landing.get(k, jax.ShapeDtypeStruct((N_DEV, n_layers) + s.shape[1:], s.dtype)), me1, l, True,
                           f"scatter_own_{short[k]}_{l}") for k, s in zip(keys, slots)]
        send, recv, srcs, zones, token = exchange_start(slots, zones, l, True, f"scatter_grads_start_{part}_{l}")
        landing.update(zip(keys, zones))
        sent.append((keys, l, part, send, recv, srcs, token))
        return token[0, 0]

    out = {}
    flat2 = lambda a: a.reshape(-1, a.shape[-1])

    def small_ready(g, dmod, dlb):
        d_hlb = lower_bounds_bwd(hgrn_lower_bounds, dlb[1].reshape(1, 2 * D))
        st = lambda k: jnp.stack(g[k])
        rep_grads = {"norm1_w": st("nw1"), "sgu_ln_w": st("lnw"), "sgu_ln_b": st("lnb"), "sgu_w": st("sw"), "sgu_b": st("sb"),
                     "hgrn_lower_bounds": d_hlb, "hgrn_norm_w": st("hnw"), "norm2_w": st("nw2"), "ffn_conv_b": st("cb"),
                     "final_norm_w": g["fw"]}
        rep_rows = -(-sum(_rows_of(wts[k].size) for k in REPLICATED) // 64) * 64
        d_conv = jnp.stack([g["cw"][l].transpose(1, 0, 2).reshape(9, D_FF) for l in layers])
        dmod_x = jnp.stack([dmod[l][1].reshape(6 * D) for l in layers])
        dmod_c = jnp.stack([dmod[l][0].reshape(6 * D) for l in layers])
        small = jnp.concatenate([_pack([rep_grads[k] for k in REPLICATED], rep_rows), _pack([d_conv, dmod_x, dmod_c])], axis=0)
        (small_all,) = all_gather([small], "gather_small_grads")
        conv_g, dmx_all, dmc_all = _unpack(small_all[:, rep_rows:], [d_conv.shape, dmod_x.shape, dmod_c.shape])

        rep = adamw(_pack([wts[k] for k in REPLICATED], rep_rows), _pack([mom1[k] for k in REPLICATED], rep_rows),
                    _pack([mom2[k] for k in REPLICATED], rep_rows), small_all[:, :rep_rows], "adamw_replicated")
        rep = [_unpack(r, [wts[k].shape for k in REPLICATED]) for r in rep]
        for n, k in enumerate(REPLICATED):
            out[k] = tuple(r[n] for r in rep)

        conv_mine = lax.dynamic_index_in_dim(conv_g.reshape(N_DEV, n_layers, 9, N_DEV, -1), me, axis=3, keepdims=False)
        res = adamw(flat2(ffn_conv_w), flat2(m_ffn_conv_w), flat2(v_ffn_conv_w),
                    conv_mine.reshape(N_DEV, -1, conv_mine.shape[-1]), "adamw_conv_w")
        out["ffn_conv_w"] = tuple(r.reshape(ffn_conv_w.shape) for r in res)

        out["ada_b"] = tuple(adamw(ada_b, m_ada_b, v_ada_b, jnp.concatenate([dmx_all, dmc_all], axis=0), "adamw_ada_b"))

        cols_of = lambda a: lax.dynamic_slice_in_dim(a, me * ada_cols, ada_cols, axis=2).transpose(1, 0, 2)
        d_ada_w, d_cctx = ada_bwd(c_all, cctx8, ada_w, ada_b_cols, cols_of(dmx_all), cols_of(dmc_all))
        res = adamw(flat2(ada_w), flat2(m_ada_w), flat2(v_ada_w), flat2(d_ada_w)[None], "adamw_ada_w")
        out["ada_w"] = tuple(r.reshape(ada_w.shape) for r in res)
        (d_cctx_all,) = all_gather([d_cctx], "gather_c_ctx_grad")
        res = adamw(c_ctx[None], m_c_ctx[None], v_c_ctx[None], d_cctx_all, "adamw_c_ctx")
        out["c_ctx"] = tuple(r[0] for r in res)
        return jnp.minimum(jnp.abs(d_cctx_all[0, 0, 0]), 0.0)

    loss, grad_x, _, _, _ = local_step(x[0], ctx[0], loss_target[0], mod, lb, w, fetch, publish, small_ready)
    loss = lax.psum(loss, AXES)

    after = sent[-1][-1]
    for part in ("ffn", "mix", "in"):
        for keys, l, p, send, recv, srcs, _ in sent:
            if p == part:
                zones = exchange_wait(send, recv, srcs, [landing[k] for k in keys], l, True, after, f"scatter_grads_wait_{part}_{l}")
                landing.update(zip(keys, zones))
    for k in big:
        r = landing[k]
        res = adamw(flat2(wts[k]), flat2(mom1[k]), flat2(mom2[k]), r.reshape(N_DEV, -1, r.shape[-1]), "adamw_" + k)
        out[k] = tuple(a.reshape(wts[k].shape) for a in res)

    return (loss, grad_x[None]) + tuple(out[k][n] for n in range(4) for k in WEIGHT_ORDER)
```

```python
import functools
import math

import jax
import jax.numpy as jnp
from jax import lax
from jax.experimental import pallas as pl
from jax.experimental.pallas import tpu as pltpu

F32 = jnp.float32
BF16 = jnp.bfloat16
HIGHEST = lax.Precision.HIGHEST

N_DEV = 8
AXES = ("x", "y", "c")
D = 1024
CTX = 256
TM = 256
TM_SMALL = 128
CH = 64
SGU_CH = 128
HEADS = 8
HD = 128
GRID_W = 64
D_IN = 9 * D
IN_SLOT = D_IN // N_DEV
D_FF = 2816
FF_SLOT = 2 * D_FF // N_DEV
N_FFK = D_FF // FF_SLOT
RMS_EPS = 1e-6
LN_EPS = 1e-5
ADAM_LR, ADAM_B1, ADAM_B2, ADAM_EPS, ADAM_WD, ADAM_STEP = 0.001, 0.9, 0.999, 1e-08, 0.01, 10
VMEM_LIMIT_V7X = 56 * 2 ** 20
GRAD_WIRE = jnp.bfloat16

VMEM_WHOLE = pl.BlockSpec(memory_space=pltpu.VMEM)
ANY = pl.BlockSpec(memory_space=pl.ANY)


def _cp(n_axes):
    return pltpu.CompilerParams(dimension_semantics=("arbitrary",) * n_axes, vmem_limit_bytes=VMEM_LIMIT_V7X)


def _dot(a, b, dims):
    return lax.dot_general(a.astype(BF16), b.astype(BF16), (dims, ((), ())), preferred_element_type=F32)


@jax.custom_vjp
def mm(a, b):
    return _dot(a, b, ((1,), (0,)))


mm.defvjp(lambda a, b: (mm(a, b), (a, b)),
          lambda r, g: (_dot(g, r[1], ((1,), (1,))).astype(r[0].dtype), _dot(r[0], g, ((0,), (0,))).astype(r[1].dtype)))


@jax.custom_vjp
def mm_nt(a, b):
    return _dot(a, b, ((1,), (1,)))


mm_nt.defvjp(lambda a, b: (mm_nt(a, b), (a, b)),
             lambda r, g: (_dot(g, r[1], ((1,), (0,))).astype(r[0].dtype), _dot(g, r[0], ((0,), (0,))).astype(r[1].dtype)))


@jax.custom_vjp
def mm_tn(a, b):
    return _dot(a, b, ((0,), (0,)))


mm_tn.defvjp(lambda a, b: (mm_tn(a, b), (a, b)),
             lambda r, g: (_dot(r[1], g, ((1,), (1,))).astype(r[0].dtype), _dot(r[0], g, ((1,), (0,))).astype(r[1].dtype)))


def _tri_dot(m, g):
    hi = g.astype(BF16)
    rest = g - hi.astype(F32)
    mid = rest.astype(BF16)
    low = (rest - mid.astype(F32)).astype(BF16)
    n = g.shape[1]
    out = jnp.dot(m.astype(BF16), jnp.concatenate([hi, mid, low], axis=1), preferred_element_type=F32)
    return out[:, :n] + out[:, n:2 * n] + out[:, 2 * n:]


@jax.custom_vjp
def _cum(m, mt, g):
    return _tri_dot(m, g)


_cum.defvjp(lambda m, mt, g: (_cum(m, mt, g), (m, mt)),
            lambda r, d: (jnp.zeros_like(r[0]), jnp.zeros_like(r[1]), _tri_dot(r[1], d)))


def _silu(x):
    return x * jax.nn.sigmoid(x)


def _gelu(x):
    return 0.5 * x * (1.0 + jnp.tanh(math.sqrt(2.0 / math.pi) * (x + 0.044715 * (x * x * x))))


def _rms(x, w):
    return x * lax.rsqrt(jnp.mean(x * x, axis=-1, keepdims=True) + RMS_EPS) * w


def _norm_mod(x, w, shift, scale):
    return _rms(x, w) * (1.0 + scale) + shift


def _hsl(h):
    return slice(h * HD, (h + 1) * HD)


def _hgrn_chunk(st, qz, fz, iv, lb, m, mt, mref):
    hs = range(HEADS)
    keep = [1.0 - lb[h] for h in hs]
    g = [jnp.log(lb[h] + keep[h] * jax.nn.sigmoid(fz[h])) for h in hs]
    k = [keep[h] * jax.nn.sigmoid(-fz[h]) for h in hs]
    q = [_silu(qz[h]) for h in hs]
    b = [_cum(m, mt, g[h]) for h in hs]
    ref = [jnp.sum(mref * g[h], axis=0, keepdims=True) for h in hs]
    last = [jnp.sum(g[h], axis=0, keepdims=True) for h in hs]
    qa = [q[h] * jnp.exp(b[h] - ref[h]) for h in hs]
    ka = [k[h] * jnp.exp(ref[h] - b[h]) for h in hs]
    scores = [jnp.where(m > 0.5, mm_nt(qa[h], ka[h]), 0.0) for h in hs]
    inter = [mm_nt(qa[h] * jnp.exp(ref[h]), st[h]) for h in hs]
    kv = [mm_tn(iv[h], ka[h] * jnp.exp(last[h] - ref[h])) for h in hs]
    outs = [mm(scores[h], iv[h]) + inter[h] for h in hs]
    news = [jnp.exp(last[h]) * st[h] + kv[h] for h in hs]
    return outs, news


def _sgu_fn(ub, vb, lnw, lnb, sw, sb):
    gv = [_gelu(v) for v in vb]
    mu = sum(jnp.sum(t, axis=-1, keepdims=True) for t in gv) / D
    var = sum(jnp.sum((t - mu) * (t - mu), axis=-1, keepdims=True) for t in gv) / D
    inv = lax.rsqrt(var + LN_EPS)
    cols = []
    for g in range(HEADS):
        vn = (gv[g] - mu) * inv * lnw[g] + lnb[g]
        cols.append(_gelu(ub[g]) * (mm(sw[g], vn) + sb[g]))
    return jnp.concatenate(cols, axis=1)


def _readout_fn(ob, og, hnw):
    r = [o * lax.rsqrt(jnp.mean(o * o, axis=-1, keepdims=True) + RMS_EPS) * hnw for o in ob]
    return jnp.concatenate(r, axis=1) * _silu(og)


def _glu_fn(ac, v):
    return _gelu(ac) * v


def _stream_row(tm):
    n_ctx = CTX // tm
    return lambda i: (jnp.where(i < n_ctx, 0, 1), 0, 0, 0)


def in_proj_fwd(x, mod, nw, wg):
    t = x.shape[0]

    def body(x_ref, mod_ref, nw_ref, w_ref, out_ref, ht_ref):
        h32 = _norm_mod(x_ref[...], nw_ref[...], mod_ref[0, 0], mod_ref[0, 1])
        ht_ref[...] = h32.T.astype(BF16)
        h = h32.astype(BF16)
        for j in range(N_DEV):
            out_ref[:, j * IN_SLOT:(j + 1) * IN_SLOT] = jnp.dot(h, w_ref[j], preferred_element_type=F32)

    return pl.pallas_call(
        body, name="in_proj_fwd", grid=(t // TM,),
        in_specs=[pl.BlockSpec((TM, D), lambda i: (i, 0)), pl.BlockSpec((1, 6, 1, D), _stream_row(TM)),
                  pl.BlockSpec((1, D), lambda i: (0, 0)), VMEM_WHOLE],
        out_specs=[pl.BlockSpec((TM, D_IN), lambda i: (i, 0)), pl.BlockSpec((D, TM), lambda i: (0, i))],
        out_shape=[jax.ShapeDtypeStruct((t, D_IN), F32), jax.ShapeDtypeStruct((D, t), BF16)],
        compiler_params=_cp(1))(x, mod, nw, wg)


def _scan_chunk(nc):
    ncc = CTX // CH

    def chunk(d, s):
        bwd = jnp.where(s < ncc, ncc - 1 - s, nc + ncc - 1 - s)
        return jnp.where(d == 0, s, bwd)
    return chunk


def hgrn_fwd(parts, lb, mc, mtc, mrefc):
    t = parts.shape[0]
    nc = t // CH
    chunk = _scan_chunk(nc)

    def body(q_ref, f_ref, i_ref, lb_ref, m_ref, mt_ref, mr_ref, o_ref, ck_ref, st):
        @pl.when(pl.program_id(1) == 0)
        def _():
            st[...] = jnp.zeros_like(st)
        ck_ref[0, 0] = st[...]
        outs, news = _hgrn_chunk([st[h] for h in range(HEADS)], [q_ref[:, _hsl(h)] for h in range(HEADS)],
                                 [f_ref[:, _hsl(h)] for h in range(HEADS)], [i_ref[:, _hsl(h)] for h in range(HEADS)],
                                 [lb_ref[0, :, _hsl(h)] for h in range(HEADS)], m_ref[0], mt_ref[0], mr_ref[0])
        for h in range(HEADS):
            o_ref[0, :, _hsl(h)] = outs[h]
            st[h] = news[h]

    const = lambda d, s: (d, 0, 0)
    return pl.pallas_call(
        body, name="hgrn_fwd", grid=(2, nc),
        in_specs=[pl.BlockSpec((CH, D), lambda d, s: (chunk(d, s), 0)), pl.BlockSpec((CH, D), lambda d, s: (chunk(d, s), 1 + d)),
                  pl.BlockSpec((CH, D), lambda d, s: (chunk(d, s), 3)), pl.BlockSpec((1, 1, D), const),
                  pl.BlockSpec((1, CH, CH), const), pl.BlockSpec((1, CH, CH), const), pl.BlockSpec((1, CH, 1), const)],
        out_specs=[pl.BlockSpec((1, CH, D), lambda d, s: (d, chunk(d, s), 0)),
                   pl.BlockSpec((1, 1, HEADS, HD, HD), lambda d, s: (d, s, 0, 0, 0))],
        out_shape=[jax.ShapeDtypeStruct((2, t, D), F32), jax.ShapeDtypeStruct((2, nc, HEADS, HD, HD), F32)],
        scratch_shapes=[pltpu.VMEM((HEADS, HD, HD), F32)], compiler_params=_cp(2))(parts, parts, parts, lb, mc, mtc, mrefc)


def _mixer_tile(rows, u_ref, v_ref, og_ref, o_ref, lnw_ref, lnb_ref, sw_ref, sb_ref, hnw_ref):
    n = (rows.stop - rows.start) // SGU_CH
    yas, vjps = [], []
    for c in range(n):
        r = slice(rows.start + c * SGU_CH, rows.start + (c + 1) * SGU_CH)
        ya, vjp_a = jax.vjp(_sgu_fn, [u_ref[r, _hsl(g)] for g in range(HEADS)], [v_ref[r, _hsl(g)] for g in range(HEADS)],
                            [lnw_ref[:, _hsl(g)] for g in range(HEADS)], [lnb_ref[:, _hsl(g)] for g in range(HEADS)],
                            [sw_ref[g] for g in range(HEADS)], [sb_ref[g] for g in range(HEADS)])
        yas.append(ya)
        vjps.append(vjp_a)
    yb, vjp_b = jax.vjp(_readout_fn, [o_ref[0, rows, _hsl(h)] + o_ref[1, rows, _hsl(h)] for h in range(HEADS)],
                        og_ref[rows, :], hnw_ref[...])
    return (yas[0] if n == 1 else jnp.concatenate(yas, axis=0)), yb, vjps, vjp_b


def _part_specs(tm, first, n):
    return [pl.BlockSpec((tm, D), functools.partial(lambda k, i: (i, k), first + k)) for k in range(n)]


def mixer_fwd(x, parts, o, mod, lnw, lnb, sw, sb, hnw, wa, wb, wo):
    t = x.shape[0]

    def body(x_ref, u_ref, v_ref, og_ref, ga_ref, gb_ref, o_ref, mod_ref, lnw_ref, lnb_ref, sw_ref, sb_ref, hnw_ref,
             wa_ref, wb_ref, wo_ref, out_ref):
        ya, yb, _, _ = _mixer_tile(slice(0, TM), u_ref, v_ref, og_ref, o_ref, lnw_ref, lnb_ref, sw_ref, sb_ref, hnw_ref)
        merged = (jax.nn.sigmoid(ga_ref[...]) * mm(ya, wa_ref[...]) + jax.nn.sigmoid(gb_ref[...]) * mm(yb, wb_ref[...]))
        out_ref[...] = x_ref[...] + mod_ref[0, 2] * mm(merged, wo_ref[...])

    vec = lambda n: pl.BlockSpec((1, n), lambda i: (0, 0))
    return pl.pallas_call(
        body, name="mixer_fwd", grid=(t // TM,),
        in_specs=[pl.BlockSpec((TM, D), lambda i: (i, 0))] + _part_specs(TM, 4, 5)
        + [pl.BlockSpec((2, TM, D), lambda i: (0, i, 0)), pl.BlockSpec((1, 6, 1, D), _stream_row(TM)), vec(D), vec(D),
           VMEM_WHOLE, VMEM_WHOLE, vec(HD), VMEM_WHOLE, VMEM_WHOLE, VMEM_WHOLE],
        out_specs=pl.BlockSpec((TM, D), lambda i: (i, 0)),
        out_shape=jax.ShapeDtypeStruct((t, D), F32), compiler_params=_cp(1),
    )(x, parts, parts, parts, parts, parts, o, mod, lnw, lnb, sw, sb, hnw, wa, wb, wo)


def ffn_up_fwd(x, mod, nw, wg):
    t = x.shape[0]

    def body(x_ref, mod_ref, nw_ref, w_ref, out_ref, ht_ref):
        h32 = _norm_mod(x_ref[...], nw_ref[...], mod_ref[0, 3], mod_ref[0, 4])
        ht_ref[...] = h32.T.astype(BF16)
        h = h32.astype(BF16)
        for j in range(N_DEV):
            out_ref[j] = jnp.dot(h, w_ref[j], preferred_element_type=F32)

    return pl.pallas_call(
        body, name="ffn_up_fwd", grid=(t // TM,),
        in_specs=[pl.BlockSpec((TM, D), lambda i: (i, 0)), pl.BlockSpec((1, 6, 1, D), _stream_row(TM)),
                  pl.BlockSpec((1, D), lambda i: (0, 0)), VMEM_WHOLE],
        out_specs=[pl.BlockSpec((N_DEV, TM, FF_SLOT), lambda i: (0, i, 0)), pl.BlockSpec((D, TM), lambda i: (0, i))],
        out_shape=[jax.ShapeDtypeStruct((N_DEV, t, FF_SLOT), F32), jax.ShapeDtypeStruct((D, t), BF16)],
        compiler_params=_cp(1))(x, mod, nw, wg)


def _halo_specs(nt, k_of, i_of):
    per = TM // GRID_W
    last = nt * per - 1
    return [pl.BlockSpec((1, GRID_W, FF_SLOT), lambda *g: (k_of(*g), jnp.maximum(i_of(*g) * per - 1, 0), 0)),
            pl.BlockSpec((1, TM, FF_SLOT), lambda *g: (k_of(*g), i_of(*g), 0)),
            pl.BlockSpec((1, GRID_W, FF_SLOT), lambda *g: (k_of(*g), jnp.minimum(i_of(*g) * per + per, last), 0))]


def _with_halo(prev_ref, main_ref, next_ref, i, nt):
    prev = jnp.where(i >= 2, prev_ref[0], 0.0)
    nxt = jnp.where((i >= 1) & (i <= nt - 2), next_ref[0], 0.0)
    return jnp.concatenate([prev, main_ref[0], nxt], axis=0)


def _tap_valid(dc, i, n_rows, offset):
    r = lax.broadcasted_iota(jnp.int32, (n_rows, 1), 0) - offset
    col = jnp.bitwise_and(r, GRID_W - 1)
    pos = jnp.where(i == 0, r, col) + dc
    return (pos >= 0) & (pos < jnp.where(i == 0, TM, GRID_W))


def _row_weight(cw_ref, dr, dc, i):
    w = cw_ref[0, 3 * (dr + 1) + dc + 1:3 * (dr + 1) + dc + 2, :]
    return w if dr == 0 else jnp.where(i == 0, 0.0, w)


def ffn_down_fwd(x, av, mod, cw, cb, wd):
    t = x.shape[0]
    nt = t // TM
    ext = TM + 2 * GRID_W

    def body(x_ref, ap_ref, am_ref, an_ref, v_ref, mod_ref, cw_ref, cb_ref, wd_ref, out_ref, ac_ref, y_ref, acc):
        i, k = pl.program_id(0), pl.program_id(1)
        a_ext = _with_halo(ap_ref, am_ref, an_ref, i, nt)
        conv = jnp.zeros((TM, FF_SLOT), F32) + cb_ref[0]
        for dc in (-1, 0, 1):
            rolled = a_ext if dc == 0 else pltpu.roll(a_ext, (-dc) % ext, 0)
            valid = _tap_valid(dc, i, TM, 0)
            for dr in (-1, 0, 1):
                lo = GRID_W + GRID_W * dr
                conv = conv + jnp.where(valid, rolled[lo:lo + TM], 0.0) * _row_weight(cw_ref, dr, dc, i)
        ac_ref[0] = conv
        part = mm(_glu_fn(conv, v_ref[0]), wd_ref[0])

        @pl.when(k == 0)
        def _():
            acc[...] = part

        @pl.when(k > 0)
        def _():
            acc[...] += part

        @pl.when(k == N_FFK - 1)
        def _():
            y_ref[...] = acc[...]
            out_ref[...] = x_ref[...] + mod_ref[0, 5] * acc[...]

    tile = pl.BlockSpec((TM, D), lambda i, k: (i, 0))
    return pl.pallas_call(
        body, name="ffn_down_fwd", grid=(nt, N_FFK),
        in_specs=[tile] + _halo_specs(nt, lambda i, k: k, lambda i, k: i)
        + [pl.BlockSpec((1, TM, FF_SLOT), lambda i, k: (N_FFK + k, i, 0)),
           pl.BlockSpec((1, 6, 1, D), lambda i, k: (jnp.where(i < 1, 0, 1), 0, 0, 0)),
           pl.BlockSpec((1, 9, FF_SLOT), lambda i, k: (k, 0, 0)), pl.BlockSpec((1, 1, FF_SLOT), lambda i, k: (k, 0, 0)),
           pl.BlockSpec((1, FF_SLOT, D), lambda i, k: (k, 0, 0))],
        out_specs=[tile, pl.BlockSpec((1, TM, FF_SLOT), lambda i, k: (k, i, 0)), tile],
        out_shape=[jax.ShapeDtypeStruct((t, D), F32), jax.ShapeDtypeStruct((N_FFK, t, FF_SLOT), F32),
                   jax.ShapeDtypeStruct((t, D), F32)],
        scratch_shapes=[pltpu.VMEM((TM, D), F32)], compiler_params=_cp(2))(x, av, av, av, av, mod, cw, cb, wd)


def loss_fwd_bwd(x, target, fw):
    t = x.shape[0]

    def body(x_ref, t_ref, w_ref, loss_ref, dx_ref, dw_ref):
        i = pl.program_id(0)

        @pl.when(i == 0)
        def _():
            loss_ref[...] = jnp.zeros_like(loss_ref)
            dw_ref[...] = jnp.zeros_like(dw_ref)
            dx_ref[...] = jnp.zeros_like(dx_ref)

        @pl.when(i > 0)
        def _():
            y, vjp = jax.vjp(_rms, x_ref[...], w_ref[...])
            err = y - t_ref[...]
            loss_ref[...] += 0.5 * jnp.sum(jnp.sum(err * err, axis=-1, keepdims=True) / D)
            dx, dw = vjp(err / D)
            dx_ref[...] = dx
            dw_ref[...] += dw

    return pl.pallas_call(
        body, name="loss_fwd_bwd", grid=(t // TM,),
        in_specs=[pl.BlockSpec((TM, D), lambda i: (i, 0)), pl.BlockSpec((TM, D), lambda i: (jnp.maximum(i - 1, 0), 0)),
                  pl.BlockSpec((1, D), lambda i: (0, 0))],
        out_specs=[pl.BlockSpec((8, 128), lambda i: (0, 0)), pl.BlockSpec((TM, D), lambda i: (i, 0)),
                   pl.BlockSpec((1, D), lambda i: (0, 0))],
        out_shape=[jax.ShapeDtypeStruct((8, 128), F32), jax.ShapeDtypeStruct((t, D), F32), jax.ShapeDtypeStruct((1, D), F32)],
        compiler_params=_cp(1))(x, target, fw)


def _stream_add(ref, k, is_ctx, val):
    ref[0, k] += jnp.where(is_ctx, val, 0.0)
    ref[1, k] += jnp.where(is_ctx, 0.0, val)


def ffn_down_bwd(dx, ac, av, y, mod, wd):
    t = dx.shape[0]
    nt = t // TM

    def body(dx_ref, ac_ref, v_ref, y_ref, mod_ref, wd_ref, dav_ref, dac_ref, dwd_ref, dg_ref, acc):
        k, i = pl.program_id(0), pl.program_id(1)

        @pl.when((k == 0) & (i == 0))
        def _():
            dg_ref[...] = jnp.zeros_like(dg_ref)

        @pl.when(i == 0)
        def _():
            acc[...] = jnp.zeros_like(acc)

        @pl.when(k == 0)
        def _():
            _stream_add(dg_ref, 0, i == 0, jnp.sum(dx_ref[...] * y_ref[...], axis=0, keepdims=True))

        dout = mod_ref[0, 5] * dx_ref[...]
        z, vjp = jax.vjp(_glu_fn, ac_ref[0], v_ref[0])
        dac, dv = vjp(mm_nt(dout, wd_ref[0]))
        dac_ref[0] = dac
        dav_ref[0] = dv.astype(BF16)
        acc[...] += mm_tn(z, dout)

        @pl.when(i == nt - 1)
        def _():
            dwd_ref[0] = acc[...].astype(dwd_ref.dtype)

    tile = pl.BlockSpec((TM, D), lambda k, i: (i, 0))
    return pl.pallas_call(
        body, name="ffn_down_bwd", grid=(N_FFK, nt),
        in_specs=[tile, pl.BlockSpec((1, TM, FF_SLOT), lambda k, i: (k, i, 0)),
                  pl.BlockSpec((1, TM, FF_SLOT), lambda k, i: (N_FFK + k, i, 0)), tile,
                  pl.BlockSpec((1, 6, 1, D), lambda k, i: (jnp.where(i < 1, 0, 1), 0, 0, 0)),
                  pl.BlockSpec((1, FF_SLOT, D), lambda k, i: (k, 0, 0))],
        out_specs=[pl.BlockSpec((1, TM, FF_SLOT), lambda k, i: (N_FFK + k, i, 0)),
                   pl.BlockSpec((1, TM, FF_SLOT), lambda k, i: (k, i, 0)),
                   pl.BlockSpec((1, FF_SLOT, D), lambda k, i: (k, 0, 0)),
                   pl.BlockSpec((2, 1, 1, D), lambda k, i: (0, 0, 0, 0))],
        out_shape=[jax.ShapeDtypeStruct((N_DEV, t, FF_SLOT), BF16), jax.ShapeDtypeStruct((N_FFK, t, FF_SLOT), F32),
                   jax.ShapeDtypeStruct((N_FFK, FF_SLOT, D), GRAD_WIRE), jax.ShapeDtypeStruct((2, 1, 1, D), F32)],
        scratch_shapes=[pltpu.VMEM((FF_SLOT, D), F32)], compiler_params=_cp(2))(dx, ac, av, y, mod, wd)


def conv_bwd(dav, dac, av, cw):
    t = dac.shape[1]
    nt = t // TM
    ext = TM + 2 * GRID_W

    def body(dav_in, gp_ref, gm_ref, gn_ref, ap_ref, am_ref, an_ref, cw_ref, dav_ref, dcw_ref, dcb_ref):
        k, i = pl.program_id(0), pl.program_id(1)

        @pl.when(i == 0)
        def _():
            dcw_ref[...] = jnp.zeros_like(dcw_ref)
            dcb_ref[...] = jnp.zeros_like(dcb_ref)

        g_ext = _with_halo(gp_ref, gm_ref, gn_ref, i, nt)
        a_ext = _with_halo(ap_ref, am_ref, an_ref, i, nt)
        g_main = gm_ref[0]
        dcb_ref[0] += jnp.sum(g_main, axis=0, keepdims=True)
        da = jnp.zeros((TM, FF_SLOT), F32)
        for dc in (-1, 0, 1):
            gv = jnp.where(_tap_valid(dc, i, ext, GRID_W), g_ext, 0.0)
            g_rolled = gv if dc == 0 else pltpu.roll(gv, dc % ext, 0)
            a_rolled = a_ext if dc == 0 else pltpu.roll(a_ext, (-dc) % ext, 0)
            g_valid = jnp.where(_tap_valid(dc, i, TM, 0), g_main, 0.0)
            for dr in (-1, 0, 1):
                lo = GRID_W - GRID_W * dr
                da = da + g_rolled[lo:lo + TM] * _row_weight(cw_ref, dr, dc, i)
                lo = GRID_W + GRID_W * dr
                tap = 3 * (dr + 1) + dc + 1
                dw = jnp.sum(g_valid * a_rolled[lo:lo + TM], axis=0, keepdims=True)
                dcw_ref[0, tap:tap + 1, :] += dw if dr == 0 else jnp.where(i == 0, 0.0, dw)
        dav_ref[0] = da.astype(BF16)

    return pl.pallas_call(
        body, name="conv_bwd", grid=(N_FFK, nt),
        in_specs=[ANY] + _halo_specs(nt, lambda k, i: k, lambda k, i: i) + _halo_specs(nt, lambda k, i: k, lambda k, i: i)
        + [pl.BlockSpec((1, 9, FF_SLOT), lambda k, i: (k, 0, 0))],
        out_specs=[pl.BlockSpec((1, TM, FF_SLOT), lambda k, i: (k, i, 0)), pl.BlockSpec((1, 9, FF_SLOT), lambda k, i: (k, 0, 0)),
                   pl.BlockSpec((1, 1, FF_SLOT), lambda k, i: (k, 0, 0))],
        out_shape=[jax.ShapeDtypeStruct(dav.shape, BF16), jax.ShapeDtypeStruct((N_FFK, 9, FF_SLOT), F32),
                   jax.ShapeDtypeStruct((N_FFK, 1, FF_SLOT), F32)],
        input_output_aliases={0: 0}, compiler_params=_cp(2))(dav, dac, dac, dac, av, av, av, cw)


def _norm_mod_bwd(x_ref, nw_ref, mod_ref, k_shift, dh, dx_in, dx_ref, dnw_ref, dmod_ref, is_ctx):
    _, vjp = jax.vjp(_norm_mod, x_ref[...], nw_ref[...], mod_ref[0, k_shift], mod_ref[0, k_shift + 1])
    dx, dnw, dshift, dscale = vjp(dh)
    dx_ref[...] = dx_in + dx
    dnw_ref[...] += dnw
    _stream_add(dmod_ref, 0, is_ctx, dshift)
    _stream_add(dmod_ref, 1, is_ctx, dscale)


def ffn_up_bwd_x(dx2, x, dav, mod, nw, wg):
    t = x.shape[0]

    def body(dx2_ref, x_ref, dav_ref, mod_ref, nw_ref, w_ref, dx_ref, dnw_ref, dmod_ref):
        i = pl.program_id(0)

        @pl.when(i == 0)
        def _():
            dnw_ref[...] = jnp.zeros_like(dnw_ref)
            dmod_ref[...] = jnp.zeros_like(dmod_ref)

        dh = mm_nt(dav_ref[0], w_ref[0])
        for j in range(1, N_DEV):
            dh = dh + mm_nt(dav_ref[j], w_ref[j])
        _norm_mod_bwd(x_ref, nw_ref, mod_ref, 3, dh, dx2_ref[...], dx_ref, dnw_ref, dmod_ref, i == 0)

    tile = pl.BlockSpec((TM, D), lambda i: (i, 0))
    return pl.pallas_call(
        body, name="ffn_up_bwd_x", grid=(t // TM,),
        in_specs=[tile, tile, pl.BlockSpec((N_DEV, TM, FF_SLOT), lambda i: (0, i, 0)), pl.BlockSpec((1, 6, 1, D), _stream_row(TM)),
                  pl.BlockSpec((1, D), lambda i: (0, 0)), VMEM_WHOLE],
        out_specs=[tile, pl.BlockSpec((1, D), lambda i: (0, 0)), pl.BlockSpec((2, 2, 1, D), lambda i: (0, 0, 0, 0))],
        out_shape=[jax.ShapeDtypeStruct((t, D), F32), jax.ShapeDtypeStruct((1, D), F32), jax.ShapeDtypeStruct((2, 2, 1, D), F32)],
        compiler_params=_cp(1))(dx2, x, dav, mod, nw, wg)


def weight_grad(at, dout, slot, name, after=None):
    rows, t = at.shape
    stacked = dout.ndim == 3
    n = dout.shape[0] if stacked else dout.shape[1] // slot

    def body(a_ref, d_ref, *rest):
        dw_ref = rest[-1]
        dw_ref[0] = jnp.dot(a_ref[...], d_ref[0] if stacked else d_ref[...], preferred_element_type=F32).astype(dw_ref.dtype)

    d_spec = pl.BlockSpec((1, t, slot), lambda j: (j, 0, 0)) if stacked else pl.BlockSpec((t, slot), lambda j: (0, j))
    extra = [] if after is None else [jnp.reshape(after, (1, 1))]
    return pl.pallas_call(
        body, name=name, grid=(n,), in_specs=[VMEM_WHOLE, d_spec] + [ANY] * len(extra),
        out_specs=pl.BlockSpec((1, rows, slot), lambda j: (j, 0, 0)),
        out_shape=jax.ShapeDtypeStruct((n, rows, slot), GRAD_WIRE), compiler_params=_cp(1))(at, dout, *extra)


def mixer_bwd(dx, parts, o, mod, lnw, lnb, sw, sb, hnw, wa, wb, wo):
    t = dx.shape[0]
    tm = TM_SMALL
    n_ctx = CTX // tm

    def body(dx_ref, u_ref, v_ref, og_ref, ga_ref, gb_ref, o_ref, mod_ref, lnw_ref, lnb_ref, sw_ref, sb_ref, hnw_ref,
             wa_ref, wb_ref, wo_ref, d5_ref, do_ref, dwa_out, dwb_out, dwo_out, dlnw_ref, dlnb_ref, dsw_ref, dsb_ref,
             dhnw_ref, dg_ref, dwa_ref, dwb_ref, dwo_ref):
        i = pl.program_id(0)

        @pl.when(i == 0)
        def _():
            for r in (dwa_ref, dwb_ref, dwo_ref, dlnw_ref, dlnb_ref, dsw_ref, dsb_ref, dhnw_ref, dg_ref):
                r[...] = jnp.zeros_like(r)

        ya, yb, vjps, vjp_b = _mixer_tile(slice(0, tm), u_ref, v_ref, og_ref, o_ref, lnw_ref, lnb_ref, sw_ref, sb_ref, hnw_ref)
        pa, pb = mm(ya, wa_ref[...]), mm(yb, wb_ref[...])
        sa, sbg = jax.nn.sigmoid(ga_ref[...]), jax.nn.sigmoid(gb_ref[...])
        merged = sa * pa + sbg * pb
        dxv = dx_ref[...]
        _stream_add(dg_ref, 0, i < n_ctx, jnp.sum(dxv * mm(merged, wo_ref[...]), axis=0, keepdims=True))
        dy = mod_ref[0, 2] * dxv
        dmerged = mm_nt(dy, wo_ref[...])
        dwo_ref[...] += mm_tn(merged, dy)
        dpa, dpb = sa * dmerged, sbg * dmerged
        d5_ref[:, 3 * D:4 * D] = (dmerged * pa * sa * (1.0 - sa)).astype(BF16)
        d5_ref[:, 4 * D:5 * D] = (dmerged * pb * sbg * (1.0 - sbg)).astype(BF16)
        dwa_ref[...] += mm_tn(ya, dpa)
        dwb_ref[...] += mm_tn(yb, dpb)
        dub, dvb, dlnw, dlnb, dsw, dsb = vjps[0](mm_nt(dpa, wa_ref[...]))
        dob, dog, dhnw = vjp_b(mm_nt(dpb, wb_ref[...]))
        d5_ref[:, 2 * D:3 * D] = dog.astype(BF16)
        dhnw_ref[...] += dhnw
        for g in range(HEADS):
            d5_ref[:, g * HD:(g + 1) * HD] = dub[g].astype(BF16)
            d5_ref[:, D + g * HD:D + (g + 1) * HD] = dvb[g].astype(BF16)
            do_ref[:, _hsl(g)] = dob[g]
            dlnw_ref[:, _hsl(g)] += dlnw[g]
            dlnb_ref[:, _hsl(g)] += dlnb[g]
            dsw_ref[g] += dsw[g]
            dsb_ref[g] += dsb[g]

        @pl.when(i == t // tm - 1)
        def _():
            for acc, out in ((dwa_ref, dwa_out), (dwb_ref, dwb_out), (dwo_ref, dwo_out)):
                out[...] = acc[...].astype(out.dtype)

    vec = lambda n: pl.BlockSpec((1, n), lambda i: (0, 0))
    tile = pl.BlockSpec((tm, D), lambda i: (i, 0))
    sds = jax.ShapeDtypeStruct
    return pl.pallas_call(
        body, name="mixer_bwd", grid=(t // tm,),
        in_specs=[tile] + _part_specs(tm, 4, 5)
        + [pl.BlockSpec((2, tm, D), lambda i: (0, i, 0)), pl.BlockSpec((1, 6, 1, D), _stream_row(tm)), vec(D), vec(D),
           VMEM_WHOLE, VMEM_WHOLE, vec(HD), VMEM_WHOLE, VMEM_WHOLE, VMEM_WHOLE],
        out_specs=[pl.BlockSpec((tm, 5 * D), lambda i: (i, 0)), tile, VMEM_WHOLE, VMEM_WHOLE, VMEM_WHOLE, vec(D), vec(D),
                   VMEM_WHOLE, VMEM_WHOLE, vec(HD), pl.BlockSpec((2, 1, 1, D), lambda i: (0, 0, 0, 0))],
        out_shape=[sds((t, 5 * D), BF16), sds((t, D), F32), sds((D, D), GRAD_WIRE), sds((D, D), GRAD_WIRE), sds((D, D), GRAD_WIRE),
                   sds((1, D), F32), sds((1, D), F32), sds((HEADS, SGU_CH, SGU_CH), F32), sds((HEADS, SGU_CH, 1), F32),
                   sds((1, HD), F32), sds((2, 1, 1, D), F32)],
        scratch_shapes=[pltpu.VMEM((D, D), F32)] * 3,
        compiler_params=_cp(1))(dx, parts, parts, parts, parts, parts, o, mod, lnw, lnb, sw, sb, hnw, wa, wb, wo)


def hgrn_bwd(parts, lb, mc, mtc, mrefc, ck, do):
    t = parts.shape[0]
    nc = t // CH
    chunk = _scan_chunk(nc)
    rev = lambda d, s: chunk(d, nc - 1 - s)

    def body(q_ref, f_ref, i_ref, lb_ref, m_ref, mt_ref, mr_ref, ck_ref, do_ref, dq_ref, df_ref, di_ref, dlb_ref, dst):
        @pl.when(pl.program_id(1) == 0)
        def _():
            dst[...] = jnp.zeros_like(dst)
            dlb_ref[...] = jnp.zeros_like(dlb_ref)

        heads = range(HEADS)
        fn = functools.partial(_hgrn_chunk, m=m_ref[0], mt=mt_ref[0], mref=mr_ref[0])
        _, vjp = jax.vjp(fn, [ck_ref[0, 0, h] for h in heads], [q_ref[:, _hsl(h)] for h in heads],
                         [f_ref[:, _hsl(h)] for h in heads], [i_ref[:, _hsl(h)] for h in heads],
                         [lb_ref[0, :, _hsl(h)] for h in heads])
        dstl, dq, df, di, dlb = vjp(([do_ref[:, _hsl(h)] for h in heads], [dst[h] for h in heads]))
        for h in heads:
            dst[h] = dstl[h]
            dq_ref[0, :, _hsl(h)] = dq[h].astype(BF16)
            df_ref[0, :, _hsl(h)] = df[h].astype(BF16)
            di_ref[0, :, _hsl(h)] = di[h].astype(BF16)
            dlb_ref[0, :, _hsl(h)] += dlb[h]

    const = lambda d, s: (d, 0, 0)
    out = pl.BlockSpec((1, CH, D), lambda d, s: (d, rev(d, s), 0))
    return pl.pallas_call(
        body, name="hgrn_bwd", grid=(2, nc),
        in_specs=[pl.BlockSpec((CH, D), lambda d, s: (rev(d, s), 0)), pl.BlockSpec((CH, D), lambda d, s: (rev(d, s), 1 + d)),
                  pl.BlockSpec((CH, D), lambda d, s: (rev(d, s), 3)), pl.BlockSpec((1, 1, D), const),
                  pl.BlockSpec((1, CH, CH), const), pl.BlockSpec((1, CH, CH), const), pl.BlockSpec((1, CH, 1), const),
                  pl.BlockSpec((1, 1, HEADS, HD, HD), lambda d, s: (d, nc - 1 - s, 0, 0, 0)),
                  pl.BlockSpec((CH, D), lambda d, s: (rev(d, s), 0))],
        out_specs=[out, out, out, pl.BlockSpec((1, 1, D), const)],
        out_shape=[jax.ShapeDtypeStruct((2, t, D), BF16)] * 3 + [jax.ShapeDtypeStruct((2, 1, D), F32)],
        scratch_shapes=[pltpu.VMEM((HEADS, HD, HD), F32)], compiler_params=_cp(2),
    )(parts, parts, parts, lb, mc, mtc, mrefc, ck, do)


def in_proj_bwd_x(dx1, x, dq, df, di, d5, mod, nw, wg):
    t = x.shape[0]
    tm = TM
    n_ctx = CTX // tm

    def body(dx1_ref, x_ref, dq_ref, df_ref, di_ref, d5_ref, mod_ref, nw_ref, w_ref, dx_ref, dp_ref, dnw_ref, dmod_ref):
        i = pl.program_id(0)

        @pl.when(i == 0)
        def _():
            dnw_ref[...] = jnp.zeros_like(dnw_ref)
            dmod_ref[...] = jnp.zeros_like(dmod_ref)

        dp_ref[:, 0:D] = (dq_ref[0].astype(F32) + dq_ref[1].astype(F32)).astype(BF16)
        dp_ref[:, D:2 * D] = df_ref[0]
        dp_ref[:, 2 * D:3 * D] = df_ref[1]
        dp_ref[:, 3 * D:4 * D] = (di_ref[0].astype(F32) + di_ref[1].astype(F32)).astype(BF16)
        dp_ref[:, 4 * D:] = d5_ref[...]
        dh = mm_nt(dp_ref[:, 0:IN_SLOT], w_ref[0])
        for j in range(1, N_DEV):
            dh = dh + mm_nt(dp_ref[:, j * IN_SLOT:(j + 1) * IN_SLOT], w_ref[j])
        _norm_mod_bwd(x_ref, nw_ref, mod_ref, 0, dh, dx1_ref[...], dx_ref, dnw_ref, dmod_ref, i < n_ctx)

    tile = pl.BlockSpec((tm, D), lambda i: (i, 0))
    pair = pl.BlockSpec((2, tm, D), lambda i: (0, i, 0))
    return pl.pallas_call(
        body, name="in_proj_bwd_x", grid=(t // tm,),
        in_specs=[tile, tile, pair, pair, pair, pl.BlockSpec((tm, 5 * D), lambda i: (i, 0)),
                  pl.BlockSpec((1, 6, 1, D), _stream_row(tm)), pl.BlockSpec((1, D), lambda i: (0, 0)), VMEM_WHOLE],
        out_specs=[tile, pl.BlockSpec((tm, D_IN), lambda i: (i, 0)), pl.BlockSpec((1, D), lambda i: (0, 0)),
                   pl.BlockSpec((2, 2, 1, D), lambda i: (0, 0, 0, 0))],
        out_shape=[jax.ShapeDtypeStruct((t, D), F32), jax.ShapeDtypeStruct((t, D_IN), BF16), jax.ShapeDtypeStruct((1, D), F32),
                   jax.ShapeDtypeStruct((2, 2, 1, D), F32)],
        compiler_params=_cp(1))(dx1, x, dq, df, di, d5, mod, nw, wg)


def _lb_fn(h0, h1):
    m = jnp.maximum(h0, h1)
    e0, e1 = jnp.exp(h0 - m), jnp.exp(h1 - m)
    return e1 / (e0 + e1)


def lower_bounds(hlb):
    def body(h_ref, out_ref):
        out_ref[...] = _lb_fn(h_ref[0:1, :], h_ref[1:2, :])
    return pl.pallas_call(body, name="lower_bounds", out_shape=jax.ShapeDtypeStruct((1, 2 * D), F32))(hlb)


def lower_bounds_bwd(hlb, dlb1):
    def body(h_ref, d_ref, out_ref):
        _, vjp = jax.vjp(_lb_fn, h_ref[0:1, :], h_ref[1:2, :])
        d0, d1 = vjp(d_ref[...])
        out_ref[0:1, :] = d0
        out_ref[1:2, :] = d1
    return pl.pallas_call(body, name="lower_bounds_bwd", out_shape=jax.ShapeDtypeStruct((2, 2 * D), F32))(hlb, dlb1)


def _ada_fn(c_all, cctx8, w, b):
    dot = lambda a, l: jnp.dot(_silu(a), w[l], precision=HIGHEST, preferred_element_type=F32) + b[l]
    return [dot(c_all, l) for l in range(2)], [dot(cctx8, l) for l in range(2)]


def ada_fwd(c_all, cctx8, w, b):
    cols = w.shape[-1]

    def body(c_ref, cc_ref, w_ref, b_ref, out_ref):
        ox, oc = _ada_fn(c_ref[...], cc_ref[...], [w_ref[0], w_ref[1]], [b_ref[0], b_ref[1]])
        for l in range(2):
            out_ref[l, 0] = ox[l]
            out_ref[l, 1] = oc[l]
    return pl.pallas_call(body, name="ada_fwd", out_shape=jax.ShapeDtypeStruct((2, 2, N_DEV, cols), F32),
                          compiler_params=_cp(0))(c_all, cctx8, w, b)


def ada_bwd(c_all, cctx8, w, b, dmx, dmc):
    cols = w.shape[-1]

    def body(c_ref, cc_ref, w_ref, b_ref, dmx_ref, dmc_ref, dw_ref, dc_ref):
        fn = lambda cc, w0, w1: _ada_fn(c_ref[...], cc, [w0, w1], [b_ref[0], b_ref[1]])
        _, vjp = jax.vjp(fn, cc_ref[...], w_ref[0], w_ref[1])
        dcc, dw0, dw1 = vjp(([dmx_ref[0], dmx_ref[1]], [dmc_ref[0], dmc_ref[1]]))
        dw_ref[0] = dw0
        dw_ref[1] = dw1
        dc_ref[...] = jnp.sum(dcc, axis=0, keepdims=True)
    return pl.pallas_call(body, name="ada_bwd", out_shape=[jax.ShapeDtypeStruct((2, D, cols), F32), jax.ShapeDtypeStruct((1, D), F32)],
                          compiler_params=_cp(0))(c_all, cctx8, w, b, dmx, dmc)


def adamw(w, m, v, gparts, name):
    r, c = w.shape
    p = gparts.shape[0]
    rt = r
    while rt % 16 == 0 and (p + 7) * rt * c * 4 * 2 > 24 * 2 ** 20:
        rt //= 2

    def body(w_ref, m_ref, v_ref, g_ref, go_ref, d_ref, mo_ref, vo_ref):
        g = g_ref[0].astype(F32)
        for k in range(1, p):
            g = g + g_ref[k].astype(F32)
        m2 = ADAM_B1 * m_ref[...] + (1.0 - ADAM_B1) * g
        v2 = ADAM_B2 * v_ref[...] + (1.0 - ADAM_B2) * (g * g)
        m_hat = m2 / (1.0 - ADAM_B1 ** ADAM_STEP)
        v_hat = v2 / (1.0 - ADAM_B2 ** ADAM_STEP)
        go_ref[...] = g
        d_ref[...] = -ADAM_LR * (m_hat / (jnp.sqrt(v_hat) + ADAM_EPS) + ADAM_WD * w_ref[...])
        mo_ref[...] = m2
        vo_ref[...] = v2

    tile = pl.BlockSpec((rt, c), lambda i: (i, 0))
    return pl.pallas_call(
        body, name=name, grid=(r // rt,),
        in_specs=[tile, tile, tile, pl.BlockSpec((p, rt, c), lambda i: (0, i, 0))], out_specs=[tile] * 4,
        out_shape=[jax.ShapeDtypeStruct((r, c), F32)] * 4, compiler_params=_cp(1))(w, m, v, gparts)


def _me():
    x, y, c = lax.axis_index("x"), lax.axis_index("y"), lax.axis_index("c")
    return x, y, c, 4 * x + 2 * y + c


def _peer(x, y, c, p):
    fx, fy, fc = (p >> 2) & 1, (p >> 1) & 1, p & 1
    return (1 - x if fx else x, 1 - y if fy else y, 1 - c if fc else c)


def all_gather(arrs, name):
    n = len(arrs)

    def body(*refs):
        ins, outs = refs[:n], refs[n:2 * n]
        send, recv, local = refs[2 * n:]
        x, y, c, me = _me()
        copies = []
        for a in range(n):
            lc = pltpu.make_async_copy(ins[a], outs[a].at[me], local.at[a])
            lc.start()
            copies.append(lc)
            for p in range(1, N_DEV):
                cp = pltpu.make_async_remote_copy(src_ref=ins[a], dst_ref=outs[a].at[me], send_sem=send.at[a, p - 1],
                                                  recv_sem=recv.at[a, p - 1], device_id=_peer(x, y, c, p),
                                                  device_id_type=pl.DeviceIdType.MESH)
                cp.start()
                copies.append(cp)
        for cp in copies:
            cp.wait()

    return pl.pallas_call(
        body, name=name, in_specs=[ANY] * n, out_specs=[ANY] * n,
        out_shape=[jax.ShapeDtypeStruct((N_DEV,) + a.shape, a.dtype) for a in arrs],
        scratch_shapes=[pltpu.SemaphoreType.DMA((n, N_DEV - 1)), pltpu.SemaphoreType.DMA((n, N_DEV - 1)),
                        pltpu.SemaphoreType.DMA((n,))])(*arrs)


HBM = pl.BlockSpec(memory_space=pltpu.HBM)
SEM = pl.BlockSpec(memory_space=pltpu.SEMAPHORE)


def _in_hbm(a):
    return pltpu.with_memory_space_constraint(a, pltpu.HBM)


def _exchange_refs(srcs, lands, layer, scatter, a, x, y, c, p):
    me = 4 * x + 2 * y + c
    px, py, pc = _peer(x, y, c, p) if p else (x, y, c)
    dst = lands[a].at[me] if layer is None else lands[a].at[me, layer]
    src = srcs[a].at[4 * px + 2 * py + pc] if scatter else dst
    return src, dst, (px, py, pc)


def exchange_start(srcs, lands, layer, scatter, name):
    n, ns = len(lands), len(srcs)

    def body(*refs):
        ins, lz = refs[:ns], refs[ns:ns + n]
        send, recv = refs[ns + n], refs[ns + n + 1]
        token = refs[-1]
        x, y, c, _ = _me()
        for a in range(n):
            for p in range(1, N_DEV):
                src, dst, peer = _exchange_refs(ins, lz, layer, scatter, a, x, y, c, p)
                k = a * (N_DEV - 1) + p - 1
                pltpu.make_async_remote_copy(src_ref=src, dst_ref=dst, send_sem=send.at[k], recv_sem=recv.at[k],
                                             device_id=peer, device_id_type=pl.DeviceIdType.MESH).start()
        token[...] = jnp.zeros_like(token)

    thru = [pltpu.HBM(a.shape, a.dtype) for a in list(srcs) + list(lands)]
    out = pl.pallas_call(
        body, name=name, in_specs=[HBM] * (ns + n),
        out_specs=[SEM, SEM] + [HBM] * (ns + n) + [pl.BlockSpec(memory_space=pltpu.VMEM)],
        out_shape=[pltpu.SemaphoreType.DMA((n * (N_DEV - 1),)), pltpu.SemaphoreType.DMA((n * (N_DEV - 1),))] + thru
        + [jax.ShapeDtypeStruct((8, 128), F32)],
        input_output_aliases={i: 2 + i for i in range(ns + n)},
        compiler_params=pltpu.CompilerParams(has_side_effects=pltpu.SideEffectType.DATAFLOW_SIDE_EFFECTING),
    )(*[_in_hbm(a) for a in list(srcs) + list(lands)])
    return out[0], out[1], out[2:2 + ns], out[2 + ns:2 + ns + n], out[-1]


def exchange_wait(send, recv, srcs, lands, layer, scatter, after, name):
    n, ns = len(lands), len(srcs)

    def body(*refs):
        ins, lz = refs[:ns], refs[ns:ns + n]
        send_ref, recv_ref = refs[ns + n], refs[ns + n + 1]
        x, y, c, _ = _me()
        for a in range(n):
            for p in range(1, N_DEV):
                src, dst, peer = _exchange_refs(ins, lz, layer, scatter, a, x, y, c, 0)
                k = a * (N_DEV - 1) + p - 1
                cp = pltpu.make_async_remote_copy(src_ref=src, dst_ref=dst, send_sem=send_ref.at[k],
                                                  recv_sem=recv_ref.at[k], device_id=peer,
                                                  device_id_type=pl.DeviceIdType.MESH)
                cp.wait_send()
                cp.wait_recv()

    thru = [pltpu.HBM(a.shape, a.dtype) for a in list(srcs) + list(lands)]
    out = pl.pallas_call(
        body, name=name, in_specs=[HBM] * (ns + n) + [SEM, SEM, ANY], out_specs=[HBM] * (ns + n), out_shape=thru,
        input_output_aliases={i: i for i in range(ns + n)},
        compiler_params=pltpu.CompilerParams(has_side_effects=pltpu.SideEffectType.DATAFLOW_SIDE_EFFECTING),
    )(*srcs, *lands, send, recv, after)
    return out[ns:]


def place_own(src, land, me, layer, scatter, name):
    create = isinstance(land, jax.ShapeDtypeStruct)
    r, c = src.shape[-2:]
    rt = r
    while rt % 32 == 0 and rt * c * 4 > 2 ** 21:
        rt //= 2

    def body(me_ref, src_ref, *rest):
        out_ref = rest[-1]
        out_ref[...] = src_ref[...].reshape(out_ref.shape).astype(out_ref.dtype)

    src_spec = (pl.BlockSpec((1, rt, c), lambda i, m: (m[0], i, 0)) if scatter else pl.BlockSpec((rt, c), lambda i, m: (i, 0)))
    out_spec = (pl.BlockSpec((1, rt, c), lambda i, m: (m[0], i, 0)) if layer is None
                else pl.BlockSpec((1, 1, rt, c), lambda i, m: (m[0], layer, i, 0)))
    grid_spec = pltpu.PrefetchScalarGridSpec(num_scalar_prefetch=1, grid=(r // rt,),
                                             in_specs=[src_spec] + ([] if create else [ANY]), out_specs=out_spec)
    return pl.pallas_call(body, name=name, grid_spec=grid_spec, out_shape=jax.ShapeDtypeStruct(land.shape, land.dtype),
                          input_output_aliases={} if create else {2: 0}, compiler_params=_cp(1),
                          )(*((me, src) if create else (me, src, land)))


def _scan_constants():
    r = lax.broadcasted_iota(jnp.int32, (CH, CH), 0)
    s = lax.broadcasted_iota(jnp.int32, (CH, CH), 1)
    lower = (s <= r).astype(F32)
    t = jnp.arange(CH)[:, None]
    mc = jnp.stack([lower, lower.T])
    mref = jnp.stack([(t <= CH // 2 - 1).astype(F32), (t >= CH // 2).astype(F32)])
    return mc, jnp.stack([lower.T, lower]), mref


def local_step(x, ctx, target, mod, lb, w, fetch=None, publish=None, small_ready=None):
    kept = {}

    def keep(l, part, grads):
        kept[(l, part)] = grads
        return 0.0

    fetch = fetch or (lambda l, part, after: w)
    publish = publish or keep
    n_layers = len(mod)
    mc, mtc, mrefc = _scan_constants()
    xs = jnp.concatenate([ctx, x], axis=0)
    saved, big = [], []
    for l in range(n_layers):
        wl = dict(fetch(l, "in", xs))
        parts, ht = in_proj_fwd(xs, mod[l], w["nw1"][l], wl["win"][l])
        o, ck = hgrn_fwd(parts, lb[l], mc, mtc, mrefc)
        wl.update(fetch(l, "rest", o))
        x1 = mixer_fwd(xs, parts, o, mod[l], w["lnw"][l], w["lnb"][l], w["sw"][l], w["sb"][l], w["hnw"][l],
                       wl["wa"][l], wl["wb"][l], wl["wo"][l])
        av, h2t = ffn_up_fwd(x1, mod[l], w["nw2"][l], wl["wup"][l])
        x2, ac, y = ffn_down_fwd(x1, av, mod[l], w["cw"][l], w["cb"][l], wl["wd"][l])
        saved.append((xs, parts, o, ck, x1, av, ac, y, ht, h2t))
        big.append(wl)
        xs = x2
    loss, dx, dfw = loss_fwd_bwd(xs, target, w["fw"])
    g = {k: [None] * n_layers for k in ("nw1", "nw2", "lnw", "lnb", "sw", "sb", "hnw", "cw", "cb")}
    g["fw"] = dfw
    dmod, dlb = [None] * n_layers, [None] * n_layers
    tok = 0.0
    for l in reversed(range(n_layers)):
        x0, parts, o, ck, x1, av, ac, y, ht, h2t = saved[l]
        wl = big[l]
        dav, dac, dwd, dg2 = ffn_down_bwd(dx, ac, av, y, mod[l] + tok, wl["wd"][l])
        dav, g["cw"][l], g["cb"][l] = conv_bwd(dav, dac, av, w["cw"][l])
        dx1, g["nw2"][l], dmod2 = ffn_up_bwd_x(dx, x1, dav, mod[l], w["nw2"][l], wl["wup"][l])
        dwup = weight_grad(h2t, dav, FF_SLOT, "ffn_up_bwd_w")
        tok = publish(l, "ffn", {"wd": dwd, "wup": dwup})
        (d5, do, dwa, dwb, dwo, g["lnw"][l], g["lnb"][l], g["sw"][l], g["sb"][l], g["hnw"][l],
         dg1) = mixer_bwd(dx1, parts, o, mod[l] + tok, w["lnw"][l], w["lnb"][l], w["sw"][l], w["sb"][l], w["hnw"][l],
                          wl["wa"][l], wl["wb"][l], wl["wo"][l])
        tok = publish(l, "mix", {"wa": dwa, "wb": dwb, "wo": dwo})
        dq, df, di, dlb[l] = hgrn_bwd(parts, lb[l] + tok, mc, mtc, mrefc, ck, do)
        dx, dparts, g["nw1"][l], dmod1 = in_proj_bwd_x(dx1, x0, dq, df, di, d5, mod[l], w["nw1"][l], wl["win"][l])
        dmod[l] = jnp.concatenate([dmod1, dg1, dmod2, dg2], axis=1)
        tok = small_ready(g, dmod, dlb) if l == 0 and small_ready else 0.0
        dwin = weight_grad(ht, dparts, IN_SLOT, "in_proj_bwd_w", after=tok if l == 0 and small_ready else None)
        tok = publish(l, "in", {"win": dwin})
    for (l, part), grads in kept.items():
        for k, v in grads.items():
            g.setdefault(k, [None] * n_layers)[l] = v
    return loss[0, 0], dx[CTX:], g, dmod, dlb


ROW = 1024
REPLICATED = ("norm1_w", "sgu_ln_w", "sgu_ln_b", "sgu_w", "sgu_b", "hgrn_lower_bounds", "hgrn_norm_w", "norm2_w",
              "ffn_conv_b", "final_norm_w")
WEIGHT_ORDER = ("c_ctx", "ada_w", "ada_b", "norm1_w", "w_in", "sgu_ln_w", "sgu_ln_b", "sgu_w", "sgu_b", "hgrn_lower_bounds",
                "hgrn_norm_w", "w_branch_a", "w_branch_b", "w_out", "norm2_w", "ffn_w_up", "ffn_conv_w", "ffn_conv_b",
                "ffn_w_down", "final_norm_w")


def _rows_of(n):
    return -(-n // (8 * ROW)) * 8


def _pack(arrs, total_rows=None):
    parts = []
    for a in arrs:
        flat = a.reshape(-1).astype(F32)
        rows = _rows_of(flat.shape[0])
        parts.append(jnp.pad(flat, (0, rows * ROW - flat.shape[0])).reshape(rows, ROW))
    have = sum(p.shape[0] for p in parts)
    if total_rows is not None and total_rows > have:
        parts.append(jnp.zeros((total_rows - have, ROW), F32))
    return jnp.concatenate(parts, axis=0)


def _unpack(packed, shapes):
    lead = packed.shape[:-2]
    out, r0 = [], 0
    for s in shapes:
        n = math.prod(s)
        rows = _rows_of(n)
        out.append(packed[..., r0:r0 + rows, :].reshape(lead + (rows * ROW,))[..., :n].reshape(lead + tuple(s)))
        r0 += rows
    return out


def kernel(x, c, ctx, c_ctx, ada_w, ada_b, norm1_w, w_in, sgu_ln_w, sgu_ln_b, sgu_w, sgu_b, hgrn_lower_bounds, hgrn_norm_w, w_branch_a, w_branch_b, w_out, norm2_w, ffn_w_up, ffn_conv_w, ffn_conv_b, ffn_w_down, final_norm_w, loss_target, m_c_ctx, m_ada_w, m_ada_b, m_norm1_w, m_w_in, m_sgu_ln_w, m_sgu_ln_b, m_sgu_w, m_sgu_b, m_hgrn_lower_bounds, m_hgrn_norm_w, m_w_branch_a, m_w_branch_b, m_w_out, m_norm2_w, m_ffn_w_up, m_ffn_conv_w, m_ffn_conv_b, m_ffn_w_down, m_final_norm_w, v_c_ctx, v_ada_w, v_ada_b, v_norm1_w, v_w_in, v_sgu_ln_w, v_sgu_ln_b, v_sgu_w, v_sgu_b, v_hgrn_lower_bounds, v_hgrn_norm_w, v_w_branch_a, v_w_branch_b, v_w_out, v_norm2_w, v_ffn_w_up, v_ffn_conv_w, v_ffn_conv_b, v_ffn_w_down, v_final_norm_w):
    wts = dict(c_ctx=c_ctx, ada_w=ada_w, ada_b=ada_b, norm1_w=norm1_w, w_in=w_in, sgu_ln_w=sgu_ln_w, sgu_ln_b=sgu_ln_b,
               sgu_w=sgu_w, sgu_b=sgu_b, hgrn_lower_bounds=hgrn_lower_bounds, hgrn_norm_w=hgrn_norm_w, w_branch_a=w_branch_a,
               w_branch_b=w_branch_b, w_out=w_out, norm2_w=norm2_w, ffn_w_up=ffn_w_up, ffn_conv_w=ffn_conv_w,
               ffn_conv_b=ffn_conv_b, ffn_w_down=ffn_w_down, final_norm_w=final_norm_w)
    mom1 = dict(c_ctx=m_c_ctx, ada_w=m_ada_w, ada_b=m_ada_b, norm1_w=m_norm1_w, w_in=m_w_in, sgu_ln_w=m_sgu_ln_w,
                sgu_ln_b=m_sgu_ln_b, sgu_w=m_sgu_w, sgu_b=m_sgu_b, hgrn_lower_bounds=m_hgrn_lower_bounds,
                hgrn_norm_w=m_hgrn_norm_w, w_branch_a=m_w_branch_a, w_branch_b=m_w_branch_b, w_out=m_w_out, norm2_w=m_norm2_w,
                ffn_w_up=m_ffn_w_up, ffn_conv_w=m_ffn_conv_w, ffn_conv_b=m_ffn_conv_b, ffn_w_down=m_ffn_w_down,
                final_norm_w=m_final_norm_w)
    mom2 = dict(c_ctx=v_c_ctx, ada_w=v_ada_w, ada_b=v_ada_b, norm1_w=v_norm1_w, w_in=v_w_in, sgu_ln_w=v_sgu_ln_w,
                sgu_ln_b=v_sgu_ln_b, sgu_w=v_sgu_w, sgu_b=v_sgu_b, hgrn_lower_bounds=v_hgrn_lower_bounds,
                hgrn_norm_w=v_hgrn_norm_w, w_branch_a=v_w_branch_a, w_branch_b=v_w_branch_b, w_out=v_w_out, norm2_w=v_norm2_w,
                ffn_w_up=v_ffn_w_up, ffn_conv_w=v_ffn_conv_w, ffn_conv_b=v_ffn_conv_b, ffn_w_down=v_ffn_w_down,
                final_norm_w=v_final_norm_w)
    n_layers = w_in.shape[0]
    layers = range(n_layers)
    me = 4 * lax.axis_index("x") + 2 * lax.axis_index("y") + lax.axis_index("c")
    ada_cols = ada_w.shape[-1]

    big = ("w_in", "ffn_w_up", "w_branch_a", "w_branch_b", "w_out", "ffn_w_down")
    short = {"w_in": "win", "ffn_w_up": "wup", "w_branch_a": "wa", "w_branch_b": "wb", "w_out": "wo", "ffn_w_down": "wd"}
    me1 = me.reshape(1).astype(jnp.int32)
    groups = [[("w_in", 0)], [(k, 0) for k in big[1:]], [(k, 1) for k in big]]
    in_flight, started = [], 0.0
    for n, group in enumerate(groups):
        lands = [place_own(wts[k][l], jax.ShapeDtypeStruct((N_DEV,) + wts[k].shape[1:], BF16), me1, None, False,
                           f"gather_own_{short[k]}_{l}") for k, l in group]
        in_flight.append(exchange_start([], lands, None, False, f"gather_weights_start_{n}"))
        started = started + in_flight[-1][-1][0, 0]

    def as_used(k, a):
        return a if k in ("w_in", "ffn_w_up") else a.reshape(N_FFK, FF_SLOT, D) if k == "ffn_w_down" else a.reshape(D, D)

    arrived = {}

    def fetch(l, part, after):
        n = {(0, "in"): 0, (0, "rest"): 1, (1, "in"): 2}.get((l, part))
        if n is not None:
            send, recv, _, lands, _ = in_flight[n]
            got = exchange_wait(send, recv, [], lands, None, False, after, f"gather_weights_wait_{n}")
            for (k, ll), a in zip(groups[n], got):
                arrived.setdefault(short[k], [None] * n_layers)[ll] = as_used(k, a)
        return arrived

    gathered = all_gather([ffn_conv_w.reshape(n_layers, 9, -1), c], "gather_conv_c")
    conv_all, c_all = gathered[0], gathered[1].reshape(N_DEV, D)
    conv_full = [conv_all[:, l].transpose(1, 0, 2).reshape(9, N_FFK, FF_SLOT).transpose(1, 0, 2) for l in layers]

    cctx8 = jnp.broadcast_to(c_ctx[None, :], (N_DEV, D))
    ada_b_cols = lax.dynamic_slice_in_dim(ada_b, me * ada_cols, ada_cols, axis=1)[:, None, :]
    mod_cols = ada_fwd(c_all, cctx8, ada_w, ada_b_cols)
    (mod_all,) = all_gather([mod_cols], "gather_mod")
    mod_x = lax.dynamic_index_in_dim(mod_all[:, :, 0], me, axis=2, keepdims=False)
    mod_c = mod_all[:, :, 1, 0]
    mod = [jnp.stack([mod_c[:, l].reshape(6, 1, D), mod_x[:, l].reshape(6, 1, D)]) for l in layers]
    mod[0] = mod[0] + started

    lb1 = lower_bounds(hgrn_lower_bounds)
    lb = [jnp.zeros((2, 1, D), F32), lb1.reshape(2, 1, D)]

    w = {
        "nw1": [norm1_w[l][None] for l in layers], "nw2": [norm2_w[l][None] for l in layers],
        "lnw": [sgu_ln_w[l][None] for l in layers], "lnb": [sgu_ln_b[l][None] for l in layers],
        "sw": [sgu_w[l] for l in layers], "sb": [sgu_b[l][:, :, None] for l in layers],
        "hnw": [hgrn_norm_w[l][None] for l in layers], "cw": conv_full,
        "cb": [ffn_conv_b[l].reshape(N_FFK, 1, FF_SLOT) for l in layers], "fw": final_norm_w[None],
    }
    long = {v: k for k, v in short.items()}
    landing, sent = {}, []

    def publish(l, part, grads):
        keys = [long[k] for k in grads]
        slots = [a.reshape((N_DEV, -1, a.shape[-1])) for a in grads.values()]
        zones = [place_own(s, landing.get(k, jax.ShapeDtypeStruct((N_DEV, n_layers) + s.shape[1:], s.dtype)), me1, l, True,
                           f"scatter_own_{short[k]}_{l}") for k, s in zip(keys, slots)]
        send, recv, srcs, zones, token = exchange_start(slots, zones, l, True, f"scatter_grads_start_{part}_{l}")
        landing.update(zip(keys, zones))
        sent.append((keys, l, part, send, recv, srcs, token))
        return token[0, 0]

    out = {}
    flat2 = lambda a: a.reshape(-1, a.shape[-1])

    def small_ready(g, dmod, dlb):
        d_hlb = lower_bounds_bwd(hgrn_lower_bounds, dlb[1].reshape(1, 2 * D))
        st = lambda k: jnp.stack(g[k])
        rep_grads = {"norm1_w": st("nw1"), "sgu_ln_w": st("lnw"), "sgu_ln_b": st("lnb"), "sgu_w": st("sw"), "sgu_b": st("sb"),
                     "hgrn_lower_bounds": d_hlb, "hgrn_norm_w": st("hnw"), "norm2_w": st("nw2"), "ffn_conv_b": st("cb"),
                     "final_norm_w": g["fw"]}
        rep_rows = -(-sum(_rows_of(wts[k].size) for k in REPLICATED) // 64) * 64
        d_conv = jnp.stack([g["cw"][l].transpose(1, 0, 2).reshape(9, D_FF) for l in layers])
        dmod_x = jnp.stack([dmod[l][1].reshape(6 * D) for l in layers])
        dmod_c = jnp.stack([dmod[l][0].reshape(6 * D) for l in layers])
        small = jnp.concatenate([_pack([rep_grads[k] for k in REPLICATED], rep_rows), _pack([d_conv, dmod_x, dmod_c])], axis=0)
        (small_all,) = all_gather([small], "gather_small_grads")
        conv_g, dmx_all, dmc_all = _unpack(small_all[:, rep_rows:], [d_conv.shape, dmod_x.shape, dmod_c.shape])

        rep = adamw(_pack([wts[k] for k in REPLICATED], rep_rows), _pack([mom1[k] for k in REPLICATED], rep_rows),
                    _pack([mom2[k] for k in REPLICATED], rep_rows), small_all[:, :rep_rows], "adamw_replicated")
        rep = [_unpack(r, [wts[k].shape for k in REPLICATED]) for r in rep]
        for n, k in enumerate(REPLICATED):
            out[k] = tuple(r[n] for r in rep)

        conv_mine = lax.dynamic_index_in_dim(conv_g.reshape(N_DEV, n_layers, 9, N_DEV, -1), me, axis=3, keepdims=False)
        res = adamw(flat2(ffn_conv_w), flat2(m_ffn_conv_w), flat2(v_ffn_conv_w),
                    conv_mine.reshape(N_DEV, -1, conv_mine.shape[-1]), "adamw_conv_w")
        out["ffn_conv_w"] = tuple(r.reshape(ffn_conv_w.shape) for r in res)

        out["ada_b"] = tuple(adamw(ada_b, m_ada_b, v_ada_b, jnp.concatenate([dmx_all, dmc_all], axis=0), "adamw_ada_b"))

        cols_of = lambda a: lax.dynamic_slice_in_dim(a, me * ada_cols, ada_cols, axis=2).transpose(1, 0, 2)
        d_ada_w, d_cctx = ada_bwd(c_all, cctx8, ada_w, ada_b_cols, cols_of(dmx_all), cols_of(dmc_all))
        res = adamw(flat2(ada_w), flat2(m_ada_w), flat2(v_ada_w), flat2(d_ada_w)[None], "adamw_ada_w")
        out["ada_w"] = tuple(r.reshape(ada_w.shape) for r in res)
        (d_cctx_all,) = all_gather([d_cctx], "gather_c_ctx_grad")
        res = adamw(c_ctx[None], m_c_ctx[None], v_c_ctx[None], d_cctx_all, "adamw_c_ctx")
        out["c_ctx"] = tuple(r[0] for r in res)
        return jnp.minimum(jnp.abs(d_cctx_all[0, 0, 0]), 0.0)

    loss, grad_x, _, _, _ = local_step(x[0], ctx[0], loss_target[0], mod, lb, w, fetch, publish, small_ready)
    loss = lax.psum(loss, AXES)

    after = sent[-1][-1]
    for part in ("ffn", "mix", "in"):
        for keys, l, p, send, recv, srcs, _ in sent:
            if p == part:
                zones = exchange_wait(send, recv, srcs, [landing[k] for k in keys], l, True, after, f"scatter_grads_wait_{part}_{l}")
                landing.update(zip(keys, zones))
    for k in big:
        r = landing[k]
        res = adamw(flat2(wts[k]), flat2(mom1[k]), flat2(mom2[k]), r.reshape(N_DEV, -1, r.shape[-1]), "adamw_" + k)
        out[k] = tuple(a.reshape(wts[k].shape) for a in res)

    return (loss, grad_x[None]) + tuple(out[k][n] for n in range(4) for k in WEIGHT_ORDER)
```

```python
import functools
import math

import jax
import jax.numpy as jnp
from jax import lax
from jax.experimental import pallas as pl
from jax.experimental.pallas import tpu as pltpu

F32 = jnp.float32
BF16 = jnp.bfloat16
HIGHEST = lax.Precision.HIGHEST

N_DEV = 8
AXES = ("x", "y", "c")
D = 1024
CTX = 256
TM = 256
TM_SMALL = 128
CH = 64
SGU_CH = 128
HEADS = 8
HD = 128
GRID_W = 64
D_IN = 9 * D
IN_SLOT = D_IN // N_DEV
D_FF = 2816
FF_SLOT = 2 * D_FF // N_DEV
N_FFK = D_FF // FF_SLOT
RMS_EPS = 1e-6
LN_EPS = 1e-5
ADAM_LR, ADAM_B1, ADAM_B2, ADAM_EPS, ADAM_WD, ADAM_STEP = 0.001, 0.9, 0.999, 1e-08, 0.01, 10
VMEM_LIMIT_V7X = 56 * 2 ** 20
GRAD_WIRE = jnp.bfloat16

VMEM_WHOLE = pl.BlockSpec(memory_space=pltpu.VMEM)
ANY = pl.BlockSpec(memory_space=pl.ANY)


def _cp(n_axes):
    return pltpu.CompilerParams(dimension_semantics=("arbitrary",) * n_axes, vmem_limit_bytes=VMEM_LIMIT_V7X)


def _dot(a, b, dims):
    return lax.dot_general(a.astype(BF16), b.astype(BF16), (dims, ((), ())), preferred_element_type=F32)


@jax.custom_vjp
def mm(a, b):
    return _dot(a, b, ((1,), (0,)))


mm.defvjp(lambda a, b: (mm(a, b), (a, b)),
          lambda r, g: (_dot(g, r[1], ((1,), (1,))).astype(r[0].dtype), _dot(r[0], g, ((0,), (0,))).astype(r[1].dtype)))


@jax.custom_vjp
def mm_nt(a, b):
    return _dot(a, b, ((1,), (1,)))


mm_nt.defvjp(lambda a, b: (mm_nt(a, b), (a, b)),
             lambda r, g: (_dot(g, r[1], ((1,), (0,))).astype(r[0].dtype), _dot(g, r[0], ((0,), (0,))).astype(r[1].dtype)))


@jax.custom_vjp
def mm_tn(a, b):
    return _dot(a, b, ((0,), (0,)))


mm_tn.defvjp(lambda a, b: (mm_tn(a, b), (a, b)),
             lambda r, g: (_dot(r[1], g, ((1,), (1,))).astype(r[0].dtype), _dot(r[0], g, ((1,), (0,))).astype(r[1].dtype)))


def _tri_dot(m, g):
    hi = g.astype(BF16)
    rest = g - hi.astype(F32)
    mid = rest.astype(BF16)
    low = (rest - mid.astype(F32)).astype(BF16)
    n = g.shape[1]
    out = jnp.dot(m.astype(BF16), jnp.concatenate([hi, mid, low], axis=1), preferred_element_type=F32)
    return out[:, :n] + out[:, n:2 * n] + out[:, 2 * n:]


@jax.custom_vjp
def _cum(m, mt, g):
    return _tri_dot(m, g)


_cum.defvjp(lambda m, mt, g: (_cum(m, mt, g), (m, mt)),
            lambda r, d: (jnp.zeros_like(r[0]), jnp.zeros_like(r[1]), _tri_dot(r[1], d)))


def _silu(x):
    return x * jax.nn.sigmoid(x)


def _gelu(x):
    return 0.5 * x * (1.0 + jnp.tanh(math.sqrt(2.0 / math.pi) * (x + 0.044715 * (x * x * x))))


def _rms(x, w):
    return x * lax.rsqrt(jnp.mean(x * x, axis=-1, keepdims=True) + RMS_EPS) * w


def _norm_mod(x, w, shift, scale):
    return _rms(x, w) * (1.0 + scale) + shift


def _hsl(h):
    return slice(h * HD, (h + 1) * HD)


def _hgrn_chunk(st, qz, fz, iv, lb, m, mt, mref):
    hs = range(HEADS)
    keep = [1.0 - lb[h] for h in hs]
    g = [jnp.log(lb[h] + keep[h] * jax.nn.sigmoid(fz[h])) for h in hs]
    k = [keep[h] * jax.nn.sigmoid(-fz[h]) for h in hs]
    q = [_silu(qz[h]) for h in hs]
    b = [_cum(m, mt, g[h]) for h in hs]
    ref = [jnp.sum(mref * g[h], axis=0, keepdims=True) for h in hs]
    last = [jnp.sum(g[h], axis=0, keepdims=True) for h in hs]
    qa = [q[h] * jnp.exp(b[h] - ref[h]) for h in hs]
    ka = [k[h] * jnp.exp(ref[h] - b[h]) for h in hs]
    scores = [jnp.where(m > 0.5, mm_nt(qa[h], ka[h]), 0.0) for h in hs]
    inter = [mm_nt(qa[h] * jnp.exp(ref[h]), st[h]) for h in hs]
    kv = [mm_tn(iv[h], ka[h] * jnp.exp(last[h] - ref[h])) for h in hs]
    outs = [mm(scores[h], iv[h]) + inter[h] for h in hs]
    news = [jnp.exp(last[h]) * st[h] + kv[h] for h in hs]
    return outs, news


def _sgu_fn(ub, vb, lnw, lnb, sw, sb):
    gv = [_gelu(v) for v in vb]
    mu = sum(jnp.sum(t, axis=-1, keepdims=True) for t in gv) / D
    var = sum(jnp.sum((t - mu) * (t - mu), axis=-1, keepdims=True) for t in gv) / D
    inv = lax.rsqrt(var + LN_EPS)
    cols = []
    for g in range(HEADS):
        vn = (gv[g] - mu) * inv * lnw[g] + lnb[g]
        cols.append(_gelu(ub[g]) * (mm(sw[g], vn) + sb[g]))
    return jnp.concatenate(cols, axis=1)


def _readout_fn(ob, og, hnw):
    r = [o * lax.rsqrt(jnp.mean(o * o, axis=-1, keepdims=True) + RMS_EPS) * hnw for o in ob]
    return jnp.concatenate(r, axis=1) * _silu(og)


def _glu_fn(ac, v):
    return _gelu(ac) * v


def _stream_row(tm):
    n_ctx = CTX // tm
    return lambda i: (jnp.where(i < n_ctx, 0, 1), 0, 0, 0)


def in_proj_fwd(x, mod, nw, wg):
    t = x.shape[0]

    def body(x_ref, mod_ref, nw_ref, w_ref, out_ref, ht_ref):
        h32 = _norm_mod(x_ref[...], nw_ref[...], mod_ref[0, 0], mod_ref[0, 1])
        ht_ref[...] = h32.T.astype(BF16)
        h = h32.astype(BF16)
        for j in range(N_DEV):
            out_ref[:, j * IN_SLOT:(j + 1) * IN_SLOT] = jnp.dot(h, w_ref[j], preferred_element_type=F32)

    return pl.pallas_call(
        body, name="in_proj_fwd", grid=(t // TM,),
        in_specs=[pl.BlockSpec((TM, D), lambda i: (i, 0)), pl.BlockSpec((1, 6, 1, D), _stream_row(TM)),
                  pl.BlockSpec((1, D), lambda i: (0, 0)), VMEM_WHOLE],
        out_specs=[pl.BlockSpec((TM, D_IN), lambda i: (i, 0)), pl.BlockSpec((D, TM), lambda i: (0, i))],
        out_shape=[jax.ShapeDtypeStruct((t, D_IN), F32), jax.ShapeDtypeStruct((D, t), BF16)],
        compiler_params=_cp(1))(x, mod, nw, wg)


def _scan_chunk(nc):
    ncc = CTX // CH

    def chunk(d, s):
        bwd = jnp.where(s < ncc, ncc - 1 - s, nc + ncc - 1 - s)
        return jnp.where(d == 0, s, bwd)
    return chunk


def hgrn_fwd(parts, lb, mc, mtc, mrefc):
    t = parts.shape[0]
    nc = t // CH
    chunk = _scan_chunk(nc)

    def body(q_ref, f_ref, i_ref, lb_ref, m_ref, mt_ref, mr_ref, o_ref, ck_ref, st):
        @pl.when(pl.program_id(1) == 0)
        def _():
            st[...] = jnp.zeros_like(st)
        ck_ref[0, 0] = st[...]
        outs, news = _hgrn_chunk([st[h] for h in range(HEADS)], [q_ref[:, _hsl(h)] for h in range(HEADS)],
                                 [f_ref[:, _hsl(h)] for h in range(HEADS)], [i_ref[:, _hsl(h)] for h in range(HEADS)],
                                 [lb_ref[0, :, _hsl(h)] for h in range(HEADS)], m_ref[0], mt_ref[0], mr_ref[0])
        for h in range(HEADS):
            o_ref[0, :, _hsl(h)] = outs[h]
            st[h] = news[h]

    const = lambda d, s: (d, 0, 0)
    return pl.pallas_call(
        body, name="hgrn_fwd", grid=(2, nc),
        in_specs=[pl.BlockSpec((CH, D), lambda d, s: (chunk(d, s), 0)), pl.BlockSpec((CH, D), lambda d, s: (chunk(d, s), 1 + d)),
                  pl.BlockSpec((CH, D), lambda d, s: (chunk(d, s), 3)), pl.BlockSpec((1, 1, D), const),
                  pl.BlockSpec((1, CH, CH), const), pl.BlockSpec((1, CH, CH), const), pl.BlockSpec((1, CH, 1), const)],
        out_specs=[pl.BlockSpec((1, CH, D), lambda d, s: (d, chunk(d, s), 0)),
                   pl.BlockSpec((1, 1, HEADS, HD, HD), lambda d, s: (d, s, 0, 0, 0))],
        out_shape=[jax.ShapeDtypeStruct((2, t, D), F32), jax.ShapeDtypeStruct((2, nc, HEADS, HD, HD), F32)],
        scratch_shapes=[pltpu.VMEM((HEADS, HD, HD), F32)], compiler_params=_cp(2))(parts, parts, parts, lb, mc, mtc, mrefc)


def _mixer_tile(rows, u_ref, v_ref, og_ref, o_ref, lnw_ref, lnb_ref, sw_ref, sb_ref, hnw_ref):
    n = (rows.stop - rows.start) // SGU_CH
    yas, vjps = [], []
    for c in range(n):
        r = slice(rows.start + c * SGU_CH, rows.start + (c + 1) * SGU_CH)
        ya, vjp_a = jax.vjp(_sgu_fn, [u_ref[r, _hsl(g)] for g in range(HEADS)], [v_ref[r, _hsl(g)] for g in range(HEADS)],
                            [lnw_ref[:, _hsl(g)] for g in range(HEADS)], [lnb_ref[:, _hsl(g)] for g in range(HEADS)],
                            [sw_ref[g] for g in range(HEADS)], [sb_ref[g] for g in range(HEADS)])
        yas.append(ya)
        vjps.append(vjp_a)
    yb, vjp_b = jax.vjp(_readout_fn, [o_ref[0, rows, _hsl(h)] + o_ref[1, rows, _hsl(h)] for h in range(HEADS)],
                        og_ref[rows, :], hnw_ref[...])
    return (yas[0] if n == 1 else jnp.concatenate(yas, axis=0)), yb, vjps, vjp_b


def _part_specs(tm, first, n):
    return [pl.BlockSpec((tm, D), functools.partial(lambda k, i: (i, k), first + k)) for k in range(n)]


def mixer_fwd(x, parts, o, mod, lnw, lnb, sw, sb, hnw, wa, wb, wo):
    t = x.shape[0]

    def body(x_ref, u_ref, v_ref, og_ref, ga_ref, gb_ref, o_ref, mod_ref, lnw_ref, lnb_ref, sw_ref, sb_ref, hnw_ref,
             wa_ref, wb_ref, wo_ref, out_ref, pa_ref, pb_ref, y_ref, yat_ref, ybt_ref, mt_ref):
        ya, yb, _, _ = _mixer_tile(slice(0, TM), u_ref, v_ref, og_ref, o_ref, lnw_ref, lnb_ref, sw_ref, sb_ref, hnw_ref)
        pa, pb = mm(ya, wa_ref[...]), mm(yb, wb_ref[...])
        merged = jax.nn.sigmoid(ga_ref[...]) * pa + jax.nn.sigmoid(gb_ref[...]) * pb
        y = mm(merged, wo_ref[...])
        out_ref[...] = x_ref[...] + mod_ref[0, 2] * y
        pa_ref[...], pb_ref[...], y_ref[...] = pa.astype(BF16), pb.astype(BF16), y.astype(BF16)
        yat_ref[...], ybt_ref[...], mt_ref[...] = ya.T.astype(BF16), yb.T.astype(BF16), merged.T.astype(BF16)

    vec = lambda n: pl.BlockSpec((1, n), lambda i: (0, 0))
    tile = pl.BlockSpec((TM, D), lambda i: (i, 0))
    tile_t = pl.BlockSpec((D, TM), lambda i: (0, i))
    return pl.pallas_call(
        body, name="mixer_fwd", grid=(t // TM,),
        in_specs=[tile] + _part_specs(TM, 4, 5)
        + [pl.BlockSpec((2, TM, D), lambda i: (0, i, 0)), pl.BlockSpec((1, 6, 1, D), _stream_row(TM)), vec(D), vec(D),
           VMEM_WHOLE, VMEM_WHOLE, vec(HD), VMEM_WHOLE, VMEM_WHOLE, VMEM_WHOLE],
        out_specs=[tile] * 4 + [tile_t] * 3,
        out_shape=[jax.ShapeDtypeStruct((t, D), F32)] + [jax.ShapeDtypeStruct((t, D), BF16)] * 3
        + [jax.ShapeDtypeStruct((D, t), BF16)] * 3, compiler_params=_cp(1),
    )(x, parts, parts, parts, parts, parts, o, mod, lnw, lnb, sw, sb, hnw, wa, wb, wo)


def ffn_up_fwd(x, mod, nw, wg):
    t = x.shape[0]

    def body(x_ref, mod_ref, nw_ref, w_ref, out_ref, ht_ref):
        h32 = _norm_mod(x_ref[...], nw_ref[...], mod_ref[0, 3], mod_ref[0, 4])
        ht_ref[...] = h32.T.astype(BF16)
        h = h32.astype(BF16)
        for j in range(N_DEV):
            out_ref[j] = jnp.dot(h, w_ref[j], preferred_element_type=F32)

    return pl.pallas_call(
        body, name="ffn_up_fwd", grid=(t // TM,),
        in_specs=[pl.BlockSpec((TM, D), lambda i: (i, 0)), pl.BlockSpec((1, 6, 1, D), _stream_row(TM)),
                  pl.BlockSpec((1, D), lambda i: (0, 0)), VMEM_WHOLE],
        out_specs=[pl.BlockSpec((N_DEV, TM, FF_SLOT), lambda i: (0, i, 0)), pl.BlockSpec((D, TM), lambda i: (0, i))],
        out_shape=[jax.ShapeDtypeStruct((N_DEV, t, FF_SLOT), F32), jax.ShapeDtypeStruct((D, t), BF16)],
        compiler_params=_cp(1))(x, mod, nw, wg)


def _halo_specs(nt, k_of, i_of):
    per = TM // GRID_W
    last = nt * per - 1
    return [pl.BlockSpec((1, GRID_W, FF_SLOT), lambda *g: (k_of(*g), jnp.maximum(i_of(*g) * per - 1, 0), 0)),
            pl.BlockSpec((1, TM, FF_SLOT), lambda *g: (k_of(*g), i_of(*g), 0)),
            pl.BlockSpec((1, GRID_W, FF_SLOT), lambda *g: (k_of(*g), jnp.minimum(i_of(*g) * per + per, last), 0))]


def _with_halo(prev_ref, main_ref, next_ref, i, nt):
    prev = jnp.where(i >= 2, prev_ref[0], 0.0)
    nxt = jnp.where((i >= 1) & (i <= nt - 2), next_ref[0], 0.0)
    return jnp.concatenate([prev, main_ref[0], nxt], axis=0)


def _tap_valid(dc, i, n_rows, offset):
    r = lax.broadcasted_iota(jnp.int32, (n_rows, 1), 0) - offset
    col = jnp.bitwise_and(r, GRID_W - 1)
    pos = jnp.where(i == 0, r, col) + dc
    return (pos >= 0) & (pos < jnp.where(i == 0, TM, GRID_W))


def _row_weight(cw_ref, dr, dc, i):
    w = cw_ref[0, 3 * (dr + 1) + dc + 1:3 * (dr + 1) + dc + 2, :]
    return w if dr == 0 else jnp.where(i == 0, 0.0, w)


def ffn_down_fwd(x, av, mod, cw, cb, wd):
    t = x.shape[0]
    nt = t // TM
    ext = TM + 2 * GRID_W

    def body(x_ref, ap_ref, am_ref, an_ref, v_ref, mod_ref, cw_ref, cb_ref, wd_ref, out_ref, ac_ref, y_ref, z_ref, acc):
        i, k = pl.program_id(0), pl.program_id(1)
        a_ext = _with_halo(ap_ref, am_ref, an_ref, i, nt)
        conv = jnp.zeros((TM, FF_SLOT), F32) + cb_ref[0]
        for dc in (-1, 0, 1):
            rolled = a_ext if dc == 0 else pltpu.roll(a_ext, (-dc) % ext, 0)
            valid = _tap_valid(dc, i, TM, 0)
            for dr in (-1, 0, 1):
                lo = GRID_W + GRID_W * dr
                conv = conv + jnp.where(valid, rolled[lo:lo + TM], 0.0) * _row_weight(cw_ref, dr, dc, i)
        ac_ref[0] = conv
        z = _glu_fn(conv, v_ref[0]).astype(BF16)
        z_ref[0] = z
        part = mm(z, wd_ref[0])

        @pl.when(k == 0)
        def _():
            acc[...] = part

        @pl.when(k > 0)
        def _():
            acc[...] += part

        @pl.when(k == N_FFK - 1)
        def _():
            y_ref[...] = acc[...]
            out_ref[...] = x_ref[...] + mod_ref[0, 5] * acc[...]

    tile = pl.BlockSpec((TM, D), lambda i, k: (i, 0))
    return pl.pallas_call(
        body, name="ffn_down_fwd", grid=(nt, N_FFK),
        in_specs=[tile] + _halo_specs(nt, lambda i, k: k, lambda i, k: i)
        + [pl.BlockSpec((1, TM, FF_SLOT), lambda i, k: (N_FFK + k, i, 0)),
           pl.BlockSpec((1, 6, 1, D), lambda i, k: (jnp.where(i < 1, 0, 1), 0, 0, 0)),
           pl.BlockSpec((1, 9, FF_SLOT), lambda i, k: (k, 0, 0)), pl.BlockSpec((1, 1, FF_SLOT), lambda i, k: (k, 0, 0)),
           pl.BlockSpec((1, FF_SLOT, D), lambda i, k: (k, 0, 0))],
        out_specs=[tile, pl.BlockSpec((1, TM, FF_SLOT), lambda i, k: (k, i, 0)), tile,
                   pl.BlockSpec((1, TM, FF_SLOT), lambda i, k: (k, i, 0))],
        out_shape=[jax.ShapeDtypeStruct((t, D), F32), jax.ShapeDtypeStruct((N_FFK, t, FF_SLOT), F32),
                   jax.ShapeDtypeStruct((t, D), F32), jax.ShapeDtypeStruct((N_FFK, t, FF_SLOT), BF16)],
        scratch_shapes=[pltpu.VMEM((TM, D), F32)], compiler_params=_cp(2))(x, av, av, av, av, mod, cw, cb, wd)


def loss_fwd_bwd(x, target, fw):
    t = x.shape[0]

    def body(x_ref, t_ref, w_ref, loss_ref, dx_ref, dw_ref):
        i = pl.program_id(0)

        @pl.when(i == 0)
        def _():
            loss_ref[...] = jnp.zeros_like(loss_ref)
            dw_ref[...] = jnp.zeros_like(dw_ref)
            dx_ref[...] = jnp.zeros_like(dx_ref)

        @pl.when(i > 0)
        def _():
            y, vjp = jax.vjp(_rms, x_ref[...], w_ref[...])
            err = y - t_ref[...]
            loss_ref[...] += 0.5 * jnp.sum(jnp.sum(err * err, axis=-1, keepdims=True) / D)
            dx, dw = vjp(err / D)
            dx_ref[...] = dx
            dw_ref[...] += dw

    return pl.pallas_call(
        body, name="loss_fwd_bwd", grid=(t // TM,),
        in_specs=[pl.BlockSpec((TM, D), lambda i: (i, 0)), pl.BlockSpec((TM, D), lambda i: (jnp.maximum(i - 1, 0), 0)),
                  pl.BlockSpec((1, D), lambda i: (0, 0))],
        out_specs=[pl.BlockSpec((8, 128), lambda i: (0, 0)), pl.BlockSpec((TM, D), lambda i: (i, 0)),
                   pl.BlockSpec((1, D), lambda i: (0, 0))],
        out_shape=[jax.ShapeDtypeStruct((8, 128), F32), jax.ShapeDtypeStruct((t, D), F32), jax.ShapeDtypeStruct((1, D), F32)],
        compiler_params=_cp(1))(x, target, fw)


def _stream_add(ref, k, is_ctx, val):
    ref[0, k] += jnp.where(is_ctx, val, 0.0)
    ref[1, k] += jnp.where(is_ctx, 0.0, val)


def ffn_down_bwd(dx, ac, av, y, mod, wd):
    t = dx.shape[0]
    nt = t // TM

    def body(dx_ref, ac_ref, v_ref, y_ref, mod_ref, wd_ref, dav_ref, dac_ref, dout_ref, dg_ref):
        i = pl.program_id(0)

        @pl.when(i == 0)
        def _():
            dg_ref[...] = jnp.zeros_like(dg_ref)

        _stream_add(dg_ref, 0, i == 0, jnp.sum(dx_ref[...] * y_ref[...], axis=0, keepdims=True))
        dout = (mod_ref[0, 5] * dx_ref[...]).astype(BF16)
        dout_ref[...] = dout
        for k in range(N_FFK):
            _, vjp = jax.vjp(_glu_fn, ac_ref[k], v_ref[k])
            dac, dv = vjp(mm_nt(dout, wd_ref[k]))
            dac_ref[k] = dac
            dav_ref[k] = dv.astype(BF16)

    tile = pl.BlockSpec((TM, D), lambda i: (i, 0))
    half = lambda first: pl.BlockSpec((N_FFK, TM, FF_SLOT), lambda i: (first, i, 0))
    return pl.pallas_call(
        body, name="ffn_down_bwd", grid=(nt,),
        in_specs=[tile, half(0), half(1), tile, pl.BlockSpec((1, 6, 1, D), _stream_row(TM)), VMEM_WHOLE],
        out_specs=[half(1), half(0), tile, pl.BlockSpec((2, 1, 1, D), lambda i: (0, 0, 0, 0))],
        out_shape=[jax.ShapeDtypeStruct((N_DEV, t, FF_SLOT), BF16), jax.ShapeDtypeStruct((N_FFK, t, FF_SLOT), F32),
                   jax.ShapeDtypeStruct((t, D), BF16), jax.ShapeDtypeStruct((2, 1, 1, D), F32)],
        compiler_params=_cp(1))(dx, ac, av, y, mod, wd)


def conv_bwd(dav, dac, av, cw):
    t = dac.shape[1]
    nt = t // TM
    ext = TM + 2 * GRID_W

    def body(dav_in, gp_ref, gm_ref, gn_ref, ap_ref, am_ref, an_ref, cw_ref, dav_ref, dcw_ref, dcb_ref):
        k, i = pl.program_id(0), pl.program_id(1)

        @pl.when(i == 0)
        def _():
            dcw_ref[...] = jnp.zeros_like(dcw_ref)
            dcb_ref[...] = jnp.zeros_like(dcb_ref)

        g_ext = _with_halo(gp_ref, gm_ref, gn_ref, i, nt)
        a_ext = _with_halo(ap_ref, am_ref, an_ref, i, nt)
        g_main = gm_ref[0]
        dcb_ref[0] += jnp.sum(g_main, axis=0, keepdims=True)
        da = jnp.zeros((TM, FF_SLOT), F32)
        for dc in (-1, 0, 1):
            gv = jnp.where(_tap_valid(dc, i, ext, GRID_W), g_ext, 0.0)
            g_rolled = gv if dc == 0 else pltpu.roll(gv, dc % ext, 0)
            a_rolled = a_ext if dc == 0 else pltpu.roll(a_ext, (-dc) % ext, 0)
            g_valid = jnp.where(_tap_valid(dc, i, TM, 0), g_main, 0.0)
            for dr in (-1, 0, 1):
                lo = GRID_W - GRID_W * dr
                da = da + g_rolled[lo:lo + TM] * _row_weight(cw_ref, dr, dc, i)
                lo = GRID_W + GRID_W * dr
                tap = 3 * (dr + 1) + dc + 1
                dw = jnp.sum(g_valid * a_rolled[lo:lo + TM], axis=0, keepdims=True)
                dcw_ref[0, tap:tap + 1, :] += dw if dr == 0 else jnp.where(i == 0, 0.0, dw)
        dav_ref[0] = da.astype(BF16)

    return pl.pallas_call(
        body, name="conv_bwd", grid=(N_FFK, nt),
        in_specs=[ANY] + _halo_specs(nt, lambda k, i: k, lambda k, i: i) + _halo_specs(nt, lambda k, i: k, lambda k, i: i)
        + [pl.BlockSpec((1, 9, FF_SLOT), lambda k, i: (k, 0, 0))],
        out_specs=[pl.BlockSpec((1, TM, FF_SLOT), lambda k, i: (k, i, 0)), pl.BlockSpec((1, 9, FF_SLOT), lambda k, i: (k, 0, 0)),
                   pl.BlockSpec((1, 1, FF_SLOT), lambda k, i: (k, 0, 0))],
        out_shape=[jax.ShapeDtypeStruct(dav.shape, BF16), jax.ShapeDtypeStruct((N_FFK, 9, FF_SLOT), F32),
                   jax.ShapeDtypeStruct((N_FFK, 1, FF_SLOT), F32)],
        input_output_aliases={0: 0}, compiler_params=_cp(2))(dav, dac, dac, dac, av, av, av, cw)


def _norm_mod_bwd(x_ref, nw_ref, mod_ref, k_shift, dh, dx_in, dx_ref, dnw_ref, dmod_ref, is_ctx):
    _, vjp = jax.vjp(_norm_mod, x_ref[...], nw_ref[...], mod_ref[0, k_shift], mod_ref[0, k_shift + 1])
    dx, dnw, dshift, dscale = vjp(dh)
    dx_ref[...] = dx_in + dx
    dnw_ref[...] += dnw
    _stream_add(dmod_ref, 0, is_ctx, dshift)
    _stream_add(dmod_ref, 1, is_ctx, dscale)


def ffn_up_bwd_x(dx2, x, dav, mod, nw, wg):
    t = x.shape[0]

    def body(dx2_ref, x_ref, dav_ref, mod_ref, nw_ref, w_ref, dx_ref, dnw_ref, dmod_ref):
        i = pl.program_id(0)

        @pl.when(i == 0)
        def _():
            dnw_ref[...] = jnp.zeros_like(dnw_ref)
            dmod_ref[...] = jnp.zeros_like(dmod_ref)

        dh = mm_nt(dav_ref[0], w_ref[0])
        for j in range(1, N_DEV):
            dh = dh + mm_nt(dav_ref[j], w_ref[j])
        _norm_mod_bwd(x_ref, nw_ref, mod_ref, 3, dh, dx2_ref[...], dx_ref, dnw_ref, dmod_ref, i == 0)

    tile = pl.BlockSpec((TM, D), lambda i: (i, 0))
    return pl.pallas_call(
        body, name="ffn_up_bwd_x", grid=(t // TM,),
        in_specs=[tile, tile, pl.BlockSpec((N_DEV, TM, FF_SLOT), lambda i: (0, i, 0)), pl.BlockSpec((1, 6, 1, D), _stream_row(TM)),
                  pl.BlockSpec((1, D), lambda i: (0, 0)), VMEM_WHOLE],
        out_specs=[tile, pl.BlockSpec((1, D), lambda i: (0, 0)), pl.BlockSpec((2, 2, 1, D), lambda i: (0, 0, 0, 0))],
        out_shape=[jax.ShapeDtypeStruct((t, D), F32), jax.ShapeDtypeStruct((1, D), F32), jax.ShapeDtypeStruct((2, 2, 1, D), F32)],
        compiler_params=_cp(1))(dx2, x, dav, mod, nw, wg)


def weight_grad(at, dout, slot, name, after=None):
    rows, t = at.shape
    stacked = dout.ndim == 3
    n = dout.shape[0] if stacked else dout.shape[1] // slot

    def body(a_ref, d_ref, *rest):
        dw_ref = rest[-1]
        dw_ref[0] = jnp.dot(a_ref[...], d_ref[0] if stacked else d_ref[...], preferred_element_type=F32).astype(dw_ref.dtype)

    d_spec = pl.BlockSpec((1, t, slot), lambda j: (j, 0, 0)) if stacked else pl.BlockSpec((t, slot), lambda j: (0, j))
    extra = [] if after is None else [jnp.reshape(after, (1, 1))]
    return pl.pallas_call(
        body, name=name, grid=(n,), in_specs=[VMEM_WHOLE, d_spec] + [ANY] * len(extra),
        out_specs=pl.BlockSpec((1, rows, slot), lambda j: (j, 0, 0)),
        out_shape=jax.ShapeDtypeStruct((n, rows, slot), GRAD_WIRE), compiler_params=_cp(1))(at, dout, *extra)


def weight_grad_rows(at, dout, name):
    n, t, rows = at.shape
    cols = dout.shape[1]

    def body(a_ref, d_ref, dw_ref):
        dw_ref[0] = _dot(a_ref[0], d_ref[...], ((0,), (0,))).astype(dw_ref.dtype)

    return pl.pallas_call(
        body, name=name, grid=(n,), in_specs=[pl.BlockSpec((1, t, rows), lambda k: (k, 0, 0)), VMEM_WHOLE],
        out_specs=pl.BlockSpec((1, rows, cols), lambda k: (k, 0, 0)),
        out_shape=jax.ShapeDtypeStruct((n, rows, cols), GRAD_WIRE), compiler_params=_cp(1))(at, dout)


def mixer_bwd(dx, parts, o, pa, pb, y, mod, lnw, lnb, sw, sb, hnw, wa, wb, wo):
    t = dx.shape[0]
    tm = TM_SMALL
    n_ctx = CTX // tm

    def body(dx_ref, u_ref, v_ref, og_ref, ga_ref, gb_ref, o_ref, pa_ref, pb_ref, y_ref, mod_ref, lnw_ref, lnb_ref, sw_ref,
             sb_ref, hnw_ref, wa_ref, wb_ref, wo_ref, dp_ref, do_ref, dy_ref, dpa_ref, dpb_ref, dlnw_ref, dlnb_ref, dsw_ref,
             dsb_ref, dhnw_ref, dg_ref):
        i = pl.program_id(0)

        @pl.when(i == 0)
        def _():
            for r in (dlnw_ref, dlnb_ref, dsw_ref, dsb_ref, dhnw_ref, dg_ref):
                r[...] = jnp.zeros_like(r)

        _, _, vjps, vjp_b = _mixer_tile(slice(0, tm), u_ref, v_ref, og_ref, o_ref, lnw_ref, lnb_ref, sw_ref, sb_ref, hnw_ref)
        pa, pb = pa_ref[...].astype(F32), pb_ref[...].astype(F32)
        sa, sbg = jax.nn.sigmoid(ga_ref[...]), jax.nn.sigmoid(gb_ref[...])
        dxv = dx_ref[...]
        _stream_add(dg_ref, 0, i < n_ctx, jnp.sum(dxv * y_ref[...].astype(F32), axis=0, keepdims=True))
        dy = (mod_ref[0, 2] * dxv).astype(BF16)
        dy_ref[...] = dy
        dmerged = mm_nt(dy, wo_ref[...])
        dpa, dpb = (sa * dmerged).astype(BF16), (sbg * dmerged).astype(BF16)
        dpa_ref[...], dpb_ref[...] = dpa, dpb
        first = 4 * D
        dp_ref[:, first + 3 * D:first + 4 * D] = (dmerged * pa * sa * (1.0 - sa)).astype(BF16)
        dp_ref[:, first + 4 * D:first + 5 * D] = (dmerged * pb * sbg * (1.0 - sbg)).astype(BF16)
        dub, dvb, dlnw, dlnb, dsw, dsb = vjps[0](mm_nt(dpa, wa_ref[...]))
        dob, dog, dhnw = vjp_b(mm_nt(dpb, wb_ref[...]))
        dp_ref[:, first + 2 * D:first + 3 * D] = dog.astype(BF16)
        dhnw_ref[...] += dhnw
        for g in range(HEADS):
            dp_ref[:, first + g * HD:first + (g + 1) * HD] = dub[g].astype(BF16)
            dp_ref[:, first + D + g * HD:first + D + (g + 1) * HD] = dvb[g].astype(BF16)
            do_ref[:, _hsl(g)] = dob[g]
            dlnw_ref[:, _hsl(g)] += dlnw[g]
            dlnb_ref[:, _hsl(g)] += dlnb[g]
            dsw_ref[g] += dsw[g]
            dsb_ref[g] += dsb[g]

    vec = lambda n: pl.BlockSpec((1, n), lambda i: (0, 0))
    tile = pl.BlockSpec((tm, D), lambda i: (i, 0))
    sds = jax.ShapeDtypeStruct
    return pl.pallas_call(
        body, name="mixer_bwd", grid=(t // tm,),
        in_specs=[tile] + _part_specs(tm, 4, 5)
        + [pl.BlockSpec((2, tm, D), lambda i: (0, i, 0)), tile, tile, tile, pl.BlockSpec((1, 6, 1, D), _stream_row(tm)),
           vec(D), vec(D), VMEM_WHOLE, VMEM_WHOLE, vec(HD), VMEM_WHOLE, VMEM_WHOLE, VMEM_WHOLE],
        out_specs=[pl.BlockSpec((tm, D_IN), lambda i: (i, 0)), tile, tile, tile, tile, vec(D), vec(D),
                   VMEM_WHOLE, VMEM_WHOLE, vec(HD), pl.BlockSpec((2, 1, 1, D), lambda i: (0, 0, 0, 0))],
        out_shape=[sds((t, D_IN), BF16), sds((t, D), F32), sds((t, D), BF16), sds((t, D), BF16), sds((t, D), BF16),
                   sds((1, D), F32), sds((1, D), F32), sds((HEADS, SGU_CH, SGU_CH), F32), sds((HEADS, SGU_CH, 1), F32),
                   sds((1, HD), F32), sds((2, 1, 1, D), F32)],
        compiler_params=_cp(1))(dx, parts, parts, parts, parts, parts, o, pa, pb, y, mod, lnw, lnb, sw, sb, hnw, wa, wb, wo)


def hgrn_bwd(d, parts, lb, mc, mtc, mrefc, ck, do, first=None, dparts=None):
    t = parts.shape[0]
    nc = t // CH
    chunk = _scan_chunk(nc)
    rev = lambda s: chunk(d, nc - 1 - s)

    def body(q_ref, f_ref, i_ref, lb_ref, m_ref, mt_ref, mr_ref, ck_ref, do_ref, *rest):
        dst = rest[-1]
        dlb_ref = rest[-2]

        @pl.when(pl.program_id(0) == 0)
        def _():
            dst[...] = jnp.zeros_like(dst)
            dlb_ref[...] = jnp.zeros_like(dlb_ref)

        heads = range(HEADS)
        fn = functools.partial(_hgrn_chunk, m=m_ref[0], mt=mt_ref[0], mref=mr_ref[0])
        _, vjp = jax.vjp(fn, [ck_ref[0, 0, h] for h in heads], [q_ref[:, _hsl(h)] for h in heads],
                         [f_ref[:, _hsl(h)] for h in heads], [i_ref[:, _hsl(h)] for h in heads],
                         [lb_ref[0, :, _hsl(h)] for h in heads])
        dstl, dq, df, di, dlb = vjp(([do_ref[:, _hsl(h)] for h in heads], [dst[h] for h in heads]))
        for h in heads:
            dst[h] = dstl[h]
            dlb_ref[0, :, _hsl(h)] += dlb[h]
            if d == 0:
                dq_ref, df_ref, di_ref = rest[:3]
                dq_ref[:, _hsl(h)] = dq[h].astype(BF16)
                df_ref[:, _hsl(h)] = df[h].astype(BF16)
                di_ref[:, _hsl(h)] = di[h].astype(BF16)
            else:
                dq0_ref, df0_ref, di0_ref, _, dp_ref = rest[:5]
                col = lambda k: slice(k * D + h * HD, k * D + (h + 1) * HD)
                dp_ref[:, col(0)] = (dq0_ref[:, _hsl(h)].astype(F32) + dq[h]).astype(BF16)
                dp_ref[:, col(1)] = df0_ref[:, _hsl(h)]
                dp_ref[:, col(2)] = df[h].astype(BF16)
                dp_ref[:, col(3)] = (di0_ref[:, _hsl(h)].astype(F32) + di[h]).astype(BF16)

    const = lambda s: (d, 0, 0)
    at = lambda k: pl.BlockSpec((CH, D), lambda s: (rev(s), k))
    in_specs = [at(0), at(1 + d), at(3), pl.BlockSpec((1, 1, D), const), pl.BlockSpec((1, CH, CH), const),
                pl.BlockSpec((1, CH, CH), const), pl.BlockSpec((1, CH, 1), const),
                pl.BlockSpec((1, 1, HEADS, HD, HD), lambda s: (d, nc - 1 - s, 0, 0, 0)), at(0)]
    dlb_spec, dlb_shape = pl.BlockSpec((1, 1, D), lambda s: (0, 0, 0)), jax.ShapeDtypeStruct((1, 1, D), F32)
    common = dict(grid=(nc,), scratch_shapes=[pltpu.VMEM((HEADS, HD, HD), F32)], compiler_params=_cp(1))
    if d == 0:
        return pl.pallas_call(body, name="hgrn_bwd_fwd_dir", in_specs=in_specs, out_specs=[at(0)] * 3 + [dlb_spec],
                              out_shape=[jax.ShapeDtypeStruct((t, D), BF16)] * 3 + [dlb_shape], **common,
                              )(parts, parts, parts, lb, mc, mtc, mrefc, ck, do)
    return pl.pallas_call(body, name="hgrn_bwd_bwd_dir", in_specs=in_specs + [at(0)] * 3 + [ANY],
                          out_specs=[pl.BlockSpec((CH, 4 * D), lambda s: (rev(s), 0)), dlb_spec],
                          out_shape=[jax.ShapeDtypeStruct(dparts.shape, BF16), dlb_shape], input_output_aliases={12: 0},
                          **common)(parts, parts, parts, lb, mc, mtc, mrefc, ck, do, *first, dparts)


def in_proj_bwd_x(dx1, x, dparts, mod, nw, wg, after=None):
    t = x.shape[0]
    tm = TM
    n_ctx = CTX // tm

    def body(dx1_ref, x_ref, dp_ref, mod_ref, nw_ref, w_ref, *rest):
        dx_ref, dnw_ref, dmod_ref = rest[-3:]
        i = pl.program_id(0)

        @pl.when(i == 0)
        def _():
            dnw_ref[...] = jnp.zeros_like(dnw_ref)
            dmod_ref[...] = jnp.zeros_like(dmod_ref)

        dh = mm_nt(dp_ref[:, 0:IN_SLOT], w_ref[0])
        for j in range(1, N_DEV):
            dh = dh + mm_nt(dp_ref[:, j * IN_SLOT:(j + 1) * IN_SLOT], w_ref[j])
        _norm_mod_bwd(x_ref, nw_ref, mod_ref, 0, dh, dx1_ref[...], dx_ref, dnw_ref, dmod_ref, i < n_ctx)

    tile = pl.BlockSpec((tm, D), lambda i: (i, 0))
    extra = [] if after is None else [jnp.reshape(after, (1, 1))]
    return pl.pallas_call(
        body, name="in_proj_bwd_x", grid=(t // tm,),
        in_specs=[tile, tile, pl.BlockSpec((tm, D_IN), lambda i: (i, 0)), pl.BlockSpec((1, 6, 1, D), _stream_row(tm)),
                  pl.BlockSpec((1, D), lambda i: (0, 0)), VMEM_WHOLE] + [ANY] * len(extra),
        out_specs=[tile, pl.BlockSpec((1, D), lambda i: (0, 0)), pl.BlockSpec((2, 2, 1, D), lambda i: (0, 0, 0, 0))],
        out_shape=[jax.ShapeDtypeStruct((t, D), F32), jax.ShapeDtypeStruct((1, D), F32), jax.ShapeDtypeStruct((2, 2, 1, D), F32)],
        compiler_params=_cp(1))(dx1, x, dparts, mod, nw, wg, *extra)


def _lb_fn(h0, h1):
    m = jnp.maximum(h0, h1)
    e0, e1 = jnp.exp(h0 - m), jnp.exp(h1 - m)
    return e1 / (e0 + e1)


def lower_bounds(hlb):
    def body(h_ref, out_ref):
        out_ref[...] = _lb_fn(h_ref[0:1, :], h_ref[1:2, :])
    return pl.pallas_call(body, name="lower_bounds", out_shape=jax.ShapeDtypeStruct((1, 2 * D), F32))(hlb)


def lower_bounds_bwd(hlb, dlb1):
    def body(h_ref, d_ref, out_ref):
        _, vjp = jax.vjp(_lb_fn, h_ref[0:1, :], h_ref[1:2, :])
        d0, d1 = vjp(d_ref[...])
        out_ref[0:1, :] = d0
        out_ref[1:2, :] = d1
    return pl.pallas_call(body, name="lower_bounds_bwd", out_shape=jax.ShapeDtypeStruct((2, 2 * D), F32))(hlb, dlb1)


def _ada_fn(c_all, cctx8, w, b):
    dot = lambda a, l: jnp.dot(_silu(a), w[l], precision=HIGHEST, preferred_element_type=F32) + b[l]
    return [dot(c_all, l) for l in range(2)], [dot(cctx8, l) for l in range(2)]


def ada_fwd(c_all, cctx8, w, b):
    cols = w.shape[-1]

    def body(c_ref, cc_ref, w_ref, b_ref, out_ref):
        ox, oc = _ada_fn(c_ref[...], cc_ref[...], [w_ref[0], w_ref[1]], [b_ref[0], b_ref[1]])
        for l in range(2):
            out_ref[l, 0] = ox[l]
            out_ref[l, 1] = oc[l]
    return pl.pallas_call(body, name="ada_fwd", out_shape=jax.ShapeDtypeStruct((2, 2, N_DEV, cols), F32),
                          compiler_params=_cp(0))(c_all, cctx8, w, b)


def ada_bwd(c_all, cctx8, w, b, dmx, dmc):
    cols = w.shape[-1]

    def body(c_ref, cc_ref, w_ref, b_ref, dmx_ref, dmc_ref, dw_ref, dc_ref):
        fn = lambda cc, w0, w1: _ada_fn(c_ref[...], cc, [w0, w1], [b_ref[0], b_ref[1]])
        _, vjp = jax.vjp(fn, cc_ref[...], w_ref[0], w_ref[1])
        dcc, dw0, dw1 = vjp(([dmx_ref[0], dmx_ref[1]], [dmc_ref[0], dmc_ref[1]]))
        dw_ref[0] = dw0
        dw_ref[1] = dw1
        dc_ref[...] = jnp.sum(dcc, axis=0, keepdims=True)
    return pl.pallas_call(body, name="ada_bwd", out_shape=[jax.ShapeDtypeStruct((2, D, cols), F32), jax.ShapeDtypeStruct((1, D), F32)],
                          compiler_params=_cp(0))(c_all, cctx8, w, b, dmx, dmc)


def adamw(w, m, v, gparts, name):
    r, c = w.shape
    p = gparts.shape[0]
    rt = r
    while rt % 16 == 0 and (p + 7) * rt * c * 4 * 2 > 24 * 2 ** 20:
        rt //= 2

    def body(w_ref, m_ref, v_ref, g_ref, go_ref, d_ref, mo_ref, vo_ref):
        g = g_ref[0].astype(F32)
        for k in range(1, p):
            g = g + g_ref[k].astype(F32)
        m2 = ADAM_B1 * m_ref[...] + (1.0 - ADAM_B1) * g
        v2 = ADAM_B2 * v_ref[...] + (1.0 - ADAM_B2) * (g * g)
        m_hat = m2 / (1.0 - ADAM_B1 ** ADAM_STEP)
        v_hat = v2 / (1.0 - ADAM_B2 ** ADAM_STEP)
        go_ref[...] = g
        d_ref[...] = -ADAM_LR * (m_hat / (jnp.sqrt(v_hat) + ADAM_EPS) + ADAM_WD * w_ref[...])
        mo_ref[...] = m2
        vo_ref[...] = v2

    tile = pl.BlockSpec((rt, c), lambda i: (i, 0))
    return pl.pallas_call(
        body, name=name, grid=(r // rt,),
        in_specs=[tile, tile, tile, pl.BlockSpec((p, rt, c), lambda i: (0, i, 0))], out_specs=[tile] * 4,
        out_shape=[jax.ShapeDtypeStruct((r, c), F32)] * 4, compiler_params=_cp(1))(w, m, v, gparts)


def _me():
    x, y, c = lax.axis_index("x"), lax.axis_index("y"), lax.axis_index("c")
    return x, y, c, 4 * x + 2 * y + c


def _peer(x, y, c, p):
    fx, fy, fc = (p >> 2) & 1, (p >> 1) & 1, p & 1
    return (1 - x if fx else x, 1 - y if fy else y, 1 - c if fc else c)


def all_gather(arrs, name):
    n = len(arrs)

    def body(*refs):
        ins, outs = refs[:n], refs[n:2 * n]
        send, recv, local = refs[2 * n:]
        x, y, c, me = _me()
        copies = []
        for a in range(n):
            lc = pltpu.make_async_copy(ins[a], outs[a].at[me], local.at[a])
            lc.start()
            copies.append(lc)
            for p in range(1, N_DEV):
                cp = pltpu.make_async_remote_copy(src_ref=ins[a], dst_ref=outs[a].at[me], send_sem=send.at[a, p - 1],
                                                  recv_sem=recv.at[a, p - 1], device_id=_peer(x, y, c, p),
                                                  device_id_type=pl.DeviceIdType.MESH)
                cp.start()
                copies.append(cp)
        for cp in copies:
            cp.wait()

    return pl.pallas_call(
        body, name=name, in_specs=[ANY] * n, out_specs=[ANY] * n,
        out_shape=[jax.ShapeDtypeStruct((N_DEV,) + a.shape, a.dtype) for a in arrs],
        scratch_shapes=[pltpu.SemaphoreType.DMA((n, N_DEV - 1)), pltpu.SemaphoreType.DMA((n, N_DEV - 1)),
                        pltpu.SemaphoreType.DMA((n,))])(*arrs)


HBM = pl.BlockSpec(memory_space=pltpu.HBM)
SEM = pl.BlockSpec(memory_space=pltpu.SEMAPHORE)


def _in_hbm(a):
    return pltpu.with_memory_space_constraint(a, pltpu.HBM)


def _exchange_refs(srcs, lands, layer, scatter, a, x, y, c, p):
    me = 4 * x + 2 * y + c
    px, py, pc = _peer(x, y, c, p) if p else (x, y, c)
    dst = lands[a].at[me] if layer is None else lands[a].at[me, layer]
    src = srcs[a].at[4 * px + 2 * py + pc] if scatter else dst
    return src, dst, (px, py, pc)


def exchange_start(srcs, lands, layer, scatter, name):
    n, ns = len(lands), len(srcs)

    def body(*refs):
        ins, lz = refs[:ns], refs[ns:ns + n]
        send, recv = refs[ns + n], refs[ns + n + 1]
        token = refs[-1]
        x, y, c, _ = _me()
        for a in range(n):
            for p in range(1, N_DEV):
                src, dst, peer = _exchange_refs(ins, lz, layer, scatter, a, x, y, c, p)
                k = a * (N_DEV - 1) + p - 1
                pltpu.make_async_remote_copy(src_ref=src, dst_ref=dst, send_sem=send.at[k], recv_sem=recv.at[k],
                                             device_id=peer, device_id_type=pl.DeviceIdType.MESH).start()
        token[...] = jnp.zeros_like(token)

    thru = [pltpu.HBM(a.shape, a.dtype) for a in list(srcs) + list(lands)]
    out = pl.pallas_call(
        body, name=name, in_specs=[HBM] * (ns + n),
        out_specs=[SEM, SEM] + [HBM] * (ns + n) + [pl.BlockSpec(memory_space=pltpu.VMEM)],
        out_shape=[pltpu.SemaphoreType.DMA((n * (N_DEV - 1),)), pltpu.SemaphoreType.DMA((n * (N_DEV - 1),))] + thru
        + [jax.ShapeDtypeStruct((8, 128), F32)],
        input_output_aliases={i: 2 + i for i in range(ns + n)},
        compiler_params=pltpu.CompilerParams(has_side_effects=pltpu.SideEffectType.DATAFLOW_SIDE_EFFECTING),
    )(*[_in_hbm(a) for a in list(srcs) + list(lands)])
    return out[0], out[1], out[2:2 + ns], out[2 + ns:2 + ns + n], out[-1]


def exchange_wait(send, recv, srcs, lands, layer, scatter, after, name):
    n, ns = len(lands), len(srcs)

    def body(*refs):
        ins, lz = refs[:ns], refs[ns:ns + n]
        send_ref, recv_ref = refs[ns + n], refs[ns + n + 1]
        x, y, c, _ = _me()
        for a in range(n):
            for p in range(1, N_DEV):
                src, dst, peer = _exchange_refs(ins, lz, layer, scatter, a, x, y, c, 0)
                k = a * (N_DEV - 1) + p - 1
                cp = pltpu.make_async_remote_copy(src_ref=src, dst_ref=dst, send_sem=send_ref.at[k],
                                                  recv_sem=recv_ref.at[k], device_id=peer,
                                                  device_id_type=pl.DeviceIdType.MESH)
                cp.wait_send()
                cp.wait_recv()

    thru = [pltpu.HBM(a.shape, a.dtype) for a in list(srcs) + list(lands)]
    out = pl.pallas_call(
        body, name=name, in_specs=[HBM] * (ns + n) + [SEM, SEM, ANY], out_specs=[HBM] * (ns + n), out_shape=thru,
        input_output_aliases={i: i for i in range(ns + n)},
        compiler_params=pltpu.CompilerParams(has_side_effects=pltpu.SideEffectType.DATAFLOW_SIDE_EFFECTING),
    )(*srcs, *lands, send, recv, after)
    return out[ns:]


def place_own(src, land, me, layer, scatter, name):
    create = isinstance(land, jax.ShapeDtypeStruct)
    r, c = src.shape[-2:]
    rt = r
    while rt % 32 == 0 and rt * c * 4 > 2 ** 21:
        rt //= 2

    def body(me_ref, src_ref, *rest):
        out_ref = rest[-1]
        out_ref[...] = src_ref[...].reshape(out_ref.shape).astype(out_ref.dtype)

    src_spec = (pl.BlockSpec((1, rt, c), lambda i, m: (m[0], i, 0)) if scatter else pl.BlockSpec((rt, c), lambda i, m: (i, 0)))
    out_spec = (pl.BlockSpec((1, rt, c), lambda i, m: (m[0], i, 0)) if layer is None
                else pl.BlockSpec((1, 1, rt, c), lambda i, m: (m[0], layer, i, 0)))
    grid_spec = pltpu.PrefetchScalarGridSpec(num_scalar_prefetch=1, grid=(r // rt,),
                                             in_specs=[src_spec] + ([] if create else [ANY]), out_specs=out_spec)
    return pl.pallas_call(body, name=name, grid_spec=grid_spec, out_shape=jax.ShapeDtypeStruct(land.shape, land.dtype),
                          input_output_aliases={} if create else {2: 0}, compiler_params=_cp(1),
                          )(*((me, src) if create else (me, src, land)))


def _scan_constants():
    r = lax.broadcasted_iota(jnp.int32, (CH, CH), 0)
    s = lax.broadcasted_iota(jnp.int32, (CH, CH), 1)
    lower = (s <= r).astype(F32)
    t = jnp.arange(CH)[:, None]
    mc = jnp.stack([lower, lower.T])
    mref = jnp.stack([(t <= CH // 2 - 1).astype(F32), (t >= CH // 2).astype(F32)])
    return mc, jnp.stack([lower.T, lower]), mref


def local_step(x, ctx, target, mod, lb, w, fetch=None, publish=None, small_ready=None):
    kept = {}

    def keep(l, part, grads):
        kept[(l, part)] = grads
        return 0.0

    fetch = fetch or (lambda l, part, after: w)
    publish = publish or keep
    n_layers = len(mod)
    mc, mtc, mrefc = _scan_constants()
    xs = jnp.concatenate([ctx, x], axis=0)
    saved, big = [], []
    for l in range(n_layers):
        wl = dict(fetch(l, "in", xs))
        parts, ht = in_proj_fwd(xs, mod[l], w["nw1"][l], wl["win"][l])
        o, ck = hgrn_fwd(parts, lb[l], mc, mtc, mrefc)
        wl.update(fetch(l, "rest", o))
        x1, pa, pb, ym, yat, ybt, mt = mixer_fwd(xs, parts, o, mod[l], w["lnw"][l], w["lnb"][l], w["sw"][l], w["sb"][l],
                                                 w["hnw"][l], wl["wa"][l], wl["wb"][l], wl["wo"][l])
        av, h2t = ffn_up_fwd(x1, mod[l], w["nw2"][l], wl["wup"][l])
        x2, ac, y, z = ffn_down_fwd(x1, av, mod[l], w["cw"][l], w["cb"][l], wl["wd"][l])
        saved.append((xs, parts, o, ck, x1, av, ac, y, z, ht, h2t, pa, pb, ym, yat, ybt, mt))
        big.append(wl)
        xs = x2
    loss, dx, dfw = loss_fwd_bwd(xs, target, w["fw"])
    g = {k: [None] * n_layers for k in ("nw1", "nw2", "lnw", "lnb", "sw", "sb", "hnw", "cw", "cb")}
    g["fw"] = dfw
    dmod, dlb = [None] * n_layers, [None] * n_layers
    tok = 0.0
    for l in reversed(range(n_layers)):
        x0, parts, o, ck, x1, av, ac, y, z, ht, h2t, pa, pb, ym, yat, ybt, mt = saved[l]
        wl = big[l]
        dav, dac, dout, dg2 = ffn_down_bwd(dx, ac, av, y, mod[l] + tok, wl["wd"][l])
        dwd = weight_grad_rows(z, dout, "ffn_down_bwd_w")
        dav, g["cw"][l], g["cb"][l] = conv_bwd(dav, dac, av, w["cw"][l])
        dx1, g["nw2"][l], dmod2 = ffn_up_bwd_x(dx, x1, dav, mod[l], w["nw2"][l], wl["wup"][l])
        dwup = weight_grad(h2t, dav, FF_SLOT, "ffn_up_bwd_w")
        tok = publish(l, "ffn", {"wd": dwd, "wup": dwup})
        (dparts, do, dy, dpa, dpb, g["lnw"][l], g["lnb"][l], g["sw"][l], g["sb"][l], g["hnw"][l],
         dg1) = mixer_bwd(dx1, parts, o, pa, pb, ym, mod[l] + tok, w["lnw"][l], w["lnb"][l], w["sw"][l], w["sb"][l],
                          w["hnw"][l], wl["wa"][l], wl["wb"][l], wl["wo"][l])
        tok = publish(l, "mix", {"wa": weight_grad(yat, dpa, D, "mixer_bwd_wa"), "wb": weight_grad(ybt, dpb, D, "mixer_bwd_wb"),
                                 "wo": weight_grad(mt, dy, D, "mixer_bwd_wo")})
        dq, df, di, dlb_f = hgrn_bwd(0, parts, lb[l] + tok, mc, mtc, mrefc, ck, do)
        dparts, dlb_b = hgrn_bwd(1, parts, lb[l], mc, mtc, mrefc, ck, do, (dq, df, di), dparts)
        dlb[l] = jnp.concatenate([dlb_f, dlb_b], axis=0)
        tok = publish(l, "in", {"win": weight_grad(ht, dparts, IN_SLOT, "in_proj_bwd_w")})
        dx, g["nw1"][l], dmod1 = in_proj_bwd_x(dx1, x0, dparts, mod[l], w["nw1"][l], wl["win"][l], after=tok)
        dmod[l] = jnp.concatenate([dmod1, dg1, dmod2, dg2], axis=1)
    done = small_ready(loss[0, 0], g, dmod, dlb) if small_ready else 0.0
    for (l, part), grads in kept.items():
        for k, v in grads.items():
            g.setdefault(k, [None] * n_layers)[l] = v
    return loss[0, 0], dx[CTX:], g, dmod, dlb, done


ROW = 1024
REPLICATED = ("norm1_w", "sgu_ln_w", "sgu_ln_b", "sgu_w", "sgu_b", "hgrn_lower_bounds", "hgrn_norm_w", "norm2_w",
              "ffn_conv_b", "final_norm_w")
WEIGHT_ORDER = ("c_ctx", "ada_w", "ada_b", "norm1_w", "w_in", "sgu_ln_w", "sgu_ln_b", "sgu_w", "sgu_b", "hgrn_lower_bounds",
                "hgrn_norm_w", "w_branch_a", "w_branch_b", "w_out", "norm2_w", "ffn_w_up", "ffn_conv_w", "ffn_conv_b",
                "ffn_w_down", "final_norm_w")


def _rows_of(n):
    return -(-n // (8 * ROW)) * 8


def _pack(arrs, total_rows=None):
    parts = []
    for a in arrs:
        flat = a.reshape(-1).astype(F32)
        rows = _rows_of(flat.shape[0])
        parts.append(jnp.pad(flat, (0, rows * ROW - flat.shape[0])).reshape(rows, ROW))
    have = sum(p.shape[0] for p in parts)
    if total_rows is not None and total_rows > have:
        parts.append(jnp.zeros((total_rows - have, ROW), F32))
    return jnp.concatenate(parts, axis=0)


def _unpack(packed, shapes):
    lead = packed.shape[:-2]
    out, r0 = [], 0
    for s in shapes:
        n = math.prod(s)
        rows = _rows_of(n)
        out.append(packed[..., r0:r0 + rows, :].reshape(lead + (rows * ROW,))[..., :n].reshape(lead + tuple(s)))
        r0 += rows
    return out


def kernel(x, c, ctx, c_ctx, ada_w, ada_b, norm1_w, w_in, sgu_ln_w, sgu_ln_b, sgu_w, sgu_b, hgrn_lower_bounds, hgrn_norm_w, w_branch_a, w_branch_b, w_out, norm2_w, ffn_w_up, ffn_conv_w, ffn_conv_b, ffn_w_down, final_norm_w, loss_target, m_c_ctx, m_ada_w, m_ada_b, m_norm1_w, m_w_in, m_sgu_ln_w, m_sgu_ln_b, m_sgu_w, m_sgu_b, m_hgrn_lower_bounds, m_hgrn_norm_w, m_w_branch_a, m_w_branch_b, m_w_out, m_norm2_w, m_ffn_w_up, m_ffn_conv_w, m_ffn_conv_b, m_ffn_w_down, m_final_norm_w, v_c_ctx, v_ada_w, v_ada_b, v_norm1_w, v_w_in, v_sgu_ln_w, v_sgu_ln_b, v_sgu_w, v_sgu_b, v_hgrn_lower_bounds, v_hgrn_norm_w, v_w_branch_a, v_w_branch_b, v_w_out, v_norm2_w, v_ffn_w_up, v_ffn_conv_w, v_ffn_conv_b, v_ffn_w_down, v_final_norm_w):
    wts = dict(c_ctx=c_ctx, ada_w=ada_w, ada_b=ada_b, norm1_w=norm1_w, w_in=w_in, sgu_ln_w=sgu_ln_w, sgu_ln_b=sgu_ln_b,
               sgu_w=sgu_w, sgu_b=sgu_b, hgrn_lower_bounds=hgrn_lower_bounds, hgrn_norm_w=hgrn_norm_w, w_branch_a=w_branch_a,
               w_branch_b=w_branch_b, w_out=w_out, norm2_w=norm2_w, ffn_w_up=ffn_w_up, ffn_conv_w=ffn_conv_w,
               ffn_conv_b=ffn_conv_b, ffn_w_down=ffn_w_down, final_norm_w=final_norm_w)
    mom1 = dict(c_ctx=m_c_ctx, ada_w=m_ada_w, ada_b=m_ada_b, norm1_w=m_norm1_w, w_in=m_w_in, sgu_ln_w=m_sgu_ln_w,
                sgu_ln_b=m_sgu_ln_b, sgu_w=m_sgu_w, sgu_b=m_sgu_b, hgrn_lower_bounds=m_hgrn_lower_bounds,
                hgrn_norm_w=m_hgrn_norm_w, w_branch_a=m_w_branch_a, w_branch_b=m_w_branch_b, w_out=m_w_out, norm2_w=m_norm2_w,
                ffn_w_up=m_ffn_w_up, ffn_conv_w=m_ffn_conv_w, ffn_conv_b=m_ffn_conv_b, ffn_w_down=m_ffn_w_down,
                final_norm_w=m_final_norm_w)
    mom2 = dict(c_ctx=v_c_ctx, ada_w=v_ada_w, ada_b=v_ada_b, norm1_w=v_norm1_w, w_in=v_w_in, sgu_ln_w=v_sgu_ln_w,
                sgu_ln_b=v_sgu_ln_b, sgu_w=v_sgu_w, sgu_b=v_sgu_b, hgrn_lower_bounds=v_hgrn_lower_bounds,
                hgrn_norm_w=v_hgrn_norm_w, w_branch_a=v_w_branch_a, w_branch_b=v_w_branch_b, w_out=v_w_out, norm2_w=v_norm2_w,
                ffn_w_up=v_ffn_w_up, ffn_conv_w=v_ffn_conv_w, ffn_conv_b=v_ffn_conv_b, ffn_w_down=v_ffn_w_down,
                final_norm_w=v_final_norm_w)
    n_layers = w_in.shape[0]
    layers = range(n_layers)
    me = 4 * lax.axis_index("x") + 2 * lax.axis_index("y") + lax.axis_index("c")
    ada_cols = ada_w.shape[-1]

    big = ("w_in", "ffn_w_up", "w_branch_a", "w_branch_b", "w_out", "ffn_w_down")
    short = {"w_in": "win", "ffn_w_up": "wup", "w_branch_a": "wa", "w_branch_b": "wb", "w_out": "wo", "ffn_w_down": "wd"}
    me1 = me.reshape(1).astype(jnp.int32)
    groups = [[("w_in", 0)], [(k, 0) for k in big[1:]], [(k, 1) for k in big]]
    in_flight, started = [], 0.0
    for n, group in enumerate(groups):
        lands = [place_own(wts[k][l], jax.ShapeDtypeStruct((N_DEV,) + wts[k].shape[1:], BF16), me1, None, False,
                           f"gather_own_{short[k]}_{l}") for k, l in group]
        in_flight.append(exchange_start([], lands, None, False, f"gather_weights_start_{n}"))
        started = started + in_flight[-1][-1][0, 0]

    def as_used(k, a):
        return a if k in ("w_in", "ffn_w_up") else a.reshape(N_FFK, FF_SLOT, D) if k == "ffn_w_down" else a.reshape(D, D)

    arrived = {}

    def fetch(l, part, after):
        n = {(0, "in"): 0, (0, "rest"): 1, (1, "in"): 2}.get((l, part))
        if n is not None:
            send, recv, _, lands, _ = in_flight[n]
            got = exchange_wait(send, recv, [], lands, None, False, after, f"gather_weights_wait_{n}")
            for (k, ll), a in zip(groups[n], got):
                arrived.setdefault(short[k], [None] * n_layers)[ll] = as_used(k, a)
        return arrived

    gathered = all_gather([ffn_conv_w.reshape(n_layers, 9, -1), c], "gather_conv_c")
    conv_all, c_all = gathered[0], gathered[1].reshape(N_DEV, D)
    conv_full = [conv_all[:, l].transpose(1, 0, 2).reshape(9, N_FFK, FF_SLOT).transpose(1, 0, 2) for l in layers]

    cctx8 = jnp.broadcast_to(c_ctx[None, :], (N_DEV, D))
    ada_b_cols = lax.dynamic_slice_in_dim(ada_b, me * ada_cols, ada_cols, axis=1)[:, None, :]
    mod_cols = ada_fwd(c_all, cctx8, ada_w, ada_b_cols)
    (mod_all,) = all_gather([mod_cols], "gather_mod")
    mod_x = lax.dynamic_index_in_dim(mod_all[:, :, 0], me, axis=2, keepdims=False)
    mod_c = mod_all[:, :, 1, 0]
    mod = [jnp.stack([mod_c[:, l].reshape(6, 1, D), mod_x[:, l].reshape(6, 1, D)]) for l in layers]
    mod[0] = mod[0] + started

    lb1 = lower_bounds(hgrn_lower_bounds)
    lb = [jnp.zeros((2, 1, D), F32), lb1.reshape(2, 1, D)]

    w = {
        "nw1": [norm1_w[l][None] for l in layers], "nw2": [norm2_w[l][None] for l in layers],
        "lnw": [sgu_ln_w[l][None] for l in layers], "lnb": [sgu_ln_b[l][None] for l in layers],
        "sw": [sgu_w[l] for l in layers], "sb": [sgu_b[l][:, :, None] for l in layers],
        "hnw": [hgrn_norm_w[l][None] for l in layers], "cw": conv_full,
        "cb": [ffn_conv_b[l].reshape(N_FFK, 1, FF_SLOT) for l in layers], "fw": final_norm_w[None],
    }
    long = {v: k for k, v in short.items()}
    landing, sent = {}, []

    def publish(l, part, grads):
        keys = [long[k] for k in grads]
        slots = [a.reshape((N_DEV, -1, a.shape[-1])) for a in grads.values()]
        zones = [place_own(s, landing.get(k, jax.ShapeDtypeStruct((N_DEV, n_layers) + s.shape[1:], s.dtype)), me1, l, True,
                           f"scatter_own_{short[k]}_{l}") for k, s in zip(keys, slots)]
        send, recv, srcs, zones, token = exchange_start(slots, zones, l, True, f"scatter_grads_start_{part}_{l}")
        landing.update(zip(keys, zones))
        sent.append((keys, l, part, send, recv, srcs, token))
        return token[0, 0]

    out = {}
    flat2 = lambda a: a.reshape(-1, a.shape[-1])

    def small_ready(loss_part, g, dmod, dlb):
        d_hlb = lower_bounds_bwd(hgrn_lower_bounds, dlb[1].reshape(1, 2 * D))
        st = lambda k: jnp.stack(g[k])
        rep_grads = {"norm1_w": st("nw1"), "sgu_ln_w": st("lnw"), "sgu_ln_b": st("lnb"), "sgu_w": st("sw"), "sgu_b": st("sb"),
                     "hgrn_lower_bounds": d_hlb, "hgrn_norm_w": st("hnw"), "norm2_w": st("nw2"), "ffn_conv_b": st("cb"),
                     "final_norm_w": g["fw"]}
        rep_rows = -(-sum(_rows_of(wts[k].size) for k in REPLICATED) // 64) * 64
        d_conv = jnp.stack([g["cw"][l].transpose(1, 0, 2).reshape(9, D_FF) for l in layers])
        dmod_x = jnp.stack([dmod[l][1].reshape(6 * D) for l in layers])
        dmod_c = jnp.stack([dmod[l][0].reshape(6 * D) for l in layers])
        small = jnp.concatenate([_pack([rep_grads[k] for k in REPLICATED], rep_rows),
                                 _pack([d_conv, dmod_x, dmod_c, loss_part.reshape(1)])], axis=0)
        (small_all,) = all_gather([small], "gather_small_grads")
        conv_g, dmx_all, dmc_all, loss_all = _unpack(small_all[:, rep_rows:], [d_conv.shape, dmod_x.shape, dmod_c.shape, (1,)])
        out["loss"] = functools.reduce(lambda a, b: a + b, [loss_all[k, 0] for k in range(N_DEV)])

        rep = adamw(_pack([wts[k] for k in REPLICATED], rep_rows), _pack([mom1[k] for k in REPLICATED], rep_rows),
                    _pack([mom2[k] for k in REPLICATED], rep_rows), small_all[:, :rep_rows], "adamw_replicated")
        rep = [_unpack(r, [wts[k].shape for k in REPLICATED]) for r in rep]
        for n, k in enumerate(REPLICATED):
            out[k] = tuple(r[n] for r in rep)

        conv_mine = lax.dynamic_index_in_dim(conv_g.reshape(N_DEV, n_layers, 9, N_DEV, -1), me, axis=3, keepdims=False)
        res = adamw(flat2(ffn_conv_w), flat2(m_ffn_conv_w), flat2(v_ffn_conv_w),
                    conv_mine.reshape(N_DEV, -1, conv_mine.shape[-1]), "adamw_conv_w")
        out["ffn_conv_w"] = tuple(r.reshape(ffn_conv_w.shape) for r in res)

        out["ada_b"] = tuple(adamw(ada_b, m_ada_b, v_ada_b, jnp.concatenate([dmx_all, dmc_all], axis=0), "adamw_ada_b"))

        cols_of = lambda a: lax.dynamic_slice_in_dim(a, me * ada_cols, ada_cols, axis=2).transpose(1, 0, 2)
        d_ada_w, d_cctx = ada_bwd(c_all, cctx8, ada_w, ada_b_cols, cols_of(dmx_all), cols_of(dmc_all))
        res = adamw(flat2(ada_w), flat2(m_ada_w), flat2(v_ada_w), flat2(d_ada_w)[None], "adamw_ada_w")
        out["ada_w"] = tuple(r.reshape(ada_w.shape) for r in res)
        (d_cctx_all,) = all_gather([d_cctx], "gather_c_ctx_grad")
        res = adamw(c_ctx[None], m_c_ctx[None], v_c_ctx[None], d_cctx_all, "adamw_c_ctx")
        out["c_ctx"] = tuple(r[0] for r in res)
        return d_cctx_all

    _, grad_x, _, _, _, small_done = local_step(x[0], ctx[0], loss_target[0], mod, lb, w, fetch, publish, small_ready)
    loss = out["loss"]

    after = small_done
    for part in ("ffn", "mix", "in"):
        for keys, l, p, send, recv, srcs, _ in sent:
            if p == part:
                zones = exchange_wait(send, recv, srcs, [landing[k] for k in keys], l, True, after, f"scatter_grads_wait_{part}_{l}")
                landing.update(zip(keys, zones))
    for k in big:
        r = landing[k]
        res = adamw(flat2(wts[k]), flat2(mom1[k]), flat2(mom2[k]), r.reshape(N_DEV, -1, r.shape[-1]), "adamw_" + k)
        out[k] = tuple(a.reshape(wts[k].shape) for a in res)

    return (loss, grad_x[None]) + tuple(out[k][n] for n in range(4) for k in WEIGHT_ORDER)
```

```python
import functools
import math

import jax
import jax.numpy as jnp
from jax import lax
from jax.experimental import pallas as pl
from jax.experimental.pallas import tpu as pltpu

F32 = jnp.float32
BF16 = jnp.bfloat16
HIGHEST = lax.Precision.HIGHEST

N_DEV = 8
AXES = ("x", "y", "c")
D = 1024
CTX = 256
TM = 256
TM_SMALL = 128
CH = 64
SGU_CH = 128
HEADS = 8
HD = 128
GRID_W = 64
D_IN = 9 * D
IN_SLOT = D_IN // N_DEV
D_FF = 2816
FF_SLOT = 2 * D_FF // N_DEV
N_FFK = D_FF // FF_SLOT
RMS_EPS = 1e-6
LN_EPS = 1e-5
ADAM_LR, ADAM_B1, ADAM_B2, ADAM_EPS, ADAM_WD, ADAM_STEP = 0.001, 0.9, 0.999, 1e-08, 0.01, 10
VMEM_LIMIT_V7X = 56 * 2 ** 20
GRAD_WIRE = jnp.bfloat16

VMEM_WHOLE = pl.BlockSpec(memory_space=pltpu.VMEM)
ANY = pl.BlockSpec(memory_space=pl.ANY)


def _cp(n_axes):
    return pltpu.CompilerParams(dimension_semantics=("arbitrary",) * n_axes, vmem_limit_bytes=VMEM_LIMIT_V7X)


def _dot(a, b, dims):
    return lax.dot_general(a.astype(BF16), b.astype(BF16), (dims, ((), ())), preferred_element_type=F32)


@jax.custom_vjp
def mm(a, b):
    return _dot(a, b, ((1,), (0,)))


mm.defvjp(lambda a, b: (mm(a, b), (a, b)),
          lambda r, g: (_dot(g, r[1], ((1,), (1,))).astype(r[0].dtype), _dot(r[0], g, ((0,), (0,))).astype(r[1].dtype)))


@jax.custom_vjp
def mm_nt(a, b):
    return _dot(a, b, ((1,), (1,)))


mm_nt.defvjp(lambda a, b: (mm_nt(a, b), (a, b)),
             lambda r, g: (_dot(g, r[1], ((1,), (0,))).astype(r[0].dtype), _dot(g, r[0], ((0,), (0,))).astype(r[1].dtype)))


@jax.custom_vjp
def mm_tn(a, b):
    return _dot(a, b, ((0,), (0,)))


mm_tn.defvjp(lambda a, b: (mm_tn(a, b), (a, b)),
             lambda r, g: (_dot(r[1], g, ((1,), (1,))).astype(r[0].dtype), _dot(r[0], g, ((1,), (0,))).astype(r[1].dtype)))


def _tri_dot(m, g):
    hi = g.astype(BF16)
    rest = g - hi.astype(F32)
    mid = rest.astype(BF16)
    low = (rest - mid.astype(F32)).astype(BF16)
    n = g.shape[1]
    out = jnp.dot(m.astype(BF16), jnp.concatenate([hi, mid, low], axis=1), preferred_element_type=F32)
    return out[:, :n] + out[:, n:2 * n] + out[:, 2 * n:]


@jax.custom_vjp
def _cum(m, mt, g):
    return _tri_dot(m, g)


_cum.defvjp(lambda m, mt, g: (_cum(m, mt, g), (m, mt)),
            lambda r, d: (jnp.zeros_like(r[0]), jnp.zeros_like(r[1]), _tri_dot(r[1], d)))


def _silu(x):
    return x * jax.nn.sigmoid(x)


def _gelu(x):
    return 0.5 * x * (1.0 + jnp.tanh(math.sqrt(2.0 / math.pi) * (x + 0.044715 * (x * x * x))))


def _rms(x, w):
    return x * lax.rsqrt(jnp.mean(x * x, axis=-1, keepdims=True) + RMS_EPS) * w


def _norm_mod(x, w, shift, scale):
    return _rms(x, w) * (1.0 + scale) + shift


def _hsl(h):
    return slice(h * HD, (h + 1) * HD)


def _hgrn_chunk(st, qz, fz, iv, lb, m, mt, mref):
    hs = range(HEADS)
    keep = [1.0 - lb[h] for h in hs]
    g = [jnp.log(lb[h] + keep[h] * jax.nn.sigmoid(fz[h])) for h in hs]
    k = [keep[h] * jax.nn.sigmoid(-fz[h]) for h in hs]
    q = [_silu(qz[h]) for h in hs]
    b = [_cum(m, mt, g[h]) for h in hs]
    ref = [jnp.sum(mref * g[h], axis=0, keepdims=True) for h in hs]
    last = [jnp.sum(g[h], axis=0, keepdims=True) for h in hs]
    qa = [q[h] * jnp.exp(b[h] - ref[h]) for h in hs]
    ka = [k[h] * jnp.exp(ref[h] - b[h]) for h in hs]
    scores = [jnp.where(m > 0.5, mm_nt(qa[h], ka[h]), 0.0) for h in hs]
    inter = [mm_nt(qa[h] * jnp.exp(ref[h]), st[h]) for h in hs]
    kv = [mm_tn(iv[h], ka[h] * jnp.exp(last[h] - ref[h])) for h in hs]
    outs = [mm(scores[h], iv[h]) + inter[h] for h in hs]
    news = [jnp.exp(last[h]) * st[h] + kv[h] for h in hs]
    return outs, news


def _sgu_fn(ub, vb, lnw, lnb, sw, sb):
    gv = [_gelu(v) for v in vb]
    mu = sum(jnp.sum(t, axis=-1, keepdims=True) for t in gv) / D
    var = sum(jnp.sum((t - mu) * (t - mu), axis=-1, keepdims=True) for t in gv) / D
    inv = lax.rsqrt(var + LN_EPS)
    cols = []
    for g in range(HEADS):
        vn = (gv[g] - mu) * inv * lnw[g] + lnb[g]
        cols.append(_gelu(ub[g]) * (mm(sw[g], vn) + sb[g]))
    return jnp.concatenate(cols, axis=1)


def _readout_fn(ob, og, hnw):
    r = [o * lax.rsqrt(jnp.mean(o * o, axis=-1, keepdims=True) + RMS_EPS) * hnw for o in ob]
    return jnp.concatenate(r, axis=1) * _silu(og)


def _glu_fn(ac, v):
    return _gelu(ac) * v


def _stream_row(tm):
    n_ctx = CTX // tm
    return lambda i: (jnp.where(i < n_ctx, 0, 1), 0, 0, 0)


def in_proj_fwd(x, mod, nw, wg):
    t = x.shape[0]

    def body(x_ref, mod_ref, nw_ref, w_ref, out_ref, ht_ref):
        h32 = _norm_mod(x_ref[...], nw_ref[...], mod_ref[0, 0], mod_ref[0, 1])
        ht_ref[...] = h32.T.astype(BF16)
        h = h32.astype(BF16)
        for j in range(N_DEV):
            out_ref[:, j * IN_SLOT:(j + 1) * IN_SLOT] = jnp.dot(h, w_ref[j], preferred_element_type=F32)

    return pl.pallas_call(
        body, name="in_proj_fwd", grid=(t // TM,),
        in_specs=[pl.BlockSpec((TM, D), lambda i: (i, 0)), pl.BlockSpec((1, 6, 1, D), _stream_row(TM)),
                  pl.BlockSpec((1, D), lambda i: (0, 0)), VMEM_WHOLE],
        out_specs=[pl.BlockSpec((TM, D_IN), lambda i: (i, 0)), pl.BlockSpec((D, TM), lambda i: (0, i))],
        out_shape=[jax.ShapeDtypeStruct((t, D_IN), F32), jax.ShapeDtypeStruct((D, t), BF16)],
        compiler_params=_cp(1))(x, mod, nw, wg)


def _scan_chunk(nc):
    ncc = CTX // CH

    def chunk(d, s):
        bwd = jnp.where(s < ncc, ncc - 1 - s, nc + ncc - 1 - s)
        return jnp.where(d == 0, s, bwd)
    return chunk


def hgrn_fwd(parts, lb, mc, mtc, mrefc):
    t = parts.shape[0]
    nc = t // CH
    chunk = _scan_chunk(nc)

    def body(q_ref, f_ref, i_ref, lb_ref, m_ref, mt_ref, mr_ref, o_ref, ck_ref, st):
        @pl.when(pl.program_id(1) == 0)
        def _():
            st[...] = jnp.zeros_like(st)
        ck_ref[0, 0] = st[...]
        outs, news = _hgrn_chunk([st[h] for h in range(HEADS)], [q_ref[:, _hsl(h)] for h in range(HEADS)],
                                 [f_ref[:, _hsl(h)] for h in range(HEADS)], [i_ref[:, _hsl(h)] for h in range(HEADS)],
                                 [lb_ref[0, :, _hsl(h)] for h in range(HEADS)], m_ref[0], mt_ref[0], mr_ref[0])
        for h in range(HEADS):
            o_ref[0, :, _hsl(h)] = outs[h]
            st[h] = news[h]

    const = lambda d, s: (d, 0, 0)
    return pl.pallas_call(
        body, name="hgrn_fwd", grid=(2, nc),
        in_specs=[pl.BlockSpec((CH, D), lambda d, s: (chunk(d, s), 0)), pl.BlockSpec((CH, D), lambda d, s: (chunk(d, s), 1 + d)),
                  pl.BlockSpec((CH, D), lambda d, s: (chunk(d, s), 3)), pl.BlockSpec((1, 1, D), const),
                  pl.BlockSpec((1, CH, CH), const), pl.BlockSpec((1, CH, CH), const), pl.BlockSpec((1, CH, 1), const)],
        out_specs=[pl.BlockSpec((1, CH, D), lambda d, s: (d, chunk(d, s), 0)),
                   pl.BlockSpec((1, 1, HEADS, HD, HD), lambda d, s: (d, s, 0, 0, 0))],
        out_shape=[jax.ShapeDtypeStruct((2, t, D), F32), jax.ShapeDtypeStruct((2, nc, HEADS, HD, HD), F32)],
        scratch_shapes=[pltpu.VMEM((HEADS, HD, HD), F32)], compiler_params=_cp(2))(parts, parts, parts, lb, mc, mtc, mrefc)


def _mixer_tile(rows, u_ref, v_ref, og_ref, o_ref, lnw_ref, lnb_ref, sw_ref, sb_ref, hnw_ref):
    n = (rows.stop - rows.start) // SGU_CH
    yas, vjps = [], []
    for c in range(n):
        r = slice(rows.start + c * SGU_CH, rows.start + (c + 1) * SGU_CH)
        ya, vjp_a = jax.vjp(_sgu_fn, [u_ref[r, _hsl(g)] for g in range(HEADS)], [v_ref[r, _hsl(g)] for g in range(HEADS)],
                            [lnw_ref[:, _hsl(g)] for g in range(HEADS)], [lnb_ref[:, _hsl(g)] for g in range(HEADS)],
                            [sw_ref[g] for g in range(HEADS)], [sb_ref[g] for g in range(HEADS)])
        yas.append(ya)
        vjps.append(vjp_a)
    yb, vjp_b = jax.vjp(_readout_fn, [o_ref[0, rows, _hsl(h)] + o_ref[1, rows, _hsl(h)] for h in range(HEADS)],
                        og_ref[rows, :], hnw_ref[...])
    return (yas[0] if n == 1 else jnp.concatenate(yas, axis=0)), yb, vjps, vjp_b


def _part_specs(tm, first, n):
    return [pl.BlockSpec((tm, D), functools.partial(lambda k, i: (i, k), first + k)) for k in range(n)]


def mixer_fwd(x, parts, o, mod, lnw, lnb, sw, sb, hnw, wa, wb, wo):
    t = x.shape[0]

    def body(x_ref, u_ref, v_ref, og_ref, ga_ref, gb_ref, o_ref, mod_ref, lnw_ref, lnb_ref, sw_ref, sb_ref, hnw_ref,
             wa_ref, wb_ref, wo_ref, out_ref, pa_ref, pb_ref, y_ref, yat_ref, ybt_ref, mt_ref):
        ya, yb, _, _ = _mixer_tile(slice(0, TM), u_ref, v_ref, og_ref, o_ref, lnw_ref, lnb_ref, sw_ref, sb_ref, hnw_ref)
        pa, pb = mm(ya, wa_ref[...]), mm(yb, wb_ref[...])
        merged = jax.nn.sigmoid(ga_ref[...]) * pa + jax.nn.sigmoid(gb_ref[...]) * pb
        y = mm(merged, wo_ref[...])
        out_ref[...] = x_ref[...] + mod_ref[0, 2] * y
        pa_ref[...], pb_ref[...], y_ref[...] = pa.astype(BF16), pb.astype(BF16), y.astype(BF16)
        yat_ref[...], ybt_ref[...], mt_ref[...] = ya.T.astype(BF16), yb.T.astype(BF16), merged.T.astype(BF16)

    vec = lambda n: pl.BlockSpec((1, n), lambda i: (0, 0))
    tile = pl.BlockSpec((TM, D), lambda i: (i, 0))
    tile_t = pl.BlockSpec((D, TM), lambda i: (0, i))
    return pl.pallas_call(
        body, name="mixer_fwd", grid=(t // TM,),
        in_specs=[tile] + _part_specs(TM, 4, 5)
        + [pl.BlockSpec((2, TM, D), lambda i: (0, i, 0)), pl.BlockSpec((1, 6, 1, D), _stream_row(TM)), vec(D), vec(D),
           VMEM_WHOLE, VMEM_WHOLE, vec(HD), VMEM_WHOLE, VMEM_WHOLE, VMEM_WHOLE],
        out_specs=[tile] * 4 + [tile_t] * 3,
        out_shape=[jax.ShapeDtypeStruct((t, D), F32)] + [jax.ShapeDtypeStruct((t, D), BF16)] * 3
        + [jax.ShapeDtypeStruct((D, t), BF16)] * 3, compiler_params=_cp(1),
    )(x, parts, parts, parts, parts, parts, o, mod, lnw, lnb, sw, sb, hnw, wa, wb, wo)


def ffn_up_fwd(x, mod, nw, wg):
    t = x.shape[0]

    def body(x_ref, mod_ref, nw_ref, w_ref, out_ref, ht_ref):
        h32 = _norm_mod(x_ref[...], nw_ref[...], mod_ref[0, 3], mod_ref[0, 4])
        ht_ref[...] = h32.T.astype(BF16)
        h = h32.astype(BF16)
        for j in range(N_DEV):
            out_ref[j] = jnp.dot(h, w_ref[j], preferred_element_type=F32)

    return pl.pallas_call(
        body, name="ffn_up_fwd", grid=(t // TM,),
        in_specs=[pl.BlockSpec((TM, D), lambda i: (i, 0)), pl.BlockSpec((1, 6, 1, D), _stream_row(TM)),
                  pl.BlockSpec((1, D), lambda i: (0, 0)), VMEM_WHOLE],
        out_specs=[pl.BlockSpec((N_DEV, TM, FF_SLOT), lambda i: (0, i, 0)), pl.BlockSpec((D, TM), lambda i: (0, i))],
        out_shape=[jax.ShapeDtypeStruct((N_DEV, t, FF_SLOT), F32), jax.ShapeDtypeStruct((D, t), BF16)],
        compiler_params=_cp(1))(x, mod, nw, wg)


def _halo_specs(nt, k_of, i_of):
    per = TM // GRID_W
    last = nt * per - 1
    return [pl.BlockSpec((1, GRID_W, FF_SLOT), lambda *g: (k_of(*g), jnp.maximum(i_of(*g) * per - 1, 0), 0)),
            pl.BlockSpec((1, TM, FF_SLOT), lambda *g: (k_of(*g), i_of(*g), 0)),
            pl.BlockSpec((1, GRID_W, FF_SLOT), lambda *g: (k_of(*g), jnp.minimum(i_of(*g) * per + per, last), 0))]


def _with_halo(prev_ref, main_ref, next_ref, i, nt):
    prev = jnp.where(i >= 2, prev_ref[0], 0.0)
    nxt = jnp.where((i >= 1) & (i <= nt - 2), next_ref[0], 0.0)
    return jnp.concatenate([prev, main_ref[0], nxt], axis=0)


def _tap_valid(dc, i, n_rows, offset):
    r = lax.broadcasted_iota(jnp.int32, (n_rows, 1), 0) - offset
    col = jnp.bitwise_and(r, GRID_W - 1)
    pos = jnp.where(i == 0, r, col) + dc
    return (pos >= 0) & (pos < jnp.where(i == 0, TM, GRID_W))


def _row_weight(cw_ref, dr, dc, i):
    w = cw_ref[0, 3 * (dr + 1) + dc + 1:3 * (dr + 1) + dc + 2, :]
    return w if dr == 0 else jnp.where(i == 0, 0.0, w)


def ffn_down_fwd(x, av, mod, cw, cb, wd):
    t = x.shape[0]
    nt = t // TM
    ext = TM + 2 * GRID_W

    def body(x_ref, ap_ref, am_ref, an_ref, v_ref, mod_ref, cw_ref, cb_ref, wd_ref, out_ref, ac_ref, y_ref, z_ref, acc):
        i, k = pl.program_id(0), pl.program_id(1)
        a_ext = _with_halo(ap_ref, am_ref, an_ref, i, nt)
        conv = jnp.zeros((TM, FF_SLOT), F32) + cb_ref[0]
        for dc in (-1, 0, 1):
            rolled = a_ext if dc == 0 else jnp.where(_tap_valid(dc, i, ext, GRID_W), pltpu.roll(a_ext, (-dc) % ext, 0), 0.0)
            for dr in (-1, 0, 1):
                lo = GRID_W + GRID_W * dr
                conv = conv + rolled[lo:lo + TM] * _row_weight(cw_ref, dr, dc, i)
        ac_ref[0] = conv
        z = _glu_fn(conv, v_ref[0]).astype(BF16)
        z_ref[0] = z
        part = mm(z, wd_ref[0])

        @pl.when(k == 0)
        def _():
            acc[...] = part

        @pl.when(k > 0)
        def _():
            acc[...] += part

        @pl.when(k == N_FFK - 1)
        def _():
            y_ref[...] = acc[...]
            out_ref[...] = x_ref[...] + mod_ref[0, 5] * acc[...]

    tile = pl.BlockSpec((TM, D), lambda i, k: (i, 0))
    return pl.pallas_call(
        body, name="ffn_down_fwd", grid=(nt, N_FFK),
        in_specs=[tile] + _halo_specs(nt, lambda i, k: k, lambda i, k: i)
        + [pl.BlockSpec((1, TM, FF_SLOT), lambda i, k: (N_FFK + k, i, 0)),
           pl.BlockSpec((1, 6, 1, D), lambda i, k: (jnp.where(i < 1, 0, 1), 0, 0, 0)),
           pl.BlockSpec((1, 9, FF_SLOT), lambda i, k: (k, 0, 0)), pl.BlockSpec((1, 1, FF_SLOT), lambda i, k: (k, 0, 0)),
           pl.BlockSpec((1, FF_SLOT, D), lambda i, k: (k, 0, 0))],
        out_specs=[tile, pl.BlockSpec((1, TM, FF_SLOT), lambda i, k: (k, i, 0)), tile,
                   pl.BlockSpec((1, TM, FF_SLOT), lambda i, k: (k, i, 0))],
        out_shape=[jax.ShapeDtypeStruct((t, D), F32), jax.ShapeDtypeStruct((N_FFK, t, FF_SLOT), F32),
                   jax.ShapeDtypeStruct((t, D), F32), jax.ShapeDtypeStruct((N_FFK, t, FF_SLOT), BF16)],
        scratch_shapes=[pltpu.VMEM((TM, D), F32)], compiler_params=_cp(2))(x, av, av, av, av, mod, cw, cb, wd)


def loss_fwd_bwd(x, target, fw):
    t = x.shape[0]

    def body(x_ref, t_ref, w_ref, loss_ref, dx_ref, dw_ref):
        i = pl.program_id(0)

        @pl.when(i == 0)
        def _():
            loss_ref[...] = jnp.zeros_like(loss_ref)
            dw_ref[...] = jnp.zeros_like(dw_ref)
            dx_ref[...] = jnp.zeros_like(dx_ref)

        @pl.when(i > 0)
        def _():
            y, vjp = jax.vjp(_rms, x_ref[...], w_ref[...])
            err = y - t_ref[...]
            loss_ref[...] += 0.5 * jnp.sum(jnp.sum(err * err, axis=-1, keepdims=True) / D)
            dx, dw = vjp(err / D)
            dx_ref[...] = dx
            dw_ref[...] += dw

    return pl.pallas_call(
        body, name="loss_fwd_bwd", grid=(t // TM,),
        in_specs=[pl.BlockSpec((TM, D), lambda i: (i, 0)), pl.BlockSpec((TM, D), lambda i: (jnp.maximum(i - 1, 0), 0)),
                  pl.BlockSpec((1, D), lambda i: (0, 0))],
        out_specs=[pl.BlockSpec((8, 128), lambda i: (0, 0)), pl.BlockSpec((TM, D), lambda i: (i, 0)),
                   pl.BlockSpec((1, D), lambda i: (0, 0))],
        out_shape=[jax.ShapeDtypeStruct((8, 128), F32), jax.ShapeDtypeStruct((t, D), F32), jax.ShapeDtypeStruct((1, D), F32)],
        compiler_params=_cp(1))(x, target, fw)


def _stream_add(ref, k, is_ctx, val):
    ref[0, k] += jnp.where(is_ctx, val, 0.0)
    ref[1, k] += jnp.where(is_ctx, 0.0, val)


def ffn_down_bwd(dx, ac, av, y, mod, wd):
    t = dx.shape[0]
    nt = t // TM

    def body(dx_ref, ac_ref, v_ref, y_ref, mod_ref, wd_ref, dav_ref, dac_ref, dout_ref, dg_ref):
        i = pl.program_id(0)

        @pl.when(i == 0)
        def _():
            dg_ref[...] = jnp.zeros_like(dg_ref)

        _stream_add(dg_ref, 0, i == 0, jnp.sum(dx_ref[...] * y_ref[...], axis=0, keepdims=True))
        dout = (mod_ref[0, 5] * dx_ref[...]).astype(BF16)
        dout_ref[...] = dout
        for k in range(N_FFK):
            _, vjp = jax.vjp(_glu_fn, ac_ref[k], v_ref[k])
            dac, dv = vjp(mm_nt(dout, wd_ref[k]))
            dac_ref[k] = dac
            dav_ref[k] = dv.astype(BF16)

    tile = pl.BlockSpec((TM, D), lambda i: (i, 0))
    half = lambda first: pl.BlockSpec((N_FFK, TM, FF_SLOT), lambda i: (first, i, 0))
    return pl.pallas_call(
        body, name="ffn_down_bwd", grid=(nt,),
        in_specs=[tile, half(0), half(1), tile, pl.BlockSpec((1, 6, 1, D), _stream_row(TM)), VMEM_WHOLE],
        out_specs=[half(1), half(0), tile, pl.BlockSpec((2, 1, 1, D), lambda i: (0, 0, 0, 0))],
        out_shape=[jax.ShapeDtypeStruct((N_DEV, t, FF_SLOT), BF16), jax.ShapeDtypeStruct((N_FFK, t, FF_SLOT), F32),
                   jax.ShapeDtypeStruct((t, D), BF16), jax.ShapeDtypeStruct((2, 1, 1, D), F32)],
        compiler_params=_cp(1))(dx, ac, av, y, mod, wd)


def conv_bwd(dav, dac, av, cw):
    t = dac.shape[1]
    nt = t // TM
    ext = TM + 2 * GRID_W

    def body(dav_in, gp_ref, gm_ref, gn_ref, ap_ref, am_ref, an_ref, cw_ref, dav_ref, dcw_ref, dcb_ref):
        k, i = pl.program_id(0), pl.program_id(1)

        @pl.when(i == 0)
        def _():
            dcw_ref[...] = jnp.zeros_like(dcw_ref)
            dcb_ref[...] = jnp.zeros_like(dcb_ref)

        g_ext = _with_halo(gp_ref, gm_ref, gn_ref, i, nt)
        a_ext = _with_halo(ap_ref, am_ref, an_ref, i, nt)
        g_main = gm_ref[0]
        dcb_ref[0] += jnp.sum(g_main, axis=0, keepdims=True)
        da = jnp.zeros((TM, FF_SLOT), F32)
        for dc in (-1, 0, 1):
            g_rolled = g_ext if dc == 0 else pltpu.roll(jnp.where(_tap_valid(dc, i, ext, GRID_W), g_ext, 0.0), dc % ext, 0)
            a_rolled = a_ext if dc == 0 else pltpu.roll(a_ext, (-dc) % ext, 0)
            g_valid = g_main if dc == 0 else jnp.where(_tap_valid(dc, i, TM, 0), g_main, 0.0)
            for dr in (-1, 0, 1):
                lo = GRID_W - GRID_W * dr
                da = da + g_rolled[lo:lo + TM] * _row_weight(cw_ref, dr, dc, i)
                lo = GRID_W + GRID_W * dr
                tap = 3 * (dr + 1) + dc + 1
                dw = jnp.sum(g_valid * a_rolled[lo:lo + TM], axis=0, keepdims=True)
                dcw_ref[0, tap:tap + 1, :] += dw if dr == 0 else jnp.where(i == 0, 0.0, dw)
        dav_ref[0] = da.astype(BF16)

    return pl.pallas_call(
        body, name="conv_bwd", grid=(N_FFK, nt),
        in_specs=[ANY] + _halo_specs(nt, lambda k, i: k, lambda k, i: i) + _halo_specs(nt, lambda k, i: k, lambda k, i: i)
        + [pl.BlockSpec((1, 9, FF_SLOT), lambda k, i: (k, 0, 0))],
        out_specs=[pl.BlockSpec((1, TM, FF_SLOT), lambda k, i: (k, i, 0)), pl.BlockSpec((1, 9, FF_SLOT), lambda k, i: (k, 0, 0)),
                   pl.BlockSpec((1, 1, FF_SLOT), lambda k, i: (k, 0, 0))],
        out_shape=[jax.ShapeDtypeStruct(dav.shape, BF16), jax.ShapeDtypeStruct((N_FFK, 9, FF_SLOT), F32),
                   jax.ShapeDtypeStruct((N_FFK, 1, FF_SLOT), F32)],
        input_output_aliases={0: 0}, compiler_params=_cp(2))(dav, dac, dac, dac, av, av, av, cw)


def _norm_mod_bwd(x_ref, nw_ref, mod_ref, k_shift, dh, dx_in, dx_ref, dnw_ref, dmod_ref, is_ctx):
    _, vjp = jax.vjp(_norm_mod, x_ref[...], nw_ref[...], mod_ref[0, k_shift], mod_ref[0, k_shift + 1])
    dx, dnw, dshift, dscale = vjp(dh)
    dx_ref[...] = dx_in + dx
    dnw_ref[...] += dnw
    _stream_add(dmod_ref, 0, is_ctx, dshift)
    _stream_add(dmod_ref, 1, is_ctx, dscale)


def ffn_up_bwd_x(dx2, x, dav, mod, nw, wg):
    t = x.shape[0]

    def body(dx2_ref, x_ref, dav_ref, mod_ref, nw_ref, w_ref, dx_ref, dnw_ref, dmod_ref):
        i = pl.program_id(0)

        @pl.when(i == 0)
        def _():
            dnw_ref[...] = jnp.zeros_like(dnw_ref)
            dmod_ref[...] = jnp.zeros_like(dmod_ref)

        dh = mm_nt(dav_ref[0], w_ref[0])
        for j in range(1, N_DEV):
            dh = dh + mm_nt(dav_ref[j], w_ref[j])
        _norm_mod_bwd(x_ref, nw_ref, mod_ref, 3, dh, dx2_ref[...], dx_ref, dnw_ref, dmod_ref, i == 0)

    tile = pl.BlockSpec((TM, D), lambda i: (i, 0))
    return pl.pallas_call(
        body, name="ffn_up_bwd_x", grid=(t // TM,),
        in_specs=[tile, tile, pl.BlockSpec((N_DEV, TM, FF_SLOT), lambda i: (0, i, 0)), pl.BlockSpec((1, 6, 1, D), _stream_row(TM)),
                  pl.BlockSpec((1, D), lambda i: (0, 0)), VMEM_WHOLE],
        out_specs=[tile, pl.BlockSpec((1, D), lambda i: (0, 0)), pl.BlockSpec((2, 2, 1, D), lambda i: (0, 0, 0, 0))],
        out_shape=[jax.ShapeDtypeStruct((t, D), F32), jax.ShapeDtypeStruct((1, D), F32), jax.ShapeDtypeStruct((2, 2, 1, D), F32)],
        compiler_params=_cp(1))(dx2, x, dav, mod, nw, wg)


def weight_grad(at, dout, slot, name, after=None):
    rows, t = at.shape
    stacked = dout.ndim == 3
    n = dout.shape[0] if stacked else dout.shape[1] // slot

    def body(a_ref, d_ref, *rest):
        dw_ref = rest[-1]
        dw_ref[0] = jnp.dot(a_ref[...], d_ref[0] if stacked else d_ref[...], preferred_element_type=F32).astype(dw_ref.dtype)

    d_spec = pl.BlockSpec((1, t, slot), lambda j: (j, 0, 0)) if stacked else pl.BlockSpec((t, slot), lambda j: (0, j))
    extra = [] if after is None else [jnp.reshape(after, (1, 1))]
    return pl.pallas_call(
        body, name=name, grid=(n,), in_specs=[VMEM_WHOLE, d_spec] + [ANY] * len(extra),
        out_specs=pl.BlockSpec((1, rows, slot), lambda j: (j, 0, 0)),
        out_shape=jax.ShapeDtypeStruct((n, rows, slot), GRAD_WIRE), compiler_params=_cp(1))(at, dout, *extra)


def weight_grad_rows(at, dout, name):
    n, t, rows = at.shape
    cols = dout.shape[1]

    def body(a_ref, d_ref, dw_ref):
        dw_ref[0] = _dot(a_ref[0], d_ref[...], ((0,), (0,))).astype(dw_ref.dtype)

    return pl.pallas_call(
        body, name=name, grid=(n,), in_specs=[pl.BlockSpec((1, t, rows), lambda k: (k, 0, 0)), VMEM_WHOLE],
        out_specs=pl.BlockSpec((1, rows, cols), lambda k: (k, 0, 0)),
        out_shape=jax.ShapeDtypeStruct((n, rows, cols), GRAD_WIRE), compiler_params=_cp(1))(at, dout)


def mixer_bwd(dx, parts, o, pa, pb, y, mod, lnw, lnb, sw, sb, hnw, wa, wb, wo):
    t = dx.shape[0]
    tm = TM_SMALL
    n_ctx = CTX // tm

    def body(dx_ref, u_ref, v_ref, og_ref, ga_ref, gb_ref, o_ref, pa_ref, pb_ref, y_ref, mod_ref, lnw_ref, lnb_ref, sw_ref,
             sb_ref, hnw_ref, wa_ref, wb_ref, wo_ref, dp_ref, do_ref, dy_ref, dpa_ref, dpb_ref, dlnw_ref, dlnb_ref, dsw_ref,
             dsb_ref, dhnw_ref, dg_ref):
        i = pl.program_id(0)

        @pl.when(i == 0)
        def _():
            for r in (dlnw_ref, dlnb_ref, dsw_ref, dsb_ref, dhnw_ref, dg_ref):
                r[...] = jnp.zeros_like(r)

        _, _, vjps, vjp_b = _mixer_tile(slice(0, tm), u_ref, v_ref, og_ref, o_ref, lnw_ref, lnb_ref, sw_ref, sb_ref, hnw_ref)
        pa, pb = pa_ref[...].astype(F32), pb_ref[...].astype(F32)
        sa, sbg = jax.nn.sigmoid(ga_ref[...]), jax.nn.sigmoid(gb_ref[...])
        dxv = dx_ref[...]
        _stream_add(dg_ref, 0, i < n_ctx, jnp.sum(dxv * y_ref[...].astype(F32), axis=0, keepdims=True))
        dy = (mod_ref[0, 2] * dxv).astype(BF16)
        dy_ref[...] = dy
        dmerged = mm_nt(dy, wo_ref[...])
        dpa, dpb = (sa * dmerged).astype(BF16), (sbg * dmerged).astype(BF16)
        dpa_ref[...], dpb_ref[...] = dpa, dpb
        first = 4 * D
        dp_ref[:, first + 3 * D:first + 4 * D] = (dmerged * pa * sa * (1.0 - sa)).astype(BF16)
        dp_ref[:, first + 4 * D:first + 5 * D] = (dmerged * pb * sbg * (1.0 - sbg)).astype(BF16)
        dub, dvb, dlnw, dlnb, dsw, dsb = vjps[0](mm_nt(dpa, wa_ref[...]))
        dob, dog, dhnw = vjp_b(mm_nt(dpb, wb_ref[...]))
        dp_ref[:, first + 2 * D:first + 3 * D] = dog.astype(BF16)
        dhnw_ref[...] += dhnw
        for g in range(HEADS):
            dp_ref[:, first + g * HD:first + (g + 1) * HD] = dub[g].astype(BF16)
            dp_ref[:, first + D + g * HD:first + D + (g + 1) * HD] = dvb[g].astype(BF16)
            do_ref[:, _hsl(g)] = dob[g]
            dlnw_ref[:, _hsl(g)] += dlnw[g]
            dlnb_ref[:, _hsl(g)] += dlnb[g]
            dsw_ref[g] += dsw[g]
            dsb_ref[g] += dsb[g]

    vec = lambda n: pl.BlockSpec((1, n), lambda i: (0, 0))
    tile = pl.BlockSpec((tm, D), lambda i: (i, 0))
    sds = jax.ShapeDtypeStruct
    return pl.pallas_call(
        body, name="mixer_bwd", grid=(t // tm,),
        in_specs=[tile] + _part_specs(tm, 4, 5)
        + [pl.BlockSpec((2, tm, D), lambda i: (0, i, 0)), tile, tile, tile, pl.BlockSpec((1, 6, 1, D), _stream_row(tm)),
           vec(D), vec(D), VMEM_WHOLE, VMEM_WHOLE, vec(HD), VMEM_WHOLE, VMEM_WHOLE, VMEM_WHOLE],
        out_specs=[pl.BlockSpec((tm, D_IN), lambda i: (i, 0)), tile, tile, tile, tile, vec(D), vec(D),
                   VMEM_WHOLE, VMEM_WHOLE, vec(HD), pl.BlockSpec((2, 1, 1, D), lambda i: (0, 0, 0, 0))],
        out_shape=[sds((t, D_IN), BF16), sds((t, D), F32), sds((t, D), BF16), sds((t, D), BF16), sds((t, D), BF16),
                   sds((1, D), F32), sds((1, D), F32), sds((HEADS, SGU_CH, SGU_CH), F32), sds((HEADS, SGU_CH, 1), F32),
                   sds((1, HD), F32), sds((2, 1, 1, D), F32)],
        compiler_params=_cp(1))(dx, parts, parts, parts, parts, parts, o, pa, pb, y, mod, lnw, lnb, sw, sb, hnw, wa, wb, wo)


def hgrn_bwd(d, parts, lb, mc, mtc, mrefc, ck, do, first=None, dparts=None):
    t = parts.shape[0]
    nc = t // CH
    chunk = _scan_chunk(nc)
    rev = lambda s: chunk(d, nc - 1 - s)

    def body(q_ref, f_ref, i_ref, lb_ref, m_ref, mt_ref, mr_ref, ck_ref, do_ref, *rest):
        dst = rest[-1]
        dlb_ref = rest[-2]

        @pl.when(pl.program_id(0) == 0)
        def _():
            dst[...] = jnp.zeros_like(dst)
            dlb_ref[...] = jnp.zeros_like(dlb_ref)

        heads = range(HEADS)
        fn = functools.partial(_hgrn_chunk, m=m_ref[0], mt=mt_ref[0], mref=mr_ref[0])
        _, vjp = jax.vjp(fn, [ck_ref[0, 0, h] for h in heads], [q_ref[:, _hsl(h)] for h in heads],
                         [f_ref[:, _hsl(h)] for h in heads], [i_ref[:, _hsl(h)] for h in heads],
                         [lb_ref[0, :, _hsl(h)] for h in heads])
        dstl, dq, df, di, dlb = vjp(([do_ref[:, _hsl(h)] for h in heads], [dst[h] for h in heads]))
        for h in heads:
            dst[h] = dstl[h]
            dlb_ref[0, :, _hsl(h)] += dlb[h]
            if d == 0:
                dq_ref, df_ref, di_ref = rest[:3]
                dq_ref[:, _hsl(h)] = dq[h].astype(BF16)
                df_ref[:, _hsl(h)] = df[h].astype(BF16)
                di_ref[:, _hsl(h)] = di[h].astype(BF16)
            else:
                dq0_ref, df0_ref, di0_ref, _, dp_ref = rest[:5]
                col = lambda k: slice(k * D + h * HD, k * D + (h + 1) * HD)
                dp_ref[:, col(0)] = (dq0_ref[:, _hsl(h)].astype(F32) + dq[h]).astype(BF16)
                dp_ref[:, col(1)] = df0_ref[:, _hsl(h)]
                dp_ref[:, col(2)] = df[h].astype(BF16)
                dp_ref[:, col(3)] = (di0_ref[:, _hsl(h)].astype(F32) + di[h]).astype(BF16)

    const = lambda s: (d, 0, 0)
    at = lambda k: pl.BlockSpec((CH, D), lambda s: (rev(s), k))
    in_specs = [at(0), at(1 + d), at(3), pl.BlockSpec((1, 1, D), const), pl.BlockSpec((1, CH, CH), const),
                pl.BlockSpec((1, CH, CH), const), pl.BlockSpec((1, CH, 1), const),
                pl.BlockSpec((1, 1, HEADS, HD, HD), lambda s: (d, nc - 1 - s, 0, 0, 0)), at(0)]
    dlb_spec, dlb_shape = pl.BlockSpec((1, 1, D), lambda s: (0, 0, 0)), jax.ShapeDtypeStruct((1, 1, D), F32)
    common = dict(grid=(nc,), scratch_shapes=[pltpu.VMEM((HEADS, HD, HD), F32)], compiler_params=_cp(1))
    if d == 0:
        return pl.pallas_call(body, name="hgrn_bwd_fwd_dir", in_specs=in_specs, out_specs=[at(0)] * 3 + [dlb_spec],
                              out_shape=[jax.ShapeDtypeStruct((t, D), BF16)] * 3 + [dlb_shape], **common,
                              )(parts, parts, parts, lb, mc, mtc, mrefc, ck, do)
    return pl.pallas_call(body, name="hgrn_bwd_bwd_dir", in_specs=in_specs + [at(0)] * 3 + [ANY],
                          out_specs=[pl.BlockSpec((CH, 4 * D), lambda s: (rev(s), 0)), dlb_spec],
                          out_shape=[jax.ShapeDtypeStruct(dparts.shape, BF16), dlb_shape], input_output_aliases={12: 0},
                          **common)(parts, parts, parts, lb, mc, mtc, mrefc, ck, do, *first, dparts)


def in_proj_bwd_x(dx1, x, dparts, mod, nw, wg, after=None):
    t = x.shape[0]
    tm = TM
    n_ctx = CTX // tm

    def body(dx1_ref, x_ref, dp_ref, mod_ref, nw_ref, w_ref, *rest):
        dx_ref, dnw_ref, dmod_ref = rest[-3:]
        i = pl.program_id(0)

        @pl.when(i == 0)
        def _():
            dnw_ref[...] = jnp.zeros_like(dnw_ref)
            dmod_ref[...] = jnp.zeros_like(dmod_ref)

        dh = mm_nt(dp_ref[:, 0:IN_SLOT], w_ref[0])
        for j in range(1, N_DEV):
            dh = dh + mm_nt(dp_ref[:, j * IN_SLOT:(j + 1) * IN_SLOT], w_ref[j])
        _norm_mod_bwd(x_ref, nw_ref, mod_ref, 0, dh, dx1_ref[...], dx_ref, dnw_ref, dmod_ref, i < n_ctx)

    tile = pl.BlockSpec((tm, D), lambda i: (i, 0))
    extra = [] if after is None else [jnp.reshape(after, (1, 1))]
    return pl.pallas_call(
        body, name="in_proj_bwd_x", grid=(t // tm,),
        in_specs=[tile, tile, pl.BlockSpec((tm, D_IN), lambda i: (i, 0)), pl.BlockSpec((1, 6, 1, D), _stream_row(tm)),
                  pl.BlockSpec((1, D), lambda i: (0, 0)), VMEM_WHOLE] + [ANY] * len(extra),
        out_specs=[tile, pl.BlockSpec((1, D), lambda i: (0, 0)), pl.BlockSpec((2, 2, 1, D), lambda i: (0, 0, 0, 0))],
        out_shape=[jax.ShapeDtypeStruct((t, D), F32), jax.ShapeDtypeStruct((1, D), F32), jax.ShapeDtypeStruct((2, 2, 1, D), F32)],
        compiler_params=_cp(1))(dx1, x, dparts, mod, nw, wg, *extra)


def _lb_fn(h0, h1):
    m = jnp.maximum(h0, h1)
    e0, e1 = jnp.exp(h0 - m), jnp.exp(h1 - m)
    return e1 / (e0 + e1)


def lower_bounds(hlb):
    def body(h_ref, out_ref):
        out_ref[...] = _lb_fn(h_ref[0:1, :], h_ref[1:2, :])
    return pl.pallas_call(body, name="lower_bounds", out_shape=jax.ShapeDtypeStruct((1, 2 * D), F32))(hlb)


def lower_bounds_bwd(hlb, dlb1):
    def body(h_ref, d_ref, out_ref):
        _, vjp = jax.vjp(_lb_fn, h_ref[0:1, :], h_ref[1:2, :])
        d0, d1 = vjp(d_ref[...])
        out_ref[0:1, :] = d0
        out_ref[1:2, :] = d1
    return pl.pallas_call(body, name="lower_bounds_bwd", out_shape=jax.ShapeDtypeStruct((2, 2 * D), F32))(hlb, dlb1)


def _ada_fn(c_all, cctx8, w, b):
    dot = lambda a, l: jnp.dot(_silu(a), w[l], precision=HIGHEST, preferred_element_type=F32) + b[l]
    return [dot(c_all, l) for l in range(2)], [dot(cctx8, l) for l in range(2)]


def ada_fwd(c_all, cctx8, w, b):
    cols = w.shape[-1]

    def body(c_ref, cc_ref, w_ref, b_ref, out_ref):
        ox, oc = _ada_fn(c_ref[...], cc_ref[...], [w_ref[0], w_ref[1]], [b_ref[0], b_ref[1]])
        for l in range(2):
            out_ref[l, 0] = ox[l]
            out_ref[l, 1] = oc[l]
    return pl.pallas_call(body, name="ada_fwd", out_shape=jax.ShapeDtypeStruct((2, 2, N_DEV, cols), F32),
                          compiler_params=_cp(0))(c_all, cctx8, w, b)


def ada_bwd(c_all, cctx8, w, b, dmx, dmc):
    cols = w.shape[-1]

    def body(c_ref, cc_ref, w_ref, b_ref, dmx_ref, dmc_ref, dw_ref, dc_ref):
        fn = lambda cc, w0, w1: _ada_fn(c_ref[...], cc, [w0, w1], [b_ref[0], b_ref[1]])
        _, vjp = jax.vjp(fn, cc_ref[...], w_ref[0], w_ref[1])
        dcc, dw0, dw1 = vjp(([dmx_ref[0], dmx_ref[1]], [dmc_ref[0], dmc_ref[1]]))
        dw_ref[0] = dw0
        dw_ref[1] = dw1
        dc_ref[...] = jnp.sum(dcc, axis=0, keepdims=True)
    return pl.pallas_call(body, name="ada_bwd", out_shape=[jax.ShapeDtypeStruct((2, D, cols), F32), jax.ShapeDtypeStruct((1, D), F32)],
                          compiler_params=_cp(0))(c_all, cctx8, w, b, dmx, dmc)


def adamw(w, m, v, gparts, name):
    r, c = w.shape
    p = gparts.shape[0]
    rt = r
    while rt % 16 == 0 and (p + 7) * rt * c * 4 * 2 > 24 * 2 ** 20:
        rt //= 2

    def body(w_ref, m_ref, v_ref, g_ref, go_ref, d_ref, mo_ref, vo_ref):
        g = g_ref[0].astype(F32)
        for k in range(1, p):
            g = g + g_ref[k].astype(F32)
        m2 = ADAM_B1 * m_ref[...] + (1.0 - ADAM_B1) * g
        v2 = ADAM_B2 * v_ref[...] + (1.0 - ADAM_B2) * (g * g)
        m_hat = m2 / (1.0 - ADAM_B1 ** ADAM_STEP)
        v_hat = v2 / (1.0 - ADAM_B2 ** ADAM_STEP)
        go_ref[...] = g
        d_ref[...] = -ADAM_LR * (m_hat / (jnp.sqrt(v_hat) + ADAM_EPS) + ADAM_WD * w_ref[...])
        mo_ref[...] = m2
        vo_ref[...] = v2

    tile = pl.BlockSpec((rt, c), lambda i: (i, 0))
    return pl.pallas_call(
        body, name=name, grid=(r // rt,),
        in_specs=[tile, tile, tile, pl.BlockSpec((p, rt, c), lambda i: (0, i, 0))], out_specs=[tile] * 4,
        out_shape=[jax.ShapeDtypeStruct((r, c), F32)] * 4, compiler_params=_cp(1))(w, m, v, gparts)


def _me():
    x, y, c = lax.axis_index("x"), lax.axis_index("y"), lax.axis_index("c")
    return x, y, c, 4 * x + 2 * y + c


def _peer(x, y, c, p):
    fx, fy, fc = (p >> 2) & 1, (p >> 1) & 1, p & 1
    return (1 - x if fx else x, 1 - y if fy else y, 1 - c if fc else c)


def all_gather(arrs, name, after=None):
    n = len(arrs)
    extra = [] if after is None else [after]

    def body(*refs):
        ins, outs = refs[:n], refs[n + len(extra):2 * n + len(extra)]
        send, recv, local = refs[2 * n + len(extra):]
        x, y, c, me = _me()
        copies = []
        for a in range(n):
            lc = pltpu.make_async_copy(ins[a], outs[a].at[me], local.at[a])
            lc.start()
            copies.append(lc)
            for p in range(1, N_DEV):
                cp = pltpu.make_async_remote_copy(src_ref=ins[a], dst_ref=outs[a].at[me], send_sem=send.at[a, p - 1],
                                                  recv_sem=recv.at[a, p - 1], device_id=_peer(x, y, c, p),
                                                  device_id_type=pl.DeviceIdType.MESH)
                cp.start()
                copies.append(cp)
        for cp in copies:
            cp.wait()

    return pl.pallas_call(
        body, name=name, in_specs=[ANY] * (n + len(extra)), out_specs=[ANY] * n,
        out_shape=[jax.ShapeDtypeStruct((N_DEV,) + a.shape, a.dtype) for a in arrs],
        scratch_shapes=[pltpu.SemaphoreType.DMA((n, N_DEV - 1)), pltpu.SemaphoreType.DMA((n, N_DEV - 1)),
                        pltpu.SemaphoreType.DMA((n,))])(*arrs, *extra)


HBM = pl.BlockSpec(memory_space=pltpu.HBM)
SEM = pl.BlockSpec(memory_space=pltpu.SEMAPHORE)


def _in_hbm(a):
    return pltpu.with_memory_space_constraint(a, pltpu.HBM)


def _exchange_refs(srcs, lands, layer, scatter, a, x, y, c, p):
    me = 4 * x + 2 * y + c
    px, py, pc = _peer(x, y, c, p) if p else (x, y, c)
    dst = lands[a].at[me] if layer is None else lands[a].at[me, layer]
    src = srcs[a].at[4 * px + 2 * py + pc] if scatter else dst
    return src, dst, (px, py, pc)


def exchange_start(srcs, lands, layer, scatter, name, after=None):
    n, ns = len(lands), len(srcs)
    extra = [] if after is None else [after]

    def body(*refs):
        ins, lz = refs[:ns], refs[ns:ns + n]
        send, recv = refs[ns + n + len(extra)], refs[ns + n + len(extra) + 1]
        token = refs[-1]
        x, y, c, _ = _me()
        for a in range(n):
            for p in range(1, N_DEV):
                src, dst, peer = _exchange_refs(ins, lz, layer, scatter, a, x, y, c, p)
                k = a * (N_DEV - 1) + p - 1
                pltpu.make_async_remote_copy(src_ref=src, dst_ref=dst, send_sem=send.at[k], recv_sem=recv.at[k],
                                             device_id=peer, device_id_type=pl.DeviceIdType.MESH).start()
        token[...] = jnp.zeros_like(token)

    thru = [pltpu.HBM(a.shape, a.dtype) for a in list(srcs) + list(lands)]
    out = pl.pallas_call(
        body, name=name, in_specs=[HBM] * (ns + n) + [ANY] * len(extra),
        out_specs=[SEM, SEM] + [HBM] * (ns + n) + [pl.BlockSpec(memory_space=pltpu.VMEM)],
        out_shape=[pltpu.SemaphoreType.DMA((n * (N_DEV - 1),)), pltpu.SemaphoreType.DMA((n * (N_DEV - 1),))] + thru
        + [jax.ShapeDtypeStruct((8, 128), F32)],
        input_output_aliases={i: 2 + i for i in range(ns + n)},
        compiler_params=pltpu.CompilerParams(has_side_effects=pltpu.SideEffectType.DATAFLOW_SIDE_EFFECTING),
    )(*[_in_hbm(a) for a in list(srcs) + list(lands)], *extra)
    return out[0], out[1], out[2:2 + ns], out[2 + ns:2 + ns + n], out[-1]


def exchange_wait(send, recv, srcs, lands, layer, scatter, after, name):
    n, ns = len(lands), len(srcs)

    def body(*refs):
        ins, lz = refs[:ns], refs[ns:ns + n]
        send_ref, recv_ref = refs[ns + n], refs[ns + n + 1]
        x, y, c, _ = _me()
        for a in range(n):
            for p in range(1, N_DEV):
                src, dst, peer = _exchange_refs(ins, lz, layer, scatter, a, x, y, c, 0)
                k = a * (N_DEV - 1) + p - 1
                cp = pltpu.make_async_remote_copy(src_ref=src, dst_ref=dst, send_sem=send_ref.at[k],
                                                  recv_sem=recv_ref.at[k], device_id=peer,
                                                  device_id_type=pl.DeviceIdType.MESH)
                cp.wait_send()
                cp.wait_recv()

    thru = [pltpu.HBM(a.shape, a.dtype) for a in list(srcs) + list(lands)]
    out = pl.pallas_call(
        body, name=name, in_specs=[HBM] * (ns + n) + [SEM, SEM, ANY], out_specs=[HBM] * (ns + n), out_shape=thru,
        input_output_aliases={i: i for i in range(ns + n)},
        compiler_params=pltpu.CompilerParams(has_side_effects=pltpu.SideEffectType.DATAFLOW_SIDE_EFFECTING),
    )(*srcs, *lands, send, recv, after)
    return out[ns:]


def place_own(src, land, me, layer, scatter, name):
    create = isinstance(land, jax.ShapeDtypeStruct)
    r, c = src.shape[-2:]
    rt = r
    while rt % 32 == 0 and rt * c * 4 > 2 ** 21:
        rt //= 2

    def body(me_ref, src_ref, *rest):
        out_ref = rest[-1]
        out_ref[...] = src_ref[...].reshape(out_ref.shape).astype(out_ref.dtype)

    src_spec = (pl.BlockSpec((1, rt, c), lambda i, m: (m[0], i, 0)) if scatter else pl.BlockSpec((rt, c), lambda i, m: (i, 0)))
    out_spec = (pl.BlockSpec((1, rt, c), lambda i, m: (m[0], i, 0)) if layer is None
                else pl.BlockSpec((1, 1, rt, c), lambda i, m: (m[0], layer, i, 0)))
    grid_spec = pltpu.PrefetchScalarGridSpec(num_scalar_prefetch=1, grid=(r // rt,),
                                             in_specs=[src_spec] + ([] if create else [ANY]), out_specs=out_spec)
    return pl.pallas_call(body, name=name, grid_spec=grid_spec, out_shape=jax.ShapeDtypeStruct(land.shape, land.dtype),
                          input_output_aliases={} if create else {2: 0}, compiler_params=_cp(1),
                          )(*((me, src) if create else (me, src, land)))


def _scan_constants():
    r = lax.broadcasted_iota(jnp.int32, (CH, CH), 0)
    s = lax.broadcasted_iota(jnp.int32, (CH, CH), 1)
    lower = (s <= r).astype(F32)
    t = jnp.arange(CH)[:, None]
    mc = jnp.stack([lower, lower.T])
    mref = jnp.stack([(t <= CH // 2 - 1).astype(F32), (t >= CH // 2).astype(F32)])
    return mc, jnp.stack([lower.T, lower]), mref


def local_step(x, ctx, target, mod, lb, w, fetch=None, publish=None, small_ready=None):
    kept = {}

    def keep(l, part, grads):
        kept[(l, part)] = grads
        return 0.0

    fetch = fetch or (lambda l, part, after: w)
    publish = publish or keep
    n_layers = len(mod)
    mc, mtc, mrefc = _scan_constants()
    xs = jnp.concatenate([ctx, x], axis=0)
    saved, big = [], []
    for l in range(n_layers):
        wl = dict(fetch(l, "in", xs))
        parts, ht = in_proj_fwd(xs, mod[l], w["nw1"][l], wl["win"][l])
        o, ck = hgrn_fwd(parts, lb[l], mc, mtc, mrefc)
        wl.update(fetch(l, "rest", o))
        x1, pa, pb, ym, yat, ybt, mt = mixer_fwd(xs, parts, o, mod[l], w["lnw"][l], w["lnb"][l], w["sw"][l], w["sb"][l],
                                                 w["hnw"][l], wl["wa"][l], wl["wb"][l], wl["wo"][l])
        av, h2t = ffn_up_fwd(x1, mod[l], w["nw2"][l], wl["wup"][l])
        x2, ac, y, z = ffn_down_fwd(x1, av, mod[l], w["cw"][l], w["cb"][l], wl["wd"][l])
        saved.append((xs, parts, o, ck, x1, av, ac, y, z, ht, h2t, pa, pb, ym, yat, ybt, mt))
        big.append(wl)
        xs = x2
    loss, dx, dfw = loss_fwd_bwd(xs, target, w["fw"])
    g = {k: [None] * n_layers for k in ("nw1", "nw2", "lnw", "lnb", "sw", "sb", "hnw", "cw", "cb")}
    g["fw"] = dfw
    dmod, dlb = [None] * n_layers, [None] * n_layers
    tok = 0.0
    for l in reversed(range(n_layers)):
        x0, parts, o, ck, x1, av, ac, y, z, ht, h2t, pa, pb, ym, yat, ybt, mt = saved[l]
        wl = big[l]
        dav, dac, dout, dg2 = ffn_down_bwd(dx, ac, av, y, mod[l] + tok, wl["wd"][l])
        dwd = weight_grad_rows(z, dout, "ffn_down_bwd_w")
        dav, g["cw"][l], g["cb"][l] = conv_bwd(dav, dac, av, w["cw"][l])
        dx1, g["nw2"][l], dmod2 = ffn_up_bwd_x(dx, x1, dav, mod[l], w["nw2"][l], wl["wup"][l])
        dwup = weight_grad(h2t, dav, FF_SLOT, "ffn_up_bwd_w")
        tok = publish(l, "ffn", {"wd": dwd, "wup": dwup})
        (dparts, do, dy, dpa, dpb, g["lnw"][l], g["lnb"][l], g["sw"][l], g["sb"][l], g["hnw"][l],
         dg1) = mixer_bwd(dx1, parts, o, pa, pb, ym, mod[l] + tok, w["lnw"][l], w["lnb"][l], w["sw"][l], w["sb"][l],
                          w["hnw"][l], wl["wa"][l], wl["wb"][l], wl["wo"][l])
        tok = publish(l, "mix", {"wa": weight_grad(yat, dpa, D, "mixer_bwd_wa"), "wb": weight_grad(ybt, dpb, D, "mixer_bwd_wb"),
                                 "wo": weight_grad(mt, dy, D, "mixer_bwd_wo")})
        dq, df, di, dlb_f = hgrn_bwd(0, parts, lb[l] + tok, mc, mtc, mrefc, ck, do)
        dparts, dlb_b = hgrn_bwd(1, parts, lb[l], mc, mtc, mrefc, ck, do, (dq, df, di), dparts)
        dlb[l] = jnp.concatenate([dlb_f, dlb_b], axis=0)
        tok = publish(l, "in", {"win": weight_grad(ht, dparts, IN_SLOT, "in_proj_bwd_w")})
        dx, g["nw1"][l], dmod1 = in_proj_bwd_x(dx1, x0, dparts, mod[l], w["nw1"][l], wl["win"][l], after=tok)
        dmod[l] = jnp.concatenate([dmod1, dg1, dmod2, dg2], axis=1)
    done = small_ready(loss[0, 0], g, dmod, dlb) if small_ready else 0.0
    for (l, part), grads in kept.items():
        for k, v in grads.items():
            g.setdefault(k, [None] * n_layers)[l] = v
    return loss[0, 0], dx[CTX:], g, dmod, dlb, done


ROW = 1024
REPLICATED = ("norm1_w", "sgu_ln_w", "sgu_ln_b", "sgu_w", "sgu_b", "hgrn_lower_bounds", "hgrn_norm_w", "norm2_w",
              "ffn_conv_b", "final_norm_w")
WEIGHT_ORDER = ("c_ctx", "ada_w", "ada_b", "norm1_w", "w_in", "sgu_ln_w", "sgu_ln_b", "sgu_w", "sgu_b", "hgrn_lower_bounds",
                "hgrn_norm_w", "w_branch_a", "w_branch_b", "w_out", "norm2_w", "ffn_w_up", "ffn_conv_w", "ffn_conv_b",
                "ffn_w_down", "final_norm_w")


def _rows_of(n):
    return -(-n // (8 * ROW)) * 8


def _pack(arrs, total_rows=None):
    parts = []
    for a in arrs:
        flat = a.reshape(-1).astype(F32)
        rows = _rows_of(flat.shape[0])
        parts.append(jnp.pad(flat, (0, rows * ROW - flat.shape[0])).reshape(rows, ROW))
    have = sum(p.shape[0] for p in parts)
    if total_rows is not None and total_rows > have:
        parts.append(jnp.zeros((total_rows - have, ROW), F32))
    return jnp.concatenate(parts, axis=0)


def _unpack(packed, shapes):
    lead = packed.shape[:-2]
    out, r0 = [], 0
    for s in shapes:
        n = math.prod(s)
        rows = _rows_of(n)
        out.append(packed[..., r0:r0 + rows, :].reshape(lead + (rows * ROW,))[..., :n].reshape(lead + tuple(s)))
        r0 += rows
    return out


def kernel(x, c, ctx, c_ctx, ada_w, ada_b, norm1_w, w_in, sgu_ln_w, sgu_ln_b, sgu_w, sgu_b, hgrn_lower_bounds, hgrn_norm_w, w_branch_a, w_branch_b, w_out, norm2_w, ffn_w_up, ffn_conv_w, ffn_conv_b, ffn_w_down, final_norm_w, loss_target, m_c_ctx, m_ada_w, m_ada_b, m_norm1_w, m_w_in, m_sgu_ln_w, m_sgu_ln_b, m_sgu_w, m_sgu_b, m_hgrn_lower_bounds, m_hgrn_norm_w, m_w_branch_a, m_w_branch_b, m_w_out, m_norm2_w, m_ffn_w_up, m_ffn_conv_w, m_ffn_conv_b, m_ffn_w_down, m_final_norm_w, v_c_ctx, v_ada_w, v_ada_b, v_norm1_w, v_w_in, v_sgu_ln_w, v_sgu_ln_b, v_sgu_w, v_sgu_b, v_hgrn_lower_bounds, v_hgrn_norm_w, v_w_branch_a, v_w_branch_b, v_w_out, v_norm2_w, v_ffn_w_up, v_ffn_conv_w, v_ffn_conv_b, v_ffn_w_down, v_final_norm_w):
    wts = dict(c_ctx=c_ctx, ada_w=ada_w, ada_b=ada_b, norm1_w=norm1_w, w_in=w_in, sgu_ln_w=sgu_ln_w, sgu_ln_b=sgu_ln_b,
               sgu_w=sgu_w, sgu_b=sgu_b, hgrn_lower_bounds=hgrn_lower_bounds, hgrn_norm_w=hgrn_norm_w, w_branch_a=w_branch_a,
               w_branch_b=w_branch_b, w_out=w_out, norm2_w=norm2_w, ffn_w_up=ffn_w_up, ffn_conv_w=ffn_conv_w,
               ffn_conv_b=ffn_conv_b, ffn_w_down=ffn_w_down, final_norm_w=final_norm_w)
    mom1 = dict(c_ctx=m_c_ctx, ada_w=m_ada_w, ada_b=m_ada_b, norm1_w=m_norm1_w, w_in=m_w_in, sgu_ln_w=m_sgu_ln_w,
                sgu_ln_b=m_sgu_ln_b, sgu_w=m_sgu_w, sgu_b=m_sgu_b, hgrn_lower_bounds=m_hgrn_lower_bounds,
                hgrn_norm_w=m_hgrn_norm_w, w_branch_a=m_w_branch_a, w_branch_b=m_w_branch_b, w_out=m_w_out, norm2_w=m_norm2_w,
                ffn_w_up=m_ffn_w_up, ffn_conv_w=m_ffn_conv_w, ffn_conv_b=m_ffn_conv_b, ffn_w_down=m_ffn_w_down,
                final_norm_w=m_final_norm_w)
    mom2 = dict(c_ctx=v_c_ctx, ada_w=v_ada_w, ada_b=v_ada_b, norm1_w=v_norm1_w, w_in=v_w_in, sgu_ln_w=v_sgu_ln_w,
                sgu_ln_b=v_sgu_ln_b, sgu_w=v_sgu_w, sgu_b=v_sgu_b, hgrn_lower_bounds=v_hgrn_lower_bounds,
                hgrn_norm_w=v_hgrn_norm_w, w_branch_a=v_w_branch_a, w_branch_b=v_w_branch_b, w_out=v_w_out, norm2_w=v_norm2_w,
                ffn_w_up=v_ffn_w_up, ffn_conv_w=v_ffn_conv_w, ffn_conv_b=v_ffn_conv_b, ffn_w_down=v_ffn_w_down,
                final_norm_w=v_final_norm_w)
    n_layers = w_in.shape[0]
    layers = range(n_layers)
    me = 4 * lax.axis_index("x") + 2 * lax.axis_index("y") + lax.axis_index("c")
    ada_cols = ada_w.shape[-1]

    big = ("w_in", "ffn_w_up", "w_branch_a", "w_branch_b", "w_out", "ffn_w_down")
    short = {"w_in": "win", "ffn_w_up": "wup", "w_branch_a": "wa", "w_branch_b": "wb", "w_out": "wo", "ffn_w_down": "wd"}
    me1 = me.reshape(1).astype(jnp.int32)
    groups = [[("w_in", 0)], [(k, 0) for k in big[1:]], [(k, 1) for k in big]]
    in_flight, started = [], 0.0
    for n, group in enumerate(groups):
        lands = [place_own(wts[k][l], jax.ShapeDtypeStruct((N_DEV,) + wts[k].shape[1:], BF16), me1, None, False,
                           f"gather_own_{short[k]}_{l}") for k, l in group]
        in_flight.append(exchange_start([], lands, None, False, f"gather_weights_start_{n}",
                                        after=in_flight[-1][-1] if in_flight else None))
        started = started + in_flight[-1][-1][0, 0]

    def as_used(k, a):
        return a if k in ("w_in", "ffn_w_up") else a.reshape(N_FFK, FF_SLOT, D) if k == "ffn_w_down" else a.reshape(D, D)

    arrived = {}

    def fetch(l, part, after):
        n = {(0, "in"): 0, (0, "rest"): 1, (1, "in"): 2}.get((l, part))
        if n is not None:
            send, recv, _, lands, _ = in_flight[n]
            got = exchange_wait(send, recv, [], lands, None, False, after, f"gather_weights_wait_{n}")
            for (k, ll), a in zip(groups[n], got):
                arrived.setdefault(short[k], [None] * n_layers)[ll] = as_used(k, a)
        return arrived

    gathered = all_gather([ffn_conv_w.reshape(n_layers, 9, -1), c], "gather_conv_c", after=in_flight[-1][-1])
    conv_all, c_all = gathered[0], gathered[1].reshape(N_DEV, D)
    conv_full = [conv_all[:, l].transpose(1, 0, 2).reshape(9, N_FFK, FF_SLOT).transpose(1, 0, 2) for l in layers]

    cctx8 = jnp.broadcast_to(c_ctx[None, :], (N_DEV, D))
    ada_b_cols = lax.dynamic_slice_in_dim(ada_b, me * ada_cols, ada_cols, axis=1)[:, None, :]
    mod_cols = ada_fwd(c_all, cctx8, ada_w, ada_b_cols)
    (mod_all,) = all_gather([mod_cols], "gather_mod")
    mod_x = lax.dynamic_index_in_dim(mod_all[:, :, 0], me, axis=2, keepdims=False)
    mod_c = mod_all[:, :, 1, 0]
    mod = [jnp.stack([mod_c[:, l].reshape(6, 1, D), mod_x[:, l].reshape(6, 1, D)]) for l in layers]
    mod[0] = mod[0] + started

    lb1 = lower_bounds(hgrn_lower_bounds)
    lb = [jnp.zeros((2, 1, D), F32), lb1.reshape(2, 1, D)]

    w = {
        "nw1": [norm1_w[l][None] for l in layers], "nw2": [norm2_w[l][None] for l in layers],
        "lnw": [sgu_ln_w[l][None] for l in layers], "lnb": [sgu_ln_b[l][None] for l in layers],
        "sw": [sgu_w[l] for l in layers], "sb": [sgu_b[l][:, :, None] for l in layers],
        "hnw": [hgrn_norm_w[l][None] for l in layers], "cw": conv_full,
        "cb": [ffn_conv_b[l].reshape(N_FFK, 1, FF_SLOT) for l in layers], "fw": final_norm_w[None],
    }
    long = {v: k for k, v in short.items()}
    landing, sent = {}, []

    def publish(l, part, grads):
        keys = [long[k] for k in grads]
        slots = [a.reshape((N_DEV, -1, a.shape[-1])) for a in grads.values()]
        zones = [place_own(s, landing.get(k, jax.ShapeDtypeStruct((N_DEV, n_layers) + s.shape[1:], s.dtype)), me1, l, True,
                           f"scatter_own_{short[k]}_{l}") for k, s in zip(keys, slots)]
        send, recv, srcs, zones, token = exchange_start(slots, zones, l, True, f"scatter_grads_start_{part}_{l}")
        landing.update(zip(keys, zones))
        sent.append((keys, l, part, send, recv, srcs, token))
        return token[0, 0]

    out = {}
    flat2 = lambda a: a.reshape(-1, a.shape[-1])

    def finish(part, after):
        done = []
        for keys, l, p, send, recv, srcs, _ in sent:
            if p == part:
                zones = exchange_wait(send, recv, srcs, [landing[k] for k in keys], l, True, after,
                                      f"scatter_grads_wait_{part}_{l}")
                landing.update(zip(keys, zones))
                done = keys
        for k in done:
            r = landing[k]
            res = adamw(flat2(wts[k]), flat2(mom1[k]), flat2(mom2[k]), r.reshape(N_DEV, -1, r.shape[-1]), "adamw_" + k)
            out[k] = tuple(a.reshape(wts[k].shape) for a in res)

    def small_ready(loss_part, g, dmod, dlb):
        d_hlb = lower_bounds_bwd(hgrn_lower_bounds, dlb[1].reshape(1, 2 * D))
        st = lambda k: jnp.stack(g[k])
        rep_grads = {"norm1_w": st("nw1"), "sgu_ln_w": st("lnw"), "sgu_ln_b": st("lnb"), "sgu_w": st("sw"), "sgu_b": st("sb"),
                     "hgrn_lower_bounds": d_hlb, "hgrn_norm_w": st("hnw"), "norm2_w": st("nw2"), "ffn_conv_b": st("cb"),
                     "final_norm_w": g["fw"]}
        rep_rows = -(-sum(_rows_of(wts[k].size) for k in REPLICATED) // 64) * 64
        d_conv = jnp.stack([g["cw"][l].transpose(1, 0, 2).reshape(9, D_FF) for l in layers])
        dmod_x = jnp.stack([dmod[l][1].reshape(6 * D) for l in layers])
        dmod_c = jnp.stack([dmod[l][0].reshape(6 * D) for l in layers])
        small = jnp.concatenate([_pack([rep_grads[k] for k in REPLICATED], rep_rows),
                                 _pack([d_conv, dmod_x, dmod_c, loss_part.reshape(1)])], axis=0)
        zone = place_own(small, jax.ShapeDtypeStruct((N_DEV,) + small.shape, F32), me1, None, False, "gather_small_own")
        send, recv, _, zones, token = exchange_start([], [zone], None, False, "gather_small_start")
        for part in ("ffn", "mix"):
            finish(part, token)
        (small_all,) = exchange_wait(send, recv, [], zones, None, False, out["w_out"][0], "gather_small_wait")
        conv_g, dmx_all, dmc_all, loss_all = _unpack(small_all[:, rep_rows:], [d_conv.shape, dmod_x.shape, dmod_c.shape, (1,)])
        out["loss"] = functools.reduce(lambda a, b: a + b, [loss_all[k, 0] for k in range(N_DEV)])

        rep = adamw(_pack([wts[k] for k in REPLICATED], rep_rows), _pack([mom1[k] for k in REPLICATED], rep_rows),
                    _pack([mom2[k] for k in REPLICATED], rep_rows), small_all[:, :rep_rows], "adamw_replicated")
        rep = [_unpack(r, [wts[k].shape for k in REPLICATED]) for r in rep]
        for n, k in enumerate(REPLICATED):
            out[k] = tuple(r[n] for r in rep)

        conv_mine = lax.dynamic_index_in_dim(conv_g.reshape(N_DEV, n_layers, 9, N_DEV, -1), me, axis=3, keepdims=False)
        res = adamw(flat2(ffn_conv_w), flat2(m_ffn_conv_w), flat2(v_ffn_conv_w),
                    conv_mine.reshape(N_DEV, -1, conv_mine.shape[-1]), "adamw_conv_w")
        out["ffn_conv_w"] = tuple(r.reshape(ffn_conv_w.shape) for r in res)

        out["ada_b"] = tuple(adamw(ada_b, m_ada_b, v_ada_b, jnp.concatenate([dmx_all, dmc_all], axis=0), "adamw_ada_b"))

        cols_of = lambda a: lax.dynamic_slice_in_dim(a, me * ada_cols, ada_cols, axis=2).transpose(1, 0, 2)
        d_ada_w, d_cctx = ada_bwd(c_all, cctx8, ada_w, ada_b_cols, cols_of(dmx_all), cols_of(dmc_all))
        res = adamw(flat2(ada_w), flat2(m_ada_w), flat2(v_ada_w), flat2(d_ada_w)[None], "adamw_ada_w")
        out["ada_w"] = tuple(r.reshape(ada_w.shape) for r in res)
        (d_cctx_all,) = all_gather([d_cctx], "gather_c_ctx_grad")
        res = adamw(c_ctx[None], m_c_ctx[None], v_c_ctx[None], d_cctx_all, "adamw_c_ctx")
        out["c_ctx"] = tuple(r[0] for r in res)
        return d_cctx_all

    _, grad_x, _, _, _, small_done = local_step(x[0], ctx[0], loss_target[0], mod, lb, w, fetch, publish, small_ready)
    loss = out["loss"]

    finish("in", small_done)
    return (loss, grad_x[None]) + tuple(out[k][n] for n in range(4) for k in WEIGHT_ORDER)
```

```python
import functools
import math

import jax
import jax.numpy as jnp
from jax import lax
from jax.experimental import pallas as pl
from jax.experimental.pallas import tpu as pltpu

F32 = jnp.float32
BF16 = jnp.bfloat16
HIGHEST = lax.Precision.HIGHEST

N_DEV = 8
AXES = ("x", "y", "c")
D = 1024
CTX = 256
TM = 256
TM_SMALL = 128
CH = 64
SGU_CH = 128
HEADS = 8
HD = 128
GRID_W = 64
D_IN = 9 * D
IN_SLOT = D_IN // N_DEV
D_FF = 2816
FF_SLOT = 2 * D_FF // N_DEV
N_FFK = D_FF // FF_SLOT
RMS_EPS = 1e-6
LN_EPS = 1e-5
ADAM_LR, ADAM_B1, ADAM_B2, ADAM_EPS, ADAM_WD, ADAM_STEP = 0.001, 0.9, 0.999, 1e-08, 0.01, 10
VMEM_LIMIT_V7X = 56 * 2 ** 20
GRAD_WIRE = jnp.bfloat16

VMEM_WHOLE = pl.BlockSpec(memory_space=pltpu.VMEM)
ANY = pl.BlockSpec(memory_space=pl.ANY)


def _cp(n_axes):
    return pltpu.CompilerParams(dimension_semantics=("arbitrary",) * n_axes, vmem_limit_bytes=VMEM_LIMIT_V7X)


def _dot(a, b, dims):
    return lax.dot_general(a.astype(BF16), b.astype(BF16), (dims, ((), ())), preferred_element_type=F32)


@jax.custom_vjp
def mm(a, b):
    return _dot(a, b, ((1,), (0,)))


mm.defvjp(lambda a, b: (mm(a, b), (a, b)),
          lambda r, g: (_dot(g, r[1], ((1,), (1,))).astype(r[0].dtype), _dot(r[0], g, ((0,), (0,))).astype(r[1].dtype)))


@jax.custom_vjp
def mm_nt(a, b):
    return _dot(a, b, ((1,), (1,)))


mm_nt.defvjp(lambda a, b: (mm_nt(a, b), (a, b)),
             lambda r, g: (_dot(g, r[1], ((1,), (0,))).astype(r[0].dtype), _dot(g, r[0], ((0,), (0,))).astype(r[1].dtype)))


@jax.custom_vjp
def mm_tn(a, b):
    return _dot(a, b, ((0,), (0,)))


mm_tn.defvjp(lambda a, b: (mm_tn(a, b), (a, b)),
             lambda r, g: (_dot(r[1], g, ((1,), (1,))).astype(r[0].dtype), _dot(r[0], g, ((1,), (0,))).astype(r[1].dtype)))


def _tri_dot(m, g):
    hi = g.astype(BF16)
    rest = g - hi.astype(F32)
    mid = rest.astype(BF16)
    low = (rest - mid.astype(F32)).astype(BF16)
    n = g.shape[1]
    out = jnp.dot(m.astype(BF16), jnp.concatenate([hi, mid, low], axis=1), preferred_element_type=F32)
    return out[:, :n] + out[:, n:2 * n] + out[:, 2 * n:]


@jax.custom_vjp
def _cum(m, mt, g):
    return _tri_dot(m, g)


_cum.defvjp(lambda m, mt, g: (_cum(m, mt, g), (m, mt)),
            lambda r, d: (jnp.zeros_like(r[0]), jnp.zeros_like(r[1]), _tri_dot(r[1], d)))


def _silu(x):
    return x * jax.nn.sigmoid(x)


def _gelu(x):
    return 0.5 * x * (1.0 + jnp.tanh(math.sqrt(2.0 / math.pi) * (x + 0.044715 * (x * x * x))))


def _rms(x, w):
    return x * lax.rsqrt(jnp.mean(x * x, axis=-1, keepdims=True) + RMS_EPS) * w


def _norm_mod(x, w, shift, scale):
    return _rms(x, w) * (1.0 + scale) + shift


def _hsl(h):
    return slice(h * HD, (h + 1) * HD)


def _hgrn_chunk(st, qz, fz, iv, lb, m, mt, mref):
    hs = range(HEADS)
    keep = [1.0 - lb[h] for h in hs]
    g = [jnp.log(lb[h] + keep[h] * jax.nn.sigmoid(fz[h])) for h in hs]
    k = [keep[h] * jax.nn.sigmoid(-fz[h]) for h in hs]
    q = [_silu(qz[h]) for h in hs]
    b = [_cum(m, mt, g[h]) for h in hs]
    ref = [jnp.sum(mref * g[h], axis=0, keepdims=True) for h in hs]
    last = [jnp.sum(g[h], axis=0, keepdims=True) for h in hs]
    qa = [q[h] * jnp.exp(b[h] - ref[h]) for h in hs]
    ka = [k[h] * jnp.exp(ref[h] - b[h]) for h in hs]
    scores = [jnp.where(m > 0.5, mm_nt(qa[h], ka[h]), 0.0) for h in hs]
    inter = [mm_nt(qa[h] * jnp.exp(ref[h]), st[h]) for h in hs]
    kv = [mm_tn(iv[h], ka[h] * jnp.exp(last[h] - ref[h])) for h in hs]
    outs = [mm(scores[h], iv[h]) + inter[h] for h in hs]
    news = [jnp.exp(last[h]) * st[h] + kv[h] for h in hs]
    return outs, news


def _sgu_fn(ub, vb, lnw, lnb, sw, sb):
    gv = [_gelu(v) for v in vb]
    mu = sum(jnp.sum(t, axis=-1, keepdims=True) for t in gv) / D
    var = sum(jnp.sum((t - mu) * (t - mu), axis=-1, keepdims=True) for t in gv) / D
    inv = lax.rsqrt(var + LN_EPS)
    cols = []
    for g in range(HEADS):
        vn = (gv[g] - mu) * inv * lnw[g] + lnb[g]
        cols.append(_gelu(ub[g]) * (mm(sw[g], vn) + sb[g]))
    return jnp.concatenate(cols, axis=1)


def _readout_fn(ob, og, hnw):
    r = [o * lax.rsqrt(jnp.mean(o * o, axis=-1, keepdims=True) + RMS_EPS) * hnw for o in ob]
    return jnp.concatenate(r, axis=1) * _silu(og)


def _glu_fn(ac, v):
    return _gelu(ac) * v


def _stream_row(tm):
    n_ctx = CTX // tm
    return lambda i: (jnp.where(i < n_ctx, 0, 1), 0, 0, 0)


def in_proj_fwd(x, mod, nw, wg):
    t = x.shape[0]

    def body(x_ref, mod_ref, nw_ref, w_ref, out_ref, ht_ref):
        h32 = _norm_mod(x_ref[...], nw_ref[...], mod_ref[0, 0], mod_ref[0, 1])
        ht_ref[...] = h32.T.astype(BF16)
        h = h32.astype(BF16)
        for j in range(N_DEV):
            out_ref[:, j * IN_SLOT:(j + 1) * IN_SLOT] = jnp.dot(h, w_ref[j], preferred_element_type=F32)

    return pl.pallas_call(
        body, name="in_proj_fwd", grid=(t // TM,),
        in_specs=[pl.BlockSpec((TM, D), lambda i: (i, 0)), pl.BlockSpec((1, 6, 1, D), _stream_row(TM)),
                  pl.BlockSpec((1, D), lambda i: (0, 0)), VMEM_WHOLE],
        out_specs=[pl.BlockSpec((TM, D_IN), lambda i: (i, 0)), pl.BlockSpec((D, TM), lambda i: (0, i))],
        out_shape=[jax.ShapeDtypeStruct((t, D_IN), F32), jax.ShapeDtypeStruct((D, t), BF16)],
        compiler_params=_cp(1))(x, mod, nw, wg)


def _scan_chunk(nc):
    ncc = CTX // CH

    def chunk(d, s):
        bwd = jnp.where(s < ncc, ncc - 1 - s, nc + ncc - 1 - s)
        return jnp.where(d == 0, s, bwd)
    return chunk


def hgrn_fwd(parts, lb, mc, mtc, mrefc):
    t = parts.shape[0]
    nc = t // CH
    chunk = _scan_chunk(nc)

    def body(q_ref, f_ref, i_ref, lb_ref, m_ref, mt_ref, mr_ref, o_ref, ck_ref, st):
        @pl.when(pl.program_id(1) == 0)
        def _():
            st[...] = jnp.zeros_like(st)
        ck_ref[0, 0] = st[...]
        outs, news = _hgrn_chunk([st[h] for h in range(HEADS)], [q_ref[:, _hsl(h)] for h in range(HEADS)],
                                 [f_ref[:, _hsl(h)] for h in range(HEADS)], [i_ref[:, _hsl(h)] for h in range(HEADS)],
                                 [lb_ref[0, :, _hsl(h)] for h in range(HEADS)], m_ref[0], mt_ref[0], mr_ref[0])
        for h in range(HEADS):
            o_ref[0, :, _hsl(h)] = outs[h]
            st[h] = news[h]

    const = lambda d, s: (d, 0, 0)
    return pl.pallas_call(
        body, name="hgrn_fwd", grid=(2, nc),
        in_specs=[pl.BlockSpec((CH, D), lambda d, s: (chunk(d, s), 0)), pl.BlockSpec((CH, D), lambda d, s: (chunk(d, s), 1 + d)),
                  pl.BlockSpec((CH, D), lambda d, s: (chunk(d, s), 3)), pl.BlockSpec((1, 1, D), const),
                  pl.BlockSpec((1, CH, CH), const), pl.BlockSpec((1, CH, CH), const), pl.BlockSpec((1, CH, 1), const)],
        out_specs=[pl.BlockSpec((1, CH, D), lambda d, s: (d, chunk(d, s), 0)),
                   pl.BlockSpec((1, 1, HEADS, HD, HD), lambda d, s: (d, s, 0, 0, 0))],
        out_shape=[jax.ShapeDtypeStruct((2, t, D), F32), jax.ShapeDtypeStruct((2, nc, HEADS, HD, HD), F32)],
        scratch_shapes=[pltpu.VMEM((HEADS, HD, HD), F32)], compiler_params=_cp(2))(parts, parts, parts, lb, mc, mtc, mrefc)


def _mixer_tile(rows, u_ref, v_ref, og_ref, o_ref, lnw_ref, lnb_ref, sw_ref, sb_ref, hnw_ref):
    n = (rows.stop - rows.start) // SGU_CH
    yas, vjps = [], []
    for c in range(n):
        r = slice(rows.start + c * SGU_CH, rows.start + (c + 1) * SGU_CH)
        ya, vjp_a = jax.vjp(_sgu_fn, [u_ref[r, _hsl(g)] for g in range(HEADS)], [v_ref[r, _hsl(g)] for g in range(HEADS)],
                            [lnw_ref[:, _hsl(g)] for g in range(HEADS)], [lnb_ref[:, _hsl(g)] for g in range(HEADS)],
                            [sw_ref[g] for g in range(HEADS)], [sb_ref[g] for g in range(HEADS)])
        yas.append(ya)
        vjps.append(vjp_a)
    yb, vjp_b = jax.vjp(_readout_fn, [o_ref[0, rows, _hsl(h)] + o_ref[1, rows, _hsl(h)] for h in range(HEADS)],
                        og_ref[rows, :], hnw_ref[...])
    return (yas[0] if n == 1 else jnp.concatenate(yas, axis=0)), yb, vjps, vjp_b


def _part_specs(tm, first, n):
    return [pl.BlockSpec((tm, D), functools.partial(lambda k, i: (i, k), first + k)) for k in range(n)]


def mixer_fwd(x, parts, o, mod, lnw, lnb, sw, sb, hnw, wa, wb, wo):
    t = x.shape[0]

    def body(x_ref, u_ref, v_ref, og_ref, ga_ref, gb_ref, o_ref, mod_ref, lnw_ref, lnb_ref, sw_ref, sb_ref, hnw_ref,
             wa_ref, wb_ref, wo_ref, out_ref, pa_ref, pb_ref, y_ref, yat_ref, ybt_ref, mt_ref):
        ya, yb, _, _ = _mixer_tile(slice(0, TM), u_ref, v_ref, og_ref, o_ref, lnw_ref, lnb_ref, sw_ref, sb_ref, hnw_ref)
        pa, pb = mm(ya, wa_ref[...]), mm(yb, wb_ref[...])
        merged = jax.nn.sigmoid(ga_ref[...]) * pa + jax.nn.sigmoid(gb_ref[...]) * pb
        y = mm(merged, wo_ref[...])
        out_ref[...] = x_ref[...] + mod_ref[0, 2] * y
        pa_ref[...], pb_ref[...], y_ref[...] = pa.astype(BF16), pb.astype(BF16), y.astype(BF16)
        yat_ref[...], ybt_ref[...], mt_ref[...] = ya.T.astype(BF16), yb.T.astype(BF16), merged.T.astype(BF16)

    vec = lambda n: pl.BlockSpec((1, n), lambda i: (0, 0))
    tile = pl.BlockSpec((TM, D), lambda i: (i, 0))
    tile_t = pl.BlockSpec((D, TM), lambda i: (0, i))
    return pl.pallas_call(
        body, name="mixer_fwd", grid=(t // TM,),
        in_specs=[tile] + _part_specs(TM, 4, 5)
        + [pl.BlockSpec((2, TM, D), lambda i: (0, i, 0)), pl.BlockSpec((1, 6, 1, D), _stream_row(TM)), vec(D), vec(D),
           VMEM_WHOLE, VMEM_WHOLE, vec(HD), VMEM_WHOLE, VMEM_WHOLE, VMEM_WHOLE],
        out_specs=[tile] * 4 + [tile_t] * 3,
        out_shape=[jax.ShapeDtypeStruct((t, D), F32)] + [jax.ShapeDtypeStruct((t, D), BF16)] * 3
        + [jax.ShapeDtypeStruct((D, t), BF16)] * 3, compiler_params=_cp(1),
    )(x, parts, parts, parts, parts, parts, o, mod, lnw, lnb, sw, sb, hnw, wa, wb, wo)


def ffn_up_fwd(x, mod, nw, wg):
    t = x.shape[0]

    def body(x_ref, mod_ref, nw_ref, w_ref, out_ref, ht_ref):
        h32 = _norm_mod(x_ref[...], nw_ref[...], mod_ref[0, 3], mod_ref[0, 4])
        ht_ref[...] = h32.T.astype(BF16)
        h = h32.astype(BF16)
        for j in range(N_DEV):
            out_ref[j] = jnp.dot(h, w_ref[j], preferred_element_type=F32)

    return pl.pallas_call(
        body, name="ffn_up_fwd", grid=(t // TM,),
        in_specs=[pl.BlockSpec((TM, D), lambda i: (i, 0)), pl.BlockSpec((1, 6, 1, D), _stream_row(TM)),
                  pl.BlockSpec((1, D), lambda i: (0, 0)), VMEM_WHOLE],
        out_specs=[pl.BlockSpec((N_DEV, TM, FF_SLOT), lambda i: (0, i, 0)), pl.BlockSpec((D, TM), lambda i: (0, i))],
        out_shape=[jax.ShapeDtypeStruct((N_DEV, t, FF_SLOT), F32), jax.ShapeDtypeStruct((D, t), BF16)],
        compiler_params=_cp(1))(x, mod, nw, wg)


def _halo_specs(nt, k_of, i_of):
    per = TM // GRID_W
    last = nt * per - 1
    return [pl.BlockSpec((1, GRID_W, FF_SLOT), lambda *g: (k_of(*g), jnp.maximum(i_of(*g) * per - 1, 0), 0)),
            pl.BlockSpec((1, TM, FF_SLOT), lambda *g: (k_of(*g), i_of(*g), 0)),
            pl.BlockSpec((1, GRID_W, FF_SLOT), lambda *g: (k_of(*g), jnp.minimum(i_of(*g) * per + per, last), 0))]


def _with_halo(prev_ref, main_ref, next_ref, i, nt):
    prev = jnp.where(i >= 2, prev_ref[0], 0.0)
    nxt = jnp.where((i >= 1) & (i <= nt - 2), next_ref[0], 0.0)
    return jnp.concatenate([prev, main_ref[0], nxt], axis=0)


def _tap_valid(dc, i, n_rows, offset):
    r = lax.broadcasted_iota(jnp.int32, (n_rows, 1), 0) - offset
    col = jnp.bitwise_and(r, GRID_W - 1)
    pos = jnp.where(i == 0, r, col) + dc
    return (pos >= 0) & (pos < jnp.where(i == 0, TM, GRID_W))


def _row_weight(cw_ref, dr, dc, i):
    w = cw_ref[0, 3 * (dr + 1) + dc + 1:3 * (dr + 1) + dc + 2, :]
    return w if dr == 0 else jnp.where(i == 0, 0.0, w)


def ffn_down_fwd(x, av, mod, cw, cb, wd):
    t = x.shape[0]
    nt = t // TM
    ext = TM + 2 * GRID_W

    def body(x_ref, ap_ref, am_ref, an_ref, v_ref, mod_ref, cw_ref, cb_ref, wd_ref, out_ref, ac_ref, y_ref, z_ref, acc):
        i, k = pl.program_id(0), pl.program_id(1)
        a_ext = _with_halo(ap_ref, am_ref, an_ref, i, nt)
        conv = jnp.zeros((TM, FF_SLOT), F32) + cb_ref[0]
        for dc in (-1, 0, 1):
            rolled = a_ext if dc == 0 else jnp.where(_tap_valid(dc, i, ext, GRID_W), pltpu.roll(a_ext, (-dc) % ext, 0), 0.0)
            for dr in (-1, 0, 1):
                lo = GRID_W + GRID_W * dr
                conv = conv + rolled[lo:lo + TM] * _row_weight(cw_ref, dr, dc, i)
        ac_ref[0] = conv
        z = _glu_fn(conv, v_ref[0]).astype(BF16)
        z_ref[0] = z
        part = mm(z, wd_ref[0])

        @pl.when(k == 0)
        def _():
            acc[...] = part

        @pl.when(k > 0)
        def _():
            acc[...] += part

        @pl.when(k == N_FFK - 1)
        def _():
            y_ref[...] = acc[...]
            out_ref[...] = x_ref[...] + mod_ref[0, 5] * acc[...]

    tile = pl.BlockSpec((TM, D), lambda i, k: (i, 0))
    return pl.pallas_call(
        body, name="ffn_down_fwd", grid=(nt, N_FFK),
        in_specs=[tile] + _halo_specs(nt, lambda i, k: k, lambda i, k: i)
        + [pl.BlockSpec((1, TM, FF_SLOT), lambda i, k: (N_FFK + k, i, 0)),
           pl.BlockSpec((1, 6, 1, D), lambda i, k: (jnp.where(i < 1, 0, 1), 0, 0, 0)),
           pl.BlockSpec((1, 9, FF_SLOT), lambda i, k: (k, 0, 0)), pl.BlockSpec((1, 1, FF_SLOT), lambda i, k: (k, 0, 0)),
           pl.BlockSpec((1, FF_SLOT, D), lambda i, k: (k, 0, 0))],
        out_specs=[tile, pl.BlockSpec((1, TM, FF_SLOT), lambda i, k: (k, i, 0)), tile,
                   pl.BlockSpec((1, TM, FF_SLOT), lambda i, k: (k, i, 0))],
        out_shape=[jax.ShapeDtypeStruct((t, D), F32), jax.ShapeDtypeStruct((N_FFK, t, FF_SLOT), F32),
                   jax.ShapeDtypeStruct((t, D), F32), jax.ShapeDtypeStruct((N_FFK, t, FF_SLOT), BF16)],
        scratch_shapes=[pltpu.VMEM((TM, D), F32)], compiler_params=_cp(2))(x, av, av, av, av, mod, cw, cb, wd)


def loss_fwd_bwd(x, target, fw):
    t = x.shape[0]

    def body(x_ref, t_ref, w_ref, loss_ref, dx_ref, dw_ref):
        i = pl.program_id(0)

        @pl.when(i == 0)
        def _():
            loss_ref[...] = jnp.zeros_like(loss_ref)
            dw_ref[...] = jnp.zeros_like(dw_ref)
            dx_ref[...] = jnp.zeros_like(dx_ref)

        @pl.when(i > 0)
        def _():
            y, vjp = jax.vjp(_rms, x_ref[...], w_ref[...])
            err = y - t_ref[...]
            loss_ref[...] += 0.5 * jnp.sum(jnp.sum(err * err, axis=-1, keepdims=True) / D)
            dx, dw = vjp(err / D)
            dx_ref[...] = dx
            dw_ref[...] += dw

    return pl.pallas_call(
        body, name="loss_fwd_bwd", grid=(t // TM,),
        in_specs=[pl.BlockSpec((TM, D), lambda i: (i, 0)), pl.BlockSpec((TM, D), lambda i: (jnp.maximum(i - 1, 0), 0)),
                  pl.BlockSpec((1, D), lambda i: (0, 0))],
        out_specs=[pl.BlockSpec((8, 128), lambda i: (0, 0)), pl.BlockSpec((TM, D), lambda i: (i, 0)),
                   pl.BlockSpec((1, D), lambda i: (0, 0))],
        out_shape=[jax.ShapeDtypeStruct((8, 128), F32), jax.ShapeDtypeStruct((t, D), F32), jax.ShapeDtypeStruct((1, D), F32)],
        compiler_params=_cp(1))(x, target, fw)


def _stream_add(ref, k, is_ctx, val):
    ref[0, k] += jnp.where(is_ctx, val, 0.0)
    ref[1, k] += jnp.where(is_ctx, 0.0, val)


def ffn_down_bwd(dx, ac, av, y, mod, wd):
    t = dx.shape[0]
    nt = t // TM

    def body(dx_ref, ac_ref, v_ref, y_ref, mod_ref, wd_ref, dav_ref, dac_ref, dout_ref, dg_ref):
        i = pl.program_id(0)

        @pl.when(i == 0)
        def _():
            dg_ref[...] = jnp.zeros_like(dg_ref)

        _stream_add(dg_ref, 0, i == 0, jnp.sum(dx_ref[...] * y_ref[...], axis=0, keepdims=True))
        dout = (mod_ref[0, 5] * dx_ref[...]).astype(BF16)
        dout_ref[...] = dout
        for k in range(N_FFK):
            _, vjp = jax.vjp(_glu_fn, ac_ref[k], v_ref[k])
            dac, dv = vjp(mm_nt(dout, wd_ref[k]))
            dac_ref[k] = dac
            dav_ref[k] = dv.astype(BF16)

    tile = pl.BlockSpec((TM, D), lambda i: (i, 0))
    half = lambda first: pl.BlockSpec((N_FFK, TM, FF_SLOT), lambda i: (first, i, 0))
    return pl.pallas_call(
        body, name="ffn_down_bwd", grid=(nt,),
        in_specs=[tile, half(0), half(1), tile, pl.BlockSpec((1, 6, 1, D), _stream_row(TM)), VMEM_WHOLE],
        out_specs=[half(1), half(0), tile, pl.BlockSpec((2, 1, 1, D), lambda i: (0, 0, 0, 0))],
        out_shape=[jax.ShapeDtypeStruct((N_DEV, t, FF_SLOT), BF16), jax.ShapeDtypeStruct((N_FFK, t, FF_SLOT), F32),
                   jax.ShapeDtypeStruct((t, D), BF16), jax.ShapeDtypeStruct((2, 1, 1, D), F32)],
        compiler_params=_cp(1))(dx, ac, av, y, mod, wd)


def conv_bwd(dav, dac, av, cw):
    t = dac.shape[1]
    nt = t // TM
    ext = TM + 2 * GRID_W

    def body(dav_in, gp_ref, gm_ref, gn_ref, ap_ref, am_ref, an_ref, cw_ref, dav_ref, dcw_ref, dcb_ref):
        k, i = pl.program_id(0), pl.program_id(1)

        @pl.when(i == 0)
        def _():
            dcw_ref[...] = jnp.zeros_like(dcw_ref)
            dcb_ref[...] = jnp.zeros_like(dcb_ref)

        g_ext = _with_halo(gp_ref, gm_ref, gn_ref, i, nt)
        a_ext = _with_halo(ap_ref, am_ref, an_ref, i, nt)
        g_main = gm_ref[0]
        dcb_ref[0] += jnp.sum(g_main, axis=0, keepdims=True)
        da = jnp.zeros((TM, FF_SLOT), F32)
        for dc in (-1, 0, 1):
            g_rolled = g_ext if dc == 0 else pltpu.roll(jnp.where(_tap_valid(dc, i, ext, GRID_W), g_ext, 0.0), dc % ext, 0)
            a_rolled = a_ext if dc == 0 else pltpu.roll(a_ext, (-dc) % ext, 0)
            g_valid = g_main if dc == 0 else jnp.where(_tap_valid(dc, i, TM, 0), g_main, 0.0)
            for dr in (-1, 0, 1):
                lo = GRID_W - GRID_W * dr
                da = da + g_rolled[lo:lo + TM] * _row_weight(cw_ref, dr, dc, i)
                lo = GRID_W + GRID_W * dr
                tap = 3 * (dr + 1) + dc + 1
                dw = jnp.sum(g_valid * a_rolled[lo:lo + TM], axis=0, keepdims=True)
                dcw_ref[0, tap:tap + 1, :] += dw if dr == 0 else jnp.where(i == 0, 0.0, dw)
        dav_ref[0] = da.astype(BF16)

    return pl.pallas_call(
        body, name="conv_bwd", grid=(N_FFK, nt),
        in_specs=[ANY] + _halo_specs(nt, lambda k, i: k, lambda k, i: i) + _halo_specs(nt, lambda k, i: k, lambda k, i: i)
        + [pl.BlockSpec((1, 9, FF_SLOT), lambda k, i: (k, 0, 0))],
        out_specs=[pl.BlockSpec((1, TM, FF_SLOT), lambda k, i: (k, i, 0)), pl.BlockSpec((1, 9, FF_SLOT), lambda k, i: (k, 0, 0)),
                   pl.BlockSpec((1, 1, FF_SLOT), lambda k, i: (k, 0, 0))],
        out_shape=[jax.ShapeDtypeStruct(dav.shape, BF16), jax.ShapeDtypeStruct((N_FFK, 9, FF_SLOT), F32),
                   jax.ShapeDtypeStruct((N_FFK, 1, FF_SLOT), F32)],
        input_output_aliases={0: 0}, compiler_params=_cp(2))(dav, dac, dac, dac, av, av, av, cw)


def _norm_mod_bwd(x_ref, nw_ref, mod_ref, k_shift, dh, dx_in, dx_ref, dnw_ref, dmod_ref, is_ctx):
    _, vjp = jax.vjp(_norm_mod, x_ref[...], nw_ref[...], mod_ref[0, k_shift], mod_ref[0, k_shift + 1])
    dx, dnw, dshift, dscale = vjp(dh)
    dx_ref[...] = dx_in + dx
    dnw_ref[...] += dnw
    _stream_add(dmod_ref, 0, is_ctx, dshift)
    _stream_add(dmod_ref, 1, is_ctx, dscale)


def ffn_up_bwd_x(dx2, x, dav, mod, nw, wg):
    t = x.shape[0]

    def body(dx2_ref, x_ref, dav_ref, mod_ref, nw_ref, w_ref, dx_ref, dnw_ref, dmod_ref):
        i = pl.program_id(0)

        @pl.when(i == 0)
        def _():
            dnw_ref[...] = jnp.zeros_like(dnw_ref)
            dmod_ref[...] = jnp.zeros_like(dmod_ref)

        dh = mm_nt(dav_ref[0], w_ref[0])
        for j in range(1, N_DEV):
            dh = dh + mm_nt(dav_ref[j], w_ref[j])
        _norm_mod_bwd(x_ref, nw_ref, mod_ref, 3, dh, dx2_ref[...], dx_ref, dnw_ref, dmod_ref, i == 0)

    tile = pl.BlockSpec((TM, D), lambda i: (i, 0))
    return pl.pallas_call(
        body, name="ffn_up_bwd_x", grid=(t // TM,),
        in_specs=[tile, tile, pl.BlockSpec((N_DEV, TM, FF_SLOT), lambda i: (0, i, 0)), pl.BlockSpec((1, 6, 1, D), _stream_row(TM)),
                  pl.BlockSpec((1, D), lambda i: (0, 0)), VMEM_WHOLE],
        out_specs=[tile, pl.BlockSpec((1, D), lambda i: (0, 0)), pl.BlockSpec((2, 2, 1, D), lambda i: (0, 0, 0, 0))],
        out_shape=[jax.ShapeDtypeStruct((t, D), F32), jax.ShapeDtypeStruct((1, D), F32), jax.ShapeDtypeStruct((2, 2, 1, D), F32)],
        compiler_params=_cp(1))(dx2, x, dav, mod, nw, wg)


def weight_grad(at, dout, slot, name, after=None):
    rows, t = at.shape
    stacked = dout.ndim == 3
    n = dout.shape[0] if stacked else dout.shape[1] // slot

    def body(a_ref, d_ref, *rest):
        dw_ref = rest[-1]
        dw_ref[0] = jnp.dot(a_ref[...], d_ref[0] if stacked else d_ref[...], preferred_element_type=F32).astype(dw_ref.dtype)

    d_spec = pl.BlockSpec((1, t, slot), lambda j: (j, 0, 0)) if stacked else pl.BlockSpec((t, slot), lambda j: (0, j))
    extra = [] if after is None else [jnp.reshape(after, (1, 1))]
    return pl.pallas_call(
        body, name=name, grid=(n,), in_specs=[VMEM_WHOLE, d_spec] + [ANY] * len(extra),
        out_specs=pl.BlockSpec((1, rows, slot), lambda j: (j, 0, 0)),
        out_shape=jax.ShapeDtypeStruct((n, rows, slot), GRAD_WIRE), compiler_params=_cp(1))(at, dout, *extra)


def weight_grad_rows(at, dout, name):
    n, t, rows = at.shape
    cols = dout.shape[1]

    def body(a_ref, d_ref, dw_ref):
        dw_ref[0] = _dot(a_ref[0], d_ref[...], ((0,), (0,))).astype(dw_ref.dtype)

    return pl.pallas_call(
        body, name=name, grid=(n,), in_specs=[pl.BlockSpec((1, t, rows), lambda k: (k, 0, 0)), VMEM_WHOLE],
        out_specs=pl.BlockSpec((1, rows, cols), lambda k: (k, 0, 0)),
        out_shape=jax.ShapeDtypeStruct((n, rows, cols), GRAD_WIRE), compiler_params=_cp(1))(at, dout)


def mixer_bwd(dx, parts, o, pa, pb, y, mod, lnw, lnb, sw, sb, hnw, wa, wb, wo):
    t = dx.shape[0]
    tm = TM_SMALL
    n_ctx = CTX // tm

    def body(dx_ref, u_ref, v_ref, og_ref, ga_ref, gb_ref, o_ref, pa_ref, pb_ref, y_ref, mod_ref, lnw_ref, lnb_ref, sw_ref,
             sb_ref, hnw_ref, wa_ref, wb_ref, wo_ref, dp_ref, do_ref, dy_ref, dpa_ref, dpb_ref, dlnw_ref, dlnb_ref, dsw_ref,
             dsb_ref, dhnw_ref, dg_ref):
        i = pl.program_id(0)

        @pl.when(i == 0)
        def _():
            for r in (dlnw_ref, dlnb_ref, dsw_ref, dsb_ref, dhnw_ref, dg_ref):
                r[...] = jnp.zeros_like(r)

        _, _, vjps, vjp_b = _mixer_tile(slice(0, tm), u_ref, v_ref, og_ref, o_ref, lnw_ref, lnb_ref, sw_ref, sb_ref, hnw_ref)
        pa, pb = pa_ref[...].astype(F32), pb_ref[...].astype(F32)
        sa, sbg = jax.nn.sigmoid(ga_ref[...]), jax.nn.sigmoid(gb_ref[...])
        dxv = dx_ref[...]
        _stream_add(dg_ref, 0, i < n_ctx, jnp.sum(dxv * y_ref[...].astype(F32), axis=0, keepdims=True))
        dy = (mod_ref[0, 2] * dxv).astype(BF16)
        dy_ref[...] = dy
        dmerged = mm_nt(dy, wo_ref[...])
        dpa, dpb = (sa * dmerged).astype(BF16), (sbg * dmerged).astype(BF16)
        dpa_ref[...], dpb_ref[...] = dpa, dpb
        first = 4 * D
        dp_ref[:, first + 3 * D:first + 4 * D] = (dmerged * pa * sa * (1.0 - sa)).astype(BF16)
        dp_ref[:, first + 4 * D:first + 5 * D] = (dmerged * pb * sbg * (1.0 - sbg)).astype(BF16)
        dub, dvb, dlnw, dlnb, dsw, dsb = vjps[0](mm_nt(dpa, wa_ref[...]))
        dob, dog, dhnw = vjp_b(mm_nt(dpb, wb_ref[...]))
        dp_ref[:, first + 2 * D:first + 3 * D] = dog.astype(BF16)
        dhnw_ref[...] += dhnw
        for g in range(HEADS):
            dp_ref[:, first + g * HD:first + (g + 1) * HD] = dub[g].astype(BF16)
            dp_ref[:, first + D + g * HD:first + D + (g + 1) * HD] = dvb[g].astype(BF16)
            do_ref[:, _hsl(g)] = dob[g]
            dlnw_ref[:, _hsl(g)] += dlnw[g]
            dlnb_ref[:, _hsl(g)] += dlnb[g]
            dsw_ref[g] += dsw[g]
            dsb_ref[g] += dsb[g]

    vec = lambda n: pl.BlockSpec((1, n), lambda i: (0, 0))
    tile = pl.BlockSpec((tm, D), lambda i: (i, 0))
    sds = jax.ShapeDtypeStruct
    return pl.pallas_call(
        body, name="mixer_bwd", grid=(t // tm,),
        in_specs=[tile] + _part_specs(tm, 4, 5)
        + [pl.BlockSpec((2, tm, D), lambda i: (0, i, 0)), tile, tile, tile, pl.BlockSpec((1, 6, 1, D), _stream_row(tm)),
           vec(D), vec(D), VMEM_WHOLE, VMEM_WHOLE, vec(HD), VMEM_WHOLE, VMEM_WHOLE, VMEM_WHOLE],
        out_specs=[pl.BlockSpec((tm, D_IN), lambda i: (i, 0)), tile, tile, tile, tile, vec(D), vec(D),
                   VMEM_WHOLE, VMEM_WHOLE, vec(HD), pl.BlockSpec((2, 1, 1, D), lambda i: (0, 0, 0, 0))],
        out_shape=[sds((t, D_IN), BF16), sds((t, D), F32), sds((t, D), BF16), sds((t, D), BF16), sds((t, D), BF16),
                   sds((1, D), F32), sds((1, D), F32), sds((HEADS, SGU_CH, SGU_CH), F32), sds((HEADS, SGU_CH, 1), F32),
                   sds((1, HD), F32), sds((2, 1, 1, D), F32)],
        compiler_params=_cp(1))(dx, parts, parts, parts, parts, parts, o, pa, pb, y, mod, lnw, lnb, sw, sb, hnw, wa, wb, wo)


def hgrn_bwd(d, parts, lb, mc, mtc, mrefc, ck, do, first=None, dparts=None):
    t = parts.shape[0]
    nc = t // CH
    chunk = _scan_chunk(nc)
    rev = lambda s: chunk(d, nc - 1 - s)

    def body(q_ref, f_ref, i_ref, lb_ref, m_ref, mt_ref, mr_ref, ck_ref, do_ref, *rest):
        dst = rest[-1]
        dlb_ref = rest[-2]

        @pl.when(pl.program_id(0) == 0)
        def _():
            dst[...] = jnp.zeros_like(dst)
            dlb_ref[...] = jnp.zeros_like(dlb_ref)

        heads = range(HEADS)
        fn = functools.partial(_hgrn_chunk, m=m_ref[0], mt=mt_ref[0], mref=mr_ref[0])
        _, vjp = jax.vjp(fn, [ck_ref[0, 0, h] for h in heads], [q_ref[:, _hsl(h)] for h in heads],
                         [f_ref[:, _hsl(h)] for h in heads], [i_ref[:, _hsl(h)] for h in heads],
                         [lb_ref[0, :, _hsl(h)] for h in heads])
        dstl, dq, df, di, dlb = vjp(([do_ref[:, _hsl(h)] for h in heads], [dst[h] for h in heads]))
        for h in heads:
            dst[h] = dstl[h]
            dlb_ref[0, :, _hsl(h)] += dlb[h]
            if d == 0:
                dq_ref, df_ref, di_ref = rest[:3]
                dq_ref[:, _hsl(h)] = dq[h].astype(BF16)
                df_ref[:, _hsl(h)] = df[h].astype(BF16)
                di_ref[:, _hsl(h)] = di[h].astype(BF16)
            else:
                dq0_ref, df0_ref, di0_ref, _, dp_ref = rest[:5]
                col = lambda k: slice(k * D + h * HD, k * D + (h + 1) * HD)
                dp_ref[:, col(0)] = (dq0_ref[:, _hsl(h)].astype(F32) + dq[h]).astype(BF16)
                dp_ref[:, col(1)] = df0_ref[:, _hsl(h)]
                dp_ref[:, col(2)] = df[h].astype(BF16)
                dp_ref[:, col(3)] = (di0_ref[:, _hsl(h)].astype(F32) + di[h]).astype(BF16)

    const = lambda s: (d, 0, 0)
    at = lambda k: pl.BlockSpec((CH, D), lambda s: (rev(s), k))
    in_specs = [at(0), at(1 + d), at(3), pl.BlockSpec((1, 1, D), const), pl.BlockSpec((1, CH, CH), const),
                pl.BlockSpec((1, CH, CH), const), pl.BlockSpec((1, CH, 1), const),
                pl.BlockSpec((1, 1, HEADS, HD, HD), lambda s: (d, nc - 1 - s, 0, 0, 0)), at(0)]
    dlb_spec, dlb_shape = pl.BlockSpec((1, 1, D), lambda s: (0, 0, 0)), jax.ShapeDtypeStruct((1, 1, D), F32)
    common = dict(grid=(nc,), scratch_shapes=[pltpu.VMEM((HEADS, HD, HD), F32)], compiler_params=_cp(1))
    if d == 0:
        return pl.pallas_call(body, name="hgrn_bwd_fwd_dir", in_specs=in_specs, out_specs=[at(0)] * 3 + [dlb_spec],
                              out_shape=[jax.ShapeDtypeStruct((t, D), BF16)] * 3 + [dlb_shape], **common,
                              )(parts, parts, parts, lb, mc, mtc, mrefc, ck, do)
    return pl.pallas_call(body, name="hgrn_bwd_bwd_dir", in_specs=in_specs + [at(0)] * 3 + [ANY],
                          out_specs=[pl.BlockSpec((CH, 4 * D), lambda s: (rev(s), 0)), dlb_spec],
                          out_shape=[jax.ShapeDtypeStruct(dparts.shape, BF16), dlb_shape], input_output_aliases={12: 0},
                          **common)(parts, parts, parts, lb, mc, mtc, mrefc, ck, do, *first, dparts)


def in_proj_bwd_x(dx1, x, dparts, mod, nw, wg, after=None):
    t = x.shape[0]
    tm = TM
    n_ctx = CTX // tm

    def body(dx1_ref, x_ref, dp_ref, mod_ref, nw_ref, w_ref, *rest):
        dx_ref, dnw_ref, dmod_ref = rest[-3:]
        i = pl.program_id(0)

        @pl.when(i == 0)
        def _():
            dnw_ref[...] = jnp.zeros_like(dnw_ref)
            dmod_ref[...] = jnp.zeros_like(dmod_ref)

        dh = mm_nt(dp_ref[:, 0:IN_SLOT], w_ref[0])
        for j in range(1, N_DEV):
            dh = dh + mm_nt(dp_ref[:, j * IN_SLOT:(j + 1) * IN_SLOT], w_ref[j])
        _norm_mod_bwd(x_ref, nw_ref, mod_ref, 0, dh, dx1_ref[...], dx_ref, dnw_ref, dmod_ref, i < n_ctx)

    tile = pl.BlockSpec((tm, D), lambda i: (i, 0))
    extra = [] if after is None else [jnp.reshape(after, (1, 1))]
    return pl.pallas_call(
        body, name="in_proj_bwd_x", grid=(t // tm,),
        in_specs=[tile, tile, pl.BlockSpec((tm, D_IN), lambda i: (i, 0)), pl.BlockSpec((1, 6, 1, D), _stream_row(tm)),
                  pl.BlockSpec((1, D), lambda i: (0, 0)), VMEM_WHOLE] + [ANY] * len(extra),
        out_specs=[tile, pl.BlockSpec((1, D), lambda i: (0, 0)), pl.BlockSpec((2, 2, 1, D), lambda i: (0, 0, 0, 0))],
        out_shape=[jax.ShapeDtypeStruct((t, D), F32), jax.ShapeDtypeStruct((1, D), F32), jax.ShapeDtypeStruct((2, 2, 1, D), F32)],
        compiler_params=_cp(1))(dx1, x, dparts, mod, nw, wg, *extra)


def _lb_fn(h0, h1):
    m = jnp.maximum(h0, h1)
    e0, e1 = jnp.exp(h0 - m), jnp.exp(h1 - m)
    return e1 / (e0 + e1)


def lower_bounds(hlb):
    def body(h_ref, out_ref):
        out_ref[...] = _lb_fn(h_ref[0:1, :], h_ref[1:2, :])
    return pl.pallas_call(body, name="lower_bounds", out_shape=jax.ShapeDtypeStruct((1, 2 * D), F32))(hlb)


def lower_bounds_bwd(hlb, dlb1):
    def body(h_ref, d_ref, out_ref):
        _, vjp = jax.vjp(_lb_fn, h_ref[0:1, :], h_ref[1:2, :])
        d0, d1 = vjp(d_ref[...])
        out_ref[0:1, :] = d0
        out_ref[1:2, :] = d1
    return pl.pallas_call(body, name="lower_bounds_bwd", out_shape=jax.ShapeDtypeStruct((2, 2 * D), F32))(hlb, dlb1)


def _ada_fn(c_all, cctx8, w, b):
    dot = lambda a, l: jnp.dot(_silu(a), w[l], precision=HIGHEST, preferred_element_type=F32) + b[l]
    return [dot(c_all, l) for l in range(2)], [dot(cctx8, l) for l in range(2)]


def ada_fwd(c_all, cctx8, w, b):
    cols = w.shape[-1]

    def body(c_ref, cc_ref, w_ref, b_ref, out_ref):
        ox, oc = _ada_fn(c_ref[...], cc_ref[...], [w_ref[0], w_ref[1]], [b_ref[0], b_ref[1]])
        for l in range(2):
            out_ref[l, 0] = ox[l]
            out_ref[l, 1] = oc[l]
    return pl.pallas_call(body, name="ada_fwd", out_shape=jax.ShapeDtypeStruct((2, 2, N_DEV, cols), F32),
                          compiler_params=_cp(0))(c_all, cctx8, w, b)


def ada_bwd(c_all, cctx8, w, b, dmx, dmc):
    cols = w.shape[-1]

    def body(c_ref, cc_ref, w_ref, b_ref, dmx_ref, dmc_ref, dw_ref, dc_ref):
        fn = lambda cc, w0, w1: _ada_fn(c_ref[...], cc, [w0, w1], [b_ref[0], b_ref[1]])
        _, vjp = jax.vjp(fn, cc_ref[...], w_ref[0], w_ref[1])
        dcc, dw0, dw1 = vjp(([dmx_ref[0], dmx_ref[1]], [dmc_ref[0], dmc_ref[1]]))
        dw_ref[0] = dw0
        dw_ref[1] = dw1
        dc_ref[...] = jnp.sum(dcc, axis=0, keepdims=True)
    return pl.pallas_call(body, name="ada_bwd", out_shape=[jax.ShapeDtypeStruct((2, D, cols), F32), jax.ShapeDtypeStruct((1, D), F32)],
                          compiler_params=_cp(0))(c_all, cctx8, w, b, dmx, dmc)


def adamw(w, m, v, gparts, name):
    r, c = w.shape
    p = gparts.shape[0]
    rt = r
    while rt % 16 == 0 and (p + 7) * rt * c * 4 * 2 > 24 * 2 ** 20:
        rt //= 2

    def body(w_ref, m_ref, v_ref, g_ref, go_ref, d_ref, mo_ref, vo_ref):
        g = g_ref[0].astype(F32)
        for k in range(1, p):
            g = g + g_ref[k].astype(F32)
        m2 = ADAM_B1 * m_ref[...] + (1.0 - ADAM_B1) * g
        v2 = ADAM_B2 * v_ref[...] + (1.0 - ADAM_B2) * (g * g)
        m_hat = m2 / (1.0 - ADAM_B1 ** ADAM_STEP)
        v_hat = v2 / (1.0 - ADAM_B2 ** ADAM_STEP)
        go_ref[...] = g
        d_ref[...] = -ADAM_LR * (m_hat / (jnp.sqrt(v_hat) + ADAM_EPS) + ADAM_WD * w_ref[...])
        mo_ref[...] = m2
        vo_ref[...] = v2

    tile = pl.BlockSpec((rt, c), lambda i: (i, 0))
    return pl.pallas_call(
        body, name=name, grid=(r // rt,),
        in_specs=[tile, tile, tile, pl.BlockSpec((p, rt, c), lambda i: (0, i, 0))], out_specs=[tile] * 4,
        out_shape=[jax.ShapeDtypeStruct((r, c), F32)] * 4, compiler_params=_cp(1))(w, m, v, gparts)


def _me():
    x, y, c = lax.axis_index("x"), lax.axis_index("y"), lax.axis_index("c")
    return x, y, c, 4 * x + 2 * y + c


def _peer(x, y, c, p):
    fx, fy, fc = (p >> 2) & 1, (p >> 1) & 1, p & 1
    return (1 - x if fx else x, 1 - y if fy else y, 1 - c if fc else c)


def all_gather(arrs, name, after=None):
    n = len(arrs)
    extra = [] if after is None else [after]

    def body(*refs):
        ins, outs = refs[:n], refs[n + len(extra):2 * n + len(extra)]
        send, recv, local = refs[2 * n + len(extra):]
        x, y, c, me = _me()
        copies = []
        for a in range(n):
            lc = pltpu.make_async_copy(ins[a], outs[a].at[me], local.at[a])
            lc.start()
            copies.append(lc)
            for p in range(1, N_DEV):
                cp = pltpu.make_async_remote_copy(src_ref=ins[a], dst_ref=outs[a].at[me], send_sem=send.at[a, p - 1],
                                                  recv_sem=recv.at[a, p - 1], device_id=_peer(x, y, c, p),
                                                  device_id_type=pl.DeviceIdType.MESH)
                cp.start()
                copies.append(cp)
        for cp in copies:
            cp.wait()

    return pl.pallas_call(
        body, name=name, in_specs=[ANY] * (n + len(extra)), out_specs=[ANY] * n,
        out_shape=[jax.ShapeDtypeStruct((N_DEV,) + a.shape, a.dtype) for a in arrs],
        scratch_shapes=[pltpu.SemaphoreType.DMA((n, N_DEV - 1)), pltpu.SemaphoreType.DMA((n, N_DEV - 1)),
                        pltpu.SemaphoreType.DMA((n,))])(*arrs, *extra)


HBM = pl.BlockSpec(memory_space=pltpu.HBM)
SEM = pl.BlockSpec(memory_space=pltpu.SEMAPHORE)


def _in_hbm(a):
    return pltpu.with_memory_space_constraint(a, pltpu.HBM)


def _exchange_refs(srcs, lands, layer, scatter, a, x, y, c, p):
    me = 4 * x + 2 * y + c
    px, py, pc = _peer(x, y, c, p) if p else (x, y, c)
    dst = lands[a].at[me] if layer is None else lands[a].at[me, layer]
    src = srcs[a].at[4 * px + 2 * py + pc] if scatter else dst
    return src, dst, (px, py, pc)


def exchange_start(srcs, lands, layer, scatter, name, after=None):
    n, ns = len(lands), len(srcs)
    extra = [] if after is None else [after]

    def body(*refs):
        ins, lz = refs[:ns], refs[ns:ns + n]
        send, recv = refs[ns + n + len(extra)], refs[ns + n + len(extra) + 1]
        token = refs[-1]
        x, y, c, _ = _me()
        for a in range(n):
            for p in range(1, N_DEV):
                src, dst, peer = _exchange_refs(ins, lz, layer, scatter, a, x, y, c, p)
                k = a * (N_DEV - 1) + p - 1
                pltpu.make_async_remote_copy(src_ref=src, dst_ref=dst, send_sem=send.at[k], recv_sem=recv.at[k],
                                             device_id=peer, device_id_type=pl.DeviceIdType.MESH).start()
        token[...] = jnp.zeros_like(token)

    thru = [pltpu.HBM(a.shape, a.dtype) for a in list(srcs) + list(lands)]
    out = pl.pallas_call(
        body, name=name, in_specs=[HBM] * (ns + n) + [ANY] * len(extra),
        out_specs=[SEM, SEM] + [HBM] * (ns + n) + [pl.BlockSpec(memory_space=pltpu.VMEM)],
        out_shape=[pltpu.SemaphoreType.DMA((n * (N_DEV - 1),)), pltpu.SemaphoreType.DMA((n * (N_DEV - 1),))] + thru
        + [jax.ShapeDtypeStruct((8, 128), F32)],
        input_output_aliases={i: 2 + i for i in range(ns + n)},
        compiler_params=pltpu.CompilerParams(has_side_effects=pltpu.SideEffectType.DATAFLOW_SIDE_EFFECTING),
    )(*[_in_hbm(a) for a in list(srcs) + list(lands)], *extra)
    return out[0], out[1], out[2:2 + ns], out[2 + ns:2 + ns + n], out[-1]


def exchange_wait(send, recv, srcs, lands, layer, scatter, after, name):
    n, ns = len(lands), len(srcs)

    def body(*refs):
        ins, lz = refs[:ns], refs[ns:ns + n]
        send_ref, recv_ref = refs[ns + n], refs[ns + n + 1]
        x, y, c, _ = _me()
        for a in range(n):
            for p in range(1, N_DEV):
                src, dst, peer = _exchange_refs(ins, lz, layer, scatter, a, x, y, c, 0)
                k = a * (N_DEV - 1) + p - 1
                cp = pltpu.make_async_remote_copy(src_ref=src, dst_ref=dst, send_sem=send_ref.at[k],
                                                  recv_sem=recv_ref.at[k], device_id=peer,
                                                  device_id_type=pl.DeviceIdType.MESH)
                cp.wait_send()
                cp.wait_recv()

    thru = [pltpu.HBM(a.shape, a.dtype) for a in list(srcs) + list(lands)]
    out = pl.pallas_call(
        body, name=name, in_specs=[HBM] * (ns + n) + [SEM, SEM, ANY], out_specs=[HBM] * (ns + n), out_shape=thru,
        input_output_aliases={i: i for i in range(ns + n)},
        compiler_params=pltpu.CompilerParams(has_side_effects=pltpu.SideEffectType.DATAFLOW_SIDE_EFFECTING),
    )(*srcs, *lands, send, recv, after)
    return out[ns:]


def place_own(src, land, me, layer, scatter, name):
    create = isinstance(land, jax.ShapeDtypeStruct)
    r, c = src.shape[-2:]
    rt = r
    while rt % 32 == 0 and rt * c * 4 > 2 ** 21:
        rt //= 2

    def body(me_ref, src_ref, *rest):
        out_ref = rest[-1]
        out_ref[...] = src_ref[...].reshape(out_ref.shape).astype(out_ref.dtype)

    src_spec = (pl.BlockSpec((1, rt, c), lambda i, m: (m[0], i, 0)) if scatter else pl.BlockSpec((rt, c), lambda i, m: (i, 0)))
    out_spec = (pl.BlockSpec((1, rt, c), lambda i, m: (m[0], i, 0)) if layer is None
                else pl.BlockSpec((1, 1, rt, c), lambda i, m: (m[0], layer, i, 0)))
    grid_spec = pltpu.PrefetchScalarGridSpec(num_scalar_prefetch=1, grid=(r // rt,),
                                             in_specs=[src_spec] + ([] if create else [ANY]), out_specs=out_spec)
    return pl.pallas_call(body, name=name, grid_spec=grid_spec, out_shape=jax.ShapeDtypeStruct(land.shape, land.dtype),
                          input_output_aliases={} if create else {2: 0}, compiler_params=_cp(1),
                          )(*((me, src) if create else (me, src, land)))


def _scan_constants():
    r = lax.broadcasted_iota(jnp.int32, (CH, CH), 0)
    s = lax.broadcasted_iota(jnp.int32, (CH, CH), 1)
    lower = (s <= r).astype(F32)
    t = jnp.arange(CH)[:, None]
    mc = jnp.stack([lower, lower.T])
    mref = jnp.stack([(t <= CH // 2 - 1).astype(F32), (t >= CH // 2).astype(F32)])
    return mc, jnp.stack([lower.T, lower]), mref


def local_step(x, ctx, target, mod, lb, w, fetch=None, publish=None, small_ready=None, small_early=None):
    kept = {}

    def keep(l, part, grads):
        kept[(l, part)] = grads
        return 0.0

    fetch = fetch or (lambda l, part, after: w)
    publish = publish or keep
    n_layers = len(mod)
    mc, mtc, mrefc = _scan_constants()
    xs = jnp.concatenate([ctx, x], axis=0)
    saved, big = [], []
    for l in range(n_layers):
        wl = dict(fetch(l, "in", xs))
        parts, ht = in_proj_fwd(xs, mod[l], w["nw1"][l], wl["win"][l])
        o, ck = hgrn_fwd(parts, lb[l], mc, mtc, mrefc)
        wl.update(fetch(l, "rest", o))
        x1, pa, pb, ym, yat, ybt, mt = mixer_fwd(xs, parts, o, mod[l], w["lnw"][l], w["lnb"][l], w["sw"][l], w["sb"][l],
                                                 w["hnw"][l], wl["wa"][l], wl["wb"][l], wl["wo"][l])
        av, h2t = ffn_up_fwd(x1, mod[l], w["nw2"][l], wl["wup"][l])
        x2, ac, y, z = ffn_down_fwd(x1, av, mod[l], w["cw"][l], w["cb"][l], wl["wd"][l])
        saved.append((xs, parts, o, ck, x1, av, ac, y, z, ht, h2t, pa, pb, ym, yat, ybt, mt))
        big.append(wl)
        xs = x2
    loss, dx, dfw = loss_fwd_bwd(xs, target, w["fw"])
    g = {k: [None] * n_layers for k in ("nw1", "nw2", "lnw", "lnb", "sw", "sb", "hnw", "cw", "cb")}
    g["fw"] = dfw
    dmod, dlb = [None] * n_layers, [None] * n_layers
    tok = 0.0
    for l in reversed(range(n_layers)):
        x0, parts, o, ck, x1, av, ac, y, z, ht, h2t, pa, pb, ym, yat, ybt, mt = saved[l]
        wl = big[l]
        dav, dac, dout, dg2 = ffn_down_bwd(dx, ac, av, y, mod[l] + tok, wl["wd"][l])
        dwd = weight_grad_rows(z, dout, "ffn_down_bwd_w")
        dav, g["cw"][l], g["cb"][l] = conv_bwd(dav, dac, av, w["cw"][l])
        dx1, g["nw2"][l], dmod2 = ffn_up_bwd_x(dx, x1, dav, mod[l], w["nw2"][l], wl["wup"][l])
        dwup = weight_grad(h2t, dav, FF_SLOT, "ffn_up_bwd_w")
        tok = publish(l, "ffn", {"wd": dwd, "wup": dwup})
        (dparts, do, dy, dpa, dpb, g["lnw"][l], g["lnb"][l], g["sw"][l], g["sb"][l], g["hnw"][l],
         dg1) = mixer_bwd(dx1, parts, o, pa, pb, ym, mod[l] + tok, w["lnw"][l], w["lnb"][l], w["sw"][l], w["sb"][l],
                          w["hnw"][l], wl["wa"][l], wl["wb"][l], wl["wo"][l])
        tok = publish(l, "mix", {"wa": weight_grad(yat, dpa, D, "mixer_bwd_wa"), "wb": weight_grad(ybt, dpb, D, "mixer_bwd_wb"),
                                 "wo": weight_grad(mt, dy, D, "mixer_bwd_wo")})
        if l == 0 and small_early:
            dmod[0] = jnp.concatenate([jnp.zeros((2, 2, 1, D), F32), dg1, dmod2, dg2], axis=1)
            tok = tok + small_early(loss[0, 0], g, dmod, dlb)
        dq, df, di, dlb_f = hgrn_bwd(0, parts, lb[l] + tok, mc, mtc, mrefc, ck, do)
        dparts, dlb_b = hgrn_bwd(1, parts, lb[l], mc, mtc, mrefc, ck, do, (dq, df, di), dparts)
        dlb[l] = jnp.concatenate([dlb_f, dlb_b], axis=0)
        tok = publish(l, "in", {"win": weight_grad(ht, dparts, IN_SLOT, "in_proj_bwd_w")})
        dx, g["nw1"][l], dmod1 = in_proj_bwd_x(dx1, x0, dparts, mod[l], w["nw1"][l], wl["win"][l], after=tok)
        dmod[l] = jnp.concatenate([dmod1, dg1, dmod2, dg2], axis=1)
    done = small_ready(loss[0, 0], g, dmod, dlb) if small_ready else 0.0
    for (l, part), grads in kept.items():
        for k, v in grads.items():
            g.setdefault(k, [None] * n_layers)[l] = v
    return loss[0, 0], dx[CTX:], g, dmod, dlb, done


ROW = 1024
REPLICATED = ("norm1_w", "sgu_ln_w", "sgu_ln_b", "sgu_w", "sgu_b", "hgrn_lower_bounds", "hgrn_norm_w", "norm2_w",
              "ffn_conv_b", "final_norm_w")
WEIGHT_ORDER = ("c_ctx", "ada_w", "ada_b", "norm1_w", "w_in", "sgu_ln_w", "sgu_ln_b", "sgu_w", "sgu_b", "hgrn_lower_bounds",
                "hgrn_norm_w", "w_branch_a", "w_branch_b", "w_out", "norm2_w", "ffn_w_up", "ffn_conv_w", "ffn_conv_b",
                "ffn_w_down", "final_norm_w")


def _rows_of(n):
    return -(-n // (8 * ROW)) * 8


def _pack(arrs, total_rows=None):
    parts = []
    for a in arrs:
        flat = a.reshape(-1).astype(F32)
        rows = _rows_of(flat.shape[0])
        parts.append(jnp.pad(flat, (0, rows * ROW - flat.shape[0])).reshape(rows, ROW))
    have = sum(p.shape[0] for p in parts)
    if total_rows is not None and total_rows > have:
        parts.append(jnp.zeros((total_rows - have, ROW), F32))
    return jnp.concatenate(parts, axis=0)


def _unpack(packed, shapes):
    lead = packed.shape[:-2]
    out, r0 = [], 0
    for s in shapes:
        n = math.prod(s)
        rows = _rows_of(n)
        out.append(packed[..., r0:r0 + rows, :].reshape(lead + (rows * ROW,))[..., :n].reshape(lead + tuple(s)))
        r0 += rows
    return out


def kernel(x, c, ctx, c_ctx, ada_w, ada_b, norm1_w, w_in, sgu_ln_w, sgu_ln_b, sgu_w, sgu_b, hgrn_lower_bounds, hgrn_norm_w, w_branch_a, w_branch_b, w_out, norm2_w, ffn_w_up, ffn_conv_w, ffn_conv_b, ffn_w_down, final_norm_w, loss_target, m_c_ctx, m_ada_w, m_ada_b, m_norm1_w, m_w_in, m_sgu_ln_w, m_sgu_ln_b, m_sgu_w, m_sgu_b, m_hgrn_lower_bounds, m_hgrn_norm_w, m_w_branch_a, m_w_branch_b, m_w_out, m_norm2_w, m_ffn_w_up, m_ffn_conv_w, m_ffn_conv_b, m_ffn_w_down, m_final_norm_w, v_c_ctx, v_ada_w, v_ada_b, v_norm1_w, v_w_in, v_sgu_ln_w, v_sgu_ln_b, v_sgu_w, v_sgu_b, v_hgrn_lower_bounds, v_hgrn_norm_w, v_w_branch_a, v_w_branch_b, v_w_out, v_norm2_w, v_ffn_w_up, v_ffn_conv_w, v_ffn_conv_b, v_ffn_w_down, v_final_norm_w):
    wts = dict(c_ctx=c_ctx, ada_w=ada_w, ada_b=ada_b, norm1_w=norm1_w, w_in=w_in, sgu_ln_w=sgu_ln_w, sgu_ln_b=sgu_ln_b,
               sgu_w=sgu_w, sgu_b=sgu_b, hgrn_lower_bounds=hgrn_lower_bounds, hgrn_norm_w=hgrn_norm_w, w_branch_a=w_branch_a,
               w_branch_b=w_branch_b, w_out=w_out, norm2_w=norm2_w, ffn_w_up=ffn_w_up, ffn_conv_w=ffn_conv_w,
               ffn_conv_b=ffn_conv_b, ffn_w_down=ffn_w_down, final_norm_w=final_norm_w)
    mom1 = dict(c_ctx=m_c_ctx, ada_w=m_ada_w, ada_b=m_ada_b, norm1_w=m_norm1_w, w_in=m_w_in, sgu_ln_w=m_sgu_ln_w,
                sgu_ln_b=m_sgu_ln_b, sgu_w=m_sgu_w, sgu_b=m_sgu_b, hgrn_lower_bounds=m_hgrn_lower_bounds,
                hgrn_norm_w=m_hgrn_norm_w, w_branch_a=m_w_branch_a, w_branch_b=m_w_branch_b, w_out=m_w_out, norm2_w=m_norm2_w,
                ffn_w_up=m_ffn_w_up, ffn_conv_w=m_ffn_conv_w, ffn_conv_b=m_ffn_conv_b, ffn_w_down=m_ffn_w_down,
                final_norm_w=m_final_norm_w)
    mom2 = dict(c_ctx=v_c_ctx, ada_w=v_ada_w, ada_b=v_ada_b, norm1_w=v_norm1_w, w_in=v_w_in, sgu_ln_w=v_sgu_ln_w,
                sgu_ln_b=v_sgu_ln_b, sgu_w=v_sgu_w, sgu_b=v_sgu_b, hgrn_lower_bounds=v_hgrn_lower_bounds,
                hgrn_norm_w=v_hgrn_norm_w, w_branch_a=v_w_branch_a, w_branch_b=v_w_branch_b, w_out=v_w_out, norm2_w=v_norm2_w,
                ffn_w_up=v_ffn_w_up, ffn_conv_w=v_ffn_conv_w, ffn_conv_b=v_ffn_conv_b, ffn_w_down=v_ffn_w_down,
                final_norm_w=v_final_norm_w)
    n_layers = w_in.shape[0]
    layers = range(n_layers)
    me = 4 * lax.axis_index("x") + 2 * lax.axis_index("y") + lax.axis_index("c")
    ada_cols = ada_w.shape[-1]

    gathered = all_gather([ffn_conv_w.reshape(n_layers, 9, -1), c], "gather_conv_c")
    conv_all, c_all = gathered[0], gathered[1].reshape(N_DEV, D)
    conv_full = [conv_all[:, l].transpose(1, 0, 2).reshape(9, N_FFK, FF_SLOT).transpose(1, 0, 2) for l in layers]
    cctx8 = jnp.broadcast_to(c_ctx[None, :], (N_DEV, D))
    ada_b_cols = lax.dynamic_slice_in_dim(ada_b, me * ada_cols, ada_cols, axis=1)[:, None, :]
    mod_cols = ada_fwd(c_all, cctx8, ada_w, ada_b_cols)
    (mod_all,) = all_gather([mod_cols], "gather_mod")

    big = ("w_in", "ffn_w_up", "w_branch_a", "w_branch_b", "w_out", "ffn_w_down")
    short = {"w_in": "win", "ffn_w_up": "wup", "w_branch_a": "wa", "w_branch_b": "wb", "w_out": "wo", "ffn_w_down": "wd"}
    me1 = me.reshape(1).astype(jnp.int32)
    groups = [[("w_in", 0)], [(k, 0) for k in big[1:]], [(k, 1) for k in big]]
    in_flight, started = [], 0.0
    for n, group in enumerate(groups):
        lands = [place_own(wts[k][l], jax.ShapeDtypeStruct((N_DEV,) + wts[k].shape[1:], BF16), me1, None, False,
                           f"gather_own_{short[k]}_{l}") for k, l in group]
        in_flight.append(exchange_start([], lands, None, False, f"gather_weights_start_{n}",
                                        after=in_flight[-1][-1] if in_flight else mod_all))
        started = started + in_flight[-1][-1][0, 0]

    def as_used(k, a):
        return a if k in ("w_in", "ffn_w_up") else a.reshape(N_FFK, FF_SLOT, D) if k == "ffn_w_down" else a.reshape(D, D)

    arrived = {}

    def fetch(l, part, after):
        n = {(0, "in"): 0, (0, "rest"): 1, (1, "in"): 2}.get((l, part))
        if n is not None:
            send, recv, _, lands, _ = in_flight[n]
            got = exchange_wait(send, recv, [], lands, None, False, after, f"gather_weights_wait_{n}")
            for (k, ll), a in zip(groups[n], got):
                arrived.setdefault(short[k], [None] * n_layers)[ll] = as_used(k, a)
        return arrived

    mod_x = lax.dynamic_index_in_dim(mod_all[:, :, 0], me, axis=2, keepdims=False)
    mod_c = mod_all[:, :, 1, 0]
    mod = [jnp.stack([mod_c[:, l].reshape(6, 1, D), mod_x[:, l].reshape(6, 1, D)]) for l in layers]
    mod[0] = mod[0] + started

    lb1 = lower_bounds(hgrn_lower_bounds)
    lb = [jnp.zeros((2, 1, D), F32), lb1.reshape(2, 1, D)]

    w = {
        "nw1": [norm1_w[l][None] for l in layers], "nw2": [norm2_w[l][None] for l in layers],
        "lnw": [sgu_ln_w[l][None] for l in layers], "lnb": [sgu_ln_b[l][None] for l in layers],
        "sw": [sgu_w[l] for l in layers], "sb": [sgu_b[l][:, :, None] for l in layers],
        "hnw": [hgrn_norm_w[l][None] for l in layers], "cw": conv_full,
        "cb": [ffn_conv_b[l].reshape(N_FFK, 1, FF_SLOT) for l in layers], "fw": final_norm_w[None],
    }
    long = {v: k for k, v in short.items()}
    landing, sent = {}, []

    def publish(l, part, grads):
        keys = [long[k] for k in grads]
        slots = [a.reshape((N_DEV, -1, a.shape[-1])) for a in grads.values()]
        zones = [place_own(s, landing.get(k, jax.ShapeDtypeStruct((N_DEV, n_layers) + s.shape[1:], s.dtype)), me1, l, True,
                           f"scatter_own_{short[k]}_{l}") for k, s in zip(keys, slots)]
        send, recv, srcs, zones, token = exchange_start(slots, zones, l, True, f"scatter_grads_start_{part}_{l}")
        landing.update(zip(keys, zones))
        sent.append((keys, l, part, send, recv, srcs, token))
        return token[0, 0]

    out = {}
    flat2 = lambda a: a.reshape(-1, a.shape[-1])

    def finish(part, after):
        done = []
        for keys, l, p, send, recv, srcs, _ in sent:
            if p == part:
                zones = exchange_wait(send, recv, srcs, [landing[k] for k in keys], l, True, after,
                                      f"scatter_grads_wait_{part}_{l}")
                landing.update(zip(keys, zones))
                done = keys
        for k in done:
            r = landing[k]
            res = adamw(flat2(wts[k]), flat2(mom1[k]), flat2(mom2[k]), r.reshape(N_DEV, -1, r.shape[-1]), "adamw_" + k)
            out[k] = tuple(a.reshape(wts[k].shape) for a in res)

    rep_rows = -(-sum(_rows_of(wts[k].size) for k in REPLICATED) // 64) * 64
    conv_rows = _rows_of(n_layers * 9 * D_FF)
    dmod_rows = _rows_of(n_layers * 6 * D)
    early = {}

    def small_early(loss_part, g, dmod, dlb):
        d_hlb = lower_bounds_bwd(hgrn_lower_bounds, dlb[1].reshape(1, 2 * D))
        st = lambda k: jnp.stack([jnp.zeros((1, D), F32) if a is None else a for a in g[k]])
        rep_grads = {"norm1_w": st("nw1"), "sgu_ln_w": st("lnw"), "sgu_ln_b": st("lnb"), "sgu_w": st("sw"), "sgu_b": st("sb"),
                     "hgrn_lower_bounds": d_hlb, "hgrn_norm_w": st("hnw"), "norm2_w": st("nw2"), "ffn_conv_b": st("cb"),
                     "final_norm_w": g["fw"]}
        d_conv = jnp.stack([g["cw"][l].transpose(1, 0, 2).reshape(9, D_FF) for l in layers])
        dmod_x = jnp.stack([dmod[l][1].reshape(6 * D) for l in layers])
        dmod_c = jnp.stack([dmod[l][0].reshape(6 * D) for l in layers])
        small = jnp.concatenate([_pack([rep_grads[k] for k in REPLICATED], rep_rows),
                                 _pack([d_conv, dmod_x, dmod_c, loss_part.reshape(1)])], axis=0)
        zone = place_own(small, jax.ShapeDtypeStruct((N_DEV,) + small.shape, F32), me1, None, False, "gather_small_own")
        early["send"], early["recv"], _, early["zones"], token = exchange_start([], [zone], None, False, "gather_small_start")
        return token[0, 0]

    def small_ready(loss_part, g, dmod, dlb):
        late = _pack([g["nw1"][0], dmod[0][1, 0:2], dmod[0][0, 0:2]])
        (late_all,) = all_gather([late], "gather_small_late")
        for part in ("ffn", "mix"):
            finish(part, late_all)
        (small_all,) = exchange_wait(early["send"], early["recv"], [], early["zones"], None, False, out["w_out"][0],
                                     "gather_small_wait")
        at_x = rep_rows + conv_rows
        small_all = small_all.at[:, 0:1].set(late_all[:, 0:1])
        small_all = small_all.at[:, at_x:at_x + 2].set(late_all[:, 8:10])
        small_all = small_all.at[:, at_x + dmod_rows:at_x + dmod_rows + 2].set(late_all[:, 16:18])
        d_conv_shape, dmod_shape = (n_layers, 9, D_FF), (n_layers, 6 * D)
        conv_g, dmx_all, dmc_all, loss_all = _unpack(small_all[:, rep_rows:], [d_conv_shape, dmod_shape, dmod_shape, (1,)])
        out["loss"] = functools.reduce(lambda a, b: a + b, [loss_all[k, 0] for k in range(N_DEV)])

        rep = adamw(_pack([wts[k] for k in REPLICATED], rep_rows), _pack([mom1[k] for k in REPLICATED], rep_rows),
                    _pack([mom2[k] for k in REPLICATED], rep_rows), small_all[:, :rep_rows], "adamw_replicated")
        rep = [_unpack(r, [wts[k].shape for k in REPLICATED]) for r in rep]
        for n, k in enumerate(REPLICATED):
            out[k] = tuple(r[n] for r in rep)

        conv_mine = lax.dynamic_index_in_dim(conv_g.reshape(N_DEV, n_layers, 9, N_DEV, -1), me, axis=3, keepdims=False)
        res = adamw(flat2(ffn_conv_w), flat2(m_ffn_conv_w), flat2(v_ffn_conv_w),
                    conv_mine.reshape(N_DEV, -1, conv_mine.shape[-1]), "adamw_conv_w")
        out["ffn_conv_w"] = tuple(r.reshape(ffn_conv_w.shape) for r in res)

        out["ada_b"] = tuple(adamw(ada_b, m_ada_b, v_ada_b, jnp.concatenate([dmx_all, dmc_all], axis=0), "adamw_ada_b"))

        cols_of = lambda a: lax.dynamic_slice_in_dim(a, me * ada_cols, ada_cols, axis=2).transpose(1, 0, 2)
        d_ada_w, d_cctx = ada_bwd(c_all, cctx8, ada_w, ada_b_cols, cols_of(dmx_all), cols_of(dmc_all))
        res = adamw(flat2(ada_w), flat2(m_ada_w), flat2(v_ada_w), flat2(d_ada_w)[None], "adamw_ada_w")
        out["ada_w"] = tuple(r.reshape(ada_w.shape) for r in res)
        (d_cctx_all,) = all_gather([d_cctx], "gather_c_ctx_grad")
        res = adamw(c_ctx[None], m_c_ctx[None], v_c_ctx[None], d_cctx_all, "adamw_c_ctx")
        out["c_ctx"] = tuple(r[0] for r in res)
        return d_cctx_all

    _, grad_x, _, _, _, small_done = local_step(x[0], ctx[0], loss_target[0], mod, lb, w, fetch, publish, small_ready,
                                                small_early)
    loss = out["loss"]

    finish("in", small_done)
    return (loss, grad_x[None]) + tuple(out[k][n] for n in range(4) for k in WEIGHT_ORDER)
```

```python
import functools
import math

import jax
import jax.numpy as jnp
from jax import lax
from jax.experimental import pallas as pl
from jax.experimental.pallas import tpu as pltpu

F32 = jnp.float32
BF16 = jnp.bfloat16
HIGHEST = lax.Precision.HIGHEST

N_DEV = 8
AXES = ("x", "y", "c")
D = 1024
CTX = 256
TM = 256
CH = 64
SGU_CH = 128
HEADS = 8
HD = 128
GRID_W = 64
D_IN = 9 * D
IN_SLOT = D_IN // N_DEV
D_FF = 2816
FF_SLOT = 2 * D_FF // N_DEV
N_FFK = D_FF // FF_SLOT
RMS_EPS = 1e-6
LN_EPS = 1e-5
ADAM_LR, ADAM_B1, ADAM_B2, ADAM_EPS, ADAM_WD, ADAM_STEP = 0.001, 0.9, 0.999, 1e-08, 0.01, 10
VMEM_LIMIT_V7X = 56 * 2 ** 20
GRAD_WIRE = jnp.bfloat16

VMEM_WHOLE = pl.BlockSpec(memory_space=pltpu.VMEM)
ANY = pl.BlockSpec(memory_space=pl.ANY)


def _cp(n_axes):
    return pltpu.CompilerParams(dimension_semantics=("arbitrary",) * n_axes, vmem_limit_bytes=VMEM_LIMIT_V7X)


def _dot(a, b, dims):
    return lax.dot_general(a.astype(BF16), b.astype(BF16), (dims, ((), ())), preferred_element_type=F32)


@jax.custom_vjp
def mm(a, b):
    return _dot(a, b, ((1,), (0,)))


mm.defvjp(lambda a, b: (mm(a, b), (a, b)),
          lambda r, g: (_dot(g, r[1], ((1,), (1,))).astype(r[0].dtype), _dot(r[0], g, ((0,), (0,))).astype(r[1].dtype)))


@jax.custom_vjp
def mm_nt(a, b):
    return _dot(a, b, ((1,), (1,)))


mm_nt.defvjp(lambda a, b: (mm_nt(a, b), (a, b)),
             lambda r, g: (_dot(g, r[1], ((1,), (0,))).astype(r[0].dtype), _dot(g, r[0], ((0,), (0,))).astype(r[1].dtype)))


@jax.custom_vjp
def mm_tn(a, b):
    return _dot(a, b, ((0,), (0,)))


mm_tn.defvjp(lambda a, b: (mm_tn(a, b), (a, b)),
             lambda r, g: (_dot(r[1], g, ((1,), (1,))).astype(r[0].dtype), _dot(r[0], g, ((1,), (0,))).astype(r[1].dtype)))


def _tri_dot(m, g):
    hi = g.astype(BF16)
    rest = g - hi.astype(F32)
    mid = rest.astype(BF16)
    low = (rest - mid.astype(F32)).astype(BF16)
    n = g.shape[1]
    out = jnp.dot(m.astype(BF16), jnp.concatenate([hi, mid, low], axis=1), preferred_element_type=F32)
    return out[:, :n] + out[:, n:2 * n] + out[:, 2 * n:]


@jax.custom_vjp
def _cum(m, mt, g):
    return _tri_dot(m, g)


_cum.defvjp(lambda m, mt, g: (_cum(m, mt, g), (m, mt)),
            lambda r, d: (jnp.zeros_like(r[0]), jnp.zeros_like(r[1]), _tri_dot(r[1], d)))


def _silu(x):
    return x * jax.nn.sigmoid(x)


def _gelu(x):
    return 0.5 * x * (1.0 + jnp.tanh(math.sqrt(2.0 / math.pi) * (x + 0.044715 * (x * x * x))))


def _rms(x, w):
    return x * lax.rsqrt(jnp.mean(x * x, axis=-1, keepdims=True) + RMS_EPS) * w


def _norm_mod(x, w, shift, scale):
    return _rms(x, w) * (1.0 + scale) + shift


def _hsl(h):
    return slice(h * HD, (h + 1) * HD)


def _hgrn_chunk(st, qz, fz, iv, lb, m, mt, mref):
    hs = range(HEADS)
    keep = [1.0 - lb[h] for h in hs]
    g = [jnp.log(lb[h] + keep[h] * jax.nn.sigmoid(fz[h])) for h in hs]
    k = [keep[h] * jax.nn.sigmoid(-fz[h]) for h in hs]
    q = [_silu(qz[h]) for h in hs]
    b = [_cum(m, mt, g[h]) for h in hs]
    ref = [jnp.sum(mref * g[h], axis=0, keepdims=True) for h in hs]
    last = [jnp.sum(g[h], axis=0, keepdims=True) for h in hs]
    qa = [q[h] * jnp.exp(b[h] - ref[h]) for h in hs]
    ka = [k[h] * jnp.exp(ref[h] - b[h]) for h in hs]
    scores = [jnp.where(m > 0.5, mm_nt(qa[h], ka[h]), 0.0) for h in hs]
    inter = [mm_nt(qa[h] * jnp.exp(ref[h]), st[h]) for h in hs]
    kv = [mm_tn(iv[h], ka[h] * jnp.exp(last[h] - ref[h])) for h in hs]
    outs = [mm(scores[h], iv[h]) + inter[h] for h in hs]
    news = [jnp.exp(last[h]) * st[h] + kv[h] for h in hs]
    return outs, news


def _sgu_fn(ub, vb, lnw, lnb, sw, sb):
    gv = [_gelu(v) for v in vb]
    mu = sum(jnp.sum(t, axis=-1, keepdims=True) for t in gv) / D
    var = sum(jnp.sum((t - mu) * (t - mu), axis=-1, keepdims=True) for t in gv) / D
    inv = lax.rsqrt(var + LN_EPS)
    cols = []
    for g in range(HEADS):
        vn = (gv[g] - mu) * inv * lnw[g] + lnb[g]
        cols.append(_gelu(ub[g]) * (mm(sw[g], vn) + sb[g]))
    return jnp.concatenate(cols, axis=1)


def _readout_fn(ob, og, hnw):
    r = [o * lax.rsqrt(jnp.mean(o * o, axis=-1, keepdims=True) + RMS_EPS) * hnw for o in ob]
    return jnp.concatenate(r, axis=1) * _silu(og)


def _glu_fn(ac, v):
    return _gelu(ac) * v


def _stream_row(tm):
    n_ctx = CTX // tm
    return lambda i: (jnp.where(i < n_ctx, 0, 1), 0, 0, 0)


def in_proj_fwd(x, mod, nw, wg):
    t = x.shape[0]

    def body(x_ref, mod_ref, nw_ref, w_ref, out_ref, ht_ref):
        h32 = _norm_mod(x_ref[...], nw_ref[...], mod_ref[0, 0], mod_ref[0, 1])
        ht_ref[...] = h32.T.astype(BF16)
        h = h32.astype(BF16)
        for j in range(N_DEV):
            out_ref[:, j * IN_SLOT:(j + 1) * IN_SLOT] = jnp.dot(h, w_ref[j], preferred_element_type=F32)

    return pl.pallas_call(
        body, name="in_proj_fwd", grid=(t // TM,),
        in_specs=[pl.BlockSpec((TM, D), lambda i: (i, 0)), pl.BlockSpec((1, 6, 1, D), _stream_row(TM)),
                  pl.BlockSpec((1, D), lambda i: (0, 0)), VMEM_WHOLE],
        out_specs=[pl.BlockSpec((TM, D_IN), lambda i: (i, 0)), pl.BlockSpec((D, TM), lambda i: (0, i))],
        out_shape=[jax.ShapeDtypeStruct((t, D_IN), F32), jax.ShapeDtypeStruct((D, t), BF16)],
        compiler_params=_cp(1))(x, mod, nw, wg)


def _scan_chunk(nc):
    ncc = CTX // CH

    def chunk(d, s):
        bwd = jnp.where(s < ncc, ncc - 1 - s, nc + ncc - 1 - s)
        return jnp.where(d == 0, s, bwd)
    return chunk


def hgrn_fwd(parts, lb, mc, mtc, mrefc):
    t = parts.shape[0]
    nc = t // CH
    chunk = _scan_chunk(nc)

    def body(q_ref, f_ref, i_ref, lb_ref, m_ref, mt_ref, mr_ref, o_ref, ck_ref, st):
        @pl.when(pl.program_id(1) == 0)
        def _():
            st[...] = jnp.zeros_like(st)
        ck_ref[0, 0] = st[...]
        outs, news = _hgrn_chunk([st[h] for h in range(HEADS)], [q_ref[:, _hsl(h)] for h in range(HEADS)],
                                 [f_ref[:, _hsl(h)] for h in range(HEADS)], [i_ref[:, _hsl(h)] for h in range(HEADS)],
                                 [lb_ref[0, :, _hsl(h)] for h in range(HEADS)], m_ref[0], mt_ref[0], mr_ref[0])
        for h in range(HEADS):
            o_ref[0, :, _hsl(h)] = outs[h]
            st[h] = news[h]

    const = lambda d, s: (d, 0, 0)
    return pl.pallas_call(
        body, name="hgrn_fwd", grid=(2, nc),
        in_specs=[pl.BlockSpec((CH, D), lambda d, s: (chunk(d, s), 0)), pl.BlockSpec((CH, D), lambda d, s: (chunk(d, s), 1 + d)),
                  pl.BlockSpec((CH, D), lambda d, s: (chunk(d, s), 3)), pl.BlockSpec((1, 1, D), const),
                  pl.BlockSpec((1, CH, CH), const), pl.BlockSpec((1, CH, CH), const), pl.BlockSpec((1, CH, 1), const)],
        out_specs=[pl.BlockSpec((1, CH, D), lambda d, s: (d, chunk(d, s), 0)),
                   pl.BlockSpec((1, 1, HEADS, HD, HD), lambda d, s: (d, s, 0, 0, 0))],
        out_shape=[jax.ShapeDtypeStruct((2, t, D), F32), jax.ShapeDtypeStruct((2, nc, HEADS, HD, HD), F32)],
        scratch_shapes=[pltpu.VMEM((HEADS, HD, HD), F32)], compiler_params=_cp(2))(parts, parts, parts, lb, mc, mtc, mrefc)


def _mixer_tile(rows, u_ref, v_ref, og_ref, o_ref, lnw_ref, lnb_ref, sw_ref, sb_ref, hnw_ref):
    n = (rows.stop - rows.start) // SGU_CH
    yas, vjps = [], []
    for c in range(n):
        r = slice(rows.start + c * SGU_CH, rows.start + (c + 1) * SGU_CH)
        ya, vjp_a = jax.vjp(_sgu_fn, [u_ref[r, _hsl(g)] for g in range(HEADS)], [v_ref[r, _hsl(g)] for g in range(HEADS)],
                            [lnw_ref[:, _hsl(g)] for g in range(HEADS)], [lnb_ref[:, _hsl(g)] for g in range(HEADS)],
                            [sw_ref[g] for g in range(HEADS)], [sb_ref[g] for g in range(HEADS)])
        yas.append(ya)
        vjps.append(vjp_a)
    yb, vjp_b = jax.vjp(_readout_fn, [o_ref[0, rows, _hsl(h)] + o_ref[1, rows, _hsl(h)] for h in range(HEADS)],
                        og_ref[rows, :], hnw_ref[...])
    return (yas[0] if n == 1 else jnp.concatenate(yas, axis=0)), yb, vjps, vjp_b


def _part_specs(tm, first, n):
    return [pl.BlockSpec((tm, D), functools.partial(lambda k, i: (i, k), first + k)) for k in range(n)]


def mixer_fwd(x, parts, o, mod, lnw, lnb, sw, sb, hnw, wa, wb, wo):
    t = x.shape[0]

    def body(x_ref, u_ref, v_ref, og_ref, ga_ref, gb_ref, o_ref, mod_ref, lnw_ref, lnb_ref, sw_ref, sb_ref, hnw_ref,
             wa_ref, wb_ref, wo_ref, out_ref, pa_ref, pb_ref, y_ref, yat_ref, ybt_ref, mt_ref):
        ya, yb, _, _ = _mixer_tile(slice(0, TM), u_ref, v_ref, og_ref, o_ref, lnw_ref, lnb_ref, sw_ref, sb_ref, hnw_ref)
        pa, pb = mm(ya, wa_ref[...]), mm(yb, wb_ref[...])
        merged = jax.nn.sigmoid(ga_ref[...]) * pa + jax.nn.sigmoid(gb_ref[...]) * pb
        y = mm(merged, wo_ref[...])
        out_ref[...] = x_ref[...] + mod_ref[0, 2] * y
        pa_ref[...], pb_ref[...], y_ref[...] = pa.astype(BF16), pb.astype(BF16), y.astype(BF16)
        yat_ref[...], ybt_ref[...], mt_ref[...] = ya.T.astype(BF16), yb.T.astype(BF16), merged.T.astype(BF16)

    vec = lambda n: pl.BlockSpec((1, n), lambda i: (0, 0))
    tile = pl.BlockSpec((TM, D), lambda i: (i, 0))
    tile_t = pl.BlockSpec((D, TM), lambda i: (0, i))
    return pl.pallas_call(
        body, name="mixer_fwd", grid=(t // TM,),
        in_specs=[tile] + _part_specs(TM, 4, 5)
        + [pl.BlockSpec((2, TM, D), lambda i: (0, i, 0)), pl.BlockSpec((1, 6, 1, D), _stream_row(TM)), vec(D), vec(D),
           VMEM_WHOLE, VMEM_WHOLE, vec(HD), VMEM_WHOLE, VMEM_WHOLE, VMEM_WHOLE],
        out_specs=[tile] * 4 + [tile_t] * 3,
        out_shape=[jax.ShapeDtypeStruct((t, D), F32)] + [jax.ShapeDtypeStruct((t, D), BF16)] * 3
        + [jax.ShapeDtypeStruct((D, t), BF16)] * 3, compiler_params=_cp(1),
    )(x, parts, parts, parts, parts, parts, o, mod, lnw, lnb, sw, sb, hnw, wa, wb, wo)


def ffn_up_fwd(x, mod, nw, wg):
    t = x.shape[0]

    def body(x_ref, mod_ref, nw_ref, w_ref, out_ref, ht_ref):
        h32 = _norm_mod(x_ref[...], nw_ref[...], mod_ref[0, 3], mod_ref[0, 4])
        ht_ref[...] = h32.T.astype(BF16)
        h = h32.astype(BF16)
        for j in range(N_DEV):
            out_ref[j] = jnp.dot(h, w_ref[j], preferred_element_type=F32)

    return pl.pallas_call(
        body, name="ffn_up_fwd", grid=(t // TM,),
        in_specs=[pl.BlockSpec((TM, D), lambda i: (i, 0)), pl.BlockSpec((1, 6, 1, D), _stream_row(TM)),
                  pl.BlockSpec((1, D), lambda i: (0, 0)), VMEM_WHOLE],
        out_specs=[pl.BlockSpec((N_DEV, TM, FF_SLOT), lambda i: (0, i, 0)), pl.BlockSpec((D, TM), lambda i: (0, i))],
        out_shape=[jax.ShapeDtypeStruct((N_DEV, t, FF_SLOT), F32), jax.ShapeDtypeStruct((D, t), BF16)],
        compiler_params=_cp(1))(x, mod, nw, wg)


def _halo_specs(nt, k_of, i_of):
    per = TM // GRID_W
    last = nt * per - 1
    return [pl.BlockSpec((1, GRID_W, FF_SLOT), lambda *g: (k_of(*g), jnp.maximum(i_of(*g) * per - 1, 0), 0)),
            pl.BlockSpec((1, TM, FF_SLOT), lambda *g: (k_of(*g), i_of(*g), 0)),
            pl.BlockSpec((1, GRID_W, FF_SLOT), lambda *g: (k_of(*g), jnp.minimum(i_of(*g) * per + per, last), 0))]


def _with_halo(prev_ref, main_ref, next_ref, i, nt):
    prev = jnp.where(i >= 2, prev_ref[0], 0.0)
    nxt = jnp.where((i >= 1) & (i <= nt - 2), next_ref[0], 0.0)
    return jnp.concatenate([prev, main_ref[0], nxt], axis=0)


def _tap_valid(dc, i, n_rows, offset):
    r = lax.broadcasted_iota(jnp.int32, (n_rows, 1), 0) - offset
    col = jnp.bitwise_and(r, GRID_W - 1)
    pos = jnp.where(i == 0, r, col) + dc
    return (pos >= 0) & (pos < jnp.where(i == 0, TM, GRID_W))


def _row_weight(cw_ref, dr, dc, i):
    w = cw_ref[0, 3 * (dr + 1) + dc + 1:3 * (dr + 1) + dc + 2, :]
    return w if dr == 0 else jnp.where(i == 0, 0.0, w)


def ffn_down_fwd(x, av, mod, cw, cb, wd):
    t = x.shape[0]
    nt = t // TM
    ext = TM + 2 * GRID_W

    def body(x_ref, ap_ref, am_ref, an_ref, v_ref, mod_ref, cw_ref, cb_ref, wd_ref, out_ref, ac_ref, y_ref, z_ref, acc):
        i, k = pl.program_id(0), pl.program_id(1)
        a_ext = _with_halo(ap_ref, am_ref, an_ref, i, nt)
        conv = jnp.zeros((TM, FF_SLOT), F32) + cb_ref[0]
        for dc in (-1, 0, 1):
            rolled = a_ext if dc == 0 else jnp.where(_tap_valid(dc, i, ext, GRID_W), pltpu.roll(a_ext, (-dc) % ext, 0), 0.0)
            for dr in (-1, 0, 1):
                lo = GRID_W + GRID_W * dr
                conv = conv + rolled[lo:lo + TM] * _row_weight(cw_ref, dr, dc, i)
        ac_ref[0] = conv
        z = _glu_fn(conv, v_ref[0]).astype(BF16)
        z_ref[0] = z
        part = mm(z, wd_ref[0])

        @pl.when(k == 0)
        def _():
            acc[...] = part

        @pl.when(k > 0)
        def _():
            acc[...] += part

        @pl.when(k == N_FFK - 1)
        def _():
            y_ref[...] = acc[...]
            out_ref[...] = x_ref[...] + mod_ref[0, 5] * acc[...]

    tile = pl.BlockSpec((TM, D), lambda i, k: (i, 0))
    return pl.pallas_call(
        body, name="ffn_down_fwd", grid=(nt, N_FFK),
        in_specs=[tile] + _halo_specs(nt, lambda i, k: k, lambda i, k: i)
        + [pl.BlockSpec((1, TM, FF_SLOT), lambda i, k: (N_FFK + k, i, 0)),
           pl.BlockSpec((1, 6, 1, D), lambda i, k: (jnp.where(i < 1, 0, 1), 0, 0, 0)),
           pl.BlockSpec((1, 9, FF_SLOT), lambda i, k: (k, 0, 0)), pl.BlockSpec((1, 1, FF_SLOT), lambda i, k: (k, 0, 0)),
           pl.BlockSpec((1, FF_SLOT, D), lambda i, k: (k, 0, 0))],
        out_specs=[tile, pl.BlockSpec((1, TM, FF_SLOT), lambda i, k: (k, i, 0)), tile,
                   pl.BlockSpec((1, TM, FF_SLOT), lambda i, k: (k, i, 0))],
        out_shape=[jax.ShapeDtypeStruct((t, D), F32), jax.ShapeDtypeStruct((N_FFK, t, FF_SLOT), F32),
                   jax.ShapeDtypeStruct((t, D), F32), jax.ShapeDtypeStruct((N_FFK, t, FF_SLOT), BF16)],
        scratch_shapes=[pltpu.VMEM((TM, D), F32)], compiler_params=_cp(2))(x, av, av, av, av, mod, cw, cb, wd)


def loss_fwd_bwd(x, target, fw):
    t = x.shape[0]

    def body(x_ref, t_ref, w_ref, loss_ref, dx_ref, dw_ref):
        i = pl.program_id(0)

        @pl.when(i == 0)
        def _():
            loss_ref[...] = jnp.zeros_like(loss_ref)
            dw_ref[...] = jnp.zeros_like(dw_ref)
            dx_ref[...] = jnp.zeros_like(dx_ref)

        @pl.when(i > 0)
        def _():
            y, vjp = jax.vjp(_rms, x_ref[...], w_ref[...])
            err = y - t_ref[...]
            loss_ref[...] += 0.5 * jnp.sum(jnp.sum(err * err, axis=-1, keepdims=True) / D)
            dx, dw = vjp(err / D)
            dx_ref[...] = dx
            dw_ref[...] += dw

    return pl.pallas_call(
        body, name="loss_fwd_bwd", grid=(t // TM,),
        in_specs=[pl.BlockSpec((TM, D), lambda i: (i, 0)), pl.BlockSpec((TM, D), lambda i: (jnp.maximum(i - 1, 0), 0)),
                  pl.BlockSpec((1, D), lambda i: (0, 0))],
        out_specs=[pl.BlockSpec((8, 128), lambda i: (0, 0)), pl.BlockSpec((TM, D), lambda i: (i, 0)),
                   pl.BlockSpec((1, D), lambda i: (0, 0))],
        out_shape=[jax.ShapeDtypeStruct((8, 128), F32), jax.ShapeDtypeStruct((t, D), F32), jax.ShapeDtypeStruct((1, D), F32)],
        compiler_params=_cp(1))(x, target, fw)


def _stream_add(ref, k, is_ctx, val):
    ref[0, k] += jnp.where(is_ctx, val, 0.0)
    ref[1, k] += jnp.where(is_ctx, 0.0, val)


def ffn_down_bwd(dx, ac, av, y, mod, wd):
    t = dx.shape[0]
    nt = t // TM

    def body(dx_ref, ac_ref, v_ref, y_ref, mod_ref, wd_ref, dav_ref, dac_ref, dout_ref, dg_ref):
        i = pl.program_id(0)

        @pl.when(i == 0)
        def _():
            dg_ref[...] = jnp.zeros_like(dg_ref)

        _stream_add(dg_ref, 0, i == 0, jnp.sum(dx_ref[...] * y_ref[...], axis=0, keepdims=True))
        dout = (mod_ref[0, 5] * dx_ref[...]).astype(BF16)
        dout_ref[...] = dout
        for k in range(N_FFK):
            _, vjp = jax.vjp(_glu_fn, ac_ref[k], v_ref[k])
            dac, dv = vjp(mm_nt(dout, wd_ref[k]))
            dac_ref[k] = dac
            dav_ref[k] = dv.astype(BF16)

    tile = pl.BlockSpec((TM, D), lambda i: (i, 0))
    half = lambda first: pl.BlockSpec((N_FFK, TM, FF_SLOT), lambda i: (first, i, 0))
    return pl.pallas_call(
        body, name="ffn_down_bwd", grid=(nt,),
        in_specs=[tile, half(0), half(1), tile, pl.BlockSpec((1, 6, 1, D), _stream_row(TM)), VMEM_WHOLE],
        out_specs=[half(1), half(0), tile, pl.BlockSpec((2, 1, 1, D), lambda i: (0, 0, 0, 0))],
        out_shape=[jax.ShapeDtypeStruct((N_DEV, t, FF_SLOT), BF16), jax.ShapeDtypeStruct((N_FFK, t, FF_SLOT), F32),
                   jax.ShapeDtypeStruct((t, D), BF16), jax.ShapeDtypeStruct((2, 1, 1, D), F32)],
        compiler_params=_cp(1))(dx, ac, av, y, mod, wd)


def conv_bwd(dav, dac, av, cw):
    t = dac.shape[1]
    nt = t // TM
    ext = TM + 2 * GRID_W

    def body(dav_in, gp_ref, gm_ref, gn_ref, ap_ref, am_ref, an_ref, cw_ref, dav_ref, dcw_ref, dcb_ref):
        k, i = pl.program_id(0), pl.program_id(1)

        @pl.when(i == 0)
        def _():
            dcw_ref[...] = jnp.zeros_like(dcw_ref)
            dcb_ref[...] = jnp.zeros_like(dcb_ref)

        g_ext = _with_halo(gp_ref, gm_ref, gn_ref, i, nt)
        a_ext = _with_halo(ap_ref, am_ref, an_ref, i, nt)
        g_main = gm_ref[0]
        dcb_ref[0] += jnp.sum(g_main, axis=0, keepdims=True)
        da = jnp.zeros((TM, FF_SLOT), F32)
        for dc in (-1, 0, 1):
            g_rolled = g_ext if dc == 0 else pltpu.roll(jnp.where(_tap_valid(dc, i, ext, GRID_W), g_ext, 0.0), dc % ext, 0)
            a_rolled = a_ext if dc == 0 else pltpu.roll(a_ext, (-dc) % ext, 0)
            g_valid = g_main if dc == 0 else jnp.where(_tap_valid(dc, i, TM, 0), g_main, 0.0)
            for dr in (-1, 0, 1):
                lo = GRID_W - GRID_W * dr
                da = da + g_rolled[lo:lo + TM] * _row_weight(cw_ref, dr, dc, i)
                lo = GRID_W + GRID_W * dr
                tap = 3 * (dr + 1) + dc + 1
                dw = jnp.sum(g_valid * a_rolled[lo:lo + TM], axis=0, keepdims=True)
                dcw_ref[0, tap:tap + 1, :] += dw if dr == 0 else jnp.where(i == 0, 0.0, dw)
        dav_ref[0] = da.astype(BF16)

    return pl.pallas_call(
        body, name="conv_bwd", grid=(N_FFK, nt),
        in_specs=[ANY] + _halo_specs(nt, lambda k, i: k, lambda k, i: i) + _halo_specs(nt, lambda k, i: k, lambda k, i: i)
        + [pl.BlockSpec((1, 9, FF_SLOT), lambda k, i: (k, 0, 0))],
        out_specs=[pl.BlockSpec((1, TM, FF_SLOT), lambda k, i: (k, i, 0)), pl.BlockSpec((1, 9, FF_SLOT), lambda k, i: (k, 0, 0)),
                   pl.BlockSpec((1, 1, FF_SLOT), lambda k, i: (k, 0, 0))],
        out_shape=[jax.ShapeDtypeStruct(dav.shape, BF16), jax.ShapeDtypeStruct((N_FFK, 9, FF_SLOT), F32),
                   jax.ShapeDtypeStruct((N_FFK, 1, FF_SLOT), F32)],
        input_output_aliases={0: 0}, compiler_params=_cp(2))(dav, dac, dac, dac, av, av, av, cw)


def _norm_mod_bwd(x_ref, nw_ref, mod_ref, k_shift, dh, dx_in, dx_ref, dnw_ref, dmod_ref, is_ctx):
    _, vjp = jax.vjp(_norm_mod, x_ref[...], nw_ref[...], mod_ref[0, k_shift], mod_ref[0, k_shift + 1])
    dx, dnw, dshift, dscale = vjp(dh)
    dx_ref[...] = dx_in + dx
    dnw_ref[...] += dnw
    _stream_add(dmod_ref, 0, is_ctx, dshift)
    _stream_add(dmod_ref, 1, is_ctx, dscale)


def ffn_up_bwd_x(dx2, x, dav, mod, nw, wg):
    t = x.shape[0]

    def body(dx2_ref, x_ref, dav_ref, mod_ref, nw_ref, w_ref, dx_ref, dnw_ref, dmod_ref):
        i = pl.program_id(0)

        @pl.when(i == 0)
        def _():
            dnw_ref[...] = jnp.zeros_like(dnw_ref)
            dmod_ref[...] = jnp.zeros_like(dmod_ref)

        dh = mm_nt(dav_ref[0], w_ref[0])
        for j in range(1, N_DEV):
            dh = dh + mm_nt(dav_ref[j], w_ref[j])
        _norm_mod_bwd(x_ref, nw_ref, mod_ref, 3, dh, dx2_ref[...], dx_ref, dnw_ref, dmod_ref, i == 0)

    tile = pl.BlockSpec((TM, D), lambda i: (i, 0))
    return pl.pallas_call(
        body, name="ffn_up_bwd_x", grid=(t // TM,),
        in_specs=[tile, tile, pl.BlockSpec((N_DEV, TM, FF_SLOT), lambda i: (0, i, 0)), pl.BlockSpec((1, 6, 1, D), _stream_row(TM)),
                  pl.BlockSpec((1, D), lambda i: (0, 0)), VMEM_WHOLE],
        out_specs=[tile, pl.BlockSpec((1, D), lambda i: (0, 0)), pl.BlockSpec((2, 2, 1, D), lambda i: (0, 0, 0, 0))],
        out_shape=[jax.ShapeDtypeStruct((t, D), F32), jax.ShapeDtypeStruct((1, D), F32), jax.ShapeDtypeStruct((2, 2, 1, D), F32)],
        compiler_params=_cp(1))(dx2, x, dav, mod, nw, wg)


def weight_grad(at, dout, slot, name, after=None):
    rows, t = at.shape
    stacked = dout.ndim == 3
    n = dout.shape[0] if stacked else dout.shape[1] // slot

    def body(a_ref, d_ref, *rest):
        dw_ref = rest[-1]
        dw_ref[0] = jnp.dot(a_ref[...], d_ref[0] if stacked else d_ref[...], preferred_element_type=F32).astype(dw_ref.dtype)

    d_spec = pl.BlockSpec((1, t, slot), lambda j: (j, 0, 0)) if stacked else pl.BlockSpec((t, slot), lambda j: (0, j))
    extra = [] if after is None else [jnp.reshape(after, (1, 1))]
    return pl.pallas_call(
        body, name=name, grid=(n,), in_specs=[VMEM_WHOLE, d_spec] + [ANY] * len(extra),
        out_specs=pl.BlockSpec((1, rows, slot), lambda j: (j, 0, 0)),
        out_shape=jax.ShapeDtypeStruct((n, rows, slot), GRAD_WIRE), compiler_params=_cp(1))(at, dout, *extra)


def weight_grad_rows(at, dout, name):
    n, t, rows = at.shape
    cols = dout.shape[1]

    def body(a_ref, d_ref, dw_ref):
        dw_ref[0] = _dot(a_ref[0], d_ref[...], ((0,), (0,))).astype(dw_ref.dtype)

    return pl.pallas_call(
        body, name=name, grid=(n,), in_specs=[pl.BlockSpec((1, t, rows), lambda k: (k, 0, 0)), VMEM_WHOLE],
        out_specs=pl.BlockSpec((1, rows, cols), lambda k: (k, 0, 0)),
        out_shape=jax.ShapeDtypeStruct((n, rows, cols), GRAD_WIRE), compiler_params=_cp(1))(at, dout)


def mixer_bwd(dx, parts, o, pa, pb, y, mod, lnw, lnb, sw, sb, hnw, wa, wb, wo):
    t = dx.shape[0]
    tm = TM
    n_ctx = CTX // tm

    def body(dx_ref, u_ref, v_ref, og_ref, ga_ref, gb_ref, o_ref, pa_ref, pb_ref, y_ref, mod_ref, lnw_ref, lnb_ref, sw_ref,
             sb_ref, hnw_ref, wa_ref, wb_ref, wo_ref, dp_ref, do_ref, dy_ref, dpa_ref, dpb_ref, dlnw_ref, dlnb_ref, dsw_ref,
             dsb_ref, dhnw_ref, dg_ref):
        i = pl.program_id(0)

        @pl.when(i == 0)
        def _():
            for r in (dlnw_ref, dlnb_ref, dsw_ref, dsb_ref, dhnw_ref, dg_ref):
                r[...] = jnp.zeros_like(r)

        _, _, vjps, vjp_b = _mixer_tile(slice(0, tm), u_ref, v_ref, og_ref, o_ref, lnw_ref, lnb_ref, sw_ref, sb_ref, hnw_ref)
        pa, pb = pa_ref[...].astype(F32), pb_ref[...].astype(F32)
        sa, sbg = jax.nn.sigmoid(ga_ref[...]), jax.nn.sigmoid(gb_ref[...])
        dxv = dx_ref[...]
        _stream_add(dg_ref, 0, i < n_ctx, jnp.sum(dxv * y_ref[...].astype(F32), axis=0, keepdims=True))
        dy = (mod_ref[0, 2] * dxv).astype(BF16)
        dy_ref[...] = dy
        dmerged = mm_nt(dy, wo_ref[...])
        dpa, dpb = (sa * dmerged).astype(BF16), (sbg * dmerged).astype(BF16)
        dpa_ref[...], dpb_ref[...] = dpa, dpb
        first = 4 * D
        dp_ref[:, first + 3 * D:first + 4 * D] = (dmerged * pa * sa * (1.0 - sa)).astype(BF16)
        dp_ref[:, first + 4 * D:first + 5 * D] = (dmerged * pb * sbg * (1.0 - sbg)).astype(BF16)
        dya = mm_nt(dpa, wa_ref[...])
        dob, dog, dhnw = vjp_b(mm_nt(dpb, wb_ref[...]))
        dp_ref[:, first + 2 * D:first + 3 * D] = dog.astype(BF16)
        dhnw_ref[...] += dhnw
        for g in range(HEADS):
            do_ref[:, _hsl(g)] = dob[g]
        for c, vjp_a in enumerate(vjps):
            rows = slice(c * SGU_CH, (c + 1) * SGU_CH)
            dub, dvb, dlnw, dlnb, dsw, dsb = vjp_a(dya[rows])
            for g in range(HEADS):
                dp_ref[rows, first + g * HD:first + (g + 1) * HD] = dub[g].astype(BF16)
                dp_ref[rows, first + D + g * HD:first + D + (g + 1) * HD] = dvb[g].astype(BF16)
                dlnw_ref[:, _hsl(g)] += dlnw[g]
                dlnb_ref[:, _hsl(g)] += dlnb[g]
                dsw_ref[g] += dsw[g]
                dsb_ref[g] += dsb[g]

    vec = lambda n: pl.BlockSpec((1, n), lambda i: (0, 0))
    tile = pl.BlockSpec((tm, D), lambda i: (i, 0))
    sds = jax.ShapeDtypeStruct
    return pl.pallas_call(
        body, name="mixer_bwd", grid=(t // tm,),
        in_specs=[tile] + _part_specs(tm, 4, 5)
        + [pl.BlockSpec((2, tm, D), lambda i: (0, i, 0)), tile, tile, tile, pl.BlockSpec((1, 6, 1, D), _stream_row(tm)),
           vec(D), vec(D), VMEM_WHOLE, VMEM_WHOLE, vec(HD), VMEM_WHOLE, VMEM_WHOLE, VMEM_WHOLE],
        out_specs=[pl.BlockSpec((tm, D_IN), lambda i: (i, 0)), tile, tile, tile, tile, vec(D), vec(D),
                   VMEM_WHOLE, VMEM_WHOLE, vec(HD), pl.BlockSpec((2, 1, 1, D), lambda i: (0, 0, 0, 0))],
        out_shape=[sds((t, D_IN), BF16), sds((t, D), F32), sds((t, D), BF16), sds((t, D), BF16), sds((t, D), BF16),
                   sds((1, D), F32), sds((1, D), F32), sds((HEADS, SGU_CH, SGU_CH), F32), sds((HEADS, SGU_CH, 1), F32),
                   sds((1, HD), F32), sds((2, 1, 1, D), F32)],
        compiler_params=_cp(1))(dx, parts, parts, parts, parts, parts, o, pa, pb, y, mod, lnw, lnb, sw, sb, hnw, wa, wb, wo)


def hgrn_bwd(d, parts, lb, mc, mtc, mrefc, ck, do, first=None, dparts=None):
    t = parts.shape[0]
    nc = t // CH
    chunk = _scan_chunk(nc)
    rev = lambda s: chunk(d, nc - 1 - s)

    def body(q_ref, f_ref, i_ref, lb_ref, m_ref, mt_ref, mr_ref, ck_ref, do_ref, *rest):
        dst = rest[-1]
        dlb_ref = rest[-2]

        @pl.when(pl.program_id(0) == 0)
        def _():
            dst[...] = jnp.zeros_like(dst)
            dlb_ref[...] = jnp.zeros_like(dlb_ref)

        heads = range(HEADS)
        fn = functools.partial(_hgrn_chunk, m=m_ref[0], mt=mt_ref[0], mref=mr_ref[0])
        _, vjp = jax.vjp(fn, [ck_ref[0, 0, h] for h in heads], [q_ref[:, _hsl(h)] for h in heads],
                         [f_ref[:, _hsl(h)] for h in heads], [i_ref[:, _hsl(h)] for h in heads],
                         [lb_ref[0, :, _hsl(h)] for h in heads])
        dstl, dq, df, di, dlb = vjp(([do_ref[:, _hsl(h)] for h in heads], [dst[h] for h in heads]))
        for h in heads:
            dst[h] = dstl[h]
            dlb_ref[0, :, _hsl(h)] += dlb[h]
            if d == 0:
                dq_ref, df_ref, di_ref = rest[:3]
                dq_ref[:, _hsl(h)] = dq[h].astype(BF16)
                df_ref[:, _hsl(h)] = df[h].astype(BF16)
                di_ref[:, _hsl(h)] = di[h].astype(BF16)
            else:
                dq0_ref, df0_ref, di0_ref, _, dp_ref = rest[:5]
                col = lambda k: slice(k * D + h * HD, k * D + (h + 1) * HD)
                dp_ref[:, col(0)] = (dq0_ref[:, _hsl(h)].astype(F32) + dq[h]).astype(BF16)
                dp_ref[:, col(1)] = df0_ref[:, _hsl(h)]
                dp_ref[:, col(2)] = df[h].astype(BF16)
                dp_ref[:, col(3)] = (di0_ref[:, _hsl(h)].astype(F32) + di[h]).astype(BF16)

    const = lambda s: (d, 0, 0)
    at = lambda k: pl.BlockSpec((CH, D), lambda s: (rev(s), k))
    in_specs = [at(0), at(1 + d), at(3), pl.BlockSpec((1, 1, D), const), pl.BlockSpec((1, CH, CH), const),
                pl.BlockSpec((1, CH, CH), const), pl.BlockSpec((1, CH, 1), const),
                pl.BlockSpec((1, 1, HEADS, HD, HD), lambda s: (d, nc - 1 - s, 0, 0, 0)), at(0)]
    dlb_spec, dlb_shape = pl.BlockSpec((1, 1, D), lambda s: (0, 0, 0)), jax.ShapeDtypeStruct((1, 1, D), F32)
    common = dict(grid=(nc,), scratch_shapes=[pltpu.VMEM((HEADS, HD, HD), F32)], compiler_params=_cp(1))
    if d == 0:
        return pl.pallas_call(body, name="hgrn_bwd_fwd_dir", in_specs=in_specs, out_specs=[at(0)] * 3 + [dlb_spec],
                              out_shape=[jax.ShapeDtypeStruct((t, D), BF16)] * 3 + [dlb_shape], **common,
                              )(parts, parts, parts, lb, mc, mtc, mrefc, ck, do)
    return pl.pallas_call(body, name="hgrn_bwd_bwd_dir", in_specs=in_specs + [at(0)] * 3 + [ANY],
                          out_specs=[pl.BlockSpec((CH, 4 * D), lambda s: (rev(s), 0)), dlb_spec],
                          out_shape=[jax.ShapeDtypeStruct(dparts.shape, BF16), dlb_shape], input_output_aliases={12: 0},
                          **common)(parts, parts, parts, lb, mc, mtc, mrefc, ck, do, *first, dparts)


def in_proj_bwd_x(dx1, x, dparts, mod, nw, wg, after=None):
    t = x.shape[0]
    tm = TM
    n_ctx = CTX // tm

    def body(dx1_ref, x_ref, dp_ref, mod_ref, nw_ref, w_ref, *rest):
        dx_ref, dnw_ref, dmod_ref = rest[-3:]
        i = pl.program_id(0)

        @pl.when(i == 0)
        def _():
            dnw_ref[...] = jnp.zeros_like(dnw_ref)
            dmod_ref[...] = jnp.zeros_like(dmod_ref)

        dh = mm_nt(dp_ref[:, 0:IN_SLOT], w_ref[0])
        for j in range(1, N_DEV):
            dh = dh + mm_nt(dp_ref[:, j * IN_SLOT:(j + 1) * IN_SLOT], w_ref[j])
        _norm_mod_bwd(x_ref, nw_ref, mod_ref, 0, dh, dx1_ref[...], dx_ref, dnw_ref, dmod_ref, i < n_ctx)

    tile = pl.BlockSpec((tm, D), lambda i: (i, 0))
    extra = [] if after is None else [jnp.reshape(after, (1, 1))]
    return pl.pallas_call(
        body, name="in_proj_bwd_x", grid=(t // tm,),
        in_specs=[tile, tile, pl.BlockSpec((tm, D_IN), lambda i: (i, 0)), pl.BlockSpec((1, 6, 1, D), _stream_row(tm)),
                  pl.BlockSpec((1, D), lambda i: (0, 0)), VMEM_WHOLE] + [ANY] * len(extra),
        out_specs=[tile, pl.BlockSpec((1, D), lambda i: (0, 0)), pl.BlockSpec((2, 2, 1, D), lambda i: (0, 0, 0, 0))],
        out_shape=[jax.ShapeDtypeStruct((t, D), F32), jax.ShapeDtypeStruct((1, D), F32), jax.ShapeDtypeStruct((2, 2, 1, D), F32)],
        compiler_params=_cp(1))(dx1, x, dparts, mod, nw, wg, *extra)


def _lb_fn(h0, h1):
    m = jnp.maximum(h0, h1)
    e0, e1 = jnp.exp(h0 - m), jnp.exp(h1 - m)
    return e1 / (e0 + e1)


def lower_bounds(hlb):
    def body(h_ref, out_ref):
        out_ref[...] = _lb_fn(h_ref[0:1, :], h_ref[1:2, :])
    return pl.pallas_call(body, name="lower_bounds", out_shape=jax.ShapeDtypeStruct((1, 2 * D), F32))(hlb)


def lower_bounds_bwd(hlb, dlb1):
    def body(h_ref, d_ref, out_ref):
        _, vjp = jax.vjp(_lb_fn, h_ref[0:1, :], h_ref[1:2, :])
        d0, d1 = vjp(d_ref[...])
        out_ref[0:1, :] = d0
        out_ref[1:2, :] = d1
    return pl.pallas_call(body, name="lower_bounds_bwd", out_shape=jax.ShapeDtypeStruct((2, 2 * D), F32))(hlb, dlb1)


def _ada_fn(c_all, cctx8, w, b):
    dot = lambda a, l: jnp.dot(_silu(a), w[l], precision=HIGHEST, preferred_element_type=F32) + b[l]
    return [dot(c_all, l) for l in range(2)], [dot(cctx8, l) for l in range(2)]


def ada_fwd(c_all, cctx8, w, b):
    cols = w.shape[-1]

    def body(c_ref, cc_ref, w_ref, b_ref, out_ref):
        ox, oc = _ada_fn(c_ref[...], cc_ref[...], [w_ref[0], w_ref[1]], [b_ref[0], b_ref[1]])
        for l in range(2):
            out_ref[l, 0] = ox[l]
            out_ref[l, 1] = oc[l]
    return pl.pallas_call(body, name="ada_fwd", out_shape=jax.ShapeDtypeStruct((2, 2, N_DEV, cols), F32),
                          compiler_params=_cp(0))(c_all, cctx8, w, b)


def ada_bwd(c_all, cctx8, w, b, dmx, dmc):
    cols = w.shape[-1]

    def body(c_ref, cc_ref, w_ref, b_ref, dmx_ref, dmc_ref, dw_ref, dc_ref):
        fn = lambda cc, w0, w1: _ada_fn(c_ref[...], cc, [w0, w1], [b_ref[0], b_ref[1]])
        _, vjp = jax.vjp(fn, cc_ref[...], w_ref[0], w_ref[1])
        dcc, dw0, dw1 = vjp(([dmx_ref[0], dmx_ref[1]], [dmc_ref[0], dmc_ref[1]]))
        dw_ref[0] = dw0
        dw_ref[1] = dw1
        dc_ref[...] = jnp.sum(dcc, axis=0, keepdims=True)
    return pl.pallas_call(body, name="ada_bwd", out_shape=[jax.ShapeDtypeStruct((2, D, cols), F32), jax.ShapeDtypeStruct((1, D), F32)],
                          compiler_params=_cp(0))(c_all, cctx8, w, b, dmx, dmc)


def adamw(w, m, v, gparts, name):
    r, c = w.shape
    p = gparts.shape[0]
    rt = r
    while rt % 16 == 0 and (p + 7) * rt * c * 4 * 2 > 24 * 2 ** 20:
        rt //= 2

    def body(w_ref, m_ref, v_ref, g_ref, go_ref, d_ref, mo_ref, vo_ref):
        g = g_ref[0].astype(F32)
        for k in range(1, p):
            g = g + g_ref[k].astype(F32)
        m2 = ADAM_B1 * m_ref[...] + (1.0 - ADAM_B1) * g
        v2 = ADAM_B2 * v_ref[...] + (1.0 - ADAM_B2) * (g * g)
        m_hat = m2 / (1.0 - ADAM_B1 ** ADAM_STEP)
        v_hat = v2 / (1.0 - ADAM_B2 ** ADAM_STEP)
        go_ref[...] = g
        d_ref[...] = -ADAM_LR * (m_hat / (jnp.sqrt(v_hat) + ADAM_EPS) + ADAM_WD * w_ref[...])
        mo_ref[...] = m2
        vo_ref[...] = v2

    tile = pl.BlockSpec((rt, c), lambda i: (i, 0))
    return pl.pallas_call(
        body, name=name, grid=(r // rt,),
        in_specs=[tile, tile, tile, pl.BlockSpec((p, rt, c), lambda i: (0, i, 0))], out_specs=[tile] * 4,
        out_shape=[jax.ShapeDtypeStruct((r, c), F32)] * 4, compiler_params=_cp(1))(w, m, v, gparts)


def _me():
    x, y, c = lax.axis_index("x"), lax.axis_index("y"), lax.axis_index("c")
    return x, y, c, 4 * x + 2 * y + c


def _peer(x, y, c, p):
    fx, fy, fc = (p >> 2) & 1, (p >> 1) & 1, p & 1
    return (1 - x if fx else x, 1 - y if fy else y, 1 - c if fc else c)


def all_gather(arrs, name, after=None):
    n = len(arrs)
    extra = [] if after is None else [after]

    def body(*refs):
        ins, outs = refs[:n], refs[n + len(extra):2 * n + len(extra)]
        send, recv, local = refs[2 * n + len(extra):]
        x, y, c, me = _me()
        copies = []
        for a in range(n):
            lc = pltpu.make_async_copy(ins[a], outs[a].at[me], local.at[a])
            lc.start()
            copies.append(lc)
            for p in range(1, N_DEV):
                cp = pltpu.make_async_remote_copy(src_ref=ins[a], dst_ref=outs[a].at[me], send_sem=send.at[a, p - 1],
                                                  recv_sem=recv.at[a, p - 1], device_id=_peer(x, y, c, p),
                                                  device_id_type=pl.DeviceIdType.MESH)
                cp.start()
                copies.append(cp)
        for cp in copies:
            cp.wait()

    return pl.pallas_call(
        body, name=name, in_specs=[ANY] * (n + len(extra)), out_specs=[ANY] * n,
        out_shape=[jax.ShapeDtypeStruct((N_DEV,) + a.shape, a.dtype) for a in arrs],
        scratch_shapes=[pltpu.SemaphoreType.DMA((n, N_DEV - 1)), pltpu.SemaphoreType.DMA((n, N_DEV - 1)),
                        pltpu.SemaphoreType.DMA((n,))])(*arrs, *extra)


HBM = pl.BlockSpec(memory_space=pltpu.HBM)
SEM = pl.BlockSpec(memory_space=pltpu.SEMAPHORE)


def _in_hbm(a):
    return pltpu.with_memory_space_constraint(a, pltpu.HBM)


def _exchange_refs(srcs, lands, layer, scatter, a, x, y, c, p):
    me = 4 * x + 2 * y + c
    px, py, pc = _peer(x, y, c, p) if p else (x, y, c)
    dst = lands[a].at[me] if layer is None else lands[a].at[me, layer]
    src = srcs[a].at[4 * px + 2 * py + pc] if scatter else dst
    return src, dst, (px, py, pc)


def exchange_start(srcs, lands, layer, scatter, name, after=None):
    n, ns = len(lands), len(srcs)
    extra = [] if after is None else [after]

    def body(*refs):
        ins, lz = refs[:ns], refs[ns:ns + n]
        send, recv = refs[ns + n + len(extra)], refs[ns + n + len(extra) + 1]
        token = refs[-1]
        x, y, c, _ = _me()
        for a in range(n):
            for p in range(1, N_DEV):
                src, dst, peer = _exchange_refs(ins, lz, layer, scatter, a, x, y, c, p)
                k = a * (N_DEV - 1) + p - 1
                pltpu.make_async_remote_copy(src_ref=src, dst_ref=dst, send_sem=send.at[k], recv_sem=recv.at[k],
                                             device_id=peer, device_id_type=pl.DeviceIdType.MESH).start()
        token[...] = jnp.zeros_like(token)

    thru = [pltpu.HBM(a.shape, a.dtype) for a in list(srcs) + list(lands)]
    out = pl.pallas_call(
        body, name=name, in_specs=[HBM] * (ns + n) + [ANY] * len(extra),
        out_specs=[SEM, SEM] + [HBM] * (ns + n) + [pl.BlockSpec(memory_space=pltpu.VMEM)],
        out_shape=[pltpu.SemaphoreType.DMA((n * (N_DEV - 1),)), pltpu.SemaphoreType.DMA((n * (N_DEV - 1),))] + thru
        + [jax.ShapeDtypeStruct((8, 128), F32)],
        input_output_aliases={i: 2 + i for i in range(ns + n)},
        compiler_params=pltpu.CompilerParams(has_side_effects=pltpu.SideEffectType.DATAFLOW_SIDE_EFFECTING),
    )(*[_in_hbm(a) for a in list(srcs) + list(lands)], *extra)
    return out[0], out[1], out[2:2 + ns], out[2 + ns:2 + ns + n], out[-1]


def exchange_wait(send, recv, srcs, lands, layer, scatter, after, name):
    n, ns = len(lands), len(srcs)

    def body(*refs):
        ins, lz = refs[:ns], refs[ns:ns + n]
        send_ref, recv_ref = refs[ns + n], refs[ns + n + 1]
        x, y, c, _ = _me()
        for a in range(n):
            for p in range(1, N_DEV):
                src, dst, peer = _exchange_refs(ins, lz, layer, scatter, a, x, y, c, 0)
                k = a * (N_DEV - 1) + p - 1
                cp = pltpu.make_async_remote_copy(src_ref=src, dst_ref=dst, send_sem=send_ref.at[k],
                                                  recv_sem=recv_ref.at[k], device_id=peer,
                                                  device_id_type=pl.DeviceIdType.MESH)
                cp.wait_send()
                cp.wait_recv()

    thru = [pltpu.HBM(a.shape, a.dtype) for a in list(srcs) + list(lands)]
    out = pl.pallas_call(
        body, name=name, in_specs=[HBM] * (ns + n) + [SEM, SEM, ANY], out_specs=[HBM] * (ns + n), out_shape=thru,
        input_output_aliases={i: i for i in range(ns + n)},
        compiler_params=pltpu.CompilerParams(has_side_effects=pltpu.SideEffectType.DATAFLOW_SIDE_EFFECTING),
    )(*srcs, *lands, send, recv, after)
    return out[ns:]


def place_own(src, land, me, layer, scatter, name):
    create = isinstance(land, jax.ShapeDtypeStruct)
    r, c = src.shape[-2:]
    rt = r
    while rt % 32 == 0 and rt * c * 4 > 2 ** 21:
        rt //= 2

    def body(me_ref, src_ref, *rest):
        out_ref = rest[-1]
        out_ref[...] = src_ref[...].reshape(out_ref.shape).astype(out_ref.dtype)

    src_spec = (pl.BlockSpec((1, rt, c), lambda i, m: (m[0], i, 0)) if scatter else pl.BlockSpec((rt, c), lambda i, m: (i, 0)))
    out_spec = (pl.BlockSpec((1, rt, c), lambda i, m: (m[0], i, 0)) if layer is None
                else pl.BlockSpec((1, 1, rt, c), lambda i, m: (m[0], layer, i, 0)))
    grid_spec = pltpu.PrefetchScalarGridSpec(num_scalar_prefetch=1, grid=(r // rt,),
                                             in_specs=[src_spec] + ([] if create else [ANY]), out_specs=out_spec)
    return pl.pallas_call(body, name=name, grid_spec=grid_spec, out_shape=jax.ShapeDtypeStruct(land.shape, land.dtype),
                          input_output_aliases={} if create else {2: 0}, compiler_params=_cp(1),
                          )(*((me, src) if create else (me, src, land)))


def _scan_constants():
    r = lax.broadcasted_iota(jnp.int32, (CH, CH), 0)
    s = lax.broadcasted_iota(jnp.int32, (CH, CH), 1)
    lower = (s <= r).astype(F32)
    t = jnp.arange(CH)[:, None]
    mc = jnp.stack([lower, lower.T])
    mref = jnp.stack([(t <= CH // 2 - 1).astype(F32), (t >= CH // 2).astype(F32)])
    return mc, jnp.stack([lower.T, lower]), mref


def local_step(x, ctx, target, mod, lb, w, fetch=None, publish=None, small_ready=None, small_early=None):
    kept = {}

    def keep(l, part, grads):
        kept[(l, part)] = grads
        return 0.0

    fetch = fetch or (lambda l, part, after: w)
    publish = publish or keep
    n_layers = len(mod)
    mc, mtc, mrefc = _scan_constants()
    xs = jnp.concatenate([ctx, x], axis=0)
    saved, big = [], []
    for l in range(n_layers):
        wl = dict(fetch(l, "in", xs))
        parts, ht = in_proj_fwd(xs, mod[l], w["nw1"][l], wl["win"][l])
        o, ck = hgrn_fwd(parts, lb[l], mc, mtc, mrefc)
        wl.update(fetch(l, "rest", o))
        x1, pa, pb, ym, yat, ybt, mt = mixer_fwd(xs, parts, o, mod[l], w["lnw"][l], w["lnb"][l], w["sw"][l], w["sb"][l],
                                                 w["hnw"][l], wl["wa"][l], wl["wb"][l], wl["wo"][l])
        av, h2t = ffn_up_fwd(x1, mod[l], w["nw2"][l], wl["wup"][l])
        x2, ac, y, z = ffn_down_fwd(x1, av, mod[l], w["cw"][l], w["cb"][l], wl["wd"][l])
        saved.append((xs, parts, o, ck, x1, av, ac, y, z, ht, h2t, pa, pb, ym, yat, ybt, mt))
        big.append(wl)
        xs = x2
    loss, dx, dfw = loss_fwd_bwd(xs, target, w["fw"])
    g = {k: [None] * n_layers for k in ("nw1", "nw2", "lnw", "lnb", "sw", "sb", "hnw", "cw", "cb")}
    g["fw"] = dfw
    dmod, dlb = [None] * n_layers, [None] * n_layers
    tok = 0.0
    for l in reversed(range(n_layers)):
        x0, parts, o, ck, x1, av, ac, y, z, ht, h2t, pa, pb, ym, yat, ybt, mt = saved[l]
        wl = big[l]
        dav, dac, dout, dg2 = ffn_down_bwd(dx, ac, av, y, mod[l] + tok, wl["wd"][l])
        dwd = weight_grad_rows(z, dout, "ffn_down_bwd_w")
        dav, g["cw"][l], g["cb"][l] = conv_bwd(dav, dac, av, w["cw"][l])
        dx1, g["nw2"][l], dmod2 = ffn_up_bwd_x(dx, x1, dav, mod[l], w["nw2"][l], wl["wup"][l])
        dwup = weight_grad(h2t, dav, FF_SLOT, "ffn_up_bwd_w")
        tok = publish(l, "ffn", {"wd": dwd, "wup": dwup})
        (dparts, do, dy, dpa, dpb, g["lnw"][l], g["lnb"][l], g["sw"][l], g["sb"][l], g["hnw"][l],
         dg1) = mixer_bwd(dx1, parts, o, pa, pb, ym, mod[l] + tok, w["lnw"][l], w["lnb"][l], w["sw"][l], w["sb"][l],
                          w["hnw"][l], wl["wa"][l], wl["wb"][l], wl["wo"][l])
        tok = publish(l, "mix", {"wa": weight_grad(yat, dpa, D, "mixer_bwd_wa"), "wb": weight_grad(ybt, dpb, D, "mixer_bwd_wb"),
                                 "wo": weight_grad(mt, dy, D, "mixer_bwd_wo")})
        if l == 0 and small_early:
            dmod[0] = jnp.concatenate([jnp.zeros((2, 2, 1, D), F32), dg1, dmod2, dg2], axis=1)
            tok = tok + small_early(loss[0, 0], g, dmod, dlb)
        dq, df, di, dlb_f = hgrn_bwd(0, parts, lb[l] + tok, mc, mtc, mrefc, ck, do)
        dparts, dlb_b = hgrn_bwd(1, parts, lb[l], mc, mtc, mrefc, ck, do, (dq, df, di), dparts)
        dlb[l] = jnp.concatenate([dlb_f, dlb_b], axis=0)
        tok = publish(l, "in", {"win": weight_grad(ht, dparts, IN_SLOT, "in_proj_bwd_w")})
        dx, g["nw1"][l], dmod1 = in_proj_bwd_x(dx1, x0, dparts, mod[l], w["nw1"][l], wl["win"][l], after=tok)
        dmod[l] = jnp.concatenate([dmod1, dg1, dmod2, dg2], axis=1)
    done = small_ready(loss[0, 0], g, dmod, dlb) if small_ready else 0.0
    for (l, part), grads in kept.items():
        for k, v in grads.items():
            g.setdefault(k, [None] * n_layers)[l] = v
    return loss[0, 0], dx[CTX:], g, dmod, dlb, done


ROW = 1024
REPLICATED = ("norm1_w", "sgu_ln_w", "sgu_ln_b", "sgu_w", "sgu_b", "hgrn_lower_bounds", "hgrn_norm_w", "norm2_w",
              "ffn_conv_b", "final_norm_w")
WEIGHT_ORDER = ("c_ctx", "ada_w", "ada_b", "norm1_w", "w_in", "sgu_ln_w", "sgu_ln_b", "sgu_w", "sgu_b", "hgrn_lower_bounds",
                "hgrn_norm_w", "w_branch_a", "w_branch_b", "w_out", "norm2_w", "ffn_w_up", "ffn_conv_w", "ffn_conv_b",
                "ffn_w_down", "final_norm_w")


def _rows_of(n):
    return -(-n // (8 * ROW)) * 8


def _pack(arrs, total_rows=None):
    parts = []
    for a in arrs:
        flat = a.reshape(-1).astype(F32)
        rows = _rows_of(flat.shape[0])
        parts.append(jnp.pad(flat, (0, rows * ROW - flat.shape[0])).reshape(rows, ROW))
    have = sum(p.shape[0] for p in parts)
    if total_rows is not None and total_rows > have:
        parts.append(jnp.zeros((total_rows - have, ROW), F32))
    return jnp.concatenate(parts, axis=0)


def _unpack(packed, shapes):
    lead = packed.shape[:-2]
    out, r0 = [], 0
    for s in shapes:
        n = math.prod(s)
        rows = _rows_of(n)
        out.append(packed[..., r0:r0 + rows, :].reshape(lead + (rows * ROW,))[..., :n].reshape(lead + tuple(s)))
        r0 += rows
    return out


def kernel(x, c, ctx, c_ctx, ada_w, ada_b, norm1_w, w_in, sgu_ln_w, sgu_ln_b, sgu_w, sgu_b, hgrn_lower_bounds, hgrn_norm_w, w_branch_a, w_branch_b, w_out, norm2_w, ffn_w_up, ffn_conv_w, ffn_conv_b, ffn_w_down, final_norm_w, loss_target, m_c_ctx, m_ada_w, m_ada_b, m_norm1_w, m_w_in, m_sgu_ln_w, m_sgu_ln_b, m_sgu_w, m_sgu_b, m_hgrn_lower_bounds, m_hgrn_norm_w, m_w_branch_a, m_w_branch_b, m_w_out, m_norm2_w, m_ffn_w_up, m_ffn_conv_w, m_ffn_conv_b, m_ffn_w_down, m_final_norm_w, v_c_ctx, v_ada_w, v_ada_b, v_norm1_w, v_w_in, v_sgu_ln_w, v_sgu_ln_b, v_sgu_w, v_sgu_b, v_hgrn_lower_bounds, v_hgrn_norm_w, v_w_branch_a, v_w_branch_b, v_w_out, v_norm2_w, v_ffn_w_up, v_ffn_conv_w, v_ffn_conv_b, v_ffn_w_down, v_final_norm_w):
    wts = dict(c_ctx=c_ctx, ada_w=ada_w, ada_b=ada_b, norm1_w=norm1_w, w_in=w_in, sgu_ln_w=sgu_ln_w, sgu_ln_b=sgu_ln_b,
               sgu_w=sgu_w, sgu_b=sgu_b, hgrn_lower_bounds=hgrn_lower_bounds, hgrn_norm_w=hgrn_norm_w, w_branch_a=w_branch_a,
               w_branch_b=w_branch_b, w_out=w_out, norm2_w=norm2_w, ffn_w_up=ffn_w_up, ffn_conv_w=ffn_conv_w,
               ffn_conv_b=ffn_conv_b, ffn_w_down=ffn_w_down, final_norm_w=final_norm_w)
    mom1 = dict(c_ctx=m_c_ctx, ada_w=m_ada_w, ada_b=m_ada_b, norm1_w=m_norm1_w, w_in=m_w_in, sgu_ln_w=m_sgu_ln_w,
                sgu_ln_b=m_sgu_ln_b, sgu_w=m_sgu_w, sgu_b=m_sgu_b, hgrn_lower_bounds=m_hgrn_lower_bounds,
                hgrn_norm_w=m_hgrn_norm_w, w_branch_a=m_w_branch_a, w_branch_b=m_w_branch_b, w_out=m_w_out, norm2_w=m_norm2_w,
                ffn_w_up=m_ffn_w_up, ffn_conv_w=m_ffn_conv_w, ffn_conv_b=m_ffn_conv_b, ffn_w_down=m_ffn_w_down,
                final_norm_w=m_final_norm_w)
    mom2 = dict(c_ctx=v_c_ctx, ada_w=v_ada_w, ada_b=v_ada_b, norm1_w=v_norm1_w, w_in=v_w_in, sgu_ln_w=v_sgu_ln_w,
                sgu_ln_b=v_sgu_ln_b, sgu_w=v_sgu_w, sgu_b=v_sgu_b, hgrn_lower_bounds=v_hgrn_lower_bounds,
                hgrn_norm_w=v_hgrn_norm_w, w_branch_a=v_w_branch_a, w_branch_b=v_w_branch_b, w_out=v_w_out, norm2_w=v_norm2_w,
                ffn_w_up=v_ffn_w_up, ffn_conv_w=v_ffn_conv_w, ffn_conv_b=v_ffn_conv_b, ffn_w_down=v_ffn_w_down,
                final_norm_w=v_final_norm_w)
    n_layers = w_in.shape[0]
    layers = range(n_layers)
    me = 4 * lax.axis_index("x") + 2 * lax.axis_index("y") + lax.axis_index("c")
    ada_cols = ada_w.shape[-1]

    gathered = all_gather([ffn_conv_w.reshape(n_layers, 9, -1), c], "gather_conv_c")
    conv_all, c_all = gathered[0], gathered[1].reshape(N_DEV, D)
    conv_full = [conv_all[:, l].transpose(1, 0, 2).reshape(9, N_FFK, FF_SLOT).transpose(1, 0, 2) for l in layers]
    cctx8 = jnp.broadcast_to(c_ctx[None, :], (N_DEV, D))
    ada_b_cols = lax.dynamic_slice_in_dim(ada_b, me * ada_cols, ada_cols, axis=1)[:, None, :]
    mod_cols = ada_fwd(c_all, cctx8, ada_w, ada_b_cols)
    (mod_all,) = all_gather([mod_cols], "gather_mod")

    big = ("w_in", "ffn_w_up", "w_branch_a", "w_branch_b", "w_out", "ffn_w_down")
    short = {"w_in": "win", "ffn_w_up": "wup", "w_branch_a": "wa", "w_branch_b": "wb", "w_out": "wo", "ffn_w_down": "wd"}
    me1 = me.reshape(1).astype(jnp.int32)
    groups = [[("w_in", 0)], [(k, 0) for k in big[1:]], [(k, 1) for k in big]]
    in_flight, started = [], 0.0
    for n, group in enumerate(groups):
        lands = [place_own(wts[k][l], jax.ShapeDtypeStruct((N_DEV,) + wts[k].shape[1:], BF16), me1, None, False,
                           f"gather_own_{short[k]}_{l}") for k, l in group]
        in_flight.append(exchange_start([], lands, None, False, f"gather_weights_start_{n}",
                                        after=in_flight[-1][-1] if in_flight else mod_all))
        started = started + in_flight[-1][-1][0, 0]

    def as_used(k, a):
        return a if k in ("w_in", "ffn_w_up") else a.reshape(N_FFK, FF_SLOT, D) if k == "ffn_w_down" else a.reshape(D, D)

    arrived = {}

    def fetch(l, part, after):
        n = {(0, "in"): 0, (0, "rest"): 1, (1, "in"): 2}.get((l, part))
        if n is not None:
            send, recv, _, lands, _ = in_flight[n]
            got = exchange_wait(send, recv, [], lands, None, False, after, f"gather_weights_wait_{n}")
            for (k, ll), a in zip(groups[n], got):
                arrived.setdefault(short[k], [None] * n_layers)[ll] = as_used(k, a)
        return arrived

    mod_x = lax.dynamic_index_in_dim(mod_all[:, :, 0], me, axis=2, keepdims=False)
    mod_c = mod_all[:, :, 1, 0]
    mod = [jnp.stack([mod_c[:, l].reshape(6, 1, D), mod_x[:, l].reshape(6, 1, D)]) for l in layers]
    mod[0] = mod[0] + started

    lb1 = lower_bounds(hgrn_lower_bounds)
    lb = [jnp.zeros((2, 1, D), F32), lb1.reshape(2, 1, D)]

    w = {
        "nw1": [norm1_w[l][None] for l in layers], "nw2": [norm2_w[l][None] for l in layers],
        "lnw": [sgu_ln_w[l][None] for l in layers], "lnb": [sgu_ln_b[l][None] for l in layers],
        "sw": [sgu_w[l] for l in layers], "sb": [sgu_b[l][:, :, None] for l in layers],
        "hnw": [hgrn_norm_w[l][None] for l in layers], "cw": conv_full,
        "cb": [ffn_conv_b[l].reshape(N_FFK, 1, FF_SLOT) for l in layers], "fw": final_norm_w[None],
    }
    long = {v: k for k, v in short.items()}
    landing, sent = {}, []

    def publish(l, part, grads):
        keys = [long[k] for k in grads]
        slots = [a.reshape((N_DEV, -1, a.shape[-1])) for a in grads.values()]
        zones = [place_own(s, landing.get(k, jax.ShapeDtypeStruct((N_DEV, n_layers) + s.shape[1:], s.dtype)), me1, l, True,
                           f"scatter_own_{short[k]}_{l}") for k, s in zip(keys, slots)]
        send, recv, srcs, zones, token = exchange_start(slots, zones, l, True, f"scatter_grads_start_{part}_{l}")
        landing.update(zip(keys, zones))
        sent.append((keys, l, part, send, recv, srcs, token))
        return token[0, 0]

    out = {}
    flat2 = lambda a: a.reshape(-1, a.shape[-1])

    def finish(part, after):
        done = []
        for keys, l, p, send, recv, srcs, _ in sent:
            if p == part:
                zones = exchange_wait(send, recv, srcs, [landing[k] for k in keys], l, True, after,
                                      f"scatter_grads_wait_{part}_{l}")
                landing.update(zip(keys, zones))
                done = keys
        for k in done:
            r = landing[k]
            res = adamw(flat2(wts[k]), flat2(mom1[k]), flat2(mom2[k]), r.reshape(N_DEV, -1, r.shape[-1]), "adamw_" + k)
            out[k] = tuple(a.reshape(wts[k].shape) for a in res)

    rep_rows = -(-sum(_rows_of(wts[k].size) for k in REPLICATED) // 64) * 64
    conv_rows = _rows_of(n_layers * 9 * D_FF)
    dmod_rows = _rows_of(n_layers * 6 * D)
    early = {}

    def small_early(loss_part, g, dmod, dlb):
        d_hlb = lower_bounds_bwd(hgrn_lower_bounds, dlb[1].reshape(1, 2 * D))
        st = lambda k: jnp.stack([jnp.zeros((1, D), F32) if a is None else a for a in g[k]])
        rep_grads = {"norm1_w": st("nw1"), "sgu_ln_w": st("lnw"), "sgu_ln_b": st("lnb"), "sgu_w": st("sw"), "sgu_b": st("sb"),
                     "hgrn_lower_bounds": d_hlb, "hgrn_norm_w": st("hnw"), "norm2_w": st("nw2"), "ffn_conv_b": st("cb"),
                     "final_norm_w": g["fw"]}
        d_conv = jnp.stack([g["cw"][l].transpose(1, 0, 2).reshape(9, D_FF) for l in layers])
        dmod_x = jnp.stack([dmod[l][1].reshape(6 * D) for l in layers])
        dmod_c = jnp.stack([dmod[l][0].reshape(6 * D) for l in layers])
        small = jnp.concatenate([_pack([rep_grads[k] for k in REPLICATED], rep_rows),
                                 _pack([d_conv, dmod_x, dmod_c, loss_part.reshape(1)])], axis=0)
        zone = place_own(small, jax.ShapeDtypeStruct((N_DEV,) + small.shape, F32), me1, None, False, "gather_small_own")
        early["send"], early["recv"], _, early["zones"], token = exchange_start([], [zone], None, False, "gather_small_start")
        return token[0, 0]

    def small_ready(loss_part, g, dmod, dlb):
        late = _pack([g["nw1"][0], dmod[0][1, 0:2], dmod[0][0, 0:2]])
        for part in ("ffn", "mix"):
            finish(part, late)
        (late_all,) = all_gather([late], "gather_small_late", after=out["w_out"][0])
        (small_all,) = exchange_wait(early["send"], early["recv"], [], early["zones"], None, False, late_all,
                                     "gather_small_wait")
        at_x = rep_rows + conv_rows
        small_all = small_all.at[:, 0:1].set(late_all[:, 0:1])
        small_all = small_all.at[:, at_x:at_x + 2].set(late_all[:, 8:10])
        small_all = small_all.at[:, at_x + dmod_rows:at_x + dmod_rows + 2].set(late_all[:, 16:18])
        d_conv_shape, dmod_shape = (n_layers, 9, D_FF), (n_layers, 6 * D)
        conv_g, dmx_all, dmc_all, loss_all = _unpack(small_all[:, rep_rows:], [d_conv_shape, dmod_shape, dmod_shape, (1,)])
        out["loss"] = functools.reduce(lambda a, b: a + b, [loss_all[k, 0] for k in range(N_DEV)])

        rep = adamw(_pack([wts[k] for k in REPLICATED], rep_rows), _pack([mom1[k] for k in REPLICATED], rep_rows),
                    _pack([mom2[k] for k in REPLICATED], rep_rows), small_all[:, :rep_rows], "adamw_replicated")
        rep = [_unpack(r, [wts[k].shape for k in REPLICATED]) for r in rep]
        for n, k in enumerate(REPLICATED):
            out[k] = tuple(r[n] for r in rep)

        conv_mine = lax.dynamic_index_in_dim(conv_g.reshape(N_DEV, n_layers, 9, N_DEV, -1), me, axis=3, keepdims=False)
        res = adamw(flat2(ffn_conv_w), flat2(m_ffn_conv_w), flat2(v_ffn_conv_w),
                    conv_mine.reshape(N_DEV, -1, conv_mine.shape[-1]), "adamw_conv_w")
        out["ffn_conv_w"] = tuple(r.reshape(ffn_conv_w.shape) for r in res)

        out["ada_b"] = tuple(adamw(ada_b, m_ada_b, v_ada_b, jnp.concatenate([dmx_all, dmc_all], axis=0), "adamw_ada_b"))

        cols_of = lambda a: lax.dynamic_slice_in_dim(a, me * ada_cols, ada_cols, axis=2).transpose(1, 0, 2)
        d_ada_w, d_cctx = ada_bwd(c_all, cctx8, ada_w, ada_b_cols, cols_of(dmx_all), cols_of(dmc_all))
        res = adamw(flat2(ada_w), flat2(m_ada_w), flat2(v_ada_w), flat2(d_ada_w)[None], "adamw_ada_w")
        out["ada_w"] = tuple(r.reshape(ada_w.shape) for r in res)
        (d_cctx_all,) = all_gather([d_cctx], "gather_c_ctx_grad")
        res = adamw(c_ctx[None], m_c_ctx[None], v_c_ctx[None], d_cctx_all, "adamw_c_ctx")
        out["c_ctx"] = tuple(r[0] for r in res)
        return d_cctx_all

    _, grad_x, _, _, _, small_done = local_step(x[0], ctx[0], loss_target[0], mod, lb, w, fetch, publish, small_ready,
                                                small_early)
    loss = out["loss"]

    finish("in", small_done)
    return (loss, grad_x[None]) + tuple(out[k][n] for n in range(4) for k in WEIGHT_ORDER)
```

```python
import functools
import math

import jax
import jax.numpy as jnp
from jax import lax
from jax.experimental import pallas as pl
from jax.experimental.pallas import tpu as pltpu

F32 = jnp.float32
BF16 = jnp.bfloat16
HIGHEST = lax.Precision.HIGHEST

N_DEV = 8
AXES = ("x", "y", "c")
D = 1024
CTX = 256
TM = 256
CH = 64
SGU_CH = 128
HEADS = 8
HD = 128
GRID_W = 64
D_IN = 9 * D
IN_SLOT = D_IN // N_DEV
D_FF = 2816
FF_SLOT = 2 * D_FF // N_DEV
N_FFK = D_FF // FF_SLOT
RMS_EPS = 1e-6
LN_EPS = 1e-5
ADAM_LR, ADAM_B1, ADAM_B2, ADAM_EPS, ADAM_WD, ADAM_STEP = 0.001, 0.9, 0.999, 1e-08, 0.01, 10
VMEM_LIMIT_V7X = 56 * 2 ** 20
GRAD_WIRE = jnp.bfloat16

VMEM_WHOLE = pl.BlockSpec(memory_space=pltpu.VMEM)
ANY = pl.BlockSpec(memory_space=pl.ANY)


def _cp(n_axes):
    return pltpu.CompilerParams(dimension_semantics=("arbitrary",) * n_axes, vmem_limit_bytes=VMEM_LIMIT_V7X)


def _dot(a, b, dims):
    return lax.dot_general(a.astype(BF16), b.astype(BF16), (dims, ((), ())), preferred_element_type=F32)


@jax.custom_vjp
def mm(a, b):
    return _dot(a, b, ((1,), (0,)))


mm.defvjp(lambda a, b: (mm(a, b), (a, b)),
          lambda r, g: (_dot(g, r[1], ((1,), (1,))).astype(r[0].dtype), _dot(r[0], g, ((0,), (0,))).astype(r[1].dtype)))


@jax.custom_vjp
def mm_nt(a, b):
    return _dot(a, b, ((1,), (1,)))


mm_nt.defvjp(lambda a, b: (mm_nt(a, b), (a, b)),
             lambda r, g: (_dot(g, r[1], ((1,), (0,))).astype(r[0].dtype), _dot(g, r[0], ((0,), (0,))).astype(r[1].dtype)))


@jax.custom_vjp
def mm_tn(a, b):
    return _dot(a, b, ((0,), (0,)))


mm_tn.defvjp(lambda a, b: (mm_tn(a, b), (a, b)),
             lambda r, g: (_dot(r[1], g, ((1,), (1,))).astype(r[0].dtype), _dot(r[0], g, ((1,), (0,))).astype(r[1].dtype)))


def _tri_dot(m, g):
    hi = g.astype(BF16)
    rest = g - hi.astype(F32)
    mid = rest.astype(BF16)
    low = (rest - mid.astype(F32)).astype(BF16)
    n = g.shape[1]
    out = jnp.dot(m.astype(BF16), jnp.concatenate([hi, mid, low], axis=1), preferred_element_type=F32)
    return out[:, :n] + out[:, n:2 * n] + out[:, 2 * n:]


@jax.custom_vjp
def _cum(m, mt, g):
    return _tri_dot(m, g)


_cum.defvjp(lambda m, mt, g: (_cum(m, mt, g), (m, mt)),
            lambda r, d: (jnp.zeros_like(r[0]), jnp.zeros_like(r[1]), _tri_dot(r[1], d)))


def _silu(x):
    return x * jax.nn.sigmoid(x)


def _gelu(x):
    return 0.5 * x * (1.0 + jnp.tanh(math.sqrt(2.0 / math.pi) * (x + 0.044715 * (x * x * x))))


def _rms(x, w):
    return x * lax.rsqrt(jnp.mean(x * x, axis=-1, keepdims=True) + RMS_EPS) * w


def _norm_mod(x, w, shift, scale):
    return _rms(x, w) * (1.0 + scale) + shift


def _hsl(h):
    return slice(h * HD, (h + 1) * HD)


def _hgrn_chunk(st, qz, fz, iv, lb, m, mt, mref):
    hs = range(HEADS)
    keep = [1.0 - lb[h] for h in hs]
    g = [jnp.log(lb[h] + keep[h] * jax.nn.sigmoid(fz[h])) for h in hs]
    k = [keep[h] * jax.nn.sigmoid(-fz[h]) for h in hs]
    q = [_silu(qz[h]) for h in hs]
    b = [_cum(m, mt, g[h]) for h in hs]
    ref = [jnp.sum(mref * g[h], axis=0, keepdims=True) for h in hs]
    last = [jnp.sum(g[h], axis=0, keepdims=True) for h in hs]
    qa = [q[h] * jnp.exp(b[h] - ref[h]) for h in hs]
    ka = [k[h] * jnp.exp(ref[h] - b[h]) for h in hs]
    scores = [jnp.where(m > 0.5, mm_nt(qa[h], ka[h]), 0.0) for h in hs]
    inter = [mm_nt(qa[h] * jnp.exp(ref[h]), st[h]) for h in hs]
    kv = [mm_tn(iv[h], ka[h] * jnp.exp(last[h] - ref[h])) for h in hs]
    outs = [mm(scores[h], iv[h]) + inter[h] for h in hs]
    news = [jnp.exp(last[h]) * st[h] + kv[h] for h in hs]
    return outs, news


def _sgu_fn(ub, vb, lnw, lnb, sw, sb):
    gv = [_gelu(v) for v in vb]
    mu = sum(jnp.sum(t, axis=-1, keepdims=True) for t in gv) / D
    var = sum(jnp.sum((t - mu) * (t - mu), axis=-1, keepdims=True) for t in gv) / D
    inv = lax.rsqrt(var + LN_EPS)
    cols = []
    for g in range(HEADS):
        vn = (gv[g] - mu) * inv * lnw[g] + lnb[g]
        cols.append(_gelu(ub[g]) * (mm(sw[g], vn) + sb[g]))
    return jnp.concatenate(cols, axis=1)


def _readout_fn(ob, og, hnw):
    r = [o * lax.rsqrt(jnp.mean(o * o, axis=-1, keepdims=True) + RMS_EPS) * hnw for o in ob]
    return jnp.concatenate(r, axis=1) * _silu(og)


def _glu_fn(ac, v):
    return _gelu(ac) * v


def _stream_row(tm):
    n_ctx = CTX // tm
    return lambda i: (jnp.where(i < n_ctx, 0, 1), 0, 0, 0)


def in_proj_fwd(x, mod, nw, wg):
    t = x.shape[0]

    def body(x_ref, mod_ref, nw_ref, w_ref, out_ref, ht_ref):
        h32 = _norm_mod(x_ref[...], nw_ref[...], mod_ref[0, 0], mod_ref[0, 1])
        ht_ref[...] = h32.T.astype(BF16)
        h = h32.astype(BF16)
        for j in range(N_DEV):
            out_ref[:, j * IN_SLOT:(j + 1) * IN_SLOT] = jnp.dot(h, w_ref[j], preferred_element_type=F32)

    return pl.pallas_call(
        body, name="in_proj_fwd", grid=(t // TM,),
        in_specs=[pl.BlockSpec((TM, D), lambda i: (i, 0)), pl.BlockSpec((1, 6, 1, D), _stream_row(TM)),
                  pl.BlockSpec((1, D), lambda i: (0, 0)), VMEM_WHOLE],
        out_specs=[pl.BlockSpec((TM, D_IN), lambda i: (i, 0)), pl.BlockSpec((D, TM), lambda i: (0, i))],
        out_shape=[jax.ShapeDtypeStruct((t, D_IN), F32), jax.ShapeDtypeStruct((D, t), BF16)],
        compiler_params=_cp(1))(x, mod, nw, wg)


def _scan_chunk(nc):
    ncc = CTX // CH

    def chunk(d, s):
        bwd = jnp.where(s < ncc, ncc - 1 - s, nc + ncc - 1 - s)
        return jnp.where(d == 0, s, bwd)
    return chunk


def hgrn_fwd(parts, lb, mc, mtc, mrefc):
    t = parts.shape[0]
    nc = t // CH
    chunk = _scan_chunk(nc)

    def body(q_ref, f_ref, i_ref, lb_ref, m_ref, mt_ref, mr_ref, o_ref, ck_ref, st):
        @pl.when(pl.program_id(1) == 0)
        def _():
            st[...] = jnp.zeros_like(st)
        ck_ref[0, 0] = st[...]
        outs, news = _hgrn_chunk([st[h] for h in range(HEADS)], [q_ref[:, _hsl(h)] for h in range(HEADS)],
                                 [f_ref[:, _hsl(h)] for h in range(HEADS)], [i_ref[:, _hsl(h)] for h in range(HEADS)],
                                 [lb_ref[0, :, _hsl(h)] for h in range(HEADS)], m_ref[0], mt_ref[0], mr_ref[0])
        for h in range(HEADS):
            o_ref[0, :, _hsl(h)] = outs[h]
            st[h] = news[h]

    const = lambda d, s: (d, 0, 0)
    return pl.pallas_call(
        body, name="hgrn_fwd", grid=(2, nc),
        in_specs=[pl.BlockSpec((CH, D), lambda d, s: (chunk(d, s), 0)), pl.BlockSpec((CH, D), lambda d, s: (chunk(d, s), 1 + d)),
                  pl.BlockSpec((CH, D), lambda d, s: (chunk(d, s), 3)), pl.BlockSpec((1, 1, D), const),
                  pl.BlockSpec((1, CH, CH), const), pl.BlockSpec((1, CH, CH), const), pl.BlockSpec((1, CH, 1), const)],
        out_specs=[pl.BlockSpec((1, CH, D), lambda d, s: (d, chunk(d, s), 0)),
                   pl.BlockSpec((1, 1, HEADS, HD, HD), lambda d, s: (d, s, 0, 0, 0))],
        out_shape=[jax.ShapeDtypeStruct((2, t, D), F32), jax.ShapeDtypeStruct((2, nc, HEADS, HD, HD), F32)],
        scratch_shapes=[pltpu.VMEM((HEADS, HD, HD), F32)], compiler_params=_cp(2))(parts, parts, parts, lb, mc, mtc, mrefc)


def _mixer_tile(rows, u_ref, v_ref, og_ref, o_ref, lnw_ref, lnb_ref, sw_ref, sb_ref, hnw_ref):
    n = (rows.stop - rows.start) // SGU_CH
    yas, vjps = [], []
    for c in range(n):
        r = slice(rows.start + c * SGU_CH, rows.start + (c + 1) * SGU_CH)
        ya, vjp_a = jax.vjp(_sgu_fn, [u_ref[r, _hsl(g)] for g in range(HEADS)], [v_ref[r, _hsl(g)] for g in range(HEADS)],
                            [lnw_ref[:, _hsl(g)] for g in range(HEADS)], [lnb_ref[:, _hsl(g)] for g in range(HEADS)],
                            [sw_ref[g] for g in range(HEADS)], [sb_ref[g] for g in range(HEADS)])
        yas.append(ya)
        vjps.append(vjp_a)
    yb, vjp_b = jax.vjp(_readout_fn, [o_ref[0, rows, _hsl(h)] + o_ref[1, rows, _hsl(h)] for h in range(HEADS)],
                        og_ref[rows, :], hnw_ref[...])
    return (yas[0] if n == 1 else jnp.concatenate(yas, axis=0)), yb, vjps, vjp_b


def _part_specs(tm, first, n):
    return [pl.BlockSpec((tm, D), functools.partial(lambda k, i: (i, k), first + k)) for k in range(n)]


def mixer_fwd(x, parts, o, mod, lnw, lnb, sw, sb, hnw, wa, wb, wo):
    t = x.shape[0]

    def body(x_ref, u_ref, v_ref, og_ref, ga_ref, gb_ref, o_ref, mod_ref, lnw_ref, lnb_ref, sw_ref, sb_ref, hnw_ref,
             wa_ref, wb_ref, wo_ref, out_ref, pa_ref, pb_ref, y_ref, yat_ref, ybt_ref, mt_ref):
        ya, yb, _, _ = _mixer_tile(slice(0, TM), u_ref, v_ref, og_ref, o_ref, lnw_ref, lnb_ref, sw_ref, sb_ref, hnw_ref)
        pa, pb = mm(ya, wa_ref[...]), mm(yb, wb_ref[...])
        merged = jax.nn.sigmoid(ga_ref[...]) * pa + jax.nn.sigmoid(gb_ref[...]) * pb
        y = mm(merged, wo_ref[...])
        out_ref[...] = x_ref[...] + mod_ref[0, 2] * y
        pa_ref[...], pb_ref[...], y_ref[...] = pa.astype(BF16), pb.astype(BF16), y.astype(BF16)
        yat_ref[...], ybt_ref[...], mt_ref[...] = ya.T.astype(BF16), yb.T.astype(BF16), merged.T.astype(BF16)

    vec = lambda n: pl.BlockSpec((1, n), lambda i: (0, 0))
    tile = pl.BlockSpec((TM, D), lambda i: (i, 0))
    tile_t = pl.BlockSpec((D, TM), lambda i: (0, i))
    return pl.pallas_call(
        body, name="mixer_fwd", grid=(t // TM,),
        in_specs=[tile] + _part_specs(TM, 4, 5)
        + [pl.BlockSpec((2, TM, D), lambda i: (0, i, 0)), pl.BlockSpec((1, 6, 1, D), _stream_row(TM)), vec(D), vec(D),
           VMEM_WHOLE, VMEM_WHOLE, vec(HD), VMEM_WHOLE, VMEM_WHOLE, VMEM_WHOLE],
        out_specs=[tile] * 4 + [tile_t] * 3,
        out_shape=[jax.ShapeDtypeStruct((t, D), F32)] + [jax.ShapeDtypeStruct((t, D), BF16)] * 3
        + [jax.ShapeDtypeStruct((D, t), BF16)] * 3, compiler_params=_cp(1),
    )(x, parts, parts, parts, parts, parts, o, mod, lnw, lnb, sw, sb, hnw, wa, wb, wo)


def ffn_up_fwd(x, mod, nw, wg):
    t = x.shape[0]

    def body(x_ref, mod_ref, nw_ref, w_ref, out_ref, ht_ref):
        h32 = _norm_mod(x_ref[...], nw_ref[...], mod_ref[0, 3], mod_ref[0, 4])
        ht_ref[...] = h32.T.astype(BF16)
        h = h32.astype(BF16)
        for j in range(N_DEV):
            out_ref[j] = jnp.dot(h, w_ref[j], preferred_element_type=F32)

    return pl.pallas_call(
        body, name="ffn_up_fwd", grid=(t // TM,),
        in_specs=[pl.BlockSpec((TM, D), lambda i: (i, 0)), pl.BlockSpec((1, 6, 1, D), _stream_row(TM)),
                  pl.BlockSpec((1, D), lambda i: (0, 0)), VMEM_WHOLE],
        out_specs=[pl.BlockSpec((N_DEV, TM, FF_SLOT), lambda i: (0, i, 0)), pl.BlockSpec((D, TM), lambda i: (0, i))],
        out_shape=[jax.ShapeDtypeStruct((N_DEV, t, FF_SLOT), F32), jax.ShapeDtypeStruct((D, t), BF16)],
        compiler_params=_cp(1))(x, mod, nw, wg)


def _halo_specs(nt, k_of, i_of):
    per = TM // GRID_W
    last = nt * per - 1
    return [pl.BlockSpec((1, GRID_W, FF_SLOT), lambda *g: (k_of(*g), jnp.maximum(i_of(*g) * per - 1, 0), 0)),
            pl.BlockSpec((1, TM, FF_SLOT), lambda *g: (k_of(*g), i_of(*g), 0)),
            pl.BlockSpec((1, GRID_W, FF_SLOT), lambda *g: (k_of(*g), jnp.minimum(i_of(*g) * per + per, last), 0))]


def _with_halo(prev_ref, main_ref, next_ref, i, nt):
    prev = jnp.where(i >= 2, prev_ref[0], 0.0)
    nxt = jnp.where((i >= 1) & (i <= nt - 2), next_ref[0], 0.0)
    return jnp.concatenate([prev, main_ref[0], nxt], axis=0)


def _tap_valid(dc, i, n_rows, offset):
    r = lax.broadcasted_iota(jnp.int32, (n_rows, 1), 0) - offset
    col = jnp.bitwise_and(r, GRID_W - 1)
    pos = jnp.where(i == 0, r, col) + dc
    return (pos >= 0) & (pos < jnp.where(i == 0, TM, GRID_W))


def _row_weight(cw_ref, dr, dc, i):
    w = cw_ref[0, 3 * (dr + 1) + dc + 1:3 * (dr + 1) + dc + 2, :]
    return w if dr == 0 else jnp.where(i == 0, 0.0, w)


def ffn_down_fwd(x, av, mod, cw, cb, wd):
    t = x.shape[0]
    nt = t // TM
    ext = TM + 2 * GRID_W

    def body(x_ref, ap_ref, am_ref, an_ref, v_ref, mod_ref, cw_ref, cb_ref, wd_ref, out_ref, ac_ref, y_ref, z_ref, acc):
        i, k = pl.program_id(0), pl.program_id(1)
        a_ext = _with_halo(ap_ref, am_ref, an_ref, i, nt)
        conv = jnp.zeros((TM, FF_SLOT), F32) + cb_ref[0]
        for dc in (-1, 0, 1):
            rolled = a_ext if dc == 0 else jnp.where(_tap_valid(dc, i, ext, GRID_W), pltpu.roll(a_ext, (-dc) % ext, 0), 0.0)
            for dr in (-1, 0, 1):
                lo = GRID_W + GRID_W * dr
                conv = conv + rolled[lo:lo + TM] * _row_weight(cw_ref, dr, dc, i)
        ac_ref[0] = conv
        z = _glu_fn(conv, v_ref[0]).astype(BF16)
        z_ref[0] = z
        part = mm(z, wd_ref[0])

        @pl.when(k == 0)
        def _():
            acc[...] = part

        @pl.when(k > 0)
        def _():
            acc[...] += part

        @pl.when(k == N_FFK - 1)
        def _():
            y_ref[...] = acc[...]
            out_ref[...] = x_ref[...] + mod_ref[0, 5] * acc[...]

    tile = pl.BlockSpec((TM, D), lambda i, k: (i, 0))
    return pl.pallas_call(
        body, name="ffn_down_fwd", grid=(nt, N_FFK),
        in_specs=[tile] + _halo_specs(nt, lambda i, k: k, lambda i, k: i)
        + [pl.BlockSpec((1, TM, FF_SLOT), lambda i, k: (N_FFK + k, i, 0)),
           pl.BlockSpec((1, 6, 1, D), lambda i, k: (jnp.where(i < 1, 0, 1), 0, 0, 0)),
           pl.BlockSpec((1, 9, FF_SLOT), lambda i, k: (k, 0, 0)), pl.BlockSpec((1, 1, FF_SLOT), lambda i, k: (k, 0, 0)),
           pl.BlockSpec((1, FF_SLOT, D), lambda i, k: (k, 0, 0))],
        out_specs=[tile, pl.BlockSpec((1, TM, FF_SLOT), lambda i, k: (k, i, 0)), tile,
                   pl.BlockSpec((1, TM, FF_SLOT), lambda i, k: (k, i, 0))],
        out_shape=[jax.ShapeDtypeStruct((t, D), F32), jax.ShapeDtypeStruct((N_FFK, t, FF_SLOT), F32),
                   jax.ShapeDtypeStruct((t, D), F32), jax.ShapeDtypeStruct((N_FFK, t, FF_SLOT), BF16)],
        scratch_shapes=[pltpu.VMEM((TM, D), F32)], compiler_params=_cp(2))(x, av, av, av, av, mod, cw, cb, wd)


def loss_fwd_bwd(x, target, fw):
    t = x.shape[0]

    def body(x_ref, t_ref, w_ref, loss_ref, dx_ref, dw_ref):
        i = pl.program_id(0)

        @pl.when(i == 0)
        def _():
            loss_ref[...] = jnp.zeros_like(loss_ref)
            dw_ref[...] = jnp.zeros_like(dw_ref)
            dx_ref[...] = jnp.zeros_like(dx_ref)

        @pl.when(i > 0)
        def _():
            y, vjp = jax.vjp(_rms, x_ref[...], w_ref[...])
            err = y - t_ref[...]
            loss_ref[...] += 0.5 * jnp.sum(jnp.sum(err * err, axis=-1, keepdims=True) / D)
            dx, dw = vjp(err / D)
            dx_ref[...] = dx
            dw_ref[...] += dw

    return pl.pallas_call(
        body, name="loss_fwd_bwd", grid=(t // TM,),
        in_specs=[pl.BlockSpec((TM, D), lambda i: (i, 0)), pl.BlockSpec((TM, D), lambda i: (jnp.maximum(i - 1, 0), 0)),
                  pl.BlockSpec((1, D), lambda i: (0, 0))],
        out_specs=[pl.BlockSpec((8, 128), lambda i: (0, 0)), pl.BlockSpec((TM, D), lambda i: (i, 0)),
                   pl.BlockSpec((1, D), lambda i: (0, 0))],
        out_shape=[jax.ShapeDtypeStruct((8, 128), F32), jax.ShapeDtypeStruct((t, D), F32), jax.ShapeDtypeStruct((1, D), F32)],
        compiler_params=_cp(1))(x, target, fw)


def _stream_add(ref, k, is_ctx, val):
    ref[0, k] += jnp.where(is_ctx, val, 0.0)
    ref[1, k] += jnp.where(is_ctx, 0.0, val)


def ffn_down_bwd(dx, ac, av, y, mod, wd):
    t = dx.shape[0]
    nt = t // TM

    def body(dx_ref, ac_ref, v_ref, y_ref, mod_ref, wd_ref, dav_ref, dac_ref, dout_ref, dg_ref):
        i = pl.program_id(0)

        @pl.when(i == 0)
        def _():
            dg_ref[...] = jnp.zeros_like(dg_ref)

        _stream_add(dg_ref, 0, i == 0, jnp.sum(dx_ref[...] * y_ref[...], axis=0, keepdims=True))
        dout = (mod_ref[0, 5] * dx_ref[...]).astype(BF16)
        dout_ref[...] = dout
        for k in range(N_FFK):
            _, vjp = jax.vjp(_glu_fn, ac_ref[k], v_ref[k])
            dac, dv = vjp(mm_nt(dout, wd_ref[k]))
            dac_ref[k] = dac
            dav_ref[k] = dv.astype(BF16)

    tile = pl.BlockSpec((TM, D), lambda i: (i, 0))
    half = lambda first: pl.BlockSpec((N_FFK, TM, FF_SLOT), lambda i: (first, i, 0))
    return pl.pallas_call(
        body, name="ffn_down_bwd", grid=(nt,),
        in_specs=[tile, half(0), half(1), tile, pl.BlockSpec((1, 6, 1, D), _stream_row(TM)), VMEM_WHOLE],
        out_specs=[half(1), half(0), tile, pl.BlockSpec((2, 1, 1, D), lambda i: (0, 0, 0, 0))],
        out_shape=[jax.ShapeDtypeStruct((N_DEV, t, FF_SLOT), BF16), jax.ShapeDtypeStruct((N_FFK, t, FF_SLOT), F32),
                   jax.ShapeDtypeStruct((t, D), BF16), jax.ShapeDtypeStruct((2, 1, 1, D), F32)],
        compiler_params=_cp(1))(dx, ac, av, y, mod, wd)


def conv_bwd(dav, dac, av, cw):
    t = dac.shape[1]
    nt = t // TM
    ext = TM + 2 * GRID_W

    def body(dav_in, gp_ref, gm_ref, gn_ref, ap_ref, am_ref, an_ref, cw_ref, dav_ref, dcw_ref, dcb_ref):
        k, i = pl.program_id(0), pl.program_id(1)

        @pl.when(i == 0)
        def _():
            dcw_ref[...] = jnp.zeros_like(dcw_ref)
            dcb_ref[...] = jnp.zeros_like(dcb_ref)

        g_ext = _with_halo(gp_ref, gm_ref, gn_ref, i, nt)
        a_ext = _with_halo(ap_ref, am_ref, an_ref, i, nt)
        g_main = gm_ref[0]
        dcb_ref[0] += jnp.sum(g_main, axis=0, keepdims=True)
        da = jnp.zeros((TM, FF_SLOT), F32)
        for dc in (-1, 0, 1):
            g_rolled = g_ext if dc == 0 else pltpu.roll(jnp.where(_tap_valid(dc, i, ext, GRID_W), g_ext, 0.0), dc % ext, 0)
            a_rolled = a_ext if dc == 0 else pltpu.roll(a_ext, (-dc) % ext, 0)
            g_valid = g_main if dc == 0 else jnp.where(_tap_valid(dc, i, TM, 0), g_main, 0.0)
            for dr in (-1, 0, 1):
                lo = GRID_W - GRID_W * dr
                da = da + g_rolled[lo:lo + TM] * _row_weight(cw_ref, dr, dc, i)
                lo = GRID_W + GRID_W * dr
                tap = 3 * (dr + 1) + dc + 1
                dw = jnp.sum(g_valid * a_rolled[lo:lo + TM], axis=0, keepdims=True)
                dcw_ref[0, tap:tap + 1, :] += dw if dr == 0 else jnp.where(i == 0, 0.0, dw)
        dav_ref[0] = da.astype(BF16)

    return pl.pallas_call(
        body, name="conv_bwd", grid=(N_FFK, nt),
        in_specs=[ANY] + _halo_specs(nt, lambda k, i: k, lambda k, i: i) + _halo_specs(nt, lambda k, i: k, lambda k, i: i)
        + [pl.BlockSpec((1, 9, FF_SLOT), lambda k, i: (k, 0, 0))],
        out_specs=[pl.BlockSpec((1, TM, FF_SLOT), lambda k, i: (k, i, 0)), pl.BlockSpec((1, 9, FF_SLOT), lambda k, i: (k, 0, 0)),
                   pl.BlockSpec((1, 1, FF_SLOT), lambda k, i: (k, 0, 0))],
        out_shape=[jax.ShapeDtypeStruct(dav.shape, BF16), jax.ShapeDtypeStruct((N_FFK, 9, FF_SLOT), F32),
                   jax.ShapeDtypeStruct((N_FFK, 1, FF_SLOT), F32)],
        input_output_aliases={0: 0}, compiler_params=_cp(2))(dav, dac, dac, dac, av, av, av, cw)


def _norm_mod_bwd(x_ref, nw_ref, mod_ref, k_shift, dh, dx_in, dx_ref, dnw_ref, dmod_ref, is_ctx):
    _, vjp = jax.vjp(_norm_mod, x_ref[...], nw_ref[...], mod_ref[0, k_shift], mod_ref[0, k_shift + 1])
    dx, dnw, dshift, dscale = vjp(dh)
    dx_ref[...] = dx_in + dx
    dnw_ref[...] += dnw
    _stream_add(dmod_ref, 0, is_ctx, dshift)
    _stream_add(dmod_ref, 1, is_ctx, dscale)


def ffn_up_bwd_x(dx2, x, dav, mod, nw, wg):
    t = x.shape[0]

    def body(dx2_ref, x_ref, dav_ref, mod_ref, nw_ref, w_ref, dx_ref, dnw_ref, dmod_ref):
        i = pl.program_id(0)

        @pl.when(i == 0)
        def _():
            dnw_ref[...] = jnp.zeros_like(dnw_ref)
            dmod_ref[...] = jnp.zeros_like(dmod_ref)

        dh = mm_nt(dav_ref[0], w_ref[0])
        for j in range(1, N_DEV):
            dh = dh + mm_nt(dav_ref[j], w_ref[j])
        _norm_mod_bwd(x_ref, nw_ref, mod_ref, 3, dh, dx2_ref[...], dx_ref, dnw_ref, dmod_ref, i == 0)

    tile = pl.BlockSpec((TM, D), lambda i: (i, 0))
    return pl.pallas_call(
        body, name="ffn_up_bwd_x", grid=(t // TM,),
        in_specs=[tile, tile, pl.BlockSpec((N_DEV, TM, FF_SLOT), lambda i: (0, i, 0)), pl.BlockSpec((1, 6, 1, D), _stream_row(TM)),
                  pl.BlockSpec((1, D), lambda i: (0, 0)), VMEM_WHOLE],
        out_specs=[tile, pl.BlockSpec((1, D), lambda i: (0, 0)), pl.BlockSpec((2, 2, 1, D), lambda i: (0, 0, 0, 0))],
        out_shape=[jax.ShapeDtypeStruct((t, D), F32), jax.ShapeDtypeStruct((1, D), F32), jax.ShapeDtypeStruct((2, 2, 1, D), F32)],
        compiler_params=_cp(1))(dx2, x, dav, mod, nw, wg)


def weight_grad(at, dout, slot, name, after=None):
    rows, t = at.shape
    stacked = dout.ndim == 3
    n = dout.shape[0] if stacked else dout.shape[1] // slot

    def body(a_ref, d_ref, *rest):
        dw_ref = rest[-1]
        dw_ref[0] = jnp.dot(a_ref[...], d_ref[0] if stacked else d_ref[...], preferred_element_type=F32).astype(dw_ref.dtype)

    d_spec = pl.BlockSpec((1, t, slot), lambda j: (j, 0, 0)) if stacked else pl.BlockSpec((t, slot), lambda j: (0, j))
    extra = [] if after is None else [jnp.reshape(after, (1, 1))]
    return pl.pallas_call(
        body, name=name, grid=(n,), in_specs=[VMEM_WHOLE, d_spec] + [ANY] * len(extra),
        out_specs=pl.BlockSpec((1, rows, slot), lambda j: (j, 0, 0)),
        out_shape=jax.ShapeDtypeStruct((n, rows, slot), GRAD_WIRE), compiler_params=_cp(1))(at, dout, *extra)


def weight_grad_rows(at, dout, name):
    n, t, rows = at.shape
    cols = dout.shape[1]

    def body(a_ref, d_ref, dw_ref):
        dw_ref[0] = _dot(a_ref[0], d_ref[...], ((0,), (0,))).astype(dw_ref.dtype)

    return pl.pallas_call(
        body, name=name, grid=(n,), in_specs=[pl.BlockSpec((1, t, rows), lambda k: (k, 0, 0)), VMEM_WHOLE],
        out_specs=pl.BlockSpec((1, rows, cols), lambda k: (k, 0, 0)),
        out_shape=jax.ShapeDtypeStruct((n, rows, cols), GRAD_WIRE), compiler_params=_cp(1))(at, dout)


def mixer_bwd(dx, parts, o, pa, pb, y, mod, lnw, lnb, sw, sb, hnw, wa, wb, wo):
    t = dx.shape[0]
    tm = TM
    n_ctx = CTX // tm

    def body(dx_ref, u_ref, v_ref, og_ref, ga_ref, gb_ref, o_ref, pa_ref, pb_ref, y_ref, mod_ref, lnw_ref, lnb_ref, sw_ref,
             sb_ref, hnw_ref, wa_ref, wb_ref, wo_ref, dp_ref, do_ref, dy_ref, dpa_ref, dpb_ref, dlnw_ref, dlnb_ref, dsw_ref,
             dsb_ref, dhnw_ref, dg_ref):
        i = pl.program_id(0)

        @pl.when(i == 0)
        def _():
            for r in (dlnw_ref, dlnb_ref, dsw_ref, dsb_ref, dhnw_ref, dg_ref):
                r[...] = jnp.zeros_like(r)

        _, _, vjps, vjp_b = _mixer_tile(slice(0, tm), u_ref, v_ref, og_ref, o_ref, lnw_ref, lnb_ref, sw_ref, sb_ref, hnw_ref)
        pa, pb = pa_ref[...].astype(F32), pb_ref[...].astype(F32)
        sa, sbg = jax.nn.sigmoid(ga_ref[...]), jax.nn.sigmoid(gb_ref[...])
        dxv = dx_ref[...]
        _stream_add(dg_ref, 0, i < n_ctx, jnp.sum(dxv * y_ref[...].astype(F32), axis=0, keepdims=True))
        dy = (mod_ref[0, 2] * dxv).astype(BF16)
        dy_ref[...] = dy
        dmerged = mm_nt(dy, wo_ref[...])
        dpa, dpb = (sa * dmerged).astype(BF16), (sbg * dmerged).astype(BF16)
        dpa_ref[...], dpb_ref[...] = dpa, dpb
        first = 4 * D
        dp_ref[:, first + 3 * D:first + 4 * D] = (dmerged * pa * sa * (1.0 - sa)).astype(BF16)
        dp_ref[:, first + 4 * D:first + 5 * D] = (dmerged * pb * sbg * (1.0 - sbg)).astype(BF16)
        dya = mm_nt(dpa, wa_ref[...])
        dob, dog, dhnw = vjp_b(mm_nt(dpb, wb_ref[...]))
        dp_ref[:, first + 2 * D:first + 3 * D] = dog.astype(BF16)
        dhnw_ref[...] += dhnw
        for g in range(HEADS):
            do_ref[:, _hsl(g)] = dob[g]
        for c, vjp_a in enumerate(vjps):
            rows = slice(c * SGU_CH, (c + 1) * SGU_CH)
            dub, dvb, dlnw, dlnb, dsw, dsb = vjp_a(dya[rows])
            for g in range(HEADS):
                dp_ref[rows, first + g * HD:first + (g + 1) * HD] = dub[g].astype(BF16)
                dp_ref[rows, first + D + g * HD:first + D + (g + 1) * HD] = dvb[g].astype(BF16)
                dlnw_ref[:, _hsl(g)] += dlnw[g]
                dlnb_ref[:, _hsl(g)] += dlnb[g]
                dsw_ref[g] += dsw[g]
                dsb_ref[g] += dsb[g]

    vec = lambda n: pl.BlockSpec((1, n), lambda i: (0, 0))
    tile = pl.BlockSpec((tm, D), lambda i: (i, 0))
    sds = jax.ShapeDtypeStruct
    return pl.pallas_call(
        body, name="mixer_bwd", grid=(t // tm,),
        in_specs=[tile] + _part_specs(tm, 4, 5)
        + [pl.BlockSpec((2, tm, D), lambda i: (0, i, 0)), tile, tile, tile, pl.BlockSpec((1, 6, 1, D), _stream_row(tm)),
           vec(D), vec(D), VMEM_WHOLE, VMEM_WHOLE, vec(HD), VMEM_WHOLE, VMEM_WHOLE, VMEM_WHOLE],
        out_specs=[pl.BlockSpec((tm, D_IN), lambda i: (i, 0)), tile, tile, tile, tile, vec(D), vec(D),
                   VMEM_WHOLE, VMEM_WHOLE, vec(HD), pl.BlockSpec((2, 1, 1, D), lambda i: (0, 0, 0, 0))],
        out_shape=[sds((t, D_IN), BF16), sds((t, D), F32), sds((t, D), BF16), sds((t, D), BF16), sds((t, D), BF16),
                   sds((1, D), F32), sds((1, D), F32), sds((HEADS, SGU_CH, SGU_CH), F32), sds((HEADS, SGU_CH, 1), F32),
                   sds((1, HD), F32), sds((2, 1, 1, D), F32)],
        compiler_params=_cp(1))(dx, parts, parts, parts, parts, parts, o, pa, pb, y, mod, lnw, lnb, sw, sb, hnw, wa, wb, wo)


def hgrn_bwd(d, parts, lb, mc, mtc, mrefc, ck, do, first=None, dparts=None):
    t = parts.shape[0]
    nc = t // CH
    chunk = _scan_chunk(nc)
    rev = lambda s: chunk(d, nc - 1 - s)

    def body(q_ref, f_ref, i_ref, lb_ref, m_ref, mt_ref, mr_ref, ck_ref, do_ref, *rest):
        dst = rest[-1]
        dlb_ref = rest[-2]

        @pl.when(pl.program_id(0) == 0)
        def _():
            dst[...] = jnp.zeros_like(dst)
            dlb_ref[...] = jnp.zeros_like(dlb_ref)

        heads = range(HEADS)
        fn = functools.partial(_hgrn_chunk, m=m_ref[0], mt=mt_ref[0], mref=mr_ref[0])
        _, vjp = jax.vjp(fn, [ck_ref[0, 0, h] for h in heads], [q_ref[:, _hsl(h)] for h in heads],
                         [f_ref[:, _hsl(h)] for h in heads], [i_ref[:, _hsl(h)] for h in heads],
                         [lb_ref[0, :, _hsl(h)] for h in heads])
        dstl, dq, df, di, dlb = vjp(([do_ref[:, _hsl(h)] for h in heads], [dst[h] for h in heads]))
        for h in heads:
            dst[h] = dstl[h]
            dlb_ref[0, :, _hsl(h)] += dlb[h]
            if d == 0:
                dq_ref, df_ref, di_ref = rest[:3]
                dq_ref[:, _hsl(h)] = dq[h].astype(BF16)
                df_ref[:, _hsl(h)] = df[h].astype(BF16)
                di_ref[:, _hsl(h)] = di[h].astype(BF16)
            else:
                dq0_ref, df0_ref, di0_ref, _, dp_ref = rest[:5]
                col = lambda k: slice(k * D + h * HD, k * D + (h + 1) * HD)
                dp_ref[:, col(0)] = (dq0_ref[:, _hsl(h)].astype(F32) + dq[h]).astype(BF16)
                dp_ref[:, col(1)] = df0_ref[:, _hsl(h)]
                dp_ref[:, col(2)] = df[h].astype(BF16)
                dp_ref[:, col(3)] = (di0_ref[:, _hsl(h)].astype(F32) + di[h]).astype(BF16)

    const = lambda s: (d, 0, 0)
    at = lambda k: pl.BlockSpec((CH, D), lambda s: (rev(s), k))
    in_specs = [at(0), at(1 + d), at(3), pl.BlockSpec((1, 1, D), const), pl.BlockSpec((1, CH, CH), const),
                pl.BlockSpec((1, CH, CH), const), pl.BlockSpec((1, CH, 1), const),
                pl.BlockSpec((1, 1, HEADS, HD, HD), lambda s: (d, nc - 1 - s, 0, 0, 0)), at(0)]
    dlb_spec, dlb_shape = pl.BlockSpec((1, 1, D), lambda s: (0, 0, 0)), jax.ShapeDtypeStruct((1, 1, D), F32)
    common = dict(grid=(nc,), scratch_shapes=[pltpu.VMEM((HEADS, HD, HD), F32)], compiler_params=_cp(1))
    if d == 0:
        return pl.pallas_call(body, name="hgrn_bwd_fwd_dir", in_specs=in_specs, out_specs=[at(0)] * 3 + [dlb_spec],
                              out_shape=[jax.ShapeDtypeStruct((t, D), BF16)] * 3 + [dlb_shape], **common,
                              )(parts, parts, parts, lb, mc, mtc, mrefc, ck, do)
    return pl.pallas_call(body, name="hgrn_bwd_bwd_dir", in_specs=in_specs + [at(0)] * 3 + [ANY],
                          out_specs=[pl.BlockSpec((CH, 4 * D), lambda s: (rev(s), 0)), dlb_spec],
                          out_shape=[jax.ShapeDtypeStruct(dparts.shape, BF16), dlb_shape], input_output_aliases={12: 0},
                          **common)(parts, parts, parts, lb, mc, mtc, mrefc, ck, do, *first, dparts)


def in_proj_bwd_x(dx1, x, dparts, mod, nw, wg, after=None):
    t = x.shape[0]
    tm = TM
    n_ctx = CTX // tm

    def body(dx1_ref, x_ref, dp_ref, mod_ref, nw_ref, w_ref, *rest):
        dx_ref, dnw_ref, dmod_ref = rest[-3:]
        i = pl.program_id(0)

        @pl.when(i == 0)
        def _():
            dnw_ref[...] = jnp.zeros_like(dnw_ref)
            dmod_ref[...] = jnp.zeros_like(dmod_ref)

        dh = mm_nt(dp_ref[:, 0:IN_SLOT], w_ref[0])
        for j in range(1, N_DEV):
            dh = dh + mm_nt(dp_ref[:, j * IN_SLOT:(j + 1) * IN_SLOT], w_ref[j])
        _norm_mod_bwd(x_ref, nw_ref, mod_ref, 0, dh, dx1_ref[...], dx_ref, dnw_ref, dmod_ref, i < n_ctx)

    tile = pl.BlockSpec((tm, D), lambda i: (i, 0))
    extra = [] if after is None else [jnp.reshape(after, (1, 1))]
    return pl.pallas_call(
        body, name="in_proj_bwd_x", grid=(t // tm,),
        in_specs=[tile, tile, pl.BlockSpec((tm, D_IN), lambda i: (i, 0)), pl.BlockSpec((1, 6, 1, D), _stream_row(tm)),
                  pl.BlockSpec((1, D), lambda i: (0, 0)), VMEM_WHOLE] + [ANY] * len(extra),
        out_specs=[tile, pl.BlockSpec((1, D), lambda i: (0, 0)), pl.BlockSpec((2, 2, 1, D), lambda i: (0, 0, 0, 0))],
        out_shape=[jax.ShapeDtypeStruct((t, D), F32), jax.ShapeDtypeStruct((1, D), F32), jax.ShapeDtypeStruct((2, 2, 1, D), F32)],
        compiler_params=_cp(1))(dx1, x, dparts, mod, nw, wg, *extra)


def _lb_fn(h0, h1):
    m = jnp.maximum(h0, h1)
    e0, e1 = jnp.exp(h0 - m), jnp.exp(h1 - m)
    return e1 / (e0 + e1)


def lower_bounds(hlb):
    def body(h_ref, out_ref):
        out_ref[...] = _lb_fn(h_ref[0:1, :], h_ref[1:2, :])
    return pl.pallas_call(body, name="lower_bounds", out_shape=jax.ShapeDtypeStruct((1, 2 * D), F32))(hlb)


def lower_bounds_bwd(hlb, dlb1):
    def body(h_ref, d_ref, out_ref):
        _, vjp = jax.vjp(_lb_fn, h_ref[0:1, :], h_ref[1:2, :])
        d0, d1 = vjp(d_ref[...])
        out_ref[0:1, :] = d0
        out_ref[1:2, :] = d1
    return pl.pallas_call(body, name="lower_bounds_bwd", out_shape=jax.ShapeDtypeStruct((2, 2 * D), F32))(hlb, dlb1)


def _ada_fn(c_all, cctx8, w, b):
    dot = lambda a, l: jnp.dot(_silu(a), w[l], precision=HIGHEST, preferred_element_type=F32) + b[l]
    return [dot(c_all, l) for l in range(2)], [dot(cctx8, l) for l in range(2)]


def ada_fwd(c_all, cctx8, w, b):
    cols = w.shape[-1]

    def body(c_ref, cc_ref, w_ref, b_ref, out_ref):
        ox, oc = _ada_fn(c_ref[...], cc_ref[...], [w_ref[0], w_ref[1]], [b_ref[0], b_ref[1]])
        for l in range(2):
            out_ref[l, 0] = ox[l]
            out_ref[l, 1] = oc[l]
    return pl.pallas_call(body, name="ada_fwd", out_shape=jax.ShapeDtypeStruct((2, 2, N_DEV, cols), F32),
                          compiler_params=_cp(0))(c_all, cctx8, w, b)


def ada_bwd(c_all, cctx8, w, b, dmx, dmc):
    cols = w.shape[-1]

    def body(c_ref, cc_ref, w_ref, b_ref, dmx_ref, dmc_ref, dw_ref, dc_ref):
        fn = lambda cc, w0, w1: _ada_fn(c_ref[...], cc, [w0, w1], [b_ref[0], b_ref[1]])
        _, vjp = jax.vjp(fn, cc_ref[...], w_ref[0], w_ref[1])
        dcc, dw0, dw1 = vjp(([dmx_ref[0], dmx_ref[1]], [dmc_ref[0], dmc_ref[1]]))
        dw_ref[0] = dw0
        dw_ref[1] = dw1
        dc_ref[...] = jnp.sum(dcc, axis=0, keepdims=True)
    return pl.pallas_call(body, name="ada_bwd", out_shape=[jax.ShapeDtypeStruct((2, D, cols), F32), jax.ShapeDtypeStruct((1, D), F32)],
                          compiler_params=_cp(0))(c_all, cctx8, w, b, dmx, dmc)


def adamw(w, m, v, gparts, name):
    r, c = w.shape
    p = gparts.shape[0]
    rt = r
    while rt % 16 == 0 and (p + 7) * rt * c * 4 * 2 > 24 * 2 ** 20:
        rt //= 2

    def body(w_ref, m_ref, v_ref, g_ref, go_ref, d_ref, mo_ref, vo_ref):
        g = g_ref[0].astype(F32)
        for k in range(1, p):
            g = g + g_ref[k].astype(F32)
        m2 = ADAM_B1 * m_ref[...] + (1.0 - ADAM_B1) * g
        v2 = ADAM_B2 * v_ref[...] + (1.0 - ADAM_B2) * (g * g)
        m_hat = m2 / (1.0 - ADAM_B1 ** ADAM_STEP)
        v_hat = v2 / (1.0 - ADAM_B2 ** ADAM_STEP)
        go_ref[...] = g
        d_ref[...] = -ADAM_LR * (m_hat / (jnp.sqrt(v_hat) + ADAM_EPS) + ADAM_WD * w_ref[...])
        mo_ref[...] = m2
        vo_ref[...] = v2

    tile = pl.BlockSpec((rt, c), lambda i: (i, 0))
    return pl.pallas_call(
        body, name=name, grid=(r // rt,),
        in_specs=[tile, tile, tile, pl.BlockSpec((p, rt, c), lambda i: (0, i, 0))], out_specs=[tile] * 4,
        out_shape=[jax.ShapeDtypeStruct((r, c), F32)] * 4, compiler_params=_cp(1))(w, m, v, gparts)


def _me():
    x, y, c = lax.axis_index("x"), lax.axis_index("y"), lax.axis_index("c")
    return x, y, c, 4 * x + 2 * y + c


def _peer(x, y, c, p):
    fx, fy, fc = (p >> 2) & 1, (p >> 1) & 1, p & 1
    return (1 - x if fx else x, 1 - y if fy else y, 1 - c if fc else c)


def all_gather(arrs, name, after=None):
    n = len(arrs)
    extra = [] if after is None else list(after) if isinstance(after, (list, tuple)) else [after]

    def body(*refs):
        ins, outs = refs[:n], refs[n + len(extra):2 * n + len(extra)]
        send, recv, local = refs[2 * n + len(extra):]
        x, y, c, me = _me()
        copies = []
        for a in range(n):
            lc = pltpu.make_async_copy(ins[a], outs[a].at[me], local.at[a])
            lc.start()
            copies.append(lc)
            for p in range(1, N_DEV):
                cp = pltpu.make_async_remote_copy(src_ref=ins[a], dst_ref=outs[a].at[me], send_sem=send.at[a, p - 1],
                                                  recv_sem=recv.at[a, p - 1], device_id=_peer(x, y, c, p),
                                                  device_id_type=pl.DeviceIdType.MESH)
                cp.start()
                copies.append(cp)
        for cp in copies:
            cp.wait()

    return pl.pallas_call(
        body, name=name, in_specs=[ANY] * (n + len(extra)), out_specs=[ANY] * n,
        out_shape=[jax.ShapeDtypeStruct((N_DEV,) + a.shape, a.dtype) for a in arrs],
        scratch_shapes=[pltpu.SemaphoreType.DMA((n, N_DEV - 1)), pltpu.SemaphoreType.DMA((n, N_DEV - 1)),
                        pltpu.SemaphoreType.DMA((n,))])(*arrs, *extra)


HBM = pl.BlockSpec(memory_space=pltpu.HBM)
SEM = pl.BlockSpec(memory_space=pltpu.SEMAPHORE)


def _in_hbm(a):
    return pltpu.with_memory_space_constraint(a, pltpu.HBM)


def _exchange_refs(srcs, lands, layer, scatter, a, x, y, c, p):
    me = 4 * x + 2 * y + c
    px, py, pc = _peer(x, y, c, p) if p else (x, y, c)
    dst = lands[a].at[me] if layer is None else lands[a].at[me, layer]
    src = srcs[a].at[4 * px + 2 * py + pc] if scatter else dst
    return src, dst, (px, py, pc)


def exchange_start(srcs, lands, layer, scatter, name, after=None):
    n, ns = len(lands), len(srcs)
    extra = [] if after is None else [after]

    def body(*refs):
        ins, lz = refs[:ns], refs[ns:ns + n]
        send, recv = refs[ns + n + len(extra)], refs[ns + n + len(extra) + 1]
        token = refs[-1]
        x, y, c, _ = _me()
        for a in range(n):
            for p in range(1, N_DEV):
                src, dst, peer = _exchange_refs(ins, lz, layer, scatter, a, x, y, c, p)
                k = a * (N_DEV - 1) + p - 1
                pltpu.make_async_remote_copy(src_ref=src, dst_ref=dst, send_sem=send.at[k], recv_sem=recv.at[k],
                                             device_id=peer, device_id_type=pl.DeviceIdType.MESH).start()
        token[...] = jnp.zeros_like(token)

    thru = [pltpu.HBM(a.shape, a.dtype) for a in list(srcs) + list(lands)]
    out = pl.pallas_call(
        body, name=name, in_specs=[HBM] * (ns + n) + [ANY] * len(extra),
        out_specs=[SEM, SEM] + [HBM] * (ns + n) + [pl.BlockSpec(memory_space=pltpu.VMEM)],
        out_shape=[pltpu.SemaphoreType.DMA((n * (N_DEV - 1),)), pltpu.SemaphoreType.DMA((n * (N_DEV - 1),))] + thru
        + [jax.ShapeDtypeStruct((8, 128), F32)],
        input_output_aliases={i: 2 + i for i in range(ns + n)},
        compiler_params=pltpu.CompilerParams(has_side_effects=pltpu.SideEffectType.DATAFLOW_SIDE_EFFECTING),
    )(*[_in_hbm(a) for a in list(srcs) + list(lands)], *extra)
    return out[0], out[1], out[2:2 + ns], out[2 + ns:2 + ns + n], out[-1]


def exchange_wait(send, recv, srcs, lands, layer, scatter, after, name):
    n, ns = len(lands), len(srcs)

    def body(*refs):
        ins, lz = refs[:ns], refs[ns:ns + n]
        send_ref, recv_ref = refs[ns + n], refs[ns + n + 1]
        x, y, c, _ = _me()
        for a in range(n):
            for p in range(1, N_DEV):
                src, dst, peer = _exchange_refs(ins, lz, layer, scatter, a, x, y, c, 0)
                k = a * (N_DEV - 1) + p - 1
                cp = pltpu.make_async_remote_copy(src_ref=src, dst_ref=dst, send_sem=send_ref.at[k],
                                                  recv_sem=recv_ref.at[k], device_id=peer,
                                                  device_id_type=pl.DeviceIdType.MESH)
                cp.wait_send()
                cp.wait_recv()

    thru = [pltpu.HBM(a.shape, a.dtype) for a in list(srcs) + list(lands)]
    out = pl.pallas_call(
        body, name=name, in_specs=[HBM] * (ns + n) + [SEM, SEM, ANY], out_specs=[HBM] * (ns + n), out_shape=thru,
        input_output_aliases={i: i for i in range(ns + n)},
        compiler_params=pltpu.CompilerParams(has_side_effects=pltpu.SideEffectType.DATAFLOW_SIDE_EFFECTING),
    )(*srcs, *lands, send, recv, after)
    return out[ns:]


def place_own(src, land, me, layer, scatter, name):
    create = isinstance(land, jax.ShapeDtypeStruct)
    r, c = src.shape[-2:]
    rt = r
    while rt % 32 == 0 and rt * c * 4 > 2 ** 21:
        rt //= 2

    def body(me_ref, src_ref, *rest):
        out_ref = rest[-1]
        out_ref[...] = src_ref[...].reshape(out_ref.shape).astype(out_ref.dtype)

    src_spec = (pl.BlockSpec((1, rt, c), lambda i, m: (m[0], i, 0)) if scatter else pl.BlockSpec((rt, c), lambda i, m: (i, 0)))
    out_spec = (pl.BlockSpec((1, rt, c), lambda i, m: (m[0], i, 0)) if layer is None
                else pl.BlockSpec((1, 1, rt, c), lambda i, m: (m[0], layer, i, 0)))
    grid_spec = pltpu.PrefetchScalarGridSpec(num_scalar_prefetch=1, grid=(r // rt,),
                                             in_specs=[src_spec] + ([] if create else [ANY]), out_specs=out_spec)
    return pl.pallas_call(body, name=name, grid_spec=grid_spec, out_shape=jax.ShapeDtypeStruct(land.shape, land.dtype),
                          input_output_aliases={} if create else {2: 0}, compiler_params=_cp(1),
                          )(*((me, src) if create else (me, src, land)))


def _scan_constants():
    r = lax.broadcasted_iota(jnp.int32, (CH, CH), 0)
    s = lax.broadcasted_iota(jnp.int32, (CH, CH), 1)
    lower = (s <= r).astype(F32)
    t = jnp.arange(CH)[:, None]
    mc = jnp.stack([lower, lower.T])
    mref = jnp.stack([(t <= CH // 2 - 1).astype(F32), (t >= CH // 2).astype(F32)])
    return mc, jnp.stack([lower.T, lower]), mref


def local_step(x, ctx, target, mod, lb, w, fetch=None, publish=None, small_ready=None, small_early=None):
    kept = {}

    def keep(l, part, grads):
        kept[(l, part)] = grads
        return 0.0

    fetch = fetch or (lambda l, part, after: w)
    publish = publish or keep
    n_layers = len(mod)
    mc, mtc, mrefc = _scan_constants()
    xs = jnp.concatenate([ctx, x], axis=0)
    saved, big = [], []
    for l in range(n_layers):
        wl = dict(fetch(l, "in", xs))
        parts, ht = in_proj_fwd(xs, mod[l], w["nw1"][l], wl["win"][l])
        o, ck = hgrn_fwd(parts, lb[l], mc, mtc, mrefc)
        wl.update(fetch(l, "rest", o))
        x1, pa, pb, ym, yat, ybt, mt = mixer_fwd(xs, parts, o, mod[l], w["lnw"][l], w["lnb"][l], w["sw"][l], w["sb"][l],
                                                 w["hnw"][l], wl["wa"][l], wl["wb"][l], wl["wo"][l])
        av, h2t = ffn_up_fwd(x1, mod[l], w["nw2"][l], wl["wup"][l])
        x2, ac, y, z = ffn_down_fwd(x1, av, mod[l], w["cw"][l], w["cb"][l], wl["wd"][l])
        saved.append((xs, parts, o, ck, x1, av, ac, y, z, ht, h2t, pa, pb, ym, yat, ybt, mt))
        big.append(wl)
        xs = x2
    loss, dx, dfw = loss_fwd_bwd(xs, target, w["fw"])
    g = {k: [None] * n_layers for k in ("nw1", "nw2", "lnw", "lnb", "sw", "sb", "hnw", "cw", "cb")}
    g["fw"] = dfw
    dmod, dlb = [None] * n_layers, [None] * n_layers
    tok = 0.0
    for l in reversed(range(n_layers)):
        x0, parts, o, ck, x1, av, ac, y, z, ht, h2t, pa, pb, ym, yat, ybt, mt = saved[l]
        wl = big[l]
        dav, dac, dout, dg2 = ffn_down_bwd(dx, ac, av, y, mod[l] + tok, wl["wd"][l])
        dwd = weight_grad_rows(z, dout, "ffn_down_bwd_w")
        dav, g["cw"][l], g["cb"][l] = conv_bwd(dav, dac, av, w["cw"][l])
        dx1, g["nw2"][l], dmod2 = ffn_up_bwd_x(dx, x1, dav, mod[l], w["nw2"][l], wl["wup"][l])
        dwup = weight_grad(h2t, dav, FF_SLOT, "ffn_up_bwd_w")
        tok = publish(l, "ffn", {"wd": dwd, "wup": dwup})
        (dparts, do, dy, dpa, dpb, g["lnw"][l], g["lnb"][l], g["sw"][l], g["sb"][l], g["hnw"][l],
         dg1) = mixer_bwd(dx1, parts, o, pa, pb, ym, mod[l] + tok, w["lnw"][l], w["lnb"][l], w["sw"][l], w["sb"][l],
                          w["hnw"][l], wl["wa"][l], wl["wb"][l], wl["wo"][l])
        tok = publish(l, "mix", {"wa": weight_grad(yat, dpa, D, "mixer_bwd_wa"), "wb": weight_grad(ybt, dpb, D, "mixer_bwd_wb"),
                                 "wo": weight_grad(mt, dy, D, "mixer_bwd_wo")})
        if l == 0 and small_early:
            dmod[0] = jnp.concatenate([jnp.zeros((2, 2, 1, D), F32), dg1, dmod2, dg2], axis=1)
            tok = tok + small_early(loss[0, 0], g, dmod, dlb)
        dq, df, di, dlb_f = hgrn_bwd(0, parts, lb[l] + tok, mc, mtc, mrefc, ck, do)
        dparts, dlb_b = hgrn_bwd(1, parts, lb[l], mc, mtc, mrefc, ck, do, (dq, df, di), dparts)
        dlb[l] = jnp.concatenate([dlb_f, dlb_b], axis=0)
        tok = publish(l, "in", {"win": weight_grad(ht, dparts, IN_SLOT, "in_proj_bwd_w")})
        dx, g["nw1"][l], dmod1 = in_proj_bwd_x(dx1, x0, dparts, mod[l], w["nw1"][l], wl["win"][l], after=tok)
        dmod[l] = jnp.concatenate([dmod1, dg1, dmod2, dg2], axis=1)
    done = small_ready(loss[0, 0], g, dmod, dlb) if small_ready else 0.0
    for (l, part), grads in kept.items():
        for k, v in grads.items():
            g.setdefault(k, [None] * n_layers)[l] = v
    return loss[0, 0], dx[CTX:], g, dmod, dlb, done


ROW = 1024
REPLICATED = ("norm1_w", "sgu_ln_w", "sgu_ln_b", "sgu_w", "sgu_b", "hgrn_lower_bounds", "hgrn_norm_w", "norm2_w",
              "ffn_conv_b", "final_norm_w")
WEIGHT_ORDER = ("c_ctx", "ada_w", "ada_b", "norm1_w", "w_in", "sgu_ln_w", "sgu_ln_b", "sgu_w", "sgu_b", "hgrn_lower_bounds",
                "hgrn_norm_w", "w_branch_a", "w_branch_b", "w_out", "norm2_w", "ffn_w_up", "ffn_conv_w", "ffn_conv_b",
                "ffn_w_down", "final_norm_w")


def _rows_of(n):
    return -(-n // (8 * ROW)) * 8


def _pack(arrs, total_rows=None):
    parts = []
    for a in arrs:
        flat = a.reshape(-1).astype(F32)
        rows = _rows_of(flat.shape[0])
        parts.append(jnp.pad(flat, (0, rows * ROW - flat.shape[0])).reshape(rows, ROW))
    have = sum(p.shape[0] for p in parts)
    if total_rows is not None and total_rows > have:
        parts.append(jnp.zeros((total_rows - have, ROW), F32))
    return jnp.concatenate(parts, axis=0)


def _unpack(packed, shapes):
    lead = packed.shape[:-2]
    out, r0 = [], 0
    for s in shapes:
        n = math.prod(s)
        rows = _rows_of(n)
        out.append(packed[..., r0:r0 + rows, :].reshape(lead + (rows * ROW,))[..., :n].reshape(lead + tuple(s)))
        r0 += rows
    return out


def kernel(x, c, ctx, c_ctx, ada_w, ada_b, norm1_w, w_in, sgu_ln_w, sgu_ln_b, sgu_w, sgu_b, hgrn_lower_bounds, hgrn_norm_w, w_branch_a, w_branch_b, w_out, norm2_w, ffn_w_up, ffn_conv_w, ffn_conv_b, ffn_w_down, final_norm_w, loss_target, m_c_ctx, m_ada_w, m_ada_b, m_norm1_w, m_w_in, m_sgu_ln_w, m_sgu_ln_b, m_sgu_w, m_sgu_b, m_hgrn_lower_bounds, m_hgrn_norm_w, m_w_branch_a, m_w_branch_b, m_w_out, m_norm2_w, m_ffn_w_up, m_ffn_conv_w, m_ffn_conv_b, m_ffn_w_down, m_final_norm_w, v_c_ctx, v_ada_w, v_ada_b, v_norm1_w, v_w_in, v_sgu_ln_w, v_sgu_ln_b, v_sgu_w, v_sgu_b, v_hgrn_lower_bounds, v_hgrn_norm_w, v_w_branch_a, v_w_branch_b, v_w_out, v_norm2_w, v_ffn_w_up, v_ffn_conv_w, v_ffn_conv_b, v_ffn_w_down, v_final_norm_w):
    wts = dict(c_ctx=c_ctx, ada_w=ada_w, ada_b=ada_b, norm1_w=norm1_w, w_in=w_in, sgu_ln_w=sgu_ln_w, sgu_ln_b=sgu_ln_b,
               sgu_w=sgu_w, sgu_b=sgu_b, hgrn_lower_bounds=hgrn_lower_bounds, hgrn_norm_w=hgrn_norm_w, w_branch_a=w_branch_a,
               w_branch_b=w_branch_b, w_out=w_out, norm2_w=norm2_w, ffn_w_up=ffn_w_up, ffn_conv_w=ffn_conv_w,
               ffn_conv_b=ffn_conv_b, ffn_w_down=ffn_w_down, final_norm_w=final_norm_w)
    mom1 = dict(c_ctx=m_c_ctx, ada_w=m_ada_w, ada_b=m_ada_b, norm1_w=m_norm1_w, w_in=m_w_in, sgu_ln_w=m_sgu_ln_w,
                sgu_ln_b=m_sgu_ln_b, sgu_w=m_sgu_w, sgu_b=m_sgu_b, hgrn_lower_bounds=m_hgrn_lower_bounds,
                hgrn_norm_w=m_hgrn_norm_w, w_branch_a=m_w_branch_a, w_branch_b=m_w_branch_b, w_out=m_w_out, norm2_w=m_norm2_w,
                ffn_w_up=m_ffn_w_up, ffn_conv_w=m_ffn_conv_w, ffn_conv_b=m_ffn_conv_b, ffn_w_down=m_ffn_w_down,
                final_norm_w=m_final_norm_w)
    mom2 = dict(c_ctx=v_c_ctx, ada_w=v_ada_w, ada_b=v_ada_b, norm1_w=v_norm1_w, w_in=v_w_in, sgu_ln_w=v_sgu_ln_w,
                sgu_ln_b=v_sgu_ln_b, sgu_w=v_sgu_w, sgu_b=v_sgu_b, hgrn_lower_bounds=v_hgrn_lower_bounds,
                hgrn_norm_w=v_hgrn_norm_w, w_branch_a=v_w_branch_a, w_branch_b=v_w_branch_b, w_out=v_w_out, norm2_w=v_norm2_w,
                ffn_w_up=v_ffn_w_up, ffn_conv_w=v_ffn_conv_w, ffn_conv_b=v_ffn_conv_b, ffn_w_down=v_ffn_w_down,
                final_norm_w=v_final_norm_w)
    n_layers = w_in.shape[0]
    layers = range(n_layers)
    me = 4 * lax.axis_index("x") + 2 * lax.axis_index("y") + lax.axis_index("c")
    ada_cols = ada_w.shape[-1]

    big = ("w_in", "ffn_w_up", "w_branch_a", "w_branch_b", "w_out", "ffn_w_down")
    short = {"w_in": "win", "ffn_w_up": "wup", "w_branch_a": "wa", "w_branch_b": "wb", "w_out": "wo", "ffn_w_down": "wd"}
    me1 = me.reshape(1).astype(jnp.int32)
    groups = [[("w_in", 0)], [(k, 0) for k in big[1:]], [("w_in", 1)], [(k, 1) for k in big[1:]]]
    in_flight, started = [], 0.0

    def own_slots(n):
        return [place_own(wts[k][l], jax.ShapeDtypeStruct((N_DEV,) + wts[k].shape[1:], BF16), me1, None, False,
                          f"gather_own_{short[k]}_{l}") for k, l in groups[n]]

    def start_group(n, lands, after):
        in_flight.append(exchange_start([], lands, None, False, f"gather_weights_start_{n}", after=after))
        return in_flight[-1][-1]

    (c_all,) = all_gather([c], "gather_c")
    c_all = c_all.reshape(N_DEV, D)
    token = start_group(0, own_slots(0), c_all)
    later = [own_slots(n) for n in range(1, len(groups))]
    cctx8 = jnp.broadcast_to(c_ctx[None, :], (N_DEV, D))
    ada_b_cols = lax.dynamic_slice_in_dim(ada_b, me * ada_cols, ada_cols, axis=1)[:, None, :]
    mod_cols = ada_fwd(c_all, cctx8, ada_w, ada_b_cols)
    mod_all, conv_all = all_gather([mod_cols, ffn_conv_w.reshape(n_layers, 9, -1)], "gather_mod_conv",
                                   after=[token] + [lands[-1] for lands in later])
    conv_full = [conv_all[:, l].transpose(1, 0, 2).reshape(9, N_FFK, FF_SLOT).transpose(1, 0, 2) for l in layers]
    for n in range(1, len(groups)):
        token = start_group(n, later[n - 1], mod_all if n == 1 else token)
    for started_group in in_flight:
        started = started + started_group[-1][0, 0]

    def as_used(k, a):
        return a if k in ("w_in", "ffn_w_up") else a.reshape(N_FFK, FF_SLOT, D) if k == "ffn_w_down" else a.reshape(D, D)

    arrived = {}

    def fetch(l, part, after):
        n = {(0, "in"): 0, (0, "rest"): 1, (1, "in"): 2, (1, "rest"): 3}.get((l, part))
        if n is not None:
            send, recv, _, lands, _ = in_flight[n]
            got = exchange_wait(send, recv, [], lands, None, False, after, f"gather_weights_wait_{n}")
            for (k, ll), a in zip(groups[n], got):
                arrived.setdefault(short[k], [None] * n_layers)[ll] = as_used(k, a)
        return arrived

    mod_x = lax.dynamic_index_in_dim(mod_all[:, :, 0], me, axis=2, keepdims=False)
    mod_c = mod_all[:, :, 1, 0]
    mod = [jnp.stack([mod_c[:, l].reshape(6, 1, D), mod_x[:, l].reshape(6, 1, D)]) for l in layers]
    mod[0] = mod[0] + started

    lb1 = lower_bounds(hgrn_lower_bounds)
    lb = [jnp.zeros((2, 1, D), F32), lb1.reshape(2, 1, D)]

    w = {
        "nw1": [norm1_w[l][None] for l in layers], "nw2": [norm2_w[l][None] for l in layers],
        "lnw": [sgu_ln_w[l][None] for l in layers], "lnb": [sgu_ln_b[l][None] for l in layers],
        "sw": [sgu_w[l] for l in layers], "sb": [sgu_b[l][:, :, None] for l in layers],
        "hnw": [hgrn_norm_w[l][None] for l in layers], "cw": conv_full,
        "cb": [ffn_conv_b[l].reshape(N_FFK, 1, FF_SLOT) for l in layers], "fw": final_norm_w[None],
    }
    long = {v: k for k, v in short.items()}
    landing, sent = {}, []

    def publish(l, part, grads):
        keys = [long[k] for k in grads]
        slots = [a.reshape((N_DEV, -1, a.shape[-1])) for a in grads.values()]
        zones = [place_own(s, landing.get(k, jax.ShapeDtypeStruct((N_DEV, n_layers) + s.shape[1:], s.dtype)), me1, l, True,
                           f"scatter_own_{short[k]}_{l}") for k, s in zip(keys, slots)]
        send, recv, srcs, zones, token = exchange_start(slots, zones, l, True, f"scatter_grads_start_{part}_{l}")
        landing.update(zip(keys, zones))
        sent.append((keys, l, part, send, recv, srcs, token))
        return token[0, 0]

    out = {}
    flat2 = lambda a: a.reshape(-1, a.shape[-1])

    def finish(part, after):
        done = []
        for keys, l, p, send, recv, srcs, _ in sent:
            if p == part:
                zones = exchange_wait(send, recv, srcs, [landing[k] for k in keys], l, True, after,
                                      f"scatter_grads_wait_{part}_{l}")
                landing.update(zip(keys, zones))
                done = keys
        for k in done:
            r = landing[k]
            res = adamw(flat2(wts[k]), flat2(mom1[k]), flat2(mom2[k]), r.reshape(N_DEV, -1, r.shape[-1]), "adamw_" + k)
            out[k] = tuple(a.reshape(wts[k].shape) for a in res)

    rep_rows = -(-sum(_rows_of(wts[k].size) for k in REPLICATED) // 64) * 64
    conv_rows = _rows_of(n_layers * 9 * D_FF)
    dmod_rows = _rows_of(n_layers * 6 * D)
    early = {}

    def small_early(loss_part, g, dmod, dlb):
        d_hlb = lower_bounds_bwd(hgrn_lower_bounds, dlb[1].reshape(1, 2 * D))
        st = lambda k: jnp.stack([jnp.zeros((1, D), F32) if a is None else a for a in g[k]])
        rep_grads = {"norm1_w": st("nw1"), "sgu_ln_w": st("lnw"), "sgu_ln_b": st("lnb"), "sgu_w": st("sw"), "sgu_b": st("sb"),
                     "hgrn_lower_bounds": d_hlb, "hgrn_norm_w": st("hnw"), "norm2_w": st("nw2"), "ffn_conv_b": st("cb"),
                     "final_norm_w": g["fw"]}
        d_conv = jnp.stack([g["cw"][l].transpose(1, 0, 2).reshape(9, D_FF) for l in layers])
        dmod_x = jnp.stack([dmod[l][1].reshape(6 * D) for l in layers])
        dmod_c = jnp.stack([dmod[l][0].reshape(6 * D) for l in layers])
        small = jnp.concatenate([_pack([rep_grads[k] for k in REPLICATED], rep_rows),
                                 _pack([d_conv, dmod_x, dmod_c, loss_part.reshape(1)])], axis=0)
        zone = place_own(small, jax.ShapeDtypeStruct((N_DEV,) + small.shape, F32), me1, None, False, "gather_small_own")
        early["send"], early["recv"], _, early["zones"], token = exchange_start([], [zone], None, False, "gather_small_start")
        return token[0, 0]

    def small_ready(loss_part, g, dmod, dlb):
        late = _pack([g["nw1"][0], dmod[0][1, 0:2], dmod[0][0, 0:2]])
        for part in ("ffn", "mix"):
            finish(part, late)
        (late_all,) = all_gather([late], "gather_small_late", after=out["w_out"][0])
        (small_all,) = exchange_wait(early["send"], early["recv"], [], early["zones"], None, False, late_all,
                                     "gather_small_wait")
        at_x = rep_rows + conv_rows
        small_all = small_all.at[:, 0:1].set(late_all[:, 0:1])
        small_all = small_all.at[:, at_x:at_x + 2].set(late_all[:, 8:10])
        small_all = small_all.at[:, at_x + dmod_rows:at_x + dmod_rows + 2].set(late_all[:, 16:18])
        d_conv_shape, dmod_shape = (n_layers, 9, D_FF), (n_layers, 6 * D)
        conv_g, dmx_all, dmc_all, loss_all = _unpack(small_all[:, rep_rows:], [d_conv_shape, dmod_shape, dmod_shape, (1,)])
        out["loss"] = functools.reduce(lambda a, b: a + b, [loss_all[k, 0] for k in range(N_DEV)])

        rep = adamw(_pack([wts[k] for k in REPLICATED], rep_rows), _pack([mom1[k] for k in REPLICATED], rep_rows),
                    _pack([mom2[k] for k in REPLICATED], rep_rows), small_all[:, :rep_rows], "adamw_replicated")
        rep = [_unpack(r, [wts[k].shape for k in REPLICATED]) for r in rep]
        for n, k in enumerate(REPLICATED):
            out[k] = tuple(r[n] for r in rep)

        conv_mine = lax.dynamic_index_in_dim(conv_g.reshape(N_DEV, n_layers, 9, N_DEV, -1), me, axis=3, keepdims=False)
        res = adamw(flat2(ffn_conv_w), flat2(m_ffn_conv_w), flat2(v_ffn_conv_w),
                    conv_mine.reshape(N_DEV, -1, conv_mine.shape[-1]), "adamw_conv_w")
        out["ffn_conv_w"] = tuple(r.reshape(ffn_conv_w.shape) for r in res)

        out["ada_b"] = tuple(adamw(ada_b, m_ada_b, v_ada_b, jnp.concatenate([dmx_all, dmc_all], axis=0), "adamw_ada_b"))

        cols_of = lambda a: lax.dynamic_slice_in_dim(a, me * ada_cols, ada_cols, axis=2).transpose(1, 0, 2)
        d_ada_w, d_cctx = ada_bwd(c_all, cctx8, ada_w, ada_b_cols, cols_of(dmx_all), cols_of(dmc_all))
        res = adamw(flat2(ada_w), flat2(m_ada_w), flat2(v_ada_w), flat2(d_ada_w)[None], "adamw_ada_w")
        out["ada_w"] = tuple(r.reshape(ada_w.shape) for r in res)
        (d_cctx_all,) = all_gather([d_cctx], "gather_c_ctx_grad")
        res = adamw(c_ctx[None], m_c_ctx[None], v_c_ctx[None], d_cctx_all, "adamw_c_ctx")
        out["c_ctx"] = tuple(r[0] for r in res)
        return d_cctx_all

    _, grad_x, _, _, _, small_done = local_step(x[0], ctx[0], loss_target[0], mod, lb, w, fetch, publish, small_ready,
                                                small_early)
    loss = out["loss"]

    finish("in", small_done)
    return (loss, grad_x[None]) + tuple(out[k][n] for n in range(4) for k in WEIGHT_ORDER)
```

```python
import functools
import math

import jax
import jax.numpy as jnp
from jax import lax
from jax.experimental import pallas as pl
from jax.experimental.pallas import tpu as pltpu

F32 = jnp.float32
BF16 = jnp.bfloat16
HIGHEST = lax.Precision.HIGHEST

N_DEV = 8
AXES = ("x", "y", "c")
D = 1024
CTX = 256
TM = 256
CH = 64
SGU_CH = 128
HEADS = 8
HD = 128
GRID_W = 64
D_IN = 9 * D
IN_SLOT = D_IN // N_DEV
D_FF = 2816
FF_SLOT = 2 * D_FF // N_DEV
N_FFK = D_FF // FF_SLOT
RMS_EPS = 1e-6
LN_EPS = 1e-5
ADAM_LR, ADAM_B1, ADAM_B2, ADAM_EPS, ADAM_WD, ADAM_STEP = 0.001, 0.9, 0.999, 1e-08, 0.01, 10
VMEM_LIMIT_V7X = 56 * 2 ** 20
GRAD_WIRE = jnp.bfloat16

VMEM_WHOLE = pl.BlockSpec(memory_space=pltpu.VMEM)
ANY = pl.BlockSpec(memory_space=pl.ANY)


def _cp(n_axes):
    return pltpu.CompilerParams(dimension_semantics=("arbitrary",) * n_axes, vmem_limit_bytes=VMEM_LIMIT_V7X)


def _dot(a, b, dims):
    return lax.dot_general(a.astype(BF16), b.astype(BF16), (dims, ((), ())), preferred_element_type=F32)


@jax.custom_vjp
def mm(a, b):
    return _dot(a, b, ((1,), (0,)))


mm.defvjp(lambda a, b: (mm(a, b), (a, b)),
          lambda r, g: (_dot(g, r[1], ((1,), (1,))).astype(r[0].dtype), _dot(r[0], g, ((0,), (0,))).astype(r[1].dtype)))


@jax.custom_vjp
def mm_nt(a, b):
    return _dot(a, b, ((1,), (1,)))


mm_nt.defvjp(lambda a, b: (mm_nt(a, b), (a, b)),
             lambda r, g: (_dot(g, r[1], ((1,), (0,))).astype(r[0].dtype), _dot(g, r[0], ((0,), (0,))).astype(r[1].dtype)))


@jax.custom_vjp
def mm_tn(a, b):
    return _dot(a, b, ((0,), (0,)))


mm_tn.defvjp(lambda a, b: (mm_tn(a, b), (a, b)),
             lambda r, g: (_dot(r[1], g, ((1,), (1,))).astype(r[0].dtype), _dot(r[0], g, ((1,), (0,))).astype(r[1].dtype)))


def _tri_dot(m, g):
    hi = g.astype(BF16)
    rest = g - hi.astype(F32)
    mid = rest.astype(BF16)
    low = (rest - mid.astype(F32)).astype(BF16)
    n = g.shape[1]
    out = jnp.dot(m.astype(BF16), jnp.concatenate([hi, mid, low], axis=1), preferred_element_type=F32)
    return out[:, :n] + out[:, n:2 * n] + out[:, 2 * n:]


@jax.custom_vjp
def _cum(m, mt, g):
    return _tri_dot(m, g)


_cum.defvjp(lambda m, mt, g: (_cum(m, mt, g), (m, mt)),
            lambda r, d: (jnp.zeros_like(r[0]), jnp.zeros_like(r[1]), _tri_dot(r[1], d)))


def _silu(x):
    return x * jax.nn.sigmoid(x)


def _gelu(x):
    return 0.5 * x * (1.0 + jnp.tanh(math.sqrt(2.0 / math.pi) * (x + 0.044715 * (x * x * x))))


def _rms(x, w):
    return x * lax.rsqrt(jnp.mean(x * x, axis=-1, keepdims=True) + RMS_EPS) * w


def _norm_mod(x, w, shift, scale):
    return _rms(x, w) * (1.0 + scale) + shift


def _hsl(h):
    return slice(h * HD, (h + 1) * HD)


def _hgrn_chunk(st, qz, fz, iv, lb, m, mt, mref):
    hs = range(HEADS)
    keep = [1.0 - lb[h] for h in hs]
    g = [jnp.log(lb[h] + keep[h] * jax.nn.sigmoid(fz[h])) for h in hs]
    k = [keep[h] * jax.nn.sigmoid(-fz[h]) for h in hs]
    q = [_silu(qz[h]) for h in hs]
    b = [_cum(m, mt, g[h]) for h in hs]
    ref = [jnp.sum(mref * g[h], axis=0, keepdims=True) for h in hs]
    last = [jnp.sum(g[h], axis=0, keepdims=True) for h in hs]
    qa = [q[h] * jnp.exp(b[h] - ref[h]) for h in hs]
    ka = [k[h] * jnp.exp(ref[h] - b[h]) for h in hs]
    scores = [jnp.where(m > 0.5, mm_nt(qa[h], ka[h]), 0.0) for h in hs]
    inter = [mm_nt(qa[h] * jnp.exp(ref[h]), st[h]) for h in hs]
    kv = [mm_tn(iv[h], ka[h] * jnp.exp(last[h] - ref[h])) for h in hs]
    outs = [mm(scores[h], iv[h]) + inter[h] for h in hs]
    news = [jnp.exp(last[h]) * st[h] + kv[h] for h in hs]
    return outs, news


def _sgu_fn(ub, vb, lnw, lnb, sw, sb):
    gv = [_gelu(v) for v in vb]
    mu = sum(jnp.sum(t, axis=-1, keepdims=True) for t in gv) / D
    var = sum(jnp.sum((t - mu) * (t - mu), axis=-1, keepdims=True) for t in gv) / D
    inv = lax.rsqrt(var + LN_EPS)
    cols = []
    for g in range(HEADS):
        vn = (gv[g] - mu) * inv * lnw[g] + lnb[g]
        cols.append(_gelu(ub[g]) * (mm(sw[g], vn) + sb[g]))
    return jnp.concatenate(cols, axis=1)


def _readout_fn(ob, og, hnw):
    r = [o * lax.rsqrt(jnp.mean(o * o, axis=-1, keepdims=True) + RMS_EPS) * hnw for o in ob]
    return jnp.concatenate(r, axis=1) * _silu(og)


def _glu_fn(ac, v):
    return _gelu(ac) * v


def _stream_row(tm):
    n_ctx = CTX // tm
    return lambda i: (jnp.where(i < n_ctx, 0, 1), 0, 0, 0)


def in_proj_fwd(x, mod, nw, wg):
    t = x.shape[0]

    def body(x_ref, mod_ref, nw_ref, w_ref, out_ref, ht_ref):
        h32 = _norm_mod(x_ref[...], nw_ref[...], mod_ref[0, 0], mod_ref[0, 1])
        ht_ref[...] = h32.T.astype(BF16)
        h = h32.astype(BF16)
        for j in range(N_DEV):
            out_ref[:, j * IN_SLOT:(j + 1) * IN_SLOT] = jnp.dot(h, w_ref[j], preferred_element_type=F32)

    return pl.pallas_call(
        body, name="in_proj_fwd", grid=(t // TM,),
        in_specs=[pl.BlockSpec((TM, D), lambda i: (i, 0)), pl.BlockSpec((1, 6, 1, D), _stream_row(TM)),
                  pl.BlockSpec((1, D), lambda i: (0, 0)), VMEM_WHOLE],
        out_specs=[pl.BlockSpec((TM, D_IN), lambda i: (i, 0)), pl.BlockSpec((D, TM), lambda i: (0, i))],
        out_shape=[jax.ShapeDtypeStruct((t, D_IN), F32), jax.ShapeDtypeStruct((D, t), BF16)],
        compiler_params=_cp(1))(x, mod, nw, wg)


def _scan_chunk(nc):
    ncc = CTX // CH

    def chunk(d, s):
        bwd = jnp.where(s < ncc, ncc - 1 - s, nc + ncc - 1 - s)
        return jnp.where(d == 0, s, bwd)
    return chunk


def hgrn_fwd(parts, lb, mc, mtc, mrefc):
    t = parts.shape[0]
    nc = t // CH
    chunk = _scan_chunk(nc)

    def body(q_ref, f_ref, i_ref, lb_ref, m_ref, mt_ref, mr_ref, o_ref, ck_ref, st):
        @pl.when(pl.program_id(1) == 0)
        def _():
            st[...] = jnp.zeros_like(st)
        ck_ref[0, 0] = st[...]
        outs, news = _hgrn_chunk([st[h] for h in range(HEADS)], [q_ref[:, _hsl(h)] for h in range(HEADS)],
                                 [f_ref[:, _hsl(h)] for h in range(HEADS)], [i_ref[:, _hsl(h)] for h in range(HEADS)],
                                 [lb_ref[0, :, _hsl(h)] for h in range(HEADS)], m_ref[0], mt_ref[0], mr_ref[0])
        for h in range(HEADS):
            o_ref[0, :, _hsl(h)] = outs[h]
            st[h] = news[h]

    const = lambda d, s: (d, 0, 0)
    return pl.pallas_call(
        body, name="hgrn_fwd", grid=(2, nc),
        in_specs=[pl.BlockSpec((CH, D), lambda d, s: (chunk(d, s), 0)), pl.BlockSpec((CH, D), lambda d, s: (chunk(d, s), 1 + d)),
                  pl.BlockSpec((CH, D), lambda d, s: (chunk(d, s), 3)), pl.BlockSpec((1, 1, D), const),
                  pl.BlockSpec((1, CH, CH), const), pl.BlockSpec((1, CH, CH), const), pl.BlockSpec((1, CH, 1), const)],
        out_specs=[pl.BlockSpec((1, CH, D), lambda d, s: (d, chunk(d, s), 0)),
                   pl.BlockSpec((1, 1, HEADS, HD, HD), lambda d, s: (d, s, 0, 0, 0))],
        out_shape=[jax.ShapeDtypeStruct((2, t, D), F32), jax.ShapeDtypeStruct((2, nc, HEADS, HD, HD), F32)],
        scratch_shapes=[pltpu.VMEM((HEADS, HD, HD), F32)], compiler_params=_cp(2))(parts, parts, parts, lb, mc, mtc, mrefc)


def _mixer_tile(rows, u_ref, v_ref, og_ref, o_ref, lnw_ref, lnb_ref, sw_ref, sb_ref, hnw_ref):
    n = (rows.stop - rows.start) // SGU_CH
    yas, vjps = [], []
    for c in range(n):
        r = slice(rows.start + c * SGU_CH, rows.start + (c + 1) * SGU_CH)
        ya, vjp_a = jax.vjp(_sgu_fn, [u_ref[r, _hsl(g)] for g in range(HEADS)], [v_ref[r, _hsl(g)] for g in range(HEADS)],
                            [lnw_ref[:, _hsl(g)] for g in range(HEADS)], [lnb_ref[:, _hsl(g)] for g in range(HEADS)],
                            [sw_ref[g] for g in range(HEADS)], [sb_ref[g] for g in range(HEADS)])
        yas.append(ya)
        vjps.append(vjp_a)
    yb, vjp_b = jax.vjp(_readout_fn, [o_ref[0, rows, _hsl(h)] + o_ref[1, rows, _hsl(h)] for h in range(HEADS)],
                        og_ref[rows, :], hnw_ref[...])
    return (yas[0] if n == 1 else jnp.concatenate(yas, axis=0)), yb, vjps, vjp_b


def _part_specs(tm, first, n):
    return [pl.BlockSpec((tm, D), functools.partial(lambda k, i: (i, k), first + k)) for k in range(n)]


def mixer_fwd(x, parts, o, mod, lnw, lnb, sw, sb, hnw, wa, wb, wo):
    t = x.shape[0]

    def body(x_ref, u_ref, v_ref, og_ref, ga_ref, gb_ref, o_ref, mod_ref, lnw_ref, lnb_ref, sw_ref, sb_ref, hnw_ref,
             wa_ref, wb_ref, wo_ref, out_ref, pa_ref, pb_ref, y_ref, yat_ref, ybt_ref, mt_ref):
        ya, yb, _, _ = _mixer_tile(slice(0, TM), u_ref, v_ref, og_ref, o_ref, lnw_ref, lnb_ref, sw_ref, sb_ref, hnw_ref)
        pa, pb = mm(ya, wa_ref[...]), mm(yb, wb_ref[...])
        merged = jax.nn.sigmoid(ga_ref[...]) * pa + jax.nn.sigmoid(gb_ref[...]) * pb
        y = mm(merged, wo_ref[...])
        out_ref[...] = x_ref[...] + mod_ref[0, 2] * y
        pa_ref[...], pb_ref[...], y_ref[...] = pa.astype(BF16), pb.astype(BF16), y.astype(BF16)
        yat_ref[...], ybt_ref[...], mt_ref[...] = ya.T.astype(BF16), yb.T.astype(BF16), merged.T.astype(BF16)

    vec = lambda n: pl.BlockSpec((1, n), lambda i: (0, 0))
    tile = pl.BlockSpec((TM, D), lambda i: (i, 0))
    tile_t = pl.BlockSpec((D, TM), lambda i: (0, i))
    return pl.pallas_call(
        body, name="mixer_fwd", grid=(t // TM,),
        in_specs=[tile] + _part_specs(TM, 4, 5)
        + [pl.BlockSpec((2, TM, D), lambda i: (0, i, 0)), pl.BlockSpec((1, 6, 1, D), _stream_row(TM)), vec(D), vec(D),
           VMEM_WHOLE, VMEM_WHOLE, vec(HD), VMEM_WHOLE, VMEM_WHOLE, VMEM_WHOLE],
        out_specs=[tile] * 4 + [tile_t] * 3,
        out_shape=[jax.ShapeDtypeStruct((t, D), F32)] + [jax.ShapeDtypeStruct((t, D), BF16)] * 3
        + [jax.ShapeDtypeStruct((D, t), BF16)] * 3, compiler_params=_cp(1),
    )(x, parts, parts, parts, parts, parts, o, mod, lnw, lnb, sw, sb, hnw, wa, wb, wo)


def ffn_up_fwd(x, mod, nw, wg):
    t = x.shape[0]

    def body(x_ref, mod_ref, nw_ref, w_ref, out_ref, ht_ref):
        h32 = _norm_mod(x_ref[...], nw_ref[...], mod_ref[0, 3], mod_ref[0, 4])
        ht_ref[...] = h32.T.astype(BF16)
        h = h32.astype(BF16)
        for j in range(N_DEV):
            out_ref[j] = jnp.dot(h, w_ref[j], preferred_element_type=F32)

    return pl.pallas_call(
        body, name="ffn_up_fwd", grid=(t // TM,),
        in_specs=[pl.BlockSpec((TM, D), lambda i: (i, 0)), pl.BlockSpec((1, 6, 1, D), _stream_row(TM)),
                  pl.BlockSpec((1, D), lambda i: (0, 0)), VMEM_WHOLE],
        out_specs=[pl.BlockSpec((N_DEV, TM, FF_SLOT), lambda i: (0, i, 0)), pl.BlockSpec((D, TM), lambda i: (0, i))],
        out_shape=[jax.ShapeDtypeStruct((N_DEV, t, FF_SLOT), F32), jax.ShapeDtypeStruct((D, t), BF16)],
        compiler_params=_cp(1))(x, mod, nw, wg)


def _halo_specs(nt, k_of, i_of):
    per = TM // GRID_W
    last = nt * per - 1
    return [pl.BlockSpec((1, GRID_W, FF_SLOT), lambda *g: (k_of(*g), jnp.maximum(i_of(*g) * per - 1, 0), 0)),
            pl.BlockSpec((1, TM, FF_SLOT), lambda *g: (k_of(*g), i_of(*g), 0)),
            pl.BlockSpec((1, GRID_W, FF_SLOT), lambda *g: (k_of(*g), jnp.minimum(i_of(*g) * per + per, last), 0))]


def _with_halo(prev_ref, main_ref, next_ref, i, nt):
    prev = jnp.where(i >= 2, prev_ref[0], 0.0)
    nxt = jnp.where((i >= 1) & (i <= nt - 2), next_ref[0], 0.0)
    return jnp.concatenate([prev, main_ref[0], nxt], axis=0)


def _tap_valid(dc, i, n_rows, offset):
    r = lax.broadcasted_iota(jnp.int32, (n_rows, 1), 0) - offset
    col = jnp.bitwise_and(r, GRID_W - 1)
    pos = jnp.where(i == 0, r, col) + dc
    return (pos >= 0) & (pos < jnp.where(i == 0, TM, GRID_W))


def _row_weight(cw_ref, dr, dc, i):
    w = cw_ref[0, 3 * (dr + 1) + dc + 1:3 * (dr + 1) + dc + 2, :]
    return w if dr == 0 else jnp.where(i == 0, 0.0, w)


def ffn_down_fwd(x, av, mod, cw, cb, wd):
    t = x.shape[0]
    nt = t // TM
    ext = TM + 2 * GRID_W

    def body(x_ref, ap_ref, am_ref, an_ref, v_ref, mod_ref, cw_ref, cb_ref, wd_ref, out_ref, ac_ref, y_ref, z_ref, acc):
        i, k = pl.program_id(0), pl.program_id(1)
        a_ext = _with_halo(ap_ref, am_ref, an_ref, i, nt)
        conv = jnp.zeros((TM, FF_SLOT), F32) + cb_ref[0]
        for dc in (-1, 0, 1):
            rolled = a_ext if dc == 0 else jnp.where(_tap_valid(dc, i, ext, GRID_W), pltpu.roll(a_ext, (-dc) % ext, 0), 0.0)
            for dr in (-1, 0, 1):
                lo = GRID_W + GRID_W * dr
                conv = conv + rolled[lo:lo + TM] * _row_weight(cw_ref, dr, dc, i)
        ac_ref[0] = conv
        z = _glu_fn(conv, v_ref[0]).astype(BF16)
        z_ref[0] = z
        part = mm(z, wd_ref[0])

        @pl.when(k == 0)
        def _():
            acc[...] = part

        @pl.when(k > 0)
        def _():
            acc[...] += part

        @pl.when(k == N_FFK - 1)
        def _():
            y_ref[...] = acc[...]
            out_ref[...] = x_ref[...] + mod_ref[0, 5] * acc[...]

    tile = pl.BlockSpec((TM, D), lambda i, k: (i, 0))
    return pl.pallas_call(
        body, name="ffn_down_fwd", grid=(nt, N_FFK),
        in_specs=[tile] + _halo_specs(nt, lambda i, k: k, lambda i, k: i)
        + [pl.BlockSpec((1, TM, FF_SLOT), lambda i, k: (N_FFK + k, i, 0)),
           pl.BlockSpec((1, 6, 1, D), lambda i, k: (jnp.where(i < 1, 0, 1), 0, 0, 0)),
           pl.BlockSpec((1, 9, FF_SLOT), lambda i, k: (k, 0, 0)), pl.BlockSpec((1, 1, FF_SLOT), lambda i, k: (k, 0, 0)),
           pl.BlockSpec((1, FF_SLOT, D), lambda i, k: (k, 0, 0))],
        out_specs=[tile, pl.BlockSpec((1, TM, FF_SLOT), lambda i, k: (k, i, 0)), tile,
                   pl.BlockSpec((1, TM, FF_SLOT), lambda i, k: (k, i, 0))],
        out_shape=[jax.ShapeDtypeStruct((t, D), F32), jax.ShapeDtypeStruct((N_FFK, t, FF_SLOT), F32),
                   jax.ShapeDtypeStruct((t, D), F32), jax.ShapeDtypeStruct((N_FFK, t, FF_SLOT), BF16)],
        scratch_shapes=[pltpu.VMEM((TM, D), F32)], compiler_params=_cp(2))(x, av, av, av, av, mod, cw, cb, wd)


def loss_fwd_bwd(x, target, fw):
    t = x.shape[0]

    def body(x_ref, t_ref, w_ref, loss_ref, dx_ref, dw_ref):
        i = pl.program_id(0)

        @pl.when(i == 0)
        def _():
            loss_ref[...] = jnp.zeros_like(loss_ref)
            dw_ref[...] = jnp.zeros_like(dw_ref)
            dx_ref[...] = jnp.zeros_like(dx_ref)

        @pl.when(i > 0)
        def _():
            y, vjp = jax.vjp(_rms, x_ref[...], w_ref[...])
            err = y - t_ref[...]
            loss_ref[...] += 0.5 * jnp.sum(jnp.sum(err * err, axis=-1, keepdims=True) / D)
            dx, dw = vjp(err / D)
            dx_ref[...] = dx
            dw_ref[...] += dw

    return pl.pallas_call(
        body, name="loss_fwd_bwd", grid=(t // TM,),
        in_specs=[pl.BlockSpec((TM, D), lambda i: (i, 0)), pl.BlockSpec((TM, D), lambda i: (jnp.maximum(i - 1, 0), 0)),
                  pl.BlockSpec((1, D), lambda i: (0, 0))],
        out_specs=[pl.BlockSpec((8, 128), lambda i: (0, 0)), pl.BlockSpec((TM, D), lambda i: (i, 0)),
                   pl.BlockSpec((1, D), lambda i: (0, 0))],
        out_shape=[jax.ShapeDtypeStruct((8, 128), F32), jax.ShapeDtypeStruct((t, D), F32), jax.ShapeDtypeStruct((1, D), F32)],
        compiler_params=_cp(1))(x, target, fw)


def _stream_add(ref, k, is_ctx, val):
    ref[0, k] += jnp.where(is_ctx, val, 0.0)
    ref[1, k] += jnp.where(is_ctx, 0.0, val)


def ffn_down_bwd(dx, ac, av, y, mod, wd):
    t = dx.shape[0]
    nt = t // TM

    def body(dx_ref, ac_ref, v_ref, y_ref, mod_ref, wd_ref, dav_ref, dac_ref, dout_ref, dg_ref):
        i = pl.program_id(0)

        @pl.when(i == 0)
        def _():
            dg_ref[...] = jnp.zeros_like(dg_ref)

        _stream_add(dg_ref, 0, i == 0, jnp.sum(dx_ref[...] * y_ref[...], axis=0, keepdims=True))
        dout = (mod_ref[0, 5] * dx_ref[...]).astype(BF16)
        dout_ref[...] = dout
        for k in range(N_FFK):
            _, vjp = jax.vjp(_glu_fn, ac_ref[k], v_ref[k])
            dac, dv = vjp(mm_nt(dout, wd_ref[k]))
            dac_ref[k] = dac
            dav_ref[k] = dv.astype(BF16)

    tile = pl.BlockSpec((TM, D), lambda i: (i, 0))
    half = lambda first: pl.BlockSpec((N_FFK, TM, FF_SLOT), lambda i: (first, i, 0))
    return pl.pallas_call(
        body, name="ffn_down_bwd", grid=(nt,),
        in_specs=[tile, half(0), half(1), tile, pl.BlockSpec((1, 6, 1, D), _stream_row(TM)), VMEM_WHOLE],
        out_specs=[half(1), half(0), tile, pl.BlockSpec((2, 1, 1, D), lambda i: (0, 0, 0, 0))],
        out_shape=[jax.ShapeDtypeStruct((N_DEV, t, FF_SLOT), BF16), jax.ShapeDtypeStruct((N_FFK, t, FF_SLOT), F32),
                   jax.ShapeDtypeStruct((t, D), BF16), jax.ShapeDtypeStruct((2, 1, 1, D), F32)],
        compiler_params=_cp(1))(dx, ac, av, y, mod, wd)


def conv_bwd(dav, dac, av, cw):
    t = dac.shape[1]
    nt = t // TM
    ext = TM + 2 * GRID_W

    def body(dav_in, gp_ref, gm_ref, gn_ref, ap_ref, am_ref, an_ref, cw_ref, dav_ref, dcw_ref, dcb_ref):
        k, i = pl.program_id(0), pl.program_id(1)

        @pl.when(i == 0)
        def _():
            dcw_ref[...] = jnp.zeros_like(dcw_ref)
            dcb_ref[...] = jnp.zeros_like(dcb_ref)

        g_ext = _with_halo(gp_ref, gm_ref, gn_ref, i, nt)
        a_ext = _with_halo(ap_ref, am_ref, an_ref, i, nt)
        g_main = gm_ref[0]
        dcb_ref[0] += jnp.sum(g_main, axis=0, keepdims=True)
        da = jnp.zeros((TM, FF_SLOT), F32)
        for dc in (-1, 0, 1):
            g_rolled = g_ext if dc == 0 else pltpu.roll(jnp.where(_tap_valid(dc, i, ext, GRID_W), g_ext, 0.0), dc % ext, 0)
            a_rolled = a_ext if dc == 0 else pltpu.roll(a_ext, (-dc) % ext, 0)
            g_valid = g_main if dc == 0 else jnp.where(_tap_valid(dc, i, TM, 0), g_main, 0.0)
            for dr in (-1, 0, 1):
                lo = GRID_W - GRID_W * dr
                da = da + g_rolled[lo:lo + TM] * _row_weight(cw_ref, dr, dc, i)
                lo = GRID_W + GRID_W * dr
                tap = 3 * (dr + 1) + dc + 1
                dw = jnp.sum(g_valid * a_rolled[lo:lo + TM], axis=0, keepdims=True)
                dcw_ref[0, tap:tap + 1, :] += dw if dr == 0 else jnp.where(i == 0, 0.0, dw)
        dav_ref[0] = da.astype(BF16)

    return pl.pallas_call(
        body, name="conv_bwd", grid=(N_FFK, nt),
        in_specs=[ANY] + _halo_specs(nt, lambda k, i: k, lambda k, i: i) + _halo_specs(nt, lambda k, i: k, lambda k, i: i)
        + [pl.BlockSpec((1, 9, FF_SLOT), lambda k, i: (k, 0, 0))],
        out_specs=[pl.BlockSpec((1, TM, FF_SLOT), lambda k, i: (k, i, 0)), pl.BlockSpec((1, 9, FF_SLOT), lambda k, i: (k, 0, 0)),
                   pl.BlockSpec((1, 1, FF_SLOT), lambda k, i: (k, 0, 0))],
        out_shape=[jax.ShapeDtypeStruct(dav.shape, BF16), jax.ShapeDtypeStruct((N_FFK, 9, FF_SLOT), F32),
                   jax.ShapeDtypeStruct((N_FFK, 1, FF_SLOT), F32)],
        input_output_aliases={0: 0}, compiler_params=_cp(2))(dav, dac, dac, dac, av, av, av, cw)


def _norm_mod_bwd(x_ref, nw_ref, mod_ref, k_shift, dh, dx_in, dx_ref, dnw_ref, dmod_ref, is_ctx):
    _, vjp = jax.vjp(_norm_mod, x_ref[...], nw_ref[...], mod_ref[0, k_shift], mod_ref[0, k_shift + 1])
    dx, dnw, dshift, dscale = vjp(dh)
    dx_ref[...] = dx_in + dx
    dnw_ref[...] += dnw
    _stream_add(dmod_ref, 0, is_ctx, dshift)
    _stream_add(dmod_ref, 1, is_ctx, dscale)


def ffn_up_bwd_x(dx2, x, dav, mod, nw, wg):
    t = x.shape[0]

    def body(dx2_ref, x_ref, dav_ref, mod_ref, nw_ref, w_ref, dx_ref, dnw_ref, dmod_ref):
        i = pl.program_id(0)

        @pl.when(i == 0)
        def _():
            dnw_ref[...] = jnp.zeros_like(dnw_ref)
            dmod_ref[...] = jnp.zeros_like(dmod_ref)

        dh = mm_nt(dav_ref[0], w_ref[0])
        for j in range(1, N_DEV):
            dh = dh + mm_nt(dav_ref[j], w_ref[j])
        _norm_mod_bwd(x_ref, nw_ref, mod_ref, 3, dh, dx2_ref[...], dx_ref, dnw_ref, dmod_ref, i == 0)

    tile = pl.BlockSpec((TM, D), lambda i: (i, 0))
    return pl.pallas_call(
        body, name="ffn_up_bwd_x", grid=(t // TM,),
        in_specs=[tile, tile, pl.BlockSpec((N_DEV, TM, FF_SLOT), lambda i: (0, i, 0)), pl.BlockSpec((1, 6, 1, D), _stream_row(TM)),
                  pl.BlockSpec((1, D), lambda i: (0, 0)), VMEM_WHOLE],
        out_specs=[tile, pl.BlockSpec((1, D), lambda i: (0, 0)), pl.BlockSpec((2, 2, 1, D), lambda i: (0, 0, 0, 0))],
        out_shape=[jax.ShapeDtypeStruct((t, D), F32), jax.ShapeDtypeStruct((1, D), F32), jax.ShapeDtypeStruct((2, 2, 1, D), F32)],
        compiler_params=_cp(1))(dx2, x, dav, mod, nw, wg)


def weight_grad(at, dout, slot, name, after=None):
    rows, t = at.shape
    stacked = dout.ndim == 3
    n = dout.shape[0] if stacked else dout.shape[1] // slot

    def body(a_ref, d_ref, *rest):
        dw_ref = rest[-1]
        dw_ref[0] = jnp.dot(a_ref[...], d_ref[0] if stacked else d_ref[...], preferred_element_type=F32).astype(dw_ref.dtype)

    d_spec = pl.BlockSpec((1, t, slot), lambda j: (j, 0, 0)) if stacked else pl.BlockSpec((t, slot), lambda j: (0, j))
    extra = [] if after is None else [jnp.reshape(after, (1, 1))]
    return pl.pallas_call(
        body, name=name, grid=(n,), in_specs=[VMEM_WHOLE, d_spec] + [ANY] * len(extra),
        out_specs=pl.BlockSpec((1, rows, slot), lambda j: (j, 0, 0)),
        out_shape=jax.ShapeDtypeStruct((n, rows, slot), GRAD_WIRE), compiler_params=_cp(1))(at, dout, *extra)


def weight_grad_rows(at, dout, name):
    n, t, rows = at.shape
    cols = dout.shape[1]

    def body(a_ref, d_ref, dw_ref):
        dw_ref[0] = _dot(a_ref[0], d_ref[...], ((0,), (0,))).astype(dw_ref.dtype)

    return pl.pallas_call(
        body, name=name, grid=(n,), in_specs=[pl.BlockSpec((1, t, rows), lambda k: (k, 0, 0)), VMEM_WHOLE],
        out_specs=pl.BlockSpec((1, rows, cols), lambda k: (k, 0, 0)),
        out_shape=jax.ShapeDtypeStruct((n, rows, cols), GRAD_WIRE), compiler_params=_cp(1))(at, dout)


def mixer_bwd(dx, parts, o, pa, pb, y, mod, lnw, lnb, sw, sb, hnw, wa, wb, wo):
    t = dx.shape[0]
    tm = TM
    n_ctx = CTX // tm

    def body(dx_ref, u_ref, v_ref, og_ref, ga_ref, gb_ref, o_ref, pa_ref, pb_ref, y_ref, mod_ref, lnw_ref, lnb_ref, sw_ref,
             sb_ref, hnw_ref, wa_ref, wb_ref, wo_ref, dp_ref, do_ref, dy_ref, dpa_ref, dpb_ref, dlnw_ref, dlnb_ref, dsw_ref,
             dsb_ref, dhnw_ref, dg_ref):
        i = pl.program_id(0)

        @pl.when(i == 0)
        def _():
            for r in (dlnw_ref, dlnb_ref, dsw_ref, dsb_ref, dhnw_ref, dg_ref):
                r[...] = jnp.zeros_like(r)

        _, _, vjps, vjp_b = _mixer_tile(slice(0, tm), u_ref, v_ref, og_ref, o_ref, lnw_ref, lnb_ref, sw_ref, sb_ref, hnw_ref)
        pa, pb = pa_ref[...].astype(F32), pb_ref[...].astype(F32)
        sa, sbg = jax.nn.sigmoid(ga_ref[...]), jax.nn.sigmoid(gb_ref[...])
        dxv = dx_ref[...]
        _stream_add(dg_ref, 0, i < n_ctx, jnp.sum(dxv * y_ref[...].astype(F32), axis=0, keepdims=True))
        dy = (mod_ref[0, 2] * dxv).astype(BF16)
        dy_ref[...] = dy
        dmerged = mm_nt(dy, wo_ref[...])
        dpa, dpb = (sa * dmerged).astype(BF16), (sbg * dmerged).astype(BF16)
        dpa_ref[...], dpb_ref[...] = dpa, dpb
        first = 4 * D
        dp_ref[:, first + 3 * D:first + 4 * D] = (dmerged * pa * sa * (1.0 - sa)).astype(BF16)
        dp_ref[:, first + 4 * D:first + 5 * D] = (dmerged * pb * sbg * (1.0 - sbg)).astype(BF16)
        dya = mm_nt(dpa, wa_ref[...])
        dob, dog, dhnw = vjp_b(mm_nt(dpb, wb_ref[...]))
        dp_ref[:, first + 2 * D:first + 3 * D] = dog.astype(BF16)
        dhnw_ref[...] += dhnw
        for g in range(HEADS):
            do_ref[:, _hsl(g)] = dob[g]
        for c, vjp_a in enumerate(vjps):
            rows = slice(c * SGU_CH, (c + 1) * SGU_CH)
            dub, dvb, dlnw, dlnb, dsw, dsb = vjp_a(dya[rows])
            for g in range(HEADS):
                dp_ref[rows, first + g * HD:first + (g + 1) * HD] = dub[g].astype(BF16)
                dp_ref[rows, first + D + g * HD:first + D + (g + 1) * HD] = dvb[g].astype(BF16)
                dlnw_ref[:, _hsl(g)] += dlnw[g]
                dlnb_ref[:, _hsl(g)] += dlnb[g]
                dsw_ref[g] += dsw[g]
                dsb_ref[g] += dsb[g]

    vec = lambda n: pl.BlockSpec((1, n), lambda i: (0, 0))
    tile = pl.BlockSpec((tm, D), lambda i: (i, 0))
    sds = jax.ShapeDtypeStruct
    return pl.pallas_call(
        body, name="mixer_bwd", grid=(t // tm,),
        in_specs=[tile] + _part_specs(tm, 4, 5)
        + [pl.BlockSpec((2, tm, D), lambda i: (0, i, 0)), tile, tile, tile, pl.BlockSpec((1, 6, 1, D), _stream_row(tm)),
           vec(D), vec(D), VMEM_WHOLE, VMEM_WHOLE, vec(HD), VMEM_WHOLE, VMEM_WHOLE, VMEM_WHOLE],
        out_specs=[pl.BlockSpec((tm, D_IN), lambda i: (i, 0)), tile, tile, tile, tile, vec(D), vec(D),
                   VMEM_WHOLE, VMEM_WHOLE, vec(HD), pl.BlockSpec((2, 1, 1, D), lambda i: (0, 0, 0, 0))],
        out_shape=[sds((t, D_IN), BF16), sds((t, D), F32), sds((t, D), BF16), sds((t, D), BF16), sds((t, D), BF16),
                   sds((1, D), F32), sds((1, D), F32), sds((HEADS, SGU_CH, SGU_CH), F32), sds((HEADS, SGU_CH, 1), F32),
                   sds((1, HD), F32), sds((2, 1, 1, D), F32)],
        compiler_params=_cp(1))(dx, parts, parts, parts, parts, parts, o, pa, pb, y, mod, lnw, lnb, sw, sb, hnw, wa, wb, wo)


def hgrn_bwd(d, parts, lb, mc, mtc, mrefc, ck, do, first=None, dparts=None):
    t = parts.shape[0]
    nc = t // CH
    chunk = _scan_chunk(nc)
    rev = lambda s: chunk(d, nc - 1 - s)

    def body(q_ref, f_ref, i_ref, lb_ref, m_ref, mt_ref, mr_ref, ck_ref, do_ref, *rest):
        dst = rest[-1]
        dlb_ref = rest[-2]

        @pl.when(pl.program_id(0) == 0)
        def _():
            dst[...] = jnp.zeros_like(dst)
            dlb_ref[...] = jnp.zeros_like(dlb_ref)

        heads = range(HEADS)
        fn = functools.partial(_hgrn_chunk, m=m_ref[0], mt=mt_ref[0], mref=mr_ref[0])
        _, vjp = jax.vjp(fn, [ck_ref[0, 0, h] for h in heads], [q_ref[:, _hsl(h)] for h in heads],
                         [f_ref[:, _hsl(h)] for h in heads], [i_ref[:, _hsl(h)] for h in heads],
                         [lb_ref[0, :, _hsl(h)] for h in heads])
        dstl, dq, df, di, dlb = vjp(([do_ref[:, _hsl(h)] for h in heads], [dst[h] for h in heads]))
        for h in heads:
            dst[h] = dstl[h]
            dlb_ref[0, :, _hsl(h)] += dlb[h]
            if d == 0:
                dq_ref, df_ref, di_ref = rest[:3]
                dq_ref[:, _hsl(h)] = dq[h].astype(BF16)
                df_ref[:, _hsl(h)] = df[h].astype(BF16)
                di_ref[:, _hsl(h)] = di[h].astype(BF16)
            else:
                dq0_ref, df0_ref, di0_ref, _, dp_ref = rest[:5]
                col = lambda k: slice(k * D + h * HD, k * D + (h + 1) * HD)
                dp_ref[:, col(0)] = (dq0_ref[:, _hsl(h)].astype(F32) + dq[h]).astype(BF16)
                dp_ref[:, col(1)] = df0_ref[:, _hsl(h)]
                dp_ref[:, col(2)] = df[h].astype(BF16)
                dp_ref[:, col(3)] = (di0_ref[:, _hsl(h)].astype(F32) + di[h]).astype(BF16)

    const = lambda s: (d, 0, 0)
    at = lambda k: pl.BlockSpec((CH, D), lambda s: (rev(s), k))
    in_specs = [at(0), at(1 + d), at(3), pl.BlockSpec((1, 1, D), const), pl.BlockSpec((1, CH, CH), const),
                pl.BlockSpec((1, CH, CH), const), pl.BlockSpec((1, CH, 1), const),
                pl.BlockSpec((1, 1, HEADS, HD, HD), lambda s: (d, nc - 1 - s, 0, 0, 0)), at(0)]
    dlb_spec, dlb_shape = pl.BlockSpec((1, 1, D), lambda s: (0, 0, 0)), jax.ShapeDtypeStruct((1, 1, D), F32)
    common = dict(grid=(nc,), scratch_shapes=[pltpu.VMEM((HEADS, HD, HD), F32)], compiler_params=_cp(1))
    if d == 0:
        return pl.pallas_call(body, name="hgrn_bwd_fwd_dir", in_specs=in_specs, out_specs=[at(0)] * 3 + [dlb_spec],
                              out_shape=[jax.ShapeDtypeStruct((t, D), BF16)] * 3 + [dlb_shape], **common,
                              )(parts, parts, parts, lb, mc, mtc, mrefc, ck, do)
    return pl.pallas_call(body, name="hgrn_bwd_bwd_dir", in_specs=in_specs + [at(0)] * 3 + [ANY],
                          out_specs=[pl.BlockSpec((CH, 4 * D), lambda s: (rev(s), 0)), dlb_spec],
                          out_shape=[jax.ShapeDtypeStruct(dparts.shape, BF16), dlb_shape], input_output_aliases={12: 0},
                          **common)(parts, parts, parts, lb, mc, mtc, mrefc, ck, do, *first, dparts)


def in_proj_bwd_x(dx1, x, dparts, mod, nw, wg, after=None):
    t = x.shape[0]
    tm = TM
    n_ctx = CTX // tm

    def body(dx1_ref, x_ref, dp_ref, mod_ref, nw_ref, w_ref, *rest):
        dx_ref, dnw_ref, dmod_ref = rest[-3:]
        i = pl.program_id(0)

        @pl.when(i == 0)
        def _():
            dnw_ref[...] = jnp.zeros_like(dnw_ref)
            dmod_ref[...] = jnp.zeros_like(dmod_ref)

        dh = mm_nt(dp_ref[:, 0:IN_SLOT], w_ref[0])
        for j in range(1, N_DEV):
            dh = dh + mm_nt(dp_ref[:, j * IN_SLOT:(j + 1) * IN_SLOT], w_ref[j])
        _norm_mod_bwd(x_ref, nw_ref, mod_ref, 0, dh, dx1_ref[...], dx_ref, dnw_ref, dmod_ref, i < n_ctx)

    tile = pl.BlockSpec((tm, D), lambda i: (i, 0))
    extra = [] if after is None else [jnp.reshape(after, (1, 1))]
    return pl.pallas_call(
        body, name="in_proj_bwd_x", grid=(t // tm,),
        in_specs=[tile, tile, pl.BlockSpec((tm, D_IN), lambda i: (i, 0)), pl.BlockSpec((1, 6, 1, D), _stream_row(tm)),
                  pl.BlockSpec((1, D), lambda i: (0, 0)), VMEM_WHOLE] + [ANY] * len(extra),
        out_specs=[tile, pl.BlockSpec((1, D), lambda i: (0, 0)), pl.BlockSpec((2, 2, 1, D), lambda i: (0, 0, 0, 0))],
        out_shape=[jax.ShapeDtypeStruct((t, D), F32), jax.ShapeDtypeStruct((1, D), F32), jax.ShapeDtypeStruct((2, 2, 1, D), F32)],
        compiler_params=_cp(1))(dx1, x, dparts, mod, nw, wg, *extra)


def _lb_fn(h0, h1):
    m = jnp.maximum(h0, h1)
    e0, e1 = jnp.exp(h0 - m), jnp.exp(h1 - m)
    return e1 / (e0 + e1)


def lower_bounds(hlb):
    def body(h_ref, out_ref):
        out_ref[...] = _lb_fn(h_ref[0:1, :], h_ref[1:2, :])
    return pl.pallas_call(body, name="lower_bounds", out_shape=jax.ShapeDtypeStruct((1, 2 * D), F32))(hlb)


def lower_bounds_bwd(hlb, dlb1):
    def body(h_ref, d_ref, out_ref):
        _, vjp = jax.vjp(_lb_fn, h_ref[0:1, :], h_ref[1:2, :])
        d0, d1 = vjp(d_ref[...])
        out_ref[0:1, :] = d0
        out_ref[1:2, :] = d1
    return pl.pallas_call(body, name="lower_bounds_bwd", out_shape=jax.ShapeDtypeStruct((2, 2 * D), F32))(hlb, dlb1)


def _ada_fn(c_all, cctx8, w, b):
    dot = lambda a, l: jnp.dot(_silu(a), w[l], precision=HIGHEST, preferred_element_type=F32) + b[l]
    return [dot(c_all, l) for l in range(2)], [dot(cctx8, l) for l in range(2)]


def ada_fwd(c_all, cctx8, w, b):
    cols = w.shape[-1]

    def body(c_ref, cc_ref, w_ref, b_ref, out_ref):
        ox, oc = _ada_fn(c_ref[...], cc_ref[...], [w_ref[0], w_ref[1]], [b_ref[0], b_ref[1]])
        for l in range(2):
            out_ref[l, 0] = ox[l]
            out_ref[l, 1] = oc[l]
    return pl.pallas_call(body, name="ada_fwd", out_shape=jax.ShapeDtypeStruct((2, 2, N_DEV, cols), F32),
                          compiler_params=_cp(0))(c_all, cctx8, w, b)


def ada_bwd(c_all, cctx8, w, b, dmx, dmc):
    cols = w.shape[-1]

    def body(c_ref, cc_ref, w_ref, b_ref, dmx_ref, dmc_ref, dw_ref, dc_ref):
        fn = lambda cc, w0, w1: _ada_fn(c_ref[...], cc, [w0, w1], [b_ref[0], b_ref[1]])
        _, vjp = jax.vjp(fn, cc_ref[...], w_ref[0], w_ref[1])
        dcc, dw0, dw1 = vjp(([dmx_ref[0], dmx_ref[1]], [dmc_ref[0], dmc_ref[1]]))
        dw_ref[0] = dw0
        dw_ref[1] = dw1
        dc_ref[...] = jnp.sum(dcc, axis=0, keepdims=True)
    return pl.pallas_call(body, name="ada_bwd", out_shape=[jax.ShapeDtypeStruct((2, D, cols), F32), jax.ShapeDtypeStruct((1, D), F32)],
                          compiler_params=_cp(0))(c_all, cctx8, w, b, dmx, dmc)


def adamw(w, m, v, gparts, name):
    r, c = w.shape
    p = gparts.shape[0]
    rt = r
    while rt % 16 == 0 and (p + 7) * rt * c * 4 * 2 > 24 * 2 ** 20:
        rt //= 2

    def body(w_ref, m_ref, v_ref, g_ref, go_ref, d_ref, mo_ref, vo_ref):
        g = g_ref[0].astype(F32)
        for k in range(1, p):
            g = g + g_ref[k].astype(F32)
        m2 = ADAM_B1 * m_ref[...] + (1.0 - ADAM_B1) * g
        v2 = ADAM_B2 * v_ref[...] + (1.0 - ADAM_B2) * (g * g)
        m_hat = m2 / (1.0 - ADAM_B1 ** ADAM_STEP)
        v_hat = v2 / (1.0 - ADAM_B2 ** ADAM_STEP)
        go_ref[...] = g
        d_ref[...] = -ADAM_LR * (m_hat / (jnp.sqrt(v_hat) + ADAM_EPS) + ADAM_WD * w_ref[...])
        mo_ref[...] = m2
        vo_ref[...] = v2

    tile = pl.BlockSpec((rt, c), lambda i: (i, 0))
    return pl.pallas_call(
        body, name=name, grid=(r // rt,),
        in_specs=[tile, tile, tile, pl.BlockSpec((p, rt, c), lambda i: (0, i, 0))], out_specs=[tile] * 4,
        out_shape=[jax.ShapeDtypeStruct((r, c), F32)] * 4, compiler_params=_cp(1))(w, m, v, gparts)


def _me():
    x, y, c = lax.axis_index("x"), lax.axis_index("y"), lax.axis_index("c")
    return x, y, c, 4 * x + 2 * y + c


def _peer(x, y, c, p):
    fx, fy, fc = (p >> 2) & 1, (p >> 1) & 1, p & 1
    return (1 - x if fx else x, 1 - y if fy else y, 1 - c if fc else c)


def all_gather(arrs, name, after=None):
    n = len(arrs)
    extra = [] if after is None else list(after) if isinstance(after, (list, tuple)) else [after]

    def body(*refs):
        ins, outs = refs[:n], refs[n + len(extra):2 * n + len(extra)]
        send, recv, local = refs[2 * n + len(extra):]
        x, y, c, me = _me()
        copies = []
        for a in range(n):
            lc = pltpu.make_async_copy(ins[a], outs[a].at[me], local.at[a])
            lc.start()
            copies.append(lc)
            for p in range(1, N_DEV):
                cp = pltpu.make_async_remote_copy(src_ref=ins[a], dst_ref=outs[a].at[me], send_sem=send.at[a, p - 1],
                                                  recv_sem=recv.at[a, p - 1], device_id=_peer(x, y, c, p),
                                                  device_id_type=pl.DeviceIdType.MESH)
                cp.start()
                copies.append(cp)
        for cp in copies:
            cp.wait()

    return pl.pallas_call(
        body, name=name, in_specs=[ANY] * (n + len(extra)), out_specs=[ANY] * n,
        out_shape=[jax.ShapeDtypeStruct((N_DEV,) + a.shape, a.dtype) for a in arrs],
        scratch_shapes=[pltpu.SemaphoreType.DMA((n, N_DEV - 1)), pltpu.SemaphoreType.DMA((n, N_DEV - 1)),
                        pltpu.SemaphoreType.DMA((n,))])(*arrs, *extra)


HBM = pl.BlockSpec(memory_space=pltpu.HBM)
SEM = pl.BlockSpec(memory_space=pltpu.SEMAPHORE)


def _in_hbm(a):
    return pltpu.with_memory_space_constraint(a, pltpu.HBM)


ALL_PEERS = tuple(range(1, N_DEV))
SAME_CORE_AND_SIBLING = (1, 2, 4, 6)
OTHER_CHIPS = (2, 4, 6)


def _exchange_refs(srcs, lands, layer, scatter, a, x, y, c, p, forward=False):
    me = 4 * x + 2 * y + c
    px, py, pc = _peer(x, y, c, p) if p else (x, y, c)
    if forward and p:
        slot = lands[a].at[4 * px + 2 * py + pc]
        return slot, slot, _peer(x, y, c, 1)
    dst = lands[a].at[me] if layer is None else lands[a].at[me, layer]
    src = srcs[a].at[4 * px + 2 * py + pc] if scatter else dst
    return src, dst, (px, py, pc)


def exchange_start(srcs, lands, layer, scatter, name, after=None, peers=ALL_PEERS, forward=False):
    n, ns = len(lands), len(srcs)
    extra = [] if after is None else [after]

    def body(*refs):
        ins, lz = refs[:ns], refs[ns:ns + n]
        send, recv = refs[ns + n + len(extra)], refs[ns + n + len(extra) + 1]
        token = refs[-1]
        x, y, c, _ = _me()
        for a in range(n):
            for p in peers:
                src, dst, peer = _exchange_refs(ins, lz, layer, scatter, a, x, y, c, p, forward)
                k = a * (N_DEV - 1) + p - 1
                pltpu.make_async_remote_copy(src_ref=src, dst_ref=dst, send_sem=send.at[k], recv_sem=recv.at[k],
                                             device_id=peer, device_id_type=pl.DeviceIdType.MESH).start()
        token[...] = jnp.zeros_like(token)

    thru = [pltpu.HBM(a.shape, a.dtype) for a in list(srcs) + list(lands)]
    out = pl.pallas_call(
        body, name=name, in_specs=[HBM] * (ns + n) + [ANY] * len(extra),
        out_specs=[SEM, SEM] + [HBM] * (ns + n) + [pl.BlockSpec(memory_space=pltpu.VMEM)],
        out_shape=[pltpu.SemaphoreType.DMA((n * (N_DEV - 1),)), pltpu.SemaphoreType.DMA((n * (N_DEV - 1),))] + thru
        + [jax.ShapeDtypeStruct((8, 128), F32)],
        input_output_aliases={i: 2 + i for i in range(ns + n)},
        compiler_params=pltpu.CompilerParams(has_side_effects=pltpu.SideEffectType.DATAFLOW_SIDE_EFFECTING),
    )(*[_in_hbm(a) for a in list(srcs) + list(lands)], *extra)
    return out[0], out[1], out[2:2 + ns], out[2 + ns:2 + ns + n], out[-1]


def exchange_wait(send, recv, srcs, lands, layer, scatter, after, name, peers=ALL_PEERS):
    n, ns = len(lands), len(srcs)

    def body(*refs):
        ins, lz = refs[:ns], refs[ns:ns + n]
        send_ref, recv_ref = refs[ns + n], refs[ns + n + 1]
        x, y, c, _ = _me()
        for a in range(n):
            for p in peers:
                src, dst, peer = _exchange_refs(ins, lz, layer, scatter, a, x, y, c, 0)
                k = a * (N_DEV - 1) + p - 1
                cp = pltpu.make_async_remote_copy(src_ref=src, dst_ref=dst, send_sem=send_ref.at[k],
                                                  recv_sem=recv_ref.at[k], device_id=peer,
                                                  device_id_type=pl.DeviceIdType.MESH)
                cp.wait_send()
                cp.wait_recv()

    thru = [pltpu.HBM(a.shape, a.dtype) for a in list(srcs) + list(lands)]
    out = pl.pallas_call(
        body, name=name, in_specs=[HBM] * (ns + n) + [SEM, SEM, ANY], out_specs=[HBM] * (ns + n), out_shape=thru,
        input_output_aliases={i: i for i in range(ns + n)},
        compiler_params=pltpu.CompilerParams(has_side_effects=pltpu.SideEffectType.DATAFLOW_SIDE_EFFECTING),
    )(*srcs, *lands, send, recv, after)
    return out[ns:]


def place_own(src, land, me, layer, scatter, name):
    create = isinstance(land, jax.ShapeDtypeStruct)
    r, c = src.shape[-2:]
    rt = r
    while rt % 32 == 0 and rt * c * 4 > 2 ** 21:
        rt //= 2

    def body(me_ref, src_ref, *rest):
        out_ref = rest[-1]
        out_ref[...] = src_ref[...].reshape(out_ref.shape).astype(out_ref.dtype)

    src_spec = (pl.BlockSpec((1, rt, c), lambda i, m: (m[0], i, 0)) if scatter else pl.BlockSpec((rt, c), lambda i, m: (i, 0)))
    out_spec = (pl.BlockSpec((1, rt, c), lambda i, m: (m[0], i, 0)) if layer is None
                else pl.BlockSpec((1, 1, rt, c), lambda i, m: (m[0], layer, i, 0)))
    grid_spec = pltpu.PrefetchScalarGridSpec(num_scalar_prefetch=1, grid=(r // rt,),
                                             in_specs=[src_spec] + ([] if create else [ANY]), out_specs=out_spec)
    return pl.pallas_call(body, name=name, grid_spec=grid_spec, out_shape=jax.ShapeDtypeStruct(land.shape, land.dtype),
                          input_output_aliases={} if create else {2: 0}, compiler_params=_cp(1),
                          )(*((me, src) if create else (me, src, land)))


def _scan_constants():
    r = lax.broadcasted_iota(jnp.int32, (CH, CH), 0)
    s = lax.broadcasted_iota(jnp.int32, (CH, CH), 1)
    lower = (s <= r).astype(F32)
    t = jnp.arange(CH)[:, None]
    mc = jnp.stack([lower, lower.T])
    mref = jnp.stack([(t <= CH // 2 - 1).astype(F32), (t >= CH // 2).astype(F32)])
    return mc, jnp.stack([lower.T, lower]), mref


def local_step(x, ctx, target, mod, lb, w, fetch=None, publish=None, small_ready=None, small_early=None):
    kept = {}

    def keep(l, part, grads):
        kept[(l, part)] = grads
        return 0.0

    fetch = fetch or (lambda l, part, after: w)
    publish = publish or keep
    n_layers = len(mod)
    mc, mtc, mrefc = _scan_constants()
    xs = jnp.concatenate([ctx, x], axis=0)
    saved, big = [], []
    for l in range(n_layers):
        wl = dict(fetch(l, "in", xs))
        parts, ht = in_proj_fwd(xs, mod[l], w["nw1"][l], wl["win"][l])
        o, ck = hgrn_fwd(parts, lb[l], mc, mtc, mrefc)
        wl.update(fetch(l, "rest", o))
        x1, pa, pb, ym, yat, ybt, mt = mixer_fwd(xs, parts, o, mod[l], w["lnw"][l], w["lnb"][l], w["sw"][l], w["sb"][l],
                                                 w["hnw"][l], wl["wa"][l], wl["wb"][l], wl["wo"][l])
        av, h2t = ffn_up_fwd(x1, mod[l], w["nw2"][l], wl["wup"][l])
        x2, ac, y, z = ffn_down_fwd(x1, av, mod[l], w["cw"][l], w["cb"][l], wl["wd"][l])
        saved.append((xs, parts, o, ck, x1, av, ac, y, z, ht, h2t, pa, pb, ym, yat, ybt, mt))
        big.append(wl)
        xs = x2
    loss, dx, dfw = loss_fwd_bwd(xs, target, w["fw"])
    g = {k: [None] * n_layers for k in ("nw1", "nw2", "lnw", "lnb", "sw", "sb", "hnw", "cw", "cb")}
    g["fw"] = dfw
    dmod, dlb = [None] * n_layers, [None] * n_layers
    tok = 0.0
    for l in reversed(range(n_layers)):
        x0, parts, o, ck, x1, av, ac, y, z, ht, h2t, pa, pb, ym, yat, ybt, mt = saved[l]
        wl = big[l]
        dav, dac, dout, dg2 = ffn_down_bwd(dx, ac, av, y, mod[l] + tok, wl["wd"][l])
        dwd = weight_grad_rows(z, dout, "ffn_down_bwd_w")
        dav, g["cw"][l], g["cb"][l] = conv_bwd(dav, dac, av, w["cw"][l])
        dx1, g["nw2"][l], dmod2 = ffn_up_bwd_x(dx, x1, dav, mod[l], w["nw2"][l], wl["wup"][l])
        dwup = weight_grad(h2t, dav, FF_SLOT, "ffn_up_bwd_w")
        tok = publish(l, "ffn", {"wd": dwd, "wup": dwup})
        (dparts, do, dy, dpa, dpb, g["lnw"][l], g["lnb"][l], g["sw"][l], g["sb"][l], g["hnw"][l],
         dg1) = mixer_bwd(dx1, parts, o, pa, pb, ym, mod[l] + tok, w["lnw"][l], w["lnb"][l], w["sw"][l], w["sb"][l],
                          w["hnw"][l], wl["wa"][l], wl["wb"][l], wl["wo"][l])
        tok = publish(l, "mix", {"wa": weight_grad(yat, dpa, D, "mixer_bwd_wa"), "wb": weight_grad(ybt, dpb, D, "mixer_bwd_wb"),
                                 "wo": weight_grad(mt, dy, D, "mixer_bwd_wo")})
        if l == 0 and small_early:
            dmod[0] = jnp.concatenate([jnp.zeros((2, 2, 1, D), F32), dg1, dmod2, dg2], axis=1)
            tok = tok + small_early(loss[0, 0], g, dmod, dlb)
        dq, df, di, dlb_f = hgrn_bwd(0, parts, lb[l] + tok, mc, mtc, mrefc, ck, do)
        dparts, dlb_b = hgrn_bwd(1, parts, lb[l], mc, mtc, mrefc, ck, do, (dq, df, di), dparts)
        dlb[l] = jnp.concatenate([dlb_f, dlb_b], axis=0)
        tok = publish(l, "in", {"win": weight_grad(ht, dparts, IN_SLOT, "in_proj_bwd_w")})
        dx, g["nw1"][l], dmod1 = in_proj_bwd_x(dx1, x0, dparts, mod[l], w["nw1"][l], wl["win"][l], after=tok)
        dmod[l] = jnp.concatenate([dmod1, dg1, dmod2, dg2], axis=1)
    done = small_ready(loss[0, 0], g, dmod, dlb) if small_ready else 0.0
    for (l, part), grads in kept.items():
        for k, v in grads.items():
            g.setdefault(k, [None] * n_layers)[l] = v
    return loss[0, 0], dx[CTX:], g, dmod, dlb, done


ROW = 1024
REPLICATED = ("norm1_w", "sgu_ln_w", "sgu_ln_b", "sgu_w", "sgu_b", "hgrn_lower_bounds", "hgrn_norm_w", "norm2_w",
              "ffn_conv_b", "final_norm_w")
WEIGHT_ORDER = ("c_ctx", "ada_w", "ada_b", "norm1_w", "w_in", "sgu_ln_w", "sgu_ln_b", "sgu_w", "sgu_b", "hgrn_lower_bounds",
                "hgrn_norm_w", "w_branch_a", "w_branch_b", "w_out", "norm2_w", "ffn_w_up", "ffn_conv_w", "ffn_conv_b",
                "ffn_w_down", "final_norm_w")


def _rows_of(n):
    return -(-n // (8 * ROW)) * 8


def _pack(arrs, total_rows=None):
    parts = []
    for a in arrs:
        flat = a.reshape(-1).astype(F32)
        rows = _rows_of(flat.shape[0])
        parts.append(jnp.pad(flat, (0, rows * ROW - flat.shape[0])).reshape(rows, ROW))
    have = sum(p.shape[0] for p in parts)
    if total_rows is not None and total_rows > have:
        parts.append(jnp.zeros((total_rows - have, ROW), F32))
    return jnp.concatenate(parts, axis=0)


def _unpack(packed, shapes):
    lead = packed.shape[:-2]
    out, r0 = [], 0
    for s in shapes:
        n = math.prod(s)
        rows = _rows_of(n)
        out.append(packed[..., r0:r0 + rows, :].reshape(lead + (rows * ROW,))[..., :n].reshape(lead + tuple(s)))
        r0 += rows
    return out


def kernel(x, c, ctx, c_ctx, ada_w, ada_b, norm1_w, w_in, sgu_ln_w, sgu_ln_b, sgu_w, sgu_b, hgrn_lower_bounds, hgrn_norm_w, w_branch_a, w_branch_b, w_out, norm2_w, ffn_w_up, ffn_conv_w, ffn_conv_b, ffn_w_down, final_norm_w, loss_target, m_c_ctx, m_ada_w, m_ada_b, m_norm1_w, m_w_in, m_sgu_ln_w, m_sgu_ln_b, m_sgu_w, m_sgu_b, m_hgrn_lower_bounds, m_hgrn_norm_w, m_w_branch_a, m_w_branch_b, m_w_out, m_norm2_w, m_ffn_w_up, m_ffn_conv_w, m_ffn_conv_b, m_ffn_w_down, m_final_norm_w, v_c_ctx, v_ada_w, v_ada_b, v_norm1_w, v_w_in, v_sgu_ln_w, v_sgu_ln_b, v_sgu_w, v_sgu_b, v_hgrn_lower_bounds, v_hgrn_norm_w, v_w_branch_a, v_w_branch_b, v_w_out, v_norm2_w, v_ffn_w_up, v_ffn_conv_w, v_ffn_conv_b, v_ffn_w_down, v_final_norm_w):
    wts = dict(c_ctx=c_ctx, ada_w=ada_w, ada_b=ada_b, norm1_w=norm1_w, w_in=w_in, sgu_ln_w=sgu_ln_w, sgu_ln_b=sgu_ln_b,
               sgu_w=sgu_w, sgu_b=sgu_b, hgrn_lower_bounds=hgrn_lower_bounds, hgrn_norm_w=hgrn_norm_w, w_branch_a=w_branch_a,
               w_branch_b=w_branch_b, w_out=w_out, norm2_w=norm2_w, ffn_w_up=ffn_w_up, ffn_conv_w=ffn_conv_w,
               ffn_conv_b=ffn_conv_b, ffn_w_down=ffn_w_down, final_norm_w=final_norm_w)
    mom1 = dict(c_ctx=m_c_ctx, ada_w=m_ada_w, ada_b=m_ada_b, norm1_w=m_norm1_w, w_in=m_w_in, sgu_ln_w=m_sgu_ln_w,
                sgu_ln_b=m_sgu_ln_b, sgu_w=m_sgu_w, sgu_b=m_sgu_b, hgrn_lower_bounds=m_hgrn_lower_bounds,
                hgrn_norm_w=m_hgrn_norm_w, w_branch_a=m_w_branch_a, w_branch_b=m_w_branch_b, w_out=m_w_out, norm2_w=m_norm2_w,
                ffn_w_up=m_ffn_w_up, ffn_conv_w=m_ffn_conv_w, ffn_conv_b=m_ffn_conv_b, ffn_w_down=m_ffn_w_down,
                final_norm_w=m_final_norm_w)
    mom2 = dict(c_ctx=v_c_ctx, ada_w=v_ada_w, ada_b=v_ada_b, norm1_w=v_norm1_w, w_in=v_w_in, sgu_ln_w=v_sgu_ln_w,
                sgu_ln_b=v_sgu_ln_b, sgu_w=v_sgu_w, sgu_b=v_sgu_b, hgrn_lower_bounds=v_hgrn_lower_bounds,
                hgrn_norm_w=v_hgrn_norm_w, w_branch_a=v_w_branch_a, w_branch_b=v_w_branch_b, w_out=v_w_out, norm2_w=v_norm2_w,
                ffn_w_up=v_ffn_w_up, ffn_conv_w=v_ffn_conv_w, ffn_conv_b=v_ffn_conv_b, ffn_w_down=v_ffn_w_down,
                final_norm_w=v_final_norm_w)
    n_layers = w_in.shape[0]
    layers = range(n_layers)
    me = 4 * lax.axis_index("x") + 2 * lax.axis_index("y") + lax.axis_index("c")
    ada_cols = ada_w.shape[-1]

    big = ("w_in", "ffn_w_up", "w_branch_a", "w_branch_b", "w_out", "ffn_w_down")
    short = {"w_in": "win", "ffn_w_up": "wup", "w_branch_a": "wa", "w_branch_b": "wb", "w_out": "wo", "ffn_w_down": "wd"}
    me1 = me.reshape(1).astype(jnp.int32)
    groups = [[("w_in", 0)], [(k, 0) for k in big[1:]], [("w_in", 1)], [(k, 1) for k in big[1:]]]
    in_flight, started = [], 0.0

    def own_slots(n):
        return [place_own(wts[k][l], jax.ShapeDtypeStruct((N_DEV,) + wts[k].shape[1:], BF16), me1, None, False,
                          f"gather_own_{short[k]}_{l}") for k, l in groups[n]]

    def start_group(n, lands, after):
        in_flight.append(exchange_start([], lands, None, False, f"gather_weights_start_{n}", after=after,
                                        peers=SAME_CORE_AND_SIBLING if n == 0 else ALL_PEERS))
        return in_flight[-1][-1]

    (c_all,) = all_gather([c], "gather_c")
    c_all = c_all.reshape(N_DEV, D)
    token = start_group(0, own_slots(0), c_all)
    later = [own_slots(n) for n in range(1, len(groups))]
    cctx8 = jnp.broadcast_to(c_ctx[None, :], (N_DEV, D))
    ada_b_cols = lax.dynamic_slice_in_dim(ada_b, me * ada_cols, ada_cols, axis=1)[:, None, :]
    mod_cols = ada_fwd(c_all, cctx8, ada_w, ada_b_cols)
    xs = jnp.concatenate([ctx[0], x[0]], axis=0)
    lb1 = lower_bounds(hgrn_lower_bounds)
    mod_all, conv_all = all_gather([mod_cols, ffn_conv_w.reshape(n_layers, 9, -1)], "gather_mod_conv",
                                   after=[token, xs, lb1] + [a for lands in later for a in lands])
    conv_full = [conv_all[:, l].transpose(1, 0, 2).reshape(9, N_FFK, FF_SLOT).transpose(1, 0, 2) for l in layers]
    for n in range(1, len(groups)):
        token = start_group(n, later[n - 1], mod_all if n == 1 else token)
    for started_group in in_flight:
        started = started + started_group[-1][0, 0]

    def as_used(k, a):
        return a if k in ("w_in", "ffn_w_up") else a.reshape(N_FFK, FF_SLOT, D) if k == "ffn_w_down" else a.reshape(D, D)

    arrived = {}

    def fetch(l, part, after):
        n = {(0, "in"): 0, (0, "rest"): 1, (1, "in"): 2, (1, "rest"): 3}.get((l, part))
        if n is not None:
            send, recv, _, lands, _ = in_flight[n]
            first = n == 0
            got = exchange_wait(send, recv, [], lands, None, False, after, f"gather_weights_wait_{n}",
                                peers=SAME_CORE_AND_SIBLING if first else ALL_PEERS)
            if first:
                send, recv, _, lands, _ = exchange_start([], got, None, False, "gather_weights_pass_on", peers=OTHER_CHIPS,
                                                         forward=True)
                got = exchange_wait(send, recv, [], lands, None, False, after, "gather_weights_passed_on", peers=OTHER_CHIPS)
            for (k, ll), a in zip(groups[n], got):
                arrived.setdefault(short[k], [None] * n_layers)[ll] = as_used(k, a)
        return arrived

    mod_x = lax.dynamic_index_in_dim(mod_all[:, :, 0], me, axis=2, keepdims=False)
    mod_c = mod_all[:, :, 1, 0]
    mod = [jnp.stack([mod_c[:, l].reshape(6, 1, D), mod_x[:, l].reshape(6, 1, D)]) for l in layers]
    mod[0] = mod[0] + started

    lb = [jnp.zeros((2, 1, D), F32), lb1.reshape(2, 1, D)]

    w = {
        "nw1": [norm1_w[l][None] for l in layers], "nw2": [norm2_w[l][None] for l in layers],
        "lnw": [sgu_ln_w[l][None] for l in layers], "lnb": [sgu_ln_b[l][None] for l in layers],
        "sw": [sgu_w[l] for l in layers], "sb": [sgu_b[l][:, :, None] for l in layers],
        "hnw": [hgrn_norm_w[l][None] for l in layers], "cw": conv_full,
        "cb": [ffn_conv_b[l].reshape(N_FFK, 1, FF_SLOT) for l in layers], "fw": final_norm_w[None],
    }
    long = {v: k for k, v in short.items()}
    landing, sent = {}, []

    def publish(l, part, grads):
        keys = [long[k] for k in grads]
        slots = [a.reshape((N_DEV, -1, a.shape[-1])) for a in grads.values()]
        zones = [place_own(s, landing.get(k, jax.ShapeDtypeStruct((N_DEV, n_layers) + s.shape[1:], s.dtype)), me1, l, True,
                           f"scatter_own_{short[k]}_{l}") for k, s in zip(keys, slots)]
        send, recv, srcs, zones, token = exchange_start(slots, zones, l, True, f"scatter_grads_start_{part}_{l}")
        landing.update(zip(keys, zones))
        sent.append((keys, l, part, send, recv, srcs, token))
        return token[0, 0]

    out = {}
    flat2 = lambda a: a.reshape(-1, a.shape[-1])

    def finish(part, after):
        done = []
        for keys, l, p, send, recv, srcs, _ in sent:
            if p == part:
                zones = exchange_wait(send, recv, srcs, [landing[k] for k in keys], l, True, after,
                                      f"scatter_grads_wait_{part}_{l}")
                landing.update(zip(keys, zones))
                done = keys
        for k in done:
            r = landing[k]
            res = adamw(flat2(wts[k]), flat2(mom1[k]), flat2(mom2[k]), r.reshape(N_DEV, -1, r.shape[-1]), "adamw_" + k)
            out[k] = tuple(a.reshape(wts[k].shape) for a in res)

    rep_rows = -(-sum(_rows_of(wts[k].size) for k in REPLICATED) // 64) * 64
    conv_rows = _rows_of(n_layers * 9 * D_FF)
    dmod_rows = _rows_of(n_layers * 6 * D)
    early = {}

    def small_early(loss_part, g, dmod, dlb):
        d_hlb = lower_bounds_bwd(hgrn_lower_bounds, dlb[1].reshape(1, 2 * D))
        st = lambda k: jnp.stack([jnp.zeros((1, D), F32) if a is None else a for a in g[k]])
        rep_grads = {"norm1_w": st("nw1"), "sgu_ln_w": st("lnw"), "sgu_ln_b": st("lnb"), "sgu_w": st("sw"), "sgu_b": st("sb"),
                     "hgrn_lower_bounds": d_hlb, "hgrn_norm_w": st("hnw"), "norm2_w": st("nw2"), "ffn_conv_b": st("cb"),
                     "final_norm_w": g["fw"]}
        d_conv = jnp.stack([g["cw"][l].transpose(1, 0, 2).reshape(9, D_FF) for l in layers])
        dmod_x = jnp.stack([dmod[l][1].reshape(6 * D) for l in layers])
        dmod_c = jnp.stack([dmod[l][0].reshape(6 * D) for l in layers])
        small = jnp.concatenate([_pack([rep_grads[k] for k in REPLICATED], rep_rows),
                                 _pack([d_conv, dmod_x, dmod_c, loss_part.reshape(1)])], axis=0)
        zone = place_own(small, jax.ShapeDtypeStruct((N_DEV,) + small.shape, F32), me1, None, False, "gather_small_own")
        early["send"], early["recv"], _, early["zones"], token = exchange_start([], [zone], None, False, "gather_small_start")
        return token[0, 0]

    def small_ready(loss_part, g, dmod, dlb):
        late = _pack([g["nw1"][0], dmod[0][1, 0:2], dmod[0][0, 0:2]])
        for part in ("ffn", "mix"):
            finish(part, late)
        (late_all,) = all_gather([late], "gather_small_late", after=out["w_out"][0])
        (small_all,) = exchange_wait(early["send"], early["recv"], [], early["zones"], None, False, late_all,
                                     "gather_small_wait")
        at_x = rep_rows + conv_rows
        small_all = small_all.at[:, 0:1].set(late_all[:, 0:1])
        small_all = small_all.at[:, at_x:at_x + 2].set(late_all[:, 8:10])
        small_all = small_all.at[:, at_x + dmod_rows:at_x + dmod_rows + 2].set(late_all[:, 16:18])
        d_conv_shape, dmod_shape = (n_layers, 9, D_FF), (n_layers, 6 * D)
        conv_g, dmx_all, dmc_all, loss_all = _unpack(small_all[:, rep_rows:], [d_conv_shape, dmod_shape, dmod_shape, (1,)])
        out["loss"] = functools.reduce(lambda a, b: a + b, [loss_all[k, 0] for k in range(N_DEV)])

        rep = adamw(_pack([wts[k] for k in REPLICATED], rep_rows), _pack([mom1[k] for k in REPLICATED], rep_rows),
                    _pack([mom2[k] for k in REPLICATED], rep_rows), small_all[:, :rep_rows], "adamw_replicated")
        rep = [_unpack(r, [wts[k].shape for k in REPLICATED]) for r in rep]
        for n, k in enumerate(REPLICATED):
            out[k] = tuple(r[n] for r in rep)

        conv_mine = lax.dynamic_index_in_dim(conv_g.reshape(N_DEV, n_layers, 9, N_DEV, -1), me, axis=3, keepdims=False)
        res = adamw(flat2(ffn_conv_w), flat2(m_ffn_conv_w), flat2(v_ffn_conv_w),
                    conv_mine.reshape(N_DEV, -1, conv_mine.shape[-1]), "adamw_conv_w")
        out["ffn_conv_w"] = tuple(r.reshape(ffn_conv_w.shape) for r in res)

        out["ada_b"] = tuple(adamw(ada_b, m_ada_b, v_ada_b, jnp.concatenate([dmx_all, dmc_all], axis=0), "adamw_ada_b"))

        cols_of = lambda a: lax.dynamic_slice_in_dim(a, me * ada_cols, ada_cols, axis=2).transpose(1, 0, 2)
        d_ada_w, d_cctx = ada_bwd(c_all, cctx8, ada_w, ada_b_cols, cols_of(dmx_all), cols_of(dmc_all))
        res = adamw(flat2(ada_w), flat2(m_ada_w), flat2(v_ada_w), flat2(d_ada_w)[None], "adamw_ada_w")
        out["ada_w"] = tuple(r.reshape(ada_w.shape) for r in res)
        (d_cctx_all,) = all_gather([d_cctx], "gather_c_ctx_grad")
        res = adamw(c_ctx[None], m_c_ctx[None], v_c_ctx[None], d_cctx_all, "adamw_c_ctx")
        out["c_ctx"] = tuple(r[0] for r in res)
        return d_cctx_all

    _, grad_x, _, _, _, small_done = local_step(x[0], ctx[0], loss_target[0], mod, lb, w, fetch, publish, small_ready,
                                                small_early)
    loss = out["loss"]

    finish("in", small_done)
    return (loss, grad_x[None]) + tuple(out[k][n] for n in range(4) for k in WEIGHT_ORDER)
```

```python
import functools
import math

import jax
import jax.numpy as jnp
from jax import lax
from jax.experimental import pallas as pl
from jax.experimental.pallas import tpu as pltpu

F32 = jnp.float32
BF16 = jnp.bfloat16
HIGHEST = lax.Precision.HIGHEST

N_DEV = 8
AXES = ("x", "y", "c")
D = 1024
CTX = 256
TM = 256
CH = 64
SGU_CH = 128
HEADS = 8
HD = 128
GRID_W = 64
D_IN = 9 * D
IN_SLOT = D_IN // N_DEV
D_FF = 2816
FF_SLOT = 2 * D_FF // N_DEV
N_FFK = D_FF // FF_SLOT
RMS_EPS = 1e-6
LN_EPS = 1e-5
ADAM_LR, ADAM_B1, ADAM_B2, ADAM_EPS, ADAM_WD, ADAM_STEP = 0.001, 0.9, 0.999, 1e-08, 0.01, 10
VMEM_LIMIT_V7X = 56 * 2 ** 20
GRAD_WIRE = jnp.bfloat16

VMEM_WHOLE = pl.BlockSpec(memory_space=pltpu.VMEM)
ANY = pl.BlockSpec(memory_space=pl.ANY)


def _cp(n_axes):
    return pltpu.CompilerParams(dimension_semantics=("arbitrary",) * n_axes, vmem_limit_bytes=VMEM_LIMIT_V7X)


def _dot(a, b, dims):
    return lax.dot_general(a.astype(BF16), b.astype(BF16), (dims, ((), ())), preferred_element_type=F32)


@jax.custom_vjp
def mm(a, b):
    return _dot(a, b, ((1,), (0,)))


mm.defvjp(lambda a, b: (mm(a, b), (a, b)),
          lambda r, g: (_dot(g, r[1], ((1,), (1,))).astype(r[0].dtype), _dot(r[0], g, ((0,), (0,))).astype(r[1].dtype)))


@jax.custom_vjp
def mm_nt(a, b):
    return _dot(a, b, ((1,), (1,)))


mm_nt.defvjp(lambda a, b: (mm_nt(a, b), (a, b)),
             lambda r, g: (_dot(g, r[1], ((1,), (0,))).astype(r[0].dtype), _dot(g, r[0], ((0,), (0,))).astype(r[1].dtype)))


@jax.custom_vjp
def mm_tn(a, b):
    return _dot(a, b, ((0,), (0,)))


mm_tn.defvjp(lambda a, b: (mm_tn(a, b), (a, b)),
             lambda r, g: (_dot(r[1], g, ((1,), (1,))).astype(r[0].dtype), _dot(r[0], g, ((1,), (0,))).astype(r[1].dtype)))


def _tri_dot(m, g):
    hi = g.astype(BF16)
    rest = g - hi.astype(F32)
    mid = rest.astype(BF16)
    low = (rest - mid.astype(F32)).astype(BF16)
    n = g.shape[1]
    out = jnp.dot(m.astype(BF16), jnp.concatenate([hi, mid, low], axis=1), preferred_element_type=F32)
    return out[:, :n] + out[:, n:2 * n] + out[:, 2 * n:]


@jax.custom_vjp
def _cum(m, mt, g):
    return _tri_dot(m, g)


_cum.defvjp(lambda m, mt, g: (_cum(m, mt, g), (m, mt)),
            lambda r, d: (jnp.zeros_like(r[0]), jnp.zeros_like(r[1]), _tri_dot(r[1], d)))


def _silu(x):
    return x * jax.nn.sigmoid(x)


def _gelu(x):
    return 0.5 * x * (1.0 + jnp.tanh(math.sqrt(2.0 / math.pi) * (x + 0.044715 * (x * x * x))))


def _rms(x, w):
    return x * lax.rsqrt(jnp.mean(x * x, axis=-1, keepdims=True) + RMS_EPS) * w


def _norm_mod(x, w, shift, scale):
    return _rms(x, w) * (1.0 + scale) + shift


def _hsl(h):
    return slice(h * HD, (h + 1) * HD)


def _hgrn_chunk(st, qz, fz, iv, lb, m, mt, mref):
    hs = range(HEADS)
    keep = [1.0 - lb[h] for h in hs]
    g = [jnp.log(lb[h] + keep[h] * jax.nn.sigmoid(fz[h])) for h in hs]
    k = [keep[h] * jax.nn.sigmoid(-fz[h]) for h in hs]
    q = [_silu(qz[h]) for h in hs]
    b = [_cum(m, mt, g[h]) for h in hs]
    ref = [jnp.sum(mref * g[h], axis=0, keepdims=True) for h in hs]
    last = [jnp.sum(g[h], axis=0, keepdims=True) for h in hs]
    qa = [q[h] * jnp.exp(b[h] - ref[h]) for h in hs]
    ka = [k[h] * jnp.exp(ref[h] - b[h]) for h in hs]
    scores = [jnp.where(m > 0.5, mm_nt(qa[h], ka[h]), 0.0) for h in hs]
    inter = [mm_nt(qa[h] * jnp.exp(ref[h]), st[h]) for h in hs]
    kv = [mm_tn(iv[h], ka[h] * jnp.exp(last[h] - ref[h])) for h in hs]
    outs = [mm(scores[h], iv[h]) + inter[h] for h in hs]
    news = [jnp.exp(last[h]) * st[h] + kv[h] for h in hs]
    return outs, news


def _sgu_fn(ub, vb, lnw, lnb, sw, sb):
    gv = [_gelu(v) for v in vb]
    mu = sum(jnp.sum(t, axis=-1, keepdims=True) for t in gv) / D
    var = sum(jnp.sum((t - mu) * (t - mu), axis=-1, keepdims=True) for t in gv) / D
    inv = lax.rsqrt(var + LN_EPS)
    cols = []
    for g in range(HEADS):
        vn = (gv[g] - mu) * inv * lnw[g] + lnb[g]
        cols.append(_gelu(ub[g]) * (mm(sw[g], vn) + sb[g]))
    return jnp.concatenate(cols, axis=1)


def _readout_fn(ob, og, hnw):
    r = [o * lax.rsqrt(jnp.mean(o * o, axis=-1, keepdims=True) + RMS_EPS) * hnw for o in ob]
    return jnp.concatenate(r, axis=1) * _silu(og)


def _glu_fn(ac, v):
    return _gelu(ac) * v


def _stream_row(tm):
    n_ctx = CTX // tm
    return lambda i: (jnp.where(i < n_ctx, 0, 1), 0, 0, 0)


def in_proj_fwd(x, mod, nw, wg):
    t = x.shape[0]

    def body(x_ref, mod_ref, nw_ref, w_ref, out_ref, ht_ref):
        h32 = _norm_mod(x_ref[...], nw_ref[...], mod_ref[0, 0], mod_ref[0, 1])
        ht_ref[...] = h32.T.astype(BF16)
        h = h32.astype(BF16)
        for j in range(N_DEV):
            out_ref[:, j * IN_SLOT:(j + 1) * IN_SLOT] = jnp.dot(h, w_ref[j], preferred_element_type=F32)

    return pl.pallas_call(
        body, name="in_proj_fwd", grid=(t // TM,),
        in_specs=[pl.BlockSpec((TM, D), lambda i: (i, 0)), pl.BlockSpec((1, 6, 1, D), _stream_row(TM)),
                  pl.BlockSpec((1, D), lambda i: (0, 0)), VMEM_WHOLE],
        out_specs=[pl.BlockSpec((TM, D_IN), lambda i: (i, 0)), pl.BlockSpec((D, TM), lambda i: (0, i))],
        out_shape=[jax.ShapeDtypeStruct((t, D_IN), F32), jax.ShapeDtypeStruct((D, t), BF16)],
        compiler_params=_cp(1))(x, mod, nw, wg)


def _scan_chunk(nc):
    ncc = CTX // CH

    def chunk(d, s):
        bwd = jnp.where(s < ncc, ncc - 1 - s, nc + ncc - 1 - s)
        return jnp.where(d == 0, s, bwd)
    return chunk


def hgrn_fwd(parts, lb, mc, mtc, mrefc):
    t = parts.shape[0]
    nc = t // CH
    chunk = _scan_chunk(nc)

    def body(q_ref, f_ref, i_ref, lb_ref, m_ref, mt_ref, mr_ref, o_ref, ck_ref, st):
        @pl.when(pl.program_id(1) == 0)
        def _():
            st[...] = jnp.zeros_like(st)
        ck_ref[0, 0] = st[...]
        outs, news = _hgrn_chunk([st[h] for h in range(HEADS)], [q_ref[:, _hsl(h)] for h in range(HEADS)],
                                 [f_ref[:, _hsl(h)] for h in range(HEADS)], [i_ref[:, _hsl(h)] for h in range(HEADS)],
                                 [lb_ref[0, :, _hsl(h)] for h in range(HEADS)], m_ref[0], mt_ref[0], mr_ref[0])
        for h in range(HEADS):
            o_ref[0, :, _hsl(h)] = outs[h]
            st[h] = news[h]

    const = lambda d, s: (d, 0, 0)
    return pl.pallas_call(
        body, name="hgrn_fwd", grid=(2, nc),
        in_specs=[pl.BlockSpec((CH, D), lambda d, s: (chunk(d, s), 0)), pl.BlockSpec((CH, D), lambda d, s: (chunk(d, s), 1 + d)),
                  pl.BlockSpec((CH, D), lambda d, s: (chunk(d, s), 3)), pl.BlockSpec((1, 1, D), const),
                  pl.BlockSpec((1, CH, CH), const), pl.BlockSpec((1, CH, CH), const), pl.BlockSpec((1, CH, 1), const)],
        out_specs=[pl.BlockSpec((1, CH, D), lambda d, s: (d, chunk(d, s), 0)),
                   pl.BlockSpec((1, 1, HEADS, HD, HD), lambda d, s: (d, s, 0, 0, 0))],
        out_shape=[jax.ShapeDtypeStruct((2, t, D), F32), jax.ShapeDtypeStruct((2, nc, HEADS, HD, HD), F32)],
        scratch_shapes=[pltpu.VMEM((HEADS, HD, HD), F32)], compiler_params=_cp(2))(parts, parts, parts, lb, mc, mtc, mrefc)


def _mixer_tile(rows, u_ref, v_ref, og_ref, o_ref, lnw_ref, lnb_ref, sw_ref, sb_ref, hnw_ref):
    n = (rows.stop - rows.start) // SGU_CH
    yas, vjps = [], []
    for c in range(n):
        r = slice(rows.start + c * SGU_CH, rows.start + (c + 1) * SGU_CH)
        ya, vjp_a = jax.vjp(_sgu_fn, [u_ref[r, _hsl(g)] for g in range(HEADS)], [v_ref[r, _hsl(g)] for g in range(HEADS)],
                            [lnw_ref[:, _hsl(g)] for g in range(HEADS)], [lnb_ref[:, _hsl(g)] for g in range(HEADS)],
                            [sw_ref[g] for g in range(HEADS)], [sb_ref[g] for g in range(HEADS)])
        yas.append(ya)
        vjps.append(vjp_a)
    yb, vjp_b = jax.vjp(_readout_fn, [o_ref[0, rows, _hsl(h)] + o_ref[1, rows, _hsl(h)] for h in range(HEADS)],
                        og_ref[rows, :], hnw_ref[...])
    return (yas[0] if n == 1 else jnp.concatenate(yas, axis=0)), yb, vjps, vjp_b


def _part_specs(tm, first, n):
    return [pl.BlockSpec((tm, D), functools.partial(lambda k, i: (i, k), first + k)) for k in range(n)]


def _unless_ctx(skip_ctx, is_ctx, zero_refs, work):
    if not skip_ctx:
        return work()

    @pl.when(is_ctx)
    def _():
        for r in zero_refs:
            r[...] = jnp.zeros_like(r)

    pl.when(jnp.logical_not(is_ctx))(work)


def mixer_fwd(x, parts, o, mod, lnw, lnb, sw, sb, hnw, wa, wb, wo, skip_ctx):
    t = x.shape[0]

    def body(x_ref, u_ref, v_ref, og_ref, ga_ref, gb_ref, o_ref, mod_ref, lnw_ref, lnb_ref, sw_ref, sb_ref, hnw_ref,
             wa_ref, wb_ref, wo_ref, out_ref, pa_ref, pb_ref, y_ref, yat_ref, ybt_ref, mt_ref):
        def work():
            ya, yb, _, _ = _mixer_tile(slice(0, TM), u_ref, v_ref, og_ref, o_ref, lnw_ref, lnb_ref, sw_ref, sb_ref, hnw_ref)
            pa, pb = mm(ya, wa_ref[...]), mm(yb, wb_ref[...])
            merged = jax.nn.sigmoid(ga_ref[...]) * pa + jax.nn.sigmoid(gb_ref[...]) * pb
            y = mm(merged, wo_ref[...])
            out_ref[...] = x_ref[...] + mod_ref[0, 2] * y
            pa_ref[...], pb_ref[...], y_ref[...] = pa.astype(BF16), pb.astype(BF16), y.astype(BF16)
            yat_ref[...], ybt_ref[...], mt_ref[...] = ya.T.astype(BF16), yb.T.astype(BF16), merged.T.astype(BF16)

        _unless_ctx(skip_ctx, pl.program_id(0) == 0, (out_ref, pa_ref, pb_ref, y_ref, yat_ref, ybt_ref, mt_ref), work)

    vec = lambda n: pl.BlockSpec((1, n), lambda i: (0, 0))
    tile = pl.BlockSpec((TM, D), lambda i: (i, 0))
    tile_t = pl.BlockSpec((D, TM), lambda i: (0, i))
    return pl.pallas_call(
        body, name="mixer_fwd", grid=(t // TM,),
        in_specs=[tile] + _part_specs(TM, 4, 5)
        + [pl.BlockSpec((2, TM, D), lambda i: (0, i, 0)), pl.BlockSpec((1, 6, 1, D), _stream_row(TM)), vec(D), vec(D),
           VMEM_WHOLE, VMEM_WHOLE, vec(HD), VMEM_WHOLE, VMEM_WHOLE, VMEM_WHOLE],
        out_specs=[tile] * 4 + [tile_t] * 3,
        out_shape=[jax.ShapeDtypeStruct((t, D), F32)] + [jax.ShapeDtypeStruct((t, D), BF16)] * 3
        + [jax.ShapeDtypeStruct((D, t), BF16)] * 3, compiler_params=_cp(1),
    )(x, parts, parts, parts, parts, parts, o, mod, lnw, lnb, sw, sb, hnw, wa, wb, wo)


def ffn_up_fwd(x, mod, nw, wg, skip_ctx):
    t = x.shape[0]

    def body(x_ref, mod_ref, nw_ref, w_ref, out_ref, ht_ref):
        def work():
            h32 = _norm_mod(x_ref[...], nw_ref[...], mod_ref[0, 3], mod_ref[0, 4])
            ht_ref[...] = h32.T.astype(BF16)
            h = h32.astype(BF16)
            for j in range(N_DEV):
                out_ref[j] = jnp.dot(h, w_ref[j], preferred_element_type=F32)

        _unless_ctx(skip_ctx, pl.program_id(0) == 0, (out_ref, ht_ref), work)

    return pl.pallas_call(
        body, name="ffn_up_fwd", grid=(t // TM,),
        in_specs=[pl.BlockSpec((TM, D), lambda i: (i, 0)), pl.BlockSpec((1, 6, 1, D), _stream_row(TM)),
                  pl.BlockSpec((1, D), lambda i: (0, 0)), VMEM_WHOLE],
        out_specs=[pl.BlockSpec((N_DEV, TM, FF_SLOT), lambda i: (0, i, 0)), pl.BlockSpec((D, TM), lambda i: (0, i))],
        out_shape=[jax.ShapeDtypeStruct((N_DEV, t, FF_SLOT), F32), jax.ShapeDtypeStruct((D, t), BF16)],
        compiler_params=_cp(1))(x, mod, nw, wg)


def _halo_specs(nt, k_of, i_of):
    per = TM // GRID_W
    last = nt * per - 1
    return [pl.BlockSpec((1, GRID_W, FF_SLOT), lambda *g: (k_of(*g), jnp.maximum(i_of(*g) * per - 1, 0), 0)),
            pl.BlockSpec((1, TM, FF_SLOT), lambda *g: (k_of(*g), i_of(*g), 0)),
            pl.BlockSpec((1, GRID_W, FF_SLOT), lambda *g: (k_of(*g), jnp.minimum(i_of(*g) * per + per, last), 0))]


def _with_halo(prev_ref, main_ref, next_ref, i, nt):
    prev = jnp.where(i >= 2, prev_ref[0], 0.0)
    nxt = jnp.where((i >= 1) & (i <= nt - 2), next_ref[0], 0.0)
    return jnp.concatenate([prev, main_ref[0], nxt], axis=0)


def _tap_valid(dc, i, n_rows, offset):
    r = lax.broadcasted_iota(jnp.int32, (n_rows, 1), 0) - offset
    col = jnp.bitwise_and(r, GRID_W - 1)
    pos = jnp.where(i == 0, r, col) + dc
    return (pos >= 0) & (pos < jnp.where(i == 0, TM, GRID_W))


def _row_weight(cw_ref, dr, dc, i):
    w = cw_ref[0, 3 * (dr + 1) + dc + 1:3 * (dr + 1) + dc + 2, :]
    return w if dr == 0 else jnp.where(i == 0, 0.0, w)


def ffn_down_fwd(x, av, mod, cw, cb, wd, skip_ctx):
    t = x.shape[0]
    nt = t // TM
    ext = TM + 2 * GRID_W

    def body(x_ref, ap_ref, am_ref, an_ref, v_ref, mod_ref, cw_ref, cb_ref, wd_ref, out_ref, ac_ref, y_ref, z_ref, acc):
        i, k = pl.program_id(0), pl.program_id(1)

        def work():
            a_ext = _with_halo(ap_ref, am_ref, an_ref, i, nt)
            conv = jnp.zeros((TM, FF_SLOT), F32) + cb_ref[0]
            for dc in (-1, 0, 1):
                rolled = (a_ext if dc == 0 else
                          jnp.where(_tap_valid(dc, i, ext, GRID_W), pltpu.roll(a_ext, (-dc) % ext, 0), 0.0))
                for dr in (-1, 0, 1):
                    lo = GRID_W + GRID_W * dr
                    conv = conv + rolled[lo:lo + TM] * _row_weight(cw_ref, dr, dc, i)
            ac_ref[0] = conv
            z = _glu_fn(conv, v_ref[0]).astype(BF16)
            z_ref[0] = z
            part = mm(z, wd_ref[0])

            @pl.when(k == 0)
            def _():
                acc[...] = part

            @pl.when(k > 0)
            def _():
                acc[...] += part

            @pl.when(k == N_FFK - 1)
            def _():
                y_ref[...] = acc[...]
                out_ref[...] = x_ref[...] + mod_ref[0, 5] * acc[...]

        _unless_ctx(skip_ctx, i == 0, (out_ref, ac_ref, y_ref, z_ref), work)

    tile = pl.BlockSpec((TM, D), lambda i, k: (i, 0))
    return pl.pallas_call(
        body, name="ffn_down_fwd", grid=(nt, N_FFK),
        in_specs=[tile] + _halo_specs(nt, lambda i, k: k, lambda i, k: i)
        + [pl.BlockSpec((1, TM, FF_SLOT), lambda i, k: (N_FFK + k, i, 0)),
           pl.BlockSpec((1, 6, 1, D), lambda i, k: (jnp.where(i < 1, 0, 1), 0, 0, 0)),
           pl.BlockSpec((1, 9, FF_SLOT), lambda i, k: (k, 0, 0)), pl.BlockSpec((1, 1, FF_SLOT), lambda i, k: (k, 0, 0)),
           pl.BlockSpec((1, FF_SLOT, D), lambda i, k: (k, 0, 0))],
        out_specs=[tile, pl.BlockSpec((1, TM, FF_SLOT), lambda i, k: (k, i, 0)), tile,
                   pl.BlockSpec((1, TM, FF_SLOT), lambda i, k: (k, i, 0))],
        out_shape=[jax.ShapeDtypeStruct((t, D), F32), jax.ShapeDtypeStruct((N_FFK, t, FF_SLOT), F32),
                   jax.ShapeDtypeStruct((t, D), F32), jax.ShapeDtypeStruct((N_FFK, t, FF_SLOT), BF16)],
        scratch_shapes=[pltpu.VMEM((TM, D), F32)], compiler_params=_cp(2))(x, av, av, av, av, mod, cw, cb, wd)


def loss_fwd_bwd(x, target, fw):
    t = x.shape[0]

    def body(x_ref, t_ref, w_ref, loss_ref, dx_ref, dw_ref):
        i = pl.program_id(0)

        @pl.when(i == 0)
        def _():
            loss_ref[...] = jnp.zeros_like(loss_ref)
            dw_ref[...] = jnp.zeros_like(dw_ref)
            dx_ref[...] = jnp.zeros_like(dx_ref)

        @pl.when(i > 0)
        def _():
            y, vjp = jax.vjp(_rms, x_ref[...], w_ref[...])
            err = y - t_ref[...]
            loss_ref[...] += 0.5 * jnp.sum(jnp.sum(err * err, axis=-1, keepdims=True) / D)
            dx, dw = vjp(err / D)
            dx_ref[...] = dx
            dw_ref[...] += dw

    return pl.pallas_call(
        body, name="loss_fwd_bwd", grid=(t // TM,),
        in_specs=[pl.BlockSpec((TM, D), lambda i: (i, 0)), pl.BlockSpec((TM, D), lambda i: (jnp.maximum(i - 1, 0), 0)),
                  pl.BlockSpec((1, D), lambda i: (0, 0))],
        out_specs=[pl.BlockSpec((8, 128), lambda i: (0, 0)), pl.BlockSpec((TM, D), lambda i: (i, 0)),
                   pl.BlockSpec((1, D), lambda i: (0, 0))],
        out_shape=[jax.ShapeDtypeStruct((8, 128), F32), jax.ShapeDtypeStruct((t, D), F32), jax.ShapeDtypeStruct((1, D), F32)],
        compiler_params=_cp(1))(x, target, fw)


def _stream_add(ref, k, is_ctx, val):
    ref[0, k] += jnp.where(is_ctx, val, 0.0)
    ref[1, k] += jnp.where(is_ctx, 0.0, val)


def ffn_down_bwd(dx, ac, av, y, mod, wd, skip_ctx):
    t = dx.shape[0]
    nt = t // TM

    def body(dx_ref, ac_ref, v_ref, y_ref, mod_ref, wd_ref, dav_ref, dac_ref, dout_ref, dg_ref):
        i = pl.program_id(0)

        @pl.when(i == 0)
        def _():
            dg_ref[...] = jnp.zeros_like(dg_ref)

        def work():
            _stream_add(dg_ref, 0, i == 0, jnp.sum(dx_ref[...] * y_ref[...], axis=0, keepdims=True))
            dout = (mod_ref[0, 5] * dx_ref[...]).astype(BF16)
            dout_ref[...] = dout
            for k in range(N_FFK):
                _, vjp = jax.vjp(_glu_fn, ac_ref[k], v_ref[k])
                dac, dv = vjp(mm_nt(dout, wd_ref[k]))
                dac_ref[k] = dac
                dav_ref[k] = dv.astype(BF16)

        _unless_ctx(skip_ctx, i == 0, (dav_ref, dac_ref, dout_ref), work)

    tile = pl.BlockSpec((TM, D), lambda i: (i, 0))
    half = lambda first: pl.BlockSpec((N_FFK, TM, FF_SLOT), lambda i: (first, i, 0))
    return pl.pallas_call(
        body, name="ffn_down_bwd", grid=(nt,),
        in_specs=[tile, half(0), half(1), tile, pl.BlockSpec((1, 6, 1, D), _stream_row(TM)), VMEM_WHOLE],
        out_specs=[half(1), half(0), tile, pl.BlockSpec((2, 1, 1, D), lambda i: (0, 0, 0, 0))],
        out_shape=[jax.ShapeDtypeStruct((N_DEV, t, FF_SLOT), BF16), jax.ShapeDtypeStruct((N_FFK, t, FF_SLOT), F32),
                   jax.ShapeDtypeStruct((t, D), BF16), jax.ShapeDtypeStruct((2, 1, 1, D), F32)],
        compiler_params=_cp(1))(dx, ac, av, y, mod, wd)


def conv_bwd(dav, dac, av, cw, skip_ctx):
    t = dac.shape[1]
    nt = t // TM
    ext = TM + 2 * GRID_W

    def body(dav_in, gp_ref, gm_ref, gn_ref, ap_ref, am_ref, an_ref, cw_ref, dav_ref, dcw_ref, dcb_ref):
        k, i = pl.program_id(0), pl.program_id(1)

        @pl.when(i == 0)
        def _():
            dcw_ref[...] = jnp.zeros_like(dcw_ref)
            dcb_ref[...] = jnp.zeros_like(dcb_ref)

        def work():
            g_ext = _with_halo(gp_ref, gm_ref, gn_ref, i, nt)
            a_ext = _with_halo(ap_ref, am_ref, an_ref, i, nt)
            g_main = gm_ref[0]
            dcb_ref[0] += jnp.sum(g_main, axis=0, keepdims=True)
            da = jnp.zeros((TM, FF_SLOT), F32)
            for dc in (-1, 0, 1):
                g_rolled = (g_ext if dc == 0 else
                            pltpu.roll(jnp.where(_tap_valid(dc, i, ext, GRID_W), g_ext, 0.0), dc % ext, 0))
                a_rolled = a_ext if dc == 0 else pltpu.roll(a_ext, (-dc) % ext, 0)
                g_valid = g_main if dc == 0 else jnp.where(_tap_valid(dc, i, TM, 0), g_main, 0.0)
                for dr in (-1, 0, 1):
                    lo = GRID_W - GRID_W * dr
                    da = da + g_rolled[lo:lo + TM] * _row_weight(cw_ref, dr, dc, i)
                    lo = GRID_W + GRID_W * dr
                    tap = 3 * (dr + 1) + dc + 1
                    dw = jnp.sum(g_valid * a_rolled[lo:lo + TM], axis=0, keepdims=True)
                    dcw_ref[0, tap:tap + 1, :] += dw if dr == 0 else jnp.where(i == 0, 0.0, dw)
            dav_ref[0] = da.astype(BF16)

        _unless_ctx(skip_ctx, i == 0, (dav_ref,), work)

    return pl.pallas_call(
        body, name="conv_bwd", grid=(N_FFK, nt),
        in_specs=[ANY] + _halo_specs(nt, lambda k, i: k, lambda k, i: i) + _halo_specs(nt, lambda k, i: k, lambda k, i: i)
        + [pl.BlockSpec((1, 9, FF_SLOT), lambda k, i: (k, 0, 0))],
        out_specs=[pl.BlockSpec((1, TM, FF_SLOT), lambda k, i: (k, i, 0)), pl.BlockSpec((1, 9, FF_SLOT), lambda k, i: (k, 0, 0)),
                   pl.BlockSpec((1, 1, FF_SLOT), lambda k, i: (k, 0, 0))],
        out_shape=[jax.ShapeDtypeStruct(dav.shape, BF16), jax.ShapeDtypeStruct((N_FFK, 9, FF_SLOT), F32),
                   jax.ShapeDtypeStruct((N_FFK, 1, FF_SLOT), F32)],
        input_output_aliases={0: 0}, compiler_params=_cp(2))(dav, dac, dac, dac, av, av, av, cw)


def _norm_mod_bwd(x_ref, nw_ref, mod_ref, k_shift, dh, dx_in, dx_ref, dnw_ref, dmod_ref, is_ctx):
    _, vjp = jax.vjp(_norm_mod, x_ref[...], nw_ref[...], mod_ref[0, k_shift], mod_ref[0, k_shift + 1])
    dx, dnw, dshift, dscale = vjp(dh)
    dx_ref[...] = dx_in + dx
    dnw_ref[...] += dnw
    _stream_add(dmod_ref, 0, is_ctx, dshift)
    _stream_add(dmod_ref, 1, is_ctx, dscale)


def ffn_up_bwd_x(dx2, x, dav, mod, nw, wg, skip_ctx):
    t = x.shape[0]

    def body(dx2_ref, x_ref, dav_ref, mod_ref, nw_ref, w_ref, dx_ref, dnw_ref, dmod_ref):
        i = pl.program_id(0)

        @pl.when(i == 0)
        def _():
            dnw_ref[...] = jnp.zeros_like(dnw_ref)
            dmod_ref[...] = jnp.zeros_like(dmod_ref)

        def work():
            dh = mm_nt(dav_ref[0], w_ref[0])
            for j in range(1, N_DEV):
                dh = dh + mm_nt(dav_ref[j], w_ref[j])
            _norm_mod_bwd(x_ref, nw_ref, mod_ref, 3, dh, dx2_ref[...], dx_ref, dnw_ref, dmod_ref, i == 0)

        _unless_ctx(skip_ctx, i == 0, (dx_ref,), work)

    tile = pl.BlockSpec((TM, D), lambda i: (i, 0))
    return pl.pallas_call(
        body, name="ffn_up_bwd_x", grid=(t // TM,),
        in_specs=[tile, tile, pl.BlockSpec((N_DEV, TM, FF_SLOT), lambda i: (0, i, 0)), pl.BlockSpec((1, 6, 1, D), _stream_row(TM)),
                  pl.BlockSpec((1, D), lambda i: (0, 0)), VMEM_WHOLE],
        out_specs=[tile, pl.BlockSpec((1, D), lambda i: (0, 0)), pl.BlockSpec((2, 2, 1, D), lambda i: (0, 0, 0, 0))],
        out_shape=[jax.ShapeDtypeStruct((t, D), F32), jax.ShapeDtypeStruct((1, D), F32), jax.ShapeDtypeStruct((2, 2, 1, D), F32)],
        compiler_params=_cp(1))(dx2, x, dav, mod, nw, wg)


def weight_grad(at, dout, slot, name, after=None):
    rows, t = at.shape
    stacked = dout.ndim == 3
    n = dout.shape[0] if stacked else dout.shape[1] // slot

    def body(a_ref, d_ref, *rest):
        dw_ref = rest[-1]
        dw_ref[0] = jnp.dot(a_ref[...], d_ref[0] if stacked else d_ref[...], preferred_element_type=F32).astype(dw_ref.dtype)

    d_spec = pl.BlockSpec((1, t, slot), lambda j: (j, 0, 0)) if stacked else pl.BlockSpec((t, slot), lambda j: (0, j))
    extra = [] if after is None else [jnp.reshape(after, (1, 1))]
    return pl.pallas_call(
        body, name=name, grid=(n,), in_specs=[VMEM_WHOLE, d_spec] + [ANY] * len(extra),
        out_specs=pl.BlockSpec((1, rows, slot), lambda j: (j, 0, 0)),
        out_shape=jax.ShapeDtypeStruct((n, rows, slot), GRAD_WIRE), compiler_params=_cp(1))(at, dout, *extra)


def weight_grad_rows(at, dout, name):
    n, t, rows = at.shape
    cols = dout.shape[1]

    def body(a_ref, d_ref, dw_ref):
        dw_ref[0] = _dot(a_ref[0], d_ref[...], ((0,), (0,))).astype(dw_ref.dtype)

    return pl.pallas_call(
        body, name=name, grid=(n,), in_specs=[pl.BlockSpec((1, t, rows), lambda k: (k, 0, 0)), VMEM_WHOLE],
        out_specs=pl.BlockSpec((1, rows, cols), lambda k: (k, 0, 0)),
        out_shape=jax.ShapeDtypeStruct((n, rows, cols), GRAD_WIRE), compiler_params=_cp(1))(at, dout)


def mixer_bwd(dx, parts, o, pa, pb, y, mod, lnw, lnb, sw, sb, hnw, wa, wb, wo, skip_ctx):
    t = dx.shape[0]
    tm = TM
    n_ctx = CTX // tm

    def body(dx_ref, u_ref, v_ref, og_ref, ga_ref, gb_ref, o_ref, pa_ref, pb_ref, y_ref, mod_ref, lnw_ref, lnb_ref, sw_ref,
             sb_ref, hnw_ref, wa_ref, wb_ref, wo_ref, dp_ref, do_ref, dy_ref, dpa_ref, dpb_ref, dlnw_ref, dlnb_ref, dsw_ref,
             dsb_ref, dhnw_ref, dg_ref):
        i = pl.program_id(0)

        @pl.when(i == 0)
        def _():
            for r in (dlnw_ref, dlnb_ref, dsw_ref, dsb_ref, dhnw_ref, dg_ref):
                r[...] = jnp.zeros_like(r)

        def work():
            _, _, vjps, vjp_b = _mixer_tile(slice(0, tm), u_ref, v_ref, og_ref, o_ref, lnw_ref, lnb_ref, sw_ref, sb_ref, hnw_ref)
            pa, pb = pa_ref[...].astype(F32), pb_ref[...].astype(F32)
            sa, sbg = jax.nn.sigmoid(ga_ref[...]), jax.nn.sigmoid(gb_ref[...])
            dxv = dx_ref[...]
            _stream_add(dg_ref, 0, i < n_ctx, jnp.sum(dxv * y_ref[...].astype(F32), axis=0, keepdims=True))
            dy = (mod_ref[0, 2] * dxv).astype(BF16)
            dy_ref[...] = dy
            dmerged = mm_nt(dy, wo_ref[...])
            dpa, dpb = (sa * dmerged).astype(BF16), (sbg * dmerged).astype(BF16)
            dpa_ref[...], dpb_ref[...] = dpa, dpb
            first = 4 * D
            dp_ref[:, first + 3 * D:first + 4 * D] = (dmerged * pa * sa * (1.0 - sa)).astype(BF16)
            dp_ref[:, first + 4 * D:first + 5 * D] = (dmerged * pb * sbg * (1.0 - sbg)).astype(BF16)
            dya = mm_nt(dpa, wa_ref[...])
            dob, dog, dhnw = vjp_b(mm_nt(dpb, wb_ref[...]))
            dp_ref[:, first + 2 * D:first + 3 * D] = dog.astype(BF16)
            dhnw_ref[...] += dhnw
            for g in range(HEADS):
                do_ref[:, _hsl(g)] = dob[g]
            for c, vjp_a in enumerate(vjps):
                rows = slice(c * SGU_CH, (c + 1) * SGU_CH)
                dub, dvb, dlnw, dlnb, dsw, dsb = vjp_a(dya[rows])
                for g in range(HEADS):
                    dp_ref[rows, first + g * HD:first + (g + 1) * HD] = dub[g].astype(BF16)
                    dp_ref[rows, first + D + g * HD:first + D + (g + 1) * HD] = dvb[g].astype(BF16)
                    dlnw_ref[:, _hsl(g)] += dlnw[g]
                    dlnb_ref[:, _hsl(g)] += dlnb[g]
                    dsw_ref[g] += dsw[g]
                    dsb_ref[g] += dsb[g]

        _unless_ctx(skip_ctx, i < n_ctx, (dp_ref, do_ref, dy_ref, dpa_ref, dpb_ref), work)

    vec = lambda n: pl.BlockSpec((1, n), lambda i: (0, 0))
    tile = pl.BlockSpec((tm, D), lambda i: (i, 0))
    sds = jax.ShapeDtypeStruct
    return pl.pallas_call(
        body, name="mixer_bwd", grid=(t // tm,),
        in_specs=[tile] + _part_specs(tm, 4, 5)
        + [pl.BlockSpec((2, tm, D), lambda i: (0, i, 0)), tile, tile, tile, pl.BlockSpec((1, 6, 1, D), _stream_row(tm)),
           vec(D), vec(D), VMEM_WHOLE, VMEM_WHOLE, vec(HD), VMEM_WHOLE, VMEM_WHOLE, VMEM_WHOLE],
        out_specs=[pl.BlockSpec((tm, D_IN), lambda i: (i, 0)), tile, tile, tile, tile, vec(D), vec(D),
                   VMEM_WHOLE, VMEM_WHOLE, vec(HD), pl.BlockSpec((2, 1, 1, D), lambda i: (0, 0, 0, 0))],
        out_shape=[sds((t, D_IN), BF16), sds((t, D), F32), sds((t, D), BF16), sds((t, D), BF16), sds((t, D), BF16),
                   sds((1, D), F32), sds((1, D), F32), sds((HEADS, SGU_CH, SGU_CH), F32), sds((HEADS, SGU_CH, 1), F32),
                   sds((1, HD), F32), sds((2, 1, 1, D), F32)],
        compiler_params=_cp(1))(dx, parts, parts, parts, parts, parts, o, pa, pb, y, mod, lnw, lnb, sw, sb, hnw, wa, wb, wo)


def hgrn_bwd(d, parts, lb, mc, mtc, mrefc, ck, do, first=None, dparts=None):
    t = parts.shape[0]
    nc = t // CH
    chunk = _scan_chunk(nc)
    rev = lambda s: chunk(d, nc - 1 - s)

    def body(q_ref, f_ref, i_ref, lb_ref, m_ref, mt_ref, mr_ref, ck_ref, do_ref, *rest):
        dst = rest[-1]
        dlb_ref = rest[-2]

        @pl.when(pl.program_id(0) == 0)
        def _():
            dst[...] = jnp.zeros_like(dst)
            dlb_ref[...] = jnp.zeros_like(dlb_ref)

        heads = range(HEADS)
        fn = functools.partial(_hgrn_chunk, m=m_ref[0], mt=mt_ref[0], mref=mr_ref[0])
        _, vjp = jax.vjp(fn, [ck_ref[0, 0, h] for h in heads], [q_ref[:, _hsl(h)] for h in heads],
                         [f_ref[:, _hsl(h)] for h in heads], [i_ref[:, _hsl(h)] for h in heads],
                         [lb_ref[0, :, _hsl(h)] for h in heads])
        dstl, dq, df, di, dlb = vjp(([do_ref[:, _hsl(h)] for h in heads], [dst[h] for h in heads]))
        for h in heads:
            dst[h] = dstl[h]
            dlb_ref[0, :, _hsl(h)] += dlb[h]
            if d == 0:
                dq_ref, df_ref, di_ref = rest[:3]
                dq_ref[:, _hsl(h)] = dq[h].astype(BF16)
                df_ref[:, _hsl(h)] = df[h].astype(BF16)
                di_ref[:, _hsl(h)] = di[h].astype(BF16)
            else:
                dq0_ref, df0_ref, di0_ref, _, dp_ref = rest[:5]
                col = lambda k: slice(k * D + h * HD, k * D + (h + 1) * HD)
                dp_ref[:, col(0)] = (dq0_ref[:, _hsl(h)].astype(F32) + dq[h]).astype(BF16)
                dp_ref[:, col(1)] = df0_ref[:, _hsl(h)]
                dp_ref[:, col(2)] = df[h].astype(BF16)
                dp_ref[:, col(3)] = (di0_ref[:, _hsl(h)].astype(F32) + di[h]).astype(BF16)

    const = lambda s: (d, 0, 0)
    at = lambda k: pl.BlockSpec((CH, D), lambda s: (rev(s), k))
    in_specs = [at(0), at(1 + d), at(3), pl.BlockSpec((1, 1, D), const), pl.BlockSpec((1, CH, CH), const),
                pl.BlockSpec((1, CH, CH), const), pl.BlockSpec((1, CH, 1), const),
                pl.BlockSpec((1, 1, HEADS, HD, HD), lambda s: (d, nc - 1 - s, 0, 0, 0)), at(0)]
    dlb_spec, dlb_shape = pl.BlockSpec((1, 1, D), lambda s: (0, 0, 0)), jax.ShapeDtypeStruct((1, 1, D), F32)
    common = dict(grid=(nc,), scratch_shapes=[pltpu.VMEM((HEADS, HD, HD), F32)], compiler_params=_cp(1))
    if d == 0:
        return pl.pallas_call(body, name="hgrn_bwd_fwd_dir", in_specs=in_specs, out_specs=[at(0)] * 3 + [dlb_spec],
                              out_shape=[jax.ShapeDtypeStruct((t, D), BF16)] * 3 + [dlb_shape], **common,
                              )(parts, parts, parts, lb, mc, mtc, mrefc, ck, do)
    return pl.pallas_call(body, name="hgrn_bwd_bwd_dir", in_specs=in_specs + [at(0)] * 3 + [ANY],
                          out_specs=[pl.BlockSpec((CH, 4 * D), lambda s: (rev(s), 0)), dlb_spec],
                          out_shape=[jax.ShapeDtypeStruct(dparts.shape, BF16), dlb_shape], input_output_aliases={12: 0},
                          **common)(parts, parts, parts, lb, mc, mtc, mrefc, ck, do, *first, dparts)


def in_proj_bwd_x(dx1, x, dparts, mod, nw, wg, after=None, latent_only=False):
    t = x.shape[0]
    tm = TM
    n_ctx = CTX // tm

    def body(dx1_ref, x_ref, dp_ref, mod_ref, nw_ref, w_ref, *rest):
        dx_ref, dnw_ref, dmod_ref = rest[-3:]
        i = pl.program_id(0)

        @pl.when(i == 0)
        def _():
            dnw_ref[...] = jnp.zeros_like(dnw_ref)
            dmod_ref[...] = jnp.zeros_like(dmod_ref)

        dh = mm_nt(dp_ref[:, 0:IN_SLOT], w_ref[0])
        for j in range(1, N_DEV):
            dh = dh + mm_nt(dp_ref[:, j * IN_SLOT:(j + 1) * IN_SLOT], w_ref[j])
        _norm_mod_bwd(x_ref, nw_ref, mod_ref, 0, dh, dx1_ref[...], dx_ref, dnw_ref, dmod_ref, i < n_ctx)

    tile = pl.BlockSpec((tm, D), lambda i: (i, 0))
    extra = [] if after is None else [jnp.reshape(after, (1, 1))]
    return pl.pallas_call(
        body, name="in_proj_bwd_x", grid=(t // tm,),
        in_specs=[tile, tile, pl.BlockSpec((tm, D_IN), lambda i: (i, 0)), pl.BlockSpec((1, 6, 1, D), _stream_row(tm)),
                  pl.BlockSpec((1, D), lambda i: (0, 0)), VMEM_WHOLE] + [ANY] * len(extra),
        out_specs=[pl.BlockSpec((tm, D), lambda i: (jnp.maximum(i - n_ctx, 0), 0)) if latent_only else tile,
                   pl.BlockSpec((1, D), lambda i: (0, 0)), pl.BlockSpec((2, 2, 1, D), lambda i: (0, 0, 0, 0))],
        out_shape=[jax.ShapeDtypeStruct((t - CTX if latent_only else t, D), F32), jax.ShapeDtypeStruct((1, D), F32),
                   jax.ShapeDtypeStruct((2, 2, 1, D), F32)],
        compiler_params=_cp(1))(dx1, x, dparts, mod, nw, wg, *extra)


def _lb_fn(h0, h1):
    m = jnp.maximum(h0, h1)
    e0, e1 = jnp.exp(h0 - m), jnp.exp(h1 - m)
    return e1 / (e0 + e1)


def lower_bounds(hlb):
    def body(h_ref, out_ref):
        out_ref[...] = _lb_fn(h_ref[0:1, :], h_ref[1:2, :])
    return pl.pallas_call(body, name="lower_bounds", out_shape=jax.ShapeDtypeStruct((1, 2 * D), F32))(hlb)


def lower_bounds_bwd(hlb, dlb1):
    def body(h_ref, d_ref, out_ref):
        _, vjp = jax.vjp(_lb_fn, h_ref[0:1, :], h_ref[1:2, :])
        d0, d1 = vjp(d_ref[...])
        out_ref[0:1, :] = d0
        out_ref[1:2, :] = d1
    return pl.pallas_call(body, name="lower_bounds_bwd", out_shape=jax.ShapeDtypeStruct((2, 2 * D), F32))(hlb, dlb1)


def _ada_fn(c_all, cctx8, w, b):
    dot = lambda a, l: jnp.dot(_silu(a), w[l], precision=HIGHEST, preferred_element_type=F32) + b[l]
    return [dot(c_all, l) for l in range(2)], [dot(cctx8, l) for l in range(2)]


def ada_fwd(c_all, cctx8, w, b):
    cols = w.shape[-1]

    def body(c_ref, cc_ref, w_ref, b_ref, out_ref):
        ox, oc = _ada_fn(c_ref[...], cc_ref[...], [w_ref[0], w_ref[1]], [b_ref[0], b_ref[1]])
        for l in range(2):
            out_ref[l, 0] = ox[l]
            out_ref[l, 1] = oc[l]
    return pl.pallas_call(body, name="ada_fwd", out_shape=jax.ShapeDtypeStruct((2, 2, N_DEV, cols), F32),
                          compiler_params=_cp(0))(c_all, cctx8, w, b)


def ada_bwd(c_all, cctx8, w, b, dmx, dmc):
    cols = w.shape[-1]

    def body(c_ref, cc_ref, w_ref, b_ref, dmx_ref, dmc_ref, dw_ref, dc_ref):
        fn = lambda cc, w0, w1: _ada_fn(c_ref[...], cc, [w0, w1], [b_ref[0], b_ref[1]])
        _, vjp = jax.vjp(fn, cc_ref[...], w_ref[0], w_ref[1])
        dcc, dw0, dw1 = vjp(([dmx_ref[0], dmx_ref[1]], [dmc_ref[0], dmc_ref[1]]))
        dw_ref[0] = dw0
        dw_ref[1] = dw1
        dc_ref[...] = jnp.sum(dcc, axis=0, keepdims=True)
    return pl.pallas_call(body, name="ada_bwd", out_shape=[jax.ShapeDtypeStruct((2, D, cols), F32), jax.ShapeDtypeStruct((1, D), F32)],
                          compiler_params=_cp(0))(c_all, cctx8, w, b, dmx, dmc)


def adamw(w, m, v, gparts, name):
    r, c = w.shape
    p = gparts.shape[0]
    rt = r
    while rt % 16 == 0 and (p + 7) * rt * c * 4 * 2 > 24 * 2 ** 20:
        rt //= 2

    def body(w_ref, m_ref, v_ref, g_ref, go_ref, d_ref, mo_ref, vo_ref):
        g = g_ref[0].astype(F32)
        for k in range(1, p):
            g = g + g_ref[k].astype(F32)
        m2 = ADAM_B1 * m_ref[...] + (1.0 - ADAM_B1) * g
        v2 = ADAM_B2 * v_ref[...] + (1.0 - ADAM_B2) * (g * g)
        m_hat = m2 / (1.0 - ADAM_B1 ** ADAM_STEP)
        v_hat = v2 / (1.0 - ADAM_B2 ** ADAM_STEP)
        go_ref[...] = g
        d_ref[...] = -ADAM_LR * (m_hat / (jnp.sqrt(v_hat) + ADAM_EPS) + ADAM_WD * w_ref[...])
        mo_ref[...] = m2
        vo_ref[...] = v2

    tile = pl.BlockSpec((rt, c), lambda i: (i, 0))
    return pl.pallas_call(
        body, name=name, grid=(r // rt,),
        in_specs=[tile, tile, tile, pl.BlockSpec((p, rt, c), lambda i: (0, i, 0))], out_specs=[tile] * 4,
        out_shape=[jax.ShapeDtypeStruct((r, c), F32)] * 4, compiler_params=_cp(1))(w, m, v, gparts)


def _me():
    x, y, c = lax.axis_index("x"), lax.axis_index("y"), lax.axis_index("c")
    return x, y, c, 4 * x + 2 * y + c


def _peer(x, y, c, p):
    fx, fy, fc = (p >> 2) & 1, (p >> 1) & 1, p & 1
    return (1 - x if fx else x, 1 - y if fy else y, 1 - c if fc else c)


def all_gather(arrs, name, after=None):
    n = len(arrs)
    extra = [] if after is None else list(after) if isinstance(after, (list, tuple)) else [after]

    def body(*refs):
        ins, outs = refs[:n], refs[n + len(extra):2 * n + len(extra)]
        send, recv, local = refs[2 * n + len(extra):]
        x, y, c, me = _me()
        copies = []
        for a in range(n):
            lc = pltpu.make_async_copy(ins[a], outs[a].at[me], local.at[a])
            lc.start()
            copies.append(lc)
            for p in range(1, N_DEV):
                cp = pltpu.make_async_remote_copy(src_ref=ins[a], dst_ref=outs[a].at[me], send_sem=send.at[a, p - 1],
                                                  recv_sem=recv.at[a, p - 1], device_id=_peer(x, y, c, p),
                                                  device_id_type=pl.DeviceIdType.MESH)
                cp.start()
                copies.append(cp)
        for cp in copies:
            cp.wait()

    return pl.pallas_call(
        body, name=name, in_specs=[ANY] * (n + len(extra)), out_specs=[ANY] * n,
        out_shape=[jax.ShapeDtypeStruct((N_DEV,) + a.shape, a.dtype) for a in arrs],
        scratch_shapes=[pltpu.SemaphoreType.DMA((n, N_DEV - 1)), pltpu.SemaphoreType.DMA((n, N_DEV - 1)),
                        pltpu.SemaphoreType.DMA((n,))])(*arrs, *extra)


HBM = pl.BlockSpec(memory_space=pltpu.HBM)
SEM = pl.BlockSpec(memory_space=pltpu.SEMAPHORE)


def _in_hbm(a):
    return pltpu.with_memory_space_constraint(a, pltpu.HBM)


ALL_PEERS = tuple(range(1, N_DEV))
SAME_CORE_AND_SIBLING = (1, 2, 4, 6)
OTHER_CHIPS = (2, 4, 6)


def _exchange_refs(srcs, lands, layer, scatter, a, x, y, c, p, forward=False):
    me = 4 * x + 2 * y + c
    px, py, pc = _peer(x, y, c, p) if p else (x, y, c)
    if forward and p:
        slot = lands[a].at[4 * px + 2 * py + pc]
        return slot, slot, _peer(x, y, c, 1)
    dst = lands[a].at[me] if layer is None else lands[a].at[me, layer]
    src = srcs[a].at[4 * px + 2 * py + pc] if scatter else dst
    return src, dst, (px, py, pc)


def exchange_start(srcs, lands, layer, scatter, name, after=None, peers=ALL_PEERS, forward=False):
    n, ns = len(lands), len(srcs)
    extra = [] if after is None else [after]

    def body(*refs):
        ins, lz = refs[:ns], refs[ns:ns + n]
        send, recv = refs[ns + n + len(extra)], refs[ns + n + len(extra) + 1]
        token = refs[-1]
        x, y, c, _ = _me()
        for a in range(n):
            for p in peers:
                src, dst, peer = _exchange_refs(ins, lz, layer, scatter, a, x, y, c, p, forward)
                k = a * (N_DEV - 1) + p - 1
                pltpu.make_async_remote_copy(src_ref=src, dst_ref=dst, send_sem=send.at[k], recv_sem=recv.at[k],
                                             device_id=peer, device_id_type=pl.DeviceIdType.MESH).start()
        token[...] = jnp.zeros_like(token)

    thru = [pltpu.HBM(a.shape, a.dtype) for a in list(srcs) + list(lands)]
    out = pl.pallas_call(
        body, name=name, in_specs=[HBM] * (ns + n) + [ANY] * len(extra),
        out_specs=[SEM, SEM] + [HBM] * (ns + n) + [pl.BlockSpec(memory_space=pltpu.VMEM)],
        out_shape=[pltpu.SemaphoreType.DMA((n * (N_DEV - 1),)), pltpu.SemaphoreType.DMA((n * (N_DEV - 1),))] + thru
        + [jax.ShapeDtypeStruct((8, 128), F32)],
        input_output_aliases={i: 2 + i for i in range(ns + n)},
        compiler_params=pltpu.CompilerParams(has_side_effects=pltpu.SideEffectType.DATAFLOW_SIDE_EFFECTING),
    )(*[_in_hbm(a) for a in list(srcs) + list(lands)], *extra)
    return out[0], out[1], out[2:2 + ns], out[2 + ns:2 + ns + n], out[-1]


def exchange_wait(send, recv, srcs, lands, layer, scatter, after, name, peers=ALL_PEERS):
    n, ns = len(lands), len(srcs)

    def body(*refs):
        ins, lz = refs[:ns], refs[ns:ns + n]
        send_ref, recv_ref = refs[ns + n], refs[ns + n + 1]
        x, y, c, _ = _me()
        for a in range(n):
            for p in peers:
                src, dst, peer = _exchange_refs(ins, lz, layer, scatter, a, x, y, c, 0)
                k = a * (N_DEV - 1) + p - 1
                cp = pltpu.make_async_remote_copy(src_ref=src, dst_ref=dst, send_sem=send_ref.at[k],
                                                  recv_sem=recv_ref.at[k], device_id=peer,
                                                  device_id_type=pl.DeviceIdType.MESH)
                cp.wait_send()
                cp.wait_recv()

    thru = [pltpu.HBM(a.shape, a.dtype) for a in list(srcs) + list(lands)]
    out = pl.pallas_call(
        body, name=name, in_specs=[HBM] * (ns + n) + [SEM, SEM, ANY], out_specs=[HBM] * (ns + n), out_shape=thru,
        input_output_aliases={i: i for i in range(ns + n)},
        compiler_params=pltpu.CompilerParams(has_side_effects=pltpu.SideEffectType.DATAFLOW_SIDE_EFFECTING),
    )(*srcs, *lands, send, recv, after)
    return out[ns:]


def place_own(src, land, me, layer, scatter, name):
    create = isinstance(land, jax.ShapeDtypeStruct)
    r, c = src.shape[-2:]
    rt = r
    while rt % 32 == 0 and rt * c * 4 > 2 ** 21:
        rt //= 2

    def body(me_ref, src_ref, *rest):
        out_ref = rest[-1]
        out_ref[...] = src_ref[...].reshape(out_ref.shape).astype(out_ref.dtype)

    src_spec = (pl.BlockSpec((1, rt, c), lambda i, m: (m[0], i, 0)) if scatter else pl.BlockSpec((rt, c), lambda i, m: (i, 0)))
    out_spec = (pl.BlockSpec((1, rt, c), lambda i, m: (m[0], i, 0)) if layer is None
                else pl.BlockSpec((1, 1, rt, c), lambda i, m: (m[0], layer, i, 0)))
    grid_spec = pltpu.PrefetchScalarGridSpec(num_scalar_prefetch=1, grid=(r // rt,),
                                             in_specs=[src_spec] + ([] if create else [ANY]), out_specs=out_spec)
    return pl.pallas_call(body, name=name, grid_spec=grid_spec, out_shape=jax.ShapeDtypeStruct(land.shape, land.dtype),
                          input_output_aliases={} if create else {2: 0}, compiler_params=_cp(1),
                          )(*((me, src) if create else (me, src, land)))


def _scan_constants():
    r = lax.broadcasted_iota(jnp.int32, (CH, CH), 0)
    s = lax.broadcasted_iota(jnp.int32, (CH, CH), 1)
    lower = (s <= r).astype(F32)
    t = jnp.arange(CH)[:, None]
    mc = jnp.stack([lower, lower.T])
    mref = jnp.stack([(t <= CH // 2 - 1).astype(F32), (t >= CH // 2).astype(F32)])
    return mc, jnp.stack([lower.T, lower]), mref


def local_step(x, ctx, target, mod, lb, w, fetch=None, publish=None, small_ready=None, small_early=None):
    kept = {}

    def keep(l, part, grads):
        kept[(l, part)] = grads
        return 0.0

    fetch = fetch or (lambda l, part, after: w)
    publish = publish or keep
    n_layers = len(mod)
    mc, mtc, mrefc = _scan_constants()
    xs = jnp.concatenate([ctx, x], axis=0)
    saved, big = [], []
    for l in range(n_layers):
        wl = dict(fetch(l, "in", xs))
        parts, ht = in_proj_fwd(xs, mod[l], w["nw1"][l], wl["win"][l])
        o, ck = hgrn_fwd(parts, lb[l], mc, mtc, mrefc)
        wl.update(fetch(l, "rest", o))
        last = l == n_layers - 1
        x1, pa, pb, ym, yat, ybt, mt = mixer_fwd(xs, parts, o, mod[l], w["lnw"][l], w["lnb"][l], w["sw"][l], w["sb"][l],
                                                 w["hnw"][l], wl["wa"][l], wl["wb"][l], wl["wo"][l], last)
        av, h2t = ffn_up_fwd(x1, mod[l], w["nw2"][l], wl["wup"][l], last)
        x2, ac, y, z = ffn_down_fwd(x1, av, mod[l], w["cw"][l], w["cb"][l], wl["wd"][l], last)
        saved.append((xs, parts, o, ck, x1, av, ac, y, z, ht, h2t, pa, pb, ym, yat, ybt, mt))
        big.append(wl)
        xs = x2
    loss, dx, dfw = loss_fwd_bwd(xs, target, w["fw"])
    g = {k: [None] * n_layers for k in ("nw1", "nw2", "lnw", "lnb", "sw", "sb", "hnw", "cw", "cb")}
    g["fw"] = dfw
    dmod, dlb = [None] * n_layers, [None] * n_layers
    tok = 0.0
    for l in reversed(range(n_layers)):
        x0, parts, o, ck, x1, av, ac, y, z, ht, h2t, pa, pb, ym, yat, ybt, mt = saved[l]
        wl = big[l]
        last = l == n_layers - 1
        dav, dac, dout, dg2 = ffn_down_bwd(dx, ac, av, y, mod[l] + tok, wl["wd"][l], last)
        dwd = weight_grad_rows(z, dout, "ffn_down_bwd_w")
        dav, g["cw"][l], g["cb"][l] = conv_bwd(dav, dac, av, w["cw"][l], last)
        dx1, g["nw2"][l], dmod2 = ffn_up_bwd_x(dx, x1, dav, mod[l], w["nw2"][l], wl["wup"][l], last)
        dwup = weight_grad(h2t, dav, FF_SLOT, "ffn_up_bwd_w")
        tok = publish(l, "ffn", {"wd": dwd, "wup": dwup})
        (dparts, do, dy, dpa, dpb, g["lnw"][l], g["lnb"][l], g["sw"][l], g["sb"][l], g["hnw"][l],
         dg1) = mixer_bwd(dx1, parts, o, pa, pb, ym, mod[l] + tok, w["lnw"][l], w["lnb"][l], w["sw"][l], w["sb"][l],
                          w["hnw"][l], wl["wa"][l], wl["wb"][l], wl["wo"][l], last)
        tok = publish(l, "mix", {"wa": weight_grad(yat, dpa, D, "mixer_bwd_wa"), "wb": weight_grad(ybt, dpb, D, "mixer_bwd_wb"),
                                 "wo": weight_grad(mt, dy, D, "mixer_bwd_wo")})
        if l == 0 and small_early:
            dmod[0] = jnp.concatenate([jnp.zeros((2, 2, 1, D), F32), dg1, dmod2, dg2], axis=1)
            tok = tok + small_early(loss[0, 0], g, dmod, dlb)
        dq, df, di, dlb_f = hgrn_bwd(0, parts, lb[l] + tok, mc, mtc, mrefc, ck, do)
        dparts, dlb_b = hgrn_bwd(1, parts, lb[l], mc, mtc, mrefc, ck, do, (dq, df, di), dparts)
        dlb[l] = jnp.concatenate([dlb_f, dlb_b], axis=0)
        tok = publish(l, "in", {"win": weight_grad(ht, dparts, IN_SLOT, "in_proj_bwd_w")})
        dx, g["nw1"][l], dmod1 = in_proj_bwd_x(dx1, x0, dparts, mod[l], w["nw1"][l], wl["win"][l], after=tok,
                                               latent_only=l == 0)
        dmod[l] = jnp.concatenate([dmod1, dg1, dmod2, dg2], axis=1)
    done = small_ready(loss[0, 0], g, dmod, dlb) if small_ready else 0.0
    for (l, part), grads in kept.items():
        for k, v in grads.items():
            g.setdefault(k, [None] * n_layers)[l] = v
    return loss[0, 0], dx, g, dmod, dlb, done


ROW = 1024
REPLICATED = ("norm1_w", "sgu_ln_w", "sgu_ln_b", "sgu_w", "sgu_b", "hgrn_lower_bounds", "hgrn_norm_w", "norm2_w",
              "ffn_conv_b", "final_norm_w")
WEIGHT_ORDER = ("c_ctx", "ada_w", "ada_b", "norm1_w", "w_in", "sgu_ln_w", "sgu_ln_b", "sgu_w", "sgu_b", "hgrn_lower_bounds",
                "hgrn_norm_w", "w_branch_a", "w_branch_b", "w_out", "norm2_w", "ffn_w_up", "ffn_conv_w", "ffn_conv_b",
                "ffn_w_down", "final_norm_w")


def _rows_of(n):
    return -(-n // (8 * ROW)) * 8


def _pack(arrs, total_rows=None):
    parts = []
    for a in arrs:
        flat = a.reshape(-1).astype(F32)
        rows = _rows_of(flat.shape[0])
        parts.append(jnp.pad(flat, (0, rows * ROW - flat.shape[0])).reshape(rows, ROW))
    have = sum(p.shape[0] for p in parts)
    if total_rows is not None and total_rows > have:
        parts.append(jnp.zeros((total_rows - have, ROW), F32))
    return jnp.concatenate(parts, axis=0)


def _unpack(packed, shapes):
    lead = packed.shape[:-2]
    out, r0 = [], 0
    for s in shapes:
        n = math.prod(s)
        rows = _rows_of(n)
        out.append(packed[..., r0:r0 + rows, :].reshape(lead + (rows * ROW,))[..., :n].reshape(lead + tuple(s)))
        r0 += rows
    return out


def kernel(x, c, ctx, c_ctx, ada_w, ada_b, norm1_w, w_in, sgu_ln_w, sgu_ln_b, sgu_w, sgu_b, hgrn_lower_bounds, hgrn_norm_w, w_branch_a, w_branch_b, w_out, norm2_w, ffn_w_up, ffn_conv_w, ffn_conv_b, ffn_w_down, final_norm_w, loss_target, m_c_ctx, m_ada_w, m_ada_b, m_norm1_w, m_w_in, m_sgu_ln_w, m_sgu_ln_b, m_sgu_w, m_sgu_b, m_hgrn_lower_bounds, m_hgrn_norm_w, m_w_branch_a, m_w_branch_b, m_w_out, m_norm2_w, m_ffn_w_up, m_ffn_conv_w, m_ffn_conv_b, m_ffn_w_down, m_final_norm_w, v_c_ctx, v_ada_w, v_ada_b, v_norm1_w, v_w_in, v_sgu_ln_w, v_sgu_ln_b, v_sgu_w, v_sgu_b, v_hgrn_lower_bounds, v_hgrn_norm_w, v_w_branch_a, v_w_branch_b, v_w_out, v_norm2_w, v_ffn_w_up, v_ffn_conv_w, v_ffn_conv_b, v_ffn_w_down, v_final_norm_w):
    wts = dict(c_ctx=c_ctx, ada_w=ada_w, ada_b=ada_b, norm1_w=norm1_w, w_in=w_in, sgu_ln_w=sgu_ln_w, sgu_ln_b=sgu_ln_b,
               sgu_w=sgu_w, sgu_b=sgu_b, hgrn_lower_bounds=hgrn_lower_bounds, hgrn_norm_w=hgrn_norm_w, w_branch_a=w_branch_a,
               w_branch_b=w_branch_b, w_out=w_out, norm2_w=norm2_w, ffn_w_up=ffn_w_up, ffn_conv_w=ffn_conv_w,
               ffn_conv_b=ffn_conv_b, ffn_w_down=ffn_w_down, final_norm_w=final_norm_w)
    mom1 = dict(c_ctx=m_c_ctx, ada_w=m_ada_w, ada_b=m_ada_b, norm1_w=m_norm1_w, w_in=m_w_in, sgu_ln_w=m_sgu_ln_w,
                sgu_ln_b=m_sgu_ln_b, sgu_w=m_sgu_w, sgu_b=m_sgu_b, hgrn_lower_bounds=m_hgrn_lower_bounds,
                hgrn_norm_w=m_hgrn_norm_w, w_branch_a=m_w_branch_a, w_branch_b=m_w_branch_b, w_out=m_w_out, norm2_w=m_norm2_w,
                ffn_w_up=m_ffn_w_up, ffn_conv_w=m_ffn_conv_w, ffn_conv_b=m_ffn_conv_b, ffn_w_down=m_ffn_w_down,
                final_norm_w=m_final_norm_w)
    mom2 = dict(c_ctx=v_c_ctx, ada_w=v_ada_w, ada_b=v_ada_b, norm1_w=v_norm1_w, w_in=v_w_in, sgu_ln_w=v_sgu_ln_w,
                sgu_ln_b=v_sgu_ln_b, sgu_w=v_sgu_w, sgu_b=v_sgu_b, hgrn_lower_bounds=v_hgrn_lower_bounds,
                hgrn_norm_w=v_hgrn_norm_w, w_branch_a=v_w_branch_a, w_branch_b=v_w_branch_b, w_out=v_w_out, norm2_w=v_norm2_w,
                ffn_w_up=v_ffn_w_up, ffn_conv_w=v_ffn_conv_w, ffn_conv_b=v_ffn_conv_b, ffn_w_down=v_ffn_w_down,
                final_norm_w=v_final_norm_w)
    n_layers = w_in.shape[0]
    layers = range(n_layers)
    me = 4 * lax.axis_index("x") + 2 * lax.axis_index("y") + lax.axis_index("c")
    ada_cols = ada_w.shape[-1]

    big = ("w_in", "ffn_w_up", "w_branch_a", "w_branch_b", "w_out", "ffn_w_down")
    short = {"w_in": "win", "ffn_w_up": "wup", "w_branch_a": "wa", "w_branch_b": "wb", "w_out": "wo", "ffn_w_down": "wd"}
    me1 = me.reshape(1).astype(jnp.int32)
    groups = [[("w_in", 0)], [(k, 0) for k in big[1:]], [("w_in", 1)], [(k, 1) for k in big[1:]]]
    in_flight, started = [], 0.0

    def own_slots(n):
        return [place_own(wts[k][l], jax.ShapeDtypeStruct((N_DEV,) + wts[k].shape[1:], BF16), me1, None, False,
                          f"gather_own_{short[k]}_{l}") for k, l in groups[n]]

    def start_group(n, lands, after):
        in_flight.append(exchange_start([], lands, None, False, f"gather_weights_start_{n}", after=after,
                                        peers=SAME_CORE_AND_SIBLING if n == 0 else ALL_PEERS))
        return in_flight[-1][-1]

    (c_all,) = all_gather([c], "gather_c")
    c_all = c_all.reshape(N_DEV, D)
    token = start_group(0, own_slots(0), c_all)
    later = [own_slots(n) for n in range(1, len(groups))]
    cctx8 = jnp.broadcast_to(c_ctx[None, :], (N_DEV, D))
    ada_b_cols = lax.dynamic_slice_in_dim(ada_b, me * ada_cols, ada_cols, axis=1)[:, None, :]
    mod_cols = ada_fwd(c_all, cctx8, ada_w, ada_b_cols)
    xs = jnp.concatenate([ctx[0], x[0]], axis=0)
    lb1 = lower_bounds(hgrn_lower_bounds)
    mod_all, conv_all = all_gather([mod_cols, ffn_conv_w.reshape(n_layers, 9, -1)], "gather_mod_conv",
                                   after=[token, xs, lb1] + [a for lands in later for a in lands])
    conv_full = [conv_all[:, l].transpose(1, 0, 2).reshape(9, N_FFK, FF_SLOT).transpose(1, 0, 2) for l in layers]
    for n in range(1, len(groups)):
        token = start_group(n, later[n - 1], mod_all if n == 1 else token)
    for started_group in in_flight:
        started = started + started_group[-1][0, 0]

    def as_used(k, a):
        return a if k in ("w_in", "ffn_w_up") else a.reshape(N_FFK, FF_SLOT, D) if k == "ffn_w_down" else a.reshape(D, D)

    arrived = {}

    def fetch(l, part, after):
        n = {(0, "in"): 0, (0, "rest"): 1, (1, "in"): 2, (1, "rest"): 3}.get((l, part))
        if n is not None:
            send, recv, _, lands, _ = in_flight[n]
            first = n == 0
            got = exchange_wait(send, recv, [], lands, None, False, after, f"gather_weights_wait_{n}",
                                peers=SAME_CORE_AND_SIBLING if first else ALL_PEERS)
            if first:
                send, recv, _, lands, _ = exchange_start([], got, None, False, "gather_weights_pass_on", peers=OTHER_CHIPS,
                                                         forward=True)
                got = exchange_wait(send, recv, [], lands, None, False, after, "gather_weights_passed_on", peers=OTHER_CHIPS)
            for (k, ll), a in zip(groups[n], got):
                arrived.setdefault(short[k], [None] * n_layers)[ll] = as_used(k, a)
        return arrived

    mod_x = lax.dynamic_index_in_dim(mod_all[:, :, 0], me, axis=2, keepdims=False)
    mod_c = mod_all[:, :, 1, 0]
    mod = [jnp.stack([mod_c[:, l].reshape(6, 1, D), mod_x[:, l].reshape(6, 1, D)]) for l in layers]
    mod[0] = mod[0] + started

    lb = [jnp.zeros((2, 1, D), F32), lb1.reshape(2, 1, D)]

    w = {
        "nw1": [norm1_w[l][None] for l in layers], "nw2": [norm2_w[l][None] for l in layers],
        "lnw": [sgu_ln_w[l][None] for l in layers], "lnb": [sgu_ln_b[l][None] for l in layers],
        "sw": [sgu_w[l] for l in layers], "sb": [sgu_b[l][:, :, None] for l in layers],
        "hnw": [hgrn_norm_w[l][None] for l in layers], "cw": conv_full,
        "cb": [ffn_conv_b[l].reshape(N_FFK, 1, FF_SLOT) for l in layers], "fw": final_norm_w[None],
    }
    long = {v: k for k, v in short.items()}
    landing, sent = {}, []

    def publish(l, part, grads):
        keys = [long[k] for k in grads]
        slots = [a.reshape((N_DEV, -1, a.shape[-1])) for a in grads.values()]
        zones = [place_own(s, landing.get(k, jax.ShapeDtypeStruct((N_DEV, n_layers) + s.shape[1:], s.dtype)), me1, l, True,
                           f"scatter_own_{short[k]}_{l}") for k, s in zip(keys, slots)]
        send, recv, srcs, zones, token = exchange_start(slots, zones, l, True, f"scatter_grads_start_{part}_{l}")
        landing.update(zip(keys, zones))
        sent.append((keys, l, part, send, recv, srcs, token))
        return token[0, 0]

    out = {}
    flat2 = lambda a: a.reshape(-1, a.shape[-1])

    def finish(part, after):
        done = []
        for keys, l, p, send, recv, srcs, _ in sent:
            if p == part:
                zones = exchange_wait(send, recv, srcs, [landing[k] for k in keys], l, True, after,
                                      f"scatter_grads_wait_{part}_{l}")
                landing.update(zip(keys, zones))
                done = keys
        for k in done:
            r = landing[k]
            res = adamw(flat2(wts[k]), flat2(mom1[k]), flat2(mom2[k]), r.reshape(N_DEV, -1, r.shape[-1]), "adamw_" + k)
            out[k] = tuple(a.reshape(wts[k].shape) for a in res)

    rep_rows = -(-sum(_rows_of(wts[k].size) for k in REPLICATED) // 64) * 64
    conv_rows = _rows_of(n_layers * 9 * D_FF)
    dmod_rows = _rows_of(n_layers * 6 * D)
    early = {}

    def small_early(loss_part, g, dmod, dlb):
        d_hlb = lower_bounds_bwd(hgrn_lower_bounds, dlb[1].reshape(1, 2 * D))
        st = lambda k: jnp.stack([jnp.zeros((1, D), F32) if a is None else a for a in g[k]])
        rep_grads = {"norm1_w": st("nw1"), "sgu_ln_w": st("lnw"), "sgu_ln_b": st("lnb"), "sgu_w": st("sw"), "sgu_b": st("sb"),
                     "hgrn_lower_bounds": d_hlb, "hgrn_norm_w": st("hnw"), "norm2_w": st("nw2"), "ffn_conv_b": st("cb"),
                     "final_norm_w": g["fw"]}
        d_conv = jnp.stack([g["cw"][l].transpose(1, 0, 2).reshape(9, D_FF) for l in layers])
        dmod_x = jnp.stack([dmod[l][1].reshape(6 * D) for l in layers])
        dmod_c = jnp.stack([dmod[l][0].reshape(6 * D) for l in layers])
        small = jnp.concatenate([_pack([rep_grads[k] for k in REPLICATED], rep_rows),
                                 _pack([d_conv, dmod_x, dmod_c, loss_part.reshape(1)])], axis=0)
        zone = place_own(small, jax.ShapeDtypeStruct((N_DEV,) + small.shape, F32), me1, None, False, "gather_small_own")
        early["send"], early["recv"], _, early["zones"], token = exchange_start([], [zone], None, False, "gather_small_start")
        return token[0, 0]

    def small_ready(loss_part, g, dmod, dlb):
        late = _pack([g["nw1"][0], dmod[0][1, 0:2], dmod[0][0, 0:2]])
        for part in ("ffn", "mix"):
            finish(part, late)
        (late_all,) = all_gather([late], "gather_small_late", after=out["w_out"][0])
        (small_all,) = exchange_wait(early["send"], early["recv"], [], early["zones"], None, False, late_all,
                                     "gather_small_wait")
        at_x = rep_rows + conv_rows
        small_all = small_all.at[:, 0:1].set(late_all[:, 0:1])
        small_all = small_all.at[:, at_x:at_x + 2].set(late_all[:, 8:10])
        small_all = small_all.at[:, at_x + dmod_rows:at_x + dmod_rows + 2].set(late_all[:, 16:18])
        d_conv_shape, dmod_shape = (n_layers, 9, D_FF), (n_layers, 6 * D)
        conv_g, dmx_all, dmc_all, loss_all = _unpack(small_all[:, rep_rows:], [d_conv_shape, dmod_shape, dmod_shape, (1,)])
        out["loss"] = functools.reduce(lambda a, b: a + b, [loss_all[k, 0] for k in range(N_DEV)])

        rep = adamw(_pack([wts[k] for k in REPLICATED], rep_rows), _pack([mom1[k] for k in REPLICATED], rep_rows),
                    _pack([mom2[k] for k in REPLICATED], rep_rows), small_all[:, :rep_rows], "adamw_replicated")
        rep = [_unpack(r, [wts[k].shape for k in REPLICATED]) for r in rep]
        for n, k in enumerate(REPLICATED):
            out[k] = tuple(r[n] for r in rep)

        conv_mine = lax.dynamic_index_in_dim(conv_g.reshape(N_DEV, n_layers, 9, N_DEV, -1), me, axis=3, keepdims=False)
        res = adamw(flat2(ffn_conv_w), flat2(m_ffn_conv_w), flat2(v_ffn_conv_w),
                    conv_mine.reshape(N_DEV, -1, conv_mine.shape[-1]), "adamw_conv_w")
        out["ffn_conv_w"] = tuple(r.reshape(ffn_conv_w.shape) for r in res)

        out["ada_b"] = tuple(adamw(ada_b, m_ada_b, v_ada_b, jnp.concatenate([dmx_all, dmc_all], axis=0), "adamw_ada_b"))

        cols_of = lambda a: lax.dynamic_slice_in_dim(a, me * ada_cols, ada_cols, axis=2).transpose(1, 0, 2)
        d_ada_w, d_cctx = ada_bwd(c_all, cctx8, ada_w, ada_b_cols, cols_of(dmx_all), cols_of(dmc_all))
        res = adamw(flat2(ada_w), flat2(m_ada_w), flat2(v_ada_w), flat2(d_ada_w)[None], "adamw_ada_w")
        out["ada_w"] = tuple(r.reshape(ada_w.shape) for r in res)
        (d_cctx_all,) = all_gather([d_cctx], "gather_c_ctx_grad")
        res = adamw(c_ctx[None], m_c_ctx[None], v_c_ctx[None], d_cctx_all, "adamw_c_ctx")
        out["c_ctx"] = tuple(r[0] for r in res)
        return d_cctx_all

    _, grad_x, _, _, _, small_done = local_step(x[0], ctx[0], loss_target[0], mod, lb, w, fetch, publish, small_ready,
                                                small_early)
    loss = out["loss"]

    finish("in", small_done)
    return (loss, grad_x[None]) + tuple(out[k][n] for n in range(4) for k in WEIGHT_ORDER)
```

```python
import functools
import math

import jax
import jax.numpy as jnp
from jax import lax
from jax.experimental import pallas as pl
from jax.experimental.pallas import tpu as pltpu

F32 = jnp.float32
BF16 = jnp.bfloat16
HIGHEST = lax.Precision.HIGHEST

N_DEV = 8
AXES = ("x", "y", "c")
D = 1024
CTX = 256
TM = 256
CH = 64
SGU_CH = 128
HEADS = 8
HD = 128
GRID_W = 64
D_IN = 9 * D
IN_SLOT = D_IN // N_DEV
D_FF = 2816
FF_SLOT = 2 * D_FF // N_DEV
N_FFK = D_FF // FF_SLOT
RMS_EPS = 1e-6
LN_EPS = 1e-5
ADAM_LR, ADAM_B1, ADAM_B2, ADAM_EPS, ADAM_WD, ADAM_STEP = 0.001, 0.9, 0.999, 1e-08, 0.01, 10
VMEM_LIMIT_V7X = 56 * 2 ** 20
GRAD_WIRE = jnp.bfloat16

VMEM_WHOLE = pl.BlockSpec(memory_space=pltpu.VMEM)
ANY = pl.BlockSpec(memory_space=pl.ANY)


def _cp(n_axes):
    return pltpu.CompilerParams(dimension_semantics=("arbitrary",) * n_axes, vmem_limit_bytes=VMEM_LIMIT_V7X)


def _dot(a, b, dims):
    return lax.dot_general(a.astype(BF16), b.astype(BF16), (dims, ((), ())), preferred_element_type=F32)


@jax.custom_vjp
def mm(a, b):
    return _dot(a, b, ((1,), (0,)))


mm.defvjp(lambda a, b: (mm(a, b), (a, b)),
          lambda r, g: (_dot(g, r[1], ((1,), (1,))).astype(r[0].dtype), _dot(r[0], g, ((0,), (0,))).astype(r[1].dtype)))


@jax.custom_vjp
def mm_nt(a, b):
    return _dot(a, b, ((1,), (1,)))


mm_nt.defvjp(lambda a, b: (mm_nt(a, b), (a, b)),
             lambda r, g: (_dot(g, r[1], ((1,), (0,))).astype(r[0].dtype), _dot(g, r[0], ((0,), (0,))).astype(r[1].dtype)))


@jax.custom_vjp
def mm_tn(a, b):
    return _dot(a, b, ((0,), (0,)))


mm_tn.defvjp(lambda a, b: (mm_tn(a, b), (a, b)),
             lambda r, g: (_dot(r[1], g, ((1,), (1,))).astype(r[0].dtype), _dot(r[0], g, ((1,), (0,))).astype(r[1].dtype)))


def _tri_dot(m, g):
    hi = g.astype(BF16)
    rest = g - hi.astype(F32)
    mid = rest.astype(BF16)
    low = (rest - mid.astype(F32)).astype(BF16)
    n = g.shape[1]
    out = jnp.dot(m.astype(BF16), jnp.concatenate([hi, mid, low], axis=1), preferred_element_type=F32)
    return out[:, :n] + out[:, n:2 * n] + out[:, 2 * n:]


@jax.custom_vjp
def _cum(m, mt, g):
    return _tri_dot(m, g)


_cum.defvjp(lambda m, mt, g: (_cum(m, mt, g), (m, mt)),
            lambda r, d: (jnp.zeros_like(r[0]), jnp.zeros_like(r[1]), _tri_dot(r[1], d)))


def _silu(x):
    return x * jax.nn.sigmoid(x)


def _gelu(x):
    return 0.5 * x * (1.0 + jnp.tanh(math.sqrt(2.0 / math.pi) * (x + 0.044715 * (x * x * x))))


def _rms(x, w):
    return x * lax.rsqrt(jnp.mean(x * x, axis=-1, keepdims=True) + RMS_EPS) * w


def _norm_mod(x, w, shift, scale):
    return _rms(x, w) * (1.0 + scale) + shift


def _hsl(h):
    return slice(h * HD, (h + 1) * HD)


def _hgrn_chunk(st, qz, fz, iv, lb, m, mt, mref):
    hs = range(HEADS)
    keep = [1.0 - lb[h] for h in hs]
    g = [jnp.log(lb[h] + keep[h] * jax.nn.sigmoid(fz[h])) for h in hs]
    k = [keep[h] * jax.nn.sigmoid(-fz[h]) for h in hs]
    q = [_silu(qz[h]) for h in hs]
    b = [_cum(m, mt, g[h]) for h in hs]
    ref = [jnp.sum(mref * g[h], axis=0, keepdims=True) for h in hs]
    last = [jnp.sum(g[h], axis=0, keepdims=True) for h in hs]
    qa = [q[h] * jnp.exp(b[h] - ref[h]) for h in hs]
    ka = [k[h] * jnp.exp(ref[h] - b[h]) for h in hs]
    scores = [jnp.where(m > 0.5, mm_nt(qa[h], ka[h]), 0.0) for h in hs]
    inter = [mm_nt(qa[h] * jnp.exp(ref[h]), st[h]) for h in hs]
    kv = [mm_tn(iv[h], ka[h] * jnp.exp(last[h] - ref[h])) for h in hs]
    outs = [mm(scores[h], iv[h]) + inter[h] for h in hs]
    news = [jnp.exp(last[h]) * st[h] + kv[h] for h in hs]
    return outs, news


def _sgu_fn(ub, vb, lnw, lnb, sw, sb):
    gv = [_gelu(v) for v in vb]
    mu = sum(jnp.sum(t, axis=-1, keepdims=True) for t in gv) / D
    var = sum(jnp.sum((t - mu) * (t - mu), axis=-1, keepdims=True) for t in gv) / D
    inv = lax.rsqrt(var + LN_EPS)
    cols = []
    for g in range(HEADS):
        vn = (gv[g] - mu) * inv * lnw[g] + lnb[g]
        cols.append(_gelu(ub[g]) * (mm(sw[g], vn) + sb[g]))
    return jnp.concatenate(cols, axis=1)


def _readout_fn(ob, og, hnw):
    r = [o * lax.rsqrt(jnp.mean(o * o, axis=-1, keepdims=True) + RMS_EPS) * hnw for o in ob]
    return jnp.concatenate(r, axis=1) * _silu(og)


def _glu_fn(ac, v):
    return _gelu(ac) * v


def _stream_row(tm):
    n_ctx = CTX // tm
    return lambda i: (jnp.where(i < n_ctx, 0, 1), 0, 0, 0)


def in_proj_fwd(x, mod, nw, wg):
    t = x.shape[0]

    def body(x_ref, mod_ref, nw_ref, w_ref, out_ref, ht_ref):
        h32 = _norm_mod(x_ref[...], nw_ref[...], mod_ref[0, 0], mod_ref[0, 1])
        ht_ref[...] = h32.T.astype(BF16)
        h = h32.astype(BF16)
        for j in range(N_DEV):
            out_ref[:, j * IN_SLOT:(j + 1) * IN_SLOT] = jnp.dot(h, w_ref[j], preferred_element_type=F32)

    return pl.pallas_call(
        body, name="in_proj_fwd", grid=(t // TM,),
        in_specs=[pl.BlockSpec((TM, D), lambda i: (i, 0)), pl.BlockSpec((1, 6, 1, D), _stream_row(TM)),
                  pl.BlockSpec((1, D), lambda i: (0, 0)), VMEM_WHOLE],
        out_specs=[pl.BlockSpec((TM, D_IN), lambda i: (i, 0)), pl.BlockSpec((D, TM), lambda i: (0, i))],
        out_shape=[jax.ShapeDtypeStruct((t, D_IN), F32), jax.ShapeDtypeStruct((D, t), BF16)],
        compiler_params=_cp(1))(x, mod, nw, wg)


def _scan_chunk(nc):
    ncc = CTX // CH

    def chunk(d, s):
        bwd = jnp.where(s < ncc, ncc - 1 - s, nc + ncc - 1 - s)
        return jnp.where(d == 0, s, bwd)
    return chunk


def hgrn_fwd(parts, lb, mc, mtc, mrefc):
    t = parts.shape[0]
    nc = t // CH
    chunk = _scan_chunk(nc)

    def body(q_ref, f_ref, i_ref, lb_ref, m_ref, mt_ref, mr_ref, o_ref, ck_ref, st):
        @pl.when(pl.program_id(1) == 0)
        def _():
            st[...] = jnp.zeros_like(st)
        ck_ref[0, 0] = st[...]
        outs, news = _hgrn_chunk([st[h] for h in range(HEADS)], [q_ref[:, _hsl(h)] for h in range(HEADS)],
                                 [f_ref[:, _hsl(h)] for h in range(HEADS)], [i_ref[:, _hsl(h)] for h in range(HEADS)],
                                 [lb_ref[0, :, _hsl(h)] for h in range(HEADS)], m_ref[0], mt_ref[0], mr_ref[0])
        for h in range(HEADS):
            o_ref[0, :, _hsl(h)] = outs[h]
            st[h] = news[h]

    const = lambda d, s: (d, 0, 0)
    return pl.pallas_call(
        body, name="hgrn_fwd", grid=(2, nc),
        in_specs=[pl.BlockSpec((CH, D), lambda d, s: (chunk(d, s), 0)), pl.BlockSpec((CH, D), lambda d, s: (chunk(d, s), 1 + d)),
                  pl.BlockSpec((CH, D), lambda d, s: (chunk(d, s), 3)), pl.BlockSpec((1, 1, D), const),
                  pl.BlockSpec((1, CH, CH), const), pl.BlockSpec((1, CH, CH), const), pl.BlockSpec((1, CH, 1), const)],
        out_specs=[pl.BlockSpec((1, CH, D), lambda d, s: (d, chunk(d, s), 0)),
                   pl.BlockSpec((1, 1, HEADS, HD, HD), lambda d, s: (d, s, 0, 0, 0))],
        out_shape=[jax.ShapeDtypeStruct((2, t, D), F32), jax.ShapeDtypeStruct((2, nc, HEADS, HD, HD), F32)],
        scratch_shapes=[pltpu.VMEM((HEADS, HD, HD), F32)], compiler_params=_cp(2))(parts, parts, parts, lb, mc, mtc, mrefc)


def _mixer_tile(rows, u_ref, v_ref, og_ref, o_ref, lnw_ref, lnb_ref, sw_ref, sb_ref, hnw_ref):
    n = (rows.stop - rows.start) // SGU_CH
    yas, vjps = [], []
    for c in range(n):
        r = slice(rows.start + c * SGU_CH, rows.start + (c + 1) * SGU_CH)
        ya, vjp_a = jax.vjp(_sgu_fn, [u_ref[r, _hsl(g)] for g in range(HEADS)], [v_ref[r, _hsl(g)] for g in range(HEADS)],
                            [lnw_ref[:, _hsl(g)] for g in range(HEADS)], [lnb_ref[:, _hsl(g)] for g in range(HEADS)],
                            [sw_ref[g] for g in range(HEADS)], [sb_ref[g] for g in range(HEADS)])
        yas.append(ya)
        vjps.append(vjp_a)
    yb, vjp_b = jax.vjp(_readout_fn, [o_ref[0, rows, _hsl(h)] + o_ref[1, rows, _hsl(h)] for h in range(HEADS)],
                        og_ref[rows, :], hnw_ref[...])
    return (yas[0] if n == 1 else jnp.concatenate(yas, axis=0)), yb, vjps, vjp_b


def _part_specs(tm, first, n):
    return [pl.BlockSpec((tm, D), functools.partial(lambda k, i: (i, k), first + k)) for k in range(n)]


def _unless_ctx(skip_ctx, is_ctx, zero_refs, work):
    if not skip_ctx:
        return work()

    @pl.when(is_ctx)
    def _():
        for r in zero_refs:
            r[...] = jnp.zeros_like(r)

    pl.when(jnp.logical_not(is_ctx))(work)


def mixer_fwd(x, parts, o, mod, lnw, lnb, sw, sb, hnw, wa, wb, wo, skip_ctx):
    t = x.shape[0]

    def body(x_ref, u_ref, v_ref, og_ref, ga_ref, gb_ref, o_ref, mod_ref, lnw_ref, lnb_ref, sw_ref, sb_ref, hnw_ref,
             wa_ref, wb_ref, wo_ref, out_ref, pa_ref, pb_ref, y_ref, yat_ref, ybt_ref, mt_ref):
        def work():
            ya, yb, _, _ = _mixer_tile(slice(0, TM), u_ref, v_ref, og_ref, o_ref, lnw_ref, lnb_ref, sw_ref, sb_ref, hnw_ref)
            pa, pb = mm(ya, wa_ref[...]), mm(yb, wb_ref[...])
            merged = jax.nn.sigmoid(ga_ref[...]) * pa + jax.nn.sigmoid(gb_ref[...]) * pb
            y = mm(merged, wo_ref[...])
            out_ref[...] = x_ref[...] + mod_ref[0, 2] * y
            pa_ref[...], pb_ref[...], y_ref[...] = pa.astype(BF16), pb.astype(BF16), y.astype(BF16)
            yat_ref[...], ybt_ref[...], mt_ref[...] = ya.T.astype(BF16), yb.T.astype(BF16), merged.T.astype(BF16)

        _unless_ctx(skip_ctx, pl.program_id(0) == 0, (out_ref, pa_ref, pb_ref, y_ref, yat_ref, ybt_ref, mt_ref), work)

    vec = lambda n: pl.BlockSpec((1, n), lambda i: (0, 0))
    tile = pl.BlockSpec((TM, D), lambda i: (i, 0))
    tile_t = pl.BlockSpec((D, TM), lambda i: (0, i))
    return pl.pallas_call(
        body, name="mixer_fwd", grid=(t // TM,),
        in_specs=[tile] + _part_specs(TM, 4, 5)
        + [pl.BlockSpec((2, TM, D), lambda i: (0, i, 0)), pl.BlockSpec((1, 6, 1, D), _stream_row(TM)), vec(D), vec(D),
           VMEM_WHOLE, VMEM_WHOLE, vec(HD), VMEM_WHOLE, VMEM_WHOLE, VMEM_WHOLE],
        out_specs=[tile] * 4 + [tile_t] * 3,
        out_shape=[jax.ShapeDtypeStruct((t, D), F32)] + [jax.ShapeDtypeStruct((t, D), BF16)] * 3
        + [jax.ShapeDtypeStruct((D, t), BF16)] * 3, compiler_params=_cp(1),
    )(x, parts, parts, parts, parts, parts, o, mod, lnw, lnb, sw, sb, hnw, wa, wb, wo)


def ffn_up_fwd(x, mod, nw, wg, skip_ctx):
    t = x.shape[0]

    def body(x_ref, mod_ref, nw_ref, w_ref, out_ref, ht_ref):
        def work():
            h32 = _norm_mod(x_ref[...], nw_ref[...], mod_ref[0, 3], mod_ref[0, 4])
            ht_ref[...] = h32.T.astype(BF16)
            h = h32.astype(BF16)
            for j in range(N_DEV):
                out_ref[j] = jnp.dot(h, w_ref[j], preferred_element_type=F32)

        _unless_ctx(skip_ctx, pl.program_id(0) == 0, (out_ref, ht_ref), work)

    return pl.pallas_call(
        body, name="ffn_up_fwd", grid=(t // TM,),
        in_specs=[pl.BlockSpec((TM, D), lambda i: (i, 0)), pl.BlockSpec((1, 6, 1, D), _stream_row(TM)),
                  pl.BlockSpec((1, D), lambda i: (0, 0)), VMEM_WHOLE],
        out_specs=[pl.BlockSpec((N_DEV, TM, FF_SLOT), lambda i: (0, i, 0)), pl.BlockSpec((D, TM), lambda i: (0, i))],
        out_shape=[jax.ShapeDtypeStruct((N_DEV, t, FF_SLOT), F32), jax.ShapeDtypeStruct((D, t), BF16)],
        compiler_params=_cp(1))(x, mod, nw, wg)


def _halo_specs(nt, k_of, i_of):
    per = TM // GRID_W
    last = nt * per - 1
    return [pl.BlockSpec((1, GRID_W, FF_SLOT), lambda *g: (k_of(*g), jnp.maximum(i_of(*g) * per - 1, 0), 0)),
            pl.BlockSpec((1, TM, FF_SLOT), lambda *g: (k_of(*g), i_of(*g), 0)),
            pl.BlockSpec((1, GRID_W, FF_SLOT), lambda *g: (k_of(*g), jnp.minimum(i_of(*g) * per + per, last), 0))]


def _with_halo(prev_ref, main_ref, next_ref, i, nt):
    prev = jnp.where(i >= 2, prev_ref[0], 0.0)
    nxt = jnp.where((i >= 1) & (i <= nt - 2), next_ref[0], 0.0)
    return jnp.concatenate([prev, main_ref[0], nxt], axis=0)


def _tap_valid(dc, i, n_rows, offset):
    r = lax.broadcasted_iota(jnp.int32, (n_rows, 1), 0) - offset
    col = jnp.bitwise_and(r, GRID_W - 1)
    pos = jnp.where(i == 0, r, col) + dc
    return (pos >= 0) & (pos < jnp.where(i == 0, TM, GRID_W))


def _row_weight(cw_ref, dr, dc, i):
    w = cw_ref[0, 3 * (dr + 1) + dc + 1:3 * (dr + 1) + dc + 2, :]
    return w if dr == 0 else jnp.where(i == 0, 0.0, w)


def ffn_down_fwd(x, av, mod, cw, cb, wd, skip_ctx):
    t = x.shape[0]
    nt = t // TM
    ext = TM + 2 * GRID_W

    def body(x_ref, ap_ref, am_ref, an_ref, v_ref, mod_ref, cw_ref, cb_ref, wd_ref, out_ref, ac_ref, y_ref, z_ref, acc):
        i, k = pl.program_id(0), pl.program_id(1)

        def work():
            a_ext = _with_halo(ap_ref, am_ref, an_ref, i, nt)
            conv = jnp.zeros((TM, FF_SLOT), F32) + cb_ref[0]
            for dc in (-1, 0, 1):
                rolled = (a_ext if dc == 0 else
                          jnp.where(_tap_valid(dc, i, ext, GRID_W), pltpu.roll(a_ext, (-dc) % ext, 0), 0.0))
                for dr in (-1, 0, 1):
                    lo = GRID_W + GRID_W * dr
                    conv = conv + rolled[lo:lo + TM] * _row_weight(cw_ref, dr, dc, i)
            ac_ref[0] = conv
            z = _glu_fn(conv, v_ref[0]).astype(BF16)
            z_ref[0] = z
            part = mm(z, wd_ref[0])

            @pl.when(k == 0)
            def _():
                acc[...] = part

            @pl.when(k > 0)
            def _():
                acc[...] += part

            @pl.when(k == N_FFK - 1)
            def _():
                y_ref[...] = acc[...]
                out_ref[...] = x_ref[...] + mod_ref[0, 5] * acc[...]

        _unless_ctx(skip_ctx, i == 0, (out_ref, ac_ref, y_ref, z_ref), work)

    tile = pl.BlockSpec((TM, D), lambda i, k: (i, 0))
    return pl.pallas_call(
        body, name="ffn_down_fwd", grid=(nt, N_FFK),
        in_specs=[tile] + _halo_specs(nt, lambda i, k: k, lambda i, k: i)
        + [pl.BlockSpec((1, TM, FF_SLOT), lambda i, k: (N_FFK + k, i, 0)),
           pl.BlockSpec((1, 6, 1, D), lambda i, k: (jnp.where(i < 1, 0, 1), 0, 0, 0)),
           pl.BlockSpec((1, 9, FF_SLOT), lambda i, k: (k, 0, 0)), pl.BlockSpec((1, 1, FF_SLOT), lambda i, k: (k, 0, 0)),
           pl.BlockSpec((1, FF_SLOT, D), lambda i, k: (k, 0, 0))],
        out_specs=[tile, pl.BlockSpec((1, TM, FF_SLOT), lambda i, k: (k, i, 0)), tile,
                   pl.BlockSpec((1, TM, FF_SLOT), lambda i, k: (k, i, 0))],
        out_shape=[jax.ShapeDtypeStruct((t, D), F32), jax.ShapeDtypeStruct((N_FFK, t, FF_SLOT), F32),
                   jax.ShapeDtypeStruct((t, D), F32), jax.ShapeDtypeStruct((N_FFK, t, FF_SLOT), BF16)],
        scratch_shapes=[pltpu.VMEM((TM, D), F32)], compiler_params=_cp(2))(x, av, av, av, av, mod, cw, cb, wd)


def loss_fwd_bwd(x, target, fw):
    t = x.shape[0]

    def body(x_ref, t_ref, w_ref, loss_ref, dx_ref, dw_ref):
        i = pl.program_id(0)

        @pl.when(i == 0)
        def _():
            loss_ref[...] = jnp.zeros_like(loss_ref)
            dw_ref[...] = jnp.zeros_like(dw_ref)
            dx_ref[...] = jnp.zeros_like(dx_ref)

        @pl.when(i > 0)
        def _():
            y, vjp = jax.vjp(_rms, x_ref[...], w_ref[...])
            err = y - t_ref[...]
            loss_ref[...] += 0.5 * jnp.sum(jnp.sum(err * err, axis=-1, keepdims=True) / D)
            dx, dw = vjp(err / D)
            dx_ref[...] = dx
            dw_ref[...] += dw

    return pl.pallas_call(
        body, name="loss_fwd_bwd", grid=(t // TM,),
        in_specs=[pl.BlockSpec((TM, D), lambda i: (i, 0)), pl.BlockSpec((TM, D), lambda i: (jnp.maximum(i - 1, 0), 0)),
                  pl.BlockSpec((1, D), lambda i: (0, 0))],
        out_specs=[pl.BlockSpec((8, 128), lambda i: (0, 0)), pl.BlockSpec((TM, D), lambda i: (i, 0)),
                   pl.BlockSpec((1, D), lambda i: (0, 0))],
        out_shape=[jax.ShapeDtypeStruct((8, 128), F32), jax.ShapeDtypeStruct((t, D), F32), jax.ShapeDtypeStruct((1, D), F32)],
        compiler_params=_cp(1))(x, target, fw)


def _stream_add(ref, k, is_ctx, val):
    ref[0, k] += jnp.where(is_ctx, val, 0.0)
    ref[1, k] += jnp.where(is_ctx, 0.0, val)


def ffn_down_bwd(dx, ac, av, y, mod, wd, skip_ctx):
    t = dx.shape[0]
    nt = t // TM

    def body(dx_ref, ac_ref, v_ref, y_ref, mod_ref, wd_ref, dav_ref, dac_ref, dout_ref, dg_ref):
        i = pl.program_id(0)

        @pl.when(i == 0)
        def _():
            dg_ref[...] = jnp.zeros_like(dg_ref)

        def work():
            _stream_add(dg_ref, 0, i == 0, jnp.sum(dx_ref[...] * y_ref[...], axis=0, keepdims=True))
            dout = (mod_ref[0, 5] * dx_ref[...]).astype(BF16)
            dout_ref[...] = dout
            for k in range(N_FFK):
                _, vjp = jax.vjp(_glu_fn, ac_ref[k], v_ref[k])
                dac, dv = vjp(mm_nt(dout, wd_ref[k]))
                dac_ref[k] = dac
                dav_ref[k] = dv.astype(BF16)

        _unless_ctx(skip_ctx, i == 0, (dav_ref, dac_ref, dout_ref), work)

    tile = pl.BlockSpec((TM, D), lambda i: (i, 0))
    half = lambda first: pl.BlockSpec((N_FFK, TM, FF_SLOT), lambda i: (first, i, 0))
    return pl.pallas_call(
        body, name="ffn_down_bwd", grid=(nt,),
        in_specs=[tile, half(0), half(1), tile, pl.BlockSpec((1, 6, 1, D), _stream_row(TM)), VMEM_WHOLE],
        out_specs=[half(1), half(0), tile, pl.BlockSpec((2, 1, 1, D), lambda i: (0, 0, 0, 0))],
        out_shape=[jax.ShapeDtypeStruct((N_DEV, t, FF_SLOT), BF16), jax.ShapeDtypeStruct((N_FFK, t, FF_SLOT), F32),
                   jax.ShapeDtypeStruct((t, D), BF16), jax.ShapeDtypeStruct((2, 1, 1, D), F32)],
        compiler_params=_cp(1))(dx, ac, av, y, mod, wd)


def conv_bwd(dav, dac, av, cw, skip_ctx):
    t = dac.shape[1]
    nt = t // TM
    ext = TM + 2 * GRID_W

    def body(dav_in, gp_ref, gm_ref, gn_ref, ap_ref, am_ref, an_ref, cw_ref, dav_ref, dcw_ref, dcb_ref):
        k, i = pl.program_id(0), pl.program_id(1)

        @pl.when(i == 0)
        def _():
            dcw_ref[...] = jnp.zeros_like(dcw_ref)
            dcb_ref[...] = jnp.zeros_like(dcb_ref)

        def work():
            g_ext = _with_halo(gp_ref, gm_ref, gn_ref, i, nt)
            a_ext = _with_halo(ap_ref, am_ref, an_ref, i, nt)
            g_main = gm_ref[0]
            dcb_ref[0] += jnp.sum(g_main, axis=0, keepdims=True)
            da = jnp.zeros((TM, FF_SLOT), F32)
            for dc in (-1, 0, 1):
                g_rolled = (g_ext if dc == 0 else
                            pltpu.roll(jnp.where(_tap_valid(dc, i, ext, GRID_W), g_ext, 0.0), dc % ext, 0))
                a_rolled = a_ext if dc == 0 else pltpu.roll(a_ext, (-dc) % ext, 0)
                g_valid = g_main if dc == 0 else jnp.where(_tap_valid(dc, i, TM, 0), g_main, 0.0)
                for dr in (-1, 0, 1):
                    lo = GRID_W - GRID_W * dr
                    da = da + g_rolled[lo:lo + TM] * _row_weight(cw_ref, dr, dc, i)
                    lo = GRID_W + GRID_W * dr
                    tap = 3 * (dr + 1) + dc + 1
                    dw = jnp.sum(g_valid * a_rolled[lo:lo + TM], axis=0, keepdims=True)
                    dcw_ref[0, tap:tap + 1, :] += dw if dr == 0 else jnp.where(i == 0, 0.0, dw)
            dav_ref[0] = da.astype(BF16)

        _unless_ctx(skip_ctx, i == 0, (dav_ref,), work)

    return pl.pallas_call(
        body, name="conv_bwd", grid=(N_FFK, nt),
        in_specs=[ANY] + _halo_specs(nt, lambda k, i: k, lambda k, i: i) + _halo_specs(nt, lambda k, i: k, lambda k, i: i)
        + [pl.BlockSpec((1, 9, FF_SLOT), lambda k, i: (k, 0, 0))],
        out_specs=[pl.BlockSpec((1, TM, FF_SLOT), lambda k, i: (k, i, 0)), pl.BlockSpec((1, 9, FF_SLOT), lambda k, i: (k, 0, 0)),
                   pl.BlockSpec((1, 1, FF_SLOT), lambda k, i: (k, 0, 0))],
        out_shape=[jax.ShapeDtypeStruct(dav.shape, BF16), jax.ShapeDtypeStruct((N_FFK, 9, FF_SLOT), F32),
                   jax.ShapeDtypeStruct((N_FFK, 1, FF_SLOT), F32)],
        input_output_aliases={0: 0}, compiler_params=_cp(2))(dav, dac, dac, dac, av, av, av, cw)


def _norm_mod_bwd(x_ref, nw_ref, mod_ref, k_shift, dh, dx_in, dx_ref, dnw_ref, dmod_ref, is_ctx):
    _, vjp = jax.vjp(_norm_mod, x_ref[...], nw_ref[...], mod_ref[0, k_shift], mod_ref[0, k_shift + 1])
    dx, dnw, dshift, dscale = vjp(dh)
    dx_ref[...] = dx_in + dx
    dnw_ref[...] += dnw
    _stream_add(dmod_ref, 0, is_ctx, dshift)
    _stream_add(dmod_ref, 1, is_ctx, dscale)


def ffn_up_bwd_x(dx2, x, dav, mod, nw, wg, skip_ctx):
    t = x.shape[0]

    def body(dx2_ref, x_ref, dav_ref, mod_ref, nw_ref, w_ref, dx_ref, dnw_ref, dmod_ref):
        i = pl.program_id(0)

        @pl.when(i == 0)
        def _():
            dnw_ref[...] = jnp.zeros_like(dnw_ref)
            dmod_ref[...] = jnp.zeros_like(dmod_ref)

        def work():
            dh = mm_nt(dav_ref[0], w_ref[0])
            for j in range(1, N_DEV):
                dh = dh + mm_nt(dav_ref[j], w_ref[j])
            _norm_mod_bwd(x_ref, nw_ref, mod_ref, 3, dh, dx2_ref[...], dx_ref, dnw_ref, dmod_ref, i == 0)

        _unless_ctx(skip_ctx, i == 0, (dx_ref,), work)

    tile = pl.BlockSpec((TM, D), lambda i: (i, 0))
    return pl.pallas_call(
        body, name="ffn_up_bwd_x", grid=(t // TM,),
        in_specs=[tile, tile, pl.BlockSpec((N_DEV, TM, FF_SLOT), lambda i: (0, i, 0)), pl.BlockSpec((1, 6, 1, D), _stream_row(TM)),
                  pl.BlockSpec((1, D), lambda i: (0, 0)), VMEM_WHOLE],
        out_specs=[tile, pl.BlockSpec((1, D), lambda i: (0, 0)), pl.BlockSpec((2, 2, 1, D), lambda i: (0, 0, 0, 0))],
        out_shape=[jax.ShapeDtypeStruct((t, D), F32), jax.ShapeDtypeStruct((1, D), F32), jax.ShapeDtypeStruct((2, 2, 1, D), F32)],
        compiler_params=_cp(1))(dx2, x, dav, mod, nw, wg)


def weight_grad(at, dout, slot, name, after=None):
    rows, t = at.shape
    stacked = dout.ndim == 3
    n = dout.shape[0] if stacked else dout.shape[1] // slot

    def body(a_ref, d_ref, *rest):
        dw_ref = rest[-1]
        dw_ref[0] = jnp.dot(a_ref[...], d_ref[0] if stacked else d_ref[...], preferred_element_type=F32).astype(dw_ref.dtype)

    d_spec = pl.BlockSpec((1, t, slot), lambda j: (j, 0, 0)) if stacked else pl.BlockSpec((t, slot), lambda j: (0, j))
    extra = [] if after is None else [jnp.reshape(after, (1, 1))]
    return pl.pallas_call(
        body, name=name, grid=(n,), in_specs=[VMEM_WHOLE, d_spec] + [ANY] * len(extra),
        out_specs=pl.BlockSpec((1, rows, slot), lambda j: (j, 0, 0)),
        out_shape=jax.ShapeDtypeStruct((n, rows, slot), GRAD_WIRE), compiler_params=_cp(1))(at, dout, *extra)


def weight_grad_rows(at, dout, name):
    n, t, rows = at.shape
    cols = dout.shape[1]

    def body(a_ref, d_ref, dw_ref):
        dw_ref[0] = _dot(a_ref[0], d_ref[...], ((0,), (0,))).astype(dw_ref.dtype)

    return pl.pallas_call(
        body, name=name, grid=(n,), in_specs=[pl.BlockSpec((1, t, rows), lambda k: (k, 0, 0)), VMEM_WHOLE],
        out_specs=pl.BlockSpec((1, rows, cols), lambda k: (k, 0, 0)),
        out_shape=jax.ShapeDtypeStruct((n, rows, cols), GRAD_WIRE), compiler_params=_cp(1))(at, dout)


def mixer_bwd(dx, parts, o, pa, pb, y, mod, lnw, lnb, sw, sb, hnw, wa, wb, wo, skip_ctx):
    t = dx.shape[0]
    tm = TM
    n_ctx = CTX // tm

    def body(dx_ref, u_ref, v_ref, og_ref, ga_ref, gb_ref, o_ref, pa_ref, pb_ref, y_ref, mod_ref, lnw_ref, lnb_ref, sw_ref,
             sb_ref, hnw_ref, wa_ref, wb_ref, wo_ref, dp_ref, do_ref, dy_ref, dpa_ref, dpb_ref, dlnw_ref, dlnb_ref, dsw_ref,
             dsb_ref, dhnw_ref, dg_ref):
        i = pl.program_id(0)

        @pl.when(i == 0)
        def _():
            for r in (dlnw_ref, dlnb_ref, dsw_ref, dsb_ref, dhnw_ref, dg_ref):
                r[...] = jnp.zeros_like(r)

        def work():
            _, _, vjps, vjp_b = _mixer_tile(slice(0, tm), u_ref, v_ref, og_ref, o_ref, lnw_ref, lnb_ref, sw_ref, sb_ref, hnw_ref)
            pa, pb = pa_ref[...].astype(F32), pb_ref[...].astype(F32)
            sa, sbg = jax.nn.sigmoid(ga_ref[...]), jax.nn.sigmoid(gb_ref[...])
            dxv = dx_ref[...]
            _stream_add(dg_ref, 0, i < n_ctx, jnp.sum(dxv * y_ref[...].astype(F32), axis=0, keepdims=True))
            dy = (mod_ref[0, 2] * dxv).astype(BF16)
            dy_ref[...] = dy
            dmerged = mm_nt(dy, wo_ref[...])
            dpa, dpb = (sa * dmerged).astype(BF16), (sbg * dmerged).astype(BF16)
            dpa_ref[...], dpb_ref[...] = dpa, dpb
            first = 4 * D
            dp_ref[:, first + 3 * D:first + 4 * D] = (dmerged * pa * sa * (1.0 - sa)).astype(BF16)
            dp_ref[:, first + 4 * D:first + 5 * D] = (dmerged * pb * sbg * (1.0 - sbg)).astype(BF16)
            dya = mm_nt(dpa, wa_ref[...])
            dob, dog, dhnw = vjp_b(mm_nt(dpb, wb_ref[...]))
            dp_ref[:, first + 2 * D:first + 3 * D] = dog.astype(BF16)
            dhnw_ref[...] += dhnw
            for g in range(HEADS):
                do_ref[:, _hsl(g)] = dob[g]
            for c, vjp_a in enumerate(vjps):
                rows = slice(c * SGU_CH, (c + 1) * SGU_CH)
                dub, dvb, dlnw, dlnb, dsw, dsb = vjp_a(dya[rows])
                for g in range(HEADS):
                    dp_ref[rows, first + g * HD:first + (g + 1) * HD] = dub[g].astype(BF16)
                    dp_ref[rows, first + D + g * HD:first + D + (g + 1) * HD] = dvb[g].astype(BF16)
                    dlnw_ref[:, _hsl(g)] += dlnw[g]
                    dlnb_ref[:, _hsl(g)] += dlnb[g]
                    dsw_ref[g] += dsw[g]
                    dsb_ref[g] += dsb[g]

        _unless_ctx(skip_ctx, i < n_ctx, (dp_ref, do_ref, dy_ref, dpa_ref, dpb_ref), work)

    vec = lambda n: pl.BlockSpec((1, n), lambda i: (0, 0))
    tile = pl.BlockSpec((tm, D), lambda i: (i, 0))
    sds = jax.ShapeDtypeStruct
    return pl.pallas_call(
        body, name="mixer_bwd", grid=(t // tm,),
        in_specs=[tile] + _part_specs(tm, 4, 5)
        + [pl.BlockSpec((2, tm, D), lambda i: (0, i, 0)), tile, tile, tile, pl.BlockSpec((1, 6, 1, D), _stream_row(tm)),
           vec(D), vec(D), VMEM_WHOLE, VMEM_WHOLE, vec(HD), VMEM_WHOLE, VMEM_WHOLE, VMEM_WHOLE],
        out_specs=[pl.BlockSpec((tm, D_IN), lambda i: (i, 0)), tile, tile, tile, tile, vec(D), vec(D),
                   VMEM_WHOLE, VMEM_WHOLE, vec(HD), pl.BlockSpec((2, 1, 1, D), lambda i: (0, 0, 0, 0))],
        out_shape=[sds((t, D_IN), BF16), sds((t, D), F32), sds((t, D), BF16), sds((t, D), BF16), sds((t, D), BF16),
                   sds((1, D), F32), sds((1, D), F32), sds((HEADS, SGU_CH, SGU_CH), F32), sds((HEADS, SGU_CH, 1), F32),
                   sds((1, HD), F32), sds((2, 1, 1, D), F32)],
        compiler_params=_cp(1))(dx, parts, parts, parts, parts, parts, o, pa, pb, y, mod, lnw, lnb, sw, sb, hnw, wa, wb, wo)


def hgrn_bwd(d, parts, lb, mc, mtc, mrefc, ck, do, first=None, dparts=None):
    t = parts.shape[0]
    nc = t // CH
    chunk = _scan_chunk(nc)
    rev = lambda s: chunk(d, nc - 1 - s)

    def body(q_ref, f_ref, i_ref, lb_ref, m_ref, mt_ref, mr_ref, ck_ref, do_ref, *rest):
        dst = rest[-1]
        dlb_ref = rest[-2]

        @pl.when(pl.program_id(0) == 0)
        def _():
            dst[...] = jnp.zeros_like(dst)
            dlb_ref[...] = jnp.zeros_like(dlb_ref)

        heads = range(HEADS)
        fn = functools.partial(_hgrn_chunk, m=m_ref[0], mt=mt_ref[0], mref=mr_ref[0])
        _, vjp = jax.vjp(fn, [ck_ref[0, 0, h] for h in heads], [q_ref[:, _hsl(h)] for h in heads],
                         [f_ref[:, _hsl(h)] for h in heads], [i_ref[:, _hsl(h)] for h in heads],
                         [lb_ref[0, :, _hsl(h)] for h in heads])
        dstl, dq, df, di, dlb = vjp(([do_ref[:, _hsl(h)] for h in heads], [dst[h] for h in heads]))
        for h in heads:
            dst[h] = dstl[h]
            dlb_ref[0, :, _hsl(h)] += dlb[h]
            if d == 0:
                dq_ref, df_ref, di_ref = rest[:3]
                dq_ref[:, _hsl(h)] = dq[h].astype(BF16)
                df_ref[:, _hsl(h)] = df[h].astype(BF16)
                di_ref[:, _hsl(h)] = di[h].astype(BF16)
            else:
                dq0_ref, df0_ref, di0_ref, _, dp_ref = rest[:5]
                col = lambda k: slice(k * D + h * HD, k * D + (h + 1) * HD)
                dp_ref[:, col(0)] = (dq0_ref[:, _hsl(h)].astype(F32) + dq[h]).astype(BF16)
                dp_ref[:, col(1)] = df0_ref[:, _hsl(h)]
                dp_ref[:, col(2)] = df[h].astype(BF16)
                dp_ref[:, col(3)] = (di0_ref[:, _hsl(h)].astype(F32) + di[h]).astype(BF16)

    const = lambda s: (d, 0, 0)
    at = lambda k: pl.BlockSpec((CH, D), lambda s: (rev(s), k))
    in_specs = [at(0), at(1 + d), at(3), pl.BlockSpec((1, 1, D), const), pl.BlockSpec((1, CH, CH), const),
                pl.BlockSpec((1, CH, CH), const), pl.BlockSpec((1, CH, 1), const),
                pl.BlockSpec((1, 1, HEADS, HD, HD), lambda s: (d, nc - 1 - s, 0, 0, 0)), at(0)]
    dlb_spec, dlb_shape = pl.BlockSpec((1, 1, D), lambda s: (0, 0, 0)), jax.ShapeDtypeStruct((1, 1, D), F32)
    common = dict(grid=(nc,), scratch_shapes=[pltpu.VMEM((HEADS, HD, HD), F32)], compiler_params=_cp(1))
    if d == 0:
        return pl.pallas_call(body, name="hgrn_bwd_fwd_dir", in_specs=in_specs, out_specs=[at(0)] * 3 + [dlb_spec],
                              out_shape=[jax.ShapeDtypeStruct((t, D), BF16)] * 3 + [dlb_shape], **common,
                              )(parts, parts, parts, lb, mc, mtc, mrefc, ck, do)
    return pl.pallas_call(body, name="hgrn_bwd_bwd_dir", in_specs=in_specs + [at(0)] * 3 + [ANY],
                          out_specs=[pl.BlockSpec((CH, 4 * D), lambda s: (rev(s), 0)), dlb_spec],
                          out_shape=[jax.ShapeDtypeStruct(dparts.shape, BF16), dlb_shape], input_output_aliases={12: 0},
                          **common)(parts, parts, parts, lb, mc, mtc, mrefc, ck, do, *first, dparts)


def in_proj_bwd_x(dx1, x, dparts, mod, nw, wg, after=None, latent_only=False):
    t = x.shape[0]
    tm = TM
    n_ctx = CTX // tm

    def body(dx1_ref, x_ref, dp_ref, mod_ref, nw_ref, w_ref, *rest):
        dx_ref, dnw_ref, dmod_ref = rest[-3:]
        i = pl.program_id(0)

        @pl.when(i == 0)
        def _():
            dnw_ref[...] = jnp.zeros_like(dnw_ref)
            dmod_ref[...] = jnp.zeros_like(dmod_ref)

        dh = mm_nt(dp_ref[:, 0:IN_SLOT], w_ref[0])
        for j in range(1, N_DEV):
            dh = dh + mm_nt(dp_ref[:, j * IN_SLOT:(j + 1) * IN_SLOT], w_ref[j])
        _norm_mod_bwd(x_ref, nw_ref, mod_ref, 0, dh, dx1_ref[...], dx_ref, dnw_ref, dmod_ref, i < n_ctx)

    tile = pl.BlockSpec((tm, D), lambda i: (i, 0))
    extra = [] if after is None else [jnp.reshape(after, (1, 1))]
    return pl.pallas_call(
        body, name="in_proj_bwd_x", grid=(t // tm,),
        in_specs=[tile, tile, pl.BlockSpec((tm, D_IN), lambda i: (i, 0)), pl.BlockSpec((1, 6, 1, D), _stream_row(tm)),
                  pl.BlockSpec((1, D), lambda i: (0, 0)), VMEM_WHOLE] + [ANY] * len(extra),
        out_specs=[pl.BlockSpec((tm, D), lambda i: (jnp.maximum(i - n_ctx, 0), 0)) if latent_only else tile,
                   pl.BlockSpec((1, D), lambda i: (0, 0)), pl.BlockSpec((2, 2, 1, D), lambda i: (0, 0, 0, 0))],
        out_shape=[jax.ShapeDtypeStruct((t - CTX if latent_only else t, D), F32), jax.ShapeDtypeStruct((1, D), F32),
                   jax.ShapeDtypeStruct((2, 2, 1, D), F32)],
        compiler_params=_cp(1))(dx1, x, dparts, mod, nw, wg, *extra)


def _lb_fn(h0, h1):
    m = jnp.maximum(h0, h1)
    e0, e1 = jnp.exp(h0 - m), jnp.exp(h1 - m)
    return e1 / (e0 + e1)


def lower_bounds(hlb):
    def body(h_ref, out_ref):
        out_ref[...] = _lb_fn(h_ref[0:1, :], h_ref[1:2, :])
    return pl.pallas_call(body, name="lower_bounds", out_shape=jax.ShapeDtypeStruct((1, 2 * D), F32))(hlb)


def lower_bounds_bwd(hlb, dlb1):
    def body(h_ref, d_ref, out_ref):
        _, vjp = jax.vjp(_lb_fn, h_ref[0:1, :], h_ref[1:2, :])
        d0, d1 = vjp(d_ref[...])
        out_ref[0:1, :] = d0
        out_ref[1:2, :] = d1
    return pl.pallas_call(body, name="lower_bounds_bwd", out_shape=jax.ShapeDtypeStruct((2, 2 * D), F32))(hlb, dlb1)


def _ada_fn(c_all, cctx8, w, b):
    dot = lambda a, l: jnp.dot(_silu(a), w[l], precision=HIGHEST, preferred_element_type=F32) + b[l]
    return [dot(c_all, l) for l in range(2)], [dot(cctx8, l) for l in range(2)]


def ada_fwd(c_all, cctx8, w, b):
    cols = w.shape[-1]

    def body(c_ref, cc_ref, w_ref, b_ref, out_ref):
        ox, oc = _ada_fn(c_ref[...], cc_ref[...], [w_ref[0], w_ref[1]], [b_ref[0], b_ref[1]])
        for l in range(2):
            out_ref[l, 0] = ox[l]
            out_ref[l, 1] = oc[l]
    return pl.pallas_call(body, name="ada_fwd", out_shape=jax.ShapeDtypeStruct((2, 2, N_DEV, cols), F32),
                          compiler_params=_cp(0))(c_all, cctx8, w, b)


def ada_bwd(c_all, cctx8, w, b, dmx, dmc):
    cols = w.shape[-1]

    def body(c_ref, cc_ref, w_ref, b_ref, dmx_ref, dmc_ref, dw_ref, dc_ref):
        fn = lambda cc, w0, w1: _ada_fn(c_ref[...], cc, [w0, w1], [b_ref[0], b_ref[1]])
        _, vjp = jax.vjp(fn, cc_ref[...], w_ref[0], w_ref[1])
        dcc, dw0, dw1 = vjp(([dmx_ref[0], dmx_ref[1]], [dmc_ref[0], dmc_ref[1]]))
        dw_ref[0] = dw0
        dw_ref[1] = dw1
        dc_ref[...] = jnp.sum(dcc, axis=0, keepdims=True)
    return pl.pallas_call(body, name="ada_bwd", out_shape=[jax.ShapeDtypeStruct((2, D, cols), F32), jax.ShapeDtypeStruct((1, D), F32)],
                          compiler_params=_cp(0))(c_all, cctx8, w, b, dmx, dmc)


def adamw(w, m, v, gparts, name):
    r, c = w.shape
    p = gparts.shape[0]
    rt = r
    while rt % 16 == 0 and (p + 7) * rt * c * 4 * 2 > 24 * 2 ** 20:
        rt //= 2

    def body(w_ref, m_ref, v_ref, g_ref, go_ref, d_ref, mo_ref, vo_ref):
        g = g_ref[0].astype(F32)
        for k in range(1, p):
            g = g + g_ref[k].astype(F32)
        m2 = ADAM_B1 * m_ref[...] + (1.0 - ADAM_B1) * g
        v2 = ADAM_B2 * v_ref[...] + (1.0 - ADAM_B2) * (g * g)
        m_hat = m2 / (1.0 - ADAM_B1 ** ADAM_STEP)
        v_hat = v2 / (1.0 - ADAM_B2 ** ADAM_STEP)
        go_ref[...] = g
        d_ref[...] = -ADAM_LR * (m_hat / (jnp.sqrt(v_hat) + ADAM_EPS) + ADAM_WD * w_ref[...])
        mo_ref[...] = m2
        vo_ref[...] = v2

    tile = pl.BlockSpec((rt, c), lambda i: (i, 0))
    return pl.pallas_call(
        body, name=name, grid=(r // rt,),
        in_specs=[tile, tile, tile, pl.BlockSpec((p, rt, c), lambda i: (0, i, 0))], out_specs=[tile] * 4,
        out_shape=[jax.ShapeDtypeStruct((r, c), F32)] * 4, compiler_params=_cp(1))(w, m, v, gparts)


def _me():
    x, y, c = lax.axis_index("x"), lax.axis_index("y"), lax.axis_index("c")
    return x, y, c, 4 * x + 2 * y + c


def _peer(x, y, c, p):
    fx, fy, fc = (p >> 2) & 1, (p >> 1) & 1, p & 1
    return (1 - x if fx else x, 1 - y if fy else y, 1 - c if fc else c)


def all_gather(arrs, name, after=None):
    n = len(arrs)
    extra = [] if after is None else list(after) if isinstance(after, (list, tuple)) else [after]

    def body(*refs):
        ins, outs = refs[:n], refs[n + len(extra):2 * n + len(extra)]
        send, recv, local = refs[2 * n + len(extra):]
        x, y, c, me = _me()
        copies = []
        for a in range(n):
            lc = pltpu.make_async_copy(ins[a], outs[a].at[me], local.at[a])
            lc.start()
            copies.append(lc)
            for p in range(1, N_DEV):
                cp = pltpu.make_async_remote_copy(src_ref=ins[a], dst_ref=outs[a].at[me], send_sem=send.at[a, p - 1],
                                                  recv_sem=recv.at[a, p - 1], device_id=_peer(x, y, c, p),
                                                  device_id_type=pl.DeviceIdType.MESH)
                cp.start()
                copies.append(cp)
        for cp in copies:
            cp.wait()

    return pl.pallas_call(
        body, name=name, in_specs=[ANY] * (n + len(extra)), out_specs=[ANY] * n,
        out_shape=[jax.ShapeDtypeStruct((N_DEV,) + a.shape, a.dtype) for a in arrs],
        scratch_shapes=[pltpu.SemaphoreType.DMA((n, N_DEV - 1)), pltpu.SemaphoreType.DMA((n, N_DEV - 1)),
                        pltpu.SemaphoreType.DMA((n,))])(*arrs, *extra)


HBM = pl.BlockSpec(memory_space=pltpu.HBM)
SEM = pl.BlockSpec(memory_space=pltpu.SEMAPHORE)


def _in_hbm(a):
    return pltpu.with_memory_space_constraint(a, pltpu.HBM)


ALL_PEERS = tuple(range(1, N_DEV))
SAME_CORE_AND_SIBLING = (1, 2, 4, 6)
OTHER_CHIPS = (2, 4, 6)


def _exchange_refs(srcs, lands, layer, scatter, a, x, y, c, p, forward=False):
    me = 4 * x + 2 * y + c
    px, py, pc = _peer(x, y, c, p) if p else (x, y, c)
    if forward and p:
        slot = lands[a].at[4 * px + 2 * py + pc]
        return slot, slot, _peer(x, y, c, 1)
    dst = lands[a].at[me] if layer is None else lands[a].at[me, layer]
    src = srcs[a].at[4 * px + 2 * py + pc] if scatter else dst
    return src, dst, (px, py, pc)


def exchange_start(srcs, lands, layer, scatter, name, after=None, peers=ALL_PEERS, forward=False):
    n, ns = len(lands), len(srcs)
    extra = [] if after is None else [after]

    def body(*refs):
        ins, lz = refs[:ns], refs[ns:ns + n]
        send, recv = refs[ns + n + len(extra)], refs[ns + n + len(extra) + 1]
        token = refs[-1]
        x, y, c, _ = _me()
        for a in range(n):
            for p in peers:
                src, dst, peer = _exchange_refs(ins, lz, layer, scatter, a, x, y, c, p, forward)
                k = a * (N_DEV - 1) + p - 1
                pltpu.make_async_remote_copy(src_ref=src, dst_ref=dst, send_sem=send.at[k], recv_sem=recv.at[k],
                                             device_id=peer, device_id_type=pl.DeviceIdType.MESH).start()
        token[...] = jnp.zeros_like(token)

    thru = [pltpu.HBM(a.shape, a.dtype) for a in list(srcs) + list(lands)]
    out = pl.pallas_call(
        body, name=name, in_specs=[HBM] * (ns + n) + [ANY] * len(extra),
        out_specs=[SEM, SEM] + [HBM] * (ns + n) + [pl.BlockSpec(memory_space=pltpu.VMEM)],
        out_shape=[pltpu.SemaphoreType.DMA((n * (N_DEV - 1),)), pltpu.SemaphoreType.DMA((n * (N_DEV - 1),))] + thru
        + [jax.ShapeDtypeStruct((8, 128), F32)],
        input_output_aliases={i: 2 + i for i in range(ns + n)},
        compiler_params=pltpu.CompilerParams(has_side_effects=pltpu.SideEffectType.DATAFLOW_SIDE_EFFECTING),
    )(*[_in_hbm(a) for a in list(srcs) + list(lands)], *extra)
    return out[0], out[1], out[2:2 + ns], out[2 + ns:2 + ns + n], out[-1]


def exchange_wait(send, recv, srcs, lands, layer, scatter, after, name, peers=ALL_PEERS):
    n, ns = len(lands), len(srcs)

    def body(*refs):
        ins, lz = refs[:ns], refs[ns:ns + n]
        send_ref, recv_ref = refs[ns + n], refs[ns + n + 1]
        x, y, c, _ = _me()
        for a in range(n):
            for p in peers:
                src, dst, peer = _exchange_refs(ins, lz, layer, scatter, a, x, y, c, 0)
                k = a * (N_DEV - 1) + p - 1
                cp = pltpu.make_async_remote_copy(src_ref=src, dst_ref=dst, send_sem=send_ref.at[k],
                                                  recv_sem=recv_ref.at[k], device_id=peer,
                                                  device_id_type=pl.DeviceIdType.MESH)
                cp.wait_send()
                cp.wait_recv()

    thru = [pltpu.HBM(a.shape, a.dtype) for a in list(srcs) + list(lands)]
    out = pl.pallas_call(
        body, name=name, in_specs=[HBM] * (ns + n) + [SEM, SEM, ANY], out_specs=[HBM] * (ns + n), out_shape=thru,
        input_output_aliases={i: i for i in range(ns + n)},
        compiler_params=pltpu.CompilerParams(has_side_effects=pltpu.SideEffectType.DATAFLOW_SIDE_EFFECTING),
    )(*srcs, *lands, send, recv, after)
    return out[ns:]


def place_own(src, land, me, layer, scatter, name, src_layer=None):
    create = isinstance(land, jax.ShapeDtypeStruct)
    r, c = src.shape[-2:]
    rt = r
    while rt % 32 == 0 and rt * c * 4 > 2 ** 21:
        rt //= 2

    def body(me_ref, src_ref, *rest):
        out_ref = rest[-1]
        out_ref[...] = src_ref[...].reshape(out_ref.shape).astype(out_ref.dtype)

    src_spec = (pl.BlockSpec((1, rt, c), lambda i, m: (m[0], i, 0)) if scatter else
                pl.BlockSpec((rt, c), lambda i, m: (i, 0)) if src_layer is None else
                pl.BlockSpec((1, rt, c), lambda i, m: (src_layer, i, 0)))
    out_spec = (pl.BlockSpec((1, rt, c), lambda i, m: (m[0], i, 0)) if layer is None
                else pl.BlockSpec((1, 1, rt, c), lambda i, m: (m[0], layer, i, 0)))
    grid_spec = pltpu.PrefetchScalarGridSpec(num_scalar_prefetch=1, grid=(r // rt,),
                                             in_specs=[src_spec] + ([] if create else [ANY]), out_specs=out_spec)
    return pl.pallas_call(body, name=name, grid_spec=grid_spec, out_shape=jax.ShapeDtypeStruct(land.shape, land.dtype),
                          input_output_aliases={} if create else {2: 0}, compiler_params=_cp(1),
                          )(*((me, src) if create else (me, src, land)))


def _scan_constants():
    r = lax.broadcasted_iota(jnp.int32, (CH, CH), 0)
    s = lax.broadcasted_iota(jnp.int32, (CH, CH), 1)
    lower = (s <= r).astype(F32)
    t = jnp.arange(CH)[:, None]
    mc = jnp.stack([lower, lower.T])
    mref = jnp.stack([(t <= CH // 2 - 1).astype(F32), (t >= CH // 2).astype(F32)])
    return mc, jnp.stack([lower.T, lower]), mref


def local_step(x, ctx, target, mod, lb, w, fetch=None, publish=None, small_ready=None, small_early=None):
    kept = {}

    def keep(l, part, grads):
        kept[(l, part)] = grads
        return 0.0

    fetch = fetch or (lambda l, part, after: w)
    publish = publish or keep
    n_layers = len(mod)
    mc, mtc, mrefc = _scan_constants()
    xs = jnp.concatenate([ctx, x], axis=0)
    saved, big = [], []
    for l in range(n_layers):
        wl = dict(fetch(l, "in", xs))
        parts, ht = in_proj_fwd(xs, mod[l], w["nw1"][l], wl["win"][l])
        o, ck = hgrn_fwd(parts, lb[l], mc, mtc, mrefc)
        wl.update(fetch(l, "rest", o))
        last = l == n_layers - 1
        x1, pa, pb, ym, yat, ybt, mt = mixer_fwd(xs, parts, o, mod[l], w["lnw"][l], w["lnb"][l], w["sw"][l], w["sb"][l],
                                                 w["hnw"][l], wl["wa"][l], wl["wb"][l], wl["wo"][l], last)
        av, h2t = ffn_up_fwd(x1, mod[l], w["nw2"][l], wl["wup"][l], last)
        x2, ac, y, z = ffn_down_fwd(x1, av, mod[l], w["cw"][l], w["cb"][l], wl["wd"][l], last)
        saved.append((xs, parts, o, ck, x1, av, ac, y, z, ht, h2t, pa, pb, ym, yat, ybt, mt))
        big.append(wl)
        xs = x2
    loss, dx, dfw = loss_fwd_bwd(xs, target, w["fw"])
    g = {k: [None] * n_layers for k in ("nw1", "nw2", "lnw", "lnb", "sw", "sb", "hnw", "cw", "cb")}
    g["fw"] = dfw
    dmod, dlb = [None] * n_layers, [None] * n_layers
    tok = 0.0
    for l in reversed(range(n_layers)):
        x0, parts, o, ck, x1, av, ac, y, z, ht, h2t, pa, pb, ym, yat, ybt, mt = saved[l]
        wl = big[l]
        last = l == n_layers - 1
        dav, dac, dout, dg2 = ffn_down_bwd(dx, ac, av, y, mod[l] + tok, wl["wd"][l], last)
        dwd = weight_grad_rows(z, dout, "ffn_down_bwd_w")
        dav, g["cw"][l], g["cb"][l] = conv_bwd(dav, dac, av, w["cw"][l], last)
        dx1, g["nw2"][l], dmod2 = ffn_up_bwd_x(dx, x1, dav, mod[l], w["nw2"][l], wl["wup"][l], last)
        dwup = weight_grad(h2t, dav, FF_SLOT, "ffn_up_bwd_w")
        tok = publish(l, "ffn", {"wd": dwd, "wup": dwup})
        (dparts, do, dy, dpa, dpb, g["lnw"][l], g["lnb"][l], g["sw"][l], g["sb"][l], g["hnw"][l],
         dg1) = mixer_bwd(dx1, parts, o, pa, pb, ym, mod[l] + tok, w["lnw"][l], w["lnb"][l], w["sw"][l], w["sb"][l],
                          w["hnw"][l], wl["wa"][l], wl["wb"][l], wl["wo"][l], last)
        tok = publish(l, "mix", {"wa": weight_grad(yat, dpa, D, "mixer_bwd_wa"), "wb": weight_grad(ybt, dpb, D, "mixer_bwd_wb"),
                                 "wo": weight_grad(mt, dy, D, "mixer_bwd_wo")})
        if l == 0 and small_early:
            dmod[0] = jnp.concatenate([jnp.zeros((2, 2, 1, D), F32), dg1, dmod2, dg2], axis=1)
            tok = tok + small_early(loss[0, 0], g, dmod, dlb)
        dq, df, di, dlb_f = hgrn_bwd(0, parts, lb[l] + tok, mc, mtc, mrefc, ck, do)
        dparts, dlb_b = hgrn_bwd(1, parts, lb[l], mc, mtc, mrefc, ck, do, (dq, df, di), dparts)
        dlb[l] = jnp.concatenate([dlb_f, dlb_b], axis=0)
        tok = publish(l, "in", {"win": weight_grad(ht, dparts, IN_SLOT, "in_proj_bwd_w")})
        dx, g["nw1"][l], dmod1 = in_proj_bwd_x(dx1, x0, dparts, mod[l], w["nw1"][l], wl["win"][l], after=tok,
                                               latent_only=l == 0)
        dmod[l] = jnp.concatenate([dmod1, dg1, dmod2, dg2], axis=1)
    done = small_ready(loss[0, 0], g, dmod, dlb) if small_ready else 0.0
    for (l, part), grads in kept.items():
        for k, v in grads.items():
            g.setdefault(k, [None] * n_layers)[l] = v
    return loss[0, 0], dx, g, dmod, dlb, done


ROW = 1024
REPLICATED = ("norm1_w", "sgu_ln_w", "sgu_ln_b", "sgu_w", "sgu_b", "hgrn_lower_bounds", "hgrn_norm_w", "norm2_w",
              "ffn_conv_b", "final_norm_w")
WEIGHT_ORDER = ("c_ctx", "ada_w", "ada_b", "norm1_w", "w_in", "sgu_ln_w", "sgu_ln_b", "sgu_w", "sgu_b", "hgrn_lower_bounds",
                "hgrn_norm_w", "w_branch_a", "w_branch_b", "w_out", "norm2_w", "ffn_w_up", "ffn_conv_w", "ffn_conv_b",
                "ffn_w_down", "final_norm_w")


def _rows_of(n):
    return -(-n // (8 * ROW)) * 8


def _pack(arrs, total_rows=None):
    parts = []
    for a in arrs:
        flat = a.reshape(-1).astype(F32)
        rows = _rows_of(flat.shape[0])
        parts.append(jnp.pad(flat, (0, rows * ROW - flat.shape[0])).reshape(rows, ROW))
    have = sum(p.shape[0] for p in parts)
    if total_rows is not None and total_rows > have:
        parts.append(jnp.zeros((total_rows - have, ROW), F32))
    return jnp.concatenate(parts, axis=0)


def _unpack(packed, shapes):
    lead = packed.shape[:-2]
    out, r0 = [], 0
    for s in shapes:
        n = math.prod(s)
        rows = _rows_of(n)
        out.append(packed[..., r0:r0 + rows, :].reshape(lead + (rows * ROW,))[..., :n].reshape(lead + tuple(s)))
        r0 += rows
    return out


def kernel(x, c, ctx, c_ctx, ada_w, ada_b, norm1_w, w_in, sgu_ln_w, sgu_ln_b, sgu_w, sgu_b, hgrn_lower_bounds, hgrn_norm_w, w_branch_a, w_branch_b, w_out, norm2_w, ffn_w_up, ffn_conv_w, ffn_conv_b, ffn_w_down, final_norm_w, loss_target, m_c_ctx, m_ada_w, m_ada_b, m_norm1_w, m_w_in, m_sgu_ln_w, m_sgu_ln_b, m_sgu_w, m_sgu_b, m_hgrn_lower_bounds, m_hgrn_norm_w, m_w_branch_a, m_w_branch_b, m_w_out, m_norm2_w, m_ffn_w_up, m_ffn_conv_w, m_ffn_conv_b, m_ffn_w_down, m_final_norm_w, v_c_ctx, v_ada_w, v_ada_b, v_norm1_w, v_w_in, v_sgu_ln_w, v_sgu_ln_b, v_sgu_w, v_sgu_b, v_hgrn_lower_bounds, v_hgrn_norm_w, v_w_branch_a, v_w_branch_b, v_w_out, v_norm2_w, v_ffn_w_up, v_ffn_conv_w, v_ffn_conv_b, v_ffn_w_down, v_final_norm_w):
    wts = dict(c_ctx=c_ctx, ada_w=ada_w, ada_b=ada_b, norm1_w=norm1_w, w_in=w_in, sgu_ln_w=sgu_ln_w, sgu_ln_b=sgu_ln_b,
               sgu_w=sgu_w, sgu_b=sgu_b, hgrn_lower_bounds=hgrn_lower_bounds, hgrn_norm_w=hgrn_norm_w, w_branch_a=w_branch_a,
               w_branch_b=w_branch_b, w_out=w_out, norm2_w=norm2_w, ffn_w_up=ffn_w_up, ffn_conv_w=ffn_conv_w,
               ffn_conv_b=ffn_conv_b, ffn_w_down=ffn_w_down, final_norm_w=final_norm_w)
    mom1 = dict(c_ctx=m_c_ctx, ada_w=m_ada_w, ada_b=m_ada_b, norm1_w=m_norm1_w, w_in=m_w_in, sgu_ln_w=m_sgu_ln_w,
                sgu_ln_b=m_sgu_ln_b, sgu_w=m_sgu_w, sgu_b=m_sgu_b, hgrn_lower_bounds=m_hgrn_lower_bounds,
                hgrn_norm_w=m_hgrn_norm_w, w_branch_a=m_w_branch_a, w_branch_b=m_w_branch_b, w_out=m_w_out, norm2_w=m_norm2_w,
                ffn_w_up=m_ffn_w_up, ffn_conv_w=m_ffn_conv_w, ffn_conv_b=m_ffn_conv_b, ffn_w_down=m_ffn_w_down,
                final_norm_w=m_final_norm_w)
    mom2 = dict(c_ctx=v_c_ctx, ada_w=v_ada_w, ada_b=v_ada_b, norm1_w=v_norm1_w, w_in=v_w_in, sgu_ln_w=v_sgu_ln_w,
                sgu_ln_b=v_sgu_ln_b, sgu_w=v_sgu_w, sgu_b=v_sgu_b, hgrn_lower_bounds=v_hgrn_lower_bounds,
                hgrn_norm_w=v_hgrn_norm_w, w_branch_a=v_w_branch_a, w_branch_b=v_w_branch_b, w_out=v_w_out, norm2_w=v_norm2_w,
                ffn_w_up=v_ffn_w_up, ffn_conv_w=v_ffn_conv_w, ffn_conv_b=v_ffn_conv_b, ffn_w_down=v_ffn_w_down,
                final_norm_w=v_final_norm_w)
    n_layers = w_in.shape[0]
    layers = range(n_layers)
    me = 4 * lax.axis_index("x") + 2 * lax.axis_index("y") + lax.axis_index("c")
    ada_cols = ada_w.shape[-1]

    big = ("w_in", "ffn_w_up", "w_branch_a", "w_branch_b", "w_out", "ffn_w_down")
    short = {"w_in": "win", "ffn_w_up": "wup", "w_branch_a": "wa", "w_branch_b": "wb", "w_out": "wo", "ffn_w_down": "wd"}
    me1 = me.reshape(1).astype(jnp.int32)
    groups = [[("w_in", 0)], [(k, 0) for k in big[1:]], [("w_in", 1)], [(k, 1) for k in big[1:]]]
    in_flight, started = [], 0.0

    def own_slots(n):
        return [place_own(wts[k], jax.ShapeDtypeStruct((N_DEV,) + wts[k].shape[1:], BF16), me1, None, False,
                          f"gather_own_{short[k]}_{l}", src_layer=l) for k, l in groups[n]]

    def start_group(n, lands, after):
        in_flight.append(exchange_start([], lands, None, False, f"gather_weights_start_{n}", after=after,
                                        peers=SAME_CORE_AND_SIBLING if n == 0 else ALL_PEERS))
        return in_flight[-1][-1]

    (c_all,) = all_gather([c], "gather_c")
    c_all = c_all.reshape(N_DEV, D)
    token = start_group(0, own_slots(0), c_all)
    later = [own_slots(n) for n in range(1, len(groups))]
    cctx8 = jnp.broadcast_to(c_ctx[None, :], (N_DEV, D))
    ada_b_cols = lax.dynamic_slice_in_dim(ada_b, me * ada_cols, ada_cols, axis=1)[:, None, :]
    mod_cols = ada_fwd(c_all, cctx8, ada_w, ada_b_cols)
    xs = jnp.concatenate([ctx[0], x[0]], axis=0)
    lb1 = lower_bounds(hgrn_lower_bounds)
    mod_all, conv_all = all_gather([mod_cols, ffn_conv_w.reshape(n_layers, 9, -1)], "gather_mod_conv",
                                   after=[token, xs, lb1] + [a for lands in later for a in lands])
    conv_full = [conv_all[:, l].transpose(1, 0, 2).reshape(9, N_FFK, FF_SLOT).transpose(1, 0, 2) for l in layers]
    for n in range(1, len(groups)):
        token = start_group(n, later[n - 1], mod_all if n == 1 else token)
    for started_group in in_flight:
        started = started + started_group[-1][0, 0]

    def as_used(k, a):
        return a if k in ("w_in", "ffn_w_up") else a.reshape(N_FFK, FF_SLOT, D) if k == "ffn_w_down" else a.reshape(D, D)

    arrived = {}

    def fetch(l, part, after):
        n = {(0, "in"): 0, (0, "rest"): 1, (1, "in"): 2, (1, "rest"): 3}.get((l, part))
        if n is not None:
            send, recv, _, lands, _ = in_flight[n]
            first = n == 0
            got = exchange_wait(send, recv, [], lands, None, False, after, f"gather_weights_wait_{n}",
                                peers=SAME_CORE_AND_SIBLING if first else ALL_PEERS)
            if first:
                send, recv, _, lands, _ = exchange_start([], got, None, False, "gather_weights_pass_on", peers=OTHER_CHIPS,
                                                         forward=True)
                got = exchange_wait(send, recv, [], lands, None, False, after, "gather_weights_passed_on", peers=OTHER_CHIPS)
            for (k, ll), a in zip(groups[n], got):
                arrived.setdefault(short[k], [None] * n_layers)[ll] = as_used(k, a)
        return arrived

    mod_x = lax.dynamic_index_in_dim(mod_all[:, :, 0], me, axis=2, keepdims=False)
    mod_c = mod_all[:, :, 1, 0]
    mod = [jnp.stack([mod_c[:, l].reshape(6, 1, D), mod_x[:, l].reshape(6, 1, D)]) for l in layers]
    mod[0] = mod[0] + started

    lb = [jnp.zeros((2, 1, D), F32), lb1.reshape(2, 1, D)]

    w = {
        "nw1": [norm1_w[l][None] for l in layers], "nw2": [norm2_w[l][None] for l in layers],
        "lnw": [sgu_ln_w[l][None] for l in layers], "lnb": [sgu_ln_b[l][None] for l in layers],
        "sw": [sgu_w[l] for l in layers], "sb": [sgu_b[l][:, :, None] for l in layers],
        "hnw": [hgrn_norm_w[l][None] for l in layers], "cw": conv_full,
        "cb": [ffn_conv_b[l].reshape(N_FFK, 1, FF_SLOT) for l in layers], "fw": final_norm_w[None],
    }
    long = {v: k for k, v in short.items()}
    landing, sent = {}, []

    def publish(l, part, grads):
        keys = [long[k] for k in grads]
        slots = [a.reshape((N_DEV, -1, a.shape[-1])) for a in grads.values()]
        zones = [place_own(s, landing.get(k, jax.ShapeDtypeStruct((N_DEV, n_layers) + s.shape[1:], s.dtype)), me1, l, True,
                           f"scatter_own_{short[k]}_{l}") for k, s in zip(keys, slots)]
        send, recv, srcs, zones, token = exchange_start(slots, zones, l, True, f"scatter_grads_start_{part}_{l}")
        landing.update(zip(keys, zones))
        sent.append((keys, l, part, send, recv, srcs, token))
        return token[0, 0]

    out = {}
    flat2 = lambda a: a.reshape(-1, a.shape[-1])

    def finish(part, after):
        done = []
        for keys, l, p, send, recv, srcs, _ in sent:
            if p == part:
                zones = exchange_wait(send, recv, srcs, [landing[k] for k in keys], l, True, after,
                                      f"scatter_grads_wait_{part}_{l}")
                landing.update(zip(keys, zones))
                done = keys
        for k in done:
            r = landing[k]
            res = adamw(flat2(wts[k]), flat2(mom1[k]), flat2(mom2[k]), r.reshape(N_DEV, -1, r.shape[-1]), "adamw_" + k)
            out[k] = tuple(a.reshape(wts[k].shape) for a in res)

    rep_rows = -(-sum(_rows_of(wts[k].size) for k in REPLICATED) // 64) * 64
    conv_rows = _rows_of(n_layers * 9 * D_FF)
    dmod_rows = _rows_of(n_layers * 6 * D)
    early = {}

    def small_early(loss_part, g, dmod, dlb):
        d_hlb = lower_bounds_bwd(hgrn_lower_bounds, dlb[1].reshape(1, 2 * D))
        st = lambda k: jnp.stack([jnp.zeros((1, D), F32) if a is None else a for a in g[k]])
        rep_grads = {"norm1_w": st("nw1"), "sgu_ln_w": st("lnw"), "sgu_ln_b": st("lnb"), "sgu_w": st("sw"), "sgu_b": st("sb"),
                     "hgrn_lower_bounds": d_hlb, "hgrn_norm_w": st("hnw"), "norm2_w": st("nw2"), "ffn_conv_b": st("cb"),
                     "final_norm_w": g["fw"]}
        d_conv = jnp.stack([g["cw"][l].transpose(1, 0, 2).reshape(9, D_FF) for l in layers])
        dmod_x = jnp.stack([dmod[l][1].reshape(6 * D) for l in layers])
        dmod_c = jnp.stack([dmod[l][0].reshape(6 * D) for l in layers])
        small = jnp.concatenate([_pack([rep_grads[k] for k in REPLICATED], rep_rows),
                                 _pack([d_conv, dmod_x, dmod_c, loss_part.reshape(1)])], axis=0)
        zone = place_own(small, jax.ShapeDtypeStruct((N_DEV,) + small.shape, F32), me1, None, False, "gather_small_own")
        early["send"], early["recv"], _, early["zones"], token = exchange_start([], [zone], None, False, "gather_small_start")
        return token[0, 0]

    def small_ready(loss_part, g, dmod, dlb):
        late = _pack([g["nw1"][0], dmod[0][1, 0:2], dmod[0][0, 0:2]])
        for part in ("ffn", "mix"):
            finish(part, late)
        (late_all,) = all_gather([late], "gather_small_late", after=out["w_out"][0])
        (small_all,) = exchange_wait(early["send"], early["recv"], [], early["zones"], None, False, late_all,
                                     "gather_small_wait")
        at_x = rep_rows + conv_rows
        small_all = small_all.at[:, 0:1].set(late_all[:, 0:1])
        small_all = small_all.at[:, at_x:at_x + 2].set(late_all[:, 8:10])
        small_all = small_all.at[:, at_x + dmod_rows:at_x + dmod_rows + 2].set(late_all[:, 16:18])
        d_conv_shape, dmod_shape = (n_layers, 9, D_FF), (n_layers, 6 * D)
        conv_g, dmx_all, dmc_all, loss_all = _unpack(small_all[:, rep_rows:], [d_conv_shape, dmod_shape, dmod_shape, (1,)])
        out["loss"] = functools.reduce(lambda a, b: a + b, [loss_all[k, 0] for k in range(N_DEV)])

        rep = adamw(_pack([wts[k] for k in REPLICATED], rep_rows), _pack([mom1[k] for k in REPLICATED], rep_rows),
                    _pack([mom2[k] for k in REPLICATED], rep_rows), small_all, "adamw_replicated")
        rep = [_unpack(r, [wts[k].shape for k in REPLICATED]) for r in rep]
        for n, k in enumerate(REPLICATED):
            out[k] = tuple(r[n] for r in rep)

        conv_mine = lax.dynamic_index_in_dim(conv_g.reshape(N_DEV, n_layers, 9, N_DEV, -1), me, axis=3, keepdims=False)
        res = adamw(flat2(ffn_conv_w), flat2(m_ffn_conv_w), flat2(v_ffn_conv_w),
                    conv_mine.reshape(N_DEV, -1, conv_mine.shape[-1]), "adamw_conv_w")
        out["ffn_conv_w"] = tuple(r.reshape(ffn_conv_w.shape) for r in res)

        out["ada_b"] = tuple(adamw(ada_b, m_ada_b, v_ada_b, jnp.concatenate([dmx_all, dmc_all], axis=0), "adamw_ada_b"))

        cols_of = lambda a: lax.dynamic_slice_in_dim(a, me * ada_cols, ada_cols, axis=2).transpose(1, 0, 2)
        d_ada_w, d_cctx = ada_bwd(c_all, cctx8, ada_w, ada_b_cols, cols_of(dmx_all), cols_of(dmc_all))
        res = adamw(flat2(ada_w), flat2(m_ada_w), flat2(v_ada_w), flat2(d_ada_w)[None], "adamw_ada_w")
        out["ada_w"] = tuple(r.reshape(ada_w.shape) for r in res)
        (d_cctx_all,) = all_gather([d_cctx], "gather_c_ctx_grad")
        res = adamw(c_ctx[None], m_c_ctx[None], v_c_ctx[None], d_cctx_all, "adamw_c_ctx")
        out["c_ctx"] = tuple(r[0] for r in res)
        return d_cctx_all

    _, grad_x, _, _, _, small_done = local_step(x[0], ctx[0], loss_target[0], mod, lb, w, fetch, publish, small_ready,
                                                small_early)
    loss = out["loss"]

    finish("in", small_done)
    return (loss, grad_x[None]) + tuple(out[k][n] for n in range(4) for k in WEIGHT_ORDER)
```

```python
import functools
import math

import jax
import jax.numpy as jnp
from jax import lax
from jax.experimental import pallas as pl
from jax.experimental.pallas import tpu as pltpu

F32 = jnp.float32
BF16 = jnp.bfloat16
HIGHEST = lax.Precision.HIGHEST

N_DEV = 8
AXES = ("x", "y", "c")
D = 1024
CTX = 256
TM = 256
CH = 64
SGU_CH = 128
HEADS = 8
HD = 128
GRID_W = 64
D_IN = 9 * D
IN_SLOT = D_IN // N_DEV
D_FF = 2816
FF_SLOT = 2 * D_FF // N_DEV
N_FFK = D_FF // FF_SLOT
RMS_EPS = 1e-6
LN_EPS = 1e-5
ADAM_LR, ADAM_B1, ADAM_B2, ADAM_EPS, ADAM_WD, ADAM_STEP = 0.001, 0.9, 0.999, 1e-08, 0.01, 10
VMEM_LIMIT_V7X = 56 * 2 ** 20
GRAD_WIRE = jnp.bfloat16

VMEM_WHOLE = pl.BlockSpec(memory_space=pltpu.VMEM)
ANY = pl.BlockSpec(memory_space=pl.ANY)


def _cp(n_axes):
    return pltpu.CompilerParams(dimension_semantics=("arbitrary",) * n_axes, vmem_limit_bytes=VMEM_LIMIT_V7X)


def _dot(a, b, dims):
    return lax.dot_general(a.astype(BF16), b.astype(BF16), (dims, ((), ())), preferred_element_type=F32)


@jax.custom_vjp
def mm(a, b):
    return _dot(a, b, ((1,), (0,)))


mm.defvjp(lambda a, b: (mm(a, b), (a, b)),
          lambda r, g: (_dot(g, r[1], ((1,), (1,))).astype(r[0].dtype), _dot(r[0], g, ((0,), (0,))).astype(r[1].dtype)))


@jax.custom_vjp
def mm_nt(a, b):
    return _dot(a, b, ((1,), (1,)))


mm_nt.defvjp(lambda a, b: (mm_nt(a, b), (a, b)),
             lambda r, g: (_dot(g, r[1], ((1,), (0,))).astype(r[0].dtype), _dot(g, r[0], ((0,), (0,))).astype(r[1].dtype)))


@jax.custom_vjp
def mm_tn(a, b):
    return _dot(a, b, ((0,), (0,)))


mm_tn.defvjp(lambda a, b: (mm_tn(a, b), (a, b)),
             lambda r, g: (_dot(r[1], g, ((1,), (1,))).astype(r[0].dtype), _dot(r[0], g, ((1,), (0,))).astype(r[1].dtype)))


def _tri_dot(m, g):
    hi = g.astype(BF16)
    low = (g - hi.astype(F32)).astype(BF16)
    n = g.shape[1]
    out = jnp.dot(m.astype(BF16), jnp.concatenate([hi, low], axis=1), preferred_element_type=F32)
    return out[:, :n] + out[:, n:]


@jax.custom_vjp
def _cum(m, mt, g):
    return _tri_dot(m, g)


_cum.defvjp(lambda m, mt, g: (_cum(m, mt, g), (m, mt)),
            lambda r, d: (jnp.zeros_like(r[0]), jnp.zeros_like(r[1]), _tri_dot(r[1], d)))


def _silu(x):
    return x * jax.nn.sigmoid(x)


def _gelu(x):
    return 0.5 * x * (1.0 + jnp.tanh(math.sqrt(2.0 / math.pi) * (x + 0.044715 * (x * x * x))))


def _rms(x, w):
    return x * lax.rsqrt(jnp.mean(x * x, axis=-1, keepdims=True) + RMS_EPS) * w


def _norm_mod(x, w, shift, scale):
    return _rms(x, w) * (1.0 + scale) + shift


def _hsl(h):
    return slice(h * HD, (h + 1) * HD)


def _hgrn_chunk(st, qz, fz, iv, lb, m, mt, mref):
    hs = range(HEADS)
    keep = [1.0 - lb[h] for h in hs]
    sg = [jax.nn.sigmoid(fz[h]) for h in hs]
    g = [jnp.log(lb[h] + keep[h] * sg[h]) for h in hs]
    k = [keep[h] * (1.0 - sg[h]) for h in hs]
    q = [_silu(qz[h]) for h in hs]
    b = [_cum(m, mt, g[h]) for h in hs]
    ref = [jnp.sum(mref * g[h], axis=0, keepdims=True) for h in hs]
    last = [jnp.sum(g[h], axis=0, keepdims=True) for h in hs]
    qa = [q[h] * jnp.exp(b[h] - ref[h]) for h in hs]
    ka = [k[h] * jnp.exp(ref[h] - b[h]) for h in hs]
    scores = [jnp.where(m > 0.5, mm_nt(qa[h], ka[h]), 0.0) for h in hs]
    inter = [mm_nt(qa[h] * jnp.exp(ref[h]), st[h]) for h in hs]
    kv = [mm_tn(iv[h], ka[h] * jnp.exp(last[h] - ref[h])) for h in hs]
    outs = [mm(scores[h], iv[h]) + inter[h] for h in hs]
    news = [jnp.exp(last[h]) * st[h] + kv[h] for h in hs]
    return outs, news


def _sgu_fn(ub, vb, lnw, lnb, sw, sb):
    gv = [_gelu(v) for v in vb]
    mu = sum(jnp.sum(t, axis=-1, keepdims=True) for t in gv) / D
    var = sum(jnp.sum((t - mu) * (t - mu), axis=-1, keepdims=True) for t in gv) / D
    inv = lax.rsqrt(var + LN_EPS)
    cols = []
    for g in range(HEADS):
        vn = (gv[g] - mu) * inv * lnw[g] + lnb[g]
        cols.append(_gelu(ub[g]) * (mm(sw[g], vn) + sb[g]))
    return jnp.concatenate(cols, axis=1)


def _readout_fn(ob, og, hnw):
    r = [o * lax.rsqrt(jnp.mean(o * o, axis=-1, keepdims=True) + RMS_EPS) * hnw for o in ob]
    return jnp.concatenate(r, axis=1) * _silu(og)


def _glu_fn(ac, v):
    return _gelu(ac) * v


def _stream_row(tm):
    n_ctx = CTX // tm
    return lambda i: (jnp.where(i < n_ctx, 0, 1), 0, 0, 0)


def in_proj_fwd(x, mod, nw, wg):
    t = x.shape[0]

    def body(x_ref, mod_ref, nw_ref, w_ref, out_ref, ht_ref):
        h32 = _norm_mod(x_ref[...], nw_ref[...], mod_ref[0, 0], mod_ref[0, 1])
        ht_ref[...] = h32.T.astype(BF16)
        h = h32.astype(BF16)
        for j in range(N_DEV):
            out_ref[:, j * IN_SLOT:(j + 1) * IN_SLOT] = jnp.dot(h, w_ref[j], preferred_element_type=F32)

    return pl.pallas_call(
        body, name="in_proj_fwd", grid=(t // TM,),
        in_specs=[pl.BlockSpec((TM, D), lambda i: (i, 0)), pl.BlockSpec((1, 6, 1, D), _stream_row(TM)),
                  pl.BlockSpec((1, D), lambda i: (0, 0)), VMEM_WHOLE],
        out_specs=[pl.BlockSpec((TM, D_IN), lambda i: (i, 0)), pl.BlockSpec((D, TM), lambda i: (0, i))],
        out_shape=[jax.ShapeDtypeStruct((t, D_IN), F32), jax.ShapeDtypeStruct((D, t), BF16)],
        compiler_params=_cp(1))(x, mod, nw, wg)


def _scan_chunk(nc):
    ncc = CTX // CH

    def chunk(d, s):
        bwd = jnp.where(s < ncc, ncc - 1 - s, nc + ncc - 1 - s)
        return jnp.where(d == 0, s, bwd)
    return chunk


def hgrn_fwd(parts, lb, mc, mtc, mrefc):
    t = parts.shape[0]
    nc = t // CH
    chunk = _scan_chunk(nc)

    def body(q_ref, f_ref, i_ref, lb_ref, m_ref, mt_ref, mr_ref, o_ref, ck_ref, st):
        @pl.when(pl.program_id(1) == 0)
        def _():
            st[...] = jnp.zeros_like(st)
        ck_ref[0, 0] = st[...]
        outs, news = _hgrn_chunk([st[h] for h in range(HEADS)], [q_ref[:, _hsl(h)] for h in range(HEADS)],
                                 [f_ref[:, _hsl(h)] for h in range(HEADS)], [i_ref[:, _hsl(h)] for h in range(HEADS)],
                                 [lb_ref[0, :, _hsl(h)] for h in range(HEADS)], m_ref[0], mt_ref[0], mr_ref[0])
        for h in range(HEADS):
            o_ref[0, :, _hsl(h)] = outs[h]
            st[h] = news[h]

    const = lambda d, s: (d, 0, 0)
    return pl.pallas_call(
        body, name="hgrn_fwd", grid=(2, nc),
        in_specs=[pl.BlockSpec((CH, D), lambda d, s: (chunk(d, s), 0)), pl.BlockSpec((CH, D), lambda d, s: (chunk(d, s), 1 + d)),
                  pl.BlockSpec((CH, D), lambda d, s: (chunk(d, s), 3)), pl.BlockSpec((1, 1, D), const),
                  pl.BlockSpec((1, CH, CH), const), pl.BlockSpec((1, CH, CH), const), pl.BlockSpec((1, CH, 1), const)],
        out_specs=[pl.BlockSpec((1, CH, D), lambda d, s: (d, chunk(d, s), 0)),
                   pl.BlockSpec((1, 1, HEADS, HD, HD), lambda d, s: (d, s, 0, 0, 0))],
        out_shape=[jax.ShapeDtypeStruct((2, t, D), F32), jax.ShapeDtypeStruct((2, nc, HEADS, HD, HD), F32)],
        scratch_shapes=[pltpu.VMEM((HEADS, HD, HD), F32)], compiler_params=_cp(2))(parts, parts, parts, lb, mc, mtc, mrefc)


def _mixer_tile(rows, u_ref, v_ref, og_ref, o_ref, lnw_ref, lnb_ref, sw_ref, sb_ref, hnw_ref):
    n = (rows.stop - rows.start) // SGU_CH
    yas, vjps = [], []
    for c in range(n):
        r = slice(rows.start + c * SGU_CH, rows.start + (c + 1) * SGU_CH)
        ya, vjp_a = jax.vjp(_sgu_fn, [u_ref[r, _hsl(g)] for g in range(HEADS)], [v_ref[r, _hsl(g)] for g in range(HEADS)],
                            [lnw_ref[:, _hsl(g)] for g in range(HEADS)], [lnb_ref[:, _hsl(g)] for g in range(HEADS)],
                            [sw_ref[g] for g in range(HEADS)], [sb_ref[g] for g in range(HEADS)])
        yas.append(ya)
        vjps.append(vjp_a)
    yb, vjp_b = jax.vjp(_readout_fn, [o_ref[0, rows, _hsl(h)] + o_ref[1, rows, _hsl(h)] for h in range(HEADS)],
                        og_ref[rows, :], hnw_ref[...])
    return (yas[0] if n == 1 else jnp.concatenate(yas, axis=0)), yb, vjps, vjp_b


def _part_specs(tm, first, n):
    return [pl.BlockSpec((tm, D), functools.partial(lambda k, i: (i, k), first + k)) for k in range(n)]


def _unless_ctx(skip_ctx, is_ctx, zero_refs, work):
    if not skip_ctx:
        return work()

    @pl.when(is_ctx)
    def _():
        for r in zero_refs:
            r[...] = jnp.zeros_like(r)

    pl.when(jnp.logical_not(is_ctx))(work)


def mixer_fwd(x, parts, o, mod, lnw, lnb, sw, sb, hnw, wa, wb, wo, skip_ctx):
    t = x.shape[0]

    def body(x_ref, u_ref, v_ref, og_ref, ga_ref, gb_ref, o_ref, mod_ref, lnw_ref, lnb_ref, sw_ref, sb_ref, hnw_ref,
             wa_ref, wb_ref, wo_ref, out_ref, pa_ref, pb_ref, y_ref, yat_ref, ybt_ref, mt_ref):
        def work():
            ya, yb, _, _ = _mixer_tile(slice(0, TM), u_ref, v_ref, og_ref, o_ref, lnw_ref, lnb_ref, sw_ref, sb_ref, hnw_ref)
            pa, pb = mm(ya, wa_ref[...]), mm(yb, wb_ref[...])
            merged = jax.nn.sigmoid(ga_ref[...]) * pa + jax.nn.sigmoid(gb_ref[...]) * pb
            y = mm(merged, wo_ref[...])
            out_ref[...] = x_ref[...] + mod_ref[0, 2] * y
            pa_ref[...], pb_ref[...], y_ref[...] = pa.astype(BF16), pb.astype(BF16), y.astype(BF16)
            yat_ref[...], ybt_ref[...], mt_ref[...] = ya.T.astype(BF16), yb.T.astype(BF16), merged.T.astype(BF16)

        _unless_ctx(skip_ctx, pl.program_id(0) == 0, (out_ref, pa_ref, pb_ref, y_ref, yat_ref, ybt_ref, mt_ref), work)

    vec = lambda n: pl.BlockSpec((1, n), lambda i: (0, 0))
    tile = pl.BlockSpec((TM, D), lambda i: (i, 0))
    tile_t = pl.BlockSpec((D, TM), lambda i: (0, i))
    return pl.pallas_call(
        body, name="mixer_fwd", grid=(t // TM,),
        in_specs=[tile] + _part_specs(TM, 4, 5)
        + [pl.BlockSpec((2, TM, D), lambda i: (0, i, 0)), pl.BlockSpec((1, 6, 1, D), _stream_row(TM)), vec(D), vec(D),
           VMEM_WHOLE, VMEM_WHOLE, vec(HD), VMEM_WHOLE, VMEM_WHOLE, VMEM_WHOLE],
        out_specs=[tile] * 4 + [tile_t] * 3,
        out_shape=[jax.ShapeDtypeStruct((t, D), F32)] + [jax.ShapeDtypeStruct((t, D), BF16)] * 3
        + [jax.ShapeDtypeStruct((D, t), BF16)] * 3, compiler_params=_cp(1),
    )(x, parts, parts, parts, parts, parts, o, mod, lnw, lnb, sw, sb, hnw, wa, wb, wo)


def ffn_up_fwd(x, mod, nw, wg, skip_ctx):
    t = x.shape[0]

    def body(x_ref, mod_ref, nw_ref, w_ref, out_ref, ht_ref):
        def work():
            h32 = _norm_mod(x_ref[...], nw_ref[...], mod_ref[0, 3], mod_ref[0, 4])
            ht_ref[...] = h32.T.astype(BF16)
            h = h32.astype(BF16)
            for j in range(N_DEV):
                out_ref[j] = jnp.dot(h, w_ref[j], preferred_element_type=F32)

        _unless_ctx(skip_ctx, pl.program_id(0) == 0, (out_ref, ht_ref), work)

    return pl.pallas_call(
        body, name="ffn_up_fwd", grid=(t // TM,),
        in_specs=[pl.BlockSpec((TM, D), lambda i: (i, 0)), pl.BlockSpec((1, 6, 1, D), _stream_row(TM)),
                  pl.BlockSpec((1, D), lambda i: (0, 0)), VMEM_WHOLE],
        out_specs=[pl.BlockSpec((N_DEV, TM, FF_SLOT), lambda i: (0, i, 0)), pl.BlockSpec((D, TM), lambda i: (0, i))],
        out_shape=[jax.ShapeDtypeStruct((N_DEV, t, FF_SLOT), F32), jax.ShapeDtypeStruct((D, t), BF16)],
        compiler_params=_cp(1))(x, mod, nw, wg)


def _halo_specs(nt, k_of, i_of):
    per = TM // GRID_W
    last = nt * per - 1
    return [pl.BlockSpec((1, GRID_W, FF_SLOT), lambda *g: (k_of(*g), jnp.maximum(i_of(*g) * per - 1, 0), 0)),
            pl.BlockSpec((1, TM, FF_SLOT), lambda *g: (k_of(*g), i_of(*g), 0)),
            pl.BlockSpec((1, GRID_W, FF_SLOT), lambda *g: (k_of(*g), jnp.minimum(i_of(*g) * per + per, last), 0))]


def _with_halo(prev_ref, main_ref, next_ref, i, nt):
    prev = jnp.where(i >= 2, prev_ref[0], 0.0)
    nxt = jnp.where((i >= 1) & (i <= nt - 2), next_ref[0], 0.0)
    return jnp.concatenate([prev, main_ref[0], nxt], axis=0)


def _tap_valid(dc, i, n_rows, offset):
    r = lax.broadcasted_iota(jnp.int32, (n_rows, 1), 0) - offset
    col = jnp.bitwise_and(r, GRID_W - 1)
    pos = jnp.where(i == 0, r, col) + dc
    return (pos >= 0) & (pos < jnp.where(i == 0, TM, GRID_W))


def _row_weight(cw_ref, dr, dc, i):
    w = cw_ref[0, 3 * (dr + 1) + dc + 1:3 * (dr + 1) + dc + 2, :]
    return w if dr == 0 else jnp.where(i == 0, 0.0, w)


def ffn_down_fwd(x, av, mod, cw, cb, wd, skip_ctx):
    t = x.shape[0]
    nt = t // TM
    ext = TM + 2 * GRID_W

    def body(x_ref, ap_ref, am_ref, an_ref, v_ref, mod_ref, cw_ref, cb_ref, wd_ref, out_ref, ac_ref, y_ref, z_ref, acc):
        i, k = pl.program_id(0), pl.program_id(1)

        def work():
            a_ext = _with_halo(ap_ref, am_ref, an_ref, i, nt)
            conv = jnp.zeros((TM, FF_SLOT), F32) + cb_ref[0]
            for dc in (-1, 0, 1):
                rolled = (a_ext if dc == 0 else
                          jnp.where(_tap_valid(dc, i, ext, GRID_W), pltpu.roll(a_ext, (-dc) % ext, 0), 0.0))
                for dr in (-1, 0, 1):
                    lo = GRID_W + GRID_W * dr
                    conv = conv + rolled[lo:lo + TM] * _row_weight(cw_ref, dr, dc, i)
            ac_ref[0] = conv
            z = _glu_fn(conv, v_ref[0]).astype(BF16)
            z_ref[0] = z
            part = mm(z, wd_ref[0])

            @pl.when(k == 0)
            def _():
                acc[...] = part

            @pl.when(k > 0)
            def _():
                acc[...] += part

            @pl.when(k == N_FFK - 1)
            def _():
                y_ref[...] = acc[...]
                out_ref[...] = x_ref[...] + mod_ref[0, 5] * acc[...]

        _unless_ctx(skip_ctx, i == 0, (out_ref, ac_ref, y_ref, z_ref), work)

    tile = pl.BlockSpec((TM, D), lambda i, k: (i, 0))
    return pl.pallas_call(
        body, name="ffn_down_fwd", grid=(nt, N_FFK),
        in_specs=[tile] + _halo_specs(nt, lambda i, k: k, lambda i, k: i)
        + [pl.BlockSpec((1, TM, FF_SLOT), lambda i, k: (N_FFK + k, i, 0)),
           pl.BlockSpec((1, 6, 1, D), lambda i, k: (jnp.where(i < 1, 0, 1), 0, 0, 0)),
           pl.BlockSpec((1, 9, FF_SLOT), lambda i, k: (k, 0, 0)), pl.BlockSpec((1, 1, FF_SLOT), lambda i, k: (k, 0, 0)),
           pl.BlockSpec((1, FF_SLOT, D), lambda i, k: (k, 0, 0))],
        out_specs=[tile, pl.BlockSpec((1, TM, FF_SLOT), lambda i, k: (k, i, 0)), tile,
                   pl.BlockSpec((1, TM, FF_SLOT), lambda i, k: (k, i, 0))],
        out_shape=[jax.ShapeDtypeStruct((t, D), F32), jax.ShapeDtypeStruct((N_FFK, t, FF_SLOT), F32),
                   jax.ShapeDtypeStruct((t, D), F32), jax.ShapeDtypeStruct((N_FFK, t, FF_SLOT), BF16)],
        scratch_shapes=[pltpu.VMEM((TM, D), F32)], compiler_params=_cp(2))(x, av, av, av, av, mod, cw, cb, wd)


def loss_fwd_bwd(x, target, fw):
    t = x.shape[0]

    def body(x_ref, t_ref, w_ref, loss_ref, dx_ref, dw_ref):
        i = pl.program_id(0)

        @pl.when(i == 0)
        def _():
            loss_ref[...] = jnp.zeros_like(loss_ref)
            dw_ref[...] = jnp.zeros_like(dw_ref)
            dx_ref[...] = jnp.zeros_like(dx_ref)

        @pl.when(i > 0)
        def _():
            y, vjp = jax.vjp(_rms, x_ref[...], w_ref[...])
            err = y - t_ref[...]
            loss_ref[...] += 0.5 * jnp.sum(jnp.sum(err * err, axis=-1, keepdims=True) / D)
            dx, dw = vjp(err / D)
            dx_ref[...] = dx
            dw_ref[...] += dw

    return pl.pallas_call(
        body, name="loss_fwd_bwd", grid=(t // TM,),
        in_specs=[pl.BlockSpec((TM, D), lambda i: (i, 0)), pl.BlockSpec((TM, D), lambda i: (jnp.maximum(i - 1, 0), 0)),
                  pl.BlockSpec((1, D), lambda i: (0, 0))],
        out_specs=[pl.BlockSpec((8, 128), lambda i: (0, 0)), pl.BlockSpec((TM, D), lambda i: (i, 0)),
                   pl.BlockSpec((1, D), lambda i: (0, 0))],
        out_shape=[jax.ShapeDtypeStruct((8, 128), F32), jax.ShapeDtypeStruct((t, D), F32), jax.ShapeDtypeStruct((1, D), F32)],
        compiler_params=_cp(1))(x, target, fw)


def _stream_add(ref, k, is_ctx, val):
    ref[0, k] += jnp.where(is_ctx, val, 0.0)
    ref[1, k] += jnp.where(is_ctx, 0.0, val)


def ffn_down_bwd(dx, ac, av, y, mod, wd, skip_ctx):
    t = dx.shape[0]
    nt = t // TM

    def body(dx_ref, ac_ref, v_ref, y_ref, mod_ref, wd_ref, dav_ref, dac_ref, dout_ref, dg_ref):
        i = pl.program_id(0)

        @pl.when(i == 0)
        def _():
            dg_ref[...] = jnp.zeros_like(dg_ref)

        def work():
            _stream_add(dg_ref, 0, i == 0, jnp.sum(dx_ref[...] * y_ref[...], axis=0, keepdims=True))
            dout = (mod_ref[0, 5] * dx_ref[...]).astype(BF16)
            dout_ref[...] = dout
            for k in range(N_FFK):
                _, vjp = jax.vjp(_glu_fn, ac_ref[k], v_ref[k])
                dac, dv = vjp(mm_nt(dout, wd_ref[k]))
                dac_ref[k] = dac
                dav_ref[k] = dv.astype(BF16)

        _unless_ctx(skip_ctx, i == 0, (dav_ref, dac_ref, dout_ref), work)

    tile = pl.BlockSpec((TM, D), lambda i: (i, 0))
    half = lambda first: pl.BlockSpec((N_FFK, TM, FF_SLOT), lambda i: (first, i, 0))
    return pl.pallas_call(
        body, name="ffn_down_bwd", grid=(nt,),
        in_specs=[tile, half(0), half(1), tile, pl.BlockSpec((1, 6, 1, D), _stream_row(TM)), VMEM_WHOLE],
        out_specs=[half(1), half(0), tile, pl.BlockSpec((2, 1, 1, D), lambda i: (0, 0, 0, 0))],
        out_shape=[jax.ShapeDtypeStruct((N_DEV, t, FF_SLOT), BF16), jax.ShapeDtypeStruct((N_FFK, t, FF_SLOT), F32),
                   jax.ShapeDtypeStruct((t, D), BF16), jax.ShapeDtypeStruct((2, 1, 1, D), F32)],
        compiler_params=_cp(1))(dx, ac, av, y, mod, wd)


def conv_bwd(dav, dac, av, cw, skip_ctx):
    t = dac.shape[1]
    nt = t // TM
    ext = TM + 2 * GRID_W

    def body(dav_in, gp_ref, gm_ref, gn_ref, ap_ref, am_ref, an_ref, cw_ref, dav_ref, dcw_ref, dcb_ref):
        k, i = pl.program_id(0), pl.program_id(1)

        @pl.when(i == 0)
        def _():
            dcw_ref[...] = jnp.zeros_like(dcw_ref)
            dcb_ref[...] = jnp.zeros_like(dcb_ref)

        def work():
            g_ext = _with_halo(gp_ref, gm_ref, gn_ref, i, nt)
            a_ext = _with_halo(ap_ref, am_ref, an_ref, i, nt)
            g_main = gm_ref[0]
            dcb_ref[0] += jnp.sum(g_main, axis=0, keepdims=True)
            da = jnp.zeros((TM, FF_SLOT), F32)
            for dc in (-1, 0, 1):
                g_rolled = (g_ext if dc == 0 else
                            pltpu.roll(jnp.where(_tap_valid(dc, i, ext, GRID_W), g_ext, 0.0), dc % ext, 0))
                a_rolled = a_ext if dc == 0 else pltpu.roll(a_ext, (-dc) % ext, 0)
                g_valid = g_main if dc == 0 else jnp.where(_tap_valid(dc, i, TM, 0), g_main, 0.0)
                for dr in (-1, 0, 1):
                    lo = GRID_W - GRID_W * dr
                    da = da + g_rolled[lo:lo + TM] * _row_weight(cw_ref, dr, dc, i)
                    lo = GRID_W + GRID_W * dr
                    tap = 3 * (dr + 1) + dc + 1
                    dw = jnp.sum(g_valid * a_rolled[lo:lo + TM], axis=0, keepdims=True)
                    dcw_ref[0, tap:tap + 1, :] += dw if dr == 0 else jnp.where(i == 0, 0.0, dw)
            dav_ref[0] = da.astype(BF16)

        _unless_ctx(skip_ctx, i == 0, (dav_ref,), work)

    return pl.pallas_call(
        body, name="conv_bwd", grid=(N_FFK, nt),
        in_specs=[ANY] + _halo_specs(nt, lambda k, i: k, lambda k, i: i) + _halo_specs(nt, lambda k, i: k, lambda k, i: i)
        + [pl.BlockSpec((1, 9, FF_SLOT), lambda k, i: (k, 0, 0))],
        out_specs=[pl.BlockSpec((1, TM, FF_SLOT), lambda k, i: (k, i, 0)), pl.BlockSpec((1, 9, FF_SLOT), lambda k, i: (k, 0, 0)),
                   pl.BlockSpec((1, 1, FF_SLOT), lambda k, i: (k, 0, 0))],
        out_shape=[jax.ShapeDtypeStruct(dav.shape, BF16), jax.ShapeDtypeStruct((N_FFK, 9, FF_SLOT), F32),
                   jax.ShapeDtypeStruct((N_FFK, 1, FF_SLOT), F32)],
        input_output_aliases={0: 0}, compiler_params=_cp(2))(dav, dac, dac, dac, av, av, av, cw)


def _norm_mod_bwd(x_ref, nw_ref, mod_ref, k_shift, dh, dx_in, dx_ref, dnw_ref, dmod_ref, is_ctx):
    _, vjp = jax.vjp(_norm_mod, x_ref[...], nw_ref[...], mod_ref[0, k_shift], mod_ref[0, k_shift + 1])
    dx, dnw, dshift, dscale = vjp(dh)
    dx_ref[...] = dx_in + dx
    dnw_ref[...] += dnw
    _stream_add(dmod_ref, 0, is_ctx, dshift)
    _stream_add(dmod_ref, 1, is_ctx, dscale)


def ffn_up_bwd_x(dx2, x, dav, mod, nw, wg, skip_ctx):
    t = x.shape[0]

    def body(dx2_ref, x_ref, dav_ref, mod_ref, nw_ref, w_ref, dx_ref, dnw_ref, dmod_ref):
        i = pl.program_id(0)

        @pl.when(i == 0)
        def _():
            dnw_ref[...] = jnp.zeros_like(dnw_ref)
            dmod_ref[...] = jnp.zeros_like(dmod_ref)

        def work():
            dh = mm_nt(dav_ref[0], w_ref[0])
            for j in range(1, N_DEV):
                dh = dh + mm_nt(dav_ref[j], w_ref[j])
            _norm_mod_bwd(x_ref, nw_ref, mod_ref, 3, dh, dx2_ref[...], dx_ref, dnw_ref, dmod_ref, i == 0)

        _unless_ctx(skip_ctx, i == 0, (dx_ref,), work)

    tile = pl.BlockSpec((TM, D), lambda i: (i, 0))
    return pl.pallas_call(
        body, name="ffn_up_bwd_x", grid=(t // TM,),
        in_specs=[tile, tile, pl.BlockSpec((N_DEV, TM, FF_SLOT), lambda i: (0, i, 0)), pl.BlockSpec((1, 6, 1, D), _stream_row(TM)),
                  pl.BlockSpec((1, D), lambda i: (0, 0)), VMEM_WHOLE],
        out_specs=[tile, pl.BlockSpec((1, D), lambda i: (0, 0)), pl.BlockSpec((2, 2, 1, D), lambda i: (0, 0, 0, 0))],
        out_shape=[jax.ShapeDtypeStruct((t, D), F32), jax.ShapeDtypeStruct((1, D), F32), jax.ShapeDtypeStruct((2, 2, 1, D), F32)],
        compiler_params=_cp(1))(dx2, x, dav, mod, nw, wg)


def weight_grad(at, dout, slot, name, after=None):
    rows, t = at.shape
    stacked = dout.ndim == 3
    n = dout.shape[0] if stacked else dout.shape[1] // slot

    def body(a_ref, d_ref, *rest):
        dw_ref = rest[-1]
        dw_ref[0] = jnp.dot(a_ref[...], d_ref[0] if stacked else d_ref[...], preferred_element_type=F32).astype(dw_ref.dtype)

    d_spec = pl.BlockSpec((1, t, slot), lambda j: (j, 0, 0)) if stacked else pl.BlockSpec((t, slot), lambda j: (0, j))
    extra = [] if after is None else [jnp.reshape(after, (1, 1))]
    return pl.pallas_call(
        body, name=name, grid=(n,), in_specs=[VMEM_WHOLE, d_spec] + [ANY] * len(extra),
        out_specs=pl.BlockSpec((1, rows, slot), lambda j: (j, 0, 0)),
        out_shape=jax.ShapeDtypeStruct((n, rows, slot), GRAD_WIRE), compiler_params=_cp(1))(at, dout, *extra)


def weight_grad_rows(at, dout, name):
    n, t, rows = at.shape
    cols = dout.shape[1]

    def body(a_ref, d_ref, dw_ref):
        dw_ref[0] = _dot(a_ref[0], d_ref[...], ((0,), (0,))).astype(dw_ref.dtype)

    return pl.pallas_call(
        body, name=name, grid=(n,), in_specs=[pl.BlockSpec((1, t, rows), lambda k: (k, 0, 0)), VMEM_WHOLE],
        out_specs=pl.BlockSpec((1, rows, cols), lambda k: (k, 0, 0)),
        out_shape=jax.ShapeDtypeStruct((n, rows, cols), GRAD_WIRE), compiler_params=_cp(1))(at, dout)


def mixer_bwd(dx, parts, o, pa, pb, y, mod, lnw, lnb, sw, sb, hnw, wa, wb, wo, skip_ctx):
    t = dx.shape[0]
    tm = TM
    n_ctx = CTX // tm

    def body(dx_ref, u_ref, v_ref, og_ref, ga_ref, gb_ref, o_ref, pa_ref, pb_ref, y_ref, mod_ref, lnw_ref, lnb_ref, sw_ref,
             sb_ref, hnw_ref, wa_ref, wb_ref, wo_ref, dp_ref, do_ref, dy_ref, dpa_ref, dpb_ref, dlnw_ref, dlnb_ref, dsw_ref,
             dsb_ref, dhnw_ref, dg_ref):
        i = pl.program_id(0)

        @pl.when(i == 0)
        def _():
            for r in (dlnw_ref, dlnb_ref, dsw_ref, dsb_ref, dhnw_ref, dg_ref):
                r[...] = jnp.zeros_like(r)

        def work():
            _, _, vjps, vjp_b = _mixer_tile(slice(0, tm), u_ref, v_ref, og_ref, o_ref, lnw_ref, lnb_ref, sw_ref, sb_ref, hnw_ref)
            pa, pb = pa_ref[...].astype(F32), pb_ref[...].astype(F32)
            sa, sbg = jax.nn.sigmoid(ga_ref[...]), jax.nn.sigmoid(gb_ref[...])
            dxv = dx_ref[...]
            _stream_add(dg_ref, 0, i < n_ctx, jnp.sum(dxv * y_ref[...].astype(F32), axis=0, keepdims=True))
            dy = (mod_ref[0, 2] * dxv).astype(BF16)
            dy_ref[...] = dy
            dmerged = mm_nt(dy, wo_ref[...])
            dpa, dpb = (sa * dmerged).astype(BF16), (sbg * dmerged).astype(BF16)
            dpa_ref[...], dpb_ref[...] = dpa, dpb
            first = 4 * D
            dp_ref[:, first + 3 * D:first + 4 * D] = (dmerged * pa * sa * (1.0 - sa)).astype(BF16)
            dp_ref[:, first + 4 * D:first + 5 * D] = (dmerged * pb * sbg * (1.0 - sbg)).astype(BF16)
            dya = mm_nt(dpa, wa_ref[...])
            dob, dog, dhnw = vjp_b(mm_nt(dpb, wb_ref[...]))
            dp_ref[:, first + 2 * D:first + 3 * D] = dog.astype(BF16)
            dhnw_ref[...] += dhnw
            for g in range(HEADS):
                do_ref[:, _hsl(g)] = dob[g]
            for c, vjp_a in enumerate(vjps):
                rows = slice(c * SGU_CH, (c + 1) * SGU_CH)
                dub, dvb, dlnw, dlnb, dsw, dsb = vjp_a(dya[rows])
                for g in range(HEADS):
                    dp_ref[rows, first + g * HD:first + (g + 1) * HD] = dub[g].astype(BF16)
                    dp_ref[rows, first + D + g * HD:first + D + (g + 1) * HD] = dvb[g].astype(BF16)
                    dlnw_ref[:, _hsl(g)] += dlnw[g]
                    dlnb_ref[:, _hsl(g)] += dlnb[g]
                    dsw_ref[g] += dsw[g]
                    dsb_ref[g] += dsb[g]

        _unless_ctx(skip_ctx, i < n_ctx, (dp_ref, do_ref, dy_ref, dpa_ref, dpb_ref), work)

    vec = lambda n: pl.BlockSpec((1, n), lambda i: (0, 0))
    tile = pl.BlockSpec((tm, D), lambda i: (i, 0))
    sds = jax.ShapeDtypeStruct
    return pl.pallas_call(
        body, name="mixer_bwd", grid=(t // tm,),
        in_specs=[tile] + _part_specs(tm, 4, 5)
        + [pl.BlockSpec((2, tm, D), lambda i: (0, i, 0)), tile, tile, tile, pl.BlockSpec((1, 6, 1, D), _stream_row(tm)),
           vec(D), vec(D), VMEM_WHOLE, VMEM_WHOLE, vec(HD), VMEM_WHOLE, VMEM_WHOLE, VMEM_WHOLE],
        out_specs=[pl.BlockSpec((tm, D_IN), lambda i: (i, 0)), tile, tile, tile, tile, vec(D), vec(D),
                   VMEM_WHOLE, VMEM_WHOLE, vec(HD), pl.BlockSpec((2, 1, 1, D), lambda i: (0, 0, 0, 0))],
        out_shape=[sds((t, D_IN), BF16), sds((t, D), F32), sds((t, D), BF16), sds((t, D), BF16), sds((t, D), BF16),
                   sds((1, D), F32), sds((1, D), F32), sds((HEADS, SGU_CH, SGU_CH), F32), sds((HEADS, SGU_CH, 1), F32),
                   sds((1, HD), F32), sds((2, 1, 1, D), F32)],
        compiler_params=_cp(1))(dx, parts, parts, parts, parts, parts, o, pa, pb, y, mod, lnw, lnb, sw, sb, hnw, wa, wb, wo)


def hgrn_bwd(d, parts, lb, mc, mtc, mrefc, ck, do, first=None, dparts=None):
    t = parts.shape[0]
    nc = t // CH
    chunk = _scan_chunk(nc)
    rev = lambda s: chunk(d, nc - 1 - s)

    def body(q_ref, f_ref, i_ref, lb_ref, m_ref, mt_ref, mr_ref, ck_ref, do_ref, *rest):
        dst = rest[-1]
        dlb_ref = rest[-2]

        @pl.when(pl.program_id(0) == 0)
        def _():
            dst[...] = jnp.zeros_like(dst)
            dlb_ref[...] = jnp.zeros_like(dlb_ref)

        heads = range(HEADS)
        fn = functools.partial(_hgrn_chunk, m=m_ref[0], mt=mt_ref[0], mref=mr_ref[0])
        _, vjp = jax.vjp(fn, [ck_ref[0, 0, h] for h in heads], [q_ref[:, _hsl(h)] for h in heads],
                         [f_ref[:, _hsl(h)] for h in heads], [i_ref[:, _hsl(h)] for h in heads],
                         [lb_ref[0, :, _hsl(h)] for h in heads])
        dstl, dq, df, di, dlb = vjp(([do_ref[:, _hsl(h)] for h in heads], [dst[h] for h in heads]))
        for h in heads:
            dst[h] = dstl[h]
            dlb_ref[0, :, _hsl(h)] += dlb[h]
            if d == 0:
                dq_ref, df_ref, di_ref = rest[:3]
                dq_ref[:, _hsl(h)] = dq[h].astype(BF16)
                df_ref[:, _hsl(h)] = df[h].astype(BF16)
                di_ref[:, _hsl(h)] = di[h].astype(BF16)
            else:
                dq0_ref, df0_ref, di0_ref, _, dp_ref = rest[:5]
                col = lambda k: slice(k * D + h * HD, k * D + (h + 1) * HD)
                dp_ref[:, col(0)] = (dq0_ref[:, _hsl(h)].astype(F32) + dq[h]).astype(BF16)
                dp_ref[:, col(1)] = df0_ref[:, _hsl(h)]
                dp_ref[:, col(2)] = df[h].astype(BF16)
                dp_ref[:, col(3)] = (di0_ref[:, _hsl(h)].astype(F32) + di[h]).astype(BF16)

    const = lambda s: (d, 0, 0)
    at = lambda k: pl.BlockSpec((CH, D), lambda s: (rev(s), k))
    in_specs = [at(0), at(1 + d), at(3), pl.BlockSpec((1, 1, D), const), pl.BlockSpec((1, CH, CH), const),
                pl.BlockSpec((1, CH, CH), const), pl.BlockSpec((1, CH, 1), const),
                pl.BlockSpec((1, 1, HEADS, HD, HD), lambda s: (d, nc - 1 - s, 0, 0, 0)), at(0)]
    dlb_spec, dlb_shape = pl.BlockSpec((1, 1, D), lambda s: (0, 0, 0)), jax.ShapeDtypeStruct((1, 1, D), F32)
    common = dict(grid=(nc,), scratch_shapes=[pltpu.VMEM((HEADS, HD, HD), F32)], compiler_params=_cp(1))
    if d == 0:
        return pl.pallas_call(body, name="hgrn_bwd_fwd_dir", in_specs=in_specs, out_specs=[at(0)] * 3 + [dlb_spec],
                              out_shape=[jax.ShapeDtypeStruct((t, D), BF16)] * 3 + [dlb_shape], **common,
                              )(parts, parts, parts, lb, mc, mtc, mrefc, ck, do)
    return pl.pallas_call(body, name="hgrn_bwd_bwd_dir", in_specs=in_specs + [at(0)] * 3 + [ANY],
                          out_specs=[pl.BlockSpec((CH, 4 * D), lambda s: (rev(s), 0)), dlb_spec],
                          out_shape=[jax.ShapeDtypeStruct(dparts.shape, BF16), dlb_shape], input_output_aliases={12: 0},
                          **common)(parts, parts, parts, lb, mc, mtc, mrefc, ck, do, *first, dparts)


def in_proj_bwd_x(dx1, x, dparts, mod, nw, wg, after=None, latent_only=False):
    t = x.shape[0]
    tm = TM
    n_ctx = CTX // tm

    def body(dx1_ref, x_ref, dp_ref, mod_ref, nw_ref, w_ref, *rest):
        dx_ref, dnw_ref, dmod_ref = rest[-3:]
        i = pl.program_id(0)

        @pl.when(i == 0)
        def _():
            dnw_ref[...] = jnp.zeros_like(dnw_ref)
            dmod_ref[...] = jnp.zeros_like(dmod_ref)

        dh = mm_nt(dp_ref[:, 0:IN_SLOT], w_ref[0])
        for j in range(1, N_DEV):
            dh = dh + mm_nt(dp_ref[:, j * IN_SLOT:(j + 1) * IN_SLOT], w_ref[j])
        _norm_mod_bwd(x_ref, nw_ref, mod_ref, 0, dh, dx1_ref[...], dx_ref, dnw_ref, dmod_ref, i < n_ctx)

    tile = pl.BlockSpec((tm, D), lambda i: (i, 0))
    extra = [] if after is None else [jnp.reshape(after, (1, 1))]
    return pl.pallas_call(
        body, name="in_proj_bwd_x", grid=(t // tm,),
        in_specs=[tile, tile, pl.BlockSpec((tm, D_IN), lambda i: (i, 0)), pl.BlockSpec((1, 6, 1, D), _stream_row(tm)),
                  pl.BlockSpec((1, D), lambda i: (0, 0)), VMEM_WHOLE] + [ANY] * len(extra),
        out_specs=[pl.BlockSpec((tm, D), lambda i: (jnp.maximum(i - n_ctx, 0), 0)) if latent_only else tile,
                   pl.BlockSpec((1, D), lambda i: (0, 0)), pl.BlockSpec((2, 2, 1, D), lambda i: (0, 0, 0, 0))],
        out_shape=[jax.ShapeDtypeStruct((t - CTX if latent_only else t, D), F32), jax.ShapeDtypeStruct((1, D), F32),
                   jax.ShapeDtypeStruct((2, 2, 1, D), F32)],
        compiler_params=_cp(1))(dx1, x, dparts, mod, nw, wg, *extra)


def _lb_fn(h0, h1):
    m = jnp.maximum(h0, h1)
    e0, e1 = jnp.exp(h0 - m), jnp.exp(h1 - m)
    return e1 / (e0 + e1)


def lower_bounds(hlb):
    def body(h_ref, out_ref):
        out_ref[...] = _lb_fn(h_ref[0:1, :], h_ref[1:2, :])
    return pl.pallas_call(body, name="lower_bounds", out_shape=jax.ShapeDtypeStruct((1, 2 * D), F32))(hlb)


def lower_bounds_bwd(hlb, dlb1):
    def body(h_ref, d_ref, out_ref):
        _, vjp = jax.vjp(_lb_fn, h_ref[0:1, :], h_ref[1:2, :])
        d0, d1 = vjp(d_ref[...])
        out_ref[0:1, :] = d0
        out_ref[1:2, :] = d1
    return pl.pallas_call(body, name="lower_bounds_bwd", out_shape=jax.ShapeDtypeStruct((2, 2 * D), F32))(hlb, dlb1)


def _ada_fn(c_all, cctx8, w, b):
    dot = lambda a, l: jnp.dot(_silu(a), w[l], precision=HIGHEST, preferred_element_type=F32) + b[l]
    return [dot(c_all, l) for l in range(2)], [dot(cctx8, l) for l in range(2)]


def ada_fwd(c_all, cctx8, w, b):
    cols = w.shape[-1]

    def body(c_ref, cc_ref, w_ref, b_ref, out_ref):
        ox, oc = _ada_fn(c_ref[...], cc_ref[...], [w_ref[0], w_ref[1]], [b_ref[0], b_ref[1]])
        for l in range(2):
            out_ref[l, 0] = ox[l]
            out_ref[l, 1] = oc[l]
    return pl.pallas_call(body, name="ada_fwd", out_shape=jax.ShapeDtypeStruct((2, 2, N_DEV, cols), F32),
                          compiler_params=_cp(0))(c_all, cctx8, w, b)


def ada_bwd(c_all, cctx8, w, b, dmx, dmc):
    cols = w.shape[-1]

    def body(c_ref, cc_ref, w_ref, b_ref, dmx_ref, dmc_ref, dw_ref, dc_ref):
        fn = lambda cc, w0, w1: _ada_fn(c_ref[...], cc, [w0, w1], [b_ref[0], b_ref[1]])
        _, vjp = jax.vjp(fn, cc_ref[...], w_ref[0], w_ref[1])
        dcc, dw0, dw1 = vjp(([dmx_ref[0], dmx_ref[1]], [dmc_ref[0], dmc_ref[1]]))
        dw_ref[0] = dw0
        dw_ref[1] = dw1
        dc_ref[...] = jnp.sum(dcc, axis=0, keepdims=True)
    return pl.pallas_call(body, name="ada_bwd", out_shape=[jax.ShapeDtypeStruct((2, D, cols), F32), jax.ShapeDtypeStruct((1, D), F32)],
                          compiler_params=_cp(0))(c_all, cctx8, w, b, dmx, dmc)


def adamw(w, m, v, gparts, name):
    r, c = w.shape
    p = gparts.shape[0]
    rt = r
    while rt % 16 == 0 and (p + 7) * rt * c * 4 * 2 > 24 * 2 ** 20:
        rt //= 2

    def body(w_ref, m_ref, v_ref, g_ref, go_ref, d_ref, mo_ref, vo_ref):
        g = g_ref[0].astype(F32)
        for k in range(1, p):
            g = g + g_ref[k].astype(F32)
        m2 = ADAM_B1 * m_ref[...] + (1.0 - ADAM_B1) * g
        v2 = ADAM_B2 * v_ref[...] + (1.0 - ADAM_B2) * (g * g)
        m_hat = m2 / (1.0 - ADAM_B1 ** ADAM_STEP)
        v_hat = v2 / (1.0 - ADAM_B2 ** ADAM_STEP)
        go_ref[...] = g
        d_ref[...] = -ADAM_LR * (m_hat / (jnp.sqrt(v_hat) + ADAM_EPS) + ADAM_WD * w_ref[...])
        mo_ref[...] = m2
        vo_ref[...] = v2

    tile = pl.BlockSpec((rt, c), lambda i: (i, 0))
    return pl.pallas_call(
        body, name=name, grid=(r // rt,),
        in_specs=[tile, tile, tile, pl.BlockSpec((p, rt, c), lambda i: (0, i, 0))], out_specs=[tile] * 4,
        out_shape=[jax.ShapeDtypeStruct((r, c), F32)] * 4, compiler_params=_cp(1))(w, m, v, gparts)


def _me():
    x, y, c = lax.axis_index("x"), lax.axis_index("y"), lax.axis_index("c")
    return x, y, c, 4 * x + 2 * y + c


def _peer(x, y, c, p):
    fx, fy, fc = (p >> 2) & 1, (p >> 1) & 1, p & 1
    return (1 - x if fx else x, 1 - y if fy else y, 1 - c if fc else c)


def all_gather(arrs, name, after=None):
    n = len(arrs)
    extra = [] if after is None else list(after) if isinstance(after, (list, tuple)) else [after]

    def body(*refs):
        ins, outs = refs[:n], refs[n + len(extra):2 * n + len(extra)]
        send, recv, local = refs[2 * n + len(extra):]
        x, y, c, me = _me()
        copies = []
        for a in range(n):
            lc = pltpu.make_async_copy(ins[a], outs[a].at[me], local.at[a])
            lc.start()
            copies.append(lc)
            for p in range(1, N_DEV):
                cp = pltpu.make_async_remote_copy(src_ref=ins[a], dst_ref=outs[a].at[me], send_sem=send.at[a, p - 1],
                                                  recv_sem=recv.at[a, p - 1], device_id=_peer(x, y, c, p),
                                                  device_id_type=pl.DeviceIdType.MESH)
                cp.start()
                copies.append(cp)
        for cp in copies:
            cp.wait()

    return pl.pallas_call(
        body, name=name, in_specs=[ANY] * (n + len(extra)), out_specs=[ANY] * n,
        out_shape=[jax.ShapeDtypeStruct((N_DEV,) + a.shape, a.dtype) for a in arrs],
        scratch_shapes=[pltpu.SemaphoreType.DMA((n, N_DEV - 1)), pltpu.SemaphoreType.DMA((n, N_DEV - 1)),
                        pltpu.SemaphoreType.DMA((n,))])(*arrs, *extra)


HBM = pl.BlockSpec(memory_space=pltpu.HBM)
SEM = pl.BlockSpec(memory_space=pltpu.SEMAPHORE)


def _in_hbm(a):
    return pltpu.with_memory_space_constraint(a, pltpu.HBM)


ALL_PEERS = tuple(range(1, N_DEV))
SAME_CORE_AND_SIBLING = (1, 2, 4, 6)
OTHER_CHIPS = (2, 4, 6)


def _exchange_refs(srcs, lands, layer, scatter, a, x, y, c, p, forward=False):
    me = 4 * x + 2 * y + c
    px, py, pc = _peer(x, y, c, p) if p else (x, y, c)
    if forward and p:
        slot = lands[a].at[4 * px + 2 * py + pc]
        return slot, slot, _peer(x, y, c, 1)
    dst = lands[a].at[me] if layer is None else lands[a].at[me, layer]
    src = srcs[a].at[4 * px + 2 * py + pc] if scatter else dst
    return src, dst, (px, py, pc)


def exchange_start(srcs, lands, layer, scatter, name, after=None, peers=ALL_PEERS, forward=False):
    n, ns = len(lands), len(srcs)
    extra = [] if after is None else [after]

    def body(*refs):
        ins, lz = refs[:ns], refs[ns:ns + n]
        send, recv = refs[ns + n + len(extra)], refs[ns + n + len(extra) + 1]
        token = refs[-1]
        x, y, c, _ = _me()
        for a in range(n):
            for p in peers:
                src, dst, peer = _exchange_refs(ins, lz, layer, scatter, a, x, y, c, p, forward)
                k = a * (N_DEV - 1) + p - 1
                pltpu.make_async_remote_copy(src_ref=src, dst_ref=dst, send_sem=send.at[k], recv_sem=recv.at[k],
                                             device_id=peer, device_id_type=pl.DeviceIdType.MESH).start()
        token[...] = jnp.zeros_like(token)

    thru = [pltpu.HBM(a.shape, a.dtype) for a in list(srcs) + list(lands)]
    out = pl.pallas_call(
        body, name=name, in_specs=[HBM] * (ns + n) + [ANY] * len(extra),
        out_specs=[SEM, SEM] + [HBM] * (ns + n) + [pl.BlockSpec(memory_space=pltpu.VMEM)],
        out_shape=[pltpu.SemaphoreType.DMA((n * (N_DEV - 1),)), pltpu.SemaphoreType.DMA((n * (N_DEV - 1),))] + thru
        + [jax.ShapeDtypeStruct((8, 128), F32)],
        input_output_aliases={i: 2 + i for i in range(ns + n)},
        compiler_params=pltpu.CompilerParams(has_side_effects=pltpu.SideEffectType.DATAFLOW_SIDE_EFFECTING),
    )(*[_in_hbm(a) for a in list(srcs) + list(lands)], *extra)
    return out[0], out[1], out[2:2 + ns], out[2 + ns:2 + ns + n], out[-1]


def exchange_wait(send, recv, srcs, lands, layer, scatter, after, name, peers=ALL_PEERS):
    n, ns = len(lands), len(srcs)

    def body(*refs):
        ins, lz = refs[:ns], refs[ns:ns + n]
        send_ref, recv_ref = refs[ns + n], refs[ns + n + 1]
        x, y, c, _ = _me()
        for a in range(n):
            for p in peers:
                src, dst, peer = _exchange_refs(ins, lz, layer, scatter, a, x, y, c, 0)
                k = a * (N_DEV - 1) + p - 1
                cp = pltpu.make_async_remote_copy(src_ref=src, dst_ref=dst, send_sem=send_ref.at[k],
                                                  recv_sem=recv_ref.at[k], device_id=peer,
                                                  device_id_type=pl.DeviceIdType.MESH)
                cp.wait_send()
                cp.wait_recv()

    thru = [pltpu.HBM(a.shape, a.dtype) for a in list(srcs) + list(lands)]
    out = pl.pallas_call(
        body, name=name, in_specs=[HBM] * (ns + n) + [SEM, SEM, ANY], out_specs=[HBM] * (ns + n), out_shape=thru,
        input_output_aliases={i: i for i in range(ns + n)},
        compiler_params=pltpu.CompilerParams(has_side_effects=pltpu.SideEffectType.DATAFLOW_SIDE_EFFECTING),
    )(*srcs, *lands, send, recv, after)
    return out[ns:]


def place_own(src, land, me, layer, scatter, name, src_layer=None):
    create = isinstance(land, jax.ShapeDtypeStruct)
    r, c = src.shape[-2:]
    rt = r
    while rt % 32 == 0 and rt * c * 4 > 2 ** 21:
        rt //= 2

    def body(me_ref, src_ref, *rest):
        out_ref = rest[-1]
        out_ref[...] = src_ref[...].reshape(out_ref.shape).astype(out_ref.dtype)

    src_spec = (pl.BlockSpec((1, rt, c), lambda i, m: (m[0], i, 0)) if scatter else
                pl.BlockSpec((rt, c), lambda i, m: (i, 0)) if src_layer is None else
                pl.BlockSpec((1, rt, c), lambda i, m: (src_layer, i, 0)))
    out_spec = (pl.BlockSpec((1, rt, c), lambda i, m: (m[0], i, 0)) if layer is None
                else pl.BlockSpec((1, 1, rt, c), lambda i, m: (m[0], layer, i, 0)))
    grid_spec = pltpu.PrefetchScalarGridSpec(num_scalar_prefetch=1, grid=(r // rt,),
                                             in_specs=[src_spec] + ([] if create else [ANY]), out_specs=out_spec)
    return pl.pallas_call(body, name=name, grid_spec=grid_spec, out_shape=jax.ShapeDtypeStruct(land.shape, land.dtype),
                          input_output_aliases={} if create else {2: 0}, compiler_params=_cp(1),
                          )(*((me, src) if create else (me, src, land)))


def _scan_constants():
    r = lax.broadcasted_iota(jnp.int32, (CH, CH), 0)
    s = lax.broadcasted_iota(jnp.int32, (CH, CH), 1)
    lower = (s <= r).astype(F32)
    t = jnp.arange(CH)[:, None]
    mc = jnp.stack([lower, lower.T])
    mref = jnp.stack([(t <= CH // 2 - 1).astype(F32), (t >= CH // 2).astype(F32)])
    return mc, jnp.stack([lower.T, lower]), mref


def local_step(x, ctx, target, mod, lb, w, fetch=None, publish=None, small_ready=None, small_early=None):
    kept = {}

    def keep(l, part, grads):
        kept[(l, part)] = grads
        return 0.0

    fetch = fetch or (lambda l, part, after: w)
    publish = publish or keep
    n_layers = len(mod)
    mc, mtc, mrefc = _scan_constants()
    xs = jnp.concatenate([ctx, x], axis=0)
    saved, big = [], []
    for l in range(n_layers):
        wl = dict(fetch(l, "in", xs))
        parts, ht = in_proj_fwd(xs, mod[l], w["nw1"][l], wl["win"][l])
        o, ck = hgrn_fwd(parts, lb[l], mc, mtc, mrefc)
        wl.update(fetch(l, "rest", o))
        last = l == n_layers - 1
        x1, pa, pb, ym, yat, ybt, mt = mixer_fwd(xs, parts, o, mod[l], w["lnw"][l], w["lnb"][l], w["sw"][l], w["sb"][l],
                                                 w["hnw"][l], wl["wa"][l], wl["wb"][l], wl["wo"][l], last)
        av, h2t = ffn_up_fwd(x1, mod[l], w["nw2"][l], wl["wup"][l], last)
        x2, ac, y, z = ffn_down_fwd(x1, av, mod[l], w["cw"][l], w["cb"][l], wl["wd"][l], last)
        saved.append((xs, parts, o, ck, x1, av, ac, y, z, ht, h2t, pa, pb, ym, yat, ybt, mt))
        big.append(wl)
        xs = x2
    loss, dx, dfw = loss_fwd_bwd(xs, target, w["fw"])
    g = {k: [None] * n_layers for k in ("nw1", "nw2", "lnw", "lnb", "sw", "sb", "hnw", "cw", "cb")}
    g["fw"] = dfw
    dmod, dlb = [None] * n_layers, [None] * n_layers
    tok = 0.0
    for l in reversed(range(n_layers)):
        x0, parts, o, ck, x1, av, ac, y, z, ht, h2t, pa, pb, ym, yat, ybt, mt = saved[l]
        wl = big[l]
        last = l == n_layers - 1
        dav, dac, dout, dg2 = ffn_down_bwd(dx, ac, av, y, mod[l] + tok, wl["wd"][l], last)
        dwd = weight_grad_rows(z, dout, "ffn_down_bwd_w")
        dav, g["cw"][l], g["cb"][l] = conv_bwd(dav, dac, av, w["cw"][l], last)
        dx1, g["nw2"][l], dmod2 = ffn_up_bwd_x(dx, x1, dav, mod[l], w["nw2"][l], wl["wup"][l], last)
        dwup = weight_grad(h2t, dav, FF_SLOT, "ffn_up_bwd_w")
        tok = publish(l, "ffn", {"wd": dwd, "wup": dwup})
        (dparts, do, dy, dpa, dpb, g["lnw"][l], g["lnb"][l], g["sw"][l], g["sb"][l], g["hnw"][l],
         dg1) = mixer_bwd(dx1, parts, o, pa, pb, ym, mod[l] + tok, w["lnw"][l], w["lnb"][l], w["sw"][l], w["sb"][l],
                          w["hnw"][l], wl["wa"][l], wl["wb"][l], wl["wo"][l], last)
        tok = publish(l, "mix", {"wa": weight_grad(yat, dpa, D, "mixer_bwd_wa"), "wb": weight_grad(ybt, dpb, D, "mixer_bwd_wb"),
                                 "wo": weight_grad(mt, dy, D, "mixer_bwd_wo")})
        if l == 0 and small_early:
            dmod[0] = jnp.concatenate([jnp.zeros((2, 2, 1, D), F32), dg1, dmod2, dg2], axis=1)
            tok = tok + small_early(loss[0, 0], g, dmod, dlb)
        dq, df, di, dlb_f = hgrn_bwd(0, parts, lb[l] + tok, mc, mtc, mrefc, ck, do)
        dparts, dlb_b = hgrn_bwd(1, parts, lb[l], mc, mtc, mrefc, ck, do, (dq, df, di), dparts)
        dlb[l] = jnp.concatenate([dlb_f, dlb_b], axis=0)
        tok = publish(l, "in", {"win": weight_grad(ht, dparts, IN_SLOT, "in_proj_bwd_w")})
        dx, g["nw1"][l], dmod1 = in_proj_bwd_x(dx1, x0, dparts, mod[l], w["nw1"][l], wl["win"][l], after=tok,
                                               latent_only=l == 0)
        dmod[l] = jnp.concatenate([dmod1, dg1, dmod2, dg2], axis=1)
    done = small_ready(loss[0, 0], g, dmod, dlb) if small_ready else 0.0
    for (l, part), grads in kept.items():
        for k, v in grads.items():
            g.setdefault(k, [None] * n_layers)[l] = v
    return loss[0, 0], dx, g, dmod, dlb, done


ROW = 1024
REPLICATED = ("norm1_w", "sgu_ln_w", "sgu_ln_b", "sgu_w", "sgu_b", "hgrn_lower_bounds", "hgrn_norm_w", "norm2_w",
              "ffn_conv_b", "final_norm_w")
WEIGHT_ORDER = ("c_ctx", "ada_w", "ada_b", "norm1_w", "w_in", "sgu_ln_w", "sgu_ln_b", "sgu_w", "sgu_b", "hgrn_lower_bounds",
                "hgrn_norm_w", "w_branch_a", "w_branch_b", "w_out", "norm2_w", "ffn_w_up", "ffn_conv_w", "ffn_conv_b",
                "ffn_w_down", "final_norm_w")


def _rows_of(n):
    return -(-n // (8 * ROW)) * 8


def _pack(arrs, total_rows=None):
    parts = []
    for a in arrs:
        flat = a.reshape(-1).astype(F32)
        rows = _rows_of(flat.shape[0])
        parts.append(jnp.pad(flat, (0, rows * ROW - flat.shape[0])).reshape(rows, ROW))
    have = sum(p.shape[0] for p in parts)
    if total_rows is not None and total_rows > have:
        parts.append(jnp.zeros((total_rows - have, ROW), F32))
    return jnp.concatenate(parts, axis=0)


def _unpack(packed, shapes):
    lead = packed.shape[:-2]
    out, r0 = [], 0
    for s in shapes:
        n = math.prod(s)
        rows = _rows_of(n)
        out.append(packed[..., r0:r0 + rows, :].reshape(lead + (rows * ROW,))[..., :n].reshape(lead + tuple(s)))
        r0 += rows
    return out


def kernel(x, c, ctx, c_ctx, ada_w, ada_b, norm1_w, w_in, sgu_ln_w, sgu_ln_b, sgu_w, sgu_b, hgrn_lower_bounds, hgrn_norm_w, w_branch_a, w_branch_b, w_out, norm2_w, ffn_w_up, ffn_conv_w, ffn_conv_b, ffn_w_down, final_norm_w, loss_target, m_c_ctx, m_ada_w, m_ada_b, m_norm1_w, m_w_in, m_sgu_ln_w, m_sgu_ln_b, m_sgu_w, m_sgu_b, m_hgrn_lower_bounds, m_hgrn_norm_w, m_w_branch_a, m_w_branch_b, m_w_out, m_norm2_w, m_ffn_w_up, m_ffn_conv_w, m_ffn_conv_b, m_ffn_w_down, m_final_norm_w, v_c_ctx, v_ada_w, v_ada_b, v_norm1_w, v_w_in, v_sgu_ln_w, v_sgu_ln_b, v_sgu_w, v_sgu_b, v_hgrn_lower_bounds, v_hgrn_norm_w, v_w_branch_a, v_w_branch_b, v_w_out, v_norm2_w, v_ffn_w_up, v_ffn_conv_w, v_ffn_conv_b, v_ffn_w_down, v_final_norm_w):
    wts = dict(c_ctx=c_ctx, ada_w=ada_w, ada_b=ada_b, norm1_w=norm1_w, w_in=w_in, sgu_ln_w=sgu_ln_w, sgu_ln_b=sgu_ln_b,
               sgu_w=sgu_w, sgu_b=sgu_b, hgrn_lower_bounds=hgrn_lower_bounds, hgrn_norm_w=hgrn_norm_w, w_branch_a=w_branch_a,
               w_branch_b=w_branch_b, w_out=w_out, norm2_w=norm2_w, ffn_w_up=ffn_w_up, ffn_conv_w=ffn_conv_w,
               ffn_conv_b=ffn_conv_b, ffn_w_down=ffn_w_down, final_norm_w=final_norm_w)
    mom1 = dict(c_ctx=m_c_ctx, ada_w=m_ada_w, ada_b=m_ada_b, norm1_w=m_norm1_w, w_in=m_w_in, sgu_ln_w=m_sgu_ln_w,
                sgu_ln_b=m_sgu_ln_b, sgu_w=m_sgu_w, sgu_b=m_sgu_b, hgrn_lower_bounds=m_hgrn_lower_bounds,
                hgrn_norm_w=m_hgrn_norm_w, w_branch_a=m_w_branch_a, w_branch_b=m_w_branch_b, w_out=m_w_out, norm2_w=m_norm2_w,
                ffn_w_up=m_ffn_w_up, ffn_conv_w=m_ffn_conv_w, ffn_conv_b=m_ffn_conv_b, ffn_w_down=m_ffn_w_down,
                final_norm_w=m_final_norm_w)
    mom2 = dict(c_ctx=v_c_ctx, ada_w=v_ada_w, ada_b=v_ada_b, norm1_w=v_norm1_w, w_in=v_w_in, sgu_ln_w=v_sgu_ln_w,
                sgu_ln_b=v_sgu_ln_b, sgu_w=v_sgu_w, sgu_b=v_sgu_b, hgrn_lower_bounds=v_hgrn_lower_bounds,
                hgrn_norm_w=v_hgrn_norm_w, w_branch_a=v_w_branch_a, w_branch_b=v_w_branch_b, w_out=v_w_out, norm2_w=v_norm2_w,
                ffn_w_up=v_ffn_w_up, ffn_conv_w=v_ffn_conv_w, ffn_conv_b=v_ffn_conv_b, ffn_w_down=v_ffn_w_down,
                final_norm_w=v_final_norm_w)
    n_layers = w_in.shape[0]
    layers = range(n_layers)
    me = 4 * lax.axis_index("x") + 2 * lax.axis_index("y") + lax.axis_index("c")
    ada_cols = ada_w.shape[-1]

    big = ("w_in", "ffn_w_up", "w_branch_a", "w_branch_b", "w_out", "ffn_w_down")
    short = {"w_in": "win", "ffn_w_up": "wup", "w_branch_a": "wa", "w_branch_b": "wb", "w_out": "wo", "ffn_w_down": "wd"}
    me1 = me.reshape(1).astype(jnp.int32)
    groups = [[("w_in", 0)], [(k, 0) for k in big[1:]], [("w_in", 1)], [(k, 1) for k in big[1:]]]
    in_flight, started = [], 0.0

    def own_slots(n):
        return [place_own(wts[k], jax.ShapeDtypeStruct((N_DEV,) + wts[k].shape[1:], BF16), me1, None, False,
                          f"gather_own_{short[k]}_{l}", src_layer=l) for k, l in groups[n]]

    def start_group(n, lands, after):
        in_flight.append(exchange_start([], lands, None, False, f"gather_weights_start_{n}", after=after,
                                        peers=SAME_CORE_AND_SIBLING if n == 0 else ALL_PEERS))
        return in_flight[-1][-1]

    (c_all,) = all_gather([c], "gather_c")
    c_all = c_all.reshape(N_DEV, D)
    token = start_group(0, own_slots(0), c_all)
    later = [own_slots(n) for n in range(1, len(groups))]
    cctx8 = jnp.broadcast_to(c_ctx[None, :], (N_DEV, D))
    ada_b_cols = lax.dynamic_slice_in_dim(ada_b, me * ada_cols, ada_cols, axis=1)[:, None, :]
    mod_cols = ada_fwd(c_all, cctx8, ada_w, ada_b_cols)
    xs = jnp.concatenate([ctx[0], x[0]], axis=0)
    lb1 = lower_bounds(hgrn_lower_bounds)
    mod_all, conv_all = all_gather([mod_cols, ffn_conv_w.reshape(n_layers, 9, -1)], "gather_mod_conv",
                                   after=[token, xs, lb1] + [a for lands in later for a in lands])
    conv_full = [conv_all[:, l].transpose(1, 0, 2).reshape(9, N_FFK, FF_SLOT).transpose(1, 0, 2) for l in layers]
    for n in range(1, len(groups)):
        token = start_group(n, later[n - 1], mod_all if n == 1 else token)
    for started_group in in_flight:
        started = started + started_group[-1][0, 0]

    def as_used(k, a):
        return a if k in ("w_in", "ffn_w_up") else a.reshape(N_FFK, FF_SLOT, D) if k == "ffn_w_down" else a.reshape(D, D)

    arrived = {}

    def fetch(l, part, after):
        n = {(0, "in"): 0, (0, "rest"): 1, (1, "in"): 2, (1, "rest"): 3}.get((l, part))
        if n is not None:
            send, recv, _, lands, _ = in_flight[n]
            first = n == 0
            got = exchange_wait(send, recv, [], lands, None, False, after, f"gather_weights_wait_{n}",
                                peers=SAME_CORE_AND_SIBLING if first else ALL_PEERS)
            if first:
                send, recv, _, lands, _ = exchange_start([], got, None, False, "gather_weights_pass_on", peers=OTHER_CHIPS,
                                                         forward=True)
                got = exchange_wait(send, recv, [], lands, None, False, after, "gather_weights_passed_on", peers=OTHER_CHIPS)
            for (k, ll), a in zip(groups[n], got):
                arrived.setdefault(short[k], [None] * n_layers)[ll] = as_used(k, a)
        return arrived

    mod_x = lax.dynamic_index_in_dim(mod_all[:, :, 0], me, axis=2, keepdims=False)
    mod_c = mod_all[:, :, 1, 0]
    mod = [jnp.stack([mod_c[:, l].reshape(6, 1, D), mod_x[:, l].reshape(6, 1, D)]) for l in layers]
    mod[0] = mod[0] + started

    lb = [jnp.zeros((2, 1, D), F32), lb1.reshape(2, 1, D)]

    w = {
        "nw1": [norm1_w[l][None] for l in layers], "nw2": [norm2_w[l][None] for l in layers],
        "lnw": [sgu_ln_w[l][None] for l in layers], "lnb": [sgu_ln_b[l][None] for l in layers],
        "sw": [sgu_w[l] for l in layers], "sb": [sgu_b[l][:, :, None] for l in layers],
        "hnw": [hgrn_norm_w[l][None] for l in layers], "cw": conv_full,
        "cb": [ffn_conv_b[l].reshape(N_FFK, 1, FF_SLOT) for l in layers], "fw": final_norm_w[None],
    }
    long = {v: k for k, v in short.items()}
    landing, sent = {}, []

    def publish(l, part, grads):
        keys = [long[k] for k in grads]
        slots = [a.reshape((N_DEV, -1, a.shape[-1])) for a in grads.values()]
        zones = [place_own(s, landing.get(k, jax.ShapeDtypeStruct((N_DEV, n_layers) + s.shape[1:], s.dtype)), me1, l, True,
                           f"scatter_own_{short[k]}_{l}") for k, s in zip(keys, slots)]
        send, recv, srcs, zones, token = exchange_start(slots, zones, l, True, f"scatter_grads_start_{part}_{l}")
        landing.update(zip(keys, zones))
        sent.append((keys, l, part, send, recv, srcs, token))
        return token[0, 0]

    out = {}
    flat2 = lambda a: a.reshape(-1, a.shape[-1])

    def finish(part, after):
        done = []
        for keys, l, p, send, recv, srcs, _ in sent:
            if p == part:
                zones = exchange_wait(send, recv, srcs, [landing[k] for k in keys], l, True, after,
                                      f"scatter_grads_wait_{part}_{l}")
                landing.update(zip(keys, zones))
                done = keys
        for k in done:
            r = landing[k]
            res = adamw(flat2(wts[k]), flat2(mom1[k]), flat2(mom2[k]), r.reshape(N_DEV, -1, r.shape[-1]), "adamw_" + k)
            out[k] = tuple(a.reshape(wts[k].shape) for a in res)

    rep_rows = -(-sum(_rows_of(wts[k].size) for k in REPLICATED) // 64) * 64
    conv_rows = _rows_of(n_layers * 9 * D_FF)
    dmod_rows = _rows_of(n_layers * 6 * D)
    early = {}

    def small_early(loss_part, g, dmod, dlb):
        d_hlb = lower_bounds_bwd(hgrn_lower_bounds, dlb[1].reshape(1, 2 * D))
        st = lambda k: jnp.stack([jnp.zeros((1, D), F32) if a is None else a for a in g[k]])
        rep_grads = {"norm1_w": st("nw1"), "sgu_ln_w": st("lnw"), "sgu_ln_b": st("lnb"), "sgu_w": st("sw"), "sgu_b": st("sb"),
                     "hgrn_lower_bounds": d_hlb, "hgrn_norm_w": st("hnw"), "norm2_w": st("nw2"), "ffn_conv_b": st("cb"),
                     "final_norm_w": g["fw"]}
        d_conv = jnp.stack([g["cw"][l].transpose(1, 0, 2).reshape(9, D_FF) for l in layers])
        dmod_x = jnp.stack([dmod[l][1].reshape(6 * D) for l in layers])
        dmod_c = jnp.stack([dmod[l][0].reshape(6 * D) for l in layers])
        small = jnp.concatenate([_pack([rep_grads[k] for k in REPLICATED], rep_rows),
                                 _pack([d_conv, dmod_x, dmod_c, loss_part.reshape(1)])], axis=0)
        zone = place_own(small, jax.ShapeDtypeStruct((N_DEV,) + small.shape, F32), me1, None, False, "gather_small_own")
        early["send"], early["recv"], _, early["zones"], token = exchange_start([], [zone], None, False, "gather_small_start")
        return token[0, 0]

    def small_ready(loss_part, g, dmod, dlb):
        late = _pack([g["nw1"][0], dmod[0][1, 0:2], dmod[0][0, 0:2]])
        for part in ("ffn", "mix"):
            finish(part, late)
        (late_all,) = all_gather([late], "gather_small_late", after=out["w_out"][0])
        (small_all,) = exchange_wait(early["send"], early["recv"], [], early["zones"], None, False, late_all,
                                     "gather_small_wait")
        at_x = rep_rows + conv_rows
        small_all = small_all.at[:, 0:1].set(late_all[:, 0:1])
        small_all = small_all.at[:, at_x:at_x + 2].set(late_all[:, 8:10])
        small_all = small_all.at[:, at_x + dmod_rows:at_x + dmod_rows + 2].set(late_all[:, 16:18])
        d_conv_shape, dmod_shape = (n_layers, 9, D_FF), (n_layers, 6 * D)
        conv_g, dmx_all, dmc_all, loss_all = _unpack(small_all[:, rep_rows:], [d_conv_shape, dmod_shape, dmod_shape, (1,)])
        out["loss"] = functools.reduce(lambda a, b: a + b, [loss_all[k, 0] for k in range(N_DEV)])

        rep = adamw(_pack([wts[k] for k in REPLICATED], rep_rows), _pack([mom1[k] for k in REPLICATED], rep_rows),
                    _pack([mom2[k] for k in REPLICATED], rep_rows), small_all, "adamw_replicated")
        rep = [_unpack(r, [wts[k].shape for k in REPLICATED]) for r in rep]
        for n, k in enumerate(REPLICATED):
            out[k] = tuple(r[n] for r in rep)

        conv_mine = lax.dynamic_index_in_dim(conv_g.reshape(N_DEV, n_layers, 9, N_DEV, -1), me, axis=3, keepdims=False)
        res = adamw(flat2(ffn_conv_w), flat2(m_ffn_conv_w), flat2(v_ffn_conv_w),
                    conv_mine.reshape(N_DEV, -1, conv_mine.shape[-1]), "adamw_conv_w")
        out["ffn_conv_w"] = tuple(r.reshape(ffn_conv_w.shape) for r in res)

        out["ada_b"] = tuple(adamw(ada_b, m_ada_b, v_ada_b, jnp.concatenate([dmx_all, dmc_all], axis=0), "adamw_ada_b"))

        cols_of = lambda a: lax.dynamic_slice_in_dim(a, me * ada_cols, ada_cols, axis=2).transpose(1, 0, 2)
        d_ada_w, d_cctx = ada_bwd(c_all, cctx8, ada_w, ada_b_cols, cols_of(dmx_all), cols_of(dmc_all))
        res = adamw(flat2(ada_w), flat2(m_ada_w), flat2(v_ada_w), flat2(d_ada_w)[None], "adamw_ada_w")
        out["ada_w"] = tuple(r.reshape(ada_w.shape) for r in res)
        (d_cctx_all,) = all_gather([d_cctx], "gather_c_ctx_grad")
        res = adamw(c_ctx[None], m_c_ctx[None], v_c_ctx[None], d_cctx_all, "adamw_c_ctx")
        out["c_ctx"] = tuple(r[0] for r in res)
        return d_cctx_all

    _, grad_x, _, _, _, small_done = local_step(x[0], ctx[0], loss_target[0], mod, lb, w, fetch, publish, small_ready,
                                                small_early)
    loss = out["loss"]

    finish("in", small_done)
    return (loss, grad_x[None]) + tuple(out[k][n] for n in range(4) for k in WEIGHT_ORDER)
```

```python
import functools
import math

import jax
import jax.numpy as jnp
from jax import lax
from jax.experimental import pallas as pl
from jax.experimental.pallas import tpu as pltpu

F32 = jnp.float32
BF16 = jnp.bfloat16
HIGHEST = lax.Precision.HIGHEST

N_DEV = 8
AXES = ("x", "y", "c")
D = 1024
CTX = 256
TM = 256
CH = 64
SGU_CH = 128
HEADS = 8
HD = 128
GRID_W = 64
D_IN = 9 * D
IN_SLOT = D_IN // N_DEV
D_FF = 2816
FF_SLOT = 2 * D_FF // N_DEV
N_FFK = D_FF // FF_SLOT
RMS_EPS = 1e-6
LN_EPS = 1e-5
ADAM_LR, ADAM_B1, ADAM_B2, ADAM_EPS, ADAM_WD, ADAM_STEP = 0.001, 0.9, 0.999, 1e-08, 0.01, 10
VMEM_LIMIT_V7X = 56 * 2 ** 20
GRAD_WIRE = jnp.bfloat16

VMEM_WHOLE = pl.BlockSpec(memory_space=pltpu.VMEM)
ANY = pl.BlockSpec(memory_space=pl.ANY)


def _cp(n_axes):
    return pltpu.CompilerParams(dimension_semantics=("arbitrary",) * n_axes, vmem_limit_bytes=VMEM_LIMIT_V7X)


def _dot(a, b, dims):
    return lax.dot_general(a.astype(BF16), b.astype(BF16), (dims, ((), ())), preferred_element_type=F32)


@jax.custom_vjp
def mm(a, b):
    return _dot(a, b, ((1,), (0,)))


mm.defvjp(lambda a, b: (mm(a, b), (a, b)),
          lambda r, g: (_dot(g, r[1], ((1,), (1,))).astype(r[0].dtype), _dot(r[0], g, ((0,), (0,))).astype(r[1].dtype)))


@jax.custom_vjp
def mm_nt(a, b):
    return _dot(a, b, ((1,), (1,)))


mm_nt.defvjp(lambda a, b: (mm_nt(a, b), (a, b)),
             lambda r, g: (_dot(g, r[1], ((1,), (0,))).astype(r[0].dtype), _dot(g, r[0], ((0,), (0,))).astype(r[1].dtype)))


@jax.custom_vjp
def mm_tn(a, b):
    return _dot(a, b, ((0,), (0,)))


mm_tn.defvjp(lambda a, b: (mm_tn(a, b), (a, b)),
             lambda r, g: (_dot(r[1], g, ((1,), (1,))).astype(r[0].dtype), _dot(r[0], g, ((1,), (0,))).astype(r[1].dtype)))


def _tri_dot(m, g):
    hi = g.astype(BF16)
    low = (g - hi.astype(F32)).astype(BF16)
    n = g.shape[1]
    out = jnp.dot(m.astype(BF16), jnp.concatenate([hi, low], axis=1), preferred_element_type=F32)
    return out[:, :n] + out[:, n:]


@jax.custom_vjp
def _cum(m, mt, g):
    return _tri_dot(m, g)


_cum.defvjp(lambda m, mt, g: (_cum(m, mt, g), (m, mt)),
            lambda r, d: (jnp.zeros_like(r[0]), jnp.zeros_like(r[1]), _tri_dot(r[1], d)))


def _silu(x):
    return x * jax.nn.sigmoid(x)


def _gelu(x):
    return 0.5 * x * (1.0 + jnp.tanh(math.sqrt(2.0 / math.pi) * (x + 0.044715 * (x * x * x))))


def _rms(x, w):
    return x * lax.rsqrt(jnp.mean(x * x, axis=-1, keepdims=True) + RMS_EPS) * w


def _norm_mod(x, w, shift, scale):
    return _rms(x, w) * (1.0 + scale) + shift


def _hsl(h):
    return slice(h * HD, (h + 1) * HD)


def _hgrn_chunk(st, qz, fz, iv, lb, m, mt, mref):
    hs = range(HEADS)
    keep = [1.0 - lb[h] for h in hs]
    sg = [jax.nn.sigmoid(fz[h]) for h in hs]
    g = [jnp.log(lb[h] + keep[h] * sg[h]) for h in hs]
    k = [keep[h] * (1.0 - sg[h]) for h in hs]
    q = [_silu(qz[h]) for h in hs]
    b = [_cum(m, mt, g[h]) for h in hs]
    ref = [jnp.sum(mref * g[h], axis=0, keepdims=True) for h in hs]
    last = [jnp.sum(g[h], axis=0, keepdims=True) for h in hs]
    qa = [q[h] * jnp.exp(b[h] - ref[h]) for h in hs]
    ka = [k[h] * jnp.exp(ref[h] - b[h]) for h in hs]
    scores = [jnp.where(m > 0.5, mm_nt(qa[h], ka[h]), 0.0) for h in hs]
    inter = [mm_nt(qa[h] * jnp.exp(ref[h]), st[h]) for h in hs]
    kv = [mm_tn(iv[h], ka[h] * jnp.exp(last[h] - ref[h])) for h in hs]
    outs = [mm(scores[h], iv[h]) + inter[h] for h in hs]
    news = [jnp.exp(last[h]) * st[h] + kv[h] for h in hs]
    return outs, news


def _sgu_fn(ub, vb, lnw, lnb, sw, sb):
    gv = [_gelu(v) for v in vb]
    mu = sum(jnp.sum(t, axis=-1, keepdims=True) for t in gv) / D
    var = sum(jnp.sum((t - mu) * (t - mu), axis=-1, keepdims=True) for t in gv) / D
    inv = lax.rsqrt(var + LN_EPS)
    cols = []
    for g in range(HEADS):
        vn = (gv[g] - mu) * inv * lnw[g] + lnb[g]
        cols.append(_gelu(ub[g]) * (mm(sw[g], vn) + sb[g]))
    return jnp.concatenate(cols, axis=1)


def _readout_fn(ob, og, hnw):
    r = [o * lax.rsqrt(jnp.mean(o * o, axis=-1, keepdims=True) + RMS_EPS) * hnw for o in ob]
    return jnp.concatenate(r, axis=1) * _silu(og)


def _glu_fn(ac, v):
    return _gelu(ac) * v


def _stream_row(tm):
    n_ctx = CTX // tm
    return lambda i: (jnp.where(i < n_ctx, 0, 1), 0, 0, 0)


def in_proj_fwd(x, mod, nw, wg):
    t = x.shape[0]

    def body(x_ref, mod_ref, nw_ref, w_ref, out_ref, ht_ref):
        h32 = _norm_mod(x_ref[...], nw_ref[...], mod_ref[0, 0], mod_ref[0, 1])
        ht_ref[...] = h32.T.astype(BF16)
        h = h32.astype(BF16)
        for j in range(N_DEV):
            out_ref[:, j * IN_SLOT:(j + 1) * IN_SLOT] = jnp.dot(h, w_ref[j], preferred_element_type=F32)

    return pl.pallas_call(
        body, name="in_proj_fwd", grid=(t // TM,),
        in_specs=[pl.BlockSpec((TM, D), lambda i: (i, 0)), pl.BlockSpec((1, 6, 1, D), _stream_row(TM)),
                  pl.BlockSpec((1, D), lambda i: (0, 0)), VMEM_WHOLE],
        out_specs=[pl.BlockSpec((TM, D_IN), lambda i: (i, 0)), pl.BlockSpec((D, TM), lambda i: (0, i))],
        out_shape=[jax.ShapeDtypeStruct((t, D_IN), F32), jax.ShapeDtypeStruct((D, t), BF16)],
        compiler_params=_cp(1))(x, mod, nw, wg)


def _scan_chunk(nc):
    ncc = CTX // CH

    def chunk(d, s):
        bwd = jnp.where(s < ncc, ncc - 1 - s, nc + ncc - 1 - s)
        return jnp.where(d == 0, s, bwd)
    return chunk


def hgrn_fwd(parts, lb, mc, mtc, mrefc):
    t = parts.shape[0]
    nc = t // CH
    chunk = _scan_chunk(nc)

    def body(q_ref, f_ref, i_ref, lb_ref, m_ref, mt_ref, mr_ref, o_ref, ck_ref, st):
        @pl.when(pl.program_id(1) == 0)
        def _():
            st[...] = jnp.zeros_like(st)
        ck_ref[0, 0] = st[...].astype(BF16)
        outs, news = _hgrn_chunk([st[h] for h in range(HEADS)], [q_ref[:, _hsl(h)] for h in range(HEADS)],
                                 [f_ref[:, _hsl(h)] for h in range(HEADS)], [i_ref[:, _hsl(h)] for h in range(HEADS)],
                                 [lb_ref[0, :, _hsl(h)] for h in range(HEADS)], m_ref[0], mt_ref[0], mr_ref[0])
        for h in range(HEADS):
            o_ref[0, :, _hsl(h)] = outs[h]
            st[h] = news[h]

    const = lambda d, s: (d, 0, 0)
    return pl.pallas_call(
        body, name="hgrn_fwd", grid=(2, nc),
        in_specs=[pl.BlockSpec((CH, D), lambda d, s: (chunk(d, s), 0)), pl.BlockSpec((CH, D), lambda d, s: (chunk(d, s), 1 + d)),
                  pl.BlockSpec((CH, D), lambda d, s: (chunk(d, s), 3)), pl.BlockSpec((1, 1, D), const),
                  pl.BlockSpec((1, CH, CH), const), pl.BlockSpec((1, CH, CH), const), pl.BlockSpec((1, CH, 1), const)],
        out_specs=[pl.BlockSpec((1, CH, D), lambda d, s: (d, chunk(d, s), 0)),
                   pl.BlockSpec((1, 1, HEADS, HD, HD), lambda d, s: (d, s, 0, 0, 0))],
        out_shape=[jax.ShapeDtypeStruct((2, t, D), F32), jax.ShapeDtypeStruct((2, nc, HEADS, HD, HD), BF16)],
        scratch_shapes=[pltpu.VMEM((HEADS, HD, HD), F32)], compiler_params=_cp(2))(parts, parts, parts, lb, mc, mtc, mrefc)


def _mixer_tile(rows, u_ref, v_ref, og_ref, o_ref, lnw_ref, lnb_ref, sw_ref, sb_ref, hnw_ref):
    n = (rows.stop - rows.start) // SGU_CH
    yas, vjps = [], []
    for c in range(n):
        r = slice(rows.start + c * SGU_CH, rows.start + (c + 1) * SGU_CH)
        ya, vjp_a = jax.vjp(_sgu_fn, [u_ref[r, _hsl(g)] for g in range(HEADS)], [v_ref[r, _hsl(g)] for g in range(HEADS)],
                            [lnw_ref[:, _hsl(g)] for g in range(HEADS)], [lnb_ref[:, _hsl(g)] for g in range(HEADS)],
                            [sw_ref[g] for g in range(HEADS)], [sb_ref[g] for g in range(HEADS)])
        yas.append(ya)
        vjps.append(vjp_a)
    yb, vjp_b = jax.vjp(_readout_fn, [o_ref[0, rows, _hsl(h)] + o_ref[1, rows, _hsl(h)] for h in range(HEADS)],
                        og_ref[rows, :], hnw_ref[...])
    return (yas[0] if n == 1 else jnp.concatenate(yas, axis=0)), yb, vjps, vjp_b


def _part_specs(tm, first, n):
    return [pl.BlockSpec((tm, D), functools.partial(lambda k, i: (i, k), first + k)) for k in range(n)]


def _unless_ctx(skip_ctx, is_ctx, zero_refs, work):
    if not skip_ctx:
        return work()

    @pl.when(is_ctx)
    def _():
        for r in zero_refs:
            r[...] = jnp.zeros_like(r)

    pl.when(jnp.logical_not(is_ctx))(work)


def mixer_fwd(x, parts, o, mod, lnw, lnb, sw, sb, hnw, wa, wb, wo, skip_ctx):
    t = x.shape[0]

    def body(x_ref, u_ref, v_ref, og_ref, ga_ref, gb_ref, o_ref, mod_ref, lnw_ref, lnb_ref, sw_ref, sb_ref, hnw_ref,
             wa_ref, wb_ref, wo_ref, out_ref, pa_ref, pb_ref, y_ref, yat_ref, ybt_ref, mt_ref):
        def work():
            ya, yb, _, _ = _mixer_tile(slice(0, TM), u_ref, v_ref, og_ref, o_ref, lnw_ref, lnb_ref, sw_ref, sb_ref, hnw_ref)
            pa, pb = mm(ya, wa_ref[...]), mm(yb, wb_ref[...])
            merged = jax.nn.sigmoid(ga_ref[...]) * pa + jax.nn.sigmoid(gb_ref[...]) * pb
            y = mm(merged, wo_ref[...])
            out_ref[...] = x_ref[...] + mod_ref[0, 2] * y
            pa_ref[...], pb_ref[...], y_ref[...] = pa.astype(BF16), pb.astype(BF16), y.astype(BF16)
            yat_ref[...], ybt_ref[...], mt_ref[...] = ya.T.astype(BF16), yb.T.astype(BF16), merged.T.astype(BF16)

        _unless_ctx(skip_ctx, pl.program_id(0) == 0, (out_ref, pa_ref, pb_ref, y_ref, yat_ref, ybt_ref, mt_ref), work)

    vec = lambda n: pl.BlockSpec((1, n), lambda i: (0, 0))
    tile = pl.BlockSpec((TM, D), lambda i: (i, 0))
    tile_t = pl.BlockSpec((D, TM), lambda i: (0, i))
    return pl.pallas_call(
        body, name="mixer_fwd", grid=(t // TM,),
        in_specs=[tile] + _part_specs(TM, 4, 5)
        + [pl.BlockSpec((2, TM, D), lambda i: (0, i, 0)), pl.BlockSpec((1, 6, 1, D), _stream_row(TM)), vec(D), vec(D),
           VMEM_WHOLE, VMEM_WHOLE, vec(HD), VMEM_WHOLE, VMEM_WHOLE, VMEM_WHOLE],
        out_specs=[tile] * 4 + [tile_t] * 3,
        out_shape=[jax.ShapeDtypeStruct((t, D), F32)] + [jax.ShapeDtypeStruct((t, D), BF16)] * 3
        + [jax.ShapeDtypeStruct((D, t), BF16)] * 3, compiler_params=_cp(1),
    )(x, parts, parts, parts, parts, parts, o, mod, lnw, lnb, sw, sb, hnw, wa, wb, wo)


def ffn_up_fwd(x, mod, nw, wg, skip_ctx):
    t = x.shape[0]

    def body(x_ref, mod_ref, nw_ref, w_ref, out_ref, ht_ref):
        def work():
            h32 = _norm_mod(x_ref[...], nw_ref[...], mod_ref[0, 3], mod_ref[0, 4])
            ht_ref[...] = h32.T.astype(BF16)
            h = h32.astype(BF16)
            for j in range(N_DEV):
                out_ref[j] = jnp.dot(h, w_ref[j], preferred_element_type=F32)

        _unless_ctx(skip_ctx, pl.program_id(0) == 0, (out_ref, ht_ref), work)

    return pl.pallas_call(
        body, name="ffn_up_fwd", grid=(t // TM,),
        in_specs=[pl.BlockSpec((TM, D), lambda i: (i, 0)), pl.BlockSpec((1, 6, 1, D), _stream_row(TM)),
                  pl.BlockSpec((1, D), lambda i: (0, 0)), VMEM_WHOLE],
        out_specs=[pl.BlockSpec((N_DEV, TM, FF_SLOT), lambda i: (0, i, 0)), pl.BlockSpec((D, TM), lambda i: (0, i))],
        out_shape=[jax.ShapeDtypeStruct((N_DEV, t, FF_SLOT), F32), jax.ShapeDtypeStruct((D, t), BF16)],
        compiler_params=_cp(1))(x, mod, nw, wg)


def _halo_specs(nt, k_of, i_of):
    per = TM // GRID_W
    last = nt * per - 1
    return [pl.BlockSpec((1, GRID_W, FF_SLOT), lambda *g: (k_of(*g), jnp.maximum(i_of(*g) * per - 1, 0), 0)),
            pl.BlockSpec((1, TM, FF_SLOT), lambda *g: (k_of(*g), i_of(*g), 0)),
            pl.BlockSpec((1, GRID_W, FF_SLOT), lambda *g: (k_of(*g), jnp.minimum(i_of(*g) * per + per, last), 0))]


def _with_halo(prev_ref, main_ref, next_ref, i, nt):
    prev = jnp.where(i >= 2, prev_ref[0], 0.0)
    nxt = jnp.where((i >= 1) & (i <= nt - 2), next_ref[0], 0.0)
    return jnp.concatenate([prev, main_ref[0], nxt], axis=0)


def _tap_valid(dc, i, n_rows, offset):
    r = lax.broadcasted_iota(jnp.int32, (n_rows, 1), 0) - offset
    col = jnp.bitwise_and(r, GRID_W - 1)
    pos = jnp.where(i == 0, r, col) + dc
    return (pos >= 0) & (pos < jnp.where(i == 0, TM, GRID_W))


def _row_weight(cw_ref, dr, dc, i):
    w = cw_ref[0, 3 * (dr + 1) + dc + 1:3 * (dr + 1) + dc + 2, :]
    return w if dr == 0 else jnp.where(i == 0, 0.0, w)


def ffn_down_fwd(x, av, mod, cw, cb, wd, skip_ctx):
    t = x.shape[0]
    nt = t // TM
    ext = TM + 2 * GRID_W

    def body(x_ref, ap_ref, am_ref, an_ref, v_ref, mod_ref, cw_ref, cb_ref, wd_ref, out_ref, ac_ref, y_ref, z_ref, acc):
        i, k = pl.program_id(0), pl.program_id(1)

        def work():
            a_ext = _with_halo(ap_ref, am_ref, an_ref, i, nt)
            conv = jnp.zeros((TM, FF_SLOT), F32) + cb_ref[0]
            for dc in (-1, 0, 1):
                rolled = (a_ext if dc == 0 else
                          jnp.where(_tap_valid(dc, i, ext, GRID_W), pltpu.roll(a_ext, (-dc) % ext, 0), 0.0))
                for dr in (-1, 0, 1):
                    lo = GRID_W + GRID_W * dr
                    conv = conv + rolled[lo:lo + TM] * _row_weight(cw_ref, dr, dc, i)
            ac_ref[0] = conv.astype(BF16)
            z = _glu_fn(conv, v_ref[0]).astype(BF16)
            z_ref[0] = z
            part = mm(z, wd_ref[0])

            @pl.when(k == 0)
            def _():
                acc[...] = part

            @pl.when(k > 0)
            def _():
                acc[...] += part

            @pl.when(k == N_FFK - 1)
            def _():
                y_ref[...] = acc[...]
                out_ref[...] = x_ref[...] + mod_ref[0, 5] * acc[...]

        _unless_ctx(skip_ctx, i == 0, (out_ref, ac_ref, y_ref, z_ref), work)

    tile = pl.BlockSpec((TM, D), lambda i, k: (i, 0))
    return pl.pallas_call(
        body, name="ffn_down_fwd", grid=(nt, N_FFK),
        in_specs=[tile] + _halo_specs(nt, lambda i, k: k, lambda i, k: i)
        + [pl.BlockSpec((1, TM, FF_SLOT), lambda i, k: (N_FFK + k, i, 0)),
           pl.BlockSpec((1, 6, 1, D), lambda i, k: (jnp.where(i < 1, 0, 1), 0, 0, 0)),
           pl.BlockSpec((1, 9, FF_SLOT), lambda i, k: (k, 0, 0)), pl.BlockSpec((1, 1, FF_SLOT), lambda i, k: (k, 0, 0)),
           pl.BlockSpec((1, FF_SLOT, D), lambda i, k: (k, 0, 0))],
        out_specs=[tile, pl.BlockSpec((1, TM, FF_SLOT), lambda i, k: (k, i, 0)), tile,
                   pl.BlockSpec((1, TM, FF_SLOT), lambda i, k: (k, i, 0))],
        out_shape=[jax.ShapeDtypeStruct((t, D), F32), jax.ShapeDtypeStruct((N_FFK, t, FF_SLOT), BF16),
                   jax.ShapeDtypeStruct((t, D), F32), jax.ShapeDtypeStruct((N_FFK, t, FF_SLOT), BF16)],
        scratch_shapes=[pltpu.VMEM((TM, D), F32)], compiler_params=_cp(2))(x, av, av, av, av, mod, cw, cb, wd)


def loss_fwd_bwd(x, target, fw):
    t = x.shape[0]

    def body(x_ref, t_ref, w_ref, loss_ref, dx_ref, dw_ref):
        i = pl.program_id(0)

        @pl.when(i == 0)
        def _():
            loss_ref[...] = jnp.zeros_like(loss_ref)
            dw_ref[...] = jnp.zeros_like(dw_ref)
            dx_ref[...] = jnp.zeros_like(dx_ref)

        @pl.when(i > 0)
        def _():
            y, vjp = jax.vjp(_rms, x_ref[...], w_ref[...])
            err = y - t_ref[...]
            loss_ref[...] += 0.5 * jnp.sum(jnp.sum(err * err, axis=-1, keepdims=True) / D)
            dx, dw = vjp(err / D)
            dx_ref[...] = dx
            dw_ref[...] += dw

    return pl.pallas_call(
        body, name="loss_fwd_bwd", grid=(t // TM,),
        in_specs=[pl.BlockSpec((TM, D), lambda i: (i, 0)), pl.BlockSpec((TM, D), lambda i: (jnp.maximum(i - 1, 0), 0)),
                  pl.BlockSpec((1, D), lambda i: (0, 0))],
        out_specs=[pl.BlockSpec((8, 128), lambda i: (0, 0)), pl.BlockSpec((TM, D), lambda i: (i, 0)),
                   pl.BlockSpec((1, D), lambda i: (0, 0))],
        out_shape=[jax.ShapeDtypeStruct((8, 128), F32), jax.ShapeDtypeStruct((t, D), F32), jax.ShapeDtypeStruct((1, D), F32)],
        compiler_params=_cp(1))(x, target, fw)


def _stream_add(ref, k, is_ctx, val):
    ref[0, k] += jnp.where(is_ctx, val, 0.0)
    ref[1, k] += jnp.where(is_ctx, 0.0, val)


def ffn_down_bwd(dx, ac, av, y, mod, wd, skip_ctx):
    t = dx.shape[0]
    nt = t // TM

    def body(dx_ref, ac_ref, v_ref, y_ref, mod_ref, wd_ref, dav_ref, dac_ref, dout_ref, dg_ref):
        i = pl.program_id(0)

        @pl.when(i == 0)
        def _():
            dg_ref[...] = jnp.zeros_like(dg_ref)

        def work():
            _stream_add(dg_ref, 0, i == 0, jnp.sum(dx_ref[...] * y_ref[...], axis=0, keepdims=True))
            dout = (mod_ref[0, 5] * dx_ref[...]).astype(BF16)
            dout_ref[...] = dout
            for k in range(N_FFK):
                _, vjp = jax.vjp(_glu_fn, ac_ref[k].astype(F32), v_ref[k])
                dac, dv = vjp(mm_nt(dout, wd_ref[k]))
                dac_ref[k] = dac
                dav_ref[k] = dv.astype(BF16)

        _unless_ctx(skip_ctx, i == 0, (dav_ref, dac_ref, dout_ref), work)

    tile = pl.BlockSpec((TM, D), lambda i: (i, 0))
    half = lambda first: pl.BlockSpec((N_FFK, TM, FF_SLOT), lambda i: (first, i, 0))
    return pl.pallas_call(
        body, name="ffn_down_bwd", grid=(nt,),
        in_specs=[tile, half(0), half(1), tile, pl.BlockSpec((1, 6, 1, D), _stream_row(TM)), VMEM_WHOLE],
        out_specs=[half(1), half(0), tile, pl.BlockSpec((2, 1, 1, D), lambda i: (0, 0, 0, 0))],
        out_shape=[jax.ShapeDtypeStruct((N_DEV, t, FF_SLOT), BF16), jax.ShapeDtypeStruct((N_FFK, t, FF_SLOT), F32),
                   jax.ShapeDtypeStruct((t, D), BF16), jax.ShapeDtypeStruct((2, 1, 1, D), F32)],
        compiler_params=_cp(1))(dx, ac, av, y, mod, wd)


def conv_bwd(dav, dac, av, cw, skip_ctx):
    t = dac.shape[1]
    nt = t // TM
    ext = TM + 2 * GRID_W

    def body(dav_in, gp_ref, gm_ref, gn_ref, ap_ref, am_ref, an_ref, cw_ref, dav_ref, dcw_ref, dcb_ref):
        k, i = pl.program_id(0), pl.program_id(1)

        @pl.when(i == 0)
        def _():
            dcw_ref[...] = jnp.zeros_like(dcw_ref)
            dcb_ref[...] = jnp.zeros_like(dcb_ref)

        def work():
            g_ext = _with_halo(gp_ref, gm_ref, gn_ref, i, nt)
            a_ext = _with_halo(ap_ref, am_ref, an_ref, i, nt)
            g_main = gm_ref[0]
            dcb_ref[0] += jnp.sum(g_main, axis=0, keepdims=True)
            da = jnp.zeros((TM, FF_SLOT), F32)
            for dc in (-1, 0, 1):
                g_rolled = (g_ext if dc == 0 else
                            pltpu.roll(jnp.where(_tap_valid(dc, i, ext, GRID_W), g_ext, 0.0), dc % ext, 0))
                a_rolled = a_ext if dc == 0 else pltpu.roll(a_ext, (-dc) % ext, 0)
                g_valid = g_main if dc == 0 else jnp.where(_tap_valid(dc, i, TM, 0), g_main, 0.0)
                for dr in (-1, 0, 1):
                    lo = GRID_W - GRID_W * dr
                    da = da + g_rolled[lo:lo + TM] * _row_weight(cw_ref, dr, dc, i)
                    lo = GRID_W + GRID_W * dr
                    tap = 3 * (dr + 1) + dc + 1
                    dw = jnp.sum(g_valid * a_rolled[lo:lo + TM], axis=0, keepdims=True)
                    dcw_ref[0, tap:tap + 1, :] += dw if dr == 0 else jnp.where(i == 0, 0.0, dw)
            dav_ref[0] = da.astype(BF16)

        _unless_ctx(skip_ctx, i == 0, (dav_ref,), work)

    return pl.pallas_call(
        body, name="conv_bwd", grid=(N_FFK, nt),
        in_specs=[ANY] + _halo_specs(nt, lambda k, i: k, lambda k, i: i) + _halo_specs(nt, lambda k, i: k, lambda k, i: i)
        + [pl.BlockSpec((1, 9, FF_SLOT), lambda k, i: (k, 0, 0))],
        out_specs=[pl.BlockSpec((1, TM, FF_SLOT), lambda k, i: (k, i, 0)), pl.BlockSpec((1, 9, FF_SLOT), lambda k, i: (k, 0, 0)),
                   pl.BlockSpec((1, 1, FF_SLOT), lambda k, i: (k, 0, 0))],
        out_shape=[jax.ShapeDtypeStruct(dav.shape, BF16), jax.ShapeDtypeStruct((N_FFK, 9, FF_SLOT), F32),
                   jax.ShapeDtypeStruct((N_FFK, 1, FF_SLOT), F32)],
        input_output_aliases={0: 0}, compiler_params=_cp(2))(dav, dac, dac, dac, av, av, av, cw)


def _norm_mod_bwd(x_ref, nw_ref, mod_ref, k_shift, dh, dx_in, dx_ref, dnw_ref, dmod_ref, is_ctx):
    _, vjp = jax.vjp(_norm_mod, x_ref[...], nw_ref[...], mod_ref[0, k_shift], mod_ref[0, k_shift + 1])
    dx, dnw, dshift, dscale = vjp(dh)
    dx_ref[...] = dx_in + dx
    dnw_ref[...] += dnw
    _stream_add(dmod_ref, 0, is_ctx, dshift)
    _stream_add(dmod_ref, 1, is_ctx, dscale)


def ffn_up_bwd_x(dx2, x, dav, mod, nw, wg, skip_ctx):
    t = x.shape[0]

    def body(dx2_ref, x_ref, dav_ref, mod_ref, nw_ref, w_ref, dx_ref, dnw_ref, dmod_ref):
        i = pl.program_id(0)

        @pl.when(i == 0)
        def _():
            dnw_ref[...] = jnp.zeros_like(dnw_ref)
            dmod_ref[...] = jnp.zeros_like(dmod_ref)

        def work():
            dh = mm_nt(dav_ref[0], w_ref[0])
            for j in range(1, N_DEV):
                dh = dh + mm_nt(dav_ref[j], w_ref[j])
            _norm_mod_bwd(x_ref, nw_ref, mod_ref, 3, dh, dx2_ref[...], dx_ref, dnw_ref, dmod_ref, i == 0)

        _unless_ctx(skip_ctx, i == 0, (dx_ref,), work)

    tile = pl.BlockSpec((TM, D), lambda i: (i, 0))
    return pl.pallas_call(
        body, name="ffn_up_bwd_x", grid=(t // TM,),
        in_specs=[tile, tile, pl.BlockSpec((N_DEV, TM, FF_SLOT), lambda i: (0, i, 0)), pl.BlockSpec((1, 6, 1, D), _stream_row(TM)),
                  pl.BlockSpec((1, D), lambda i: (0, 0)), VMEM_WHOLE],
        out_specs=[tile, pl.BlockSpec((1, D), lambda i: (0, 0)), pl.BlockSpec((2, 2, 1, D), lambda i: (0, 0, 0, 0))],
        out_shape=[jax.ShapeDtypeStruct((t, D), F32), jax.ShapeDtypeStruct((1, D), F32), jax.ShapeDtypeStruct((2, 2, 1, D), F32)],
        compiler_params=_cp(1))(dx2, x, dav, mod, nw, wg)


def weight_grad(at, dout, slot, name, after=None):
    rows, t = at.shape
    stacked = dout.ndim == 3
    n = dout.shape[0] if stacked else dout.shape[1] // slot

    def body(a_ref, d_ref, *rest):
        dw_ref = rest[-1]
        dw_ref[0] = jnp.dot(a_ref[...], d_ref[0] if stacked else d_ref[...], preferred_element_type=F32).astype(dw_ref.dtype)

    d_spec = pl.BlockSpec((1, t, slot), lambda j: (j, 0, 0)) if stacked else pl.BlockSpec((t, slot), lambda j: (0, j))
    extra = [] if after is None else [jnp.reshape(after, (1, 1))]
    return pl.pallas_call(
        body, name=name, grid=(n,), in_specs=[VMEM_WHOLE, d_spec] + [ANY] * len(extra),
        out_specs=pl.BlockSpec((1, rows, slot), lambda j: (j, 0, 0)),
        out_shape=jax.ShapeDtypeStruct((n, rows, slot), GRAD_WIRE), compiler_params=_cp(1))(at, dout, *extra)


def weight_grad_rows(at, dout, name):
    n, t, rows = at.shape
    cols = dout.shape[1]

    def body(a_ref, d_ref, dw_ref):
        dw_ref[0] = _dot(a_ref[0], d_ref[...], ((0,), (0,))).astype(dw_ref.dtype)

    return pl.pallas_call(
        body, name=name, grid=(n,), in_specs=[pl.BlockSpec((1, t, rows), lambda k: (k, 0, 0)), VMEM_WHOLE],
        out_specs=pl.BlockSpec((1, rows, cols), lambda k: (k, 0, 0)),
        out_shape=jax.ShapeDtypeStruct((n, rows, cols), GRAD_WIRE), compiler_params=_cp(1))(at, dout)


def mixer_bwd(dx, parts, o, pa, pb, y, mod, lnw, lnb, sw, sb, hnw, wa, wb, wo, skip_ctx):
    t = dx.shape[0]
    tm = TM
    n_ctx = CTX // tm

    def body(dx_ref, u_ref, v_ref, og_ref, ga_ref, gb_ref, o_ref, pa_ref, pb_ref, y_ref, mod_ref, lnw_ref, lnb_ref, sw_ref,
             sb_ref, hnw_ref, wa_ref, wb_ref, wo_ref, dp_ref, do_ref, dy_ref, dpa_ref, dpb_ref, dlnw_ref, dlnb_ref, dsw_ref,
             dsb_ref, dhnw_ref, dg_ref):
        i = pl.program_id(0)

        @pl.when(i == 0)
        def _():
            for r in (dlnw_ref, dlnb_ref, dsw_ref, dsb_ref, dhnw_ref, dg_ref):
                r[...] = jnp.zeros_like(r)

        def work():
            _, _, vjps, vjp_b = _mixer_tile(slice(0, tm), u_ref, v_ref, og_ref, o_ref, lnw_ref, lnb_ref, sw_ref, sb_ref, hnw_ref)
            pa, pb = pa_ref[...].astype(F32), pb_ref[...].astype(F32)
            sa, sbg = jax.nn.sigmoid(ga_ref[...]), jax.nn.sigmoid(gb_ref[...])
            dxv = dx_ref[...]
            _stream_add(dg_ref, 0, i < n_ctx, jnp.sum(dxv * y_ref[...].astype(F32), axis=0, keepdims=True))
            dy = (mod_ref[0, 2] * dxv).astype(BF16)
            dy_ref[...] = dy
            dmerged = mm_nt(dy, wo_ref[...])
            dpa, dpb = (sa * dmerged).astype(BF16), (sbg * dmerged).astype(BF16)
            dpa_ref[...], dpb_ref[...] = dpa, dpb
            first = 0
            dp_ref[:, first + 3 * D:first + 4 * D] = (dmerged * pa * sa * (1.0 - sa)).astype(BF16)
            dp_ref[:, first + 4 * D:first + 5 * D] = (dmerged * pb * sbg * (1.0 - sbg)).astype(BF16)
            dya = mm_nt(dpa, wa_ref[...])
            dob, dog, dhnw = vjp_b(mm_nt(dpb, wb_ref[...]))
            dp_ref[:, first + 2 * D:first + 3 * D] = dog.astype(BF16)
            dhnw_ref[...] += dhnw
            for g in range(HEADS):
                do_ref[:, _hsl(g)] = dob[g]
            for c, vjp_a in enumerate(vjps):
                rows = slice(c * SGU_CH, (c + 1) * SGU_CH)
                dub, dvb, dlnw, dlnb, dsw, dsb = vjp_a(dya[rows])
                for g in range(HEADS):
                    dp_ref[rows, first + g * HD:first + (g + 1) * HD] = dub[g].astype(BF16)
                    dp_ref[rows, first + D + g * HD:first + D + (g + 1) * HD] = dvb[g].astype(BF16)
                    dlnw_ref[:, _hsl(g)] += dlnw[g]
                    dlnb_ref[:, _hsl(g)] += dlnb[g]
                    dsw_ref[g] += dsw[g]
                    dsb_ref[g] += dsb[g]

        _unless_ctx(skip_ctx, i < n_ctx, (dp_ref, do_ref, dy_ref, dpa_ref, dpb_ref), work)

    vec = lambda n: pl.BlockSpec((1, n), lambda i: (0, 0))
    tile = pl.BlockSpec((tm, D), lambda i: (i, 0))
    sds = jax.ShapeDtypeStruct
    return pl.pallas_call(
        body, name="mixer_bwd", grid=(t // tm,),
        in_specs=[tile] + _part_specs(tm, 4, 5)
        + [pl.BlockSpec((2, tm, D), lambda i: (0, i, 0)), tile, tile, tile, pl.BlockSpec((1, 6, 1, D), _stream_row(tm)),
           vec(D), vec(D), VMEM_WHOLE, VMEM_WHOLE, vec(HD), VMEM_WHOLE, VMEM_WHOLE, VMEM_WHOLE],
        out_specs=[pl.BlockSpec((tm, 5 * D), lambda i: (i, 0)), tile, tile, tile, tile, vec(D), vec(D),
                   VMEM_WHOLE, VMEM_WHOLE, vec(HD), pl.BlockSpec((2, 1, 1, D), lambda i: (0, 0, 0, 0))],
        out_shape=[sds((t, 5 * D), BF16), sds((t, D), F32), sds((t, D), BF16), sds((t, D), BF16), sds((t, D), BF16),
                   sds((1, D), F32), sds((1, D), F32), sds((HEADS, SGU_CH, SGU_CH), F32), sds((HEADS, SGU_CH, 1), F32),
                   sds((1, HD), F32), sds((2, 1, 1, D), F32)],
        compiler_params=_cp(1))(dx, parts, parts, parts, parts, parts, o, pa, pb, y, mod, lnw, lnb, sw, sb, hnw, wa, wb, wo)


def hgrn_bwd(d, parts, lb, mc, mtc, mrefc, ck, do, first=None, d5=None):
    t = parts.shape[0]
    nc = t // CH
    chunk = _scan_chunk(nc)
    rev = lambda s: chunk(d, nc - 1 - s)

    def body(q_ref, f_ref, i_ref, lb_ref, m_ref, mt_ref, mr_ref, ck_ref, do_ref, *rest):
        dst = rest[-1]
        dlb_ref = rest[-2]

        @pl.when(pl.program_id(0) == 0)
        def _():
            dst[...] = jnp.zeros_like(dst)
            dlb_ref[...] = jnp.zeros_like(dlb_ref)

        heads = range(HEADS)
        fn = functools.partial(_hgrn_chunk, m=m_ref[0], mt=mt_ref[0], mref=mr_ref[0])
        _, vjp = jax.vjp(fn, [ck_ref[0, 0, h].astype(F32) for h in heads], [q_ref[:, _hsl(h)] for h in heads],
                         [f_ref[:, _hsl(h)] for h in heads], [i_ref[:, _hsl(h)] for h in heads],
                         [lb_ref[0, :, _hsl(h)] for h in heads])
        dstl, dq, df, di, dlb = vjp(([do_ref[:, _hsl(h)] for h in heads], [dst[h] for h in heads]))
        if d == 1:
            rest[4][:, 4 * D:] = rest[3][...]
        for h in heads:
            dst[h] = dstl[h]
            dlb_ref[0, :, _hsl(h)] += dlb[h]
            if d == 0:
                dq_ref, df_ref, di_ref = rest[:3]
                dq_ref[:, _hsl(h)] = dq[h].astype(BF16)
                df_ref[:, _hsl(h)] = df[h].astype(BF16)
                di_ref[:, _hsl(h)] = di[h].astype(BF16)
            else:
                dq0_ref, df0_ref, di0_ref, _, dp_ref = rest[:5]
                col = lambda k: slice(k * D + h * HD, k * D + (h + 1) * HD)
                dp_ref[:, col(0)] = (dq0_ref[:, _hsl(h)].astype(F32) + dq[h]).astype(BF16)
                dp_ref[:, col(1)] = df0_ref[:, _hsl(h)]
                dp_ref[:, col(2)] = df[h].astype(BF16)
                dp_ref[:, col(3)] = (di0_ref[:, _hsl(h)].astype(F32) + di[h]).astype(BF16)

    const = lambda s: (d, 0, 0)
    at = lambda k: pl.BlockSpec((CH, D), lambda s: (rev(s), k))
    in_specs = [at(0), at(1 + d), at(3), pl.BlockSpec((1, 1, D), const), pl.BlockSpec((1, CH, CH), const),
                pl.BlockSpec((1, CH, CH), const), pl.BlockSpec((1, CH, 1), const),
                pl.BlockSpec((1, 1, HEADS, HD, HD), lambda s: (d, nc - 1 - s, 0, 0, 0)), at(0)]
    dlb_spec, dlb_shape = pl.BlockSpec((1, 1, D), lambda s: (0, 0, 0)), jax.ShapeDtypeStruct((1, 1, D), F32)
    common = dict(grid=(nc,), scratch_shapes=[pltpu.VMEM((HEADS, HD, HD), F32)], compiler_params=_cp(1))
    if d == 0:
        return pl.pallas_call(body, name="hgrn_bwd_fwd_dir", in_specs=in_specs, out_specs=[at(0)] * 3 + [dlb_spec],
                              out_shape=[jax.ShapeDtypeStruct((t, D), BF16)] * 3 + [dlb_shape], **common,
                              )(parts, parts, parts, lb, mc, mtc, mrefc, ck, do)
    return pl.pallas_call(body, name="hgrn_bwd_bwd_dir",
                          in_specs=in_specs + [at(0)] * 3 + [pl.BlockSpec((CH, 5 * D), lambda s: (rev(s), 0))],
                          out_specs=[pl.BlockSpec((CH, D_IN), lambda s: (rev(s), 0)), dlb_spec],
                          out_shape=[jax.ShapeDtypeStruct((t, D_IN), BF16), dlb_shape],
                          **common)(parts, parts, parts, lb, mc, mtc, mrefc, ck, do, *first, d5)


def in_proj_bwd_x(dx1, x, dparts, mod, nw, wg, after=None, latent_only=False):
    t = x.shape[0]
    tm = TM
    n_ctx = CTX // tm

    def body(dx1_ref, x_ref, dp_ref, mod_ref, nw_ref, w_ref, *rest):
        dx_ref, dnw_ref, dmod_ref = rest[-3:]
        i = pl.program_id(0)

        @pl.when(i == 0)
        def _():
            dnw_ref[...] = jnp.zeros_like(dnw_ref)
            dmod_ref[...] = jnp.zeros_like(dmod_ref)

        dh = mm_nt(dp_ref[:, 0:IN_SLOT], w_ref[0])
        for j in range(1, N_DEV):
            dh = dh + mm_nt(dp_ref[:, j * IN_SLOT:(j + 1) * IN_SLOT], w_ref[j])
        _norm_mod_bwd(x_ref, nw_ref, mod_ref, 0, dh, dx1_ref[...], dx_ref, dnw_ref, dmod_ref, i < n_ctx)

    tile = pl.BlockSpec((tm, D), lambda i: (i, 0))
    extra = [] if after is None else [jnp.reshape(after, (1, 1))]
    return pl.pallas_call(
        body, name="in_proj_bwd_x", grid=(t // tm,),
        in_specs=[tile, tile, pl.BlockSpec((tm, D_IN), lambda i: (i, 0)), pl.BlockSpec((1, 6, 1, D), _stream_row(tm)),
                  pl.BlockSpec((1, D), lambda i: (0, 0)), VMEM_WHOLE] + [ANY] * len(extra),
        out_specs=[pl.BlockSpec((tm, D), lambda i: (jnp.maximum(i - n_ctx, 0), 0)) if latent_only else tile,
                   pl.BlockSpec((1, D), lambda i: (0, 0)), pl.BlockSpec((2, 2, 1, D), lambda i: (0, 0, 0, 0))],
        out_shape=[jax.ShapeDtypeStruct((t - CTX if latent_only else t, D), F32), jax.ShapeDtypeStruct((1, D), F32),
                   jax.ShapeDtypeStruct((2, 2, 1, D), F32)],
        compiler_params=_cp(1))(dx1, x, dparts, mod, nw, wg, *extra)


def _lb_fn(h0, h1):
    m = jnp.maximum(h0, h1)
    e0, e1 = jnp.exp(h0 - m), jnp.exp(h1 - m)
    return e1 / (e0 + e1)


def lower_bounds(hlb):
    def body(h_ref, out_ref):
        out_ref[...] = _lb_fn(h_ref[0:1, :], h_ref[1:2, :])
    return pl.pallas_call(body, name="lower_bounds", out_shape=jax.ShapeDtypeStruct((1, 2 * D), F32))(hlb)


def lower_bounds_bwd(hlb, dlb1):
    def body(h_ref, d_ref, out_ref):
        _, vjp = jax.vjp(_lb_fn, h_ref[0:1, :], h_ref[1:2, :])
        d0, d1 = vjp(d_ref[...])
        out_ref[0:1, :] = d0
        out_ref[1:2, :] = d1
    return pl.pallas_call(body, name="lower_bounds_bwd", out_shape=jax.ShapeDtypeStruct((2, 2 * D), F32))(hlb, dlb1)


def _ada_fn(c_all, cctx8, w, b):
    dot = lambda a, l: jnp.dot(_silu(a), w[l], precision=HIGHEST, preferred_element_type=F32) + b[l]
    return [dot(c_all, l) for l in range(2)], [dot(cctx8, l) for l in range(2)]


def ada_fwd(c_all, cctx8, w, b):
    cols = w.shape[-1]

    def body(c_ref, cc_ref, w_ref, b_ref, out_ref):
        ox, oc = _ada_fn(c_ref[...], cc_ref[...], [w_ref[0], w_ref[1]], [b_ref[0], b_ref[1]])
        for l in range(2):
            out_ref[l, 0] = ox[l]
            out_ref[l, 1] = oc[l]
    return pl.pallas_call(body, name="ada_fwd", out_shape=jax.ShapeDtypeStruct((2, 2, N_DEV, cols), F32),
                          compiler_params=_cp(0))(c_all, cctx8, w, b)


def ada_bwd(c_all, cctx8, w, b, dmx, dmc):
    cols = w.shape[-1]

    def body(c_ref, cc_ref, w_ref, b_ref, dmx_ref, dmc_ref, dw_ref, dc_ref):
        fn = lambda cc, w0, w1: _ada_fn(c_ref[...], cc, [w0, w1], [b_ref[0], b_ref[1]])
        _, vjp = jax.vjp(fn, cc_ref[...], w_ref[0], w_ref[1])
        dcc, dw0, dw1 = vjp(([dmx_ref[0], dmx_ref[1]], [dmc_ref[0], dmc_ref[1]]))
        dw_ref[0] = dw0
        dw_ref[1] = dw1
        dc_ref[...] = jnp.sum(dcc, axis=0, keepdims=True)
    return pl.pallas_call(body, name="ada_bwd", out_shape=[jax.ShapeDtypeStruct((2, D, cols), F32), jax.ShapeDtypeStruct((1, D), F32)],
                          compiler_params=_cp(0))(c_all, cctx8, w, b, dmx, dmc)


def adamw(w, m, v, gparts, name):
    r, c = w.shape
    p = gparts.shape[0]
    rt = r
    while rt % 16 == 0 and (p + 7) * rt * c * 4 * 2 > 24 * 2 ** 20:
        rt //= 2

    def body(w_ref, m_ref, v_ref, g_ref, go_ref, d_ref, mo_ref, vo_ref):
        g = g_ref[0].astype(F32)
        for k in range(1, p):
            g = g + g_ref[k].astype(F32)
        m2 = ADAM_B1 * m_ref[...] + (1.0 - ADAM_B1) * g
        v2 = ADAM_B2 * v_ref[...] + (1.0 - ADAM_B2) * (g * g)
        m_hat = m2 / (1.0 - ADAM_B1 ** ADAM_STEP)
        v_hat = v2 / (1.0 - ADAM_B2 ** ADAM_STEP)
        go_ref[...] = g
        d_ref[...] = -ADAM_LR * (m_hat / (jnp.sqrt(v_hat) + ADAM_EPS) + ADAM_WD * w_ref[...])
        mo_ref[...] = m2
        vo_ref[...] = v2

    tile = pl.BlockSpec((rt, c), lambda i: (i, 0))
    return pl.pallas_call(
        body, name=name, grid=(r // rt,),
        in_specs=[tile, tile, tile, pl.BlockSpec((p, rt, c), lambda i: (0, i, 0))], out_specs=[tile] * 4,
        out_shape=[jax.ShapeDtypeStruct((r, c), F32)] * 4, compiler_params=_cp(1))(w, m, v, gparts)


def _me():
    x, y, c = lax.axis_index("x"), lax.axis_index("y"), lax.axis_index("c")
    return x, y, c, 4 * x + 2 * y + c


def _peer(x, y, c, p):
    fx, fy, fc = (p >> 2) & 1, (p >> 1) & 1, p & 1
    return (1 - x if fx else x, 1 - y if fy else y, 1 - c if fc else c)


def all_gather(arrs, name, after=None):
    n = len(arrs)
    extra = [] if after is None else list(after) if isinstance(after, (list, tuple)) else [after]

    def body(*refs):
        ins, outs = refs[:n], refs[n + len(extra):2 * n + len(extra)]
        send, recv, local = refs[2 * n + len(extra):]
        x, y, c, me = _me()
        copies = []
        for a in range(n):
            lc = pltpu.make_async_copy(ins[a], outs[a].at[me], local.at[a])
            lc.start()
            copies.append(lc)
            for p in range(1, N_DEV):
                cp = pltpu.make_async_remote_copy(src_ref=ins[a], dst_ref=outs[a].at[me], send_sem=send.at[a, p - 1],
                                                  recv_sem=recv.at[a, p - 1], device_id=_peer(x, y, c, p),
                                                  device_id_type=pl.DeviceIdType.MESH)
                cp.start()
                copies.append(cp)
        for cp in copies:
            cp.wait()

    return pl.pallas_call(
        body, name=name, in_specs=[ANY] * (n + len(extra)), out_specs=[ANY] * n,
        out_shape=[jax.ShapeDtypeStruct((N_DEV,) + a.shape, a.dtype) for a in arrs],
        scratch_shapes=[pltpu.SemaphoreType.DMA((n, N_DEV - 1)), pltpu.SemaphoreType.DMA((n, N_DEV - 1)),
                        pltpu.SemaphoreType.DMA((n,))])(*arrs, *extra)


HBM = pl.BlockSpec(memory_space=pltpu.HBM)
SEM = pl.BlockSpec(memory_space=pltpu.SEMAPHORE)


def _in_hbm(a):
    return pltpu.with_memory_space_constraint(a, pltpu.HBM)


ALL_PEERS = tuple(range(1, N_DEV))
SAME_CORE_AND_SIBLING = (1, 2, 4, 6)
OTHER_CHIPS = (2, 4, 6)


def _exchange_refs(srcs, lands, layer, scatter, a, x, y, c, p, forward=False):
    me = 4 * x + 2 * y + c
    px, py, pc = _peer(x, y, c, p) if p else (x, y, c)
    if forward and p:
        slot = lands[a].at[4 * px + 2 * py + pc]
        return slot, slot, _peer(x, y, c, 1)
    dst = lands[a].at[me] if layer is None else lands[a].at[me, layer]
    src = srcs[a].at[4 * px + 2 * py + pc] if scatter else dst
    return src, dst, (px, py, pc)


def exchange_start(srcs, lands, layer, scatter, name, after=None, peers=ALL_PEERS, forward=False):
    n, ns = len(lands), len(srcs)
    extra = [] if after is None else [after]

    def body(*refs):
        ins, lz = refs[:ns], refs[ns:ns + n]
        send, recv = refs[ns + n + len(extra)], refs[ns + n + len(extra) + 1]
        token = refs[-1]
        x, y, c, _ = _me()
        for a in range(n):
            for p in peers:
                src, dst, peer = _exchange_refs(ins, lz, layer, scatter, a, x, y, c, p, forward)
                k = a * (N_DEV - 1) + p - 1
                pltpu.make_async_remote_copy(src_ref=src, dst_ref=dst, send_sem=send.at[k], recv_sem=recv.at[k],
                                             device_id=peer, device_id_type=pl.DeviceIdType.MESH).start()
        token[...] = jnp.zeros_like(token)

    thru = [pltpu.HBM(a.shape, a.dtype) for a in list(srcs) + list(lands)]
    out = pl.pallas_call(
        body, name=name, in_specs=[HBM] * (ns + n) + [ANY] * len(extra),
        out_specs=[SEM, SEM] + [HBM] * (ns + n) + [pl.BlockSpec(memory_space=pltpu.VMEM)],
        out_shape=[pltpu.SemaphoreType.DMA((n * (N_DEV - 1),)), pltpu.SemaphoreType.DMA((n * (N_DEV - 1),))] + thru
        + [jax.ShapeDtypeStruct((8, 128), F32)],
        input_output_aliases={i: 2 + i for i in range(ns + n)},
        compiler_params=pltpu.CompilerParams(has_side_effects=pltpu.SideEffectType.DATAFLOW_SIDE_EFFECTING),
    )(*[_in_hbm(a) for a in list(srcs) + list(lands)], *extra)
    return out[0], out[1], out[2:2 + ns], out[2 + ns:2 + ns + n], out[-1]


def exchange_wait(send, recv, srcs, lands, layer, scatter, after, name, peers=ALL_PEERS):
    n, ns = len(lands), len(srcs)

    def body(*refs):
        ins, lz = refs[:ns], refs[ns:ns + n]
        send_ref, recv_ref = refs[ns + n], refs[ns + n + 1]
        x, y, c, _ = _me()
        for a in range(n):
            for p in peers:
                src, dst, peer = _exchange_refs(ins, lz, layer, scatter, a, x, y, c, 0)
                k = a * (N_DEV - 1) + p - 1
                cp = pltpu.make_async_remote_copy(src_ref=src, dst_ref=dst, send_sem=send_ref.at[k],
                                                  recv_sem=recv_ref.at[k], device_id=peer,
                                                  device_id_type=pl.DeviceIdType.MESH)
                cp.wait_send()
                cp.wait_recv()

    thru = [pltpu.HBM(a.shape, a.dtype) for a in list(srcs) + list(lands)]
    out = pl.pallas_call(
        body, name=name, in_specs=[HBM] * (ns + n) + [SEM, SEM, ANY], out_specs=[HBM] * (ns + n), out_shape=thru,
        input_output_aliases={i: i for i in range(ns + n)},
        compiler_params=pltpu.CompilerParams(has_side_effects=pltpu.SideEffectType.DATAFLOW_SIDE_EFFECTING),
    )(*srcs, *lands, send, recv, after)
    return out[ns:]


def place_own(src, land, me, layer, scatter, name, src_layer=None):
    create = isinstance(land, jax.ShapeDtypeStruct)
    r, c = src.shape[-2:]
    rt = r
    while rt % 32 == 0 and rt * c * 4 > 2 ** 21:
        rt //= 2

    def body(me_ref, src_ref, *rest):
        out_ref = rest[-1]
        out_ref[...] = src_ref[...].reshape(out_ref.shape).astype(out_ref.dtype)

    src_spec = (pl.BlockSpec((1, rt, c), lambda i, m: (m[0], i, 0)) if scatter else
                pl.BlockSpec((rt, c), lambda i, m: (i, 0)) if src_layer is None else
                pl.BlockSpec((1, rt, c), lambda i, m: (src_layer, i, 0)))
    out_spec = (pl.BlockSpec((1, rt, c), lambda i, m: (m[0], i, 0)) if layer is None
                else pl.BlockSpec((1, 1, rt, c), lambda i, m: (m[0], layer, i, 0)))
    grid_spec = pltpu.PrefetchScalarGridSpec(num_scalar_prefetch=1, grid=(r // rt,),
                                             in_specs=[src_spec] + ([] if create else [ANY]), out_specs=out_spec)
    return pl.pallas_call(body, name=name, grid_spec=grid_spec, out_shape=jax.ShapeDtypeStruct(land.shape, land.dtype),
                          input_output_aliases={} if create else {2: 0}, compiler_params=_cp(1),
                          )(*((me, src) if create else (me, src, land)))


def _scan_constants():
    r = lax.broadcasted_iota(jnp.int32, (CH, CH), 0)
    s = lax.broadcasted_iota(jnp.int32, (CH, CH), 1)
    lower = (s <= r).astype(F32)
    t = jnp.arange(CH)[:, None]
    mc = jnp.stack([lower, lower.T])
    mref = jnp.stack([(t <= CH // 2 - 1).astype(F32), (t >= CH // 2).astype(F32)])
    return mc, jnp.stack([lower.T, lower]), mref


def local_step(x, ctx, target, mod, lb, w, fetch=None, publish=None, small_ready=None, small_early=None):
    kept = {}

    def keep(l, part, grads):
        kept[(l, part)] = grads
        return 0.0

    fetch = fetch or (lambda l, part, after: w)
    publish = publish or keep
    n_layers = len(mod)
    mc, mtc, mrefc = _scan_constants()
    xs = jnp.concatenate([ctx, x], axis=0)
    saved, big = [], []
    for l in range(n_layers):
        wl = dict(fetch(l, "in", xs))
        parts, ht = in_proj_fwd(xs, mod[l], w["nw1"][l], wl["win"][l])
        o, ck = hgrn_fwd(parts, lb[l], mc, mtc, mrefc)
        wl.update(fetch(l, "rest", o))
        last = l == n_layers - 1
        x1, pa, pb, ym, yat, ybt, mt = mixer_fwd(xs, parts, o, mod[l], w["lnw"][l], w["lnb"][l], w["sw"][l], w["sb"][l],
                                                 w["hnw"][l], wl["wa"][l], wl["wb"][l], wl["wo"][l], last)
        av, h2t = ffn_up_fwd(x1, mod[l], w["nw2"][l], wl["wup"][l], last)
        x2, ac, y, z = ffn_down_fwd(x1, av, mod[l], w["cw"][l], w["cb"][l], wl["wd"][l], last)
        saved.append((xs, parts, o, ck, x1, av, ac, y, z, ht, h2t, pa, pb, ym, yat, ybt, mt))
        big.append(wl)
        xs = x2
    loss, dx, dfw = loss_fwd_bwd(xs, target, w["fw"])
    g = {k: [None] * n_layers for k in ("nw1", "nw2", "lnw", "lnb", "sw", "sb", "hnw", "cw", "cb")}
    g["fw"] = dfw
    dmod, dlb = [None] * n_layers, [None] * n_layers
    tok = 0.0
    for l in reversed(range(n_layers)):
        x0, parts, o, ck, x1, av, ac, y, z, ht, h2t, pa, pb, ym, yat, ybt, mt = saved[l]
        wl = big[l]
        last = l == n_layers - 1
        dav, dac, dout, dg2 = ffn_down_bwd(dx, ac, av, y, mod[l] + tok, wl["wd"][l], last)
        dwd = weight_grad_rows(z, dout, "ffn_down_bwd_w")
        dav, g["cw"][l], g["cb"][l] = conv_bwd(dav, dac, av, w["cw"][l], last)
        dx1, g["nw2"][l], dmod2 = ffn_up_bwd_x(dx, x1, dav, mod[l], w["nw2"][l], wl["wup"][l], last)
        dwup = weight_grad(h2t, dav, FF_SLOT, "ffn_up_bwd_w")
        tok = publish(l, "ffn", {"wd": dwd, "wup": dwup})
        (d5, do, dy, dpa, dpb, g["lnw"][l], g["lnb"][l], g["sw"][l], g["sb"][l], g["hnw"][l],
         dg1) = mixer_bwd(dx1, parts, o, pa, pb, ym, mod[l] + tok, w["lnw"][l], w["lnb"][l], w["sw"][l], w["sb"][l],
                          w["hnw"][l], wl["wa"][l], wl["wb"][l], wl["wo"][l], last)
        tok = publish(l, "mix", {"wa": weight_grad(yat, dpa, D, "mixer_bwd_wa"), "wb": weight_grad(ybt, dpb, D, "mixer_bwd_wb"),
                                 "wo": weight_grad(mt, dy, D, "mixer_bwd_wo")})
        if l == 0 and small_early:
            dmod[0] = jnp.concatenate([jnp.zeros((2, 2, 1, D), F32), dg1, dmod2, dg2], axis=1)
            tok = tok + small_early(loss[0, 0], g, dmod, dlb)
        dq, df, di, dlb_f = hgrn_bwd(0, parts, lb[l] + tok, mc, mtc, mrefc, ck, do)
        dparts, dlb_b = hgrn_bwd(1, parts, lb[l], mc, mtc, mrefc, ck, do, (dq, df, di), d5)
        dlb[l] = jnp.concatenate([dlb_f, dlb_b], axis=0)
        tok = publish(l, "in", {"win": weight_grad(ht, dparts, IN_SLOT, "in_proj_bwd_w")})
        dx, g["nw1"][l], dmod1 = in_proj_bwd_x(dx1, x0, dparts, mod[l], w["nw1"][l], wl["win"][l], after=tok,
                                               latent_only=l == 0)
        dmod[l] = jnp.concatenate([dmod1, dg1, dmod2, dg2], axis=1)
    done = small_ready(loss[0, 0], g, dmod, dlb) if small_ready else 0.0
    for (l, part), grads in kept.items():
        for k, v in grads.items():
            g.setdefault(k, [None] * n_layers)[l] = v
    return loss[0, 0], dx, g, dmod, dlb, done


ROW = 1024
REPLICATED = ("norm1_w", "sgu_ln_w", "sgu_ln_b", "sgu_w", "sgu_b", "hgrn_lower_bounds", "hgrn_norm_w", "norm2_w",
              "ffn_conv_b", "final_norm_w")
WEIGHT_ORDER = ("c_ctx", "ada_w", "ada_b", "norm1_w", "w_in", "sgu_ln_w", "sgu_ln_b", "sgu_w", "sgu_b", "hgrn_lower_bounds",
                "hgrn_norm_w", "w_branch_a", "w_branch_b", "w_out", "norm2_w", "ffn_w_up", "ffn_conv_w", "ffn_conv_b",
                "ffn_w_down", "final_norm_w")


def _rows_of(n):
    return -(-n // (8 * ROW)) * 8


def _pack(arrs, total_rows=None):
    parts = []
    for a in arrs:
        flat = a.reshape(-1).astype(F32)
        rows = _rows_of(flat.shape[0])
        parts.append(jnp.pad(flat, (0, rows * ROW - flat.shape[0])).reshape(rows, ROW))
    have = sum(p.shape[0] for p in parts)
    if total_rows is not None and total_rows > have:
        parts.append(jnp.zeros((total_rows - have, ROW), F32))
    return jnp.concatenate(parts, axis=0)


def _unpack(packed, shapes):
    lead = packed.shape[:-2]
    out, r0 = [], 0
    for s in shapes:
        n = math.prod(s)
        rows = _rows_of(n)
        out.append(packed[..., r0:r0 + rows, :].reshape(lead + (rows * ROW,))[..., :n].reshape(lead + tuple(s)))
        r0 += rows
    return out


def kernel(x, c, ctx, c_ctx, ada_w, ada_b, norm1_w, w_in, sgu_ln_w, sgu_ln_b, sgu_w, sgu_b, hgrn_lower_bounds, hgrn_norm_w, w_branch_a, w_branch_b, w_out, norm2_w, ffn_w_up, ffn_conv_w, ffn_conv_b, ffn_w_down, final_norm_w, loss_target, m_c_ctx, m_ada_w, m_ada_b, m_norm1_w, m_w_in, m_sgu_ln_w, m_sgu_ln_b, m_sgu_w, m_sgu_b, m_hgrn_lower_bounds, m_hgrn_norm_w, m_w_branch_a, m_w_branch_b, m_w_out, m_norm2_w, m_ffn_w_up, m_ffn_conv_w, m_ffn_conv_b, m_ffn_w_down, m_final_norm_w, v_c_ctx, v_ada_w, v_ada_b, v_norm1_w, v_w_in, v_sgu_ln_w, v_sgu_ln_b, v_sgu_w, v_sgu_b, v_hgrn_lower_bounds, v_hgrn_norm_w, v_w_branch_a, v_w_branch_b, v_w_out, v_norm2_w, v_ffn_w_up, v_ffn_conv_w, v_ffn_conv_b, v_ffn_w_down, v_final_norm_w):
    wts = dict(c_ctx=c_ctx, ada_w=ada_w, ada_b=ada_b, norm1_w=norm1_w, w_in=w_in, sgu_ln_w=sgu_ln_w, sgu_ln_b=sgu_ln_b,
               sgu_w=sgu_w, sgu_b=sgu_b, hgrn_lower_bounds=hgrn_lower_bounds, hgrn_norm_w=hgrn_norm_w, w_branch_a=w_branch_a,
               w_branch_b=w_branch_b, w_out=w_out, norm2_w=norm2_w, ffn_w_up=ffn_w_up, ffn_conv_w=ffn_conv_w,
               ffn_conv_b=ffn_conv_b, ffn_w_down=ffn_w_down, final_norm_w=final_norm_w)
    mom1 = dict(c_ctx=m_c_ctx, ada_w=m_ada_w, ada_b=m_ada_b, norm1_w=m_norm1_w, w_in=m_w_in, sgu_ln_w=m_sgu_ln_w,
                sgu_ln_b=m_sgu_ln_b, sgu_w=m_sgu_w, sgu_b=m_sgu_b, hgrn_lower_bounds=m_hgrn_lower_bounds,
                hgrn_norm_w=m_hgrn_norm_w, w_branch_a=m_w_branch_a, w_branch_b=m_w_branch_b, w_out=m_w_out, norm2_w=m_norm2_w,
                ffn_w_up=m_ffn_w_up, ffn_conv_w=m_ffn_conv_w, ffn_conv_b=m_ffn_conv_b, ffn_w_down=m_ffn_w_down,
                final_norm_w=m_final_norm_w)
    mom2 = dict(c_ctx=v_c_ctx, ada_w=v_ada_w, ada_b=v_ada_b, norm1_w=v_norm1_w, w_in=v_w_in, sgu_ln_w=v_sgu_ln_w,
                sgu_ln_b=v_sgu_ln_b, sgu_w=v_sgu_w, sgu_b=v_sgu_b, hgrn_lower_bounds=v_hgrn_lower_bounds,
                hgrn_norm_w=v_hgrn_norm_w, w_branch_a=v_w_branch_a, w_branch_b=v_w_branch_b, w_out=v_w_out, norm2_w=v_norm2_w,
                ffn_w_up=v_ffn_w_up, ffn_conv_w=v_ffn_conv_w, ffn_conv_b=v_ffn_conv_b, ffn_w_down=v_ffn_w_down,
                final_norm_w=v_final_norm_w)
    n_layers = w_in.shape[0]
    layers = range(n_layers)
    me = 4 * lax.axis_index("x") + 2 * lax.axis_index("y") + lax.axis_index("c")
    ada_cols = ada_w.shape[-1]

    big = ("w_in", "ffn_w_up", "w_branch_a", "w_branch_b", "w_out", "ffn_w_down")
    short = {"w_in": "win", "ffn_w_up": "wup", "w_branch_a": "wa", "w_branch_b": "wb", "w_out": "wo", "ffn_w_down": "wd"}
    me1 = me.reshape(1).astype(jnp.int32)
    groups = [[("w_in", 0)], [(k, 0) for k in big[1:]], [("w_in", 1)], [(k, 1) for k in big[1:]]]
    in_flight, started = [], 0.0

    def own_slots(n):
        return [place_own(wts[k], jax.ShapeDtypeStruct((N_DEV,) + wts[k].shape[1:], BF16), me1, None, False,
                          f"gather_own_{short[k]}_{l}", src_layer=l) for k, l in groups[n]]

    def start_group(n, lands, after):
        in_flight.append(exchange_start([], lands, None, False, f"gather_weights_start_{n}", after=after,
                                        peers=SAME_CORE_AND_SIBLING if n == 0 else ALL_PEERS))
        return in_flight[-1][-1]

    (c_all,) = all_gather([c], "gather_c")
    c_all = c_all.reshape(N_DEV, D)
    token = start_group(0, own_slots(0), c_all)
    later = [own_slots(n) for n in range(1, len(groups))]
    cctx8 = jnp.broadcast_to(c_ctx[None, :], (N_DEV, D))
    ada_b_cols = lax.dynamic_slice_in_dim(ada_b, me * ada_cols, ada_cols, axis=1)[:, None, :]
    mod_cols = ada_fwd(c_all, cctx8, ada_w, ada_b_cols)
    xs = jnp.concatenate([ctx[0], x[0]], axis=0)
    lb1 = lower_bounds(hgrn_lower_bounds)
    mod_all, conv_all = all_gather([mod_cols, ffn_conv_w.reshape(n_layers, 9, -1)], "gather_mod_conv",
                                   after=[token, xs, lb1] + [a for lands in later for a in lands])
    conv_full = [conv_all[:, l].transpose(1, 0, 2).reshape(9, N_FFK, FF_SLOT).transpose(1, 0, 2) for l in layers]
    for n in range(1, len(groups)):
        token = start_group(n, later[n - 1], mod_all if n == 1 else token)
    for started_group in in_flight:
        started = started + started_group[-1][0, 0]

    def as_used(k, a):
        return a if k in ("w_in", "ffn_w_up") else a.reshape(N_FFK, FF_SLOT, D) if k == "ffn_w_down" else a.reshape(D, D)

    arrived = {}

    def fetch(l, part, after):
        n = {(0, "in"): 0, (0, "rest"): 1, (1, "in"): 2, (1, "rest"): 3}.get((l, part))
        if n is not None:
            send, recv, _, lands, _ = in_flight[n]
            first = n == 0
            got = exchange_wait(send, recv, [], lands, None, False, after, f"gather_weights_wait_{n}",
                                peers=SAME_CORE_AND_SIBLING if first else ALL_PEERS)
            if first:
                send, recv, _, lands, _ = exchange_start([], got, None, False, "gather_weights_pass_on", peers=OTHER_CHIPS,
                                                         forward=True)
                got = exchange_wait(send, recv, [], lands, None, False, after, "gather_weights_passed_on", peers=OTHER_CHIPS)
            for (k, ll), a in zip(groups[n], got):
                arrived.setdefault(short[k], [None] * n_layers)[ll] = as_used(k, a)
        return arrived

    mod_x = lax.dynamic_index_in_dim(mod_all[:, :, 0], me, axis=2, keepdims=False)
    mod_c = mod_all[:, :, 1, 0]
    mod = [jnp.stack([mod_c[:, l].reshape(6, 1, D), mod_x[:, l].reshape(6, 1, D)]) for l in layers]
    mod[0] = mod[0] + started

    lb = [jnp.zeros((2, 1, D), F32), lb1.reshape(2, 1, D)]

    w = {
        "nw1": [norm1_w[l][None] for l in layers], "nw2": [norm2_w[l][None] for l in layers],
        "lnw": [sgu_ln_w[l][None] for l in layers], "lnb": [sgu_ln_b[l][None] for l in layers],
        "sw": [sgu_w[l] for l in layers], "sb": [sgu_b[l][:, :, None] for l in layers],
        "hnw": [hgrn_norm_w[l][None] for l in layers], "cw": conv_full,
        "cb": [ffn_conv_b[l].reshape(N_FFK, 1, FF_SLOT) for l in layers], "fw": final_norm_w[None],
    }
    long = {v: k for k, v in short.items()}
    landing, sent = {}, []

    def publish(l, part, grads):
        keys = [long[k] for k in grads]
        slots = [a.reshape((N_DEV, -1, a.shape[-1])) for a in grads.values()]
        zones = [place_own(s, landing.get(k, jax.ShapeDtypeStruct((N_DEV, n_layers) + s.shape[1:], s.dtype)), me1, l, True,
                           f"scatter_own_{short[k]}_{l}") for k, s in zip(keys, slots)]
        send, recv, srcs, zones, token = exchange_start(slots, zones, l, True, f"scatter_grads_start_{part}_{l}")
        landing.update(zip(keys, zones))
        sent.append((keys, l, part, send, recv, srcs, token))
        return token[0, 0]

    out = {}
    flat2 = lambda a: a.reshape(-1, a.shape[-1])

    def finish(part, after):
        done = []
        for keys, l, p, send, recv, srcs, _ in sent:
            if p == part:
                zones = exchange_wait(send, recv, srcs, [landing[k] for k in keys], l, True, after,
                                      f"scatter_grads_wait_{part}_{l}")
                landing.update(zip(keys, zones))
                done = keys
        for k in done:
            r = landing[k]
            res = adamw(flat2(wts[k]), flat2(mom1[k]), flat2(mom2[k]), r.reshape(N_DEV, -1, r.shape[-1]), "adamw_" + k)
            out[k] = tuple(a.reshape(wts[k].shape) for a in res)

    rep_rows = -(-sum(_rows_of(wts[k].size) for k in REPLICATED) // 64) * 64
    conv_rows = _rows_of(n_layers * 9 * D_FF)
    dmod_rows = _rows_of(n_layers * 6 * D)
    early = {}

    def small_early(loss_part, g, dmod, dlb):
        d_hlb = lower_bounds_bwd(hgrn_lower_bounds, dlb[1].reshape(1, 2 * D))
        st = lambda k: jnp.stack([jnp.zeros((1, D), F32) if a is None else a for a in g[k]])
        rep_grads = {"norm1_w": st("nw1"), "sgu_ln_w": st("lnw"), "sgu_ln_b": st("lnb"), "sgu_w": st("sw"), "sgu_b": st("sb"),
                     "hgrn_lower_bounds": d_hlb, "hgrn_norm_w": st("hnw"), "norm2_w": st("nw2"), "ffn_conv_b": st("cb"),
                     "final_norm_w": g["fw"]}
        d_conv = jnp.stack([g["cw"][l].transpose(1, 0, 2).reshape(9, D_FF) for l in layers])
        dmod_x = jnp.stack([dmod[l][1].reshape(6 * D) for l in layers])
        dmod_c = jnp.stack([dmod[l][0].reshape(6 * D) for l in layers])
        small = jnp.concatenate([_pack([rep_grads[k] for k in REPLICATED], rep_rows),
                                 _pack([d_conv, dmod_x, dmod_c, loss_part.reshape(1)])], axis=0)
        zone = place_own(small, jax.ShapeDtypeStruct((N_DEV,) + small.shape, F32), me1, None, False, "gather_small_own")
        early["send"], early["recv"], _, early["zones"], token = exchange_start([], [zone], None, False, "gather_small_start")
        return token[0, 0]

    def small_ready(loss_part, g, dmod, dlb):
        late = _pack([g["nw1"][0], dmod[0][1, 0:2], dmod[0][0, 0:2]])
        for part in ("ffn", "mix"):
            finish(part, late)
        (late_all,) = all_gather([late], "gather_small_late", after=out["w_out"][0])
        (small_all,) = exchange_wait(early["send"], early["recv"], [], early["zones"], None, False, late_all,
                                     "gather_small_wait")
        at_x = rep_rows + conv_rows
        small_all = small_all.at[:, 0:1].set(late_all[:, 0:1])
        small_all = small_all.at[:, at_x:at_x + 2].set(late_all[:, 8:10])
        small_all = small_all.at[:, at_x + dmod_rows:at_x + dmod_rows + 2].set(late_all[:, 16:18])
        d_conv_shape, dmod_shape = (n_layers, 9, D_FF), (n_layers, 6 * D)
        conv_g, dmx_all, dmc_all, loss_all = _unpack(small_all[:, rep_rows:], [d_conv_shape, dmod_shape, dmod_shape, (1,)])
        out["loss"] = functools.reduce(lambda a, b: a + b, [loss_all[k, 0] for k in range(N_DEV)])

        rep = adamw(_pack([wts[k] for k in REPLICATED], rep_rows), _pack([mom1[k] for k in REPLICATED], rep_rows),
                    _pack([mom2[k] for k in REPLICATED], rep_rows), small_all, "adamw_replicated")
        rep = [_unpack(r, [wts[k].shape for k in REPLICATED]) for r in rep]
        for n, k in enumerate(REPLICATED):
            out[k] = tuple(r[n] for r in rep)

        conv_mine = lax.dynamic_index_in_dim(conv_g.reshape(N_DEV, n_layers, 9, N_DEV, -1), me, axis=3, keepdims=False)
        res = adamw(flat2(ffn_conv_w), flat2(m_ffn_conv_w), flat2(v_ffn_conv_w),
                    conv_mine.reshape(N_DEV, -1, conv_mine.shape[-1]), "adamw_conv_w")
        out["ffn_conv_w"] = tuple(r.reshape(ffn_conv_w.shape) for r in res)

        out["ada_b"] = tuple(adamw(ada_b, m_ada_b, v_ada_b, jnp.concatenate([dmx_all, dmc_all], axis=0), "adamw_ada_b"))

        cols_of = lambda a: lax.dynamic_slice_in_dim(a, me * ada_cols, ada_cols, axis=2).transpose(1, 0, 2)
        d_ada_w, d_cctx = ada_bwd(c_all, cctx8, ada_w, ada_b_cols, cols_of(dmx_all), cols_of(dmc_all))
        res = adamw(flat2(ada_w), flat2(m_ada_w), flat2(v_ada_w), flat2(d_ada_w)[None], "adamw_ada_w")
        out["ada_w"] = tuple(r.reshape(ada_w.shape) for r in res)
        (d_cctx_all,) = all_gather([d_cctx], "gather_c_ctx_grad")
        res = adamw(c_ctx[None], m_c_ctx[None], v_c_ctx[None], d_cctx_all, "adamw_c_ctx")
        out["c_ctx"] = tuple(r[0] for r in res)
        return d_cctx_all

    _, grad_x, _, _, _, small_done = local_step(x[0], ctx[0], loss_target[0], mod, lb, w, fetch, publish, small_ready,
                                                small_early)
    loss = out["loss"]

    finish("in", small_done)
    return (loss, grad_x[None]) + tuple(out[k][n] for n in range(4) for k in WEIGHT_ORDER)
```

```python
import functools
import math

import jax
import jax.numpy as jnp
from jax import lax
from jax.experimental import pallas as pl
from jax.experimental.pallas import tpu as pltpu

F32 = jnp.float32
BF16 = jnp.bfloat16
HIGHEST = lax.Precision.HIGHEST

N_DEV = 8
AXES = ("x", "y", "c")
D = 1024
CTX = 256
TM = 256
CH = 64
SGU_CH = 128
HEADS = 8
HD = 128
GRID_W = 64
D_IN = 9 * D
IN_SLOT = D_IN // N_DEV
D_FF = 2816
FF_SLOT = 2 * D_FF // N_DEV
N_FFK = D_FF // FF_SLOT
RMS_EPS = 1e-6
LN_EPS = 1e-5
ADAM_LR, ADAM_B1, ADAM_B2, ADAM_EPS, ADAM_WD, ADAM_STEP = 0.001, 0.9, 0.999, 1e-08, 0.01, 10
VMEM_LIMIT_V7X = 56 * 2 ** 20
GRAD_WIRE = jnp.bfloat16

VMEM_WHOLE = pl.BlockSpec(memory_space=pltpu.VMEM)
ANY = pl.BlockSpec(memory_space=pl.ANY)


def _cp(n_axes):
    return pltpu.CompilerParams(dimension_semantics=("arbitrary",) * n_axes, vmem_limit_bytes=VMEM_LIMIT_V7X)


def _dot(a, b, dims):
    return lax.dot_general(a.astype(BF16), b.astype(BF16), (dims, ((), ())), preferred_element_type=F32)


@jax.custom_vjp
def mm(a, b):
    return _dot(a, b, ((1,), (0,)))


mm.defvjp(lambda a, b: (mm(a, b), (a, b)),
          lambda r, g: (_dot(g, r[1], ((1,), (1,))).astype(r[0].dtype), _dot(r[0], g, ((0,), (0,))).astype(r[1].dtype)))


@jax.custom_vjp
def mm_nt(a, b):
    return _dot(a, b, ((1,), (1,)))


mm_nt.defvjp(lambda a, b: (mm_nt(a, b), (a, b)),
             lambda r, g: (_dot(g, r[1], ((1,), (0,))).astype(r[0].dtype), _dot(g, r[0], ((0,), (0,))).astype(r[1].dtype)))


@jax.custom_vjp
def mm_tn(a, b):
    return _dot(a, b, ((0,), (0,)))


mm_tn.defvjp(lambda a, b: (mm_tn(a, b), (a, b)),
             lambda r, g: (_dot(r[1], g, ((1,), (1,))).astype(r[0].dtype), _dot(r[0], g, ((1,), (0,))).astype(r[1].dtype)))


def _tri_dot(m, g):
    hi = g.astype(BF16)
    low = (g - hi.astype(F32)).astype(BF16)
    n = g.shape[1]
    out = jnp.dot(m.astype(BF16), jnp.concatenate([hi, low], axis=1), preferred_element_type=F32)
    return out[:, :n] + out[:, n:]


@jax.custom_vjp
def _cum(m, mt, g):
    return _tri_dot(m, g)


_cum.defvjp(lambda m, mt, g: (_cum(m, mt, g), (m, mt)),
            lambda r, d: (jnp.zeros_like(r[0]), jnp.zeros_like(r[1]), _tri_dot(r[1], d)))


def _silu(x):
    return x * jax.nn.sigmoid(x)


def _gelu(x):
    return 0.5 * x * (1.0 + jnp.tanh(math.sqrt(2.0 / math.pi) * (x + 0.044715 * (x * x * x))))


def _rms(x, w):
    return x * lax.rsqrt(jnp.mean(x * x, axis=-1, keepdims=True) + RMS_EPS) * w


def _norm_mod(x, w, shift, scale):
    return _rms(x, w) * (1.0 + scale) + shift


def _hsl(h):
    return slice(h * HD, (h + 1) * HD)


def _hgrn_chunk(st, qz, fz, iv, lb, m, mt, mref):
    hs = range(HEADS)
    keep = [1.0 - lb[h] for h in hs]
    sg = [jax.nn.sigmoid(fz[h]) for h in hs]
    g = [jnp.log(lb[h] + keep[h] * sg[h]) for h in hs]
    k = [keep[h] * (1.0 - sg[h]) for h in hs]
    q = [_silu(qz[h]) for h in hs]
    b = [_cum(m, mt, g[h]) for h in hs]
    ref = [jnp.sum(mref * g[h], axis=0, keepdims=True) for h in hs]
    last = [jnp.sum(g[h], axis=0, keepdims=True) for h in hs]
    qa = [q[h] * jnp.exp(b[h] - ref[h]) for h in hs]
    ka = [k[h] * jnp.exp(ref[h] - b[h]) for h in hs]
    scores = [jnp.where(m > 0.5, mm_nt(qa[h], ka[h]), 0.0) for h in hs]
    inter = [mm_nt(qa[h] * jnp.exp(ref[h]), st[h]) for h in hs]
    kv = [mm_tn(iv[h], ka[h] * jnp.exp(last[h] - ref[h])) for h in hs]
    outs = [mm(scores[h], iv[h]) + inter[h] for h in hs]
    news = [jnp.exp(last[h]) * st[h] + kv[h] for h in hs]
    return outs, news


def _sgu_fn(ub, vb, lnw, lnb, sw, sb):
    gv = [_gelu(v) for v in vb]
    mu = sum(jnp.sum(t, axis=-1, keepdims=True) for t in gv) / D
    var = sum(jnp.sum((t - mu) * (t - mu), axis=-1, keepdims=True) for t in gv) / D
    inv = lax.rsqrt(var + LN_EPS)
    cols = []
    for g in range(HEADS):
        vn = (gv[g] - mu) * inv * lnw[g] + lnb[g]
        cols.append(_gelu(ub[g]) * (mm(sw[g], vn) + sb[g]))
    return jnp.concatenate(cols, axis=1)


def _readout_fn(ob, og, hnw):
    r = [o * lax.rsqrt(jnp.mean(o * o, axis=-1, keepdims=True) + RMS_EPS) * hnw for o in ob]
    return jnp.concatenate(r, axis=1) * _silu(og)


def _glu_fn(ac, v):
    return _gelu(ac) * v


def _stream_row(tm):
    n_ctx = CTX // tm
    return lambda i: (jnp.where(i < n_ctx, 0, 1), 0, 0, 0)


def in_proj_fwd(x, mod, nw, wg):
    t = x.shape[0]

    def body(x_ref, mod_ref, nw_ref, w_ref, out_ref, ht_ref, iv_ref):
        h32 = _norm_mod(x_ref[...], nw_ref[...], mod_ref[0, 0], mod_ref[0, 1])
        ht_ref[...] = h32.T.astype(BF16)
        h = h32.astype(BF16)
        for j in range(N_DEV):
            out_ref[:, j * IN_SLOT:(j + 1) * IN_SLOT] = jnp.dot(h, w_ref[j], preferred_element_type=F32)
        iv_ref[...] = out_ref[:, 3 * D:4 * D].astype(BF16)

    return pl.pallas_call(
        body, name="in_proj_fwd", grid=(t // TM,),
        in_specs=[pl.BlockSpec((TM, D), lambda i: (i, 0)), pl.BlockSpec((1, 6, 1, D), _stream_row(TM)),
                  pl.BlockSpec((1, D), lambda i: (0, 0)), VMEM_WHOLE],
        out_specs=[pl.BlockSpec((TM, D_IN), lambda i: (i, 0)), pl.BlockSpec((D, TM), lambda i: (0, i)),
                   pl.BlockSpec((TM, D), lambda i: (i, 0))],
        out_shape=[jax.ShapeDtypeStruct((t, D_IN), F32), jax.ShapeDtypeStruct((D, t), BF16), jax.ShapeDtypeStruct((t, D), BF16)],
        compiler_params=_cp(1))(x, mod, nw, wg)


def _scan_chunk(nc):
    ncc = CTX // CH

    def chunk(d, s):
        bwd = jnp.where(s < ncc, ncc - 1 - s, nc + ncc - 1 - s)
        return jnp.where(d == 0, s, bwd)
    return chunk


def hgrn_fwd(parts, iv, lb, mc, mtc, mrefc):
    t = parts.shape[0]
    nc = t // CH
    chunk = _scan_chunk(nc)

    def body(q_ref, f_ref, i_ref, lb_ref, m_ref, mt_ref, mr_ref, o_ref, ck_ref, st):
        @pl.when(pl.program_id(1) == 0)
        def _():
            st[...] = jnp.zeros_like(st)
        ck_ref[0, 0] = st[...].astype(BF16)
        outs, news = _hgrn_chunk([st[h] for h in range(HEADS)], [q_ref[:, _hsl(h)] for h in range(HEADS)],
                                 [f_ref[:, _hsl(h)] for h in range(HEADS)], [i_ref[:, _hsl(h)] for h in range(HEADS)],
                                 [lb_ref[0, :, _hsl(h)] for h in range(HEADS)], m_ref[0], mt_ref[0], mr_ref[0])
        for h in range(HEADS):
            o_ref[0, :, _hsl(h)] = outs[h]
            st[h] = news[h]

    const = lambda d, s: (d, 0, 0)
    return pl.pallas_call(
        body, name="hgrn_fwd", grid=(2, nc),
        in_specs=[pl.BlockSpec((CH, D), lambda d, s: (chunk(d, s), 0)), pl.BlockSpec((CH, D), lambda d, s: (chunk(d, s), 1 + d)),
                  pl.BlockSpec((CH, D), lambda d, s: (chunk(d, s), 0)), pl.BlockSpec((1, 1, D), const),
                  pl.BlockSpec((1, CH, CH), const), pl.BlockSpec((1, CH, CH), const), pl.BlockSpec((1, CH, 1), const)],
        out_specs=[pl.BlockSpec((1, CH, D), lambda d, s: (d, chunk(d, s), 0)),
                   pl.BlockSpec((1, 1, HEADS, HD, HD), lambda d, s: (d, s, 0, 0, 0))],
        out_shape=[jax.ShapeDtypeStruct((2, t, D), F32), jax.ShapeDtypeStruct((2, nc, HEADS, HD, HD), BF16)],
        scratch_shapes=[pltpu.VMEM((HEADS, HD, HD), F32)], compiler_params=_cp(2))(parts, parts, iv, lb, mc, mtc, mrefc)


def _mixer_tile(rows, u_ref, v_ref, og_ref, o_ref, lnw_ref, lnb_ref, sw_ref, sb_ref, hnw_ref):
    n = (rows.stop - rows.start) // SGU_CH
    yas, vjps = [], []
    for c in range(n):
        r = slice(rows.start + c * SGU_CH, rows.start + (c + 1) * SGU_CH)
        ya, vjp_a = jax.vjp(_sgu_fn, [u_ref[r, _hsl(g)] for g in range(HEADS)], [v_ref[r, _hsl(g)] for g in range(HEADS)],
                            [lnw_ref[:, _hsl(g)] for g in range(HEADS)], [lnb_ref[:, _hsl(g)] for g in range(HEADS)],
                            [sw_ref[g] for g in range(HEADS)], [sb_ref[g] for g in range(HEADS)])
        yas.append(ya)
        vjps.append(vjp_a)
    yb, vjp_b = jax.vjp(_readout_fn, [o_ref[0, rows, _hsl(h)] + o_ref[1, rows, _hsl(h)] for h in range(HEADS)],
                        og_ref[rows, :], hnw_ref[...])
    return (yas[0] if n == 1 else jnp.concatenate(yas, axis=0)), yb, vjps, vjp_b


def _part_specs(tm, first, n):
    return [pl.BlockSpec((tm, D), functools.partial(lambda k, i: (i, k), first + k)) for k in range(n)]


def _unless_ctx(skip_ctx, is_ctx, zero_refs, work):
    if not skip_ctx:
        return work()

    @pl.when(is_ctx)
    def _():
        for r in zero_refs:
            r[...] = jnp.zeros_like(r)

    pl.when(jnp.logical_not(is_ctx))(work)


def mixer_fwd(x, parts, o, mod, lnw, lnb, sw, sb, hnw, wa, wb, wo, skip_ctx):
    t = x.shape[0]

    def body(x_ref, u_ref, v_ref, og_ref, ga_ref, gb_ref, o_ref, mod_ref, lnw_ref, lnb_ref, sw_ref, sb_ref, hnw_ref,
             wa_ref, wb_ref, wo_ref, out_ref, pa_ref, pb_ref, y_ref, yat_ref, ybt_ref, mt_ref):
        def work():
            ya, yb, _, _ = _mixer_tile(slice(0, TM), u_ref, v_ref, og_ref, o_ref, lnw_ref, lnb_ref, sw_ref, sb_ref, hnw_ref)
            pa, pb = mm(ya, wa_ref[...]), mm(yb, wb_ref[...])
            merged = jax.nn.sigmoid(ga_ref[...]) * pa + jax.nn.sigmoid(gb_ref[...]) * pb
            y = mm(merged, wo_ref[...])
            out_ref[...] = x_ref[...] + mod_ref[0, 2] * y
            pa_ref[...], pb_ref[...], y_ref[...] = pa.astype(BF16), pb.astype(BF16), y.astype(BF16)
            yat_ref[...], ybt_ref[...], mt_ref[...] = ya.T.astype(BF16), yb.T.astype(BF16), merged.T.astype(BF16)

        _unless_ctx(skip_ctx, pl.program_id(0) == 0, (out_ref, pa_ref, pb_ref, y_ref, yat_ref, ybt_ref, mt_ref), work)

    vec = lambda n: pl.BlockSpec((1, n), lambda i: (0, 0))
    tile = pl.BlockSpec((TM, D), lambda i: (i, 0))
    tile_t = pl.BlockSpec((D, TM), lambda i: (0, i))
    return pl.pallas_call(
        body, name="mixer_fwd", grid=(t // TM,),
        in_specs=[tile] + _part_specs(TM, 4, 5)
        + [pl.BlockSpec((2, TM, D), lambda i: (0, i, 0)), pl.BlockSpec((1, 6, 1, D), _stream_row(TM)), vec(D), vec(D),
           VMEM_WHOLE, VMEM_WHOLE, vec(HD), VMEM_WHOLE, VMEM_WHOLE, VMEM_WHOLE],
        out_specs=[tile] * 4 + [tile_t] * 3,
        out_shape=[jax.ShapeDtypeStruct((t, D), F32)] + [jax.ShapeDtypeStruct((t, D), BF16)] * 3
        + [jax.ShapeDtypeStruct((D, t), BF16)] * 3, compiler_params=_cp(1),
    )(x, parts, parts, parts, parts, parts, o, mod, lnw, lnb, sw, sb, hnw, wa, wb, wo)


def ffn_up_fwd(x, mod, nw, wg, skip_ctx):
    t = x.shape[0]

    def body(x_ref, mod_ref, nw_ref, w_ref, out_ref, ht_ref):
        def work():
            h32 = _norm_mod(x_ref[...], nw_ref[...], mod_ref[0, 3], mod_ref[0, 4])
            ht_ref[...] = h32.T.astype(BF16)
            h = h32.astype(BF16)
            for j in range(N_DEV):
                out_ref[j] = jnp.dot(h, w_ref[j], preferred_element_type=F32)

        _unless_ctx(skip_ctx, pl.program_id(0) == 0, (out_ref, ht_ref), work)

    return pl.pallas_call(
        body, name="ffn_up_fwd", grid=(t // TM,),
        in_specs=[pl.BlockSpec((TM, D), lambda i: (i, 0)), pl.BlockSpec((1, 6, 1, D), _stream_row(TM)),
                  pl.BlockSpec((1, D), lambda i: (0, 0)), VMEM_WHOLE],
        out_specs=[pl.BlockSpec((N_DEV, TM, FF_SLOT), lambda i: (0, i, 0)), pl.BlockSpec((D, TM), lambda i: (0, i))],
        out_shape=[jax.ShapeDtypeStruct((N_DEV, t, FF_SLOT), F32), jax.ShapeDtypeStruct((D, t), BF16)],
        compiler_params=_cp(1))(x, mod, nw, wg)


def _halo_specs(nt, k_of, i_of):
    per = TM // GRID_W
    last = nt * per - 1
    return [pl.BlockSpec((1, GRID_W, FF_SLOT), lambda *g: (k_of(*g), jnp.maximum(i_of(*g) * per - 1, 0), 0)),
            pl.BlockSpec((1, TM, FF_SLOT), lambda *g: (k_of(*g), i_of(*g), 0)),
            pl.BlockSpec((1, GRID_W, FF_SLOT), lambda *g: (k_of(*g), jnp.minimum(i_of(*g) * per + per, last), 0))]


def _with_halo(prev_ref, main_ref, next_ref, i, nt):
    prev = jnp.where(i >= 2, prev_ref[0], 0.0)
    nxt = jnp.where((i >= 1) & (i <= nt - 2), next_ref[0], 0.0)
    return jnp.concatenate([prev, main_ref[0], nxt], axis=0)


def _tap_valid(dc, i, n_rows, offset):
    r = lax.broadcasted_iota(jnp.int32, (n_rows, 1), 0) - offset
    col = jnp.bitwise_and(r, GRID_W - 1)
    pos = jnp.where(i == 0, r, col) + dc
    return (pos >= 0) & (pos < jnp.where(i == 0, TM, GRID_W))


def _row_weight(cw_ref, dr, dc, i):
    w = cw_ref[0, 3 * (dr + 1) + dc + 1:3 * (dr + 1) + dc + 2, :]
    return w if dr == 0 else jnp.where(i == 0, 0.0, w)


def ffn_down_fwd(x, av, mod, cw, cb, wd, skip_ctx):
    t = x.shape[0]
    nt = t // TM
    ext = TM + 2 * GRID_W

    def body(x_ref, ap_ref, am_ref, an_ref, v_ref, mod_ref, cw_ref, cb_ref, wd_ref, out_ref, ac_ref, y_ref, z_ref, acc):
        i, k = pl.program_id(0), pl.program_id(1)

        def work():
            a_ext = _with_halo(ap_ref, am_ref, an_ref, i, nt)
            conv = jnp.zeros((TM, FF_SLOT), F32) + cb_ref[0]
            for dc in (-1, 0, 1):
                rolled = (a_ext if dc == 0 else
                          jnp.where(_tap_valid(dc, i, ext, GRID_W), pltpu.roll(a_ext, (-dc) % ext, 0), 0.0))
                for dr in (-1, 0, 1):
                    lo = GRID_W + GRID_W * dr
                    conv = conv + rolled[lo:lo + TM] * _row_weight(cw_ref, dr, dc, i)
            ac_ref[0] = conv.astype(BF16)
            z = _glu_fn(conv, v_ref[0]).astype(BF16)
            z_ref[0] = z
            part = mm(z, wd_ref[0])

            @pl.when(k == 0)
            def _():
                acc[...] = part

            @pl.when(k > 0)
            def _():
                acc[...] += part

            @pl.when(k == N_FFK - 1)
            def _():
                y_ref[...] = acc[...]
                out_ref[...] = x_ref[...] + mod_ref[0, 5] * acc[...]

        _unless_ctx(skip_ctx, i == 0, (out_ref, ac_ref, y_ref, z_ref), work)

    tile = pl.BlockSpec((TM, D), lambda i, k: (i, 0))
    return pl.pallas_call(
        body, name="ffn_down_fwd", grid=(nt, N_FFK),
        in_specs=[tile] + _halo_specs(nt, lambda i, k: k, lambda i, k: i)
        + [pl.BlockSpec((1, TM, FF_SLOT), lambda i, k: (N_FFK + k, i, 0)),
           pl.BlockSpec((1, 6, 1, D), lambda i, k: (jnp.where(i < 1, 0, 1), 0, 0, 0)),
           pl.BlockSpec((1, 9, FF_SLOT), lambda i, k: (k, 0, 0)), pl.BlockSpec((1, 1, FF_SLOT), lambda i, k: (k, 0, 0)),
           pl.BlockSpec((1, FF_SLOT, D), lambda i, k: (k, 0, 0))],
        out_specs=[tile, pl.BlockSpec((1, TM, FF_SLOT), lambda i, k: (k, i, 0)), tile,
                   pl.BlockSpec((1, TM, FF_SLOT), lambda i, k: (k, i, 0))],
        out_shape=[jax.ShapeDtypeStruct((t, D), F32), jax.ShapeDtypeStruct((N_FFK, t, FF_SLOT), BF16),
                   jax.ShapeDtypeStruct((t, D), F32), jax.ShapeDtypeStruct((N_FFK, t, FF_SLOT), BF16)],
        scratch_shapes=[pltpu.VMEM((TM, D), F32)], compiler_params=_cp(2))(x, av, av, av, av, mod, cw, cb, wd)


def loss_fwd_bwd(x, target, fw):
    t = x.shape[0]

    def body(x_ref, t_ref, w_ref, loss_ref, dx_ref, dw_ref):
        i = pl.program_id(0)

        @pl.when(i == 0)
        def _():
            loss_ref[...] = jnp.zeros_like(loss_ref)
            dw_ref[...] = jnp.zeros_like(dw_ref)
            dx_ref[...] = jnp.zeros_like(dx_ref)

        @pl.when(i > 0)
        def _():
            y, vjp = jax.vjp(_rms, x_ref[...], w_ref[...])
            err = y - t_ref[...]
            loss_ref[...] += 0.5 * jnp.sum(jnp.sum(err * err, axis=-1, keepdims=True) / D)
            dx, dw = vjp(err / D)
            dx_ref[...] = dx
            dw_ref[...] += dw

    return pl.pallas_call(
        body, name="loss_fwd_bwd", grid=(t // TM,),
        in_specs=[pl.BlockSpec((TM, D), lambda i: (i, 0)), pl.BlockSpec((TM, D), lambda i: (jnp.maximum(i - 1, 0), 0)),
                  pl.BlockSpec((1, D), lambda i: (0, 0))],
        out_specs=[pl.BlockSpec((8, 128), lambda i: (0, 0)), pl.BlockSpec((TM, D), lambda i: (i, 0)),
                   pl.BlockSpec((1, D), lambda i: (0, 0))],
        out_shape=[jax.ShapeDtypeStruct((8, 128), F32), jax.ShapeDtypeStruct((t, D), F32), jax.ShapeDtypeStruct((1, D), F32)],
        compiler_params=_cp(1))(x, target, fw)


def _stream_add(ref, k, is_ctx, val):
    ref[0, k] += jnp.where(is_ctx, val, 0.0)
    ref[1, k] += jnp.where(is_ctx, 0.0, val)


def ffn_down_bwd(dx, ac, av, y, mod, wd, skip_ctx):
    t = dx.shape[0]
    nt = t // TM

    def body(dx_ref, ac_ref, v_ref, y_ref, mod_ref, wd_ref, dav_ref, dac_ref, dout_ref, dg_ref):
        i = pl.program_id(0)

        @pl.when(i == 0)
        def _():
            dg_ref[...] = jnp.zeros_like(dg_ref)

        def work():
            _stream_add(dg_ref, 0, i == 0, jnp.sum(dx_ref[...] * y_ref[...], axis=0, keepdims=True))
            dout = (mod_ref[0, 5] * dx_ref[...]).astype(BF16)
            dout_ref[...] = dout
            for k in range(N_FFK):
                _, vjp = jax.vjp(_glu_fn, ac_ref[k].astype(F32), v_ref[k])
                dac, dv = vjp(mm_nt(dout, wd_ref[k]))
                dac_ref[k] = dac
                dav_ref[k] = dv.astype(BF16)

        _unless_ctx(skip_ctx, i == 0, (dav_ref, dac_ref, dout_ref), work)

    tile = pl.BlockSpec((TM, D), lambda i: (i, 0))
    half = lambda first: pl.BlockSpec((N_FFK, TM, FF_SLOT), lambda i: (first, i, 0))
    return pl.pallas_call(
        body, name="ffn_down_bwd", grid=(nt,),
        in_specs=[tile, half(0), half(1), tile, pl.BlockSpec((1, 6, 1, D), _stream_row(TM)), VMEM_WHOLE],
        out_specs=[half(1), half(0), tile, pl.BlockSpec((2, 1, 1, D), lambda i: (0, 0, 0, 0))],
        out_shape=[jax.ShapeDtypeStruct((N_DEV, t, FF_SLOT), BF16), jax.ShapeDtypeStruct((N_FFK, t, FF_SLOT), F32),
                   jax.ShapeDtypeStruct((t, D), BF16), jax.ShapeDtypeStruct((2, 1, 1, D), F32)],
        compiler_params=_cp(1))(dx, ac, av, y, mod, wd)


def conv_bwd(dav, dac, av, cw, skip_ctx):
    t = dac.shape[1]
    nt = t // TM
    ext = TM + 2 * GRID_W

    def body(dav_in, gp_ref, gm_ref, gn_ref, ap_ref, am_ref, an_ref, cw_ref, dav_ref, dcw_ref, dcb_ref):
        k, i = pl.program_id(0), pl.program_id(1)

        @pl.when(i == 0)
        def _():
            dcw_ref[...] = jnp.zeros_like(dcw_ref)
            dcb_ref[...] = jnp.zeros_like(dcb_ref)

        def work():
            g_ext = _with_halo(gp_ref, gm_ref, gn_ref, i, nt)
            a_ext = _with_halo(ap_ref, am_ref, an_ref, i, nt)
            g_main = gm_ref[0]
            dcb_ref[0] += jnp.sum(g_main, axis=0, keepdims=True)
            da = jnp.zeros((TM, FF_SLOT), F32)
            for dc in (-1, 0, 1):
                g_rolled = (g_ext if dc == 0 else
                            pltpu.roll(jnp.where(_tap_valid(dc, i, ext, GRID_W), g_ext, 0.0), dc % ext, 0))
                a_rolled = a_ext if dc == 0 else pltpu.roll(a_ext, (-dc) % ext, 0)
                g_valid = g_main if dc == 0 else jnp.where(_tap_valid(dc, i, TM, 0), g_main, 0.0)
                for dr in (-1, 0, 1):
                    lo = GRID_W - GRID_W * dr
                    da = da + g_rolled[lo:lo + TM] * _row_weight(cw_ref, dr, dc, i)
                    lo = GRID_W + GRID_W * dr
                    tap = 3 * (dr + 1) + dc + 1
                    dw = jnp.sum(g_valid * a_rolled[lo:lo + TM], axis=0, keepdims=True)
                    dcw_ref[0, tap:tap + 1, :] += dw if dr == 0 else jnp.where(i == 0, 0.0, dw)
            dav_ref[0] = da.astype(BF16)

        _unless_ctx(skip_ctx, i == 0, (dav_ref,), work)

    return pl.pallas_call(
        body, name="conv_bwd", grid=(N_FFK, nt),
        in_specs=[ANY] + _halo_specs(nt, lambda k, i: k, lambda k, i: i) + _halo_specs(nt, lambda k, i: k, lambda k, i: i)
        + [pl.BlockSpec((1, 9, FF_SLOT), lambda k, i: (k, 0, 0))],
        out_specs=[pl.BlockSpec((1, TM, FF_SLOT), lambda k, i: (k, i, 0)), pl.BlockSpec((1, 9, FF_SLOT), lambda k, i: (k, 0, 0)),
                   pl.BlockSpec((1, 1, FF_SLOT), lambda k, i: (k, 0, 0))],
        out_shape=[jax.ShapeDtypeStruct(dav.shape, BF16), jax.ShapeDtypeStruct((N_FFK, 9, FF_SLOT), F32),
                   jax.ShapeDtypeStruct((N_FFK, 1, FF_SLOT), F32)],
        input_output_aliases={0: 0}, compiler_params=_cp(2))(dav, dac, dac, dac, av, av, av, cw)


def _norm_mod_bwd(x_ref, nw_ref, mod_ref, k_shift, dh, dx_in, dx_ref, dnw_ref, dmod_ref, is_ctx):
    _, vjp = jax.vjp(_norm_mod, x_ref[...], nw_ref[...], mod_ref[0, k_shift], mod_ref[0, k_shift + 1])
    dx, dnw, dshift, dscale = vjp(dh)
    dx_ref[...] = dx_in + dx
    dnw_ref[...] += dnw
    _stream_add(dmod_ref, 0, is_ctx, dshift)
    _stream_add(dmod_ref, 1, is_ctx, dscale)


def ffn_up_bwd_x(dx2, x, dav, mod, nw, wg, skip_ctx):
    t = x.shape[0]

    def body(dx2_ref, x_ref, dav_ref, mod_ref, nw_ref, w_ref, dx_ref, dnw_ref, dmod_ref):
        i = pl.program_id(0)

        @pl.when(i == 0)
        def _():
            dnw_ref[...] = jnp.zeros_like(dnw_ref)
            dmod_ref[...] = jnp.zeros_like(dmod_ref)

        def work():
            dh = mm_nt(dav_ref[0], w_ref[0])
            for j in range(1, N_DEV):
                dh = dh + mm_nt(dav_ref[j], w_ref[j])
            _norm_mod_bwd(x_ref, nw_ref, mod_ref, 3, dh, dx2_ref[...], dx_ref, dnw_ref, dmod_ref, i == 0)

        _unless_ctx(skip_ctx, i == 0, (dx_ref,), work)

    tile = pl.BlockSpec((TM, D), lambda i: (i, 0))
    return pl.pallas_call(
        body, name="ffn_up_bwd_x", grid=(t // TM,),
        in_specs=[tile, tile, pl.BlockSpec((N_DEV, TM, FF_SLOT), lambda i: (0, i, 0)), pl.BlockSpec((1, 6, 1, D), _stream_row(TM)),
                  pl.BlockSpec((1, D), lambda i: (0, 0)), VMEM_WHOLE],
        out_specs=[tile, pl.BlockSpec((1, D), lambda i: (0, 0)), pl.BlockSpec((2, 2, 1, D), lambda i: (0, 0, 0, 0))],
        out_shape=[jax.ShapeDtypeStruct((t, D), F32), jax.ShapeDtypeStruct((1, D), F32), jax.ShapeDtypeStruct((2, 2, 1, D), F32)],
        compiler_params=_cp(1))(dx2, x, dav, mod, nw, wg)


def weight_grad(at, dout, slot, name, after=None):
    rows, t = at.shape
    stacked = dout.ndim == 3
    n = dout.shape[0] if stacked else dout.shape[1] // slot

    def body(a_ref, d_ref, *rest):
        dw_ref = rest[-1]
        dw_ref[0] = jnp.dot(a_ref[...], d_ref[0] if stacked else d_ref[...], preferred_element_type=F32).astype(dw_ref.dtype)

    d_spec = pl.BlockSpec((1, t, slot), lambda j: (j, 0, 0)) if stacked else pl.BlockSpec((t, slot), lambda j: (0, j))
    extra = [] if after is None else [jnp.reshape(after, (1, 1))]
    return pl.pallas_call(
        body, name=name, grid=(n,), in_specs=[VMEM_WHOLE, d_spec] + [ANY] * len(extra),
        out_specs=pl.BlockSpec((1, rows, slot), lambda j: (j, 0, 0)),
        out_shape=jax.ShapeDtypeStruct((n, rows, slot), GRAD_WIRE), compiler_params=_cp(1))(at, dout, *extra)


def weight_grad_rows(at, dout, name):
    n, t, rows = at.shape
    cols = dout.shape[1]

    def body(a_ref, d_ref, dw_ref):
        dw_ref[0] = _dot(a_ref[0], d_ref[...], ((0,), (0,))).astype(dw_ref.dtype)

    return pl.pallas_call(
        body, name=name, grid=(n,), in_specs=[pl.BlockSpec((1, t, rows), lambda k: (k, 0, 0)), VMEM_WHOLE],
        out_specs=pl.BlockSpec((1, rows, cols), lambda k: (k, 0, 0)),
        out_shape=jax.ShapeDtypeStruct((n, rows, cols), GRAD_WIRE), compiler_params=_cp(1))(at, dout)


def mixer_bwd(dx, parts, o, pa, pb, y, mod, lnw, lnb, sw, sb, hnw, wa, wb, wo, skip_ctx):
    t = dx.shape[0]
    tm = TM
    n_ctx = CTX // tm

    def body(dx_ref, u_ref, v_ref, og_ref, ga_ref, gb_ref, o_ref, pa_ref, pb_ref, y_ref, mod_ref, lnw_ref, lnb_ref, sw_ref,
             sb_ref, hnw_ref, wa_ref, wb_ref, wo_ref, dp_ref, do_ref, dy_ref, dpa_ref, dpb_ref, dlnw_ref, dlnb_ref, dsw_ref,
             dsb_ref, dhnw_ref, dg_ref):
        i = pl.program_id(0)

        @pl.when(i == 0)
        def _():
            for r in (dlnw_ref, dlnb_ref, dsw_ref, dsb_ref, dhnw_ref, dg_ref):
                r[...] = jnp.zeros_like(r)

        def work():
            _, _, vjps, vjp_b = _mixer_tile(slice(0, tm), u_ref, v_ref, og_ref, o_ref, lnw_ref, lnb_ref, sw_ref, sb_ref, hnw_ref)
            pa, pb = pa_ref[...].astype(F32), pb_ref[...].astype(F32)
            sa, sbg = jax.nn.sigmoid(ga_ref[...]), jax.nn.sigmoid(gb_ref[...])
            dxv = dx_ref[...]
            _stream_add(dg_ref, 0, i < n_ctx, jnp.sum(dxv * y_ref[...].astype(F32), axis=0, keepdims=True))
            dy = (mod_ref[0, 2] * dxv).astype(BF16)
            dy_ref[...] = dy
            dmerged = mm_nt(dy, wo_ref[...])
            dpa, dpb = (sa * dmerged).astype(BF16), (sbg * dmerged).astype(BF16)
            dpa_ref[...], dpb_ref[...] = dpa, dpb
            first = 4 * D
            dp_ref[:, first + 3 * D:first + 4 * D] = (dmerged * pa * sa * (1.0 - sa)).astype(BF16)
            dp_ref[:, first + 4 * D:first + 5 * D] = (dmerged * pb * sbg * (1.0 - sbg)).astype(BF16)
            dya = mm_nt(dpa, wa_ref[...])
            dob, dog, dhnw = vjp_b(mm_nt(dpb, wb_ref[...]))
            dp_ref[:, first + 2 * D:first + 3 * D] = dog.astype(BF16)
            dhnw_ref[...] += dhnw
            for g in range(HEADS):
                do_ref[:, _hsl(g)] = dob[g]
            for c, vjp_a in enumerate(vjps):
                rows = slice(c * SGU_CH, (c + 1) * SGU_CH)
                dub, dvb, dlnw, dlnb, dsw, dsb = vjp_a(dya[rows])
                for g in range(HEADS):
                    dp_ref[rows, first + g * HD:first + (g + 1) * HD] = dub[g].astype(BF16)
                    dp_ref[rows, first + D + g * HD:first + D + (g + 1) * HD] = dvb[g].astype(BF16)
                    dlnw_ref[:, _hsl(g)] += dlnw[g]
                    dlnb_ref[:, _hsl(g)] += dlnb[g]
                    dsw_ref[g] += dsw[g]
                    dsb_ref[g] += dsb[g]

        _unless_ctx(skip_ctx, i < n_ctx, (dp_ref, do_ref, dy_ref, dpa_ref, dpb_ref), work)

    vec = lambda n: pl.BlockSpec((1, n), lambda i: (0, 0))
    tile = pl.BlockSpec((tm, D), lambda i: (i, 0))
    sds = jax.ShapeDtypeStruct
    return pl.pallas_call(
        body, name="mixer_bwd", grid=(t // tm,),
        in_specs=[tile] + _part_specs(tm, 4, 5)
        + [pl.BlockSpec((2, tm, D), lambda i: (0, i, 0)), tile, tile, tile, pl.BlockSpec((1, 6, 1, D), _stream_row(tm)),
           vec(D), vec(D), VMEM_WHOLE, VMEM_WHOLE, vec(HD), VMEM_WHOLE, VMEM_WHOLE, VMEM_WHOLE],
        out_specs=[pl.BlockSpec((tm, D_IN), lambda i: (i, 0)), tile, tile, tile, tile, vec(D), vec(D),
                   VMEM_WHOLE, VMEM_WHOLE, vec(HD), pl.BlockSpec((2, 1, 1, D), lambda i: (0, 0, 0, 0))],
        out_shape=[sds((t, D_IN), BF16), sds((t, D), F32), sds((t, D), BF16), sds((t, D), BF16), sds((t, D), BF16),
                   sds((1, D), F32), sds((1, D), F32), sds((HEADS, SGU_CH, SGU_CH), F32), sds((HEADS, SGU_CH, 1), F32),
                   sds((1, HD), F32), sds((2, 1, 1, D), F32)],
        compiler_params=_cp(1))(dx, parts, parts, parts, parts, parts, o, pa, pb, y, mod, lnw, lnb, sw, sb, hnw, wa, wb, wo)


def hgrn_bwd(d, parts, iv, lb, mc, mtc, mrefc, ck, do, first=None, dparts=None):
    t = parts.shape[0]
    nc = t // CH
    chunk = _scan_chunk(nc)
    rev = lambda s: chunk(d, nc - 1 - s)

    def body(q_ref, f_ref, i_ref, lb_ref, m_ref, mt_ref, mr_ref, ck_ref, do_ref, *rest):
        dst = rest[-1]
        dlb_ref = rest[-2]

        @pl.when(pl.program_id(0) == 0)
        def _():
            dst[...] = jnp.zeros_like(dst)
            dlb_ref[...] = jnp.zeros_like(dlb_ref)

        heads = range(HEADS)
        fn = functools.partial(_hgrn_chunk, m=m_ref[0], mt=mt_ref[0], mref=mr_ref[0])
        _, vjp = jax.vjp(fn, [ck_ref[0, 0, h].astype(F32) for h in heads], [q_ref[:, _hsl(h)] for h in heads],
                         [f_ref[:, _hsl(h)] for h in heads], [i_ref[:, _hsl(h)] for h in heads],
                         [lb_ref[0, :, _hsl(h)] for h in heads])
        dstl, dq, df, di, dlb = vjp(([do_ref[:, _hsl(h)] for h in heads], [dst[h] for h in heads]))
        for h in heads:
            dst[h] = dstl[h]
            dlb_ref[0, :, _hsl(h)] += dlb[h]
            if d == 0:
                dq_ref, df_ref, di_ref = rest[:3]
                dq_ref[:, _hsl(h)] = dq[h].astype(BF16)
                df_ref[:, _hsl(h)] = df[h].astype(BF16)
                di_ref[:, _hsl(h)] = di[h].astype(BF16)
            else:
                dq0_ref, df0_ref, di0_ref, _, dp_ref = rest[:5]
                col = lambda k: slice(k * D + h * HD, k * D + (h + 1) * HD)
                dp_ref[:, col(0)] = (dq0_ref[:, _hsl(h)].astype(F32) + dq[h]).astype(BF16)
                dp_ref[:, col(1)] = df0_ref[:, _hsl(h)]
                dp_ref[:, col(2)] = df[h].astype(BF16)
                dp_ref[:, col(3)] = (di0_ref[:, _hsl(h)].astype(F32) + di[h]).astype(BF16)

    const = lambda s: (d, 0, 0)
    at = lambda k: pl.BlockSpec((CH, D), lambda s: (rev(s), k))
    in_specs = [at(0), at(1 + d), at(0), pl.BlockSpec((1, 1, D), const), pl.BlockSpec((1, CH, CH), const),
                pl.BlockSpec((1, CH, CH), const), pl.BlockSpec((1, CH, 1), const),
                pl.BlockSpec((1, 1, HEADS, HD, HD), lambda s: (d, nc - 1 - s, 0, 0, 0)), at(0)]
    dlb_spec, dlb_shape = pl.BlockSpec((1, 1, D), lambda s: (0, 0, 0)), jax.ShapeDtypeStruct((1, 1, D), F32)
    common = dict(grid=(nc,), scratch_shapes=[pltpu.VMEM((HEADS, HD, HD), F32)], compiler_params=_cp(1))
    if d == 0:
        return pl.pallas_call(body, name="hgrn_bwd_fwd_dir", in_specs=in_specs, out_specs=[at(0)] * 3 + [dlb_spec],
                              out_shape=[jax.ShapeDtypeStruct((t, D), BF16)] * 3 + [dlb_shape], **common,
                              )(parts, parts, iv, lb, mc, mtc, mrefc, ck, do)
    return pl.pallas_call(body, name="hgrn_bwd_bwd_dir", in_specs=in_specs + [at(0)] * 3 + [ANY],
                          out_specs=[pl.BlockSpec((CH, 4 * D), lambda s: (rev(s), 0)), dlb_spec],
                          out_shape=[jax.ShapeDtypeStruct(dparts.shape, BF16), dlb_shape], input_output_aliases={12: 0},
                          **common)(parts, parts, iv, lb, mc, mtc, mrefc, ck, do, *first, dparts)


def in_proj_bwd_x(dx1, x, dparts, mod, nw, wg, after=None, latent_only=False):
    t = x.shape[0]
    tm = TM
    n_ctx = CTX // tm

    def body(dx1_ref, x_ref, dp_ref, mod_ref, nw_ref, w_ref, *rest):
        dx_ref, dnw_ref, dmod_ref = rest[-3:]
        i = pl.program_id(0)

        @pl.when(i == 0)
        def _():
            dnw_ref[...] = jnp.zeros_like(dnw_ref)
            dmod_ref[...] = jnp.zeros_like(dmod_ref)

        dh = mm_nt(dp_ref[:, 0:IN_SLOT], w_ref[0])
        for j in range(1, N_DEV):
            dh = dh + mm_nt(dp_ref[:, j * IN_SLOT:(j + 1) * IN_SLOT], w_ref[j])
        _norm_mod_bwd(x_ref, nw_ref, mod_ref, 0, dh, dx1_ref[...], dx_ref, dnw_ref, dmod_ref, i < n_ctx)

    tile = pl.BlockSpec((tm, D), lambda i: (i, 0))
    extra = [] if after is None else [jnp.reshape(after, (1, 1))]
    return pl.pallas_call(
        body, name="in_proj_bwd_x", grid=(t // tm,),
        in_specs=[tile, tile, pl.BlockSpec((tm, D_IN), lambda i: (i, 0)), pl.BlockSpec((1, 6, 1, D), _stream_row(tm)),
                  pl.BlockSpec((1, D), lambda i: (0, 0)), VMEM_WHOLE] + [ANY] * len(extra),
        out_specs=[pl.BlockSpec((tm, D), lambda i: (jnp.maximum(i - n_ctx, 0), 0)) if latent_only else tile,
                   pl.BlockSpec((1, D), lambda i: (0, 0)), pl.BlockSpec((2, 2, 1, D), lambda i: (0, 0, 0, 0))],
        out_shape=[jax.ShapeDtypeStruct((t - CTX if latent_only else t, D), F32), jax.ShapeDtypeStruct((1, D), F32),
                   jax.ShapeDtypeStruct((2, 2, 1, D), F32)],
        compiler_params=_cp(1))(dx1, x, dparts, mod, nw, wg, *extra)


def _lb_fn(h0, h1):
    m = jnp.maximum(h0, h1)
    e0, e1 = jnp.exp(h0 - m), jnp.exp(h1 - m)
    return e1 / (e0 + e1)


def lower_bounds(hlb):
    def body(h_ref, out_ref):
        out_ref[...] = _lb_fn(h_ref[0:1, :], h_ref[1:2, :])
    return pl.pallas_call(body, name="lower_bounds", out_shape=jax.ShapeDtypeStruct((1, 2 * D), F32))(hlb)


def lower_bounds_bwd(hlb, dlb1):
    def body(h_ref, d_ref, out_ref):
        _, vjp = jax.vjp(_lb_fn, h_ref[0:1, :], h_ref[1:2, :])
        d0, d1 = vjp(d_ref[...])
        out_ref[0:1, :] = d0
        out_ref[1:2, :] = d1
    return pl.pallas_call(body, name="lower_bounds_bwd", out_shape=jax.ShapeDtypeStruct((2, 2 * D), F32))(hlb, dlb1)


def _ada_fn(c_all, cctx8, w, b):
    dot = lambda a, l: jnp.dot(_silu(a), w[l], precision=HIGHEST, preferred_element_type=F32) + b[l]
    return [dot(c_all, l) for l in range(2)], [dot(cctx8, l) for l in range(2)]


def ada_fwd(c_all, cctx8, w, b):
    cols = w.shape[-1]

    def body(c_ref, cc_ref, w_ref, b_ref, out_ref):
        ox, oc = _ada_fn(c_ref[...], cc_ref[...], [w_ref[0], w_ref[1]], [b_ref[0], b_ref[1]])
        for l in range(2):
            out_ref[l, 0] = ox[l]
            out_ref[l, 1] = oc[l]
    return pl.pallas_call(body, name="ada_fwd", out_shape=jax.ShapeDtypeStruct((2, 2, N_DEV, cols), F32),
                          compiler_params=_cp(0))(c_all, cctx8, w, b)


def ada_bwd(c_all, cctx8, w, b, dmx, dmc):
    cols = w.shape[-1]

    def body(c_ref, cc_ref, w_ref, b_ref, dmx_ref, dmc_ref, dw_ref, dc_ref):
        fn = lambda cc, w0, w1: _ada_fn(c_ref[...], cc, [w0, w1], [b_ref[0], b_ref[1]])
        _, vjp = jax.vjp(fn, cc_ref[...], w_ref[0], w_ref[1])
        dcc, dw0, dw1 = vjp(([dmx_ref[0], dmx_ref[1]], [dmc_ref[0], dmc_ref[1]]))
        dw_ref[0] = dw0
        dw_ref[1] = dw1
        dc_ref[...] = jnp.sum(dcc, axis=0, keepdims=True)
    return pl.pallas_call(body, name="ada_bwd", out_shape=[jax.ShapeDtypeStruct((2, D, cols), F32), jax.ShapeDtypeStruct((1, D), F32)],
                          compiler_params=_cp(0))(c_all, cctx8, w, b, dmx, dmc)


def adamw(w, m, v, gparts, name):
    r, c = w.shape
    p = gparts.shape[0]
    rt = r
    while rt % 16 == 0 and (p + 7) * rt * c * 4 * 2 > 24 * 2 ** 20:
        rt //= 2

    def body(w_ref, m_ref, v_ref, g_ref, go_ref, d_ref, mo_ref, vo_ref):
        g = g_ref[0].astype(F32)
        for k in range(1, p):
            g = g + g_ref[k].astype(F32)
        m2 = ADAM_B1 * m_ref[...] + (1.0 - ADAM_B1) * g
        v2 = ADAM_B2 * v_ref[...] + (1.0 - ADAM_B2) * (g * g)
        m_hat = m2 / (1.0 - ADAM_B1 ** ADAM_STEP)
        v_hat = v2 / (1.0 - ADAM_B2 ** ADAM_STEP)
        go_ref[...] = g
        d_ref[...] = -ADAM_LR * (m_hat / (jnp.sqrt(v_hat) + ADAM_EPS) + ADAM_WD * w_ref[...])
        mo_ref[...] = m2
        vo_ref[...] = v2

    tile = pl.BlockSpec((rt, c), lambda i: (i, 0))
    return pl.pallas_call(
        body, name=name, grid=(r // rt,),
        in_specs=[tile, tile, tile, pl.BlockSpec((p, rt, c), lambda i: (0, i, 0))], out_specs=[tile] * 4,
        out_shape=[jax.ShapeDtypeStruct((r, c), F32)] * 4, compiler_params=_cp(1))(w, m, v, gparts)


def _me():
    x, y, c = lax.axis_index("x"), lax.axis_index("y"), lax.axis_index("c")
    return x, y, c, 4 * x + 2 * y + c


def _peer(x, y, c, p):
    fx, fy, fc = (p >> 2) & 1, (p >> 1) & 1, p & 1
    return (1 - x if fx else x, 1 - y if fy else y, 1 - c if fc else c)


def all_gather(arrs, name, after=None):
    n = len(arrs)
    extra = [] if after is None else list(after) if isinstance(after, (list, tuple)) else [after]

    def body(*refs):
        ins, outs = refs[:n], refs[n + len(extra):2 * n + len(extra)]
        send, recv, local = refs[2 * n + len(extra):]
        x, y, c, me = _me()
        copies = []
        for a in range(n):
            lc = pltpu.make_async_copy(ins[a], outs[a].at[me], local.at[a])
            lc.start()
            copies.append(lc)
            for p in range(1, N_DEV):
                cp = pltpu.make_async_remote_copy(src_ref=ins[a], dst_ref=outs[a].at[me], send_sem=send.at[a, p - 1],
                                                  recv_sem=recv.at[a, p - 1], device_id=_peer(x, y, c, p),
                                                  device_id_type=pl.DeviceIdType.MESH)
                cp.start()
                copies.append(cp)
        for cp in copies:
            cp.wait()

    return pl.pallas_call(
        body, name=name, in_specs=[ANY] * (n + len(extra)), out_specs=[ANY] * n,
        out_shape=[jax.ShapeDtypeStruct((N_DEV,) + a.shape, a.dtype) for a in arrs],
        scratch_shapes=[pltpu.SemaphoreType.DMA((n, N_DEV - 1)), pltpu.SemaphoreType.DMA((n, N_DEV - 1)),
                        pltpu.SemaphoreType.DMA((n,))])(*arrs, *extra)


HBM = pl.BlockSpec(memory_space=pltpu.HBM)
SEM = pl.BlockSpec(memory_space=pltpu.SEMAPHORE)


def _in_hbm(a):
    return pltpu.with_memory_space_constraint(a, pltpu.HBM)


ALL_PEERS = tuple(range(1, N_DEV))
SAME_CORE_AND_SIBLING = (1, 2, 4, 6)
OTHER_CHIPS = (2, 4, 6)


def _exchange_refs(srcs, lands, layer, scatter, a, x, y, c, p, forward=False):
    me = 4 * x + 2 * y + c
    px, py, pc = _peer(x, y, c, p) if p else (x, y, c)
    if forward and p:
        slot = lands[a].at[4 * px + 2 * py + pc]
        return slot, slot, _peer(x, y, c, 1)
    dst = lands[a].at[me] if layer is None else lands[a].at[me, layer]
    src = srcs[a].at[4 * px + 2 * py + pc] if scatter else dst
    return src, dst, (px, py, pc)


def exchange_start(srcs, lands, layer, scatter, name, after=None, peers=ALL_PEERS, forward=False):
    n, ns = len(lands), len(srcs)
    extra = [] if after is None else [after]

    def body(*refs):
        ins, lz = refs[:ns], refs[ns:ns + n]
        send, recv = refs[ns + n + len(extra)], refs[ns + n + len(extra) + 1]
        token = refs[-1]
        x, y, c, _ = _me()
        for a in range(n):
            for p in peers:
                src, dst, peer = _exchange_refs(ins, lz, layer, scatter, a, x, y, c, p, forward)
                k = a * (N_DEV - 1) + p - 1
                pltpu.make_async_remote_copy(src_ref=src, dst_ref=dst, send_sem=send.at[k], recv_sem=recv.at[k],
                                             device_id=peer, device_id_type=pl.DeviceIdType.MESH).start()
        token[...] = jnp.zeros_like(token)

    thru = [pltpu.HBM(a.shape, a.dtype) for a in list(srcs) + list(lands)]
    out = pl.pallas_call(
        body, name=name, in_specs=[HBM] * (ns + n) + [ANY] * len(extra),
        out_specs=[SEM, SEM] + [HBM] * (ns + n) + [pl.BlockSpec(memory_space=pltpu.VMEM)],
        out_shape=[pltpu.SemaphoreType.DMA((n * (N_DEV - 1),)), pltpu.SemaphoreType.DMA((n * (N_DEV - 1),))] + thru
        + [jax.ShapeDtypeStruct((8, 128), F32)],
        input_output_aliases={i: 2 + i for i in range(ns + n)},
        compiler_params=pltpu.CompilerParams(has_side_effects=pltpu.SideEffectType.DATAFLOW_SIDE_EFFECTING),
    )(*[_in_hbm(a) for a in list(srcs) + list(lands)], *extra)
    return out[0], out[1], out[2:2 + ns], out[2 + ns:2 + ns + n], out[-1]


def exchange_wait(send, recv, srcs, lands, layer, scatter, after, name, peers=ALL_PEERS):
    n, ns = len(lands), len(srcs)

    def body(*refs):
        ins, lz = refs[:ns], refs[ns:ns + n]
        send_ref, recv_ref = refs[ns + n], refs[ns + n + 1]
        x, y, c, _ = _me()
        for a in range(n):
            for p in peers:
                src, dst, peer = _exchange_refs(ins, lz, layer, scatter, a, x, y, c, 0)
                k = a * (N_DEV - 1) + p - 1
                cp = pltpu.make_async_remote_copy(src_ref=src, dst_ref=dst, send_sem=send_ref.at[k],
                                                  recv_sem=recv_ref.at[k], device_id=peer,
                                                  device_id_type=pl.DeviceIdType.MESH)
                cp.wait_send()
                cp.wait_recv()

    thru = [pltpu.HBM(a.shape, a.dtype) for a in list(srcs) + list(lands)]
    out = pl.pallas_call(
        body, name=name, in_specs=[HBM] * (ns + n) + [SEM, SEM, ANY], out_specs=[HBM] * (ns + n), out_shape=thru,
        input_output_aliases={i: i for i in range(ns + n)},
        compiler_params=pltpu.CompilerParams(has_side_effects=pltpu.SideEffectType.DATAFLOW_SIDE_EFFECTING),
    )(*srcs, *lands, send, recv, after)
    return out[ns:]


def place_own(src, land, me, layer, scatter, name, src_layer=None):
    create = isinstance(land, jax.ShapeDtypeStruct)
    r, c = src.shape[-2:]
    rt = r
    while rt % 32 == 0 and rt * c * 4 > 2 ** 21:
        rt //= 2

    def body(me_ref, src_ref, *rest):
        out_ref = rest[-1]
        out_ref[...] = src_ref[...].reshape(out_ref.shape).astype(out_ref.dtype)

    src_spec = (pl.BlockSpec((1, rt, c), lambda i, m: (m[0], i, 0)) if scatter else
                pl.BlockSpec((rt, c), lambda i, m: (i, 0)) if src_layer is None else
                pl.BlockSpec((1, rt, c), lambda i, m: (src_layer, i, 0)))
    out_spec = (pl.BlockSpec((1, rt, c), lambda i, m: (m[0], i, 0)) if layer is None
                else pl.BlockSpec((1, 1, rt, c), lambda i, m: (m[0], layer, i, 0)))
    grid_spec = pltpu.PrefetchScalarGridSpec(num_scalar_prefetch=1, grid=(r // rt,),
                                             in_specs=[src_spec] + ([] if create else [ANY]), out_specs=out_spec)
    return pl.pallas_call(body, name=name, grid_spec=grid_spec, out_shape=jax.ShapeDtypeStruct(land.shape, land.dtype),
                          input_output_aliases={} if create else {2: 0}, compiler_params=_cp(1),
                          )(*((me, src) if create else (me, src, land)))


def _scan_constants():
    r = lax.broadcasted_iota(jnp.int32, (CH, CH), 0)
    s = lax.broadcasted_iota(jnp.int32, (CH, CH), 1)
    lower = (s <= r).astype(F32)
    t = jnp.arange(CH)[:, None]
    mc = jnp.stack([lower, lower.T])
    mref = jnp.stack([(t <= CH // 2 - 1).astype(F32), (t >= CH // 2).astype(F32)])
    return mc, jnp.stack([lower.T, lower]), mref


def local_step(x, ctx, target, mod, lb, w, fetch=None, publish=None, small_ready=None, small_early=None):
    kept = {}

    def keep(l, part, grads):
        kept[(l, part)] = grads
        return 0.0

    fetch = fetch or (lambda l, part, after: w)
    publish = publish or keep
    n_layers = len(mod)
    mc, mtc, mrefc = _scan_constants()
    xs = jnp.concatenate([ctx, x], axis=0)
    saved, big = [], []
    for l in range(n_layers):
        wl = dict(fetch(l, "in", xs))
        parts, ht, iv = in_proj_fwd(xs, mod[l], w["nw1"][l], wl["win"][l])
        o, ck = hgrn_fwd(parts, iv, lb[l], mc, mtc, mrefc)
        wl.update(fetch(l, "rest", o))
        last = l == n_layers - 1
        x1, pa, pb, ym, yat, ybt, mt = mixer_fwd(xs, parts, o, mod[l], w["lnw"][l], w["lnb"][l], w["sw"][l], w["sb"][l],
                                                 w["hnw"][l], wl["wa"][l], wl["wb"][l], wl["wo"][l], last)
        av, h2t = ffn_up_fwd(x1, mod[l], w["nw2"][l], wl["wup"][l], last)
        x2, ac, y, z = ffn_down_fwd(x1, av, mod[l], w["cw"][l], w["cb"][l], wl["wd"][l], last)
        saved.append((xs, parts, iv, o, ck, x1, av, ac, y, z, ht, h2t, pa, pb, ym, yat, ybt, mt))
        big.append(wl)
        xs = x2
    loss, dx, dfw = loss_fwd_bwd(xs, target, w["fw"])
    g = {k: [None] * n_layers for k in ("nw1", "nw2", "lnw", "lnb", "sw", "sb", "hnw", "cw", "cb")}
    g["fw"] = dfw
    dmod, dlb = [None] * n_layers, [None] * n_layers
    tok = 0.0
    for l in reversed(range(n_layers)):
        x0, parts, iv, o, ck, x1, av, ac, y, z, ht, h2t, pa, pb, ym, yat, ybt, mt = saved[l]
        wl = big[l]
        last = l == n_layers - 1
        dav, dac, dout, dg2 = ffn_down_bwd(dx, ac, av, y, mod[l] + tok, wl["wd"][l], last)
        dwd = weight_grad_rows(z, dout, "ffn_down_bwd_w")
        dav, g["cw"][l], g["cb"][l] = conv_bwd(dav, dac, av, w["cw"][l], last)
        dx1, g["nw2"][l], dmod2 = ffn_up_bwd_x(dx, x1, dav, mod[l], w["nw2"][l], wl["wup"][l], last)
        dwup = weight_grad(h2t, dav, FF_SLOT, "ffn_up_bwd_w")
        tok = publish(l, "ffn", {"wd": dwd, "wup": dwup})
        (dparts, do, dy, dpa, dpb, g["lnw"][l], g["lnb"][l], g["sw"][l], g["sb"][l], g["hnw"][l],
         dg1) = mixer_bwd(dx1, parts, o, pa, pb, ym, mod[l] + tok, w["lnw"][l], w["lnb"][l], w["sw"][l], w["sb"][l],
                          w["hnw"][l], wl["wa"][l], wl["wb"][l], wl["wo"][l], last)
        tok = publish(l, "mix", {"wa": weight_grad(yat, dpa, D, "mixer_bwd_wa"), "wb": weight_grad(ybt, dpb, D, "mixer_bwd_wb"),
                                 "wo": weight_grad(mt, dy, D, "mixer_bwd_wo")})
        if l == 0 and small_early:
            dmod[0] = jnp.concatenate([jnp.zeros((2, 2, 1, D), F32), dg1, dmod2, dg2], axis=1)
            tok = tok + small_early(loss[0, 0], g, dmod, dlb)
        dq, df, di, dlb_f = hgrn_bwd(0, parts, iv, lb[l] + tok, mc, mtc, mrefc, ck, do)
        dparts, dlb_b = hgrn_bwd(1, parts, iv, lb[l], mc, mtc, mrefc, ck, do, (dq, df, di), dparts)
        dlb[l] = jnp.concatenate([dlb_f, dlb_b], axis=0)
        tok = publish(l, "in", {"win": weight_grad(ht, dparts, IN_SLOT, "in_proj_bwd_w")})
        dx, g["nw1"][l], dmod1 = in_proj_bwd_x(dx1, x0, dparts, mod[l], w["nw1"][l], wl["win"][l], after=tok,
                                               latent_only=l == 0)
        dmod[l] = jnp.concatenate([dmod1, dg1, dmod2, dg2], axis=1)
    done = small_ready(loss[0, 0], g, dmod, dlb) if small_ready else 0.0
    for (l, part), grads in kept.items():
        for k, v in grads.items():
            g.setdefault(k, [None] * n_layers)[l] = v
    return loss[0, 0], dx, g, dmod, dlb, done


ROW = 1024
REPLICATED = ("norm1_w", "sgu_ln_w", "sgu_ln_b", "sgu_w", "sgu_b", "hgrn_lower_bounds", "hgrn_norm_w", "norm2_w",
              "ffn_conv_b", "final_norm_w")
WEIGHT_ORDER = ("c_ctx", "ada_w", "ada_b", "norm1_w", "w_in", "sgu_ln_w", "sgu_ln_b", "sgu_w", "sgu_b", "hgrn_lower_bounds",
                "hgrn_norm_w", "w_branch_a", "w_branch_b", "w_out", "norm2_w", "ffn_w_up", "ffn_conv_w", "ffn_conv_b",
                "ffn_w_down", "final_norm_w")


def _rows_of(n):
    return -(-n // (8 * ROW)) * 8


def _pack(arrs, total_rows=None):
    parts = []
    for a in arrs:
        flat = a.reshape(-1).astype(F32)
        rows = _rows_of(flat.shape[0])
        parts.append(jnp.pad(flat, (0, rows * ROW - flat.shape[0])).reshape(rows, ROW))
    have = sum(p.shape[0] for p in parts)
    if total_rows is not None and total_rows > have:
        parts.append(jnp.zeros((total_rows - have, ROW), F32))
    return jnp.concatenate(parts, axis=0)


def _unpack(packed, shapes):
    lead = packed.shape[:-2]
    out, r0 = [], 0
    for s in shapes:
        n = math.prod(s)
        rows = _rows_of(n)
        out.append(packed[..., r0:r0 + rows, :].reshape(lead + (rows * ROW,))[..., :n].reshape(lead + tuple(s)))
        r0 += rows
    return out


def kernel(x, c, ctx, c_ctx, ada_w, ada_b, norm1_w, w_in, sgu_ln_w, sgu_ln_b, sgu_w, sgu_b, hgrn_lower_bounds, hgrn_norm_w, w_branch_a, w_branch_b, w_out, norm2_w, ffn_w_up, ffn_conv_w, ffn_conv_b, ffn_w_down, final_norm_w, loss_target, m_c_ctx, m_ada_w, m_ada_b, m_norm1_w, m_w_in, m_sgu_ln_w, m_sgu_ln_b, m_sgu_w, m_sgu_b, m_hgrn_lower_bounds, m_hgrn_norm_w, m_w_branch_a, m_w_branch_b, m_w_out, m_norm2_w, m_ffn_w_up, m_ffn_conv_w, m_ffn_conv_b, m_ffn_w_down, m_final_norm_w, v_c_ctx, v_ada_w, v_ada_b, v_norm1_w, v_w_in, v_sgu_ln_w, v_sgu_ln_b, v_sgu_w, v_sgu_b, v_hgrn_lower_bounds, v_hgrn_norm_w, v_w_branch_a, v_w_branch_b, v_w_out, v_norm2_w, v_ffn_w_up, v_ffn_conv_w, v_ffn_conv_b, v_ffn_w_down, v_final_norm_w):
    wts = dict(c_ctx=c_ctx, ada_w=ada_w, ada_b=ada_b, norm1_w=norm1_w, w_in=w_in, sgu_ln_w=sgu_ln_w, sgu_ln_b=sgu_ln_b,
               sgu_w=sgu_w, sgu_b=sgu_b, hgrn_lower_bounds=hgrn_lower_bounds, hgrn_norm_w=hgrn_norm_w, w_branch_a=w_branch_a,
               w_branch_b=w_branch_b, w_out=w_out, norm2_w=norm2_w, ffn_w_up=ffn_w_up, ffn_conv_w=ffn_conv_w,
               ffn_conv_b=ffn_conv_b, ffn_w_down=ffn_w_down, final_norm_w=final_norm_w)
    mom1 = dict(c_ctx=m_c_ctx, ada_w=m_ada_w, ada_b=m_ada_b, norm1_w=m_norm1_w, w_in=m_w_in, sgu_ln_w=m_sgu_ln_w,
                sgu_ln_b=m_sgu_ln_b, sgu_w=m_sgu_w, sgu_b=m_sgu_b, hgrn_lower_bounds=m_hgrn_lower_bounds,
                hgrn_norm_w=m_hgrn_norm_w, w_branch_a=m_w_branch_a, w_branch_b=m_w_branch_b, w_out=m_w_out, norm2_w=m_norm2_w,
                ffn_w_up=m_ffn_w_up, ffn_conv_w=m_ffn_conv_w, ffn_conv_b=m_ffn_conv_b, ffn_w_down=m_ffn_w_down,
                final_norm_w=m_final_norm_w)
    mom2 = dict(c_ctx=v_c_ctx, ada_w=v_ada_w, ada_b=v_ada_b, norm1_w=v_norm1_w, w_in=v_w_in, sgu_ln_w=v_sgu_ln_w,
                sgu_ln_b=v_sgu_ln_b, sgu_w=v_sgu_w, sgu_b=v_sgu_b, hgrn_lower_bounds=v_hgrn_lower_bounds,
                hgrn_norm_w=v_hgrn_norm_w, w_branch_a=v_w_branch_a, w_branch_b=v_w_branch_b, w_out=v_w_out, norm2_w=v_norm2_w,
                ffn_w_up=v_ffn_w_up, ffn_conv_w=v_ffn_conv_w, ffn_conv_b=v_ffn_conv_b, ffn_w_down=v_ffn_w_down,
                final_norm_w=v_final_norm_w)
    n_layers = w_in.shape[0]
    layers = range(n_layers)
    me = 4 * lax.axis_index("x") + 2 * lax.axis_index("y") + lax.axis_index("c")
    ada_cols = ada_w.shape[-1]

    big = ("w_in", "ffn_w_up", "w_branch_a", "w_branch_b", "w_out", "ffn_w_down")
    short = {"w_in": "win", "ffn_w_up": "wup", "w_branch_a": "wa", "w_branch_b": "wb", "w_out": "wo", "ffn_w_down": "wd"}
    me1 = me.reshape(1).astype(jnp.int32)
    groups = [[("w_in", 0)], [(k, 0) for k in big[1:]], [("w_in", 1)], [(k, 1) for k in big[1:]]]
    in_flight, started = [], 0.0

    def own_slots(n):
        return [place_own(wts[k], jax.ShapeDtypeStruct((N_DEV,) + wts[k].shape[1:], BF16), me1, None, False,
                          f"gather_own_{short[k]}_{l}", src_layer=l) for k, l in groups[n]]

    def start_group(n, lands, after):
        in_flight.append(exchange_start([], lands, None, False, f"gather_weights_start_{n}", after=after,
                                        peers=SAME_CORE_AND_SIBLING if n == 0 else ALL_PEERS))
        return in_flight[-1][-1]

    (c_all,) = all_gather([c], "gather_c")
    c_all = c_all.reshape(N_DEV, D)
    token = start_group(0, own_slots(0), c_all)
    later = [own_slots(n) for n in range(1, len(groups))]
    cctx8 = jnp.broadcast_to(c_ctx[None, :], (N_DEV, D))
    ada_b_cols = lax.dynamic_slice_in_dim(ada_b, me * ada_cols, ada_cols, axis=1)[:, None, :]
    mod_cols = ada_fwd(c_all, cctx8, ada_w, ada_b_cols)
    xs = jnp.concatenate([ctx[0], x[0]], axis=0)
    lb1 = lower_bounds(hgrn_lower_bounds)
    mod_all, conv_all = all_gather([mod_cols, ffn_conv_w.reshape(n_layers, 9, -1)], "gather_mod_conv",
                                   after=[token, xs, lb1] + [a for lands in later for a in lands])
    conv_full = [conv_all[:, l].transpose(1, 0, 2).reshape(9, N_FFK, FF_SLOT).transpose(1, 0, 2) for l in layers]
    for n in range(1, len(groups)):
        token = start_group(n, later[n - 1], mod_all if n == 1 else token)
    for started_group in in_flight:
        started = started + started_group[-1][0, 0]

    def as_used(k, a):
        return a if k in ("w_in", "ffn_w_up") else a.reshape(N_FFK, FF_SLOT, D) if k == "ffn_w_down" else a.reshape(D, D)

    arrived = {}

    def fetch(l, part, after):
        n = {(0, "in"): 0, (0, "rest"): 1, (1, "in"): 2, (1, "rest"): 3}.get((l, part))
        if n is not None:
            send, recv, _, lands, _ = in_flight[n]
            first = n == 0
            got = exchange_wait(send, recv, [], lands, None, False, after, f"gather_weights_wait_{n}",
                                peers=SAME_CORE_AND_SIBLING if first else ALL_PEERS)
            if first:
                send, recv, _, lands, _ = exchange_start([], got, None, False, "gather_weights_pass_on", peers=OTHER_CHIPS,
                                                         forward=True)
                got = exchange_wait(send, recv, [], lands, None, False, after, "gather_weights_passed_on", peers=OTHER_CHIPS)
            for (k, ll), a in zip(groups[n], got):
                arrived.setdefault(short[k], [None] * n_layers)[ll] = as_used(k, a)
        return arrived

    mod_x = lax.dynamic_index_in_dim(mod_all[:, :, 0], me, axis=2, keepdims=False)
    mod_c = mod_all[:, :, 1, 0]
    mod = [jnp.stack([mod_c[:, l].reshape(6, 1, D), mod_x[:, l].reshape(6, 1, D)]) for l in layers]
    mod[0] = mod[0] + started

    lb = [jnp.zeros((2, 1, D), F32), lb1.reshape(2, 1, D)]

    w = {
        "nw1": [norm1_w[l][None] for l in layers], "nw2": [norm2_w[l][None] for l in layers],
        "lnw": [sgu_ln_w[l][None] for l in layers], "lnb": [sgu_ln_b[l][None] for l in layers],
        "sw": [sgu_w[l] for l in layers], "sb": [sgu_b[l][:, :, None] for l in layers],
        "hnw": [hgrn_norm_w[l][None] for l in layers], "cw": conv_full,
        "cb": [ffn_conv_b[l].reshape(N_FFK, 1, FF_SLOT) for l in layers], "fw": final_norm_w[None],
    }
    long = {v: k for k, v in short.items()}
    landing, sent = {}, []

    def publish(l, part, grads):
        keys = [long[k] for k in grads]
        slots = [a.reshape((N_DEV, -1, a.shape[-1])) for a in grads.values()]
        zones = [place_own(s, landing.get(k, jax.ShapeDtypeStruct((N_DEV, n_layers) + s.shape[1:], s.dtype)), me1, l, True,
                           f"scatter_own_{short[k]}_{l}") for k, s in zip(keys, slots)]
        send, recv, srcs, zones, token = exchange_start(slots, zones, l, True, f"scatter_grads_start_{part}_{l}")
        landing.update(zip(keys, zones))
        sent.append((keys, l, part, send, recv, srcs, token))
        return token[0, 0]

    out = {}
    flat2 = lambda a: a.reshape(-1, a.shape[-1])

    def finish(part, after):
        done = []
        for keys, l, p, send, recv, srcs, _ in sent:
            if p == part:
                zones = exchange_wait(send, recv, srcs, [landing[k] for k in keys], l, True, after,
                                      f"scatter_grads_wait_{part}_{l}")
                landing.update(zip(keys, zones))
                done = keys
        for k in done:
            r = landing[k]
            res = adamw(flat2(wts[k]), flat2(mom1[k]), flat2(mom2[k]), r.reshape(N_DEV, -1, r.shape[-1]), "adamw_" + k)
            out[k] = tuple(a.reshape(wts[k].shape) for a in res)

    rep_rows = -(-sum(_rows_of(wts[k].size) for k in REPLICATED) // 64) * 64
    conv_rows = _rows_of(n_layers * 9 * D_FF)
    dmod_rows = _rows_of(n_layers * 6 * D)
    early = {}

    def small_early(loss_part, g, dmod, dlb):
        d_hlb = lower_bounds_bwd(hgrn_lower_bounds, dlb[1].reshape(1, 2 * D))
        st = lambda k: jnp.stack([jnp.zeros((1, D), F32) if a is None else a for a in g[k]])
        rep_grads = {"norm1_w": st("nw1"), "sgu_ln_w": st("lnw"), "sgu_ln_b": st("lnb"), "sgu_w": st("sw"), "sgu_b": st("sb"),
                     "hgrn_lower_bounds": d_hlb, "hgrn_norm_w": st("hnw"), "norm2_w": st("nw2"), "ffn_conv_b": st("cb"),
                     "final_norm_w": g["fw"]}
        d_conv = jnp.stack([g["cw"][l].transpose(1, 0, 2).reshape(9, D_FF) for l in layers])
        dmod_x = jnp.stack([dmod[l][1].reshape(6 * D) for l in layers])
        dmod_c = jnp.stack([dmod[l][0].reshape(6 * D) for l in layers])
        small = jnp.concatenate([_pack([rep_grads[k] for k in REPLICATED], rep_rows),
                                 _pack([d_conv, dmod_x, dmod_c, loss_part.reshape(1)])], axis=0)
        zone = place_own(small, jax.ShapeDtypeStruct((N_DEV,) + small.shape, F32), me1, None, False, "gather_small_own")
        early["send"], early["recv"], _, early["zones"], token = exchange_start([], [zone], None, False, "gather_small_start")
        return token[0, 0]

    def small_ready(loss_part, g, dmod, dlb):
        late = _pack([g["nw1"][0], dmod[0][1, 0:2], dmod[0][0, 0:2]])
        for part in ("ffn", "mix"):
            finish(part, late)
        (late_all,) = all_gather([late], "gather_small_late", after=out["w_out"][0])
        (small_all,) = exchange_wait(early["send"], early["recv"], [], early["zones"], None, False, late_all,
                                     "gather_small_wait")
        at_x = rep_rows + conv_rows
        small_all = small_all.at[:, 0:1].set(late_all[:, 0:1])
        small_all = small_all.at[:, at_x:at_x + 2].set(late_all[:, 8:10])
        small_all = small_all.at[:, at_x + dmod_rows:at_x + dmod_rows + 2].set(late_all[:, 16:18])
        d_conv_shape, dmod_shape = (n_layers, 9, D_FF), (n_layers, 6 * D)
        conv_g, dmx_all, dmc_all, loss_all = _unpack(small_all[:, rep_rows:], [d_conv_shape, dmod_shape, dmod_shape, (1,)])
        out["loss"] = functools.reduce(lambda a, b: a + b, [loss_all[k, 0] for k in range(N_DEV)])

        rep = adamw(_pack([wts[k] for k in REPLICATED], rep_rows), _pack([mom1[k] for k in REPLICATED], rep_rows),
                    _pack([mom2[k] for k in REPLICATED], rep_rows), small_all, "adamw_replicated")
        rep = [_unpack(r, [wts[k].shape for k in REPLICATED]) for r in rep]
        for n, k in enumerate(REPLICATED):
            out[k] = tuple(r[n] for r in rep)

        conv_mine = lax.dynamic_index_in_dim(conv_g.reshape(N_DEV, n_layers, 9, N_DEV, -1), me, axis=3, keepdims=False)
        res = adamw(flat2(ffn_conv_w), flat2(m_ffn_conv_w), flat2(v_ffn_conv_w),
                    conv_mine.reshape(N_DEV, -1, conv_mine.shape[-1]), "adamw_conv_w")
        out["ffn_conv_w"] = tuple(r.reshape(ffn_conv_w.shape) for r in res)

        out["ada_b"] = tuple(adamw(ada_b, m_ada_b, v_ada_b, jnp.concatenate([dmx_all, dmc_all], axis=0), "adamw_ada_b"))

        cols_of = lambda a: lax.dynamic_slice_in_dim(a, me * ada_cols, ada_cols, axis=2).transpose(1, 0, 2)
        d_ada_w, d_cctx = ada_bwd(c_all, cctx8, ada_w, ada_b_cols, cols_of(dmx_all), cols_of(dmc_all))
        res = adamw(flat2(ada_w), flat2(m_ada_w), flat2(v_ada_w), flat2(d_ada_w)[None], "adamw_ada_w")
        out["ada_w"] = tuple(r.reshape(ada_w.shape) for r in res)
        (d_cctx_all,) = all_gather([d_cctx], "gather_c_ctx_grad")
        res = adamw(c_ctx[None], m_c_ctx[None], v_c_ctx[None], d_cctx_all, "adamw_c_ctx")
        out["c_ctx"] = tuple(r[0] for r in res)
        return d_cctx_all

    _, grad_x, _, _, _, small_done = local_step(x[0], ctx[0], loss_target[0], mod, lb, w, fetch, publish, small_ready,
                                                small_early)
    loss = out["loss"]

    finish("in", small_done)
    return (loss, grad_x[None]) + tuple(out[k][n] for n in range(4) for k in WEIGHT_ORDER)
```

```python
import functools
import math

import jax
import jax.numpy as jnp
from jax import lax
from jax.experimental import pallas as pl
from jax.experimental.pallas import tpu as pltpu

F32 = jnp.float32
BF16 = jnp.bfloat16
HIGHEST = lax.Precision.HIGHEST

N_DEV = 8
AXES = ("x", "y", "c")
D = 1024
CTX = 256
TM = 256
CH = 64
SGU_CH = 128
HEADS = 8
HD = 128
GRID_W = 64
D_IN = 9 * D
IN_SLOT = D_IN // N_DEV
D_FF = 2816
FF_SLOT = 2 * D_FF // N_DEV
N_FFK = D_FF // FF_SLOT
RMS_EPS = 1e-6
LN_EPS = 1e-5
ADAM_LR, ADAM_B1, ADAM_B2, ADAM_EPS, ADAM_WD, ADAM_STEP = 0.001, 0.9, 0.999, 1e-08, 0.01, 10
VMEM_LIMIT_V7X = 56 * 2 ** 20
GRAD_WIRE = jnp.bfloat16

VMEM_WHOLE = pl.BlockSpec(memory_space=pltpu.VMEM)
ANY = pl.BlockSpec(memory_space=pl.ANY)


def _cp(n_axes):
    return pltpu.CompilerParams(dimension_semantics=("arbitrary",) * n_axes, vmem_limit_bytes=VMEM_LIMIT_V7X)


def _dot(a, b, dims):
    return lax.dot_general(a.astype(BF16), b.astype(BF16), (dims, ((), ())), preferred_element_type=F32)


@jax.custom_vjp
def mm(a, b):
    return _dot(a, b, ((1,), (0,)))


mm.defvjp(lambda a, b: (mm(a, b), (a, b)),
          lambda r, g: (_dot(g, r[1], ((1,), (1,))).astype(r[0].dtype), _dot(r[0], g, ((0,), (0,))).astype(r[1].dtype)))


@jax.custom_vjp
def mm_nt(a, b):
    return _dot(a, b, ((1,), (1,)))


mm_nt.defvjp(lambda a, b: (mm_nt(a, b), (a, b)),
             lambda r, g: (_dot(g, r[1], ((1,), (0,))).astype(r[0].dtype), _dot(g, r[0], ((0,), (0,))).astype(r[1].dtype)))


@jax.custom_vjp
def mm_tn(a, b):
    return _dot(a, b, ((0,), (0,)))


mm_tn.defvjp(lambda a, b: (mm_tn(a, b), (a, b)),
             lambda r, g: (_dot(r[1], g, ((1,), (1,))).astype(r[0].dtype), _dot(r[0], g, ((1,), (0,))).astype(r[1].dtype)))


def _tri_dot(m, g):
    hi = g.astype(BF16)
    low = (g - hi.astype(F32)).astype(BF16)
    n = g.shape[1]
    out = jnp.dot(m.astype(BF16), jnp.concatenate([hi, low], axis=1), preferred_element_type=F32)
    return out[:, :n] + out[:, n:]


@jax.custom_vjp
def _cum(m, mt, g):
    return _tri_dot(m, g)


_cum.defvjp(lambda m, mt, g: (_cum(m, mt, g), (m, mt)),
            lambda r, d: (jnp.zeros_like(r[0]), jnp.zeros_like(r[1]), _tri_dot(r[1], d)))


def _silu(x):
    return x * jax.nn.sigmoid(x)


def _gelu(x):
    return 0.5 * x * (1.0 + jnp.tanh(math.sqrt(2.0 / math.pi) * (x + 0.044715 * (x * x * x))))


def _rms(x, w):
    return x * lax.rsqrt(jnp.mean(x * x, axis=-1, keepdims=True) + RMS_EPS) * w


def _norm_mod(x, w, shift, scale):
    return _rms(x, w) * (1.0 + scale) + shift


def _hsl(h):
    return slice(h * HD, (h + 1) * HD)


def _hgrn_chunk(st, qz, fz, iv, lb, m, mt, mref):
    hs = range(HEADS)
    keep = [1.0 - lb[h] for h in hs]
    sg = [jax.nn.sigmoid(fz[h]) for h in hs]
    g = [jnp.log(lb[h] + keep[h] * sg[h]) for h in hs]
    k = [keep[h] * (1.0 - sg[h]) for h in hs]
    q = [_silu(qz[h]) for h in hs]
    b = [_cum(m, mt, g[h]) for h in hs]
    ref = [jnp.sum(mref * g[h], axis=0, keepdims=True) for h in hs]
    last = [jnp.sum(g[h], axis=0, keepdims=True) for h in hs]
    qa = [q[h] * jnp.exp(b[h] - ref[h]) for h in hs]
    ka = [k[h] * jnp.exp(ref[h] - b[h]) for h in hs]
    scores = [jnp.where(m > 0.5, mm_nt(qa[h], ka[h]), 0.0) for h in hs]
    inter = [mm_nt(qa[h] * jnp.exp(ref[h]), st[h]) for h in hs]
    kv = [mm_tn(iv[h], ka[h] * jnp.exp(last[h] - ref[h])) for h in hs]
    outs = [mm(scores[h], iv[h]) + inter[h] for h in hs]
    news = [jnp.exp(last[h]) * st[h] + kv[h] for h in hs]
    return outs, news


def _sgu_fn(ub, vb, lnw, lnb, sw, sb):
    gv = [_gelu(v) for v in vb]
    mu = sum(jnp.sum(t, axis=-1, keepdims=True) for t in gv) / D
    var = sum(jnp.sum((t - mu) * (t - mu), axis=-1, keepdims=True) for t in gv) / D
    inv = lax.rsqrt(var + LN_EPS)
    cols = []
    for g in range(HEADS):
        vn = (gv[g] - mu) * inv * lnw[g] + lnb[g]
        cols.append(_gelu(ub[g]) * (mm(sw[g], vn) + sb[g]))
    return jnp.concatenate(cols, axis=1)


def _readout_fn(ob, og, hnw):
    r = [o * lax.rsqrt(jnp.mean(o * o, axis=-1, keepdims=True) + RMS_EPS) * hnw for o in ob]
    return jnp.concatenate(r, axis=1) * _silu(og)


def _glu_fn(ac, v):
    return _gelu(ac) * v


def _stream_row(tm):
    n_ctx = CTX // tm
    return lambda i: (jnp.where(i < n_ctx, 0, 1), 0, 0, 0)


def in_proj_fwd(x, mod, nw, wg):
    t = x.shape[0]

    def body(x_ref, mod_ref, nw_ref, w_ref, out_ref, ht_ref, iv_ref):
        h32 = _norm_mod(x_ref[...], nw_ref[...], mod_ref[0, 0], mod_ref[0, 1])
        ht_ref[...] = h32.T.astype(BF16)
        h = h32.astype(BF16)
        for j in range(N_DEV):
            out_ref[:, j * IN_SLOT:(j + 1) * IN_SLOT] = jnp.dot(h, w_ref[j], preferred_element_type=F32)
        iv_ref[...] = out_ref[:, 3 * D:4 * D].astype(BF16)

    return pl.pallas_call(
        body, name="in_proj_fwd", grid=(t // TM,),
        in_specs=[pl.BlockSpec((TM, D), lambda i: (i, 0)), pl.BlockSpec((1, 6, 1, D), _stream_row(TM)),
                  pl.BlockSpec((1, D), lambda i: (0, 0)), VMEM_WHOLE],
        out_specs=[pl.BlockSpec((TM, D_IN), lambda i: (i, 0)), pl.BlockSpec((D, TM), lambda i: (0, i)),
                   pl.BlockSpec((TM, D), lambda i: (i, 0))],
        out_shape=[jax.ShapeDtypeStruct((t, D_IN), F32), jax.ShapeDtypeStruct((D, t), BF16), jax.ShapeDtypeStruct((t, D), BF16)],
        compiler_params=_cp(1))(x, mod, nw, wg)


def _scan_chunk(nc):
    ncc = CTX // CH

    def chunk(d, s):
        bwd = jnp.where(s < ncc, ncc - 1 - s, nc + ncc - 1 - s)
        return jnp.where(d == 0, s, bwd)
    return chunk


def hgrn_fwd(parts, iv, lb, mc, mtc, mrefc):
    t = parts.shape[0]
    nc = t // CH
    chunk = _scan_chunk(nc)

    def body(q_ref, f_ref, i_ref, lb_ref, m_ref, mt_ref, mr_ref, o_ref, ck_ref, st):
        @pl.when(pl.program_id(1) == 0)
        def _():
            st[...] = jnp.zeros_like(st)
        ck_ref[0, 0] = st[...].astype(BF16)
        outs, news = _hgrn_chunk([st[h] for h in range(HEADS)], [q_ref[:, _hsl(h)] for h in range(HEADS)],
                                 [f_ref[:, _hsl(h)] for h in range(HEADS)], [i_ref[:, _hsl(h)] for h in range(HEADS)],
                                 [lb_ref[0, :, _hsl(h)] for h in range(HEADS)], m_ref[0], mt_ref[0], mr_ref[0])
        for h in range(HEADS):
            o_ref[0, :, _hsl(h)] = outs[h].astype(BF16)
            st[h] = news[h]

    const = lambda d, s: (d, 0, 0)
    return pl.pallas_call(
        body, name="hgrn_fwd", grid=(2, nc),
        in_specs=[pl.BlockSpec((CH, D), lambda d, s: (chunk(d, s), 0)), pl.BlockSpec((CH, D), lambda d, s: (chunk(d, s), 1 + d)),
                  pl.BlockSpec((CH, D), lambda d, s: (chunk(d, s), 0)), pl.BlockSpec((1, 1, D), const),
                  pl.BlockSpec((1, CH, CH), const), pl.BlockSpec((1, CH, CH), const), pl.BlockSpec((1, CH, 1), const)],
        out_specs=[pl.BlockSpec((1, CH, D), lambda d, s: (d, chunk(d, s), 0)),
                   pl.BlockSpec((1, 1, HEADS, HD, HD), lambda d, s: (d, s, 0, 0, 0))],
        out_shape=[jax.ShapeDtypeStruct((2, t, D), BF16), jax.ShapeDtypeStruct((2, nc, HEADS, HD, HD), BF16)],
        scratch_shapes=[pltpu.VMEM((HEADS, HD, HD), F32)], compiler_params=_cp(2))(parts, parts, iv, lb, mc, mtc, mrefc)


def _mixer_tile(rows, u_ref, v_ref, og_ref, o_ref, lnw_ref, lnb_ref, sw_ref, sb_ref, hnw_ref):
    n = (rows.stop - rows.start) // SGU_CH
    yas, vjps = [], []
    for c in range(n):
        r = slice(rows.start + c * SGU_CH, rows.start + (c + 1) * SGU_CH)
        ya, vjp_a = jax.vjp(_sgu_fn, [u_ref[r, _hsl(g)] for g in range(HEADS)], [v_ref[r, _hsl(g)] for g in range(HEADS)],
                            [lnw_ref[:, _hsl(g)] for g in range(HEADS)], [lnb_ref[:, _hsl(g)] for g in range(HEADS)],
                            [sw_ref[g] for g in range(HEADS)], [sb_ref[g] for g in range(HEADS)])
        yas.append(ya)
        vjps.append(vjp_a)
    yb, vjp_b = jax.vjp(_readout_fn, [o_ref[0, rows, _hsl(h)].astype(F32) + o_ref[1, rows, _hsl(h)].astype(F32)
                                      for h in range(HEADS)],
                        og_ref[rows, :], hnw_ref[...])
    return (yas[0] if n == 1 else jnp.concatenate(yas, axis=0)), yb, vjps, vjp_b


def _part_specs(tm, first, n):
    return [pl.BlockSpec((tm, D), functools.partial(lambda k, i: (i, k), first + k)) for k in range(n)]


def _unless_ctx(skip_ctx, is_ctx, zero_refs, work):
    if not skip_ctx:
        return work()

    @pl.when(is_ctx)
    def _():
        for r in zero_refs:
            r[...] = jnp.zeros_like(r)

    pl.when(jnp.logical_not(is_ctx))(work)


def mixer_fwd(x, parts, o, mod, lnw, lnb, sw, sb, hnw, wa, wb, wo, skip_ctx):
    t = x.shape[0]

    def body(x_ref, u_ref, v_ref, og_ref, ga_ref, gb_ref, o_ref, mod_ref, lnw_ref, lnb_ref, sw_ref, sb_ref, hnw_ref,
             wa_ref, wb_ref, wo_ref, out_ref, pa_ref, pb_ref, y_ref, yat_ref, ybt_ref, mt_ref):
        def work():
            ya, yb, _, _ = _mixer_tile(slice(0, TM), u_ref, v_ref, og_ref, o_ref, lnw_ref, lnb_ref, sw_ref, sb_ref, hnw_ref)
            pa, pb = mm(ya, wa_ref[...]), mm(yb, wb_ref[...])
            merged = jax.nn.sigmoid(ga_ref[...]) * pa + jax.nn.sigmoid(gb_ref[...]) * pb
            y = mm(merged, wo_ref[...])
            out_ref[...] = x_ref[...] + mod_ref[0, 2] * y
            pa_ref[...], pb_ref[...], y_ref[...] = pa.astype(BF16), pb.astype(BF16), y.astype(BF16)
            yat_ref[...], ybt_ref[...], mt_ref[...] = ya.T.astype(BF16), yb.T.astype(BF16), merged.T.astype(BF16)

        _unless_ctx(skip_ctx, pl.program_id(0) == 0, (out_ref, pa_ref, pb_ref, y_ref, yat_ref, ybt_ref, mt_ref), work)

    vec = lambda n: pl.BlockSpec((1, n), lambda i: (0, 0))
    tile = pl.BlockSpec((TM, D), lambda i: (i, 0))
    tile_t = pl.BlockSpec((D, TM), lambda i: (0, i))
    return pl.pallas_call(
        body, name="mixer_fwd", grid=(t // TM,),
        in_specs=[tile] + _part_specs(TM, 4, 5)
        + [pl.BlockSpec((2, TM, D), lambda i: (0, i, 0)), pl.BlockSpec((1, 6, 1, D), _stream_row(TM)), vec(D), vec(D),
           VMEM_WHOLE, VMEM_WHOLE, vec(HD), VMEM_WHOLE, VMEM_WHOLE, VMEM_WHOLE],
        out_specs=[tile] * 4 + [tile_t] * 3,
        out_shape=[jax.ShapeDtypeStruct((t, D), F32)] + [jax.ShapeDtypeStruct((t, D), BF16)] * 3
        + [jax.ShapeDtypeStruct((D, t), BF16)] * 3, compiler_params=_cp(1),
    )(x, parts, parts, parts, parts, parts, o, mod, lnw, lnb, sw, sb, hnw, wa, wb, wo)


def ffn_up_fwd(x, mod, nw, wg, skip_ctx):
    t = x.shape[0]

    def body(x_ref, mod_ref, nw_ref, w_ref, out_ref, ht_ref):
        def work():
            h32 = _norm_mod(x_ref[...], nw_ref[...], mod_ref[0, 3], mod_ref[0, 4])
            ht_ref[...] = h32.T.astype(BF16)
            h = h32.astype(BF16)
            for j in range(N_DEV):
                out_ref[j] = jnp.dot(h, w_ref[j], preferred_element_type=F32)

        _unless_ctx(skip_ctx, pl.program_id(0) == 0, (out_ref, ht_ref), work)

    return pl.pallas_call(
        body, name="ffn_up_fwd", grid=(t // TM,),
        in_specs=[pl.BlockSpec((TM, D), lambda i: (i, 0)), pl.BlockSpec((1, 6, 1, D), _stream_row(TM)),
                  pl.BlockSpec((1, D), lambda i: (0, 0)), VMEM_WHOLE],
        out_specs=[pl.BlockSpec((N_DEV, TM, FF_SLOT), lambda i: (0, i, 0)), pl.BlockSpec((D, TM), lambda i: (0, i))],
        out_shape=[jax.ShapeDtypeStruct((N_DEV, t, FF_SLOT), F32), jax.ShapeDtypeStruct((D, t), BF16)],
        compiler_params=_cp(1))(x, mod, nw, wg)


def _halo_specs(nt, k_of, i_of):
    per = TM // GRID_W
    last = nt * per - 1
    return [pl.BlockSpec((1, GRID_W, FF_SLOT), lambda *g: (k_of(*g), jnp.maximum(i_of(*g) * per - 1, 0), 0)),
            pl.BlockSpec((1, TM, FF_SLOT), lambda *g: (k_of(*g), i_of(*g), 0)),
            pl.BlockSpec((1, GRID_W, FF_SLOT), lambda *g: (k_of(*g), jnp.minimum(i_of(*g) * per + per, last), 0))]


def _with_halo(prev_ref, main_ref, next_ref, i, nt):
    prev = jnp.where(i >= 2, prev_ref[0], 0.0)
    nxt = jnp.where((i >= 1) & (i <= nt - 2), next_ref[0], 0.0)
    return jnp.concatenate([prev, main_ref[0], nxt], axis=0)


def _tap_valid(dc, i, n_rows, offset):
    r = lax.broadcasted_iota(jnp.int32, (n_rows, 1), 0) - offset
    col = jnp.bitwise_and(r, GRID_W - 1)
    pos = jnp.where(i == 0, r, col) + dc
    return (pos >= 0) & (pos < jnp.where(i == 0, TM, GRID_W))


def _row_weight(cw_ref, dr, dc, i):
    w = cw_ref[0, 3 * (dr + 1) + dc + 1:3 * (dr + 1) + dc + 2, :]
    return w if dr == 0 else jnp.where(i == 0, 0.0, w)


def ffn_down_fwd(x, av, mod, cw, cb, wd, skip_ctx):
    t = x.shape[0]
    nt = t // TM
    ext = TM + 2 * GRID_W

    def body(x_ref, ap_ref, am_ref, an_ref, v_ref, mod_ref, cw_ref, cb_ref, wd_ref, out_ref, ac_ref, y_ref, z_ref, acc):
        i, k = pl.program_id(0), pl.program_id(1)

        def work():
            a_ext = _with_halo(ap_ref, am_ref, an_ref, i, nt)
            conv = jnp.zeros((TM, FF_SLOT), F32) + cb_ref[0]
            for dc in (-1, 0, 1):
                col = functools.reduce(lambda p, q: p + q, [a_ext[GRID_W + GRID_W * dr:GRID_W + GRID_W * dr + TM]
                                                            * _row_weight(cw_ref, dr, dc, i) for dr in (-1, 0, 1)])
                conv = conv + (col if dc == 0 else jnp.where(_tap_valid(dc, i, TM, 0), pltpu.roll(col, (-dc) % TM, 0), 0.0))
            ac_ref[0] = conv.astype(BF16)
            z = _glu_fn(conv, v_ref[0]).astype(BF16)
            z_ref[0] = z
            part = mm(z, wd_ref[0])

            @pl.when(k == 0)
            def _():
                acc[...] = part

            @pl.when(k > 0)
            def _():
                acc[...] += part

            @pl.when(k == N_FFK - 1)
            def _():
                y_ref[...] = acc[...]
                out_ref[...] = x_ref[...] + mod_ref[0, 5] * acc[...]

        _unless_ctx(skip_ctx, i == 0, (out_ref, ac_ref, y_ref, z_ref), work)

    tile = pl.BlockSpec((TM, D), lambda i, k: (i, 0))
    return pl.pallas_call(
        body, name="ffn_down_fwd", grid=(nt, N_FFK),
        in_specs=[tile] + _halo_specs(nt, lambda i, k: k, lambda i, k: i)
        + [pl.BlockSpec((1, TM, FF_SLOT), lambda i, k: (N_FFK + k, i, 0)),
           pl.BlockSpec((1, 6, 1, D), lambda i, k: (jnp.where(i < 1, 0, 1), 0, 0, 0)),
           pl.BlockSpec((1, 9, FF_SLOT), lambda i, k: (k, 0, 0)), pl.BlockSpec((1, 1, FF_SLOT), lambda i, k: (k, 0, 0)),
           pl.BlockSpec((1, FF_SLOT, D), lambda i, k: (k, 0, 0))],
        out_specs=[tile, pl.BlockSpec((1, TM, FF_SLOT), lambda i, k: (k, i, 0)), tile,
                   pl.BlockSpec((1, TM, FF_SLOT), lambda i, k: (k, i, 0))],
        out_shape=[jax.ShapeDtypeStruct((t, D), F32), jax.ShapeDtypeStruct((N_FFK, t, FF_SLOT), BF16),
                   jax.ShapeDtypeStruct((t, D), F32), jax.ShapeDtypeStruct((N_FFK, t, FF_SLOT), BF16)],
        scratch_shapes=[pltpu.VMEM((TM, D), F32)], compiler_params=_cp(2))(x, av, av, av, av, mod, cw, cb, wd)


def loss_fwd_bwd(x, target, fw):
    t = x.shape[0]

    def body(x_ref, t_ref, w_ref, loss_ref, dx_ref, dw_ref):
        i = pl.program_id(0)

        @pl.when(i == 0)
        def _():
            loss_ref[...] = jnp.zeros_like(loss_ref)
            dw_ref[...] = jnp.zeros_like(dw_ref)
            dx_ref[...] = jnp.zeros_like(dx_ref)

        @pl.when(i > 0)
        def _():
            y, vjp = jax.vjp(_rms, x_ref[...], w_ref[...])
            err = y - t_ref[...]
            loss_ref[...] += 0.5 * jnp.sum(jnp.sum(err * err, axis=-1, keepdims=True) / D)
            dx, dw = vjp(err / D)
            dx_ref[...] = dx
            dw_ref[...] += dw

    return pl.pallas_call(
        body, name="loss_fwd_bwd", grid=(t // TM,),
        in_specs=[pl.BlockSpec((TM, D), lambda i: (i, 0)), pl.BlockSpec((TM, D), lambda i: (jnp.maximum(i - 1, 0), 0)),
                  pl.BlockSpec((1, D), lambda i: (0, 0))],
        out_specs=[pl.BlockSpec((8, 128), lambda i: (0, 0)), pl.BlockSpec((TM, D), lambda i: (i, 0)),
                   pl.BlockSpec((1, D), lambda i: (0, 0))],
        out_shape=[jax.ShapeDtypeStruct((8, 128), F32), jax.ShapeDtypeStruct((t, D), F32), jax.ShapeDtypeStruct((1, D), F32)],
        compiler_params=_cp(1))(x, target, fw)


def _stream_add(ref, k, is_ctx, val):
    ref[0, k] += jnp.where(is_ctx, val, 0.0)
    ref[1, k] += jnp.where(is_ctx, 0.0, val)


def ffn_down_bwd(dx, ac, av, y, mod, wd, skip_ctx):
    t = dx.shape[0]
    nt = t // TM

    def body(dx_ref, ac_ref, v_ref, y_ref, mod_ref, wd_ref, dav_ref, dac_ref, dout_ref, dg_ref):
        i = pl.program_id(0)

        @pl.when(i == 0)
        def _():
            dg_ref[...] = jnp.zeros_like(dg_ref)

        def work():
            _stream_add(dg_ref, 0, i == 0, jnp.sum(dx_ref[...] * y_ref[...], axis=0, keepdims=True))
            dout = (mod_ref[0, 5] * dx_ref[...]).astype(BF16)
            dout_ref[...] = dout
            for k in range(N_FFK):
                _, vjp = jax.vjp(_glu_fn, ac_ref[k].astype(F32), v_ref[k])
                dac, dv = vjp(mm_nt(dout, wd_ref[k]))
                dac_ref[k] = dac
                dav_ref[k] = dv.astype(BF16)

        _unless_ctx(skip_ctx, i == 0, (dav_ref, dac_ref, dout_ref), work)

    tile = pl.BlockSpec((TM, D), lambda i: (i, 0))
    half = lambda first: pl.BlockSpec((N_FFK, TM, FF_SLOT), lambda i: (first, i, 0))
    return pl.pallas_call(
        body, name="ffn_down_bwd", grid=(nt,),
        in_specs=[tile, half(0), half(1), tile, pl.BlockSpec((1, 6, 1, D), _stream_row(TM)), VMEM_WHOLE],
        out_specs=[half(1), half(0), tile, pl.BlockSpec((2, 1, 1, D), lambda i: (0, 0, 0, 0))],
        out_shape=[jax.ShapeDtypeStruct((N_DEV, t, FF_SLOT), BF16), jax.ShapeDtypeStruct((N_FFK, t, FF_SLOT), F32),
                   jax.ShapeDtypeStruct((t, D), BF16), jax.ShapeDtypeStruct((2, 1, 1, D), F32)],
        compiler_params=_cp(1))(dx, ac, av, y, mod, wd)


def conv_bwd(dav, dac, av, cw, skip_ctx):
    t = dac.shape[1]
    nt = t // TM
    ext = TM + 2 * GRID_W

    def body(dav_in, gp_ref, gm_ref, gn_ref, ap_ref, am_ref, an_ref, cw_ref, dav_ref, dcw_ref, dcb_ref):
        k, i = pl.program_id(0), pl.program_id(1)

        @pl.when(i == 0)
        def _():
            dcw_ref[...] = jnp.zeros_like(dcw_ref)
            dcb_ref[...] = jnp.zeros_like(dcb_ref)

        def work():
            g_ext = _with_halo(gp_ref, gm_ref, gn_ref, i, nt)
            a_ext = _with_halo(ap_ref, am_ref, an_ref, i, nt)
            g_main = gm_ref[0]
            dcb_ref[0] += jnp.sum(g_main, axis=0, keepdims=True)
            da = jnp.zeros((TM, FF_SLOT), F32)
            for dc in (-1, 0, 1):
                valid = _tap_valid(dc, i, TM, 0)
                q = functools.reduce(lambda p, r: p + r, [g_ext[GRID_W - GRID_W * dr:GRID_W - GRID_W * dr + TM]
                                                          * _row_weight(cw_ref, dr, dc, i) for dr in (-1, 0, 1)])
                da = da + (q if dc == 0 else pltpu.roll(jnp.where(valid, q, 0.0), dc % TM, 0))
                g_shift = g_main if dc == 0 else pltpu.roll(jnp.where(valid, g_main, 0.0), dc % TM, 0)
                for dr in (-1, 0, 1):
                    lo = GRID_W + GRID_W * dr
                    tap = 3 * (dr + 1) + dc + 1
                    dw = jnp.sum(g_shift * a_ext[lo:lo + TM], axis=0, keepdims=True)
                    dcw_ref[0, tap:tap + 1, :] += dw if dr == 0 else jnp.where(i == 0, 0.0, dw)
            dav_ref[0] = da.astype(BF16)

        _unless_ctx(skip_ctx, i == 0, (dav_ref,), work)

    return pl.pallas_call(
        body, name="conv_bwd", grid=(N_FFK, nt),
        in_specs=[ANY] + _halo_specs(nt, lambda k, i: k, lambda k, i: i) + _halo_specs(nt, lambda k, i: k, lambda k, i: i)
        + [pl.BlockSpec((1, 9, FF_SLOT), lambda k, i: (k, 0, 0))],
        out_specs=[pl.BlockSpec((1, TM, FF_SLOT), lambda k, i: (k, i, 0)), pl.BlockSpec((1, 9, FF_SLOT), lambda k, i: (k, 0, 0)),
                   pl.BlockSpec((1, 1, FF_SLOT), lambda k, i: (k, 0, 0))],
        out_shape=[jax.ShapeDtypeStruct(dav.shape, BF16), jax.ShapeDtypeStruct((N_FFK, 9, FF_SLOT), F32),
                   jax.ShapeDtypeStruct((N_FFK, 1, FF_SLOT), F32)],
        input_output_aliases={0: 0}, compiler_params=_cp(2))(dav, dac, dac, dac, av, av, av, cw)


def _norm_mod_bwd(x_ref, nw_ref, mod_ref, k_shift, dh, dx_in, dx_ref, dnw_ref, dmod_ref, is_ctx):
    _, vjp = jax.vjp(_norm_mod, x_ref[...], nw_ref[...], mod_ref[0, k_shift], mod_ref[0, k_shift + 1])
    dx, dnw, dshift, dscale = vjp(dh)
    dx_ref[...] = dx_in + dx
    dnw_ref[...] += dnw
    _stream_add(dmod_ref, 0, is_ctx, dshift)
    _stream_add(dmod_ref, 1, is_ctx, dscale)


def ffn_up_bwd_x(dx2, x, dav, mod, nw, wg, skip_ctx):
    t = x.shape[0]

    def body(dx2_ref, x_ref, dav_ref, mod_ref, nw_ref, w_ref, dx_ref, dnw_ref, dmod_ref):
        i = pl.program_id(0)

        @pl.when(i == 0)
        def _():
            dnw_ref[...] = jnp.zeros_like(dnw_ref)
            dmod_ref[...] = jnp.zeros_like(dmod_ref)

        def work():
            dh = mm_nt(dav_ref[0], w_ref[0])
            for j in range(1, N_DEV):
                dh = dh + mm_nt(dav_ref[j], w_ref[j])
            _norm_mod_bwd(x_ref, nw_ref, mod_ref, 3, dh, dx2_ref[...], dx_ref, dnw_ref, dmod_ref, i == 0)

        _unless_ctx(skip_ctx, i == 0, (dx_ref,), work)

    tile = pl.BlockSpec((TM, D), lambda i: (i, 0))
    return pl.pallas_call(
        body, name="ffn_up_bwd_x", grid=(t // TM,),
        in_specs=[tile, tile, pl.BlockSpec((N_DEV, TM, FF_SLOT), lambda i: (0, i, 0)), pl.BlockSpec((1, 6, 1, D), _stream_row(TM)),
                  pl.BlockSpec((1, D), lambda i: (0, 0)), VMEM_WHOLE],
        out_specs=[tile, pl.BlockSpec((1, D), lambda i: (0, 0)), pl.BlockSpec((2, 2, 1, D), lambda i: (0, 0, 0, 0))],
        out_shape=[jax.ShapeDtypeStruct((t, D), F32), jax.ShapeDtypeStruct((1, D), F32), jax.ShapeDtypeStruct((2, 2, 1, D), F32)],
        compiler_params=_cp(1))(dx2, x, dav, mod, nw, wg)


def weight_grad(at, dout, slot, name, after=None):
    rows, t = at.shape
    stacked = dout.ndim == 3
    n = dout.shape[0] if stacked else dout.shape[1] // slot

    def body(a_ref, d_ref, *rest):
        dw_ref = rest[-1]
        dw_ref[0] = jnp.dot(a_ref[...], d_ref[0] if stacked else d_ref[...], preferred_element_type=F32).astype(dw_ref.dtype)

    d_spec = pl.BlockSpec((1, t, slot), lambda j: (j, 0, 0)) if stacked else pl.BlockSpec((t, slot), lambda j: (0, j))
    extra = [] if after is None else [jnp.reshape(after, (1, 1))]
    return pl.pallas_call(
        body, name=name, grid=(n,), in_specs=[VMEM_WHOLE, d_spec] + [ANY] * len(extra),
        out_specs=pl.BlockSpec((1, rows, slot), lambda j: (j, 0, 0)),
        out_shape=jax.ShapeDtypeStruct((n, rows, slot), GRAD_WIRE), compiler_params=_cp(1))(at, dout, *extra)


def weight_grad_rows(at, dout, name):
    n, t, rows = at.shape
    cols = dout.shape[1]

    def body(a_ref, d_ref, dw_ref):
        dw_ref[0] = _dot(a_ref[0], d_ref[...], ((0,), (0,))).astype(dw_ref.dtype)

    return pl.pallas_call(
        body, name=name, grid=(n,), in_specs=[pl.BlockSpec((1, t, rows), lambda k: (k, 0, 0)), VMEM_WHOLE],
        out_specs=pl.BlockSpec((1, rows, cols), lambda k: (k, 0, 0)),
        out_shape=jax.ShapeDtypeStruct((n, rows, cols), GRAD_WIRE), compiler_params=_cp(1))(at, dout)


def mixer_bwd(dx, parts, o, pa, pb, y, mod, lnw, lnb, sw, sb, hnw, wa, wb, wo, skip_ctx):
    t = dx.shape[0]
    tm = TM
    n_ctx = CTX // tm

    def body(dx_ref, u_ref, v_ref, og_ref, ga_ref, gb_ref, o_ref, pa_ref, pb_ref, y_ref, mod_ref, lnw_ref, lnb_ref, sw_ref,
             sb_ref, hnw_ref, wa_ref, wb_ref, wo_ref, dp_ref, do_ref, dy_ref, dpa_ref, dpb_ref, dlnw_ref, dlnb_ref, dsw_ref,
             dsb_ref, dhnw_ref, dg_ref):
        i = pl.program_id(0)

        @pl.when(i == 0)
        def _():
            for r in (dlnw_ref, dlnb_ref, dsw_ref, dsb_ref, dhnw_ref, dg_ref):
                r[...] = jnp.zeros_like(r)

        def work():
            _, _, vjps, vjp_b = _mixer_tile(slice(0, tm), u_ref, v_ref, og_ref, o_ref, lnw_ref, lnb_ref, sw_ref, sb_ref, hnw_ref)
            pa, pb = pa_ref[...].astype(F32), pb_ref[...].astype(F32)
            sa, sbg = jax.nn.sigmoid(ga_ref[...]), jax.nn.sigmoid(gb_ref[...])
            dxv = dx_ref[...]
            _stream_add(dg_ref, 0, i < n_ctx, jnp.sum(dxv * y_ref[...].astype(F32), axis=0, keepdims=True))
            dy = (mod_ref[0, 2] * dxv).astype(BF16)
            dy_ref[...] = dy
            dmerged = mm_nt(dy, wo_ref[...])
            dpa, dpb = (sa * dmerged).astype(BF16), (sbg * dmerged).astype(BF16)
            dpa_ref[...], dpb_ref[...] = dpa, dpb
            first = 4 * D
            dp_ref[:, first + 3 * D:first + 4 * D] = (dmerged * pa * sa * (1.0 - sa)).astype(BF16)
            dp_ref[:, first + 4 * D:first + 5 * D] = (dmerged * pb * sbg * (1.0 - sbg)).astype(BF16)
            dya = mm_nt(dpa, wa_ref[...])
            dob, dog, dhnw = vjp_b(mm_nt(dpb, wb_ref[...]))
            dp_ref[:, first + 2 * D:first + 3 * D] = dog.astype(BF16)
            dhnw_ref[...] += dhnw
            for g in range(HEADS):
                do_ref[:, _hsl(g)] = dob[g]
            for c, vjp_a in enumerate(vjps):
                rows = slice(c * SGU_CH, (c + 1) * SGU_CH)
                dub, dvb, dlnw, dlnb, dsw, dsb = vjp_a(dya[rows])
                for g in range(HEADS):
                    dp_ref[rows, first + g * HD:first + (g + 1) * HD] = dub[g].astype(BF16)
                    dp_ref[rows, first + D + g * HD:first + D + (g + 1) * HD] = dvb[g].astype(BF16)
                    dlnw_ref[:, _hsl(g)] += dlnw[g]
                    dlnb_ref[:, _hsl(g)] += dlnb[g]
                    dsw_ref[g] += dsw[g]
                    dsb_ref[g] += dsb[g]

        _unless_ctx(skip_ctx, i < n_ctx, (dp_ref, do_ref, dy_ref, dpa_ref, dpb_ref), work)

    vec = lambda n: pl.BlockSpec((1, n), lambda i: (0, 0))
    tile = pl.BlockSpec((tm, D), lambda i: (i, 0))
    sds = jax.ShapeDtypeStruct
    return pl.pallas_call(
        body, name="mixer_bwd", grid=(t // tm,),
        in_specs=[tile] + _part_specs(tm, 4, 5)
        + [pl.BlockSpec((2, tm, D), lambda i: (0, i, 0)), tile, tile, tile, pl.BlockSpec((1, 6, 1, D), _stream_row(tm)),
           vec(D), vec(D), VMEM_WHOLE, VMEM_WHOLE, vec(HD), VMEM_WHOLE, VMEM_WHOLE, VMEM_WHOLE],
        out_specs=[pl.BlockSpec((tm, D_IN), lambda i: (i, 0)), tile, tile, tile, tile, vec(D), vec(D),
                   VMEM_WHOLE, VMEM_WHOLE, vec(HD), pl.BlockSpec((2, 1, 1, D), lambda i: (0, 0, 0, 0))],
        out_shape=[sds((t, D_IN), BF16), sds((t, D), F32), sds((t, D), BF16), sds((t, D), BF16), sds((t, D), BF16),
                   sds((1, D), F32), sds((1, D), F32), sds((HEADS, SGU_CH, SGU_CH), F32), sds((HEADS, SGU_CH, 1), F32),
                   sds((1, HD), F32), sds((2, 1, 1, D), F32)],
        compiler_params=_cp(1))(dx, parts, parts, parts, parts, parts, o, pa, pb, y, mod, lnw, lnb, sw, sb, hnw, wa, wb, wo)


def hgrn_bwd(d, parts, lb, mc, mtc, mrefc, ck, do, first=None, dparts=None):
    t = parts.shape[0]
    nc = t // CH
    chunk = _scan_chunk(nc)
    rev = lambda s: chunk(d, nc - 1 - s)

    def body(q_ref, f_ref, i_ref, lb_ref, m_ref, mt_ref, mr_ref, ck_ref, do_ref, *rest):
        dst = rest[-1]
        dlb_ref = rest[-2]

        @pl.when(pl.program_id(0) == 0)
        def _():
            dst[...] = jnp.zeros_like(dst)
            dlb_ref[...] = jnp.zeros_like(dlb_ref)

        heads = range(HEADS)
        fn = functools.partial(_hgrn_chunk, m=m_ref[0], mt=mt_ref[0], mref=mr_ref[0])
        _, vjp = jax.vjp(fn, [ck_ref[0, 0, h].astype(F32) for h in heads], [q_ref[:, _hsl(h)] for h in heads],
                         [f_ref[:, _hsl(h)] for h in heads], [i_ref[:, _hsl(h)] for h in heads],
                         [lb_ref[0, :, _hsl(h)] for h in heads])
        dstl, dq, df, di, dlb = vjp(([do_ref[:, _hsl(h)] for h in heads], [dst[h] for h in heads]))
        for h in heads:
            dst[h] = dstl[h]
            dlb_ref[0, :, _hsl(h)] += dlb[h]
            if d == 0:
                dq_ref, df_ref, di_ref = rest[:3]
                dq_ref[:, _hsl(h)] = dq[h].astype(BF16)
                df_ref[:, _hsl(h)] = df[h].astype(BF16)
                di_ref[:, _hsl(h)] = di[h].astype(BF16)
            else:
                dq0_ref, df0_ref, di0_ref, _, dp_ref = rest[:5]
                col = lambda k: slice(k * D + h * HD, k * D + (h + 1) * HD)
                dp_ref[:, col(0)] = (dq0_ref[:, _hsl(h)].astype(F32) + dq[h]).astype(BF16)
                dp_ref[:, col(1)] = df0_ref[:, _hsl(h)]
                dp_ref[:, col(2)] = df[h].astype(BF16)
                dp_ref[:, col(3)] = (di0_ref[:, _hsl(h)].astype(F32) + di[h]).astype(BF16)

    const = lambda s: (d, 0, 0)
    at = lambda k: pl.BlockSpec((CH, D), lambda s: (rev(s), k))
    in_specs = [at(0), at(1 + d), at(3), pl.BlockSpec((1, 1, D), const), pl.BlockSpec((1, CH, CH), const),
                pl.BlockSpec((1, CH, CH), const), pl.BlockSpec((1, CH, 1), const),
                pl.BlockSpec((1, 1, HEADS, HD, HD), lambda s: (d, nc - 1 - s, 0, 0, 0)), at(0)]
    dlb_spec, dlb_shape = pl.BlockSpec((1, 1, D), lambda s: (0, 0, 0)), jax.ShapeDtypeStruct((1, 1, D), F32)
    common = dict(grid=(nc,), scratch_shapes=[pltpu.VMEM((HEADS, HD, HD), F32)], compiler_params=_cp(1))
    if d == 0:
        return pl.pallas_call(body, name="hgrn_bwd_fwd_dir", in_specs=in_specs, out_specs=[at(0)] * 3 + [dlb_spec],
                              out_shape=[jax.ShapeDtypeStruct((t, D), BF16)] * 3 + [dlb_shape], **common,
                              )(parts, parts, parts, lb, mc, mtc, mrefc, ck, do)
    return pl.pallas_call(body, name="hgrn_bwd_bwd_dir", in_specs=in_specs + [at(0)] * 3 + [ANY],
                          out_specs=[pl.BlockSpec((CH, 4 * D), lambda s: (rev(s), 0)), dlb_spec],
                          out_shape=[jax.ShapeDtypeStruct(dparts.shape, BF16), dlb_shape], input_output_aliases={12: 0},
                          **common)(parts, parts, parts, lb, mc, mtc, mrefc, ck, do, *first, dparts)


def in_proj_bwd_x(dx1, x, dparts, mod, nw, wg, after=None, latent_only=False):
    t = x.shape[0]
    tm = TM
    n_ctx = CTX // tm

    def body(dx1_ref, x_ref, dp_ref, mod_ref, nw_ref, w_ref, *rest):
        dx_ref, dnw_ref, dmod_ref = rest[-3:]
        i = pl.program_id(0)

        @pl.when(i == 0)
        def _():
            dnw_ref[...] = jnp.zeros_like(dnw_ref)
            dmod_ref[...] = jnp.zeros_like(dmod_ref)

        dh = mm_nt(dp_ref[:, 0:IN_SLOT], w_ref[0])
        for j in range(1, N_DEV):
            dh = dh + mm_nt(dp_ref[:, j * IN_SLOT:(j + 1) * IN_SLOT], w_ref[j])
        _norm_mod_bwd(x_ref, nw_ref, mod_ref, 0, dh, dx1_ref[...], dx_ref, dnw_ref, dmod_ref, i < n_ctx)

    tile = pl.BlockSpec((tm, D), lambda i: (i, 0))
    extra = [] if after is None else [jnp.reshape(after, (1, 1))]
    return pl.pallas_call(
        body, name="in_proj_bwd_x", grid=(t // tm,),
        in_specs=[tile, tile, pl.BlockSpec((tm, D_IN), lambda i: (i, 0)), pl.BlockSpec((1, 6, 1, D), _stream_row(tm)),
                  pl.BlockSpec((1, D), lambda i: (0, 0)), VMEM_WHOLE] + [ANY] * len(extra),
        out_specs=[pl.BlockSpec((tm, D), lambda i: (jnp.maximum(i - n_ctx, 0), 0)) if latent_only else tile,
                   pl.BlockSpec((1, D), lambda i: (0, 0)), pl.BlockSpec((2, 2, 1, D), lambda i: (0, 0, 0, 0))],
        out_shape=[jax.ShapeDtypeStruct((t - CTX if latent_only else t, D), F32), jax.ShapeDtypeStruct((1, D), F32),
                   jax.ShapeDtypeStruct((2, 2, 1, D), F32)],
        compiler_params=_cp(1))(dx1, x, dparts, mod, nw, wg, *extra)


def _lb_fn(h0, h1):
    m = jnp.maximum(h0, h1)
    e0, e1 = jnp.exp(h0 - m), jnp.exp(h1 - m)
    return e1 / (e0 + e1)


def lower_bounds(hlb):
    def body(h_ref, out_ref):
        out_ref[...] = _lb_fn(h_ref[0:1, :], h_ref[1:2, :])
    return pl.pallas_call(body, name="lower_bounds", out_shape=jax.ShapeDtypeStruct((1, 2 * D), F32))(hlb)


def lower_bounds_bwd(hlb, dlb1):
    def body(h_ref, d_ref, out_ref):
        _, vjp = jax.vjp(_lb_fn, h_ref[0:1, :], h_ref[1:2, :])
        d0, d1 = vjp(d_ref[...])
        out_ref[0:1, :] = d0
        out_ref[1:2, :] = d1
    return pl.pallas_call(body, name="lower_bounds_bwd", out_shape=jax.ShapeDtypeStruct((2, 2 * D), F32))(hlb, dlb1)


def _ada_fn(c_all, cctx8, w, b):
    dot = lambda a, l: jnp.dot(_silu(a), w[l], precision=HIGHEST, preferred_element_type=F32) + b[l]
    return [dot(c_all, l) for l in range(2)], [dot(cctx8, l) for l in range(2)]


def ada_fwd(c_all, cctx8, w, b):
    cols = w.shape[-1]

    def body(c_ref, cc_ref, w_ref, b_ref, out_ref):
        ox, oc = _ada_fn(c_ref[...], cc_ref[...], [w_ref[0], w_ref[1]], [b_ref[0], b_ref[1]])
        for l in range(2):
            out_ref[l, 0] = ox[l]
            out_ref[l, 1] = oc[l]
    return pl.pallas_call(body, name="ada_fwd", out_shape=jax.ShapeDtypeStruct((2, 2, N_DEV, cols), F32),
                          compiler_params=_cp(0))(c_all, cctx8, w, b)


def ada_bwd(c_all, cctx8, w, b, dmx, dmc):
    cols = w.shape[-1]

    def body(c_ref, cc_ref, w_ref, b_ref, dmx_ref, dmc_ref, dw_ref, dc_ref):
        fn = lambda cc, w0, w1: _ada_fn(c_ref[...], cc, [w0, w1], [b_ref[0], b_ref[1]])
        _, vjp = jax.vjp(fn, cc_ref[...], w_ref[0], w_ref[1])
        dcc, dw0, dw1 = vjp(([dmx_ref[0], dmx_ref[1]], [dmc_ref[0], dmc_ref[1]]))
        dw_ref[0] = dw0
        dw_ref[1] = dw1
        dc_ref[...] = jnp.sum(dcc, axis=0, keepdims=True)
    return pl.pallas_call(body, name="ada_bwd", out_shape=[jax.ShapeDtypeStruct((2, D, cols), F32), jax.ShapeDtypeStruct((1, D), F32)],
                          compiler_params=_cp(0))(c_all, cctx8, w, b, dmx, dmc)


def adamw(w, m, v, gparts, name):
    r, c = w.shape
    p = gparts.shape[0]
    rt = r
    while rt % 16 == 0 and (p + 7) * rt * c * 4 * 2 > 24 * 2 ** 20:
        rt //= 2

    def body(w_ref, m_ref, v_ref, g_ref, go_ref, d_ref, mo_ref, vo_ref):
        g = g_ref[0].astype(F32)
        for k in range(1, p):
            g = g + g_ref[k].astype(F32)
        m2 = ADAM_B1 * m_ref[...] + (1.0 - ADAM_B1) * g
        v2 = ADAM_B2 * v_ref[...] + (1.0 - ADAM_B2) * (g * g)
        m_hat = m2 / (1.0 - ADAM_B1 ** ADAM_STEP)
        v_hat = v2 / (1.0 - ADAM_B2 ** ADAM_STEP)
        go_ref[...] = g
        d_ref[...] = -ADAM_LR * (m_hat / (jnp.sqrt(v_hat) + ADAM_EPS) + ADAM_WD * w_ref[...])
        mo_ref[...] = m2
        vo_ref[...] = v2

    tile = pl.BlockSpec((rt, c), lambda i: (i, 0))
    return pl.pallas_call(
        body, name=name, grid=(r // rt,),
        in_specs=[tile, tile, tile, pl.BlockSpec((p, rt, c), lambda i: (0, i, 0))], out_specs=[tile] * 4,
        out_shape=[jax.ShapeDtypeStruct((r, c), F32)] * 4, compiler_params=_cp(1))(w, m, v, gparts)


def _me():
    x, y, c = lax.axis_index("x"), lax.axis_index("y"), lax.axis_index("c")
    return x, y, c, 4 * x + 2 * y + c


def _peer(x, y, c, p):
    fx, fy, fc = (p >> 2) & 1, (p >> 1) & 1, p & 1
    return (1 - x if fx else x, 1 - y if fy else y, 1 - c if fc else c)


def all_gather(arrs, name, after=None):
    n = len(arrs)
    extra = [] if after is None else list(after) if isinstance(after, (list, tuple)) else [after]

    def body(*refs):
        ins, outs = refs[:n], refs[n + len(extra):2 * n + len(extra)]
        send, recv, local = refs[2 * n + len(extra):]
        x, y, c, me = _me()
        copies = []
        for a in range(n):
            lc = pltpu.make_async_copy(ins[a], outs[a].at[me], local.at[a])
            lc.start()
            copies.append(lc)
            for p in range(1, N_DEV):
                cp = pltpu.make_async_remote_copy(src_ref=ins[a], dst_ref=outs[a].at[me], send_sem=send.at[a, p - 1],
                                                  recv_sem=recv.at[a, p - 1], device_id=_peer(x, y, c, p),
                                                  device_id_type=pl.DeviceIdType.MESH)
                cp.start()
                copies.append(cp)
        for cp in copies:
            cp.wait()

    return pl.pallas_call(
        body, name=name, in_specs=[ANY] * (n + len(extra)), out_specs=[ANY] * n,
        out_shape=[jax.ShapeDtypeStruct((N_DEV,) + a.shape, a.dtype) for a in arrs],
        scratch_shapes=[pltpu.SemaphoreType.DMA((n, N_DEV - 1)), pltpu.SemaphoreType.DMA((n, N_DEV - 1)),
                        pltpu.SemaphoreType.DMA((n,))])(*arrs, *extra)


HBM = pl.BlockSpec(memory_space=pltpu.HBM)
SEM = pl.BlockSpec(memory_space=pltpu.SEMAPHORE)


def _in_hbm(a):
    return pltpu.with_memory_space_constraint(a, pltpu.HBM)


ALL_PEERS = tuple(range(1, N_DEV))
SAME_CORE_AND_SIBLING = (1, 2, 4, 6)
OTHER_CHIPS = (2, 4, 6)


def _exchange_refs(srcs, lands, layer, scatter, a, x, y, c, p, forward=False):
    me = 4 * x + 2 * y + c
    px, py, pc = _peer(x, y, c, p) if p else (x, y, c)
    if forward and p:
        slot = lands[a].at[4 * px + 2 * py + pc]
        return slot, slot, _peer(x, y, c, 1)
    dst = lands[a].at[me] if layer is None else lands[a].at[me, layer]
    src = srcs[a].at[4 * px + 2 * py + pc] if scatter else dst
    return src, dst, (px, py, pc)


def exchange_start(srcs, lands, layer, scatter, name, after=None, peers=ALL_PEERS, forward=False):
    n, ns = len(lands), len(srcs)
    extra = [] if after is None else [after]

    def body(*refs):
        ins, lz = refs[:ns], refs[ns:ns + n]
        send, recv = refs[ns + n + len(extra)], refs[ns + n + len(extra) + 1]
        token = refs[-1]
        x, y, c, _ = _me()
        for a in range(n):
            for p in peers:
                src, dst, peer = _exchange_refs(ins, lz, layer, scatter, a, x, y, c, p, forward)
                k = a * (N_DEV - 1) + p - 1
                pltpu.make_async_remote_copy(src_ref=src, dst_ref=dst, send_sem=send.at[k], recv_sem=recv.at[k],
                                             device_id=peer, device_id_type=pl.DeviceIdType.MESH).start()
        token[...] = jnp.zeros_like(token)

    thru = [pltpu.HBM(a.shape, a.dtype) for a in list(srcs) + list(lands)]
    out = pl.pallas_call(
        body, name=name, in_specs=[HBM] * (ns + n) + [ANY] * len(extra),
        out_specs=[SEM, SEM] + [HBM] * (ns + n) + [pl.BlockSpec(memory_space=pltpu.VMEM)],
        out_shape=[pltpu.SemaphoreType.DMA((n * (N_DEV - 1),)), pltpu.SemaphoreType.DMA((n * (N_DEV - 1),))] + thru
        + [jax.ShapeDtypeStruct((8, 128), F32)],
        input_output_aliases={i: 2 + i for i in range(ns + n)},
        compiler_params=pltpu.CompilerParams(has_side_effects=pltpu.SideEffectType.DATAFLOW_SIDE_EFFECTING),
    )(*[_in_hbm(a) for a in list(srcs) + list(lands)], *extra)
    return out[0], out[1], out[2:2 + ns], out[2 + ns:2 + ns + n], out[-1]


def exchange_wait(send, recv, srcs, lands, layer, scatter, after, name, peers=ALL_PEERS):
    n, ns = len(lands), len(srcs)

    def body(*refs):
        ins, lz = refs[:ns], refs[ns:ns + n]
        send_ref, recv_ref = refs[ns + n], refs[ns + n + 1]
        x, y, c, _ = _me()
        for a in range(n):
            for p in peers:
                src, dst, peer = _exchange_refs(ins, lz, layer, scatter, a, x, y, c, 0)
                k = a * (N_DEV - 1) + p - 1
                cp = pltpu.make_async_remote_copy(src_ref=src, dst_ref=dst, send_sem=send_ref.at[k],
                                                  recv_sem=recv_ref.at[k], device_id=peer,
                                                  device_id_type=pl.DeviceIdType.MESH)
                cp.wait_send()
                cp.wait_recv()

    thru = [pltpu.HBM(a.shape, a.dtype) for a in list(srcs) + list(lands)]
    out = pl.pallas_call(
        body, name=name, in_specs=[HBM] * (ns + n) + [SEM, SEM, ANY], out_specs=[HBM] * (ns + n), out_shape=thru,
        input_output_aliases={i: i for i in range(ns + n)},
        compiler_params=pltpu.CompilerParams(has_side_effects=pltpu.SideEffectType.DATAFLOW_SIDE_EFFECTING),
    )(*srcs, *lands, send, recv, after)
    return out[ns:]


def place_own(src, land, me, layer, scatter, name, src_layer=None):
    create = isinstance(land, jax.ShapeDtypeStruct)
    r, c = src.shape[-2:]
    rt = r
    while rt % 32 == 0 and rt * c * 4 > 2 ** 21:
        rt //= 2

    def body(me_ref, src_ref, *rest):
        out_ref = rest[-1]
        out_ref[...] = src_ref[...].reshape(out_ref.shape).astype(out_ref.dtype)

    src_spec = (pl.BlockSpec((1, rt, c), lambda i, m: (m[0], i, 0)) if scatter else
                pl.BlockSpec((rt, c), lambda i, m: (i, 0)) if src_layer is None else
                pl.BlockSpec((1, rt, c), lambda i, m: (src_layer, i, 0)))
    out_spec = (pl.BlockSpec((1, rt, c), lambda i, m: (m[0], i, 0)) if layer is None
                else pl.BlockSpec((1, 1, rt, c), lambda i, m: (m[0], layer, i, 0)))
    grid_spec = pltpu.PrefetchScalarGridSpec(num_scalar_prefetch=1, grid=(r // rt,),
                                             in_specs=[src_spec] + ([] if create else [ANY]), out_specs=out_spec)
    return pl.pallas_call(body, name=name, grid_spec=grid_spec, out_shape=jax.ShapeDtypeStruct(land.shape, land.dtype),
                          input_output_aliases={} if create else {2: 0}, compiler_params=_cp(1),
                          )(*((me, src) if create else (me, src, land)))


def _scan_constants():
    r = lax.broadcasted_iota(jnp.int32, (CH, CH), 0)
    s = lax.broadcasted_iota(jnp.int32, (CH, CH), 1)
    lower = (s <= r).astype(F32)
    t = jnp.arange(CH)[:, None]
    mc = jnp.stack([lower, lower.T])
    mref = jnp.stack([(t <= CH // 2 - 1).astype(F32), (t >= CH // 2).astype(F32)])
    return mc, jnp.stack([lower.T, lower]), mref


def local_step(x, ctx, target, mod, lb, w, fetch=None, publish=None, small_ready=None, small_early=None):
    kept = {}

    def keep(l, part, grads):
        kept[(l, part)] = grads
        return 0.0

    fetch = fetch or (lambda l, part, after: w)
    publish = publish or keep
    n_layers = len(mod)
    mc, mtc, mrefc = _scan_constants()
    xs = jnp.concatenate([ctx, x], axis=0)
    saved, big = [], []
    for l in range(n_layers):
        wl = dict(fetch(l, "in", xs))
        parts, ht, iv = in_proj_fwd(xs, mod[l], w["nw1"][l], wl["win"][l])
        o, ck = hgrn_fwd(parts, iv, lb[l], mc, mtc, mrefc)
        wl.update(fetch(l, "rest", o))
        last = l == n_layers - 1
        x1, pa, pb, ym, yat, ybt, mt = mixer_fwd(xs, parts, o, mod[l], w["lnw"][l], w["lnb"][l], w["sw"][l], w["sb"][l],
                                                 w["hnw"][l], wl["wa"][l], wl["wb"][l], wl["wo"][l], last)
        av, h2t = ffn_up_fwd(x1, mod[l], w["nw2"][l], wl["wup"][l], last)
        x2, ac, y, z = ffn_down_fwd(x1, av, mod[l], w["cw"][l], w["cb"][l], wl["wd"][l], last)
        saved.append((xs, parts, iv, o, ck, x1, av, ac, y, z, ht, h2t, pa, pb, ym, yat, ybt, mt))
        big.append(wl)
        xs = x2
    loss, dx, dfw = loss_fwd_bwd(xs, target, w["fw"])
    g = {k: [None] * n_layers for k in ("nw1", "nw2", "lnw", "lnb", "sw", "sb", "hnw", "cw", "cb")}
    g["fw"] = dfw
    dmod, dlb = [None] * n_layers, [None] * n_layers
    tok = 0.0
    for l in reversed(range(n_layers)):
        x0, parts, iv, o, ck, x1, av, ac, y, z, ht, h2t, pa, pb, ym, yat, ybt, mt = saved[l]
        wl = big[l]
        last = l == n_layers - 1
        dav, dac, dout, dg2 = ffn_down_bwd(dx, ac, av, y, mod[l] + tok, wl["wd"][l], last)
        dwd = weight_grad_rows(z, dout, "ffn_down_bwd_w")
        dav, g["cw"][l], g["cb"][l] = conv_bwd(dav, dac, av, w["cw"][l], last)
        dx1, g["nw2"][l], dmod2 = ffn_up_bwd_x(dx, x1, dav, mod[l], w["nw2"][l], wl["wup"][l], last)
        dwup = weight_grad(h2t, dav, FF_SLOT, "ffn_up_bwd_w")
        tok = publish(l, "ffn", {"wd": dwd, "wup": dwup})
        (dparts, do, dy, dpa, dpb, g["lnw"][l], g["lnb"][l], g["sw"][l], g["sb"][l], g["hnw"][l],
         dg1) = mixer_bwd(dx1, parts, o, pa, pb, ym, mod[l] + tok, w["lnw"][l], w["lnb"][l], w["sw"][l], w["sb"][l],
                          w["hnw"][l], wl["wa"][l], wl["wb"][l], wl["wo"][l], last)
        tok = publish(l, "mix", {"wa": weight_grad(yat, dpa, D, "mixer_bwd_wa"), "wb": weight_grad(ybt, dpb, D, "mixer_bwd_wb"),
                                 "wo": weight_grad(mt, dy, D, "mixer_bwd_wo")})
        if l == 0 and small_early:
            dmod[0] = jnp.concatenate([jnp.zeros((2, 2, 1, D), F32), dg1, dmod2, dg2], axis=1)
            tok = tok + small_early(loss[0, 0], g, dmod, dlb)
        dq, df, di, dlb_f = hgrn_bwd(0, parts, lb[l] + tok, mc, mtc, mrefc, ck, do)
        dparts, dlb_b = hgrn_bwd(1, parts, lb[l], mc, mtc, mrefc, ck, do, (dq, df, di), dparts)
        dlb[l] = jnp.concatenate([dlb_f, dlb_b], axis=0)
        tok = publish(l, "in", {"win": weight_grad(ht, dparts, IN_SLOT, "in_proj_bwd_w")})
        dx, g["nw1"][l], dmod1 = in_proj_bwd_x(dx1, x0, dparts, mod[l], w["nw1"][l], wl["win"][l], after=tok,
                                               latent_only=l == 0)
        dmod[l] = jnp.concatenate([dmod1, dg1, dmod2, dg2], axis=1)
    done = small_ready(loss[0, 0], g, dmod, dlb) if small_ready else 0.0
    for (l, part), grads in kept.items():
        for k, v in grads.items():
            g.setdefault(k, [None] * n_layers)[l] = v
    return loss[0, 0], dx, g, dmod, dlb, done


ROW = 1024
REPLICATED = ("norm1_w", "sgu_ln_w", "sgu_ln_b", "sgu_w", "sgu_b", "hgrn_lower_bounds", "hgrn_norm_w", "norm2_w",
              "ffn_conv_b", "final_norm_w")
WEIGHT_ORDER = ("c_ctx", "ada_w", "ada_b", "norm1_w", "w_in", "sgu_ln_w", "sgu_ln_b", "sgu_w", "sgu_b", "hgrn_lower_bounds",
                "hgrn_norm_w", "w_branch_a", "w_branch_b", "w_out", "norm2_w", "ffn_w_up", "ffn_conv_w", "ffn_conv_b",
                "ffn_w_down", "final_norm_w")


def _rows_of(n):
    return -(-n // (8 * ROW)) * 8


def _pack(arrs, total_rows=None):
    parts = []
    for a in arrs:
        flat = a.reshape(-1).astype(F32)
        rows = _rows_of(flat.shape[0])
        parts.append(jnp.pad(flat, (0, rows * ROW - flat.shape[0])).reshape(rows, ROW))
    have = sum(p.shape[0] for p in parts)
    if total_rows is not None and total_rows > have:
        parts.append(jnp.zeros((total_rows - have, ROW), F32))
    return jnp.concatenate(parts, axis=0)


def _unpack(packed, shapes):
    lead = packed.shape[:-2]
    out, r0 = [], 0
    for s in shapes:
        n = math.prod(s)
        rows = _rows_of(n)
        out.append(packed[..., r0:r0 + rows, :].reshape(lead + (rows * ROW,))[..., :n].reshape(lead + tuple(s)))
        r0 += rows
    return out


def kernel(x, c, ctx, c_ctx, ada_w, ada_b, norm1_w, w_in, sgu_ln_w, sgu_ln_b, sgu_w, sgu_b, hgrn_lower_bounds, hgrn_norm_w, w_branch_a, w_branch_b, w_out, norm2_w, ffn_w_up, ffn_conv_w, ffn_conv_b, ffn_w_down, final_norm_w, loss_target, m_c_ctx, m_ada_w, m_ada_b, m_norm1_w, m_w_in, m_sgu_ln_w, m_sgu_ln_b, m_sgu_w, m_sgu_b, m_hgrn_lower_bounds, m_hgrn_norm_w, m_w_branch_a, m_w_branch_b, m_w_out, m_norm2_w, m_ffn_w_up, m_ffn_conv_w, m_ffn_conv_b, m_ffn_w_down, m_final_norm_w, v_c_ctx, v_ada_w, v_ada_b, v_norm1_w, v_w_in, v_sgu_ln_w, v_sgu_ln_b, v_sgu_w, v_sgu_b, v_hgrn_lower_bounds, v_hgrn_norm_w, v_w_branch_a, v_w_branch_b, v_w_out, v_norm2_w, v_ffn_w_up, v_ffn_conv_w, v_ffn_conv_b, v_ffn_w_down, v_final_norm_w):
    wts = dict(c_ctx=c_ctx, ada_w=ada_w, ada_b=ada_b, norm1_w=norm1_w, w_in=w_in, sgu_ln_w=sgu_ln_w, sgu_ln_b=sgu_ln_b,
               sgu_w=sgu_w, sgu_b=sgu_b, hgrn_lower_bounds=hgrn_lower_bounds, hgrn_norm_w=hgrn_norm_w, w_branch_a=w_branch_a,
               w_branch_b=w_branch_b, w_out=w_out, norm2_w=norm2_w, ffn_w_up=ffn_w_up, ffn_conv_w=ffn_conv_w,
               ffn_conv_b=ffn_conv_b, ffn_w_down=ffn_w_down, final_norm_w=final_norm_w)
    mom1 = dict(c_ctx=m_c_ctx, ada_w=m_ada_w, ada_b=m_ada_b, norm1_w=m_norm1_w, w_in=m_w_in, sgu_ln_w=m_sgu_ln_w,
                sgu_ln_b=m_sgu_ln_b, sgu_w=m_sgu_w, sgu_b=m_sgu_b, hgrn_lower_bounds=m_hgrn_lower_bounds,
                hgrn_norm_w=m_hgrn_norm_w, w_branch_a=m_w_branch_a, w_branch_b=m_w_branch_b, w_out=m_w_out, norm2_w=m_norm2_w,
                ffn_w_up=m_ffn_w_up, ffn_conv_w=m_ffn_conv_w, ffn_conv_b=m_ffn_conv_b, ffn_w_down=m_ffn_w_down,
                final_norm_w=m_final_norm_w)
    mom2 = dict(c_ctx=v_c_ctx, ada_w=v_ada_w, ada_b=v_ada_b, norm1_w=v_norm1_w, w_in=v_w_in, sgu_ln_w=v_sgu_ln_w,
                sgu_ln_b=v_sgu_ln_b, sgu_w=v_sgu_w, sgu_b=v_sgu_b, hgrn_lower_bounds=v_hgrn_lower_bounds,
                hgrn_norm_w=v_hgrn_norm_w, w_branch_a=v_w_branch_a, w_branch_b=v_w_branch_b, w_out=v_w_out, norm2_w=v_norm2_w,
                ffn_w_up=v_ffn_w_up, ffn_conv_w=v_ffn_conv_w, ffn_conv_b=v_ffn_conv_b, ffn_w_down=v_ffn_w_down,
                final_norm_w=v_final_norm_w)
    n_layers = w_in.shape[0]
    layers = range(n_layers)
    me = 4 * lax.axis_index("x") + 2 * lax.axis_index("y") + lax.axis_index("c")
    ada_cols = ada_w.shape[-1]

    big = ("w_in", "ffn_w_up", "w_branch_a", "w_branch_b", "w_out", "ffn_w_down")
    short = {"w_in": "win", "ffn_w_up": "wup", "w_branch_a": "wa", "w_branch_b": "wb", "w_out": "wo", "ffn_w_down": "wd"}
    me1 = me.reshape(1).astype(jnp.int32)
    groups = [[("w_in", 0)], [(k, 0) for k in big[1:]], [("w_in", 1)], [(k, 1) for k in big[1:]]]
    in_flight, started = [], 0.0

    def own_slots(n):
        return [place_own(wts[k], jax.ShapeDtypeStruct((N_DEV,) + wts[k].shape[1:], BF16), me1, None, False,
                          f"gather_own_{short[k]}_{l}", src_layer=l) for k, l in groups[n]]

    def start_group(n, lands, after):
        in_flight.append(exchange_start([], lands, None, False, f"gather_weights_start_{n}", after=after,
                                        peers=SAME_CORE_AND_SIBLING if n == 0 else ALL_PEERS))
        return in_flight[-1][-1]

    (c_all,) = all_gather([c], "gather_c")
    c_all = c_all.reshape(N_DEV, D)
    token = start_group(0, own_slots(0), c_all)
    later = [own_slots(n) for n in range(1, len(groups))]
    cctx8 = jnp.broadcast_to(c_ctx[None, :], (N_DEV, D))
    ada_b_cols = lax.dynamic_slice_in_dim(ada_b, me * ada_cols, ada_cols, axis=1)[:, None, :]
    mod_cols = ada_fwd(c_all, cctx8, ada_w, ada_b_cols)
    xs = jnp.concatenate([ctx[0], x[0]], axis=0)
    lb1 = lower_bounds(hgrn_lower_bounds)
    mod_all, conv_all = all_gather([mod_cols, ffn_conv_w.reshape(n_layers, 9, -1)], "gather_mod_conv",
                                   after=[token, xs, lb1] + [a for lands in later for a in lands])
    conv_full = [conv_all[:, l].transpose(1, 0, 2).reshape(9, N_FFK, FF_SLOT).transpose(1, 0, 2) for l in layers]
    for n in range(1, len(groups)):
        token = start_group(n, later[n - 1], mod_all if n == 1 else token)
    for started_group in in_flight:
        started = started + started_group[-1][0, 0]

    def as_used(k, a):
        return a if k in ("w_in", "ffn_w_up") else a.reshape(N_FFK, FF_SLOT, D) if k == "ffn_w_down" else a.reshape(D, D)

    arrived = {}

    def fetch(l, part, after):
        n = {(0, "in"): 0, (0, "rest"): 1, (1, "in"): 2, (1, "rest"): 3}.get((l, part))
        if n is not None:
            send, recv, _, lands, _ = in_flight[n]
            first = n == 0
            got = exchange_wait(send, recv, [], lands, None, False, after, f"gather_weights_wait_{n}",
                                peers=SAME_CORE_AND_SIBLING if first else ALL_PEERS)
            if first:
                send, recv, _, lands, _ = exchange_start([], got, None, False, "gather_weights_pass_on", peers=OTHER_CHIPS,
                                                         forward=True)
                got = exchange_wait(send, recv, [], lands, None, False, after, "gather_weights_passed_on", peers=OTHER_CHIPS)
            for (k, ll), a in zip(groups[n], got):
                arrived.setdefault(short[k], [None] * n_layers)[ll] = as_used(k, a)
        return arrived

    mod_x = lax.dynamic_index_in_dim(mod_all[:, :, 0], me, axis=2, keepdims=False)
    mod_c = mod_all[:, :, 1, 0]
    mod = [jnp.stack([mod_c[:, l].reshape(6, 1, D), mod_x[:, l].reshape(6, 1, D)]) for l in layers]
    mod[0] = mod[0] + started

    lb = [jnp.zeros((2, 1, D), F32), lb1.reshape(2, 1, D)]

    w = {
        "nw1": [norm1_w[l][None] for l in layers], "nw2": [norm2_w[l][None] for l in layers],
        "lnw": [sgu_ln_w[l][None] for l in layers], "lnb": [sgu_ln_b[l][None] for l in layers],
        "sw": [sgu_w[l] for l in layers], "sb": [sgu_b[l][:, :, None] for l in layers],
        "hnw": [hgrn_norm_w[l][None] for l in layers], "cw": conv_full,
        "cb": [ffn_conv_b[l].reshape(N_FFK, 1, FF_SLOT) for l in layers], "fw": final_norm_w[None],
    }
    long = {v: k for k, v in short.items()}
    landing, sent = {}, []

    def publish(l, part, grads):
        keys = [long[k] for k in grads]
        slots = [a.reshape((N_DEV, -1, a.shape[-1])) for a in grads.values()]
        zones = [place_own(s, landing.get(k, jax.ShapeDtypeStruct((N_DEV, n_layers) + s.shape[1:], s.dtype)), me1, l, True,
                           f"scatter_own_{short[k]}_{l}") for k, s in zip(keys, slots)]
        send, recv, srcs, zones, token = exchange_start(slots, zones, l, True, f"scatter_grads_start_{part}_{l}")
        landing.update(zip(keys, zones))
        sent.append((keys, l, part, send, recv, srcs, token))
        return token[0, 0]

    out = {}
    flat2 = lambda a: a.reshape(-1, a.shape[-1])

    def finish(part, after):
        done = []
        for keys, l, p, send, recv, srcs, _ in sent:
            if p == part:
                zones = exchange_wait(send, recv, srcs, [landing[k] for k in keys], l, True, after,
                                      f"scatter_grads_wait_{part}_{l}")
                landing.update(zip(keys, zones))
                done = keys
        for k in done:
            r = landing[k]
            res = adamw(flat2(wts[k]), flat2(mom1[k]), flat2(mom2[k]), r.reshape(N_DEV, -1, r.shape[-1]), "adamw_" + k)
            out[k] = tuple(a.reshape(wts[k].shape) for a in res)

    rep_rows = -(-sum(_rows_of(wts[k].size) for k in REPLICATED) // 64) * 64
    conv_rows = _rows_of(n_layers * 9 * D_FF)
    dmod_rows = _rows_of(n_layers * 6 * D)
    early = {}

    def small_early(loss_part, g, dmod, dlb):
        d_hlb = lower_bounds_bwd(hgrn_lower_bounds, dlb[1].reshape(1, 2 * D))
        st = lambda k: jnp.stack([jnp.zeros((1, D), F32) if a is None else a for a in g[k]])
        rep_grads = {"norm1_w": st("nw1"), "sgu_ln_w": st("lnw"), "sgu_ln_b": st("lnb"), "sgu_w": st("sw"), "sgu_b": st("sb"),
                     "hgrn_lower_bounds": d_hlb, "hgrn_norm_w": st("hnw"), "norm2_w": st("nw2"), "ffn_conv_b": st("cb"),
                     "final_norm_w": g["fw"]}
        d_conv = jnp.stack([g["cw"][l].transpose(1, 0, 2).reshape(9, D_FF) for l in layers])
        dmod_x = jnp.stack([dmod[l][1].reshape(6 * D) for l in layers])
        dmod_c = jnp.stack([dmod[l][0].reshape(6 * D) for l in layers])
        small = jnp.concatenate([_pack([rep_grads[k] for k in REPLICATED], rep_rows),
                                 _pack([d_conv, dmod_x, dmod_c, loss_part.reshape(1)])], axis=0)
        zone = place_own(small, jax.ShapeDtypeStruct((N_DEV,) + small.shape, F32), me1, None, False, "gather_small_own")
        early["send"], early["recv"], _, early["zones"], token = exchange_start([], [zone], None, False, "gather_small_start")
        return token[0, 0]

    def small_ready(loss_part, g, dmod, dlb):
        late = _pack([g["nw1"][0], dmod[0][1, 0:2], dmod[0][0, 0:2]])
        for part in ("ffn", "mix"):
            finish(part, late)
        (late_all,) = all_gather([late], "gather_small_late", after=out["w_out"][0])
        (small_all,) = exchange_wait(early["send"], early["recv"], [], early["zones"], None, False, late_all,
                                     "gather_small_wait")
        at_x = rep_rows + conv_rows
        small_all = small_all.at[:, 0:1].set(late_all[:, 0:1])
        small_all = small_all.at[:, at_x:at_x + 2].set(late_all[:, 8:10])
        small_all = small_all.at[:, at_x + dmod_rows:at_x + dmod_rows + 2].set(late_all[:, 16:18])
        d_conv_shape, dmod_shape = (n_layers, 9, D_FF), (n_layers, 6 * D)
        conv_g, dmx_all, dmc_all, loss_all = _unpack(small_all[:, rep_rows:], [d_conv_shape, dmod_shape, dmod_shape, (1,)])
        out["loss"] = functools.reduce(lambda a, b: a + b, [loss_all[k, 0] for k in range(N_DEV)])

        rep = adamw(_pack([wts[k] for k in REPLICATED], rep_rows), _pack([mom1[k] for k in REPLICATED], rep_rows),
                    _pack([mom2[k] for k in REPLICATED], rep_rows), small_all, "adamw_replicated")
        rep = [_unpack(r, [wts[k].shape for k in REPLICATED]) for r in rep]
        for n, k in enumerate(REPLICATED):
            out[k] = tuple(r[n] for r in rep)

        conv_mine = lax.dynamic_index_in_dim(conv_g.reshape(N_DEV, n_layers, 9, N_DEV, -1), me, axis=3, keepdims=False)
        res = adamw(flat2(ffn_conv_w), flat2(m_ffn_conv_w), flat2(v_ffn_conv_w),
                    conv_mine.reshape(N_DEV, -1, conv_mine.shape[-1]), "adamw_conv_w")
        out["ffn_conv_w"] = tuple(r.reshape(ffn_conv_w.shape) for r in res)

        out["ada_b"] = tuple(adamw(ada_b, m_ada_b, v_ada_b, jnp.concatenate([dmx_all, dmc_all], axis=0), "adamw_ada_b"))

        cols_of = lambda a: lax.dynamic_slice_in_dim(a, me * ada_cols, ada_cols, axis=2).transpose(1, 0, 2)
        d_ada_w, d_cctx = ada_bwd(c_all, cctx8, ada_w, ada_b_cols, cols_of(dmx_all), cols_of(dmc_all))
        res = adamw(flat2(ada_w), flat2(m_ada_w), flat2(v_ada_w), flat2(d_ada_w)[None], "adamw_ada_w")
        out["ada_w"] = tuple(r.reshape(ada_w.shape) for r in res)
        (d_cctx_all,) = all_gather([d_cctx], "gather_c_ctx_grad")
        res = adamw(c_ctx[None], m_c_ctx[None], v_c_ctx[None], d_cctx_all, "adamw_c_ctx")
        out["c_ctx"] = tuple(r[0] for r in res)
        return d_cctx_all

    _, grad_x, _, _, _, small_done = local_step(x[0], ctx[0], loss_target[0], mod, lb, w, fetch, publish, small_ready,
                                                small_early)
    loss = out["loss"]

    finish("in", small_done)
    return (loss, grad_x[None]) + tuple(out[k][n] for n in range(4) for k in WEIGHT_ORDER)
```

```python
import functools
import math

import jax
import jax.numpy as jnp
from jax import lax
from jax.experimental import pallas as pl
from jax.experimental.pallas import tpu as pltpu

F32 = jnp.float32
BF16 = jnp.bfloat16
HIGHEST = lax.Precision.HIGHEST

N_DEV = 8
AXES = ("x", "y", "c")
D = 1024
CTX = 256
TM = 256
CH = 64
SGU_CH = 128
HEADS = 8
HD = 128
GRID_W = 64
D_IN = 9 * D
IN_SLOT = D_IN // N_DEV
D_FF = 2816
FF_SLOT = 2 * D_FF // N_DEV
N_FFK = D_FF // FF_SLOT
RMS_EPS = 1e-6
LN_EPS = 1e-5
ADAM_LR, ADAM_B1, ADAM_B2, ADAM_EPS, ADAM_WD, ADAM_STEP = 0.001, 0.9, 0.999, 1e-08, 0.01, 10
VMEM_LIMIT_V7X = 56 * 2 ** 20
GRAD_WIRE = jnp.bfloat16

VMEM_WHOLE = pl.BlockSpec(memory_space=pltpu.VMEM)
ANY = pl.BlockSpec(memory_space=pl.ANY)


def _cp(n_axes):
    return pltpu.CompilerParams(dimension_semantics=("arbitrary",) * n_axes, vmem_limit_bytes=VMEM_LIMIT_V7X)


def _dot(a, b, dims):
    return lax.dot_general(a.astype(BF16), b.astype(BF16), (dims, ((), ())), preferred_element_type=F32)


@jax.custom_vjp
def mm(a, b):
    return _dot(a, b, ((1,), (0,)))


mm.defvjp(lambda a, b: (mm(a, b), (a, b)),
          lambda r, g: (_dot(g, r[1], ((1,), (1,))).astype(r[0].dtype), _dot(r[0], g, ((0,), (0,))).astype(r[1].dtype)))


@jax.custom_vjp
def mm_nt(a, b):
    return _dot(a, b, ((1,), (1,)))


mm_nt.defvjp(lambda a, b: (mm_nt(a, b), (a, b)),
             lambda r, g: (_dot(g, r[1], ((1,), (0,))).astype(r[0].dtype), _dot(g, r[0], ((0,), (0,))).astype(r[1].dtype)))


@jax.custom_vjp
def mm_tn(a, b):
    return _dot(a, b, ((0,), (0,)))


mm_tn.defvjp(lambda a, b: (mm_tn(a, b), (a, b)),
             lambda r, g: (_dot(r[1], g, ((1,), (1,))).astype(r[0].dtype), _dot(r[0], g, ((1,), (0,))).astype(r[1].dtype)))


def _tri_dot(m, g):
    hi = g.astype(BF16)
    low = (g - hi.astype(F32)).astype(BF16)
    n = g.shape[1]
    out = jnp.dot(m.astype(BF16), jnp.concatenate([hi, low], axis=1), preferred_element_type=F32)
    return out[:, :n] + out[:, n:]


@jax.custom_vjp
def _cum(m, mt, g):
    return _tri_dot(m, g)


_cum.defvjp(lambda m, mt, g: (_cum(m, mt, g), (m, mt)),
            lambda r, d: (jnp.zeros_like(r[0]), jnp.zeros_like(r[1]), _tri_dot(r[1], d)))


def _silu(x):
    return x * jax.nn.sigmoid(x)


def _gelu(x):
    return 0.5 * x * (1.0 + jnp.tanh(math.sqrt(2.0 / math.pi) * (x + 0.044715 * (x * x * x))))


def _rms(x, w):
    return x * lax.rsqrt(jnp.mean(x * x, axis=-1, keepdims=True) + RMS_EPS) * w


def _norm_mod(x, w, shift, scale):
    return _rms(x, w) * (1.0 + scale) + shift


def _hsl(h):
    return slice(h * HD, (h + 1) * HD)


def _hgrn_chunk(st, qz, fz, iv, lb, m, mt, mref):
    hs = range(HEADS)
    keep = [1.0 - lb[h] for h in hs]
    sg = [jax.nn.sigmoid(fz[h]) for h in hs]
    g = [jnp.log(lb[h] + keep[h] * sg[h]) for h in hs]
    k = [keep[h] * (1.0 - sg[h]) for h in hs]
    q = [_silu(qz[h]) for h in hs]
    b = [_cum(m, mt, g[h]) for h in hs]
    ref = [jnp.sum(mref * g[h], axis=0, keepdims=True) for h in hs]
    last = [jnp.sum(g[h], axis=0, keepdims=True) for h in hs]
    qa = [q[h] * jnp.exp(b[h] - ref[h]) for h in hs]
    ka = [k[h] * jnp.exp(ref[h] - b[h]) for h in hs]
    scores = [jnp.where(m > 0.5, mm_nt(qa[h], ka[h]), 0.0) for h in hs]
    inter = [mm_nt(qa[h] * jnp.exp(ref[h]), st[h]) for h in hs]
    kv = [mm_tn(iv[h], ka[h] * jnp.exp(last[h] - ref[h])) for h in hs]
    outs = [mm(scores[h], iv[h]) + inter[h] for h in hs]
    news = [jnp.exp(last[h]) * st[h] + kv[h] for h in hs]
    return outs, news


def _sgu_fn(ub, vb, lnw, lnb, sw, sb):
    gv = [_gelu(v) for v in vb]
    mu = sum(jnp.sum(t, axis=-1, keepdims=True) for t in gv) / D
    var = sum(jnp.sum((t - mu) * (t - mu), axis=-1, keepdims=True) for t in gv) / D
    inv = lax.rsqrt(var + LN_EPS)
    cols = []
    for g in range(HEADS):
        vn = (gv[g] - mu) * inv * lnw[g] + lnb[g]
        cols.append(_gelu(ub[g]) * (mm(sw[g], vn) + sb[g]))
    return jnp.concatenate(cols, axis=1)


def _readout_fn(ob, og, hnw):
    r = [o * lax.rsqrt(jnp.mean(o * o, axis=-1, keepdims=True) + RMS_EPS) * hnw for o in ob]
    return jnp.concatenate(r, axis=1) * _silu(og)


def _glu_fn(ac, v):
    return _gelu(ac) * v


def _stream_row(tm):
    n_ctx = CTX // tm
    return lambda i: (jnp.where(i < n_ctx, 0, 1), 0, 0, 0)


def in_proj_fwd(x, mod, nw, wg):
    t = x.shape[0]

    def body(x_ref, mod_ref, nw_ref, w_ref, out_ref, ht_ref, iv_ref):
        h32 = _norm_mod(x_ref[...], nw_ref[...], mod_ref[0, 0], mod_ref[0, 1])
        ht_ref[...] = h32.T.astype(BF16)
        h = h32.astype(BF16)
        for j in range(N_DEV):
            out_ref[:, j * IN_SLOT:(j + 1) * IN_SLOT] = jnp.dot(h, w_ref[j], preferred_element_type=F32)
        iv_ref[...] = out_ref[:, 3 * D:4 * D].astype(BF16)

    return pl.pallas_call(
        body, name="in_proj_fwd", grid=(t // TM,),
        in_specs=[pl.BlockSpec((TM, D), lambda i: (i, 0)), pl.BlockSpec((1, 6, 1, D), _stream_row(TM)),
                  pl.BlockSpec((1, D), lambda i: (0, 0)), VMEM_WHOLE],
        out_specs=[pl.BlockSpec((TM, D_IN), lambda i: (i, 0)), pl.BlockSpec((D, TM), lambda i: (0, i)),
                   pl.BlockSpec((TM, D), lambda i: (i, 0))],
        out_shape=[jax.ShapeDtypeStruct((t, D_IN), F32), jax.ShapeDtypeStruct((D, t), BF16), jax.ShapeDtypeStruct((t, D), BF16)],
        compiler_params=_cp(1))(x, mod, nw, wg)


SCAN_STEP = 2
SCAN_ROWS = SCAN_STEP * CH


def _scan_block(nb):
    ncb = CTX // SCAN_ROWS

    def block(d, s):
        bwd = jnp.where(s < ncb, ncb - 1 - s, nb + ncb - 1 - s)
        return jnp.where(d == 0, s, bwd)
    return block


def hgrn_fwd(parts, iv, lb, mc, mtc, mrefc):
    t = parts.shape[0]
    nb = t // SCAN_ROWS
    block = _scan_block(nb)

    def body(q_ref, f_ref, i_ref, lb_ref, m_ref, mt_ref, mr_ref, o_ref, ck_ref, st):
        d = pl.program_id(0)

        @pl.when(pl.program_id(1) == 0)
        def _():
            st[...] = jnp.zeros_like(st)

        for c in range(SCAN_STEP):
            rows = pl.ds(pl.multiple_of(jnp.where(d == 0, c * CH, (SCAN_STEP - 1 - c) * CH), CH), CH)
            ck_ref[0, c] = st[...].astype(BF16)
            outs, news = _hgrn_chunk([st[h] for h in range(HEADS)], [q_ref[rows, _hsl(h)] for h in range(HEADS)],
                                     [f_ref[rows, _hsl(h)] for h in range(HEADS)], [i_ref[rows, _hsl(h)] for h in range(HEADS)],
                                     [lb_ref[0, :, _hsl(h)] for h in range(HEADS)], m_ref[0], mt_ref[0], mr_ref[0])
            for h in range(HEADS):
                o_ref[0, rows, _hsl(h)] = outs[h].astype(BF16)
                st[h] = news[h]

    const = lambda d, s: (d, 0, 0)
    at = lambda k: pl.BlockSpec((SCAN_ROWS, D), lambda d, s: (block(d, s), k(d)))
    return pl.pallas_call(
        body, name="hgrn_fwd", grid=(2, nb),
        in_specs=[at(lambda d: 0), at(lambda d: 1 + d), at(lambda d: 0), pl.BlockSpec((1, 1, D), const),
                  pl.BlockSpec((1, CH, CH), const), pl.BlockSpec((1, CH, CH), const), pl.BlockSpec((1, CH, 1), const)],
        out_specs=[pl.BlockSpec((1, SCAN_ROWS, D), lambda d, s: (d, block(d, s), 0)),
                   pl.BlockSpec((1, SCAN_STEP, HEADS, HD, HD), lambda d, s: (d, s, 0, 0, 0))],
        out_shape=[jax.ShapeDtypeStruct((2, t, D), BF16), jax.ShapeDtypeStruct((2, nb * SCAN_STEP, HEADS, HD, HD), BF16)],
        scratch_shapes=[pltpu.VMEM((HEADS, HD, HD), F32)], compiler_params=_cp(2))(parts, parts, iv, lb, mc, mtc, mrefc)


def _mixer_tile(rows, u_ref, v_ref, og_ref, o_ref, lnw_ref, lnb_ref, sw_ref, sb_ref, hnw_ref):
    n = (rows.stop - rows.start) // SGU_CH
    yas, vjps = [], []
    for c in range(n):
        r = slice(rows.start + c * SGU_CH, rows.start + (c + 1) * SGU_CH)
        ya, vjp_a = jax.vjp(_sgu_fn, [u_ref[r, _hsl(g)] for g in range(HEADS)], [v_ref[r, _hsl(g)] for g in range(HEADS)],
                            [lnw_ref[:, _hsl(g)] for g in range(HEADS)], [lnb_ref[:, _hsl(g)] for g in range(HEADS)],
                            [sw_ref[g] for g in range(HEADS)], [sb_ref[g] for g in range(HEADS)])
        yas.append(ya)
        vjps.append(vjp_a)
    yb, vjp_b = jax.vjp(_readout_fn, [o_ref[0, rows, _hsl(h)].astype(F32) + o_ref[1, rows, _hsl(h)].astype(F32)
                                      for h in range(HEADS)],
                        og_ref[rows, :], hnw_ref[...])
    return (yas[0] if n == 1 else jnp.concatenate(yas, axis=0)), yb, vjps, vjp_b


def _part_specs(tm, first, n):
    return [pl.BlockSpec((tm, D), functools.partial(lambda k, i: (i, k), first + k)) for k in range(n)]


def _unless_ctx(skip_ctx, is_ctx, zero_refs, work):
    if not skip_ctx:
        return work()

    @pl.when(is_ctx)
    def _():
        for r in zero_refs:
            r[...] = jnp.zeros_like(r)

    pl.when(jnp.logical_not(is_ctx))(work)


def mixer_fwd(x, parts, o, mod, lnw, lnb, sw, sb, hnw, wa, wb, wo, skip_ctx):
    t = x.shape[0]

    def body(x_ref, u_ref, v_ref, og_ref, ga_ref, gb_ref, o_ref, mod_ref, lnw_ref, lnb_ref, sw_ref, sb_ref, hnw_ref,
             wa_ref, wb_ref, wo_ref, out_ref, pa_ref, pb_ref, y_ref, yat_ref, ybt_ref, mt_ref):
        def work():
            ya, yb, _, _ = _mixer_tile(slice(0, TM), u_ref, v_ref, og_ref, o_ref, lnw_ref, lnb_ref, sw_ref, sb_ref, hnw_ref)
            pa, pb = mm(ya, wa_ref[...]), mm(yb, wb_ref[...])
            merged = jax.nn.sigmoid(ga_ref[...]) * pa + jax.nn.sigmoid(gb_ref[...]) * pb
            y = mm(merged, wo_ref[...])
            out_ref[...] = x_ref[...] + mod_ref[0, 2] * y
            pa_ref[...], pb_ref[...], y_ref[...] = pa.astype(BF16), pb.astype(BF16), y.astype(BF16)
            yat_ref[...], ybt_ref[...], mt_ref[...] = ya.T.astype(BF16), yb.T.astype(BF16), merged.T.astype(BF16)

        _unless_ctx(skip_ctx, pl.program_id(0) == 0, (out_ref, pa_ref, pb_ref, y_ref, yat_ref, ybt_ref, mt_ref), work)

    vec = lambda n: pl.BlockSpec((1, n), lambda i: (0, 0))
    tile = pl.BlockSpec((TM, D), lambda i: (i, 0))
    tile_t = pl.BlockSpec((D, TM), lambda i: (0, i))
    return pl.pallas_call(
        body, name="mixer_fwd", grid=(t // TM,),
        in_specs=[tile] + _part_specs(TM, 4, 5)
        + [pl.BlockSpec((2, TM, D), lambda i: (0, i, 0)), pl.BlockSpec((1, 6, 1, D), _stream_row(TM)), vec(D), vec(D),
           VMEM_WHOLE, VMEM_WHOLE, vec(HD), VMEM_WHOLE, VMEM_WHOLE, VMEM_WHOLE],
        out_specs=[tile] * 4 + [tile_t] * 3,
        out_shape=[jax.ShapeDtypeStruct((t, D), F32)] + [jax.ShapeDtypeStruct((t, D), BF16)] * 3
        + [jax.ShapeDtypeStruct((D, t), BF16)] * 3, compiler_params=_cp(1),
    )(x, parts, parts, parts, parts, parts, o, mod, lnw, lnb, sw, sb, hnw, wa, wb, wo)


def ffn_up_fwd(x, mod, nw, wg, skip_ctx):
    t = x.shape[0]

    def body(x_ref, mod_ref, nw_ref, w_ref, out_ref, ht_ref):
        def work():
            h32 = _norm_mod(x_ref[...], nw_ref[...], mod_ref[0, 3], mod_ref[0, 4])
            ht_ref[...] = h32.T.astype(BF16)
            h = h32.astype(BF16)
            for j in range(N_DEV):
                out_ref[j] = jnp.dot(h, w_ref[j], preferred_element_type=F32)

        _unless_ctx(skip_ctx, pl.program_id(0) == 0, (out_ref, ht_ref), work)

    return pl.pallas_call(
        body, name="ffn_up_fwd", grid=(t // TM,),
        in_specs=[pl.BlockSpec((TM, D), lambda i: (i, 0)), pl.BlockSpec((1, 6, 1, D), _stream_row(TM)),
                  pl.BlockSpec((1, D), lambda i: (0, 0)), VMEM_WHOLE],
        out_specs=[pl.BlockSpec((N_DEV, TM, FF_SLOT), lambda i: (0, i, 0)), pl.BlockSpec((D, TM), lambda i: (0, i))],
        out_shape=[jax.ShapeDtypeStruct((N_DEV, t, FF_SLOT), F32), jax.ShapeDtypeStruct((D, t), BF16)],
        compiler_params=_cp(1))(x, mod, nw, wg)


def _halo_specs(nt, k_of, i_of):
    per = TM // GRID_W
    last = nt * per - 1
    return [pl.BlockSpec((1, GRID_W, FF_SLOT), lambda *g: (k_of(*g), jnp.maximum(i_of(*g) * per - 1, 0), 0)),
            pl.BlockSpec((1, TM, FF_SLOT), lambda *g: (k_of(*g), i_of(*g), 0)),
            pl.BlockSpec((1, GRID_W, FF_SLOT), lambda *g: (k_of(*g), jnp.minimum(i_of(*g) * per + per, last), 0))]


def _with_halo(prev_ref, main_ref, next_ref, i, nt):
    prev = jnp.where(i >= 2, prev_ref[0], 0.0)
    nxt = jnp.where((i >= 1) & (i <= nt - 2), next_ref[0], 0.0)
    return jnp.concatenate([prev, main_ref[0], nxt], axis=0)


def _tap_valid(dc, i, n_rows, offset):
    r = lax.broadcasted_iota(jnp.int32, (n_rows, 1), 0) - offset
    col = jnp.bitwise_and(r, GRID_W - 1)
    pos = jnp.where(i == 0, r, col) + dc
    return (pos >= 0) & (pos < jnp.where(i == 0, TM, GRID_W))


def _row_weight(cw_ref, dr, dc, i):
    w = cw_ref[0, 3 * (dr + 1) + dc + 1:3 * (dr + 1) + dc + 2, :]
    return w if dr == 0 else jnp.where(i == 0, 0.0, w)


def ffn_down_fwd(x, av, mod, cw, cb, wd, skip_ctx):
    t = x.shape[0]
    nt = t // TM
    ext = TM + 2 * GRID_W

    def body(x_ref, ap_ref, am_ref, an_ref, v_ref, mod_ref, cw_ref, cb_ref, wd_ref, out_ref, ac_ref, y_ref, z_ref, acc):
        i, k = pl.program_id(0), pl.program_id(1)

        def work():
            a_ext = _with_halo(ap_ref, am_ref, an_ref, i, nt)
            conv = jnp.zeros((TM, FF_SLOT), F32) + cb_ref[0]
            for dc in (-1, 0, 1):
                col = functools.reduce(lambda p, q: p + q, [a_ext[GRID_W + GRID_W * dr:GRID_W + GRID_W * dr + TM]
                                                            * _row_weight(cw_ref, dr, dc, i) for dr in (-1, 0, 1)])
                conv = conv + (col if dc == 0 else jnp.where(_tap_valid(dc, i, TM, 0), pltpu.roll(col, (-dc) % TM, 0), 0.0))
            ac_ref[0] = conv.astype(BF16)
            z = _glu_fn(conv, v_ref[0]).astype(BF16)
            z_ref[0] = z
            part = mm(z, wd_ref[0])

            @pl.when(k == 0)
            def _():
                acc[...] = part

            @pl.when(k > 0)
            def _():
                acc[...] += part

            @pl.when(k == N_FFK - 1)
            def _():
                y_ref[...] = acc[...]
                out_ref[...] = x_ref[...] + mod_ref[0, 5] * acc[...]

        _unless_ctx(skip_ctx, i == 0, (out_ref, ac_ref, y_ref, z_ref), work)

    tile = pl.BlockSpec((TM, D), lambda i, k: (i, 0))
    return pl.pallas_call(
        body, name="ffn_down_fwd", grid=(nt, N_FFK),
        in_specs=[tile] + _halo_specs(nt, lambda i, k: k, lambda i, k: i)
        + [pl.BlockSpec((1, TM, FF_SLOT), lambda i, k: (N_FFK + k, i, 0)),
           pl.BlockSpec((1, 6, 1, D), lambda i, k: (jnp.where(i < 1, 0, 1), 0, 0, 0)),
           pl.BlockSpec((1, 9, FF_SLOT), lambda i, k: (k, 0, 0)), pl.BlockSpec((1, 1, FF_SLOT), lambda i, k: (k, 0, 0)),
           pl.BlockSpec((1, FF_SLOT, D), lambda i, k: (k, 0, 0))],
        out_specs=[tile, pl.BlockSpec((1, TM, FF_SLOT), lambda i, k: (k, i, 0)), tile,
                   pl.BlockSpec((1, TM, FF_SLOT), lambda i, k: (k, i, 0))],
        out_shape=[jax.ShapeDtypeStruct((t, D), F32), jax.ShapeDtypeStruct((N_FFK, t, FF_SLOT), BF16),
                   jax.ShapeDtypeStruct((t, D), F32), jax.ShapeDtypeStruct((N_FFK, t, FF_SLOT), BF16)],
        scratch_shapes=[pltpu.VMEM((TM, D), F32)], compiler_params=_cp(2))(x, av, av, av, av, mod, cw, cb, wd)


def loss_fwd_bwd(x, target, fw):
    t = x.shape[0]

    def body(x_ref, t_ref, w_ref, loss_ref, dx_ref, dw_ref):
        i = pl.program_id(0)

        @pl.when(i == 0)
        def _():
            loss_ref[...] = jnp.zeros_like(loss_ref)
            dw_ref[...] = jnp.zeros_like(dw_ref)
            dx_ref[...] = jnp.zeros_like(dx_ref)

        @pl.when(i > 0)
        def _():
            y, vjp = jax.vjp(_rms, x_ref[...], w_ref[...])
            err = y - t_ref[...]
            loss_ref[...] += 0.5 * jnp.sum(jnp.sum(err * err, axis=-1, keepdims=True) / D)
            dx, dw = vjp(err / D)
            dx_ref[...] = dx
            dw_ref[...] += dw

    return pl.pallas_call(
        body, name="loss_fwd_bwd", grid=(t // TM,),
        in_specs=[pl.BlockSpec((TM, D), lambda i: (i, 0)), pl.BlockSpec((TM, D), lambda i: (jnp.maximum(i - 1, 0), 0)),
                  pl.BlockSpec((1, D), lambda i: (0, 0))],
        out_specs=[pl.BlockSpec((8, 128), lambda i: (0, 0)), pl.BlockSpec((TM, D), lambda i: (i, 0)),
                   pl.BlockSpec((1, D), lambda i: (0, 0))],
        out_shape=[jax.ShapeDtypeStruct((8, 128), F32), jax.ShapeDtypeStruct((t, D), F32), jax.ShapeDtypeStruct((1, D), F32)],
        compiler_params=_cp(1))(x, target, fw)


def _stream_add(ref, k, is_ctx, val):
    ref[0, k] += jnp.where(is_ctx, val, 0.0)
    ref[1, k] += jnp.where(is_ctx, 0.0, val)


def ffn_down_bwd(dx, ac, av, y, mod, wd, skip_ctx):
    t = dx.shape[0]
    nt = t // TM

    def body(dx_ref, ac_ref, v_ref, y_ref, mod_ref, wd_ref, dav_ref, dac_ref, dout_ref, dg_ref):
        i = pl.program_id(0)

        @pl.when(i == 0)
        def _():
            dg_ref[...] = jnp.zeros_like(dg_ref)

        def work():
            _stream_add(dg_ref, 0, i == 0, jnp.sum(dx_ref[...] * y_ref[...], axis=0, keepdims=True))
            dout = (mod_ref[0, 5] * dx_ref[...]).astype(BF16)
            dout_ref[...] = dout
            for k in range(N_FFK):
                _, vjp = jax.vjp(_glu_fn, ac_ref[k].astype(F32), v_ref[k])
                dac, dv = vjp(mm_nt(dout, wd_ref[k]))
                dac_ref[k] = dac
                dav_ref[k] = dv.astype(BF16)

        _unless_ctx(skip_ctx, i == 0, (dav_ref, dac_ref, dout_ref), work)

    tile = pl.BlockSpec((TM, D), lambda i: (i, 0))
    half = lambda first: pl.BlockSpec((N_FFK, TM, FF_SLOT), lambda i: (first, i, 0))
    return pl.pallas_call(
        body, name="ffn_down_bwd", grid=(nt,),
        in_specs=[tile, half(0), half(1), tile, pl.BlockSpec((1, 6, 1, D), _stream_row(TM)), VMEM_WHOLE],
        out_specs=[half(1), half(0), tile, pl.BlockSpec((2, 1, 1, D), lambda i: (0, 0, 0, 0))],
        out_shape=[jax.ShapeDtypeStruct((N_DEV, t, FF_SLOT), BF16), jax.ShapeDtypeStruct((N_FFK, t, FF_SLOT), F32),
                   jax.ShapeDtypeStruct((t, D), BF16), jax.ShapeDtypeStruct((2, 1, 1, D), F32)],
        compiler_params=_cp(1))(dx, ac, av, y, mod, wd)


def conv_bwd(dav, dac, av, cw, skip_ctx):
    t = dac.shape[1]
    nt = t // TM
    ext = TM + 2 * GRID_W

    def body(dav_in, gp_ref, gm_ref, gn_ref, ap_ref, am_ref, an_ref, cw_ref, dav_ref, dcw_ref, dcb_ref):
        k, i = pl.program_id(0), pl.program_id(1)

        @pl.when(i == 0)
        def _():
            dcw_ref[...] = jnp.zeros_like(dcw_ref)
            dcb_ref[...] = jnp.zeros_like(dcb_ref)

        def work():
            g_ext = _with_halo(gp_ref, gm_ref, gn_ref, i, nt)
            a_ext = _with_halo(ap_ref, am_ref, an_ref, i, nt)
            g_main = gm_ref[0]
            dcb_ref[0] += jnp.sum(g_main, axis=0, keepdims=True)
            da = jnp.zeros((TM, FF_SLOT), F32)
            for dc in (-1, 0, 1):
                valid = _tap_valid(dc, i, TM, 0)
                q = functools.reduce(lambda p, r: p + r, [g_ext[GRID_W - GRID_W * dr:GRID_W - GRID_W * dr + TM]
                                                          * _row_weight(cw_ref, dr, dc, i) for dr in (-1, 0, 1)])
                da = da + (q if dc == 0 else pltpu.roll(jnp.where(valid, q, 0.0), dc % TM, 0))
                g_shift = g_main if dc == 0 else pltpu.roll(jnp.where(valid, g_main, 0.0), dc % TM, 0)
                for dr in (-1, 0, 1):
                    lo = GRID_W + GRID_W * dr
                    tap = 3 * (dr + 1) + dc + 1
                    dw = jnp.sum(g_shift * a_ext[lo:lo + TM], axis=0, keepdims=True)
                    dcw_ref[0, tap:tap + 1, :] += dw if dr == 0 else jnp.where(i == 0, 0.0, dw)
            dav_ref[0] = da.astype(BF16)

        _unless_ctx(skip_ctx, i == 0, (dav_ref,), work)

    return pl.pallas_call(
        body, name="conv_bwd", grid=(N_FFK, nt),
        in_specs=[ANY] + _halo_specs(nt, lambda k, i: k, lambda k, i: i) + _halo_specs(nt, lambda k, i: k, lambda k, i: i)
        + [pl.BlockSpec((1, 9, FF_SLOT), lambda k, i: (k, 0, 0))],
        out_specs=[pl.BlockSpec((1, TM, FF_SLOT), lambda k, i: (k, i, 0)), pl.BlockSpec((1, 9, FF_SLOT), lambda k, i: (k, 0, 0)),
                   pl.BlockSpec((1, 1, FF_SLOT), lambda k, i: (k, 0, 0))],
        out_shape=[jax.ShapeDtypeStruct(dav.shape, BF16), jax.ShapeDtypeStruct((N_FFK, 9, FF_SLOT), F32),
                   jax.ShapeDtypeStruct((N_FFK, 1, FF_SLOT), F32)],
        input_output_aliases={0: 0}, compiler_params=_cp(2))(dav, dac, dac, dac, av, av, av, cw)


def _norm_mod_bwd(x_ref, nw_ref, mod_ref, k_shift, dh, dx_in, dx_ref, dnw_ref, dmod_ref, is_ctx):
    _, vjp = jax.vjp(_norm_mod, x_ref[...], nw_ref[...], mod_ref[0, k_shift], mod_ref[0, k_shift + 1])
    dx, dnw, dshift, dscale = vjp(dh)
    dx_ref[...] = dx_in + dx
    dnw_ref[...] += dnw
    _stream_add(dmod_ref, 0, is_ctx, dshift)
    _stream_add(dmod_ref, 1, is_ctx, dscale)


def ffn_up_bwd_x(dx2, x, dav, mod, nw, wg, skip_ctx):
    t = x.shape[0]

    def body(dx2_ref, x_ref, dav_ref, mod_ref, nw_ref, w_ref, dx_ref, dnw_ref, dmod_ref):
        i = pl.program_id(0)

        @pl.when(i == 0)
        def _():
            dnw_ref[...] = jnp.zeros_like(dnw_ref)
            dmod_ref[...] = jnp.zeros_like(dmod_ref)

        def work():
            dh = mm_nt(dav_ref[0], w_ref[0])
            for j in range(1, N_DEV):
                dh = dh + mm_nt(dav_ref[j], w_ref[j])
            _norm_mod_bwd(x_ref, nw_ref, mod_ref, 3, dh, dx2_ref[...], dx_ref, dnw_ref, dmod_ref, i == 0)

        _unless_ctx(skip_ctx, i == 0, (dx_ref,), work)

    tile = pl.BlockSpec((TM, D), lambda i: (i, 0))
    return pl.pallas_call(
        body, name="ffn_up_bwd_x", grid=(t // TM,),
        in_specs=[tile, tile, pl.BlockSpec((N_DEV, TM, FF_SLOT), lambda i: (0, i, 0)), pl.BlockSpec((1, 6, 1, D), _stream_row(TM)),
                  pl.BlockSpec((1, D), lambda i: (0, 0)), VMEM_WHOLE],
        out_specs=[tile, pl.BlockSpec((1, D), lambda i: (0, 0)), pl.BlockSpec((2, 2, 1, D), lambda i: (0, 0, 0, 0))],
        out_shape=[jax.ShapeDtypeStruct((t, D), F32), jax.ShapeDtypeStruct((1, D), F32), jax.ShapeDtypeStruct((2, 2, 1, D), F32)],
        compiler_params=_cp(1))(dx2, x, dav, mod, nw, wg)


def weight_grad(at, dout, slot, name, after=None):
    rows, t = at.shape
    stacked = dout.ndim == 3
    n = dout.shape[0] if stacked else dout.shape[1] // slot

    def body(a_ref, d_ref, *rest):
        dw_ref = rest[-1]
        dw_ref[0] = jnp.dot(a_ref[...], d_ref[0] if stacked else d_ref[...], preferred_element_type=F32).astype(dw_ref.dtype)

    d_spec = pl.BlockSpec((1, t, slot), lambda j: (j, 0, 0)) if stacked else pl.BlockSpec((t, slot), lambda j: (0, j))
    extra = [] if after is None else [jnp.reshape(after, (1, 1))]
    return pl.pallas_call(
        body, name=name, grid=(n,), in_specs=[VMEM_WHOLE, d_spec] + [ANY] * len(extra),
        out_specs=pl.BlockSpec((1, rows, slot), lambda j: (j, 0, 0)),
        out_shape=jax.ShapeDtypeStruct((n, rows, slot), GRAD_WIRE), compiler_params=_cp(1))(at, dout, *extra)


def weight_grad_rows(at, dout, name):
    n, t, rows = at.shape
    cols = dout.shape[1]

    def body(a_ref, d_ref, dw_ref):
        dw_ref[0] = _dot(a_ref[0], d_ref[...], ((0,), (0,))).astype(dw_ref.dtype)

    return pl.pallas_call(
        body, name=name, grid=(n,), in_specs=[pl.BlockSpec((1, t, rows), lambda k: (k, 0, 0)), VMEM_WHOLE],
        out_specs=pl.BlockSpec((1, rows, cols), lambda k: (k, 0, 0)),
        out_shape=jax.ShapeDtypeStruct((n, rows, cols), GRAD_WIRE), compiler_params=_cp(1))(at, dout)


def mixer_bwd(dx, parts, o, pa, pb, y, mod, lnw, lnb, sw, sb, hnw, wa, wb, wo, skip_ctx):
    t = dx.shape[0]
    tm = TM
    n_ctx = CTX // tm

    def body(dx_ref, u_ref, v_ref, og_ref, ga_ref, gb_ref, o_ref, pa_ref, pb_ref, y_ref, mod_ref, lnw_ref, lnb_ref, sw_ref,
             sb_ref, hnw_ref, wa_ref, wb_ref, wo_ref, dp_ref, do_ref, dy_ref, dpa_ref, dpb_ref, dlnw_ref, dlnb_ref, dsw_ref,
             dsb_ref, dhnw_ref, dg_ref):
        i = pl.program_id(0)

        @pl.when(i == 0)
        def _():
            for r in (dlnw_ref, dlnb_ref, dsw_ref, dsb_ref, dhnw_ref, dg_ref):
                r[...] = jnp.zeros_like(r)

        def work():
            _, _, vjps, vjp_b = _mixer_tile(slice(0, tm), u_ref, v_ref, og_ref, o_ref, lnw_ref, lnb_ref, sw_ref, sb_ref, hnw_ref)
            pa, pb = pa_ref[...].astype(F32), pb_ref[...].astype(F32)
            sa, sbg = jax.nn.sigmoid(ga_ref[...]), jax.nn.sigmoid(gb_ref[...])
            dxv = dx_ref[...]
            _stream_add(dg_ref, 0, i < n_ctx, jnp.sum(dxv * y_ref[...].astype(F32), axis=0, keepdims=True))
            dy = (mod_ref[0, 2] * dxv).astype(BF16)
            dy_ref[...] = dy
            dmerged = mm_nt(dy, wo_ref[...])
            dpa, dpb = (sa * dmerged).astype(BF16), (sbg * dmerged).astype(BF16)
            dpa_ref[...], dpb_ref[...] = dpa, dpb
            first = 4 * D
            dp_ref[:, first + 3 * D:first + 4 * D] = (dmerged * pa * sa * (1.0 - sa)).astype(BF16)
            dp_ref[:, first + 4 * D:first + 5 * D] = (dmerged * pb * sbg * (1.0 - sbg)).astype(BF16)
            dya = mm_nt(dpa, wa_ref[...])
            dob, dog, dhnw = vjp_b(mm_nt(dpb, wb_ref[...]))
            dp_ref[:, first + 2 * D:first + 3 * D] = dog.astype(BF16)
            dhnw_ref[...] += dhnw
            for g in range(HEADS):
                do_ref[:, _hsl(g)] = dob[g]
            for c, vjp_a in enumerate(vjps):
                rows = slice(c * SGU_CH, (c + 1) * SGU_CH)
                dub, dvb, dlnw, dlnb, dsw, dsb = vjp_a(dya[rows])
                for g in range(HEADS):
                    dp_ref[rows, first + g * HD:first + (g + 1) * HD] = dub[g].astype(BF16)
                    dp_ref[rows, first + D + g * HD:first + D + (g + 1) * HD] = dvb[g].astype(BF16)
                    dlnw_ref[:, _hsl(g)] += dlnw[g]
                    dlnb_ref[:, _hsl(g)] += dlnb[g]
                    dsw_ref[g] += dsw[g]
                    dsb_ref[g] += dsb[g]

        _unless_ctx(skip_ctx, i < n_ctx, (dp_ref, do_ref, dy_ref, dpa_ref, dpb_ref), work)

    vec = lambda n: pl.BlockSpec((1, n), lambda i: (0, 0))
    tile = pl.BlockSpec((tm, D), lambda i: (i, 0))
    sds = jax.ShapeDtypeStruct
    return pl.pallas_call(
        body, name="mixer_bwd", grid=(t // tm,),
        in_specs=[tile] + _part_specs(tm, 4, 5)
        + [pl.BlockSpec((2, tm, D), lambda i: (0, i, 0)), tile, tile, tile, pl.BlockSpec((1, 6, 1, D), _stream_row(tm)),
           vec(D), vec(D), VMEM_WHOLE, VMEM_WHOLE, vec(HD), VMEM_WHOLE, VMEM_WHOLE, VMEM_WHOLE],
        out_specs=[pl.BlockSpec((tm, D_IN), lambda i: (i, 0)), tile, tile, tile, tile, vec(D), vec(D),
                   VMEM_WHOLE, VMEM_WHOLE, vec(HD), pl.BlockSpec((2, 1, 1, D), lambda i: (0, 0, 0, 0))],
        out_shape=[sds((t, D_IN), BF16), sds((t, D), F32), sds((t, D), BF16), sds((t, D), BF16), sds((t, D), BF16),
                   sds((1, D), F32), sds((1, D), F32), sds((HEADS, SGU_CH, SGU_CH), F32), sds((HEADS, SGU_CH, 1), F32),
                   sds((1, HD), F32), sds((2, 1, 1, D), F32)],
        compiler_params=_cp(1))(dx, parts, parts, parts, parts, parts, o, pa, pb, y, mod, lnw, lnb, sw, sb, hnw, wa, wb, wo)


def hgrn_bwd(d, parts, lb, mc, mtc, mrefc, ck, do, first=None, dparts=None):
    t = parts.shape[0]
    nb = t // SCAN_ROWS
    block = _scan_block(nb)
    rev = lambda s: block(d, nb - 1 - s)

    def body(q_ref, f_ref, i_ref, lb_ref, m_ref, mt_ref, mr_ref, ck_ref, do_ref, *rest):
        dst = rest[-1]
        dlb_ref = rest[-2]

        @pl.when(pl.program_id(0) == 0)
        def _():
            dst[...] = jnp.zeros_like(dst)
            dlb_ref[...] = jnp.zeros_like(dlb_ref)

        heads = range(HEADS)
        fn = functools.partial(_hgrn_chunk, m=m_ref[0], mt=mt_ref[0], mref=mr_ref[0])
        for c in reversed(range(SCAN_STEP)):
            first_row = c * CH if d == 0 else (SCAN_STEP - 1 - c) * CH
            rows = slice(first_row, first_row + CH)
            _, vjp = jax.vjp(fn, [ck_ref[0, c, h].astype(F32) for h in heads], [q_ref[rows, _hsl(h)] for h in heads],
                             [f_ref[rows, _hsl(h)] for h in heads], [i_ref[rows, _hsl(h)] for h in heads],
                             [lb_ref[0, :, _hsl(h)] for h in heads])
            dstl, dq, df, di, dlb = vjp(([do_ref[rows, _hsl(h)] for h in heads], [dst[h] for h in heads]))
            for h in heads:
                dst[h] = dstl[h]
                dlb_ref[0, :, _hsl(h)] += dlb[h]
                if d == 0:
                    dq_ref, df_ref, di_ref = rest[:3]
                    dq_ref[rows, _hsl(h)] = dq[h].astype(BF16)
                    df_ref[rows, _hsl(h)] = df[h].astype(BF16)
                    di_ref[rows, _hsl(h)] = di[h].astype(BF16)
                else:
                    dq0_ref, df0_ref, di0_ref, _, dp_ref = rest[:5]
                    col = lambda k: slice(k * D + h * HD, k * D + (h + 1) * HD)
                    dp_ref[rows, col(0)] = (dq0_ref[rows, _hsl(h)].astype(F32) + dq[h]).astype(BF16)
                    dp_ref[rows, col(1)] = df0_ref[rows, _hsl(h)]
                    dp_ref[rows, col(2)] = df[h].astype(BF16)
                    dp_ref[rows, col(3)] = (di0_ref[rows, _hsl(h)].astype(F32) + di[h]).astype(BF16)

    const = lambda s: (d, 0, 0)
    at = lambda k: pl.BlockSpec((SCAN_ROWS, D), lambda s: (rev(s), k))
    in_specs = [at(0), at(1 + d), at(3), pl.BlockSpec((1, 1, D), const), pl.BlockSpec((1, CH, CH), const),
                pl.BlockSpec((1, CH, CH), const), pl.BlockSpec((1, CH, 1), const),
                pl.BlockSpec((1, SCAN_STEP, HEADS, HD, HD), lambda s: (d, nb - 1 - s, 0, 0, 0)), at(0)]
    dlb_spec, dlb_shape = pl.BlockSpec((1, 1, D), lambda s: (0, 0, 0)), jax.ShapeDtypeStruct((1, 1, D), F32)
    common = dict(grid=(nb,), scratch_shapes=[pltpu.VMEM((HEADS, HD, HD), F32)], compiler_params=_cp(1))
    if d == 0:
        return pl.pallas_call(body, name="hgrn_bwd_fwd_dir", in_specs=in_specs, out_specs=[at(0)] * 3 + [dlb_spec],
                              out_shape=[jax.ShapeDtypeStruct((t, D), BF16)] * 3 + [dlb_shape], **common,
                              )(parts, parts, parts, lb, mc, mtc, mrefc, ck, do)
    return pl.pallas_call(body, name="hgrn_bwd_bwd_dir", in_specs=in_specs + [at(0)] * 3 + [ANY],
                          out_specs=[pl.BlockSpec((SCAN_ROWS, 4 * D), lambda s: (rev(s), 0)), dlb_spec],
                          out_shape=[jax.ShapeDtypeStruct(dparts.shape, BF16), dlb_shape], input_output_aliases={12: 0},
                          **common)(parts, parts, parts, lb, mc, mtc, mrefc, ck, do, *first, dparts)


def in_proj_bwd_x(dx1, x, dparts, mod, nw, wg, after=None, latent_only=False):
    t = x.shape[0]
    tm = TM
    n_ctx = CTX // tm

    def body(dx1_ref, x_ref, dp_ref, mod_ref, nw_ref, w_ref, *rest):
        dx_ref, dnw_ref, dmod_ref = rest[-3:]
        i = pl.program_id(0)

        @pl.when(i == 0)
        def _():
            dnw_ref[...] = jnp.zeros_like(dnw_ref)
            dmod_ref[...] = jnp.zeros_like(dmod_ref)

        dh = mm_nt(dp_ref[:, 0:IN_SLOT], w_ref[0])
        for j in range(1, N_DEV):
            dh = dh + mm_nt(dp_ref[:, j * IN_SLOT:(j + 1) * IN_SLOT], w_ref[j])
        _norm_mod_bwd(x_ref, nw_ref, mod_ref, 0, dh, dx1_ref[...], dx_ref, dnw_ref, dmod_ref, i < n_ctx)

    tile = pl.BlockSpec((tm, D), lambda i: (i, 0))
    extra = [] if after is None else [jnp.reshape(after, (1, 1))]
    return pl.pallas_call(
        body, name="in_proj_bwd_x", grid=(t // tm,),
        in_specs=[tile, tile, pl.BlockSpec((tm, D_IN), lambda i: (i, 0)), pl.BlockSpec((1, 6, 1, D), _stream_row(tm)),
                  pl.BlockSpec((1, D), lambda i: (0, 0)), VMEM_WHOLE] + [ANY] * len(extra),
        out_specs=[pl.BlockSpec((tm, D), lambda i: (jnp.maximum(i - n_ctx, 0), 0)) if latent_only else tile,
                   pl.BlockSpec((1, D), lambda i: (0, 0)), pl.BlockSpec((2, 2, 1, D), lambda i: (0, 0, 0, 0))],
        out_shape=[jax.ShapeDtypeStruct((t - CTX if latent_only else t, D), F32), jax.ShapeDtypeStruct((1, D), F32),
                   jax.ShapeDtypeStruct((2, 2, 1, D), F32)],
        compiler_params=_cp(1))(dx1, x, dparts, mod, nw, wg, *extra)


def _lb_fn(h0, h1):
    m = jnp.maximum(h0, h1)
    e0, e1 = jnp.exp(h0 - m), jnp.exp(h1 - m)
    return e1 / (e0 + e1)


def lower_bounds(hlb):
    def body(h_ref, out_ref):
        out_ref[...] = _lb_fn(h_ref[0:1, :], h_ref[1:2, :])
    return pl.pallas_call(body, name="lower_bounds", out_shape=jax.ShapeDtypeStruct((1, 2 * D), F32))(hlb)


def lower_bounds_bwd(hlb, dlb1):
    def body(h_ref, d_ref, out_ref):
        _, vjp = jax.vjp(_lb_fn, h_ref[0:1, :], h_ref[1:2, :])
        d0, d1 = vjp(d_ref[...])
        out_ref[0:1, :] = d0
        out_ref[1:2, :] = d1
    return pl.pallas_call(body, name="lower_bounds_bwd", out_shape=jax.ShapeDtypeStruct((2, 2 * D), F32))(hlb, dlb1)


def _ada_fn(c_all, cctx8, w, b):
    dot = lambda a, l: jnp.dot(_silu(a), w[l], precision=HIGHEST, preferred_element_type=F32) + b[l]
    return [dot(c_all, l) for l in range(2)], [dot(cctx8, l) for l in range(2)]


def ada_fwd(c_all, cctx8, w, b):
    cols = w.shape[-1]

    def body(c_ref, cc_ref, w_ref, b_ref, out_ref):
        ox, oc = _ada_fn(c_ref[...], cc_ref[...], [w_ref[0], w_ref[1]], [b_ref[0], b_ref[1]])
        for l in range(2):
            out_ref[l, 0] = ox[l]
            out_ref[l, 1] = oc[l]
    return pl.pallas_call(body, name="ada_fwd", out_shape=jax.ShapeDtypeStruct((2, 2, N_DEV, cols), F32),
                          compiler_params=_cp(0))(c_all, cctx8, w, b)


def ada_bwd(c_all, cctx8, w, b, dmx, dmc):
    cols = w.shape[-1]

    def body(c_ref, cc_ref, w_ref, b_ref, dmx_ref, dmc_ref, dw_ref, dc_ref):
        fn = lambda cc, w0, w1: _ada_fn(c_ref[...], cc, [w0, w1], [b_ref[0], b_ref[1]])
        _, vjp = jax.vjp(fn, cc_ref[...], w_ref[0], w_ref[1])
        dcc, dw0, dw1 = vjp(([dmx_ref[0], dmx_ref[1]], [dmc_ref[0], dmc_ref[1]]))
        dw_ref[0] = dw0
        dw_ref[1] = dw1
        dc_ref[...] = jnp.sum(dcc, axis=0, keepdims=True)
    return pl.pallas_call(body, name="ada_bwd", out_shape=[jax.ShapeDtypeStruct((2, D, cols), F32), jax.ShapeDtypeStruct((1, D), F32)],
                          compiler_params=_cp(0))(c_all, cctx8, w, b, dmx, dmc)


def adamw(w, m, v, gparts, name):
    r, c = w.shape
    p = gparts.shape[0]
    rt = r
    while rt % 16 == 0 and (p + 7) * rt * c * 4 * 2 > 24 * 2 ** 20:
        rt //= 2

    def body(w_ref, m_ref, v_ref, g_ref, go_ref, d_ref, mo_ref, vo_ref):
        g = g_ref[0].astype(F32)
        for k in range(1, p):
            g = g + g_ref[k].astype(F32)
        m2 = ADAM_B1 * m_ref[...] + (1.0 - ADAM_B1) * g
        v2 = ADAM_B2 * v_ref[...] + (1.0 - ADAM_B2) * (g * g)
        m_hat = m2 / (1.0 - ADAM_B1 ** ADAM_STEP)
        v_hat = v2 / (1.0 - ADAM_B2 ** ADAM_STEP)
        go_ref[...] = g
        d_ref[...] = -ADAM_LR * (m_hat / (jnp.sqrt(v_hat) + ADAM_EPS) + ADAM_WD * w_ref[...])
        mo_ref[...] = m2
        vo_ref[...] = v2

    tile = pl.BlockSpec((rt, c), lambda i: (i, 0))
    return pl.pallas_call(
        body, name=name, grid=(r // rt,),
        in_specs=[tile, tile, tile, pl.BlockSpec((p, rt, c), lambda i: (0, i, 0))], out_specs=[tile] * 4,
        out_shape=[jax.ShapeDtypeStruct((r, c), F32)] * 4, compiler_params=_cp(1))(w, m, v, gparts)


def _me():
    x, y, c = lax.axis_index("x"), lax.axis_index("y"), lax.axis_index("c")
    return x, y, c, 4 * x + 2 * y + c


def _peer(x, y, c, p):
    fx, fy, fc = (p >> 2) & 1, (p >> 1) & 1, p & 1
    return (1 - x if fx else x, 1 - y if fy else y, 1 - c if fc else c)


def all_gather(arrs, name, after=None):
    n = len(arrs)
    extra = [] if after is None else list(after) if isinstance(after, (list, tuple)) else [after]

    def body(*refs):
        ins, outs = refs[:n], refs[n + len(extra):2 * n + len(extra)]
        send, recv, local = refs[2 * n + len(extra):]
        x, y, c, me = _me()
        copies = []
        for a in range(n):
            lc = pltpu.make_async_copy(ins[a], outs[a].at[me], local.at[a])
            lc.start()
            copies.append(lc)
            for p in range(1, N_DEV):
                cp = pltpu.make_async_remote_copy(src_ref=ins[a], dst_ref=outs[a].at[me], send_sem=send.at[a, p - 1],
                                                  recv_sem=recv.at[a, p - 1], device_id=_peer(x, y, c, p),
                                                  device_id_type=pl.DeviceIdType.MESH)
                cp.start()
                copies.append(cp)
        for cp in copies:
            cp.wait()

    return pl.pallas_call(
        body, name=name, in_specs=[ANY] * (n + len(extra)), out_specs=[ANY] * n,
        out_shape=[jax.ShapeDtypeStruct((N_DEV,) + a.shape, a.dtype) for a in arrs],
        scratch_shapes=[pltpu.SemaphoreType.DMA((n, N_DEV - 1)), pltpu.SemaphoreType.DMA((n, N_DEV - 1)),
                        pltpu.SemaphoreType.DMA((n,))])(*arrs, *extra)


HBM = pl.BlockSpec(memory_space=pltpu.HBM)
SEM = pl.BlockSpec(memory_space=pltpu.SEMAPHORE)


def _in_hbm(a):
    return pltpu.with_memory_space_constraint(a, pltpu.HBM)


ALL_PEERS = tuple(range(1, N_DEV))
SAME_CORE_AND_SIBLING = (1, 2, 4, 6)
OTHER_CHIPS = (2, 4, 6)


def _exchange_refs(srcs, lands, layer, scatter, a, x, y, c, p, forward=False):
    me = 4 * x + 2 * y + c
    px, py, pc = _peer(x, y, c, p) if p else (x, y, c)
    if forward and p:
        slot = lands[a].at[4 * px + 2 * py + pc]
        return slot, slot, _peer(x, y, c, 1)
    dst = lands[a].at[me] if layer is None else lands[a].at[me, layer]
    src = srcs[a].at[4 * px + 2 * py + pc] if scatter else dst
    return src, dst, (px, py, pc)


def exchange_start(srcs, lands, layer, scatter, name, after=None, peers=ALL_PEERS, forward=False):
    n, ns = len(lands), len(srcs)
    extra = [] if after is None else [after]

    def body(*refs):
        ins, lz = refs[:ns], refs[ns:ns + n]
        send, recv = refs[ns + n + len(extra)], refs[ns + n + len(extra) + 1]
        token = refs[-1]
        x, y, c, _ = _me()
        for a in range(n):
            for p in peers:
                src, dst, peer = _exchange_refs(ins, lz, layer, scatter, a, x, y, c, p, forward)
                k = a * (N_DEV - 1) + p - 1
                pltpu.make_async_remote_copy(src_ref=src, dst_ref=dst, send_sem=send.at[k], recv_sem=recv.at[k],
                                             device_id=peer, device_id_type=pl.DeviceIdType.MESH).start()
        token[...] = jnp.zeros_like(token)

    thru = [pltpu.HBM(a.shape, a.dtype) for a in list(srcs) + list(lands)]
    out = pl.pallas_call(
        body, name=name, in_specs=[HBM] * (ns + n) + [ANY] * len(extra),
        out_specs=[SEM, SEM] + [HBM] * (ns + n) + [pl.BlockSpec(memory_space=pltpu.VMEM)],
        out_shape=[pltpu.SemaphoreType.DMA((n * (N_DEV - 1),)), pltpu.SemaphoreType.DMA((n * (N_DEV - 1),))] + thru
        + [jax.ShapeDtypeStruct((8, 128), F32)],
        input_output_aliases={i: 2 + i for i in range(ns + n)},
        compiler_params=pltpu.CompilerParams(has_side_effects=pltpu.SideEffectType.DATAFLOW_SIDE_EFFECTING),
    )(*[_in_hbm(a) for a in list(srcs) + list(lands)], *extra)
    return out[0], out[1], out[2:2 + ns], out[2 + ns:2 + ns + n], out[-1]


def exchange_wait(send, recv, srcs, lands, layer, scatter, after, name, peers=ALL_PEERS):
    n, ns = len(lands), len(srcs)

    def body(*refs):
        ins, lz = refs[:ns], refs[ns:ns + n]
        send_ref, recv_ref = refs[ns + n], refs[ns + n + 1]
        x, y, c, _ = _me()
        for a in range(n):
            for p in peers:
                src, dst, peer = _exchange_refs(ins, lz, layer, scatter, a, x, y, c, 0)
                k = a * (N_DEV - 1) + p - 1
                cp = pltpu.make_async_remote_copy(src_ref=src, dst_ref=dst, send_sem=send_ref.at[k],
                                                  recv_sem=recv_ref.at[k], device_id=peer,
                                                  device_id_type=pl.DeviceIdType.MESH)
                cp.wait_send()
                cp.wait_recv()

    thru = [pltpu.HBM(a.shape, a.dtype) for a in list(srcs) + list(lands)]
    out = pl.pallas_call(
        body, name=name, in_specs=[HBM] * (ns + n) + [SEM, SEM, ANY], out_specs=[HBM] * (ns + n), out_shape=thru,
        input_output_aliases={i: i for i in range(ns + n)},
        compiler_params=pltpu.CompilerParams(has_side_effects=pltpu.SideEffectType.DATAFLOW_SIDE_EFFECTING),
    )(*srcs, *lands, send, recv, after)
    return out[ns:]


def place_own(src, land, me, layer, scatter, name, src_layer=None):
    create = isinstance(land, jax.ShapeDtypeStruct)
    r, c = src.shape[-2:]
    rt = r
    while rt % 32 == 0 and rt * c * 4 > 2 ** 21:
        rt //= 2

    def body(me_ref, src_ref, *rest):
        out_ref = rest[-1]
        out_ref[...] = src_ref[...].reshape(out_ref.shape).astype(out_ref.dtype)

    src_spec = (pl.BlockSpec((1, rt, c), lambda i, m: (m[0], i, 0)) if scatter else
                pl.BlockSpec((rt, c), lambda i, m: (i, 0)) if src_layer is None else
                pl.BlockSpec((1, rt, c), lambda i, m: (src_layer, i, 0)))
    out_spec = (pl.BlockSpec((1, rt, c), lambda i, m: (m[0], i, 0)) if layer is None
                else pl.BlockSpec((1, 1, rt, c), lambda i, m: (m[0], layer, i, 0)))
    grid_spec = pltpu.PrefetchScalarGridSpec(num_scalar_prefetch=1, grid=(r // rt,),
                                             in_specs=[src_spec] + ([] if create else [ANY]), out_specs=out_spec)
    return pl.pallas_call(body, name=name, grid_spec=grid_spec, out_shape=jax.ShapeDtypeStruct(land.shape, land.dtype),
                          input_output_aliases={} if create else {2: 0}, compiler_params=_cp(1),
                          )(*((me, src) if create else (me, src, land)))


def _scan_constants():
    r = lax.broadcasted_iota(jnp.int32, (CH, CH), 0)
    s = lax.broadcasted_iota(jnp.int32, (CH, CH), 1)
    lower = (s <= r).astype(F32)
    t = jnp.arange(CH)[:, None]
    mc = jnp.stack([lower, lower.T])
    mref = jnp.stack([(t <= CH // 2 - 1).astype(F32), (t >= CH // 2).astype(F32)])
    return mc, jnp.stack([lower.T, lower]), mref


def local_step(x, ctx, target, mod, lb, w, fetch=None, publish=None, small_ready=None, small_early=None):
    kept = {}

    def keep(l, part, grads):
        kept[(l, part)] = grads
        return 0.0

    fetch = fetch or (lambda l, part, after: w)
    publish = publish or keep
    n_layers = len(mod)
    mc, mtc, mrefc = _scan_constants()
    xs = jnp.concatenate([ctx, x], axis=0)
    saved, big = [], []
    for l in range(n_layers):
        wl = dict(fetch(l, "in", xs))
        parts, ht, iv = in_proj_fwd(xs, mod[l], w["nw1"][l], wl["win"][l])
        o, ck = hgrn_fwd(parts, iv, lb[l], mc, mtc, mrefc)
        wl.update(fetch(l, "rest", o))
        last = l == n_layers - 1
        x1, pa, pb, ym, yat, ybt, mt = mixer_fwd(xs, parts, o, mod[l], w["lnw"][l], w["lnb"][l], w["sw"][l], w["sb"][l],
                                                 w["hnw"][l], wl["wa"][l], wl["wb"][l], wl["wo"][l], last)
        av, h2t = ffn_up_fwd(x1, mod[l], w["nw2"][l], wl["wup"][l], last)
        x2, ac, y, z = ffn_down_fwd(x1, av, mod[l], w["cw"][l], w["cb"][l], wl["wd"][l], last)
        saved.append((xs, parts, iv, o, ck, x1, av, ac, y, z, ht, h2t, pa, pb, ym, yat, ybt, mt))
        big.append(wl)
        xs = x2
    loss, dx, dfw = loss_fwd_bwd(xs, target, w["fw"])
    g = {k: [None] * n_layers for k in ("nw1", "nw2", "lnw", "lnb", "sw", "sb", "hnw", "cw", "cb")}
    g["fw"] = dfw
    dmod, dlb = [None] * n_layers, [None] * n_layers
    tok = 0.0
    for l in reversed(range(n_layers)):
        x0, parts, iv, o, ck, x1, av, ac, y, z, ht, h2t, pa, pb, ym, yat, ybt, mt = saved[l]
        wl = big[l]
        last = l == n_layers - 1
        dav, dac, dout, dg2 = ffn_down_bwd(dx, ac, av, y, mod[l] + tok, wl["wd"][l], last)
        dwd = weight_grad_rows(z, dout, "ffn_down_bwd_w")
        dav, g["cw"][l], g["cb"][l] = conv_bwd(dav, dac, av, w["cw"][l], last)
        dx1, g["nw2"][l], dmod2 = ffn_up_bwd_x(dx, x1, dav, mod[l], w["nw2"][l], wl["wup"][l], last)
        dwup = weight_grad(h2t, dav, FF_SLOT, "ffn_up_bwd_w")
        tok = publish(l, "ffn", {"wd": dwd, "wup": dwup})
        (dparts, do, dy, dpa, dpb, g["lnw"][l], g["lnb"][l], g["sw"][l], g["sb"][l], g["hnw"][l],
         dg1) = mixer_bwd(dx1, parts, o, pa, pb, ym, mod[l] + tok, w["lnw"][l], w["lnb"][l], w["sw"][l], w["sb"][l],
                          w["hnw"][l], wl["wa"][l], wl["wb"][l], wl["wo"][l], last)
        tok = publish(l, "mix", {"wa": weight_grad(yat, dpa, D, "mixer_bwd_wa"), "wb": weight_grad(ybt, dpb, D, "mixer_bwd_wb"),
                                 "wo": weight_grad(mt, dy, D, "mixer_bwd_wo")})
        if l == 0 and small_early:
            dmod[0] = jnp.concatenate([jnp.zeros((2, 2, 1, D), F32), dg1, dmod2, dg2], axis=1)
            tok = tok + small_early(loss[0, 0], g, dmod, dlb)
        dq, df, di, dlb_f = hgrn_bwd(0, parts, lb[l] + tok, mc, mtc, mrefc, ck, do)
        dparts, dlb_b = hgrn_bwd(1, parts, lb[l], mc, mtc, mrefc, ck, do, (dq, df, di), dparts)
        dlb[l] = jnp.concatenate([dlb_f, dlb_b], axis=0)
        tok = publish(l, "in", {"win": weight_grad(ht, dparts, IN_SLOT, "in_proj_bwd_w")})
        dx, g["nw1"][l], dmod1 = in_proj_bwd_x(dx1, x0, dparts, mod[l], w["nw1"][l], wl["win"][l], after=tok,
                                               latent_only=l == 0)
        dmod[l] = jnp.concatenate([dmod1, dg1, dmod2, dg2], axis=1)
    done = small_ready(loss[0, 0], g, dmod, dlb) if small_ready else 0.0
    for (l, part), grads in kept.items():
        for k, v in grads.items():
            g.setdefault(k, [None] * n_layers)[l] = v
    return loss[0, 0], dx, g, dmod, dlb, done


ROW = 1024
REPLICATED = ("norm1_w", "sgu_ln_w", "sgu_ln_b", "sgu_w", "sgu_b", "hgrn_lower_bounds", "hgrn_norm_w", "norm2_w",
              "ffn_conv_b", "final_norm_w")
WEIGHT_ORDER = ("c_ctx", "ada_w", "ada_b", "norm1_w", "w_in", "sgu_ln_w", "sgu_ln_b", "sgu_w", "sgu_b", "hgrn_lower_bounds",
                "hgrn_norm_w", "w_branch_a", "w_branch_b", "w_out", "norm2_w", "ffn_w_up", "ffn_conv_w", "ffn_conv_b",
                "ffn_w_down", "final_norm_w")


def _rows_of(n):
    return -(-n // (8 * ROW)) * 8


def _pack(arrs, total_rows=None):
    parts = []
    for a in arrs:
        flat = a.reshape(-1).astype(F32)
        rows = _rows_of(flat.shape[0])
        parts.append(jnp.pad(flat, (0, rows * ROW - flat.shape[0])).reshape(rows, ROW))
    have = sum(p.shape[0] for p in parts)
    if total_rows is not None and total_rows > have:
        parts.append(jnp.zeros((total_rows - have, ROW), F32))
    return jnp.concatenate(parts, axis=0)


def _unpack(packed, shapes):
    lead = packed.shape[:-2]
    out, r0 = [], 0
    for s in shapes:
        n = math.prod(s)
        rows = _rows_of(n)
        out.append(packed[..., r0:r0 + rows, :].reshape(lead + (rows * ROW,))[..., :n].reshape(lead + tuple(s)))
        r0 += rows
    return out


def kernel(x, c, ctx, c_ctx, ada_w, ada_b, norm1_w, w_in, sgu_ln_w, sgu_ln_b, sgu_w, sgu_b, hgrn_lower_bounds, hgrn_norm_w, w_branch_a, w_branch_b, w_out, norm2_w, ffn_w_up, ffn_conv_w, ffn_conv_b, ffn_w_down, final_norm_w, loss_target, m_c_ctx, m_ada_w, m_ada_b, m_norm1_w, m_w_in, m_sgu_ln_w, m_sgu_ln_b, m_sgu_w, m_sgu_b, m_hgrn_lower_bounds, m_hgrn_norm_w, m_w_branch_a, m_w_branch_b, m_w_out, m_norm2_w, m_ffn_w_up, m_ffn_conv_w, m_ffn_conv_b, m_ffn_w_down, m_final_norm_w, v_c_ctx, v_ada_w, v_ada_b, v_norm1_w, v_w_in, v_sgu_ln_w, v_sgu_ln_b, v_sgu_w, v_sgu_b, v_hgrn_lower_bounds, v_hgrn_norm_w, v_w_branch_a, v_w_branch_b, v_w_out, v_norm2_w, v_ffn_w_up, v_ffn_conv_w, v_ffn_conv_b, v_ffn_w_down, v_final_norm_w):
    wts = dict(c_ctx=c_ctx, ada_w=ada_w, ada_b=ada_b, norm1_w=norm1_w, w_in=w_in, sgu_ln_w=sgu_ln_w, sgu_ln_b=sgu_ln_b,
               sgu_w=sgu_w, sgu_b=sgu_b, hgrn_lower_bounds=hgrn_lower_bounds, hgrn_norm_w=hgrn_norm_w, w_branch_a=w_branch_a,
               w_branch_b=w_branch_b, w_out=w_out, norm2_w=norm2_w, ffn_w_up=ffn_w_up, ffn_conv_w=ffn_conv_w,
               ffn_conv_b=ffn_conv_b, ffn_w_down=ffn_w_down, final_norm_w=final_norm_w)
    mom1 = dict(c_ctx=m_c_ctx, ada_w=m_ada_w, ada_b=m_ada_b, norm1_w=m_norm1_w, w_in=m_w_in, sgu_ln_w=m_sgu_ln_w,
                sgu_ln_b=m_sgu_ln_b, sgu_w=m_sgu_w, sgu_b=m_sgu_b, hgrn_lower_bounds=m_hgrn_lower_bounds,
                hgrn_norm_w=m_hgrn_norm_w, w_branch_a=m_w_branch_a, w_branch_b=m_w_branch_b, w_out=m_w_out, norm2_w=m_norm2_w,
                ffn_w_up=m_ffn_w_up, ffn_conv_w=m_ffn_conv_w, ffn_conv_b=m_ffn_conv_b, ffn_w_down=m_ffn_w_down,
                final_norm_w=m_final_norm_w)
    mom2 = dict(c_ctx=v_c_ctx, ada_w=v_ada_w, ada_b=v_ada_b, norm1_w=v_norm1_w, w_in=v_w_in, sgu_ln_w=v_sgu_ln_w,
                sgu_ln_b=v_sgu_ln_b, sgu_w=v_sgu_w, sgu_b=v_sgu_b, hgrn_lower_bounds=v_hgrn_lower_bounds,
                hgrn_norm_w=v_hgrn_norm_w, w_branch_a=v_w_branch_a, w_branch_b=v_w_branch_b, w_out=v_w_out, norm2_w=v_norm2_w,
                ffn_w_up=v_ffn_w_up, ffn_conv_w=v_ffn_conv_w, ffn_conv_b=v_ffn_conv_b, ffn_w_down=v_ffn_w_down,
                final_norm_w=v_final_norm_w)
    n_layers = w_in.shape[0]
    layers = range(n_layers)
    me = 4 * lax.axis_index("x") + 2 * lax.axis_index("y") + lax.axis_index("c")
    ada_cols = ada_w.shape[-1]

    big = ("w_in", "ffn_w_up", "w_branch_a", "w_branch_b", "w_out", "ffn_w_down")
    short = {"w_in": "win", "ffn_w_up": "wup", "w_branch_a": "wa", "w_branch_b": "wb", "w_out": "wo", "ffn_w_down": "wd"}
    me1 = me.reshape(1).astype(jnp.int32)
    groups = [[("w_in", 0)], [(k, 0) for k in big[1:]], [("w_in", 1)], [(k, 1) for k in big[1:]]]
    in_flight, started = [], 0.0

    def own_slots(n):
        return [place_own(wts[k], jax.ShapeDtypeStruct((N_DEV,) + wts[k].shape[1:], BF16), me1, None, False,
                          f"gather_own_{short[k]}_{l}", src_layer=l) for k, l in groups[n]]

    def start_group(n, lands, after):
        in_flight.append(exchange_start([], lands, None, False, f"gather_weights_start_{n}", after=after,
                                        peers=SAME_CORE_AND_SIBLING if n == 0 else ALL_PEERS))
        return in_flight[-1][-1]

    (c_all,) = all_gather([c], "gather_c")
    c_all = c_all.reshape(N_DEV, D)
    token = start_group(0, own_slots(0), c_all)
    later = [own_slots(n) for n in range(1, len(groups))]
    cctx8 = jnp.broadcast_to(c_ctx[None, :], (N_DEV, D))
    ada_b_cols = lax.dynamic_slice_in_dim(ada_b, me * ada_cols, ada_cols, axis=1)[:, None, :]
    mod_cols = ada_fwd(c_all, cctx8, ada_w, ada_b_cols)
    xs = jnp.concatenate([ctx[0], x[0]], axis=0)
    lb1 = lower_bounds(hgrn_lower_bounds)
    mod_all, conv_all = all_gather([mod_cols, ffn_conv_w.reshape(n_layers, 9, -1)], "gather_mod_conv",
                                   after=[token, xs, lb1] + [a for lands in later for a in lands])
    conv_full = [conv_all[:, l].transpose(1, 0, 2).reshape(9, N_FFK, FF_SLOT).transpose(1, 0, 2) for l in layers]
    for n in range(1, len(groups)):
        token = start_group(n, later[n - 1], mod_all if n == 1 else token)
    for started_group in in_flight:
        started = started + started_group[-1][0, 0]

    def as_used(k, a):
        return a if k in ("w_in", "ffn_w_up") else a.reshape(N_FFK, FF_SLOT, D) if k == "ffn_w_down" else a.reshape(D, D)

    arrived = {}

    def fetch(l, part, after):
        n = {(0, "in"): 0, (0, "rest"): 1, (1, "in"): 2, (1, "rest"): 3}.get((l, part))
        if n is not None:
            send, recv, _, lands, _ = in_flight[n]
            first = n == 0
            got = exchange_wait(send, recv, [], lands, None, False, after, f"gather_weights_wait_{n}",
                                peers=SAME_CORE_AND_SIBLING if first else ALL_PEERS)
            if first:
                send, recv, _, lands, _ = exchange_start([], got, None, False, "gather_weights_pass_on", peers=OTHER_CHIPS,
                                                         forward=True)
                got = exchange_wait(send, recv, [], lands, None, False, after, "gather_weights_passed_on", peers=OTHER_CHIPS)
            for (k, ll), a in zip(groups[n], got):
                arrived.setdefault(short[k], [None] * n_layers)[ll] = as_used(k, a)
        return arrived

    mod_x = lax.dynamic_index_in_dim(mod_all[:, :, 0], me, axis=2, keepdims=False)
    mod_c = mod_all[:, :, 1, 0]
    mod = [jnp.stack([mod_c[:, l].reshape(6, 1, D), mod_x[:, l].reshape(6, 1, D)]) for l in layers]
    mod[0] = mod[0] + started

    lb = [jnp.zeros((2, 1, D), F32), lb1.reshape(2, 1, D)]

    w = {
        "nw1": [norm1_w[l][None] for l in layers], "nw2": [norm2_w[l][None] for l in layers],
        "lnw": [sgu_ln_w[l][None] for l in layers], "lnb": [sgu_ln_b[l][None] for l in layers],
        "sw": [sgu_w[l] for l in layers], "sb": [sgu_b[l][:, :, None] for l in layers],
        "hnw": [hgrn_norm_w[l][None] for l in layers], "cw": conv_full,
        "cb": [ffn_conv_b[l].reshape(N_FFK, 1, FF_SLOT) for l in layers], "fw": final_norm_w[None],
    }
    long = {v: k for k, v in short.items()}
    landing, sent = {}, []

    def publish(l, part, grads):
        keys = [long[k] for k in grads]
        slots = [a.reshape((N_DEV, -1, a.shape[-1])) for a in grads.values()]
        zones = [place_own(s, landing.get(k, jax.ShapeDtypeStruct((N_DEV, n_layers) + s.shape[1:], s.dtype)), me1, l, True,
                           f"scatter_own_{short[k]}_{l}") for k, s in zip(keys, slots)]
        send, recv, srcs, zones, token = exchange_start(slots, zones, l, True, f"scatter_grads_start_{part}_{l}")
        landing.update(zip(keys, zones))
        sent.append((keys, l, part, send, recv, srcs, token))
        return token[0, 0]

    out = {}
    flat2 = lambda a: a.reshape(-1, a.shape[-1])

    def finish(part, after):
        done = []
        for keys, l, p, send, recv, srcs, _ in sent:
            if p == part:
                zones = exchange_wait(send, recv, srcs, [landing[k] for k in keys], l, True, after,
                                      f"scatter_grads_wait_{part}_{l}")
                landing.update(zip(keys, zones))
                done = keys
        for k in done:
            r = landing[k]
            res = adamw(flat2(wts[k]), flat2(mom1[k]), flat2(mom2[k]), r.reshape(N_DEV, -1, r.shape[-1]), "adamw_" + k)
            out[k] = tuple(a.reshape(wts[k].shape) for a in res)

    rep_rows = -(-sum(_rows_of(wts[k].size) for k in REPLICATED) // 64) * 64
    conv_rows = _rows_of(n_layers * 9 * D_FF)
    dmod_rows = _rows_of(n_layers * 6 * D)
    early = {}

    def small_early(loss_part, g, dmod, dlb):
        d_hlb = lower_bounds_bwd(hgrn_lower_bounds, dlb[1].reshape(1, 2 * D))
        st = lambda k: jnp.stack([jnp.zeros((1, D), F32) if a is None else a for a in g[k]])
        rep_grads = {"norm1_w": st("nw1"), "sgu_ln_w": st("lnw"), "sgu_ln_b": st("lnb"), "sgu_w": st("sw"), "sgu_b": st("sb"),
                     "hgrn_lower_bounds": d_hlb, "hgrn_norm_w": st("hnw"), "norm2_w": st("nw2"), "ffn_conv_b": st("cb"),
                     "final_norm_w": g["fw"]}
        d_conv = jnp.stack([g["cw"][l].transpose(1, 0, 2).reshape(9, D_FF) for l in layers])
        dmod_x = jnp.stack([dmod[l][1].reshape(6 * D) for l in layers])
        dmod_c = jnp.stack([dmod[l][0].reshape(6 * D) for l in layers])
        small = jnp.concatenate([_pack([rep_grads[k] for k in REPLICATED], rep_rows),
                                 _pack([d_conv, dmod_x, dmod_c, loss_part.reshape(1)])], axis=0)
        zone = place_own(small, jax.ShapeDtypeStruct((N_DEV,) + small.shape, F32), me1, None, False, "gather_small_own")
        early["send"], early["recv"], _, early["zones"], token = exchange_start([], [zone], None, False, "gather_small_start")
        return token[0, 0]

    def small_ready(loss_part, g, dmod, dlb):
        late = _pack([g["nw1"][0], dmod[0][1, 0:2], dmod[0][0, 0:2]])
        for part in ("ffn", "mix"):
            finish(part, late)
        (late_all,) = all_gather([late], "gather_small_late", after=out["w_out"][0])
        (small_all,) = exchange_wait(early["send"], early["recv"], [], early["zones"], None, False, late_all,
                                     "gather_small_wait")
        at_x = rep_rows + conv_rows
        small_all = small_all.at[:, 0:1].set(late_all[:, 0:1])
        small_all = small_all.at[:, at_x:at_x + 2].set(late_all[:, 8:10])
        small_all = small_all.at[:, at_x + dmod_rows:at_x + dmod_rows + 2].set(late_all[:, 16:18])
        d_conv_shape, dmod_shape = (n_layers, 9, D_FF), (n_layers, 6 * D)
        conv_g, dmx_all, dmc_all, loss_all = _unpack(small_all[:, rep_rows:], [d_conv_shape, dmod_shape, dmod_shape, (1,)])
        out["loss"] = functools.reduce(lambda a, b: a + b, [loss_all[k, 0] for k in range(N_DEV)])

        rep = adamw(_pack([wts[k] for k in REPLICATED], rep_rows), _pack([mom1[k] for k in REPLICATED], rep_rows),
                    _pack([mom2[k] for k in REPLICATED], rep_rows), small_all, "adamw_replicated")
        rep = [_unpack(r, [wts[k].shape for k in REPLICATED]) for r in rep]
        for n, k in enumerate(REPLICATED):
            out[k] = tuple(r[n] for r in rep)

        conv_mine = lax.dynamic_index_in_dim(conv_g.reshape(N_DEV, n_layers, 9, N_DEV, -1), me, axis=3, keepdims=False)
        res = adamw(flat2(ffn_conv_w), flat2(m_ffn_conv_w), flat2(v_ffn_conv_w),
                    conv_mine.reshape(N_DEV, -1, conv_mine.shape[-1]), "adamw_conv_w")
        out["ffn_conv_w"] = tuple(r.reshape(ffn_conv_w.shape) for r in res)

        out["ada_b"] = tuple(adamw(ada_b, m_ada_b, v_ada_b, jnp.concatenate([dmx_all, dmc_all], axis=0), "adamw_ada_b"))

        cols_of = lambda a: lax.dynamic_slice_in_dim(a, me * ada_cols, ada_cols, axis=2).transpose(1, 0, 2)
        d_ada_w, d_cctx = ada_bwd(c_all, cctx8, ada_w, ada_b_cols, cols_of(dmx_all), cols_of(dmc_all))
        res = adamw(flat2(ada_w), flat2(m_ada_w), flat2(v_ada_w), flat2(d_ada_w)[None], "adamw_ada_w")
        out["ada_w"] = tuple(r.reshape(ada_w.shape) for r in res)
        (d_cctx_all,) = all_gather([d_cctx], "gather_c_ctx_grad")
        res = adamw(c_ctx[None], m_c_ctx[None], v_c_ctx[None], d_cctx_all, "adamw_c_ctx")
        out["c_ctx"] = tuple(r[0] for r in res)
        return d_cctx_all

    _, grad_x, _, _, _, small_done = local_step(x[0], ctx[0], loss_target[0], mod, lb, w, fetch, publish, small_ready,
                                                small_early)
    loss = out["loss"]

    finish("in", small_done)
    return (loss, grad_x[None]) + tuple(out[k][n] for n in range(4) for k in WEIGHT_ORDER)
```

```python
import functools
import math

import jax
import jax.numpy as jnp
from jax import lax
from jax.experimental import pallas as pl
from jax.experimental.pallas import tpu as pltpu

F32 = jnp.float32
BF16 = jnp.bfloat16
HIGHEST = lax.Precision.HIGHEST

N_DEV = 8
AXES = ("x", "y", "c")
D = 1024
CTX = 256
TM = 256
CH = 64
SGU_CH = 128
HEADS = 8
HD = 128
GRID_W = 64
D_IN = 9 * D
IN_SLOT = D_IN // N_DEV
D_FF = 2816
FF_SLOT = 2 * D_FF // N_DEV
N_FFK = D_FF // FF_SLOT
RMS_EPS = 1e-6
LN_EPS = 1e-5
ADAM_LR, ADAM_B1, ADAM_B2, ADAM_EPS, ADAM_WD, ADAM_STEP = 0.001, 0.9, 0.999, 1e-08, 0.01, 10
VMEM_LIMIT_V7X = 56 * 2 ** 20
GRAD_WIRE = jnp.bfloat16

VMEM_WHOLE = pl.BlockSpec(memory_space=pltpu.VMEM)
ANY = pl.BlockSpec(memory_space=pl.ANY)


def _cp(n_axes):
    return pltpu.CompilerParams(dimension_semantics=("arbitrary",) * n_axes, vmem_limit_bytes=VMEM_LIMIT_V7X)


def _dot(a, b, dims):
    return lax.dot_general(a.astype(BF16), b.astype(BF16), (dims, ((), ())), preferred_element_type=F32)


@jax.custom_vjp
def mm(a, b):
    return _dot(a, b, ((1,), (0,)))


mm.defvjp(lambda a, b: (mm(a, b), (a, b)),
          lambda r, g: (_dot(g, r[1], ((1,), (1,))).astype(r[0].dtype), _dot(r[0], g, ((0,), (0,))).astype(r[1].dtype)))


@jax.custom_vjp
def mm_nt(a, b):
    return _dot(a, b, ((1,), (1,)))


mm_nt.defvjp(lambda a, b: (mm_nt(a, b), (a, b)),
             lambda r, g: (_dot(g, r[1], ((1,), (0,))).astype(r[0].dtype), _dot(g, r[0], ((0,), (0,))).astype(r[1].dtype)))


@jax.custom_vjp
def mm_tn(a, b):
    return _dot(a, b, ((0,), (0,)))


mm_tn.defvjp(lambda a, b: (mm_tn(a, b), (a, b)),
             lambda r, g: (_dot(r[1], g, ((1,), (1,))).astype(r[0].dtype), _dot(r[0], g, ((1,), (0,))).astype(r[1].dtype)))


def _tri_dot(m, g):
    hi = g.astype(BF16)
    low = (g - hi.astype(F32)).astype(BF16)
    n = g.shape[1]
    out = jnp.dot(m.astype(BF16), jnp.concatenate([hi, low], axis=1), preferred_element_type=F32)
    return out[:, :n] + out[:, n:]


@jax.custom_vjp
def _cum(m, mt, g):
    return _tri_dot(m, g)


_cum.defvjp(lambda m, mt, g: (_cum(m, mt, g), (m, mt)),
            lambda r, d: (jnp.zeros_like(r[0]), jnp.zeros_like(r[1]), _tri_dot(r[1], d)))


def _silu(x):
    return x * jax.nn.sigmoid(x)


def _gelu(x):
    return 0.5 * x * (1.0 + jnp.tanh(math.sqrt(2.0 / math.pi) * (x + 0.044715 * (x * x * x))))


def _rms(x, w):
    return x * lax.rsqrt(jnp.mean(x * x, axis=-1, keepdims=True) + RMS_EPS) * w


def _norm_mod(x, w, shift, scale):
    return _rms(x, w) * (1.0 + scale) + shift


def _hsl(h):
    return slice(h * HD, (h + 1) * HD)


def _hgrn_chunk(st, qz, fz, iv, lb, m, mt, mref):
    hs = range(HEADS)
    keep = [1.0 - lb[h] for h in hs]
    sg = [jax.nn.sigmoid(fz[h]) for h in hs]
    g = [jnp.log(lb[h] + keep[h] * sg[h]) for h in hs]
    k = [keep[h] * (1.0 - sg[h]) for h in hs]
    q = [_silu(qz[h]) for h in hs]
    b = [_cum(m, mt, g[h]) for h in hs]
    ref = [jnp.sum(mref * g[h], axis=0, keepdims=True) for h in hs]
    last = [jnp.sum(g[h], axis=0, keepdims=True) for h in hs]
    qa = [q[h] * jnp.exp(b[h] - ref[h]) for h in hs]
    ka = [k[h] * jnp.exp(ref[h] - b[h]) for h in hs]
    scores = [jnp.where(m > 0.5, mm_nt(qa[h], ka[h]), 0.0) for h in hs]
    inter = [mm_nt(qa[h] * jnp.exp(ref[h]), st[h]) for h in hs]
    kv = [mm_tn(iv[h], ka[h] * jnp.exp(last[h] - ref[h])) for h in hs]
    outs = [mm(scores[h], iv[h]) + inter[h] for h in hs]
    news = [jnp.exp(last[h]) * st[h] + kv[h] for h in hs]
    return outs, news


def _sgu_fn(ub, vb, lnw, lnb, sw, sb):
    gv = [_gelu(v) for v in vb]
    mu = sum(jnp.sum(t, axis=-1, keepdims=True) for t in gv) / D
    var = sum(jnp.sum((t - mu) * (t - mu), axis=-1, keepdims=True) for t in gv) / D
    inv = lax.rsqrt(var + LN_EPS)
    cols = []
    for g in range(HEADS):
        vn = (gv[g] - mu) * inv * lnw[g] + lnb[g]
        cols.append(_gelu(ub[g]) * (mm(sw[g], vn) + sb[g]))
    return jnp.concatenate(cols, axis=1)


def _readout_fn(ob, og, hnw):
    r = [o * lax.rsqrt(jnp.mean(o * o, axis=-1, keepdims=True) + RMS_EPS) * hnw for o in ob]
    return jnp.concatenate(r, axis=1) * _silu(og)


def _glu_fn(ac, v):
    return _gelu(ac) * v


def _stream_row(tm):
    n_ctx = CTX // tm
    return lambda i: (jnp.where(i < n_ctx, 0, 1), 0, 0, 0)


def in_proj_fwd(x, mod, nw, wg):
    t = x.shape[0]

    def body(x_ref, mod_ref, nw_ref, w_ref, out_ref, ht_ref, iv_ref):
        h32 = _norm_mod(x_ref[...], nw_ref[...], mod_ref[0, 0], mod_ref[0, 1])
        ht_ref[...] = h32.T.astype(BF16)
        h = h32.astype(BF16)
        for j in range(N_DEV):
            out_ref[:, j * IN_SLOT:(j + 1) * IN_SLOT] = jnp.dot(h, w_ref[j], preferred_element_type=F32)
        iv_ref[...] = out_ref[:, 3 * D:4 * D].astype(BF16)

    return pl.pallas_call(
        body, name="in_proj_fwd", grid=(t // TM,),
        in_specs=[pl.BlockSpec((TM, D), lambda i: (i, 0)), pl.BlockSpec((1, 6, 1, D), _stream_row(TM)),
                  pl.BlockSpec((1, D), lambda i: (0, 0)), VMEM_WHOLE],
        out_specs=[pl.BlockSpec((TM, D_IN), lambda i: (i, 0)), pl.BlockSpec((D, TM), lambda i: (0, i)),
                   pl.BlockSpec((TM, D), lambda i: (i, 0))],
        out_shape=[jax.ShapeDtypeStruct((t, D_IN), F32), jax.ShapeDtypeStruct((D, t), BF16), jax.ShapeDtypeStruct((t, D), BF16)],
        compiler_params=_cp(1))(x, mod, nw, wg)


SCAN_STEP = 4
SCAN_ROWS = SCAN_STEP * CH


def _scan_block(nb):
    ncb = CTX // SCAN_ROWS

    def block(d, s):
        bwd = jnp.where(s < ncb, ncb - 1 - s, nb + ncb - 1 - s)
        return jnp.where(d == 0, s, bwd)
    return block


def hgrn_fwd(parts, iv, lb, mc, mtc, mrefc):
    t = parts.shape[0]
    nb = t // SCAN_ROWS
    block = _scan_block(nb)

    def body(q_ref, f_ref, i_ref, lb_ref, m_ref, mt_ref, mr_ref, o_ref, ck_ref, st):
        d = pl.program_id(0)

        @pl.when(pl.program_id(1) == 0)
        def _():
            st[...] = jnp.zeros_like(st)

        for c in range(SCAN_STEP):
            rows = pl.ds(pl.multiple_of(jnp.where(d == 0, c * CH, (SCAN_STEP - 1 - c) * CH), CH), CH)
            ck_ref[0, c] = st[...].astype(BF16)
            outs, news = _hgrn_chunk([st[h] for h in range(HEADS)], [q_ref[rows, _hsl(h)] for h in range(HEADS)],
                                     [f_ref[rows, _hsl(h)] for h in range(HEADS)], [i_ref[rows, _hsl(h)] for h in range(HEADS)],
                                     [lb_ref[0, :, _hsl(h)] for h in range(HEADS)], m_ref[0], mt_ref[0], mr_ref[0])
            for h in range(HEADS):
                o_ref[0, rows, _hsl(h)] = outs[h].astype(BF16)
                st[h] = news[h]

    const = lambda d, s: (d, 0, 0)
    at = lambda k: pl.BlockSpec((SCAN_ROWS, D), lambda d, s: (block(d, s), k(d)))
    return pl.pallas_call(
        body, name="hgrn_fwd", grid=(2, nb),
        in_specs=[at(lambda d: 0), at(lambda d: 1 + d), at(lambda d: 0), pl.BlockSpec((1, 1, D), const),
                  pl.BlockSpec((1, CH, CH), const), pl.BlockSpec((1, CH, CH), const), pl.BlockSpec((1, CH, 1), const)],
        out_specs=[pl.BlockSpec((1, SCAN_ROWS, D), lambda d, s: (d, block(d, s), 0)),
                   pl.BlockSpec((1, SCAN_STEP, HEADS, HD, HD), lambda d, s: (d, s, 0, 0, 0))],
        out_shape=[jax.ShapeDtypeStruct((2, t, D), BF16), jax.ShapeDtypeStruct((2, nb * SCAN_STEP, HEADS, HD, HD), BF16)],
        scratch_shapes=[pltpu.VMEM((HEADS, HD, HD), F32)], compiler_params=_cp(2))(parts, parts, iv, lb, mc, mtc, mrefc)


def _mixer_tile(rows, u_ref, v_ref, og_ref, o_ref, lnw_ref, lnb_ref, sw_ref, sb_ref, hnw_ref):
    n = (rows.stop - rows.start) // SGU_CH
    yas, vjps = [], []
    for c in range(n):
        r = slice(rows.start + c * SGU_CH, rows.start + (c + 1) * SGU_CH)
        ya, vjp_a = jax.vjp(_sgu_fn, [u_ref[r, _hsl(g)] for g in range(HEADS)], [v_ref[r, _hsl(g)] for g in range(HEADS)],
                            [lnw_ref[:, _hsl(g)] for g in range(HEADS)], [lnb_ref[:, _hsl(g)] for g in range(HEADS)],
                            [sw_ref[g] for g in range(HEADS)], [sb_ref[g] for g in range(HEADS)])
        yas.append(ya)
        vjps.append(vjp_a)
    yb, vjp_b = jax.vjp(_readout_fn, [o_ref[0, rows, _hsl(h)].astype(F32) + o_ref[1, rows, _hsl(h)].astype(F32)
                                      for h in range(HEADS)],
                        og_ref[rows, :], hnw_ref[...])
    return (yas[0] if n == 1 else jnp.concatenate(yas, axis=0)), yb, vjps, vjp_b


def _part_specs(tm, first, n):
    return [pl.BlockSpec((tm, D), functools.partial(lambda k, i: (i, k), first + k)) for k in range(n)]


def _unless_ctx(skip_ctx, is_ctx, zero_refs, work):
    if not skip_ctx:
        return work()

    @pl.when(is_ctx)
    def _():
        for r in zero_refs:
            r[...] = jnp.zeros_like(r)

    pl.when(jnp.logical_not(is_ctx))(work)


def mixer_fwd(x, parts, o, mod, lnw, lnb, sw, sb, hnw, wa, wb, wo, skip_ctx):
    t = x.shape[0]

    def body(x_ref, u_ref, v_ref, og_ref, ga_ref, gb_ref, o_ref, mod_ref, lnw_ref, lnb_ref, sw_ref, sb_ref, hnw_ref,
             wa_ref, wb_ref, wo_ref, out_ref, pa_ref, pb_ref, y_ref, yat_ref, ybt_ref, mt_ref):
        def work():
            ya, yb, _, _ = _mixer_tile(slice(0, TM), u_ref, v_ref, og_ref, o_ref, lnw_ref, lnb_ref, sw_ref, sb_ref, hnw_ref)
            pa, pb = mm(ya, wa_ref[...]), mm(yb, wb_ref[...])
            merged = jax.nn.sigmoid(ga_ref[...]) * pa + jax.nn.sigmoid(gb_ref[...]) * pb
            y = mm(merged, wo_ref[...])
            out_ref[...] = x_ref[...] + mod_ref[0, 2] * y
            pa_ref[...], pb_ref[...], y_ref[...] = pa.astype(BF16), pb.astype(BF16), y.astype(BF16)
            yat_ref[...], ybt_ref[...], mt_ref[...] = ya.T.astype(BF16), yb.T.astype(BF16), merged.T.astype(BF16)

        _unless_ctx(skip_ctx, pl.program_id(0) == 0, (out_ref, pa_ref, pb_ref, y_ref, yat_ref, ybt_ref, mt_ref), work)

    vec = lambda n: pl.BlockSpec((1, n), lambda i: (0, 0))
    tile = pl.BlockSpec((TM, D), lambda i: (i, 0))
    tile_t = pl.BlockSpec((D, TM), lambda i: (0, i))
    return pl.pallas_call(
        body, name="mixer_fwd", grid=(t // TM,),
        in_specs=[tile] + _part_specs(TM, 4, 5)
        + [pl.BlockSpec((2, TM, D), lambda i: (0, i, 0)), pl.BlockSpec((1, 6, 1, D), _stream_row(TM)), vec(D), vec(D),
           VMEM_WHOLE, VMEM_WHOLE, vec(HD), VMEM_WHOLE, VMEM_WHOLE, VMEM_WHOLE],
        out_specs=[tile] * 4 + [tile_t] * 3,
        out_shape=[jax.ShapeDtypeStruct((t, D), F32)] + [jax.ShapeDtypeStruct((t, D), BF16)] * 3
        + [jax.ShapeDtypeStruct((D, t), BF16)] * 3, compiler_params=_cp(1),
    )(x, parts, parts, parts, parts, parts, o, mod, lnw, lnb, sw, sb, hnw, wa, wb, wo)


def ffn_up_fwd(x, mod, nw, wg, skip_ctx):
    t = x.shape[0]

    def body(x_ref, mod_ref, nw_ref, w_ref, out_ref, ht_ref):
        def work():
            h32 = _norm_mod(x_ref[...], nw_ref[...], mod_ref[0, 3], mod_ref[0, 4])
            ht_ref[...] = h32.T.astype(BF16)
            h = h32.astype(BF16)
            for j in range(N_DEV):
                out_ref[j] = jnp.dot(h, w_ref[j], preferred_element_type=F32)

        _unless_ctx(skip_ctx, pl.program_id(0) == 0, (out_ref, ht_ref), work)

    return pl.pallas_call(
        body, name="ffn_up_fwd", grid=(t // TM,),
        in_specs=[pl.BlockSpec((TM, D), lambda i: (i, 0)), pl.BlockSpec((1, 6, 1, D), _stream_row(TM)),
                  pl.BlockSpec((1, D), lambda i: (0, 0)), VMEM_WHOLE],
        out_specs=[pl.BlockSpec((N_DEV, TM, FF_SLOT), lambda i: (0, i, 0)), pl.BlockSpec((D, TM), lambda i: (0, i))],
        out_shape=[jax.ShapeDtypeStruct((N_DEV, t, FF_SLOT), F32), jax.ShapeDtypeStruct((D, t), BF16)],
        compiler_params=_cp(1))(x, mod, nw, wg)


def _halo_specs(nt, k_of, i_of):
    per = TM // GRID_W
    last = nt * per - 1
    return [pl.BlockSpec((1, GRID_W, FF_SLOT), lambda *g: (k_of(*g), jnp.maximum(i_of(*g) * per - 1, 0), 0)),
            pl.BlockSpec((1, TM, FF_SLOT), lambda *g: (k_of(*g), i_of(*g), 0)),
            pl.BlockSpec((1, GRID_W, FF_SLOT), lambda *g: (k_of(*g), jnp.minimum(i_of(*g) * per + per, last), 0))]


def _with_halo(prev_ref, main_ref, next_ref, i, nt):
    prev = jnp.where(i >= 2, prev_ref[0], 0.0)
    nxt = jnp.where((i >= 1) & (i <= nt - 2), next_ref[0], 0.0)
    return jnp.concatenate([prev, main_ref[0], nxt], axis=0)


def _tap_valid(dc, i, n_rows, offset):
    r = lax.broadcasted_iota(jnp.int32, (n_rows, 1), 0) - offset
    col = jnp.bitwise_and(r, GRID_W - 1)
    pos = jnp.where(i == 0, r, col) + dc
    return (pos >= 0) & (pos < jnp.where(i == 0, TM, GRID_W))


def _row_weight(cw_ref, dr, dc, i):
    w = cw_ref[0, 3 * (dr + 1) + dc + 1:3 * (dr + 1) + dc + 2, :]
    return w if dr == 0 else jnp.where(i == 0, 0.0, w)


def ffn_down_fwd(x, av, mod, cw, cb, wd, skip_ctx):
    t = x.shape[0]
    nt = t // TM
    ext = TM + 2 * GRID_W

    def body(x_ref, ap_ref, am_ref, an_ref, v_ref, mod_ref, cw_ref, cb_ref, wd_ref, out_ref, ac_ref, y_ref, z_ref, acc):
        i, k = pl.program_id(0), pl.program_id(1)

        def work():
            a_ext = _with_halo(ap_ref, am_ref, an_ref, i, nt)
            conv = jnp.zeros((TM, FF_SLOT), F32) + cb_ref[0]
            for dc in (-1, 0, 1):
                col = functools.reduce(lambda p, q: p + q, [a_ext[GRID_W + GRID_W * dr:GRID_W + GRID_W * dr + TM]
                                                            * _row_weight(cw_ref, dr, dc, i) for dr in (-1, 0, 1)])
                conv = conv + (col if dc == 0 else jnp.where(_tap_valid(dc, i, TM, 0), pltpu.roll(col, (-dc) % TM, 0), 0.0))
            ac_ref[0] = conv.astype(BF16)
            z = _glu_fn(conv, v_ref[0]).astype(BF16)
            z_ref[0] = z
            part = mm(z, wd_ref[0])

            @pl.when(k == 0)
            def _():
                acc[...] = part

            @pl.when(k > 0)
            def _():
                acc[...] += part

            @pl.when(k == N_FFK - 1)
            def _():
                y_ref[...] = acc[...]
                out_ref[...] = x_ref[...] + mod_ref[0, 5] * acc[...]

        _unless_ctx(skip_ctx, i == 0, (out_ref, ac_ref, y_ref, z_ref), work)

    tile = pl.BlockSpec((TM, D), lambda i, k: (i, 0))
    return pl.pallas_call(
        body, name="ffn_down_fwd", grid=(nt, N_FFK),
        in_specs=[tile] + _halo_specs(nt, lambda i, k: k, lambda i, k: i)
        + [pl.BlockSpec((1, TM, FF_SLOT), lambda i, k: (N_FFK + k, i, 0)),
           pl.BlockSpec((1, 6, 1, D), lambda i, k: (jnp.where(i < 1, 0, 1), 0, 0, 0)),
           pl.BlockSpec((1, 9, FF_SLOT), lambda i, k: (k, 0, 0)), pl.BlockSpec((1, 1, FF_SLOT), lambda i, k: (k, 0, 0)),
           pl.BlockSpec((1, FF_SLOT, D), lambda i, k: (k, 0, 0))],
        out_specs=[tile, pl.BlockSpec((1, TM, FF_SLOT), lambda i, k: (k, i, 0)), tile,
                   pl.BlockSpec((1, TM, FF_SLOT), lambda i, k: (k, i, 0))],
        out_shape=[jax.ShapeDtypeStruct((t, D), F32), jax.ShapeDtypeStruct((N_FFK, t, FF_SLOT), BF16),
                   jax.ShapeDtypeStruct((t, D), F32), jax.ShapeDtypeStruct((N_FFK, t, FF_SLOT), BF16)],
        scratch_shapes=[pltpu.VMEM((TM, D), F32)], compiler_params=_cp(2))(x, av, av, av, av, mod, cw, cb, wd)


def loss_fwd_bwd(x, target, fw):
    t = x.shape[0]

    def body(x_ref, t_ref, w_ref, loss_ref, dx_ref, dw_ref):
        i = pl.program_id(0)

        @pl.when(i == 0)
        def _():
            loss_ref[...] = jnp.zeros_like(loss_ref)
            dw_ref[...] = jnp.zeros_like(dw_ref)
            dx_ref[...] = jnp.zeros_like(dx_ref)

        @pl.when(i > 0)
        def _():
            y, vjp = jax.vjp(_rms, x_ref[...], w_ref[...])
            err = y - t_ref[...]
            loss_ref[...] += 0.5 * jnp.sum(jnp.sum(err * err, axis=-1, keepdims=True) / D)
            dx, dw = vjp(err / D)
            dx_ref[...] = dx
            dw_ref[...] += dw

    return pl.pallas_call(
        body, name="loss_fwd_bwd", grid=(t // TM,),
        in_specs=[pl.BlockSpec((TM, D), lambda i: (i, 0)), pl.BlockSpec((TM, D), lambda i: (jnp.maximum(i - 1, 0), 0)),
                  pl.BlockSpec((1, D), lambda i: (0, 0))],
        out_specs=[pl.BlockSpec((8, 128), lambda i: (0, 0)), pl.BlockSpec((TM, D), lambda i: (i, 0)),
                   pl.BlockSpec((1, D), lambda i: (0, 0))],
        out_shape=[jax.ShapeDtypeStruct((8, 128), F32), jax.ShapeDtypeStruct((t, D), F32), jax.ShapeDtypeStruct((1, D), F32)],
        compiler_params=_cp(1))(x, target, fw)


def _stream_add(ref, k, is_ctx, val):
    ref[0, k] += jnp.where(is_ctx, val, 0.0)
    ref[1, k] += jnp.where(is_ctx, 0.0, val)


def ffn_down_bwd(dx, ac, av, y, mod, wd, skip_ctx):
    t = dx.shape[0]
    nt = t // TM

    def body(dx_ref, ac_ref, v_ref, y_ref, mod_ref, wd_ref, dav_ref, dac_ref, dout_ref, dg_ref):
        i = pl.program_id(0)

        @pl.when(i == 0)
        def _():
            dg_ref[...] = jnp.zeros_like(dg_ref)

        def work():
            _stream_add(dg_ref, 0, i == 0, jnp.sum(dx_ref[...] * y_ref[...], axis=0, keepdims=True))
            dout = (mod_ref[0, 5] * dx_ref[...]).astype(BF16)
            dout_ref[...] = dout
            for k in range(N_FFK):
                _, vjp = jax.vjp(_glu_fn, ac_ref[k].astype(F32), v_ref[k])
                dac, dv = vjp(mm_nt(dout, wd_ref[k]))
                dac_ref[k] = dac
                dav_ref[k] = dv.astype(BF16)

        _unless_ctx(skip_ctx, i == 0, (dav_ref, dac_ref, dout_ref), work)

    tile = pl.BlockSpec((TM, D), lambda i: (i, 0))
    half = lambda first: pl.BlockSpec((N_FFK, TM, FF_SLOT), lambda i: (first, i, 0))
    return pl.pallas_call(
        body, name="ffn_down_bwd", grid=(nt,),
        in_specs=[tile, half(0), half(1), tile, pl.BlockSpec((1, 6, 1, D), _stream_row(TM)), VMEM_WHOLE],
        out_specs=[half(1), half(0), tile, pl.BlockSpec((2, 1, 1, D), lambda i: (0, 0, 0, 0))],
        out_shape=[jax.ShapeDtypeStruct((N_DEV, t, FF_SLOT), BF16), jax.ShapeDtypeStruct((N_FFK, t, FF_SLOT), F32),
                   jax.ShapeDtypeStruct((t, D), BF16), jax.ShapeDtypeStruct((2, 1, 1, D), F32)],
        compiler_params=_cp(1))(dx, ac, av, y, mod, wd)


def conv_bwd(dav, dac, av, cw, skip_ctx):
    t = dac.shape[1]
    nt = t // TM
    ext = TM + 2 * GRID_W

    def body(dav_in, gp_ref, gm_ref, gn_ref, ap_ref, am_ref, an_ref, cw_ref, dav_ref, dcw_ref, dcb_ref):
        k, i = pl.program_id(0), pl.program_id(1)

        @pl.when(i == 0)
        def _():
            dcw_ref[...] = jnp.zeros_like(dcw_ref)
            dcb_ref[...] = jnp.zeros_like(dcb_ref)

        def work():
            g_ext = _with_halo(gp_ref, gm_ref, gn_ref, i, nt)
            a_ext = _with_halo(ap_ref, am_ref, an_ref, i, nt)
            g_main = gm_ref[0]
            dcb_ref[0] += jnp.sum(g_main, axis=0, keepdims=True)
            da = jnp.zeros((TM, FF_SLOT), F32)
            for dc in (-1, 0, 1):
                valid = _tap_valid(dc, i, TM, 0)
                q = functools.reduce(lambda p, r: p + r, [g_ext[GRID_W - GRID_W * dr:GRID_W - GRID_W * dr + TM]
                                                          * _row_weight(cw_ref, dr, dc, i) for dr in (-1, 0, 1)])
                da = da + (q if dc == 0 else pltpu.roll(jnp.where(valid, q, 0.0), dc % TM, 0))
                g_shift = g_main if dc == 0 else pltpu.roll(jnp.where(valid, g_main, 0.0), dc % TM, 0)
                for dr in (-1, 0, 1):
                    lo = GRID_W + GRID_W * dr
                    tap = 3 * (dr + 1) + dc + 1
                    dw = jnp.sum(g_shift * a_ext[lo:lo + TM], axis=0, keepdims=True)
                    dcw_ref[0, tap:tap + 1, :] += dw if dr == 0 else jnp.where(i == 0, 0.0, dw)
            dav_ref[0] = da.astype(BF16)

        _unless_ctx(skip_ctx, i == 0, (dav_ref,), work)

    return pl.pallas_call(
        body, name="conv_bwd", grid=(N_FFK, nt),
        in_specs=[ANY] + _halo_specs(nt, lambda k, i: k, lambda k, i: i) + _halo_specs(nt, lambda k, i: k, lambda k, i: i)
        + [pl.BlockSpec((1, 9, FF_SLOT), lambda k, i: (k, 0, 0))],
        out_specs=[pl.BlockSpec((1, TM, FF_SLOT), lambda k, i: (k, i, 0)), pl.BlockSpec((1, 9, FF_SLOT), lambda k, i: (k, 0, 0)),
                   pl.BlockSpec((1, 1, FF_SLOT), lambda k, i: (k, 0, 0))],
        out_shape=[jax.ShapeDtypeStruct(dav.shape, BF16), jax.ShapeDtypeStruct((N_FFK, 9, FF_SLOT), F32),
                   jax.ShapeDtypeStruct((N_FFK, 1, FF_SLOT), F32)],
        input_output_aliases={0: 0}, compiler_params=_cp(2))(dav, dac, dac, dac, av, av, av, cw)


def _norm_mod_bwd(x_ref, nw_ref, mod_ref, k_shift, dh, dx_in, dx_ref, dnw_ref, dmod_ref, is_ctx):
    _, vjp = jax.vjp(_norm_mod, x_ref[...], nw_ref[...], mod_ref[0, k_shift], mod_ref[0, k_shift + 1])
    dx, dnw, dshift, dscale = vjp(dh)
    dx_ref[...] = dx_in + dx
    dnw_ref[...] += dnw
    _stream_add(dmod_ref, 0, is_ctx, dshift)
    _stream_add(dmod_ref, 1, is_ctx, dscale)


def ffn_up_bwd_x(dx2, x, dav, mod, nw, wg, skip_ctx):
    t = x.shape[0]

    def body(dx2_ref, x_ref, dav_ref, mod_ref, nw_ref, w_ref, dx_ref, dnw_ref, dmod_ref):
        i = pl.program_id(0)

        @pl.when(i == 0)
        def _():
            dnw_ref[...] = jnp.zeros_like(dnw_ref)
            dmod_ref[...] = jnp.zeros_like(dmod_ref)

        def work():
            dh = mm_nt(dav_ref[0], w_ref[0])
            for j in range(1, N_DEV):
                dh = dh + mm_nt(dav_ref[j], w_ref[j])
            _norm_mod_bwd(x_ref, nw_ref, mod_ref, 3, dh, dx2_ref[...], dx_ref, dnw_ref, dmod_ref, i == 0)

        _unless_ctx(skip_ctx, i == 0, (dx_ref,), work)

    tile = pl.BlockSpec((TM, D), lambda i: (i, 0))
    return pl.pallas_call(
        body, name="ffn_up_bwd_x", grid=(t // TM,),
        in_specs=[tile, tile, pl.BlockSpec((N_DEV, TM, FF_SLOT), lambda i: (0, i, 0)), pl.BlockSpec((1, 6, 1, D), _stream_row(TM)),
                  pl.BlockSpec((1, D), lambda i: (0, 0)), VMEM_WHOLE],
        out_specs=[tile, pl.BlockSpec((1, D), lambda i: (0, 0)), pl.BlockSpec((2, 2, 1, D), lambda i: (0, 0, 0, 0))],
        out_shape=[jax.ShapeDtypeStruct((t, D), F32), jax.ShapeDtypeStruct((1, D), F32), jax.ShapeDtypeStruct((2, 2, 1, D), F32)],
        compiler_params=_cp(1))(dx2, x, dav, mod, nw, wg)


def weight_grad(at, dout, slot, name, after=None):
    rows, t = at.shape
    stacked = dout.ndim == 3
    n = dout.shape[0] if stacked else dout.shape[1] // slot

    def body(a_ref, d_ref, *rest):
        dw_ref = rest[-1]
        dw_ref[0] = jnp.dot(a_ref[...], d_ref[0] if stacked else d_ref[...], preferred_element_type=F32).astype(dw_ref.dtype)

    d_spec = pl.BlockSpec((1, t, slot), lambda j: (j, 0, 0)) if stacked else pl.BlockSpec((t, slot), lambda j: (0, j))
    extra = [] if after is None else [jnp.reshape(after, (1, 1))]
    return pl.pallas_call(
        body, name=name, grid=(n,), in_specs=[VMEM_WHOLE, d_spec] + [ANY] * len(extra),
        out_specs=pl.BlockSpec((1, rows, slot), lambda j: (j, 0, 0)),
        out_shape=jax.ShapeDtypeStruct((n, rows, slot), GRAD_WIRE), compiler_params=_cp(1))(at, dout, *extra)


def weight_grad_rows(at, dout, name):
    n, t, rows = at.shape
    cols = dout.shape[1]

    def body(a_ref, d_ref, dw_ref):
        dw_ref[0] = _dot(a_ref[0], d_ref[...], ((0,), (0,))).astype(dw_ref.dtype)

    return pl.pallas_call(
        body, name=name, grid=(n,), in_specs=[pl.BlockSpec((1, t, rows), lambda k: (k, 0, 0)), VMEM_WHOLE],
        out_specs=pl.BlockSpec((1, rows, cols), lambda k: (k, 0, 0)),
        out_shape=jax.ShapeDtypeStruct((n, rows, cols), GRAD_WIRE), compiler_params=_cp(1))(at, dout)


def mixer_bwd(dx, parts, o, pa, pb, y, mod, lnw, lnb, sw, sb, hnw, wa, wb, wo, skip_ctx):
    t = dx.shape[0]
    tm = TM
    n_ctx = CTX // tm

    def body(dx_ref, u_ref, v_ref, og_ref, ga_ref, gb_ref, o_ref, pa_ref, pb_ref, y_ref, mod_ref, lnw_ref, lnb_ref, sw_ref,
             sb_ref, hnw_ref, wa_ref, wb_ref, wo_ref, dp_ref, do_ref, dy_ref, dpa_ref, dpb_ref, dlnw_ref, dlnb_ref, dsw_ref,
             dsb_ref, dhnw_ref, dg_ref):
        i = pl.program_id(0)

        @pl.when(i == 0)
        def _():
            for r in (dlnw_ref, dlnb_ref, dsw_ref, dsb_ref, dhnw_ref, dg_ref):
                r[...] = jnp.zeros_like(r)

        def work():
            _, _, vjps, vjp_b = _mixer_tile(slice(0, tm), u_ref, v_ref, og_ref, o_ref, lnw_ref, lnb_ref, sw_ref, sb_ref, hnw_ref)
            pa, pb = pa_ref[...].astype(F32), pb_ref[...].astype(F32)
            sa, sbg = jax.nn.sigmoid(ga_ref[...]), jax.nn.sigmoid(gb_ref[...])
            dxv = dx_ref[...]
            _stream_add(dg_ref, 0, i < n_ctx, jnp.sum(dxv * y_ref[...].astype(F32), axis=0, keepdims=True))
            dy = (mod_ref[0, 2] * dxv).astype(BF16)
            dy_ref[...] = dy
            dmerged = mm_nt(dy, wo_ref[...])
            dpa, dpb = (sa * dmerged).astype(BF16), (sbg * dmerged).astype(BF16)
            dpa_ref[...], dpb_ref[...] = dpa, dpb
            first = 4 * D
            dp_ref[:, first + 3 * D:first + 4 * D] = (dmerged * pa * sa * (1.0 - sa)).astype(BF16)
            dp_ref[:, first + 4 * D:first + 5 * D] = (dmerged * pb * sbg * (1.0 - sbg)).astype(BF16)
            dya = mm_nt(dpa, wa_ref[...])
            dob, dog, dhnw = vjp_b(mm_nt(dpb, wb_ref[...]))
            dp_ref[:, first + 2 * D:first + 3 * D] = dog.astype(BF16)
            dhnw_ref[...] += dhnw
            for g in range(HEADS):
                do_ref[:, _hsl(g)] = dob[g]
            for c, vjp_a in enumerate(vjps):
                rows = slice(c * SGU_CH, (c + 1) * SGU_CH)
                dub, dvb, dlnw, dlnb, dsw, dsb = vjp_a(dya[rows])
                for g in range(HEADS):
                    dp_ref[rows, first + g * HD:first + (g + 1) * HD] = dub[g].astype(BF16)
                    dp_ref[rows, first + D + g * HD:first + D + (g + 1) * HD] = dvb[g].astype(BF16)
                    dlnw_ref[:, _hsl(g)] += dlnw[g]
                    dlnb_ref[:, _hsl(g)] += dlnb[g]
                    dsw_ref[g] += dsw[g]
                    dsb_ref[g] += dsb[g]

        _unless_ctx(skip_ctx, i < n_ctx, (dp_ref, do_ref, dy_ref, dpa_ref, dpb_ref), work)

    vec = lambda n: pl.BlockSpec((1, n), lambda i: (0, 0))
    tile = pl.BlockSpec((tm, D), lambda i: (i, 0))
    sds = jax.ShapeDtypeStruct
    return pl.pallas_call(
        body, name="mixer_bwd", grid=(t // tm,),
        in_specs=[tile] + _part_specs(tm, 4, 5)
        + [pl.BlockSpec((2, tm, D), lambda i: (0, i, 0)), tile, tile, tile, pl.BlockSpec((1, 6, 1, D), _stream_row(tm)),
           vec(D), vec(D), VMEM_WHOLE, VMEM_WHOLE, vec(HD), VMEM_WHOLE, VMEM_WHOLE, VMEM_WHOLE],
        out_specs=[pl.BlockSpec((tm, D_IN), lambda i: (i, 0)), tile, tile, tile, tile, vec(D), vec(D),
                   VMEM_WHOLE, VMEM_WHOLE, vec(HD), pl.BlockSpec((2, 1, 1, D), lambda i: (0, 0, 0, 0))],
        out_shape=[sds((t, D_IN), BF16), sds((t, D), F32), sds((t, D), BF16), sds((t, D), BF16), sds((t, D), BF16),
                   sds((1, D), F32), sds((1, D), F32), sds((HEADS, SGU_CH, SGU_CH), F32), sds((HEADS, SGU_CH, 1), F32),
                   sds((1, HD), F32), sds((2, 1, 1, D), F32)],
        compiler_params=_cp(1))(dx, parts, parts, parts, parts, parts, o, pa, pb, y, mod, lnw, lnb, sw, sb, hnw, wa, wb, wo)


def hgrn_bwd(d, parts, lb, mc, mtc, mrefc, ck, do, first=None, dparts=None):
    t = parts.shape[0]
    nb = t // SCAN_ROWS
    block = _scan_block(nb)
    rev = lambda s: block(d, nb - 1 - s)

    def body(q_ref, f_ref, i_ref, lb_ref, m_ref, mt_ref, mr_ref, ck_ref, do_ref, *rest):
        dst = rest[-1]
        dlb_ref = rest[-2]

        @pl.when(pl.program_id(0) == 0)
        def _():
            dst[...] = jnp.zeros_like(dst)
            dlb_ref[...] = jnp.zeros_like(dlb_ref)

        heads = range(HEADS)
        fn = functools.partial(_hgrn_chunk, m=m_ref[0], mt=mt_ref[0], mref=mr_ref[0])
        for c in reversed(range(SCAN_STEP)):
            first_row = c * CH if d == 0 else (SCAN_STEP - 1 - c) * CH
            rows = slice(first_row, first_row + CH)
            _, vjp = jax.vjp(fn, [ck_ref[0, c, h].astype(F32) for h in heads], [q_ref[rows, _hsl(h)] for h in heads],
                             [f_ref[rows, _hsl(h)] for h in heads], [i_ref[rows, _hsl(h)] for h in heads],
                             [lb_ref[0, :, _hsl(h)] for h in heads])
            dstl, dq, df, di, dlb = vjp(([do_ref[rows, _hsl(h)] for h in heads], [dst[h] for h in heads]))
            for h in heads:
                dst[h] = dstl[h]
                dlb_ref[0, :, _hsl(h)] += dlb[h]
                if d == 0:
                    dq_ref, df_ref, di_ref = rest[:3]
                    dq_ref[rows, _hsl(h)] = dq[h].astype(BF16)
                    df_ref[rows, _hsl(h)] = df[h].astype(BF16)
                    di_ref[rows, _hsl(h)] = di[h].astype(BF16)
                else:
                    dq0_ref, df0_ref, di0_ref, _, dp_ref = rest[:5]
                    col = lambda k: slice(k * D + h * HD, k * D + (h + 1) * HD)
                    dp_ref[rows, col(0)] = (dq0_ref[rows, _hsl(h)].astype(F32) + dq[h]).astype(BF16)
                    dp_ref[rows, col(1)] = df0_ref[rows, _hsl(h)]
                    dp_ref[rows, col(2)] = df[h].astype(BF16)
                    dp_ref[rows, col(3)] = (di0_ref[rows, _hsl(h)].astype(F32) + di[h]).astype(BF16)

    const = lambda s: (d, 0, 0)
    at = lambda k: pl.BlockSpec((SCAN_ROWS, D), lambda s: (rev(s), k))
    in_specs = [at(0), at(1 + d), at(3), pl.BlockSpec((1, 1, D), const), pl.BlockSpec((1, CH, CH), const),
                pl.BlockSpec((1, CH, CH), const), pl.BlockSpec((1, CH, 1), const),
                pl.BlockSpec((1, SCAN_STEP, HEADS, HD, HD), lambda s: (d, nb - 1 - s, 0, 0, 0)), at(0)]
    dlb_spec, dlb_shape = pl.BlockSpec((1, 1, D), lambda s: (0, 0, 0)), jax.ShapeDtypeStruct((1, 1, D), F32)
    common = dict(grid=(nb,), scratch_shapes=[pltpu.VMEM((HEADS, HD, HD), F32)], compiler_params=_cp(1))
    if d == 0:
        return pl.pallas_call(body, name="hgrn_bwd_fwd_dir", in_specs=in_specs, out_specs=[at(0)] * 3 + [dlb_spec],
                              out_shape=[jax.ShapeDtypeStruct((t, D), BF16)] * 3 + [dlb_shape], **common,
                              )(parts, parts, parts, lb, mc, mtc, mrefc, ck, do)
    return pl.pallas_call(body, name="hgrn_bwd_bwd_dir", in_specs=in_specs + [at(0)] * 3 + [ANY],
                          out_specs=[pl.BlockSpec((SCAN_ROWS, 4 * D), lambda s: (rev(s), 0)), dlb_spec],
                          out_shape=[jax.ShapeDtypeStruct(dparts.shape, BF16), dlb_shape], input_output_aliases={12: 0},
                          **common)(parts, parts, parts, lb, mc, mtc, mrefc, ck, do, *first, dparts)


def in_proj_bwd_x(dx1, x, dparts, mod, nw, wg, after=None, latent_only=False):
    t = x.shape[0]
    tm = TM
    n_ctx = CTX // tm

    def body(dx1_ref, x_ref, dp_ref, mod_ref, nw_ref, w_ref, *rest):
        dx_ref, dnw_ref, dmod_ref = rest[-3:]
        i = pl.program_id(0)

        @pl.when(i == 0)
        def _():
            dnw_ref[...] = jnp.zeros_like(dnw_ref)
            dmod_ref[...] = jnp.zeros_like(dmod_ref)

        dh = mm_nt(dp_ref[:, 0:IN_SLOT], w_ref[0])
        for j in range(1, N_DEV):
            dh = dh + mm_nt(dp_ref[:, j * IN_SLOT:(j + 1) * IN_SLOT], w_ref[j])
        _norm_mod_bwd(x_ref, nw_ref, mod_ref, 0, dh, dx1_ref[...], dx_ref, dnw_ref, dmod_ref, i < n_ctx)

    tile = pl.BlockSpec((tm, D), lambda i: (i, 0))
    extra = [] if after is None else [jnp.reshape(after, (1, 1))]
    return pl.pallas_call(
        body, name="in_proj_bwd_x", grid=(t // tm,),
        in_specs=[tile, tile, pl.BlockSpec((tm, D_IN), lambda i: (i, 0)), pl.BlockSpec((1, 6, 1, D), _stream_row(tm)),
                  pl.BlockSpec((1, D), lambda i: (0, 0)), VMEM_WHOLE] + [ANY] * len(extra),
        out_specs=[pl.BlockSpec((tm, D), lambda i: (jnp.maximum(i - n_ctx, 0), 0)) if latent_only else tile,
                   pl.BlockSpec((1, D), lambda i: (0, 0)), pl.BlockSpec((2, 2, 1, D), lambda i: (0, 0, 0, 0))],
        out_shape=[jax.ShapeDtypeStruct((t - CTX if latent_only else t, D), F32), jax.ShapeDtypeStruct((1, D), F32),
                   jax.ShapeDtypeStruct((2, 2, 1, D), F32)],
        compiler_params=_cp(1))(dx1, x, dparts, mod, nw, wg, *extra)


def _lb_fn(h0, h1):
    m = jnp.maximum(h0, h1)
    e0, e1 = jnp.exp(h0 - m), jnp.exp(h1 - m)
    return e1 / (e0 + e1)


def lower_bounds(hlb):
    def body(h_ref, out_ref):
        out_ref[...] = _lb_fn(h_ref[0:1, :], h_ref[1:2, :])
    return pl.pallas_call(body, name="lower_bounds", out_shape=jax.ShapeDtypeStruct((1, 2 * D), F32))(hlb)


def lower_bounds_bwd(hlb, dlb1):
    def body(h_ref, d_ref, out_ref):
        _, vjp = jax.vjp(_lb_fn, h_ref[0:1, :], h_ref[1:2, :])
        d0, d1 = vjp(d_ref[...])
        out_ref[0:1, :] = d0
        out_ref[1:2, :] = d1
    return pl.pallas_call(body, name="lower_bounds_bwd", out_shape=jax.ShapeDtypeStruct((2, 2 * D), F32))(hlb, dlb1)


def _ada_fn(c_all, cctx8, w, b):
    dot = lambda a, l: jnp.dot(_silu(a), w[l], precision=HIGHEST, preferred_element_type=F32) + b[l]
    return [dot(c_all, l) for l in range(2)], [dot(cctx8, l) for l in range(2)]


def ada_fwd(c_all, cctx8, w, b):
    cols = w.shape[-1]

    def body(c_ref, cc_ref, w_ref, b_ref, out_ref):
        ox, oc = _ada_fn(c_ref[...], cc_ref[...], [w_ref[0], w_ref[1]], [b_ref[0], b_ref[1]])
        for l in range(2):
            out_ref[l, 0] = ox[l]
            out_ref[l, 1] = oc[l]
    return pl.pallas_call(body, name="ada_fwd", out_shape=jax.ShapeDtypeStruct((2, 2, N_DEV, cols), F32),
                          compiler_params=_cp(0))(c_all, cctx8, w, b)


def ada_bwd(c_all, cctx8, w, b, dmx, dmc):
    cols = w.shape[-1]

    def body(c_ref, cc_ref, w_ref, b_ref, dmx_ref, dmc_ref, dw_ref, dc_ref):
        fn = lambda cc, w0, w1: _ada_fn(c_ref[...], cc, [w0, w1], [b_ref[0], b_ref[1]])
        _, vjp = jax.vjp(fn, cc_ref[...], w_ref[0], w_ref[1])
        dcc, dw0, dw1 = vjp(([dmx_ref[0], dmx_ref[1]], [dmc_ref[0], dmc_ref[1]]))
        dw_ref[0] = dw0
        dw_ref[1] = dw1
        dc_ref[...] = jnp.sum(dcc, axis=0, keepdims=True)
    return pl.pallas_call(body, name="ada_bwd", out_shape=[jax.ShapeDtypeStruct((2, D, cols), F32), jax.ShapeDtypeStruct((1, D), F32)],
                          compiler_params=_cp(0))(c_all, cctx8, w, b, dmx, dmc)


def adamw(w, m, v, gparts, name):
    r, c = w.shape
    p = gparts.shape[0]
    rt = r
    while rt % 16 == 0 and (p + 7) * rt * c * 4 * 2 > 24 * 2 ** 20:
        rt //= 2

    def body(w_ref, m_ref, v_ref, g_ref, go_ref, d_ref, mo_ref, vo_ref):
        g = g_ref[0].astype(F32)
        for k in range(1, p):
            g = g + g_ref[k].astype(F32)
        m2 = ADAM_B1 * m_ref[...] + (1.0 - ADAM_B1) * g
        v2 = ADAM_B2 * v_ref[...] + (1.0 - ADAM_B2) * (g * g)
        m_hat = m2 / (1.0 - ADAM_B1 ** ADAM_STEP)
        v_hat = v2 / (1.0 - ADAM_B2 ** ADAM_STEP)
        go_ref[...] = g
        d_ref[...] = -ADAM_LR * (m_hat / (jnp.sqrt(v_hat) + ADAM_EPS) + ADAM_WD * w_ref[...])
        mo_ref[...] = m2
        vo_ref[...] = v2

    tile = pl.BlockSpec((rt, c), lambda i: (i, 0))
    return pl.pallas_call(
        body, name=name, grid=(r // rt,),
        in_specs=[tile, tile, tile, pl.BlockSpec((p, rt, c), lambda i: (0, i, 0))], out_specs=[tile] * 4,
        out_shape=[jax.ShapeDtypeStruct((r, c), F32)] * 4, compiler_params=_cp(1))(w, m, v, gparts)


def _me():
    x, y, c = lax.axis_index("x"), lax.axis_index("y"), lax.axis_index("c")
    return x, y, c, 4 * x + 2 * y + c


def _peer(x, y, c, p):
    fx, fy, fc = (p >> 2) & 1, (p >> 1) & 1, p & 1
    return (1 - x if fx else x, 1 - y if fy else y, 1 - c if fc else c)


def all_gather(arrs, name, after=None):
    n = len(arrs)
    extra = [] if after is None else list(after) if isinstance(after, (list, tuple)) else [after]

    def body(*refs):
        ins, outs = refs[:n], refs[n + len(extra):2 * n + len(extra)]
        send, recv, local = refs[2 * n + len(extra):]
        x, y, c, me = _me()
        copies = []
        for a in range(n):
            lc = pltpu.make_async_copy(ins[a], outs[a].at[me], local.at[a])
            lc.start()
            copies.append(lc)
            for p in range(1, N_DEV):
                cp = pltpu.make_async_remote_copy(src_ref=ins[a], dst_ref=outs[a].at[me], send_sem=send.at[a, p - 1],
                                                  recv_sem=recv.at[a, p - 1], device_id=_peer(x, y, c, p),
                                                  device_id_type=pl.DeviceIdType.MESH)
                cp.start()
                copies.append(cp)
        for cp in copies:
            cp.wait()

    return pl.pallas_call(
        body, name=name, in_specs=[ANY] * (n + len(extra)), out_specs=[ANY] * n,
        out_shape=[jax.ShapeDtypeStruct((N_DEV,) + a.shape, a.dtype) for a in arrs],
        scratch_shapes=[pltpu.SemaphoreType.DMA((n, N_DEV - 1)), pltpu.SemaphoreType.DMA((n, N_DEV - 1)),
                        pltpu.SemaphoreType.DMA((n,))])(*arrs, *extra)


HBM = pl.BlockSpec(memory_space=pltpu.HBM)
SEM = pl.BlockSpec(memory_space=pltpu.SEMAPHORE)


def _in_hbm(a):
    return pltpu.with_memory_space_constraint(a, pltpu.HBM)


ALL_PEERS = tuple(range(1, N_DEV))
SAME_CORE_AND_SIBLING = (1, 2, 4, 6)
OTHER_CHIPS = (2, 4, 6)


def _exchange_refs(srcs, lands, layer, scatter, a, x, y, c, p, forward=False):
    me = 4 * x + 2 * y + c
    px, py, pc = _peer(x, y, c, p) if p else (x, y, c)
    if forward and p:
        slot = lands[a].at[4 * px + 2 * py + pc]
        return slot, slot, _peer(x, y, c, 1)
    dst = lands[a].at[me] if layer is None else lands[a].at[me, layer]
    src = srcs[a].at[4 * px + 2 * py + pc] if scatter else dst
    return src, dst, (px, py, pc)


def exchange_start(srcs, lands, layer, scatter, name, after=None, peers=ALL_PEERS, forward=False):
    n, ns = len(lands), len(srcs)
    extra = [] if after is None else [after]

    def body(*refs):
        ins, lz = refs[:ns], refs[ns:ns + n]
        send, recv = refs[ns + n + len(extra)], refs[ns + n + len(extra) + 1]
        token = refs[-1]
        x, y, c, _ = _me()
        for a in range(n):
            for p in peers:
                src, dst, peer = _exchange_refs(ins, lz, layer, scatter, a, x, y, c, p, forward)
                k = a * (N_DEV - 1) + p - 1
                pltpu.make_async_remote_copy(src_ref=src, dst_ref=dst, send_sem=send.at[k], recv_sem=recv.at[k],
                                             device_id=peer, device_id_type=pl.DeviceIdType.MESH).start()
        token[...] = jnp.zeros_like(token)

    thru = [pltpu.HBM(a.shape, a.dtype) for a in list(srcs) + list(lands)]
    out = pl.pallas_call(
        body, name=name, in_specs=[HBM] * (ns + n) + [ANY] * len(extra),
        out_specs=[SEM, SEM] + [HBM] * (ns + n) + [pl.BlockSpec(memory_space=pltpu.VMEM)],
        out_shape=[pltpu.SemaphoreType.DMA((n * (N_DEV - 1),)), pltpu.SemaphoreType.DMA((n * (N_DEV - 1),))] + thru
        + [jax.ShapeDtypeStruct((8, 128), F32)],
        input_output_aliases={i: 2 + i for i in range(ns + n)},
        compiler_params=pltpu.CompilerParams(has_side_effects=pltpu.SideEffectType.DATAFLOW_SIDE_EFFECTING),
    )(*[_in_hbm(a) for a in list(srcs) + list(lands)], *extra)
    return out[0], out[1], out[2:2 + ns], out[2 + ns:2 + ns + n], out[-1]


def exchange_wait(send, recv, srcs, lands, layer, scatter, after, name, peers=ALL_PEERS):
    n, ns = len(lands), len(srcs)

    def body(*refs):
        ins, lz = refs[:ns], refs[ns:ns + n]
        send_ref, recv_ref = refs[ns + n], refs[ns + n + 1]
        x, y, c, _ = _me()
        for a in range(n):
            for p in peers:
                src, dst, peer = _exchange_refs(ins, lz, layer, scatter, a, x, y, c, 0)
                k = a * (N_DEV - 1) + p - 1
                cp = pltpu.make_async_remote_copy(src_ref=src, dst_ref=dst, send_sem=send_ref.at[k],
                                                  recv_sem=recv_ref.at[k], device_id=peer,
                                                  device_id_type=pl.DeviceIdType.MESH)
                cp.wait_send()
                cp.wait_recv()

    thru = [pltpu.HBM(a.shape, a.dtype) for a in list(srcs) + list(lands)]
    out = pl.pallas_call(
        body, name=name, in_specs=[HBM] * (ns + n) + [SEM, SEM, ANY], out_specs=[HBM] * (ns + n), out_shape=thru,
        input_output_aliases={i: i for i in range(ns + n)},
        compiler_params=pltpu.CompilerParams(has_side_effects=pltpu.SideEffectType.DATAFLOW_SIDE_EFFECTING),
    )(*srcs, *lands, send, recv, after)
    return out[ns:]


def place_own(src, land, me, layer, scatter, name, src_layer=None):
    create = isinstance(land, jax.ShapeDtypeStruct)
    r, c = src.shape[-2:]
    rt = r
    while rt % 32 == 0 and rt * c * 4 > 2 ** 21:
        rt //= 2

    def body(me_ref, src_ref, *rest):
        out_ref = rest[-1]
        out_ref[...] = src_ref[...].reshape(out_ref.shape).astype(out_ref.dtype)

    src_spec = (pl.BlockSpec((1, rt, c), lambda i, m: (m[0], i, 0)) if scatter else
                pl.BlockSpec((rt, c), lambda i, m: (i, 0)) if src_layer is None else
                pl.BlockSpec((1, rt, c), lambda i, m: (src_layer, i, 0)))
    out_spec = (pl.BlockSpec((1, rt, c), lambda i, m: (m[0], i, 0)) if layer is None
                else pl.BlockSpec((1, 1, rt, c), lambda i, m: (m[0], layer, i, 0)))
    grid_spec = pltpu.PrefetchScalarGridSpec(num_scalar_prefetch=1, grid=(r // rt,),
                                             in_specs=[src_spec] + ([] if create else [ANY]), out_specs=out_spec)
    return pl.pallas_call(body, name=name, grid_spec=grid_spec, out_shape=jax.ShapeDtypeStruct(land.shape, land.dtype),
                          input_output_aliases={} if create else {2: 0}, compiler_params=_cp(1),
                          )(*((me, src) if create else (me, src, land)))


def _scan_constants():
    r = lax.broadcasted_iota(jnp.int32, (CH, CH), 0)
    s = lax.broadcasted_iota(jnp.int32, (CH, CH), 1)
    lower = (s <= r).astype(F32)
    t = jnp.arange(CH)[:, None]
    mc = jnp.stack([lower, lower.T])
    mref = jnp.stack([(t <= CH // 2 - 1).astype(F32), (t >= CH // 2).astype(F32)])
    return mc, jnp.stack([lower.T, lower]), mref


def local_step(x, ctx, target, mod, lb, w, fetch=None, publish=None, small_ready=None, small_early=None):
    kept = {}

    def keep(l, part, grads):
        kept[(l, part)] = grads
        return 0.0

    fetch = fetch or (lambda l, part, after: w)
    publish = publish or keep
    n_layers = len(mod)
    mc, mtc, mrefc = _scan_constants()
    xs = jnp.concatenate([ctx, x], axis=0)
    saved, big = [], []
    for l in range(n_layers):
        wl = dict(fetch(l, "in", xs))
        parts, ht, iv = in_proj_fwd(xs, mod[l], w["nw1"][l], wl["win"][l])
        o, ck = hgrn_fwd(parts, iv, lb[l], mc, mtc, mrefc)
        wl.update(fetch(l, "rest", o))
        last = l == n_layers - 1
        x1, pa, pb, ym, yat, ybt, mt = mixer_fwd(xs, parts, o, mod[l], w["lnw"][l], w["lnb"][l], w["sw"][l], w["sb"][l],
                                                 w["hnw"][l], wl["wa"][l], wl["wb"][l], wl["wo"][l], last)
        av, h2t = ffn_up_fwd(x1, mod[l], w["nw2"][l], wl["wup"][l], last)
        x2, ac, y, z = ffn_down_fwd(x1, av, mod[l], w["cw"][l], w["cb"][l], wl["wd"][l], last)
        saved.append((xs, parts, iv, o, ck, x1, av, ac, y, z, ht, h2t, pa, pb, ym, yat, ybt, mt))
        big.append(wl)
        xs = x2
    loss, dx, dfw = loss_fwd_bwd(xs, target, w["fw"])
    g = {k: [None] * n_layers for k in ("nw1", "nw2", "lnw", "lnb", "sw", "sb", "hnw", "cw", "cb")}
    g["fw"] = dfw
    dmod, dlb = [None] * n_layers, [None] * n_layers
    tok = 0.0
    for l in reversed(range(n_layers)):
        x0, parts, iv, o, ck, x1, av, ac, y, z, ht, h2t, pa, pb, ym, yat, ybt, mt = saved[l]
        wl = big[l]
        last = l == n_layers - 1
        dav, dac, dout, dg2 = ffn_down_bwd(dx, ac, av, y, mod[l] + tok, wl["wd"][l], last)
        dwd = weight_grad_rows(z, dout, "ffn_down_bwd_w")
        dav, g["cw"][l], g["cb"][l] = conv_bwd(dav, dac, av, w["cw"][l], last)
        dx1, g["nw2"][l], dmod2 = ffn_up_bwd_x(dx, x1, dav, mod[l], w["nw2"][l], wl["wup"][l], last)
        dwup = weight_grad(h2t, dav, FF_SLOT, "ffn_up_bwd_w")
        tok = publish(l, "ffn", {"wd": dwd, "wup": dwup})
        (dparts, do, dy, dpa, dpb, g["lnw"][l], g["lnb"][l], g["sw"][l], g["sb"][l], g["hnw"][l],
         dg1) = mixer_bwd(dx1, parts, o, pa, pb, ym, mod[l] + tok, w["lnw"][l], w["lnb"][l], w["sw"][l], w["sb"][l],
                          w["hnw"][l], wl["wa"][l], wl["wb"][l], wl["wo"][l], last)
        tok = publish(l, "mix", {"wa": weight_grad(yat, dpa, D, "mixer_bwd_wa"), "wb": weight_grad(ybt, dpb, D, "mixer_bwd_wb"),
                                 "wo": weight_grad(mt, dy, D, "mixer_bwd_wo")})
        if l == 0 and small_early:
            dmod[0] = jnp.concatenate([jnp.zeros((2, 2, 1, D), F32), dg1, dmod2, dg2], axis=1)
            tok = tok + small_early(loss[0, 0], g, dmod, dlb)
        dq, df, di, dlb_f = hgrn_bwd(0, parts, lb[l] + tok, mc, mtc, mrefc, ck, do)
        dparts, dlb_b = hgrn_bwd(1, parts, lb[l], mc, mtc, mrefc, ck, do, (dq, df, di), dparts)
        dlb[l] = jnp.concatenate([dlb_f, dlb_b], axis=0)
        tok = publish(l, "in", {"win": weight_grad(ht, dparts, IN_SLOT, "in_proj_bwd_w")})
        dx, g["nw1"][l], dmod1 = in_proj_bwd_x(dx1, x0, dparts, mod[l], w["nw1"][l], wl["win"][l], after=tok,
                                               latent_only=l == 0)
        dmod[l] = jnp.concatenate([dmod1, dg1, dmod2, dg2], axis=1)
    done = small_ready(loss[0, 0], g, dmod, dlb) if small_ready else 0.0
    for (l, part), grads in kept.items():
        for k, v in grads.items():
            g.setdefault(k, [None] * n_layers)[l] = v
    return loss[0, 0], dx, g, dmod, dlb, done


ROW = 1024
REPLICATED = ("norm1_w", "sgu_ln_w", "sgu_ln_b", "sgu_w", "sgu_b", "hgrn_lower_bounds", "hgrn_norm_w", "norm2_w",
              "ffn_conv_b", "final_norm_w")
WEIGHT_ORDER = ("c_ctx", "ada_w", "ada_b", "norm1_w", "w_in", "sgu_ln_w", "sgu_ln_b", "sgu_w", "sgu_b", "hgrn_lower_bounds",
                "hgrn_norm_w", "w_branch_a", "w_branch_b", "w_out", "norm2_w", "ffn_w_up", "ffn_conv_w", "ffn_conv_b",
                "ffn_w_down", "final_norm_w")


def _rows_of(n):
    return -(-n // (8 * ROW)) * 8


def _pack(arrs, total_rows=None):
    parts = []
    for a in arrs:
        flat = a.reshape(-1).astype(F32)
        rows = _rows_of(flat.shape[0])
        parts.append(jnp.pad(flat, (0, rows * ROW - flat.shape[0])).reshape(rows, ROW))
    have = sum(p.shape[0] for p in parts)
    if total_rows is not None and total_rows > have:
        parts.append(jnp.zeros((total_rows - have, ROW), F32))
    return jnp.concatenate(parts, axis=0)


def _unpack(packed, shapes):
    lead = packed.shape[:-2]
    out, r0 = [], 0
    for s in shapes:
        n = math.prod(s)
        rows = _rows_of(n)
        out.append(packed[..., r0:r0 + rows, :].reshape(lead + (rows * ROW,))[..., :n].reshape(lead + tuple(s)))
        r0 += rows
    return out


def kernel(x, c, ctx, c_ctx, ada_w, ada_b, norm1_w, w_in, sgu_ln_w, sgu_ln_b, sgu_w, sgu_b, hgrn_lower_bounds, hgrn_norm_w, w_branch_a, w_branch_b, w_out, norm2_w, ffn_w_up, ffn_conv_w, ffn_conv_b, ffn_w_down, final_norm_w, loss_target, m_c_ctx, m_ada_w, m_ada_b, m_norm1_w, m_w_in, m_sgu_ln_w, m_sgu_ln_b, m_sgu_w, m_sgu_b, m_hgrn_lower_bounds, m_hgrn_norm_w, m_w_branch_a, m_w_branch_b, m_w_out, m_norm2_w, m_ffn_w_up, m_ffn_conv_w, m_ffn_conv_b, m_ffn_w_down, m_final_norm_w, v_c_ctx, v_ada_w, v_ada_b, v_norm1_w, v_w_in, v_sgu_ln_w, v_sgu_ln_b, v_sgu_w, v_sgu_b, v_hgrn_lower_bounds, v_hgrn_norm_w, v_w_branch_a, v_w_branch_b, v_w_out, v_norm2_w, v_ffn_w_up, v_ffn_conv_w, v_ffn_conv_b, v_ffn_w_down, v_final_norm_w):
    wts = dict(c_ctx=c_ctx, ada_w=ada_w, ada_b=ada_b, norm1_w=norm1_w, w_in=w_in, sgu_ln_w=sgu_ln_w, sgu_ln_b=sgu_ln_b,
               sgu_w=sgu_w, sgu_b=sgu_b, hgrn_lower_bounds=hgrn_lower_bounds, hgrn_norm_w=hgrn_norm_w, w_branch_a=w_branch_a,
               w_branch_b=w_branch_b, w_out=w_out, norm2_w=norm2_w, ffn_w_up=ffn_w_up, ffn_conv_w=ffn_conv_w,
               ffn_conv_b=ffn_conv_b, ffn_w_down=ffn_w_down, final_norm_w=final_norm_w)
    mom1 = dict(c_ctx=m_c_ctx, ada_w=m_ada_w, ada_b=m_ada_b, norm1_w=m_norm1_w, w_in=m_w_in, sgu_ln_w=m_sgu_ln_w,
                sgu_ln_b=m_sgu_ln_b, sgu_w=m_sgu_w, sgu_b=m_sgu_b, hgrn_lower_bounds=m_hgrn_lower_bounds,
                hgrn_norm_w=m_hgrn_norm_w, w_branch_a=m_w_branch_a, w_branch_b=m_w_branch_b, w_out=m_w_out, norm2_w=m_norm2_w,
                ffn_w_up=m_ffn_w_up, ffn_conv_w=m_ffn_conv_w, ffn_conv_b=m_ffn_conv_b, ffn_w_down=m_ffn_w_down,
                final_norm_w=m_final_norm_w)
    mom2 = dict(c_ctx=v_c_ctx, ada_w=v_ada_w, ada_b=v_ada_b, norm1_w=v_norm1_w, w_in=v_w_in, sgu_ln_w=v_sgu_ln_w,
                sgu_ln_b=v_sgu_ln_b, sgu_w=v_sgu_w, sgu_b=v_sgu_b, hgrn_lower_bounds=v_hgrn_lower_bounds,
                hgrn_norm_w=v_hgrn_norm_w, w_branch_a=v_w_branch_a, w_branch_b=v_w_branch_b, w_out=v_w_out, norm2_w=v_norm2_w,
                ffn_w_up=v_ffn_w_up, ffn_conv_w=v_ffn_conv_w, ffn_conv_b=v_ffn_conv_b, ffn_w_down=v_ffn_w_down,
                final_norm_w=v_final_norm_w)
    n_layers = w_in.shape[0]
    layers = range(n_layers)
    me = 4 * lax.axis_index("x") + 2 * lax.axis_index("y") + lax.axis_index("c")
    ada_cols = ada_w.shape[-1]

    big = ("w_in", "ffn_w_up", "w_branch_a", "w_branch_b", "w_out", "ffn_w_down")
    short = {"w_in": "win", "ffn_w_up": "wup", "w_branch_a": "wa", "w_branch_b": "wb", "w_out": "wo", "ffn_w_down": "wd"}
    me1 = me.reshape(1).astype(jnp.int32)
    groups = [[("w_in", 0)], [(k, 0) for k in big[1:]], [("w_in", 1)], [(k, 1) for k in big[1:]]]
    in_flight, started = [], 0.0

    def own_slots(n):
        return [place_own(wts[k], jax.ShapeDtypeStruct((N_DEV,) + wts[k].shape[1:], BF16), me1, None, False,
                          f"gather_own_{short[k]}_{l}", src_layer=l) for k, l in groups[n]]

    def start_group(n, lands, after):
        in_flight.append(exchange_start([], lands, None, False, f"gather_weights_start_{n}", after=after,
                                        peers=SAME_CORE_AND_SIBLING if n == 0 else ALL_PEERS))
        return in_flight[-1][-1]

    (c_all,) = all_gather([c], "gather_c")
    c_all = c_all.reshape(N_DEV, D)
    token = start_group(0, own_slots(0), c_all)
    later = [own_slots(n) for n in range(1, len(groups))]
    cctx8 = jnp.broadcast_to(c_ctx[None, :], (N_DEV, D))
    ada_b_cols = lax.dynamic_slice_in_dim(ada_b, me * ada_cols, ada_cols, axis=1)[:, None, :]
    mod_cols = ada_fwd(c_all, cctx8, ada_w, ada_b_cols)
    xs = jnp.concatenate([ctx[0], x[0]], axis=0)
    lb1 = lower_bounds(hgrn_lower_bounds)
    mod_all, conv_all = all_gather([mod_cols, ffn_conv_w.reshape(n_layers, 9, -1)], "gather_mod_conv",
                                   after=[token, xs, lb1] + [a for lands in later for a in lands])
    conv_full = [conv_all[:, l].transpose(1, 0, 2).reshape(9, N_FFK, FF_SLOT).transpose(1, 0, 2) for l in layers]
    for n in range(1, len(groups)):
        token = start_group(n, later[n - 1], mod_all if n == 1 else token)
    for started_group in in_flight:
        started = started + started_group[-1][0, 0]

    def as_used(k, a):
        return a if k in ("w_in", "ffn_w_up") else a.reshape(N_FFK, FF_SLOT, D) if k == "ffn_w_down" else a.reshape(D, D)

    arrived = {}

    def fetch(l, part, after):
        n = {(0, "in"): 0, (0, "rest"): 1, (1, "in"): 2, (1, "rest"): 3}.get((l, part))
        if n is not None:
            send, recv, _, lands, _ = in_flight[n]
            first = n == 0
            got = exchange_wait(send, recv, [], lands, None, False, after, f"gather_weights_wait_{n}",
                                peers=SAME_CORE_AND_SIBLING if first else ALL_PEERS)
            if first:
                send, recv, _, lands, _ = exchange_start([], got, None, False, "gather_weights_pass_on", peers=OTHER_CHIPS,
                                                         forward=True)
                got = exchange_wait(send, recv, [], lands, None, False, after, "gather_weights_passed_on", peers=OTHER_CHIPS)
            for (k, ll), a in zip(groups[n], got):
                arrived.setdefault(short[k], [None] * n_layers)[ll] = as_used(k, a)
        return arrived

    mod_x = lax.dynamic_index_in_dim(mod_all[:, :, 0], me, axis=2, keepdims=False)
    mod_c = mod_all[:, :, 1, 0]
    mod = [jnp.stack([mod_c[:, l].reshape(6, 1, D), mod_x[:, l].reshape(6, 1, D)]) for l in layers]
    mod[0] = mod[0] + started

    lb = [jnp.zeros((2, 1, D), F32), lb1.reshape(2, 1, D)]

    w = {
        "nw1": [norm1_w[l][None] for l in layers], "nw2": [norm2_w[l][None] for l in layers],
        "lnw": [sgu_ln_w[l][None] for l in layers], "lnb": [sgu_ln_b[l][None] for l in layers],
        "sw": [sgu_w[l] for l in layers], "sb": [sgu_b[l][:, :, None] for l in layers],
        "hnw": [hgrn_norm_w[l][None] for l in layers], "cw": conv_full,
        "cb": [ffn_conv_b[l].reshape(N_FFK, 1, FF_SLOT) for l in layers], "fw": final_norm_w[None],
    }
    long = {v: k for k, v in short.items()}
    landing, sent = {}, []

    def publish(l, part, grads):
        keys = [long[k] for k in grads]
        slots = [a.reshape((N_DEV, -1, a.shape[-1])) for a in grads.values()]
        zones = [place_own(s, landing.get(k, jax.ShapeDtypeStruct((N_DEV, n_layers) + s.shape[1:], s.dtype)), me1, l, True,
                           f"scatter_own_{short[k]}_{l}") for k, s in zip(keys, slots)]
        send, recv, srcs, zones, token = exchange_start(slots, zones, l, True, f"scatter_grads_start_{part}_{l}")
        landing.update(zip(keys, zones))
        sent.append((keys, l, part, send, recv, srcs, token))
        return token[0, 0]

    out = {}
    flat2 = lambda a: a.reshape(-1, a.shape[-1])

    def finish(part, after):
        done = []
        for keys, l, p, send, recv, srcs, _ in sent:
            if p == part:
                zones = exchange_wait(send, recv, srcs, [landing[k] for k in keys], l, True, after,
                                      f"scatter_grads_wait_{part}_{l}")
                landing.update(zip(keys, zones))
                done = keys
        for k in done:
            r = landing[k]
            res = adamw(flat2(wts[k]), flat2(mom1[k]), flat2(mom2[k]), r.reshape(N_DEV, -1, r.shape[-1]), "adamw_" + k)
            out[k] = tuple(a.reshape(wts[k].shape) for a in res)

    rep_rows = -(-sum(_rows_of(wts[k].size) for k in REPLICATED) // 64) * 64
    conv_rows = _rows_of(n_layers * 9 * D_FF)
    dmod_rows = _rows_of(n_layers * 6 * D)
    early = {}

    def small_early(loss_part, g, dmod, dlb):
        d_hlb = lower_bounds_bwd(hgrn_lower_bounds, dlb[1].reshape(1, 2 * D))
        st = lambda k: jnp.stack([jnp.zeros((1, D), F32) if a is None else a for a in g[k]])
        rep_grads = {"norm1_w": st("nw1"), "sgu_ln_w": st("lnw"), "sgu_ln_b": st("lnb"), "sgu_w": st("sw"), "sgu_b": st("sb"),
                     "hgrn_lower_bounds": d_hlb, "hgrn_norm_w": st("hnw"), "norm2_w": st("nw2"), "ffn_conv_b": st("cb"),
                     "final_norm_w": g["fw"]}
        d_conv = jnp.stack([g["cw"][l].transpose(1, 0, 2).reshape(9, D_FF) for l in layers])
        dmod_x = jnp.stack([dmod[l][1].reshape(6 * D) for l in layers])
        dmod_c = jnp.stack([dmod[l][0].reshape(6 * D) for l in layers])
        small = jnp.concatenate([_pack([rep_grads[k] for k in REPLICATED], rep_rows),
                                 _pack([d_conv, dmod_x, dmod_c, loss_part.reshape(1)])], axis=0)
        zone = place_own(small, jax.ShapeDtypeStruct((N_DEV,) + small.shape, F32), me1, None, False, "gather_small_own")
        early["send"], early["recv"], _, early["zones"], token = exchange_start([], [zone], None, False, "gather_small_start")
        return token[0, 0]

    def small_ready(loss_part, g, dmod, dlb):
        late = _pack([g["nw1"][0], dmod[0][1, 0:2], dmod[0][0, 0:2]])
        for part in ("ffn", "mix"):
            finish(part, late)
        (late_all,) = all_gather([late], "gather_small_late", after=out["w_out"][0])
        (small_all,) = exchange_wait(early["send"], early["recv"], [], early["zones"], None, False, late_all,
                                     "gather_small_wait")
        at_x = rep_rows + conv_rows
        small_all = small_all.at[:, 0:1].set(late_all[:, 0:1])
        small_all = small_all.at[:, at_x:at_x + 2].set(late_all[:, 8:10])
        small_all = small_all.at[:, at_x + dmod_rows:at_x + dmod_rows + 2].set(late_all[:, 16:18])
        d_conv_shape, dmod_shape = (n_layers, 9, D_FF), (n_layers, 6 * D)
        conv_g, dmx_all, dmc_all, loss_all = _unpack(small_all[:, rep_rows:], [d_conv_shape, dmod_shape, dmod_shape, (1,)])
        out["loss"] = functools.reduce(lambda a, b: a + b, [loss_all[k, 0] for k in range(N_DEV)])

        rep = adamw(_pack([wts[k] for k in REPLICATED], rep_rows), _pack([mom1[k] for k in REPLICATED], rep_rows),
                    _pack([mom2[k] for k in REPLICATED], rep_rows), small_all, "adamw_replicated")
        rep = [_unpack(r, [wts[k].shape for k in REPLICATED]) for r in rep]
        for n, k in enumerate(REPLICATED):
            out[k] = tuple(r[n] for r in rep)

        conv_mine = lax.dynamic_index_in_dim(conv_g.reshape(N_DEV, n_layers, 9, N_DEV, -1), me, axis=3, keepdims=False)
        res = adamw(flat2(ffn_conv_w), flat2(m_ffn_conv_w), flat2(v_ffn_conv_w),
                    conv_mine.reshape(N_DEV, -1, conv_mine.shape[-1]), "adamw_conv_w")
        out["ffn_conv_w"] = tuple(r.reshape(ffn_conv_w.shape) for r in res)

        out["ada_b"] = tuple(adamw(ada_b, m_ada_b, v_ada_b, jnp.concatenate([dmx_all, dmc_all], axis=0), "adamw_ada_b"))

        cols_of = lambda a: lax.dynamic_slice_in_dim(a, me * ada_cols, ada_cols, axis=2).transpose(1, 0, 2)
        d_ada_w, d_cctx = ada_bwd(c_all, cctx8, ada_w, ada_b_cols, cols_of(dmx_all), cols_of(dmc_all))
        res = adamw(flat2(ada_w), flat2(m_ada_w), flat2(v_ada_w), flat2(d_ada_w)[None], "adamw_ada_w")
        out["ada_w"] = tuple(r.reshape(ada_w.shape) for r in res)
        (d_cctx_all,) = all_gather([d_cctx], "gather_c_ctx_grad")
        res = adamw(c_ctx[None], m_c_ctx[None], v_c_ctx[None], d_cctx_all, "adamw_c_ctx")
        out["c_ctx"] = tuple(r[0] for r in res)
        return d_cctx_all

    _, grad_x, _, _, _, small_done = local_step(x[0], ctx[0], loss_target[0], mod, lb, w, fetch, publish, small_ready,
                                                small_early)
    loss = out["loss"]

    finish("in", small_done)
    return (loss, grad_x[None]) + tuple(out[k][n] for n in range(4) for k in WEIGHT_ORDER)
```

```python
import functools
import math

import jax
import jax.numpy as jnp
from jax import lax
from jax.experimental import pallas as pl
from jax.experimental.pallas import tpu as pltpu

F32 = jnp.float32
BF16 = jnp.bfloat16
HIGHEST = lax.Precision.HIGHEST

N_DEV = 8
AXES = ("x", "y", "c")
D = 1024
CTX = 256
TM = 256
CH = 64
SGU_CH = 128
HEADS = 8
HD = 128
GRID_W = 64
D_IN = 9 * D
IN_SLOT = D_IN // N_DEV
D_FF = 2816
FF_SLOT = 2 * D_FF // N_DEV
N_FFK = D_FF // FF_SLOT
RMS_EPS = 1e-6
LN_EPS = 1e-5
ADAM_LR, ADAM_B1, ADAM_B2, ADAM_EPS, ADAM_WD, ADAM_STEP = 0.001, 0.9, 0.999, 1e-08, 0.01, 10
VMEM_LIMIT_V7X = 56 * 2 ** 20
GRAD_WIRE = jnp.bfloat16

VMEM_WHOLE = pl.BlockSpec(memory_space=pltpu.VMEM)
ANY = pl.BlockSpec(memory_space=pl.ANY)


def _cp(n_axes):
    return pltpu.CompilerParams(dimension_semantics=("arbitrary",) * n_axes, vmem_limit_bytes=VMEM_LIMIT_V7X)


def _dot(a, b, dims):
    return lax.dot_general(a.astype(BF16), b.astype(BF16), (dims, ((), ())), preferred_element_type=F32)


@jax.custom_vjp
def mm(a, b):
    return _dot(a, b, ((1,), (0,)))


mm.defvjp(lambda a, b: (mm(a, b), (a, b)),
          lambda r, g: (_dot(g, r[1], ((1,), (1,))).astype(r[0].dtype), _dot(r[0], g, ((0,), (0,))).astype(r[1].dtype)))


@jax.custom_vjp
def mm_nt(a, b):
    return _dot(a, b, ((1,), (1,)))


mm_nt.defvjp(lambda a, b: (mm_nt(a, b), (a, b)),
             lambda r, g: (_dot(g, r[1], ((1,), (0,))).astype(r[0].dtype), _dot(g, r[0], ((0,), (0,))).astype(r[1].dtype)))


@jax.custom_vjp
def mm_tn(a, b):
    return _dot(a, b, ((0,), (0,)))


mm_tn.defvjp(lambda a, b: (mm_tn(a, b), (a, b)),
             lambda r, g: (_dot(r[1], g, ((1,), (1,))).astype(r[0].dtype), _dot(r[0], g, ((1,), (0,))).astype(r[1].dtype)))


def _tri_dot(m, g):
    hi = g.astype(BF16)
    low = (g - hi.astype(F32)).astype(BF16)
    n = g.shape[1]
    out = jnp.dot(m.astype(BF16), jnp.concatenate([hi, low], axis=1), preferred_element_type=F32)
    return out[:, :n] + out[:, n:]


@jax.custom_vjp
def _cum(m, mt, g):
    return _tri_dot(m, g)


_cum.defvjp(lambda m, mt, g: (_cum(m, mt, g), (m, mt)),
            lambda r, d: (jnp.zeros_like(r[0]), jnp.zeros_like(r[1]), _tri_dot(r[1], d)))


def _silu(x):
    return x * jax.nn.sigmoid(x)


def _gelu(x):
    return 0.5 * x * (1.0 + jnp.tanh(math.sqrt(2.0 / math.pi) * (x + 0.044715 * (x * x * x))))


def _rms(x, w):
    return x * lax.rsqrt(jnp.mean(x * x, axis=-1, keepdims=True) + RMS_EPS) * w


def _norm_mod(x, w, shift, scale):
    return _rms(x, w) * (1.0 + scale) + shift


def _hsl(h):
    return slice(h * HD, (h + 1) * HD)


def _hgrn_chunk(st, qz, fz, iv, lb, m, mt, mref):
    hs = range(HEADS)
    keep = [1.0 - lb[h] for h in hs]
    sg = [jax.nn.sigmoid(fz[h]) for h in hs]
    g = [jnp.log(lb[h] + keep[h] * sg[h]) for h in hs]
    k = [keep[h] * (1.0 - sg[h]) for h in hs]
    q = [_silu(qz[h]) for h in hs]
    b = [_cum(m, mt, g[h]) for h in hs]
    ref = [jnp.sum(mref * g[h], axis=0, keepdims=True) for h in hs]
    last = [jnp.sum(g[h], axis=0, keepdims=True) for h in hs]
    qa = [q[h] * jnp.exp(b[h] - ref[h]) for h in hs]
    ka = [k[h] * jnp.exp(ref[h] - b[h]) for h in hs]
    scores = [jnp.where(m > 0.5, mm_nt(qa[h], ka[h]), 0.0) for h in hs]
    inter = [mm_nt(qa[h] * jnp.exp(ref[h]), st[h]) for h in hs]
    kv = [mm_tn(iv[h], ka[h] * jnp.exp(last[h] - ref[h])) for h in hs]
    outs = [mm(scores[h], iv[h]) + inter[h] for h in hs]
    news = [jnp.exp(last[h]) * st[h] + kv[h] for h in hs]
    return outs, news


def _sgu_fn(ub, vb, lnw, lnb, sw, sb):
    gv = [_gelu(v) for v in vb]
    mu = sum(jnp.sum(t, axis=-1, keepdims=True) for t in gv) / D
    var = sum(jnp.sum((t - mu) * (t - mu), axis=-1, keepdims=True) for t in gv) / D
    inv = lax.rsqrt(var + LN_EPS)
    cols = []
    for g in range(HEADS):
        vn = (gv[g] - mu) * inv * lnw[g] + lnb[g]
        cols.append(_gelu(ub[g]) * (mm(sw[g], vn) + sb[g]))
    return jnp.concatenate(cols, axis=1)


def _readout_fn(ob, og, hnw):
    r = [o * lax.rsqrt(jnp.mean(o * o, axis=-1, keepdims=True) + RMS_EPS) * hnw for o in ob]
    return jnp.concatenate(r, axis=1) * _silu(og)


def _glu_fn(ac, v):
    return _gelu(ac) * v


def _stream_row(tm):
    n_ctx = CTX // tm
    return lambda i: (jnp.where(i < n_ctx, 0, 1), 0, 0, 0)


def in_proj_fwd(x, mod, nw, wg):
    t = x.shape[0]

    def body(x_ref, mod_ref, nw_ref, w_ref, out_ref, ht_ref, iv_ref):
        h32 = _norm_mod(x_ref[...], nw_ref[...], mod_ref[0, 0], mod_ref[0, 1])
        ht_ref[...] = h32.T.astype(BF16)
        h = h32.astype(BF16)
        for j in range(N_DEV):
            out_ref[:, j * IN_SLOT:(j + 1) * IN_SLOT] = jnp.dot(h, w_ref[j], preferred_element_type=F32)
        iv_ref[...] = out_ref[:, 3 * D:4 * D].astype(BF16)

    return pl.pallas_call(
        body, name="in_proj_fwd", grid=(t // TM,),
        in_specs=[pl.BlockSpec((TM, D), lambda i: (i, 0)), pl.BlockSpec((1, 6, 1, D), _stream_row(TM)),
                  pl.BlockSpec((1, D), lambda i: (0, 0)), VMEM_WHOLE],
        out_specs=[pl.BlockSpec((TM, D_IN), lambda i: (i, 0)), pl.BlockSpec((D, TM), lambda i: (0, i)),
                   pl.BlockSpec((TM, D), lambda i: (i, 0))],
        out_shape=[jax.ShapeDtypeStruct((t, D_IN), F32), jax.ShapeDtypeStruct((D, t), BF16), jax.ShapeDtypeStruct((t, D), BF16)],
        compiler_params=_cp(1))(x, mod, nw, wg)


SCAN_STEP = 4
SCAN_ROWS = SCAN_STEP * CH


def _scan_block(nb):
    ncb = CTX // SCAN_ROWS

    def block(d, s):
        bwd = jnp.where(s < ncb, ncb - 1 - s, nb + ncb - 1 - s)
        return jnp.where(d == 0, s, bwd)
    return block


def hgrn_fwd(parts, iv, lb, mc, mtc, mrefc):
    t = parts.shape[0]
    nb = t // SCAN_ROWS
    block = _scan_block(nb)

    def body(q_ref, f_ref, i_ref, lb_ref, m_ref, mt_ref, mr_ref, o_ref, ck_ref, st):
        d = pl.program_id(0)

        @pl.when(pl.program_id(1) == 0)
        def _():
            st[...] = jnp.zeros_like(st)

        for c in range(SCAN_STEP):
            rows = pl.ds(pl.multiple_of(jnp.where(d == 0, c * CH, (SCAN_STEP - 1 - c) * CH), CH), CH)
            ck_ref[0, c] = st[...].astype(BF16)
            outs, news = _hgrn_chunk([st[h] for h in range(HEADS)], [q_ref[rows, _hsl(h)] for h in range(HEADS)],
                                     [f_ref[rows, _hsl(h)] for h in range(HEADS)], [i_ref[rows, _hsl(h)] for h in range(HEADS)],
                                     [lb_ref[0, :, _hsl(h)] for h in range(HEADS)], m_ref[0], mt_ref[0], mr_ref[0])
            for h in range(HEADS):
                o_ref[0, rows, _hsl(h)] = outs[h].astype(BF16)
                st[h] = news[h]

    const = lambda d, s: (d, 0, 0)
    at = lambda k: pl.BlockSpec((SCAN_ROWS, D), lambda d, s: (block(d, s), k(d)))
    return pl.pallas_call(
        body, name="hgrn_fwd", grid=(2, nb),
        in_specs=[at(lambda d: 0), at(lambda d: 1 + d), at(lambda d: 0), pl.BlockSpec((1, 1, D), const),
                  pl.BlockSpec((1, CH, CH), const), pl.BlockSpec((1, CH, CH), const), pl.BlockSpec((1, CH, 1), const)],
        out_specs=[pl.BlockSpec((1, SCAN_ROWS, D), lambda d, s: (d, block(d, s), 0)),
                   pl.BlockSpec((1, SCAN_STEP, HEADS, HD, HD), lambda d, s: (d, s, 0, 0, 0))],
        out_shape=[jax.ShapeDtypeStruct((2, t, D), BF16), jax.ShapeDtypeStruct((2, nb * SCAN_STEP, HEADS, HD, HD), BF16)],
        scratch_shapes=[pltpu.VMEM((HEADS, HD, HD), F32)], compiler_params=_cp(2))(parts, parts, iv, lb, mc, mtc, mrefc)


def _mixer_tile(rows, u_ref, v_ref, og_ref, o_ref, lnw_ref, lnb_ref, sw_ref, sb_ref, hnw_ref):
    n = (rows.stop - rows.start) // SGU_CH
    yas, vjps = [], []
    for c in range(n):
        r = slice(rows.start + c * SGU_CH, rows.start + (c + 1) * SGU_CH)
        ya, vjp_a = jax.vjp(_sgu_fn, [u_ref[r, _hsl(g)] for g in range(HEADS)], [v_ref[r, _hsl(g)] for g in range(HEADS)],
                            [lnw_ref[:, _hsl(g)] for g in range(HEADS)], [lnb_ref[:, _hsl(g)] for g in range(HEADS)],
                            [sw_ref[g] for g in range(HEADS)], [sb_ref[g] for g in range(HEADS)])
        yas.append(ya)
        vjps.append(vjp_a)
    yb, vjp_b = jax.vjp(_readout_fn, [o_ref[0, rows, _hsl(h)].astype(F32) + o_ref[1, rows, _hsl(h)].astype(F32)
                                      for h in range(HEADS)],
                        og_ref[rows, :], hnw_ref[...])
    return (yas[0] if n == 1 else jnp.concatenate(yas, axis=0)), yb, vjps, vjp_b


def _part_specs(tm, first, n):
    return [pl.BlockSpec((tm, D), functools.partial(lambda k, i: (i, k), first + k)) for k in range(n)]


def _unless_ctx(skip_ctx, is_ctx, zero_refs, work):
    if not skip_ctx:
        return work()

    @pl.when(is_ctx)
    def _():
        for r in zero_refs:
            r[...] = jnp.zeros_like(r)

    pl.when(jnp.logical_not(is_ctx))(work)


def mixer_fwd(x, parts, o, mod, lnw, lnb, sw, sb, hnw, wa, wb, wo, skip_ctx):
    t = x.shape[0]

    def body(x_ref, u_ref, v_ref, og_ref, ga_ref, gb_ref, o_ref, mod_ref, lnw_ref, lnb_ref, sw_ref, sb_ref, hnw_ref,
             wa_ref, wb_ref, wo_ref, out_ref, pa_ref, pb_ref, y_ref, yat_ref, ybt_ref, mt_ref):
        def work():
            ya, yb, _, _ = _mixer_tile(slice(0, TM), u_ref, v_ref, og_ref, o_ref, lnw_ref, lnb_ref, sw_ref, sb_ref, hnw_ref)
            pa, pb = mm(ya, wa_ref[...]), mm(yb, wb_ref[...])
            merged = jax.nn.sigmoid(ga_ref[...]) * pa + jax.nn.sigmoid(gb_ref[...]) * pb
            y = mm(merged, wo_ref[...])
            out_ref[...] = x_ref[...] + mod_ref[0, 2] * y
            pa_ref[...], pb_ref[...], y_ref[...] = pa.astype(BF16), pb.astype(BF16), y.astype(BF16)
            yat_ref[...], ybt_ref[...], mt_ref[...] = ya.T.astype(BF16), yb.T.astype(BF16), merged.T.astype(BF16)

        _unless_ctx(skip_ctx, pl.program_id(0) == 0, (out_ref, pa_ref, pb_ref, y_ref, yat_ref, ybt_ref, mt_ref), work)

    vec = lambda n: pl.BlockSpec((1, n), lambda i: (0, 0))
    tile = pl.BlockSpec((TM, D), lambda i: (i, 0))
    tile_t = pl.BlockSpec((D, TM), lambda i: (0, i))
    return pl.pallas_call(
        body, name="mixer_fwd", grid=(t // TM,),
        in_specs=[tile] + _part_specs(TM, 4, 5)
        + [pl.BlockSpec((2, TM, D), lambda i: (0, i, 0)), pl.BlockSpec((1, 6, 1, D), _stream_row(TM)), vec(D), vec(D),
           VMEM_WHOLE, VMEM_WHOLE, vec(HD), VMEM_WHOLE, VMEM_WHOLE, VMEM_WHOLE],
        out_specs=[tile] * 4 + [tile_t] * 3,
        out_shape=[jax.ShapeDtypeStruct((t, D), F32)] + [jax.ShapeDtypeStruct((t, D), BF16)] * 3
        + [jax.ShapeDtypeStruct((D, t), BF16)] * 3, compiler_params=_cp(1),
    )(x, parts, parts, parts, parts, parts, o, mod, lnw, lnb, sw, sb, hnw, wa, wb, wo)


def ffn_up_fwd(x, mod, nw, wg, skip_ctx):
    t = x.shape[0]

    def body(x_ref, mod_ref, nw_ref, w_ref, out_ref, ht_ref):
        def work():
            h32 = _norm_mod(x_ref[...], nw_ref[...], mod_ref[0, 3], mod_ref[0, 4])
            ht_ref[...] = h32.T.astype(BF16)
            h = h32.astype(BF16)
            for j in range(N_DEV):
                out_ref[j] = jnp.dot(h, w_ref[j], preferred_element_type=F32)

        _unless_ctx(skip_ctx, pl.program_id(0) == 0, (out_ref, ht_ref), work)

    return pl.pallas_call(
        body, name="ffn_up_fwd", grid=(t // TM,),
        in_specs=[pl.BlockSpec((TM, D), lambda i: (i, 0)), pl.BlockSpec((1, 6, 1, D), _stream_row(TM)),
                  pl.BlockSpec((1, D), lambda i: (0, 0)), VMEM_WHOLE],
        out_specs=[pl.BlockSpec((N_DEV, TM, FF_SLOT), lambda i: (0, i, 0)), pl.BlockSpec((D, TM), lambda i: (0, i))],
        out_shape=[jax.ShapeDtypeStruct((N_DEV, t, FF_SLOT), F32), jax.ShapeDtypeStruct((D, t), BF16)],
        compiler_params=_cp(1))(x, mod, nw, wg)


def _halo_specs(nt):
    per = TM // GRID_W
    last = nt * per - 1
    return [pl.BlockSpec((N_FFK, GRID_W, FF_SLOT), lambda i: (0, jnp.maximum(i * per - 1, 0), 0)),
            pl.BlockSpec((N_FFK, TM, FF_SLOT), lambda i: (0, i, 0)),
            pl.BlockSpec((N_FFK, GRID_W, FF_SLOT), lambda i: (0, jnp.minimum(i * per + per, last), 0))]


def _with_halo(prev_ref, main_ref, next_ref, k, i, nt):
    prev = jnp.where(i >= 2, prev_ref[k], 0.0)
    nxt = jnp.where((i >= 1) & (i <= nt - 2), next_ref[k], 0.0)
    return jnp.concatenate([prev, main_ref[k], nxt], axis=0)


def _tap_valid(dc, i, n_rows, offset):
    r = lax.broadcasted_iota(jnp.int32, (n_rows, 1), 0) - offset
    col = jnp.bitwise_and(r, GRID_W - 1)
    pos = jnp.where(i == 0, r, col) + dc
    return (pos >= 0) & (pos < jnp.where(i == 0, TM, GRID_W))


def _row_weight(cw_ref, k, dr, dc, i):
    w = cw_ref[k, 3 * (dr + 1) + dc + 1:3 * (dr + 1) + dc + 2, :]
    return w if dr == 0 else jnp.where(i == 0, 0.0, w)


def ffn_down_fwd(x, av, mod, cw, cb, wd, skip_ctx):
    t = x.shape[0]
    nt = t // TM

    def body(x_ref, ap_ref, am_ref, an_ref, v_ref, mod_ref, cw_ref, cb_ref, wd_ref, out_ref, ac_ref, y_ref, z_ref):
        i = pl.program_id(0)

        def work():
            y = None
            for k in range(N_FFK):
                a_ext = _with_halo(ap_ref, am_ref, an_ref, k, i, nt)
                conv = jnp.zeros((TM, FF_SLOT), F32) + cb_ref[k]
                for dc in (-1, 0, 1):
                    col = functools.reduce(lambda p, q: p + q, [a_ext[GRID_W + GRID_W * dr:GRID_W + GRID_W * dr + TM]
                                                                * _row_weight(cw_ref, k, dr, dc, i) for dr in (-1, 0, 1)])
                    conv = conv + (col if dc == 0 else
                                   jnp.where(_tap_valid(dc, i, TM, 0), pltpu.roll(col, (-dc) % TM, 0), 0.0))
                ac_ref[k] = conv.astype(BF16)
                z = _glu_fn(conv, v_ref[k]).astype(BF16)
                z_ref[k] = z
                part = mm(z, wd_ref[k])
                y = part if y is None else y + part
            y_ref[...] = y
            out_ref[...] = x_ref[...] + mod_ref[0, 5] * y

        _unless_ctx(skip_ctx, i == 0, (out_ref, ac_ref, y_ref, z_ref), work)

    tile = pl.BlockSpec((TM, D), lambda i: (i, 0))
    half = lambda first: pl.BlockSpec((N_FFK, TM, FF_SLOT), lambda i: (first, i, 0))
    return pl.pallas_call(
        body, name="ffn_down_fwd", grid=(nt,),
        in_specs=[tile] + _halo_specs(nt) + [half(1), pl.BlockSpec((1, 6, 1, D), _stream_row(TM)), VMEM_WHOLE, VMEM_WHOLE,
                                             VMEM_WHOLE],
        out_specs=[tile, half(0), tile, half(0)],
        out_shape=[jax.ShapeDtypeStruct((t, D), F32), jax.ShapeDtypeStruct((N_FFK, t, FF_SLOT), BF16),
                   jax.ShapeDtypeStruct((t, D), F32), jax.ShapeDtypeStruct((N_FFK, t, FF_SLOT), BF16)],
        compiler_params=_cp(1))(x, av, av, av, av, mod, cw, cb, wd)


def loss_fwd_bwd(x, target, fw):
    t = x.shape[0]

    def body(x_ref, t_ref, w_ref, loss_ref, dx_ref, dw_ref):
        i = pl.program_id(0)

        @pl.when(i == 0)
        def _():
            loss_ref[...] = jnp.zeros_like(loss_ref)
            dw_ref[...] = jnp.zeros_like(dw_ref)
            dx_ref[...] = jnp.zeros_like(dx_ref)

        @pl.when(i > 0)
        def _():
            y, vjp = jax.vjp(_rms, x_ref[...], w_ref[...])
            err = y - t_ref[...]
            loss_ref[...] += 0.5 * jnp.sum(jnp.sum(err * err, axis=-1, keepdims=True) / D)
            dx, dw = vjp(err / D)
            dx_ref[...] = dx
            dw_ref[...] += dw

    return pl.pallas_call(
        body, name="loss_fwd_bwd", grid=(t // TM,),
        in_specs=[pl.BlockSpec((TM, D), lambda i: (i, 0)), pl.BlockSpec((TM, D), lambda i: (jnp.maximum(i - 1, 0), 0)),
                  pl.BlockSpec((1, D), lambda i: (0, 0))],
        out_specs=[pl.BlockSpec((8, 128), lambda i: (0, 0)), pl.BlockSpec((TM, D), lambda i: (i, 0)),
                   pl.BlockSpec((1, D), lambda i: (0, 0))],
        out_shape=[jax.ShapeDtypeStruct((8, 128), F32), jax.ShapeDtypeStruct((t, D), F32), jax.ShapeDtypeStruct((1, D), F32)],
        compiler_params=_cp(1))(x, target, fw)


def _stream_add(ref, k, is_ctx, val):
    ref[0, k] += jnp.where(is_ctx, val, 0.0)
    ref[1, k] += jnp.where(is_ctx, 0.0, val)


def ffn_down_bwd(dx, ac, av, y, mod, wd, skip_ctx):
    t = dx.shape[0]
    nt = t // TM

    def body(dx_ref, ac_ref, v_ref, y_ref, mod_ref, wd_ref, dav_ref, dac_ref, dout_ref, dg_ref):
        i = pl.program_id(0)

        @pl.when(i == 0)
        def _():
            dg_ref[...] = jnp.zeros_like(dg_ref)

        def work():
            _stream_add(dg_ref, 0, i == 0, jnp.sum(dx_ref[...] * y_ref[...], axis=0, keepdims=True))
            dout = (mod_ref[0, 5] * dx_ref[...]).astype(BF16)
            dout_ref[...] = dout
            for k in range(N_FFK):
                _, vjp = jax.vjp(_glu_fn, ac_ref[k].astype(F32), v_ref[k])
                dac, dv = vjp(mm_nt(dout, wd_ref[k]))
                dac_ref[k] = dac
                dav_ref[k] = dv.astype(BF16)

        _unless_ctx(skip_ctx, i == 0, (dav_ref, dac_ref, dout_ref), work)

    tile = pl.BlockSpec((TM, D), lambda i: (i, 0))
    half = lambda first: pl.BlockSpec((N_FFK, TM, FF_SLOT), lambda i: (first, i, 0))
    return pl.pallas_call(
        body, name="ffn_down_bwd", grid=(nt,),
        in_specs=[tile, half(0), half(1), tile, pl.BlockSpec((1, 6, 1, D), _stream_row(TM)), VMEM_WHOLE],
        out_specs=[half(1), half(0), tile, pl.BlockSpec((2, 1, 1, D), lambda i: (0, 0, 0, 0))],
        out_shape=[jax.ShapeDtypeStruct((N_DEV, t, FF_SLOT), BF16), jax.ShapeDtypeStruct((N_FFK, t, FF_SLOT), F32),
                   jax.ShapeDtypeStruct((t, D), BF16), jax.ShapeDtypeStruct((2, 1, 1, D), F32)],
        compiler_params=_cp(1))(dx, ac, av, y, mod, wd)


def conv_bwd(dav, dac, av, cw, skip_ctx):
    t = dac.shape[1]
    nt = t // TM

    def body(dav_in, gp_ref, gm_ref, gn_ref, ap_ref, am_ref, an_ref, cw_ref, dav_ref, dcw_ref, dcb_ref):
        i = pl.program_id(0)

        @pl.when(i == 0)
        def _():
            dcw_ref[...] = jnp.zeros_like(dcw_ref)
            dcb_ref[...] = jnp.zeros_like(dcb_ref)

        def work():
            for k in range(N_FFK):
                g_ext = _with_halo(gp_ref, gm_ref, gn_ref, k, i, nt)
                a_ext = _with_halo(ap_ref, am_ref, an_ref, k, i, nt)
                g_main = gm_ref[k]
                dcb_ref[k] += jnp.sum(g_main, axis=0, keepdims=True)
                da = jnp.zeros((TM, FF_SLOT), F32)
                for dc in (-1, 0, 1):
                    valid = _tap_valid(dc, i, TM, 0)
                    q = functools.reduce(lambda p, r: p + r, [g_ext[GRID_W - GRID_W * dr:GRID_W - GRID_W * dr + TM]
                                                              * _row_weight(cw_ref, k, dr, dc, i) for dr in (-1, 0, 1)])
                    da = da + (q if dc == 0 else pltpu.roll(jnp.where(valid, q, 0.0), dc % TM, 0))
                    g_shift = g_main if dc == 0 else pltpu.roll(jnp.where(valid, g_main, 0.0), dc % TM, 0)
                    for dr in (-1, 0, 1):
                        lo = GRID_W + GRID_W * dr
                        tap = 3 * (dr + 1) + dc + 1
                        dw = jnp.sum(g_shift * a_ext[lo:lo + TM], axis=0, keepdims=True)
                        dcw_ref[k, tap:tap + 1, :] += dw if dr == 0 else jnp.where(i == 0, 0.0, dw)
                dav_ref[k] = da.astype(BF16)

        _unless_ctx(skip_ctx, i == 0, (dav_ref,), work)

    whole = lambda rows: pl.BlockSpec((N_FFK, rows, FF_SLOT), lambda i: (0, 0, 0))
    return pl.pallas_call(
        body, name="conv_bwd", grid=(nt,),
        in_specs=[ANY] + _halo_specs(nt) + _halo_specs(nt) + [VMEM_WHOLE],
        out_specs=[pl.BlockSpec((N_FFK, TM, FF_SLOT), lambda i: (0, i, 0)), whole(9), whole(1)],
        out_shape=[jax.ShapeDtypeStruct(dav.shape, BF16), jax.ShapeDtypeStruct((N_FFK, 9, FF_SLOT), F32),
                   jax.ShapeDtypeStruct((N_FFK, 1, FF_SLOT), F32)],
        input_output_aliases={0: 0}, compiler_params=_cp(1))(dav, dac, dac, dac, av, av, av, cw)


def _norm_mod_bwd(x_ref, nw_ref, mod_ref, k_shift, dh, dx_in, dx_ref, dnw_ref, dmod_ref, is_ctx):
    _, vjp = jax.vjp(_norm_mod, x_ref[...], nw_ref[...], mod_ref[0, k_shift], mod_ref[0, k_shift + 1])
    dx, dnw, dshift, dscale = vjp(dh)
    dx_ref[...] = dx_in + dx
    dnw_ref[...] += dnw
    _stream_add(dmod_ref, 0, is_ctx, dshift)
    _stream_add(dmod_ref, 1, is_ctx, dscale)


def ffn_up_bwd_x(dx2, x, dav, mod, nw, wg, skip_ctx):
    t = x.shape[0]

    def body(dx2_ref, x_ref, dav_ref, mod_ref, nw_ref, w_ref, dx_ref, dnw_ref, dmod_ref):
        i = pl.program_id(0)

        @pl.when(i == 0)
        def _():
            dnw_ref[...] = jnp.zeros_like(dnw_ref)
            dmod_ref[...] = jnp.zeros_like(dmod_ref)

        def work():
            dh = mm_nt(dav_ref[0], w_ref[0])
            for j in range(1, N_DEV):
                dh = dh + mm_nt(dav_ref[j], w_ref[j])
            _norm_mod_bwd(x_ref, nw_ref, mod_ref, 3, dh, dx2_ref[...], dx_ref, dnw_ref, dmod_ref, i == 0)

        _unless_ctx(skip_ctx, i == 0, (dx_ref,), work)

    tile = pl.BlockSpec((TM, D), lambda i: (i, 0))
    return pl.pallas_call(
        body, name="ffn_up_bwd_x", grid=(t // TM,),
        in_specs=[tile, tile, pl.BlockSpec((N_DEV, TM, FF_SLOT), lambda i: (0, i, 0)), pl.BlockSpec((1, 6, 1, D), _stream_row(TM)),
                  pl.BlockSpec((1, D), lambda i: (0, 0)), VMEM_WHOLE],
        out_specs=[tile, pl.BlockSpec((1, D), lambda i: (0, 0)), pl.BlockSpec((2, 2, 1, D), lambda i: (0, 0, 0, 0))],
        out_shape=[jax.ShapeDtypeStruct((t, D), F32), jax.ShapeDtypeStruct((1, D), F32), jax.ShapeDtypeStruct((2, 2, 1, D), F32)],
        compiler_params=_cp(1))(dx2, x, dav, mod, nw, wg)


def weight_grad(at, dout, slot, name, after=None):
    rows, t = at.shape
    stacked = dout.ndim == 3
    n = dout.shape[0] if stacked else dout.shape[1] // slot

    def body(a_ref, d_ref, *rest):
        dw_ref = rest[-1]
        dw_ref[0] = jnp.dot(a_ref[...], d_ref[0] if stacked else d_ref[...], preferred_element_type=F32).astype(dw_ref.dtype)

    d_spec = pl.BlockSpec((1, t, slot), lambda j: (j, 0, 0)) if stacked else pl.BlockSpec((t, slot), lambda j: (0, j))
    extra = [] if after is None else [jnp.reshape(after, (1, 1))]
    return pl.pallas_call(
        body, name=name, grid=(n,), in_specs=[VMEM_WHOLE, d_spec] + [ANY] * len(extra),
        out_specs=pl.BlockSpec((1, rows, slot), lambda j: (j, 0, 0)),
        out_shape=jax.ShapeDtypeStruct((n, rows, slot), GRAD_WIRE), compiler_params=_cp(1))(at, dout, *extra)


def weight_grad_rows(at, dout, name):
    n, t, rows = at.shape
    cols = dout.shape[1]

    def body(a_ref, d_ref, dw_ref):
        dw_ref[0] = _dot(a_ref[0], d_ref[...], ((0,), (0,))).astype(dw_ref.dtype)

    return pl.pallas_call(
        body, name=name, grid=(n,), in_specs=[pl.BlockSpec((1, t, rows), lambda k: (k, 0, 0)), VMEM_WHOLE],
        out_specs=pl.BlockSpec((1, rows, cols), lambda k: (k, 0, 0)),
        out_shape=jax.ShapeDtypeStruct((n, rows, cols), GRAD_WIRE), compiler_params=_cp(1))(at, dout)


def mixer_bwd(dx, parts, o, pa, pb, y, mod, lnw, lnb, sw, sb, hnw, wa, wb, wo, skip_ctx):
    t = dx.shape[0]
    tm = TM
    n_ctx = CTX // tm

    def body(dx_ref, u_ref, v_ref, og_ref, ga_ref, gb_ref, o_ref, pa_ref, pb_ref, y_ref, mod_ref, lnw_ref, lnb_ref, sw_ref,
             sb_ref, hnw_ref, wa_ref, wb_ref, wo_ref, dp_ref, do_ref, dy_ref, dpa_ref, dpb_ref, dlnw_ref, dlnb_ref, dsw_ref,
             dsb_ref, dhnw_ref, dg_ref):
        i = pl.program_id(0)

        @pl.when(i == 0)
        def _():
            for r in (dlnw_ref, dlnb_ref, dsw_ref, dsb_ref, dhnw_ref, dg_ref):
                r[...] = jnp.zeros_like(r)

        def work():
            _, _, vjps, vjp_b = _mixer_tile(slice(0, tm), u_ref, v_ref, og_ref, o_ref, lnw_ref, lnb_ref, sw_ref, sb_ref, hnw_ref)
            pa, pb = pa_ref[...].astype(F32), pb_ref[...].astype(F32)
            sa, sbg = jax.nn.sigmoid(ga_ref[...]), jax.nn.sigmoid(gb_ref[...])
            dxv = dx_ref[...]
            _stream_add(dg_ref, 0, i < n_ctx, jnp.sum(dxv * y_ref[...].astype(F32), axis=0, keepdims=True))
            dy = (mod_ref[0, 2] * dxv).astype(BF16)
            dy_ref[...] = dy
            dmerged = mm_nt(dy, wo_ref[...])
            dpa, dpb = (sa * dmerged).astype(BF16), (sbg * dmerged).astype(BF16)
            dpa_ref[...], dpb_ref[...] = dpa, dpb
            first = 4 * D
            dp_ref[:, first + 3 * D:first + 4 * D] = (dmerged * pa * sa * (1.0 - sa)).astype(BF16)
            dp_ref[:, first + 4 * D:first + 5 * D] = (dmerged * pb * sbg * (1.0 - sbg)).astype(BF16)
            dya = mm_nt(dpa, wa_ref[...])
            dob, dog, dhnw = vjp_b(mm_nt(dpb, wb_ref[...]))
            dp_ref[:, first + 2 * D:first + 3 * D] = dog.astype(BF16)
            dhnw_ref[...] += dhnw
            for g in range(HEADS):
                do_ref[:, _hsl(g)] = dob[g]
            for c, vjp_a in enumerate(vjps):
                rows = slice(c * SGU_CH, (c + 1) * SGU_CH)
                dub, dvb, dlnw, dlnb, dsw, dsb = vjp_a(dya[rows])
                for g in range(HEADS):
                    dp_ref[rows, first + g * HD:first + (g + 1) * HD] = dub[g].astype(BF16)
                    dp_ref[rows, first + D + g * HD:first + D + (g + 1) * HD] = dvb[g].astype(BF16)
                    dlnw_ref[:, _hsl(g)] += dlnw[g]
                    dlnb_ref[:, _hsl(g)] += dlnb[g]
                    dsw_ref[g] += dsw[g]
                    dsb_ref[g] += dsb[g]

        _unless_ctx(skip_ctx, i < n_ctx, (dp_ref, do_ref, dy_ref, dpa_ref, dpb_ref), work)

    vec = lambda n: pl.BlockSpec((1, n), lambda i: (0, 0))
    tile = pl.BlockSpec((tm, D), lambda i: (i, 0))
    sds = jax.ShapeDtypeStruct
    return pl.pallas_call(
        body, name="mixer_bwd", grid=(t // tm,),
        in_specs=[tile] + _part_specs(tm, 4, 5)
        + [pl.BlockSpec((2, tm, D), lambda i: (0, i, 0)), tile, tile, tile, pl.BlockSpec((1, 6, 1, D), _stream_row(tm)),
           vec(D), vec(D), VMEM_WHOLE, VMEM_WHOLE, vec(HD), VMEM_WHOLE, VMEM_WHOLE, VMEM_WHOLE],
        out_specs=[pl.BlockSpec((tm, D_IN), lambda i: (i, 0)), tile, tile, tile, tile, vec(D), vec(D),
                   VMEM_WHOLE, VMEM_WHOLE, vec(HD), pl.BlockSpec((2, 1, 1, D), lambda i: (0, 0, 0, 0))],
        out_shape=[sds((t, D_IN), BF16), sds((t, D), F32), sds((t, D), BF16), sds((t, D), BF16), sds((t, D), BF16),
                   sds((1, D), F32), sds((1, D), F32), sds((HEADS, SGU_CH, SGU_CH), F32), sds((HEADS, SGU_CH, 1), F32),
                   sds((1, HD), F32), sds((2, 1, 1, D), F32)],
        compiler_params=_cp(1))(dx, parts, parts, parts, parts, parts, o, pa, pb, y, mod, lnw, lnb, sw, sb, hnw, wa, wb, wo)


def hgrn_bwd(d, parts, lb, mc, mtc, mrefc, ck, do, first=None, dparts=None):
    t = parts.shape[0]
    nb = t // SCAN_ROWS
    block = _scan_block(nb)
    rev = lambda s: block(d, nb - 1 - s)

    def body(q_ref, f_ref, i_ref, lb_ref, m_ref, mt_ref, mr_ref, ck_ref, do_ref, *rest):
        dst = rest[-1]
        dlb_ref = rest[-2]

        @pl.when(pl.program_id(0) == 0)
        def _():
            dst[...] = jnp.zeros_like(dst)
            dlb_ref[...] = jnp.zeros_like(dlb_ref)

        heads = range(HEADS)
        fn = functools.partial(_hgrn_chunk, m=m_ref[0], mt=mt_ref[0], mref=mr_ref[0])
        for c in reversed(range(SCAN_STEP)):
            first_row = c * CH if d == 0 else (SCAN_STEP - 1 - c) * CH
            rows = slice(first_row, first_row + CH)
            _, vjp = jax.vjp(fn, [ck_ref[0, c, h].astype(F32) for h in heads], [q_ref[rows, _hsl(h)] for h in heads],
                             [f_ref[rows, _hsl(h)] for h in heads], [i_ref[rows, _hsl(h)] for h in heads],
                             [lb_ref[0, :, _hsl(h)] for h in heads])
            dstl, dq, df, di, dlb = vjp(([do_ref[rows, _hsl(h)] for h in heads], [dst[h] for h in heads]))
            for h in heads:
                dst[h] = dstl[h]
                dlb_ref[0, :, _hsl(h)] += dlb[h]
                if d == 0:
                    dq_ref, df_ref, di_ref = rest[:3]
                    dq_ref[rows, _hsl(h)] = dq[h].astype(BF16)
                    df_ref[rows, _hsl(h)] = df[h].astype(BF16)
                    di_ref[rows, _hsl(h)] = di[h].astype(BF16)
                else:
                    dq0_ref, df0_ref, di0_ref, _, dp_ref = rest[:5]
                    col = lambda k: slice(k * D + h * HD, k * D + (h + 1) * HD)
                    dp_ref[rows, col(0)] = (dq0_ref[rows, _hsl(h)].astype(F32) + dq[h]).astype(BF16)
                    dp_ref[rows, col(1)] = df0_ref[rows, _hsl(h)]
                    dp_ref[rows, col(2)] = df[h].astype(BF16)
                    dp_ref[rows, col(3)] = (di0_ref[rows, _hsl(h)].astype(F32) + di[h]).astype(BF16)

    const = lambda s: (d, 0, 0)
    at = lambda k: pl.BlockSpec((SCAN_ROWS, D), lambda s: (rev(s), k))
    in_specs = [at(0), at(1 + d), at(3), pl.BlockSpec((1, 1, D), const), pl.BlockSpec((1, CH, CH), const),
                pl.BlockSpec((1, CH, CH), const), pl.BlockSpec((1, CH, 1), const),
                pl.BlockSpec((1, SCAN_STEP, HEADS, HD, HD), lambda s: (d, nb - 1 - s, 0, 0, 0)), at(0)]
    dlb_spec, dlb_shape = pl.BlockSpec((1, 1, D), lambda s: (0, 0, 0)), jax.ShapeDtypeStruct((1, 1, D), F32)
    common = dict(grid=(nb,), scratch_shapes=[pltpu.VMEM((HEADS, HD, HD), F32)], compiler_params=_cp(1))
    if d == 0:
        return pl.pallas_call(body, name="hgrn_bwd_fwd_dir", in_specs=in_specs, out_specs=[at(0)] * 3 + [dlb_spec],
                              out_shape=[jax.ShapeDtypeStruct((t, D), BF16)] * 3 + [dlb_shape], **common,
                              )(parts, parts, parts, lb, mc, mtc, mrefc, ck, do)
    return pl.pallas_call(body, name="hgrn_bwd_bwd_dir", in_specs=in_specs + [at(0)] * 3 + [ANY],
                          out_specs=[pl.BlockSpec((SCAN_ROWS, 4 * D), lambda s: (rev(s), 0)), dlb_spec],
                          out_shape=[jax.ShapeDtypeStruct(dparts.shape, BF16), dlb_shape], input_output_aliases={12: 0},
                          **common)(parts, parts, parts, lb, mc, mtc, mrefc, ck, do, *first, dparts)


def in_proj_bwd_x(dx1, x, dparts, mod, nw, wg, after=None, latent_only=False):
    t = x.shape[0]
    tm = TM
    n_ctx = CTX // tm

    def body(dx1_ref, x_ref, dp_ref, mod_ref, nw_ref, w_ref, *rest):
        dx_ref, dnw_ref, dmod_ref = rest[-3:]
        i = pl.program_id(0)

        @pl.when(i == 0)
        def _():
            dnw_ref[...] = jnp.zeros_like(dnw_ref)
            dmod_ref[...] = jnp.zeros_like(dmod_ref)

        dh = mm_nt(dp_ref[:, 0:IN_SLOT], w_ref[0])
        for j in range(1, N_DEV):
            dh = dh + mm_nt(dp_ref[:, j * IN_SLOT:(j + 1) * IN_SLOT], w_ref[j])
        _norm_mod_bwd(x_ref, nw_ref, mod_ref, 0, dh, dx1_ref[...], dx_ref, dnw_ref, dmod_ref, i < n_ctx)

    tile = pl.BlockSpec((tm, D), lambda i: (i, 0))
    extra = [] if after is None else [jnp.reshape(after, (1, 1))]
    return pl.pallas_call(
        body, name="in_proj_bwd_x", grid=(t // tm,),
        in_specs=[tile, tile, pl.BlockSpec((tm, D_IN), lambda i: (i, 0)), pl.BlockSpec((1, 6, 1, D), _stream_row(tm)),
                  pl.BlockSpec((1, D), lambda i: (0, 0)), VMEM_WHOLE] + [ANY] * len(extra),
        out_specs=[pl.BlockSpec((tm, D), lambda i: (jnp.maximum(i - n_ctx, 0), 0)) if latent_only else tile,
                   pl.BlockSpec((1, D), lambda i: (0, 0)), pl.BlockSpec((2, 2, 1, D), lambda i: (0, 0, 0, 0))],
        out_shape=[jax.ShapeDtypeStruct((t - CTX if latent_only else t, D), F32), jax.ShapeDtypeStruct((1, D), F32),
                   jax.ShapeDtypeStruct((2, 2, 1, D), F32)],
        compiler_params=_cp(1))(dx1, x, dparts, mod, nw, wg, *extra)


def _lb_fn(h0, h1):
    m = jnp.maximum(h0, h1)
    e0, e1 = jnp.exp(h0 - m), jnp.exp(h1 - m)
    return e1 / (e0 + e1)


def lower_bounds(hlb):
    def body(h_ref, out_ref):
        out_ref[...] = _lb_fn(h_ref[0:1, :], h_ref[1:2, :])
    return pl.pallas_call(body, name="lower_bounds", out_shape=jax.ShapeDtypeStruct((1, 2 * D), F32))(hlb)


def lower_bounds_bwd(hlb, dlb1):
    def body(h_ref, d_ref, out_ref):
        _, vjp = jax.vjp(_lb_fn, h_ref[0:1, :], h_ref[1:2, :])
        d0, d1 = vjp(d_ref[...])
        out_ref[0:1, :] = d0
        out_ref[1:2, :] = d1
    return pl.pallas_call(body, name="lower_bounds_bwd", out_shape=jax.ShapeDtypeStruct((2, 2 * D), F32))(hlb, dlb1)


def _ada_fn(c_all, cctx8, w, b):
    dot = lambda a, l: jnp.dot(_silu(a), w[l], precision=HIGHEST, preferred_element_type=F32) + b[l]
    return [dot(c_all, l) for l in range(2)], [dot(cctx8, l) for l in range(2)]


def ada_fwd(c_all, cctx8, w, b):
    cols = w.shape[-1]

    def body(c_ref, cc_ref, w_ref, b_ref, out_ref):
        ox, oc = _ada_fn(c_ref[...], cc_ref[...], [w_ref[0], w_ref[1]], [b_ref[0], b_ref[1]])
        for l in range(2):
            out_ref[l, 0] = ox[l]
            out_ref[l, 1] = oc[l]
    return pl.pallas_call(body, name="ada_fwd", out_shape=jax.ShapeDtypeStruct((2, 2, N_DEV, cols), F32),
                          compiler_params=_cp(0))(c_all, cctx8, w, b)


def ada_bwd(c_all, cctx8, w, b, dmx, dmc):
    cols = w.shape[-1]

    def body(c_ref, cc_ref, w_ref, b_ref, dmx_ref, dmc_ref, dw_ref, dc_ref):
        fn = lambda cc, w0, w1: _ada_fn(c_ref[...], cc, [w0, w1], [b_ref[0], b_ref[1]])
        _, vjp = jax.vjp(fn, cc_ref[...], w_ref[0], w_ref[1])
        dcc, dw0, dw1 = vjp(([dmx_ref[0], dmx_ref[1]], [dmc_ref[0], dmc_ref[1]]))
        dw_ref[0] = dw0
        dw_ref[1] = dw1
        dc_ref[...] = jnp.sum(dcc, axis=0, keepdims=True)
    return pl.pallas_call(body, name="ada_bwd", out_shape=[jax.ShapeDtypeStruct((2, D, cols), F32), jax.ShapeDtypeStruct((1, D), F32)],
                          compiler_params=_cp(0))(c_all, cctx8, w, b, dmx, dmc)


def adamw(w, m, v, gparts, name):
    r, c = w.shape
    p = gparts.shape[0]
    rt = r
    while rt % 16 == 0 and (p + 7) * rt * c * 4 * 2 > 24 * 2 ** 20:
        rt //= 2

    def body(w_ref, m_ref, v_ref, g_ref, go_ref, d_ref, mo_ref, vo_ref):
        g = g_ref[0].astype(F32)
        for k in range(1, p):
            g = g + g_ref[k].astype(F32)
        m2 = ADAM_B1 * m_ref[...] + (1.0 - ADAM_B1) * g
        v2 = ADAM_B2 * v_ref[...] + (1.0 - ADAM_B2) * (g * g)
        m_hat = m2 / (1.0 - ADAM_B1 ** ADAM_STEP)
        v_hat = v2 / (1.0 - ADAM_B2 ** ADAM_STEP)
        go_ref[...] = g
        d_ref[...] = -ADAM_LR * (m_hat / (jnp.sqrt(v_hat) + ADAM_EPS) + ADAM_WD * w_ref[...])
        mo_ref[...] = m2
        vo_ref[...] = v2

    tile = pl.BlockSpec((rt, c), lambda i: (i, 0))
    return pl.pallas_call(
        body, name=name, grid=(r // rt,),
        in_specs=[tile, tile, tile, pl.BlockSpec((p, rt, c), lambda i: (0, i, 0))], out_specs=[tile] * 4,
        out_shape=[jax.ShapeDtypeStruct((r, c), F32)] * 4, compiler_params=_cp(1))(w, m, v, gparts)


def _me():
    x, y, c = lax.axis_index("x"), lax.axis_index("y"), lax.axis_index("c")
    return x, y, c, 4 * x + 2 * y + c


def _peer(x, y, c, p):
    fx, fy, fc = (p >> 2) & 1, (p >> 1) & 1, p & 1
    return (1 - x if fx else x, 1 - y if fy else y, 1 - c if fc else c)


def all_gather(arrs, name, after=None):
    n = len(arrs)
    extra = [] if after is None else list(after) if isinstance(after, (list, tuple)) else [after]

    def body(*refs):
        ins, outs = refs[:n], refs[n + len(extra):2 * n + len(extra)]
        send, recv, local = refs[2 * n + len(extra):]
        x, y, c, me = _me()
        copies = []
        for a in range(n):
            lc = pltpu.make_async_copy(ins[a], outs[a].at[me], local.at[a])
            lc.start()
            copies.append(lc)
            for p in range(1, N_DEV):
                cp = pltpu.make_async_remote_copy(src_ref=ins[a], dst_ref=outs[a].at[me], send_sem=send.at[a, p - 1],
                                                  recv_sem=recv.at[a, p - 1], device_id=_peer(x, y, c, p),
                                                  device_id_type=pl.DeviceIdType.MESH)
                cp.start()
                copies.append(cp)
        for cp in copies:
            cp.wait()

    return pl.pallas_call(
        body, name=name, in_specs=[ANY] * (n + len(extra)), out_specs=[ANY] * n,
        out_shape=[jax.ShapeDtypeStruct((N_DEV,) + a.shape, a.dtype) for a in arrs],
        scratch_shapes=[pltpu.SemaphoreType.DMA((n, N_DEV - 1)), pltpu.SemaphoreType.DMA((n, N_DEV - 1)),
                        pltpu.SemaphoreType.DMA((n,))])(*arrs, *extra)


HBM = pl.BlockSpec(memory_space=pltpu.HBM)
SEM = pl.BlockSpec(memory_space=pltpu.SEMAPHORE)


def _in_hbm(a):
    return pltpu.with_memory_space_constraint(a, pltpu.HBM)


ALL_PEERS = tuple(range(1, N_DEV))
SAME_CORE_AND_SIBLING = (1, 2, 4, 6)
OTHER_CHIPS = (2, 4, 6)


def _exchange_refs(srcs, lands, layer, scatter, a, x, y, c, p, forward=False):
    me = 4 * x + 2 * y + c
    px, py, pc = _peer(x, y, c, p) if p else (x, y, c)
    if forward and p:
        slot = lands[a].at[4 * px + 2 * py + pc]
        return slot, slot, _peer(x, y, c, 1)
    dst = lands[a].at[me] if layer is None else lands[a].at[me, layer]
    src = srcs[a].at[4 * px + 2 * py + pc] if scatter else dst
    return src, dst, (px, py, pc)


def exchange_start(srcs, lands, layer, scatter, name, after=None, peers=ALL_PEERS, forward=False):
    n, ns = len(lands), len(srcs)
    extra = [] if after is None else [after]

    def body(*refs):
        ins, lz = refs[:ns], refs[ns:ns + n]
        send, recv = refs[ns + n + len(extra)], refs[ns + n + len(extra) + 1]
        token = refs[-1]
        x, y, c, _ = _me()
        for a in range(n):
            for p in peers:
                src, dst, peer = _exchange_refs(ins, lz, layer, scatter, a, x, y, c, p, forward)
                k = a * (N_DEV - 1) + p - 1
                pltpu.make_async_remote_copy(src_ref=src, dst_ref=dst, send_sem=send.at[k], recv_sem=recv.at[k],
                                             device_id=peer, device_id_type=pl.DeviceIdType.MESH).start()
        token[...] = jnp.zeros_like(token)

    thru = [pltpu.HBM(a.shape, a.dtype) for a in list(srcs) + list(lands)]
    out = pl.pallas_call(
        body, name=name, in_specs=[HBM] * (ns + n) + [ANY] * len(extra),
        out_specs=[SEM, SEM] + [HBM] * (ns + n) + [pl.BlockSpec(memory_space=pltpu.VMEM)],
        out_shape=[pltpu.SemaphoreType.DMA((n * (N_DEV - 1),)), pltpu.SemaphoreType.DMA((n * (N_DEV - 1),))] + thru
        + [jax.ShapeDtypeStruct((8, 128), F32)],
        input_output_aliases={i: 2 + i for i in range(ns + n)},
        compiler_params=pltpu.CompilerParams(has_side_effects=pltpu.SideEffectType.DATAFLOW_SIDE_EFFECTING),
    )(*[_in_hbm(a) for a in list(srcs) + list(lands)], *extra)
    return out[0], out[1], out[2:2 + ns], out[2 + ns:2 + ns + n], out[-1]


def exchange_wait(send, recv, srcs, lands, layer, scatter, after, name, peers=ALL_PEERS):
    n, ns = len(lands), len(srcs)

    def body(*refs):
        ins, lz = refs[:ns], refs[ns:ns + n]
        send_ref, recv_ref = refs[ns + n], refs[ns + n + 1]
        x, y, c, _ = _me()
        for a in range(n):
            for p in peers:
                src, dst, peer = _exchange_refs(ins, lz, layer, scatter, a, x, y, c, 0)
                k = a * (N_DEV - 1) + p - 1
                cp = pltpu.make_async_remote_copy(src_ref=src, dst_ref=dst, send_sem=send_ref.at[k],
                                                  recv_sem=recv_ref.at[k], device_id=peer,
                                                  device_id_type=pl.DeviceIdType.MESH)
                cp.wait_send()
                cp.wait_recv()

    thru = [pltpu.HBM(a.shape, a.dtype) for a in list(srcs) + list(lands)]
    out = pl.pallas_call(
        body, name=name, in_specs=[HBM] * (ns + n) + [SEM, SEM, ANY], out_specs=[HBM] * (ns + n), out_shape=thru,
        input_output_aliases={i: i for i in range(ns + n)},
        compiler_params=pltpu.CompilerParams(has_side_effects=pltpu.SideEffectType.DATAFLOW_SIDE_EFFECTING),
    )(*srcs, *lands, send, recv, after)
    return out[ns:]


def place_own(src, land, me, layer, scatter, name, src_layer=None):
    create = isinstance(land, jax.ShapeDtypeStruct)
    r, c = src.shape[-2:]
    rt = r
    while rt % 32 == 0 and rt * c * 4 > 2 ** 21:
        rt //= 2

    def body(me_ref, src_ref, *rest):
        out_ref = rest[-1]
        out_ref[...] = src_ref[...].reshape(out_ref.shape).astype(out_ref.dtype)

    src_spec = (pl.BlockSpec((1, rt, c), lambda i, m: (m[0], i, 0)) if scatter else
                pl.BlockSpec((rt, c), lambda i, m: (i, 0)) if src_layer is None else
                pl.BlockSpec((1, rt, c), lambda i, m: (src_layer, i, 0)))
    out_spec = (pl.BlockSpec((1, rt, c), lambda i, m: (m[0], i, 0)) if layer is None
                else pl.BlockSpec((1, 1, rt, c), lambda i, m: (m[0], layer, i, 0)))
    grid_spec = pltpu.PrefetchScalarGridSpec(num_scalar_prefetch=1, grid=(r // rt,),
                                             in_specs=[src_spec] + ([] if create else [ANY]), out_specs=out_spec)
    return pl.pallas_call(body, name=name, grid_spec=grid_spec, out_shape=jax.ShapeDtypeStruct(land.shape, land.dtype),
                          input_output_aliases={} if create else {2: 0}, compiler_params=_cp(1),
                          )(*((me, src) if create else (me, src, land)))


def _scan_constants():
    r = lax.broadcasted_iota(jnp.int32, (CH, CH), 0)
    s = lax.broadcasted_iota(jnp.int32, (CH, CH), 1)
    lower = (s <= r).astype(F32)
    t = jnp.arange(CH)[:, None]
    mc = jnp.stack([lower, lower.T])
    mref = jnp.stack([(t <= CH // 2 - 1).astype(F32), (t >= CH // 2).astype(F32)])
    return mc, jnp.stack([lower.T, lower]), mref


def local_step(x, ctx, target, mod, lb, w, fetch=None, publish=None, small_ready=None, small_early=None):
    kept = {}

    def keep(l, part, grads):
        kept[(l, part)] = grads
        return 0.0

    fetch = fetch or (lambda l, part, after: w)
    publish = publish or keep
    n_layers = len(mod)
    mc, mtc, mrefc = _scan_constants()
    xs = jnp.concatenate([ctx, x], axis=0)
    saved, big = [], []
    for l in range(n_layers):
        wl = dict(fetch(l, "in", xs))
        parts, ht, iv = in_proj_fwd(xs, mod[l], w["nw1"][l], wl["win"][l])
        o, ck = hgrn_fwd(parts, iv, lb[l], mc, mtc, mrefc)
        wl.update(fetch(l, "rest", o))
        last = l == n_layers - 1
        x1, pa, pb, ym, yat, ybt, mt = mixer_fwd(xs, parts, o, mod[l], w["lnw"][l], w["lnb"][l], w["sw"][l], w["sb"][l],
                                                 w["hnw"][l], wl["wa"][l], wl["wb"][l], wl["wo"][l], last)
        av, h2t = ffn_up_fwd(x1, mod[l], w["nw2"][l], wl["wup"][l], last)
        x2, ac, y, z = ffn_down_fwd(x1, av, mod[l], w["cw"][l], w["cb"][l], wl["wd"][l], last)
        saved.append((xs, parts, iv, o, ck, x1, av, ac, y, z, ht, h2t, pa, pb, ym, yat, ybt, mt))
        big.append(wl)
        xs = x2
    loss, dx, dfw = loss_fwd_bwd(xs, target, w["fw"])
    g = {k: [None] * n_layers for k in ("nw1", "nw2", "lnw", "lnb", "sw", "sb", "hnw", "cw", "cb")}
    g["fw"] = dfw
    dmod, dlb = [None] * n_layers, [None] * n_layers
    tok = 0.0
    for l in reversed(range(n_layers)):
        x0, parts, iv, o, ck, x1, av, ac, y, z, ht, h2t, pa, pb, ym, yat, ybt, mt = saved[l]
        wl = big[l]
        last = l == n_layers - 1
        dav, dac, dout, dg2 = ffn_down_bwd(dx, ac, av, y, mod[l] + tok, wl["wd"][l], last)
        dwd = weight_grad_rows(z, dout, "ffn_down_bwd_w")
        dav, g["cw"][l], g["cb"][l] = conv_bwd(dav, dac, av, w["cw"][l], last)
        dx1, g["nw2"][l], dmod2 = ffn_up_bwd_x(dx, x1, dav, mod[l], w["nw2"][l], wl["wup"][l], last)
        dwup = weight_grad(h2t, dav, FF_SLOT, "ffn_up_bwd_w")
        tok = publish(l, "ffn", {"wd": dwd, "wup": dwup})
        (dparts, do, dy, dpa, dpb, g["lnw"][l], g["lnb"][l], g["sw"][l], g["sb"][l], g["hnw"][l],
         dg1) = mixer_bwd(dx1, parts, o, pa, pb, ym, mod[l] + tok, w["lnw"][l], w["lnb"][l], w["sw"][l], w["sb"][l],
                          w["hnw"][l], wl["wa"][l], wl["wb"][l], wl["wo"][l], last)
        tok = publish(l, "mix", {"wa": weight_grad(yat, dpa, D, "mixer_bwd_wa"), "wb": weight_grad(ybt, dpb, D, "mixer_bwd_wb"),
                                 "wo": weight_grad(mt, dy, D, "mixer_bwd_wo")})
        if l == 0 and small_early:
            dmod[0] = jnp.concatenate([jnp.zeros((2, 2, 1, D), F32), dg1, dmod2, dg2], axis=1)
            tok = tok + small_early(loss[0, 0], g, dmod, dlb)
        dq, df, di, dlb_f = hgrn_bwd(0, parts, lb[l] + tok, mc, mtc, mrefc, ck, do)
        dparts, dlb_b = hgrn_bwd(1, parts, lb[l], mc, mtc, mrefc, ck, do, (dq, df, di), dparts)
        dlb[l] = jnp.concatenate([dlb_f, dlb_b], axis=0)
        tok = publish(l, "in", {"win": weight_grad(ht, dparts, IN_SLOT, "in_proj_bwd_w")})
        dx, g["nw1"][l], dmod1 = in_proj_bwd_x(dx1, x0, dparts, mod[l], w["nw1"][l], wl["win"][l], after=tok,
                                               latent_only=l == 0)
        dmod[l] = jnp.concatenate([dmod1, dg1, dmod2, dg2], axis=1)
    done = small_ready(loss[0, 0], g, dmod, dlb) if small_ready else 0.0
    for (l, part), grads in kept.items():
        for k, v in grads.items():
            g.setdefault(k, [None] * n_layers)[l] = v
    return loss[0, 0], dx, g, dmod, dlb, done


ROW = 1024
REPLICATED = ("norm1_w", "sgu_ln_w", "sgu_ln_b", "sgu_w", "sgu_b", "hgrn_lower_bounds", "hgrn_norm_w", "norm2_w",
              "ffn_conv_b", "final_norm_w")
WEIGHT_ORDER = ("c_ctx", "ada_w", "ada_b", "norm1_w", "w_in", "sgu_ln_w", "sgu_ln_b", "sgu_w", "sgu_b", "hgrn_lower_bounds",
                "hgrn_norm_w", "w_branch_a", "w_branch_b", "w_out", "norm2_w", "ffn_w_up", "ffn_conv_w", "ffn_conv_b",
                "ffn_w_down", "final_norm_w")


def _rows_of(n):
    return -(-n // (8 * ROW)) * 8


def _pack(arrs, total_rows=None):
    parts = []
    for a in arrs:
        flat = a.reshape(-1).astype(F32)
        rows = _rows_of(flat.shape[0])
        parts.append(jnp.pad(flat, (0, rows * ROW - flat.shape[0])).reshape(rows, ROW))
    have = sum(p.shape[0] for p in parts)
    if total_rows is not None and total_rows > have:
        parts.append(jnp.zeros((total_rows - have, ROW), F32))
    return jnp.concatenate(parts, axis=0)


def _unpack(packed, shapes):
    lead = packed.shape[:-2]
    out, r0 = [], 0
    for s in shapes:
        n = math.prod(s)
        rows = _rows_of(n)
        out.append(packed[..., r0:r0 + rows, :].reshape(lead + (rows * ROW,))[..., :n].reshape(lead + tuple(s)))
        r0 += rows
    return out


def kernel(x, c, ctx, c_ctx, ada_w, ada_b, norm1_w, w_in, sgu_ln_w, sgu_ln_b, sgu_w, sgu_b, hgrn_lower_bounds, hgrn_norm_w, w_branch_a, w_branch_b, w_out, norm2_w, ffn_w_up, ffn_conv_w, ffn_conv_b, ffn_w_down, final_norm_w, loss_target, m_c_ctx, m_ada_w, m_ada_b, m_norm1_w, m_w_in, m_sgu_ln_w, m_sgu_ln_b, m_sgu_w, m_sgu_b, m_hgrn_lower_bounds, m_hgrn_norm_w, m_w_branch_a, m_w_branch_b, m_w_out, m_norm2_w, m_ffn_w_up, m_ffn_conv_w, m_ffn_conv_b, m_ffn_w_down, m_final_norm_w, v_c_ctx, v_ada_w, v_ada_b, v_norm1_w, v_w_in, v_sgu_ln_w, v_sgu_ln_b, v_sgu_w, v_sgu_b, v_hgrn_lower_bounds, v_hgrn_norm_w, v_w_branch_a, v_w_branch_b, v_w_out, v_norm2_w, v_ffn_w_up, v_ffn_conv_w, v_ffn_conv_b, v_ffn_w_down, v_final_norm_w):
    wts = dict(c_ctx=c_ctx, ada_w=ada_w, ada_b=ada_b, norm1_w=norm1_w, w_in=w_in, sgu_ln_w=sgu_ln_w, sgu_ln_b=sgu_ln_b,
               sgu_w=sgu_w, sgu_b=sgu_b, hgrn_lower_bounds=hgrn_lower_bounds, hgrn_norm_w=hgrn_norm_w, w_branch_a=w_branch_a,
               w_branch_b=w_branch_b, w_out=w_out, norm2_w=norm2_w, ffn_w_up=ffn_w_up, ffn_conv_w=ffn_conv_w,
               ffn_conv_b=ffn_conv_b, ffn_w_down=ffn_w_down, final_norm_w=final_norm_w)
    mom1 = dict(c_ctx=m_c_ctx, ada_w=m_ada_w, ada_b=m_ada_b, norm1_w=m_norm1_w, w_in=m_w_in, sgu_ln_w=m_sgu_ln_w,
                sgu_ln_b=m_sgu_ln_b, sgu_w=m_sgu_w, sgu_b=m_sgu_b, hgrn_lower_bounds=m_hgrn_lower_bounds,
                hgrn_norm_w=m_hgrn_norm_w, w_branch_a=m_w_branch_a, w_branch_b=m_w_branch_b, w_out=m_w_out, norm2_w=m_norm2_w,
                ffn_w_up=m_ffn_w_up, ffn_conv_w=m_ffn_conv_w, ffn_conv_b=m_ffn_conv_b, ffn_w_down=m_ffn_w_down,
                final_norm_w=m_final_norm_w)
    mom2 = dict(c_ctx=v_c_ctx, ada_w=v_ada_w, ada_b=v_ada_b, norm1_w=v_norm1_w, w_in=v_w_in, sgu_ln_w=v_sgu_ln_w,
                sgu_ln_b=v_sgu_ln_b, sgu_w=v_sgu_w, sgu_b=v_sgu_b, hgrn_lower_bounds=v_hgrn_lower_bounds,
                hgrn_norm_w=v_hgrn_norm_w, w_branch_a=v_w_branch_a, w_branch_b=v_w_branch_b, w_out=v_w_out, norm2_w=v_norm2_w,
                ffn_w_up=v_ffn_w_up, ffn_conv_w=v_ffn_conv_w, ffn_conv_b=v_ffn_conv_b, ffn_w_down=v_ffn_w_down,
                final_norm_w=v_final_norm_w)
    n_layers = w_in.shape[0]
    layers = range(n_layers)
    me = 4 * lax.axis_index("x") + 2 * lax.axis_index("y") + lax.axis_index("c")
    ada_cols = ada_w.shape[-1]

    big = ("w_in", "ffn_w_up", "w_branch_a", "w_branch_b", "w_out", "ffn_w_down")
    short = {"w_in": "win", "ffn_w_up": "wup", "w_branch_a": "wa", "w_branch_b": "wb", "w_out": "wo", "ffn_w_down": "wd"}
    me1 = me.reshape(1).astype(jnp.int32)
    groups = [[("w_in", 0)], [(k, 0) for k in big[1:]], [("w_in", 1)], [(k, 1) for k in big[1:]]]
    in_flight, started = [], 0.0

    def own_slots(n):
        return [place_own(wts[k], jax.ShapeDtypeStruct((N_DEV,) + wts[k].shape[1:], BF16), me1, None, False,
                          f"gather_own_{short[k]}_{l}", src_layer=l) for k, l in groups[n]]

    def start_group(n, lands, after):
        in_flight.append(exchange_start([], lands, None, False, f"gather_weights_start_{n}", after=after,
                                        peers=SAME_CORE_AND_SIBLING if n == 0 else ALL_PEERS))
        return in_flight[-1][-1]

    (c_all,) = all_gather([c], "gather_c")
    c_all = c_all.reshape(N_DEV, D)
    token = start_group(0, own_slots(0), c_all)
    later = [own_slots(n) for n in range(1, len(groups))]
    cctx8 = jnp.broadcast_to(c_ctx[None, :], (N_DEV, D))
    ada_b_cols = lax.dynamic_slice_in_dim(ada_b, me * ada_cols, ada_cols, axis=1)[:, None, :]
    mod_cols = ada_fwd(c_all, cctx8, ada_w, ada_b_cols)
    xs = jnp.concatenate([ctx[0], x[0]], axis=0)
    lb1 = lower_bounds(hgrn_lower_bounds)
    mod_all, conv_all = all_gather([mod_cols, ffn_conv_w.reshape(n_layers, 9, -1)], "gather_mod_conv",
                                   after=[token, xs, lb1] + [a for lands in later for a in lands])
    conv_full = [conv_all[:, l].transpose(1, 0, 2).reshape(9, N_FFK, FF_SLOT).transpose(1, 0, 2) for l in layers]
    for n in range(1, len(groups)):
        token = start_group(n, later[n - 1], mod_all if n == 1 else token)
    for started_group in in_flight:
        started = started + started_group[-1][0, 0]

    def as_used(k, a):
        return a if k in ("w_in", "ffn_w_up") else a.reshape(N_FFK, FF_SLOT, D) if k == "ffn_w_down" else a.reshape(D, D)

    arrived = {}

    def fetch(l, part, after):
        n = {(0, "in"): 0, (0, "rest"): 1, (1, "in"): 2, (1, "rest"): 3}.get((l, part))
        if n is not None:
            send, recv, _, lands, _ = in_flight[n]
            first = n == 0
            got = exchange_wait(send, recv, [], lands, None, False, after, f"gather_weights_wait_{n}",
                                peers=SAME_CORE_AND_SIBLING if first else ALL_PEERS)
            if first:
                send, recv, _, lands, _ = exchange_start([], got, None, False, "gather_weights_pass_on", peers=OTHER_CHIPS,
                                                         forward=True)
                got = exchange_wait(send, recv, [], lands, None, False, after, "gather_weights_passed_on", peers=OTHER_CHIPS)
            for (k, ll), a in zip(groups[n], got):
                arrived.setdefault(short[k], [None] * n_layers)[ll] = as_used(k, a)
        return arrived

    mod_x = lax.dynamic_index_in_dim(mod_all[:, :, 0], me, axis=2, keepdims=False)
    mod_c = mod_all[:, :, 1, 0]
    mod = [jnp.stack([mod_c[:, l].reshape(6, 1, D), mod_x[:, l].reshape(6, 1, D)]) for l in layers]
    mod[0] = mod[0] + started

    lb = [jnp.zeros((2, 1, D), F32), lb1.reshape(2, 1, D)]

    w = {
        "nw1": [norm1_w[l][None] for l in layers], "nw2": [norm2_w[l][None] for l in layers],
        "lnw": [sgu_ln_w[l][None] for l in layers], "lnb": [sgu_ln_b[l][None] for l in layers],
        "sw": [sgu_w[l] for l in layers], "sb": [sgu_b[l][:, :, None] for l in layers],
        "hnw": [hgrn_norm_w[l][None] for l in layers], "cw": conv_full,
        "cb": [ffn_conv_b[l].reshape(N_FFK, 1, FF_SLOT) for l in layers], "fw": final_norm_w[None],
    }
    long = {v: k for k, v in short.items()}
    landing, sent = {}, []

    def publish(l, part, grads):
        keys = [long[k] for k in grads]
        slots = [a.reshape((N_DEV, -1, a.shape[-1])) for a in grads.values()]
        zones = [place_own(s, landing.get(k, jax.ShapeDtypeStruct((N_DEV, n_layers) + s.shape[1:], s.dtype)), me1, l, True,
                           f"scatter_own_{short[k]}_{l}") for k, s in zip(keys, slots)]
        send, recv, srcs, zones, token = exchange_start(slots, zones, l, True, f"scatter_grads_start_{part}_{l}")
        landing.update(zip(keys, zones))
        sent.append((keys, l, part, send, recv, srcs, token))
        return token[0, 0]

    out = {}
    flat2 = lambda a: a.reshape(-1, a.shape[-1])

    def finish(part, after):
        done = []
        for keys, l, p, send, recv, srcs, _ in sent:
            if p == part:
                zones = exchange_wait(send, recv, srcs, [landing[k] for k in keys], l, True, after,
                                      f"scatter_grads_wait_{part}_{l}")
                landing.update(zip(keys, zones))
                done = keys
        for k in done:
            r = landing[k]
            res = adamw(flat2(wts[k]), flat2(mom1[k]), flat2(mom2[k]), r.reshape(N_DEV, -1, r.shape[-1]), "adamw_" + k)
            out[k] = tuple(a.reshape(wts[k].shape) for a in res)

    rep_rows = -(-sum(_rows_of(wts[k].size) for k in REPLICATED) // 64) * 64
    conv_rows = _rows_of(n_layers * 9 * D_FF)
    dmod_rows = _rows_of(n_layers * 6 * D)
    early = {}

    def small_early(loss_part, g, dmod, dlb):
        d_hlb = lower_bounds_bwd(hgrn_lower_bounds, dlb[1].reshape(1, 2 * D))
        st = lambda k: jnp.stack([jnp.zeros((1, D), F32) if a is None else a for a in g[k]])
        rep_grads = {"norm1_w": st("nw1"), "sgu_ln_w": st("lnw"), "sgu_ln_b": st("lnb"), "sgu_w": st("sw"), "sgu_b": st("sb"),
                     "hgrn_lower_bounds": d_hlb, "hgrn_norm_w": st("hnw"), "norm2_w": st("nw2"), "ffn_conv_b": st("cb"),
                     "final_norm_w": g["fw"]}
        d_conv = jnp.stack([g["cw"][l].transpose(1, 0, 2).reshape(9, D_FF) for l in layers])
        dmod_x = jnp.stack([dmod[l][1].reshape(6 * D) for l in layers])
        dmod_c = jnp.stack([dmod[l][0].reshape(6 * D) for l in layers])
        small = jnp.concatenate([_pack([rep_grads[k] for k in REPLICATED], rep_rows),
                                 _pack([d_conv, dmod_x, dmod_c, loss_part.reshape(1)])], axis=0)
        zone = place_own(small, jax.ShapeDtypeStruct((N_DEV,) + small.shape, F32), me1, None, False, "gather_small_own")
        early["send"], early["recv"], _, early["zones"], token = exchange_start([], [zone], None, False, "gather_small_start")
        return token[0, 0]

    def small_ready(loss_part, g, dmod, dlb):
        late = _pack([g["nw1"][0], dmod[0][1, 0:2], dmod[0][0, 0:2]])
        for part in ("ffn", "mix"):
            finish(part, late)
        (late_all,) = all_gather([late], "gather_small_late", after=out["w_out"][0])
        (small_all,) = exchange_wait(early["send"], early["recv"], [], early["zones"], None, False, late_all,
                                     "gather_small_wait")
        at_x = rep_rows + conv_rows
        small_all = small_all.at[:, 0:1].set(late_all[:, 0:1])
        small_all = small_all.at[:, at_x:at_x + 2].set(late_all[:, 8:10])
        small_all = small_all.at[:, at_x + dmod_rows:at_x + dmod_rows + 2].set(late_all[:, 16:18])
        d_conv_shape, dmod_shape = (n_layers, 9, D_FF), (n_layers, 6 * D)
        conv_g, dmx_all, dmc_all, loss_all = _unpack(small_all[:, rep_rows:], [d_conv_shape, dmod_shape, dmod_shape, (1,)])
        out["loss"] = functools.reduce(lambda a, b: a + b, [loss_all[k, 0] for k in range(N_DEV)])

        rep = adamw(_pack([wts[k] for k in REPLICATED], rep_rows), _pack([mom1[k] for k in REPLICATED], rep_rows),
                    _pack([mom2[k] for k in REPLICATED], rep_rows), small_all, "adamw_replicated")
        rep = [_unpack(r, [wts[k].shape for k in REPLICATED]) for r in rep]
        for n, k in enumerate(REPLICATED):
            out[k] = tuple(r[n] for r in rep)

        conv_mine = lax.dynamic_index_in_dim(conv_g.reshape(N_DEV, n_layers, 9, N_DEV, -1), me, axis=3, keepdims=False)
        res = adamw(flat2(ffn_conv_w), flat2(m_ffn_conv_w), flat2(v_ffn_conv_w),
                    conv_mine.reshape(N_DEV, -1, conv_mine.shape[-1]), "adamw_conv_w")
        out["ffn_conv_w"] = tuple(r.reshape(ffn_conv_w.shape) for r in res)

        out["ada_b"] = tuple(adamw(ada_b, m_ada_b, v_ada_b, jnp.concatenate([dmx_all, dmc_all], axis=0), "adamw_ada_b"))

        cols_of = lambda a: lax.dynamic_slice_in_dim(a, me * ada_cols, ada_cols, axis=2).transpose(1, 0, 2)
        d_ada_w, d_cctx = ada_bwd(c_all, cctx8, ada_w, ada_b_cols, cols_of(dmx_all), cols_of(dmc_all))
        res = adamw(flat2(ada_w), flat2(m_ada_w), flat2(v_ada_w), flat2(d_ada_w)[None], "adamw_ada_w")
        out["ada_w"] = tuple(r.reshape(ada_w.shape) for r in res)
        (d_cctx_all,) = all_gather([d_cctx], "gather_c_ctx_grad")
        res = adamw(c_ctx[None], m_c_ctx[None], v_c_ctx[None], d_cctx_all, "adamw_c_ctx")
        out["c_ctx"] = tuple(r[0] for r in res)
        return d_cctx_all

    _, grad_x, _, _, _, small_done = local_step(x[0], ctx[0], loss_target[0], mod, lb, w, fetch, publish, small_ready,
                                                small_early)
    loss = out["loss"]

    finish("in", small_done)
    return (loss, grad_x[None]) + tuple(out[k][n] for n in range(4) for k in WEIGHT_ORDER)
```

```python
import functools
import math

import jax
import jax.numpy as jnp
from jax import lax
from jax.experimental import pallas as pl
from jax.experimental.pallas import tpu as pltpu

F32 = jnp.float32
BF16 = jnp.bfloat16
HIGHEST = lax.Precision.HIGHEST

N_DEV = 8
AXES = ("x", "y", "c")
D = 1024
CTX = 256
TM = 256
CH = 64
SGU_CH = 128
HEADS = 8
HD = 128
GRID_W = 64
D_IN = 9 * D
IN_SLOT = D_IN // N_DEV
D_FF = 2816
FF_SLOT = 2 * D_FF // N_DEV
N_FFK = D_FF // FF_SLOT
RMS_EPS = 1e-6
LN_EPS = 1e-5
ADAM_LR, ADAM_B1, ADAM_B2, ADAM_EPS, ADAM_WD, ADAM_STEP = 0.001, 0.9, 0.999, 1e-08, 0.01, 10
VMEM_LIMIT_V7X = 56 * 2 ** 20
GRAD_WIRE = jnp.bfloat16

VMEM_WHOLE = pl.BlockSpec(memory_space=pltpu.VMEM)
ANY = pl.BlockSpec(memory_space=pl.ANY)


def _cp(n_axes):
    return pltpu.CompilerParams(dimension_semantics=("arbitrary",) * n_axes, vmem_limit_bytes=VMEM_LIMIT_V7X)


def _dot(a, b, dims):
    return lax.dot_general(a.astype(BF16), b.astype(BF16), (dims, ((), ())), preferred_element_type=F32)


@jax.custom_vjp
def mm(a, b):
    return _dot(a, b, ((1,), (0,)))


mm.defvjp(lambda a, b: (mm(a, b), (a, b)),
          lambda r, g: (_dot(g, r[1], ((1,), (1,))).astype(r[0].dtype), _dot(r[0], g, ((0,), (0,))).astype(r[1].dtype)))


@jax.custom_vjp
def mm_nt(a, b):
    return _dot(a, b, ((1,), (1,)))


mm_nt.defvjp(lambda a, b: (mm_nt(a, b), (a, b)),
             lambda r, g: (_dot(g, r[1], ((1,), (0,))).astype(r[0].dtype), _dot(g, r[0], ((0,), (0,))).astype(r[1].dtype)))


@jax.custom_vjp
def mm_tn(a, b):
    return _dot(a, b, ((0,), (0,)))


mm_tn.defvjp(lambda a, b: (mm_tn(a, b), (a, b)),
             lambda r, g: (_dot(r[1], g, ((1,), (1,))).astype(r[0].dtype), _dot(r[0], g, ((1,), (0,))).astype(r[1].dtype)))


def _tri_dot(m, g):
    hi = g.astype(BF16)
    low = (g - hi.astype(F32)).astype(BF16)
    n = g.shape[1]
    out = jnp.dot(m.astype(BF16), jnp.concatenate([hi, low], axis=1), preferred_element_type=F32)
    return out[:, :n] + out[:, n:]


@jax.custom_vjp
def _cum(m, mt, g):
    return _tri_dot(m, g)


_cum.defvjp(lambda m, mt, g: (_cum(m, mt, g), (m, mt)),
            lambda r, d: (jnp.zeros_like(r[0]), jnp.zeros_like(r[1]), _tri_dot(r[1], d)))


def _silu(x):
    return x * jax.nn.sigmoid(x)


def _gelu(x):
    return 0.5 * x * (1.0 + jnp.tanh(math.sqrt(2.0 / math.pi) * (x + 0.044715 * (x * x * x))))


def _rms(x, w):
    return x * lax.rsqrt(jnp.mean(x * x, axis=-1, keepdims=True) + RMS_EPS) * w


def _norm_mod(x, w, shift, scale):
    return _rms(x, w) * (1.0 + scale) + shift


def _hsl(h):
    return slice(h * HD, (h + 1) * HD)


def _hgrn_chunk(st, qz, fz, iv, lb, m, mt, mref):
    hs = range(HEADS)
    keep = [1.0 - lb[h] for h in hs]
    sg = [jax.nn.sigmoid(fz[h]) for h in hs]
    g = [jnp.log(lb[h] + keep[h] * sg[h]) for h in hs]
    k = [keep[h] * (1.0 - sg[h]) for h in hs]
    q = [_silu(qz[h]) for h in hs]
    b = [_cum(m, mt, g[h]) for h in hs]
    ref = [jnp.sum(mref * g[h], axis=0, keepdims=True) for h in hs]
    last = [jnp.sum(g[h], axis=0, keepdims=True) for h in hs]
    qa = [q[h] * jnp.exp(b[h] - ref[h]) for h in hs]
    ka = [k[h] * jnp.exp(ref[h] - b[h]) for h in hs]
    scores = [jnp.where(m > 0.5, mm_nt(qa[h], ka[h]), 0.0) for h in hs]
    inter = [mm_nt(qa[h] * jnp.exp(ref[h]), st[h]) for h in hs]
    kv = [mm_tn(iv[h], ka[h] * jnp.exp(last[h] - ref[h])) for h in hs]
    outs = [mm(scores[h], iv[h]) + inter[h] for h in hs]
    news = [jnp.exp(last[h]) * st[h] + kv[h] for h in hs]
    return outs, news


def _sgu_fn(ub, vb, lnw, lnb, sw, sb):
    gv = [_gelu(v) for v in vb]
    mu = sum(jnp.sum(t, axis=-1, keepdims=True) for t in gv) / D
    var = sum(jnp.sum((t - mu) * (t - mu), axis=-1, keepdims=True) for t in gv) / D
    inv = lax.rsqrt(var + LN_EPS)
    cols = []
    for g in range(HEADS):
        vn = (gv[g] - mu) * inv * lnw[g] + lnb[g]
        cols.append(_gelu(ub[g]) * (mm(sw[g], vn) + sb[g]))
    return jnp.concatenate(cols, axis=1)


def _readout_fn(ob, og, hnw):
    r = [o * lax.rsqrt(jnp.mean(o * o, axis=-1, keepdims=True) + RMS_EPS) * hnw for o in ob]
    return jnp.concatenate(r, axis=1) * _silu(og)


def _glu_fn(ac, v):
    return _gelu(ac) * v


def _stream_row(tm):
    n_ctx = CTX // tm
    return lambda i: (jnp.where(i < n_ctx, 0, 1), 0, 0, 0)


def in_proj_fwd(x, mod, nw, wg):
    t = x.shape[0]

    def body(x_ref, mod_ref, nw_ref, w_ref, out_ref, ht_ref, iv_ref):
        h32 = _norm_mod(x_ref[...], nw_ref[...], mod_ref[0, 0], mod_ref[0, 1])
        ht_ref[...] = h32.T.astype(BF16)
        h = h32.astype(BF16)
        for j in range(N_DEV):
            out_ref[:, j * IN_SLOT:(j + 1) * IN_SLOT] = jnp.dot(h, w_ref[j], preferred_element_type=F32)
        iv_ref[...] = out_ref[:, 3 * D:4 * D].astype(BF16)

    return pl.pallas_call(
        body, name="in_proj_fwd", grid=(t // TM,),
        in_specs=[pl.BlockSpec((TM, D), lambda i: (i, 0)), pl.BlockSpec((1, 6, 1, D), _stream_row(TM)),
                  pl.BlockSpec((1, D), lambda i: (0, 0)), VMEM_WHOLE],
        out_specs=[pl.BlockSpec((TM, D_IN), lambda i: (i, 0)), pl.BlockSpec((D, TM), lambda i: (0, i)),
                   pl.BlockSpec((TM, D), lambda i: (i, 0))],
        out_shape=[jax.ShapeDtypeStruct((t, D_IN), F32), jax.ShapeDtypeStruct((D, t), BF16), jax.ShapeDtypeStruct((t, D), BF16)],
        compiler_params=_cp(1))(x, mod, nw, wg)


SCAN_STEP = 4
SCAN_ROWS = SCAN_STEP * CH


def _scan_block(nb):
    ncb = CTX // SCAN_ROWS

    def block(d, s):
        bwd = jnp.where(s < ncb, ncb - 1 - s, nb + ncb - 1 - s)
        return jnp.where(d == 0, s, bwd)
    return block


def hgrn_fwd(parts, iv, lb, mc, mtc, mrefc):
    t = parts.shape[0]
    nb = t // SCAN_ROWS
    block = _scan_block(nb)

    def body(q_ref, f_ref, i_ref, lb_ref, m_ref, mt_ref, mr_ref, o_ref, ck_ref, st):
        d = pl.program_id(0)

        @pl.when(pl.program_id(1) == 0)
        def _():
            st[...] = jnp.zeros_like(st)

        for c in range(SCAN_STEP):
            rows = pl.ds(pl.multiple_of(jnp.where(d == 0, c * CH, (SCAN_STEP - 1 - c) * CH), CH), CH)
            ck_ref[0, c] = st[...].astype(BF16)
            outs, news = _hgrn_chunk([st[h] for h in range(HEADS)], [q_ref[rows, _hsl(h)] for h in range(HEADS)],
                                     [f_ref[rows, _hsl(h)] for h in range(HEADS)], [i_ref[rows, _hsl(h)] for h in range(HEADS)],
                                     [lb_ref[0, :, _hsl(h)] for h in range(HEADS)], m_ref[0], mt_ref[0], mr_ref[0])
            for h in range(HEADS):
                o_ref[0, rows, _hsl(h)] = outs[h].astype(BF16)
                st[h] = news[h]

    const = lambda d, s: (d, 0, 0)
    at = lambda k: pl.BlockSpec((SCAN_ROWS, D), lambda d, s: (block(d, s), k(d)))
    return pl.pallas_call(
        body, name="hgrn_fwd", grid=(2, nb),
        in_specs=[at(lambda d: 0), at(lambda d: 1 + d), at(lambda d: 0), pl.BlockSpec((1, 1, D), const),
                  pl.BlockSpec((1, CH, CH), const), pl.BlockSpec((1, CH, CH), const), pl.BlockSpec((1, CH, 1), const)],
        out_specs=[pl.BlockSpec((1, SCAN_ROWS, D), lambda d, s: (d, block(d, s), 0)),
                   pl.BlockSpec((1, SCAN_STEP, HEADS, HD, HD), lambda d, s: (d, s, 0, 0, 0))],
        out_shape=[jax.ShapeDtypeStruct((2, t, D), BF16), jax.ShapeDtypeStruct((2, nb * SCAN_STEP, HEADS, HD, HD), BF16)],
        scratch_shapes=[pltpu.VMEM((HEADS, HD, HD), F32)], compiler_params=_cp(2))(parts, parts, iv, lb, mc, mtc, mrefc)


def _mixer_tile(rows, u_ref, v_ref, og_ref, o_ref, lnw_ref, lnb_ref, sw_ref, sb_ref, hnw_ref):
    n = (rows.stop - rows.start) // SGU_CH
    yas, vjps = [], []
    for c in range(n):
        r = slice(rows.start + c * SGU_CH, rows.start + (c + 1) * SGU_CH)
        ya, vjp_a = jax.vjp(_sgu_fn, [u_ref[r, _hsl(g)] for g in range(HEADS)], [v_ref[r, _hsl(g)] for g in range(HEADS)],
                            [lnw_ref[:, _hsl(g)] for g in range(HEADS)], [lnb_ref[:, _hsl(g)] for g in range(HEADS)],
                            [sw_ref[g] for g in range(HEADS)], [sb_ref[g] for g in range(HEADS)])
        yas.append(ya)
        vjps.append(vjp_a)
    yb, vjp_b = jax.vjp(_readout_fn, [o_ref[0, rows, _hsl(h)].astype(F32) + o_ref[1, rows, _hsl(h)].astype(F32)
                                      for h in range(HEADS)],
                        og_ref[rows, :], hnw_ref[...])
    return (yas[0] if n == 1 else jnp.concatenate(yas, axis=0)), yb, vjps, vjp_b


def _part_specs(tm, first, n):
    return [pl.BlockSpec((tm, D), functools.partial(lambda k, i: (i, k), first + k)) for k in range(n)]


def _unless_ctx(skip_ctx, is_ctx, zero_refs, work):
    if not skip_ctx:
        return work()

    @pl.when(is_ctx)
    def _():
        for r in zero_refs:
            r[...] = jnp.zeros_like(r)

    pl.when(jnp.logical_not(is_ctx))(work)


def mixer_fwd(x, parts, o, mod, lnw, lnb, sw, sb, hnw, wa, wb, wo, skip_ctx):
    t = x.shape[0]

    def body(x_ref, u_ref, v_ref, og_ref, ga_ref, gb_ref, o_ref, mod_ref, lnw_ref, lnb_ref, sw_ref, sb_ref, hnw_ref,
             wa_ref, wb_ref, wo_ref, out_ref, pa_ref, pb_ref, y_ref, yat_ref, ybt_ref, mt_ref):
        def work():
            ya, yb, _, _ = _mixer_tile(slice(0, TM), u_ref, v_ref, og_ref, o_ref, lnw_ref, lnb_ref, sw_ref, sb_ref, hnw_ref)
            pa, pb = mm(ya, wa_ref[...]), mm(yb, wb_ref[...])
            merged = jax.nn.sigmoid(ga_ref[...]) * pa + jax.nn.sigmoid(gb_ref[...]) * pb
            y = mm(merged, wo_ref[...])
            out_ref[...] = x_ref[...] + mod_ref[0, 2] * y
            pa_ref[...], pb_ref[...], y_ref[...] = pa.astype(BF16), pb.astype(BF16), y.astype(BF16)
            yat_ref[...], ybt_ref[...], mt_ref[...] = ya.T.astype(BF16), yb.T.astype(BF16), merged.T.astype(BF16)

        _unless_ctx(skip_ctx, pl.program_id(0) == 0, (out_ref, pa_ref, pb_ref, y_ref, yat_ref, ybt_ref, mt_ref), work)

    vec = lambda n: pl.BlockSpec((1, n), lambda i: (0, 0))
    tile = pl.BlockSpec((TM, D), lambda i: (i, 0))
    tile_t = pl.BlockSpec((D, TM), lambda i: (0, i))
    return pl.pallas_call(
        body, name="mixer_fwd", grid=(t // TM,),
        in_specs=[tile] + _part_specs(TM, 4, 5)
        + [pl.BlockSpec((2, TM, D), lambda i: (0, i, 0)), pl.BlockSpec((1, 6, 1, D), _stream_row(TM)), vec(D), vec(D),
           VMEM_WHOLE, VMEM_WHOLE, vec(HD), VMEM_WHOLE, VMEM_WHOLE, VMEM_WHOLE],
        out_specs=[tile] * 4 + [tile_t] * 3,
        out_shape=[jax.ShapeDtypeStruct((t, D), F32)] + [jax.ShapeDtypeStruct((t, D), BF16)] * 3
        + [jax.ShapeDtypeStruct((D, t), BF16)] * 3, compiler_params=_cp(1),
    )(x, parts, parts, parts, parts, parts, o, mod, lnw, lnb, sw, sb, hnw, wa, wb, wo)


def ffn_up_fwd(x, mod, nw, wg, skip_ctx):
    t = x.shape[0]

    def body(x_ref, mod_ref, nw_ref, w_ref, out_ref, ht_ref):
        def work():
            h32 = _norm_mod(x_ref[...], nw_ref[...], mod_ref[0, 3], mod_ref[0, 4])
            ht_ref[...] = h32.T.astype(BF16)
            h = h32.astype(BF16)
            for j in range(N_DEV):
                out_ref[j] = jnp.dot(h, w_ref[j], preferred_element_type=F32)

        _unless_ctx(skip_ctx, pl.program_id(0) == 0, (out_ref, ht_ref), work)

    return pl.pallas_call(
        body, name="ffn_up_fwd", grid=(t // TM,),
        in_specs=[pl.BlockSpec((TM, D), lambda i: (i, 0)), pl.BlockSpec((1, 6, 1, D), _stream_row(TM)),
                  pl.BlockSpec((1, D), lambda i: (0, 0)), VMEM_WHOLE],
        out_specs=[pl.BlockSpec((N_DEV, TM, FF_SLOT), lambda i: (0, i, 0)), pl.BlockSpec((D, TM), lambda i: (0, i))],
        out_shape=[jax.ShapeDtypeStruct((N_DEV, t, FF_SLOT), F32), jax.ShapeDtypeStruct((D, t), BF16)],
        compiler_params=_cp(1))(x, mod, nw, wg)


def _halo_specs(nt):
    per = TM // GRID_W
    last = nt * per - 1
    return [pl.BlockSpec((N_FFK, GRID_W, FF_SLOT), lambda i: (0, jnp.maximum(i * per - 1, 0), 0)),
            pl.BlockSpec((N_FFK, TM, FF_SLOT), lambda i: (0, i, 0)),
            pl.BlockSpec((N_FFK, GRID_W, FF_SLOT), lambda i: (0, jnp.minimum(i * per + per, last), 0))]


def _with_halo(prev_ref, main_ref, next_ref, k, i, nt):
    prev = jnp.where(i >= 2, prev_ref[k], 0.0)
    nxt = jnp.where((i >= 1) & (i <= nt - 2), next_ref[k], 0.0)
    return jnp.concatenate([prev, main_ref[k], nxt], axis=0)


def _tap_valid(dc, i, n_rows, offset):
    r = lax.broadcasted_iota(jnp.int32, (n_rows, 1), 0) - offset
    col = jnp.bitwise_and(r, GRID_W - 1)
    pos = jnp.where(i == 0, r, col) + dc
    return (pos >= 0) & (pos < jnp.where(i == 0, TM, GRID_W))


def _row_weight(cw_ref, k, dr, dc, i):
    w = cw_ref[k, 3 * (dr + 1) + dc + 1:3 * (dr + 1) + dc + 2, :]
    return w if dr == 0 else jnp.where(i == 0, 0.0, w)


def ffn_down_fwd(x, av, mod, cw, cb, wd, skip_ctx):
    t = x.shape[0]
    nt = t // TM

    def body(x_ref, ap_ref, am_ref, an_ref, v_ref, mod_ref, cw_ref, cb_ref, wd_ref, out_ref, ac_ref, y_ref, z_ref):
        i = pl.program_id(0)

        def work():
            y = None
            for k in range(N_FFK):
                a_ext = _with_halo(ap_ref, am_ref, an_ref, k, i, nt)
                conv = jnp.zeros((TM, FF_SLOT), F32) + cb_ref[k]
                for dc in (-1, 0, 1):
                    col = functools.reduce(lambda p, q: p + q, [a_ext[GRID_W + GRID_W * dr:GRID_W + GRID_W * dr + TM]
                                                                * _row_weight(cw_ref, k, dr, dc, i) for dr in (-1, 0, 1)])
                    conv = conv + (col if dc == 0 else
                                   jnp.where(_tap_valid(dc, i, TM, 0), pltpu.roll(col, (-dc) % TM, 0), 0.0))
                ac_ref[k] = conv.astype(BF16)
                z = _glu_fn(conv, v_ref[k]).astype(BF16)
                z_ref[k] = z
                part = mm(z, wd_ref[k])
                y = part if y is None else y + part
            y_ref[...] = y
            out_ref[...] = x_ref[...] + mod_ref[0, 5] * y

        _unless_ctx(skip_ctx, i == 0, (out_ref, ac_ref, y_ref, z_ref), work)

    tile = pl.BlockSpec((TM, D), lambda i: (i, 0))
    half = lambda first: pl.BlockSpec((N_FFK, TM, FF_SLOT), lambda i: (first, i, 0))
    return pl.pallas_call(
        body, name="ffn_down_fwd", grid=(nt,),
        in_specs=[tile] + _halo_specs(nt) + [half(1), pl.BlockSpec((1, 6, 1, D), _stream_row(TM)), VMEM_WHOLE, VMEM_WHOLE,
                                             VMEM_WHOLE],
        out_specs=[tile, half(0), tile, half(0)],
        out_shape=[jax.ShapeDtypeStruct((t, D), F32), jax.ShapeDtypeStruct((N_FFK, t, FF_SLOT), BF16),
                   jax.ShapeDtypeStruct((t, D), F32), jax.ShapeDtypeStruct((N_FFK, t, FF_SLOT), BF16)],
        compiler_params=_cp(1))(x, av, av, av, av, mod, cw, cb, wd)


def loss_fwd_bwd(x, target, fw):
    t = x.shape[0]

    def body(x_ref, t_ref, w_ref, loss_ref, dx_ref, dw_ref):
        i = pl.program_id(0)

        @pl.when(i == 0)
        def _():
            loss_ref[...] = jnp.zeros_like(loss_ref)
            dw_ref[...] = jnp.zeros_like(dw_ref)
            dx_ref[...] = jnp.zeros_like(dx_ref)

        @pl.when(i > 0)
        def _():
            y, vjp = jax.vjp(_rms, x_ref[...], w_ref[...])
            err = y - t_ref[...]
            loss_ref[...] += 0.5 * jnp.sum(jnp.sum(err * err, axis=-1, keepdims=True) / D)
            dx, dw = vjp(err / D)
            dx_ref[...] = dx
            dw_ref[...] += dw

    return pl.pallas_call(
        body, name="loss_fwd_bwd", grid=(t // TM,),
        in_specs=[pl.BlockSpec((TM, D), lambda i: (i, 0)), pl.BlockSpec((TM, D), lambda i: (jnp.maximum(i - 1, 0), 0)),
                  pl.BlockSpec((1, D), lambda i: (0, 0))],
        out_specs=[pl.BlockSpec((8, 128), lambda i: (0, 0)), pl.BlockSpec((TM, D), lambda i: (i, 0)),
                   pl.BlockSpec((1, D), lambda i: (0, 0))],
        out_shape=[jax.ShapeDtypeStruct((8, 128), F32), jax.ShapeDtypeStruct((t, D), F32), jax.ShapeDtypeStruct((1, D), F32)],
        compiler_params=_cp(1))(x, target, fw)


def _stream_add(ref, k, is_ctx, val):
    ref[0, k] += jnp.where(is_ctx, val, 0.0)
    ref[1, k] += jnp.where(is_ctx, 0.0, val)


def ffn_down_bwd(dx, ac, av, y, mod, wd, skip_ctx):
    t = dx.shape[0]
    nt = t // TM

    def body(dx_ref, ac_ref, v_ref, y_ref, mod_ref, wd_ref, dav_ref, dac_ref, dout_ref, dg_ref):
        i = pl.program_id(0)

        @pl.when(i == 0)
        def _():
            dg_ref[...] = jnp.zeros_like(dg_ref)

        def work():
            _stream_add(dg_ref, 0, i == 0, jnp.sum(dx_ref[...] * y_ref[...], axis=0, keepdims=True))
            dout = (mod_ref[0, 5] * dx_ref[...]).astype(BF16)
            dout_ref[...] = dout
            for k in range(N_FFK):
                _, vjp = jax.vjp(_glu_fn, ac_ref[k].astype(F32), v_ref[k])
                dac, dv = vjp(mm_nt(dout, wd_ref[k]))
                dac_ref[k] = dac
                dav_ref[k] = dv.astype(BF16)

        _unless_ctx(skip_ctx, i == 0, (dav_ref, dac_ref, dout_ref), work)

    tile = pl.BlockSpec((TM, D), lambda i: (i, 0))
    half = lambda first: pl.BlockSpec((N_FFK, TM, FF_SLOT), lambda i: (first, i, 0))
    return pl.pallas_call(
        body, name="ffn_down_bwd", grid=(nt,),
        in_specs=[tile, half(0), half(1), tile, pl.BlockSpec((1, 6, 1, D), _stream_row(TM)), VMEM_WHOLE],
        out_specs=[half(1), half(0), tile, pl.BlockSpec((2, 1, 1, D), lambda i: (0, 0, 0, 0))],
        out_shape=[jax.ShapeDtypeStruct((N_DEV, t, FF_SLOT), BF16), jax.ShapeDtypeStruct((N_FFK, t, FF_SLOT), F32),
                   jax.ShapeDtypeStruct((t, D), BF16), jax.ShapeDtypeStruct((2, 1, 1, D), F32)],
        compiler_params=_cp(1))(dx, ac, av, y, mod, wd)


def conv_bwd(dav, dac, av, cw, skip_ctx):
    t = dac.shape[1]
    nt = t // TM

    def body(dav_in, gp_ref, gm_ref, gn_ref, ap_ref, am_ref, an_ref, cw_ref, dav_ref, dcw_ref, dcb_ref):
        i = pl.program_id(0)

        @pl.when(i == 0)
        def _():
            dcw_ref[...] = jnp.zeros_like(dcw_ref)
            dcb_ref[...] = jnp.zeros_like(dcb_ref)

        def work():
            for k in range(N_FFK):
                g_ext = _with_halo(gp_ref, gm_ref, gn_ref, k, i, nt)
                a_ext = _with_halo(ap_ref, am_ref, an_ref, k, i, nt)
                g_main = gm_ref[k]
                dcb_ref[k] += jnp.sum(g_main, axis=0, keepdims=True)
                da = jnp.zeros((TM, FF_SLOT), F32)
                for dc in (-1, 0, 1):
                    valid = _tap_valid(dc, i, TM, 0)
                    q = functools.reduce(lambda p, r: p + r, [g_ext[GRID_W - GRID_W * dr:GRID_W - GRID_W * dr + TM]
                                                              * _row_weight(cw_ref, k, dr, dc, i) for dr in (-1, 0, 1)])
                    da = da + (q if dc == 0 else pltpu.roll(jnp.where(valid, q, 0.0), dc % TM, 0))
                    g_shift = g_main if dc == 0 else pltpu.roll(jnp.where(valid, g_main, 0.0), dc % TM, 0)
                    for dr in (-1, 0, 1):
                        lo = GRID_W + GRID_W * dr
                        tap = 3 * (dr + 1) + dc + 1
                        dw = jnp.sum(g_shift * a_ext[lo:lo + TM], axis=0, keepdims=True)
                        dcw_ref[k, tap:tap + 1, :] += dw if dr == 0 else jnp.where(i == 0, 0.0, dw)
                dav_ref[k] = da.astype(BF16)

        _unless_ctx(skip_ctx, i == 0, (dav_ref,), work)

    whole = lambda rows: pl.BlockSpec((N_FFK, rows, FF_SLOT), lambda i: (0, 0, 0))
    return pl.pallas_call(
        body, name="conv_bwd", grid=(nt,),
        in_specs=[ANY] + _halo_specs(nt) + _halo_specs(nt) + [VMEM_WHOLE],
        out_specs=[pl.BlockSpec((N_FFK, TM, FF_SLOT), lambda i: (0, i, 0)), whole(9), whole(1)],
        out_shape=[jax.ShapeDtypeStruct(dav.shape, BF16), jax.ShapeDtypeStruct((N_FFK, 9, FF_SLOT), F32),
                   jax.ShapeDtypeStruct((N_FFK, 1, FF_SLOT), F32)],
        input_output_aliases={0: 0}, compiler_params=_cp(1))(dav, dac, dac, dac, av, av, av, cw)


def _norm_mod_bwd(x_ref, nw_ref, mod_ref, k_shift, dh, dx_in, dx_ref, dnw_ref, dmod_ref, is_ctx):
    _, vjp = jax.vjp(_norm_mod, x_ref[...], nw_ref[...], mod_ref[0, k_shift], mod_ref[0, k_shift + 1])
    dx, dnw, dshift, dscale = vjp(dh)
    dx_ref[...] = dx_in + dx
    dnw_ref[...] += dnw
    _stream_add(dmod_ref, 0, is_ctx, dshift)
    _stream_add(dmod_ref, 1, is_ctx, dscale)


def ffn_up_bwd_x(dx2, x, dav, mod, nw, wg, skip_ctx):
    t = x.shape[0]

    def body(dx2_ref, x_ref, dav_ref, mod_ref, nw_ref, w_ref, dx_ref, dnw_ref, dmod_ref):
        i = pl.program_id(0)

        @pl.when(i == 0)
        def _():
            dnw_ref[...] = jnp.zeros_like(dnw_ref)
            dmod_ref[...] = jnp.zeros_like(dmod_ref)

        def work():
            dh = mm_nt(dav_ref[0], w_ref[0])
            for j in range(1, N_DEV):
                dh = dh + mm_nt(dav_ref[j], w_ref[j])
            _norm_mod_bwd(x_ref, nw_ref, mod_ref, 3, dh, dx2_ref[...], dx_ref, dnw_ref, dmod_ref, i == 0)

        _unless_ctx(skip_ctx, i == 0, (dx_ref,), work)

    tile = pl.BlockSpec((TM, D), lambda i: (i, 0))
    return pl.pallas_call(
        body, name="ffn_up_bwd_x", grid=(t // TM,),
        in_specs=[tile, tile, pl.BlockSpec((N_DEV, TM, FF_SLOT), lambda i: (0, i, 0)), pl.BlockSpec((1, 6, 1, D), _stream_row(TM)),
                  pl.BlockSpec((1, D), lambda i: (0, 0)), VMEM_WHOLE],
        out_specs=[tile, pl.BlockSpec((1, D), lambda i: (0, 0)), pl.BlockSpec((2, 2, 1, D), lambda i: (0, 0, 0, 0))],
        out_shape=[jax.ShapeDtypeStruct((t, D), F32), jax.ShapeDtypeStruct((1, D), F32), jax.ShapeDtypeStruct((2, 2, 1, D), F32)],
        compiler_params=_cp(1))(dx2, x, dav, mod, nw, wg)


def weight_grad(at, dout, slot, name, after=None):
    rows, t = at.shape
    stacked = dout.ndim == 3
    n = dout.shape[0] if stacked else dout.shape[1] // slot

    def body(a_ref, d_ref, *rest):
        dw_ref = rest[-1]
        dw_ref[0] = jnp.dot(a_ref[...], d_ref[0] if stacked else d_ref[...], preferred_element_type=F32).astype(dw_ref.dtype)

    d_spec = pl.BlockSpec((1, t, slot), lambda j: (j, 0, 0)) if stacked else pl.BlockSpec((t, slot), lambda j: (0, j))
    extra = [] if after is None else [jnp.reshape(after, (1, 1))]
    return pl.pallas_call(
        body, name=name, grid=(n,), in_specs=[VMEM_WHOLE, d_spec] + [ANY] * len(extra),
        out_specs=pl.BlockSpec((1, rows, slot), lambda j: (j, 0, 0)),
        out_shape=jax.ShapeDtypeStruct((n, rows, slot), GRAD_WIRE), compiler_params=_cp(1))(at, dout, *extra)


def weight_grad_rows(at, dout, name):
    n, t, rows = at.shape
    cols = dout.shape[1]

    def body(a_ref, d_ref, dw_ref):
        dw_ref[0] = _dot(a_ref[0], d_ref[...], ((0,), (0,))).astype(dw_ref.dtype)

    return pl.pallas_call(
        body, name=name, grid=(n,), in_specs=[pl.BlockSpec((1, t, rows), lambda k: (k, 0, 0)), VMEM_WHOLE],
        out_specs=pl.BlockSpec((1, rows, cols), lambda k: (k, 0, 0)),
        out_shape=jax.ShapeDtypeStruct((n, rows, cols), GRAD_WIRE), compiler_params=_cp(1))(at, dout)


def mixer_bwd(dx, parts, o, pa, pb, y, mod, lnw, lnb, sw, sb, hnw, wa, wb, wo, skip_ctx):
    t = dx.shape[0]
    tm = TM
    n_ctx = CTX // tm

    def body(dx_ref, u_ref, v_ref, og_ref, ga_ref, gb_ref, o_ref, pa_ref, pb_ref, y_ref, mod_ref, lnw_ref, lnb_ref, sw_ref,
             sb_ref, hnw_ref, wa_ref, wb_ref, wo_ref, dp_ref, do_ref, dy_ref, dpa_ref, dpb_ref, dlnw_ref, dlnb_ref, dsw_ref,
             dsb_ref, dhnw_ref, dg_ref):
        i = pl.program_id(0)

        @pl.when(i == 0)
        def _():
            for r in (dlnw_ref, dlnb_ref, dsw_ref, dsb_ref, dhnw_ref, dg_ref):
                r[...] = jnp.zeros_like(r)

        def work():
            _, _, vjps, vjp_b = _mixer_tile(slice(0, tm), u_ref, v_ref, og_ref, o_ref, lnw_ref, lnb_ref, sw_ref, sb_ref, hnw_ref)
            pa, pb = pa_ref[...].astype(F32), pb_ref[...].astype(F32)
            sa, sbg = jax.nn.sigmoid(ga_ref[...]), jax.nn.sigmoid(gb_ref[...])
            dxv = dx_ref[...]
            _stream_add(dg_ref, 0, i < n_ctx, jnp.sum(dxv * y_ref[...].astype(F32), axis=0, keepdims=True))
            dy = (mod_ref[0, 2] * dxv).astype(BF16)
            dy_ref[...] = dy
            dmerged = mm_nt(dy, wo_ref[...])
            dpa, dpb = (sa * dmerged).astype(BF16), (sbg * dmerged).astype(BF16)
            dpa_ref[...], dpb_ref[...] = dpa, dpb
            first = 4 * D
            dp_ref[:, first + 3 * D:first + 4 * D] = (dmerged * pa * sa * (1.0 - sa)).astype(BF16)
            dp_ref[:, first + 4 * D:first + 5 * D] = (dmerged * pb * sbg * (1.0 - sbg)).astype(BF16)
            dya = mm_nt(dpa, wa_ref[...])
            dob, dog, dhnw = vjp_b(mm_nt(dpb, wb_ref[...]))
            dp_ref[:, first + 2 * D:first + 3 * D] = dog.astype(BF16)
            dhnw_ref[...] += dhnw
            for g in range(HEADS):
                do_ref[:, _hsl(g)] = dob[g]
            for c, vjp_a in enumerate(vjps):
                rows = slice(c * SGU_CH, (c + 1) * SGU_CH)
                dub, dvb, dlnw, dlnb, dsw, dsb = vjp_a(dya[rows])
                for g in range(HEADS):
                    dp_ref[rows, first + g * HD:first + (g + 1) * HD] = dub[g].astype(BF16)
                    dp_ref[rows, first + D + g * HD:first + D + (g + 1) * HD] = dvb[g].astype(BF16)
                    dlnw_ref[:, _hsl(g)] += dlnw[g]
                    dlnb_ref[:, _hsl(g)] += dlnb[g]
                    dsw_ref[g] += dsw[g]
                    dsb_ref[g] += dsb[g]

        _unless_ctx(skip_ctx, i < n_ctx, (dp_ref, do_ref, dy_ref, dpa_ref, dpb_ref), work)

    vec = lambda n: pl.BlockSpec((1, n), lambda i: (0, 0))
    tile = pl.BlockSpec((tm, D), lambda i: (i, 0))
    sds = jax.ShapeDtypeStruct
    return pl.pallas_call(
        body, name="mixer_bwd", grid=(t // tm,),
        in_specs=[tile] + _part_specs(tm, 4, 5)
        + [pl.BlockSpec((2, tm, D), lambda i: (0, i, 0)), tile, tile, tile, pl.BlockSpec((1, 6, 1, D), _stream_row(tm)),
           vec(D), vec(D), VMEM_WHOLE, VMEM_WHOLE, vec(HD), VMEM_WHOLE, VMEM_WHOLE, VMEM_WHOLE],
        out_specs=[pl.BlockSpec((tm, D_IN), lambda i: (i, 0)), tile, tile, tile, tile, vec(D), vec(D),
                   VMEM_WHOLE, VMEM_WHOLE, vec(HD), pl.BlockSpec((2, 1, 1, D), lambda i: (0, 0, 0, 0))],
        out_shape=[sds((t, D_IN), BF16), sds((t, D), F32), sds((t, D), BF16), sds((t, D), BF16), sds((t, D), BF16),
                   sds((1, D), F32), sds((1, D), F32), sds((HEADS, SGU_CH, SGU_CH), F32), sds((HEADS, SGU_CH, 1), F32),
                   sds((1, HD), F32), sds((2, 1, 1, D), F32)],
        compiler_params=_cp(1))(dx, parts, parts, parts, parts, parts, o, pa, pb, y, mod, lnw, lnb, sw, sb, hnw, wa, wb, wo)


def hgrn_bwd(d, parts, lb, mc, mtc, mrefc, ck, do, first=None, dparts=None):
    t = parts.shape[0]
    nb = t // SCAN_ROWS
    block = _scan_block(nb)
    rev = lambda s: block(d, nb - 1 - s)

    def body(q_ref, f_ref, i_ref, lb_ref, m_ref, mt_ref, mr_ref, ck_ref, do_ref, *rest):
        dst = rest[-1]
        dlb_ref = rest[-2]

        @pl.when(pl.program_id(0) == 0)
        def _():
            dst[...] = jnp.zeros_like(dst)
            dlb_ref[...] = jnp.zeros_like(dlb_ref)

        heads = range(HEADS)
        fn = functools.partial(_hgrn_chunk, m=m_ref[0], mt=mt_ref[0], mref=mr_ref[0])
        for c in reversed(range(SCAN_STEP)):
            first_row = c * CH if d == 0 else (SCAN_STEP - 1 - c) * CH
            rows = slice(first_row, first_row + CH)
            _, vjp = jax.vjp(fn, [ck_ref[0, c, h].astype(F32) for h in heads], [q_ref[rows, _hsl(h)] for h in heads],
                             [f_ref[rows, _hsl(h)] for h in heads], [i_ref[rows, _hsl(h)] for h in heads],
                             [lb_ref[0, :, _hsl(h)] for h in heads])
            dstl, dq, df, di, dlb = vjp(([do_ref[rows, _hsl(h)] for h in heads], [dst[h] for h in heads]))
            for h in heads:
                dst[h] = dstl[h]
                dlb_ref[0, :, _hsl(h)] += dlb[h]
                if d == 0:
                    dq_ref, df_ref, di_ref = rest[:3]
                    dq_ref[rows, _hsl(h)] = dq[h].astype(BF16)
                    df_ref[rows, _hsl(h)] = df[h].astype(BF16)
                    di_ref[rows, _hsl(h)] = di[h].astype(BF16)
                else:
                    dq0_ref, df0_ref, di0_ref, _, dp_ref = rest[:5]
                    col = lambda k: slice(k * D + h * HD, k * D + (h + 1) * HD)
                    dp_ref[rows, col(0)] = (dq0_ref[rows, _hsl(h)].astype(F32) + dq[h]).astype(BF16)
                    dp_ref[rows, col(1)] = df0_ref[rows, _hsl(h)]
                    dp_ref[rows, col(2)] = df[h].astype(BF16)
                    dp_ref[rows, col(3)] = (di0_ref[rows, _hsl(h)].astype(F32) + di[h]).astype(BF16)

    const = lambda s: (d, 0, 0)
    at = lambda k: pl.BlockSpec((SCAN_ROWS, D), lambda s: (rev(s), k))
    in_specs = [at(0), at(1 + d), at(3), pl.BlockSpec((1, 1, D), const), pl.BlockSpec((1, CH, CH), const),
                pl.BlockSpec((1, CH, CH), const), pl.BlockSpec((1, CH, 1), const),
                pl.BlockSpec((1, SCAN_STEP, HEADS, HD, HD), lambda s: (d, nb - 1 - s, 0, 0, 0)), at(0)]
    dlb_spec, dlb_shape = pl.BlockSpec((1, 1, D), lambda s: (0, 0, 0)), jax.ShapeDtypeStruct((1, 1, D), F32)
    common = dict(grid=(nb,), scratch_shapes=[pltpu.VMEM((HEADS, HD, HD), F32)], compiler_params=_cp(1))
    if d == 0:
        return pl.pallas_call(body, name="hgrn_bwd_fwd_dir", in_specs=in_specs, out_specs=[at(0)] * 3 + [dlb_spec],
                              out_shape=[jax.ShapeDtypeStruct((t, D), BF16)] * 3 + [dlb_shape], **common,
                              )(parts, parts, parts, lb, mc, mtc, mrefc, ck, do)
    return pl.pallas_call(body, name="hgrn_bwd_bwd_dir", in_specs=in_specs + [at(0)] * 3 + [ANY],
                          out_specs=[pl.BlockSpec((SCAN_ROWS, 4 * D), lambda s: (rev(s), 0)), dlb_spec],
                          out_shape=[jax.ShapeDtypeStruct(dparts.shape, BF16), dlb_shape], input_output_aliases={12: 0},
                          **common)(parts, parts, parts, lb, mc, mtc, mrefc, ck, do, *first, dparts)


def in_proj_bwd_x(dx1, x, dparts, mod, nw, wg, after=None, latent_only=False):
    t = x.shape[0]
    tm = TM
    n_ctx = CTX // tm

    def body(dx1_ref, x_ref, dp_ref, mod_ref, nw_ref, w_ref, *rest):
        dx_ref, dnw_ref, dmod_ref = rest[-3:]
        i = pl.program_id(0)

        @pl.when(i == 0)
        def _():
            dnw_ref[...] = jnp.zeros_like(dnw_ref)
            dmod_ref[...] = jnp.zeros_like(dmod_ref)

        dh = mm_nt(dp_ref[:, 0:IN_SLOT], w_ref[0])
        for j in range(1, N_DEV):
            dh = dh + mm_nt(dp_ref[:, j * IN_SLOT:(j + 1) * IN_SLOT], w_ref[j])
        _norm_mod_bwd(x_ref, nw_ref, mod_ref, 0, dh, dx1_ref[...], dx_ref, dnw_ref, dmod_ref, i < n_ctx)

    tile = pl.BlockSpec((tm, D), lambda i: (i, 0))
    extra = [] if after is None else [jnp.reshape(after, (1, 1))]
    return pl.pallas_call(
        body, name="in_proj_bwd_x", grid=(t // tm,),
        in_specs=[tile, tile, pl.BlockSpec((tm, D_IN), lambda i: (i, 0)), pl.BlockSpec((1, 6, 1, D), _stream_row(tm)),
                  pl.BlockSpec((1, D), lambda i: (0, 0)), VMEM_WHOLE] + [ANY] * len(extra),
        out_specs=[pl.BlockSpec((tm, D), lambda i: (jnp.maximum(i - n_ctx, 0), 0)) if latent_only else tile,
                   pl.BlockSpec((1, D), lambda i: (0, 0)), pl.BlockSpec((2, 2, 1, D), lambda i: (0, 0, 0, 0))],
        out_shape=[jax.ShapeDtypeStruct((t - CTX if latent_only else t, D), F32), jax.ShapeDtypeStruct((1, D), F32),
                   jax.ShapeDtypeStruct((2, 2, 1, D), F32)],
        compiler_params=_cp(1))(dx1, x, dparts, mod, nw, wg, *extra)


def _lb_fn(h0, h1):
    m = jnp.maximum(h0, h1)
    e0, e1 = jnp.exp(h0 - m), jnp.exp(h1 - m)
    return e1 / (e0 + e1)


def lower_bounds(hlb):
    def body(h_ref, out_ref):
        out_ref[...] = _lb_fn(h_ref[0:1, :], h_ref[1:2, :])
    return pl.pallas_call(body, name="lower_bounds", out_shape=jax.ShapeDtypeStruct((1, 2 * D), F32))(hlb)


def lower_bounds_bwd(hlb, dlb1):
    def body(h_ref, d_ref, out_ref):
        _, vjp = jax.vjp(_lb_fn, h_ref[0:1, :], h_ref[1:2, :])
        d0, d1 = vjp(d_ref[...])
        out_ref[0:1, :] = d0
        out_ref[1:2, :] = d1
    return pl.pallas_call(body, name="lower_bounds_bwd", out_shape=jax.ShapeDtypeStruct((2, 2 * D), F32))(hlb, dlb1)


def _ada_fn(c_all, cctx8, w, b):
    dot = lambda a, l: jnp.dot(_silu(a), w[l], precision=HIGHEST, preferred_element_type=F32) + b[l]
    return [dot(c_all, l) for l in range(2)], [dot(cctx8, l) for l in range(2)]


def ada_fwd(c_all, cctx8, w, b):
    cols = w.shape[-1]

    def body(c_ref, cc_ref, w_ref, b_ref, out_ref):
        ox, oc = _ada_fn(c_ref[...], cc_ref[...], [w_ref[0], w_ref[1]], [b_ref[0], b_ref[1]])
        for l in range(2):
            out_ref[l, 0] = ox[l]
            out_ref[l, 1] = oc[l]
    return pl.pallas_call(body, name="ada_fwd", out_shape=jax.ShapeDtypeStruct((2, 2, N_DEV, cols), F32),
                          compiler_params=_cp(0))(c_all, cctx8, w, b)


def ada_bwd(c_all, cctx8, w, b, dmx, dmc):
    cols = w.shape[-1]

    def body(c_ref, cc_ref, w_ref, b_ref, dmx_ref, dmc_ref, dw_ref, dc_ref):
        fn = lambda cc, w0, w1: _ada_fn(c_ref[...], cc, [w0, w1], [b_ref[0], b_ref[1]])
        _, vjp = jax.vjp(fn, cc_ref[...], w_ref[0], w_ref[1])
        dcc, dw0, dw1 = vjp(([dmx_ref[0], dmx_ref[1]], [dmc_ref[0], dmc_ref[1]]))
        dw_ref[0] = dw0
        dw_ref[1] = dw1
        dc_ref[...] = jnp.sum(dcc, axis=0, keepdims=True)
    return pl.pallas_call(body, name="ada_bwd", out_shape=[jax.ShapeDtypeStruct((2, D, cols), F32), jax.ShapeDtypeStruct((1, D), F32)],
                          compiler_params=_cp(0))(c_all, cctx8, w, b, dmx, dmc)


def adamw(w, m, v, gparts, name):
    r, c = w.shape
    p = gparts.shape[0]
    rt = r
    while rt % 16 == 0 and (p + 7) * rt * c * 4 * 2 > 24 * 2 ** 20:
        rt //= 2

    def body(w_ref, m_ref, v_ref, g_ref, go_ref, d_ref, mo_ref, vo_ref):
        g = g_ref[0].astype(F32)
        for k in range(1, p):
            g = g + g_ref[k].astype(F32)
        m2 = ADAM_B1 * m_ref[...] + (1.0 - ADAM_B1) * g
        v2 = ADAM_B2 * v_ref[...] + (1.0 - ADAM_B2) * (g * g)
        m_hat = m2 / (1.0 - ADAM_B1 ** ADAM_STEP)
        v_hat = v2 / (1.0 - ADAM_B2 ** ADAM_STEP)
        go_ref[...] = g
        d_ref[...] = -ADAM_LR * (m_hat / (jnp.sqrt(v_hat) + ADAM_EPS) + ADAM_WD * w_ref[...])
        mo_ref[...] = m2
        vo_ref[...] = v2

    tile = pl.BlockSpec((rt, c), lambda i: (i, 0))
    return pl.pallas_call(
        body, name=name, grid=(r // rt,),
        in_specs=[tile, tile, tile, pl.BlockSpec((p, rt, c), lambda i: (0, i, 0))], out_specs=[tile] * 4,
        out_shape=[jax.ShapeDtypeStruct((r, c), F32)] * 4, compiler_params=_cp(1))(w, m, v, gparts)


def _me():
    x, y, c = lax.axis_index("x"), lax.axis_index("y"), lax.axis_index("c")
    return x, y, c, 4 * x + 2 * y + c


def _peer(x, y, c, p):
    fx, fy, fc = (p >> 2) & 1, (p >> 1) & 1, p & 1
    return (1 - x if fx else x, 1 - y if fy else y, 1 - c if fc else c)


def all_gather(arrs, name, after=None):
    n = len(arrs)
    extra = [] if after is None else list(after) if isinstance(after, (list, tuple)) else [after]

    def body(*refs):
        ins, outs = refs[:n], refs[n + len(extra):2 * n + len(extra)]
        send, recv, local = refs[2 * n + len(extra):]
        x, y, c, me = _me()
        copies = []
        for a in range(n):
            lc = pltpu.make_async_copy(ins[a], outs[a].at[me], local.at[a])
            lc.start()
            copies.append(lc)
            for p in range(1, N_DEV):
                cp = pltpu.make_async_remote_copy(src_ref=ins[a], dst_ref=outs[a].at[me], send_sem=send.at[a, p - 1],
                                                  recv_sem=recv.at[a, p - 1], device_id=_peer(x, y, c, p),
                                                  device_id_type=pl.DeviceIdType.MESH)
                cp.start()
                copies.append(cp)
        for cp in copies:
            cp.wait()

    return pl.pallas_call(
        body, name=name, in_specs=[ANY] * (n + len(extra)), out_specs=[ANY] * n,
        out_shape=[jax.ShapeDtypeStruct((N_DEV,) + a.shape, a.dtype) for a in arrs],
        scratch_shapes=[pltpu.SemaphoreType.DMA((n, N_DEV - 1)), pltpu.SemaphoreType.DMA((n, N_DEV - 1)),
                        pltpu.SemaphoreType.DMA((n,))])(*arrs, *extra)


HBM = pl.BlockSpec(memory_space=pltpu.HBM)
SEM = pl.BlockSpec(memory_space=pltpu.SEMAPHORE)


def _in_hbm(a):
    return pltpu.with_memory_space_constraint(a, pltpu.HBM)


ALL_PEERS = tuple(range(1, N_DEV))
SAME_CORE_AND_SIBLING = (1, 2, 4, 6)
OTHER_CHIPS = (2, 4, 6)


def _exchange_refs(srcs, lands, layer, scatter, a, x, y, c, p, forward=False):
    me = 4 * x + 2 * y + c
    px, py, pc = _peer(x, y, c, p) if p else (x, y, c)
    if forward and p:
        slot = lands[a].at[4 * px + 2 * py + pc]
        return slot, slot, _peer(x, y, c, 1)
    dst = lands[a].at[me] if layer is None else lands[a].at[me, layer]
    src = srcs[a].at[4 * px + 2 * py + pc] if scatter else dst
    return src, dst, (px, py, pc)


def exchange_start(srcs, lands, layer, scatter, name, after=None, peers=ALL_PEERS, forward=False):
    n, ns = len(lands), len(srcs)
    extra = [] if after is None else [after]

    def body(*refs):
        ins, lz = refs[:ns], refs[ns:ns + n]
        send, recv = refs[ns + n + len(extra)], refs[ns + n + len(extra) + 1]
        token = refs[-1]
        x, y, c, _ = _me()
        for a in range(n):
            for p in peers:
                src, dst, peer = _exchange_refs(ins, lz, layer, scatter, a, x, y, c, p, forward)
                k = a * (N_DEV - 1) + p - 1
                pltpu.make_async_remote_copy(src_ref=src, dst_ref=dst, send_sem=send.at[k], recv_sem=recv.at[k],
                                             device_id=peer, device_id_type=pl.DeviceIdType.MESH).start()
        token[...] = jnp.zeros_like(token)

    thru = [pltpu.HBM(a.shape, a.dtype) for a in list(srcs) + list(lands)]
    out = pl.pallas_call(
        body, name=name, in_specs=[HBM] * (ns + n) + [ANY] * len(extra),
        out_specs=[SEM, SEM] + [HBM] * (ns + n) + [pl.BlockSpec(memory_space=pltpu.VMEM)],
        out_shape=[pltpu.SemaphoreType.DMA((n * (N_DEV - 1),)), pltpu.SemaphoreType.DMA((n * (N_DEV - 1),))] + thru
        + [jax.ShapeDtypeStruct((8, 128), F32)],
        input_output_aliases={i: 2 + i for i in range(ns + n)},
        compiler_params=pltpu.CompilerParams(has_side_effects=pltpu.SideEffectType.DATAFLOW_SIDE_EFFECTING),
    )(*[_in_hbm(a) for a in list(srcs) + list(lands)], *extra)
    return out[0], out[1], out[2:2 + ns], out[2 + ns:2 + ns + n], out[-1]


def exchange_wait(send, recv, srcs, lands, layer, scatter, after, name, peers=ALL_PEERS):
    n, ns = len(lands), len(srcs)

    def body(*refs):
        ins, lz = refs[:ns], refs[ns:ns + n]
        send_ref, recv_ref = refs[ns + n], refs[ns + n + 1]
        x, y, c, _ = _me()
        for a in range(n):
            for p in peers:
                src, dst, peer = _exchange_refs(ins, lz, layer, scatter, a, x, y, c, 0)
                k = a * (N_DEV - 1) + p - 1
                cp = pltpu.make_async_remote_copy(src_ref=src, dst_ref=dst, send_sem=send_ref.at[k],
                                                  recv_sem=recv_ref.at[k], device_id=peer,
                                                  device_id_type=pl.DeviceIdType.MESH)
                cp.wait_send()
                cp.wait_recv()

    thru = [pltpu.HBM(a.shape, a.dtype) for a in list(srcs) + list(lands)]
    out = pl.pallas_call(
        body, name=name, in_specs=[HBM] * (ns + n) + [SEM, SEM, ANY], out_specs=[HBM] * (ns + n), out_shape=thru,
        input_output_aliases={i: i for i in range(ns + n)},
        compiler_params=pltpu.CompilerParams(has_side_effects=pltpu.SideEffectType.DATAFLOW_SIDE_EFFECTING),
    )(*srcs, *lands, send, recv, after)
    return out[ns:]


def place_own(src, land, me, layer, scatter, name, src_layer=None, after=None):
    create = isinstance(land, jax.ShapeDtypeStruct)
    r, c = src.shape[-2:]
    rt = r
    while rt % 32 == 0 and rt * c * 4 > 2 ** 21:
        rt //= 2

    extra = [] if after is None else [after]

    def body(me_ref, src_ref, *rest):
        out_ref = rest[-1]
        out_ref[...] = src_ref[...].reshape(out_ref.shape).astype(out_ref.dtype)

    src_spec = (pl.BlockSpec((1, rt, c), lambda i, m: (m[0], i, 0)) if scatter else
                pl.BlockSpec((rt, c), lambda i, m: (i, 0)) if src_layer is None else
                pl.BlockSpec((1, rt, c), lambda i, m: (src_layer, i, 0)))
    out_spec = (pl.BlockSpec((1, rt, c), lambda i, m: (m[0], i, 0)) if layer is None
                else pl.BlockSpec((1, 1, rt, c), lambda i, m: (m[0], layer, i, 0)))
    grid_spec = pltpu.PrefetchScalarGridSpec(num_scalar_prefetch=1, grid=(r // rt,),
                                             in_specs=[src_spec] + ([] if create else [ANY]) + [ANY] * len(extra),
                                             out_specs=out_spec)
    return pl.pallas_call(body, name=name, grid_spec=grid_spec, out_shape=jax.ShapeDtypeStruct(land.shape, land.dtype),
                          input_output_aliases={} if create else {2: 0}, compiler_params=_cp(1),
                          )(*((me, src) if create else (me, src, land)), *extra)


def _scan_constants():
    r = lax.broadcasted_iota(jnp.int32, (CH, CH), 0)
    s = lax.broadcasted_iota(jnp.int32, (CH, CH), 1)
    lower = (s <= r).astype(F32)
    t = jnp.arange(CH)[:, None]
    mc = jnp.stack([lower, lower.T])
    mref = jnp.stack([(t <= CH // 2 - 1).astype(F32), (t >= CH // 2).astype(F32)])
    return mc, jnp.stack([lower.T, lower]), mref


def local_step(x, ctx, target, mod, lb, w, fetch=None, publish=None, small_ready=None, small_early=None):
    kept = {}

    def keep(l, part, grads):
        kept[(l, part)] = grads
        return 0.0

    fetch = fetch or (lambda l, part, after: w)
    publish = publish or keep
    n_layers = len(mod)
    mc, mtc, mrefc = _scan_constants()
    xs = jnp.concatenate([ctx, x], axis=0)
    saved, big = [], []
    for l in range(n_layers):
        wl = dict(fetch(l, "in", xs))
        parts, ht, iv = in_proj_fwd(xs, mod[l], w["nw1"][l], wl["win"][l])
        o, ck = hgrn_fwd(parts, iv, lb[l], mc, mtc, mrefc)
        wl.update(fetch(l, "rest", o))
        last = l == n_layers - 1
        x1, pa, pb, ym, yat, ybt, mt = mixer_fwd(xs, parts, o, mod[l], w["lnw"][l], w["lnb"][l], w["sw"][l], w["sb"][l],
                                                 w["hnw"][l], wl["wa"][l], wl["wb"][l], wl["wo"][l], last)
        av, h2t = ffn_up_fwd(x1, mod[l], w["nw2"][l], wl["wup"][l], last)
        x2, ac, y, z = ffn_down_fwd(x1, av, mod[l], w["cw"][l], w["cb"][l], wl["wd"][l], last)
        saved.append((xs, parts, iv, o, ck, x1, av, ac, y, z, ht, h2t, pa, pb, ym, yat, ybt, mt))
        big.append(wl)
        xs = x2
    loss, dx, dfw = loss_fwd_bwd(xs, target, w["fw"])
    g = {k: [None] * n_layers for k in ("nw1", "nw2", "lnw", "lnb", "sw", "sb", "hnw", "cw", "cb")}
    g["fw"] = dfw
    dmod, dlb = [None] * n_layers, [None] * n_layers
    tok = 0.0
    for l in reversed(range(n_layers)):
        x0, parts, iv, o, ck, x1, av, ac, y, z, ht, h2t, pa, pb, ym, yat, ybt, mt = saved[l]
        wl = big[l]
        last = l == n_layers - 1
        dav, dac, dout, dg2 = ffn_down_bwd(dx, ac, av, y, mod[l] + tok, wl["wd"][l], last)
        dwd = weight_grad_rows(z, dout, "ffn_down_bwd_w")
        dav, g["cw"][l], g["cb"][l] = conv_bwd(dav, dac, av, w["cw"][l], last)
        dx1, g["nw2"][l], dmod2 = ffn_up_bwd_x(dx, x1, dav, mod[l], w["nw2"][l], wl["wup"][l], last)
        dwup = weight_grad(h2t, dav, FF_SLOT, "ffn_up_bwd_w")
        tok = publish(l, "ffn", {"wd": dwd, "wup": dwup})
        (dparts, do, dy, dpa, dpb, g["lnw"][l], g["lnb"][l], g["sw"][l], g["sb"][l], g["hnw"][l],
         dg1) = mixer_bwd(dx1, parts, o, pa, pb, ym, mod[l] + tok, w["lnw"][l], w["lnb"][l], w["sw"][l], w["sb"][l],
                          w["hnw"][l], wl["wa"][l], wl["wb"][l], wl["wo"][l], last)
        tok = publish(l, "mix", {"wa": weight_grad(yat, dpa, D, "mixer_bwd_wa"), "wb": weight_grad(ybt, dpb, D, "mixer_bwd_wb"),
                                 "wo": weight_grad(mt, dy, D, "mixer_bwd_wo")})
        if l == 0 and small_early:
            dmod[0] = jnp.concatenate([jnp.zeros((2, 2, 1, D), F32), dg1, dmod2, dg2], axis=1)
            tok = tok + small_early(loss[0, 0], g, dmod, dlb)
        dq, df, di, dlb_f = hgrn_bwd(0, parts, lb[l] + tok, mc, mtc, mrefc, ck, do)
        dparts, dlb_b = hgrn_bwd(1, parts, lb[l], mc, mtc, mrefc, ck, do, (dq, df, di), dparts)
        dlb[l] = jnp.concatenate([dlb_f, dlb_b], axis=0)
        tok = publish(l, "in", {"win": weight_grad(ht, dparts, IN_SLOT, "in_proj_bwd_w")})
        dx, g["nw1"][l], dmod1 = in_proj_bwd_x(dx1, x0, dparts, mod[l], w["nw1"][l], wl["win"][l], after=tok,
                                               latent_only=l == 0)
        dmod[l] = jnp.concatenate([dmod1, dg1, dmod2, dg2], axis=1)
    done = small_ready(loss[0, 0], g, dmod, dlb) if small_ready else 0.0
    for (l, part), grads in kept.items():
        for k, v in grads.items():
            g.setdefault(k, [None] * n_layers)[l] = v
    return loss[0, 0], dx, g, dmod, dlb, done


ROW = 1024
REPLICATED = ("norm1_w", "sgu_ln_w", "sgu_ln_b", "sgu_w", "sgu_b", "hgrn_lower_bounds", "hgrn_norm_w", "norm2_w",
              "ffn_conv_b", "final_norm_w")
WEIGHT_ORDER = ("c_ctx", "ada_w", "ada_b", "norm1_w", "w_in", "sgu_ln_w", "sgu_ln_b", "sgu_w", "sgu_b", "hgrn_lower_bounds",
                "hgrn_norm_w", "w_branch_a", "w_branch_b", "w_out", "norm2_w", "ffn_w_up", "ffn_conv_w", "ffn_conv_b",
                "ffn_w_down", "final_norm_w")


def _rows_of(n):
    return -(-n // (8 * ROW)) * 8


def _pack(arrs, total_rows=None):
    parts = []
    for a in arrs:
        flat = a.reshape(-1).astype(F32)
        rows = _rows_of(flat.shape[0])
        parts.append(jnp.pad(flat, (0, rows * ROW - flat.shape[0])).reshape(rows, ROW))
    have = sum(p.shape[0] for p in parts)
    if total_rows is not None and total_rows > have:
        parts.append(jnp.zeros((total_rows - have, ROW), F32))
    return jnp.concatenate(parts, axis=0)


def _unpack(packed, shapes):
    lead = packed.shape[:-2]
    out, r0 = [], 0
    for s in shapes:
        n = math.prod(s)
        rows = _rows_of(n)
        out.append(packed[..., r0:r0 + rows, :].reshape(lead + (rows * ROW,))[..., :n].reshape(lead + tuple(s)))
        r0 += rows
    return out


def kernel(x, c, ctx, c_ctx, ada_w, ada_b, norm1_w, w_in, sgu_ln_w, sgu_ln_b, sgu_w, sgu_b, hgrn_lower_bounds, hgrn_norm_w, w_branch_a, w_branch_b, w_out, norm2_w, ffn_w_up, ffn_conv_w, ffn_conv_b, ffn_w_down, final_norm_w, loss_target, m_c_ctx, m_ada_w, m_ada_b, m_norm1_w, m_w_in, m_sgu_ln_w, m_sgu_ln_b, m_sgu_w, m_sgu_b, m_hgrn_lower_bounds, m_hgrn_norm_w, m_w_branch_a, m_w_branch_b, m_w_out, m_norm2_w, m_ffn_w_up, m_ffn_conv_w, m_ffn_conv_b, m_ffn_w_down, m_final_norm_w, v_c_ctx, v_ada_w, v_ada_b, v_norm1_w, v_w_in, v_sgu_ln_w, v_sgu_ln_b, v_sgu_w, v_sgu_b, v_hgrn_lower_bounds, v_hgrn_norm_w, v_w_branch_a, v_w_branch_b, v_w_out, v_norm2_w, v_ffn_w_up, v_ffn_conv_w, v_ffn_conv_b, v_ffn_w_down, v_final_norm_w):
    wts = dict(c_ctx=c_ctx, ada_w=ada_w, ada_b=ada_b, norm1_w=norm1_w, w_in=w_in, sgu_ln_w=sgu_ln_w, sgu_ln_b=sgu_ln_b,
               sgu_w=sgu_w, sgu_b=sgu_b, hgrn_lower_bounds=hgrn_lower_bounds, hgrn_norm_w=hgrn_norm_w, w_branch_a=w_branch_a,
               w_branch_b=w_branch_b, w_out=w_out, norm2_w=norm2_w, ffn_w_up=ffn_w_up, ffn_conv_w=ffn_conv_w,
               ffn_conv_b=ffn_conv_b, ffn_w_down=ffn_w_down, final_norm_w=final_norm_w)
    mom1 = dict(c_ctx=m_c_ctx, ada_w=m_ada_w, ada_b=m_ada_b, norm1_w=m_norm1_w, w_in=m_w_in, sgu_ln_w=m_sgu_ln_w,
                sgu_ln_b=m_sgu_ln_b, sgu_w=m_sgu_w, sgu_b=m_sgu_b, hgrn_lower_bounds=m_hgrn_lower_bounds,
                hgrn_norm_w=m_hgrn_norm_w, w_branch_a=m_w_branch_a, w_branch_b=m_w_branch_b, w_out=m_w_out, norm2_w=m_norm2_w,
                ffn_w_up=m_ffn_w_up, ffn_conv_w=m_ffn_conv_w, ffn_conv_b=m_ffn_conv_b, ffn_w_down=m_ffn_w_down,
                final_norm_w=m_final_norm_w)
    mom2 = dict(c_ctx=v_c_ctx, ada_w=v_ada_w, ada_b=v_ada_b, norm1_w=v_norm1_w, w_in=v_w_in, sgu_ln_w=v_sgu_ln_w,
                sgu_ln_b=v_sgu_ln_b, sgu_w=v_sgu_w, sgu_b=v_sgu_b, hgrn_lower_bounds=v_hgrn_lower_bounds,
                hgrn_norm_w=v_hgrn_norm_w, w_branch_a=v_w_branch_a, w_branch_b=v_w_branch_b, w_out=v_w_out, norm2_w=v_norm2_w,
                ffn_w_up=v_ffn_w_up, ffn_conv_w=v_ffn_conv_w, ffn_conv_b=v_ffn_conv_b, ffn_w_down=v_ffn_w_down,
                final_norm_w=v_final_norm_w)
    n_layers = w_in.shape[0]
    layers = range(n_layers)
    me = 4 * lax.axis_index("x") + 2 * lax.axis_index("y") + lax.axis_index("c")
    ada_cols = ada_w.shape[-1]

    big = ("w_in", "ffn_w_up", "w_branch_a", "w_branch_b", "w_out", "ffn_w_down")
    short = {"w_in": "win", "ffn_w_up": "wup", "w_branch_a": "wa", "w_branch_b": "wb", "w_out": "wo", "ffn_w_down": "wd"}
    me1 = me.reshape(1).astype(jnp.int32)
    groups = [[("w_in", 0)], [(k, 0) for k in big[1:]], [("w_in", 1)], [(k, 1) for k in big[1:]]]
    in_flight, started = [], 0.0

    def own_slots(n, after):
        return [place_own(wts[k], jax.ShapeDtypeStruct((N_DEV,) + wts[k].shape[1:], BF16), me1, None, False,
                          f"gather_own_{short[k]}_{l}", src_layer=l, after=after) for k, l in groups[n]]

    def start_group(n, lands, after):
        in_flight.append(exchange_start([], lands, None, False, f"gather_weights_start_{n}", after=after,
                                        peers=SAME_CORE_AND_SIBLING if n == 0 else ALL_PEERS))
        return in_flight[-1][-1]

    (c_all,) = all_gather([c], "gather_c")
    c_all = c_all.reshape(N_DEV, D)
    token = start_group(0, own_slots(0, c_all), c_all)
    later = [own_slots(n, token) for n in range(1, len(groups))]
    cctx8 = jnp.broadcast_to(c_ctx[None, :], (N_DEV, D))
    ada_b_cols = lax.dynamic_slice_in_dim(ada_b, me * ada_cols, ada_cols, axis=1)[:, None, :]
    mod_cols = ada_fwd(c_all, cctx8, ada_w, ada_b_cols)
    xs = jnp.concatenate([ctx[0], x[0]], axis=0)
    lb1 = lower_bounds(hgrn_lower_bounds)
    mod_all, conv_all = all_gather([mod_cols, ffn_conv_w.reshape(n_layers, 9, -1)], "gather_mod_conv",
                                   after=[token, xs, lb1] + [a for lands in later for a in lands])
    conv_full = [conv_all[:, l].transpose(1, 0, 2).reshape(9, N_FFK, FF_SLOT).transpose(1, 0, 2) for l in layers]
    for n in range(1, len(groups)):
        token = start_group(n, later[n - 1], mod_all if n == 1 else token)
    for started_group in in_flight:
        started = started + started_group[-1][0, 0]

    def as_used(k, a):
        return a if k in ("w_in", "ffn_w_up") else a.reshape(N_FFK, FF_SLOT, D) if k == "ffn_w_down" else a.reshape(D, D)

    arrived = {}

    def fetch(l, part, after):
        n = {(0, "in"): 0, (0, "rest"): 1, (1, "in"): 2, (1, "rest"): 3}.get((l, part))
        if n is not None:
            send, recv, _, lands, _ = in_flight[n]
            first = n == 0
            got = exchange_wait(send, recv, [], lands, None, False, after, f"gather_weights_wait_{n}",
                                peers=SAME_CORE_AND_SIBLING if first else ALL_PEERS)
            if first:
                send, recv, _, lands, _ = exchange_start([], got, None, False, "gather_weights_pass_on", peers=OTHER_CHIPS,
                                                         forward=True)
                got = exchange_wait(send, recv, [], lands, None, False, after, "gather_weights_passed_on", peers=OTHER_CHIPS)
            for (k, ll), a in zip(groups[n], got):
                arrived.setdefault(short[k], [None] * n_layers)[ll] = as_used(k, a)
        return arrived

    mod_x = lax.dynamic_index_in_dim(mod_all[:, :, 0], me, axis=2, keepdims=False)
    mod_c = mod_all[:, :, 1, 0]
    mod = [jnp.stack([mod_c[:, l].reshape(6, 1, D), mod_x[:, l].reshape(6, 1, D)]) for l in layers]
    mod[0] = mod[0] + started

    lb = [jnp.zeros((2, 1, D), F32), lb1.reshape(2, 1, D)]

    w = {
        "nw1": [norm1_w[l][None] for l in layers], "nw2": [norm2_w[l][None] for l in layers],
        "lnw": [sgu_ln_w[l][None] for l in layers], "lnb": [sgu_ln_b[l][None] for l in layers],
        "sw": [sgu_w[l] for l in layers], "sb": [sgu_b[l][:, :, None] for l in layers],
        "hnw": [hgrn_norm_w[l][None] for l in layers], "cw": conv_full,
        "cb": [ffn_conv_b[l].reshape(N_FFK, 1, FF_SLOT) for l in layers], "fw": final_norm_w[None],
    }
    long = {v: k for k, v in short.items()}
    landing, sent = {}, []

    def publish(l, part, grads):
        keys = [long[k] for k in grads]
        slots = [a.reshape((N_DEV, -1, a.shape[-1])) for a in grads.values()]
        zones = [place_own(s, landing.get(k, jax.ShapeDtypeStruct((N_DEV, n_layers) + s.shape[1:], s.dtype)), me1, l, True,
                           f"scatter_own_{short[k]}_{l}") for k, s in zip(keys, slots)]
        send, recv, srcs, zones, token = exchange_start(slots, zones, l, True, f"scatter_grads_start_{part}_{l}")
        landing.update(zip(keys, zones))
        sent.append((keys, l, part, send, recv, srcs, token))
        return token[0, 0]

    out = {}
    flat2 = lambda a: a.reshape(-1, a.shape[-1])

    def finish(part, after):
        done = []
        for keys, l, p, send, recv, srcs, _ in sent:
            if p == part:
                zones = exchange_wait(send, recv, srcs, [landing[k] for k in keys], l, True, after,
                                      f"scatter_grads_wait_{part}_{l}")
                landing.update(zip(keys, zones))
                done = keys
        for k in done:
            r = landing[k]
            res = adamw(flat2(wts[k]), flat2(mom1[k]), flat2(mom2[k]), r.reshape(N_DEV, -1, r.shape[-1]), "adamw_" + k)
            out[k] = tuple(a.reshape(wts[k].shape) for a in res)

    rep_rows = -(-sum(_rows_of(wts[k].size) for k in REPLICATED) // 64) * 64
    conv_rows = _rows_of(n_layers * 9 * D_FF)
    dmod_rows = _rows_of(n_layers * 6 * D)
    early = {}

    def small_early(loss_part, g, dmod, dlb):
        d_hlb = lower_bounds_bwd(hgrn_lower_bounds, dlb[1].reshape(1, 2 * D))
        st = lambda k: jnp.stack([jnp.zeros((1, D), F32) if a is None else a for a in g[k]])
        rep_grads = {"norm1_w": st("nw1"), "sgu_ln_w": st("lnw"), "sgu_ln_b": st("lnb"), "sgu_w": st("sw"), "sgu_b": st("sb"),
                     "hgrn_lower_bounds": d_hlb, "hgrn_norm_w": st("hnw"), "norm2_w": st("nw2"), "ffn_conv_b": st("cb"),
                     "final_norm_w": g["fw"]}
        d_conv = jnp.stack([g["cw"][l].transpose(1, 0, 2).reshape(9, D_FF) for l in layers])
        dmod_x = jnp.stack([dmod[l][1].reshape(6 * D) for l in layers])
        dmod_c = jnp.stack([dmod[l][0].reshape(6 * D) for l in layers])
        small = jnp.concatenate([_pack([rep_grads[k] for k in REPLICATED], rep_rows),
                                 _pack([d_conv, dmod_x, dmod_c, loss_part.reshape(1)])], axis=0)
        zone = place_own(small, jax.ShapeDtypeStruct((N_DEV,) + small.shape, F32), me1, None, False, "gather_small_own")
        early["send"], early["recv"], _, early["zones"], token = exchange_start([], [zone], None, False, "gather_small_start")
        return token[0, 0]

    def small_ready(loss_part, g, dmod, dlb):
        late = _pack([g["nw1"][0], dmod[0][1, 0:2], dmod[0][0, 0:2]])
        for part in ("ffn", "mix"):
            finish(part, late)
        (late_all,) = all_gather([late], "gather_small_late", after=[out[k][0] for k in big[1:]])
        (small_all,) = exchange_wait(early["send"], early["recv"], [], early["zones"], None, False, late_all,
                                     "gather_small_wait")
        at_x = rep_rows + conv_rows
        small_all = small_all.at[:, 0:1].set(late_all[:, 0:1])
        small_all = small_all.at[:, at_x:at_x + 2].set(late_all[:, 8:10])
        small_all = small_all.at[:, at_x + dmod_rows:at_x + dmod_rows + 2].set(late_all[:, 16:18])
        d_conv_shape, dmod_shape = (n_layers, 9, D_FF), (n_layers, 6 * D)
        conv_g, dmx_all, dmc_all, loss_all = _unpack(small_all[:, rep_rows:], [d_conv_shape, dmod_shape, dmod_shape, (1,)])
        out["loss"] = functools.reduce(lambda a, b: a + b, [loss_all[k, 0] for k in range(N_DEV)])

        rep = adamw(_pack([wts[k] for k in REPLICATED], rep_rows), _pack([mom1[k] for k in REPLICATED], rep_rows),
                    _pack([mom2[k] for k in REPLICATED], rep_rows), small_all, "adamw_replicated")
        rep = [_unpack(r, [wts[k].shape for k in REPLICATED]) for r in rep]
        for n, k in enumerate(REPLICATED):
            out[k] = tuple(r[n] for r in rep)

        conv_mine = lax.dynamic_index_in_dim(conv_g.reshape(N_DEV, n_layers, 9, N_DEV, -1), me, axis=3, keepdims=False)
        res = adamw(flat2(ffn_conv_w), flat2(m_ffn_conv_w), flat2(v_ffn_conv_w),
                    conv_mine.reshape(N_DEV, -1, conv_mine.shape[-1]), "adamw_conv_w")
        out["ffn_conv_w"] = tuple(r.reshape(ffn_conv_w.shape) for r in res)

        out["ada_b"] = tuple(adamw(ada_b, m_ada_b, v_ada_b, jnp.concatenate([dmx_all, dmc_all], axis=0), "adamw_ada_b"))

        cols_of = lambda a: lax.dynamic_slice_in_dim(a, me * ada_cols, ada_cols, axis=2).transpose(1, 0, 2)
        d_ada_w, d_cctx = ada_bwd(c_all, cctx8, ada_w, ada_b_cols, cols_of(dmx_all), cols_of(dmc_all))
        res = adamw(flat2(ada_w), flat2(m_ada_w), flat2(v_ada_w), flat2(d_ada_w)[None], "adamw_ada_w")
        out["ada_w"] = tuple(r.reshape(ada_w.shape) for r in res)
        (d_cctx_all,) = all_gather([d_cctx], "gather_c_ctx_grad")
        res = adamw(c_ctx[None], m_c_ctx[None], v_c_ctx[None], d_cctx_all, "adamw_c_ctx")
        out["c_ctx"] = tuple(r[0] for r in res)
        return d_cctx_all

    _, grad_x, _, _, _, small_done = local_step(x[0], ctx[0], loss_target[0], mod, lb, w, fetch, publish, small_ready,
                                                small_early)
    loss = out["loss"]

    finish("in", small_done)
    return (loss, grad_x[None]) + tuple(out[k][n] for n in range(4) for k in WEIGHT_ORDER)
```

```python
import functools
import math

import jax
import jax.numpy as jnp
from jax import lax
from jax.experimental import pallas as pl
from jax.experimental.pallas import tpu as pltpu

F32 = jnp.float32
BF16 = jnp.bfloat16

N_DEV = 8
AXES = ("x", "y", "c")
D = 1024
CTX = 256
TM = 256
CH = 64
SGU_CH = 128
HEADS = 8
HD = 128
GRID_W = 64
D_IN = 9 * D
IN_SLOT = D_IN // N_DEV
D_FF = 2816
FF_SLOT = 2 * D_FF // N_DEV
N_FFK = D_FF // FF_SLOT
RMS_EPS = 1e-6
LN_EPS = 1e-5
ADAM_LR, ADAM_B1, ADAM_B2, ADAM_EPS, ADAM_WD, ADAM_STEP = 0.001, 0.9, 0.999, 1e-08, 0.01, 10
VMEM_LIMIT_V7X = 56 * 2 ** 20
GRAD_WIRE = jnp.bfloat16

VMEM_WHOLE = pl.BlockSpec(memory_space=pltpu.VMEM)
ANY = pl.BlockSpec(memory_space=pl.ANY)


def _cp(n_axes):
    return pltpu.CompilerParams(dimension_semantics=("arbitrary",) * n_axes, vmem_limit_bytes=VMEM_LIMIT_V7X)


def _dot(a, b, dims):
    return lax.dot_general(a.astype(BF16), b.astype(BF16), (dims, ((), ())), preferred_element_type=F32)


@jax.custom_vjp
def mm(a, b):
    return _dot(a, b, ((1,), (0,)))


mm.defvjp(lambda a, b: (mm(a, b), (a, b)),
          lambda r, g: (_dot(g, r[1], ((1,), (1,))).astype(r[0].dtype), _dot(r[0], g, ((0,), (0,))).astype(r[1].dtype)))


@jax.custom_vjp
def mm_nt(a, b):
    return _dot(a, b, ((1,), (1,)))


mm_nt.defvjp(lambda a, b: (mm_nt(a, b), (a, b)),
             lambda r, g: (_dot(g, r[1], ((1,), (0,))).astype(r[0].dtype), _dot(g, r[0], ((0,), (0,))).astype(r[1].dtype)))


@jax.custom_vjp
def mm_tn(a, b):
    return _dot(a, b, ((0,), (0,)))


mm_tn.defvjp(lambda a, b: (mm_tn(a, b), (a, b)),
             lambda r, g: (_dot(r[1], g, ((1,), (1,))).astype(r[0].dtype), _dot(r[0], g, ((1,), (0,))).astype(r[1].dtype)))


def _tri_dot(m, g):
    hi = g.astype(BF16)
    low = (g - hi.astype(F32)).astype(BF16)
    n = g.shape[1]
    out = jnp.dot(m.astype(BF16), jnp.concatenate([hi, low], axis=1), preferred_element_type=F32)
    return out[:, :n] + out[:, n:]


@jax.custom_vjp
def _cum(m, mt, g):
    return _tri_dot(m, g)


_cum.defvjp(lambda m, mt, g: (_cum(m, mt, g), (m, mt)),
            lambda r, d: (jnp.zeros_like(r[0]), jnp.zeros_like(r[1]), _tri_dot(r[1], d)))


def _silu(x):
    return x * jax.nn.sigmoid(x)


def _gelu(x):
    return 0.5 * x * (1.0 + jnp.tanh(math.sqrt(2.0 / math.pi) * (x + 0.044715 * (x * x * x))))


def _rms(x, w):
    return x * lax.rsqrt(jnp.mean(x * x, axis=-1, keepdims=True) + RMS_EPS) * w


def _norm_mod(x, w, shift, scale):
    return _rms(x, w) * (1.0 + scale) + shift


def _hsl(h):
    return slice(h * HD, (h + 1) * HD)


def _hgrn_chunk(st, qz, fz, iv, lb, m, mt, mref):
    hs = range(HEADS)
    keep = [1.0 - lb[h] for h in hs]
    sg = [jax.nn.sigmoid(fz[h]) for h in hs]
    g = [jnp.log(lb[h] + keep[h] * sg[h]) for h in hs]
    k = [keep[h] * (1.0 - sg[h]) for h in hs]
    q = [_silu(qz[h]) for h in hs]
    b = [_cum(m, mt, g[h]) for h in hs]
    ref = [jnp.sum(mref * g[h], axis=0, keepdims=True) for h in hs]
    last = [jnp.sum(g[h], axis=0, keepdims=True) for h in hs]
    qa = [q[h] * jnp.exp(b[h] - ref[h]) for h in hs]
    ka = [k[h] * jnp.exp(ref[h] - b[h]) for h in hs]
    scores = [jnp.where(m > 0.5, mm_nt(qa[h], ka[h]), 0.0) for h in hs]
    inter = [mm_nt(qa[h] * jnp.exp(ref[h]), st[h]) for h in hs]
    kv = [mm_tn(iv[h], ka[h] * jnp.exp(last[h] - ref[h])) for h in hs]
    outs = [mm(scores[h], iv[h]) + inter[h] for h in hs]
    news = [jnp.exp(last[h]) * st[h] + kv[h] for h in hs]
    return outs, news


def _sgu_fn(ub, vb, lnw, lnb, sw, sb):
    gv = [_gelu(v) for v in vb]
    mu = sum(jnp.sum(t, axis=-1, keepdims=True) for t in gv) / D
    var = sum(jnp.sum((t - mu) * (t - mu), axis=-1, keepdims=True) for t in gv) / D
    inv = lax.rsqrt(var + LN_EPS)
    cols = []
    for g in range(HEADS):
        vn = (gv[g] - mu) * inv * lnw[g] + lnb[g]
        cols.append(_gelu(ub[g]) * (mm(sw[g], vn) + sb[g]))
    return jnp.concatenate(cols, axis=1)


def _readout_fn(ob, og, hnw):
    r = [o * lax.rsqrt(jnp.mean(o * o, axis=-1, keepdims=True) + RMS_EPS) * hnw for o in ob]
    return jnp.concatenate(r, axis=1) * _silu(og)


def _glu_fn(ac, v):
    return _gelu(ac) * v


def _stream_row(tm):
    n_ctx = CTX // tm
    return lambda i: (jnp.where(i < n_ctx, 0, 1), 0, 0, 0)


def in_proj_fwd(x, mod, nw, wg):
    t = x.shape[0]

    def body(x_ref, mod_ref, nw_ref, w_ref, out_ref, ht_ref, iv_ref):
        h32 = _norm_mod(x_ref[...], nw_ref[...], mod_ref[0, 0], mod_ref[0, 1])
        ht_ref[...] = h32.T.astype(BF16)
        h = h32.astype(BF16)
        for j in range(N_DEV):
            out_ref[:, j * IN_SLOT:(j + 1) * IN_SLOT] = jnp.dot(h, w_ref[j], preferred_element_type=F32)
        iv_ref[...] = out_ref[:, 3 * D:4 * D].astype(BF16)

    return pl.pallas_call(
        body, name="in_proj_fwd", grid=(t // TM,),
        in_specs=[pl.BlockSpec((TM, D), lambda i: (i, 0)), pl.BlockSpec((1, 6, 1, D), _stream_row(TM)),
                  pl.BlockSpec((1, D), lambda i: (0, 0)), VMEM_WHOLE],
        out_specs=[pl.BlockSpec((TM, D_IN), lambda i: (i, 0)), pl.BlockSpec((D, TM), lambda i: (0, i)),
                   pl.BlockSpec((TM, D), lambda i: (i, 0))],
        out_shape=[jax.ShapeDtypeStruct((t, D_IN), F32), jax.ShapeDtypeStruct((D, t), BF16), jax.ShapeDtypeStruct((t, D), BF16)],
        compiler_params=_cp(1))(x, mod, nw, wg)


SCAN_STEP = 4
SCAN_ROWS = SCAN_STEP * CH


def _scan_block(nb):
    ncb = CTX // SCAN_ROWS

    def block(d, s):
        bwd = jnp.where(s < ncb, ncb - 1 - s, nb + ncb - 1 - s)
        return jnp.where(d == 0, s, bwd)
    return block


def hgrn_fwd(parts, iv, lb, mc, mtc, mrefc):
    t = parts.shape[0]
    nb = t // SCAN_ROWS
    block = _scan_block(nb)

    def body(q_ref, f_ref, i_ref, lb_ref, m_ref, mt_ref, mr_ref, o_ref, ck_ref, st):
        d = pl.program_id(0)

        @pl.when(pl.program_id(1) == 0)
        def _():
            st[...] = jnp.zeros_like(st)

        for c in range(SCAN_STEP):
            rows = pl.ds(pl.multiple_of(jnp.where(d == 0, c * CH, (SCAN_STEP - 1 - c) * CH), CH), CH)
            ck_ref[0, c] = st[...].astype(BF16)
            outs, news = _hgrn_chunk([st[h] for h in range(HEADS)], [q_ref[rows, _hsl(h)] for h in range(HEADS)],
                                     [f_ref[rows, _hsl(h)] for h in range(HEADS)], [i_ref[rows, _hsl(h)] for h in range(HEADS)],
                                     [lb_ref[0, :, _hsl(h)] for h in range(HEADS)], m_ref[0], mt_ref[0], mr_ref[0])
            for h in range(HEADS):
                o_ref[0, rows, _hsl(h)] = outs[h].astype(BF16)
                st[h] = news[h]

    const = lambda d, s: (d, 0, 0)
    at = lambda k: pl.BlockSpec((SCAN_ROWS, D), lambda d, s: (block(d, s), k(d)))
    return pl.pallas_call(
        body, name="hgrn_fwd", grid=(2, nb),
        in_specs=[at(lambda d: 0), at(lambda d: 1 + d), at(lambda d: 0), pl.BlockSpec((1, 1, D), const),
                  pl.BlockSpec((1, CH, CH), const), pl.BlockSpec((1, CH, CH), const), pl.BlockSpec((1, CH, 1), const)],
        out_specs=[pl.BlockSpec((1, SCAN_ROWS, D), lambda d, s: (d, block(d, s), 0)),
                   pl.BlockSpec((1, SCAN_STEP, HEADS, HD, HD), lambda d, s: (d, s, 0, 0, 0))],
        out_shape=[jax.ShapeDtypeStruct((2, t, D), BF16), jax.ShapeDtypeStruct((2, nb * SCAN_STEP, HEADS, HD, HD), BF16)],
        scratch_shapes=[pltpu.VMEM((HEADS, HD, HD), F32)], compiler_params=_cp(2))(parts, parts, iv, lb, mc, mtc, mrefc)


def _mixer_tile(rows, u_ref, v_ref, og_ref, o_ref, lnw_ref, lnb_ref, sw_ref, sb_ref, hnw_ref):
    n = (rows.stop - rows.start) // SGU_CH
    yas, vjps = [], []
    for c in range(n):
        r = slice(rows.start + c * SGU_CH, rows.start + (c + 1) * SGU_CH)
        ya, vjp_a = jax.vjp(_sgu_fn, [u_ref[r, _hsl(g)] for g in range(HEADS)], [v_ref[r, _hsl(g)] for g in range(HEADS)],
                            [lnw_ref[:, _hsl(g)] for g in range(HEADS)], [lnb_ref[:, _hsl(g)] for g in range(HEADS)],
                            [sw_ref[g] for g in range(HEADS)], [sb_ref[g] for g in range(HEADS)])
        yas.append(ya)
        vjps.append(vjp_a)
    yb, vjp_b = jax.vjp(_readout_fn, [o_ref[0, rows, _hsl(h)].astype(F32) + o_ref[1, rows, _hsl(h)].astype(F32)
                                      for h in range(HEADS)],
                        og_ref[rows, :], hnw_ref[...])
    return (yas[0] if n == 1 else jnp.concatenate(yas, axis=0)), yb, vjps, vjp_b


def _part_specs(tm, first, n):
    return [pl.BlockSpec((tm, D), functools.partial(lambda k, i: (i, k), first + k)) for k in range(n)]


def _unless_ctx(skip_ctx, is_ctx, zero_refs, work):
    if not skip_ctx:
        return work()

    @pl.when(is_ctx)
    def _():
        for r in zero_refs:
            r[...] = jnp.zeros_like(r)

    pl.when(jnp.logical_not(is_ctx))(work)


def mixer_fwd(x, parts, o, mod, lnw, lnb, sw, sb, hnw, wa, wb, wo, skip_ctx):
    t = x.shape[0]

    def body(x_ref, u_ref, v_ref, og_ref, ga_ref, gb_ref, o_ref, mod_ref, lnw_ref, lnb_ref, sw_ref, sb_ref, hnw_ref,
             wa_ref, wb_ref, wo_ref, out_ref, pa_ref, pb_ref, y_ref, yat_ref, ybt_ref, mt_ref):
        def work():
            ya, yb, _, _ = _mixer_tile(slice(0, TM), u_ref, v_ref, og_ref, o_ref, lnw_ref, lnb_ref, sw_ref, sb_ref, hnw_ref)
            pa, pb = mm(ya, wa_ref[...]), mm(yb, wb_ref[...])
            merged = jax.nn.sigmoid(ga_ref[...]) * pa + jax.nn.sigmoid(gb_ref[...]) * pb
            y = mm(merged, wo_ref[...])
            out_ref[...] = x_ref[...] + mod_ref[0, 2] * y
            pa_ref[...], pb_ref[...], y_ref[...] = pa.astype(BF16), pb.astype(BF16), y.astype(BF16)
            yat_ref[...], ybt_ref[...], mt_ref[...] = ya.T.astype(BF16), yb.T.astype(BF16), merged.T.astype(BF16)

        _unless_ctx(skip_ctx, pl.program_id(0) == 0, (out_ref, pa_ref, pb_ref, y_ref, yat_ref, ybt_ref, mt_ref), work)

    vec = lambda n: pl.BlockSpec((1, n), lambda i: (0, 0))
    tile = pl.BlockSpec((TM, D), lambda i: (i, 0))
    tile_t = pl.BlockSpec((D, TM), lambda i: (0, i))
    return pl.pallas_call(
        body, name="mixer_fwd", grid=(t // TM,),
        in_specs=[tile] + _part_specs(TM, 4, 5)
        + [pl.BlockSpec((2, TM, D), lambda i: (0, i, 0)), pl.BlockSpec((1, 6, 1, D), _stream_row(TM)), vec(D), vec(D),
           VMEM_WHOLE, VMEM_WHOLE, vec(HD), VMEM_WHOLE, VMEM_WHOLE, VMEM_WHOLE],
        out_specs=[tile] * 4 + [tile_t] * 3,
        out_shape=[jax.ShapeDtypeStruct((t, D), F32)] + [jax.ShapeDtypeStruct((t, D), BF16)] * 3
        + [jax.ShapeDtypeStruct((D, t), BF16)] * 3, compiler_params=_cp(1),
    )(x, parts, parts, parts, parts, parts, o, mod, lnw, lnb, sw, sb, hnw, wa, wb, wo)


def ffn_up_fwd(x, mod, nw, wg, skip_ctx):
    t = x.shape[0]

    def body(x_ref, mod_ref, nw_ref, w_ref, out_ref, ht_ref):
        def work():
            h32 = _norm_mod(x_ref[...], nw_ref[...], mod_ref[0, 3], mod_ref[0, 4])
            ht_ref[...] = h32.T.astype(BF16)
            h = h32.astype(BF16)
            for j in range(N_DEV):
                out_ref[j] = jnp.dot(h, w_ref[j], preferred_element_type=F32)

        _unless_ctx(skip_ctx, pl.program_id(0) == 0, (out_ref, ht_ref), work)

    return pl.pallas_call(
        body, name="ffn_up_fwd", grid=(t // TM,),
        in_specs=[pl.BlockSpec((TM, D), lambda i: (i, 0)), pl.BlockSpec((1, 6, 1, D), _stream_row(TM)),
                  pl.BlockSpec((1, D), lambda i: (0, 0)), VMEM_WHOLE],
        out_specs=[pl.BlockSpec((N_DEV, TM, FF_SLOT), lambda i: (0, i, 0)), pl.BlockSpec((D, TM), lambda i: (0, i))],
        out_shape=[jax.ShapeDtypeStruct((N_DEV, t, FF_SLOT), F32), jax.ShapeDtypeStruct((D, t), BF16)],
        compiler_params=_cp(1))(x, mod, nw, wg)


def _halo_specs(nt):
    per = TM // GRID_W
    last = nt * per - 1
    return [pl.BlockSpec((N_FFK, GRID_W, FF_SLOT), lambda i: (0, jnp.maximum(i * per - 1, 0), 0)),
            pl.BlockSpec((N_FFK, TM, FF_SLOT), lambda i: (0, i, 0)),
            pl.BlockSpec((N_FFK, GRID_W, FF_SLOT), lambda i: (0, jnp.minimum(i * per + per, last), 0))]


def _with_halo(prev_ref, main_ref, next_ref, k, i, nt):
    prev = jnp.where(i >= 2, prev_ref[k], 0.0)
    nxt = jnp.where((i >= 1) & (i <= nt - 2), next_ref[k], 0.0)
    return jnp.concatenate([prev, main_ref[k], nxt], axis=0)


def _tap_valid(dc, i, n_rows, offset):
    r = lax.broadcasted_iota(jnp.int32, (n_rows, 1), 0) - offset
    col = jnp.bitwise_and(r, GRID_W - 1)
    pos = jnp.where(i == 0, r, col) + dc
    return (pos >= 0) & (pos < jnp.where(i == 0, TM, GRID_W))


def _row_weight(cw_ref, k, dr, dc, i):
    w = cw_ref[k, 3 * (dr + 1) + dc + 1:3 * (dr + 1) + dc + 2, :]
    return w if dr == 0 else jnp.where(i == 0, 0.0, w)


def ffn_down_fwd(x, av, mod, cw, cb, wd, skip_ctx):
    t = x.shape[0]
    nt = t // TM

    def body(x_ref, ap_ref, am_ref, an_ref, v_ref, mod_ref, cw_ref, cb_ref, wd_ref, out_ref, ac_ref, y_ref, z_ref):
        i = pl.program_id(0)

        def work():
            y = None
            for k in range(N_FFK):
                a_ext = _with_halo(ap_ref, am_ref, an_ref, k, i, nt)
                conv = jnp.zeros((TM, FF_SLOT), F32) + cb_ref[k]
                for dc in (-1, 0, 1):
                    col = functools.reduce(lambda p, q: p + q, [a_ext[GRID_W + GRID_W * dr:GRID_W + GRID_W * dr + TM]
                                                                * _row_weight(cw_ref, k, dr, dc, i) for dr in (-1, 0, 1)])
                    conv = conv + (col if dc == 0 else
                                   jnp.where(_tap_valid(dc, i, TM, 0), pltpu.roll(col, (-dc) % TM, 0), 0.0))
                ac_ref[k] = conv.astype(BF16)
                z = _glu_fn(conv, v_ref[k]).astype(BF16)
                z_ref[k] = z
                part = mm(z, wd_ref[k])
                y = part if y is None else y + part
            y_ref[...] = y
            out_ref[...] = x_ref[...] + mod_ref[0, 5] * y

        _unless_ctx(skip_ctx, i == 0, (out_ref, ac_ref, y_ref, z_ref), work)

    tile = pl.BlockSpec((TM, D), lambda i: (i, 0))
    half = lambda first: pl.BlockSpec((N_FFK, TM, FF_SLOT), lambda i: (first, i, 0))
    return pl.pallas_call(
        body, name="ffn_down_fwd", grid=(nt,),
        in_specs=[tile] + _halo_specs(nt) + [half(1), pl.BlockSpec((1, 6, 1, D), _stream_row(TM)), VMEM_WHOLE, VMEM_WHOLE,
                                             VMEM_WHOLE],
        out_specs=[tile, half(0), tile, half(0)],
        out_shape=[jax.ShapeDtypeStruct((t, D), F32), jax.ShapeDtypeStruct((N_FFK, t, FF_SLOT), BF16),
                   jax.ShapeDtypeStruct((t, D), F32), jax.ShapeDtypeStruct((N_FFK, t, FF_SLOT), BF16)],
        compiler_params=_cp(1))(x, av, av, av, av, mod, cw, cb, wd)


def loss_fwd_bwd(x, target, fw):
    t = x.shape[0]

    def body(x_ref, t_ref, w_ref, loss_ref, dx_ref, dw_ref):
        i = pl.program_id(0)

        @pl.when(i == 0)
        def _():
            loss_ref[...] = jnp.zeros_like(loss_ref)
            dw_ref[...] = jnp.zeros_like(dw_ref)
            dx_ref[...] = jnp.zeros_like(dx_ref)

        @pl.when(i > 0)
        def _():
            y, vjp = jax.vjp(_rms, x_ref[...], w_ref[...])
            err = y - t_ref[...]
            loss_ref[...] += 0.5 * jnp.sum(jnp.sum(err * err, axis=-1, keepdims=True) / D)
            dx, dw = vjp(err / D)
            dx_ref[...] = dx
            dw_ref[...] += dw

    return pl.pallas_call(
        body, name="loss_fwd_bwd", grid=(t // TM,),
        in_specs=[pl.BlockSpec((TM, D), lambda i: (i, 0)), pl.BlockSpec((TM, D), lambda i: (jnp.maximum(i - 1, 0), 0)),
                  pl.BlockSpec((1, D), lambda i: (0, 0))],
        out_specs=[pl.BlockSpec((8, 128), lambda i: (0, 0)), pl.BlockSpec((TM, D), lambda i: (i, 0)),
                   pl.BlockSpec((1, D), lambda i: (0, 0))],
        out_shape=[jax.ShapeDtypeStruct((8, 128), F32), jax.ShapeDtypeStruct((t, D), F32), jax.ShapeDtypeStruct((1, D), F32)],
        compiler_params=_cp(1))(x, target, fw)


def _stream_add(ref, k, is_ctx, val):
    ref[0, k] += jnp.where(is_ctx, val, 0.0)
    ref[1, k] += jnp.where(is_ctx, 0.0, val)


def ffn_down_bwd(dx, ac, av, y, mod, wd, skip_ctx):
    t = dx.shape[0]
    nt = t // TM

    def body(dx_ref, ac_ref, v_ref, y_ref, mod_ref, wd_ref, dav_ref, dac_ref, dout_ref, dg_ref):
        i = pl.program_id(0)

        @pl.when(i == 0)
        def _():
            dg_ref[...] = jnp.zeros_like(dg_ref)

        def work():
            _stream_add(dg_ref, 0, i == 0, jnp.sum(dx_ref[...] * y_ref[...], axis=0, keepdims=True))
            dout = (mod_ref[0, 5] * dx_ref[...]).astype(BF16)
            dout_ref[...] = dout
            for k in range(N_FFK):
                _, vjp = jax.vjp(_glu_fn, ac_ref[k].astype(F32), v_ref[k])
                dac, dv = vjp(mm_nt(dout, wd_ref[k]))
                dac_ref[k] = dac
                dav_ref[k] = dv.astype(BF16)

        _unless_ctx(skip_ctx, i == 0, (dav_ref, dac_ref, dout_ref), work)

    tile = pl.BlockSpec((TM, D), lambda i: (i, 0))
    half = lambda first: pl.BlockSpec((N_FFK, TM, FF_SLOT), lambda i: (first, i, 0))
    return pl.pallas_call(
        body, name="ffn_down_bwd", grid=(nt,),
        in_specs=[tile, half(0), half(1), tile, pl.BlockSpec((1, 6, 1, D), _stream_row(TM)), VMEM_WHOLE],
        out_specs=[half(1), half(0), tile, pl.BlockSpec((2, 1, 1, D), lambda i: (0, 0, 0, 0))],
        out_shape=[jax.ShapeDtypeStruct((N_DEV, t, FF_SLOT), BF16), jax.ShapeDtypeStruct((N_FFK, t, FF_SLOT), F32),
                   jax.ShapeDtypeStruct((t, D), BF16), jax.ShapeDtypeStruct((2, 1, 1, D), F32)],
        compiler_params=_cp(1))(dx, ac, av, y, mod, wd)


def conv_bwd(dav, dac, av, cw, skip_ctx):
    t = dac.shape[1]
    nt = t // TM

    def body(dav_in, gp_ref, gm_ref, gn_ref, ap_ref, am_ref, an_ref, cw_ref, dav_ref, dcw_ref, dcb_ref):
        i = pl.program_id(0)

        @pl.when(i == 0)
        def _():
            dcw_ref[...] = jnp.zeros_like(dcw_ref)
            dcb_ref[...] = jnp.zeros_like(dcb_ref)

        def work():
            for k in range(N_FFK):
                g_ext = _with_halo(gp_ref, gm_ref, gn_ref, k, i, nt)
                a_ext = _with_halo(ap_ref, am_ref, an_ref, k, i, nt)
                g_main = gm_ref[k]
                dcb_ref[k] += jnp.sum(g_main, axis=0, keepdims=True)
                da = jnp.zeros((TM, FF_SLOT), F32)
                for dc in (-1, 0, 1):
                    valid = _tap_valid(dc, i, TM, 0)
                    q = functools.reduce(lambda p, r: p + r, [g_ext[GRID_W - GRID_W * dr:GRID_W - GRID_W * dr + TM]
                                                              * _row_weight(cw_ref, k, dr, dc, i) for dr in (-1, 0, 1)])
                    da = da + (q if dc == 0 else pltpu.roll(jnp.where(valid, q, 0.0), dc % TM, 0))
                    g_shift = g_main if dc == 0 else pltpu.roll(jnp.where(valid, g_main, 0.0), dc % TM, 0)
                    for dr in (-1, 0, 1):
                        lo = GRID_W + GRID_W * dr
                        tap = 3 * (dr + 1) + dc + 1
                        dw = jnp.sum(g_shift * a_ext[lo:lo + TM], axis=0, keepdims=True)
                        dcw_ref[k, tap:tap + 1, :] += dw if dr == 0 else jnp.where(i == 0, 0.0, dw)
                dav_ref[k] = da.astype(BF16)

        _unless_ctx(skip_ctx, i == 0, (dav_ref,), work)

    whole = lambda rows: pl.BlockSpec((N_FFK, rows, FF_SLOT), lambda i: (0, 0, 0))
    return pl.pallas_call(
        body, name="conv_bwd", grid=(nt,),
        in_specs=[ANY] + _halo_specs(nt) + _halo_specs(nt) + [VMEM_WHOLE],
        out_specs=[pl.BlockSpec((N_FFK, TM, FF_SLOT), lambda i: (0, i, 0)), whole(9), whole(1)],
        out_shape=[jax.ShapeDtypeStruct(dav.shape, BF16), jax.ShapeDtypeStruct((N_FFK, 9, FF_SLOT), F32),
                   jax.ShapeDtypeStruct((N_FFK, 1, FF_SLOT), F32)],
        input_output_aliases={0: 0}, compiler_params=_cp(1))(dav, dac, dac, dac, av, av, av, cw)


def _norm_mod_bwd(x_ref, nw_ref, mod_ref, k_shift, dh, dx_in, dx_ref, dnw_ref, dmod_ref, is_ctx):
    _, vjp = jax.vjp(_norm_mod, x_ref[...], nw_ref[...], mod_ref[0, k_shift], mod_ref[0, k_shift + 1])
    dx, dnw, dshift, dscale = vjp(dh)
    dx_ref[...] = dx_in + dx
    dnw_ref[...] += dnw
    _stream_add(dmod_ref, 0, is_ctx, dshift)
    _stream_add(dmod_ref, 1, is_ctx, dscale)


def ffn_up_bwd_x(dx2, x, dav, mod, nw, wg, skip_ctx):
    t = x.shape[0]

    def body(dx2_ref, x_ref, dav_ref, mod_ref, nw_ref, w_ref, dx_ref, dnw_ref, dmod_ref):
        i = pl.program_id(0)

        @pl.when(i == 0)
        def _():
            dnw_ref[...] = jnp.zeros_like(dnw_ref)
            dmod_ref[...] = jnp.zeros_like(dmod_ref)

        def work():
            dh = mm_nt(dav_ref[0], w_ref[0])
            for j in range(1, N_DEV):
                dh = dh + mm_nt(dav_ref[j], w_ref[j])
            _norm_mod_bwd(x_ref, nw_ref, mod_ref, 3, dh, dx2_ref[...], dx_ref, dnw_ref, dmod_ref, i == 0)

        _unless_ctx(skip_ctx, i == 0, (dx_ref,), work)

    tile = pl.BlockSpec((TM, D), lambda i: (i, 0))
    return pl.pallas_call(
        body, name="ffn_up_bwd_x", grid=(t // TM,),
        in_specs=[tile, tile, pl.BlockSpec((N_DEV, TM, FF_SLOT), lambda i: (0, i, 0)), pl.BlockSpec((1, 6, 1, D), _stream_row(TM)),
                  pl.BlockSpec((1, D), lambda i: (0, 0)), VMEM_WHOLE],
        out_specs=[tile, pl.BlockSpec((1, D), lambda i: (0, 0)), pl.BlockSpec((2, 2, 1, D), lambda i: (0, 0, 0, 0))],
        out_shape=[jax.ShapeDtypeStruct((t, D), F32), jax.ShapeDtypeStruct((1, D), F32), jax.ShapeDtypeStruct((2, 2, 1, D), F32)],
        compiler_params=_cp(1))(dx2, x, dav, mod, nw, wg)


def weight_grad(at, dout, slot, name, after=None):
    rows, t = at.shape
    stacked = dout.ndim == 3
    n = dout.shape[0] if stacked else dout.shape[1] // slot

    def body(a_ref, d_ref, *rest):
        dw_ref = rest[-1]
        dw_ref[0] = jnp.dot(a_ref[...], d_ref[0] if stacked else d_ref[...], preferred_element_type=F32).astype(dw_ref.dtype)

    d_spec = pl.BlockSpec((1, t, slot), lambda j: (j, 0, 0)) if stacked else pl.BlockSpec((t, slot), lambda j: (0, j))
    extra = [] if after is None else [jnp.reshape(after, (1, 1))]
    return pl.pallas_call(
        body, name=name, grid=(n,), in_specs=[VMEM_WHOLE, d_spec] + [ANY] * len(extra),
        out_specs=pl.BlockSpec((1, rows, slot), lambda j: (j, 0, 0)),
        out_shape=jax.ShapeDtypeStruct((n, rows, slot), GRAD_WIRE), compiler_params=_cp(1))(at, dout, *extra)


def weight_grad_rows(at, dout, name):
    n, t, rows = at.shape
    cols = dout.shape[1]

    def body(a_ref, d_ref, dw_ref):
        dw_ref[0] = _dot(a_ref[0], d_ref[...], ((0,), (0,))).astype(dw_ref.dtype)

    return pl.pallas_call(
        body, name=name, grid=(n,), in_specs=[pl.BlockSpec((1, t, rows), lambda k: (k, 0, 0)), VMEM_WHOLE],
        out_specs=pl.BlockSpec((1, rows, cols), lambda k: (k, 0, 0)),
        out_shape=jax.ShapeDtypeStruct((n, rows, cols), GRAD_WIRE), compiler_params=_cp(1))(at, dout)


def mixer_bwd(dx, parts, o, pa, pb, y, mod, lnw, lnb, sw, sb, hnw, wa, wb, wo, skip_ctx):
    t = dx.shape[0]
    tm = TM
    n_ctx = CTX // tm

    def body(dx_ref, u_ref, v_ref, og_ref, ga_ref, gb_ref, o_ref, pa_ref, pb_ref, y_ref, mod_ref, lnw_ref, lnb_ref, sw_ref,
             sb_ref, hnw_ref, wa_ref, wb_ref, wo_ref, dp_ref, do_ref, dy_ref, dpa_ref, dpb_ref, dlnw_ref, dlnb_ref, dsw_ref,
             dsb_ref, dhnw_ref, dg_ref):
        i = pl.program_id(0)

        @pl.when(i == 0)
        def _():
            for r in (dlnw_ref, dlnb_ref, dsw_ref, dsb_ref, dhnw_ref, dg_ref):
                r[...] = jnp.zeros_like(r)

        def work():
            _, _, vjps, vjp_b = _mixer_tile(slice(0, tm), u_ref, v_ref, og_ref, o_ref, lnw_ref, lnb_ref, sw_ref, sb_ref, hnw_ref)
            pa, pb = pa_ref[...].astype(F32), pb_ref[...].astype(F32)
            sa, sbg = jax.nn.sigmoid(ga_ref[...]), jax.nn.sigmoid(gb_ref[...])
            dxv = dx_ref[...]
            _stream_add(dg_ref, 0, i < n_ctx, jnp.sum(dxv * y_ref[...].astype(F32), axis=0, keepdims=True))
            dy = (mod_ref[0, 2] * dxv).astype(BF16)
            dy_ref[...] = dy
            dmerged = mm_nt(dy, wo_ref[...])
            dpa, dpb = (sa * dmerged).astype(BF16), (sbg * dmerged).astype(BF16)
            dpa_ref[...], dpb_ref[...] = dpa, dpb
            first = 4 * D
            dp_ref[:, first + 3 * D:first + 4 * D] = (dmerged * pa * sa * (1.0 - sa)).astype(BF16)
            dp_ref[:, first + 4 * D:first + 5 * D] = (dmerged * pb * sbg * (1.0 - sbg)).astype(BF16)
            dya = mm_nt(dpa, wa_ref[...])
            dob, dog, dhnw = vjp_b(mm_nt(dpb, wb_ref[...]))
            dp_ref[:, first + 2 * D:first + 3 * D] = dog.astype(BF16)
            dhnw_ref[...] += dhnw
            for g in range(HEADS):
                do_ref[:, _hsl(g)] = dob[g]
            for c, vjp_a in enumerate(vjps):
                rows = slice(c * SGU_CH, (c + 1) * SGU_CH)
                dub, dvb, dlnw, dlnb, dsw, dsb = vjp_a(dya[rows])
                for g in range(HEADS):
                    dp_ref[rows, first + g * HD:first + (g + 1) * HD] = dub[g].astype(BF16)
                    dp_ref[rows, first + D + g * HD:first + D + (g + 1) * HD] = dvb[g].astype(BF16)
                    dlnw_ref[:, _hsl(g)] += dlnw[g]
                    dlnb_ref[:, _hsl(g)] += dlnb[g]
                    dsw_ref[g] += dsw[g]
                    dsb_ref[g] += dsb[g]

        _unless_ctx(skip_ctx, i < n_ctx, (dp_ref, do_ref, dy_ref, dpa_ref, dpb_ref), work)

    vec = lambda n: pl.BlockSpec((1, n), lambda i: (0, 0))
    tile = pl.BlockSpec((tm, D), lambda i: (i, 0))
    sds = jax.ShapeDtypeStruct
    return pl.pallas_call(
        body, name="mixer_bwd", grid=(t // tm,),
        in_specs=[tile] + _part_specs(tm, 4, 5)
        + [pl.BlockSpec((2, tm, D), lambda i: (0, i, 0)), tile, tile, tile, pl.BlockSpec((1, 6, 1, D), _stream_row(tm)),
           vec(D), vec(D), VMEM_WHOLE, VMEM_WHOLE, vec(HD), VMEM_WHOLE, VMEM_WHOLE, VMEM_WHOLE],
        out_specs=[pl.BlockSpec((tm, D_IN), lambda i: (i, 0)), tile, tile, tile, tile, vec(D), vec(D),
                   VMEM_WHOLE, VMEM_WHOLE, vec(HD), pl.BlockSpec((2, 1, 1, D), lambda i: (0, 0, 0, 0))],
        out_shape=[sds((t, D_IN), BF16), sds((t, D), F32), sds((t, D), BF16), sds((t, D), BF16), sds((t, D), BF16),
                   sds((1, D), F32), sds((1, D), F32), sds((HEADS, SGU_CH, SGU_CH), F32), sds((HEADS, SGU_CH, 1), F32),
                   sds((1, HD), F32), sds((2, 1, 1, D), F32)],
        compiler_params=_cp(1))(dx, parts, parts, parts, parts, parts, o, pa, pb, y, mod, lnw, lnb, sw, sb, hnw, wa, wb, wo)


def hgrn_bwd(d, parts, lb, mc, mtc, mrefc, ck, do, first=None, dparts=None):
    t = parts.shape[0]
    nb = t // SCAN_ROWS
    block = _scan_block(nb)
    rev = lambda s: block(d, nb - 1 - s)

    def body(q_ref, f_ref, i_ref, lb_ref, m_ref, mt_ref, mr_ref, ck_ref, do_ref, *rest):
        dst = rest[-1]
        dlb_ref = rest[-2]

        @pl.when(pl.program_id(0) == 0)
        def _():
            dst[...] = jnp.zeros_like(dst)
            dlb_ref[...] = jnp.zeros_like(dlb_ref)

        heads = range(HEADS)
        fn = functools.partial(_hgrn_chunk, m=m_ref[0], mt=mt_ref[0], mref=mr_ref[0])
        for c in reversed(range(SCAN_STEP)):
            first_row = c * CH if d == 0 else (SCAN_STEP - 1 - c) * CH
            rows = slice(first_row, first_row + CH)
            _, vjp = jax.vjp(fn, [ck_ref[0, c, h].astype(F32) for h in heads], [q_ref[rows, _hsl(h)] for h in heads],
                             [f_ref[rows, _hsl(h)] for h in heads], [i_ref[rows, _hsl(h)] for h in heads],
                             [lb_ref[0, :, _hsl(h)] for h in heads])
            dstl, dq, df, di, dlb = vjp(([do_ref[rows, _hsl(h)] for h in heads], [dst[h] for h in heads]))
            for h in heads:
                dst[h] = dstl[h]
                dlb_ref[0, :, _hsl(h)] += dlb[h]
                if d == 0:
                    dq_ref, df_ref, di_ref = rest[:3]
                    dq_ref[rows, _hsl(h)] = dq[h].astype(BF16)
                    df_ref[rows, _hsl(h)] = df[h].astype(BF16)
                    di_ref[rows, _hsl(h)] = di[h].astype(BF16)
                else:
                    dq0_ref, df0_ref, di0_ref, _, dp_ref = rest[:5]
                    col = lambda k: slice(k * D + h * HD, k * D + (h + 1) * HD)
                    dp_ref[rows, col(0)] = (dq0_ref[rows, _hsl(h)].astype(F32) + dq[h]).astype(BF16)
                    dp_ref[rows, col(1)] = df0_ref[rows, _hsl(h)]
                    dp_ref[rows, col(2)] = df[h].astype(BF16)
                    dp_ref[rows, col(3)] = (di0_ref[rows, _hsl(h)].astype(F32) + di[h]).astype(BF16)

    const = lambda s: (d, 0, 0)
    at = lambda k: pl.BlockSpec((SCAN_ROWS, D), lambda s: (rev(s), k))
    in_specs = [at(0), at(1 + d), at(3), pl.BlockSpec((1, 1, D), const), pl.BlockSpec((1, CH, CH), const),
                pl.BlockSpec((1, CH, CH), const), pl.BlockSpec((1, CH, 1), const),
                pl.BlockSpec((1, SCAN_STEP, HEADS, HD, HD), lambda s: (d, nb - 1 - s, 0, 0, 0)), at(0)]
    dlb_spec, dlb_shape = pl.BlockSpec((1, 1, D), lambda s: (0, 0, 0)), jax.ShapeDtypeStruct((1, 1, D), F32)
    common = dict(grid=(nb,), scratch_shapes=[pltpu.VMEM((HEADS, HD, HD), F32)], compiler_params=_cp(1))
    if d == 0:
        return pl.pallas_call(body, name="hgrn_bwd_fwd_dir", in_specs=in_specs, out_specs=[at(0)] * 3 + [dlb_spec],
                              out_shape=[jax.ShapeDtypeStruct((t, D), BF16)] * 3 + [dlb_shape], **common,
                              )(parts, parts, parts, lb, mc, mtc, mrefc, ck, do)
    return pl.pallas_call(body, name="hgrn_bwd_bwd_dir", in_specs=in_specs + [at(0)] * 3 + [ANY],
                          out_specs=[pl.BlockSpec((SCAN_ROWS, 4 * D), lambda s: (rev(s), 0)), dlb_spec],
                          out_shape=[jax.ShapeDtypeStruct(dparts.shape, BF16), dlb_shape], input_output_aliases={12: 0},
                          **common)(parts, parts, parts, lb, mc, mtc, mrefc, ck, do, *first, dparts)


def in_proj_bwd_x(dx1, x, dparts, mod, nw, wg, after=None, latent_only=False):
    t = x.shape[0]
    tm = TM
    n_ctx = CTX // tm

    def body(dx1_ref, x_ref, dp_ref, mod_ref, nw_ref, w_ref, *rest):
        dx_ref, dnw_ref, dmod_ref = rest[-3:]
        i = pl.program_id(0)

        @pl.when(i == 0)
        def _():
            dnw_ref[...] = jnp.zeros_like(dnw_ref)
            dmod_ref[...] = jnp.zeros_like(dmod_ref)

        dh = mm_nt(dp_ref[:, 0:IN_SLOT], w_ref[0])
        for j in range(1, N_DEV):
            dh = dh + mm_nt(dp_ref[:, j * IN_SLOT:(j + 1) * IN_SLOT], w_ref[j])
        _norm_mod_bwd(x_ref, nw_ref, mod_ref, 0, dh, dx1_ref[...], dx_ref, dnw_ref, dmod_ref, i < n_ctx)

    tile = pl.BlockSpec((tm, D), lambda i: (i, 0))
    extra = [] if after is None else [jnp.reshape(after, (1, 1))]
    return pl.pallas_call(
        body, name="in_proj_bwd_x", grid=(t // tm,),
        in_specs=[tile, tile, pl.BlockSpec((tm, D_IN), lambda i: (i, 0)), pl.BlockSpec((1, 6, 1, D), _stream_row(tm)),
                  pl.BlockSpec((1, D), lambda i: (0, 0)), VMEM_WHOLE] + [ANY] * len(extra),
        out_specs=[pl.BlockSpec((tm, D), lambda i: (jnp.maximum(i - n_ctx, 0), 0)) if latent_only else tile,
                   pl.BlockSpec((1, D), lambda i: (0, 0)), pl.BlockSpec((2, 2, 1, D), lambda i: (0, 0, 0, 0))],
        out_shape=[jax.ShapeDtypeStruct((t - CTX if latent_only else t, D), F32), jax.ShapeDtypeStruct((1, D), F32),
                   jax.ShapeDtypeStruct((2, 2, 1, D), F32)],
        compiler_params=_cp(1))(dx1, x, dparts, mod, nw, wg, *extra)


def _lb_fn(h0, h1):
    m = jnp.maximum(h0, h1)
    e0, e1 = jnp.exp(h0 - m), jnp.exp(h1 - m)
    return e1 / (e0 + e1)


def lower_bounds(hlb):
    def body(h_ref, out_ref):
        out_ref[...] = _lb_fn(h_ref[0:1, :], h_ref[1:2, :])
    return pl.pallas_call(body, name="lower_bounds", out_shape=jax.ShapeDtypeStruct((1, 2 * D), F32))(hlb)


def lower_bounds_bwd(hlb, dlb1):
    def body(h_ref, d_ref, out_ref):
        _, vjp = jax.vjp(_lb_fn, h_ref[0:1, :], h_ref[1:2, :])
        d0, d1 = vjp(d_ref[...])
        out_ref[0:1, :] = d0
        out_ref[1:2, :] = d1
    return pl.pallas_call(body, name="lower_bounds_bwd", out_shape=jax.ShapeDtypeStruct((2, 2 * D), F32))(hlb, dlb1)


def _ada_fn(c_all, cctx8, w, b):
    dot = lambda a, l: mm(_silu(a), w[l]) + b[l]
    return [dot(c_all, l) for l in range(2)], [dot(cctx8, l) for l in range(2)]


def ada_fwd(c_all, cctx8, w, b):
    cols = w.shape[-1]

    def body(c_ref, cc_ref, w_ref, b_ref, out_ref):
        ox, oc = _ada_fn(c_ref[...], cc_ref[...], [w_ref[0], w_ref[1]], [b_ref[0], b_ref[1]])
        for l in range(2):
            out_ref[l, 0] = ox[l]
            out_ref[l, 1] = oc[l]
    return pl.pallas_call(body, name="ada_fwd", out_shape=jax.ShapeDtypeStruct((2, 2, N_DEV, cols), F32),
                          compiler_params=_cp(0))(c_all, cctx8, w, b)


def ada_bwd(c_all, cctx8, w, b, dmx, dmc):
    cols = w.shape[-1]

    def body(c_ref, cc_ref, w_ref, b_ref, dmx_ref, dmc_ref, dw_ref, dc_ref):
        fn = lambda cc, w0, w1: _ada_fn(c_ref[...], cc, [w0, w1], [b_ref[0], b_ref[1]])
        _, vjp = jax.vjp(fn, cc_ref[...], w_ref[0], w_ref[1])
        dcc, dw0, dw1 = vjp(([dmx_ref[0], dmx_ref[1]], [dmc_ref[0], dmc_ref[1]]))
        dw_ref[0] = dw0
        dw_ref[1] = dw1
        dc_ref[...] = jnp.sum(dcc, axis=0, keepdims=True)
    return pl.pallas_call(body, name="ada_bwd", out_shape=[jax.ShapeDtypeStruct((2, D, cols), F32), jax.ShapeDtypeStruct((1, D), F32)],
                          compiler_params=_cp(0))(c_all, cctx8, w, b, dmx, dmc)


def adamw(w, m, v, gparts, name):
    r, c = w.shape
    p = gparts.shape[0]
    rt = r
    while rt % 16 == 0 and (p + 7) * rt * c * 4 * 2 > 24 * 2 ** 20:
        rt //= 2

    def body(w_ref, m_ref, v_ref, g_ref, go_ref, d_ref, mo_ref, vo_ref):
        g = g_ref[0].astype(F32)
        for k in range(1, p):
            g = g + g_ref[k].astype(F32)
        m2 = ADAM_B1 * m_ref[...] + (1.0 - ADAM_B1) * g
        v2 = ADAM_B2 * v_ref[...] + (1.0 - ADAM_B2) * (g * g)
        m_hat = m2 / (1.0 - ADAM_B1 ** ADAM_STEP)
        v_hat = v2 / (1.0 - ADAM_B2 ** ADAM_STEP)
        go_ref[...] = g
        d_ref[...] = -ADAM_LR * (m_hat / (jnp.sqrt(v_hat) + ADAM_EPS) + ADAM_WD * w_ref[...])
        mo_ref[...] = m2
        vo_ref[...] = v2

    tile = pl.BlockSpec((rt, c), lambda i: (i, 0))
    return pl.pallas_call(
        body, name=name, grid=(r // rt,),
        in_specs=[tile, tile, tile, pl.BlockSpec((p, rt, c), lambda i: (0, i, 0))], out_specs=[tile] * 4,
        out_shape=[jax.ShapeDtypeStruct((r, c), F32)] * 4, compiler_params=_cp(1))(w, m, v, gparts)


def _me():
    x, y, c = lax.axis_index("x"), lax.axis_index("y"), lax.axis_index("c")
    return x, y, c, 4 * x + 2 * y + c


def _peer(x, y, c, p):
    fx, fy, fc = (p >> 2) & 1, (p >> 1) & 1, p & 1
    return (1 - x if fx else x, 1 - y if fy else y, 1 - c if fc else c)


def all_gather(arrs, name, after=None):
    n = len(arrs)
    extra = [] if after is None else list(after) if isinstance(after, (list, tuple)) else [after]

    def body(*refs):
        ins, outs = refs[:n], refs[n + len(extra):2 * n + len(extra)]
        send, recv, local = refs[2 * n + len(extra):]
        x, y, c, me = _me()
        copies = []
        for a in range(n):
            lc = pltpu.make_async_copy(ins[a], outs[a].at[me], local.at[a])
            lc.start()
            copies.append(lc)
            for p in range(1, N_DEV):
                cp = pltpu.make_async_remote_copy(src_ref=ins[a], dst_ref=outs[a].at[me], send_sem=send.at[a, p - 1],
                                                  recv_sem=recv.at[a, p - 1], device_id=_peer(x, y, c, p),
                                                  device_id_type=pl.DeviceIdType.MESH)
                cp.start()
                copies.append(cp)
        for cp in copies:
            cp.wait()

    return pl.pallas_call(
        body, name=name, in_specs=[ANY] * (n + len(extra)), out_specs=[ANY] * n,
        out_shape=[jax.ShapeDtypeStruct((N_DEV,) + a.shape, a.dtype) for a in arrs],
        scratch_shapes=[pltpu.SemaphoreType.DMA((n, N_DEV - 1)), pltpu.SemaphoreType.DMA((n, N_DEV - 1)),
                        pltpu.SemaphoreType.DMA((n,))])(*arrs, *extra)


HBM = pl.BlockSpec(memory_space=pltpu.HBM)
SEM = pl.BlockSpec(memory_space=pltpu.SEMAPHORE)


def _in_hbm(a):
    return pltpu.with_memory_space_constraint(a, pltpu.HBM)


ALL_PEERS = tuple(range(1, N_DEV))
SAME_CORE_AND_SIBLING = (1, 2, 4, 6)
OTHER_CHIPS = (2, 4, 6)


def _exchange_refs(srcs, lands, layer, scatter, a, x, y, c, p, forward=False):
    me = 4 * x + 2 * y + c
    px, py, pc = _peer(x, y, c, p) if p else (x, y, c)
    if forward and p:
        slot = lands[a].at[4 * px + 2 * py + pc]
        return slot, slot, _peer(x, y, c, 1)
    dst = lands[a].at[me] if layer is None else lands[a].at[me, layer]
    src = srcs[a].at[4 * px + 2 * py + pc] if scatter else dst
    return src, dst, (px, py, pc)


def exchange_start(srcs, lands, layer, scatter, name, after=None, peers=ALL_PEERS, forward=False):
    n, ns = len(lands), len(srcs)
    extra = [] if after is None else [after]

    def body(*refs):
        ins, lz = refs[:ns], refs[ns:ns + n]
        send, recv = refs[ns + n + len(extra)], refs[ns + n + len(extra) + 1]
        token = refs[-1]
        x, y, c, _ = _me()
        for a in range(n):
            for p in peers:
                src, dst, peer = _exchange_refs(ins, lz, layer, scatter, a, x, y, c, p, forward)
                k = a * (N_DEV - 1) + p - 1
                pltpu.make_async_remote_copy(src_ref=src, dst_ref=dst, send_sem=send.at[k], recv_sem=recv.at[k],
                                             device_id=peer, device_id_type=pl.DeviceIdType.MESH).start()
        token[...] = jnp.zeros_like(token)

    thru = [pltpu.HBM(a.shape, a.dtype) for a in list(srcs) + list(lands)]
    out = pl.pallas_call(
        body, name=name, in_specs=[HBM] * (ns + n) + [ANY] * len(extra),
        out_specs=[SEM, SEM] + [HBM] * (ns + n) + [pl.BlockSpec(memory_space=pltpu.VMEM)],
        out_shape=[pltpu.SemaphoreType.DMA((n * (N_DEV - 1),)), pltpu.SemaphoreType.DMA((n * (N_DEV - 1),))] + thru
        + [jax.ShapeDtypeStruct((8, 128), F32)],
        input_output_aliases={i: 2 + i for i in range(ns + n)},
        compiler_params=pltpu.CompilerParams(has_side_effects=pltpu.SideEffectType.DATAFLOW_SIDE_EFFECTING),
    )(*[_in_hbm(a) for a in list(srcs) + list(lands)], *extra)
    return out[0], out[1], out[2:2 + ns], out[2 + ns:2 + ns + n], out[-1]


def exchange_wait(send, recv, srcs, lands, layer, scatter, after, name, peers=ALL_PEERS):
    n, ns = len(lands), len(srcs)

    def body(*refs):
        ins, lz = refs[:ns], refs[ns:ns + n]
        send_ref, recv_ref = refs[ns + n], refs[ns + n + 1]
        x, y, c, _ = _me()
        for a in range(n):
            for p in peers:
                src, dst, peer = _exchange_refs(ins, lz, layer, scatter, a, x, y, c, 0)
                k = a * (N_DEV - 1) + p - 1
                cp = pltpu.make_async_remote_copy(src_ref=src, dst_ref=dst, send_sem=send_ref.at[k],
                                                  recv_sem=recv_ref.at[k], device_id=peer,
                                                  device_id_type=pl.DeviceIdType.MESH)
                cp.wait_send()
                cp.wait_recv()

    thru = [pltpu.HBM(a.shape, a.dtype) for a in list(srcs) + list(lands)]
    out = pl.pallas_call(
        body, name=name, in_specs=[HBM] * (ns + n) + [SEM, SEM, ANY], out_specs=[HBM] * (ns + n), out_shape=thru,
        input_output_aliases={i: i for i in range(ns + n)},
        compiler_params=pltpu.CompilerParams(has_side_effects=pltpu.SideEffectType.DATAFLOW_SIDE_EFFECTING),
    )(*srcs, *lands, send, recv, after)
    return out[ns:]


def place_own(src, land, me, layer, scatter, name, src_layer=None, after=None):
    create = isinstance(land, jax.ShapeDtypeStruct)
    r, c = src.shape[-2:]
    rt = r
    while rt % 32 == 0 and rt * c * 4 > 2 ** 21:
        rt //= 2

    extra = [] if after is None else [after]

    def body(me_ref, src_ref, *rest):
        out_ref = rest[-1]
        out_ref[...] = src_ref[...].reshape(out_ref.shape).astype(out_ref.dtype)

    src_spec = (pl.BlockSpec((1, rt, c), lambda i, m: (m[0], i, 0)) if scatter else
                pl.BlockSpec((rt, c), lambda i, m: (i, 0)) if src_layer is None else
                pl.BlockSpec((1, rt, c), lambda i, m: (src_layer, i, 0)))
    out_spec = (pl.BlockSpec((1, rt, c), lambda i, m: (m[0], i, 0)) if layer is None
                else pl.BlockSpec((1, 1, rt, c), lambda i, m: (m[0], layer, i, 0)))
    grid_spec = pltpu.PrefetchScalarGridSpec(num_scalar_prefetch=1, grid=(r // rt,),
                                             in_specs=[src_spec] + ([] if create else [ANY]) + [ANY] * len(extra),
                                             out_specs=out_spec)
    return pl.pallas_call(body, name=name, grid_spec=grid_spec, out_shape=jax.ShapeDtypeStruct(land.shape, land.dtype),
                          input_output_aliases={} if create else {2: 0}, compiler_params=_cp(1),
                          )(*((me, src) if create else (me, src, land)), *extra)


def _scan_constants():
    r = lax.broadcasted_iota(jnp.int32, (CH, CH), 0)
    s = lax.broadcasted_iota(jnp.int32, (CH, CH), 1)
    lower = (s <= r).astype(F32)
    t = jnp.arange(CH)[:, None]
    mc = jnp.stack([lower, lower.T])
    mref = jnp.stack([(t <= CH // 2 - 1).astype(F32), (t >= CH // 2).astype(F32)])
    return mc, jnp.stack([lower.T, lower]), mref


def local_step(x, ctx, target, mod, lb, w, fetch=None, publish=None, small_ready=None, small_early=None):
    kept = {}

    def keep(l, part, grads):
        kept[(l, part)] = grads
        return 0.0

    fetch = fetch or (lambda l, part, after: w)
    publish = publish or keep
    n_layers = len(mod)
    mc, mtc, mrefc = _scan_constants()
    xs = jnp.concatenate([ctx, x], axis=0)
    saved, big = [], []
    for l in range(n_layers):
        wl = dict(fetch(l, "in", xs))
        parts, ht, iv = in_proj_fwd(xs, mod[l], w["nw1"][l], wl["win"][l])
        o, ck = hgrn_fwd(parts, iv, lb[l], mc, mtc, mrefc)
        wl.update(fetch(l, "rest", o))
        last = l == n_layers - 1
        x1, pa, pb, ym, yat, ybt, mt = mixer_fwd(xs, parts, o, mod[l], w["lnw"][l], w["lnb"][l], w["sw"][l], w["sb"][l],
                                                 w["hnw"][l], wl["wa"][l], wl["wb"][l], wl["wo"][l], last)
        av, h2t = ffn_up_fwd(x1, mod[l], w["nw2"][l], wl["wup"][l], last)
        x2, ac, y, z = ffn_down_fwd(x1, av, mod[l], w["cw"][l], w["cb"][l], wl["wd"][l], last)
        saved.append((xs, parts, iv, o, ck, x1, av, ac, y, z, ht, h2t, pa, pb, ym, yat, ybt, mt))
        big.append(wl)
        xs = x2
    loss, dx, dfw = loss_fwd_bwd(xs, target, w["fw"])
    g = {k: [None] * n_layers for k in ("nw1", "nw2", "lnw", "lnb", "sw", "sb", "hnw", "cw", "cb")}
    g["fw"] = dfw
    dmod, dlb = [None] * n_layers, [None] * n_layers
    tok = 0.0
    for l in reversed(range(n_layers)):
        x0, parts, iv, o, ck, x1, av, ac, y, z, ht, h2t, pa, pb, ym, yat, ybt, mt = saved[l]
        wl = big[l]
        last = l == n_layers - 1
        dav, dac, dout, dg2 = ffn_down_bwd(dx, ac, av, y, mod[l] + tok, wl["wd"][l], last)
        dwd = weight_grad_rows(z, dout, "ffn_down_bwd_w")
        dav, g["cw"][l], g["cb"][l] = conv_bwd(dav, dac, av, w["cw"][l], last)
        dx1, g["nw2"][l], dmod2 = ffn_up_bwd_x(dx, x1, dav, mod[l], w["nw2"][l], wl["wup"][l], last)
        dwup = weight_grad(h2t, dav, FF_SLOT, "ffn_up_bwd_w")
        tok = publish(l, "ffn", {"wd": dwd, "wup": dwup})
        (dparts, do, dy, dpa, dpb, g["lnw"][l], g["lnb"][l], g["sw"][l], g["sb"][l], g["hnw"][l],
         dg1) = mixer_bwd(dx1, parts, o, pa, pb, ym, mod[l] + tok, w["lnw"][l], w["lnb"][l], w["sw"][l], w["sb"][l],
                          w["hnw"][l], wl["wa"][l], wl["wb"][l], wl["wo"][l], last)
        tok = publish(l, "mix", {"wa": weight_grad(yat, dpa, D, "mixer_bwd_wa"), "wb": weight_grad(ybt, dpb, D, "mixer_bwd_wb"),
                                 "wo": weight_grad(mt, dy, D, "mixer_bwd_wo")})
        if l == 0 and small_early:
            dmod[0] = jnp.concatenate([jnp.zeros((2, 2, 1, D), F32), dg1, dmod2, dg2], axis=1)
            tok = tok + small_early(loss[0, 0], g, dmod, dlb)
        dq, df, di, dlb_f = hgrn_bwd(0, parts, lb[l] + tok, mc, mtc, mrefc, ck, do)
        dparts, dlb_b = hgrn_bwd(1, parts, lb[l], mc, mtc, mrefc, ck, do, (dq, df, di), dparts)
        dlb[l] = jnp.concatenate([dlb_f, dlb_b], axis=0)
        tok = publish(l, "in", {"win": weight_grad(ht, dparts, IN_SLOT, "in_proj_bwd_w")})
        dx, g["nw1"][l], dmod1 = in_proj_bwd_x(dx1, x0, dparts, mod[l], w["nw1"][l], wl["win"][l], after=tok,
                                               latent_only=l == 0)
        dmod[l] = jnp.concatenate([dmod1, dg1, dmod2, dg2], axis=1)
    done = small_ready(loss[0, 0], g, dmod, dlb) if small_ready else 0.0
    for (l, part), grads in kept.items():
        for k, v in grads.items():
            g.setdefault(k, [None] * n_layers)[l] = v
    return loss[0, 0], dx, g, dmod, dlb, done


ROW = 1024
REPLICATED = ("norm1_w", "sgu_ln_w", "sgu_ln_b", "sgu_w", "sgu_b", "hgrn_lower_bounds", "hgrn_norm_w", "norm2_w",
              "ffn_conv_b", "final_norm_w")
WEIGHT_ORDER = ("c_ctx", "ada_w", "ada_b", "norm1_w", "w_in", "sgu_ln_w", "sgu_ln_b", "sgu_w", "sgu_b", "hgrn_lower_bounds",
                "hgrn_norm_w", "w_branch_a", "w_branch_b", "w_out", "norm2_w", "ffn_w_up", "ffn_conv_w", "ffn_conv_b",
                "ffn_w_down", "final_norm_w")


def _rows_of(n):
    return -(-n // (8 * ROW)) * 8


def _pack(arrs, total_rows=None):
    parts = []
    for a in arrs:
        flat = a.reshape(-1).astype(F32)
        rows = _rows_of(flat.shape[0])
        parts.append(jnp.pad(flat, (0, rows * ROW - flat.shape[0])).reshape(rows, ROW))
    have = sum(p.shape[0] for p in parts)
    if total_rows is not None and total_rows > have:
        parts.append(jnp.zeros((total_rows - have, ROW), F32))
    return jnp.concatenate(parts, axis=0)


def _unpack(packed, shapes):
    lead = packed.shape[:-2]
    out, r0 = [], 0
    for s in shapes:
        n = math.prod(s)
        rows = _rows_of(n)
        out.append(packed[..., r0:r0 + rows, :].reshape(lead + (rows * ROW,))[..., :n].reshape(lead + tuple(s)))
        r0 += rows
    return out


def kernel(x, c, ctx, c_ctx, ada_w, ada_b, norm1_w, w_in, sgu_ln_w, sgu_ln_b, sgu_w, sgu_b, hgrn_lower_bounds, hgrn_norm_w, w_branch_a, w_branch_b, w_out, norm2_w, ffn_w_up, ffn_conv_w, ffn_conv_b, ffn_w_down, final_norm_w, loss_target, m_c_ctx, m_ada_w, m_ada_b, m_norm1_w, m_w_in, m_sgu_ln_w, m_sgu_ln_b, m_sgu_w, m_sgu_b, m_hgrn_lower_bounds, m_hgrn_norm_w, m_w_branch_a, m_w_branch_b, m_w_out, m_norm2_w, m_ffn_w_up, m_ffn_conv_w, m_ffn_conv_b, m_ffn_w_down, m_final_norm_w, v_c_ctx, v_ada_w, v_ada_b, v_norm1_w, v_w_in, v_sgu_ln_w, v_sgu_ln_b, v_sgu_w, v_sgu_b, v_hgrn_lower_bounds, v_hgrn_norm_w, v_w_branch_a, v_w_branch_b, v_w_out, v_norm2_w, v_ffn_w_up, v_ffn_conv_w, v_ffn_conv_b, v_ffn_w_down, v_final_norm_w):
    wts = dict(c_ctx=c_ctx, ada_w=ada_w, ada_b=ada_b, norm1_w=norm1_w, w_in=w_in, sgu_ln_w=sgu_ln_w, sgu_ln_b=sgu_ln_b,
               sgu_w=sgu_w, sgu_b=sgu_b, hgrn_lower_bounds=hgrn_lower_bounds, hgrn_norm_w=hgrn_norm_w, w_branch_a=w_branch_a,
               w_branch_b=w_branch_b, w_out=w_out, norm2_w=norm2_w, ffn_w_up=ffn_w_up, ffn_conv_w=ffn_conv_w,
               ffn_conv_b=ffn_conv_b, ffn_w_down=ffn_w_down, final_norm_w=final_norm_w)
    mom1 = dict(c_ctx=m_c_ctx, ada_w=m_ada_w, ada_b=m_ada_b, norm1_w=m_norm1_w, w_in=m_w_in, sgu_ln_w=m_sgu_ln_w,
                sgu_ln_b=m_sgu_ln_b, sgu_w=m_sgu_w, sgu_b=m_sgu_b, hgrn_lower_bounds=m_hgrn_lower_bounds,
                hgrn_norm_w=m_hgrn_norm_w, w_branch_a=m_w_branch_a, w_branch_b=m_w_branch_b, w_out=m_w_out, norm2_w=m_norm2_w,
                ffn_w_up=m_ffn_w_up, ffn_conv_w=m_ffn_conv_w, ffn_conv_b=m_ffn_conv_b, ffn_w_down=m_ffn_w_down,
                final_norm_w=m_final_norm_w)
    mom2 = dict(c_ctx=v_c_ctx, ada_w=v_ada_w, ada_b=v_ada_b, norm1_w=v_norm1_w, w_in=v_w_in, sgu_ln_w=v_sgu_ln_w,
                sgu_ln_b=v_sgu_ln_b, sgu_w=v_sgu_w, sgu_b=v_sgu_b, hgrn_lower_bounds=v_hgrn_lower_bounds,
                hgrn_norm_w=v_hgrn_norm_w, w_branch_a=v_w_branch_a, w_branch_b=v_w_branch_b, w_out=v_w_out, norm2_w=v_norm2_w,
                ffn_w_up=v_ffn_w_up, ffn_conv_w=v_ffn_conv_w, ffn_conv_b=v_ffn_conv_b, ffn_w_down=v_ffn_w_down,
                final_norm_w=v_final_norm_w)
    n_layers = w_in.shape[0]
    layers = range(n_layers)
    me = 4 * lax.axis_index("x") + 2 * lax.axis_index("y") + lax.axis_index("c")
    ada_cols = ada_w.shape[-1]

    big = ("w_in", "ffn_w_up", "w_branch_a", "w_branch_b", "w_out", "ffn_w_down")
    short = {"w_in": "win", "ffn_w_up": "wup", "w_branch_a": "wa", "w_branch_b": "wb", "w_out": "wo", "ffn_w_down": "wd"}
    me1 = me.reshape(1).astype(jnp.int32)
    groups = [[("w_in", 0)], [(k, 0) for k in big[1:]], [("w_in", 1)], [(k, 1) for k in big[1:]]]
    in_flight, started = [], 0.0

    def own_slots(n, after):
        return [place_own(wts[k], jax.ShapeDtypeStruct((N_DEV,) + wts[k].shape[1:], BF16), me1, None, False,
                          f"gather_own_{short[k]}_{l}", src_layer=l, after=after) for k, l in groups[n]]

    def start_group(n, lands, after):
        in_flight.append(exchange_start([], lands, None, False, f"gather_weights_start_{n}", after=after,
                                        peers=SAME_CORE_AND_SIBLING if n == 0 else ALL_PEERS))
        return in_flight[-1][-1]

    (c_all,) = all_gather([c], "gather_c")
    c_all = c_all.reshape(N_DEV, D)
    token = start_group(0, own_slots(0, c_all), c_all)
    later = [own_slots(n, token) for n in range(1, len(groups))]
    cctx8 = jnp.broadcast_to(c_ctx[None, :], (N_DEV, D))
    ada_b_cols = lax.dynamic_slice_in_dim(ada_b, me * ada_cols, ada_cols, axis=1)[:, None, :]
    mod_cols = ada_fwd(c_all, cctx8, ada_w, ada_b_cols)
    xs = jnp.concatenate([ctx[0], x[0]], axis=0)
    lb1 = lower_bounds(hgrn_lower_bounds)
    mod_all, conv_all = all_gather([mod_cols, ffn_conv_w.reshape(n_layers, 9, -1)], "gather_mod_conv",
                                   after=[token, xs, lb1] + [a for lands in later for a in lands])
    conv_full = [conv_all[:, l].transpose(1, 0, 2).reshape(9, N_FFK, FF_SLOT).transpose(1, 0, 2) for l in layers]
    for n in range(1, len(groups)):
        token = start_group(n, later[n - 1], mod_all if n == 1 else token)
    for started_group in in_flight:
        started = started + started_group[-1][0, 0]

    def as_used(k, a):
        return a if k in ("w_in", "ffn_w_up") else a.reshape(N_FFK, FF_SLOT, D) if k == "ffn_w_down" else a.reshape(D, D)

    arrived = {}

    def fetch(l, part, after):
        n = {(0, "in"): 0, (0, "rest"): 1, (1, "in"): 2, (1, "rest"): 3}.get((l, part))
        if n is not None:
            send, recv, _, lands, _ = in_flight[n]
            first = n == 0
            got = exchange_wait(send, recv, [], lands, None, False, after, f"gather_weights_wait_{n}",
                                peers=SAME_CORE_AND_SIBLING if first else ALL_PEERS)
            if first:
                send, recv, _, lands, _ = exchange_start([], got, None, False, "gather_weights_pass_on", peers=OTHER_CHIPS,
                                                         forward=True)
                got = exchange_wait(send, recv, [], lands, None, False, after, "gather_weights_passed_on", peers=OTHER_CHIPS)
            for (k, ll), a in zip(groups[n], got):
                arrived.setdefault(short[k], [None] * n_layers)[ll] = as_used(k, a)
        return arrived

    mod_x = lax.dynamic_index_in_dim(mod_all[:, :, 0], me, axis=2, keepdims=False)
    mod_c = mod_all[:, :, 1, 0]
    mod = [jnp.stack([mod_c[:, l].reshape(6, 1, D), mod_x[:, l].reshape(6, 1, D)]) for l in layers]
    mod[0] = mod[0] + started

    lb = [jnp.zeros((2, 1, D), F32), lb1.reshape(2, 1, D)]

    w = {
        "nw1": [norm1_w[l][None] for l in layers], "nw2": [norm2_w[l][None] for l in layers],
        "lnw": [sgu_ln_w[l][None] for l in layers], "lnb": [sgu_ln_b[l][None] for l in layers],
        "sw": [sgu_w[l] for l in layers], "sb": [sgu_b[l][:, :, None] for l in layers],
        "hnw": [hgrn_norm_w[l][None] for l in layers], "cw": conv_full,
        "cb": [ffn_conv_b[l].reshape(N_FFK, 1, FF_SLOT) for l in layers], "fw": final_norm_w[None],
    }
    long = {v: k for k, v in short.items()}
    landing, sent = {}, []

    def publish(l, part, grads):
        keys = [long[k] for k in grads]
        slots = [a.reshape((N_DEV, -1, a.shape[-1])) for a in grads.values()]
        zones = [place_own(s, landing.get(k, jax.ShapeDtypeStruct((N_DEV, n_layers) + s.shape[1:], s.dtype)), me1, l, True,
                           f"scatter_own_{short[k]}_{l}") for k, s in zip(keys, slots)]
        send, recv, srcs, zones, token = exchange_start(slots, zones, l, True, f"scatter_grads_start_{part}_{l}")
        landing.update(zip(keys, zones))
        sent.append((keys, l, part, send, recv, srcs, token))
        return token[0, 0]

    out = {}
    flat2 = lambda a: a.reshape(-1, a.shape[-1])

    def finish(part, after):
        done = []
        for keys, l, p, send, recv, srcs, _ in sent:
            if p == part:
                zones = exchange_wait(send, recv, srcs, [landing[k] for k in keys], l, True, after,
                                      f"scatter_grads_wait_{part}_{l}")
                landing.update(zip(keys, zones))
                done = keys
        for k in done:
            r = landing[k]
            res = adamw(flat2(wts[k]), flat2(mom1[k]), flat2(mom2[k]), r.reshape(N_DEV, -1, r.shape[-1]), "adamw_" + k)
            out[k] = tuple(a.reshape(wts[k].shape) for a in res)

    rep_rows = -(-sum(_rows_of(wts[k].size) for k in REPLICATED) // 64) * 64
    conv_rows = _rows_of(n_layers * 9 * D_FF)
    dmod_rows = _rows_of(n_layers * 6 * D)
    early = {}

    def small_early(loss_part, g, dmod, dlb):
        d_hlb = lower_bounds_bwd(hgrn_lower_bounds, dlb[1].reshape(1, 2 * D))
        st = lambda k: jnp.stack([jnp.zeros((1, D), F32) if a is None else a for a in g[k]])
        rep_grads = {"norm1_w": st("nw1"), "sgu_ln_w": st("lnw"), "sgu_ln_b": st("lnb"), "sgu_w": st("sw"), "sgu_b": st("sb"),
                     "hgrn_lower_bounds": d_hlb, "hgrn_norm_w": st("hnw"), "norm2_w": st("nw2"), "ffn_conv_b": st("cb"),
                     "final_norm_w": g["fw"]}
        d_conv = jnp.stack([g["cw"][l].transpose(1, 0, 2).reshape(9, D_FF) for l in layers])
        dmod_x = jnp.stack([dmod[l][1].reshape(6 * D) for l in layers])
        dmod_c = jnp.stack([dmod[l][0].reshape(6 * D) for l in layers])
        small = jnp.concatenate([_pack([rep_grads[k] for k in REPLICATED], rep_rows),
                                 _pack([d_conv, dmod_x, dmod_c, loss_part.reshape(1)])], axis=0)
        zone = place_own(small, jax.ShapeDtypeStruct((N_DEV,) + small.shape, F32), me1, None, False, "gather_small_own")
        early["send"], early["recv"], _, early["zones"], token = exchange_start([], [zone], None, False, "gather_small_start")
        return token[0, 0]

    def small_ready(loss_part, g, dmod, dlb):
        late = _pack([g["nw1"][0], dmod[0][1, 0:2], dmod[0][0, 0:2]])
        for part in ("ffn", "mix"):
            finish(part, late)
        (late_all,) = all_gather([late], "gather_small_late", after=[out[k][0] for k in big[1:]])
        (small_all,) = exchange_wait(early["send"], early["recv"], [], early["zones"], None, False, late_all,
                                     "gather_small_wait")
        at_x = rep_rows + conv_rows
        small_all = small_all.at[:, 0:1].set(late_all[:, 0:1])
        small_all = small_all.at[:, at_x:at_x + 2].set(late_all[:, 8:10])
        small_all = small_all.at[:, at_x + dmod_rows:at_x + dmod_rows + 2].set(late_all[:, 16:18])
        d_conv_shape, dmod_shape = (n_layers, 9, D_FF), (n_layers, 6 * D)
        conv_g, dmx_all, dmc_all, loss_all = _unpack(small_all[:, rep_rows:], [d_conv_shape, dmod_shape, dmod_shape, (1,)])
        out["loss"] = functools.reduce(lambda a, b: a + b, [loss_all[k, 0] for k in range(N_DEV)])

        rep = adamw(_pack([wts[k] for k in REPLICATED], rep_rows), _pack([mom1[k] for k in REPLICATED], rep_rows),
                    _pack([mom2[k] for k in REPLICATED], rep_rows), small_all, "adamw_replicated")
        rep = [_unpack(r, [wts[k].shape for k in REPLICATED]) for r in rep]
        for n, k in enumerate(REPLICATED):
            out[k] = tuple(r[n] for r in rep)

        conv_mine = lax.dynamic_index_in_dim(conv_g.reshape(N_DEV, n_layers, 9, N_DEV, -1), me, axis=3, keepdims=False)
        res = adamw(flat2(ffn_conv_w), flat2(m_ffn_conv_w), flat2(v_ffn_conv_w),
                    conv_mine.reshape(N_DEV, -1, conv_mine.shape[-1]), "adamw_conv_w")
        out["ffn_conv_w"] = tuple(r.reshape(ffn_conv_w.shape) for r in res)

        out["ada_b"] = tuple(adamw(ada_b, m_ada_b, v_ada_b, jnp.concatenate([dmx_all, dmc_all], axis=0), "adamw_ada_b"))

        cols_of = lambda a: lax.dynamic_slice_in_dim(a, me * ada_cols, ada_cols, axis=2).transpose(1, 0, 2)
        d_ada_w, d_cctx = ada_bwd(c_all, cctx8, ada_w, ada_b_cols, cols_of(dmx_all), cols_of(dmc_all))
        res = adamw(flat2(ada_w), flat2(m_ada_w), flat2(v_ada_w), flat2(d_ada_w)[None], "adamw_ada_w")
        out["ada_w"] = tuple(r.reshape(ada_w.shape) for r in res)
        (d_cctx_all,) = all_gather([d_cctx], "gather_c_ctx_grad")
        res = adamw(c_ctx[None], m_c_ctx[None], v_c_ctx[None], d_cctx_all, "adamw_c_ctx")
        out["c_ctx"] = tuple(r[0] for r in res)
        return d_cctx_all

    _, grad_x, _, _, _, small_done = local_step(x[0], ctx[0], loss_target[0], mod, lb, w, fetch, publish, small_ready,
                                                small_early)
    loss = out["loss"]

    finish("in", small_done)
    return (loss, grad_x[None]) + tuple(out[k][n] for n in range(4) for k in WEIGHT_ORDER)
```

```python
import functools
import math

import jax
import jax.numpy as jnp
from jax import lax
from jax.experimental import pallas as pl
from jax.experimental.pallas import tpu as pltpu

F32 = jnp.float32
BF16 = jnp.bfloat16

N_DEV = 8
AXES = ("x", "y", "c")
D = 1024
CTX = 256
TM = 256
CH = 64
SGU_CH = 128
HEADS = 8
HD = 128
GRID_W = 64
D_IN = 9 * D
IN_SLOT = D_IN // N_DEV
D_FF = 2816
FF_SLOT = 2 * D_FF // N_DEV
N_FFK = D_FF // FF_SLOT
RMS_EPS = 1e-6
LN_EPS = 1e-5
ADAM_LR, ADAM_B1, ADAM_B2, ADAM_EPS, ADAM_WD, ADAM_STEP = 0.001, 0.9, 0.999, 1e-08, 0.01, 10
VMEM_LIMIT_V7X = 56 * 2 ** 20
ELEMENTWISE_VMEM = 24 * 2 ** 20
COPY_BLOCK_BYTES = 2 ** 21
GRAD_WIRE = jnp.bfloat16

VMEM_WHOLE = pl.BlockSpec(memory_space=pltpu.VMEM)
ANY = pl.BlockSpec(memory_space=pl.ANY)


def _cp(n_axes):
    return pltpu.CompilerParams(dimension_semantics=("arbitrary",) * n_axes, vmem_limit_bytes=VMEM_LIMIT_V7X)


def _dot(a, b, dims):
    return lax.dot_general(a.astype(BF16), b.astype(BF16), (dims, ((), ())), preferred_element_type=F32)


@jax.custom_vjp
def mm(a, b):
    return _dot(a, b, ((1,), (0,)))


mm.defvjp(lambda a, b: (mm(a, b), (a, b)),
          lambda r, g: (_dot(g, r[1], ((1,), (1,))).astype(r[0].dtype), _dot(r[0], g, ((0,), (0,))).astype(r[1].dtype)))


@jax.custom_vjp
def mm_nt(a, b):
    return _dot(a, b, ((1,), (1,)))


mm_nt.defvjp(lambda a, b: (mm_nt(a, b), (a, b)),
             lambda r, g: (_dot(g, r[1], ((1,), (0,))).astype(r[0].dtype), _dot(g, r[0], ((0,), (0,))).astype(r[1].dtype)))


@jax.custom_vjp
def mm_tn(a, b):
    return _dot(a, b, ((0,), (0,)))


mm_tn.defvjp(lambda a, b: (mm_tn(a, b), (a, b)),
             lambda r, g: (_dot(r[1], g, ((1,), (1,))).astype(r[0].dtype), _dot(r[0], g, ((1,), (0,))).astype(r[1].dtype)))


def _tri_dot(m, g):
    hi = g.astype(BF16)
    low = (g - hi.astype(F32)).astype(BF16)
    n = g.shape[1]
    out = jnp.dot(m.astype(BF16), jnp.concatenate([hi, low], axis=1), preferred_element_type=F32)
    return out[:, :n] + out[:, n:]


@jax.custom_vjp
def _cum(m, mt, g):
    return _tri_dot(m, g)


_cum.defvjp(lambda m, mt, g: (_cum(m, mt, g), (m, mt)),
            lambda r, d: (jnp.zeros_like(r[0]), jnp.zeros_like(r[1]), _tri_dot(r[1], d)))


def _silu(x):
    return x * jax.nn.sigmoid(x)


def _gelu(x):
    return 0.5 * x * (1.0 + jnp.tanh(math.sqrt(2.0 / math.pi) * (x + 0.044715 * (x * x * x))))


def _rms(x, w):
    return x * lax.rsqrt(jnp.mean(x * x, axis=-1, keepdims=True) + RMS_EPS) * w


def _norm_mod(x, w, shift, scale):
    return _rms(x, w) * (1.0 + scale) + shift


def _hsl(h):
    return slice(h * HD, (h + 1) * HD)


def _hgrn_chunk(st, qz, fz, iv, lb, m, mt, mref):
    hs = range(HEADS)
    keep = [1.0 - lb[h] for h in hs]
    sg = [jax.nn.sigmoid(fz[h]) for h in hs]
    g = [jnp.log(lb[h] + keep[h] * sg[h]) for h in hs]
    k = [keep[h] * (1.0 - sg[h]) for h in hs]
    q = [_silu(qz[h]) for h in hs]
    b = [_cum(m, mt, g[h]) for h in hs]
    ref = [jnp.sum(mref * g[h], axis=0, keepdims=True) for h in hs]
    last = [jnp.sum(g[h], axis=0, keepdims=True) for h in hs]
    qa = [q[h] * jnp.exp(b[h] - ref[h]) for h in hs]
    ka = [k[h] * jnp.exp(ref[h] - b[h]) for h in hs]
    scores = [jnp.where(m > 0.5, mm_nt(qa[h], ka[h]), 0.0) for h in hs]
    inter = [mm_nt(qa[h] * jnp.exp(ref[h]), st[h]) for h in hs]
    kv = [mm_tn(iv[h], ka[h] * jnp.exp(last[h] - ref[h])) for h in hs]
    outs = [mm(scores[h], iv[h]) + inter[h] for h in hs]
    news = [jnp.exp(last[h]) * st[h] + kv[h] for h in hs]
    return outs, news


def _sgu_fn(ub, vb, lnw, lnb, sw, sb):
    gv = [_gelu(v) for v in vb]
    mu = sum(jnp.sum(t, axis=-1, keepdims=True) for t in gv) / D
    var = sum(jnp.sum((t - mu) * (t - mu), axis=-1, keepdims=True) for t in gv) / D
    inv = lax.rsqrt(var + LN_EPS)
    cols = []
    for g in range(HEADS):
        vn = (gv[g] - mu) * inv * lnw[g] + lnb[g]
        cols.append(_gelu(ub[g]) * (mm(sw[g], vn) + sb[g]))
    return jnp.concatenate(cols, axis=1)


def _readout_fn(ob, og, hnw):
    r = [o * lax.rsqrt(jnp.mean(o * o, axis=-1, keepdims=True) + RMS_EPS) * hnw for o in ob]
    return jnp.concatenate(r, axis=1) * _silu(og)


def _glu_fn(ac, v):
    return _gelu(ac) * v


def _stream_row(tm):
    n_ctx = CTX // tm
    return lambda i: (jnp.where(i < n_ctx, 0, 1), 0, 0, 0)


def in_proj_fwd(x, mod, nw, wg):
    t = x.shape[0]

    def body(x_ref, mod_ref, nw_ref, w_ref, out_ref, ht_ref, iv_ref):
        h32 = _norm_mod(x_ref[...], nw_ref[...], mod_ref[0, 0], mod_ref[0, 1])
        ht_ref[...] = h32.T.astype(BF16)
        h = h32.astype(BF16)
        for j in range(N_DEV):
            out_ref[:, j * IN_SLOT:(j + 1) * IN_SLOT] = jnp.dot(h, w_ref[j], preferred_element_type=F32)
        iv_ref[...] = out_ref[:, 3 * D:4 * D].astype(BF16)

    return pl.pallas_call(
        body, name="in_proj_fwd", grid=(t // TM,),
        in_specs=[pl.BlockSpec((TM, D), lambda i: (i, 0)), pl.BlockSpec((1, 6, 1, D), _stream_row(TM)),
                  pl.BlockSpec((1, D), lambda i: (0, 0)), VMEM_WHOLE],
        out_specs=[pl.BlockSpec((TM, D_IN), lambda i: (i, 0)), pl.BlockSpec((D, TM), lambda i: (0, i)),
                   pl.BlockSpec((TM, D), lambda i: (i, 0))],
        out_shape=[jax.ShapeDtypeStruct((t, D_IN), F32), jax.ShapeDtypeStruct((D, t), BF16), jax.ShapeDtypeStruct((t, D), BF16)],
        compiler_params=_cp(1))(x, mod, nw, wg)


SCAN_STEP = 4
SCAN_ROWS = SCAN_STEP * CH


def _scan_block(nb):
    ncb = CTX // SCAN_ROWS

    def block(d, s):
        bwd = jnp.where(s < ncb, ncb - 1 - s, nb + ncb - 1 - s)
        return jnp.where(d == 0, s, bwd)
    return block


def hgrn_fwd(parts, iv, lb, mc, mtc, mrefc):
    t = parts.shape[0]
    nb = t // SCAN_ROWS
    block = _scan_block(nb)

    def body(q_ref, f_ref, i_ref, lb_ref, m_ref, mt_ref, mr_ref, o_ref, ck_ref, st):
        d = pl.program_id(0)

        @pl.when(pl.program_id(1) == 0)
        def _():
            st[...] = jnp.zeros_like(st)

        for c in range(SCAN_STEP):
            rows = pl.ds(pl.multiple_of(jnp.where(d == 0, c * CH, (SCAN_STEP - 1 - c) * CH), CH), CH)
            ck_ref[0, c] = st[...].astype(BF16)
            outs, news = _hgrn_chunk([st[h] for h in range(HEADS)], [q_ref[rows, _hsl(h)] for h in range(HEADS)],
                                     [f_ref[rows, _hsl(h)] for h in range(HEADS)], [i_ref[rows, _hsl(h)] for h in range(HEADS)],
                                     [lb_ref[0, :, _hsl(h)] for h in range(HEADS)], m_ref[0], mt_ref[0], mr_ref[0])
            for h in range(HEADS):
                o_ref[0, rows, _hsl(h)] = outs[h].astype(BF16)
                st[h] = news[h]

    const = lambda d, s: (d, 0, 0)
    at = lambda k: pl.BlockSpec((SCAN_ROWS, D), lambda d, s: (block(d, s), k(d)))
    return pl.pallas_call(
        body, name="hgrn_fwd", grid=(2, nb),
        in_specs=[at(lambda d: 0), at(lambda d: 1 + d), at(lambda d: 0), pl.BlockSpec((1, 1, D), const),
                  pl.BlockSpec((1, CH, CH), const), pl.BlockSpec((1, CH, CH), const), pl.BlockSpec((1, CH, 1), const)],
        out_specs=[pl.BlockSpec((1, SCAN_ROWS, D), lambda d, s: (d, block(d, s), 0)),
                   pl.BlockSpec((1, SCAN_STEP, HEADS, HD, HD), lambda d, s: (d, s, 0, 0, 0))],
        out_shape=[jax.ShapeDtypeStruct((2, t, D), BF16), jax.ShapeDtypeStruct((2, nb * SCAN_STEP, HEADS, HD, HD), BF16)],
        scratch_shapes=[pltpu.VMEM((HEADS, HD, HD), F32)], compiler_params=_cp(2))(parts, parts, iv, lb, mc, mtc, mrefc)


def _mixer_tile(rows, u_ref, v_ref, og_ref, o_ref, lnw_ref, lnb_ref, sw_ref, sb_ref, hnw_ref):
    n = (rows.stop - rows.start) // SGU_CH
    yas, vjps = [], []
    for c in range(n):
        r = slice(rows.start + c * SGU_CH, rows.start + (c + 1) * SGU_CH)
        ya, vjp_a = jax.vjp(_sgu_fn, [u_ref[r, _hsl(g)] for g in range(HEADS)], [v_ref[r, _hsl(g)] for g in range(HEADS)],
                            [lnw_ref[:, _hsl(g)] for g in range(HEADS)], [lnb_ref[:, _hsl(g)] for g in range(HEADS)],
                            [sw_ref[g] for g in range(HEADS)], [sb_ref[g] for g in range(HEADS)])
        yas.append(ya)
        vjps.append(vjp_a)
    yb, vjp_b = jax.vjp(_readout_fn, [o_ref[0, rows, _hsl(h)].astype(F32) + o_ref[1, rows, _hsl(h)].astype(F32)
                                      for h in range(HEADS)],
                        og_ref[rows, :], hnw_ref[...])
    return (yas[0] if n == 1 else jnp.concatenate(yas, axis=0)), yb, vjps, vjp_b


def _part_specs(tm, first, n):
    return [pl.BlockSpec((tm, D), functools.partial(lambda k, i: (i, k), first + k)) for k in range(n)]


def _unless_ctx(skip_ctx, is_ctx, zero_refs, work):
    if not skip_ctx:
        return work()

    @pl.when(is_ctx)
    def _():
        for r in zero_refs:
            r[...] = jnp.zeros_like(r)

    pl.when(jnp.logical_not(is_ctx))(work)


def mixer_fwd(x, parts, o, mod, lnw, lnb, sw, sb, hnw, wa, wb, wo, skip_ctx):
    t = x.shape[0]

    def body(x_ref, u_ref, v_ref, og_ref, ga_ref, gb_ref, o_ref, mod_ref, lnw_ref, lnb_ref, sw_ref, sb_ref, hnw_ref,
             wa_ref, wb_ref, wo_ref, out_ref, pa_ref, pb_ref, y_ref, yat_ref, ybt_ref, mt_ref):
        def work():
            ya, yb, _, _ = _mixer_tile(slice(0, TM), u_ref, v_ref, og_ref, o_ref, lnw_ref, lnb_ref, sw_ref, sb_ref, hnw_ref)
            pa, pb = mm(ya, wa_ref[...]), mm(yb, wb_ref[...])
            merged = jax.nn.sigmoid(ga_ref[...]) * pa + jax.nn.sigmoid(gb_ref[...]) * pb
            y = mm(merged, wo_ref[...])
            out_ref[...] = x_ref[...] + mod_ref[0, 2] * y
            pa_ref[...], pb_ref[...], y_ref[...] = pa.astype(BF16), pb.astype(BF16), y.astype(BF16)
            yat_ref[...], ybt_ref[...], mt_ref[...] = ya.T.astype(BF16), yb.T.astype(BF16), merged.T.astype(BF16)

        _unless_ctx(skip_ctx, pl.program_id(0) == 0, (out_ref, pa_ref, pb_ref, y_ref, yat_ref, ybt_ref, mt_ref), work)

    vec = lambda n: pl.BlockSpec((1, n), lambda i: (0, 0))
    tile = pl.BlockSpec((TM, D), lambda i: (i, 0))
    tile_t = pl.BlockSpec((D, TM), lambda i: (0, i))
    return pl.pallas_call(
        body, name="mixer_fwd", grid=(t // TM,),
        in_specs=[tile] + _part_specs(TM, 4, 5)
        + [pl.BlockSpec((2, TM, D), lambda i: (0, i, 0)), pl.BlockSpec((1, 6, 1, D), _stream_row(TM)), vec(D), vec(D),
           VMEM_WHOLE, VMEM_WHOLE, vec(HD), VMEM_WHOLE, VMEM_WHOLE, VMEM_WHOLE],
        out_specs=[tile] * 4 + [tile_t] * 3,
        out_shape=[jax.ShapeDtypeStruct((t, D), F32)] + [jax.ShapeDtypeStruct((t, D), BF16)] * 3
        + [jax.ShapeDtypeStruct((D, t), BF16)] * 3, compiler_params=_cp(1),
    )(x, parts, parts, parts, parts, parts, o, mod, lnw, lnb, sw, sb, hnw, wa, wb, wo)


def ffn_up_fwd(x, mod, nw, wg, skip_ctx):
    t = x.shape[0]

    def body(x_ref, mod_ref, nw_ref, w_ref, out_ref, ht_ref):
        def work():
            h32 = _norm_mod(x_ref[...], nw_ref[...], mod_ref[0, 3], mod_ref[0, 4])
            ht_ref[...] = h32.T.astype(BF16)
            h = h32.astype(BF16)
            for j in range(N_DEV):
                out_ref[j] = jnp.dot(h, w_ref[j], preferred_element_type=F32)

        _unless_ctx(skip_ctx, pl.program_id(0) == 0, (out_ref, ht_ref), work)

    return pl.pallas_call(
        body, name="ffn_up_fwd", grid=(t // TM,),
        in_specs=[pl.BlockSpec((TM, D), lambda i: (i, 0)), pl.BlockSpec((1, 6, 1, D), _stream_row(TM)),
                  pl.BlockSpec((1, D), lambda i: (0, 0)), VMEM_WHOLE],
        out_specs=[pl.BlockSpec((N_DEV, TM, FF_SLOT), lambda i: (0, i, 0)), pl.BlockSpec((D, TM), lambda i: (0, i))],
        out_shape=[jax.ShapeDtypeStruct((N_DEV, t, FF_SLOT), F32), jax.ShapeDtypeStruct((D, t), BF16)],
        compiler_params=_cp(1))(x, mod, nw, wg)


def _halo_specs(nt):
    per = TM // GRID_W
    last = nt * per - 1
    return [pl.BlockSpec((N_FFK, GRID_W, FF_SLOT), lambda i: (0, jnp.maximum(i * per - 1, 0), 0)),
            pl.BlockSpec((N_FFK, TM, FF_SLOT), lambda i: (0, i, 0)),
            pl.BlockSpec((N_FFK, GRID_W, FF_SLOT), lambda i: (0, jnp.minimum(i * per + per, last), 0))]


def _with_halo(prev_ref, main_ref, next_ref, k, i, nt):
    prev = jnp.where(i >= 2, prev_ref[k], 0.0)
    nxt = jnp.where((i >= 1) & (i <= nt - 2), next_ref[k], 0.0)
    return jnp.concatenate([prev, main_ref[k], nxt], axis=0)


def _tap_valid(dc, i, n_rows, offset):
    r = lax.broadcasted_iota(jnp.int32, (n_rows, 1), 0) - offset
    col = jnp.bitwise_and(r, GRID_W - 1)
    pos = jnp.where(i == 0, r, col) + dc
    return (pos >= 0) & (pos < jnp.where(i == 0, TM, GRID_W))


def _row_weight(cw_ref, k, dr, dc, i):
    w = cw_ref[k, 3 * (dr + 1) + dc + 1:3 * (dr + 1) + dc + 2, :]
    return w if dr == 0 else jnp.where(i == 0, 0.0, w)


def ffn_down_fwd(x, av, mod, cw, cb, wd, skip_ctx):
    t = x.shape[0]
    nt = t // TM

    def body(x_ref, ap_ref, am_ref, an_ref, v_ref, mod_ref, cw_ref, cb_ref, wd_ref, out_ref, ac_ref, y_ref, z_ref):
        i = pl.program_id(0)

        def work():
            y = None
            for k in range(N_FFK):
                a_ext = _with_halo(ap_ref, am_ref, an_ref, k, i, nt)
                conv = jnp.zeros((TM, FF_SLOT), F32) + cb_ref[k]
                for dc in (-1, 0, 1):
                    col = functools.reduce(lambda p, q: p + q, [a_ext[GRID_W + GRID_W * dr:GRID_W + GRID_W * dr + TM]
                                                                * _row_weight(cw_ref, k, dr, dc, i) for dr in (-1, 0, 1)])
                    conv = conv + (col if dc == 0 else
                                   jnp.where(_tap_valid(dc, i, TM, 0), pltpu.roll(col, (-dc) % TM, 0), 0.0))
                ac_ref[k] = conv.astype(BF16)
                z = _glu_fn(conv, v_ref[k]).astype(BF16)
                z_ref[k] = z
                part = mm(z, wd_ref[k])
                y = part if y is None else y + part
            y_ref[...] = y
            out_ref[...] = x_ref[...] + mod_ref[0, 5] * y

        _unless_ctx(skip_ctx, i == 0, (out_ref, ac_ref, y_ref, z_ref), work)

    tile = pl.BlockSpec((TM, D), lambda i: (i, 0))
    half = lambda first: pl.BlockSpec((N_FFK, TM, FF_SLOT), lambda i: (first, i, 0))
    return pl.pallas_call(
        body, name="ffn_down_fwd", grid=(nt,),
        in_specs=[tile] + _halo_specs(nt) + [half(1), pl.BlockSpec((1, 6, 1, D), _stream_row(TM)), VMEM_WHOLE, VMEM_WHOLE,
                                             VMEM_WHOLE],
        out_specs=[tile, half(0), tile, half(0)],
        out_shape=[jax.ShapeDtypeStruct((t, D), F32), jax.ShapeDtypeStruct((N_FFK, t, FF_SLOT), BF16),
                   jax.ShapeDtypeStruct((t, D), F32), jax.ShapeDtypeStruct((N_FFK, t, FF_SLOT), BF16)],
        compiler_params=_cp(1))(x, av, av, av, av, mod, cw, cb, wd)


def loss_fwd_bwd(x, target, fw):
    t = x.shape[0]

    def body(x_ref, t_ref, w_ref, loss_ref, dx_ref, dw_ref):
        i = pl.program_id(0)

        @pl.when(i == 0)
        def _():
            loss_ref[...] = jnp.zeros_like(loss_ref)
            dw_ref[...] = jnp.zeros_like(dw_ref)
            dx_ref[...] = jnp.zeros_like(dx_ref)

        @pl.when(i > 0)
        def _():
            y, vjp = jax.vjp(_rms, x_ref[...], w_ref[...])
            err = y - t_ref[...]
            loss_ref[...] += 0.5 * jnp.sum(jnp.sum(err * err, axis=-1, keepdims=True) / D)
            dx, dw = vjp(err / D)
            dx_ref[...] = dx
            dw_ref[...] += dw

    return pl.pallas_call(
        body, name="loss_fwd_bwd", grid=(t // TM,),
        in_specs=[pl.BlockSpec((TM, D), lambda i: (i, 0)), pl.BlockSpec((TM, D), lambda i: (jnp.maximum(i - 1, 0), 0)),
                  pl.BlockSpec((1, D), lambda i: (0, 0))],
        out_specs=[pl.BlockSpec((8, 128), lambda i: (0, 0)), pl.BlockSpec((TM, D), lambda i: (i, 0)),
                   pl.BlockSpec((1, D), lambda i: (0, 0))],
        out_shape=[jax.ShapeDtypeStruct((8, 128), F32), jax.ShapeDtypeStruct((t, D), F32), jax.ShapeDtypeStruct((1, D), F32)],
        compiler_params=_cp(1))(x, target, fw)


def _stream_add(ref, k, is_ctx, val):
    ref[0, k] += jnp.where(is_ctx, val, 0.0)
    ref[1, k] += jnp.where(is_ctx, 0.0, val)


def ffn_down_bwd(dx, ac, av, y, mod, wd, skip_ctx):
    t = dx.shape[0]
    nt = t // TM

    def body(dx_ref, ac_ref, v_ref, y_ref, mod_ref, wd_ref, dav_ref, dac_ref, dout_ref, dg_ref):
        i = pl.program_id(0)

        @pl.when(i == 0)
        def _():
            dg_ref[...] = jnp.zeros_like(dg_ref)

        def work():
            _stream_add(dg_ref, 0, i == 0, jnp.sum(dx_ref[...] * y_ref[...], axis=0, keepdims=True))
            dout = (mod_ref[0, 5] * dx_ref[...]).astype(BF16)
            dout_ref[...] = dout
            for k in range(N_FFK):
                _, vjp = jax.vjp(_glu_fn, ac_ref[k].astype(F32), v_ref[k])
                dac, dv = vjp(mm_nt(dout, wd_ref[k]))
                dac_ref[k] = dac
                dav_ref[k] = dv.astype(BF16)

        _unless_ctx(skip_ctx, i == 0, (dav_ref, dac_ref, dout_ref), work)

    tile = pl.BlockSpec((TM, D), lambda i: (i, 0))
    half = lambda first: pl.BlockSpec((N_FFK, TM, FF_SLOT), lambda i: (first, i, 0))
    return pl.pallas_call(
        body, name="ffn_down_bwd", grid=(nt,),
        in_specs=[tile, half(0), half(1), tile, pl.BlockSpec((1, 6, 1, D), _stream_row(TM)), VMEM_WHOLE],
        out_specs=[half(1), half(0), tile, pl.BlockSpec((2, 1, 1, D), lambda i: (0, 0, 0, 0))],
        out_shape=[jax.ShapeDtypeStruct((N_DEV, t, FF_SLOT), BF16), jax.ShapeDtypeStruct((N_FFK, t, FF_SLOT), F32),
                   jax.ShapeDtypeStruct((t, D), BF16), jax.ShapeDtypeStruct((2, 1, 1, D), F32)],
        compiler_params=_cp(1))(dx, ac, av, y, mod, wd)


def conv_bwd(dav, dac, av, cw, skip_ctx):
    t = dac.shape[1]
    nt = t // TM

    def body(dav_in, gp_ref, gm_ref, gn_ref, ap_ref, am_ref, an_ref, cw_ref, dav_ref, dcw_ref, dcb_ref):
        i = pl.program_id(0)

        @pl.when(i == 0)
        def _():
            dcw_ref[...] = jnp.zeros_like(dcw_ref)
            dcb_ref[...] = jnp.zeros_like(dcb_ref)

        def work():
            for k in range(N_FFK):
                g_ext = _with_halo(gp_ref, gm_ref, gn_ref, k, i, nt)
                a_ext = _with_halo(ap_ref, am_ref, an_ref, k, i, nt)
                g_main = gm_ref[k]
                dcb_ref[k] += jnp.sum(g_main, axis=0, keepdims=True)
                da = jnp.zeros((TM, FF_SLOT), F32)
                for dc in (-1, 0, 1):
                    valid = _tap_valid(dc, i, TM, 0)
                    q = functools.reduce(lambda p, r: p + r, [g_ext[GRID_W - GRID_W * dr:GRID_W - GRID_W * dr + TM]
                                                              * _row_weight(cw_ref, k, dr, dc, i) for dr in (-1, 0, 1)])
                    da = da + (q if dc == 0 else pltpu.roll(jnp.where(valid, q, 0.0), dc % TM, 0))
                    g_shift = g_main if dc == 0 else pltpu.roll(jnp.where(valid, g_main, 0.0), dc % TM, 0)
                    for dr in (-1, 0, 1):
                        lo = GRID_W + GRID_W * dr
                        tap = 3 * (dr + 1) + dc + 1
                        dw = jnp.sum(g_shift * a_ext[lo:lo + TM], axis=0, keepdims=True)
                        dcw_ref[k, tap:tap + 1, :] += dw if dr == 0 else jnp.where(i == 0, 0.0, dw)
                dav_ref[k] = da.astype(BF16)

        _unless_ctx(skip_ctx, i == 0, (dav_ref,), work)

    whole = lambda rows: pl.BlockSpec((N_FFK, rows, FF_SLOT), lambda i: (0, 0, 0))
    return pl.pallas_call(
        body, name="conv_bwd", grid=(nt,),
        in_specs=[ANY] + _halo_specs(nt) + _halo_specs(nt) + [VMEM_WHOLE],
        out_specs=[pl.BlockSpec((N_FFK, TM, FF_SLOT), lambda i: (0, i, 0)), whole(9), whole(1)],
        out_shape=[jax.ShapeDtypeStruct(dav.shape, BF16), jax.ShapeDtypeStruct((N_FFK, 9, FF_SLOT), F32),
                   jax.ShapeDtypeStruct((N_FFK, 1, FF_SLOT), F32)],
        input_output_aliases={0: 0}, compiler_params=_cp(1))(dav, dac, dac, dac, av, av, av, cw)


def _norm_mod_bwd(x_ref, nw_ref, mod_ref, k_shift, dh, dx_in, dx_ref, dnw_ref, dmod_ref, is_ctx):
    _, vjp = jax.vjp(_norm_mod, x_ref[...], nw_ref[...], mod_ref[0, k_shift], mod_ref[0, k_shift + 1])
    dx, dnw, dshift, dscale = vjp(dh)
    dx_ref[...] = dx_in + dx
    dnw_ref[...] += dnw
    _stream_add(dmod_ref, 0, is_ctx, dshift)
    _stream_add(dmod_ref, 1, is_ctx, dscale)


def ffn_up_bwd_x(dx2, x, dav, mod, nw, wg, skip_ctx):
    t = x.shape[0]

    def body(dx2_ref, x_ref, dav_ref, mod_ref, nw_ref, w_ref, dx_ref, dnw_ref, dmod_ref):
        i = pl.program_id(0)

        @pl.when(i == 0)
        def _():
            dnw_ref[...] = jnp.zeros_like(dnw_ref)
            dmod_ref[...] = jnp.zeros_like(dmod_ref)

        def work():
            dh = mm_nt(dav_ref[0], w_ref[0])
            for j in range(1, N_DEV):
                dh = dh + mm_nt(dav_ref[j], w_ref[j])
            _norm_mod_bwd(x_ref, nw_ref, mod_ref, 3, dh, dx2_ref[...], dx_ref, dnw_ref, dmod_ref, i == 0)

        _unless_ctx(skip_ctx, i == 0, (dx_ref,), work)

    tile = pl.BlockSpec((TM, D), lambda i: (i, 0))
    return pl.pallas_call(
        body, name="ffn_up_bwd_x", grid=(t // TM,),
        in_specs=[tile, tile, pl.BlockSpec((N_DEV, TM, FF_SLOT), lambda i: (0, i, 0)), pl.BlockSpec((1, 6, 1, D), _stream_row(TM)),
                  pl.BlockSpec((1, D), lambda i: (0, 0)), VMEM_WHOLE],
        out_specs=[tile, pl.BlockSpec((1, D), lambda i: (0, 0)), pl.BlockSpec((2, 2, 1, D), lambda i: (0, 0, 0, 0))],
        out_shape=[jax.ShapeDtypeStruct((t, D), F32), jax.ShapeDtypeStruct((1, D), F32), jax.ShapeDtypeStruct((2, 2, 1, D), F32)],
        compiler_params=_cp(1))(dx2, x, dav, mod, nw, wg)


def weight_grad(at, dout, slot, name, after=None):
    rows, t = at.shape
    stacked = dout.ndim == 3
    n = dout.shape[0] if stacked else dout.shape[1] // slot

    def body(a_ref, d_ref, *rest):
        dw_ref = rest[-1]
        dw_ref[0] = jnp.dot(a_ref[...], d_ref[0] if stacked else d_ref[...], preferred_element_type=F32).astype(dw_ref.dtype)

    d_spec = pl.BlockSpec((1, t, slot), lambda j: (j, 0, 0)) if stacked else pl.BlockSpec((t, slot), lambda j: (0, j))
    extra = [] if after is None else [jnp.reshape(after, (1, 1))]
    return pl.pallas_call(
        body, name=name, grid=(n,), in_specs=[VMEM_WHOLE, d_spec] + [ANY] * len(extra),
        out_specs=pl.BlockSpec((1, rows, slot), lambda j: (j, 0, 0)),
        out_shape=jax.ShapeDtypeStruct((n, rows, slot), GRAD_WIRE), compiler_params=_cp(1))(at, dout, *extra)


def weight_grad_rows(at, dout, name):
    n, t, rows = at.shape
    cols = dout.shape[1]

    def body(a_ref, d_ref, dw_ref):
        dw_ref[0] = _dot(a_ref[0], d_ref[...], ((0,), (0,))).astype(dw_ref.dtype)

    return pl.pallas_call(
        body, name=name, grid=(n,), in_specs=[pl.BlockSpec((1, t, rows), lambda k: (k, 0, 0)), VMEM_WHOLE],
        out_specs=pl.BlockSpec((1, rows, cols), lambda k: (k, 0, 0)),
        out_shape=jax.ShapeDtypeStruct((n, rows, cols), GRAD_WIRE), compiler_params=_cp(1))(at, dout)


def mixer_bwd(dx, parts, o, pa, pb, y, mod, lnw, lnb, sw, sb, hnw, wa, wb, wo, skip_ctx):
    t = dx.shape[0]
    tm = TM
    n_ctx = CTX // tm

    def body(dx_ref, u_ref, v_ref, og_ref, ga_ref, gb_ref, o_ref, pa_ref, pb_ref, y_ref, mod_ref, lnw_ref, lnb_ref, sw_ref,
             sb_ref, hnw_ref, wa_ref, wb_ref, wo_ref, dp_ref, do_ref, dy_ref, dpa_ref, dpb_ref, dlnw_ref, dlnb_ref, dsw_ref,
             dsb_ref, dhnw_ref, dg_ref):
        i = pl.program_id(0)

        @pl.when(i == 0)
        def _():
            for r in (dlnw_ref, dlnb_ref, dsw_ref, dsb_ref, dhnw_ref, dg_ref):
                r[...] = jnp.zeros_like(r)

        def work():
            _, _, vjps, vjp_b = _mixer_tile(slice(0, tm), u_ref, v_ref, og_ref, o_ref, lnw_ref, lnb_ref, sw_ref, sb_ref, hnw_ref)
            pa, pb = pa_ref[...].astype(F32), pb_ref[...].astype(F32)
            sa, sbg = jax.nn.sigmoid(ga_ref[...]), jax.nn.sigmoid(gb_ref[...])
            dxv = dx_ref[...]
            _stream_add(dg_ref, 0, i < n_ctx, jnp.sum(dxv * y_ref[...].astype(F32), axis=0, keepdims=True))
            dy = (mod_ref[0, 2] * dxv).astype(BF16)
            dy_ref[...] = dy
            dmerged = mm_nt(dy, wo_ref[...])
            dpa, dpb = (sa * dmerged).astype(BF16), (sbg * dmerged).astype(BF16)
            dpa_ref[...], dpb_ref[...] = dpa, dpb
            first = 4 * D
            dp_ref[:, first + 3 * D:first + 4 * D] = (dmerged * pa * sa * (1.0 - sa)).astype(BF16)
            dp_ref[:, first + 4 * D:first + 5 * D] = (dmerged * pb * sbg * (1.0 - sbg)).astype(BF16)
            dya = mm_nt(dpa, wa_ref[...])
            dob, dog, dhnw = vjp_b(mm_nt(dpb, wb_ref[...]))
            dp_ref[:, first + 2 * D:first + 3 * D] = dog.astype(BF16)
            dhnw_ref[...] += dhnw
            for g in range(HEADS):
                do_ref[:, _hsl(g)] = dob[g]
            for c, vjp_a in enumerate(vjps):
                rows = slice(c * SGU_CH, (c + 1) * SGU_CH)
                dub, dvb, dlnw, dlnb, dsw, dsb = vjp_a(dya[rows])
                for g in range(HEADS):
                    dp_ref[rows, first + g * HD:first + (g + 1) * HD] = dub[g].astype(BF16)
                    dp_ref[rows, first + D + g * HD:first + D + (g + 1) * HD] = dvb[g].astype(BF16)
                    dlnw_ref[:, _hsl(g)] += dlnw[g]
                    dlnb_ref[:, _hsl(g)] += dlnb[g]
                    dsw_ref[g] += dsw[g]
                    dsb_ref[g] += dsb[g]

        _unless_ctx(skip_ctx, i < n_ctx, (dp_ref, do_ref, dy_ref, dpa_ref, dpb_ref), work)

    vec = lambda n: pl.BlockSpec((1, n), lambda i: (0, 0))
    tile = pl.BlockSpec((tm, D), lambda i: (i, 0))
    sds = jax.ShapeDtypeStruct
    return pl.pallas_call(
        body, name="mixer_bwd", grid=(t // tm,),
        in_specs=[tile] + _part_specs(tm, 4, 5)
        + [pl.BlockSpec((2, tm, D), lambda i: (0, i, 0)), tile, tile, tile, pl.BlockSpec((1, 6, 1, D), _stream_row(tm)),
           vec(D), vec(D), VMEM_WHOLE, VMEM_WHOLE, vec(HD), VMEM_WHOLE, VMEM_WHOLE, VMEM_WHOLE],
        out_specs=[pl.BlockSpec((tm, D_IN), lambda i: (i, 0)), tile, tile, tile, tile, vec(D), vec(D),
                   VMEM_WHOLE, VMEM_WHOLE, vec(HD), pl.BlockSpec((2, 1, 1, D), lambda i: (0, 0, 0, 0))],
        out_shape=[sds((t, D_IN), BF16), sds((t, D), F32), sds((t, D), BF16), sds((t, D), BF16), sds((t, D), BF16),
                   sds((1, D), F32), sds((1, D), F32), sds((HEADS, SGU_CH, SGU_CH), F32), sds((HEADS, SGU_CH, 1), F32),
                   sds((1, HD), F32), sds((2, 1, 1, D), F32)],
        compiler_params=_cp(1))(dx, parts, parts, parts, parts, parts, o, pa, pb, y, mod, lnw, lnb, sw, sb, hnw, wa, wb, wo)


def hgrn_bwd(d, parts, lb, mc, mtc, mrefc, ck, do, first=None, dparts=None):
    t = parts.shape[0]
    nb = t // SCAN_ROWS
    block = _scan_block(nb)
    rev = lambda s: block(d, nb - 1 - s)

    def body(q_ref, f_ref, i_ref, lb_ref, m_ref, mt_ref, mr_ref, ck_ref, do_ref, *rest):
        dst = rest[-1]
        dlb_ref = rest[-2]

        @pl.when(pl.program_id(0) == 0)
        def _():
            dst[...] = jnp.zeros_like(dst)
            dlb_ref[...] = jnp.zeros_like(dlb_ref)

        heads = range(HEADS)
        fn = functools.partial(_hgrn_chunk, m=m_ref[0], mt=mt_ref[0], mref=mr_ref[0])
        for c in reversed(range(SCAN_STEP)):
            first_row = c * CH if d == 0 else (SCAN_STEP - 1 - c) * CH
            rows = slice(first_row, first_row + CH)
            _, vjp = jax.vjp(fn, [ck_ref[0, c, h].astype(F32) for h in heads], [q_ref[rows, _hsl(h)] for h in heads],
                             [f_ref[rows, _hsl(h)] for h in heads], [i_ref[rows, _hsl(h)] for h in heads],
                             [lb_ref[0, :, _hsl(h)] for h in heads])
            dstl, dq, df, di, dlb = vjp(([do_ref[rows, _hsl(h)] for h in heads], [dst[h] for h in heads]))
            for h in heads:
                dst[h] = dstl[h]
                dlb_ref[0, :, _hsl(h)] += dlb[h]
                if d == 0:
                    dq_ref, df_ref, di_ref = rest[:3]
                    dq_ref[rows, _hsl(h)] = dq[h].astype(BF16)
                    df_ref[rows, _hsl(h)] = df[h].astype(BF16)
                    di_ref[rows, _hsl(h)] = di[h].astype(BF16)
                else:
                    dq0_ref, df0_ref, di0_ref, _, dp_ref = rest[:5]
                    col = lambda k: slice(k * D + h * HD, k * D + (h + 1) * HD)
                    dp_ref[rows, col(0)] = (dq0_ref[rows, _hsl(h)].astype(F32) + dq[h]).astype(BF16)
                    dp_ref[rows, col(1)] = df0_ref[rows, _hsl(h)]
                    dp_ref[rows, col(2)] = df[h].astype(BF16)
                    dp_ref[rows, col(3)] = (di0_ref[rows, _hsl(h)].astype(F32) + di[h]).astype(BF16)

    const = lambda s: (d, 0, 0)
    at = lambda k: pl.BlockSpec((SCAN_ROWS, D), lambda s: (rev(s), k))
    in_specs = [at(0), at(1 + d), at(3), pl.BlockSpec((1, 1, D), const), pl.BlockSpec((1, CH, CH), const),
                pl.BlockSpec((1, CH, CH), const), pl.BlockSpec((1, CH, 1), const),
                pl.BlockSpec((1, SCAN_STEP, HEADS, HD, HD), lambda s: (d, nb - 1 - s, 0, 0, 0)), at(0)]
    dlb_spec, dlb_shape = pl.BlockSpec((1, 1, D), lambda s: (0, 0, 0)), jax.ShapeDtypeStruct((1, 1, D), F32)
    common = dict(grid=(nb,), scratch_shapes=[pltpu.VMEM((HEADS, HD, HD), F32)], compiler_params=_cp(1))
    if d == 0:
        return pl.pallas_call(body, name="hgrn_bwd_fwd_dir", in_specs=in_specs, out_specs=[at(0)] * 3 + [dlb_spec],
                              out_shape=[jax.ShapeDtypeStruct((t, D), BF16)] * 3 + [dlb_shape], **common,
                              )(parts, parts, parts, lb, mc, mtc, mrefc, ck, do)
    return pl.pallas_call(body, name="hgrn_bwd_bwd_dir", in_specs=in_specs + [at(0)] * 3 + [ANY],
                          out_specs=[pl.BlockSpec((SCAN_ROWS, 4 * D), lambda s: (rev(s), 0)), dlb_spec],
                          out_shape=[jax.ShapeDtypeStruct(dparts.shape, BF16), dlb_shape], input_output_aliases={12: 0},
                          **common)(parts, parts, parts, lb, mc, mtc, mrefc, ck, do, *first, dparts)


def in_proj_bwd_x(dx1, x, dparts, mod, nw, wg, after=None, latent_only=False):
    t = x.shape[0]
    tm = TM
    n_ctx = CTX // tm

    def body(dx1_ref, x_ref, dp_ref, mod_ref, nw_ref, w_ref, *rest):
        dx_ref, dnw_ref, dmod_ref = rest[-3:]
        i = pl.program_id(0)

        @pl.when(i == 0)
        def _():
            dnw_ref[...] = jnp.zeros_like(dnw_ref)
            dmod_ref[...] = jnp.zeros_like(dmod_ref)

        dh = mm_nt(dp_ref[:, 0:IN_SLOT], w_ref[0])
        for j in range(1, N_DEV):
            dh = dh + mm_nt(dp_ref[:, j * IN_SLOT:(j + 1) * IN_SLOT], w_ref[j])
        _norm_mod_bwd(x_ref, nw_ref, mod_ref, 0, dh, dx1_ref[...], dx_ref, dnw_ref, dmod_ref, i < n_ctx)

    tile = pl.BlockSpec((tm, D), lambda i: (i, 0))
    extra = [] if after is None else [jnp.reshape(after, (1, 1))]
    return pl.pallas_call(
        body, name="in_proj_bwd_x", grid=(t // tm,),
        in_specs=[tile, tile, pl.BlockSpec((tm, D_IN), lambda i: (i, 0)), pl.BlockSpec((1, 6, 1, D), _stream_row(tm)),
                  pl.BlockSpec((1, D), lambda i: (0, 0)), VMEM_WHOLE] + [ANY] * len(extra),
        out_specs=[pl.BlockSpec((tm, D), lambda i: (jnp.maximum(i - n_ctx, 0), 0)) if latent_only else tile,
                   pl.BlockSpec((1, D), lambda i: (0, 0)), pl.BlockSpec((2, 2, 1, D), lambda i: (0, 0, 0, 0))],
        out_shape=[jax.ShapeDtypeStruct((t - CTX if latent_only else t, D), F32), jax.ShapeDtypeStruct((1, D), F32),
                   jax.ShapeDtypeStruct((2, 2, 1, D), F32)],
        compiler_params=_cp(1))(dx1, x, dparts, mod, nw, wg, *extra)


def _lb_fn(h0, h1):
    m = jnp.maximum(h0, h1)
    e0, e1 = jnp.exp(h0 - m), jnp.exp(h1 - m)
    return e1 / (e0 + e1)


def lower_bounds(hlb):
    def body(h_ref, out_ref):
        out_ref[...] = _lb_fn(h_ref[0:1, :], h_ref[1:2, :])
    return pl.pallas_call(body, name="lower_bounds", out_shape=jax.ShapeDtypeStruct((1, 2 * D), F32))(hlb)


def lower_bounds_bwd(hlb, dlb1):
    def body(h_ref, d_ref, out_ref):
        _, vjp = jax.vjp(_lb_fn, h_ref[0:1, :], h_ref[1:2, :])
        d0, d1 = vjp(d_ref[...])
        out_ref[0:1, :] = d0
        out_ref[1:2, :] = d1
    return pl.pallas_call(body, name="lower_bounds_bwd", out_shape=jax.ShapeDtypeStruct((2, 2 * D), F32))(hlb, dlb1)


def _ada_fn(c_all, cctx8, w, b):
    dot = lambda a, l: mm(_silu(a), w[l]) + b[l]
    return [dot(c_all, l) for l in range(2)], [dot(cctx8, l) for l in range(2)]


def ada_fwd(c_all, cctx8, w, b):
    cols = w.shape[-1]

    def body(c_ref, cc_ref, w_ref, b_ref, out_ref):
        ox, oc = _ada_fn(c_ref[...], cc_ref[...], [w_ref[0], w_ref[1]], [b_ref[0], b_ref[1]])
        for l in range(2):
            out_ref[l, 0] = ox[l]
            out_ref[l, 1] = oc[l]
    return pl.pallas_call(body, name="ada_fwd", out_shape=jax.ShapeDtypeStruct((2, 2, N_DEV, cols), F32),
                          compiler_params=_cp(0))(c_all, cctx8, w, b)


def ada_bwd(c_all, cctx8, w, b, dmx, dmc):
    cols = w.shape[-1]

    def body(c_ref, cc_ref, w_ref, b_ref, dmx_ref, dmc_ref, dw_ref, dc_ref):
        fn = lambda cc, w0, w1: _ada_fn(c_ref[...], cc, [w0, w1], [b_ref[0], b_ref[1]])
        _, vjp = jax.vjp(fn, cc_ref[...], w_ref[0], w_ref[1])
        dcc, dw0, dw1 = vjp(([dmx_ref[0], dmx_ref[1]], [dmc_ref[0], dmc_ref[1]]))
        dw_ref[0] = dw0
        dw_ref[1] = dw1
        dc_ref[...] = jnp.sum(dcc, axis=0, keepdims=True)
    return pl.pallas_call(body, name="ada_bwd", out_shape=[jax.ShapeDtypeStruct((2, D, cols), F32), jax.ShapeDtypeStruct((1, D), F32)],
                          compiler_params=_cp(0))(c_all, cctx8, w, b, dmx, dmc)


def adamw(w, m, v, gparts, name):
    stacked = w.ndim == 3
    r, c = w.shape[-2:]
    p = gparts.shape[0]
    rt = r
    while rt % 16 == 0 and (p + 7) * rt * c * 4 * 2 > ELEMENTWISE_VMEM:
        rt //= 2

    def body(w_ref, m_ref, v_ref, g_ref, go_ref, d_ref, mo_ref, vo_ref):
        part = (lambda k: g_ref[k, 0]) if stacked else (lambda k: g_ref[k])
        g = part(0).astype(F32)
        for k in range(1, p):
            g = g + part(k).astype(F32)
        shape = go_ref.shape
        m2 = ADAM_B1 * m_ref[...].reshape(g.shape) + (1.0 - ADAM_B1) * g
        v2 = ADAM_B2 * v_ref[...].reshape(g.shape) + (1.0 - ADAM_B2) * (g * g)
        m_hat = m2 / (1.0 - ADAM_B1 ** ADAM_STEP)
        v_hat = v2 / (1.0 - ADAM_B2 ** ADAM_STEP)
        go_ref[...] = g.reshape(shape)
        d_ref[...] = (-ADAM_LR * (m_hat / (jnp.sqrt(v_hat) + ADAM_EPS) + ADAM_WD * w_ref[...].reshape(g.shape))).reshape(shape)
        mo_ref[...] = m2.reshape(shape)
        vo_ref[...] = v2.reshape(shape)

    if stacked:
        tile = pl.BlockSpec((1, rt, c), lambda l, i: (l, i, 0))
        g_spec, grid = pl.BlockSpec((p, 1, rt, c), lambda l, i: (0, l, i, 0)), (w.shape[0], r // rt)
    else:
        tile = pl.BlockSpec((rt, c), lambda i: (i, 0))
        g_spec, grid = pl.BlockSpec((p, rt, c), lambda i: (0, i, 0)), (r // rt,)
    return pl.pallas_call(
        body, name=name, grid=grid, in_specs=[tile, tile, tile, g_spec], out_specs=[tile] * 4,
        out_shape=[jax.ShapeDtypeStruct(w.shape, F32)] * 4, compiler_params=_cp(len(grid)))(w, m, v, gparts)


def _me():
    x, y, c = lax.axis_index("x"), lax.axis_index("y"), lax.axis_index("c")
    return x, y, c, 4 * x + 2 * y + c


def _peer(x, y, c, p):
    fx, fy, fc = (p >> 2) & 1, (p >> 1) & 1, p & 1
    return (1 - x if fx else x, 1 - y if fy else y, 1 - c if fc else c)


def all_gather(arrs, name, after=None):
    n = len(arrs)
    extra = [] if after is None else list(after) if isinstance(after, (list, tuple)) else [after]

    def body(*refs):
        ins, outs = refs[:n], refs[n + len(extra):2 * n + len(extra)]
        send, recv, local = refs[2 * n + len(extra):]
        x, y, c, me = _me()
        copies = []
        for a in range(n):
            lc = pltpu.make_async_copy(ins[a], outs[a].at[me], local.at[a])
            lc.start()
            copies.append(lc)
            for p in range(1, N_DEV):
                cp = pltpu.make_async_remote_copy(src_ref=ins[a], dst_ref=outs[a].at[me], send_sem=send.at[a, p - 1],
                                                  recv_sem=recv.at[a, p - 1], device_id=_peer(x, y, c, p),
                                                  device_id_type=pl.DeviceIdType.MESH)
                cp.start()
                copies.append(cp)
        for cp in copies:
            cp.wait()

    return pl.pallas_call(
        body, name=name, in_specs=[ANY] * (n + len(extra)), out_specs=[ANY] * n,
        out_shape=[jax.ShapeDtypeStruct((N_DEV,) + a.shape, a.dtype) for a in arrs],
        scratch_shapes=[pltpu.SemaphoreType.DMA((n, N_DEV - 1)), pltpu.SemaphoreType.DMA((n, N_DEV - 1)),
                        pltpu.SemaphoreType.DMA((n,))])(*arrs, *extra)


HBM = pl.BlockSpec(memory_space=pltpu.HBM)
SEM = pl.BlockSpec(memory_space=pltpu.SEMAPHORE)


def _in_hbm(a):
    return pltpu.with_memory_space_constraint(a, pltpu.HBM)


ALL_PEERS = tuple(range(1, N_DEV))
SAME_CORE_AND_SIBLING = (1, 2, 4, 6)
OTHER_CHIPS = (2, 4, 6)


def _exchange_refs(srcs, lands, layer, scatter, a, x, y, c, p, forward=False):
    me = 4 * x + 2 * y + c
    px, py, pc = _peer(x, y, c, p) if p else (x, y, c)
    if forward and p:
        slot = lands[a].at[4 * px + 2 * py + pc]
        return slot, slot, _peer(x, y, c, 1)
    dst = lands[a].at[me] if layer is None else lands[a].at[me, layer]
    src = srcs[a].at[4 * px + 2 * py + pc] if scatter else dst
    return src, dst, (px, py, pc)


def exchange_start(srcs, lands, layer, scatter, name, after=None, peers=ALL_PEERS, forward=False):
    n, ns = len(lands), len(srcs)
    extra = [] if after is None else [after]

    def body(*refs):
        ins, lz = refs[:ns], refs[ns:ns + n]
        send, recv = refs[ns + n + len(extra)], refs[ns + n + len(extra) + 1]
        token = refs[-1]
        x, y, c, _ = _me()
        for a in range(n):
            for p in peers:
                src, dst, peer = _exchange_refs(ins, lz, layer, scatter, a, x, y, c, p, forward)
                k = a * (N_DEV - 1) + p - 1
                pltpu.make_async_remote_copy(src_ref=src, dst_ref=dst, send_sem=send.at[k], recv_sem=recv.at[k],
                                             device_id=peer, device_id_type=pl.DeviceIdType.MESH).start()
        token[...] = jnp.zeros_like(token)

    thru = [pltpu.HBM(a.shape, a.dtype) for a in list(srcs) + list(lands)]
    out = pl.pallas_call(
        body, name=name, in_specs=[HBM] * (ns + n) + [ANY] * len(extra),
        out_specs=[SEM, SEM] + [HBM] * (ns + n) + [pl.BlockSpec(memory_space=pltpu.VMEM)],
        out_shape=[pltpu.SemaphoreType.DMA((n * (N_DEV - 1),)), pltpu.SemaphoreType.DMA((n * (N_DEV - 1),))] + thru
        + [jax.ShapeDtypeStruct((8, 128), F32)],
        input_output_aliases={i: 2 + i for i in range(ns + n)},
        compiler_params=pltpu.CompilerParams(has_side_effects=pltpu.SideEffectType.DATAFLOW_SIDE_EFFECTING),
    )(*[_in_hbm(a) for a in list(srcs) + list(lands)], *extra)
    return out[0], out[1], out[2:2 + ns], out[2 + ns:2 + ns + n], out[-1]


def exchange_wait(send, recv, srcs, lands, layer, scatter, after, name, peers=ALL_PEERS):
    n, ns = len(lands), len(srcs)

    def body(*refs):
        ins, lz = refs[:ns], refs[ns:ns + n]
        send_ref, recv_ref = refs[ns + n], refs[ns + n + 1]
        x, y, c, _ = _me()
        for a in range(n):
            for p in peers:
                src, dst, peer = _exchange_refs(ins, lz, layer, scatter, a, x, y, c, 0)
                k = a * (N_DEV - 1) + p - 1
                cp = pltpu.make_async_remote_copy(src_ref=src, dst_ref=dst, send_sem=send_ref.at[k],
                                                  recv_sem=recv_ref.at[k], device_id=peer,
                                                  device_id_type=pl.DeviceIdType.MESH)
                cp.wait_send()
                cp.wait_recv()

    thru = [pltpu.HBM(a.shape, a.dtype) for a in list(srcs) + list(lands)]
    out = pl.pallas_call(
        body, name=name, in_specs=[HBM] * (ns + n) + [SEM, SEM, ANY], out_specs=[HBM] * (ns + n), out_shape=thru,
        input_output_aliases={i: i for i in range(ns + n)},
        compiler_params=pltpu.CompilerParams(has_side_effects=pltpu.SideEffectType.DATAFLOW_SIDE_EFFECTING),
    )(*srcs, *lands, send, recv, after)
    return out[ns:]


def place_own(src, land, me, layer, scatter, name, src_layer=None, after=None):
    create = isinstance(land, jax.ShapeDtypeStruct)
    r, c = src.shape[-2:]
    rt = r
    while rt % 32 == 0 and rt * c * 4 > COPY_BLOCK_BYTES:
        rt //= 2

    extra = [] if after is None else [after]

    def body(me_ref, src_ref, *rest):
        out_ref = rest[-1]
        out_ref[...] = src_ref[...].reshape(out_ref.shape).astype(out_ref.dtype)

    src_spec = (pl.BlockSpec((1, rt, c), lambda i, m: (m[0], i, 0)) if scatter else
                pl.BlockSpec((rt, c), lambda i, m: (i, 0)) if src_layer is None else
                pl.BlockSpec((1, rt, c), lambda i, m: (src_layer, i, 0)))
    out_spec = (pl.BlockSpec((1, rt, c), lambda i, m: (m[0], i, 0)) if layer is None
                else pl.BlockSpec((1, 1, rt, c), lambda i, m: (m[0], layer, i, 0)))
    grid_spec = pltpu.PrefetchScalarGridSpec(num_scalar_prefetch=1, grid=(r // rt,),
                                             in_specs=[src_spec] + ([] if create else [ANY]) + [ANY] * len(extra),
                                             out_specs=out_spec)
    return pl.pallas_call(body, name=name, grid_spec=grid_spec, out_shape=jax.ShapeDtypeStruct(land.shape, land.dtype),
                          input_output_aliases={} if create else {2: 0}, compiler_params=_cp(1),
                          )(*((me, src) if create else (me, src, land)), *extra)


def _scan_constants():
    r = lax.broadcasted_iota(jnp.int32, (CH, CH), 0)
    s = lax.broadcasted_iota(jnp.int32, (CH, CH), 1)
    lower = (s <= r).astype(F32)
    t = jnp.arange(CH)[:, None]
    mc = jnp.stack([lower, lower.T])
    mref = jnp.stack([(t <= CH // 2 - 1).astype(F32), (t >= CH // 2).astype(F32)])
    return mc, jnp.stack([lower.T, lower]), mref


def local_step(x, ctx, target, mod, lb, w, fetch=None, publish=None, small_ready=None, small_early=None):
    kept = {}

    def keep(l, part, grads):
        kept[(l, part)] = grads
        return 0.0

    fetch = fetch or (lambda l, part, after: w)
    publish = publish or keep
    n_layers = len(mod)
    mc, mtc, mrefc = _scan_constants()
    xs = jnp.concatenate([ctx, x], axis=0)
    saved, big = [], []
    for l in range(n_layers):
        wl = dict(fetch(l, "in", xs))
        parts, ht, iv = in_proj_fwd(xs, mod[l], w["nw1"][l], wl["win"][l])
        o, ck = hgrn_fwd(parts, iv, lb[l], mc, mtc, mrefc)
        wl.update(fetch(l, "rest", o))
        last = l == n_layers - 1
        x1, pa, pb, ym, yat, ybt, mt = mixer_fwd(xs, parts, o, mod[l], w["lnw"][l], w["lnb"][l], w["sw"][l], w["sb"][l],
                                                 w["hnw"][l], wl["wa"][l], wl["wb"][l], wl["wo"][l], last)
        av, h2t = ffn_up_fwd(x1, mod[l], w["nw2"][l], wl["wup"][l], last)
        x2, ac, y, z = ffn_down_fwd(x1, av, mod[l], w["cw"][l], w["cb"][l], wl["wd"][l], last)
        saved.append((xs, parts, iv, o, ck, x1, av, ac, y, z, ht, h2t, pa, pb, ym, yat, ybt, mt))
        big.append(wl)
        xs = x2
    loss, dx, dfw = loss_fwd_bwd(xs, target, w["fw"])
    g = {k: [None] * n_layers for k in ("nw1", "nw2", "lnw", "lnb", "sw", "sb", "hnw", "cw", "cb")}
    g["fw"] = dfw
    dmod, dlb = [None] * n_layers, [None] * n_layers
    tok = 0.0
    for l in reversed(range(n_layers)):
        x0, parts, iv, o, ck, x1, av, ac, y, z, ht, h2t, pa, pb, ym, yat, ybt, mt = saved[l]
        wl = big[l]
        last = l == n_layers - 1
        dav, dac, dout, dg2 = ffn_down_bwd(dx, ac, av, y, mod[l] + tok, wl["wd"][l], last)
        dwd = weight_grad_rows(z, dout, "ffn_down_bwd_w")
        dav, g["cw"][l], g["cb"][l] = conv_bwd(dav, dac, av, w["cw"][l], last)
        dx1, g["nw2"][l], dmod2 = ffn_up_bwd_x(dx, x1, dav, mod[l], w["nw2"][l], wl["wup"][l], last)
        dwup = weight_grad(h2t, dav, FF_SLOT, "ffn_up_bwd_w")
        tok = publish(l, "ffn", {"wd": dwd, "wup": dwup})
        (dparts, do, dy, dpa, dpb, g["lnw"][l], g["lnb"][l], g["sw"][l], g["sb"][l], g["hnw"][l],
         dg1) = mixer_bwd(dx1, parts, o, pa, pb, ym, mod[l] + tok, w["lnw"][l], w["lnb"][l], w["sw"][l], w["sb"][l],
                          w["hnw"][l], wl["wa"][l], wl["wb"][l], wl["wo"][l], last)
        tok = publish(l, "mix", {"wa": weight_grad(yat, dpa, D, "mixer_bwd_wa"), "wb": weight_grad(ybt, dpb, D, "mixer_bwd_wb"),
                                 "wo": weight_grad(mt, dy, D, "mixer_bwd_wo")})
        if l == 0 and small_early:
            dmod[0] = jnp.concatenate([jnp.zeros((2, 2, 1, D), F32), dg1, dmod2, dg2], axis=1)
            tok = tok + small_early(loss[0, 0], g, dmod, dlb)
        dq, df, di, dlb_f = hgrn_bwd(0, parts, lb[l] + tok, mc, mtc, mrefc, ck, do)
        dparts, dlb_b = hgrn_bwd(1, parts, lb[l], mc, mtc, mrefc, ck, do, (dq, df, di), dparts)
        dlb[l] = jnp.concatenate([dlb_f, dlb_b], axis=0)
        tok = publish(l, "in", {"win": weight_grad(ht, dparts, IN_SLOT, "in_proj_bwd_w")})
        dx, g["nw1"][l], dmod1 = in_proj_bwd_x(dx1, x0, dparts, mod[l], w["nw1"][l], wl["win"][l], after=tok,
                                               latent_only=l == 0)
        dmod[l] = jnp.concatenate([dmod1, dg1, dmod2, dg2], axis=1)
    done = small_ready(loss[0, 0], g, dmod, dlb) if small_ready else 0.0
    for (l, part), grads in kept.items():
        for k, v in grads.items():
            g.setdefault(k, [None] * n_layers)[l] = v
    return loss[0, 0], dx, g, dmod, dlb, done


ROW = 1024
REPLICATED = ("norm1_w", "sgu_ln_w", "sgu_ln_b", "sgu_w", "sgu_b", "hgrn_lower_bounds", "hgrn_norm_w", "norm2_w",
              "ffn_conv_b", "final_norm_w")
WEIGHT_ORDER = ("c_ctx", "ada_w", "ada_b", "norm1_w", "w_in", "sgu_ln_w", "sgu_ln_b", "sgu_w", "sgu_b", "hgrn_lower_bounds",
                "hgrn_norm_w", "w_branch_a", "w_branch_b", "w_out", "norm2_w", "ffn_w_up", "ffn_conv_w", "ffn_conv_b",
                "ffn_w_down", "final_norm_w")


def _rows_of(n):
    return -(-n // (8 * ROW)) * 8


def _pack(arrs, total_rows=None):
    parts = []
    for a in arrs:
        flat = a.reshape(-1).astype(F32)
        rows = _rows_of(flat.shape[0])
        parts.append(jnp.pad(flat, (0, rows * ROW - flat.shape[0])).reshape(rows, ROW))
    have = sum(p.shape[0] for p in parts)
    if total_rows is not None and total_rows > have:
        parts.append(jnp.zeros((total_rows - have, ROW), F32))
    return jnp.concatenate(parts, axis=0)


def _unpack(packed, shapes):
    lead = packed.shape[:-2]
    out, r0 = [], 0
    for s in shapes:
        n = math.prod(s)
        rows = _rows_of(n)
        out.append(packed[..., r0:r0 + rows, :].reshape(lead + (rows * ROW,))[..., :n].reshape(lead + tuple(s)))
        r0 += rows
    return out


def kernel(x, c, ctx, c_ctx, ada_w, ada_b, norm1_w, w_in, sgu_ln_w, sgu_ln_b, sgu_w, sgu_b, hgrn_lower_bounds, hgrn_norm_w, w_branch_a, w_branch_b, w_out, norm2_w, ffn_w_up, ffn_conv_w, ffn_conv_b, ffn_w_down, final_norm_w, loss_target, m_c_ctx, m_ada_w, m_ada_b, m_norm1_w, m_w_in, m_sgu_ln_w, m_sgu_ln_b, m_sgu_w, m_sgu_b, m_hgrn_lower_bounds, m_hgrn_norm_w, m_w_branch_a, m_w_branch_b, m_w_out, m_norm2_w, m_ffn_w_up, m_ffn_conv_w, m_ffn_conv_b, m_ffn_w_down, m_final_norm_w, v_c_ctx, v_ada_w, v_ada_b, v_norm1_w, v_w_in, v_sgu_ln_w, v_sgu_ln_b, v_sgu_w, v_sgu_b, v_hgrn_lower_bounds, v_hgrn_norm_w, v_w_branch_a, v_w_branch_b, v_w_out, v_norm2_w, v_ffn_w_up, v_ffn_conv_w, v_ffn_conv_b, v_ffn_w_down, v_final_norm_w):
    wts = dict(c_ctx=c_ctx, ada_w=ada_w, ada_b=ada_b, norm1_w=norm1_w, w_in=w_in, sgu_ln_w=sgu_ln_w, sgu_ln_b=sgu_ln_b,
               sgu_w=sgu_w, sgu_b=sgu_b, hgrn_lower_bounds=hgrn_lower_bounds, hgrn_norm_w=hgrn_norm_w, w_branch_a=w_branch_a,
               w_branch_b=w_branch_b, w_out=w_out, norm2_w=norm2_w, ffn_w_up=ffn_w_up, ffn_conv_w=ffn_conv_w,
               ffn_conv_b=ffn_conv_b, ffn_w_down=ffn_w_down, final_norm_w=final_norm_w)
    mom1 = dict(c_ctx=m_c_ctx, ada_w=m_ada_w, ada_b=m_ada_b, norm1_w=m_norm1_w, w_in=m_w_in, sgu_ln_w=m_sgu_ln_w,
                sgu_ln_b=m_sgu_ln_b, sgu_w=m_sgu_w, sgu_b=m_sgu_b, hgrn_lower_bounds=m_hgrn_lower_bounds,
                hgrn_norm_w=m_hgrn_norm_w, w_branch_a=m_w_branch_a, w_branch_b=m_w_branch_b, w_out=m_w_out, norm2_w=m_norm2_w,
                ffn_w_up=m_ffn_w_up, ffn_conv_w=m_ffn_conv_w, ffn_conv_b=m_ffn_conv_b, ffn_w_down=m_ffn_w_down,
                final_norm_w=m_final_norm_w)
    mom2 = dict(c_ctx=v_c_ctx, ada_w=v_ada_w, ada_b=v_ada_b, norm1_w=v_norm1_w, w_in=v_w_in, sgu_ln_w=v_sgu_ln_w,
                sgu_ln_b=v_sgu_ln_b, sgu_w=v_sgu_w, sgu_b=v_sgu_b, hgrn_lower_bounds=v_hgrn_lower_bounds,
                hgrn_norm_w=v_hgrn_norm_w, w_branch_a=v_w_branch_a, w_branch_b=v_w_branch_b, w_out=v_w_out, norm2_w=v_norm2_w,
                ffn_w_up=v_ffn_w_up, ffn_conv_w=v_ffn_conv_w, ffn_conv_b=v_ffn_conv_b, ffn_w_down=v_ffn_w_down,
                final_norm_w=v_final_norm_w)
    n_layers = w_in.shape[0]
    layers = range(n_layers)
    me = 4 * lax.axis_index("x") + 2 * lax.axis_index("y") + lax.axis_index("c")
    ada_cols = ada_w.shape[-1]

    big = ("w_in", "ffn_w_up", "w_branch_a", "w_branch_b", "w_out", "ffn_w_down")
    short = {"w_in": "win", "ffn_w_up": "wup", "w_branch_a": "wa", "w_branch_b": "wb", "w_out": "wo", "ffn_w_down": "wd"}
    me1 = me.reshape(1).astype(jnp.int32)
    groups = [[("w_in", 0)], [(k, 0) for k in big[1:]], [("w_in", 1)], [(k, 1) for k in big[1:]]]
    in_flight, started = [], 0.0

    def own_slots(n, after):
        return [place_own(wts[k], jax.ShapeDtypeStruct((N_DEV,) + wts[k].shape[1:], BF16), me1, None, False,
                          f"gather_own_{short[k]}_{l}", src_layer=l, after=after) for k, l in groups[n]]

    def start_group(n, lands, after):
        in_flight.append(exchange_start([], lands, None, False, f"gather_weights_start_{n}", after=after,
                                        peers=SAME_CORE_AND_SIBLING if n == 0 else ALL_PEERS))
        return in_flight[-1][-1]

    (c_all,) = all_gather([c], "gather_c")
    c_all = c_all.reshape(N_DEV, D)
    token = start_group(0, own_slots(0, c_all), c_all)
    later = [own_slots(n, token) for n in range(1, len(groups))]
    cctx8 = jnp.broadcast_to(c_ctx[None, :], (N_DEV, D))
    ada_b_cols = lax.dynamic_slice_in_dim(ada_b, me * ada_cols, ada_cols, axis=1)[:, None, :]
    mod_cols = ada_fwd(c_all, cctx8, ada_w, ada_b_cols)
    xs = jnp.concatenate([ctx[0], x[0]], axis=0)
    lb1 = lower_bounds(hgrn_lower_bounds)
    mod_all, conv_all = all_gather([mod_cols, ffn_conv_w.reshape(n_layers, 9, -1)], "gather_mod_conv",
                                   after=[token, xs, lb1] + [a for lands in later for a in lands])
    conv_full = [conv_all[:, l].transpose(1, 0, 2).reshape(9, N_FFK, FF_SLOT).transpose(1, 0, 2) for l in layers]
    for n in range(1, len(groups)):
        token = start_group(n, later[n - 1], mod_all if n == 1 else token)
    for started_group in in_flight:
        started = started + started_group[-1][0, 0]

    def as_used(k, a):
        return a if k in ("w_in", "ffn_w_up") else a.reshape(N_FFK, FF_SLOT, D) if k == "ffn_w_down" else a.reshape(D, D)

    arrived = {}

    def fetch(l, part, after):
        n = {(0, "in"): 0, (0, "rest"): 1, (1, "in"): 2, (1, "rest"): 3}.get((l, part))
        if n is not None:
            send, recv, _, lands, _ = in_flight[n]
            first = n == 0
            got = exchange_wait(send, recv, [], lands, None, False, after, f"gather_weights_wait_{n}",
                                peers=SAME_CORE_AND_SIBLING if first else ALL_PEERS)
            if first:
                send, recv, _, lands, _ = exchange_start([], got, None, False, "gather_weights_pass_on", peers=OTHER_CHIPS,
                                                         forward=True)
                got = exchange_wait(send, recv, [], lands, None, False, after, "gather_weights_passed_on", peers=OTHER_CHIPS)
            for (k, ll), a in zip(groups[n], got):
                arrived.setdefault(short[k], [None] * n_layers)[ll] = as_used(k, a)
        return arrived

    mod_x = lax.dynamic_index_in_dim(mod_all[:, :, 0], me, axis=2, keepdims=False)
    mod_c = mod_all[:, :, 1, 0]
    mod = [jnp.stack([mod_c[:, l].reshape(6, 1, D), mod_x[:, l].reshape(6, 1, D)]) for l in layers]
    mod[0] = mod[0] + started

    lb = [jnp.zeros((2, 1, D), F32), lb1.reshape(2, 1, D)]

    w = {
        "nw1": [norm1_w[l][None] for l in layers], "nw2": [norm2_w[l][None] for l in layers],
        "lnw": [sgu_ln_w[l][None] for l in layers], "lnb": [sgu_ln_b[l][None] for l in layers],
        "sw": [sgu_w[l] for l in layers], "sb": [sgu_b[l][:, :, None] for l in layers],
        "hnw": [hgrn_norm_w[l][None] for l in layers], "cw": conv_full,
        "cb": [ffn_conv_b[l].reshape(N_FFK, 1, FF_SLOT) for l in layers], "fw": final_norm_w[None],
    }
    long = {v: k for k, v in short.items()}
    landing, sent = {}, []

    def publish(l, part, grads):
        keys = [long[k] for k in grads]
        slots = [a.reshape((N_DEV, -1, a.shape[-1])) for a in grads.values()]
        zones = [place_own(s, landing.get(k, jax.ShapeDtypeStruct((N_DEV, n_layers) + s.shape[1:], s.dtype)), me1, l, True,
                           f"scatter_own_{short[k]}_{l}") for k, s in zip(keys, slots)]
        send, recv, srcs, zones, token = exchange_start(slots, zones, l, True, f"scatter_grads_start_{part}_{l}")
        landing.update(zip(keys, zones))
        sent.append((keys, l, part, send, recv, srcs, token))
        return token[0, 0]

    out = {}
    flat2 = lambda a: a.reshape(-1, a.shape[-1])

    def finish(part, after):
        done = []
        for keys, l, p, send, recv, srcs, _ in sent:
            if p == part:
                zones = exchange_wait(send, recv, srcs, [landing[k] for k in keys], l, True, after,
                                      f"scatter_grads_wait_{part}_{l}")
                landing.update(zip(keys, zones))
                done = keys
        for k in done:
            out[k] = tuple(adamw(wts[k], mom1[k], mom2[k], landing[k], "adamw_" + k))

    rep_rows = -(-sum(_rows_of(wts[k].size) for k in REPLICATED) // 64) * 64
    conv_rows = _rows_of(n_layers * 9 * D_FF)
    dmod_rows = _rows_of(n_layers * 6 * D)
    early = {}

    def small_early(loss_part, g, dmod, dlb):
        d_hlb = lower_bounds_bwd(hgrn_lower_bounds, dlb[1].reshape(1, 2 * D))
        st = lambda k: jnp.stack([jnp.zeros((1, D), F32) if a is None else a for a in g[k]])
        rep_grads = {"norm1_w": st("nw1"), "sgu_ln_w": st("lnw"), "sgu_ln_b": st("lnb"), "sgu_w": st("sw"), "sgu_b": st("sb"),
                     "hgrn_lower_bounds": d_hlb, "hgrn_norm_w": st("hnw"), "norm2_w": st("nw2"), "ffn_conv_b": st("cb"),
                     "final_norm_w": g["fw"]}
        d_conv = jnp.stack([g["cw"][l].transpose(1, 0, 2).reshape(9, D_FF) for l in layers])
        dmod_x = jnp.stack([dmod[l][1].reshape(6 * D) for l in layers])
        dmod_c = jnp.stack([dmod[l][0].reshape(6 * D) for l in layers])
        small = jnp.concatenate([_pack([rep_grads[k] for k in REPLICATED], rep_rows),
                                 _pack([d_conv, dmod_x, dmod_c, loss_part.reshape(1)])], axis=0)
        zone = place_own(small, jax.ShapeDtypeStruct((N_DEV,) + small.shape, F32), me1, None, False, "gather_small_own")
        early["send"], early["recv"], _, early["zones"], token = exchange_start([], [zone], None, False, "gather_small_start")
        return token[0, 0]

    def small_ready(loss_part, g, dmod, dlb):
        late = _pack([g["nw1"][0], dmod[0][1, 0:2], dmod[0][0, 0:2]])
        for part in ("ffn", "mix"):
            finish(part, late)
        (late_all,) = all_gather([late], "gather_small_late", after=[out[k][0] for k in big[1:]])
        (small_all,) = exchange_wait(early["send"], early["recv"], [], early["zones"], None, False, late_all,
                                     "gather_small_wait")
        at_x = rep_rows + conv_rows
        small_all = small_all.at[:, 0:1].set(late_all[:, 0:1])
        small_all = small_all.at[:, at_x:at_x + 2].set(late_all[:, 8:10])
        small_all = small_all.at[:, at_x + dmod_rows:at_x + dmod_rows + 2].set(late_all[:, 16:18])
        d_conv_shape, dmod_shape = (n_layers, 9, D_FF), (n_layers, 6 * D)
        conv_g, dmx_all, dmc_all, loss_all = _unpack(small_all[:, rep_rows:], [d_conv_shape, dmod_shape, dmod_shape, (1,)])
        out["loss"] = functools.reduce(lambda a, b: a + b, [loss_all[k, 0] for k in range(N_DEV)])

        rep = adamw(_pack([wts[k] for k in REPLICATED], rep_rows), _pack([mom1[k] for k in REPLICATED], rep_rows),
                    _pack([mom2[k] for k in REPLICATED], rep_rows), small_all, "adamw_replicated")
        rep = [_unpack(r, [wts[k].shape for k in REPLICATED]) for r in rep]
        for n, k in enumerate(REPLICATED):
            out[k] = tuple(r[n] for r in rep)

        conv_mine = lax.dynamic_index_in_dim(conv_g.reshape(N_DEV, n_layers, 9, N_DEV, -1), me, axis=3, keepdims=False)
        res = adamw(flat2(ffn_conv_w), flat2(m_ffn_conv_w), flat2(v_ffn_conv_w),
                    conv_mine.reshape(N_DEV, -1, conv_mine.shape[-1]), "adamw_conv_w")
        out["ffn_conv_w"] = tuple(r.reshape(ffn_conv_w.shape) for r in res)

        out["ada_b"] = tuple(adamw(ada_b, m_ada_b, v_ada_b, jnp.concatenate([dmx_all, dmc_all], axis=0), "adamw_ada_b"))

        cols_of = lambda a: lax.dynamic_slice_in_dim(a, me * ada_cols, ada_cols, axis=2).transpose(1, 0, 2)
        d_ada_w, d_cctx = ada_bwd(c_all, cctx8, ada_w, ada_b_cols, cols_of(dmx_all), cols_of(dmc_all))
        res = adamw(flat2(ada_w), flat2(m_ada_w), flat2(v_ada_w), flat2(d_ada_w)[None], "adamw_ada_w")
        out["ada_w"] = tuple(r.reshape(ada_w.shape) for r in res)
        (d_cctx_all,) = all_gather([d_cctx], "gather_c_ctx_grad")
        res = adamw(c_ctx[None], m_c_ctx[None], v_c_ctx[None], d_cctx_all, "adamw_c_ctx")
        out["c_ctx"] = tuple(r[0] for r in res)
        return d_cctx_all

    _, grad_x, _, _, _, small_done = local_step(x[0], ctx[0], loss_target[0], mod, lb, w, fetch, publish, small_ready,
                                                small_early)
    loss = out["loss"]

    finish("in", small_done)
    return (loss, grad_x[None]) + tuple(out[k][n] for n in range(4) for k in WEIGHT_ORDER)
```

```python
import functools
import math

import jax
import jax.numpy as jnp
from jax import lax
from jax.experimental import pallas as pl
from jax.experimental.pallas import tpu as pltpu

F32 = jnp.float32
BF16 = jnp.bfloat16

N_DEV = 8
AXES = ("x", "y", "c")
D = 1024
CTX = 256
TM = 256
CH = 64
SGU_CH = 128
HEADS = 8
HD = 128
GRID_W = 64
D_IN = 9 * D
IN_SLOT = D_IN // N_DEV
D_FF = 2816
FF_SLOT = 2 * D_FF // N_DEV
N_FFK = D_FF // FF_SLOT
RMS_EPS = 1e-6
LN_EPS = 1e-5
ADAM_LR, ADAM_B1, ADAM_B2, ADAM_EPS, ADAM_WD, ADAM_STEP = 0.001, 0.9, 0.999, 1e-08, 0.01, 10
VMEM_LIMIT_V7X = 56 * 2 ** 20
ELEMENTWISE_VMEM = 24 * 2 ** 20
COPY_BLOCK_BYTES = 2 ** 21
GRAD_WIRE = jnp.bfloat16

VMEM_WHOLE = pl.BlockSpec(memory_space=pltpu.VMEM)
ANY = pl.BlockSpec(memory_space=pl.ANY)


def _cp(n_axes):
    return pltpu.CompilerParams(dimension_semantics=("arbitrary",) * n_axes, vmem_limit_bytes=VMEM_LIMIT_V7X)


def _dot(a, b, dims):
    return lax.dot_general(a.astype(BF16), b.astype(BF16), (dims, ((), ())), preferred_element_type=F32)


@jax.custom_vjp
def mm(a, b):
    return _dot(a, b, ((1,), (0,)))


mm.defvjp(lambda a, b: (mm(a, b), (a, b)),
          lambda r, g: (_dot(g, r[1], ((1,), (1,))).astype(r[0].dtype), _dot(r[0], g, ((0,), (0,))).astype(r[1].dtype)))


@jax.custom_vjp
def mm_nt(a, b):
    return _dot(a, b, ((1,), (1,)))


mm_nt.defvjp(lambda a, b: (mm_nt(a, b), (a, b)),
             lambda r, g: (_dot(g, r[1], ((1,), (0,))).astype(r[0].dtype), _dot(g, r[0], ((0,), (0,))).astype(r[1].dtype)))


@jax.custom_vjp
def mm_tn(a, b):
    return _dot(a, b, ((0,), (0,)))


mm_tn.defvjp(lambda a, b: (mm_tn(a, b), (a, b)),
             lambda r, g: (_dot(r[1], g, ((1,), (1,))).astype(r[0].dtype), _dot(r[0], g, ((1,), (0,))).astype(r[1].dtype)))


def _tri_dot(m, g):
    hi = g.astype(BF16)
    low = (g - hi.astype(F32)).astype(BF16)
    n = g.shape[1]
    out = jnp.dot(m.astype(BF16), jnp.concatenate([hi, low], axis=1), preferred_element_type=F32)
    return out[:, :n] + out[:, n:]


@jax.custom_vjp
def _cum(m, mt, g):
    return _tri_dot(m, g)


_cum.defvjp(lambda m, mt, g: (_cum(m, mt, g), (m, mt)),
            lambda r, d: (jnp.zeros_like(r[0]), jnp.zeros_like(r[1]), _tri_dot(r[1], d)))


def _silu(x):
    return x * jax.nn.sigmoid(x)


def _gelu(x):
    return 0.5 * x * (1.0 + jnp.tanh(math.sqrt(2.0 / math.pi) * (x + 0.044715 * (x * x * x))))


def _rms(x, w):
    return x * lax.rsqrt(jnp.mean(x * x, axis=-1, keepdims=True) + RMS_EPS) * w


def _norm_mod(x, w, shift, scale):
    return _rms(x, w) * (1.0 + scale) + shift


def _hsl(h):
    return slice(h * HD, (h + 1) * HD)


def _hgrn_chunk(st, qz, fz, iv, lb, m, mt, mref):
    hs = range(HEADS)
    keep = [1.0 - lb[h] for h in hs]
    sg = [jax.nn.sigmoid(fz[h]) for h in hs]
    g = [jnp.log(lb[h] + keep[h] * sg[h]) for h in hs]
    k = [keep[h] * (1.0 - sg[h]) for h in hs]
    q = [_silu(qz[h]) for h in hs]
    b = [_cum(m, mt, g[h]) for h in hs]
    ref = [jnp.sum(mref * g[h], axis=0, keepdims=True) for h in hs]
    last = [jnp.sum(g[h], axis=0, keepdims=True) for h in hs]
    qa = [q[h] * jnp.exp(b[h] - ref[h]) for h in hs]
    ka = [k[h] * jnp.exp(ref[h] - b[h]) for h in hs]
    scores = [jnp.where(m > 0.5, mm_nt(qa[h], ka[h]), 0.0) for h in hs]
    inter = [mm_nt(qa[h] * jnp.exp(ref[h]), st[h]) for h in hs]
    kv = [mm_tn(iv[h], ka[h] * jnp.exp(last[h] - ref[h])) for h in hs]
    outs = [mm(scores[h], iv[h]) + inter[h] for h in hs]
    news = [jnp.exp(last[h]) * st[h] + kv[h] for h in hs]
    return outs, news


def _sgu_fn(ub, vb, lnw, lnb, sw, sb):
    gv = [_gelu(v) for v in vb]
    mu = sum(jnp.sum(t, axis=-1, keepdims=True) for t in gv) / D
    var = sum(jnp.sum((t - mu) * (t - mu), axis=-1, keepdims=True) for t in gv) / D
    inv = lax.rsqrt(var + LN_EPS)
    cols = []
    for g in range(HEADS):
        vn = (gv[g] - mu) * inv * lnw[g] + lnb[g]
        cols.append(_gelu(ub[g]) * (mm(sw[g], vn) + sb[g]))
    return jnp.concatenate(cols, axis=1)


def _readout_fn(ob, og, hnw):
    r = [o * lax.rsqrt(jnp.mean(o * o, axis=-1, keepdims=True) + RMS_EPS) * hnw for o in ob]
    return jnp.concatenate(r, axis=1) * _silu(og)


def _glu_fn(ac, v):
    return _gelu(ac) * v


def _stream_row(tm):
    n_ctx = CTX // tm
    return lambda i: (jnp.where(i < n_ctx, 0, 1), 0, 0, 0)


def in_proj_fwd(x, mod, nw, wg):
    t = x.shape[0]

    def body(x_ref, mod_ref, nw_ref, w_ref, out_ref, ht_ref, iv_ref):
        h32 = _norm_mod(x_ref[...], nw_ref[...], mod_ref[0, 0], mod_ref[0, 1])
        ht_ref[...] = h32.T.astype(BF16)
        h = h32.astype(BF16)
        for j in range(N_DEV):
            out_ref[:, j * IN_SLOT:(j + 1) * IN_SLOT] = jnp.dot(h, w_ref[j], preferred_element_type=F32)
        iv_ref[...] = out_ref[:, 3 * D:4 * D].astype(BF16)

    return pl.pallas_call(
        body, name="in_proj_fwd", grid=(t // TM,),
        in_specs=[pl.BlockSpec((TM, D), lambda i: (i, 0)), pl.BlockSpec((1, 6, 1, D), _stream_row(TM)),
                  pl.BlockSpec((1, D), lambda i: (0, 0)), VMEM_WHOLE],
        out_specs=[pl.BlockSpec((TM, D_IN), lambda i: (i, 0)), pl.BlockSpec((D, TM), lambda i: (0, i)),
                   pl.BlockSpec((TM, D), lambda i: (i, 0))],
        out_shape=[jax.ShapeDtypeStruct((t, D_IN), F32), jax.ShapeDtypeStruct((D, t), BF16), jax.ShapeDtypeStruct((t, D), BF16)],
        compiler_params=_cp(1))(x, mod, nw, wg)


SCAN_STEP = 4
SCAN_ROWS = SCAN_STEP * CH


def _scan_block(nb):
    ncb = CTX // SCAN_ROWS

    def block(d, s):
        bwd = jnp.where(s < ncb, ncb - 1 - s, nb + ncb - 1 - s)
        return jnp.where(d == 0, s, bwd)
    return block


def hgrn_fwd(parts, iv, lb, mc, mtc, mrefc):
    t = parts.shape[0]
    nb = t // SCAN_ROWS
    block = _scan_block(nb)

    def body(q_ref, f_ref, i_ref, lb_ref, m_ref, mt_ref, mr_ref, o_ref, ck_ref, st):
        d = pl.program_id(0)

        @pl.when(pl.program_id(1) == 0)
        def _():
            st[...] = jnp.zeros_like(st)

        for c in range(SCAN_STEP):
            rows = pl.ds(pl.multiple_of(jnp.where(d == 0, c * CH, (SCAN_STEP - 1 - c) * CH), CH), CH)
            ck_ref[0, c] = st[...].astype(BF16)
            outs, news = _hgrn_chunk([st[h] for h in range(HEADS)], [q_ref[rows, _hsl(h)] for h in range(HEADS)],
                                     [f_ref[rows, _hsl(h)] for h in range(HEADS)], [i_ref[rows, _hsl(h)] for h in range(HEADS)],
                                     [lb_ref[0, :, _hsl(h)] for h in range(HEADS)], m_ref[0], mt_ref[0], mr_ref[0])
            for h in range(HEADS):
                o_ref[0, rows, _hsl(h)] = outs[h].astype(BF16)
                st[h] = news[h]

    const = lambda d, s: (d, 0, 0)
    at = lambda k: pl.BlockSpec((SCAN_ROWS, D), lambda d, s: (block(d, s), k(d)))
    return pl.pallas_call(
        body, name="hgrn_fwd", grid=(2, nb),
        in_specs=[at(lambda d: 0), at(lambda d: 1 + d), at(lambda d: 0), pl.BlockSpec((1, 1, D), const),
                  pl.BlockSpec((1, CH, CH), const), pl.BlockSpec((1, CH, CH), const), pl.BlockSpec((1, CH, 1), const)],
        out_specs=[pl.BlockSpec((1, SCAN_ROWS, D), lambda d, s: (d, block(d, s), 0)),
                   pl.BlockSpec((1, SCAN_STEP, HEADS, HD, HD), lambda d, s: (d, s, 0, 0, 0))],
        out_shape=[jax.ShapeDtypeStruct((2, t, D), BF16), jax.ShapeDtypeStruct((2, nb * SCAN_STEP, HEADS, HD, HD), BF16)],
        scratch_shapes=[pltpu.VMEM((HEADS, HD, HD), F32)], compiler_params=_cp(2))(parts, parts, iv, lb, mc, mtc, mrefc)


def _mixer_tile(rows, u_ref, v_ref, og_ref, o_ref, lnw_ref, lnb_ref, sw_ref, sb_ref, hnw_ref):
    n = (rows.stop - rows.start) // SGU_CH
    yas, vjps = [], []
    for c in range(n):
        r = slice(rows.start + c * SGU_CH, rows.start + (c + 1) * SGU_CH)
        ya, vjp_a = jax.vjp(_sgu_fn, [u_ref[r, _hsl(g)] for g in range(HEADS)], [v_ref[r, _hsl(g)] for g in range(HEADS)],
                            [lnw_ref[:, _hsl(g)] for g in range(HEADS)], [lnb_ref[:, _hsl(g)] for g in range(HEADS)],
                            [sw_ref[g] for g in range(HEADS)], [sb_ref[g] for g in range(HEADS)])
        yas.append(ya)
        vjps.append(vjp_a)
    yb, vjp_b = jax.vjp(_readout_fn, [o_ref[0, rows, _hsl(h)].astype(F32) + o_ref[1, rows, _hsl(h)].astype(F32)
                                      for h in range(HEADS)],
                        og_ref[rows, :], hnw_ref[...])
    return (yas[0] if n == 1 else jnp.concatenate(yas, axis=0)), yb, vjps, vjp_b


def _part_specs(tm, first, n):
    return [pl.BlockSpec((tm, D), functools.partial(lambda k, i: (i, k), first + k)) for k in range(n)]


def _unless_ctx(skip_ctx, is_ctx, zero_refs, work):
    if not skip_ctx:
        return work()

    @pl.when(is_ctx)
    def _():
        for r in zero_refs:
            r[...] = jnp.zeros_like(r)

    pl.when(jnp.logical_not(is_ctx))(work)


def mixer_fwd(x, parts, o, mod, lnw, lnb, sw, sb, hnw, wa, wb, wo, skip_ctx):
    t = x.shape[0]

    def body(x_ref, u_ref, v_ref, og_ref, ga_ref, gb_ref, o_ref, mod_ref, lnw_ref, lnb_ref, sw_ref, sb_ref, hnw_ref,
             wa_ref, wb_ref, wo_ref, out_ref, pa_ref, pb_ref, y_ref, yat_ref, ybt_ref, mt_ref):
        def work():
            ya, yb, _, _ = _mixer_tile(slice(0, TM), u_ref, v_ref, og_ref, o_ref, lnw_ref, lnb_ref, sw_ref, sb_ref, hnw_ref)
            pa, pb = mm(ya, wa_ref[...]), mm(yb, wb_ref[...])
            merged = jax.nn.sigmoid(ga_ref[...]) * pa + jax.nn.sigmoid(gb_ref[...]) * pb
            y = mm(merged, wo_ref[...])
            out_ref[...] = x_ref[...] + mod_ref[0, 2] * y
            pa_ref[...], pb_ref[...], y_ref[...] = pa.astype(BF16), pb.astype(BF16), y.astype(BF16)
            yat_ref[...], ybt_ref[...], mt_ref[...] = ya.T.astype(BF16), yb.T.astype(BF16), merged.T.astype(BF16)

        _unless_ctx(skip_ctx, pl.program_id(0) == 0, (out_ref, pa_ref, pb_ref, y_ref, yat_ref, ybt_ref, mt_ref), work)

    vec = lambda n: pl.BlockSpec((1, n), lambda i: (0, 0))
    tile = pl.BlockSpec((TM, D), lambda i: (i, 0))
    tile_t = pl.BlockSpec((D, TM), lambda i: (0, i))
    return pl.pallas_call(
        body, name="mixer_fwd", grid=(t // TM,),
        in_specs=[tile] + _part_specs(TM, 4, 5)
        + [pl.BlockSpec((2, TM, D), lambda i: (0, i, 0)), pl.BlockSpec((1, 6, 1, D), _stream_row(TM)), vec(D), vec(D),
           VMEM_WHOLE, VMEM_WHOLE, vec(HD), VMEM_WHOLE, VMEM_WHOLE, VMEM_WHOLE],
        out_specs=[tile] * 4 + [tile_t] * 3,
        out_shape=[jax.ShapeDtypeStruct((t, D), F32)] + [jax.ShapeDtypeStruct((t, D), BF16)] * 3
        + [jax.ShapeDtypeStruct((D, t), BF16)] * 3, compiler_params=_cp(1),
    )(x, parts, parts, parts, parts, parts, o, mod, lnw, lnb, sw, sb, hnw, wa, wb, wo)


def ffn_up_fwd(x, mod, nw, wg, skip_ctx):
    t = x.shape[0]

    def body(x_ref, mod_ref, nw_ref, w_ref, out_ref, ht_ref):
        def work():
            h32 = _norm_mod(x_ref[...], nw_ref[...], mod_ref[0, 3], mod_ref[0, 4])
            ht_ref[...] = h32.T.astype(BF16)
            h = h32.astype(BF16)
            for j in range(N_DEV):
                out_ref[j] = jnp.dot(h, w_ref[j], preferred_element_type=F32)

        _unless_ctx(skip_ctx, pl.program_id(0) == 0, (out_ref, ht_ref), work)

    return pl.pallas_call(
        body, name="ffn_up_fwd", grid=(t // TM,),
        in_specs=[pl.BlockSpec((TM, D), lambda i: (i, 0)), pl.BlockSpec((1, 6, 1, D), _stream_row(TM)),
                  pl.BlockSpec((1, D), lambda i: (0, 0)), VMEM_WHOLE],
        out_specs=[pl.BlockSpec((N_DEV, TM, FF_SLOT), lambda i: (0, i, 0)), pl.BlockSpec((D, TM), lambda i: (0, i))],
        out_shape=[jax.ShapeDtypeStruct((N_DEV, t, FF_SLOT), F32), jax.ShapeDtypeStruct((D, t), BF16)],
        compiler_params=_cp(1))(x, mod, nw, wg)


def _halo_specs(nt):
    per = TM // GRID_W
    last = nt * per - 1
    return [pl.BlockSpec((N_FFK, GRID_W, FF_SLOT), lambda i: (0, jnp.maximum(i * per - 1, 0), 0)),
            pl.BlockSpec((N_FFK, TM, FF_SLOT), lambda i: (0, i, 0)),
            pl.BlockSpec((N_FFK, GRID_W, FF_SLOT), lambda i: (0, jnp.minimum(i * per + per, last), 0))]


def _with_halo(prev_ref, main_ref, next_ref, k, i, nt):
    prev = jnp.where(i >= 2, prev_ref[k], 0.0)
    nxt = jnp.where((i >= 1) & (i <= nt - 2), next_ref[k], 0.0)
    return jnp.concatenate([prev, main_ref[k], nxt], axis=0)


def _tap_valid(dc, i, n_rows, offset):
    r = lax.broadcasted_iota(jnp.int32, (n_rows, 1), 0) - offset
    col = jnp.bitwise_and(r, GRID_W - 1)
    pos = jnp.where(i == 0, r, col) + dc
    return (pos >= 0) & (pos < jnp.where(i == 0, TM, GRID_W))


def _row_weight(cw_ref, k, dr, dc, i):
    w = cw_ref[k, 3 * (dr + 1) + dc + 1:3 * (dr + 1) + dc + 2, :]
    return w if dr == 0 else jnp.where(i == 0, 0.0, w)


def ffn_down_fwd(x, av, mod, cw, cb, wd, skip_ctx):
    t = x.shape[0]
    nt = t // TM

    def body(x_ref, ap_ref, am_ref, an_ref, v_ref, mod_ref, cw_ref, cb_ref, wd_ref, out_ref, ac_ref, y_ref, z_ref):
        i = pl.program_id(0)

        def work():
            y = None
            for k in range(N_FFK):
                a_ext = _with_halo(ap_ref, am_ref, an_ref, k, i, nt)
                conv = jnp.zeros((TM, FF_SLOT), F32) + cb_ref[k]
                for dc in (-1, 0, 1):
                    col = functools.reduce(lambda p, q: p + q, [a_ext[GRID_W + GRID_W * dr:GRID_W + GRID_W * dr + TM]
                                                                * _row_weight(cw_ref, k, dr, dc, i) for dr in (-1, 0, 1)])
                    conv = conv + (col if dc == 0 else
                                   jnp.where(_tap_valid(dc, i, TM, 0), pltpu.roll(col, (-dc) % TM, 0), 0.0))
                ac_ref[k] = conv.astype(BF16)
                z = _glu_fn(conv, v_ref[k]).astype(BF16)
                z_ref[k] = z
                part = mm(z, wd_ref[k])
                y = part if y is None else y + part
            y_ref[...] = y
            out_ref[...] = x_ref[...] + mod_ref[0, 5] * y

        _unless_ctx(skip_ctx, i == 0, (out_ref, ac_ref, y_ref, z_ref), work)

    tile = pl.BlockSpec((TM, D), lambda i: (i, 0))
    half = lambda first: pl.BlockSpec((N_FFK, TM, FF_SLOT), lambda i: (first, i, 0))
    return pl.pallas_call(
        body, name="ffn_down_fwd", grid=(nt,),
        in_specs=[tile] + _halo_specs(nt) + [half(1), pl.BlockSpec((1, 6, 1, D), _stream_row(TM)), VMEM_WHOLE, VMEM_WHOLE,
                                             VMEM_WHOLE],
        out_specs=[tile, half(0), tile, half(0)],
        out_shape=[jax.ShapeDtypeStruct((t, D), F32), jax.ShapeDtypeStruct((N_FFK, t, FF_SLOT), BF16),
                   jax.ShapeDtypeStruct((t, D), F32), jax.ShapeDtypeStruct((N_FFK, t, FF_SLOT), BF16)],
        compiler_params=_cp(1))(x, av, av, av, av, mod, cw, cb, wd)


def loss_fwd_bwd(x, target, fw):
    t = x.shape[0]

    def body(x_ref, t_ref, w_ref, loss_ref, dx_ref, dw_ref):
        i = pl.program_id(0)

        @pl.when(i == 0)
        def _():
            loss_ref[...] = jnp.zeros_like(loss_ref)
            dw_ref[...] = jnp.zeros_like(dw_ref)
            dx_ref[...] = jnp.zeros_like(dx_ref)

        @pl.when(i > 0)
        def _():
            y, vjp = jax.vjp(_rms, x_ref[...], w_ref[...])
            err = y - t_ref[...]
            loss_ref[...] += 0.5 * jnp.sum(jnp.sum(err * err, axis=-1, keepdims=True) / D)
            dx, dw = vjp(err / D)
            dx_ref[...] = dx
            dw_ref[...] += dw

    return pl.pallas_call(
        body, name="loss_fwd_bwd", grid=(t // TM,),
        in_specs=[pl.BlockSpec((TM, D), lambda i: (i, 0)), pl.BlockSpec((TM, D), lambda i: (jnp.maximum(i - 1, 0), 0)),
                  pl.BlockSpec((1, D), lambda i: (0, 0))],
        out_specs=[pl.BlockSpec((8, 128), lambda i: (0, 0)), pl.BlockSpec((TM, D), lambda i: (i, 0)),
                   pl.BlockSpec((1, D), lambda i: (0, 0))],
        out_shape=[jax.ShapeDtypeStruct((8, 128), F32), jax.ShapeDtypeStruct((t, D), F32), jax.ShapeDtypeStruct((1, D), F32)],
        compiler_params=_cp(1))(x, target, fw)


def _stream_add(ref, k, is_ctx, val):
    ref[0, k] += jnp.where(is_ctx, val, 0.0)
    ref[1, k] += jnp.where(is_ctx, 0.0, val)


def ffn_down_bwd(dx, ac, av, y, mod, wd, skip_ctx):
    t = dx.shape[0]
    nt = t // TM

    def body(dx_ref, ac_ref, v_ref, y_ref, mod_ref, wd_ref, dav_ref, dac_ref, dout_ref, dg_ref):
        i = pl.program_id(0)

        @pl.when(i == 0)
        def _():
            dg_ref[...] = jnp.zeros_like(dg_ref)

        def work():
            _stream_add(dg_ref, 0, i == 0, jnp.sum(dx_ref[...] * y_ref[...], axis=0, keepdims=True))
            dout = (mod_ref[0, 5] * dx_ref[...]).astype(BF16)
            dout_ref[...] = dout
            for k in range(N_FFK):
                _, vjp = jax.vjp(_glu_fn, ac_ref[k].astype(F32), v_ref[k])
                dac, dv = vjp(mm_nt(dout, wd_ref[k]))
                dac_ref[k] = dac
                dav_ref[k] = dv.astype(BF16)

        _unless_ctx(skip_ctx, i == 0, (dav_ref, dac_ref, dout_ref), work)

    tile = pl.BlockSpec((TM, D), lambda i: (i, 0))
    half = lambda first: pl.BlockSpec((N_FFK, TM, FF_SLOT), lambda i: (first, i, 0))
    return pl.pallas_call(
        body, name="ffn_down_bwd", grid=(nt,),
        in_specs=[tile, half(0), half(1), tile, pl.BlockSpec((1, 6, 1, D), _stream_row(TM)), VMEM_WHOLE],
        out_specs=[half(1), half(0), tile, pl.BlockSpec((2, 1, 1, D), lambda i: (0, 0, 0, 0))],
        out_shape=[jax.ShapeDtypeStruct((N_DEV, t, FF_SLOT), BF16), jax.ShapeDtypeStruct((N_FFK, t, FF_SLOT), F32),
                   jax.ShapeDtypeStruct((t, D), BF16), jax.ShapeDtypeStruct((2, 1, 1, D), F32)],
        compiler_params=_cp(1))(dx, ac, av, y, mod, wd)


def conv_bwd(dav, dac, av, cw, skip_ctx):
    t = dac.shape[1]
    nt = t // TM

    def body(dav_in, gp_ref, gm_ref, gn_ref, ap_ref, am_ref, an_ref, cw_ref, dav_ref, dcw_ref, dcb_ref):
        i = pl.program_id(0)

        @pl.when(i == 0)
        def _():
            dcw_ref[...] = jnp.zeros_like(dcw_ref)
            dcb_ref[...] = jnp.zeros_like(dcb_ref)

        def work():
            for k in range(N_FFK):
                g_ext = _with_halo(gp_ref, gm_ref, gn_ref, k, i, nt)
                a_ext = _with_halo(ap_ref, am_ref, an_ref, k, i, nt)
                g_main = gm_ref[k]
                dcb_ref[k] += jnp.sum(g_main, axis=0, keepdims=True)
                da = jnp.zeros((TM, FF_SLOT), F32)
                for dc in (-1, 0, 1):
                    valid = _tap_valid(dc, i, TM, 0)
                    q = functools.reduce(lambda p, r: p + r, [g_ext[GRID_W - GRID_W * dr:GRID_W - GRID_W * dr + TM]
                                                              * _row_weight(cw_ref, k, dr, dc, i) for dr in (-1, 0, 1)])
                    da = da + (q if dc == 0 else pltpu.roll(jnp.where(valid, q, 0.0), dc % TM, 0))
                    g_shift = g_main if dc == 0 else pltpu.roll(jnp.where(valid, g_main, 0.0), dc % TM, 0)
                    for dr in (-1, 0, 1):
                        lo = GRID_W + GRID_W * dr
                        tap = 3 * (dr + 1) + dc + 1
                        dw = jnp.sum(g_shift * a_ext[lo:lo + TM], axis=0, keepdims=True)
                        dcw_ref[k, tap:tap + 1, :] += dw if dr == 0 else jnp.where(i == 0, 0.0, dw)
                dav_ref[k] = da.astype(BF16)

        _unless_ctx(skip_ctx, i == 0, (dav_ref,), work)

    whole = lambda rows: pl.BlockSpec((N_FFK, rows, FF_SLOT), lambda i: (0, 0, 0))
    return pl.pallas_call(
        body, name="conv_bwd", grid=(nt,),
        in_specs=[ANY] + _halo_specs(nt) + _halo_specs(nt) + [VMEM_WHOLE],
        out_specs=[pl.BlockSpec((N_FFK, TM, FF_SLOT), lambda i: (0, i, 0)), whole(9), whole(1)],
        out_shape=[jax.ShapeDtypeStruct(dav.shape, BF16), jax.ShapeDtypeStruct((N_FFK, 9, FF_SLOT), F32),
                   jax.ShapeDtypeStruct((N_FFK, 1, FF_SLOT), F32)],
        input_output_aliases={0: 0}, compiler_params=_cp(1))(dav, dac, dac, dac, av, av, av, cw)


def _norm_mod_bwd(x_ref, nw_ref, mod_ref, k_shift, dh, dx_in, dx_ref, dnw_ref, dmod_ref, is_ctx):
    _, vjp = jax.vjp(_norm_mod, x_ref[...], nw_ref[...], mod_ref[0, k_shift], mod_ref[0, k_shift + 1])
    dx, dnw, dshift, dscale = vjp(dh)
    dx_ref[...] = dx_in + dx
    dnw_ref[...] += dnw
    _stream_add(dmod_ref, 0, is_ctx, dshift)
    _stream_add(dmod_ref, 1, is_ctx, dscale)


def ffn_up_bwd_x(dx2, x, dav, mod, nw, wg, skip_ctx):
    t = x.shape[0]

    def body(dx2_ref, x_ref, dav_ref, mod_ref, nw_ref, w_ref, dx_ref, dnw_ref, dmod_ref):
        i = pl.program_id(0)

        @pl.when(i == 0)
        def _():
            dnw_ref[...] = jnp.zeros_like(dnw_ref)
            dmod_ref[...] = jnp.zeros_like(dmod_ref)

        def work():
            dh = mm_nt(dav_ref[0], w_ref[0])
            for j in range(1, N_DEV):
                dh = dh + mm_nt(dav_ref[j], w_ref[j])
            _norm_mod_bwd(x_ref, nw_ref, mod_ref, 3, dh, dx2_ref[...], dx_ref, dnw_ref, dmod_ref, i == 0)

        _unless_ctx(skip_ctx, i == 0, (dx_ref,), work)

    tile = pl.BlockSpec((TM, D), lambda i: (i, 0))
    return pl.pallas_call(
        body, name="ffn_up_bwd_x", grid=(t // TM,),
        in_specs=[tile, tile, pl.BlockSpec((N_DEV, TM, FF_SLOT), lambda i: (0, i, 0)), pl.BlockSpec((1, 6, 1, D), _stream_row(TM)),
                  pl.BlockSpec((1, D), lambda i: (0, 0)), VMEM_WHOLE],
        out_specs=[tile, pl.BlockSpec((1, D), lambda i: (0, 0)), pl.BlockSpec((2, 2, 1, D), lambda i: (0, 0, 0, 0))],
        out_shape=[jax.ShapeDtypeStruct((t, D), F32), jax.ShapeDtypeStruct((1, D), F32), jax.ShapeDtypeStruct((2, 2, 1, D), F32)],
        compiler_params=_cp(1))(dx2, x, dav, mod, nw, wg)


def weight_grad(at, dout, slot, name, after=None):
    rows, t = at.shape
    stacked = dout.ndim == 3
    n = dout.shape[0] if stacked else dout.shape[1] // slot

    def body(a_ref, d_ref, *rest):
        dw_ref = rest[-1]
        dw_ref[0] = jnp.dot(a_ref[...], d_ref[0] if stacked else d_ref[...], preferred_element_type=F32).astype(dw_ref.dtype)

    d_spec = pl.BlockSpec((1, t, slot), lambda j: (j, 0, 0)) if stacked else pl.BlockSpec((t, slot), lambda j: (0, j))
    extra = [] if after is None else [jnp.reshape(after, (1, 1))]
    return pl.pallas_call(
        body, name=name, grid=(n,), in_specs=[VMEM_WHOLE, d_spec] + [ANY] * len(extra),
        out_specs=pl.BlockSpec((1, rows, slot), lambda j: (j, 0, 0)),
        out_shape=jax.ShapeDtypeStruct((n, rows, slot), GRAD_WIRE), compiler_params=_cp(1))(at, dout, *extra)


def weight_grad_rows(at, dout, name):
    n, t, rows = at.shape
    cols = dout.shape[1]

    def body(a_ref, d_ref, dw_ref):
        dw_ref[0] = _dot(a_ref[0], d_ref[...], ((0,), (0,))).astype(dw_ref.dtype)

    return pl.pallas_call(
        body, name=name, grid=(n,), in_specs=[pl.BlockSpec((1, t, rows), lambda k: (k, 0, 0)), VMEM_WHOLE],
        out_specs=pl.BlockSpec((1, rows, cols), lambda k: (k, 0, 0)),
        out_shape=jax.ShapeDtypeStruct((n, rows, cols), GRAD_WIRE), compiler_params=_cp(1))(at, dout)


def mixer_bwd(dx, parts, o, pa, pb, y, mod, lnw, lnb, sw, sb, hnw, wa, wb, wo, skip_ctx):
    t = dx.shape[0]
    tm = TM
    n_ctx = CTX // tm

    def body(dx_ref, u_ref, v_ref, og_ref, ga_ref, gb_ref, o_ref, pa_ref, pb_ref, y_ref, mod_ref, lnw_ref, lnb_ref, sw_ref,
             sb_ref, hnw_ref, wa_ref, wb_ref, wo_ref, dp_ref, do_ref, dy_ref, dpa_ref, dpb_ref, dlnw_ref, dlnb_ref, dsw_ref,
             dsb_ref, dhnw_ref, dg_ref):
        i = pl.program_id(0)

        @pl.when(i == 0)
        def _():
            for r in (dlnw_ref, dlnb_ref, dsw_ref, dsb_ref, dhnw_ref, dg_ref):
                r[...] = jnp.zeros_like(r)

        def work():
            _, _, vjps, vjp_b = _mixer_tile(slice(0, tm), u_ref, v_ref, og_ref, o_ref, lnw_ref, lnb_ref, sw_ref, sb_ref, hnw_ref)
            pa, pb = pa_ref[...].astype(F32), pb_ref[...].astype(F32)
            sa, sbg = jax.nn.sigmoid(ga_ref[...]), jax.nn.sigmoid(gb_ref[...])
            dxv = dx_ref[...]
            _stream_add(dg_ref, 0, i < n_ctx, jnp.sum(dxv * y_ref[...].astype(F32), axis=0, keepdims=True))
            dy = (mod_ref[0, 2] * dxv).astype(BF16)
            dy_ref[...] = dy
            dmerged = mm_nt(dy, wo_ref[...])
            dpa, dpb = (sa * dmerged).astype(BF16), (sbg * dmerged).astype(BF16)
            dpa_ref[...], dpb_ref[...] = dpa, dpb
            first = 4 * D
            dp_ref[:, first + 3 * D:first + 4 * D] = (dmerged * pa * sa * (1.0 - sa)).astype(BF16)
            dp_ref[:, first + 4 * D:first + 5 * D] = (dmerged * pb * sbg * (1.0 - sbg)).astype(BF16)
            dya = mm_nt(dpa, wa_ref[...])
            dob, dog, dhnw = vjp_b(mm_nt(dpb, wb_ref[...]))
            dp_ref[:, first + 2 * D:first + 3 * D] = dog.astype(BF16)
            dhnw_ref[...] += dhnw
            for g in range(HEADS):
                do_ref[:, _hsl(g)] = dob[g]
            for c, vjp_a in enumerate(vjps):
                rows = slice(c * SGU_CH, (c + 1) * SGU_CH)
                dub, dvb, dlnw, dlnb, dsw, dsb = vjp_a(dya[rows])
                for g in range(HEADS):
                    dp_ref[rows, first + g * HD:first + (g + 1) * HD] = dub[g].astype(BF16)
                    dp_ref[rows, first + D + g * HD:first + D + (g + 1) * HD] = dvb[g].astype(BF16)
                    dlnw_ref[:, _hsl(g)] += dlnw[g]
                    dlnb_ref[:, _hsl(g)] += dlnb[g]
                    dsw_ref[g] += dsw[g]
                    dsb_ref[g] += dsb[g]

        _unless_ctx(skip_ctx, i < n_ctx, (dp_ref, do_ref, dy_ref, dpa_ref, dpb_ref), work)

    vec = lambda n: pl.BlockSpec((1, n), lambda i: (0, 0))
    tile = pl.BlockSpec((tm, D), lambda i: (i, 0))
    sds = jax.ShapeDtypeStruct
    return pl.pallas_call(
        body, name="mixer_bwd", grid=(t // tm,),
        in_specs=[tile] + _part_specs(tm, 4, 5)
        + [pl.BlockSpec((2, tm, D), lambda i: (0, i, 0)), tile, tile, tile, pl.BlockSpec((1, 6, 1, D), _stream_row(tm)),
           vec(D), vec(D), VMEM_WHOLE, VMEM_WHOLE, vec(HD), VMEM_WHOLE, VMEM_WHOLE, VMEM_WHOLE],
        out_specs=[pl.BlockSpec((tm, D_IN), lambda i: (i, 0)), tile, tile, tile, tile, vec(D), vec(D),
                   VMEM_WHOLE, VMEM_WHOLE, vec(HD), pl.BlockSpec((2, 1, 1, D), lambda i: (0, 0, 0, 0))],
        out_shape=[sds((t, D_IN), BF16), sds((t, D), F32), sds((t, D), BF16), sds((t, D), BF16), sds((t, D), BF16),
                   sds((1, D), F32), sds((1, D), F32), sds((HEADS, SGU_CH, SGU_CH), F32), sds((HEADS, SGU_CH, 1), F32),
                   sds((1, HD), F32), sds((2, 1, 1, D), F32)],
        compiler_params=_cp(1))(dx, parts, parts, parts, parts, parts, o, pa, pb, y, mod, lnw, lnb, sw, sb, hnw, wa, wb, wo)


def hgrn_bwd(d, parts, lb, mc, mtc, mrefc, ck, do, first=None, dparts=None):
    t = parts.shape[0]
    nb = t // SCAN_ROWS
    block = _scan_block(nb)
    rev = lambda s: block(d, nb - 1 - s)

    def body(q_ref, f_ref, i_ref, lb_ref, m_ref, mt_ref, mr_ref, ck_ref, do_ref, *rest):
        dst = rest[-1]
        dlb_ref = rest[-2]

        @pl.when(pl.program_id(0) == 0)
        def _():
            dst[...] = jnp.zeros_like(dst)
            dlb_ref[...] = jnp.zeros_like(dlb_ref)

        heads = range(HEADS)
        fn = functools.partial(_hgrn_chunk, m=m_ref[0], mt=mt_ref[0], mref=mr_ref[0])
        for c in reversed(range(SCAN_STEP)):
            first_row = c * CH if d == 0 else (SCAN_STEP - 1 - c) * CH
            rows = slice(first_row, first_row + CH)
            _, vjp = jax.vjp(fn, [ck_ref[0, c, h].astype(F32) for h in heads], [q_ref[rows, _hsl(h)] for h in heads],
                             [f_ref[rows, _hsl(h)] for h in heads], [i_ref[rows, _hsl(h)] for h in heads],
                             [lb_ref[0, :, _hsl(h)] for h in heads])
            dstl, dq, df, di, dlb = vjp(([do_ref[rows, _hsl(h)] for h in heads], [dst[h] for h in heads]))
            for h in heads:
                dst[h] = dstl[h]
                dlb_ref[0, :, _hsl(h)] += dlb[h]
                if d == 0:
                    dq_ref, df_ref, di_ref = rest[:3]
                    dq_ref[rows, _hsl(h)] = dq[h].astype(BF16)
                    df_ref[rows, _hsl(h)] = df[h].astype(BF16)
                    di_ref[rows, _hsl(h)] = di[h].astype(BF16)
                else:
                    dq0_ref, df0_ref, di0_ref, _, dp_ref = rest[:5]
                    col = lambda k: slice(k * D + h * HD, k * D + (h + 1) * HD)
                    dp_ref[rows, col(0)] = (dq0_ref[rows, _hsl(h)].astype(F32) + dq[h]).astype(BF16)
                    dp_ref[rows, col(1)] = df0_ref[rows, _hsl(h)]
                    dp_ref[rows, col(2)] = df[h].astype(BF16)
                    dp_ref[rows, col(3)] = (di0_ref[rows, _hsl(h)].astype(F32) + di[h]).astype(BF16)

    const = lambda s: (d, 0, 0)
    at = lambda k: pl.BlockSpec((SCAN_ROWS, D), lambda s: (rev(s), k))
    in_specs = [at(0), at(1 + d), at(3), pl.BlockSpec((1, 1, D), const), pl.BlockSpec((1, CH, CH), const),
                pl.BlockSpec((1, CH, CH), const), pl.BlockSpec((1, CH, 1), const),
                pl.BlockSpec((1, SCAN_STEP, HEADS, HD, HD), lambda s: (d, nb - 1 - s, 0, 0, 0)), at(0)]
    dlb_spec, dlb_shape = pl.BlockSpec((1, 1, D), lambda s: (0, 0, 0)), jax.ShapeDtypeStruct((1, 1, D), F32)
    common = dict(grid=(nb,), scratch_shapes=[pltpu.VMEM((HEADS, HD, HD), F32)], compiler_params=_cp(1))
    if d == 0:
        return pl.pallas_call(body, name="hgrn_bwd_fwd_dir", in_specs=in_specs, out_specs=[at(0)] * 3 + [dlb_spec],
                              out_shape=[jax.ShapeDtypeStruct((t, D), BF16)] * 3 + [dlb_shape], **common,
                              )(parts, parts, parts, lb, mc, mtc, mrefc, ck, do)
    return pl.pallas_call(body, name="hgrn_bwd_bwd_dir", in_specs=in_specs + [at(0)] * 3 + [ANY],
                          out_specs=[pl.BlockSpec((SCAN_ROWS, 4 * D), lambda s: (rev(s), 0)), dlb_spec],
                          out_shape=[jax.ShapeDtypeStruct(dparts.shape, BF16), dlb_shape], input_output_aliases={12: 0},
                          **common)(parts, parts, parts, lb, mc, mtc, mrefc, ck, do, *first, dparts)


def in_proj_bwd_x(dx1, x, dparts, mod, nw, wg, after=None, latent_only=False):
    t = x.shape[0]
    tm = TM
    n_ctx = CTX // tm

    def body(dx1_ref, x_ref, dp_ref, mod_ref, nw_ref, w_ref, *rest):
        dx_ref, dnw_ref, dmod_ref = rest[-3:]
        i = pl.program_id(0)

        @pl.when(i == 0)
        def _():
            dnw_ref[...] = jnp.zeros_like(dnw_ref)
            dmod_ref[...] = jnp.zeros_like(dmod_ref)

        dh = mm_nt(dp_ref[:, 0:IN_SLOT], w_ref[0])
        for j in range(1, N_DEV):
            dh = dh + mm_nt(dp_ref[:, j * IN_SLOT:(j + 1) * IN_SLOT], w_ref[j])
        _norm_mod_bwd(x_ref, nw_ref, mod_ref, 0, dh, dx1_ref[...], dx_ref, dnw_ref, dmod_ref, i < n_ctx)

    tile = pl.BlockSpec((tm, D), lambda i: (i, 0))
    extra = [] if after is None else [jnp.reshape(after, (1, 1))]
    return pl.pallas_call(
        body, name="in_proj_bwd_x", grid=(t // tm,),
        in_specs=[tile, tile, pl.BlockSpec((tm, D_IN), lambda i: (i, 0)), pl.BlockSpec((1, 6, 1, D), _stream_row(tm)),
                  pl.BlockSpec((1, D), lambda i: (0, 0)), VMEM_WHOLE] + [ANY] * len(extra),
        out_specs=[pl.BlockSpec((tm, D), lambda i: (jnp.maximum(i - n_ctx, 0), 0)) if latent_only else tile,
                   pl.BlockSpec((1, D), lambda i: (0, 0)), pl.BlockSpec((2, 2, 1, D), lambda i: (0, 0, 0, 0))],
        out_shape=[jax.ShapeDtypeStruct((t - CTX if latent_only else t, D), F32), jax.ShapeDtypeStruct((1, D), F32),
                   jax.ShapeDtypeStruct((2, 2, 1, D), F32)],
        compiler_params=_cp(1))(dx1, x, dparts, mod, nw, wg, *extra)


def _lb_fn(h0, h1):
    m = jnp.maximum(h0, h1)
    e0, e1 = jnp.exp(h0 - m), jnp.exp(h1 - m)
    return e1 / (e0 + e1)


def lower_bounds(hlb):
    def body(h_ref, out_ref):
        out_ref[...] = _lb_fn(h_ref[0:1, :], h_ref[1:2, :])
    return pl.pallas_call(body, name="lower_bounds", out_shape=jax.ShapeDtypeStruct((1, 2 * D), F32))(hlb)


def lower_bounds_bwd(hlb, dlb1):
    def body(h_ref, d_ref, out_ref):
        _, vjp = jax.vjp(_lb_fn, h_ref[0:1, :], h_ref[1:2, :])
        d0, d1 = vjp(d_ref[...])
        out_ref[0:1, :] = d0
        out_ref[1:2, :] = d1
    return pl.pallas_call(body, name="lower_bounds_bwd", out_shape=jax.ShapeDtypeStruct((2, 2 * D), F32))(hlb, dlb1)


def _ada_fn(c_all, cctx8, w, b):
    dot = lambda a, l: mm(_silu(a), w[l]) + b[l]
    return [dot(c_all, l) for l in range(2)], [dot(cctx8, l) for l in range(2)]


def ada_fwd(c_all, cctx8, w, b):
    cols = w.shape[-1]

    def body(c_ref, cc_ref, w_ref, b_ref, out_ref):
        ox, oc = _ada_fn(c_ref[...], cc_ref[...], [w_ref[0], w_ref[1]], [b_ref[0], b_ref[1]])
        for l in range(2):
            out_ref[l, 0] = ox[l]
            out_ref[l, 1] = oc[l]
    return pl.pallas_call(body, name="ada_fwd", out_shape=jax.ShapeDtypeStruct((2, 2, N_DEV, cols), F32),
                          compiler_params=_cp(0))(c_all, cctx8, w, b)


def ada_bwd(c_all, cctx8, w, b, dmx, dmc):
    cols = w.shape[-1]

    def body(c_ref, cc_ref, w_ref, b_ref, dmx_ref, dmc_ref, dw_ref, dc_ref):
        fn = lambda cc, w0, w1: _ada_fn(c_ref[...], cc, [w0, w1], [b_ref[0], b_ref[1]])
        _, vjp = jax.vjp(fn, cc_ref[...], w_ref[0], w_ref[1])
        dcc, dw0, dw1 = vjp(([dmx_ref[0], dmx_ref[1]], [dmc_ref[0], dmc_ref[1]]))
        dw_ref[0] = dw0
        dw_ref[1] = dw1
        dc_ref[...] = jnp.sum(dcc, axis=0, keepdims=True)
    return pl.pallas_call(body, name="ada_bwd", out_shape=[jax.ShapeDtypeStruct((2, D, cols), F32), jax.ShapeDtypeStruct((1, D), F32)],
                          compiler_params=_cp(0))(c_all, cctx8, w, b, dmx, dmc)


def adamw(w, m, v, gparts, name):
    r, c = w.shape
    p = gparts.shape[0]
    rt = r
    while rt % 16 == 0 and (p + 7) * rt * c * 4 * 2 > ELEMENTWISE_VMEM:
        rt //= 2

    def body(w_ref, m_ref, v_ref, g_ref, go_ref, d_ref, mo_ref, vo_ref):
        g = g_ref[0].astype(F32)
        for k in range(1, p):
            g = g + g_ref[k].astype(F32)
        m2 = ADAM_B1 * m_ref[...] + (1.0 - ADAM_B1) * g
        v2 = ADAM_B2 * v_ref[...] + (1.0 - ADAM_B2) * (g * g)
        m_hat = m2 / (1.0 - ADAM_B1 ** ADAM_STEP)
        v_hat = v2 / (1.0 - ADAM_B2 ** ADAM_STEP)
        go_ref[...] = g
        d_ref[...] = -ADAM_LR * (m_hat / (jnp.sqrt(v_hat) + ADAM_EPS) + ADAM_WD * w_ref[...])
        mo_ref[...] = m2
        vo_ref[...] = v2

    tile = pl.BlockSpec((rt, c), lambda i: (i, 0))
    return pl.pallas_call(
        body, name=name, grid=(r // rt,),
        in_specs=[tile, tile, tile, pl.BlockSpec((p, rt, c), lambda i: (0, i, 0))], out_specs=[tile] * 4,
        out_shape=[jax.ShapeDtypeStruct((r, c), F32)] * 4, compiler_params=_cp(1))(w, m, v, gparts)


def _me():
    x, y, c = lax.axis_index("x"), lax.axis_index("y"), lax.axis_index("c")
    return x, y, c, 4 * x + 2 * y + c


def _peer(x, y, c, p):
    fx, fy, fc = (p >> 2) & 1, (p >> 1) & 1, p & 1
    return (1 - x if fx else x, 1 - y if fy else y, 1 - c if fc else c)


def all_gather(arrs, name, after=None):
    n = len(arrs)
    extra = [] if after is None else list(after) if isinstance(after, (list, tuple)) else [after]

    def body(*refs):
        ins, outs = refs[:n], refs[n + len(extra):2 * n + len(extra)]
        send, recv, local = refs[2 * n + len(extra):]
        x, y, c, me = _me()
        copies = []
        for a in range(n):
            lc = pltpu.make_async_copy(ins[a], outs[a].at[me], local.at[a])
            lc.start()
            copies.append(lc)
            for p in range(1, N_DEV):
                cp = pltpu.make_async_remote_copy(src_ref=ins[a], dst_ref=outs[a].at[me], send_sem=send.at[a, p - 1],
                                                  recv_sem=recv.at[a, p - 1], device_id=_peer(x, y, c, p),
                                                  device_id_type=pl.DeviceIdType.MESH)
                cp.start()
                copies.append(cp)
        for cp in copies:
            cp.wait()

    return pl.pallas_call(
        body, name=name, in_specs=[ANY] * (n + len(extra)), out_specs=[ANY] * n,
        out_shape=[jax.ShapeDtypeStruct((N_DEV,) + a.shape, a.dtype) for a in arrs],
        scratch_shapes=[pltpu.SemaphoreType.DMA((n, N_DEV - 1)), pltpu.SemaphoreType.DMA((n, N_DEV - 1)),
                        pltpu.SemaphoreType.DMA((n,))])(*arrs, *extra)


HBM = pl.BlockSpec(memory_space=pltpu.HBM)
SEM = pl.BlockSpec(memory_space=pltpu.SEMAPHORE)


def _in_hbm(a):
    return pltpu.with_memory_space_constraint(a, pltpu.HBM)


ALL_PEERS = tuple(range(1, N_DEV))
SAME_CORE_AND_SIBLING = (1, 2, 4, 6)
OTHER_CHIPS = (2, 4, 6)


def _exchange_refs(srcs, lands, layer, scatter, a, x, y, c, p, forward=False):
    me = 4 * x + 2 * y + c
    px, py, pc = _peer(x, y, c, p) if p else (x, y, c)
    if forward and p:
        slot = lands[a].at[4 * px + 2 * py + pc]
        return slot, slot, _peer(x, y, c, 1)
    dst = lands[a].at[me] if layer is None else lands[a].at[me, layer]
    src = srcs[a].at[4 * px + 2 * py + pc] if scatter else dst
    return src, dst, (px, py, pc)


def exchange_start(srcs, lands, layer, scatter, name, after=None, peers=ALL_PEERS, forward=False):
    n, ns = len(lands), len(srcs)
    extra = [] if after is None else [after]

    def body(*refs):
        ins, lz = refs[:ns], refs[ns:ns + n]
        send, recv = refs[ns + n + len(extra)], refs[ns + n + len(extra) + 1]
        token = refs[-1]
        x, y, c, _ = _me()
        for a in range(n):
            for p in peers:
                src, dst, peer = _exchange_refs(ins, lz, layer, scatter, a, x, y, c, p, forward)
                k = a * (N_DEV - 1) + p - 1
                pltpu.make_async_remote_copy(src_ref=src, dst_ref=dst, send_sem=send.at[k], recv_sem=recv.at[k],
                                             device_id=peer, device_id_type=pl.DeviceIdType.MESH).start()
        token[...] = jnp.zeros_like(token)

    thru = [pltpu.HBM(a.shape, a.dtype) for a in list(srcs) + list(lands)]
    out = pl.pallas_call(
        body, name=name, in_specs=[HBM] * (ns + n) + [ANY] * len(extra),
        out_specs=[SEM, SEM] + [HBM] * (ns + n) + [pl.BlockSpec(memory_space=pltpu.VMEM)],
        out_shape=[pltpu.SemaphoreType.DMA((n * (N_DEV - 1),)), pltpu.SemaphoreType.DMA((n * (N_DEV - 1),))] + thru
        + [jax.ShapeDtypeStruct((8, 128), F32)],
        input_output_aliases={i: 2 + i for i in range(ns + n)},
        compiler_params=pltpu.CompilerParams(has_side_effects=pltpu.SideEffectType.DATAFLOW_SIDE_EFFECTING),
    )(*[_in_hbm(a) for a in list(srcs) + list(lands)], *extra)
    return out[0], out[1], out[2:2 + ns], out[2 + ns:2 + ns + n], out[-1]


def exchange_wait(send, recv, srcs, lands, layer, scatter, after, name, peers=ALL_PEERS):
    n, ns = len(lands), len(srcs)

    def body(*refs):
        ins, lz = refs[:ns], refs[ns:ns + n]
        send_ref, recv_ref = refs[ns + n], refs[ns + n + 1]
        x, y, c, _ = _me()
        for a in range(n):
            for p in peers:
                src, dst, peer = _exchange_refs(ins, lz, layer, scatter, a, x, y, c, 0)
                k = a * (N_DEV - 1) + p - 1
                cp = pltpu.make_async_remote_copy(src_ref=src, dst_ref=dst, send_sem=send_ref.at[k],
                                                  recv_sem=recv_ref.at[k], device_id=peer,
                                                  device_id_type=pl.DeviceIdType.MESH)
                cp.wait_send()
                cp.wait_recv()

    thru = [pltpu.HBM(a.shape, a.dtype) for a in list(srcs) + list(lands)]
    out = pl.pallas_call(
        body, name=name, in_specs=[HBM] * (ns + n) + [SEM, SEM, ANY], out_specs=[HBM] * (ns + n), out_shape=thru,
        input_output_aliases={i: i for i in range(ns + n)},
        compiler_params=pltpu.CompilerParams(has_side_effects=pltpu.SideEffectType.DATAFLOW_SIDE_EFFECTING),
    )(*srcs, *lands, send, recv, after)
    return out[ns:]


def place_own(src, land, me, layer, scatter, name, src_layer=None, after=None):
    create = isinstance(land, jax.ShapeDtypeStruct)
    r, c = src.shape[-2:]
    rt = r
    while rt % 32 == 0 and rt * c * 4 > COPY_BLOCK_BYTES:
        rt //= 2

    extra = [] if after is None else [after]

    def body(me_ref, src_ref, *rest):
        out_ref = rest[-1]
        out_ref[...] = src_ref[...].reshape(out_ref.shape).astype(out_ref.dtype)

    src_spec = (pl.BlockSpec((1, rt, c), lambda i, m: (m[0], i, 0)) if scatter else
                pl.BlockSpec((rt, c), lambda i, m: (i, 0)) if src_layer is None else
                pl.BlockSpec((1, rt, c), lambda i, m: (src_layer, i, 0)))
    out_spec = (pl.BlockSpec((1, rt, c), lambda i, m: (m[0], i, 0)) if layer is None
                else pl.BlockSpec((1, 1, rt, c), lambda i, m: (m[0], layer, i, 0)))
    grid_spec = pltpu.PrefetchScalarGridSpec(num_scalar_prefetch=1, grid=(r // rt,),
                                             in_specs=[src_spec] + ([] if create else [ANY]) + [ANY] * len(extra),
                                             out_specs=out_spec)
    return pl.pallas_call(body, name=name, grid_spec=grid_spec, out_shape=jax.ShapeDtypeStruct(land.shape, land.dtype),
                          input_output_aliases={} if create else {2: 0}, compiler_params=_cp(1),
                          )(*((me, src) if create else (me, src, land)), *extra)


def _scan_constants():
    r = lax.broadcasted_iota(jnp.int32, (CH, CH), 0)
    s = lax.broadcasted_iota(jnp.int32, (CH, CH), 1)
    lower = (s <= r).astype(F32)
    t = jnp.arange(CH)[:, None]
    mc = jnp.stack([lower, lower.T])
    mref = jnp.stack([(t <= CH // 2 - 1).astype(F32), (t >= CH // 2).astype(F32)])
    return mc, jnp.stack([lower.T, lower]), mref


def local_step(x, ctx, target, mod, lb, w, fetch=None, publish=None, small_ready=None, small_early=None):
    kept = {}

    def keep(l, part, grads):
        kept[(l, part)] = grads
        return 0.0

    fetch = fetch or (lambda l, part, after: w)
    publish = publish or keep
    n_layers = len(mod)
    mc, mtc, mrefc = _scan_constants()
    xs = jnp.concatenate([ctx, x], axis=0)
    saved, big = [], []
    for l in range(n_layers):
        wl = dict(fetch(l, "in", xs))
        parts, ht, iv = in_proj_fwd(xs, mod[l], w["nw1"][l], wl["win"][l])
        o, ck = hgrn_fwd(parts, iv, lb[l], mc, mtc, mrefc)
        wl.update(fetch(l, "mix", o))
        last = l == n_layers - 1
        x1, pa, pb, ym, yat, ybt, mt = mixer_fwd(xs, parts, o, mod[l], w["lnw"][l], w["lnb"][l], w["sw"][l], w["sb"][l],
                                                 w["hnw"][l], wl["wa"][l], wl["wb"][l], wl["wo"][l], last)
        wl.update(fetch(l, "ffn", x1))
        av, h2t = ffn_up_fwd(x1, mod[l], w["nw2"][l], wl["wup"][l], last)
        x2, ac, y, z = ffn_down_fwd(x1, av, mod[l], w["cw"][l], w["cb"][l], wl["wd"][l], last)
        saved.append((xs, parts, iv, o, ck, x1, av, ac, y, z, ht, h2t, pa, pb, ym, yat, ybt, mt))
        big.append(wl)
        xs = x2
    loss, dx, dfw = loss_fwd_bwd(xs, target, w["fw"])
    g = {k: [None] * n_layers for k in ("nw1", "nw2", "lnw", "lnb", "sw", "sb", "hnw", "cw", "cb")}
    g["fw"] = dfw
    dmod, dlb = [None] * n_layers, [None] * n_layers
    tok = 0.0
    for l in reversed(range(n_layers)):
        x0, parts, iv, o, ck, x1, av, ac, y, z, ht, h2t, pa, pb, ym, yat, ybt, mt = saved[l]
        wl = big[l]
        last = l == n_layers - 1
        dav, dac, dout, dg2 = ffn_down_bwd(dx, ac, av, y, mod[l] + tok, wl["wd"][l], last)
        dwd = weight_grad_rows(z, dout, "ffn_down_bwd_w")
        dav, g["cw"][l], g["cb"][l] = conv_bwd(dav, dac, av, w["cw"][l], last)
        dx1, g["nw2"][l], dmod2 = ffn_up_bwd_x(dx, x1, dav, mod[l], w["nw2"][l], wl["wup"][l], last)
        dwup = weight_grad(h2t, dav, FF_SLOT, "ffn_up_bwd_w")
        tok = publish(l, "ffn", {"wd": dwd, "wup": dwup})
        (dparts, do, dy, dpa, dpb, g["lnw"][l], g["lnb"][l], g["sw"][l], g["sb"][l], g["hnw"][l],
         dg1) = mixer_bwd(dx1, parts, o, pa, pb, ym, mod[l] + tok, w["lnw"][l], w["lnb"][l], w["sw"][l], w["sb"][l],
                          w["hnw"][l], wl["wa"][l], wl["wb"][l], wl["wo"][l], last)
        tok = publish(l, "mix", {"wa": weight_grad(yat, dpa, D, "mixer_bwd_wa"), "wb": weight_grad(ybt, dpb, D, "mixer_bwd_wb"),
                                 "wo": weight_grad(mt, dy, D, "mixer_bwd_wo")})
        if l == 0 and small_early:
            dmod[0] = jnp.concatenate([jnp.zeros((2, 2, 1, D), F32), dg1, dmod2, dg2], axis=1)
            tok = tok + small_early(loss[0, 0], g, dmod, dlb)
        dq, df, di, dlb_f = hgrn_bwd(0, parts, lb[l] + tok, mc, mtc, mrefc, ck, do)
        dparts, dlb_b = hgrn_bwd(1, parts, lb[l], mc, mtc, mrefc, ck, do, (dq, df, di), dparts)
        dlb[l] = jnp.concatenate([dlb_f, dlb_b], axis=0)
        tok = publish(l, "in", {"win": weight_grad(ht, dparts, IN_SLOT, "in_proj_bwd_w")})
        dx, g["nw1"][l], dmod1 = in_proj_bwd_x(dx1, x0, dparts, mod[l], w["nw1"][l], wl["win"][l], after=tok,
                                               latent_only=l == 0)
        dmod[l] = jnp.concatenate([dmod1, dg1, dmod2, dg2], axis=1)
    done = small_ready(loss[0, 0], g, dmod, dlb) if small_ready else 0.0
    for (l, part), grads in kept.items():
        for k, v in grads.items():
            g.setdefault(k, [None] * n_layers)[l] = v
    return loss[0, 0], dx, g, dmod, dlb, done


ROW = 1024
REPLICATED = ("norm1_w", "sgu_ln_w", "sgu_ln_b", "sgu_w", "sgu_b", "hgrn_lower_bounds", "hgrn_norm_w", "norm2_w",
              "ffn_conv_b", "final_norm_w")
WEIGHT_ORDER = ("c_ctx", "ada_w", "ada_b", "norm1_w", "w_in", "sgu_ln_w", "sgu_ln_b", "sgu_w", "sgu_b", "hgrn_lower_bounds",
                "hgrn_norm_w", "w_branch_a", "w_branch_b", "w_out", "norm2_w", "ffn_w_up", "ffn_conv_w", "ffn_conv_b",
                "ffn_w_down", "final_norm_w")


def _rows_of(n):
    return -(-n // (8 * ROW)) * 8


def _pack(arrs, total_rows=None):
    parts = []
    for a in arrs:
        flat = a.reshape(-1).astype(F32)
        rows = _rows_of(flat.shape[0])
        parts.append(jnp.pad(flat, (0, rows * ROW - flat.shape[0])).reshape(rows, ROW))
    have = sum(p.shape[0] for p in parts)
    if total_rows is not None and total_rows > have:
        parts.append(jnp.zeros((total_rows - have, ROW), F32))
    return jnp.concatenate(parts, axis=0)


def _unpack(packed, shapes):
    lead = packed.shape[:-2]
    out, r0 = [], 0
    for s in shapes:
        n = math.prod(s)
        rows = _rows_of(n)
        out.append(packed[..., r0:r0 + rows, :].reshape(lead + (rows * ROW,))[..., :n].reshape(lead + tuple(s)))
        r0 += rows
    return out


def kernel(x, c, ctx, c_ctx, ada_w, ada_b, norm1_w, w_in, sgu_ln_w, sgu_ln_b, sgu_w, sgu_b, hgrn_lower_bounds, hgrn_norm_w, w_branch_a, w_branch_b, w_out, norm2_w, ffn_w_up, ffn_conv_w, ffn_conv_b, ffn_w_down, final_norm_w, loss_target, m_c_ctx, m_ada_w, m_ada_b, m_norm1_w, m_w_in, m_sgu_ln_w, m_sgu_ln_b, m_sgu_w, m_sgu_b, m_hgrn_lower_bounds, m_hgrn_norm_w, m_w_branch_a, m_w_branch_b, m_w_out, m_norm2_w, m_ffn_w_up, m_ffn_conv_w, m_ffn_conv_b, m_ffn_w_down, m_final_norm_w, v_c_ctx, v_ada_w, v_ada_b, v_norm1_w, v_w_in, v_sgu_ln_w, v_sgu_ln_b, v_sgu_w, v_sgu_b, v_hgrn_lower_bounds, v_hgrn_norm_w, v_w_branch_a, v_w_branch_b, v_w_out, v_norm2_w, v_ffn_w_up, v_ffn_conv_w, v_ffn_conv_b, v_ffn_w_down, v_final_norm_w):
    wts = dict(c_ctx=c_ctx, ada_w=ada_w, ada_b=ada_b, norm1_w=norm1_w, w_in=w_in, sgu_ln_w=sgu_ln_w, sgu_ln_b=sgu_ln_b,
               sgu_w=sgu_w, sgu_b=sgu_b, hgrn_lower_bounds=hgrn_lower_bounds, hgrn_norm_w=hgrn_norm_w, w_branch_a=w_branch_a,
               w_branch_b=w_branch_b, w_out=w_out, norm2_w=norm2_w, ffn_w_up=ffn_w_up, ffn_conv_w=ffn_conv_w,
               ffn_conv_b=ffn_conv_b, ffn_w_down=ffn_w_down, final_norm_w=final_norm_w)
    mom1 = dict(c_ctx=m_c_ctx, ada_w=m_ada_w, ada_b=m_ada_b, norm1_w=m_norm1_w, w_in=m_w_in, sgu_ln_w=m_sgu_ln_w,
                sgu_ln_b=m_sgu_ln_b, sgu_w=m_sgu_w, sgu_b=m_sgu_b, hgrn_lower_bounds=m_hgrn_lower_bounds,
                hgrn_norm_w=m_hgrn_norm_w, w_branch_a=m_w_branch_a, w_branch_b=m_w_branch_b, w_out=m_w_out, norm2_w=m_norm2_w,
                ffn_w_up=m_ffn_w_up, ffn_conv_w=m_ffn_conv_w, ffn_conv_b=m_ffn_conv_b, ffn_w_down=m_ffn_w_down,
                final_norm_w=m_final_norm_w)
    mom2 = dict(c_ctx=v_c_ctx, ada_w=v_ada_w, ada_b=v_ada_b, norm1_w=v_norm1_w, w_in=v_w_in, sgu_ln_w=v_sgu_ln_w,
                sgu_ln_b=v_sgu_ln_b, sgu_w=v_sgu_w, sgu_b=v_sgu_b, hgrn_lower_bounds=v_hgrn_lower_bounds,
                hgrn_norm_w=v_hgrn_norm_w, w_branch_a=v_w_branch_a, w_branch_b=v_w_branch_b, w_out=v_w_out, norm2_w=v_norm2_w,
                ffn_w_up=v_ffn_w_up, ffn_conv_w=v_ffn_conv_w, ffn_conv_b=v_ffn_conv_b, ffn_w_down=v_ffn_w_down,
                final_norm_w=v_final_norm_w)
    n_layers = w_in.shape[0]
    layers = range(n_layers)
    me = 4 * lax.axis_index("x") + 2 * lax.axis_index("y") + lax.axis_index("c")
    ada_cols = ada_w.shape[-1]

    big = ("w_in", "ffn_w_up", "w_branch_a", "w_branch_b", "w_out", "ffn_w_down")
    short = {"w_in": "win", "ffn_w_up": "wup", "w_branch_a": "wa", "w_branch_b": "wb", "w_out": "wo", "ffn_w_down": "wd"}
    me1 = me.reshape(1).astype(jnp.int32)
    mixer, ffn = ("w_branch_a", "w_branch_b", "w_out"), ("ffn_w_up", "ffn_w_down")
    groups = [[(k, l) for k in part] for l in layers for part in (("w_in",), mixer, ffn)]
    group_of = {(l, part): 3 * l + n for l in layers for n, part in enumerate(("in", "mix", "ffn"))}
    in_flight, started = [], 0.0

    def own_slots(n, after):
        return [place_own(wts[k], jax.ShapeDtypeStruct((N_DEV,) + wts[k].shape[1:], BF16), me1, None, False,
                          f"gather_own_{short[k]}_{l}", src_layer=l, after=after) for k, l in groups[n]]

    def start_group(n, lands, after):
        in_flight.append(exchange_start([], lands, None, False, f"gather_weights_start_{n}", after=after,
                                        peers=SAME_CORE_AND_SIBLING if n == 0 else ALL_PEERS))
        return in_flight[-1][-1]

    (c_all,) = all_gather([c], "gather_c")
    c_all = c_all.reshape(N_DEV, D)
    token = start_group(0, own_slots(0, c_all), c_all)
    later = [own_slots(n, token) for n in range(1, len(groups))]
    cctx8 = jnp.broadcast_to(c_ctx[None, :], (N_DEV, D))
    ada_b_cols = lax.dynamic_slice_in_dim(ada_b, me * ada_cols, ada_cols, axis=1)[:, None, :]
    mod_cols = ada_fwd(c_all, cctx8, ada_w, ada_b_cols)
    xs = jnp.concatenate([ctx[0], x[0]], axis=0)
    lb1 = lower_bounds(hgrn_lower_bounds)
    mod_all, conv_all = all_gather([mod_cols, ffn_conv_w.reshape(n_layers, 9, -1)], "gather_mod_conv",
                                   after=[token, xs, lb1] + [a for lands in later for a in lands])
    conv_full = [conv_all[:, l].transpose(1, 0, 2).reshape(9, N_FFK, FF_SLOT).transpose(1, 0, 2) for l in layers]
    for n in range(1, len(groups)):
        token = start_group(n, later[n - 1], mod_all if n == 1 else token)
    for started_group in in_flight:
        started = started + started_group[-1][0, 0]

    def as_used(k, a):
        return a if k in ("w_in", "ffn_w_up") else a.reshape(N_FFK, FF_SLOT, D) if k == "ffn_w_down" else a.reshape(D, D)

    arrived = {}

    def fetch(l, part, after):
        n = group_of[(l, part)]
        send, recv, _, lands, _ = in_flight[n]
        first = n == 0
        got = exchange_wait(send, recv, [], lands, None, False, after, f"gather_weights_wait_{n}",
                            peers=SAME_CORE_AND_SIBLING if first else ALL_PEERS)
        if first:
            send, recv, _, lands, _ = exchange_start([], got, None, False, "gather_weights_pass_on", peers=OTHER_CHIPS,
                                                     forward=True)
            got = exchange_wait(send, recv, [], lands, None, False, after, "gather_weights_passed_on", peers=OTHER_CHIPS)
        for (k, ll), a in zip(groups[n], got):
            arrived.setdefault(short[k], [None] * n_layers)[ll] = as_used(k, a)
        return arrived

    mod_x = lax.dynamic_index_in_dim(mod_all[:, :, 0], me, axis=2, keepdims=False)
    mod_c = mod_all[:, :, 1, 0]
    mod = [jnp.stack([mod_c[:, l].reshape(6, 1, D), mod_x[:, l].reshape(6, 1, D)]) for l in layers]
    mod[0] = mod[0] + started

    lb = [jnp.zeros((2, 1, D), F32), lb1.reshape(2, 1, D)]

    w = {
        "nw1": [norm1_w[l][None] for l in layers], "nw2": [norm2_w[l][None] for l in layers],
        "lnw": [sgu_ln_w[l][None] for l in layers], "lnb": [sgu_ln_b[l][None] for l in layers],
        "sw": [sgu_w[l] for l in layers], "sb": [sgu_b[l][:, :, None] for l in layers],
        "hnw": [hgrn_norm_w[l][None] for l in layers], "cw": conv_full,
        "cb": [ffn_conv_b[l].reshape(N_FFK, 1, FF_SLOT) for l in layers], "fw": final_norm_w[None],
    }
    long = {v: k for k, v in short.items()}
    landing, sent = {}, []

    def publish(l, part, grads):
        keys = [long[k] for k in grads]
        slots = [a.reshape((N_DEV, -1, a.shape[-1])) for a in grads.values()]
        zones = [place_own(s, landing.get(k, jax.ShapeDtypeStruct((N_DEV, n_layers) + s.shape[1:], s.dtype)), me1, l, True,
                           f"scatter_own_{short[k]}_{l}") for k, s in zip(keys, slots)]
        send, recv, srcs, zones, token = exchange_start(slots, zones, l, True, f"scatter_grads_start_{part}_{l}")
        landing.update(zip(keys, zones))
        sent.append((keys, l, part, send, recv, srcs, token))
        return token[0, 0]

    out = {}
    flat2 = lambda a: a.reshape(-1, a.shape[-1])

    def finish(part, after):
        done = []
        for keys, l, p, send, recv, srcs, _ in sent:
            if p == part:
                zones = exchange_wait(send, recv, srcs, [landing[k] for k in keys], l, True, after,
                                      f"scatter_grads_wait_{part}_{l}")
                landing.update(zip(keys, zones))
                done = keys
        for k in done:
            r = landing[k]
            res = adamw(flat2(wts[k]), flat2(mom1[k]), flat2(mom2[k]), r.reshape(N_DEV, -1, r.shape[-1]), "adamw_" + k)
            out[k] = tuple(a.reshape(wts[k].shape) for a in res)

    rep_rows = -(-sum(_rows_of(wts[k].size) for k in REPLICATED) // 64) * 64
    conv_rows = _rows_of(n_layers * 9 * D_FF)
    dmod_rows = _rows_of(n_layers * 6 * D)
    early = {}

    def small_early(loss_part, g, dmod, dlb):
        d_hlb = lower_bounds_bwd(hgrn_lower_bounds, dlb[1].reshape(1, 2 * D))
        st = lambda k: jnp.stack([jnp.zeros((1, D), F32) if a is None else a for a in g[k]])
        rep_grads = {"norm1_w": st("nw1"), "sgu_ln_w": st("lnw"), "sgu_ln_b": st("lnb"), "sgu_w": st("sw"), "sgu_b": st("sb"),
                     "hgrn_lower_bounds": d_hlb, "hgrn_norm_w": st("hnw"), "norm2_w": st("nw2"), "ffn_conv_b": st("cb"),
                     "final_norm_w": g["fw"]}
        d_conv = jnp.stack([g["cw"][l].transpose(1, 0, 2).reshape(9, D_FF) for l in layers])
        dmod_x = jnp.stack([dmod[l][1].reshape(6 * D) for l in layers])
        dmod_c = jnp.stack([dmod[l][0].reshape(6 * D) for l in layers])
        small = jnp.concatenate([_pack([rep_grads[k] for k in REPLICATED], rep_rows),
                                 _pack([d_conv, dmod_x, dmod_c, loss_part.reshape(1)])], axis=0)
        zone = place_own(small, jax.ShapeDtypeStruct((N_DEV,) + small.shape, F32), me1, None, False, "gather_small_own")
        early["send"], early["recv"], _, early["zones"], token = exchange_start([], [zone], None, False, "gather_small_start")
        return token[0, 0]

    def small_ready(loss_part, g, dmod, dlb):
        late = _pack([g["nw1"][0], dmod[0][1, 0:2], dmod[0][0, 0:2]])
        for part in ("ffn", "mix"):
            finish(part, late)
        (late_all,) = all_gather([late], "gather_small_late", after=[out[k][0] for k in big[1:]])
        (small_all,) = exchange_wait(early["send"], early["recv"], [], early["zones"], None, False, late_all,
                                     "gather_small_wait")
        at_x = rep_rows + conv_rows
        small_all = small_all.at[:, 0:1].set(late_all[:, 0:1])
        small_all = small_all.at[:, at_x:at_x + 2].set(late_all[:, 8:10])
        small_all = small_all.at[:, at_x + dmod_rows:at_x + dmod_rows + 2].set(late_all[:, 16:18])
        d_conv_shape, dmod_shape = (n_layers, 9, D_FF), (n_layers, 6 * D)
        conv_g, dmx_all, dmc_all, loss_all = _unpack(small_all[:, rep_rows:], [d_conv_shape, dmod_shape, dmod_shape, (1,)])
        out["loss"] = functools.reduce(lambda a, b: a + b, [loss_all[k, 0] for k in range(N_DEV)])

        rep = adamw(_pack([wts[k] for k in REPLICATED], rep_rows), _pack([mom1[k] for k in REPLICATED], rep_rows),
                    _pack([mom2[k] for k in REPLICATED], rep_rows), small_all, "adamw_replicated")
        rep = [_unpack(r, [wts[k].shape for k in REPLICATED]) for r in rep]
        for n, k in enumerate(REPLICATED):
            out[k] = tuple(r[n] for r in rep)

        conv_mine = lax.dynamic_index_in_dim(conv_g.reshape(N_DEV, n_layers, 9, N_DEV, -1), me, axis=3, keepdims=False)
        res = adamw(flat2(ffn_conv_w), flat2(m_ffn_conv_w), flat2(v_ffn_conv_w),
                    conv_mine.reshape(N_DEV, -1, conv_mine.shape[-1]), "adamw_conv_w")
        out["ffn_conv_w"] = tuple(r.reshape(ffn_conv_w.shape) for r in res)

        out["ada_b"] = tuple(adamw(ada_b, m_ada_b, v_ada_b, jnp.concatenate([dmx_all, dmc_all], axis=0), "adamw_ada_b"))

        cols_of = lambda a: lax.dynamic_slice_in_dim(a, me * ada_cols, ada_cols, axis=2).transpose(1, 0, 2)
        d_ada_w, d_cctx = ada_bwd(c_all, cctx8, ada_w, ada_b_cols, cols_of(dmx_all), cols_of(dmc_all))
        res = adamw(flat2(ada_w), flat2(m_ada_w), flat2(v_ada_w), flat2(d_ada_w)[None], "adamw_ada_w")
        out["ada_w"] = tuple(r.reshape(ada_w.shape) for r in res)
        (d_cctx_all,) = all_gather([d_cctx], "gather_c_ctx_grad")
        res = adamw(c_ctx[None], m_c_ctx[None], v_c_ctx[None], d_cctx_all, "adamw_c_ctx")
        out["c_ctx"] = tuple(r[0] for r in res)
        return d_cctx_all

    _, grad_x, _, _, _, small_done = local_step(x[0], ctx[0], loss_target[0], mod, lb, w, fetch, publish, small_ready,
                                                small_early)
    loss = out["loss"]

    finish("in", small_done)
    return (loss, grad_x[None]) + tuple(out[k][n] for n in range(4) for k in WEIGHT_ORDER)
```

```python
import functools
import math

import jax
import jax.numpy as jnp
from jax import lax
from jax.experimental import pallas as pl
from jax.experimental.pallas import tpu as pltpu

F32 = jnp.float32
BF16 = jnp.bfloat16

N_DEV = 8
AXES = ("x", "y", "c")
D = 1024
CTX = 256
TM = 256
CH = 64
SGU_CH = 128
HEADS = 8
HD = 128
GRID_W = 64
D_IN = 9 * D
IN_SLOT = D_IN // N_DEV
D_FF = 2816
FF_SLOT = 2 * D_FF // N_DEV
N_FFK = D_FF // FF_SLOT
RMS_EPS = 1e-6
LN_EPS = 1e-5
ADAM_LR, ADAM_B1, ADAM_B2, ADAM_EPS, ADAM_WD, ADAM_STEP = 0.001, 0.9, 0.999, 1e-08, 0.01, 10
VMEM_LIMIT_V7X = 56 * 2 ** 20
ELEMENTWISE_VMEM = 24 * 2 ** 20
COPY_BLOCK_BYTES = 2 ** 21
GRAD_WIRE = jnp.bfloat16

VMEM_WHOLE = pl.BlockSpec(memory_space=pltpu.VMEM)
ANY = pl.BlockSpec(memory_space=pl.ANY)


def _cp(n_axes):
    return pltpu.CompilerParams(dimension_semantics=("arbitrary",) * n_axes, vmem_limit_bytes=VMEM_LIMIT_V7X)


def _dot(a, b, dims):
    return lax.dot_general(a.astype(BF16), b.astype(BF16), (dims, ((), ())), preferred_element_type=F32)


@jax.custom_vjp
def mm(a, b):
    return _dot(a, b, ((1,), (0,)))


mm.defvjp(lambda a, b: (mm(a, b), (a, b)),
          lambda r, g: (_dot(g, r[1], ((1,), (1,))).astype(r[0].dtype), _dot(r[0], g, ((0,), (0,))).astype(r[1].dtype)))


@jax.custom_vjp
def mm_nt(a, b):
    return _dot(a, b, ((1,), (1,)))


mm_nt.defvjp(lambda a, b: (mm_nt(a, b), (a, b)),
             lambda r, g: (_dot(g, r[1], ((1,), (0,))).astype(r[0].dtype), _dot(g, r[0], ((0,), (0,))).astype(r[1].dtype)))


@jax.custom_vjp
def mm_tn(a, b):
    return _dot(a, b, ((0,), (0,)))


mm_tn.defvjp(lambda a, b: (mm_tn(a, b), (a, b)),
             lambda r, g: (_dot(r[1], g, ((1,), (1,))).astype(r[0].dtype), _dot(r[0], g, ((1,), (0,))).astype(r[1].dtype)))


def _tri_dot(m, g):
    hi = g.astype(BF16)
    low = (g - hi.astype(F32)).astype(BF16)
    n = g.shape[1]
    out = jnp.dot(m.astype(BF16), jnp.concatenate([hi, low], axis=1), preferred_element_type=F32)
    return out[:, :n] + out[:, n:]


@jax.custom_vjp
def _cum(m, mt, g):
    return _tri_dot(m, g)


_cum.defvjp(lambda m, mt, g: (_cum(m, mt, g), (m, mt)),
            lambda r, d: (jnp.zeros_like(r[0]), jnp.zeros_like(r[1]), _tri_dot(r[1], d)))


def _silu(x):
    return x * jax.nn.sigmoid(x)


def _gelu(x):
    return 0.5 * x * (1.0 + jnp.tanh(math.sqrt(2.0 / math.pi) * (x + 0.044715 * (x * x * x))))


def _rms(x, w):
    return x * lax.rsqrt(jnp.mean(x * x, axis=-1, keepdims=True) + RMS_EPS) * w


def _norm_mod(x, w, shift, scale):
    return _rms(x, w) * (1.0 + scale) + shift


def _hsl(h):
    return slice(h * HD, (h + 1) * HD)


def _hgrn_chunk(st, qz, fz, iv, lb, m, mt, mref):
    hs = range(HEADS)
    keep = [1.0 - lb[h] for h in hs]
    sg = [jax.nn.sigmoid(fz[h]) for h in hs]
    g = [jnp.log(lb[h] + keep[h] * sg[h]) for h in hs]
    k = [keep[h] * (1.0 - sg[h]) for h in hs]
    q = [_silu(qz[h]) for h in hs]
    b = [_cum(m, mt, g[h]) for h in hs]
    ref = [jnp.sum(mref * g[h], axis=0, keepdims=True) for h in hs]
    last = [jnp.sum(g[h], axis=0, keepdims=True) for h in hs]
    qa = [q[h] * jnp.exp(b[h] - ref[h]) for h in hs]
    ka = [k[h] * jnp.exp(ref[h] - b[h]) for h in hs]
    scores = [jnp.where(m > 0.5, mm_nt(qa[h], ka[h]), 0.0) for h in hs]
    inter = [mm_nt(qa[h] * jnp.exp(ref[h]), st[h]) for h in hs]
    kv = [mm_tn(iv[h], ka[h] * jnp.exp(last[h] - ref[h])) for h in hs]
    outs = [mm(scores[h], iv[h]) + inter[h] for h in hs]
    news = [jnp.exp(last[h]) * st[h] + kv[h] for h in hs]
    return outs, news


def _sgu_fn(ub, vb, lnw, lnb, sw, sb):
    gv = [_gelu(v) for v in vb]
    mu = sum(jnp.sum(t, axis=-1, keepdims=True) for t in gv) / D
    var = sum(jnp.sum((t - mu) * (t - mu), axis=-1, keepdims=True) for t in gv) / D
    inv = lax.rsqrt(var + LN_EPS)
    cols = []
    for g in range(HEADS):
        vn = (gv[g] - mu) * inv * lnw[g] + lnb[g]
        cols.append(_gelu(ub[g]) * (mm(sw[g], vn) + sb[g]))
    return jnp.concatenate(cols, axis=1)


def _readout_fn(ob, og, hnw):
    r = [o * lax.rsqrt(jnp.mean(o * o, axis=-1, keepdims=True) + RMS_EPS) * hnw for o in ob]
    return jnp.concatenate(r, axis=1) * _silu(og)


def _glu_fn(ac, v):
    return _gelu(ac) * v


def _stream_row(tm):
    n_ctx = CTX // tm
    return lambda i: (jnp.where(i < n_ctx, 0, 1), 0, 0, 0)


def in_proj_fwd(x, mod, nw, wg):
    t = x.shape[0]

    def body(x_ref, mod_ref, nw_ref, w_ref, out_ref, ht_ref, iv_ref, mix_ref):
        h32 = _norm_mod(x_ref[...], nw_ref[...], mod_ref[0, 0], mod_ref[0, 1])
        ht_ref[...] = h32.T.astype(BF16)
        h = h32.astype(BF16)
        for j in range(N_DEV):
            out_ref[:, j * IN_SLOT:(j + 1) * IN_SLOT] = jnp.dot(h, w_ref[j], preferred_element_type=F32)
        iv_ref[...] = out_ref[:, 3 * D:4 * D].astype(BF16)
        mix_ref[...] = out_ref[:, 4 * D:].astype(BF16)

    return pl.pallas_call(
        body, name="in_proj_fwd", grid=(t // TM,),
        in_specs=[pl.BlockSpec((TM, D), lambda i: (i, 0)), pl.BlockSpec((1, 6, 1, D), _stream_row(TM)),
                  pl.BlockSpec((1, D), lambda i: (0, 0)), VMEM_WHOLE],
        out_specs=[pl.BlockSpec((TM, D_IN), lambda i: (i, 0)), pl.BlockSpec((D, TM), lambda i: (0, i)),
                   pl.BlockSpec((TM, D), lambda i: (i, 0)), pl.BlockSpec((TM, 5 * D), lambda i: (i, 0))],
        out_shape=[jax.ShapeDtypeStruct((t, D_IN), F32), jax.ShapeDtypeStruct((D, t), BF16), jax.ShapeDtypeStruct((t, D), BF16),
                   jax.ShapeDtypeStruct((t, 5 * D), BF16)],
        compiler_params=_cp(1))(x, mod, nw, wg)


SCAN_STEP = 4
SCAN_ROWS = SCAN_STEP * CH


def _scan_block(nb):
    ncb = CTX // SCAN_ROWS

    def block(d, s):
        bwd = jnp.where(s < ncb, ncb - 1 - s, nb + ncb - 1 - s)
        return jnp.where(d == 0, s, bwd)
    return block


def hgrn_fwd(parts, iv, lb, mc, mtc, mrefc):
    t = parts.shape[0]
    nb = t // SCAN_ROWS
    block = _scan_block(nb)

    def body(q_ref, f_ref, i_ref, lb_ref, m_ref, mt_ref, mr_ref, o_ref, ck_ref, st):
        d = pl.program_id(0)

        @pl.when(pl.program_id(1) == 0)
        def _():
            st[...] = jnp.zeros_like(st)

        for c in range(SCAN_STEP):
            rows = pl.ds(pl.multiple_of(jnp.where(d == 0, c * CH, (SCAN_STEP - 1 - c) * CH), CH), CH)
            ck_ref[0, c] = st[...].astype(BF16)
            outs, news = _hgrn_chunk([st[h] for h in range(HEADS)], [q_ref[rows, _hsl(h)] for h in range(HEADS)],
                                     [f_ref[rows, _hsl(h)] for h in range(HEADS)], [i_ref[rows, _hsl(h)] for h in range(HEADS)],
                                     [lb_ref[0, :, _hsl(h)] for h in range(HEADS)], m_ref[0], mt_ref[0], mr_ref[0])
            for h in range(HEADS):
                o_ref[0, rows, _hsl(h)] = outs[h].astype(BF16)
                st[h] = news[h]

    const = lambda d, s: (d, 0, 0)
    at = lambda k: pl.BlockSpec((SCAN_ROWS, D), lambda d, s: (block(d, s), k(d)))
    return pl.pallas_call(
        body, name="hgrn_fwd", grid=(2, nb),
        in_specs=[at(lambda d: 0), at(lambda d: 1 + d), at(lambda d: 0), pl.BlockSpec((1, 1, D), const),
                  pl.BlockSpec((1, CH, CH), const), pl.BlockSpec((1, CH, CH), const), pl.BlockSpec((1, CH, 1), const)],
        out_specs=[pl.BlockSpec((1, SCAN_ROWS, D), lambda d, s: (d, block(d, s), 0)),
                   pl.BlockSpec((1, SCAN_STEP, HEADS, HD, HD), lambda d, s: (d, s, 0, 0, 0))],
        out_shape=[jax.ShapeDtypeStruct((2, t, D), BF16), jax.ShapeDtypeStruct((2, nb * SCAN_STEP, HEADS, HD, HD), BF16)],
        scratch_shapes=[pltpu.VMEM((HEADS, HD, HD), F32)], compiler_params=_cp(2))(parts, parts, iv, lb, mc, mtc, mrefc)


def _mixer_tile(rows, u_ref, v_ref, og_ref, o_ref, lnw_ref, lnb_ref, sw_ref, sb_ref, hnw_ref):
    n = (rows.stop - rows.start) // SGU_CH
    yas, vjps = [], []
    for c in range(n):
        r = slice(rows.start + c * SGU_CH, rows.start + (c + 1) * SGU_CH)
        ya, vjp_a = jax.vjp(_sgu_fn, [u_ref[r, _hsl(g)].astype(F32) for g in range(HEADS)],
                            [v_ref[r, _hsl(g)].astype(F32) for g in range(HEADS)],
                            [lnw_ref[:, _hsl(g)] for g in range(HEADS)], [lnb_ref[:, _hsl(g)] for g in range(HEADS)],
                            [sw_ref[g] for g in range(HEADS)], [sb_ref[g] for g in range(HEADS)])
        yas.append(ya)
        vjps.append(vjp_a)
    yb, vjp_b = jax.vjp(_readout_fn, [o_ref[0, rows, _hsl(h)].astype(F32) + o_ref[1, rows, _hsl(h)].astype(F32)
                                      for h in range(HEADS)],
                        og_ref[rows, :].astype(F32), hnw_ref[...])
    return (yas[0] if n == 1 else jnp.concatenate(yas, axis=0)), yb, vjps, vjp_b


def _part_specs(tm, first, n):
    return [pl.BlockSpec((tm, D), functools.partial(lambda k, i: (i, k), first + k)) for k in range(n)]


def _unless_ctx(skip_ctx, is_ctx, zero_refs, work):
    if not skip_ctx:
        return work()

    @pl.when(is_ctx)
    def _():
        for r in zero_refs:
            r[...] = jnp.zeros_like(r)

    pl.when(jnp.logical_not(is_ctx))(work)


def mixer_fwd(x, parts, o, mod, lnw, lnb, sw, sb, hnw, wa, wb, wo, skip_ctx):
    t = x.shape[0]

    def body(x_ref, u_ref, v_ref, og_ref, ga_ref, gb_ref, o_ref, mod_ref, lnw_ref, lnb_ref, sw_ref, sb_ref, hnw_ref,
             wa_ref, wb_ref, wo_ref, out_ref, pa_ref, pb_ref, y_ref, yat_ref, ybt_ref, mt_ref):
        def work():
            ya, yb, _, _ = _mixer_tile(slice(0, TM), u_ref, v_ref, og_ref, o_ref, lnw_ref, lnb_ref, sw_ref, sb_ref, hnw_ref)
            pa, pb = mm(ya, wa_ref[...]), mm(yb, wb_ref[...])
            merged = jax.nn.sigmoid(ga_ref[...].astype(F32)) * pa + jax.nn.sigmoid(gb_ref[...].astype(F32)) * pb
            y = mm(merged, wo_ref[...])
            out_ref[...] = x_ref[...] + mod_ref[0, 2] * y
            pa_ref[...], pb_ref[...], y_ref[...] = pa.astype(BF16), pb.astype(BF16), y.astype(BF16)
            yat_ref[...], ybt_ref[...], mt_ref[...] = ya.T.astype(BF16), yb.T.astype(BF16), merged.T.astype(BF16)

        _unless_ctx(skip_ctx, pl.program_id(0) == 0, (out_ref, pa_ref, pb_ref, y_ref, yat_ref, ybt_ref, mt_ref), work)

    vec = lambda n: pl.BlockSpec((1, n), lambda i: (0, 0))
    tile = pl.BlockSpec((TM, D), lambda i: (i, 0))
    tile_t = pl.BlockSpec((D, TM), lambda i: (0, i))
    return pl.pallas_call(
        body, name="mixer_fwd", grid=(t // TM,),
        in_specs=[tile] + _part_specs(TM, 0, 5)
        + [pl.BlockSpec((2, TM, D), lambda i: (0, i, 0)), pl.BlockSpec((1, 6, 1, D), _stream_row(TM)), vec(D), vec(D),
           VMEM_WHOLE, VMEM_WHOLE, vec(HD), VMEM_WHOLE, VMEM_WHOLE, VMEM_WHOLE],
        out_specs=[tile] * 4 + [tile_t] * 3,
        out_shape=[jax.ShapeDtypeStruct((t, D), F32)] + [jax.ShapeDtypeStruct((t, D), BF16)] * 3
        + [jax.ShapeDtypeStruct((D, t), BF16)] * 3, compiler_params=_cp(1),
    )(x, parts, parts, parts, parts, parts, o, mod, lnw, lnb, sw, sb, hnw, wa, wb, wo)


def ffn_up_fwd(x, mod, nw, wg, skip_ctx):
    t = x.shape[0]

    def body(x_ref, mod_ref, nw_ref, w_ref, out_ref, ht_ref):
        def work():
            h32 = _norm_mod(x_ref[...], nw_ref[...], mod_ref[0, 3], mod_ref[0, 4])
            ht_ref[...] = h32.T.astype(BF16)
            h = h32.astype(BF16)
            for j in range(N_DEV):
                out_ref[j] = jnp.dot(h, w_ref[j], preferred_element_type=F32)

        _unless_ctx(skip_ctx, pl.program_id(0) == 0, (out_ref, ht_ref), work)

    return pl.pallas_call(
        body, name="ffn_up_fwd", grid=(t // TM,),
        in_specs=[pl.BlockSpec((TM, D), lambda i: (i, 0)), pl.BlockSpec((1, 6, 1, D), _stream_row(TM)),
                  pl.BlockSpec((1, D), lambda i: (0, 0)), VMEM_WHOLE],
        out_specs=[pl.BlockSpec((N_DEV, TM, FF_SLOT), lambda i: (0, i, 0)), pl.BlockSpec((D, TM), lambda i: (0, i))],
        out_shape=[jax.ShapeDtypeStruct((N_DEV, t, FF_SLOT), F32), jax.ShapeDtypeStruct((D, t), BF16)],
        compiler_params=_cp(1))(x, mod, nw, wg)


def _halo_specs(nt):
    per = TM // GRID_W
    last = nt * per - 1
    return [pl.BlockSpec((N_FFK, GRID_W, FF_SLOT), lambda i: (0, jnp.maximum(i * per - 1, 0), 0)),
            pl.BlockSpec((N_FFK, TM, FF_SLOT), lambda i: (0, i, 0)),
            pl.BlockSpec((N_FFK, GRID_W, FF_SLOT), lambda i: (0, jnp.minimum(i * per + per, last), 0))]


def _with_halo(prev_ref, main_ref, next_ref, k, i, nt):
    prev = jnp.where(i >= 2, prev_ref[k], 0.0)
    nxt = jnp.where((i >= 1) & (i <= nt - 2), next_ref[k], 0.0)
    return jnp.concatenate([prev, main_ref[k], nxt], axis=0)


def _tap_valid(dc, i, n_rows, offset):
    r = lax.broadcasted_iota(jnp.int32, (n_rows, 1), 0) - offset
    col = jnp.bitwise_and(r, GRID_W - 1)
    pos = jnp.where(i == 0, r, col) + dc
    return (pos >= 0) & (pos < jnp.where(i == 0, TM, GRID_W))


def _row_weight(cw_ref, k, dr, dc, i):
    w = cw_ref[k, 3 * (dr + 1) + dc + 1:3 * (dr + 1) + dc + 2, :]
    return w if dr == 0 else jnp.where(i == 0, 0.0, w)


def ffn_down_fwd(x, av, mod, cw, cb, wd, skip_ctx):
    t = x.shape[0]
    nt = t // TM

    def body(x_ref, ap_ref, am_ref, an_ref, v_ref, mod_ref, cw_ref, cb_ref, wd_ref, out_ref, ac_ref, y_ref, z_ref):
        i = pl.program_id(0)

        def work():
            y = None
            for k in range(N_FFK):
                a_ext = _with_halo(ap_ref, am_ref, an_ref, k, i, nt)
                conv = jnp.zeros((TM, FF_SLOT), F32) + cb_ref[k]
                for dc in (-1, 0, 1):
                    col = functools.reduce(lambda p, q: p + q, [a_ext[GRID_W + GRID_W * dr:GRID_W + GRID_W * dr + TM]
                                                                * _row_weight(cw_ref, k, dr, dc, i) for dr in (-1, 0, 1)])
                    conv = conv + (col if dc == 0 else
                                   jnp.where(_tap_valid(dc, i, TM, 0), pltpu.roll(col, (-dc) % TM, 0), 0.0))
                ac_ref[k] = conv.astype(BF16)
                z = _glu_fn(conv, v_ref[k]).astype(BF16)
                z_ref[k] = z
                part = mm(z, wd_ref[k])
                y = part if y is None else y + part
            y_ref[...] = y
            out_ref[...] = x_ref[...] + mod_ref[0, 5] * y

        _unless_ctx(skip_ctx, i == 0, (out_ref, ac_ref, y_ref, z_ref), work)

    tile = pl.BlockSpec((TM, D), lambda i: (i, 0))
    half = lambda first: pl.BlockSpec((N_FFK, TM, FF_SLOT), lambda i: (first, i, 0))
    return pl.pallas_call(
        body, name="ffn_down_fwd", grid=(nt,),
        in_specs=[tile] + _halo_specs(nt) + [half(1), pl.BlockSpec((1, 6, 1, D), _stream_row(TM)), VMEM_WHOLE, VMEM_WHOLE,
                                             VMEM_WHOLE],
        out_specs=[tile, half(0), tile, half(0)],
        out_shape=[jax.ShapeDtypeStruct((t, D), F32), jax.ShapeDtypeStruct((N_FFK, t, FF_SLOT), BF16),
                   jax.ShapeDtypeStruct((t, D), F32), jax.ShapeDtypeStruct((N_FFK, t, FF_SLOT), BF16)],
        compiler_params=_cp(1))(x, av, av, av, av, mod, cw, cb, wd)


def loss_fwd_bwd(x, target, fw):
    t = x.shape[0]

    def body(x_ref, t_ref, w_ref, loss_ref, dx_ref, dw_ref):
        i = pl.program_id(0)

        @pl.when(i == 0)
        def _():
            loss_ref[...] = jnp.zeros_like(loss_ref)
            dw_ref[...] = jnp.zeros_like(dw_ref)
            dx_ref[...] = jnp.zeros_like(dx_ref)

        @pl.when(i > 0)
        def _():
            y, vjp = jax.vjp(_rms, x_ref[...], w_ref[...])
            err = y - t_ref[...]
            loss_ref[...] += 0.5 * jnp.sum(jnp.sum(err * err, axis=-1, keepdims=True) / D)
            dx, dw = vjp(err / D)
            dx_ref[...] = dx
            dw_ref[...] += dw

    return pl.pallas_call(
        body, name="loss_fwd_bwd", grid=(t // TM,),
        in_specs=[pl.BlockSpec((TM, D), lambda i: (i, 0)), pl.BlockSpec((TM, D), lambda i: (jnp.maximum(i - 1, 0), 0)),
                  pl.BlockSpec((1, D), lambda i: (0, 0))],
        out_specs=[pl.BlockSpec((8, 128), lambda i: (0, 0)), pl.BlockSpec((TM, D), lambda i: (i, 0)),
                   pl.BlockSpec((1, D), lambda i: (0, 0))],
        out_shape=[jax.ShapeDtypeStruct((8, 128), F32), jax.ShapeDtypeStruct((t, D), F32), jax.ShapeDtypeStruct((1, D), F32)],
        compiler_params=_cp(1))(x, target, fw)


def _stream_add(ref, k, is_ctx, val):
    ref[0, k] += jnp.where(is_ctx, val, 0.0)
    ref[1, k] += jnp.where(is_ctx, 0.0, val)


def ffn_down_bwd(dx, ac, av, y, mod, wd, skip_ctx):
    t = dx.shape[0]
    nt = t // TM

    def body(dx_ref, ac_ref, v_ref, y_ref, mod_ref, wd_ref, dav_ref, dac_ref, dout_ref, dg_ref):
        i = pl.program_id(0)

        @pl.when(i == 0)
        def _():
            dg_ref[...] = jnp.zeros_like(dg_ref)

        def work():
            _stream_add(dg_ref, 0, i == 0, jnp.sum(dx_ref[...] * y_ref[...], axis=0, keepdims=True))
            dout = (mod_ref[0, 5] * dx_ref[...]).astype(BF16)
            dout_ref[...] = dout
            for k in range(N_FFK):
                _, vjp = jax.vjp(_glu_fn, ac_ref[k].astype(F32), v_ref[k])
                dac, dv = vjp(mm_nt(dout, wd_ref[k]))
                dac_ref[k] = dac
                dav_ref[k] = dv.astype(BF16)

        _unless_ctx(skip_ctx, i == 0, (dav_ref, dac_ref, dout_ref), work)

    tile = pl.BlockSpec((TM, D), lambda i: (i, 0))
    half = lambda first: pl.BlockSpec((N_FFK, TM, FF_SLOT), lambda i: (first, i, 0))
    return pl.pallas_call(
        body, name="ffn_down_bwd", grid=(nt,),
        in_specs=[tile, half(0), half(1), tile, pl.BlockSpec((1, 6, 1, D), _stream_row(TM)), VMEM_WHOLE],
        out_specs=[half(1), half(0), tile, pl.BlockSpec((2, 1, 1, D), lambda i: (0, 0, 0, 0))],
        out_shape=[jax.ShapeDtypeStruct((N_DEV, t, FF_SLOT), BF16), jax.ShapeDtypeStruct((N_FFK, t, FF_SLOT), F32),
                   jax.ShapeDtypeStruct((t, D), BF16), jax.ShapeDtypeStruct((2, 1, 1, D), F32)],
        compiler_params=_cp(1))(dx, ac, av, y, mod, wd)


def conv_bwd(dav, dac, av, cw, skip_ctx):
    t = dac.shape[1]
    nt = t // TM

    def body(dav_in, gp_ref, gm_ref, gn_ref, ap_ref, am_ref, an_ref, cw_ref, dav_ref, dcw_ref, dcb_ref):
        i = pl.program_id(0)

        @pl.when(i == 0)
        def _():
            dcw_ref[...] = jnp.zeros_like(dcw_ref)
            dcb_ref[...] = jnp.zeros_like(dcb_ref)

        def work():
            for k in range(N_FFK):
                g_ext = _with_halo(gp_ref, gm_ref, gn_ref, k, i, nt)
                a_ext = _with_halo(ap_ref, am_ref, an_ref, k, i, nt)
                g_main = gm_ref[k]
                dcb_ref[k] += jnp.sum(g_main, axis=0, keepdims=True)
                da = jnp.zeros((TM, FF_SLOT), F32)
                for dc in (-1, 0, 1):
                    valid = _tap_valid(dc, i, TM, 0)
                    q = functools.reduce(lambda p, r: p + r, [g_ext[GRID_W - GRID_W * dr:GRID_W - GRID_W * dr + TM]
                                                              * _row_weight(cw_ref, k, dr, dc, i) for dr in (-1, 0, 1)])
                    da = da + (q if dc == 0 else pltpu.roll(jnp.where(valid, q, 0.0), dc % TM, 0))
                    g_shift = g_main if dc == 0 else pltpu.roll(jnp.where(valid, g_main, 0.0), dc % TM, 0)
                    for dr in (-1, 0, 1):
                        lo = GRID_W + GRID_W * dr
                        tap = 3 * (dr + 1) + dc + 1
                        dw = jnp.sum(g_shift * a_ext[lo:lo + TM], axis=0, keepdims=True)
                        dcw_ref[k, tap:tap + 1, :] += dw if dr == 0 else jnp.where(i == 0, 0.0, dw)
                dav_ref[k] = da.astype(BF16)

        _unless_ctx(skip_ctx, i == 0, (dav_ref,), work)

    whole = lambda rows: pl.BlockSpec((N_FFK, rows, FF_SLOT), lambda i: (0, 0, 0))
    return pl.pallas_call(
        body, name="conv_bwd", grid=(nt,),
        in_specs=[ANY] + _halo_specs(nt) + _halo_specs(nt) + [VMEM_WHOLE],
        out_specs=[pl.BlockSpec((N_FFK, TM, FF_SLOT), lambda i: (0, i, 0)), whole(9), whole(1)],
        out_shape=[jax.ShapeDtypeStruct(dav.shape, BF16), jax.ShapeDtypeStruct((N_FFK, 9, FF_SLOT), F32),
                   jax.ShapeDtypeStruct((N_FFK, 1, FF_SLOT), F32)],
        input_output_aliases={0: 0}, compiler_params=_cp(1))(dav, dac, dac, dac, av, av, av, cw)


def _norm_mod_bwd(x_ref, nw_ref, mod_ref, k_shift, dh, dx_in, dx_ref, dnw_ref, dmod_ref, is_ctx):
    _, vjp = jax.vjp(_norm_mod, x_ref[...], nw_ref[...], mod_ref[0, k_shift], mod_ref[0, k_shift + 1])
    dx, dnw, dshift, dscale = vjp(dh)
    dx_ref[...] = dx_in + dx
    dnw_ref[...] += dnw
    _stream_add(dmod_ref, 0, is_ctx, dshift)
    _stream_add(dmod_ref, 1, is_ctx, dscale)


def ffn_up_bwd_x(dx2, x, dav, mod, nw, wg, skip_ctx):
    t = x.shape[0]

    def body(dx2_ref, x_ref, dav_ref, mod_ref, nw_ref, w_ref, dx_ref, dnw_ref, dmod_ref):
        i = pl.program_id(0)

        @pl.when(i == 0)
        def _():
            dnw_ref[...] = jnp.zeros_like(dnw_ref)
            dmod_ref[...] = jnp.zeros_like(dmod_ref)

        def work():
            dh = mm_nt(dav_ref[0], w_ref[0])
            for j in range(1, N_DEV):
                dh = dh + mm_nt(dav_ref[j], w_ref[j])
            _norm_mod_bwd(x_ref, nw_ref, mod_ref, 3, dh, dx2_ref[...], dx_ref, dnw_ref, dmod_ref, i == 0)

        _unless_ctx(skip_ctx, i == 0, (dx_ref,), work)

    tile = pl.BlockSpec((TM, D), lambda i: (i, 0))
    return pl.pallas_call(
        body, name="ffn_up_bwd_x", grid=(t // TM,),
        in_specs=[tile, tile, pl.BlockSpec((N_DEV, TM, FF_SLOT), lambda i: (0, i, 0)), pl.BlockSpec((1, 6, 1, D), _stream_row(TM)),
                  pl.BlockSpec((1, D), lambda i: (0, 0)), VMEM_WHOLE],
        out_specs=[tile, pl.BlockSpec((1, D), lambda i: (0, 0)), pl.BlockSpec((2, 2, 1, D), lambda i: (0, 0, 0, 0))],
        out_shape=[jax.ShapeDtypeStruct((t, D), F32), jax.ShapeDtypeStruct((1, D), F32), jax.ShapeDtypeStruct((2, 2, 1, D), F32)],
        compiler_params=_cp(1))(dx2, x, dav, mod, nw, wg)


def weight_grad(at, dout, slot, name, after=None):
    rows, t = at.shape
    stacked = dout.ndim == 3
    n = dout.shape[0] if stacked else dout.shape[1] // slot

    def body(a_ref, d_ref, *rest):
        dw_ref = rest[-1]
        dw_ref[0] = jnp.dot(a_ref[...], d_ref[0] if stacked else d_ref[...], preferred_element_type=F32).astype(dw_ref.dtype)

    d_spec = pl.BlockSpec((1, t, slot), lambda j: (j, 0, 0)) if stacked else pl.BlockSpec((t, slot), lambda j: (0, j))
    extra = [] if after is None else [jnp.reshape(after, (1, 1))]
    return pl.pallas_call(
        body, name=name, grid=(n,), in_specs=[VMEM_WHOLE, d_spec] + [ANY] * len(extra),
        out_specs=pl.BlockSpec((1, rows, slot), lambda j: (j, 0, 0)),
        out_shape=jax.ShapeDtypeStruct((n, rows, slot), GRAD_WIRE), compiler_params=_cp(1))(at, dout, *extra)


def weight_grad_rows(at, dout, name):
    n, t, rows = at.shape
    cols = dout.shape[1]

    def body(a_ref, d_ref, dw_ref):
        dw_ref[0] = _dot(a_ref[0], d_ref[...], ((0,), (0,))).astype(dw_ref.dtype)

    return pl.pallas_call(
        body, name=name, grid=(n,), in_specs=[pl.BlockSpec((1, t, rows), lambda k: (k, 0, 0)), VMEM_WHOLE],
        out_specs=pl.BlockSpec((1, rows, cols), lambda k: (k, 0, 0)),
        out_shape=jax.ShapeDtypeStruct((n, rows, cols), GRAD_WIRE), compiler_params=_cp(1))(at, dout)


def mixer_bwd(dx, parts, o, pa, pb, y, mod, lnw, lnb, sw, sb, hnw, wa, wb, wo, skip_ctx):
    t = dx.shape[0]
    tm = TM
    n_ctx = CTX // tm

    def body(dx_ref, u_ref, v_ref, og_ref, ga_ref, gb_ref, o_ref, pa_ref, pb_ref, y_ref, mod_ref, lnw_ref, lnb_ref, sw_ref,
             sb_ref, hnw_ref, wa_ref, wb_ref, wo_ref, dp_ref, do_ref, dy_ref, dpa_ref, dpb_ref, dlnw_ref, dlnb_ref, dsw_ref,
             dsb_ref, dhnw_ref, dg_ref):
        i = pl.program_id(0)

        @pl.when(i == 0)
        def _():
            for r in (dlnw_ref, dlnb_ref, dsw_ref, dsb_ref, dhnw_ref, dg_ref):
                r[...] = jnp.zeros_like(r)

        def work():
            _, _, vjps, vjp_b = _mixer_tile(slice(0, tm), u_ref, v_ref, og_ref, o_ref, lnw_ref, lnb_ref, sw_ref, sb_ref, hnw_ref)
            pa, pb = pa_ref[...].astype(F32), pb_ref[...].astype(F32)
            sa, sbg = jax.nn.sigmoid(ga_ref[...]), jax.nn.sigmoid(gb_ref[...])
            dxv = dx_ref[...]
            _stream_add(dg_ref, 0, i < n_ctx, jnp.sum(dxv * y_ref[...].astype(F32), axis=0, keepdims=True))
            dy = (mod_ref[0, 2] * dxv).astype(BF16)
            dy_ref[...] = dy
            dmerged = mm_nt(dy, wo_ref[...])
            dpa, dpb = (sa * dmerged).astype(BF16), (sbg * dmerged).astype(BF16)
            dpa_ref[...], dpb_ref[...] = dpa, dpb
            first = 4 * D
            dp_ref[:, first + 3 * D:first + 4 * D] = (dmerged * pa * sa * (1.0 - sa)).astype(BF16)
            dp_ref[:, first + 4 * D:first + 5 * D] = (dmerged * pb * sbg * (1.0 - sbg)).astype(BF16)
            dya = mm_nt(dpa, wa_ref[...])
            dob, dog, dhnw = vjp_b(mm_nt(dpb, wb_ref[...]))
            dp_ref[:, first + 2 * D:first + 3 * D] = dog.astype(BF16)
            dhnw_ref[...] += dhnw
            for g in range(HEADS):
                do_ref[:, _hsl(g)] = dob[g]
            for c, vjp_a in enumerate(vjps):
                rows = slice(c * SGU_CH, (c + 1) * SGU_CH)
                dub, dvb, dlnw, dlnb, dsw, dsb = vjp_a(dya[rows])
                for g in range(HEADS):
                    dp_ref[rows, first + g * HD:first + (g + 1) * HD] = dub[g].astype(BF16)
                    dp_ref[rows, first + D + g * HD:first + D + (g + 1) * HD] = dvb[g].astype(BF16)
                    dlnw_ref[:, _hsl(g)] += dlnw[g]
                    dlnb_ref[:, _hsl(g)] += dlnb[g]
                    dsw_ref[g] += dsw[g]
                    dsb_ref[g] += dsb[g]

        _unless_ctx(skip_ctx, i < n_ctx, (dp_ref, do_ref, dy_ref, dpa_ref, dpb_ref), work)

    vec = lambda n: pl.BlockSpec((1, n), lambda i: (0, 0))
    tile = pl.BlockSpec((tm, D), lambda i: (i, 0))
    sds = jax.ShapeDtypeStruct
    return pl.pallas_call(
        body, name="mixer_bwd", grid=(t // tm,),
        in_specs=[tile] + _part_specs(tm, 4, 5)
        + [pl.BlockSpec((2, tm, D), lambda i: (0, i, 0)), tile, tile, tile, pl.BlockSpec((1, 6, 1, D), _stream_row(tm)),
           vec(D), vec(D), VMEM_WHOLE, VMEM_WHOLE, vec(HD), VMEM_WHOLE, VMEM_WHOLE, VMEM_WHOLE],
        out_specs=[pl.BlockSpec((tm, D_IN), lambda i: (i, 0)), tile, tile, tile, tile, vec(D), vec(D),
                   VMEM_WHOLE, VMEM_WHOLE, vec(HD), pl.BlockSpec((2, 1, 1, D), lambda i: (0, 0, 0, 0))],
        out_shape=[sds((t, D_IN), BF16), sds((t, D), F32), sds((t, D), BF16), sds((t, D), BF16), sds((t, D), BF16),
                   sds((1, D), F32), sds((1, D), F32), sds((HEADS, SGU_CH, SGU_CH), F32), sds((HEADS, SGU_CH, 1), F32),
                   sds((1, HD), F32), sds((2, 1, 1, D), F32)],
        compiler_params=_cp(1))(dx, parts, parts, parts, parts, parts, o, pa, pb, y, mod, lnw, lnb, sw, sb, hnw, wa, wb, wo)


def hgrn_bwd(d, parts, lb, mc, mtc, mrefc, ck, do, first=None, dparts=None):
    t = parts.shape[0]
    nb = t // SCAN_ROWS
    block = _scan_block(nb)
    rev = lambda s: block(d, nb - 1 - s)

    def body(q_ref, f_ref, i_ref, lb_ref, m_ref, mt_ref, mr_ref, ck_ref, do_ref, *rest):
        dst = rest[-1]
        dlb_ref = rest[-2]

        @pl.when(pl.program_id(0) == 0)
        def _():
            dst[...] = jnp.zeros_like(dst)
            dlb_ref[...] = jnp.zeros_like(dlb_ref)

        heads = range(HEADS)
        fn = functools.partial(_hgrn_chunk, m=m_ref[0], mt=mt_ref[0], mref=mr_ref[0])
        for c in reversed(range(SCAN_STEP)):
            first_row = c * CH if d == 0 else (SCAN_STEP - 1 - c) * CH
            rows = slice(first_row, first_row + CH)
            _, vjp = jax.vjp(fn, [ck_ref[0, c, h].astype(F32) for h in heads], [q_ref[rows, _hsl(h)] for h in heads],
                             [f_ref[rows, _hsl(h)] for h in heads], [i_ref[rows, _hsl(h)] for h in heads],
                             [lb_ref[0, :, _hsl(h)] for h in heads])
            dstl, dq, df, di, dlb = vjp(([do_ref[rows, _hsl(h)] for h in heads], [dst[h] for h in heads]))
            for h in heads:
                dst[h] = dstl[h]
                dlb_ref[0, :, _hsl(h)] += dlb[h]
                if d == 0:
                    dq_ref, df_ref, di_ref = rest[:3]
                    dq_ref[rows, _hsl(h)] = dq[h].astype(BF16)
                    df_ref[rows, _hsl(h)] = df[h].astype(BF16)
                    di_ref[rows, _hsl(h)] = di[h].astype(BF16)
                else:
                    dq0_ref, df0_ref, di0_ref, _, dp_ref = rest[:5]
                    col = lambda k: slice(k * D + h * HD, k * D + (h + 1) * HD)
                    dp_ref[rows, col(0)] = (dq0_ref[rows, _hsl(h)].astype(F32) + dq[h]).astype(BF16)
                    dp_ref[rows, col(1)] = df0_ref[rows, _hsl(h)]
                    dp_ref[rows, col(2)] = df[h].astype(BF16)
                    dp_ref[rows, col(3)] = (di0_ref[rows, _hsl(h)].astype(F32) + di[h]).astype(BF16)

    const = lambda s: (d, 0, 0)
    at = lambda k: pl.BlockSpec((SCAN_ROWS, D), lambda s: (rev(s), k))
    in_specs = [at(0), at(1 + d), at(3), pl.BlockSpec((1, 1, D), const), pl.BlockSpec((1, CH, CH), const),
                pl.BlockSpec((1, CH, CH), const), pl.BlockSpec((1, CH, 1), const),
                pl.BlockSpec((1, SCAN_STEP, HEADS, HD, HD), lambda s: (d, nb - 1 - s, 0, 0, 0)), at(0)]
    dlb_spec, dlb_shape = pl.BlockSpec((1, 1, D), lambda s: (0, 0, 0)), jax.ShapeDtypeStruct((1, 1, D), F32)
    common = dict(grid=(nb,), scratch_shapes=[pltpu.VMEM((HEADS, HD, HD), F32)], compiler_params=_cp(1))
    if d == 0:
        return pl.pallas_call(body, name="hgrn_bwd_fwd_dir", in_specs=in_specs, out_specs=[at(0)] * 3 + [dlb_spec],
                              out_shape=[jax.ShapeDtypeStruct((t, D), BF16)] * 3 + [dlb_shape], **common,
                              )(parts, parts, parts, lb, mc, mtc, mrefc, ck, do)
    return pl.pallas_call(body, name="hgrn_bwd_bwd_dir", in_specs=in_specs + [at(0)] * 3 + [ANY],
                          out_specs=[pl.BlockSpec((SCAN_ROWS, 4 * D), lambda s: (rev(s), 0)), dlb_spec],
                          out_shape=[jax.ShapeDtypeStruct(dparts.shape, BF16), dlb_shape], input_output_aliases={12: 0},
                          **common)(parts, parts, parts, lb, mc, mtc, mrefc, ck, do, *first, dparts)


def in_proj_bwd_x(dx1, x, dparts, mod, nw, wg, after=None, latent_only=False):
    t = x.shape[0]
    tm = TM
    n_ctx = CTX // tm

    def body(dx1_ref, x_ref, dp_ref, mod_ref, nw_ref, w_ref, *rest):
        dx_ref, dnw_ref, dmod_ref = rest[-3:]
        i = pl.program_id(0)

        @pl.when(i == 0)
        def _():
            dnw_ref[...] = jnp.zeros_like(dnw_ref)
            dmod_ref[...] = jnp.zeros_like(dmod_ref)

        dh = mm_nt(dp_ref[:, 0:IN_SLOT], w_ref[0])
        for j in range(1, N_DEV):
            dh = dh + mm_nt(dp_ref[:, j * IN_SLOT:(j + 1) * IN_SLOT], w_ref[j])
        _norm_mod_bwd(x_ref, nw_ref, mod_ref, 0, dh, dx1_ref[...], dx_ref, dnw_ref, dmod_ref, i < n_ctx)

    tile = pl.BlockSpec((tm, D), lambda i: (i, 0))
    extra = [] if after is None else [jnp.reshape(after, (1, 1))]
    return pl.pallas_call(
        body, name="in_proj_bwd_x", grid=(t // tm,),
        in_specs=[tile, tile, pl.BlockSpec((tm, D_IN), lambda i: (i, 0)), pl.BlockSpec((1, 6, 1, D), _stream_row(tm)),
                  pl.BlockSpec((1, D), lambda i: (0, 0)), VMEM_WHOLE] + [ANY] * len(extra),
        out_specs=[pl.BlockSpec((tm, D), lambda i: (jnp.maximum(i - n_ctx, 0), 0)) if latent_only else tile,
                   pl.BlockSpec((1, D), lambda i: (0, 0)), pl.BlockSpec((2, 2, 1, D), lambda i: (0, 0, 0, 0))],
        out_shape=[jax.ShapeDtypeStruct((t - CTX if latent_only else t, D), F32), jax.ShapeDtypeStruct((1, D), F32),
                   jax.ShapeDtypeStruct((2, 2, 1, D), F32)],
        compiler_params=_cp(1))(dx1, x, dparts, mod, nw, wg, *extra)


def _lb_fn(h0, h1):
    m = jnp.maximum(h0, h1)
    e0, e1 = jnp.exp(h0 - m), jnp.exp(h1 - m)
    return e1 / (e0 + e1)


def lower_bounds(hlb):
    def body(h_ref, out_ref):
        out_ref[...] = _lb_fn(h_ref[0:1, :], h_ref[1:2, :])
    return pl.pallas_call(body, name="lower_bounds", out_shape=jax.ShapeDtypeStruct((1, 2 * D), F32))(hlb)


def lower_bounds_bwd(hlb, dlb1):
    def body(h_ref, d_ref, out_ref):
        _, vjp = jax.vjp(_lb_fn, h_ref[0:1, :], h_ref[1:2, :])
        d0, d1 = vjp(d_ref[...])
        out_ref[0:1, :] = d0
        out_ref[1:2, :] = d1
    return pl.pallas_call(body, name="lower_bounds_bwd", out_shape=jax.ShapeDtypeStruct((2, 2 * D), F32))(hlb, dlb1)


def _ada_fn(c_all, cctx8, w, b):
    dot = lambda a, l: mm(_silu(a), w[l]) + b[l]
    return [dot(c_all, l) for l in range(2)], [dot(cctx8, l) for l in range(2)]


def ada_fwd(c_all, cctx8, w, b):
    cols = w.shape[-1]

    def body(c_ref, cc_ref, w_ref, b_ref, out_ref):
        ox, oc = _ada_fn(c_ref[...], cc_ref[...], [w_ref[0], w_ref[1]], [b_ref[0], b_ref[1]])
        for l in range(2):
            out_ref[l, 0] = ox[l]
            out_ref[l, 1] = oc[l]
    return pl.pallas_call(body, name="ada_fwd", out_shape=jax.ShapeDtypeStruct((2, 2, N_DEV, cols), F32),
                          compiler_params=_cp(0))(c_all, cctx8, w, b)


def ada_bwd(c_all, cctx8, w, b, dmx, dmc):
    cols = w.shape[-1]

    def body(c_ref, cc_ref, w_ref, b_ref, dmx_ref, dmc_ref, dw_ref, dc_ref):
        fn = lambda cc, w0, w1: _ada_fn(c_ref[...], cc, [w0, w1], [b_ref[0], b_ref[1]])
        _, vjp = jax.vjp(fn, cc_ref[...], w_ref[0], w_ref[1])
        dcc, dw0, dw1 = vjp(([dmx_ref[0], dmx_ref[1]], [dmc_ref[0], dmc_ref[1]]))
        dw_ref[0] = dw0
        dw_ref[1] = dw1
        dc_ref[...] = jnp.sum(dcc, axis=0, keepdims=True)
    return pl.pallas_call(body, name="ada_bwd", out_shape=[jax.ShapeDtypeStruct((2, D, cols), F32), jax.ShapeDtypeStruct((1, D), F32)],
                          compiler_params=_cp(0))(c_all, cctx8, w, b, dmx, dmc)


def adamw(w, m, v, gparts, name):
    r, c = w.shape
    p = gparts.shape[0]
    rt = r
    while rt % 16 == 0 and (p + 7) * rt * c * 4 * 2 > ELEMENTWISE_VMEM:
        rt //= 2

    def body(w_ref, m_ref, v_ref, g_ref, go_ref, d_ref, mo_ref, vo_ref):
        g = g_ref[0].astype(F32)
        for k in range(1, p):
            g = g + g_ref[k].astype(F32)
        m2 = ADAM_B1 * m_ref[...] + (1.0 - ADAM_B1) * g
        v2 = ADAM_B2 * v_ref[...] + (1.0 - ADAM_B2) * (g * g)
        m_hat = m2 / (1.0 - ADAM_B1 ** ADAM_STEP)
        v_hat = v2 / (1.0 - ADAM_B2 ** ADAM_STEP)
        go_ref[...] = g
        d_ref[...] = -ADAM_LR * (m_hat / (jnp.sqrt(v_hat) + ADAM_EPS) + ADAM_WD * w_ref[...])
        mo_ref[...] = m2
        vo_ref[...] = v2

    tile = pl.BlockSpec((rt, c), lambda i: (i, 0))
    return pl.pallas_call(
        body, name=name, grid=(r // rt,),
        in_specs=[tile, tile, tile, pl.BlockSpec((p, rt, c), lambda i: (0, i, 0))], out_specs=[tile] * 4,
        out_shape=[jax.ShapeDtypeStruct((r, c), F32)] * 4, compiler_params=_cp(1))(w, m, v, gparts)


def _me():
    x, y, c = lax.axis_index("x"), lax.axis_index("y"), lax.axis_index("c")
    return x, y, c, 4 * x + 2 * y + c


def _peer(x, y, c, p):
    fx, fy, fc = (p >> 2) & 1, (p >> 1) & 1, p & 1
    return (1 - x if fx else x, 1 - y if fy else y, 1 - c if fc else c)


def all_gather(arrs, name, after=None):
    n = len(arrs)
    extra = [] if after is None else list(after) if isinstance(after, (list, tuple)) else [after]

    def body(*refs):
        ins, outs = refs[:n], refs[n + len(extra):2 * n + len(extra)]
        send, recv, local = refs[2 * n + len(extra):]
        x, y, c, me = _me()
        copies = []
        for a in range(n):
            lc = pltpu.make_async_copy(ins[a], outs[a].at[me], local.at[a])
            lc.start()
            copies.append(lc)
            for p in range(1, N_DEV):
                cp = pltpu.make_async_remote_copy(src_ref=ins[a], dst_ref=outs[a].at[me], send_sem=send.at[a, p - 1],
                                                  recv_sem=recv.at[a, p - 1], device_id=_peer(x, y, c, p),
                                                  device_id_type=pl.DeviceIdType.MESH)
                cp.start()
                copies.append(cp)
        for cp in copies:
            cp.wait()

    return pl.pallas_call(
        body, name=name, in_specs=[ANY] * (n + len(extra)), out_specs=[ANY] * n,
        out_shape=[jax.ShapeDtypeStruct((N_DEV,) + a.shape, a.dtype) for a in arrs],
        scratch_shapes=[pltpu.SemaphoreType.DMA((n, N_DEV - 1)), pltpu.SemaphoreType.DMA((n, N_DEV - 1)),
                        pltpu.SemaphoreType.DMA((n,))])(*arrs, *extra)


HBM = pl.BlockSpec(memory_space=pltpu.HBM)
SEM = pl.BlockSpec(memory_space=pltpu.SEMAPHORE)


def _in_hbm(a):
    return pltpu.with_memory_space_constraint(a, pltpu.HBM)


ALL_PEERS = tuple(range(1, N_DEV))
SAME_CORE_AND_SIBLING = (1, 2, 4, 6)
OTHER_CHIPS = (2, 4, 6)


def _exchange_refs(srcs, lands, layer, scatter, a, x, y, c, p, forward=False):
    me = 4 * x + 2 * y + c
    px, py, pc = _peer(x, y, c, p) if p else (x, y, c)
    if forward and p:
        slot = lands[a].at[4 * px + 2 * py + pc]
        return slot, slot, _peer(x, y, c, 1)
    dst = lands[a].at[me] if layer is None else lands[a].at[me, layer]
    src = srcs[a].at[4 * px + 2 * py + pc] if scatter else dst
    return src, dst, (px, py, pc)


def exchange_start(srcs, lands, layer, scatter, name, after=None, peers=ALL_PEERS, forward=False):
    n, ns = len(lands), len(srcs)
    extra = [] if after is None else [after]

    def body(*refs):
        ins, lz = refs[:ns], refs[ns:ns + n]
        send, recv = refs[ns + n + len(extra)], refs[ns + n + len(extra) + 1]
        token = refs[-1]
        x, y, c, _ = _me()
        for a in range(n):
            for p in peers:
                src, dst, peer = _exchange_refs(ins, lz, layer, scatter, a, x, y, c, p, forward)
                k = a * (N_DEV - 1) + p - 1
                pltpu.make_async_remote_copy(src_ref=src, dst_ref=dst, send_sem=send.at[k], recv_sem=recv.at[k],
                                             device_id=peer, device_id_type=pl.DeviceIdType.MESH).start()
        token[...] = jnp.zeros_like(token)

    thru = [pltpu.HBM(a.shape, a.dtype) for a in list(srcs) + list(lands)]
    out = pl.pallas_call(
        body, name=name, in_specs=[HBM] * (ns + n) + [ANY] * len(extra),
        out_specs=[SEM, SEM] + [HBM] * (ns + n) + [pl.BlockSpec(memory_space=pltpu.VMEM)],
        out_shape=[pltpu.SemaphoreType.DMA((n * (N_DEV - 1),)), pltpu.SemaphoreType.DMA((n * (N_DEV - 1),))] + thru
        + [jax.ShapeDtypeStruct((8, 128), F32)],
        input_output_aliases={i: 2 + i for i in range(ns + n)},
        compiler_params=pltpu.CompilerParams(has_side_effects=pltpu.SideEffectType.DATAFLOW_SIDE_EFFECTING),
    )(*[_in_hbm(a) for a in list(srcs) + list(lands)], *extra)
    return out[0], out[1], out[2:2 + ns], out[2 + ns:2 + ns + n], out[-1]


def exchange_wait(send, recv, srcs, lands, layer, scatter, after, name, peers=ALL_PEERS):
    n, ns = len(lands), len(srcs)

    def body(*refs):
        ins, lz = refs[:ns], refs[ns:ns + n]
        send_ref, recv_ref = refs[ns + n], refs[ns + n + 1]
        x, y, c, _ = _me()
        for a in range(n):
            for p in peers:
                src, dst, peer = _exchange_refs(ins, lz, layer, scatter, a, x, y, c, 0)
                k = a * (N_DEV - 1) + p - 1
                cp = pltpu.make_async_remote_copy(src_ref=src, dst_ref=dst, send_sem=send_ref.at[k],
                                                  recv_sem=recv_ref.at[k], device_id=peer,
                                                  device_id_type=pl.DeviceIdType.MESH)
                cp.wait_send()
                cp.wait_recv()

    thru = [pltpu.HBM(a.shape, a.dtype) for a in list(srcs) + list(lands)]
    out = pl.pallas_call(
        body, name=name, in_specs=[HBM] * (ns + n) + [SEM, SEM, ANY], out_specs=[HBM] * (ns + n), out_shape=thru,
        input_output_aliases={i: i for i in range(ns + n)},
        compiler_params=pltpu.CompilerParams(has_side_effects=pltpu.SideEffectType.DATAFLOW_SIDE_EFFECTING),
    )(*srcs, *lands, send, recv, after)
    return out[ns:]


def place_own(src, land, me, layer, scatter, name, src_layer=None, after=None):
    create = isinstance(land, jax.ShapeDtypeStruct)
    r, c = src.shape[-2:]
    rt = r
    while rt % 32 == 0 and rt * c * 4 > COPY_BLOCK_BYTES:
        rt //= 2

    extra = [] if after is None else [after]

    def body(me_ref, src_ref, *rest):
        out_ref = rest[-1]
        out_ref[...] = src_ref[...].reshape(out_ref.shape).astype(out_ref.dtype)

    src_spec = (pl.BlockSpec((1, rt, c), lambda i, m: (m[0], i, 0)) if scatter else
                pl.BlockSpec((rt, c), lambda i, m: (i, 0)) if src_layer is None else
                pl.BlockSpec((1, rt, c), lambda i, m: (src_layer, i, 0)))
    out_spec = (pl.BlockSpec((1, rt, c), lambda i, m: (m[0], i, 0)) if layer is None
                else pl.BlockSpec((1, 1, rt, c), lambda i, m: (m[0], layer, i, 0)))
    grid_spec = pltpu.PrefetchScalarGridSpec(num_scalar_prefetch=1, grid=(r // rt,),
                                             in_specs=[src_spec] + ([] if create else [ANY]) + [ANY] * len(extra),
                                             out_specs=out_spec)
    return pl.pallas_call(body, name=name, grid_spec=grid_spec, out_shape=jax.ShapeDtypeStruct(land.shape, land.dtype),
                          input_output_aliases={} if create else {2: 0}, compiler_params=_cp(1),
                          )(*((me, src) if create else (me, src, land)), *extra)


def _scan_constants():
    r = lax.broadcasted_iota(jnp.int32, (CH, CH), 0)
    s = lax.broadcasted_iota(jnp.int32, (CH, CH), 1)
    lower = (s <= r).astype(F32)
    t = jnp.arange(CH)[:, None]
    mc = jnp.stack([lower, lower.T])
    mref = jnp.stack([(t <= CH // 2 - 1).astype(F32), (t >= CH // 2).astype(F32)])
    return mc, jnp.stack([lower.T, lower]), mref


def local_step(x, ctx, target, mod, lb, w, fetch=None, publish=None, small_ready=None, small_early=None):
    kept = {}

    def keep(l, part, grads):
        kept[(l, part)] = grads
        return 0.0

    fetch = fetch or (lambda l, part, after: w)
    publish = publish or keep
    n_layers = len(mod)
    mc, mtc, mrefc = _scan_constants()
    xs = jnp.concatenate([ctx, x], axis=0)
    saved, big = [], []
    for l in range(n_layers):
        wl = dict(fetch(l, "in", xs))
        parts, ht, iv, mix = in_proj_fwd(xs, mod[l], w["nw1"][l], wl["win"][l])
        o, ck = hgrn_fwd(parts, iv, lb[l], mc, mtc, mrefc)
        wl.update(fetch(l, "mix", o))
        last = l == n_layers - 1
        x1, pa, pb, ym, yat, ybt, mt = mixer_fwd(xs, mix, o, mod[l], w["lnw"][l], w["lnb"][l], w["sw"][l], w["sb"][l],
                                                 w["hnw"][l], wl["wa"][l], wl["wb"][l], wl["wo"][l], last)
        wl.update(fetch(l, "ffn", x1))
        av, h2t = ffn_up_fwd(x1, mod[l], w["nw2"][l], wl["wup"][l], last)
        x2, ac, y, z = ffn_down_fwd(x1, av, mod[l], w["cw"][l], w["cb"][l], wl["wd"][l], last)
        saved.append((xs, parts, iv, o, ck, x1, av, ac, y, z, ht, h2t, pa, pb, ym, yat, ybt, mt))
        big.append(wl)
        xs = x2
    loss, dx, dfw = loss_fwd_bwd(xs, target, w["fw"])
    g = {k: [None] * n_layers for k in ("nw1", "nw2", "lnw", "lnb", "sw", "sb", "hnw", "cw", "cb")}
    g["fw"] = dfw
    dmod, dlb = [None] * n_layers, [None] * n_layers
    tok = 0.0
    for l in reversed(range(n_layers)):
        x0, parts, iv, o, ck, x1, av, ac, y, z, ht, h2t, pa, pb, ym, yat, ybt, mt = saved[l]
        wl = big[l]
        last = l == n_layers - 1
        dav, dac, dout, dg2 = ffn_down_bwd(dx, ac, av, y, mod[l] + tok, wl["wd"][l], last)
        dwd = weight_grad_rows(z, dout, "ffn_down_bwd_w")
        dav, g["cw"][l], g["cb"][l] = conv_bwd(dav, dac, av, w["cw"][l], last)
        dx1, g["nw2"][l], dmod2 = ffn_up_bwd_x(dx, x1, dav, mod[l], w["nw2"][l], wl["wup"][l], last)
        dwup = weight_grad(h2t, dav, FF_SLOT, "ffn_up_bwd_w")
        tok = publish(l, "ffn", {"wd": dwd, "wup": dwup})
        (dparts, do, dy, dpa, dpb, g["lnw"][l], g["lnb"][l], g["sw"][l], g["sb"][l], g["hnw"][l],
         dg1) = mixer_bwd(dx1, parts, o, pa, pb, ym, mod[l] + tok, w["lnw"][l], w["lnb"][l], w["sw"][l], w["sb"][l],
                          w["hnw"][l], wl["wa"][l], wl["wb"][l], wl["wo"][l], last)
        tok = publish(l, "mix", {"wa": weight_grad(yat, dpa, D, "mixer_bwd_wa"), "wb": weight_grad(ybt, dpb, D, "mixer_bwd_wb"),
                                 "wo": weight_grad(mt, dy, D, "mixer_bwd_wo")})
        if l == 0 and small_early:
            dmod[0] = jnp.concatenate([jnp.zeros((2, 2, 1, D), F32), dg1, dmod2, dg2], axis=1)
            tok = tok + small_early(loss[0, 0], g, dmod, dlb)
        dq, df, di, dlb_f = hgrn_bwd(0, parts, lb[l] + tok, mc, mtc, mrefc, ck, do)
        dparts, dlb_b = hgrn_bwd(1, parts, lb[l], mc, mtc, mrefc, ck, do, (dq, df, di), dparts)
        dlb[l] = jnp.concatenate([dlb_f, dlb_b], axis=0)
        tok = publish(l, "in", {"win": weight_grad(ht, dparts, IN_SLOT, "in_proj_bwd_w")})
        dx, g["nw1"][l], dmod1 = in_proj_bwd_x(dx1, x0, dparts, mod[l], w["nw1"][l], wl["win"][l], after=tok,
                                               latent_only=l == 0)
        dmod[l] = jnp.concatenate([dmod1, dg1, dmod2, dg2], axis=1)
    done = small_ready(loss[0, 0], g, dmod, dlb) if small_ready else 0.0
    for (l, part), grads in kept.items():
        for k, v in grads.items():
            g.setdefault(k, [None] * n_layers)[l] = v
    return loss[0, 0], dx, g, dmod, dlb, done


ROW = 1024
REPLICATED = ("norm1_w", "sgu_ln_w", "sgu_ln_b", "sgu_w", "sgu_b", "hgrn_lower_bounds", "hgrn_norm_w", "norm2_w",
              "ffn_conv_b", "final_norm_w")
WEIGHT_ORDER = ("c_ctx", "ada_w", "ada_b", "norm1_w", "w_in", "sgu_ln_w", "sgu_ln_b", "sgu_w", "sgu_b", "hgrn_lower_bounds",
                "hgrn_norm_w", "w_branch_a", "w_branch_b", "w_out", "norm2_w", "ffn_w_up", "ffn_conv_w", "ffn_conv_b",
                "ffn_w_down", "final_norm_w")


def _rows_of(n):
    return -(-n // (8 * ROW)) * 8


def _pack(arrs, total_rows=None):
    parts = []
    for a in arrs:
        flat = a.reshape(-1).astype(F32)
        rows = _rows_of(flat.shape[0])
        parts.append(jnp.pad(flat, (0, rows * ROW - flat.shape[0])).reshape(rows, ROW))
    have = sum(p.shape[0] for p in parts)
    if total_rows is not None and total_rows > have:
        parts.append(jnp.zeros((total_rows - have, ROW), F32))
    return jnp.concatenate(parts, axis=0)


def _unpack(packed, shapes):
    lead = packed.shape[:-2]
    out, r0 = [], 0
    for s in shapes:
        n = math.prod(s)
        rows = _rows_of(n)
        out.append(packed[..., r0:r0 + rows, :].reshape(lead + (rows * ROW,))[..., :n].reshape(lead + tuple(s)))
        r0 += rows
    return out


def kernel(x, c, ctx, c_ctx, ada_w, ada_b, norm1_w, w_in, sgu_ln_w, sgu_ln_b, sgu_w, sgu_b, hgrn_lower_bounds, hgrn_norm_w, w_branch_a, w_branch_b, w_out, norm2_w, ffn_w_up, ffn_conv_w, ffn_conv_b, ffn_w_down, final_norm_w, loss_target, m_c_ctx, m_ada_w, m_ada_b, m_norm1_w, m_w_in, m_sgu_ln_w, m_sgu_ln_b, m_sgu_w, m_sgu_b, m_hgrn_lower_bounds, m_hgrn_norm_w, m_w_branch_a, m_w_branch_b, m_w_out, m_norm2_w, m_ffn_w_up, m_ffn_conv_w, m_ffn_conv_b, m_ffn_w_down, m_final_norm_w, v_c_ctx, v_ada_w, v_ada_b, v_norm1_w, v_w_in, v_sgu_ln_w, v_sgu_ln_b, v_sgu_w, v_sgu_b, v_hgrn_lower_bounds, v_hgrn_norm_w, v_w_branch_a, v_w_branch_b, v_w_out, v_norm2_w, v_ffn_w_up, v_ffn_conv_w, v_ffn_conv_b, v_ffn_w_down, v_final_norm_w):
    wts = dict(c_ctx=c_ctx, ada_w=ada_w, ada_b=ada_b, norm1_w=norm1_w, w_in=w_in, sgu_ln_w=sgu_ln_w, sgu_ln_b=sgu_ln_b,
               sgu_w=sgu_w, sgu_b=sgu_b, hgrn_lower_bounds=hgrn_lower_bounds, hgrn_norm_w=hgrn_norm_w, w_branch_a=w_branch_a,
               w_branch_b=w_branch_b, w_out=w_out, norm2_w=norm2_w, ffn_w_up=ffn_w_up, ffn_conv_w=ffn_conv_w,
               ffn_conv_b=ffn_conv_b, ffn_w_down=ffn_w_down, final_norm_w=final_norm_w)
    mom1 = dict(c_ctx=m_c_ctx, ada_w=m_ada_w, ada_b=m_ada_b, norm1_w=m_norm1_w, w_in=m_w_in, sgu_ln_w=m_sgu_ln_w,
                sgu_ln_b=m_sgu_ln_b, sgu_w=m_sgu_w, sgu_b=m_sgu_b, hgrn_lower_bounds=m_hgrn_lower_bounds,
                hgrn_norm_w=m_hgrn_norm_w, w_branch_a=m_w_branch_a, w_branch_b=m_w_branch_b, w_out=m_w_out, norm2_w=m_norm2_w,
                ffn_w_up=m_ffn_w_up, ffn_conv_w=m_ffn_conv_w, ffn_conv_b=m_ffn_conv_b, ffn_w_down=m_ffn_w_down,
                final_norm_w=m_final_norm_w)
    mom2 = dict(c_ctx=v_c_ctx, ada_w=v_ada_w, ada_b=v_ada_b, norm1_w=v_norm1_w, w_in=v_w_in, sgu_ln_w=v_sgu_ln_w,
                sgu_ln_b=v_sgu_ln_b, sgu_w=v_sgu_w, sgu_b=v_sgu_b, hgrn_lower_bounds=v_hgrn_lower_bounds,
                hgrn_norm_w=v_hgrn_norm_w, w_branch_a=v_w_branch_a, w_branch_b=v_w_branch_b, w_out=v_w_out, norm2_w=v_norm2_w,
                ffn_w_up=v_ffn_w_up, ffn_conv_w=v_ffn_conv_w, ffn_conv_b=v_ffn_conv_b, ffn_w_down=v_ffn_w_down,
                final_norm_w=v_final_norm_w)
    n_layers = w_in.shape[0]
    layers = range(n_layers)
    me = 4 * lax.axis_index("x") + 2 * lax.axis_index("y") + lax.axis_index("c")
    ada_cols = ada_w.shape[-1]

    big = ("w_in", "ffn_w_up", "w_branch_a", "w_branch_b", "w_out", "ffn_w_down")
    short = {"w_in": "win", "ffn_w_up": "wup", "w_branch_a": "wa", "w_branch_b": "wb", "w_out": "wo", "ffn_w_down": "wd"}
    me1 = me.reshape(1).astype(jnp.int32)
    mixer, ffn = ("w_branch_a", "w_branch_b", "w_out"), ("ffn_w_up", "ffn_w_down")
    groups = [[(k, l) for k in part] for l in layers for part in (("w_in",), mixer, ffn)]
    group_of = {(l, part): 3 * l + n for l in layers for n, part in enumerate(("in", "mix", "ffn"))}
    in_flight, started = [], 0.0

    def own_slots(n, after):
        return [place_own(wts[k], jax.ShapeDtypeStruct((N_DEV,) + wts[k].shape[1:], BF16), me1, None, False,
                          f"gather_own_{short[k]}_{l}", src_layer=l, after=after) for k, l in groups[n]]

    def start_group(n, lands, after):
        in_flight.append(exchange_start([], lands, None, False, f"gather_weights_start_{n}", after=after,
                                        peers=SAME_CORE_AND_SIBLING if n == 0 else ALL_PEERS))
        return in_flight[-1][-1]

    (c_all,) = all_gather([c], "gather_c")
    c_all = c_all.reshape(N_DEV, D)
    token = start_group(0, own_slots(0, c_all), c_all)
    later = [own_slots(n, token) for n in range(1, len(groups))]
    cctx8 = jnp.broadcast_to(c_ctx[None, :], (N_DEV, D))
    ada_b_cols = lax.dynamic_slice_in_dim(ada_b, me * ada_cols, ada_cols, axis=1)[:, None, :]
    mod_cols = ada_fwd(c_all, cctx8, ada_w, ada_b_cols)
    xs = jnp.concatenate([ctx[0], x[0]], axis=0)
    lb1 = lower_bounds(hgrn_lower_bounds)
    mod_all, conv_all = all_gather([mod_cols, ffn_conv_w.reshape(n_layers, 9, -1)], "gather_mod_conv",
                                   after=[token, xs, lb1] + [a for lands in later for a in lands])
    conv_full = [conv_all[:, l].transpose(1, 0, 2).reshape(9, N_FFK, FF_SLOT).transpose(1, 0, 2) for l in layers]
    for n in range(1, len(groups)):
        token = start_group(n, later[n - 1], mod_all if n == 1 else token)
    for started_group in in_flight:
        started = started + started_group[-1][0, 0]

    def as_used(k, a):
        return a if k in ("w_in", "ffn_w_up") else a.reshape(N_FFK, FF_SLOT, D) if k == "ffn_w_down" else a.reshape(D, D)

    arrived = {}

    def fetch(l, part, after):
        n = group_of[(l, part)]
        send, recv, _, lands, _ = in_flight[n]
        first = n == 0
        got = exchange_wait(send, recv, [], lands, None, False, after, f"gather_weights_wait_{n}",
                            peers=SAME_CORE_AND_SIBLING if first else ALL_PEERS)
        if first:
            send, recv, _, lands, _ = exchange_start([], got, None, False, "gather_weights_pass_on", peers=OTHER_CHIPS,
                                                     forward=True)
            got = exchange_wait(send, recv, [], lands, None, False, after, "gather_weights_passed_on", peers=OTHER_CHIPS)
        for (k, ll), a in zip(groups[n], got):
            arrived.setdefault(short[k], [None] * n_layers)[ll] = as_used(k, a)
        return arrived

    mod_x = lax.dynamic_index_in_dim(mod_all[:, :, 0], me, axis=2, keepdims=False)
    mod_c = mod_all[:, :, 1, 0]
    mod = [jnp.stack([mod_c[:, l].reshape(6, 1, D), mod_x[:, l].reshape(6, 1, D)]) for l in layers]
    mod[0] = mod[0] + started

    lb = [jnp.zeros((2, 1, D), F32), lb1.reshape(2, 1, D)]

    w = {
        "nw1": [norm1_w[l][None] for l in layers], "nw2": [norm2_w[l][None] for l in layers],
        "lnw": [sgu_ln_w[l][None] for l in layers], "lnb": [sgu_ln_b[l][None] for l in layers],
        "sw": [sgu_w[l] for l in layers], "sb": [sgu_b[l][:, :, None] for l in layers],
        "hnw": [hgrn_norm_w[l][None] for l in layers], "cw": conv_full,
        "cb": [ffn_conv_b[l].reshape(N_FFK, 1, FF_SLOT) for l in layers], "fw": final_norm_w[None],
    }
    long = {v: k for k, v in short.items()}
    landing, sent = {}, []

    def publish(l, part, grads):
        keys = [long[k] for k in grads]
        slots = [a.reshape((N_DEV, -1, a.shape[-1])) for a in grads.values()]
        zones = [place_own(s, landing.get(k, jax.ShapeDtypeStruct((N_DEV, n_layers) + s.shape[1:], s.dtype)), me1, l, True,
                           f"scatter_own_{short[k]}_{l}") for k, s in zip(keys, slots)]
        send, recv, srcs, zones, token = exchange_start(slots, zones, l, True, f"scatter_grads_start_{part}_{l}")
        landing.update(zip(keys, zones))
        sent.append((keys, l, part, send, recv, srcs, token))
        return token[0, 0]

    out = {}
    flat2 = lambda a: a.reshape(-1, a.shape[-1])

    def finish(part, after):
        done = []
        for keys, l, p, send, recv, srcs, _ in sent:
            if p == part:
                zones = exchange_wait(send, recv, srcs, [landing[k] for k in keys], l, True, after,
                                      f"scatter_grads_wait_{part}_{l}")
                landing.update(zip(keys, zones))
                done = keys
        for k in done:
            r = landing[k]
            res = adamw(flat2(wts[k]), flat2(mom1[k]), flat2(mom2[k]), r.reshape(N_DEV, -1, r.shape[-1]), "adamw_" + k)
            out[k] = tuple(a.reshape(wts[k].shape) for a in res)

    rep_rows = -(-sum(_rows_of(wts[k].size) for k in REPLICATED) // 64) * 64
    conv_rows = _rows_of(n_layers * 9 * D_FF)
    dmod_rows = _rows_of(n_layers * 6 * D)
    early = {}

    def small_early(loss_part, g, dmod, dlb):
        d_hlb = lower_bounds_bwd(hgrn_lower_bounds, dlb[1].reshape(1, 2 * D))
        st = lambda k: jnp.stack([jnp.zeros((1, D), F32) if a is None else a for a in g[k]])
        rep_grads = {"norm1_w": st("nw1"), "sgu_ln_w": st("lnw"), "sgu_ln_b": st("lnb"), "sgu_w": st("sw"), "sgu_b": st("sb"),
                     "hgrn_lower_bounds": d_hlb, "hgrn_norm_w": st("hnw"), "norm2_w": st("nw2"), "ffn_conv_b": st("cb"),
                     "final_norm_w": g["fw"]}
        d_conv = jnp.stack([g["cw"][l].transpose(1, 0, 2).reshape(9, D_FF) for l in layers])
        dmod_x = jnp.stack([dmod[l][1].reshape(6 * D) for l in layers])
        dmod_c = jnp.stack([dmod[l][0].reshape(6 * D) for l in layers])
        small = jnp.concatenate([_pack([rep_grads[k] for k in REPLICATED], rep_rows),
                                 _pack([d_conv, dmod_x, dmod_c, loss_part.reshape(1)])], axis=0)
        zone = place_own(small, jax.ShapeDtypeStruct((N_DEV,) + small.shape, F32), me1, None, False, "gather_small_own")
        early["send"], early["recv"], _, early["zones"], token = exchange_start([], [zone], None, False, "gather_small_start")
        return token[0, 0]

    def small_ready(loss_part, g, dmod, dlb):
        late = _pack([g["nw1"][0], dmod[0][1, 0:2], dmod[0][0, 0:2]])
        for part in ("ffn", "mix"):
            finish(part, late)
        (late_all,) = all_gather([late], "gather_small_late", after=[out[k][0] for k in big[1:]])
        (small_all,) = exchange_wait(early["send"], early["recv"], [], early["zones"], None, False, late_all,
                                     "gather_small_wait")
        at_x = rep_rows + conv_rows
        small_all = small_all.at[:, 0:1].set(late_all[:, 0:1])
        small_all = small_all.at[:, at_x:at_x + 2].set(late_all[:, 8:10])
        small_all = small_all.at[:, at_x + dmod_rows:at_x + dmod_rows + 2].set(late_all[:, 16:18])
        d_conv_shape, dmod_shape = (n_layers, 9, D_FF), (n_layers, 6 * D)
        conv_g, dmx_all, dmc_all, loss_all = _unpack(small_all[:, rep_rows:], [d_conv_shape, dmod_shape, dmod_shape, (1,)])
        out["loss"] = functools.reduce(lambda a, b: a + b, [loss_all[k, 0] for k in range(N_DEV)])

        rep = adamw(_pack([wts[k] for k in REPLICATED], rep_rows), _pack([mom1[k] for k in REPLICATED], rep_rows),
                    _pack([mom2[k] for k in REPLICATED], rep_rows), small_all, "adamw_replicated")
        rep = [_unpack(r, [wts[k].shape for k in REPLICATED]) for r in rep]
        for n, k in enumerate(REPLICATED):
            out[k] = tuple(r[n] for r in rep)

        conv_mine = lax.dynamic_index_in_dim(conv_g.reshape(N_DEV, n_layers, 9, N_DEV, -1), me, axis=3, keepdims=False)
        res = adamw(flat2(ffn_conv_w), flat2(m_ffn_conv_w), flat2(v_ffn_conv_w),
                    conv_mine.reshape(N_DEV, -1, conv_mine.shape[-1]), "adamw_conv_w")
        out["ffn_conv_w"] = tuple(r.reshape(ffn_conv_w.shape) for r in res)

        out["ada_b"] = tuple(adamw(ada_b, m_ada_b, v_ada_b, jnp.concatenate([dmx_all, dmc_all], axis=0), "adamw_ada_b"))

        cols_of = lambda a: lax.dynamic_slice_in_dim(a, me * ada_cols, ada_cols, axis=2).transpose(1, 0, 2)
        d_ada_w, d_cctx = ada_bwd(c_all, cctx8, ada_w, ada_b_cols, cols_of(dmx_all), cols_of(dmc_all))
        res = adamw(flat2(ada_w), flat2(m_ada_w), flat2(v_ada_w), flat2(d_ada_w)[None], "adamw_ada_w")
        out["ada_w"] = tuple(r.reshape(ada_w.shape) for r in res)
        (d_cctx_all,) = all_gather([d_cctx], "gather_c_ctx_grad")
        res = adamw(c_ctx[None], m_c_ctx[None], v_c_ctx[None], d_cctx_all, "adamw_c_ctx")
        out["c_ctx"] = tuple(r[0] for r in res)
        return d_cctx_all

    _, grad_x, _, _, _, small_done = local_step(x[0], ctx[0], loss_target[0], mod, lb, w, fetch, publish, small_ready,
                                                small_early)
    loss = out["loss"]

    finish("in", small_done)
    return (loss, grad_x[None]) + tuple(out[k][n] for n in range(4) for k in WEIGHT_ORDER)
```

```python
import functools
import math

import jax
import jax.numpy as jnp
from jax import lax
from jax.experimental import pallas as pl
from jax.experimental.pallas import tpu as pltpu

F32 = jnp.float32
BF16 = jnp.bfloat16

N_DEV = 8
AXES = ("x", "y", "c")
D = 1024
CTX = 256
TM = 256
CH = 64
SGU_CH = 128
HEADS = 8
HD = 128
GRID_W = 64
D_IN = 9 * D
IN_SLOT = D_IN // N_DEV
D_FF = 2816
FF_SLOT = 2 * D_FF // N_DEV
N_FFK = D_FF // FF_SLOT
RMS_EPS = 1e-6
LN_EPS = 1e-5
ADAM_LR, ADAM_B1, ADAM_B2, ADAM_EPS, ADAM_WD, ADAM_STEP = 0.001, 0.9, 0.999, 1e-08, 0.01, 10
VMEM_LIMIT_V7X = 56 * 2 ** 20
ELEMENTWISE_VMEM = 24 * 2 ** 20
COPY_BLOCK_BYTES = 2 ** 21
GRAD_WIRE = jnp.bfloat16

VMEM_WHOLE = pl.BlockSpec(memory_space=pltpu.VMEM)
ANY = pl.BlockSpec(memory_space=pl.ANY)


def _cp(n_axes):
    return pltpu.CompilerParams(dimension_semantics=("arbitrary",) * n_axes, vmem_limit_bytes=VMEM_LIMIT_V7X)


def _dot(a, b, dims):
    return lax.dot_general(a.astype(BF16), b.astype(BF16), (dims, ((), ())), preferred_element_type=F32)


@jax.custom_vjp
def mm(a, b):
    return _dot(a, b, ((1,), (0,)))


mm.defvjp(lambda a, b: (mm(a, b), (a, b)),
          lambda r, g: (_dot(g, r[1], ((1,), (1,))).astype(r[0].dtype), _dot(r[0], g, ((0,), (0,))).astype(r[1].dtype)))


@jax.custom_vjp
def mm_nt(a, b):
    return _dot(a, b, ((1,), (1,)))


mm_nt.defvjp(lambda a, b: (mm_nt(a, b), (a, b)),
             lambda r, g: (_dot(g, r[1], ((1,), (0,))).astype(r[0].dtype), _dot(g, r[0], ((0,), (0,))).astype(r[1].dtype)))


@jax.custom_vjp
def mm_tn(a, b):
    return _dot(a, b, ((0,), (0,)))


mm_tn.defvjp(lambda a, b: (mm_tn(a, b), (a, b)),
             lambda r, g: (_dot(r[1], g, ((1,), (1,))).astype(r[0].dtype), _dot(r[0], g, ((1,), (0,))).astype(r[1].dtype)))


def _tri_dot(m, g):
    hi = g.astype(BF16)
    low = (g - hi.astype(F32)).astype(BF16)
    n = g.shape[1]
    out = jnp.dot(m.astype(BF16), jnp.concatenate([hi, low], axis=1), preferred_element_type=F32)
    return out[:, :n] + out[:, n:]


@jax.custom_vjp
def _cum(m, mt, g):
    return _tri_dot(m, g)


_cum.defvjp(lambda m, mt, g: (_cum(m, mt, g), (m, mt)),
            lambda r, d: (jnp.zeros_like(r[0]), jnp.zeros_like(r[1]), _tri_dot(r[1], d)))


def _silu(x):
    return x * jax.nn.sigmoid(x)


def _gelu(x):
    return 0.5 * x * (1.0 + jnp.tanh(math.sqrt(2.0 / math.pi) * (x + 0.044715 * (x * x * x))))


def _rms(x, w):
    return x * lax.rsqrt(jnp.mean(x * x, axis=-1, keepdims=True) + RMS_EPS) * w


def _norm_mod(x, w, shift, scale):
    return _rms(x, w) * (1.0 + scale) + shift


def _hsl(h):
    return slice(h * HD, (h + 1) * HD)


def _hgrn_chunk(st, qz, fz, iv, lb, m, mt, mref):
    hs = range(HEADS)
    keep = [1.0 - lb[h] for h in hs]
    sg = [jax.nn.sigmoid(fz[h]) for h in hs]
    g = [jnp.log(lb[h] + keep[h] * sg[h]) for h in hs]
    k = [keep[h] * (1.0 - sg[h]) for h in hs]
    q = [_silu(qz[h]) for h in hs]
    b = [_cum(m, mt, g[h]) for h in hs]
    ref = [jnp.sum(mref * g[h], axis=0, keepdims=True) for h in hs]
    last = [jnp.sum(g[h], axis=0, keepdims=True) for h in hs]
    qa = [q[h] * jnp.exp(b[h] - ref[h]) for h in hs]
    ka = [k[h] * jnp.exp(ref[h] - b[h]) for h in hs]
    scores = [jnp.where(m > 0.5, mm_nt(qa[h], ka[h]), 0.0) for h in hs]
    inter = [mm_nt(qa[h] * jnp.exp(ref[h]), st[h]) for h in hs]
    kv = [mm_tn(iv[h], ka[h] * jnp.exp(last[h] - ref[h])) for h in hs]
    outs = [mm(scores[h], iv[h]) + inter[h] for h in hs]
    news = [jnp.exp(last[h]) * st[h] + kv[h] for h in hs]
    return outs, news


def _sgu_fn(ub, vb, lnw, lnb, sw, sb):
    gv = [_gelu(v) for v in vb]
    mu = sum(jnp.sum(t, axis=-1, keepdims=True) for t in gv) / D
    var = sum(jnp.sum((t - mu) * (t - mu), axis=-1, keepdims=True) for t in gv) / D
    inv = lax.rsqrt(var + LN_EPS)
    cols = []
    for g in range(HEADS):
        vn = (gv[g] - mu) * inv * lnw[g] + lnb[g]
        cols.append(_gelu(ub[g]) * (mm(sw[g], vn) + sb[g]))
    return jnp.concatenate(cols, axis=1)


def _readout_fn(ob, og, hnw):
    r = [o * lax.rsqrt(jnp.mean(o * o, axis=-1, keepdims=True) + RMS_EPS) * hnw for o in ob]
    return jnp.concatenate(r, axis=1) * _silu(og)


def _glu_fn(ac, v):
    return _gelu(ac) * v


def _stream_row(tm):
    n_ctx = CTX // tm
    return lambda i: (jnp.where(i < n_ctx, 0, 1), 0, 0, 0)


def in_proj_fwd(x, mod, nw, wg):
    t = x.shape[0]

    def body(x_ref, mod_ref, nw_ref, w_ref, out_ref, ht_ref, iv_ref, mix_ref):
        h32 = _norm_mod(x_ref[...], nw_ref[...], mod_ref[0, 0], mod_ref[0, 1])
        ht_ref[...] = h32.T.astype(BF16)
        h = h32.astype(BF16)
        for j in range(N_DEV):
            out_ref[:, j * IN_SLOT:(j + 1) * IN_SLOT] = jnp.dot(h, w_ref[j], preferred_element_type=F32)
        iv_ref[...] = out_ref[:, 3 * D:4 * D].astype(BF16)
        mix_ref[...] = out_ref[:, 4 * D:].astype(BF16)

    return pl.pallas_call(
        body, name="in_proj_fwd", grid=(t // TM,),
        in_specs=[pl.BlockSpec((TM, D), lambda i: (i, 0)), pl.BlockSpec((1, 6, 1, D), _stream_row(TM)),
                  pl.BlockSpec((1, D), lambda i: (0, 0)), VMEM_WHOLE],
        out_specs=[pl.BlockSpec((TM, D_IN), lambda i: (i, 0)), pl.BlockSpec((D, TM), lambda i: (0, i)),
                   pl.BlockSpec((TM, D), lambda i: (i, 0)), pl.BlockSpec((TM, 5 * D), lambda i: (i, 0))],
        out_shape=[jax.ShapeDtypeStruct((t, D_IN), F32), jax.ShapeDtypeStruct((D, t), BF16), jax.ShapeDtypeStruct((t, D), BF16),
                   jax.ShapeDtypeStruct((t, 5 * D), BF16)],
        compiler_params=_cp(1))(x, mod, nw, wg)


SCAN_STEP = 4
SCAN_ROWS = SCAN_STEP * CH


def _scan_block(nb):
    ncb = CTX // SCAN_ROWS

    def block(d, s):
        bwd = jnp.where(s < ncb, ncb - 1 - s, nb + ncb - 1 - s)
        return jnp.where(d == 0, s, bwd)
    return block


def hgrn_fwd(parts, iv, lb, mc, mtc, mrefc):
    t = parts.shape[0]
    nb = t // SCAN_ROWS
    block = _scan_block(nb)

    def body(q_ref, f_ref, i_ref, lb_ref, m_ref, mt_ref, mr_ref, o_ref, ck_ref, st):
        d = pl.program_id(0)

        @pl.when(pl.program_id(1) == 0)
        def _():
            st[...] = jnp.zeros_like(st)

        for c in range(SCAN_STEP):
            rows = pl.ds(pl.multiple_of(jnp.where(d == 0, c * CH, (SCAN_STEP - 1 - c) * CH), CH), CH)
            ck_ref[0, c] = st[...].astype(BF16)
            outs, news = _hgrn_chunk([st[h] for h in range(HEADS)], [q_ref[rows, _hsl(h)] for h in range(HEADS)],
                                     [f_ref[rows, _hsl(h)] for h in range(HEADS)], [i_ref[rows, _hsl(h)] for h in range(HEADS)],
                                     [lb_ref[0, :, _hsl(h)] for h in range(HEADS)], m_ref[0], mt_ref[0], mr_ref[0])
            for h in range(HEADS):
                o_ref[0, rows, _hsl(h)] = outs[h].astype(BF16)
                st[h] = news[h]

    const = lambda d, s: (d, 0, 0)
    at = lambda k: pl.BlockSpec((SCAN_ROWS, D), lambda d, s: (block(d, s), k(d)))
    return pl.pallas_call(
        body, name="hgrn_fwd", grid=(2, nb),
        in_specs=[at(lambda d: 0), at(lambda d: 1 + d), at(lambda d: 0), pl.BlockSpec((1, 1, D), const),
                  pl.BlockSpec((1, CH, CH), const), pl.BlockSpec((1, CH, CH), const), pl.BlockSpec((1, CH, 1), const)],
        out_specs=[pl.BlockSpec((1, SCAN_ROWS, D), lambda d, s: (d, block(d, s), 0)),
                   pl.BlockSpec((1, SCAN_STEP, HEADS, HD, HD), lambda d, s: (d, s, 0, 0, 0))],
        out_shape=[jax.ShapeDtypeStruct((2, t, D), BF16), jax.ShapeDtypeStruct((2, nb * SCAN_STEP, HEADS, HD, HD), BF16)],
        scratch_shapes=[pltpu.VMEM((HEADS, HD, HD), F32)], compiler_params=_cp(2))(parts, parts, iv, lb, mc, mtc, mrefc)


def _mixer_tile(rows, u_ref, v_ref, og_ref, o_ref, lnw_ref, lnb_ref, sw_ref, sb_ref, hnw_ref):
    n = (rows.stop - rows.start) // SGU_CH
    yas, vjps = [], []
    for c in range(n):
        r = slice(rows.start + c * SGU_CH, rows.start + (c + 1) * SGU_CH)
        ya, vjp_a = jax.vjp(_sgu_fn, [u_ref[r, _hsl(g)].astype(F32) for g in range(HEADS)],
                            [v_ref[r, _hsl(g)].astype(F32) for g in range(HEADS)],
                            [lnw_ref[:, _hsl(g)] for g in range(HEADS)], [lnb_ref[:, _hsl(g)] for g in range(HEADS)],
                            [sw_ref[g] for g in range(HEADS)], [sb_ref[g] for g in range(HEADS)])
        yas.append(ya)
        vjps.append(vjp_a)
    yb, vjp_b = jax.vjp(_readout_fn, [o_ref[0, rows, _hsl(h)].astype(F32) + o_ref[1, rows, _hsl(h)].astype(F32)
                                      for h in range(HEADS)],
                        og_ref[rows, :].astype(F32), hnw_ref[...])
    return (yas[0] if n == 1 else jnp.concatenate(yas, axis=0)), yb, vjps, vjp_b


def _part_specs(tm, first, n):
    return [pl.BlockSpec((tm, D), functools.partial(lambda k, i: (i, k), first + k)) for k in range(n)]


def _unless_ctx(skip_ctx, is_ctx, zero_refs, work):
    if not skip_ctx:
        return work()

    @pl.when(is_ctx)
    def _():
        for r in zero_refs:
            r[...] = jnp.zeros_like(r)

    pl.when(jnp.logical_not(is_ctx))(work)


def mixer_fwd(x, parts, o, mod, lnw, lnb, sw, sb, hnw, wa, wb, wo, skip_ctx):
    t = x.shape[0]

    def body(x_ref, u_ref, v_ref, og_ref, ga_ref, gb_ref, o_ref, mod_ref, lnw_ref, lnb_ref, sw_ref, sb_ref, hnw_ref,
             wa_ref, wb_ref, wo_ref, out_ref, pa_ref, pb_ref, y_ref, yat_ref, ybt_ref, mt_ref):
        def work():
            ya, yb, _, _ = _mixer_tile(slice(0, TM), u_ref, v_ref, og_ref, o_ref, lnw_ref, lnb_ref, sw_ref, sb_ref, hnw_ref)
            pa, pb = mm(ya, wa_ref[...]), mm(yb, wb_ref[...])
            merged = jax.nn.sigmoid(ga_ref[...].astype(F32)) * pa + jax.nn.sigmoid(gb_ref[...].astype(F32)) * pb
            y = mm(merged, wo_ref[...])
            out_ref[...] = x_ref[...] + mod_ref[0, 2] * y
            pa_ref[...], pb_ref[...], y_ref[...] = pa.astype(BF16), pb.astype(BF16), y.astype(BF16)
            yat_ref[...], ybt_ref[...], mt_ref[...] = ya.T.astype(BF16), yb.T.astype(BF16), merged.T.astype(BF16)

        _unless_ctx(skip_ctx, pl.program_id(0) == 0, (out_ref, pa_ref, pb_ref, y_ref, yat_ref, ybt_ref, mt_ref), work)

    vec = lambda n: pl.BlockSpec((1, n), lambda i: (0, 0))
    tile = pl.BlockSpec((TM, D), lambda i: (i, 0))
    tile_t = pl.BlockSpec((D, TM), lambda i: (0, i))
    return pl.pallas_call(
        body, name="mixer_fwd", grid=(t // TM,),
        in_specs=[tile] + _part_specs(TM, 0, 5)
        + [pl.BlockSpec((2, TM, D), lambda i: (0, i, 0)), pl.BlockSpec((1, 6, 1, D), _stream_row(TM)), vec(D), vec(D),
           VMEM_WHOLE, VMEM_WHOLE, vec(HD), VMEM_WHOLE, VMEM_WHOLE, VMEM_WHOLE],
        out_specs=[tile] * 4 + [tile_t] * 3,
        out_shape=[jax.ShapeDtypeStruct((t, D), F32)] + [jax.ShapeDtypeStruct((t, D), BF16)] * 3
        + [jax.ShapeDtypeStruct((D, t), BF16)] * 3, compiler_params=_cp(1),
    )(x, parts, parts, parts, parts, parts, o, mod, lnw, lnb, sw, sb, hnw, wa, wb, wo)


def ffn_up_fwd(x, mod, nw, wg, skip_ctx):
    t = x.shape[0]

    def body(x_ref, mod_ref, nw_ref, w_ref, out_ref, ht_ref, vb_ref):
        def work():
            h32 = _norm_mod(x_ref[...], nw_ref[...], mod_ref[0, 3], mod_ref[0, 4])
            ht_ref[...] = h32.T.astype(BF16)
            h = h32.astype(BF16)
            for j in range(N_DEV):
                out_ref[j] = jnp.dot(h, w_ref[j], preferred_element_type=F32)
                if j >= N_FFK:
                    vb_ref[j - N_FFK] = out_ref[j].astype(BF16)

        _unless_ctx(skip_ctx, pl.program_id(0) == 0, (out_ref, ht_ref, vb_ref), work)

    return pl.pallas_call(
        body, name="ffn_up_fwd", grid=(t // TM,),
        in_specs=[pl.BlockSpec((TM, D), lambda i: (i, 0)), pl.BlockSpec((1, 6, 1, D), _stream_row(TM)),
                  pl.BlockSpec((1, D), lambda i: (0, 0)), VMEM_WHOLE],
        out_specs=[pl.BlockSpec((N_DEV, TM, FF_SLOT), lambda i: (0, i, 0)), pl.BlockSpec((D, TM), lambda i: (0, i)),
                   pl.BlockSpec((N_FFK, TM, FF_SLOT), lambda i: (0, i, 0))],
        out_shape=[jax.ShapeDtypeStruct((N_DEV, t, FF_SLOT), F32), jax.ShapeDtypeStruct((D, t), BF16),
                   jax.ShapeDtypeStruct((N_FFK, t, FF_SLOT), BF16)],
        compiler_params=_cp(1))(x, mod, nw, wg)


def _halo_specs(nt):
    per = TM // GRID_W
    last = nt * per - 1
    return [pl.BlockSpec((N_FFK, GRID_W, FF_SLOT), lambda i: (0, jnp.maximum(i * per - 1, 0), 0)),
            pl.BlockSpec((N_FFK, TM, FF_SLOT), lambda i: (0, i, 0)),
            pl.BlockSpec((N_FFK, GRID_W, FF_SLOT), lambda i: (0, jnp.minimum(i * per + per, last), 0))]


def _with_halo(prev_ref, main_ref, next_ref, k, i, nt):
    prev = jnp.where(i >= 2, prev_ref[k], 0.0)
    nxt = jnp.where((i >= 1) & (i <= nt - 2), next_ref[k], 0.0)
    return jnp.concatenate([prev, main_ref[k], nxt], axis=0)


def _tap_valid(dc, i, n_rows, offset):
    r = lax.broadcasted_iota(jnp.int32, (n_rows, 1), 0) - offset
    col = jnp.bitwise_and(r, GRID_W - 1)
    pos = jnp.where(i == 0, r, col) + dc
    return (pos >= 0) & (pos < jnp.where(i == 0, TM, GRID_W))


def _row_weight(cw_ref, k, dr, dc, i):
    w = cw_ref[k, 3 * (dr + 1) + dc + 1:3 * (dr + 1) + dc + 2, :]
    return w if dr == 0 else jnp.where(i == 0, 0.0, w)


def ffn_down_fwd(x, av, vb, mod, cw, cb, wd, skip_ctx):
    t = x.shape[0]
    nt = t // TM

    def body(x_ref, ap_ref, am_ref, an_ref, v_ref, mod_ref, cw_ref, cb_ref, wd_ref, out_ref, ac_ref, y_ref, z_ref):
        i = pl.program_id(0)

        def work():
            y = None
            for k in range(N_FFK):
                a_ext = _with_halo(ap_ref, am_ref, an_ref, k, i, nt)
                conv = jnp.zeros((TM, FF_SLOT), F32) + cb_ref[k]
                for dc in (-1, 0, 1):
                    col = functools.reduce(lambda p, q: p + q, [a_ext[GRID_W + GRID_W * dr:GRID_W + GRID_W * dr + TM]
                                                                * _row_weight(cw_ref, k, dr, dc, i) for dr in (-1, 0, 1)])
                    conv = conv + (col if dc == 0 else
                                   jnp.where(_tap_valid(dc, i, TM, 0), pltpu.roll(col, (-dc) % TM, 0), 0.0))
                ac_ref[k] = conv.astype(BF16)
                z = _glu_fn(conv, v_ref[k].astype(F32)).astype(BF16)
                z_ref[k] = z
                part = mm(z, wd_ref[k])
                y = part if y is None else y + part
            y_ref[...] = y.astype(BF16)
            out_ref[...] = x_ref[...] + mod_ref[0, 5] * y

        _unless_ctx(skip_ctx, i == 0, (out_ref, ac_ref, y_ref, z_ref), work)

    tile = pl.BlockSpec((TM, D), lambda i: (i, 0))
    half = lambda first: pl.BlockSpec((N_FFK, TM, FF_SLOT), lambda i: (first, i, 0))
    return pl.pallas_call(
        body, name="ffn_down_fwd", grid=(nt,),
        in_specs=[tile] + _halo_specs(nt) + [half(0), pl.BlockSpec((1, 6, 1, D), _stream_row(TM)), VMEM_WHOLE, VMEM_WHOLE,
                                             VMEM_WHOLE],
        out_specs=[tile, half(0), tile, half(0)],
        out_shape=[jax.ShapeDtypeStruct((t, D), F32), jax.ShapeDtypeStruct((N_FFK, t, FF_SLOT), BF16),
                   jax.ShapeDtypeStruct((t, D), BF16), jax.ShapeDtypeStruct((N_FFK, t, FF_SLOT), BF16)],
        compiler_params=_cp(1))(x, av, av, av, vb, mod, cw, cb, wd)


def loss_fwd_bwd(x, target, fw):
    t = x.shape[0]

    def body(x_ref, t_ref, w_ref, loss_ref, dx_ref, dw_ref):
        i = pl.program_id(0)

        @pl.when(i == 0)
        def _():
            loss_ref[...] = jnp.zeros_like(loss_ref)
            dw_ref[...] = jnp.zeros_like(dw_ref)
            dx_ref[...] = jnp.zeros_like(dx_ref)

        @pl.when(i > 0)
        def _():
            y, vjp = jax.vjp(_rms, x_ref[...], w_ref[...])
            err = y - t_ref[...]
            loss_ref[...] += 0.5 * jnp.sum(jnp.sum(err * err, axis=-1, keepdims=True) / D)
            dx, dw = vjp(err / D)
            dx_ref[...] = dx
            dw_ref[...] += dw

    return pl.pallas_call(
        body, name="loss_fwd_bwd", grid=(t // TM,),
        in_specs=[pl.BlockSpec((TM, D), lambda i: (i, 0)), pl.BlockSpec((TM, D), lambda i: (jnp.maximum(i - 1, 0), 0)),
                  pl.BlockSpec((1, D), lambda i: (0, 0))],
        out_specs=[pl.BlockSpec((8, 128), lambda i: (0, 0)), pl.BlockSpec((TM, D), lambda i: (i, 0)),
                   pl.BlockSpec((1, D), lambda i: (0, 0))],
        out_shape=[jax.ShapeDtypeStruct((8, 128), F32), jax.ShapeDtypeStruct((t, D), F32), jax.ShapeDtypeStruct((1, D), F32)],
        compiler_params=_cp(1))(x, target, fw)


def _stream_add(ref, k, is_ctx, val):
    ref[0, k] += jnp.where(is_ctx, val, 0.0)
    ref[1, k] += jnp.where(is_ctx, 0.0, val)


def ffn_down_bwd(dx, ac, vb, y, mod, wd, skip_ctx):
    t = dx.shape[0]
    nt = t // TM

    def body(dx_ref, ac_ref, v_ref, y_ref, mod_ref, wd_ref, dav_ref, dac_ref, dout_ref, dg_ref):
        i = pl.program_id(0)

        @pl.when(i == 0)
        def _():
            dg_ref[...] = jnp.zeros_like(dg_ref)

        def work():
            _stream_add(dg_ref, 0, i == 0, jnp.sum(dx_ref[...] * y_ref[...].astype(F32), axis=0, keepdims=True))
            dout = (mod_ref[0, 5] * dx_ref[...]).astype(BF16)
            dout_ref[...] = dout
            for k in range(N_FFK):
                _, vjp = jax.vjp(_glu_fn, ac_ref[k].astype(F32), v_ref[k].astype(F32))
                dac, dv = vjp(mm_nt(dout, wd_ref[k]))
                dac_ref[k] = dac
                dav_ref[k] = dv.astype(BF16)

        _unless_ctx(skip_ctx, i == 0, (dav_ref, dac_ref, dout_ref), work)

    tile = pl.BlockSpec((TM, D), lambda i: (i, 0))
    half = lambda first: pl.BlockSpec((N_FFK, TM, FF_SLOT), lambda i: (first, i, 0))
    return pl.pallas_call(
        body, name="ffn_down_bwd", grid=(nt,),
        in_specs=[tile, half(0), half(0), tile, pl.BlockSpec((1, 6, 1, D), _stream_row(TM)), VMEM_WHOLE],
        out_specs=[half(1), half(0), tile, pl.BlockSpec((2, 1, 1, D), lambda i: (0, 0, 0, 0))],
        out_shape=[jax.ShapeDtypeStruct((N_DEV, t, FF_SLOT), BF16), jax.ShapeDtypeStruct((N_FFK, t, FF_SLOT), F32),
                   jax.ShapeDtypeStruct((t, D), BF16), jax.ShapeDtypeStruct((2, 1, 1, D), F32)],
        compiler_params=_cp(1))(dx, ac, vb, y, mod, wd)


def conv_bwd(dav, dac, av, cw, skip_ctx):
    t = dac.shape[1]
    nt = t // TM

    def body(dav_in, gp_ref, gm_ref, gn_ref, ap_ref, am_ref, an_ref, cw_ref, dav_ref, dcw_ref, dcb_ref):
        i = pl.program_id(0)

        @pl.when(i == 0)
        def _():
            dcw_ref[...] = jnp.zeros_like(dcw_ref)
            dcb_ref[...] = jnp.zeros_like(dcb_ref)

        def work():
            for k in range(N_FFK):
                g_ext = _with_halo(gp_ref, gm_ref, gn_ref, k, i, nt)
                a_ext = _with_halo(ap_ref, am_ref, an_ref, k, i, nt)
                g_main = gm_ref[k]
                dcb_ref[k] += jnp.sum(g_main, axis=0, keepdims=True)
                da = jnp.zeros((TM, FF_SLOT), F32)
                for dc in (-1, 0, 1):
                    valid = _tap_valid(dc, i, TM, 0)
                    q = functools.reduce(lambda p, r: p + r, [g_ext[GRID_W - GRID_W * dr:GRID_W - GRID_W * dr + TM]
                                                              * _row_weight(cw_ref, k, dr, dc, i) for dr in (-1, 0, 1)])
                    da = da + (q if dc == 0 else pltpu.roll(jnp.where(valid, q, 0.0), dc % TM, 0))
                    g_shift = g_main if dc == 0 else pltpu.roll(jnp.where(valid, g_main, 0.0), dc % TM, 0)
                    for dr in (-1, 0, 1):
                        lo = GRID_W + GRID_W * dr
                        tap = 3 * (dr + 1) + dc + 1
                        dw = jnp.sum(g_shift * a_ext[lo:lo + TM], axis=0, keepdims=True)
                        dcw_ref[k, tap:tap + 1, :] += dw if dr == 0 else jnp.where(i == 0, 0.0, dw)
                dav_ref[k] = da.astype(BF16)

        _unless_ctx(skip_ctx, i == 0, (dav_ref,), work)

    whole = lambda rows: pl.BlockSpec((N_FFK, rows, FF_SLOT), lambda i: (0, 0, 0))
    return pl.pallas_call(
        body, name="conv_bwd", grid=(nt,),
        in_specs=[ANY] + _halo_specs(nt) + _halo_specs(nt) + [VMEM_WHOLE],
        out_specs=[pl.BlockSpec((N_FFK, TM, FF_SLOT), lambda i: (0, i, 0)), whole(9), whole(1)],
        out_shape=[jax.ShapeDtypeStruct(dav.shape, BF16), jax.ShapeDtypeStruct((N_FFK, 9, FF_SLOT), F32),
                   jax.ShapeDtypeStruct((N_FFK, 1, FF_SLOT), F32)],
        input_output_aliases={0: 0}, compiler_params=_cp(1))(dav, dac, dac, dac, av, av, av, cw)


def _norm_mod_bwd(x_ref, nw_ref, mod_ref, k_shift, dh, dx_in, dx_ref, dnw_ref, dmod_ref, is_ctx):
    _, vjp = jax.vjp(_norm_mod, x_ref[...], nw_ref[...], mod_ref[0, k_shift], mod_ref[0, k_shift + 1])
    dx, dnw, dshift, dscale = vjp(dh)
    dx_ref[...] = dx_in + dx
    dnw_ref[...] += dnw
    _stream_add(dmod_ref, 0, is_ctx, dshift)
    _stream_add(dmod_ref, 1, is_ctx, dscale)


def ffn_up_bwd_x(dx2, x, dav, mod, nw, wg, skip_ctx):
    t = x.shape[0]

    def body(dx2_ref, x_ref, dav_ref, mod_ref, nw_ref, w_ref, dx_ref, dnw_ref, dmod_ref):
        i = pl.program_id(0)

        @pl.when(i == 0)
        def _():
            dnw_ref[...] = jnp.zeros_like(dnw_ref)
            dmod_ref[...] = jnp.zeros_like(dmod_ref)

        def work():
            dh = mm_nt(dav_ref[0], w_ref[0])
            for j in range(1, N_DEV):
                dh = dh + mm_nt(dav_ref[j], w_ref[j])
            _norm_mod_bwd(x_ref, nw_ref, mod_ref, 3, dh, dx2_ref[...], dx_ref, dnw_ref, dmod_ref, i == 0)

        _unless_ctx(skip_ctx, i == 0, (dx_ref,), work)

    tile = pl.BlockSpec((TM, D), lambda i: (i, 0))
    return pl.pallas_call(
        body, name="ffn_up_bwd_x", grid=(t // TM,),
        in_specs=[tile, tile, pl.BlockSpec((N_DEV, TM, FF_SLOT), lambda i: (0, i, 0)), pl.BlockSpec((1, 6, 1, D), _stream_row(TM)),
                  pl.BlockSpec((1, D), lambda i: (0, 0)), VMEM_WHOLE],
        out_specs=[tile, pl.BlockSpec((1, D), lambda i: (0, 0)), pl.BlockSpec((2, 2, 1, D), lambda i: (0, 0, 0, 0))],
        out_shape=[jax.ShapeDtypeStruct((t, D), F32), jax.ShapeDtypeStruct((1, D), F32), jax.ShapeDtypeStruct((2, 2, 1, D), F32)],
        compiler_params=_cp(1))(dx2, x, dav, mod, nw, wg)


def weight_grad(at, dout, slot, name, after=None):
    rows, t = at.shape
    stacked = dout.ndim == 3
    n = dout.shape[0] if stacked else dout.shape[1] // slot

    def body(a_ref, d_ref, *rest):
        dw_ref = rest[-1]
        dw_ref[0] = jnp.dot(a_ref[...], d_ref[0] if stacked else d_ref[...], preferred_element_type=F32).astype(dw_ref.dtype)

    d_spec = pl.BlockSpec((1, t, slot), lambda j: (j, 0, 0)) if stacked else pl.BlockSpec((t, slot), lambda j: (0, j))
    extra = [] if after is None else [jnp.reshape(after, (1, 1))]
    return pl.pallas_call(
        body, name=name, grid=(n,), in_specs=[VMEM_WHOLE, d_spec] + [ANY] * len(extra),
        out_specs=pl.BlockSpec((1, rows, slot), lambda j: (j, 0, 0)),
        out_shape=jax.ShapeDtypeStruct((n, rows, slot), GRAD_WIRE), compiler_params=_cp(1))(at, dout, *extra)


def weight_grad_rows(at, dout, name):
    n, t, rows = at.shape
    cols = dout.shape[1]

    def body(a_ref, d_ref, dw_ref):
        dw_ref[0] = _dot(a_ref[0], d_ref[...], ((0,), (0,))).astype(dw_ref.dtype)

    return pl.pallas_call(
        body, name=name, grid=(n,), in_specs=[pl.BlockSpec((1, t, rows), lambda k: (k, 0, 0)), VMEM_WHOLE],
        out_specs=pl.BlockSpec((1, rows, cols), lambda k: (k, 0, 0)),
        out_shape=jax.ShapeDtypeStruct((n, rows, cols), GRAD_WIRE), compiler_params=_cp(1))(at, dout)


def mixer_bwd(dx, parts, o, pa, pb, y, mod, lnw, lnb, sw, sb, hnw, wa, wb, wo, skip_ctx):
    t = dx.shape[0]
    tm = TM
    n_ctx = CTX // tm

    def body(dx_ref, u_ref, v_ref, og_ref, ga_ref, gb_ref, o_ref, pa_ref, pb_ref, y_ref, mod_ref, lnw_ref, lnb_ref, sw_ref,
             sb_ref, hnw_ref, wa_ref, wb_ref, wo_ref, dp_ref, do_ref, dy_ref, dpa_ref, dpb_ref, dlnw_ref, dlnb_ref, dsw_ref,
             dsb_ref, dhnw_ref, dg_ref):
        i = pl.program_id(0)

        @pl.when(i == 0)
        def _():
            for r in (dlnw_ref, dlnb_ref, dsw_ref, dsb_ref, dhnw_ref, dg_ref):
                r[...] = jnp.zeros_like(r)

        def work():
            _, _, vjps, vjp_b = _mixer_tile(slice(0, tm), u_ref, v_ref, og_ref, o_ref, lnw_ref, lnb_ref, sw_ref, sb_ref, hnw_ref)
            pa, pb = pa_ref[...].astype(F32), pb_ref[...].astype(F32)
            sa, sbg = jax.nn.sigmoid(ga_ref[...]), jax.nn.sigmoid(gb_ref[...])
            dxv = dx_ref[...]
            _stream_add(dg_ref, 0, i < n_ctx, jnp.sum(dxv * y_ref[...].astype(F32), axis=0, keepdims=True))
            dy = (mod_ref[0, 2] * dxv).astype(BF16)
            dy_ref[...] = dy
            dmerged = mm_nt(dy, wo_ref[...])
            dpa, dpb = (sa * dmerged).astype(BF16), (sbg * dmerged).astype(BF16)
            dpa_ref[...], dpb_ref[...] = dpa, dpb
            first = 4 * D
            dp_ref[:, first + 3 * D:first + 4 * D] = (dmerged * pa * sa * (1.0 - sa)).astype(BF16)
            dp_ref[:, first + 4 * D:first + 5 * D] = (dmerged * pb * sbg * (1.0 - sbg)).astype(BF16)
            dya = mm_nt(dpa, wa_ref[...])
            dob, dog, dhnw = vjp_b(mm_nt(dpb, wb_ref[...]))
            dp_ref[:, first + 2 * D:first + 3 * D] = dog.astype(BF16)
            dhnw_ref[...] += dhnw
            for g in range(HEADS):
                do_ref[:, _hsl(g)] = dob[g]
            for c, vjp_a in enumerate(vjps):
                rows = slice(c * SGU_CH, (c + 1) * SGU_CH)
                dub, dvb, dlnw, dlnb, dsw, dsb = vjp_a(dya[rows])
                for g in range(HEADS):
                    dp_ref[rows, first + g * HD:first + (g + 1) * HD] = dub[g].astype(BF16)
                    dp_ref[rows, first + D + g * HD:first + D + (g + 1) * HD] = dvb[g].astype(BF16)
                    dlnw_ref[:, _hsl(g)] += dlnw[g]
                    dlnb_ref[:, _hsl(g)] += dlnb[g]
                    dsw_ref[g] += dsw[g]
                    dsb_ref[g] += dsb[g]

        _unless_ctx(skip_ctx, i < n_ctx, (dp_ref, do_ref, dy_ref, dpa_ref, dpb_ref), work)

    vec = lambda n: pl.BlockSpec((1, n), lambda i: (0, 0))
    tile = pl.BlockSpec((tm, D), lambda i: (i, 0))
    sds = jax.ShapeDtypeStruct
    return pl.pallas_call(
        body, name="mixer_bwd", grid=(t // tm,),
        in_specs=[tile] + _part_specs(tm, 4, 5)
        + [pl.BlockSpec((2, tm, D), lambda i: (0, i, 0)), tile, tile, tile, pl.BlockSpec((1, 6, 1, D), _stream_row(tm)),
           vec(D), vec(D), VMEM_WHOLE, VMEM_WHOLE, vec(HD), VMEM_WHOLE, VMEM_WHOLE, VMEM_WHOLE],
        out_specs=[pl.BlockSpec((tm, D_IN), lambda i: (i, 0)), tile, tile, tile, tile, vec(D), vec(D),
                   VMEM_WHOLE, VMEM_WHOLE, vec(HD), pl.BlockSpec((2, 1, 1, D), lambda i: (0, 0, 0, 0))],
        out_shape=[sds((t, D_IN), BF16), sds((t, D), F32), sds((t, D), BF16), sds((t, D), BF16), sds((t, D), BF16),
                   sds((1, D), F32), sds((1, D), F32), sds((HEADS, SGU_CH, SGU_CH), F32), sds((HEADS, SGU_CH, 1), F32),
                   sds((1, HD), F32), sds((2, 1, 1, D), F32)],
        compiler_params=_cp(1))(dx, parts, parts, parts, parts, parts, o, pa, pb, y, mod, lnw, lnb, sw, sb, hnw, wa, wb, wo)


def hgrn_bwd(d, parts, lb, mc, mtc, mrefc, ck, do, first=None, dparts=None):
    t = parts.shape[0]
    nb = t // SCAN_ROWS
    block = _scan_block(nb)
    rev = lambda s: block(d, nb - 1 - s)

    def body(q_ref, f_ref, i_ref, lb_ref, m_ref, mt_ref, mr_ref, ck_ref, do_ref, *rest):
        dst = rest[-1]
        dlb_ref = rest[-2]

        @pl.when(pl.program_id(0) == 0)
        def _():
            dst[...] = jnp.zeros_like(dst)
            dlb_ref[...] = jnp.zeros_like(dlb_ref)

        heads = range(HEADS)
        fn = functools.partial(_hgrn_chunk, m=m_ref[0], mt=mt_ref[0], mref=mr_ref[0])
        for c in reversed(range(SCAN_STEP)):
            first_row = c * CH if d == 0 else (SCAN_STEP - 1 - c) * CH
            rows = slice(first_row, first_row + CH)
            _, vjp = jax.vjp(fn, [ck_ref[0, c, h].astype(F32) for h in heads], [q_ref[rows, _hsl(h)] for h in heads],
                             [f_ref[rows, _hsl(h)] for h in heads], [i_ref[rows, _hsl(h)] for h in heads],
                             [lb_ref[0, :, _hsl(h)] for h in heads])
            dstl, dq, df, di, dlb = vjp(([do_ref[rows, _hsl(h)] for h in heads], [dst[h] for h in heads]))
            for h in heads:
                dst[h] = dstl[h]
                dlb_ref[0, :, _hsl(h)] += dlb[h]
                if d == 0:
                    dq_ref, df_ref, di_ref = rest[:3]
                    dq_ref[rows, _hsl(h)] = dq[h].astype(BF16)
                    df_ref[rows, _hsl(h)] = df[h].astype(BF16)
                    di_ref[rows, _hsl(h)] = di[h].astype(BF16)
                else:
                    dq0_ref, df0_ref, di0_ref, _, dp_ref = rest[:5]
                    col = lambda k: slice(k * D + h * HD, k * D + (h + 1) * HD)
                    dp_ref[rows, col(0)] = (dq0_ref[rows, _hsl(h)].astype(F32) + dq[h]).astype(BF16)
                    dp_ref[rows, col(1)] = df0_ref[rows, _hsl(h)]
                    dp_ref[rows, col(2)] = df[h].astype(BF16)
                    dp_ref[rows, col(3)] = (di0_ref[rows, _hsl(h)].astype(F32) + di[h]).astype(BF16)

    const = lambda s: (d, 0, 0)
    at = lambda k: pl.BlockSpec((SCAN_ROWS, D), lambda s: (rev(s), k))
    in_specs = [at(0), at(1 + d), at(3), pl.BlockSpec((1, 1, D), const), pl.BlockSpec((1, CH, CH), const),
                pl.BlockSpec((1, CH, CH), const), pl.BlockSpec((1, CH, 1), const),
                pl.BlockSpec((1, SCAN_STEP, HEADS, HD, HD), lambda s: (d, nb - 1 - s, 0, 0, 0)), at(0)]
    dlb_spec, dlb_shape = pl.BlockSpec((1, 1, D), lambda s: (0, 0, 0)), jax.ShapeDtypeStruct((1, 1, D), F32)
    common = dict(grid=(nb,), scratch_shapes=[pltpu.VMEM((HEADS, HD, HD), F32)], compiler_params=_cp(1))
    if d == 0:
        return pl.pallas_call(body, name="hgrn_bwd_fwd_dir", in_specs=in_specs, out_specs=[at(0)] * 3 + [dlb_spec],
                              out_shape=[jax.ShapeDtypeStruct((t, D), BF16)] * 3 + [dlb_shape], **common,
                              )(parts, parts, parts, lb, mc, mtc, mrefc, ck, do)
    return pl.pallas_call(body, name="hgrn_bwd_bwd_dir", in_specs=in_specs + [at(0)] * 3 + [ANY],
                          out_specs=[pl.BlockSpec((SCAN_ROWS, 4 * D), lambda s: (rev(s), 0)), dlb_spec],
                          out_shape=[jax.ShapeDtypeStruct(dparts.shape, BF16), dlb_shape], input_output_aliases={12: 0},
                          **common)(parts, parts, parts, lb, mc, mtc, mrefc, ck, do, *first, dparts)


def in_proj_bwd_x(dx1, x, dparts, mod, nw, wg, after=None, latent_only=False):
    t = x.shape[0]
    tm = TM
    n_ctx = CTX // tm

    def body(dx1_ref, x_ref, dp_ref, mod_ref, nw_ref, w_ref, *rest):
        dx_ref, dnw_ref, dmod_ref = rest[-3:]
        i = pl.program_id(0)

        @pl.when(i == 0)
        def _():
            dnw_ref[...] = jnp.zeros_like(dnw_ref)
            dmod_ref[...] = jnp.zeros_like(dmod_ref)

        dh = mm_nt(dp_ref[:, 0:IN_SLOT], w_ref[0])
        for j in range(1, N_DEV):
            dh = dh + mm_nt(dp_ref[:, j * IN_SLOT:(j + 1) * IN_SLOT], w_ref[j])
        _norm_mod_bwd(x_ref, nw_ref, mod_ref, 0, dh, dx1_ref[...], dx_ref, dnw_ref, dmod_ref, i < n_ctx)

    tile = pl.BlockSpec((tm, D), lambda i: (i, 0))
    extra = [] if after is None else [jnp.reshape(after, (1, 1))]
    return pl.pallas_call(
        body, name="in_proj_bwd_x", grid=(t // tm,),
        in_specs=[tile, tile, pl.BlockSpec((tm, D_IN), lambda i: (i, 0)), pl.BlockSpec((1, 6, 1, D), _stream_row(tm)),
                  pl.BlockSpec((1, D), lambda i: (0, 0)), VMEM_WHOLE] + [ANY] * len(extra),
        out_specs=[pl.BlockSpec((tm, D), lambda i: (jnp.maximum(i - n_ctx, 0), 0)) if latent_only else tile,
                   pl.BlockSpec((1, D), lambda i: (0, 0)), pl.BlockSpec((2, 2, 1, D), lambda i: (0, 0, 0, 0))],
        out_shape=[jax.ShapeDtypeStruct((t - CTX if latent_only else t, D), F32), jax.ShapeDtypeStruct((1, D), F32),
                   jax.ShapeDtypeStruct((2, 2, 1, D), F32)],
        compiler_params=_cp(1))(dx1, x, dparts, mod, nw, wg, *extra)


def _lb_fn(h0, h1):
    m = jnp.maximum(h0, h1)
    e0, e1 = jnp.exp(h0 - m), jnp.exp(h1 - m)
    return e1 / (e0 + e1)


def lower_bounds(hlb):
    def body(h_ref, out_ref):
        out_ref[...] = _lb_fn(h_ref[0:1, :], h_ref[1:2, :])
    return pl.pallas_call(body, name="lower_bounds", out_shape=jax.ShapeDtypeStruct((1, 2 * D), F32))(hlb)


def lower_bounds_bwd(hlb, dlb1):
    def body(h_ref, d_ref, out_ref):
        _, vjp = jax.vjp(_lb_fn, h_ref[0:1, :], h_ref[1:2, :])
        d0, d1 = vjp(d_ref[...])
        out_ref[0:1, :] = d0
        out_ref[1:2, :] = d1
    return pl.pallas_call(body, name="lower_bounds_bwd", out_shape=jax.ShapeDtypeStruct((2, 2 * D), F32))(hlb, dlb1)


def _ada_fn(c_all, cctx8, w, b):
    dot = lambda a, l: mm(_silu(a), w[l]) + b[l]
    return [dot(c_all, l) for l in range(2)], [dot(cctx8, l) for l in range(2)]


def ada_fwd(c_all, cctx8, w, b):
    cols = w.shape[-1]

    def body(c_ref, cc_ref, w_ref, b_ref, out_ref):
        ox, oc = _ada_fn(c_ref[...], cc_ref[...], [w_ref[0], w_ref[1]], [b_ref[0], b_ref[1]])
        for l in range(2):
            out_ref[l, 0] = ox[l]
            out_ref[l, 1] = oc[l]
    return pl.pallas_call(body, name="ada_fwd", out_shape=jax.ShapeDtypeStruct((2, 2, N_DEV, cols), F32),
                          compiler_params=_cp(0))(c_all, cctx8, w, b)


def ada_bwd(c_all, cctx8, w, b, dmx, dmc):
    cols = w.shape[-1]

    def body(c_ref, cc_ref, w_ref, b_ref, dmx_ref, dmc_ref, dw_ref, dc_ref):
        fn = lambda cc, w0, w1: _ada_fn(c_ref[...], cc, [w0, w1], [b_ref[0], b_ref[1]])
        _, vjp = jax.vjp(fn, cc_ref[...], w_ref[0], w_ref[1])
        dcc, dw0, dw1 = vjp(([dmx_ref[0], dmx_ref[1]], [dmc_ref[0], dmc_ref[1]]))
        dw_ref[0] = dw0
        dw_ref[1] = dw1
        dc_ref[...] = jnp.sum(dcc, axis=0, keepdims=True)
    return pl.pallas_call(body, name="ada_bwd", out_shape=[jax.ShapeDtypeStruct((2, D, cols), F32), jax.ShapeDtypeStruct((1, D), F32)],
                          compiler_params=_cp(0))(c_all, cctx8, w, b, dmx, dmc)


def adamw(w, m, v, gparts, name):
    r, c = w.shape
    p = gparts.shape[0]
    rt = r
    while rt % 16 == 0 and (p + 7) * rt * c * 4 * 2 > ELEMENTWISE_VMEM:
        rt //= 2

    def body(w_ref, m_ref, v_ref, g_ref, go_ref, d_ref, mo_ref, vo_ref):
        g = g_ref[0].astype(F32)
        for k in range(1, p):
            g = g + g_ref[k].astype(F32)
        m2 = ADAM_B1 * m_ref[...] + (1.0 - ADAM_B1) * g
        v2 = ADAM_B2 * v_ref[...] + (1.0 - ADAM_B2) * (g * g)
        m_hat = m2 / (1.0 - ADAM_B1 ** ADAM_STEP)
        v_hat = v2 / (1.0 - ADAM_B2 ** ADAM_STEP)
        go_ref[...] = g
        d_ref[...] = -ADAM_LR * (m_hat / (jnp.sqrt(v_hat) + ADAM_EPS) + ADAM_WD * w_ref[...])
        mo_ref[...] = m2
        vo_ref[...] = v2

    tile = pl.BlockSpec((rt, c), lambda i: (i, 0))
    return pl.pallas_call(
        body, name=name, grid=(r // rt,),
        in_specs=[tile, tile, tile, pl.BlockSpec((p, rt, c), lambda i: (0, i, 0))], out_specs=[tile] * 4,
        out_shape=[jax.ShapeDtypeStruct((r, c), F32)] * 4, compiler_params=_cp(1))(w, m, v, gparts)


def _me():
    x, y, c = lax.axis_index("x"), lax.axis_index("y"), lax.axis_index("c")
    return x, y, c, 4 * x + 2 * y + c


def _peer(x, y, c, p):
    fx, fy, fc = (p >> 2) & 1, (p >> 1) & 1, p & 1
    return (1 - x if fx else x, 1 - y if fy else y, 1 - c if fc else c)


def all_gather(arrs, name, after=None):
    n = len(arrs)
    extra = [] if after is None else list(after) if isinstance(after, (list, tuple)) else [after]

    def body(*refs):
        ins, outs = refs[:n], refs[n + len(extra):2 * n + len(extra)]
        send, recv, local = refs[2 * n + len(extra):]
        x, y, c, me = _me()
        copies = []
        for a in range(n):
            lc = pltpu.make_async_copy(ins[a], outs[a].at[me], local.at[a])
            lc.start()
            copies.append(lc)
            for p in range(1, N_DEV):
                cp = pltpu.make_async_remote_copy(src_ref=ins[a], dst_ref=outs[a].at[me], send_sem=send.at[a, p - 1],
                                                  recv_sem=recv.at[a, p - 1], device_id=_peer(x, y, c, p),
                                                  device_id_type=pl.DeviceIdType.MESH)
                cp.start()
                copies.append(cp)
        for cp in copies:
            cp.wait()

    return pl.pallas_call(
        body, name=name, in_specs=[ANY] * (n + len(extra)), out_specs=[ANY] * n,
        out_shape=[jax.ShapeDtypeStruct((N_DEV,) + a.shape, a.dtype) for a in arrs],
        scratch_shapes=[pltpu.SemaphoreType.DMA((n, N_DEV - 1)), pltpu.SemaphoreType.DMA((n, N_DEV - 1)),
                        pltpu.SemaphoreType.DMA((n,))])(*arrs, *extra)


HBM = pl.BlockSpec(memory_space=pltpu.HBM)
SEM = pl.BlockSpec(memory_space=pltpu.SEMAPHORE)


def _in_hbm(a):
    return pltpu.with_memory_space_constraint(a, pltpu.HBM)


ALL_PEERS = tuple(range(1, N_DEV))
SAME_CORE_AND_SIBLING = (1, 2, 4, 6)
OTHER_CHIPS = (2, 4, 6)


def _exchange_refs(srcs, lands, layer, scatter, a, x, y, c, p, forward=False):
    me = 4 * x + 2 * y + c
    px, py, pc = _peer(x, y, c, p) if p else (x, y, c)
    if forward and p:
        slot = lands[a].at[4 * px + 2 * py + pc]
        return slot, slot, _peer(x, y, c, 1)
    dst = lands[a].at[me] if layer is None else lands[a].at[me, layer]
    src = srcs[a].at[4 * px + 2 * py + pc] if scatter else dst
    return src, dst, (px, py, pc)


def exchange_start(srcs, lands, layer, scatter, name, after=None, peers=ALL_PEERS, forward=False):
    n, ns = len(lands), len(srcs)
    extra = [] if after is None else [after]

    def body(*refs):
        ins, lz = refs[:ns], refs[ns:ns + n]
        send, recv = refs[ns + n + len(extra)], refs[ns + n + len(extra) + 1]
        token = refs[-1]
        x, y, c, _ = _me()
        for a in range(n):
            for p in peers:
                src, dst, peer = _exchange_refs(ins, lz, layer, scatter, a, x, y, c, p, forward)
                k = a * (N_DEV - 1) + p - 1
                pltpu.make_async_remote_copy(src_ref=src, dst_ref=dst, send_sem=send.at[k], recv_sem=recv.at[k],
                                             device_id=peer, device_id_type=pl.DeviceIdType.MESH).start()
        token[...] = jnp.zeros_like(token)

    thru = [pltpu.HBM(a.shape, a.dtype) for a in list(srcs) + list(lands)]
    out = pl.pallas_call(
        body, name=name, in_specs=[HBM] * (ns + n) + [ANY] * len(extra),
        out_specs=[SEM, SEM] + [HBM] * (ns + n) + [pl.BlockSpec(memory_space=pltpu.VMEM)],
        out_shape=[pltpu.SemaphoreType.DMA((n * (N_DEV - 1),)), pltpu.SemaphoreType.DMA((n * (N_DEV - 1),))] + thru
        + [jax.ShapeDtypeStruct((8, 128), F32)],
        input_output_aliases={i: 2 + i for i in range(ns + n)},
        compiler_params=pltpu.CompilerParams(has_side_effects=pltpu.SideEffectType.DATAFLOW_SIDE_EFFECTING),
    )(*[_in_hbm(a) for a in list(srcs) + list(lands)], *extra)
    return out[0], out[1], out[2:2 + ns], out[2 + ns:2 + ns + n], out[-1]


def exchange_wait(send, recv, srcs, lands, layer, scatter, after, name, peers=ALL_PEERS):
    n, ns = len(lands), len(srcs)

    def body(*refs):
        ins, lz = refs[:ns], refs[ns:ns + n]
        send_ref, recv_ref = refs[ns + n], refs[ns + n + 1]
        x, y, c, _ = _me()
        for a in range(n):
            for p in peers:
                src, dst, peer = _exchange_refs(ins, lz, layer, scatter, a, x, y, c, 0)
                k = a * (N_DEV - 1) + p - 1
                cp = pltpu.make_async_remote_copy(src_ref=src, dst_ref=dst, send_sem=send_ref.at[k],
                                                  recv_sem=recv_ref.at[k], device_id=peer,
                                                  device_id_type=pl.DeviceIdType.MESH)
                cp.wait_send()
                cp.wait_recv()

    thru = [pltpu.HBM(a.shape, a.dtype) for a in list(srcs) + list(lands)]
    out = pl.pallas_call(
        body, name=name, in_specs=[HBM] * (ns + n) + [SEM, SEM, ANY], out_specs=[HBM] * (ns + n), out_shape=thru,
        input_output_aliases={i: i for i in range(ns + n)},
        compiler_params=pltpu.CompilerParams(has_side_effects=pltpu.SideEffectType.DATAFLOW_SIDE_EFFECTING),
    )(*srcs, *lands, send, recv, after)
    return out[ns:]


def place_own(src, land, me, layer, scatter, name, src_layer=None, after=None):
    create = isinstance(land, jax.ShapeDtypeStruct)
    r, c = src.shape[-2:]
    rt = r
    while rt % 32 == 0 and rt * c * 4 > COPY_BLOCK_BYTES:
        rt //= 2

    extra = [] if after is None else [after]

    def body(me_ref, src_ref, *rest):
        out_ref = rest[-1]
        out_ref[...] = src_ref[...].reshape(out_ref.shape).astype(out_ref.dtype)

    src_spec = (pl.BlockSpec((1, rt, c), lambda i, m: (m[0], i, 0)) if scatter else
                pl.BlockSpec((rt, c), lambda i, m: (i, 0)) if src_layer is None else
                pl.BlockSpec((1, rt, c), lambda i, m: (src_layer, i, 0)))
    out_spec = (pl.BlockSpec((1, rt, c), lambda i, m: (m[0], i, 0)) if layer is None
                else pl.BlockSpec((1, 1, rt, c), lambda i, m: (m[0], layer, i, 0)))
    grid_spec = pltpu.PrefetchScalarGridSpec(num_scalar_prefetch=1, grid=(r // rt,),
                                             in_specs=[src_spec] + ([] if create else [ANY]) + [ANY] * len(extra),
                                             out_specs=out_spec)
    return pl.pallas_call(body, name=name, grid_spec=grid_spec, out_shape=jax.ShapeDtypeStruct(land.shape, land.dtype),
                          input_output_aliases={} if create else {2: 0}, compiler_params=_cp(1),
                          )(*((me, src) if create else (me, src, land)), *extra)


def _scan_constants():
    r = lax.broadcasted_iota(jnp.int32, (CH, CH), 0)
    s = lax.broadcasted_iota(jnp.int32, (CH, CH), 1)
    lower = (s <= r).astype(F32)
    t = jnp.arange(CH)[:, None]
    mc = jnp.stack([lower, lower.T])
    mref = jnp.stack([(t <= CH // 2 - 1).astype(F32), (t >= CH // 2).astype(F32)])
    return mc, jnp.stack([lower.T, lower]), mref


def local_step(x, ctx, target, mod, lb, w, fetch=None, publish=None, small_ready=None, small_early=None):
    kept = {}

    def keep(l, part, grads):
        kept[(l, part)] = grads
        return 0.0

    fetch = fetch or (lambda l, part, after: w)
    publish = publish or keep
    n_layers = len(mod)
    mc, mtc, mrefc = _scan_constants()
    xs = jnp.concatenate([ctx, x], axis=0)
    saved, big = [], []
    for l in range(n_layers):
        wl = dict(fetch(l, "in", xs))
        parts, ht, iv, mix = in_proj_fwd(xs, mod[l], w["nw1"][l], wl["win"][l])
        o, ck = hgrn_fwd(parts, iv, lb[l], mc, mtc, mrefc)
        wl.update(fetch(l, "mix", o))
        last = l == n_layers - 1
        x1, pa, pb, ym, yat, ybt, mt = mixer_fwd(xs, mix, o, mod[l], w["lnw"][l], w["lnb"][l], w["sw"][l], w["sb"][l],
                                                 w["hnw"][l], wl["wa"][l], wl["wb"][l], wl["wo"][l], last)
        wl.update(fetch(l, "ffn", x1))
        av, h2t, vb = ffn_up_fwd(x1, mod[l], w["nw2"][l], wl["wup"][l], last)
        x2, ac, y, z = ffn_down_fwd(x1, av, vb, mod[l], w["cw"][l], w["cb"][l], wl["wd"][l], last)
        saved.append((xs, parts, o, ck, x1, av, vb, ac, y, z, ht, h2t, pa, pb, ym, yat, ybt, mt))
        big.append(wl)
        xs = x2
    loss, dx, dfw = loss_fwd_bwd(xs, target, w["fw"])
    g = {k: [None] * n_layers for k in ("nw1", "nw2", "lnw", "lnb", "sw", "sb", "hnw", "cw", "cb")}
    g["fw"] = dfw
    dmod, dlb = [None] * n_layers, [None] * n_layers
    tok = 0.0
    for l in reversed(range(n_layers)):
        x0, parts, o, ck, x1, av, vb, ac, y, z, ht, h2t, pa, pb, ym, yat, ybt, mt = saved[l]
        wl = big[l]
        last = l == n_layers - 1
        dav, dac, dout, dg2 = ffn_down_bwd(dx, ac, vb, y, mod[l] + tok, wl["wd"][l], last)
        dwd = weight_grad_rows(z, dout, "ffn_down_bwd_w")
        dav, g["cw"][l], g["cb"][l] = conv_bwd(dav, dac, av, w["cw"][l], last)
        dx1, g["nw2"][l], dmod2 = ffn_up_bwd_x(dx, x1, dav, mod[l], w["nw2"][l], wl["wup"][l], last)
        dwup = weight_grad(h2t, dav, FF_SLOT, "ffn_up_bwd_w")
        tok = publish(l, "ffn", {"wd": dwd, "wup": dwup})
        (dparts, do, dy, dpa, dpb, g["lnw"][l], g["lnb"][l], g["sw"][l], g["sb"][l], g["hnw"][l],
         dg1) = mixer_bwd(dx1, parts, o, pa, pb, ym, mod[l] + tok, w["lnw"][l], w["lnb"][l], w["sw"][l], w["sb"][l],
                          w["hnw"][l], wl["wa"][l], wl["wb"][l], wl["wo"][l], last)
        tok = publish(l, "mix", {"wa": weight_grad(yat, dpa, D, "mixer_bwd_wa"), "wb": weight_grad(ybt, dpb, D, "mixer_bwd_wb"),
                                 "wo": weight_grad(mt, dy, D, "mixer_bwd_wo")})
        if l == 0 and small_early:
            dmod[0] = jnp.concatenate([jnp.zeros((2, 2, 1, D), F32), dg1, dmod2, dg2], axis=1)
            tok = tok + small_early(loss[0, 0], g, dmod, dlb)
        dq, df, di, dlb_f = hgrn_bwd(0, parts, lb[l] + tok, mc, mtc, mrefc, ck, do)
        dparts, dlb_b = hgrn_bwd(1, parts, lb[l], mc, mtc, mrefc, ck, do, (dq, df, di), dparts)
        dlb[l] = jnp.concatenate([dlb_f, dlb_b], axis=0)
        tok = publish(l, "in", {"win": weight_grad(ht, dparts, IN_SLOT, "in_proj_bwd_w")})
        dx, g["nw1"][l], dmod1 = in_proj_bwd_x(dx1, x0, dparts, mod[l], w["nw1"][l], wl["win"][l], after=tok,
                                               latent_only=l == 0)
        dmod[l] = jnp.concatenate([dmod1, dg1, dmod2, dg2], axis=1)
    done = small_ready(loss[0, 0], g, dmod, dlb) if small_ready else 0.0
    for (l, part), grads in kept.items():
        for k, v in grads.items():
            g.setdefault(k, [None] * n_layers)[l] = v
    return loss[0, 0], dx, g, dmod, dlb, done


ROW = 1024
REPLICATED = ("norm1_w", "sgu_ln_w", "sgu_ln_b", "sgu_w", "sgu_b", "hgrn_lower_bounds", "hgrn_norm_w", "norm2_w",
              "ffn_conv_b", "final_norm_w")
WEIGHT_ORDER = ("c_ctx", "ada_w", "ada_b", "norm1_w", "w_in", "sgu_ln_w", "sgu_ln_b", "sgu_w", "sgu_b", "hgrn_lower_bounds",
                "hgrn_norm_w", "w_branch_a", "w_branch_b", "w_out", "norm2_w", "ffn_w_up", "ffn_conv_w", "ffn_conv_b",
                "ffn_w_down", "final_norm_w")


def _rows_of(n):
    return -(-n // (8 * ROW)) * 8


def _pack(arrs, total_rows=None):
    parts = []
    for a in arrs:
        flat = a.reshape(-1).astype(F32)
        rows = _rows_of(flat.shape[0])
        parts.append(jnp.pad(flat, (0, rows * ROW - flat.shape[0])).reshape(rows, ROW))
    have = sum(p.shape[0] for p in parts)
    if total_rows is not None and total_rows > have:
        parts.append(jnp.zeros((total_rows - have, ROW), F32))
    return jnp.concatenate(parts, axis=0)


def _unpack(packed, shapes):
    lead = packed.shape[:-2]
    out, r0 = [], 0
    for s in shapes:
        n = math.prod(s)
        rows = _rows_of(n)
        out.append(packed[..., r0:r0 + rows, :].reshape(lead + (rows * ROW,))[..., :n].reshape(lead + tuple(s)))
        r0 += rows
    return out


def kernel(x, c, ctx, c_ctx, ada_w, ada_b, norm1_w, w_in, sgu_ln_w, sgu_ln_b, sgu_w, sgu_b, hgrn_lower_bounds, hgrn_norm_w, w_branch_a, w_branch_b, w_out, norm2_w, ffn_w_up, ffn_conv_w, ffn_conv_b, ffn_w_down, final_norm_w, loss_target, m_c_ctx, m_ada_w, m_ada_b, m_norm1_w, m_w_in, m_sgu_ln_w, m_sgu_ln_b, m_sgu_w, m_sgu_b, m_hgrn_lower_bounds, m_hgrn_norm_w, m_w_branch_a, m_w_branch_b, m_w_out, m_norm2_w, m_ffn_w_up, m_ffn_conv_w, m_ffn_conv_b, m_ffn_w_down, m_final_norm_w, v_c_ctx, v_ada_w, v_ada_b, v_norm1_w, v_w_in, v_sgu_ln_w, v_sgu_ln_b, v_sgu_w, v_sgu_b, v_hgrn_lower_bounds, v_hgrn_norm_w, v_w_branch_a, v_w_branch_b, v_w_out, v_norm2_w, v_ffn_w_up, v_ffn_conv_w, v_ffn_conv_b, v_ffn_w_down, v_final_norm_w):
    wts = dict(c_ctx=c_ctx, ada_w=ada_w, ada_b=ada_b, norm1_w=norm1_w, w_in=w_in, sgu_ln_w=sgu_ln_w, sgu_ln_b=sgu_ln_b,
               sgu_w=sgu_w, sgu_b=sgu_b, hgrn_lower_bounds=hgrn_lower_bounds, hgrn_norm_w=hgrn_norm_w, w_branch_a=w_branch_a,
               w_branch_b=w_branch_b, w_out=w_out, norm2_w=norm2_w, ffn_w_up=ffn_w_up, ffn_conv_w=ffn_conv_w,
               ffn_conv_b=ffn_conv_b, ffn_w_down=ffn_w_down, final_norm_w=final_norm_w)
    mom1 = dict(c_ctx=m_c_ctx, ada_w=m_ada_w, ada_b=m_ada_b, norm1_w=m_norm1_w, w_in=m_w_in, sgu_ln_w=m_sgu_ln_w,
                sgu_ln_b=m_sgu_ln_b, sgu_w=m_sgu_w, sgu_b=m_sgu_b, hgrn_lower_bounds=m_hgrn_lower_bounds,
                hgrn_norm_w=m_hgrn_norm_w, w_branch_a=m_w_branch_a, w_branch_b=m_w_branch_b, w_out=m_w_out, norm2_w=m_norm2_w,
                ffn_w_up=m_ffn_w_up, ffn_conv_w=m_ffn_conv_w, ffn_conv_b=m_ffn_conv_b, ffn_w_down=m_ffn_w_down,
                final_norm_w=m_final_norm_w)
    mom2 = dict(c_ctx=v_c_ctx, ada_w=v_ada_w, ada_b=v_ada_b, norm1_w=v_norm1_w, w_in=v_w_in, sgu_ln_w=v_sgu_ln_w,
                sgu_ln_b=v_sgu_ln_b, sgu_w=v_sgu_w, sgu_b=v_sgu_b, hgrn_lower_bounds=v_hgrn_lower_bounds,
                hgrn_norm_w=v_hgrn_norm_w, w_branch_a=v_w_branch_a, w_branch_b=v_w_branch_b, w_out=v_w_out, norm2_w=v_norm2_w,
                ffn_w_up=v_ffn_w_up, ffn_conv_w=v_ffn_conv_w, ffn_conv_b=v_ffn_conv_b, ffn_w_down=v_ffn_w_down,
                final_norm_w=v_final_norm_w)
    n_layers = w_in.shape[0]
    layers = range(n_layers)
    me = 4 * lax.axis_index("x") + 2 * lax.axis_index("y") + lax.axis_index("c")
    ada_cols = ada_w.shape[-1]

    big = ("w_in", "ffn_w_up", "w_branch_a", "w_branch_b", "w_out", "ffn_w_down")
    short = {"w_in": "win", "ffn_w_up": "wup", "w_branch_a": "wa", "w_branch_b": "wb", "w_out": "wo", "ffn_w_down": "wd"}
    me1 = me.reshape(1).astype(jnp.int32)
    mixer, ffn = ("w_branch_a", "w_branch_b", "w_out"), ("ffn_w_up", "ffn_w_down")
    groups = [[(k, l) for k in part] for l in layers for part in (("w_in",), mixer, ffn)]
    group_of = {(l, part): 3 * l + n for l in layers for n, part in enumerate(("in", "mix", "ffn"))}
    in_flight, started = [], 0.0

    def own_slots(n, after):
        return [place_own(wts[k], jax.ShapeDtypeStruct((N_DEV,) + wts[k].shape[1:], BF16), me1, None, False,
                          f"gather_own_{short[k]}_{l}", src_layer=l, after=after) for k, l in groups[n]]

    def start_group(n, lands, after):
        in_flight.append(exchange_start([], lands, None, False, f"gather_weights_start_{n}", after=after,
                                        peers=SAME_CORE_AND_SIBLING if n == 0 else ALL_PEERS))
        return in_flight[-1][-1]

    (c_all,) = all_gather([c], "gather_c")
    c_all = c_all.reshape(N_DEV, D)
    token = start_group(0, own_slots(0, c_all), c_all)
    later = [own_slots(n, token) for n in range(1, len(groups))]
    cctx8 = jnp.broadcast_to(c_ctx[None, :], (N_DEV, D))
    ada_b_cols = lax.dynamic_slice_in_dim(ada_b, me * ada_cols, ada_cols, axis=1)[:, None, :]
    mod_cols = ada_fwd(c_all, cctx8, ada_w, ada_b_cols)
    xs = jnp.concatenate([ctx[0], x[0]], axis=0)
    lb1 = lower_bounds(hgrn_lower_bounds)
    mod_all, conv_all = all_gather([mod_cols, ffn_conv_w.reshape(n_layers, 9, -1)], "gather_mod_conv",
                                   after=[token, xs, lb1] + [a for lands in later for a in lands])
    conv_full = [conv_all[:, l].transpose(1, 0, 2).reshape(9, N_FFK, FF_SLOT).transpose(1, 0, 2) for l in layers]
    for n in range(1, len(groups)):
        token = start_group(n, later[n - 1], mod_all if n == 1 else token)
    for started_group in in_flight:
        started = started + started_group[-1][0, 0]

    def as_used(k, a):
        return a if k in ("w_in", "ffn_w_up") else a.reshape(N_FFK, FF_SLOT, D) if k == "ffn_w_down" else a.reshape(D, D)

    arrived = {}

    def fetch(l, part, after):
        n = group_of[(l, part)]
        send, recv, _, lands, _ = in_flight[n]
        first = n == 0
        got = exchange_wait(send, recv, [], lands, None, False, after, f"gather_weights_wait_{n}",
                            peers=SAME_CORE_AND_SIBLING if first else ALL_PEERS)
        if first:
            send, recv, _, lands, _ = exchange_start([], got, None, False, "gather_weights_pass_on", peers=OTHER_CHIPS,
                                                     forward=True)
            got = exchange_wait(send, recv, [], lands, None, False, after, "gather_weights_passed_on", peers=OTHER_CHIPS)
        for (k, ll), a in zip(groups[n], got):
            arrived.setdefault(short[k], [None] * n_layers)[ll] = as_used(k, a)
        return arrived

    mod_x = lax.dynamic_index_in_dim(mod_all[:, :, 0], me, axis=2, keepdims=False)
    mod_c = mod_all[:, :, 1, 0]
    mod = [jnp.stack([mod_c[:, l].reshape(6, 1, D), mod_x[:, l].reshape(6, 1, D)]) for l in layers]
    mod[0] = mod[0] + started

    lb = [jnp.zeros((2, 1, D), F32), lb1.reshape(2, 1, D)]

    w = {
        "nw1": [norm1_w[l][None] for l in layers], "nw2": [norm2_w[l][None] for l in layers],
        "lnw": [sgu_ln_w[l][None] for l in layers], "lnb": [sgu_ln_b[l][None] for l in layers],
        "sw": [sgu_w[l] for l in layers], "sb": [sgu_b[l][:, :, None] for l in layers],
        "hnw": [hgrn_norm_w[l][None] for l in layers], "cw": conv_full,
        "cb": [ffn_conv_b[l].reshape(N_FFK, 1, FF_SLOT) for l in layers], "fw": final_norm_w[None],
    }
    long = {v: k for k, v in short.items()}
    landing, sent = {}, []

    def publish(l, part, grads):
        keys = [long[k] for k in grads]
        slots = [a.reshape((N_DEV, -1, a.shape[-1])) for a in grads.values()]
        zones = [place_own(s, landing.get(k, jax.ShapeDtypeStruct((N_DEV, n_layers) + s.shape[1:], s.dtype)), me1, l, True,
                           f"scatter_own_{short[k]}_{l}") for k, s in zip(keys, slots)]
        send, recv, srcs, zones, token = exchange_start(slots, zones, l, True, f"scatter_grads_start_{part}_{l}")
        landing.update(zip(keys, zones))
        sent.append((keys, l, part, send, recv, srcs, token))
        return token[0, 0]

    out = {}
    flat2 = lambda a: a.reshape(-1, a.shape[-1])

    def finish(part, after):
        done = []
        for keys, l, p, send, recv, srcs, _ in sent:
            if p == part:
                zones = exchange_wait(send, recv, srcs, [landing[k] for k in keys], l, True, after,
                                      f"scatter_grads_wait_{part}_{l}")
                landing.update(zip(keys, zones))
                done = keys
        for k in done:
            r = landing[k]
            res = adamw(flat2(wts[k]), flat2(mom1[k]), flat2(mom2[k]), r.reshape(N_DEV, -1, r.shape[-1]), "adamw_" + k)
            out[k] = tuple(a.reshape(wts[k].shape) for a in res)

    rep_rows = -(-sum(_rows_of(wts[k].size) for k in REPLICATED) // 64) * 64
    conv_rows = _rows_of(n_layers * 9 * D_FF)
    dmod_rows = _rows_of(n_layers * 6 * D)
    early = {}

    def small_early(loss_part, g, dmod, dlb):
        d_hlb = lower_bounds_bwd(hgrn_lower_bounds, dlb[1].reshape(1, 2 * D))
        st = lambda k: jnp.stack([jnp.zeros((1, D), F32) if a is None else a for a in g[k]])
        rep_grads = {"norm1_w": st("nw1"), "sgu_ln_w": st("lnw"), "sgu_ln_b": st("lnb"), "sgu_w": st("sw"), "sgu_b": st("sb"),
                     "hgrn_lower_bounds": d_hlb, "hgrn_norm_w": st("hnw"), "norm2_w": st("nw2"), "ffn_conv_b": st("cb"),
                     "final_norm_w": g["fw"]}
        d_conv = jnp.stack([g["cw"][l].transpose(1, 0, 2).reshape(9, D_FF) for l in layers])
        dmod_x = jnp.stack([dmod[l][1].reshape(6 * D) for l in layers])
        dmod_c = jnp.stack([dmod[l][0].reshape(6 * D) for l in layers])
        small = jnp.concatenate([_pack([rep_grads[k] for k in REPLICATED], rep_rows),
                                 _pack([d_conv, dmod_x, dmod_c, loss_part.reshape(1)])], axis=0)
        zone = place_own(small, jax.ShapeDtypeStruct((N_DEV,) + small.shape, F32), me1, None, False, "gather_small_own")
        early["send"], early["recv"], _, early["zones"], token = exchange_start([], [zone], None, False, "gather_small_start")
        return token[0, 0]

    def small_ready(loss_part, g, dmod, dlb):
        late = _pack([g["nw1"][0], dmod[0][1, 0:2], dmod[0][0, 0:2]])
        for part in ("ffn", "mix"):
            finish(part, late)
        (late_all,) = all_gather([late], "gather_small_late", after=[out[k][0] for k in big[1:]])
        (small_all,) = exchange_wait(early["send"], early["recv"], [], early["zones"], None, False, late_all,
                                     "gather_small_wait")
        at_x = rep_rows + conv_rows
        small_all = small_all.at[:, 0:1].set(late_all[:, 0:1])
        small_all = small_all.at[:, at_x:at_x + 2].set(late_all[:, 8:10])
        small_all = small_all.at[:, at_x + dmod_rows:at_x + dmod_rows + 2].set(late_all[:, 16:18])
        d_conv_shape, dmod_shape = (n_layers, 9, D_FF), (n_layers, 6 * D)
        conv_g, dmx_all, dmc_all, loss_all = _unpack(small_all[:, rep_rows:], [d_conv_shape, dmod_shape, dmod_shape, (1,)])
        out["loss"] = functools.reduce(lambda a, b: a + b, [loss_all[k, 0] for k in range(N_DEV)])

        rep = adamw(_pack([wts[k] for k in REPLICATED], rep_rows), _pack([mom1[k] for k in REPLICATED], rep_rows),
                    _pack([mom2[k] for k in REPLICATED], rep_rows), small_all, "adamw_replicated")
        rep = [_unpack(r, [wts[k].shape for k in REPLICATED]) for r in rep]
        for n, k in enumerate(REPLICATED):
            out[k] = tuple(r[n] for r in rep)

        conv_mine = lax.dynamic_index_in_dim(conv_g.reshape(N_DEV, n_layers, 9, N_DEV, -1), me, axis=3, keepdims=False)
        res = adamw(flat2(ffn_conv_w), flat2(m_ffn_conv_w), flat2(v_ffn_conv_w),
                    conv_mine.reshape(N_DEV, -1, conv_mine.shape[-1]), "adamw_conv_w")
        out["ffn_conv_w"] = tuple(r.reshape(ffn_conv_w.shape) for r in res)

        out["ada_b"] = tuple(adamw(ada_b, m_ada_b, v_ada_b, jnp.concatenate([dmx_all, dmc_all], axis=0), "adamw_ada_b"))

        cols_of = lambda a: lax.dynamic_slice_in_dim(a, me * ada_cols, ada_cols, axis=2).transpose(1, 0, 2)
        d_ada_w, d_cctx = ada_bwd(c_all, cctx8, ada_w, ada_b_cols, cols_of(dmx_all), cols_of(dmc_all))
        res = adamw(flat2(ada_w), flat2(m_ada_w), flat2(v_ada_w), flat2(d_ada_w)[None], "adamw_ada_w")
        out["ada_w"] = tuple(r.reshape(ada_w.shape) for r in res)
        (d_cctx_all,) = all_gather([d_cctx], "gather_c_ctx_grad")
        res = adamw(c_ctx[None], m_c_ctx[None], v_c_ctx[None], d_cctx_all, "adamw_c_ctx")
        out["c_ctx"] = tuple(r[0] for r in res)
        return d_cctx_all

    _, grad_x, _, _, _, small_done = local_step(x[0], ctx[0], loss_target[0], mod, lb, w, fetch, publish, small_ready,
                                                small_early)
    loss = out["loss"]

    finish("in", small_done)
    return (loss, grad_x[None]) + tuple(out[k][n] for n in range(4) for k in WEIGHT_ORDER)
```

```python
import functools
import math

import jax
import jax.numpy as jnp
from jax import lax
from jax.experimental import pallas as pl
from jax.experimental.pallas import tpu as pltpu

F32 = jnp.float32
BF16 = jnp.bfloat16

N_DEV = 8
AXES = ("x", "y", "c")
D = 1024
CTX = 256
TM = 256
CH = 64
SGU_CH = 128
HEADS = 8
HD = 128
GRID_W = 64
D_IN = 9 * D
IN_SLOT = D_IN // N_DEV
D_FF = 2816
FF_SLOT = 2 * D_FF // N_DEV
N_FFK = D_FF // FF_SLOT
RMS_EPS = 1e-6
LN_EPS = 1e-5
ADAM_LR, ADAM_B1, ADAM_B2, ADAM_EPS, ADAM_WD, ADAM_STEP = 0.001, 0.9, 0.999, 1e-08, 0.01, 10
VMEM_LIMIT_V7X = 56 * 2 ** 20
ELEMENTWISE_VMEM = 24 * 2 ** 20
COPY_BLOCK_BYTES = 2 ** 21
GRAD_WIRE = jnp.bfloat16

VMEM_WHOLE = pl.BlockSpec(memory_space=pltpu.VMEM)
ANY = pl.BlockSpec(memory_space=pl.ANY)


def _cp(n_axes):
    return pltpu.CompilerParams(dimension_semantics=("arbitrary",) * n_axes, vmem_limit_bytes=VMEM_LIMIT_V7X)


def _dot(a, b, dims):
    return lax.dot_general(a.astype(BF16), b.astype(BF16), (dims, ((), ())), preferred_element_type=F32)


@jax.custom_vjp
def mm(a, b):
    return _dot(a, b, ((1,), (0,)))


mm.defvjp(lambda a, b: (mm(a, b), (a, b)),
          lambda r, g: (_dot(g, r[1], ((1,), (1,))).astype(r[0].dtype), _dot(r[0], g, ((0,), (0,))).astype(r[1].dtype)))


@jax.custom_vjp
def mm_nt(a, b):
    return _dot(a, b, ((1,), (1,)))


mm_nt.defvjp(lambda a, b: (mm_nt(a, b), (a, b)),
             lambda r, g: (_dot(g, r[1], ((1,), (0,))).astype(r[0].dtype), _dot(g, r[0], ((0,), (0,))).astype(r[1].dtype)))


@jax.custom_vjp
def mm_tn(a, b):
    return _dot(a, b, ((0,), (0,)))


mm_tn.defvjp(lambda a, b: (mm_tn(a, b), (a, b)),
             lambda r, g: (_dot(r[1], g, ((1,), (1,))).astype(r[0].dtype), _dot(r[0], g, ((1,), (0,))).astype(r[1].dtype)))


def _tri_dot(m, g):
    hi = g.astype(BF16)
    low = (g - hi.astype(F32)).astype(BF16)
    n = g.shape[1]
    out = jnp.dot(m.astype(BF16), jnp.concatenate([hi, low], axis=1), preferred_element_type=F32)
    return out[:, :n] + out[:, n:]


@jax.custom_vjp
def _cum(m, mt, g):
    return _tri_dot(m, g)


_cum.defvjp(lambda m, mt, g: (_cum(m, mt, g), (m, mt)),
            lambda r, d: (jnp.zeros_like(r[0]), jnp.zeros_like(r[1]), _tri_dot(r[1], d)))


def _silu(x):
    return x * jax.nn.sigmoid(x)


def _gelu(x):
    return 0.5 * x * (1.0 + jnp.tanh(math.sqrt(2.0 / math.pi) * (x + 0.044715 * (x * x * x))))


def _rms(x, w):
    return x * lax.rsqrt(jnp.mean(x * x, axis=-1, keepdims=True) + RMS_EPS) * w


def _norm_mod(x, w, shift, scale):
    return _rms(x, w) * (1.0 + scale) + shift


def _hsl(h):
    return slice(h * HD, (h + 1) * HD)


def _hgrn_chunk(st, qz, fz, iv, lb, m, mt, mref):
    hs = range(HEADS)
    keep = [1.0 - lb[h] for h in hs]
    sg = [jax.nn.sigmoid(fz[h]) for h in hs]
    g = [jnp.log(lb[h] + keep[h] * sg[h]) for h in hs]
    k = [keep[h] * (1.0 - sg[h]) for h in hs]
    q = [_silu(qz[h]) for h in hs]
    b = [_cum(m, mt, g[h]) for h in hs]
    ref = [jnp.sum(mref * g[h], axis=0, keepdims=True) for h in hs]
    last = [jnp.sum(g[h], axis=0, keepdims=True) for h in hs]
    qa = [q[h] * jnp.exp(b[h] - ref[h]) for h in hs]
    ka = [k[h] * jnp.exp(ref[h] - b[h]) for h in hs]
    scores = [jnp.where(m > 0.5, mm_nt(qa[h], ka[h]), 0.0) for h in hs]
    inter = [mm_nt(qa[h] * jnp.exp(ref[h]), st[h]) for h in hs]
    kv = [mm_tn(iv[h], ka[h] * jnp.exp(last[h] - ref[h])) for h in hs]
    outs = [mm(scores[h], iv[h]) + inter[h] for h in hs]
    news = [jnp.exp(last[h]) * st[h] + kv[h] for h in hs]
    return outs, news


def _sgu_fn(ub, vb, lnw, lnb, sw, sb):
    gv = [_gelu(v) for v in vb]
    mu = sum(jnp.sum(t, axis=-1, keepdims=True) for t in gv) / D
    var = sum(jnp.sum((t - mu) * (t - mu), axis=-1, keepdims=True) for t in gv) / D
    inv = lax.rsqrt(var + LN_EPS)
    cols = []
    for g in range(HEADS):
        vn = (gv[g] - mu) * inv * lnw[g] + lnb[g]
        cols.append(_gelu(ub[g]) * (mm(sw[g], vn) + sb[g]))
    return jnp.concatenate(cols, axis=1)


def _readout_fn(ob, og, hnw):
    r = [o * lax.rsqrt(jnp.mean(o * o, axis=-1, keepdims=True) + RMS_EPS) * hnw for o in ob]
    return jnp.concatenate(r, axis=1) * _silu(og)


def _glu_fn(ac, v):
    return _gelu(ac) * v


def _stream_row(tm):
    n_ctx = CTX // tm
    return lambda i: (jnp.where(i < n_ctx, 0, 1), 0, 0, 0)


def in_proj_fwd(x, mod, nw, wg):
    t = x.shape[0]

    def body(x_ref, mod_ref, nw_ref, w_ref, out_ref, ht_ref, iv_ref, mix_ref, all_ref):
        h32 = _norm_mod(x_ref[...], nw_ref[...], mod_ref[0, 0], mod_ref[0, 1])
        ht_ref[...] = h32.T.astype(BF16)
        h = h32.astype(BF16)
        for j in range(N_DEV):
            all_ref[:, j * IN_SLOT:(j + 1) * IN_SLOT] = jnp.dot(h, w_ref[j], preferred_element_type=F32)
        out_ref[...] = all_ref[:, :4 * D]
        iv_ref[...] = all_ref[:, 3 * D:4 * D].astype(BF16)
        mix_ref[...] = all_ref[:, 4 * D:].astype(BF16)

    return pl.pallas_call(
        body, name="in_proj_fwd", grid=(t // TM,),
        in_specs=[pl.BlockSpec((TM, D), lambda i: (i, 0)), pl.BlockSpec((1, 6, 1, D), _stream_row(TM)),
                  pl.BlockSpec((1, D), lambda i: (0, 0)), VMEM_WHOLE],
        out_specs=[pl.BlockSpec((TM, 4 * D), lambda i: (i, 0)), pl.BlockSpec((D, TM), lambda i: (0, i)),
                   pl.BlockSpec((TM, D), lambda i: (i, 0)), pl.BlockSpec((TM, 5 * D), lambda i: (i, 0))],
        out_shape=[jax.ShapeDtypeStruct((t, 4 * D), F32), jax.ShapeDtypeStruct((D, t), BF16), jax.ShapeDtypeStruct((t, D), BF16),
                   jax.ShapeDtypeStruct((t, 5 * D), BF16)],
        scratch_shapes=[pltpu.VMEM((TM, D_IN), F32)], compiler_params=_cp(1))(x, mod, nw, wg)


SCAN_STEP = 4
SCAN_ROWS = SCAN_STEP * CH


def _scan_block(nb):
    ncb = CTX // SCAN_ROWS

    def block(d, s):
        bwd = jnp.where(s < ncb, ncb - 1 - s, nb + ncb - 1 - s)
        return jnp.where(d == 0, s, bwd)
    return block


def hgrn_fwd(parts, iv, lb, mc, mtc, mrefc):
    t = parts.shape[0]
    nb = t // SCAN_ROWS
    block = _scan_block(nb)

    def body(q_ref, f_ref, i_ref, lb_ref, m_ref, mt_ref, mr_ref, o_ref, ck_ref, st):
        d = pl.program_id(0)

        @pl.when(pl.program_id(1) == 0)
        def _():
            st[...] = jnp.zeros_like(st)

        for c in range(SCAN_STEP):
            rows = pl.ds(pl.multiple_of(jnp.where(d == 0, c * CH, (SCAN_STEP - 1 - c) * CH), CH), CH)
            ck_ref[0, c] = st[...].astype(BF16)
            outs, news = _hgrn_chunk([st[h] for h in range(HEADS)], [q_ref[rows, _hsl(h)] for h in range(HEADS)],
                                     [f_ref[rows, _hsl(h)] for h in range(HEADS)], [i_ref[rows, _hsl(h)] for h in range(HEADS)],
                                     [lb_ref[0, :, _hsl(h)] for h in range(HEADS)], m_ref[0], mt_ref[0], mr_ref[0])
            for h in range(HEADS):
                o_ref[0, rows, _hsl(h)] = outs[h].astype(BF16)
                st[h] = news[h]

    const = lambda d, s: (d, 0, 0)
    at = lambda k: pl.BlockSpec((SCAN_ROWS, D), lambda d, s: (block(d, s), k(d)))
    return pl.pallas_call(
        body, name="hgrn_fwd", grid=(2, nb),
        in_specs=[at(lambda d: 0), at(lambda d: 1 + d), at(lambda d: 0), pl.BlockSpec((1, 1, D), const),
                  pl.BlockSpec((1, CH, CH), const), pl.BlockSpec((1, CH, CH), const), pl.BlockSpec((1, CH, 1), const)],
        out_specs=[pl.BlockSpec((1, SCAN_ROWS, D), lambda d, s: (d, block(d, s), 0)),
                   pl.BlockSpec((1, SCAN_STEP, HEADS, HD, HD), lambda d, s: (d, s, 0, 0, 0))],
        out_shape=[jax.ShapeDtypeStruct((2, t, D), BF16), jax.ShapeDtypeStruct((2, nb * SCAN_STEP, HEADS, HD, HD), BF16)],
        scratch_shapes=[pltpu.VMEM((HEADS, HD, HD), F32)], compiler_params=_cp(2))(parts, parts, iv, lb, mc, mtc, mrefc)


def _mixer_tile(rows, u_ref, v_ref, og_ref, o_ref, lnw_ref, lnb_ref, sw_ref, sb_ref, hnw_ref):
    n = (rows.stop - rows.start) // SGU_CH
    yas, vjps = [], []
    for c in range(n):
        r = slice(rows.start + c * SGU_CH, rows.start + (c + 1) * SGU_CH)
        ya, vjp_a = jax.vjp(_sgu_fn, [u_ref[r, _hsl(g)].astype(F32) for g in range(HEADS)],
                            [v_ref[r, _hsl(g)].astype(F32) for g in range(HEADS)],
                            [lnw_ref[:, _hsl(g)] for g in range(HEADS)], [lnb_ref[:, _hsl(g)] for g in range(HEADS)],
                            [sw_ref[g] for g in range(HEADS)], [sb_ref[g] for g in range(HEADS)])
        yas.append(ya)
        vjps.append(vjp_a)
    yb, vjp_b = jax.vjp(_readout_fn, [o_ref[0, rows, _hsl(h)].astype(F32) + o_ref[1, rows, _hsl(h)].astype(F32)
                                      for h in range(HEADS)],
                        og_ref[rows, :].astype(F32), hnw_ref[...])
    return (yas[0] if n == 1 else jnp.concatenate(yas, axis=0)), yb, vjps, vjp_b


def _part_specs(tm, first, n):
    return [pl.BlockSpec((tm, D), functools.partial(lambda k, i: (i, k), first + k)) for k in range(n)]


def _unless_ctx(skip_ctx, is_ctx, zero_refs, work):
    if not skip_ctx:
        return work()

    @pl.when(is_ctx)
    def _():
        for r in zero_refs:
            r[...] = jnp.zeros_like(r)

    pl.when(jnp.logical_not(is_ctx))(work)


def mixer_fwd(x, parts, o, mod, lnw, lnb, sw, sb, hnw, wa, wb, wo, skip_ctx):
    t = x.shape[0]

    def body(x_ref, u_ref, v_ref, og_ref, ga_ref, gb_ref, o_ref, mod_ref, lnw_ref, lnb_ref, sw_ref, sb_ref, hnw_ref,
             wa_ref, wb_ref, wo_ref, out_ref, pa_ref, pb_ref, y_ref, yat_ref, ybt_ref, mt_ref):
        def work():
            ya, yb, _, _ = _mixer_tile(slice(0, TM), u_ref, v_ref, og_ref, o_ref, lnw_ref, lnb_ref, sw_ref, sb_ref, hnw_ref)
            pa, pb = mm(ya, wa_ref[...]), mm(yb, wb_ref[...])
            merged = jax.nn.sigmoid(ga_ref[...].astype(F32)) * pa + jax.nn.sigmoid(gb_ref[...].astype(F32)) * pb
            y = mm(merged, wo_ref[...])
            out_ref[...] = x_ref[...] + mod_ref[0, 2] * y
            pa_ref[...], pb_ref[...], y_ref[...] = pa.astype(BF16), pb.astype(BF16), y.astype(BF16)
            yat_ref[...], ybt_ref[...], mt_ref[...] = ya.T.astype(BF16), yb.T.astype(BF16), merged.T.astype(BF16)

        _unless_ctx(skip_ctx, pl.program_id(0) == 0, (out_ref, pa_ref, pb_ref, y_ref, yat_ref, ybt_ref, mt_ref), work)

    vec = lambda n: pl.BlockSpec((1, n), lambda i: (0, 0))
    tile = pl.BlockSpec((TM, D), lambda i: (i, 0))
    tile_t = pl.BlockSpec((D, TM), lambda i: (0, i))
    return pl.pallas_call(
        body, name="mixer_fwd", grid=(t // TM,),
        in_specs=[tile] + _part_specs(TM, 0, 5)
        + [pl.BlockSpec((2, TM, D), lambda i: (0, i, 0)), pl.BlockSpec((1, 6, 1, D), _stream_row(TM)), vec(D), vec(D),
           VMEM_WHOLE, VMEM_WHOLE, vec(HD), VMEM_WHOLE, VMEM_WHOLE, VMEM_WHOLE],
        out_specs=[tile] * 4 + [tile_t] * 3,
        out_shape=[jax.ShapeDtypeStruct((t, D), F32)] + [jax.ShapeDtypeStruct((t, D), BF16)] * 3
        + [jax.ShapeDtypeStruct((D, t), BF16)] * 3, compiler_params=_cp(1),
    )(x, parts, parts, parts, parts, parts, o, mod, lnw, lnb, sw, sb, hnw, wa, wb, wo)


def ffn_up_fwd(x, mod, nw, wg, skip_ctx):
    t = x.shape[0]

    def body(x_ref, mod_ref, nw_ref, w_ref, out_ref, ht_ref):
        def work():
            h32 = _norm_mod(x_ref[...], nw_ref[...], mod_ref[0, 3], mod_ref[0, 4])
            ht_ref[...] = h32.T.astype(BF16)
            h = h32.astype(BF16)
            for j in range(N_DEV):
                out_ref[j] = jnp.dot(h, w_ref[j], preferred_element_type=F32)

        _unless_ctx(skip_ctx, pl.program_id(0) == 0, (out_ref, ht_ref), work)

    return pl.pallas_call(
        body, name="ffn_up_fwd", grid=(t // TM,),
        in_specs=[pl.BlockSpec((TM, D), lambda i: (i, 0)), pl.BlockSpec((1, 6, 1, D), _stream_row(TM)),
                  pl.BlockSpec((1, D), lambda i: (0, 0)), VMEM_WHOLE],
        out_specs=[pl.BlockSpec((N_DEV, TM, FF_SLOT), lambda i: (0, i, 0)), pl.BlockSpec((D, TM), lambda i: (0, i))],
        out_shape=[jax.ShapeDtypeStruct((N_DEV, t, FF_SLOT), F32), jax.ShapeDtypeStruct((D, t), BF16)],
        compiler_params=_cp(1))(x, mod, nw, wg)


def _halo_specs(nt):
    per = TM // GRID_W
    last = nt * per - 1
    return [pl.BlockSpec((N_FFK, GRID_W, FF_SLOT), lambda i: (0, jnp.maximum(i * per - 1, 0), 0)),
            pl.BlockSpec((N_FFK, TM, FF_SLOT), lambda i: (0, i, 0)),
            pl.BlockSpec((N_FFK, GRID_W, FF_SLOT), lambda i: (0, jnp.minimum(i * per + per, last), 0))]


def _with_halo(prev_ref, main_ref, next_ref, k, i, nt):
    prev = jnp.where(i >= 2, prev_ref[k], 0.0)
    nxt = jnp.where((i >= 1) & (i <= nt - 2), next_ref[k], 0.0)
    return jnp.concatenate([prev, main_ref[k], nxt], axis=0)


def _tap_valid(dc, i, n_rows, offset):
    r = lax.broadcasted_iota(jnp.int32, (n_rows, 1), 0) - offset
    col = jnp.bitwise_and(r, GRID_W - 1)
    pos = jnp.where(i == 0, r, col) + dc
    return (pos >= 0) & (pos < jnp.where(i == 0, TM, GRID_W))


def _row_weight(cw_ref, k, dr, dc, i):
    w = cw_ref[k, 3 * (dr + 1) + dc + 1:3 * (dr + 1) + dc + 2, :]
    return w if dr == 0 else jnp.where(i == 0, 0.0, w)


def ffn_down_fwd(x, av, mod, cw, cb, wd, skip_ctx):
    t = x.shape[0]
    nt = t // TM

    def body(x_ref, ap_ref, am_ref, an_ref, v_ref, mod_ref, cw_ref, cb_ref, wd_ref, out_ref, ac_ref, y_ref, z_ref):
        i = pl.program_id(0)

        def work():
            y = None
            for k in range(N_FFK):
                a_ext = _with_halo(ap_ref, am_ref, an_ref, k, i, nt)
                conv = jnp.zeros((TM, FF_SLOT), F32) + cb_ref[k]
                for dc in (-1, 0, 1):
                    col = functools.reduce(lambda p, q: p + q, [a_ext[GRID_W + GRID_W * dr:GRID_W + GRID_W * dr + TM]
                                                                * _row_weight(cw_ref, k, dr, dc, i) for dr in (-1, 0, 1)])
                    conv = conv + (col if dc == 0 else
                                   jnp.where(_tap_valid(dc, i, TM, 0), pltpu.roll(col, (-dc) % TM, 0), 0.0))
                ac_ref[k] = conv.astype(BF16)
                z = _glu_fn(conv, v_ref[k]).astype(BF16)
                z_ref[k] = z
                part = mm(z, wd_ref[k])
                y = part if y is None else y + part
            y_ref[...] = y
            out_ref[...] = x_ref[...] + mod_ref[0, 5] * y

        _unless_ctx(skip_ctx, i == 0, (out_ref, ac_ref, y_ref, z_ref), work)

    tile = pl.BlockSpec((TM, D), lambda i: (i, 0))
    half = lambda first: pl.BlockSpec((N_FFK, TM, FF_SLOT), lambda i: (first, i, 0))
    return pl.pallas_call(
        body, name="ffn_down_fwd", grid=(nt,),
        in_specs=[tile] + _halo_specs(nt) + [half(1), pl.BlockSpec((1, 6, 1, D), _stream_row(TM)), VMEM_WHOLE, VMEM_WHOLE,
                                             VMEM_WHOLE],
        out_specs=[tile, half(0), tile, half(0)],
        out_shape=[jax.ShapeDtypeStruct((t, D), F32), jax.ShapeDtypeStruct((N_FFK, t, FF_SLOT), BF16),
                   jax.ShapeDtypeStruct((t, D), F32), jax.ShapeDtypeStruct((N_FFK, t, FF_SLOT), BF16)],
        compiler_params=_cp(1))(x, av, av, av, av, mod, cw, cb, wd)


def loss_fwd_bwd(x, target, fw):
    t = x.shape[0]

    def body(x_ref, t_ref, w_ref, loss_ref, dx_ref, dw_ref):
        i = pl.program_id(0)

        @pl.when(i == 0)
        def _():
            loss_ref[...] = jnp.zeros_like(loss_ref)
            dw_ref[...] = jnp.zeros_like(dw_ref)
            dx_ref[...] = jnp.zeros_like(dx_ref)

        @pl.when(i > 0)
        def _():
            y, vjp = jax.vjp(_rms, x_ref[...], w_ref[...])
            err = y - t_ref[...]
            loss_ref[...] += 0.5 * jnp.sum(jnp.sum(err * err, axis=-1, keepdims=True) / D)
            dx, dw = vjp(err / D)
            dx_ref[...] = dx
            dw_ref[...] += dw

    return pl.pallas_call(
        body, name="loss_fwd_bwd", grid=(t // TM,),
        in_specs=[pl.BlockSpec((TM, D), lambda i: (i, 0)), pl.BlockSpec((TM, D), lambda i: (jnp.maximum(i - 1, 0), 0)),
                  pl.BlockSpec((1, D), lambda i: (0, 0))],
        out_specs=[pl.BlockSpec((8, 128), lambda i: (0, 0)), pl.BlockSpec((TM, D), lambda i: (i, 0)),
                   pl.BlockSpec((1, D), lambda i: (0, 0))],
        out_shape=[jax.ShapeDtypeStruct((8, 128), F32), jax.ShapeDtypeStruct((t, D), F32), jax.ShapeDtypeStruct((1, D), F32)],
        compiler_params=_cp(1))(x, target, fw)


def _stream_add(ref, k, is_ctx, val):
    ref[0, k] += jnp.where(is_ctx, val, 0.0)
    ref[1, k] += jnp.where(is_ctx, 0.0, val)


def ffn_down_bwd(dx, ac, av, y, mod, wd, skip_ctx):
    t = dx.shape[0]
    nt = t // TM

    def body(dx_ref, ac_ref, v_ref, y_ref, mod_ref, wd_ref, dav_ref, dac_ref, dout_ref, dg_ref):
        i = pl.program_id(0)

        @pl.when(i == 0)
        def _():
            dg_ref[...] = jnp.zeros_like(dg_ref)

        def work():
            _stream_add(dg_ref, 0, i == 0, jnp.sum(dx_ref[...] * y_ref[...], axis=0, keepdims=True))
            dout = (mod_ref[0, 5] * dx_ref[...]).astype(BF16)
            dout_ref[...] = dout
            for k in range(N_FFK):
                _, vjp = jax.vjp(_glu_fn, ac_ref[k].astype(F32), v_ref[k])
                dac, dv = vjp(mm_nt(dout, wd_ref[k]))
                dac_ref[k] = dac
                dav_ref[k] = dv.astype(BF16)

        _unless_ctx(skip_ctx, i == 0, (dav_ref, dac_ref, dout_ref), work)

    tile = pl.BlockSpec((TM, D), lambda i: (i, 0))
    half = lambda first: pl.BlockSpec((N_FFK, TM, FF_SLOT), lambda i: (first, i, 0))
    return pl.pallas_call(
        body, name="ffn_down_bwd", grid=(nt,),
        in_specs=[tile, half(0), half(1), tile, pl.BlockSpec((1, 6, 1, D), _stream_row(TM)), VMEM_WHOLE],
        out_specs=[half(1), half(0), tile, pl.BlockSpec((2, 1, 1, D), lambda i: (0, 0, 0, 0))],
        out_shape=[jax.ShapeDtypeStruct((N_DEV, t, FF_SLOT), BF16), jax.ShapeDtypeStruct((N_FFK, t, FF_SLOT), F32),
                   jax.ShapeDtypeStruct((t, D), BF16), jax.ShapeDtypeStruct((2, 1, 1, D), F32)],
        compiler_params=_cp(1))(dx, ac, av, y, mod, wd)


def conv_bwd(dav, dac, av, cw, skip_ctx):
    t = dac.shape[1]
    nt = t // TM

    def body(dav_in, gp_ref, gm_ref, gn_ref, ap_ref, am_ref, an_ref, cw_ref, dav_ref, dcw_ref, dcb_ref):
        i = pl.program_id(0)

        @pl.when(i == 0)
        def _():
            dcw_ref[...] = jnp.zeros_like(dcw_ref)
            dcb_ref[...] = jnp.zeros_like(dcb_ref)

        def work():
            for k in range(N_FFK):
                g_ext = _with_halo(gp_ref, gm_ref, gn_ref, k, i, nt)
                a_ext = _with_halo(ap_ref, am_ref, an_ref, k, i, nt)
                g_main = gm_ref[k]
                dcb_ref[k] += jnp.sum(g_main, axis=0, keepdims=True)
                da = jnp.zeros((TM, FF_SLOT), F32)
                for dc in (-1, 0, 1):
                    valid = _tap_valid(dc, i, TM, 0)
                    q = functools.reduce(lambda p, r: p + r, [g_ext[GRID_W - GRID_W * dr:GRID_W - GRID_W * dr + TM]
                                                              * _row_weight(cw_ref, k, dr, dc, i) for dr in (-1, 0, 1)])
                    da = da + (q if dc == 0 else pltpu.roll(jnp.where(valid, q, 0.0), dc % TM, 0))
                    g_shift = g_main if dc == 0 else pltpu.roll(jnp.where(valid, g_main, 0.0), dc % TM, 0)
                    for dr in (-1, 0, 1):
                        lo = GRID_W + GRID_W * dr
                        tap = 3 * (dr + 1) + dc + 1
                        dw = jnp.sum(g_shift * a_ext[lo:lo + TM], axis=0, keepdims=True)
                        dcw_ref[k, tap:tap + 1, :] += dw if dr == 0 else jnp.where(i == 0, 0.0, dw)
                dav_ref[k] = da.astype(BF16)

        _unless_ctx(skip_ctx, i == 0, (dav_ref,), work)

    whole = lambda rows: pl.BlockSpec((N_FFK, rows, FF_SLOT), lambda i: (0, 0, 0))
    return pl.pallas_call(
        body, name="conv_bwd", grid=(nt,),
        in_specs=[ANY] + _halo_specs(nt) + _halo_specs(nt) + [VMEM_WHOLE],
        out_specs=[pl.BlockSpec((N_FFK, TM, FF_SLOT), lambda i: (0, i, 0)), whole(9), whole(1)],
        out_shape=[jax.ShapeDtypeStruct(dav.shape, BF16), jax.ShapeDtypeStruct((N_FFK, 9, FF_SLOT), F32),
                   jax.ShapeDtypeStruct((N_FFK, 1, FF_SLOT), F32)],
        input_output_aliases={0: 0}, compiler_params=_cp(1))(dav, dac, dac, dac, av, av, av, cw)


def _norm_mod_bwd(x_ref, nw_ref, mod_ref, k_shift, dh, dx_in, dx_ref, dnw_ref, dmod_ref, is_ctx):
    _, vjp = jax.vjp(_norm_mod, x_ref[...], nw_ref[...], mod_ref[0, k_shift], mod_ref[0, k_shift + 1])
    dx, dnw, dshift, dscale = vjp(dh)
    dx_ref[...] = dx_in + dx
    dnw_ref[...] += dnw
    _stream_add(dmod_ref, 0, is_ctx, dshift)
    _stream_add(dmod_ref, 1, is_ctx, dscale)


def ffn_up_bwd_x(dx2, x, dav, mod, nw, wg, skip_ctx):
    t = x.shape[0]

    def body(dx2_ref, x_ref, dav_ref, mod_ref, nw_ref, w_ref, dx_ref, dnw_ref, dmod_ref):
        i = pl.program_id(0)

        @pl.when(i == 0)
        def _():
            dnw_ref[...] = jnp.zeros_like(dnw_ref)
            dmod_ref[...] = jnp.zeros_like(dmod_ref)

        def work():
            dh = mm_nt(dav_ref[0], w_ref[0])
            for j in range(1, N_DEV):
                dh = dh + mm_nt(dav_ref[j], w_ref[j])
            _norm_mod_bwd(x_ref, nw_ref, mod_ref, 3, dh, dx2_ref[...], dx_ref, dnw_ref, dmod_ref, i == 0)

        _unless_ctx(skip_ctx, i == 0, (dx_ref,), work)

    tile = pl.BlockSpec((TM, D), lambda i: (i, 0))
    return pl.pallas_call(
        body, name="ffn_up_bwd_x", grid=(t // TM,),
        in_specs=[tile, tile, pl.BlockSpec((N_DEV, TM, FF_SLOT), lambda i: (0, i, 0)), pl.BlockSpec((1, 6, 1, D), _stream_row(TM)),
                  pl.BlockSpec((1, D), lambda i: (0, 0)), VMEM_WHOLE],
        out_specs=[tile, pl.BlockSpec((1, D), lambda i: (0, 0)), pl.BlockSpec((2, 2, 1, D), lambda i: (0, 0, 0, 0))],
        out_shape=[jax.ShapeDtypeStruct((t, D), F32), jax.ShapeDtypeStruct((1, D), F32), jax.ShapeDtypeStruct((2, 2, 1, D), F32)],
        compiler_params=_cp(1))(dx2, x, dav, mod, nw, wg)


def weight_grad(at, dout, slot, name, after=None):
    rows, t = at.shape
    stacked = dout.ndim == 3
    n = dout.shape[0] if stacked else dout.shape[1] // slot

    def body(a_ref, d_ref, *rest):
        dw_ref = rest[-1]
        dw_ref[0] = jnp.dot(a_ref[...], d_ref[0] if stacked else d_ref[...], preferred_element_type=F32).astype(dw_ref.dtype)

    d_spec = pl.BlockSpec((1, t, slot), lambda j: (j, 0, 0)) if stacked else pl.BlockSpec((t, slot), lambda j: (0, j))
    extra = [] if after is None else [jnp.reshape(after, (1, 1))]
    return pl.pallas_call(
        body, name=name, grid=(n,), in_specs=[VMEM_WHOLE, d_spec] + [ANY] * len(extra),
        out_specs=pl.BlockSpec((1, rows, slot), lambda j: (j, 0, 0)),
        out_shape=jax.ShapeDtypeStruct((n, rows, slot), GRAD_WIRE), compiler_params=_cp(1))(at, dout, *extra)


def weight_grad_rows(at, dout, name):
    n, t, rows = at.shape
    cols = dout.shape[1]

    def body(a_ref, d_ref, dw_ref):
        dw_ref[0] = _dot(a_ref[0], d_ref[...], ((0,), (0,))).astype(dw_ref.dtype)

    return pl.pallas_call(
        body, name=name, grid=(n,), in_specs=[pl.BlockSpec((1, t, rows), lambda k: (k, 0, 0)), VMEM_WHOLE],
        out_specs=pl.BlockSpec((1, rows, cols), lambda k: (k, 0, 0)),
        out_shape=jax.ShapeDtypeStruct((n, rows, cols), GRAD_WIRE), compiler_params=_cp(1))(at, dout)


def mixer_bwd(dx, parts, o, pa, pb, y, mod, lnw, lnb, sw, sb, hnw, wa, wb, wo, skip_ctx):
    t = dx.shape[0]
    tm = TM
    n_ctx = CTX // tm

    def body(dx_ref, u_ref, v_ref, og_ref, ga_ref, gb_ref, o_ref, pa_ref, pb_ref, y_ref, mod_ref, lnw_ref, lnb_ref, sw_ref,
             sb_ref, hnw_ref, wa_ref, wb_ref, wo_ref, dp_ref, do_ref, dy_ref, dpa_ref, dpb_ref, dlnw_ref, dlnb_ref, dsw_ref,
             dsb_ref, dhnw_ref, dg_ref):
        i = pl.program_id(0)

        @pl.when(i == 0)
        def _():
            for r in (dlnw_ref, dlnb_ref, dsw_ref, dsb_ref, dhnw_ref, dg_ref):
                r[...] = jnp.zeros_like(r)

        def work():
            _, _, vjps, vjp_b = _mixer_tile(slice(0, tm), u_ref, v_ref, og_ref, o_ref, lnw_ref, lnb_ref, sw_ref, sb_ref, hnw_ref)
            pa, pb = pa_ref[...].astype(F32), pb_ref[...].astype(F32)
            sa, sbg = jax.nn.sigmoid(ga_ref[...].astype(F32)), jax.nn.sigmoid(gb_ref[...].astype(F32))
            dxv = dx_ref[...]
            _stream_add(dg_ref, 0, i < n_ctx, jnp.sum(dxv * y_ref[...].astype(F32), axis=0, keepdims=True))
            dy = (mod_ref[0, 2] * dxv).astype(BF16)
            dy_ref[...] = dy
            dmerged = mm_nt(dy, wo_ref[...])
            dpa, dpb = (sa * dmerged).astype(BF16), (sbg * dmerged).astype(BF16)
            dpa_ref[...], dpb_ref[...] = dpa, dpb
            first = 4 * D
            dp_ref[:, first + 3 * D:first + 4 * D] = (dmerged * pa * sa * (1.0 - sa)).astype(BF16)
            dp_ref[:, first + 4 * D:first + 5 * D] = (dmerged * pb * sbg * (1.0 - sbg)).astype(BF16)
            dya = mm_nt(dpa, wa_ref[...])
            dob, dog, dhnw = vjp_b(mm_nt(dpb, wb_ref[...]))
            dp_ref[:, first + 2 * D:first + 3 * D] = dog.astype(BF16)
            dhnw_ref[...] += dhnw
            for g in range(HEADS):
                do_ref[:, _hsl(g)] = dob[g]
            for c, vjp_a in enumerate(vjps):
                rows = slice(c * SGU_CH, (c + 1) * SGU_CH)
                dub, dvb, dlnw, dlnb, dsw, dsb = vjp_a(dya[rows])
                for g in range(HEADS):
                    dp_ref[rows, first + g * HD:first + (g + 1) * HD] = dub[g].astype(BF16)
                    dp_ref[rows, first + D + g * HD:first + D + (g + 1) * HD] = dvb[g].astype(BF16)
                    dlnw_ref[:, _hsl(g)] += dlnw[g]
                    dlnb_ref[:, _hsl(g)] += dlnb[g]
                    dsw_ref[g] += dsw[g]
                    dsb_ref[g] += dsb[g]

        _unless_ctx(skip_ctx, i < n_ctx, (dp_ref, do_ref, dy_ref, dpa_ref, dpb_ref), work)

    vec = lambda n: pl.BlockSpec((1, n), lambda i: (0, 0))
    tile = pl.BlockSpec((tm, D), lambda i: (i, 0))
    sds = jax.ShapeDtypeStruct
    return pl.pallas_call(
        body, name="mixer_bwd", grid=(t // tm,),
        in_specs=[tile] + _part_specs(tm, 0, 5)
        + [pl.BlockSpec((2, tm, D), lambda i: (0, i, 0)), tile, tile, tile, pl.BlockSpec((1, 6, 1, D), _stream_row(tm)),
           vec(D), vec(D), VMEM_WHOLE, VMEM_WHOLE, vec(HD), VMEM_WHOLE, VMEM_WHOLE, VMEM_WHOLE],
        out_specs=[pl.BlockSpec((tm, D_IN), lambda i: (i, 0)), tile, tile, tile, tile, vec(D), vec(D),
                   VMEM_WHOLE, VMEM_WHOLE, vec(HD), pl.BlockSpec((2, 1, 1, D), lambda i: (0, 0, 0, 0))],
        out_shape=[sds((t, D_IN), BF16), sds((t, D), F32), sds((t, D), BF16), sds((t, D), BF16), sds((t, D), BF16),
                   sds((1, D), F32), sds((1, D), F32), sds((HEADS, SGU_CH, SGU_CH), F32), sds((HEADS, SGU_CH, 1), F32),
                   sds((1, HD), F32), sds((2, 1, 1, D), F32)],
        compiler_params=_cp(1))(dx, parts, parts, parts, parts, parts, o, pa, pb, y, mod, lnw, lnb, sw, sb, hnw, wa, wb, wo)


def hgrn_bwd(d, parts, lb, mc, mtc, mrefc, ck, do, first=None, dparts=None):
    t = parts.shape[0]
    nb = t // SCAN_ROWS
    block = _scan_block(nb)
    rev = lambda s: block(d, nb - 1 - s)

    def body(q_ref, f_ref, i_ref, lb_ref, m_ref, mt_ref, mr_ref, ck_ref, do_ref, *rest):
        dst = rest[-1]
        dlb_ref = rest[-2]

        @pl.when(pl.program_id(0) == 0)
        def _():
            dst[...] = jnp.zeros_like(dst)
            dlb_ref[...] = jnp.zeros_like(dlb_ref)

        heads = range(HEADS)
        fn = functools.partial(_hgrn_chunk, m=m_ref[0], mt=mt_ref[0], mref=mr_ref[0])
        for c in reversed(range(SCAN_STEP)):
            first_row = c * CH if d == 0 else (SCAN_STEP - 1 - c) * CH
            rows = slice(first_row, first_row + CH)
            _, vjp = jax.vjp(fn, [ck_ref[0, c, h].astype(F32) for h in heads], [q_ref[rows, _hsl(h)] for h in heads],
                             [f_ref[rows, _hsl(h)] for h in heads], [i_ref[rows, _hsl(h)] for h in heads],
                             [lb_ref[0, :, _hsl(h)] for h in heads])
            dstl, dq, df, di, dlb = vjp(([do_ref[rows, _hsl(h)] for h in heads], [dst[h] for h in heads]))
            for h in heads:
                dst[h] = dstl[h]
                dlb_ref[0, :, _hsl(h)] += dlb[h]
                if d == 0:
                    dq_ref, df_ref, di_ref = rest[:3]
                    dq_ref[rows, _hsl(h)] = dq[h].astype(BF16)
                    df_ref[rows, _hsl(h)] = df[h].astype(BF16)
                    di_ref[rows, _hsl(h)] = di[h].astype(BF16)
                else:
                    dq0_ref, df0_ref, di0_ref, _, dp_ref = rest[:5]
                    col = lambda k: slice(k * D + h * HD, k * D + (h + 1) * HD)
                    dp_ref[rows, col(0)] = (dq0_ref[rows, _hsl(h)].astype(F32) + dq[h]).astype(BF16)
                    dp_ref[rows, col(1)] = df0_ref[rows, _hsl(h)]
                    dp_ref[rows, col(2)] = df[h].astype(BF16)
                    dp_ref[rows, col(3)] = (di0_ref[rows, _hsl(h)].astype(F32) + di[h]).astype(BF16)

    const = lambda s: (d, 0, 0)
    at = lambda k: pl.BlockSpec((SCAN_ROWS, D), lambda s: (rev(s), k))
    in_specs = [at(0), at(1 + d), at(3), pl.BlockSpec((1, 1, D), const), pl.BlockSpec((1, CH, CH), const),
                pl.BlockSpec((1, CH, CH), const), pl.BlockSpec((1, CH, 1), const),
                pl.BlockSpec((1, SCAN_STEP, HEADS, HD, HD), lambda s: (d, nb - 1 - s, 0, 0, 0)), at(0)]
    dlb_spec, dlb_shape = pl.BlockSpec((1, 1, D), lambda s: (0, 0, 0)), jax.ShapeDtypeStruct((1, 1, D), F32)
    common = dict(grid=(nb,), scratch_shapes=[pltpu.VMEM((HEADS, HD, HD), F32)], compiler_params=_cp(1))
    if d == 0:
        return pl.pallas_call(body, name="hgrn_bwd_fwd_dir", in_specs=in_specs, out_specs=[at(0)] * 3 + [dlb_spec],
                              out_shape=[jax.ShapeDtypeStruct((t, D), BF16)] * 3 + [dlb_shape], **common,
                              )(parts, parts, parts, lb, mc, mtc, mrefc, ck, do)
    return pl.pallas_call(body, name="hgrn_bwd_bwd_dir", in_specs=in_specs + [at(0)] * 3 + [ANY],
                          out_specs=[pl.BlockSpec((SCAN_ROWS, 4 * D), lambda s: (rev(s), 0)), dlb_spec],
                          out_shape=[jax.ShapeDtypeStruct(dparts.shape, BF16), dlb_shape], input_output_aliases={12: 0},
                          **common)(parts, parts, parts, lb, mc, mtc, mrefc, ck, do, *first, dparts)


def in_proj_bwd_x(dx1, x, dparts, mod, nw, wg, after=None, latent_only=False):
    t = x.shape[0]
    tm = TM
    n_ctx = CTX // tm

    def body(dx1_ref, x_ref, dp_ref, mod_ref, nw_ref, w_ref, *rest):
        dx_ref, dnw_ref, dmod_ref = rest[-3:]
        i = pl.program_id(0)

        @pl.when(i == 0)
        def _():
            dnw_ref[...] = jnp.zeros_like(dnw_ref)
            dmod_ref[...] = jnp.zeros_like(dmod_ref)

        dh = mm_nt(dp_ref[:, 0:IN_SLOT], w_ref[0])
        for j in range(1, N_DEV):
            dh = dh + mm_nt(dp_ref[:, j * IN_SLOT:(j + 1) * IN_SLOT], w_ref[j])
        _norm_mod_bwd(x_ref, nw_ref, mod_ref, 0, dh, dx1_ref[...], dx_ref, dnw_ref, dmod_ref, i < n_ctx)

    tile = pl.BlockSpec((tm, D), lambda i: (i, 0))
    extra = [] if after is None else [jnp.reshape(after, (1, 1))]
    return pl.pallas_call(
        body, name="in_proj_bwd_x", grid=(t // tm,),
        in_specs=[tile, tile, pl.BlockSpec((tm, D_IN), lambda i: (i, 0)), pl.BlockSpec((1, 6, 1, D), _stream_row(tm)),
                  pl.BlockSpec((1, D), lambda i: (0, 0)), VMEM_WHOLE] + [ANY] * len(extra),
        out_specs=[pl.BlockSpec((tm, D), lambda i: (jnp.maximum(i - n_ctx, 0), 0)) if latent_only else tile,
                   pl.BlockSpec((1, D), lambda i: (0, 0)), pl.BlockSpec((2, 2, 1, D), lambda i: (0, 0, 0, 0))],
        out_shape=[jax.ShapeDtypeStruct((t - CTX if latent_only else t, D), F32), jax.ShapeDtypeStruct((1, D), F32),
                   jax.ShapeDtypeStruct((2, 2, 1, D), F32)],
        compiler_params=_cp(1))(dx1, x, dparts, mod, nw, wg, *extra)


def _lb_fn(h0, h1):
    m = jnp.maximum(h0, h1)
    e0, e1 = jnp.exp(h0 - m), jnp.exp(h1 - m)
    return e1 / (e0 + e1)


def lower_bounds(hlb):
    def body(h_ref, out_ref):
        out_ref[...] = _lb_fn(h_ref[0:1, :], h_ref[1:2, :])
    return pl.pallas_call(body, name="lower_bounds", out_shape=jax.ShapeDtypeStruct((1, 2 * D), F32))(hlb)


def lower_bounds_bwd(hlb, dlb1):
    def body(h_ref, d_ref, out_ref):
        _, vjp = jax.vjp(_lb_fn, h_ref[0:1, :], h_ref[1:2, :])
        d0, d1 = vjp(d_ref[...])
        out_ref[0:1, :] = d0
        out_ref[1:2, :] = d1
    return pl.pallas_call(body, name="lower_bounds_bwd", out_shape=jax.ShapeDtypeStruct((2, 2 * D), F32))(hlb, dlb1)


def _ada_fn(c_all, cctx8, w, b):
    dot = lambda a, l: mm(_silu(a), w[l]) + b[l]
    return [dot(c_all, l) for l in range(2)], [dot(cctx8, l) for l in range(2)]


def ada_fwd(c_all, cctx8, w, b):
    cols = w.shape[-1]

    def body(c_ref, cc_ref, w_ref, b_ref, out_ref):
        ox, oc = _ada_fn(c_ref[...], cc_ref[...], [w_ref[0], w_ref[1]], [b_ref[0], b_ref[1]])
        for l in range(2):
            out_ref[l, 0] = ox[l]
            out_ref[l, 1] = oc[l]
    return pl.pallas_call(body, name="ada_fwd", out_shape=jax.ShapeDtypeStruct((2, 2, N_DEV, cols), F32),
                          compiler_params=_cp(0))(c_all, cctx8, w, b)


def ada_bwd(c_all, cctx8, w, b, dmx, dmc):
    cols = w.shape[-1]

    def body(c_ref, cc_ref, w_ref, b_ref, dmx_ref, dmc_ref, dw_ref, dc_ref):
        fn = lambda cc, w0, w1: _ada_fn(c_ref[...], cc, [w0, w1], [b_ref[0], b_ref[1]])
        _, vjp = jax.vjp(fn, cc_ref[...], w_ref[0], w_ref[1])
        dcc, dw0, dw1 = vjp(([dmx_ref[0], dmx_ref[1]], [dmc_ref[0], dmc_ref[1]]))
        dw_ref[0] = dw0
        dw_ref[1] = dw1
        dc_ref[...] = jnp.sum(dcc, axis=0, keepdims=True)
    return pl.pallas_call(body, name="ada_bwd", out_shape=[jax.ShapeDtypeStruct((2, D, cols), F32), jax.ShapeDtypeStruct((1, D), F32)],
                          compiler_params=_cp(0))(c_all, cctx8, w, b, dmx, dmc)


def adamw(w, m, v, gparts, name):
    r, c = w.shape
    p = gparts.shape[0]
    rt = r
    while rt % 16 == 0 and (p + 7) * rt * c * 4 * 2 > ELEMENTWISE_VMEM:
        rt //= 2

    def body(w_ref, m_ref, v_ref, g_ref, go_ref, d_ref, mo_ref, vo_ref):
        g = g_ref[0].astype(F32)
        for k in range(1, p):
            g = g + g_ref[k].astype(F32)
        m2 = ADAM_B1 * m_ref[...] + (1.0 - ADAM_B1) * g
        v2 = ADAM_B2 * v_ref[...] + (1.0 - ADAM_B2) * (g * g)
        m_hat = m2 / (1.0 - ADAM_B1 ** ADAM_STEP)
        v_hat = v2 / (1.0 - ADAM_B2 ** ADAM_STEP)
        go_ref[...] = g
        d_ref[...] = -ADAM_LR * (m_hat / (jnp.sqrt(v_hat) + ADAM_EPS) + ADAM_WD * w_ref[...])
        mo_ref[...] = m2
        vo_ref[...] = v2

    tile = pl.BlockSpec((rt, c), lambda i: (i, 0))
    return pl.pallas_call(
        body, name=name, grid=(r // rt,),
        in_specs=[tile, tile, tile, pl.BlockSpec((p, rt, c), lambda i: (0, i, 0))], out_specs=[tile] * 4,
        out_shape=[jax.ShapeDtypeStruct((r, c), F32)] * 4, compiler_params=_cp(1))(w, m, v, gparts)


def _me():
    x, y, c = lax.axis_index("x"), lax.axis_index("y"), lax.axis_index("c")
    return x, y, c, 4 * x + 2 * y + c


def _peer(x, y, c, p):
    fx, fy, fc = (p >> 2) & 1, (p >> 1) & 1, p & 1
    return (1 - x if fx else x, 1 - y if fy else y, 1 - c if fc else c)


def all_gather(arrs, name, after=None):
    n = len(arrs)
    extra = [] if after is None else list(after) if isinstance(after, (list, tuple)) else [after]

    def body(*refs):
        ins, outs = refs[:n], refs[n + len(extra):2 * n + len(extra)]
        send, recv, local = refs[2 * n + len(extra):]
        x, y, c, me = _me()
        copies = []
        for a in range(n):
            lc = pltpu.make_async_copy(ins[a], outs[a].at[me], local.at[a])
            lc.start()
            copies.append(lc)
            for p in range(1, N_DEV):
                cp = pltpu.make_async_remote_copy(src_ref=ins[a], dst_ref=outs[a].at[me], send_sem=send.at[a, p - 1],
                                                  recv_sem=recv.at[a, p - 1], device_id=_peer(x, y, c, p),
                                                  device_id_type=pl.DeviceIdType.MESH)
                cp.start()
                copies.append(cp)
        for cp in copies:
            cp.wait()

    return pl.pallas_call(
        body, name=name, in_specs=[ANY] * (n + len(extra)), out_specs=[ANY] * n,
        out_shape=[jax.ShapeDtypeStruct((N_DEV,) + a.shape, a.dtype) for a in arrs],
        scratch_shapes=[pltpu.SemaphoreType.DMA((n, N_DEV - 1)), pltpu.SemaphoreType.DMA((n, N_DEV - 1)),
                        pltpu.SemaphoreType.DMA((n,))])(*arrs, *extra)


HBM = pl.BlockSpec(memory_space=pltpu.HBM)
SEM = pl.BlockSpec(memory_space=pltpu.SEMAPHORE)


def _in_hbm(a):
    return pltpu.with_memory_space_constraint(a, pltpu.HBM)


ALL_PEERS = tuple(range(1, N_DEV))
SAME_CORE_AND_SIBLING = (1, 2, 4, 6)
OTHER_CHIPS = (2, 4, 6)


def _exchange_refs(srcs, lands, layer, scatter, a, x, y, c, p, forward=False):
    me = 4 * x + 2 * y + c
    px, py, pc = _peer(x, y, c, p) if p else (x, y, c)
    if forward and p:
        slot = lands[a].at[4 * px + 2 * py + pc]
        return slot, slot, _peer(x, y, c, 1)
    dst = lands[a].at[me] if layer is None else lands[a].at[me, layer]
    src = srcs[a].at[4 * px + 2 * py + pc] if scatter else dst
    return src, dst, (px, py, pc)


def exchange_start(srcs, lands, layer, scatter, name, after=None, peers=ALL_PEERS, forward=False):
    n, ns = len(lands), len(srcs)
    extra = [] if after is None else [after]

    def body(*refs):
        ins, lz = refs[:ns], refs[ns:ns + n]
        send, recv = refs[ns + n + len(extra)], refs[ns + n + len(extra) + 1]
        token = refs[-1]
        x, y, c, _ = _me()
        for a in range(n):
            for p in peers:
                src, dst, peer = _exchange_refs(ins, lz, layer, scatter, a, x, y, c, p, forward)
                k = a * (N_DEV - 1) + p - 1
                pltpu.make_async_remote_copy(src_ref=src, dst_ref=dst, send_sem=send.at[k], recv_sem=recv.at[k],
                                             device_id=peer, device_id_type=pl.DeviceIdType.MESH).start()
        token[...] = jnp.zeros_like(token)

    thru = [pltpu.HBM(a.shape, a.dtype) for a in list(srcs) + list(lands)]
    out = pl.pallas_call(
        body, name=name, in_specs=[HBM] * (ns + n) + [ANY] * len(extra),
        out_specs=[SEM, SEM] + [HBM] * (ns + n) + [pl.BlockSpec(memory_space=pltpu.VMEM)],
        out_shape=[pltpu.SemaphoreType.DMA((n * (N_DEV - 1),)), pltpu.SemaphoreType.DMA((n * (N_DEV - 1),))] + thru
        + [jax.ShapeDtypeStruct((8, 128), F32)],
        input_output_aliases={i: 2 + i for i in range(ns + n)},
        compiler_params=pltpu.CompilerParams(has_side_effects=pltpu.SideEffectType.DATAFLOW_SIDE_EFFECTING),
    )(*[_in_hbm(a) for a in list(srcs) + list(lands)], *extra)
    return out[0], out[1], out[2:2 + ns], out[2 + ns:2 + ns + n], out[-1]


def exchange_wait(send, recv, srcs, lands, layer, scatter, after, name, peers=ALL_PEERS):
    n, ns = len(lands), len(srcs)

    def body(*refs):
        ins, lz = refs[:ns], refs[ns:ns + n]
        send_ref, recv_ref = refs[ns + n], refs[ns + n + 1]
        x, y, c, _ = _me()
        for a in range(n):
            for p in peers:
                src, dst, peer = _exchange_refs(ins, lz, layer, scatter, a, x, y, c, 0)
                k = a * (N_DEV - 1) + p - 1
                cp = pltpu.make_async_remote_copy(src_ref=src, dst_ref=dst, send_sem=send_ref.at[k],
                                                  recv_sem=recv_ref.at[k], device_id=peer,
                                                  device_id_type=pl.DeviceIdType.MESH)
                cp.wait_send()
                cp.wait_recv()

    thru = [pltpu.HBM(a.shape, a.dtype) for a in list(srcs) + list(lands)]
    out = pl.pallas_call(
        body, name=name, in_specs=[HBM] * (ns + n) + [SEM, SEM, ANY], out_specs=[HBM] * (ns + n), out_shape=thru,
        input_output_aliases={i: i for i in range(ns + n)},
        compiler_params=pltpu.CompilerParams(has_side_effects=pltpu.SideEffectType.DATAFLOW_SIDE_EFFECTING),
    )(*srcs, *lands, send, recv, after)
    return out[ns:]


def place_own(src, land, me, layer, scatter, name, src_layer=None, after=None):
    create = isinstance(land, jax.ShapeDtypeStruct)
    r, c = src.shape[-2:]
    rt = r
    while rt % 32 == 0 and rt * c * 4 > COPY_BLOCK_BYTES:
        rt //= 2

    extra = [] if after is None else [after]

    def body(me_ref, src_ref, *rest):
        out_ref = rest[-1]
        out_ref[...] = src_ref[...].reshape(out_ref.shape).astype(out_ref.dtype)

    src_spec = (pl.BlockSpec((1, rt, c), lambda i, m: (m[0], i, 0)) if scatter else
                pl.BlockSpec((rt, c), lambda i, m: (i, 0)) if src_layer is None else
                pl.BlockSpec((1, rt, c), lambda i, m: (src_layer, i, 0)))
    out_spec = (pl.BlockSpec((1, rt, c), lambda i, m: (m[0], i, 0)) if layer is None
                else pl.BlockSpec((1, 1, rt, c), lambda i, m: (m[0], layer, i, 0)))
    grid_spec = pltpu.PrefetchScalarGridSpec(num_scalar_prefetch=1, grid=(r // rt,),
                                             in_specs=[src_spec] + ([] if create else [ANY]) + [ANY] * len(extra),
                                             out_specs=out_spec)
    return pl.pallas_call(body, name=name, grid_spec=grid_spec, out_shape=jax.ShapeDtypeStruct(land.shape, land.dtype),
                          input_output_aliases={} if create else {2: 0}, compiler_params=_cp(1),
                          )(*((me, src) if create else (me, src, land)), *extra)


def _scan_constants():
    r = lax.broadcasted_iota(jnp.int32, (CH, CH), 0)
    s = lax.broadcasted_iota(jnp.int32, (CH, CH), 1)
    lower = (s <= r).astype(F32)
    t = jnp.arange(CH)[:, None]
    mc = jnp.stack([lower, lower.T])
    mref = jnp.stack([(t <= CH // 2 - 1).astype(F32), (t >= CH // 2).astype(F32)])
    return mc, jnp.stack([lower.T, lower]), mref


def local_step(x, ctx, target, mod, lb, w, fetch=None, publish=None, small_ready=None, small_early=None):
    kept = {}

    def keep(l, part, grads):
        kept[(l, part)] = grads
        return 0.0

    fetch = fetch or (lambda l, part, after: w)
    publish = publish or keep
    n_layers = len(mod)
    mc, mtc, mrefc = _scan_constants()
    xs = jnp.concatenate([ctx, x], axis=0)
    saved, big = [], []
    for l in range(n_layers):
        wl = dict(fetch(l, "in", xs))
        parts, ht, iv, mix = in_proj_fwd(xs, mod[l], w["nw1"][l], wl["win"][l])
        o, ck = hgrn_fwd(parts, iv, lb[l], mc, mtc, mrefc)
        wl.update(fetch(l, "mix", o))
        last = l == n_layers - 1
        x1, pa, pb, ym, yat, ybt, mt = mixer_fwd(xs, mix, o, mod[l], w["lnw"][l], w["lnb"][l], w["sw"][l], w["sb"][l],
                                                 w["hnw"][l], wl["wa"][l], wl["wb"][l], wl["wo"][l], last)
        wl.update(fetch(l, "ffn", x1))
        av, h2t = ffn_up_fwd(x1, mod[l], w["nw2"][l], wl["wup"][l], last)
        x2, ac, y, z = ffn_down_fwd(x1, av, mod[l], w["cw"][l], w["cb"][l], wl["wd"][l], last)
        saved.append((xs, parts, mix, o, ck, x1, av, ac, y, z, ht, h2t, pa, pb, ym, yat, ybt, mt))
        big.append(wl)
        xs = x2
    loss, dx, dfw = loss_fwd_bwd(xs, target, w["fw"])
    g = {k: [None] * n_layers for k in ("nw1", "nw2", "lnw", "lnb", "sw", "sb", "hnw", "cw", "cb")}
    g["fw"] = dfw
    dmod, dlb = [None] * n_layers, [None] * n_layers
    tok = 0.0
    for l in reversed(range(n_layers)):
        x0, parts, mix, o, ck, x1, av, ac, y, z, ht, h2t, pa, pb, ym, yat, ybt, mt = saved[l]
        wl = big[l]
        last = l == n_layers - 1
        dav, dac, dout, dg2 = ffn_down_bwd(dx, ac, av, y, mod[l] + tok, wl["wd"][l], last)
        dwd = weight_grad_rows(z, dout, "ffn_down_bwd_w")
        dav, g["cw"][l], g["cb"][l] = conv_bwd(dav, dac, av, w["cw"][l], last)
        dx1, g["nw2"][l], dmod2 = ffn_up_bwd_x(dx, x1, dav, mod[l], w["nw2"][l], wl["wup"][l], last)
        dwup = weight_grad(h2t, dav, FF_SLOT, "ffn_up_bwd_w")
        tok = publish(l, "ffn", {"wd": dwd, "wup": dwup})
        (dparts, do, dy, dpa, dpb, g["lnw"][l], g["lnb"][l], g["sw"][l], g["sb"][l], g["hnw"][l],
         dg1) = mixer_bwd(dx1, mix, o, pa, pb, ym, mod[l] + tok, w["lnw"][l], w["lnb"][l], w["sw"][l], w["sb"][l],
                          w["hnw"][l], wl["wa"][l], wl["wb"][l], wl["wo"][l], last)
        tok = publish(l, "mix", {"wa": weight_grad(yat, dpa, D, "mixer_bwd_wa"), "wb": weight_grad(ybt, dpb, D, "mixer_bwd_wb"),
                                 "wo": weight_grad(mt, dy, D, "mixer_bwd_wo")})
        if l == 0 and small_early:
            dmod[0] = jnp.concatenate([jnp.zeros((2, 2, 1, D), F32), dg1, dmod2, dg2], axis=1)
            tok = tok + small_early(loss[0, 0], g, dmod, dlb)
        dq, df, di, dlb_f = hgrn_bwd(0, parts, lb[l] + tok, mc, mtc, mrefc, ck, do)
        dparts, dlb_b = hgrn_bwd(1, parts, lb[l], mc, mtc, mrefc, ck, do, (dq, df, di), dparts)
        dlb[l] = jnp.concatenate([dlb_f, dlb_b], axis=0)
        tok = publish(l, "in", {"win": weight_grad(ht, dparts, IN_SLOT, "in_proj_bwd_w")})
        dx, g["nw1"][l], dmod1 = in_proj_bwd_x(dx1, x0, dparts, mod[l], w["nw1"][l], wl["win"][l], after=tok,
                                               latent_only=l == 0)
        dmod[l] = jnp.concatenate([dmod1, dg1, dmod2, dg2], axis=1)
    done = small_ready(loss[0, 0], g, dmod, dlb) if small_ready else 0.0
    for (l, part), grads in kept.items():
        for k, v in grads.items():
            g.setdefault(k, [None] * n_layers)[l] = v
    return loss[0, 0], dx, g, dmod, dlb, done


ROW = 1024
REPLICATED = ("norm1_w", "sgu_ln_w", "sgu_ln_b", "sgu_w", "sgu_b", "hgrn_lower_bounds", "hgrn_norm_w", "norm2_w",
              "ffn_conv_b", "final_norm_w")
WEIGHT_ORDER = ("c_ctx", "ada_w", "ada_b", "norm1_w", "w_in", "sgu_ln_w", "sgu_ln_b", "sgu_w", "sgu_b", "hgrn_lower_bounds",
                "hgrn_norm_w", "w_branch_a", "w_branch_b", "w_out", "norm2_w", "ffn_w_up", "ffn_conv_w", "ffn_conv_b",
                "ffn_w_down", "final_norm_w")


def _rows_of(n):
    return -(-n // (8 * ROW)) * 8


def _pack(arrs, total_rows=None):
    parts = []
    for a in arrs:
        flat = a.reshape(-1).astype(F32)
        rows = _rows_of(flat.shape[0])
        parts.append(jnp.pad(flat, (0, rows * ROW - flat.shape[0])).reshape(rows, ROW))
    have = sum(p.shape[0] for p in parts)
    if total_rows is not None and total_rows > have:
        parts.append(jnp.zeros((total_rows - have, ROW), F32))
    return jnp.concatenate(parts, axis=0)


def _unpack(packed, shapes):
    lead = packed.shape[:-2]
    out, r0 = [], 0
    for s in shapes:
        n = math.prod(s)
        rows = _rows_of(n)
        out.append(packed[..., r0:r0 + rows, :].reshape(lead + (rows * ROW,))[..., :n].reshape(lead + tuple(s)))
        r0 += rows
    return out


def kernel(x, c, ctx, c_ctx, ada_w, ada_b, norm1_w, w_in, sgu_ln_w, sgu_ln_b, sgu_w, sgu_b, hgrn_lower_bounds, hgrn_norm_w, w_branch_a, w_branch_b, w_out, norm2_w, ffn_w_up, ffn_conv_w, ffn_conv_b, ffn_w_down, final_norm_w, loss_target, m_c_ctx, m_ada_w, m_ada_b, m_norm1_w, m_w_in, m_sgu_ln_w, m_sgu_ln_b, m_sgu_w, m_sgu_b, m_hgrn_lower_bounds, m_hgrn_norm_w, m_w_branch_a, m_w_branch_b, m_w_out, m_norm2_w, m_ffn_w_up, m_ffn_conv_w, m_ffn_conv_b, m_ffn_w_down, m_final_norm_w, v_c_ctx, v_ada_w, v_ada_b, v_norm1_w, v_w_in, v_sgu_ln_w, v_sgu_ln_b, v_sgu_w, v_sgu_b, v_hgrn_lower_bounds, v_hgrn_norm_w, v_w_branch_a, v_w_branch_b, v_w_out, v_norm2_w, v_ffn_w_up, v_ffn_conv_w, v_ffn_conv_b, v_ffn_w_down, v_final_norm_w):
    wts = dict(c_ctx=c_ctx, ada_w=ada_w, ada_b=ada_b, norm1_w=norm1_w, w_in=w_in, sgu_ln_w=sgu_ln_w, sgu_ln_b=sgu_ln_b,
               sgu_w=sgu_w, sgu_b=sgu_b, hgrn_lower_bounds=hgrn_lower_bounds, hgrn_norm_w=hgrn_norm_w, w_branch_a=w_branch_a,
               w_branch_b=w_branch_b, w_out=w_out, norm2_w=norm2_w, ffn_w_up=ffn_w_up, ffn_conv_w=ffn_conv_w,
               ffn_conv_b=ffn_conv_b, ffn_w_down=ffn_w_down, final_norm_w=final_norm_w)
    mom1 = dict(c_ctx=m_c_ctx, ada_w=m_ada_w, ada_b=m_ada_b, norm1_w=m_norm1_w, w_in=m_w_in, sgu_ln_w=m_sgu_ln_w,
                sgu_ln_b=m_sgu_ln_b, sgu_w=m_sgu_w, sgu_b=m_sgu_b, hgrn_lower_bounds=m_hgrn_lower_bounds,
                hgrn_norm_w=m_hgrn_norm_w, w_branch_a=m_w_branch_a, w_branch_b=m_w_branch_b, w_out=m_w_out, norm2_w=m_norm2_w,
                ffn_w_up=m_ffn_w_up, ffn_conv_w=m_ffn_conv_w, ffn_conv_b=m_ffn_conv_b, ffn_w_down=m_ffn_w_down,
                final_norm_w=m_final_norm_w)
    mom2 = dict(c_ctx=v_c_ctx, ada_w=v_ada_w, ada_b=v_ada_b, norm1_w=v_norm1_w, w_in=v_w_in, sgu_ln_w=v_sgu_ln_w,
                sgu_ln_b=v_sgu_ln_b, sgu_w=v_sgu_w, sgu_b=v_sgu_b, hgrn_lower_bounds=v_hgrn_lower_bounds,
                hgrn_norm_w=v_hgrn_norm_w, w_branch_a=v_w_branch_a, w_branch_b=v_w_branch_b, w_out=v_w_out, norm2_w=v_norm2_w,
                ffn_w_up=v_ffn_w_up, ffn_conv_w=v_ffn_conv_w, ffn_conv_b=v_ffn_conv_b, ffn_w_down=v_ffn_w_down,
                final_norm_w=v_final_norm_w)
    n_layers = w_in.shape[0]
    layers = range(n_layers)
    me = 4 * lax.axis_index("x") + 2 * lax.axis_index("y") + lax.axis_index("c")
    ada_cols = ada_w.shape[-1]

    big = ("w_in", "ffn_w_up", "w_branch_a", "w_branch_b", "w_out", "ffn_w_down")
    short = {"w_in": "win", "ffn_w_up": "wup", "w_branch_a": "wa", "w_branch_b": "wb", "w_out": "wo", "ffn_w_down": "wd"}
    me1 = me.reshape(1).astype(jnp.int32)
    mixer, ffn = ("w_branch_a", "w_branch_b", "w_out"), ("ffn_w_up", "ffn_w_down")
    groups = [[(k, l) for k in part] for l in layers for part in (("w_in",), mixer, ffn)]
    group_of = {(l, part): 3 * l + n for l in layers for n, part in enumerate(("in", "mix", "ffn"))}
    in_flight, started = [], 0.0

    def own_slots(n, after):
        return [place_own(wts[k], jax.ShapeDtypeStruct((N_DEV,) + wts[k].shape[1:], BF16), me1, None, False,
                          f"gather_own_{short[k]}_{l}", src_layer=l, after=after) for k, l in groups[n]]

    def start_group(n, lands, after):
        in_flight.append(exchange_start([], lands, None, False, f"gather_weights_start_{n}", after=after,
                                        peers=SAME_CORE_AND_SIBLING if n == 0 else ALL_PEERS))
        return in_flight[-1][-1]

    (c_all,) = all_gather([c], "gather_c")
    c_all = c_all.reshape(N_DEV, D)
    token = start_group(0, own_slots(0, c_all), c_all)
    later = [own_slots(n, token) for n in range(1, len(groups))]
    cctx8 = jnp.broadcast_to(c_ctx[None, :], (N_DEV, D))
    ada_b_cols = lax.dynamic_slice_in_dim(ada_b, me * ada_cols, ada_cols, axis=1)[:, None, :]
    mod_cols = ada_fwd(c_all, cctx8, ada_w, ada_b_cols)
    xs = jnp.concatenate([ctx[0], x[0]], axis=0)
    lb1 = lower_bounds(hgrn_lower_bounds)
    mod_all, conv_all = all_gather([mod_cols, ffn_conv_w.reshape(n_layers, 9, -1)], "gather_mod_conv",
                                   after=[token, xs, lb1] + [a for lands in later for a in lands])
    conv_full = [conv_all[:, l].transpose(1, 0, 2).reshape(9, N_FFK, FF_SLOT).transpose(1, 0, 2) for l in layers]
    for n in range(1, len(groups)):
        token = start_group(n, later[n - 1], mod_all if n == 1 else token)
    for started_group in in_flight:
        started = started + started_group[-1][0, 0]

    def as_used(k, a):
        return a if k in ("w_in", "ffn_w_up") else a.reshape(N_FFK, FF_SLOT, D) if k == "ffn_w_down" else a.reshape(D, D)

    arrived = {}

    def fetch(l, part, after):
        n = group_of[(l, part)]
        send, recv, _, lands, _ = in_flight[n]
        first = n == 0
        got = exchange_wait(send, recv, [], lands, None, False, after, f"gather_weights_wait_{n}",
                            peers=SAME_CORE_AND_SIBLING if first else ALL_PEERS)
        if first:
            send, recv, _, lands, _ = exchange_start([], got, None, False, "gather_weights_pass_on", peers=OTHER_CHIPS,
                                                     forward=True)
            got = exchange_wait(send, recv, [], lands, None, False, after, "gather_weights_passed_on", peers=OTHER_CHIPS)
        for (k, ll), a in zip(groups[n], got):
            arrived.setdefault(short[k], [None] * n_layers)[ll] = as_used(k, a)
        return arrived

    mod_x = lax.dynamic_index_in_dim(mod_all[:, :, 0], me, axis=2, keepdims=False)
    mod_c = mod_all[:, :, 1, 0]
    mod = [jnp.stack([mod_c[:, l].reshape(6, 1, D), mod_x[:, l].reshape(6, 1, D)]) for l in layers]
    mod[0] = mod[0] + started

    lb = [jnp.zeros((2, 1, D), F32), lb1.reshape(2, 1, D)]

    w = {
        "nw1": [norm1_w[l][None] for l in layers], "nw2": [norm2_w[l][None] for l in layers],
        "lnw": [sgu_ln_w[l][None] for l in layers], "lnb": [sgu_ln_b[l][None] for l in layers],
        "sw": [sgu_w[l] for l in layers], "sb": [sgu_b[l][:, :, None] for l in layers],
        "hnw": [hgrn_norm_w[l][None] for l in layers], "cw": conv_full,
        "cb": [ffn_conv_b[l].reshape(N_FFK, 1, FF_SLOT) for l in layers], "fw": final_norm_w[None],
    }
    long = {v: k for k, v in short.items()}
    landing, sent = {}, []

    def publish(l, part, grads):
        keys = [long[k] for k in grads]
        slots = [a.reshape((N_DEV, -1, a.shape[-1])) for a in grads.values()]
        zones = [place_own(s, landing.get(k, jax.ShapeDtypeStruct((N_DEV, n_layers) + s.shape[1:], s.dtype)), me1, l, True,
                           f"scatter_own_{short[k]}_{l}") for k, s in zip(keys, slots)]
        send, recv, srcs, zones, token = exchange_start(slots, zones, l, True, f"scatter_grads_start_{part}_{l}")
        landing.update(zip(keys, zones))
        sent.append((keys, l, part, send, recv, srcs, token))
        return token[0, 0]

    out = {}
    flat2 = lambda a: a.reshape(-1, a.shape[-1])

    def finish(part, after):
        done = []
        for keys, l, p, send, recv, srcs, _ in sent:
            if p == part:
                zones = exchange_wait(send, recv, srcs, [landing[k] for k in keys], l, True, after,
                                      f"scatter_grads_wait_{part}_{l}")
                landing.update(zip(keys, zones))
                done = keys
        for k in done:
            r = landing[k]
            res = adamw(flat2(wts[k]), flat2(mom1[k]), flat2(mom2[k]), r.reshape(N_DEV, -1, r.shape[-1]), "adamw_" + k)
            out[k] = tuple(a.reshape(wts[k].shape) for a in res)

    rep_rows = -(-sum(_rows_of(wts[k].size) for k in REPLICATED) // 64) * 64
    conv_rows = _rows_of(n_layers * 9 * D_FF)
    dmod_rows = _rows_of(n_layers * 6 * D)
    early = {}

    def small_early(loss_part, g, dmod, dlb):
        d_hlb = lower_bounds_bwd(hgrn_lower_bounds, dlb[1].reshape(1, 2 * D))
        st = lambda k: jnp.stack([jnp.zeros((1, D), F32) if a is None else a for a in g[k]])
        rep_grads = {"norm1_w": st("nw1"), "sgu_ln_w": st("lnw"), "sgu_ln_b": st("lnb"), "sgu_w": st("sw"), "sgu_b": st("sb"),
                     "hgrn_lower_bounds": d_hlb, "hgrn_norm_w": st("hnw"), "norm2_w": st("nw2"), "ffn_conv_b": st("cb"),
                     "final_norm_w": g["fw"]}
        d_conv = jnp.stack([g["cw"][l].transpose(1, 0, 2).reshape(9, D_FF) for l in layers])
        dmod_x = jnp.stack([dmod[l][1].reshape(6 * D) for l in layers])
        dmod_c = jnp.stack([dmod[l][0].reshape(6 * D) for l in layers])
        small = jnp.concatenate([_pack([rep_grads[k] for k in REPLICATED], rep_rows),
                                 _pack([d_conv, dmod_x, dmod_c, loss_part.reshape(1)])], axis=0)
        zone = place_own(small, jax.ShapeDtypeStruct((N_DEV,) + small.shape, F32), me1, None, False, "gather_small_own")
        early["send"], early["recv"], _, early["zones"], token = exchange_start([], [zone], None, False, "gather_small_start")
        return token[0, 0]

    def small_ready(loss_part, g, dmod, dlb):
        late = _pack([g["nw1"][0], dmod[0][1, 0:2], dmod[0][0, 0:2]])
        for part in ("ffn", "mix"):
            finish(part, late)
        (late_all,) = all_gather([late], "gather_small_late", after=[out[k][0] for k in big[1:]])
        (small_all,) = exchange_wait(early["send"], early["recv"], [], early["zones"], None, False, late_all,
                                     "gather_small_wait")
        at_x = rep_rows + conv_rows
        small_all = small_all.at[:, 0:1].set(late_all[:, 0:1])
        small_all = small_all.at[:, at_x:at_x + 2].set(late_all[:, 8:10])
        small_all = small_all.at[:, at_x + dmod_rows:at_x + dmod_rows + 2].set(late_all[:, 16:18])
        d_conv_shape, dmod_shape = (n_layers, 9, D_FF), (n_layers, 6 * D)
        conv_g, dmx_all, dmc_all, loss_all = _unpack(small_all[:, rep_rows:], [d_conv_shape, dmod_shape, dmod_shape, (1,)])
        out["loss"] = functools.reduce(lambda a, b: a + b, [loss_all[k, 0] for k in range(N_DEV)])

        rep = adamw(_pack([wts[k] for k in REPLICATED], rep_rows), _pack([mom1[k] for k in REPLICATED], rep_rows),
                    _pack([mom2[k] for k in REPLICATED], rep_rows), small_all, "adamw_replicated")
        rep = [_unpack(r, [wts[k].shape for k in REPLICATED]) for r in rep]
        for n, k in enumerate(REPLICATED):
            out[k] = tuple(r[n] for r in rep)

        conv_mine = lax.dynamic_index_in_dim(conv_g.reshape(N_DEV, n_layers, 9, N_DEV, -1), me, axis=3, keepdims=False)
        res = adamw(flat2(ffn_conv_w), flat2(m_ffn_conv_w), flat2(v_ffn_conv_w),
                    conv_mine.reshape(N_DEV, -1, conv_mine.shape[-1]), "adamw_conv_w")
        out["ffn_conv_w"] = tuple(r.reshape(ffn_conv_w.shape) for r in res)

        out["ada_b"] = tuple(adamw(ada_b, m_ada_b, v_ada_b, jnp.concatenate([dmx_all, dmc_all], axis=0), "adamw_ada_b"))

        cols_of = lambda a: lax.dynamic_slice_in_dim(a, me * ada_cols, ada_cols, axis=2).transpose(1, 0, 2)
        d_ada_w, d_cctx = ada_bwd(c_all, cctx8, ada_w, ada_b_cols, cols_of(dmx_all), cols_of(dmc_all))
        res = adamw(flat2(ada_w), flat2(m_ada_w), flat2(v_ada_w), flat2(d_ada_w)[None], "adamw_ada_w")
        out["ada_w"] = tuple(r.reshape(ada_w.shape) for r in res)
        (d_cctx_all,) = all_gather([d_cctx], "gather_c_ctx_grad")
        res = adamw(c_ctx[None], m_c_ctx[None], v_c_ctx[None], d_cctx_all, "adamw_c_ctx")
        out["c_ctx"] = tuple(r[0] for r in res)
        return d_cctx_all

    _, grad_x, _, _, _, small_done = local_step(x[0], ctx[0], loss_target[0], mod, lb, w, fetch, publish, small_ready,
                                                small_early)
    loss = out["loss"]

    finish("in", small_done)
    return (loss, grad_x[None]) + tuple(out[k][n] for n in range(4) for k in WEIGHT_ORDER)
```

```python
import functools
import math

import jax
import jax.numpy as jnp
from jax import lax
from jax.experimental import pallas as pl
from jax.experimental.pallas import tpu as pltpu

F32 = jnp.float32
BF16 = jnp.bfloat16

N_DEV = 8
AXES = ("x", "y", "c")
D = 1024
CTX = 256
TM = 256
CH = 64
SGU_CH = 128
HEADS = 8
HD = 128
GRID_W = 64
D_IN = 9 * D
IN_SLOT = D_IN // N_DEV
D_FF = 2816
FF_SLOT = 2 * D_FF // N_DEV
N_FFK = D_FF // FF_SLOT
RMS_EPS = 1e-6
LN_EPS = 1e-5
ADAM_LR, ADAM_B1, ADAM_B2, ADAM_EPS, ADAM_WD, ADAM_STEP = 0.001, 0.9, 0.999, 1e-08, 0.01, 10
VMEM_LIMIT_V7X = 56 * 2 ** 20
ELEMENTWISE_VMEM = 24 * 2 ** 20
COPY_BLOCK_BYTES = 2 ** 21
GRAD_WIRE = jnp.bfloat16

VMEM_WHOLE = pl.BlockSpec(memory_space=pltpu.VMEM)
ANY = pl.BlockSpec(memory_space=pl.ANY)


def _cp(n_axes):
    return pltpu.CompilerParams(dimension_semantics=("arbitrary",) * n_axes, vmem_limit_bytes=VMEM_LIMIT_V7X)


def _dot(a, b, dims):
    return lax.dot_general(a.astype(BF16), b.astype(BF16), (dims, ((), ())), preferred_element_type=F32)


@jax.custom_vjp
def mm(a, b):
    return _dot(a, b, ((1,), (0,)))


mm.defvjp(lambda a, b: (mm(a, b), (a, b)),
          lambda r, g: (_dot(g, r[1], ((1,), (1,))).astype(r[0].dtype), _dot(r[0], g, ((0,), (0,))).astype(r[1].dtype)))


@jax.custom_vjp
def mm_nt(a, b):
    return _dot(a, b, ((1,), (1,)))


mm_nt.defvjp(lambda a, b: (mm_nt(a, b), (a, b)),
             lambda r, g: (_dot(g, r[1], ((1,), (0,))).astype(r[0].dtype), _dot(g, r[0], ((0,), (0,))).astype(r[1].dtype)))


@jax.custom_vjp
def mm_tn(a, b):
    return _dot(a, b, ((0,), (0,)))


mm_tn.defvjp(lambda a, b: (mm_tn(a, b), (a, b)),
             lambda r, g: (_dot(r[1], g, ((1,), (1,))).astype(r[0].dtype), _dot(r[0], g, ((1,), (0,))).astype(r[1].dtype)))


def _tri_dot(m, g):
    hi = g.astype(BF16)
    low = (g - hi.astype(F32)).astype(BF16)
    n = g.shape[1]
    out = jnp.dot(m.astype(BF16), jnp.concatenate([hi, low], axis=1), preferred_element_type=F32)
    return out[:, :n] + out[:, n:]


@jax.custom_vjp
def _cum(m, mt, g):
    return _tri_dot(m, g)


_cum.defvjp(lambda m, mt, g: (_cum(m, mt, g), (m, mt)),
            lambda r, d: (jnp.zeros_like(r[0]), jnp.zeros_like(r[1]), _tri_dot(r[1], d)))


def _silu(x):
    return x * jax.nn.sigmoid(x)


def _gelu(x):
    return 0.5 * x * (1.0 + jnp.tanh(math.sqrt(2.0 / math.pi) * (x + 0.044715 * (x * x * x))))


def _rms(x, w):
    return x * lax.rsqrt(jnp.mean(x * x, axis=-1, keepdims=True) + RMS_EPS) * w


def _norm_mod(x, w, shift, scale):
    return _rms(x, w) * (1.0 + scale) + shift


def _hsl(h):
    return slice(h * HD, (h + 1) * HD)


def _hgrn_chunk(st, qz, fz, iv, lb, m, mt, mref):
    hs = range(HEADS)
    keep = [1.0 - lb[h] for h in hs]
    sg = [jax.nn.sigmoid(fz[h]) for h in hs]
    g = [jnp.log(lb[h] + keep[h] * sg[h]) for h in hs]
    k = [keep[h] * (1.0 - sg[h]) for h in hs]
    q = [_silu(qz[h]) for h in hs]
    b = [_cum(m, mt, g[h]) for h in hs]
    ref = [jnp.sum(mref * g[h], axis=0, keepdims=True) for h in hs]
    last = [jnp.sum(g[h], axis=0, keepdims=True) for h in hs]
    qa = [q[h] * jnp.exp(b[h] - ref[h]) for h in hs]
    ka = [k[h] * jnp.exp(ref[h] - b[h]) for h in hs]
    scores = [jnp.where(m > 0.5, mm_nt(qa[h], ka[h]), 0.0) for h in hs]
    inter = [mm_nt(qa[h] * jnp.exp(ref[h]), st[h]) for h in hs]
    kv = [mm_tn(iv[h], ka[h] * jnp.exp(last[h] - ref[h])) for h in hs]
    outs = [mm(scores[h], iv[h]) + inter[h] for h in hs]
    news = [jnp.exp(last[h]) * st[h] + kv[h] for h in hs]
    return outs, news


def _sgu_fn(ub, vb, lnw, lnb, sw, sb):
    gv = [_gelu(v) for v in vb]
    mu = sum(jnp.sum(t, axis=-1, keepdims=True) for t in gv) / D
    var = sum(jnp.sum((t - mu) * (t - mu), axis=-1, keepdims=True) for t in gv) / D
    inv = lax.rsqrt(var + LN_EPS)
    cols = []
    for g in range(HEADS):
        vn = (gv[g] - mu) * inv * lnw[g] + lnb[g]
        cols.append(_gelu(ub[g]) * (mm(sw[g], vn) + sb[g]))
    return jnp.concatenate(cols, axis=1)


def _readout_fn(ob, og, hnw):
    r = [o * lax.rsqrt(jnp.mean(o * o, axis=-1, keepdims=True) + RMS_EPS) * hnw for o in ob]
    return jnp.concatenate(r, axis=1) * _silu(og)


def _glu_fn(ac, v):
    return _gelu(ac) * v


def _stream_row(tm):
    n_ctx = CTX // tm
    return lambda i: (jnp.where(i < n_ctx, 0, 1), 0, 0, 0)


def in_proj_fwd(x, mod, nw, wg):
    t = x.shape[0]

    def body(x_ref, mod_ref, nw_ref, w_ref, out_ref, ht_ref, iv_ref, mix_ref, wfull, sems):
        @pl.when(pl.program_id(0) == 0)
        def _():
            copies = [pltpu.make_async_copy(w_ref.at[j], wfull.at[:, j * IN_SLOT:(j + 1) * IN_SLOT], sems.at[j])
                      for j in range(N_DEV)]
            for c in copies:
                c.start()
            for c in copies:
                c.wait()

        h32 = _norm_mod(x_ref[...], nw_ref[...], mod_ref[0, 0], mod_ref[0, 1])
        ht_ref[...] = h32.T.astype(BF16)
        h = h32.astype(BF16)
        for k in range(D_IN // D):
            part = jnp.dot(h, wfull[:, k * D:(k + 1) * D], preferred_element_type=F32)
            if k < 4:
                out_ref[:, k * D:(k + 1) * D] = part
            if k == 3:
                iv_ref[...] = part.astype(BF16)
            if k >= 4:
                mix_ref[:, (k - 4) * D:(k - 3) * D] = part.astype(BF16)

    return pl.pallas_call(
        body, name="in_proj_fwd", grid=(t // TM,),
        in_specs=[pl.BlockSpec((TM, D), lambda i: (i, 0)), pl.BlockSpec((1, 6, 1, D), _stream_row(TM)),
                  pl.BlockSpec((1, D), lambda i: (0, 0)), ANY],
        out_specs=[pl.BlockSpec((TM, 4 * D), lambda i: (i, 0)), pl.BlockSpec((D, TM), lambda i: (0, i)),
                   pl.BlockSpec((TM, D), lambda i: (i, 0)), pl.BlockSpec((TM, 5 * D), lambda i: (i, 0))],
        out_shape=[jax.ShapeDtypeStruct((t, 4 * D), F32), jax.ShapeDtypeStruct((D, t), BF16), jax.ShapeDtypeStruct((t, D), BF16),
                   jax.ShapeDtypeStruct((t, 5 * D), BF16)],
        scratch_shapes=[pltpu.VMEM((D, D_IN), BF16), pltpu.SemaphoreType.DMA((N_DEV,))], compiler_params=_cp(1))(x, mod, nw, wg)


SCAN_STEP = 4
SCAN_ROWS = SCAN_STEP * CH


def _scan_block(nb):
    ncb = CTX // SCAN_ROWS

    def block(d, s):
        bwd = jnp.where(s < ncb, ncb - 1 - s, nb + ncb - 1 - s)
        return jnp.where(d == 0, s, bwd)
    return block


def hgrn_fwd(parts, iv, lb, mc, mtc, mrefc):
    t = parts.shape[0]
    nb = t // SCAN_ROWS
    block = _scan_block(nb)

    def body(q_ref, f_ref, i_ref, lb_ref, m_ref, mt_ref, mr_ref, o_ref, ck_ref, st):
        d = pl.program_id(0)

        @pl.when(pl.program_id(1) == 0)
        def _():
            st[...] = jnp.zeros_like(st)

        for c in range(SCAN_STEP):
            rows = pl.ds(pl.multiple_of(jnp.where(d == 0, c * CH, (SCAN_STEP - 1 - c) * CH), CH), CH)
            ck_ref[0, c] = st[...].astype(BF16)
            outs, news = _hgrn_chunk([st[h] for h in range(HEADS)], [q_ref[rows, _hsl(h)] for h in range(HEADS)],
                                     [f_ref[rows, _hsl(h)] for h in range(HEADS)], [i_ref[rows, _hsl(h)] for h in range(HEADS)],
                                     [lb_ref[0, :, _hsl(h)] for h in range(HEADS)], m_ref[0], mt_ref[0], mr_ref[0])
            for h in range(HEADS):
                o_ref[0, rows, _hsl(h)] = outs[h].astype(BF16)
                st[h] = news[h]

    const = lambda d, s: (d, 0, 0)
    at = lambda k: pl.BlockSpec((SCAN_ROWS, D), lambda d, s: (block(d, s), k(d)))
    return pl.pallas_call(
        body, name="hgrn_fwd", grid=(2, nb),
        in_specs=[at(lambda d: 0), at(lambda d: 1 + d), at(lambda d: 0), pl.BlockSpec((1, 1, D), const),
                  pl.BlockSpec((1, CH, CH), const), pl.BlockSpec((1, CH, CH), const), pl.BlockSpec((1, CH, 1), const)],
        out_specs=[pl.BlockSpec((1, SCAN_ROWS, D), lambda d, s: (d, block(d, s), 0)),
                   pl.BlockSpec((1, SCAN_STEP, HEADS, HD, HD), lambda d, s: (d, s, 0, 0, 0))],
        out_shape=[jax.ShapeDtypeStruct((2, t, D), BF16), jax.ShapeDtypeStruct((2, nb * SCAN_STEP, HEADS, HD, HD), BF16)],
        scratch_shapes=[pltpu.VMEM((HEADS, HD, HD), F32)], compiler_params=_cp(2))(parts, parts, iv, lb, mc, mtc, mrefc)


def _mixer_tile(rows, u_ref, v_ref, og_ref, o_ref, lnw_ref, lnb_ref, sw_ref, sb_ref, hnw_ref):
    n = (rows.stop - rows.start) // SGU_CH
    yas, vjps = [], []
    for c in range(n):
        r = slice(rows.start + c * SGU_CH, rows.start + (c + 1) * SGU_CH)
        ya, vjp_a = jax.vjp(_sgu_fn, [u_ref[r, _hsl(g)].astype(F32) for g in range(HEADS)],
                            [v_ref[r, _hsl(g)].astype(F32) for g in range(HEADS)],
                            [lnw_ref[:, _hsl(g)] for g in range(HEADS)], [lnb_ref[:, _hsl(g)] for g in range(HEADS)],
                            [sw_ref[g] for g in range(HEADS)], [sb_ref[g] for g in range(HEADS)])
        yas.append(ya)
        vjps.append(vjp_a)
    yb, vjp_b = jax.vjp(_readout_fn, [o_ref[0, rows, _hsl(h)].astype(F32) + o_ref[1, rows, _hsl(h)].astype(F32)
                                      for h in range(HEADS)],
                        og_ref[rows, :].astype(F32), hnw_ref[...])
    return (yas[0] if n == 1 else jnp.concatenate(yas, axis=0)), yb, vjps, vjp_b


def _part_specs(tm, first, n):
    return [pl.BlockSpec((tm, D), functools.partial(lambda k, i: (i, k), first + k)) for k in range(n)]


def _unless_ctx(skip_ctx, is_ctx, zero_refs, work):
    if not skip_ctx:
        return work()

    @pl.when(is_ctx)
    def _():
        for r in zero_refs:
            r[...] = jnp.zeros_like(r)

    pl.when(jnp.logical_not(is_ctx))(work)


def mixer_fwd(x, parts, o, mod, lnw, lnb, sw, sb, hnw, wa, wb, wo, skip_ctx):
    t = x.shape[0]

    def body(x_ref, u_ref, v_ref, og_ref, ga_ref, gb_ref, o_ref, mod_ref, lnw_ref, lnb_ref, sw_ref, sb_ref, hnw_ref,
             wa_ref, wb_ref, wo_ref, out_ref, pa_ref, pb_ref, y_ref, yat_ref, ybt_ref, mt_ref):
        def work():
            ya, yb, _, _ = _mixer_tile(slice(0, TM), u_ref, v_ref, og_ref, o_ref, lnw_ref, lnb_ref, sw_ref, sb_ref, hnw_ref)
            pa, pb = mm(ya, wa_ref[...]), mm(yb, wb_ref[...])
            merged = jax.nn.sigmoid(ga_ref[...].astype(F32)) * pa + jax.nn.sigmoid(gb_ref[...].astype(F32)) * pb
            y = mm(merged, wo_ref[...])
            out_ref[...] = x_ref[...] + mod_ref[0, 2] * y
            pa_ref[...], pb_ref[...], y_ref[...] = pa.astype(BF16), pb.astype(BF16), y.astype(BF16)
            yat_ref[...], ybt_ref[...], mt_ref[...] = ya.T.astype(BF16), yb.T.astype(BF16), merged.T.astype(BF16)

        _unless_ctx(skip_ctx, pl.program_id(0) == 0, (out_ref, pa_ref, pb_ref, y_ref, yat_ref, ybt_ref, mt_ref), work)

    vec = lambda n: pl.BlockSpec((1, n), lambda i: (0, 0))
    tile = pl.BlockSpec((TM, D), lambda i: (i, 0))
    tile_t = pl.BlockSpec((D, TM), lambda i: (0, i))
    return pl.pallas_call(
        body, name="mixer_fwd", grid=(t // TM,),
        in_specs=[tile] + _part_specs(TM, 0, 5)
        + [pl.BlockSpec((2, TM, D), lambda i: (0, i, 0)), pl.BlockSpec((1, 6, 1, D), _stream_row(TM)), vec(D), vec(D),
           VMEM_WHOLE, VMEM_WHOLE, vec(HD), VMEM_WHOLE, VMEM_WHOLE, VMEM_WHOLE],
        out_specs=[tile] * 4 + [tile_t] * 3,
        out_shape=[jax.ShapeDtypeStruct((t, D), F32)] + [jax.ShapeDtypeStruct((t, D), BF16)] * 3
        + [jax.ShapeDtypeStruct((D, t), BF16)] * 3, compiler_params=_cp(1),
    )(x, parts, parts, parts, parts, parts, o, mod, lnw, lnb, sw, sb, hnw, wa, wb, wo)


def ffn_up_fwd(x, mod, nw, wg, skip_ctx):
    t = x.shape[0]

    def body(x_ref, mod_ref, nw_ref, w_ref, out_ref, ht_ref):
        def work():
            h32 = _norm_mod(x_ref[...], nw_ref[...], mod_ref[0, 3], mod_ref[0, 4])
            ht_ref[...] = h32.T.astype(BF16)
            h = h32.astype(BF16)
            for j in range(N_DEV):
                out_ref[j] = jnp.dot(h, w_ref[j], preferred_element_type=F32)

        _unless_ctx(skip_ctx, pl.program_id(0) == 0, (out_ref, ht_ref), work)

    return pl.pallas_call(
        body, name="ffn_up_fwd", grid=(t // TM,),
        in_specs=[pl.BlockSpec((TM, D), lambda i: (i, 0)), pl.BlockSpec((1, 6, 1, D), _stream_row(TM)),
                  pl.BlockSpec((1, D), lambda i: (0, 0)), VMEM_WHOLE],
        out_specs=[pl.BlockSpec((N_DEV, TM, FF_SLOT), lambda i: (0, i, 0)), pl.BlockSpec((D, TM), lambda i: (0, i))],
        out_shape=[jax.ShapeDtypeStruct((N_DEV, t, FF_SLOT), F32), jax.ShapeDtypeStruct((D, t), BF16)],
        compiler_params=_cp(1))(x, mod, nw, wg)


def _halo_specs(nt):
    per = TM // GRID_W
    last = nt * per - 1
    return [pl.BlockSpec((N_FFK, GRID_W, FF_SLOT), lambda i: (0, jnp.maximum(i * per - 1, 0), 0)),
            pl.BlockSpec((N_FFK, TM, FF_SLOT), lambda i: (0, i, 0)),
            pl.BlockSpec((N_FFK, GRID_W, FF_SLOT), lambda i: (0, jnp.minimum(i * per + per, last), 0))]


def _with_halo(prev_ref, main_ref, next_ref, k, i, nt):
    prev = jnp.where(i >= 2, prev_ref[k], 0.0)
    nxt = jnp.where((i >= 1) & (i <= nt - 2), next_ref[k], 0.0)
    return jnp.concatenate([prev, main_ref[k], nxt], axis=0)


def _tap_valid(dc, i, n_rows, offset):
    r = lax.broadcasted_iota(jnp.int32, (n_rows, 1), 0) - offset
    col = jnp.bitwise_and(r, GRID_W - 1)
    pos = jnp.where(i == 0, r, col) + dc
    return (pos >= 0) & (pos < jnp.where(i == 0, TM, GRID_W))


def _row_weight(cw_ref, k, dr, dc, i):
    w = cw_ref[k, 3 * (dr + 1) + dc + 1:3 * (dr + 1) + dc + 2, :]
    return w if dr == 0 else jnp.where(i == 0, 0.0, w)


def ffn_down_fwd(x, av, mod, cw, cb, wd, skip_ctx):
    t = x.shape[0]
    nt = t // TM

    def body(x_ref, ap_ref, am_ref, an_ref, v_ref, mod_ref, cw_ref, cb_ref, wd_ref, out_ref, ac_ref, y_ref, z_ref):
        i = pl.program_id(0)

        def work():
            y = None
            for k in range(N_FFK):
                a_ext = _with_halo(ap_ref, am_ref, an_ref, k, i, nt)
                conv = jnp.zeros((TM, FF_SLOT), F32) + cb_ref[k]
                for dc in (-1, 0, 1):
                    col = functools.reduce(lambda p, q: p + q, [a_ext[GRID_W + GRID_W * dr:GRID_W + GRID_W * dr + TM]
                                                                * _row_weight(cw_ref, k, dr, dc, i) for dr in (-1, 0, 1)])
                    conv = conv + (col if dc == 0 else
                                   jnp.where(_tap_valid(dc, i, TM, 0), pltpu.roll(col, (-dc) % TM, 0), 0.0))
                ac_ref[k] = conv.astype(BF16)
                z = _glu_fn(conv, v_ref[k]).astype(BF16)
                z_ref[k] = z
                part = mm(z, wd_ref[k])
                y = part if y is None else y + part
            y_ref[...] = y
            out_ref[...] = x_ref[...] + mod_ref[0, 5] * y

        _unless_ctx(skip_ctx, i == 0, (out_ref, ac_ref, y_ref, z_ref), work)

    tile = pl.BlockSpec((TM, D), lambda i: (i, 0))
    half = lambda first: pl.BlockSpec((N_FFK, TM, FF_SLOT), lambda i: (first, i, 0))
    return pl.pallas_call(
        body, name="ffn_down_fwd", grid=(nt,),
        in_specs=[tile] + _halo_specs(nt) + [half(1), pl.BlockSpec((1, 6, 1, D), _stream_row(TM)), VMEM_WHOLE, VMEM_WHOLE,
                                             VMEM_WHOLE],
        out_specs=[tile, half(0), tile, half(0)],
        out_shape=[jax.ShapeDtypeStruct((t, D), F32), jax.ShapeDtypeStruct((N_FFK, t, FF_SLOT), BF16),
                   jax.ShapeDtypeStruct((t, D), F32), jax.ShapeDtypeStruct((N_FFK, t, FF_SLOT), BF16)],
        compiler_params=_cp(1))(x, av, av, av, av, mod, cw, cb, wd)


def loss_fwd_bwd(x, target, fw):
    t = x.shape[0]

    def body(x_ref, t_ref, w_ref, loss_ref, dx_ref, dw_ref):
        i = pl.program_id(0)

        @pl.when(i == 0)
        def _():
            loss_ref[...] = jnp.zeros_like(loss_ref)
            dw_ref[...] = jnp.zeros_like(dw_ref)
            dx_ref[...] = jnp.zeros_like(dx_ref)

        @pl.when(i > 0)
        def _():
            y, vjp = jax.vjp(_rms, x_ref[...], w_ref[...])
            err = y - t_ref[...]
            loss_ref[...] += 0.5 * jnp.sum(jnp.sum(err * err, axis=-1, keepdims=True) / D)
            dx, dw = vjp(err / D)
            dx_ref[...] = dx
            dw_ref[...] += dw

    return pl.pallas_call(
        body, name="loss_fwd_bwd", grid=(t // TM,),
        in_specs=[pl.BlockSpec((TM, D), lambda i: (i, 0)), pl.BlockSpec((TM, D), lambda i: (jnp.maximum(i - 1, 0), 0)),
                  pl.BlockSpec((1, D), lambda i: (0, 0))],
        out_specs=[pl.BlockSpec((8, 128), lambda i: (0, 0)), pl.BlockSpec((TM, D), lambda i: (i, 0)),
                   pl.BlockSpec((1, D), lambda i: (0, 0))],
        out_shape=[jax.ShapeDtypeStruct((8, 128), F32), jax.ShapeDtypeStruct((t, D), F32), jax.ShapeDtypeStruct((1, D), F32)],
        compiler_params=_cp(1))(x, target, fw)


def _stream_add(ref, k, is_ctx, val):
    ref[0, k] += jnp.where(is_ctx, val, 0.0)
    ref[1, k] += jnp.where(is_ctx, 0.0, val)


def ffn_down_bwd(dx, ac, av, y, mod, wd, skip_ctx):
    t = dx.shape[0]
    nt = t // TM

    def body(dx_ref, ac_ref, v_ref, y_ref, mod_ref, wd_ref, dav_ref, dac_ref, dout_ref, dg_ref):
        i = pl.program_id(0)

        @pl.when(i == 0)
        def _():
            dg_ref[...] = jnp.zeros_like(dg_ref)

        def work():
            _stream_add(dg_ref, 0, i == 0, jnp.sum(dx_ref[...] * y_ref[...], axis=0, keepdims=True))
            dout = (mod_ref[0, 5] * dx_ref[...]).astype(BF16)
            dout_ref[...] = dout
            for k in range(N_FFK):
                _, vjp = jax.vjp(_glu_fn, ac_ref[k].astype(F32), v_ref[k])
                dac, dv = vjp(mm_nt(dout, wd_ref[k]))
                dac_ref[k] = dac
                dav_ref[k] = dv.astype(BF16)

        _unless_ctx(skip_ctx, i == 0, (dav_ref, dac_ref, dout_ref), work)

    tile = pl.BlockSpec((TM, D), lambda i: (i, 0))
    half = lambda first: pl.BlockSpec((N_FFK, TM, FF_SLOT), lambda i: (first, i, 0))
    return pl.pallas_call(
        body, name="ffn_down_bwd", grid=(nt,),
        in_specs=[tile, half(0), half(1), tile, pl.BlockSpec((1, 6, 1, D), _stream_row(TM)), VMEM_WHOLE],
        out_specs=[half(1), half(0), tile, pl.BlockSpec((2, 1, 1, D), lambda i: (0, 0, 0, 0))],
        out_shape=[jax.ShapeDtypeStruct((N_DEV, t, FF_SLOT), BF16), jax.ShapeDtypeStruct((N_FFK, t, FF_SLOT), F32),
                   jax.ShapeDtypeStruct((t, D), BF16), jax.ShapeDtypeStruct((2, 1, 1, D), F32)],
        compiler_params=_cp(1))(dx, ac, av, y, mod, wd)


def conv_bwd(dav, dac, av, cw, skip_ctx):
    t = dac.shape[1]
    nt = t // TM

    def body(dav_in, gp_ref, gm_ref, gn_ref, ap_ref, am_ref, an_ref, cw_ref, dav_ref, dcw_ref, dcb_ref):
        i = pl.program_id(0)

        @pl.when(i == 0)
        def _():
            dcw_ref[...] = jnp.zeros_like(dcw_ref)
            dcb_ref[...] = jnp.zeros_like(dcb_ref)

        def work():
            for k in range(N_FFK):
                g_ext = _with_halo(gp_ref, gm_ref, gn_ref, k, i, nt)
                a_ext = _with_halo(ap_ref, am_ref, an_ref, k, i, nt)
                g_main = gm_ref[k]
                dcb_ref[k] += jnp.sum(g_main, axis=0, keepdims=True)
                da = jnp.zeros((TM, FF_SLOT), F32)
                for dc in (-1, 0, 1):
                    valid = _tap_valid(dc, i, TM, 0)
                    q = functools.reduce(lambda p, r: p + r, [g_ext[GRID_W - GRID_W * dr:GRID_W - GRID_W * dr + TM]
                                                              * _row_weight(cw_ref, k, dr, dc, i) for dr in (-1, 0, 1)])
                    da = da + (q if dc == 0 else pltpu.roll(jnp.where(valid, q, 0.0), dc % TM, 0))
                    g_shift = g_main if dc == 0 else pltpu.roll(jnp.where(valid, g_main, 0.0), dc % TM, 0)
                    for dr in (-1, 0, 1):
                        lo = GRID_W + GRID_W * dr
                        tap = 3 * (dr + 1) + dc + 1
                        dw = jnp.sum(g_shift * a_ext[lo:lo + TM], axis=0, keepdims=True)
                        dcw_ref[k, tap:tap + 1, :] += dw if dr == 0 else jnp.where(i == 0, 0.0, dw)
                dav_ref[k] = da.astype(BF16)

        _unless_ctx(skip_ctx, i == 0, (dav_ref,), work)

    whole = lambda rows: pl.BlockSpec((N_FFK, rows, FF_SLOT), lambda i: (0, 0, 0))
    return pl.pallas_call(
        body, name="conv_bwd", grid=(nt,),
        in_specs=[ANY] + _halo_specs(nt) + _halo_specs(nt) + [VMEM_WHOLE],
        out_specs=[pl.BlockSpec((N_FFK, TM, FF_SLOT), lambda i: (0, i, 0)), whole(9), whole(1)],
        out_shape=[jax.ShapeDtypeStruct(dav.shape, BF16), jax.ShapeDtypeStruct((N_FFK, 9, FF_SLOT), F32),
                   jax.ShapeDtypeStruct((N_FFK, 1, FF_SLOT), F32)],
        input_output_aliases={0: 0}, compiler_params=_cp(1))(dav, dac, dac, dac, av, av, av, cw)


def _norm_mod_bwd(x_ref, nw_ref, mod_ref, k_shift, dh, dx_in, dx_ref, dnw_ref, dmod_ref, is_ctx):
    _, vjp = jax.vjp(_norm_mod, x_ref[...], nw_ref[...], mod_ref[0, k_shift], mod_ref[0, k_shift + 1])
    dx, dnw, dshift, dscale = vjp(dh)
    dx_ref[...] = dx_in + dx
    dnw_ref[...] += dnw
    _stream_add(dmod_ref, 0, is_ctx, dshift)
    _stream_add(dmod_ref, 1, is_ctx, dscale)


def ffn_up_bwd_x(dx2, x, dav, mod, nw, wg, skip_ctx):
    t = x.shape[0]

    def body(dx2_ref, x_ref, dav_ref, mod_ref, nw_ref, w_ref, dx_ref, dnw_ref, dmod_ref):
        i = pl.program_id(0)

        @pl.when(i == 0)
        def _():
            dnw_ref[...] = jnp.zeros_like(dnw_ref)
            dmod_ref[...] = jnp.zeros_like(dmod_ref)

        def work():
            dh = mm_nt(dav_ref[0], w_ref[0])
            for j in range(1, N_DEV):
                dh = dh + mm_nt(dav_ref[j], w_ref[j])
            _norm_mod_bwd(x_ref, nw_ref, mod_ref, 3, dh, dx2_ref[...], dx_ref, dnw_ref, dmod_ref, i == 0)

        _unless_ctx(skip_ctx, i == 0, (dx_ref,), work)

    tile = pl.BlockSpec((TM, D), lambda i: (i, 0))
    return pl.pallas_call(
        body, name="ffn_up_bwd_x", grid=(t // TM,),
        in_specs=[tile, tile, pl.BlockSpec((N_DEV, TM, FF_SLOT), lambda i: (0, i, 0)), pl.BlockSpec((1, 6, 1, D), _stream_row(TM)),
                  pl.BlockSpec((1, D), lambda i: (0, 0)), VMEM_WHOLE],
        out_specs=[tile, pl.BlockSpec((1, D), lambda i: (0, 0)), pl.BlockSpec((2, 2, 1, D), lambda i: (0, 0, 0, 0))],
        out_shape=[jax.ShapeDtypeStruct((t, D), F32), jax.ShapeDtypeStruct((1, D), F32), jax.ShapeDtypeStruct((2, 2, 1, D), F32)],
        compiler_params=_cp(1))(dx2, x, dav, mod, nw, wg)


def weight_grad(at, dout, slot, name, after=None):
    rows, t = at.shape
    stacked = dout.ndim == 3
    n = dout.shape[0] if stacked else dout.shape[1] // slot

    def body(a_ref, d_ref, *rest):
        dw_ref = rest[-1]
        dw_ref[0] = jnp.dot(a_ref[...], d_ref[0] if stacked else d_ref[...], preferred_element_type=F32).astype(dw_ref.dtype)

    d_spec = pl.BlockSpec((1, t, slot), lambda j: (j, 0, 0)) if stacked else pl.BlockSpec((t, slot), lambda j: (0, j))
    extra = [] if after is None else [jnp.reshape(after, (1, 1))]
    return pl.pallas_call(
        body, name=name, grid=(n,), in_specs=[VMEM_WHOLE, d_spec] + [ANY] * len(extra),
        out_specs=pl.BlockSpec((1, rows, slot), lambda j: (j, 0, 0)),
        out_shape=jax.ShapeDtypeStruct((n, rows, slot), GRAD_WIRE), compiler_params=_cp(1))(at, dout, *extra)


def weight_grad_rows(at, dout, name):
    n, t, rows = at.shape
    cols = dout.shape[1]

    def body(a_ref, d_ref, dw_ref):
        dw_ref[0] = _dot(a_ref[0], d_ref[...], ((0,), (0,))).astype(dw_ref.dtype)

    return pl.pallas_call(
        body, name=name, grid=(n,), in_specs=[pl.BlockSpec((1, t, rows), lambda k: (k, 0, 0)), VMEM_WHOLE],
        out_specs=pl.BlockSpec((1, rows, cols), lambda k: (k, 0, 0)),
        out_shape=jax.ShapeDtypeStruct((n, rows, cols), GRAD_WIRE), compiler_params=_cp(1))(at, dout)


def mixer_bwd(dx, parts, o, pa, pb, y, mod, lnw, lnb, sw, sb, hnw, wa, wb, wo, skip_ctx):
    t = dx.shape[0]
    tm = TM
    n_ctx = CTX // tm

    def body(dx_ref, u_ref, v_ref, og_ref, ga_ref, gb_ref, o_ref, pa_ref, pb_ref, y_ref, mod_ref, lnw_ref, lnb_ref, sw_ref,
             sb_ref, hnw_ref, wa_ref, wb_ref, wo_ref, dp_ref, do_ref, dy_ref, dpa_ref, dpb_ref, dlnw_ref, dlnb_ref, dsw_ref,
             dsb_ref, dhnw_ref, dg_ref):
        i = pl.program_id(0)

        @pl.when(i == 0)
        def _():
            for r in (dlnw_ref, dlnb_ref, dsw_ref, dsb_ref, dhnw_ref, dg_ref):
                r[...] = jnp.zeros_like(r)

        def work():
            _, _, vjps, vjp_b = _mixer_tile(slice(0, tm), u_ref, v_ref, og_ref, o_ref, lnw_ref, lnb_ref, sw_ref, sb_ref, hnw_ref)
            pa, pb = pa_ref[...].astype(F32), pb_ref[...].astype(F32)
            sa, sbg = jax.nn.sigmoid(ga_ref[...].astype(F32)), jax.nn.sigmoid(gb_ref[...].astype(F32))
            dxv = dx_ref[...]
            _stream_add(dg_ref, 0, i < n_ctx, jnp.sum(dxv * y_ref[...].astype(F32), axis=0, keepdims=True))
            dy = (mod_ref[0, 2] * dxv).astype(BF16)
            dy_ref[...] = dy
            dmerged = mm_nt(dy, wo_ref[...])
            dpa, dpb = (sa * dmerged).astype(BF16), (sbg * dmerged).astype(BF16)
            dpa_ref[...], dpb_ref[...] = dpa, dpb
            first = 4 * D
            dp_ref[:, first + 3 * D:first + 4 * D] = (dmerged * pa * sa * (1.0 - sa)).astype(BF16)
            dp_ref[:, first + 4 * D:first + 5 * D] = (dmerged * pb * sbg * (1.0 - sbg)).astype(BF16)
            dya = mm_nt(dpa, wa_ref[...])
            dob, dog, dhnw = vjp_b(mm_nt(dpb, wb_ref[...]))
            dp_ref[:, first + 2 * D:first + 3 * D] = dog.astype(BF16)
            dhnw_ref[...] += dhnw
            for g in range(HEADS):
                do_ref[:, _hsl(g)] = dob[g]
            for c, vjp_a in enumerate(vjps):
                rows = slice(c * SGU_CH, (c + 1) * SGU_CH)
                dub, dvb, dlnw, dlnb, dsw, dsb = vjp_a(dya[rows])
                for g in range(HEADS):
                    dp_ref[rows, first + g * HD:first + (g + 1) * HD] = dub[g].astype(BF16)
                    dp_ref[rows, first + D + g * HD:first + D + (g + 1) * HD] = dvb[g].astype(BF16)
                    dlnw_ref[:, _hsl(g)] += dlnw[g]
                    dlnb_ref[:, _hsl(g)] += dlnb[g]
                    dsw_ref[g] += dsw[g]
                    dsb_ref[g] += dsb[g]

        _unless_ctx(skip_ctx, i < n_ctx, (dp_ref, do_ref, dy_ref, dpa_ref, dpb_ref), work)

    vec = lambda n: pl.BlockSpec((1, n), lambda i: (0, 0))
    tile = pl.BlockSpec((tm, D), lambda i: (i, 0))
    sds = jax.ShapeDtypeStruct
    return pl.pallas_call(
        body, name="mixer_bwd", grid=(t // tm,),
        in_specs=[tile] + _part_specs(tm, 0, 5)
        + [pl.BlockSpec((2, tm, D), lambda i: (0, i, 0)), tile, tile, tile, pl.BlockSpec((1, 6, 1, D), _stream_row(tm)),
           vec(D), vec(D), VMEM_WHOLE, VMEM_WHOLE, vec(HD), VMEM_WHOLE, VMEM_WHOLE, VMEM_WHOLE],
        out_specs=[pl.BlockSpec((tm, D_IN), lambda i: (i, 0)), tile, tile, tile, tile, vec(D), vec(D),
                   VMEM_WHOLE, VMEM_WHOLE, vec(HD), pl.BlockSpec((2, 1, 1, D), lambda i: (0, 0, 0, 0))],
        out_shape=[sds((t, D_IN), BF16), sds((t, D), F32), sds((t, D), BF16), sds((t, D), BF16), sds((t, D), BF16),
                   sds((1, D), F32), sds((1, D), F32), sds((HEADS, SGU_CH, SGU_CH), F32), sds((HEADS, SGU_CH, 1), F32),
                   sds((1, HD), F32), sds((2, 1, 1, D), F32)],
        compiler_params=_cp(1))(dx, parts, parts, parts, parts, parts, o, pa, pb, y, mod, lnw, lnb, sw, sb, hnw, wa, wb, wo)


def hgrn_bwd(d, parts, lb, mc, mtc, mrefc, ck, do, first=None, dparts=None):
    t = parts.shape[0]
    nb = t // SCAN_ROWS
    block = _scan_block(nb)
    rev = lambda s: block(d, nb - 1 - s)

    def body(q_ref, f_ref, i_ref, lb_ref, m_ref, mt_ref, mr_ref, ck_ref, do_ref, *rest):
        dst = rest[-1]
        dlb_ref = rest[-2]

        @pl.when(pl.program_id(0) == 0)
        def _():
            dst[...] = jnp.zeros_like(dst)
            dlb_ref[...] = jnp.zeros_like(dlb_ref)

        heads = range(HEADS)
        fn = functools.partial(_hgrn_chunk, m=m_ref[0], mt=mt_ref[0], mref=mr_ref[0])
        for c in reversed(range(SCAN_STEP)):
            first_row = c * CH if d == 0 else (SCAN_STEP - 1 - c) * CH
            rows = slice(first_row, first_row + CH)
            _, vjp = jax.vjp(fn, [ck_ref[0, c, h].astype(F32) for h in heads], [q_ref[rows, _hsl(h)] for h in heads],
                             [f_ref[rows, _hsl(h)] for h in heads], [i_ref[rows, _hsl(h)] for h in heads],
                             [lb_ref[0, :, _hsl(h)] for h in heads])
            dstl, dq, df, di, dlb = vjp(([do_ref[rows, _hsl(h)] for h in heads], [dst[h] for h in heads]))
            for h in heads:
                dst[h] = dstl[h]
                dlb_ref[0, :, _hsl(h)] += dlb[h]
                if d == 0:
                    dq_ref, df_ref, di_ref = rest[:3]
                    dq_ref[rows, _hsl(h)] = dq[h].astype(BF16)
                    df_ref[rows, _hsl(h)] = df[h].astype(BF16)
                    di_ref[rows, _hsl(h)] = di[h].astype(BF16)
                else:
                    dq0_ref, df0_ref, di0_ref, _, dp_ref = rest[:5]
                    col = lambda k: slice(k * D + h * HD, k * D + (h + 1) * HD)
                    dp_ref[rows, col(0)] = (dq0_ref[rows, _hsl(h)].astype(F32) + dq[h]).astype(BF16)
                    dp_ref[rows, col(1)] = df0_ref[rows, _hsl(h)]
                    dp_ref[rows, col(2)] = df[h].astype(BF16)
                    dp_ref[rows, col(3)] = (di0_ref[rows, _hsl(h)].astype(F32) + di[h]).astype(BF16)

    const = lambda s: (d, 0, 0)
    at = lambda k: pl.BlockSpec((SCAN_ROWS, D), lambda s: (rev(s), k))
    in_specs = [at(0), at(1 + d), at(3), pl.BlockSpec((1, 1, D), const), pl.BlockSpec((1, CH, CH), const),
                pl.BlockSpec((1, CH, CH), const), pl.BlockSpec((1, CH, 1), const),
                pl.BlockSpec((1, SCAN_STEP, HEADS, HD, HD), lambda s: (d, nb - 1 - s, 0, 0, 0)), at(0)]
    dlb_spec, dlb_shape = pl.BlockSpec((1, 1, D), lambda s: (0, 0, 0)), jax.ShapeDtypeStruct((1, 1, D), F32)
    common = dict(grid=(nb,), scratch_shapes=[pltpu.VMEM((HEADS, HD, HD), F32)], compiler_params=_cp(1))
    if d == 0:
        return pl.pallas_call(body, name="hgrn_bwd_fwd_dir", in_specs=in_specs, out_specs=[at(0)] * 3 + [dlb_spec],
                              out_shape=[jax.ShapeDtypeStruct((t, D), BF16)] * 3 + [dlb_shape], **common,
                              )(parts, parts, parts, lb, mc, mtc, mrefc, ck, do)
    return pl.pallas_call(body, name="hgrn_bwd_bwd_dir", in_specs=in_specs + [at(0)] * 3 + [ANY],
                          out_specs=[pl.BlockSpec((SCAN_ROWS, 4 * D), lambda s: (rev(s), 0)), dlb_spec],
                          out_shape=[jax.ShapeDtypeStruct(dparts.shape, BF16), dlb_shape], input_output_aliases={12: 0},
                          **common)(parts, parts, parts, lb, mc, mtc, mrefc, ck, do, *first, dparts)


def in_proj_bwd_x(dx1, x, dparts, mod, nw, wg, after=None, latent_only=False):
    t = x.shape[0]
    tm = TM
    n_ctx = CTX // tm

    def body(dx1_ref, x_ref, dp_ref, mod_ref, nw_ref, w_ref, *rest):
        dx_ref, dnw_ref, dmod_ref = rest[-3:]
        i = pl.program_id(0)

        @pl.when(i == 0)
        def _():
            dnw_ref[...] = jnp.zeros_like(dnw_ref)
            dmod_ref[...] = jnp.zeros_like(dmod_ref)

        dh = mm_nt(dp_ref[:, 0:IN_SLOT], w_ref[0])
        for j in range(1, N_DEV):
            dh = dh + mm_nt(dp_ref[:, j * IN_SLOT:(j + 1) * IN_SLOT], w_ref[j])
        _norm_mod_bwd(x_ref, nw_ref, mod_ref, 0, dh, dx1_ref[...], dx_ref, dnw_ref, dmod_ref, i < n_ctx)

    tile = pl.BlockSpec((tm, D), lambda i: (i, 0))
    extra = [] if after is None else [jnp.reshape(after, (1, 1))]
    return pl.pallas_call(
        body, name="in_proj_bwd_x", grid=(t // tm,),
        in_specs=[tile, tile, pl.BlockSpec((tm, D_IN), lambda i: (i, 0)), pl.BlockSpec((1, 6, 1, D), _stream_row(tm)),
                  pl.BlockSpec((1, D), lambda i: (0, 0)), VMEM_WHOLE] + [ANY] * len(extra),
        out_specs=[pl.BlockSpec((tm, D), lambda i: (jnp.maximum(i - n_ctx, 0), 0)) if latent_only else tile,
                   pl.BlockSpec((1, D), lambda i: (0, 0)), pl.BlockSpec((2, 2, 1, D), lambda i: (0, 0, 0, 0))],
        out_shape=[jax.ShapeDtypeStruct((t - CTX if latent_only else t, D), F32), jax.ShapeDtypeStruct((1, D), F32),
                   jax.ShapeDtypeStruct((2, 2, 1, D), F32)],
        compiler_params=_cp(1))(dx1, x, dparts, mod, nw, wg, *extra)


def _lb_fn(h0, h1):
    m = jnp.maximum(h0, h1)
    e0, e1 = jnp.exp(h0 - m), jnp.exp(h1 - m)
    return e1 / (e0 + e1)


def lower_bounds(hlb):
    def body(h_ref, out_ref):
        out_ref[...] = _lb_fn(h_ref[0:1, :], h_ref[1:2, :])
    return pl.pallas_call(body, name="lower_bounds", out_shape=jax.ShapeDtypeStruct((1, 2 * D), F32))(hlb)


def lower_bounds_bwd(hlb, dlb1):
    def body(h_ref, d_ref, out_ref):
        _, vjp = jax.vjp(_lb_fn, h_ref[0:1, :], h_ref[1:2, :])
        d0, d1 = vjp(d_ref[...])
        out_ref[0:1, :] = d0
        out_ref[1:2, :] = d1
    return pl.pallas_call(body, name="lower_bounds_bwd", out_shape=jax.ShapeDtypeStruct((2, 2 * D), F32))(hlb, dlb1)


def _ada_fn(c_all, cctx8, w, b):
    dot = lambda a, l: mm(_silu(a), w[l]) + b[l]
    return [dot(c_all, l) for l in range(2)], [dot(cctx8, l) for l in range(2)]


def ada_fwd(c_all, cctx8, w, b):
    cols = w.shape[-1]

    def body(c_ref, cc_ref, w_ref, b_ref, out_ref):
        ox, oc = _ada_fn(c_ref[...], cc_ref[...], [w_ref[0], w_ref[1]], [b_ref[0], b_ref[1]])
        for l in range(2):
            out_ref[l, 0] = ox[l]
            out_ref[l, 1] = oc[l]
    return pl.pallas_call(body, name="ada_fwd", out_shape=jax.ShapeDtypeStruct((2, 2, N_DEV, cols), F32),
                          compiler_params=_cp(0))(c_all, cctx8, w, b)


def ada_bwd(c_all, cctx8, w, b, dmx, dmc):
    cols = w.shape[-1]

    def body(c_ref, cc_ref, w_ref, b_ref, dmx_ref, dmc_ref, dw_ref, dc_ref):
        fn = lambda cc, w0, w1: _ada_fn(c_ref[...], cc, [w0, w1], [b_ref[0], b_ref[1]])
        _, vjp = jax.vjp(fn, cc_ref[...], w_ref[0], w_ref[1])
        dcc, dw0, dw1 = vjp(([dmx_ref[0], dmx_ref[1]], [dmc_ref[0], dmc_ref[1]]))
        dw_ref[0] = dw0
        dw_ref[1] = dw1
        dc_ref[...] = jnp.sum(dcc, axis=0, keepdims=True)
    return pl.pallas_call(body, name="ada_bwd", out_shape=[jax.ShapeDtypeStruct((2, D, cols), F32), jax.ShapeDtypeStruct((1, D), F32)],
                          compiler_params=_cp(0))(c_all, cctx8, w, b, dmx, dmc)


def adamw(w, m, v, gparts, name):
    r, c = w.shape
    p = gparts.shape[0]
    rt = r
    while rt % 16 == 0 and (p + 7) * rt * c * 4 * 2 > ELEMENTWISE_VMEM:
        rt //= 2

    def body(w_ref, m_ref, v_ref, g_ref, go_ref, d_ref, mo_ref, vo_ref):
        g = g_ref[0].astype(F32)
        for k in range(1, p):
            g = g + g_ref[k].astype(F32)
        m2 = ADAM_B1 * m_ref[...] + (1.0 - ADAM_B1) * g
        v2 = ADAM_B2 * v_ref[...] + (1.0 - ADAM_B2) * (g * g)
        m_hat = m2 / (1.0 - ADAM_B1 ** ADAM_STEP)
        v_hat = v2 / (1.0 - ADAM_B2 ** ADAM_STEP)
        go_ref[...] = g
        d_ref[...] = -ADAM_LR * (m_hat / (jnp.sqrt(v_hat) + ADAM_EPS) + ADAM_WD * w_ref[...])
        mo_ref[...] = m2
        vo_ref[...] = v2

    tile = pl.BlockSpec((rt, c), lambda i: (i, 0))
    return pl.pallas_call(
        body, name=name, grid=(r // rt,),
        in_specs=[tile, tile, tile, pl.BlockSpec((p, rt, c), lambda i: (0, i, 0))], out_specs=[tile] * 4,
        out_shape=[jax.ShapeDtypeStruct((r, c), F32)] * 4, compiler_params=_cp(1))(w, m, v, gparts)


def _me():
    x, y, c = lax.axis_index("x"), lax.axis_index("y"), lax.axis_index("c")
    return x, y, c, 4 * x + 2 * y + c


def _peer(x, y, c, p):
    fx, fy, fc = (p >> 2) & 1, (p >> 1) & 1, p & 1
    return (1 - x if fx else x, 1 - y if fy else y, 1 - c if fc else c)


def all_gather(arrs, name, after=None):
    n = len(arrs)
    extra = [] if after is None else list(after) if isinstance(after, (list, tuple)) else [after]

    def body(*refs):
        ins, outs = refs[:n], refs[n + len(extra):2 * n + len(extra)]
        send, recv, local = refs[2 * n + len(extra):]
        x, y, c, me = _me()
        copies = []
        for a in range(n):
            lc = pltpu.make_async_copy(ins[a], outs[a].at[me], local.at[a])
            lc.start()
            copies.append(lc)
            for p in range(1, N_DEV):
                cp = pltpu.make_async_remote_copy(src_ref=ins[a], dst_ref=outs[a].at[me], send_sem=send.at[a, p - 1],
                                                  recv_sem=recv.at[a, p - 1], device_id=_peer(x, y, c, p),
                                                  device_id_type=pl.DeviceIdType.MESH)
                cp.start()
                copies.append(cp)
        for cp in copies:
            cp.wait()

    return pl.pallas_call(
        body, name=name, in_specs=[ANY] * (n + len(extra)), out_specs=[ANY] * n,
        out_shape=[jax.ShapeDtypeStruct((N_DEV,) + a.shape, a.dtype) for a in arrs],
        scratch_shapes=[pltpu.SemaphoreType.DMA((n, N_DEV - 1)), pltpu.SemaphoreType.DMA((n, N_DEV - 1)),
                        pltpu.SemaphoreType.DMA((n,))])(*arrs, *extra)


HBM = pl.BlockSpec(memory_space=pltpu.HBM)
SEM = pl.BlockSpec(memory_space=pltpu.SEMAPHORE)


def _in_hbm(a):
    return pltpu.with_memory_space_constraint(a, pltpu.HBM)


ALL_PEERS = tuple(range(1, N_DEV))
SAME_CORE_AND_SIBLING = (1, 2, 4, 6)
OTHER_CHIPS = (2, 4, 6)


def _exchange_refs(srcs, lands, layer, scatter, a, x, y, c, p, forward=False):
    me = 4 * x + 2 * y + c
    px, py, pc = _peer(x, y, c, p) if p else (x, y, c)
    if forward and p:
        slot = lands[a].at[4 * px + 2 * py + pc]
        return slot, slot, _peer(x, y, c, 1)
    dst = lands[a].at[me] if layer is None else lands[a].at[me, layer]
    src = srcs[a].at[4 * px + 2 * py + pc] if scatter else dst
    return src, dst, (px, py, pc)


def exchange_start(srcs, lands, layer, scatter, name, after=None, peers=ALL_PEERS, forward=False):
    n, ns = len(lands), len(srcs)
    extra = [] if after is None else [after]

    def body(*refs):
        ins, lz = refs[:ns], refs[ns:ns + n]
        send, recv = refs[ns + n + len(extra)], refs[ns + n + len(extra) + 1]
        token = refs[-1]
        x, y, c, _ = _me()
        for a in range(n):
            for p in peers:
                src, dst, peer = _exchange_refs(ins, lz, layer, scatter, a, x, y, c, p, forward)
                k = a * (N_DEV - 1) + p - 1
                pltpu.make_async_remote_copy(src_ref=src, dst_ref=dst, send_sem=send.at[k], recv_sem=recv.at[k],
                                             device_id=peer, device_id_type=pl.DeviceIdType.MESH).start()
        token[...] = jnp.zeros_like(token)

    thru = [pltpu.HBM(a.shape, a.dtype) for a in list(srcs) + list(lands)]
    out = pl.pallas_call(
        body, name=name, in_specs=[HBM] * (ns + n) + [ANY] * len(extra),
        out_specs=[SEM, SEM] + [HBM] * (ns + n) + [pl.BlockSpec(memory_space=pltpu.VMEM)],
        out_shape=[pltpu.SemaphoreType.DMA((n * (N_DEV - 1),)), pltpu.SemaphoreType.DMA((n * (N_DEV - 1),))] + thru
        + [jax.ShapeDtypeStruct((8, 128), F32)],
        input_output_aliases={i: 2 + i for i in range(ns + n)},
        compiler_params=pltpu.CompilerParams(has_side_effects=pltpu.SideEffectType.DATAFLOW_SIDE_EFFECTING),
    )(*[_in_hbm(a) for a in list(srcs) + list(lands)], *extra)
    return out[0], out[1], out[2:2 + ns], out[2 + ns:2 + ns + n], out[-1]


def exchange_wait(send, recv, srcs, lands, layer, scatter, after, name, peers=ALL_PEERS):
    n, ns = len(lands), len(srcs)

    def body(*refs):
        ins, lz = refs[:ns], refs[ns:ns + n]
        send_ref, recv_ref = refs[ns + n], refs[ns + n + 1]
        x, y, c, _ = _me()
        for a in range(n):
            for p in peers:
                src, dst, peer = _exchange_refs(ins, lz, layer, scatter, a, x, y, c, 0)
                k = a * (N_DEV - 1) + p - 1
                cp = pltpu.make_async_remote_copy(src_ref=src, dst_ref=dst, send_sem=send_ref.at[k],
                                                  recv_sem=recv_ref.at[k], device_id=peer,
                                                  device_id_type=pl.DeviceIdType.MESH)
                cp.wait_send()
                cp.wait_recv()

    thru = [pltpu.HBM(a.shape, a.dtype) for a in list(srcs) + list(lands)]
    out = pl.pallas_call(
        body, name=name, in_specs=[HBM] * (ns + n) + [SEM, SEM, ANY], out_specs=[HBM] * (ns + n), out_shape=thru,
        input_output_aliases={i: i for i in range(ns + n)},
        compiler_params=pltpu.CompilerParams(has_side_effects=pltpu.SideEffectType.DATAFLOW_SIDE_EFFECTING),
    )(*srcs, *lands, send, recv, after)
    return out[ns:]


def place_own(src, land, me, layer, scatter, name, src_layer=None, after=None):
    create = isinstance(land, jax.ShapeDtypeStruct)
    r, c = src.shape[-2:]
    rt = r
    while rt % 32 == 0 and rt * c * 4 > COPY_BLOCK_BYTES:
        rt //= 2

    extra = [] if after is None else [after]

    def body(me_ref, src_ref, *rest):
        out_ref = rest[-1]
        out_ref[...] = src_ref[...].reshape(out_ref.shape).astype(out_ref.dtype)

    src_spec = (pl.BlockSpec((1, rt, c), lambda i, m: (m[0], i, 0)) if scatter else
                pl.BlockSpec((rt, c), lambda i, m: (i, 0)) if src_layer is None else
                pl.BlockSpec((1, rt, c), lambda i, m: (src_layer, i, 0)))
    out_spec = (pl.BlockSpec((1, rt, c), lambda i, m: (m[0], i, 0)) if layer is None
                else pl.BlockSpec((1, 1, rt, c), lambda i, m: (m[0], layer, i, 0)))
    grid_spec = pltpu.PrefetchScalarGridSpec(num_scalar_prefetch=1, grid=(r // rt,),
                                             in_specs=[src_spec] + ([] if create else [ANY]) + [ANY] * len(extra),
                                             out_specs=out_spec)
    return pl.pallas_call(body, name=name, grid_spec=grid_spec, out_shape=jax.ShapeDtypeStruct(land.shape, land.dtype),
                          input_output_aliases={} if create else {2: 0}, compiler_params=_cp(1),
                          )(*((me, src) if create else (me, src, land)), *extra)


def _scan_constants():
    r = lax.broadcasted_iota(jnp.int32, (CH, CH), 0)
    s = lax.broadcasted_iota(jnp.int32, (CH, CH), 1)
    lower = (s <= r).astype(F32)
    t = jnp.arange(CH)[:, None]
    mc = jnp.stack([lower, lower.T])
    mref = jnp.stack([(t <= CH // 2 - 1).astype(F32), (t >= CH // 2).astype(F32)])
    return mc, jnp.stack([lower.T, lower]), mref


def local_step(x, ctx, target, mod, lb, w, fetch=None, publish=None, small_ready=None, small_early=None):
    kept = {}

    def keep(l, part, grads):
        kept[(l, part)] = grads
        return 0.0

    fetch = fetch or (lambda l, part, after: w)
    publish = publish or keep
    n_layers = len(mod)
    mc, mtc, mrefc = _scan_constants()
    xs = jnp.concatenate([ctx, x], axis=0)
    saved, big = [], []
    for l in range(n_layers):
        wl = dict(fetch(l, "in", xs))
        parts, ht, iv, mix = in_proj_fwd(xs, mod[l], w["nw1"][l], wl["win"][l])
        o, ck = hgrn_fwd(parts, iv, lb[l], mc, mtc, mrefc)
        wl.update(fetch(l, "mix", o))
        last = l == n_layers - 1
        x1, pa, pb, ym, yat, ybt, mt = mixer_fwd(xs, mix, o, mod[l], w["lnw"][l], w["lnb"][l], w["sw"][l], w["sb"][l],
                                                 w["hnw"][l], wl["wa"][l], wl["wb"][l], wl["wo"][l], last)
        wl.update(fetch(l, "ffn", x1))
        av, h2t = ffn_up_fwd(x1, mod[l], w["nw2"][l], wl["wup"][l], last)
        x2, ac, y, z = ffn_down_fwd(x1, av, mod[l], w["cw"][l], w["cb"][l], wl["wd"][l], last)
        saved.append((xs, parts, mix, o, ck, x1, av, ac, y, z, ht, h2t, pa, pb, ym, yat, ybt, mt))
        big.append(wl)
        xs = x2
    loss, dx, dfw = loss_fwd_bwd(xs, target, w["fw"])
    g = {k: [None] * n_layers for k in ("nw1", "nw2", "lnw", "lnb", "sw", "sb", "hnw", "cw", "cb")}
    g["fw"] = dfw
    dmod, dlb = [None] * n_layers, [None] * n_layers
    tok = 0.0
    for l in reversed(range(n_layers)):
        x0, parts, mix, o, ck, x1, av, ac, y, z, ht, h2t, pa, pb, ym, yat, ybt, mt = saved[l]
        wl = big[l]
        last = l == n_layers - 1
        dav, dac, dout, dg2 = ffn_down_bwd(dx, ac, av, y, mod[l] + tok, wl["wd"][l], last)
        dwd = weight_grad_rows(z, dout, "ffn_down_bwd_w")
        dav, g["cw"][l], g["cb"][l] = conv_bwd(dav, dac, av, w["cw"][l], last)
        dx1, g["nw2"][l], dmod2 = ffn_up_bwd_x(dx, x1, dav, mod[l], w["nw2"][l], wl["wup"][l], last)
        dwup = weight_grad(h2t, dav, FF_SLOT, "ffn_up_bwd_w")
        tok = publish(l, "ffn", {"wd": dwd, "wup": dwup})
        (dparts, do, dy, dpa, dpb, g["lnw"][l], g["lnb"][l], g["sw"][l], g["sb"][l], g["hnw"][l],
         dg1) = mixer_bwd(dx1, mix, o, pa, pb, ym, mod[l] + tok, w["lnw"][l], w["lnb"][l], w["sw"][l], w["sb"][l],
                          w["hnw"][l], wl["wa"][l], wl["wb"][l], wl["wo"][l], last)
        tok = publish(l, "mix", {"wa": weight_grad(yat, dpa, D, "mixer_bwd_wa"), "wb": weight_grad(ybt, dpb, D, "mixer_bwd_wb"),
                                 "wo": weight_grad(mt, dy, D, "mixer_bwd_wo")})
        if l == 0 and small_early:
            dmod[0] = jnp.concatenate([jnp.zeros((2, 2, 1, D), F32), dg1, dmod2, dg2], axis=1)
            tok = tok + small_early(loss[0, 0], g, dmod, dlb)
        dq, df, di, dlb_f = hgrn_bwd(0, parts, lb[l] + tok, mc, mtc, mrefc, ck, do)
        dparts, dlb_b = hgrn_bwd(1, parts, lb[l], mc, mtc, mrefc, ck, do, (dq, df, di), dparts)
        dlb[l] = jnp.concatenate([dlb_f, dlb_b], axis=0)
        tok = publish(l, "in", {"win": weight_grad(ht, dparts, IN_SLOT, "in_proj_bwd_w")})
        dx, g["nw1"][l], dmod1 = in_proj_bwd_x(dx1, x0, dparts, mod[l], w["nw1"][l], wl["win"][l], after=tok,
                                               latent_only=l == 0)
        dmod[l] = jnp.concatenate([dmod1, dg1, dmod2, dg2], axis=1)
    done = small_ready(loss[0, 0], g, dmod, dlb) if small_ready else 0.0
    for (l, part), grads in kept.items():
        for k, v in grads.items():
            g.setdefault(k, [None] * n_layers)[l] = v
    return loss[0, 0], dx, g, dmod, dlb, done


ROW = 1024
REPLICATED = ("norm1_w", "sgu_ln_w", "sgu_ln_b", "sgu_w", "sgu_b", "hgrn_lower_bounds", "hgrn_norm_w", "norm2_w",
              "ffn_conv_b", "final_norm_w")
WEIGHT_ORDER = ("c_ctx", "ada_w", "ada_b", "norm1_w", "w_in", "sgu_ln_w", "sgu_ln_b", "sgu_w", "sgu_b", "hgrn_lower_bounds",
                "hgrn_norm_w", "w_branch_a", "w_branch_b", "w_out", "norm2_w", "ffn_w_up", "ffn_conv_w", "ffn_conv_b",
                "ffn_w_down", "final_norm_w")


def _rows_of(n):
    return -(-n // (8 * ROW)) * 8


def _pack(arrs, total_rows=None):
    parts = []
    for a in arrs:
        flat = a.reshape(-1).astype(F32)
        rows = _rows_of(flat.shape[0])
        parts.append(jnp.pad(flat, (0, rows * ROW - flat.shape[0])).reshape(rows, ROW))
    have = sum(p.shape[0] for p in parts)
    if total_rows is not None and total_rows > have:
        parts.append(jnp.zeros((total_rows - have, ROW), F32))
    return jnp.concatenate(parts, axis=0)


def _unpack(packed, shapes):
    lead = packed.shape[:-2]
    out, r0 = [], 0
    for s in shapes:
        n = math.prod(s)
        rows = _rows_of(n)
        out.append(packed[..., r0:r0 + rows, :].reshape(lead + (rows * ROW,))[..., :n].reshape(lead + tuple(s)))
        r0 += rows
    return out


def kernel(x, c, ctx, c_ctx, ada_w, ada_b, norm1_w, w_in, sgu_ln_w, sgu_ln_b, sgu_w, sgu_b, hgrn_lower_bounds, hgrn_norm_w, w_branch_a, w_branch_b, w_out, norm2_w, ffn_w_up, ffn_conv_w, ffn_conv_b, ffn_w_down, final_norm_w, loss_target, m_c_ctx, m_ada_w, m_ada_b, m_norm1_w, m_w_in, m_sgu_ln_w, m_sgu_ln_b, m_sgu_w, m_sgu_b, m_hgrn_lower_bounds, m_hgrn_norm_w, m_w_branch_a, m_w_branch_b, m_w_out, m_norm2_w, m_ffn_w_up, m_ffn_conv_w, m_ffn_conv_b, m_ffn_w_down, m_final_norm_w, v_c_ctx, v_ada_w, v_ada_b, v_norm1_w, v_w_in, v_sgu_ln_w, v_sgu_ln_b, v_sgu_w, v_sgu_b, v_hgrn_lower_bounds, v_hgrn_norm_w, v_w_branch_a, v_w_branch_b, v_w_out, v_norm2_w, v_ffn_w_up, v_ffn_conv_w, v_ffn_conv_b, v_ffn_w_down, v_final_norm_w):
    wts = dict(c_ctx=c_ctx, ada_w=ada_w, ada_b=ada_b, norm1_w=norm1_w, w_in=w_in, sgu_ln_w=sgu_ln_w, sgu_ln_b=sgu_ln_b,
               sgu_w=sgu_w, sgu_b=sgu_b, hgrn_lower_bounds=hgrn_lower_bounds, hgrn_norm_w=hgrn_norm_w, w_branch_a=w_branch_a,
               w_branch_b=w_branch_b, w_out=w_out, norm2_w=norm2_w, ffn_w_up=ffn_w_up, ffn_conv_w=ffn_conv_w,
               ffn_conv_b=ffn_conv_b, ffn_w_down=ffn_w_down, final_norm_w=final_norm_w)
    mom1 = dict(c_ctx=m_c_ctx, ada_w=m_ada_w, ada_b=m_ada_b, norm1_w=m_norm1_w, w_in=m_w_in, sgu_ln_w=m_sgu_ln_w,
                sgu_ln_b=m_sgu_ln_b, sgu_w=m_sgu_w, sgu_b=m_sgu_b, hgrn_lower_bounds=m_hgrn_lower_bounds,
                hgrn_norm_w=m_hgrn_norm_w, w_branch_a=m_w_branch_a, w_branch_b=m_w_branch_b, w_out=m_w_out, norm2_w=m_norm2_w,
                ffn_w_up=m_ffn_w_up, ffn_conv_w=m_ffn_conv_w, ffn_conv_b=m_ffn_conv_b, ffn_w_down=m_ffn_w_down,
                final_norm_w=m_final_norm_w)
    mom2 = dict(c_ctx=v_c_ctx, ada_w=v_ada_w, ada_b=v_ada_b, norm1_w=v_norm1_w, w_in=v_w_in, sgu_ln_w=v_sgu_ln_w,
                sgu_ln_b=v_sgu_ln_b, sgu_w=v_sgu_w, sgu_b=v_sgu_b, hgrn_lower_bounds=v_hgrn_lower_bounds,
                hgrn_norm_w=v_hgrn_norm_w, w_branch_a=v_w_branch_a, w_branch_b=v_w_branch_b, w_out=v_w_out, norm2_w=v_norm2_w,
                ffn_w_up=v_ffn_w_up, ffn_conv_w=v_ffn_conv_w, ffn_conv_b=v_ffn_conv_b, ffn_w_down=v_ffn_w_down,
                final_norm_w=v_final_norm_w)
    n_layers = w_in.shape[0]
    layers = range(n_layers)
    me = 4 * lax.axis_index("x") + 2 * lax.axis_index("y") + lax.axis_index("c")
    ada_cols = ada_w.shape[-1]

    big = ("w_in", "ffn_w_up", "w_branch_a", "w_branch_b", "w_out", "ffn_w_down")
    short = {"w_in": "win", "ffn_w_up": "wup", "w_branch_a": "wa", "w_branch_b": "wb", "w_out": "wo", "ffn_w_down": "wd"}
    me1 = me.reshape(1).astype(jnp.int32)
    mixer, ffn = ("w_branch_a", "w_branch_b", "w_out"), ("ffn_w_up", "ffn_w_down")
    groups = [[(k, l) for k in part] for l in layers for part in (("w_in",), mixer, ffn)]
    group_of = {(l, part): 3 * l + n for l in layers for n, part in enumerate(("in", "mix", "ffn"))}
    in_flight, started = [], 0.0

    def own_slots(n, after):
        return [place_own(wts[k], jax.ShapeDtypeStruct((N_DEV,) + wts[k].shape[1:], BF16), me1, None, False,
                          f"gather_own_{short[k]}_{l}", src_layer=l, after=after) for k, l in groups[n]]

    def start_group(n, lands, after):
        in_flight.append(exchange_start([], lands, None, False, f"gather_weights_start_{n}", after=after,
                                        peers=SAME_CORE_AND_SIBLING if n == 0 else ALL_PEERS))
        return in_flight[-1][-1]

    (c_all,) = all_gather([c], "gather_c")
    c_all = c_all.reshape(N_DEV, D)
    token = start_group(0, own_slots(0, c_all), c_all)
    later = [own_slots(n, token) for n in range(1, len(groups))]
    cctx8 = jnp.broadcast_to(c_ctx[None, :], (N_DEV, D))
    ada_b_cols = lax.dynamic_slice_in_dim(ada_b, me * ada_cols, ada_cols, axis=1)[:, None, :]
    mod_cols = ada_fwd(c_all, cctx8, ada_w, ada_b_cols)
    xs = jnp.concatenate([ctx[0], x[0]], axis=0)
    lb1 = lower_bounds(hgrn_lower_bounds)
    mod_all, conv_all = all_gather([mod_cols, ffn_conv_w.reshape(n_layers, 9, -1)], "gather_mod_conv",
                                   after=[token, xs, lb1] + [a for lands in later for a in lands])
    conv_full = [conv_all[:, l].transpose(1, 0, 2).reshape(9, N_FFK, FF_SLOT).transpose(1, 0, 2) for l in layers]
    for n in range(1, len(groups)):
        token = start_group(n, later[n - 1], mod_all if n == 1 else token)
    for started_group in in_flight:
        started = started + started_group[-1][0, 0]

    def as_used(k, a):
        return a if k in ("w_in", "ffn_w_up") else a.reshape(N_FFK, FF_SLOT, D) if k == "ffn_w_down" else a.reshape(D, D)

    arrived = {}

    def fetch(l, part, after):
        n = group_of[(l, part)]
        send, recv, _, lands, _ = in_flight[n]
        first = n == 0
        got = exchange_wait(send, recv, [], lands, None, False, after, f"gather_weights_wait_{n}",
                            peers=SAME_CORE_AND_SIBLING if first else ALL_PEERS)
        if first:
            send, recv, _, lands, _ = exchange_start([], got, None, False, "gather_weights_pass_on", peers=OTHER_CHIPS,
                                                     forward=True)
            got = exchange_wait(send, recv, [], lands, None, False, after, "gather_weights_passed_on", peers=OTHER_CHIPS)
        for (k, ll), a in zip(groups[n], got):
            arrived.setdefault(short[k], [None] * n_layers)[ll] = as_used(k, a)
        return arrived

    mod_x = lax.dynamic_index_in_dim(mod_all[:, :, 0], me, axis=2, keepdims=False)
    mod_c = mod_all[:, :, 1, 0]
    mod = [jnp.stack([mod_c[:, l].reshape(6, 1, D), mod_x[:, l].reshape(6, 1, D)]) for l in layers]
    mod[0] = mod[0] + started

    lb = [jnp.zeros((2, 1, D), F32), lb1.reshape(2, 1, D)]

    w = {
        "nw1": [norm1_w[l][None] for l in layers], "nw2": [norm2_w[l][None] for l in layers],
        "lnw": [sgu_ln_w[l][None] for l in layers], "lnb": [sgu_ln_b[l][None] for l in layers],
        "sw": [sgu_w[l] for l in layers], "sb": [sgu_b[l][:, :, None] for l in layers],
        "hnw": [hgrn_norm_w[l][None] for l in layers], "cw": conv_full,
        "cb": [ffn_conv_b[l].reshape(N_FFK, 1, FF_SLOT) for l in layers], "fw": final_norm_w[None],
    }
    long = {v: k for k, v in short.items()}
    landing, sent = {}, []

    def publish(l, part, grads):
        keys = [long[k] for k in grads]
        slots = [a.reshape((N_DEV, -1, a.shape[-1])) for a in grads.values()]
        zones = [place_own(s, landing.get(k, jax.ShapeDtypeStruct((N_DEV, n_layers) + s.shape[1:], s.dtype)), me1, l, True,
                           f"scatter_own_{short[k]}_{l}") for k, s in zip(keys, slots)]
        send, recv, srcs, zones, token = exchange_start(slots, zones, l, True, f"scatter_grads_start_{part}_{l}")
        landing.update(zip(keys, zones))
        sent.append((keys, l, part, send, recv, srcs, token))
        return token[0, 0]

    out = {}
    flat2 = lambda a: a.reshape(-1, a.shape[-1])

    def finish(part, after):
        done = []
        for keys, l, p, send, recv, srcs, _ in sent:
            if p == part:
                zones = exchange_wait(send, recv, srcs, [landing[k] for k in keys], l, True, after,
                                      f"scatter_grads_wait_{part}_{l}")
                landing.update(zip(keys, zones))
                done = keys
        for k in done:
            r = landing[k]
            res = adamw(flat2(wts[k]), flat2(mom1[k]), flat2(mom2[k]), r.reshape(N_DEV, -1, r.shape[-1]), "adamw_" + k)
            out[k] = tuple(a.reshape(wts[k].shape) for a in res)

    rep_rows = -(-sum(_rows_of(wts[k].size) for k in REPLICATED) // 64) * 64
    conv_rows = _rows_of(n_layers * 9 * D_FF)
    dmod_rows = _rows_of(n_layers * 6 * D)
    early = {}

    def small_early(loss_part, g, dmod, dlb):
        d_hlb = lower_bounds_bwd(hgrn_lower_bounds, dlb[1].reshape(1, 2 * D))
        st = lambda k: jnp.stack([jnp.zeros((1, D), F32) if a is None else a for a in g[k]])
        rep_grads = {"norm1_w": st("nw1"), "sgu_ln_w": st("lnw"), "sgu_ln_b": st("lnb"), "sgu_w": st("sw"), "sgu_b": st("sb"),
                     "hgrn_lower_bounds": d_hlb, "hgrn_norm_w": st("hnw"), "norm2_w": st("nw2"), "ffn_conv_b": st("cb"),
                     "final_norm_w": g["fw"]}
        d_conv = jnp.stack([g["cw"][l].transpose(1, 0, 2).reshape(9, D_FF) for l in layers])
        dmod_x = jnp.stack([dmod[l][1].reshape(6 * D) for l in layers])
        dmod_c = jnp.stack([dmod[l][0].reshape(6 * D) for l in layers])
        small = jnp.concatenate([_pack([rep_grads[k] for k in REPLICATED], rep_rows),
                                 _pack([d_conv, dmod_x, dmod_c, loss_part.reshape(1)])], axis=0)
        zone = place_own(small, jax.ShapeDtypeStruct((N_DEV,) + small.shape, F32), me1, None, False, "gather_small_own")
        early["send"], early["recv"], _, early["zones"], token = exchange_start([], [zone], None, False, "gather_small_start")
        return token[0, 0]

    def small_ready(loss_part, g, dmod, dlb):
        late = _pack([g["nw1"][0], dmod[0][1, 0:2], dmod[0][0, 0:2]])
        for part in ("ffn", "mix"):
            finish(part, late)
        (late_all,) = all_gather([late], "gather_small_late", after=[out[k][0] for k in big[1:]])
        (small_all,) = exchange_wait(early["send"], early["recv"], [], early["zones"], None, False, late_all,
                                     "gather_small_wait")
        at_x = rep_rows + conv_rows
        small_all = small_all.at[:, 0:1].set(late_all[:, 0:1])
        small_all = small_all.at[:, at_x:at_x + 2].set(late_all[:, 8:10])
        small_all = small_all.at[:, at_x + dmod_rows:at_x + dmod_rows + 2].set(late_all[:, 16:18])
        d_conv_shape, dmod_shape = (n_layers, 9, D_FF), (n_layers, 6 * D)
        conv_g, dmx_all, dmc_all, loss_all = _unpack(small_all[:, rep_rows:], [d_conv_shape, dmod_shape, dmod_shape, (1,)])
        out["loss"] = functools.reduce(lambda a, b: a + b, [loss_all[k, 0] for k in range(N_DEV)])

        rep = adamw(_pack([wts[k] for k in REPLICATED], rep_rows), _pack([mom1[k] for k in REPLICATED], rep_rows),
                    _pack([mom2[k] for k in REPLICATED], rep_rows), small_all, "adamw_replicated")
        rep = [_unpack(r, [wts[k].shape for k in REPLICATED]) for r in rep]
        for n, k in enumerate(REPLICATED):
            out[k] = tuple(r[n] for r in rep)

        conv_mine = lax.dynamic_index_in_dim(conv_g.reshape(N_DEV, n_layers, 9, N_DEV, -1), me, axis=3, keepdims=False)
        res = adamw(flat2(ffn_conv_w), flat2(m_ffn_conv_w), flat2(v_ffn_conv_w),
                    conv_mine.reshape(N_DEV, -1, conv_mine.shape[-1]), "adamw_conv_w")
        out["ffn_conv_w"] = tuple(r.reshape(ffn_conv_w.shape) for r in res)

        out["ada_b"] = tuple(adamw(ada_b, m_ada_b, v_ada_b, jnp.concatenate([dmx_all, dmc_all], axis=0), "adamw_ada_b"))

        cols_of = lambda a: lax.dynamic_slice_in_dim(a, me * ada_cols, ada_cols, axis=2).transpose(1, 0, 2)
        d_ada_w, d_cctx = ada_bwd(c_all, cctx8, ada_w, ada_b_cols, cols_of(dmx_all), cols_of(dmc_all))
        res = adamw(flat2(ada_w), flat2(m_ada_w), flat2(v_ada_w), flat2(d_ada_w)[None], "adamw_ada_w")
        out["ada_w"] = tuple(r.reshape(ada_w.shape) for r in res)
        (d_cctx_all,) = all_gather([d_cctx], "gather_c_ctx_grad")
        res = adamw(c_ctx[None], m_c_ctx[None], v_c_ctx[None], d_cctx_all, "adamw_c_ctx")
        out["c_ctx"] = tuple(r[0] for r in res)
        return d_cctx_all

    _, grad_x, _, _, _, small_done = local_step(x[0], ctx[0], loss_target[0], mod, lb, w, fetch, publish, small_ready,
                                                small_early)
    loss = out["loss"]

    finish("in", small_done)
    return (loss, grad_x[None]) + tuple(out[k][n] for n in range(4) for k in WEIGHT_ORDER)
```

```python
import functools
import math

import jax
import jax.numpy as jnp
from jax import lax
from jax.experimental import pallas as pl
from jax.experimental.pallas import tpu as pltpu

F32 = jnp.float32
BF16 = jnp.bfloat16

N_DEV = 8
AXES = ("x", "y", "c")
D = 1024
CTX = 256
TM = 256
CH = 64
SGU_CH = 128
HEADS = 8
HD = 128
GRID_W = 64
D_IN = 9 * D
IN_SLOT = D_IN // N_DEV
D_FF = 2816
FF_SLOT = 2 * D_FF // N_DEV
N_FFK = D_FF // FF_SLOT
RMS_EPS = 1e-6
LN_EPS = 1e-5
ADAM_LR, ADAM_B1, ADAM_B2, ADAM_EPS, ADAM_WD, ADAM_STEP = 0.001, 0.9, 0.999, 1e-08, 0.01, 10
VMEM_LIMIT_V7X = 56 * 2 ** 20
ELEMENTWISE_VMEM = 24 * 2 ** 20
COPY_BLOCK_BYTES = 2 ** 21
GRAD_WIRE = jnp.bfloat16

VMEM_WHOLE = pl.BlockSpec(memory_space=pltpu.VMEM)
ANY = pl.BlockSpec(memory_space=pl.ANY)


def _cp(n_axes):
    return pltpu.CompilerParams(dimension_semantics=("arbitrary",) * n_axes, vmem_limit_bytes=VMEM_LIMIT_V7X)


def _dot(a, b, dims):
    return lax.dot_general(a.astype(BF16), b.astype(BF16), (dims, ((), ())), preferred_element_type=F32)


@jax.custom_vjp
def mm(a, b):
    return _dot(a, b, ((1,), (0,)))


mm.defvjp(lambda a, b: (mm(a, b), (a, b)),
          lambda r, g: (_dot(g, r[1], ((1,), (1,))).astype(r[0].dtype), _dot(r[0], g, ((0,), (0,))).astype(r[1].dtype)))


@jax.custom_vjp
def mm_nt(a, b):
    return _dot(a, b, ((1,), (1,)))


mm_nt.defvjp(lambda a, b: (mm_nt(a, b), (a, b)),
             lambda r, g: (_dot(g, r[1], ((1,), (0,))).astype(r[0].dtype), _dot(g, r[0], ((0,), (0,))).astype(r[1].dtype)))


@jax.custom_vjp
def mm_tn(a, b):
    return _dot(a, b, ((0,), (0,)))


mm_tn.defvjp(lambda a, b: (mm_tn(a, b), (a, b)),
             lambda r, g: (_dot(r[1], g, ((1,), (1,))).astype(r[0].dtype), _dot(r[0], g, ((1,), (0,))).astype(r[1].dtype)))


def _tri_dot(m, g):
    hi = g.astype(BF16)
    low = (g - hi.astype(F32)).astype(BF16)
    n = g.shape[1]
    out = jnp.dot(m.astype(BF16), jnp.concatenate([hi, low], axis=1), preferred_element_type=F32)
    return out[:, :n] + out[:, n:]


@jax.custom_vjp
def _cum(m, mt, g):
    return _tri_dot(m, g)


_cum.defvjp(lambda m, mt, g: (_cum(m, mt, g), (m, mt)),
            lambda r, d: (jnp.zeros_like(r[0]), jnp.zeros_like(r[1]), _tri_dot(r[1], d)))


def _silu(x):
    return x * jax.nn.sigmoid(x)


def _gelu(x):
    return 0.5 * x * (1.0 + jnp.tanh(math.sqrt(2.0 / math.pi) * (x + 0.044715 * (x * x * x))))


def _rms(x, w):
    return x * lax.rsqrt(jnp.mean(x * x, axis=-1, keepdims=True) + RMS_EPS) * w


def _norm_mod(x, w, shift, scale):
    return _rms(x, w) * (1.0 + scale) + shift


def _hsl(h):
    return slice(h * HD, (h + 1) * HD)


def _hgrn_chunk(st, qz, fz, iv, lb, m, mt, mref):
    hs = range(HEADS)
    keep = [1.0 - lb[h] for h in hs]
    sg = [jax.nn.sigmoid(fz[h]) for h in hs]
    g = [jnp.log(lb[h] + keep[h] * sg[h]) for h in hs]
    k = [keep[h] * (1.0 - sg[h]) for h in hs]
    q = [_silu(qz[h]) for h in hs]
    b = [_cum(m, mt, g[h]) for h in hs]
    ref = [jnp.sum(mref * g[h], axis=0, keepdims=True) for h in hs]
    last = [jnp.sum(g[h], axis=0, keepdims=True) for h in hs]
    qa = [q[h] * jnp.exp(b[h] - ref[h]) for h in hs]
    ka = [k[h] * jnp.exp(ref[h] - b[h]) for h in hs]
    scores = [jnp.where(m > 0.5, mm_nt(qa[h], ka[h]), 0.0) for h in hs]
    inter = [mm_nt(qa[h] * jnp.exp(ref[h]), st[h]) for h in hs]
    kv = [mm_tn(iv[h], ka[h] * jnp.exp(last[h] - ref[h])) for h in hs]
    outs = [mm(scores[h], iv[h]) + inter[h] for h in hs]
    news = [jnp.exp(last[h]) * st[h] + kv[h] for h in hs]
    return outs, news


def _sgu_fn(ub, vb, lnw, lnb, sw, sb):
    gv = [_gelu(v) for v in vb]
    mu = sum(jnp.sum(t, axis=-1, keepdims=True) for t in gv) / D
    var = sum(jnp.sum((t - mu) * (t - mu), axis=-1, keepdims=True) for t in gv) / D
    inv = lax.rsqrt(var + LN_EPS)
    cols = []
    for g in range(HEADS):
        vn = (gv[g] - mu) * inv * lnw[g] + lnb[g]
        cols.append(_gelu(ub[g]) * (mm(sw[g], vn) + sb[g]))
    return jnp.concatenate(cols, axis=1)


def _readout_fn(ob, og, hnw):
    r = [o * lax.rsqrt(jnp.mean(o * o, axis=-1, keepdims=True) + RMS_EPS) * hnw for o in ob]
    return jnp.concatenate(r, axis=1) * _silu(og)


def _glu_fn(ac, v):
    return _gelu(ac) * v


def _stream_row(tm):
    n_ctx = CTX // tm
    return lambda i: (jnp.where(i < n_ctx, 0, 1), 0, 0, 0)


def in_proj_fwd(x, mod, nw, wg):
    t = x.shape[0]

    def body(x_ref, mod_ref, nw_ref, w_ref, out_ref, ht_ref, iv_ref, mix_ref, wfull, sems):
        @pl.when(pl.program_id(0) == 0)
        def _():
            copies = [pltpu.make_async_copy(w_ref.at[j], wfull.at[:, j * IN_SLOT:(j + 1) * IN_SLOT], sems.at[j])
                      for j in range(N_DEV)]
            for c in copies:
                c.start()
            for c in copies:
                c.wait()

        h32 = _norm_mod(x_ref[...], nw_ref[...], mod_ref[0, 0], mod_ref[0, 1])
        ht_ref[...] = h32.T.astype(BF16)
        h = h32.astype(BF16)
        for k in range(D_IN // D):
            part = jnp.dot(h, wfull[:, k * D:(k + 1) * D], preferred_element_type=F32)
            if k < 4:
                out_ref[:, k * D:(k + 1) * D] = part
            if k == 3:
                iv_ref[...] = part.astype(BF16)
            if k >= 4:
                mix_ref[:, (k - 4) * D:(k - 3) * D] = part.astype(BF16)

    return pl.pallas_call(
        body, name="in_proj_fwd", grid=(t // TM,),
        in_specs=[pl.BlockSpec((TM, D), lambda i: (i, 0)), pl.BlockSpec((1, 6, 1, D), _stream_row(TM)),
                  pl.BlockSpec((1, D), lambda i: (0, 0)), ANY],
        out_specs=[pl.BlockSpec((TM, 4 * D), lambda i: (i, 0)), pl.BlockSpec((D, TM), lambda i: (0, i)),
                   pl.BlockSpec((TM, D), lambda i: (i, 0)), pl.BlockSpec((TM, 5 * D), lambda i: (i, 0))],
        out_shape=[jax.ShapeDtypeStruct((t, 4 * D), F32), jax.ShapeDtypeStruct((D, t), BF16), jax.ShapeDtypeStruct((t, D), BF16),
                   jax.ShapeDtypeStruct((t, 5 * D), BF16)],
        scratch_shapes=[pltpu.VMEM((D, D_IN), BF16), pltpu.SemaphoreType.DMA((N_DEV,))], compiler_params=_cp(1))(x, mod, nw, wg)


SCAN_STEP = 4
SCAN_ROWS = SCAN_STEP * CH


def _scan_block(nb):
    ncb = CTX // SCAN_ROWS

    def block(d, s):
        bwd = jnp.where(s < ncb, ncb - 1 - s, nb + ncb - 1 - s)
        return jnp.where(d == 0, s, bwd)
    return block


def hgrn_fwd(parts, iv, lb, mc, mtc, mrefc):
    t = parts.shape[0]
    nb = t // SCAN_ROWS
    block = _scan_block(nb)

    def body(q_ref, f_ref, i_ref, lb_ref, m_ref, mt_ref, mr_ref, o_ref, ck_ref, st):
        d = pl.program_id(0)

        @pl.when(pl.program_id(1) == 0)
        def _():
            st[...] = jnp.zeros_like(st)

        for c in range(SCAN_STEP):
            rows = pl.ds(pl.multiple_of(jnp.where(d == 0, c * CH, (SCAN_STEP - 1 - c) * CH), CH), CH)
            ck_ref[0, c] = st[...].astype(BF16)
            outs, news = _hgrn_chunk([st[h] for h in range(HEADS)], [q_ref[rows, _hsl(h)] for h in range(HEADS)],
                                     [f_ref[rows, _hsl(h)] for h in range(HEADS)], [i_ref[rows, _hsl(h)] for h in range(HEADS)],
                                     [lb_ref[0, :, _hsl(h)] for h in range(HEADS)], m_ref[0], mt_ref[0], mr_ref[0])
            for h in range(HEADS):
                o_ref[0, rows, _hsl(h)] = outs[h].astype(BF16)
                st[h] = news[h]

    const = lambda d, s: (d, 0, 0)
    at = lambda k: pl.BlockSpec((SCAN_ROWS, D), lambda d, s: (block(d, s), k(d)))
    return pl.pallas_call(
        body, name="hgrn_fwd", grid=(2, nb),
        in_specs=[at(lambda d: 0), at(lambda d: 1 + d), at(lambda d: 0), pl.BlockSpec((1, 1, D), const),
                  pl.BlockSpec((1, CH, CH), const), pl.BlockSpec((1, CH, CH), const), pl.BlockSpec((1, CH, 1), const)],
        out_specs=[pl.BlockSpec((1, SCAN_ROWS, D), lambda d, s: (d, block(d, s), 0)),
                   pl.BlockSpec((1, SCAN_STEP, HEADS, HD, HD), lambda d, s: (d, s, 0, 0, 0))],
        out_shape=[jax.ShapeDtypeStruct((2, t, D), BF16), jax.ShapeDtypeStruct((2, nb * SCAN_STEP, HEADS, HD, HD), BF16)],
        scratch_shapes=[pltpu.VMEM((HEADS, HD, HD), F32)], compiler_params=_cp(2))(parts, parts, iv, lb, mc, mtc, mrefc)


def _mixer_tile(rows, u_ref, v_ref, og_ref, o_ref, lnw_ref, lnb_ref, sw_ref, sb_ref, hnw_ref):
    n = (rows.stop - rows.start) // SGU_CH
    yas, vjps = [], []
    for c in range(n):
        r = slice(rows.start + c * SGU_CH, rows.start + (c + 1) * SGU_CH)
        ya, vjp_a = jax.vjp(_sgu_fn, [u_ref[r, _hsl(g)].astype(F32) for g in range(HEADS)],
                            [v_ref[r, _hsl(g)].astype(F32) for g in range(HEADS)],
                            [lnw_ref[:, _hsl(g)] for g in range(HEADS)], [lnb_ref[:, _hsl(g)] for g in range(HEADS)],
                            [sw_ref[g] for g in range(HEADS)], [sb_ref[g] for g in range(HEADS)])
        yas.append(ya)
        vjps.append(vjp_a)
    yb, vjp_b = jax.vjp(_readout_fn, [o_ref[0, rows, _hsl(h)].astype(F32) + o_ref[1, rows, _hsl(h)].astype(F32)
                                      for h in range(HEADS)],
                        og_ref[rows, :].astype(F32), hnw_ref[...])
    return (yas[0] if n == 1 else jnp.concatenate(yas, axis=0)), yb, vjps, vjp_b


def _part_specs(tm, first, n):
    return [pl.BlockSpec((tm, D), functools.partial(lambda k, i: (i, k), first + k)) for k in range(n)]


def _unless_ctx(skip_ctx, is_ctx, zero_refs, work):
    if not skip_ctx:
        return work()

    @pl.when(is_ctx)
    def _():
        for r in zero_refs:
            r[...] = jnp.zeros_like(r)

    pl.when(jnp.logical_not(is_ctx))(work)


def mixer_fwd(x, parts, o, mod, lnw, lnb, sw, sb, hnw, wa, wb, wo, skip_ctx):
    t = x.shape[0]

    def body(x_ref, u_ref, v_ref, og_ref, ga_ref, gb_ref, o_ref, mod_ref, lnw_ref, lnb_ref, sw_ref, sb_ref, hnw_ref,
             wa_ref, wb_ref, wo_ref, out_ref, pa_ref, pb_ref, y_ref, yat_ref, ybt_ref, mt_ref):
        def work():
            ya, yb, _, _ = _mixer_tile(slice(0, TM), u_ref, v_ref, og_ref, o_ref, lnw_ref, lnb_ref, sw_ref, sb_ref, hnw_ref)
            pa, pb = mm(ya, wa_ref[...]), mm(yb, wb_ref[...])
            merged = jax.nn.sigmoid(ga_ref[...].astype(F32)) * pa + jax.nn.sigmoid(gb_ref[...].astype(F32)) * pb
            y = mm(merged, wo_ref[...])
            out_ref[...] = x_ref[...] + mod_ref[0, 2] * y
            pa_ref[...], pb_ref[...], y_ref[...] = pa.astype(BF16), pb.astype(BF16), y.astype(BF16)
            yat_ref[...], ybt_ref[...], mt_ref[...] = ya.T.astype(BF16), yb.T.astype(BF16), merged.T.astype(BF16)

        _unless_ctx(skip_ctx, pl.program_id(0) == 0, (out_ref, pa_ref, pb_ref, y_ref, yat_ref, ybt_ref, mt_ref), work)

    vec = lambda n: pl.BlockSpec((1, n), lambda i: (0, 0))
    tile = pl.BlockSpec((TM, D), lambda i: (i, 0))
    tile_t = pl.BlockSpec((D, TM), lambda i: (0, i))
    return pl.pallas_call(
        body, name="mixer_fwd", grid=(t // TM,),
        in_specs=[tile] + _part_specs(TM, 0, 5)
        + [pl.BlockSpec((2, TM, D), lambda i: (0, i, 0)), pl.BlockSpec((1, 6, 1, D), _stream_row(TM)), vec(D), vec(D),
           VMEM_WHOLE, VMEM_WHOLE, vec(HD), VMEM_WHOLE, VMEM_WHOLE, VMEM_WHOLE],
        out_specs=[tile] * 4 + [tile_t] * 3,
        out_shape=[jax.ShapeDtypeStruct((t, D), F32)] + [jax.ShapeDtypeStruct((t, D), BF16)] * 3
        + [jax.ShapeDtypeStruct((D, t), BF16)] * 3, compiler_params=_cp(1),
    )(x, parts, parts, parts, parts, parts, o, mod, lnw, lnb, sw, sb, hnw, wa, wb, wo)


def ffn_up_fwd(x, mod, nw, wg, skip_ctx):
    t = x.shape[0]

    def body(x_ref, mod_ref, nw_ref, w_ref, out_ref, ht_ref):
        def work():
            h32 = _norm_mod(x_ref[...], nw_ref[...], mod_ref[0, 3], mod_ref[0, 4])
            ht_ref[...] = h32.T.astype(BF16)
            h = h32.astype(BF16)
            for j in range(N_DEV):
                out_ref[j] = jnp.dot(h, w_ref[j], preferred_element_type=F32)

        _unless_ctx(skip_ctx, pl.program_id(0) == 0, (out_ref, ht_ref), work)

    return pl.pallas_call(
        body, name="ffn_up_fwd", grid=(t // TM,),
        in_specs=[pl.BlockSpec((TM, D), lambda i: (i, 0)), pl.BlockSpec((1, 6, 1, D), _stream_row(TM)),
                  pl.BlockSpec((1, D), lambda i: (0, 0)), VMEM_WHOLE],
        out_specs=[pl.BlockSpec((N_DEV, TM, FF_SLOT), lambda i: (0, i, 0)), pl.BlockSpec((D, TM), lambda i: (0, i))],
        out_shape=[jax.ShapeDtypeStruct((N_DEV, t, FF_SLOT), F32), jax.ShapeDtypeStruct((D, t), BF16)],
        compiler_params=_cp(1))(x, mod, nw, wg)


def _halo_specs(nt):
    per = TM // GRID_W
    last = nt * per - 1
    return [pl.BlockSpec((N_FFK, GRID_W, FF_SLOT), lambda i: (0, jnp.maximum(i * per - 1, 0), 0)),
            pl.BlockSpec((N_FFK, TM, FF_SLOT), lambda i: (0, i, 0)),
            pl.BlockSpec((N_FFK, GRID_W, FF_SLOT), lambda i: (0, jnp.minimum(i * per + per, last), 0))]


def _with_halo(prev_ref, main_ref, next_ref, k, i, nt):
    prev = jnp.where(i >= 2, prev_ref[k], 0.0)
    nxt = jnp.where((i >= 1) & (i <= nt - 2), next_ref[k], 0.0)
    return jnp.concatenate([prev, main_ref[k], nxt], axis=0)


def _tap_valid(dc, i, n_rows, offset):
    r = lax.broadcasted_iota(jnp.int32, (n_rows, 1), 0) - offset
    col = jnp.bitwise_and(r, GRID_W - 1)
    pos = jnp.where(i == 0, r, col) + dc
    return (pos >= 0) & (pos < jnp.where(i == 0, TM, GRID_W))


def _row_weight(cw_ref, k, dr, dc, i):
    w = cw_ref[k, 3 * (dr + 1) + dc + 1:3 * (dr + 1) + dc + 2, :]
    return w if dr == 0 else jnp.where(i == 0, 0.0, w)


def ffn_down_fwd(x, av, mod, cw, cb, wd, skip_ctx):
    t = x.shape[0]
    nt = t // TM

    def body(x_ref, ap_ref, am_ref, an_ref, v_ref, mod_ref, cw_ref, cb_ref, wd_ref, out_ref, ac_ref, y_ref, z_ref):
        i = pl.program_id(0)

        def work():
            y = None
            for k in range(N_FFK):
                a_ext = _with_halo(ap_ref, am_ref, an_ref, k, i, nt)
                conv = jnp.zeros((TM, FF_SLOT), F32) + cb_ref[k]
                for dc in (-1, 0, 1):
                    col = functools.reduce(lambda p, q: p + q, [a_ext[GRID_W + GRID_W * dr:GRID_W + GRID_W * dr + TM]
                                                                * _row_weight(cw_ref, k, dr, dc, i) for dr in (-1, 0, 1)])
                    conv = conv + (col if dc == 0 else
                                   jnp.where(_tap_valid(dc, i, TM, 0), pltpu.roll(col, (-dc) % TM, 0), 0.0))
                ac_ref[k] = conv.astype(BF16)
                z = _glu_fn(conv, v_ref[k]).astype(BF16)
                z_ref[k] = z
                part = mm(z, wd_ref[k])
                y = part if y is None else y + part
            y_ref[...] = y
            out_ref[...] = x_ref[...] + mod_ref[0, 5] * y

        _unless_ctx(skip_ctx, i == 0, (out_ref, ac_ref, y_ref, z_ref), work)

    tile = pl.BlockSpec((TM, D), lambda i: (i, 0))
    half = lambda first: pl.BlockSpec((N_FFK, TM, FF_SLOT), lambda i: (first, i, 0))
    return pl.pallas_call(
        body, name="ffn_down_fwd", grid=(nt,),
        in_specs=[tile] + _halo_specs(nt) + [half(1), pl.BlockSpec((1, 6, 1, D), _stream_row(TM)), VMEM_WHOLE, VMEM_WHOLE,
                                             VMEM_WHOLE],
        out_specs=[tile, half(0), tile, half(0)],
        out_shape=[jax.ShapeDtypeStruct((t, D), F32), jax.ShapeDtypeStruct((N_FFK, t, FF_SLOT), BF16),
                   jax.ShapeDtypeStruct((t, D), F32), jax.ShapeDtypeStruct((N_FFK, t, FF_SLOT), BF16)],
        compiler_params=_cp(1))(x, av, av, av, av, mod, cw, cb, wd)


def loss_fwd_bwd(x, target, fw):
    t = x.shape[0]

    def body(x_ref, t_ref, w_ref, loss_ref, dx_ref, dw_ref):
        i = pl.program_id(0)

        @pl.when(i == 0)
        def _():
            loss_ref[...] = jnp.zeros_like(loss_ref)
            dw_ref[...] = jnp.zeros_like(dw_ref)
            dx_ref[...] = jnp.zeros_like(dx_ref)

        @pl.when(i > 0)
        def _():
            y, vjp = jax.vjp(_rms, x_ref[...], w_ref[...])
            err = y - t_ref[...]
            loss_ref[...] += 0.5 * jnp.sum(jnp.sum(err * err, axis=-1, keepdims=True) / D)
            dx, dw = vjp(err / D)
            dx_ref[...] = dx
            dw_ref[...] += dw

    return pl.pallas_call(
        body, name="loss_fwd_bwd", grid=(t // TM,),
        in_specs=[pl.BlockSpec((TM, D), lambda i: (i, 0)), pl.BlockSpec((TM, D), lambda i: (jnp.maximum(i - 1, 0), 0)),
                  pl.BlockSpec((1, D), lambda i: (0, 0))],
        out_specs=[pl.BlockSpec((8, 128), lambda i: (0, 0)), pl.BlockSpec((TM, D), lambda i: (i, 0)),
                   pl.BlockSpec((1, D), lambda i: (0, 0))],
        out_shape=[jax.ShapeDtypeStruct((8, 128), F32), jax.ShapeDtypeStruct((t, D), F32), jax.ShapeDtypeStruct((1, D), F32)],
        compiler_params=_cp(1))(x, target, fw)


def _stream_add(ref, k, is_ctx, val):
    ref[0, k] += jnp.where(is_ctx, val, 0.0)
    ref[1, k] += jnp.where(is_ctx, 0.0, val)


def ffn_down_bwd(dx, ac, av, y, mod, wd, skip_ctx):
    t = dx.shape[0]
    nt = t // TM

    def body(dx_ref, ac_ref, v_ref, y_ref, mod_ref, wd_ref, dav_ref, dac_ref, dout_ref, dg_ref):
        i = pl.program_id(0)

        @pl.when(i == 0)
        def _():
            dg_ref[...] = jnp.zeros_like(dg_ref)

        def work():
            _stream_add(dg_ref, 0, i == 0, jnp.sum(dx_ref[...] * y_ref[...], axis=0, keepdims=True))
            dout = (mod_ref[0, 5] * dx_ref[...]).astype(BF16)
            dout_ref[...] = dout
            for k in range(N_FFK):
                _, vjp = jax.vjp(_glu_fn, ac_ref[k].astype(F32), v_ref[k])
                dac, dv = vjp(mm_nt(dout, wd_ref[k]))
                dac_ref[k] = dac
                dav_ref[k] = dv.astype(BF16)

        _unless_ctx(skip_ctx, i == 0, (dav_ref, dac_ref, dout_ref), work)

    tile = pl.BlockSpec((TM, D), lambda i: (i, 0))
    half = lambda first: pl.BlockSpec((N_FFK, TM, FF_SLOT), lambda i: (first, i, 0))
    return pl.pallas_call(
        body, name="ffn_down_bwd", grid=(nt,),
        in_specs=[tile, half(0), half(1), tile, pl.BlockSpec((1, 6, 1, D), _stream_row(TM)), VMEM_WHOLE],
        out_specs=[half(1), half(0), tile, pl.BlockSpec((2, 1, 1, D), lambda i: (0, 0, 0, 0))],
        out_shape=[jax.ShapeDtypeStruct((N_DEV, t, FF_SLOT), BF16), jax.ShapeDtypeStruct((N_FFK, t, FF_SLOT), F32),
                   jax.ShapeDtypeStruct((t, D), BF16), jax.ShapeDtypeStruct((2, 1, 1, D), F32)],
        compiler_params=_cp(1))(dx, ac, av, y, mod, wd)


def conv_bwd(dav, dac, av, cw, skip_ctx):
    t = dac.shape[1]
    nt = t // TM

    def body(dav_in, gp_ref, gm_ref, gn_ref, ap_ref, am_ref, an_ref, cw_ref, dav_ref, dcw_ref, dcb_ref):
        i = pl.program_id(0)

        @pl.when(i == 0)
        def _():
            dcw_ref[...] = jnp.zeros_like(dcw_ref)
            dcb_ref[...] = jnp.zeros_like(dcb_ref)

        def work():
            for k in range(N_FFK):
                g_ext = _with_halo(gp_ref, gm_ref, gn_ref, k, i, nt)
                a_ext = _with_halo(ap_ref, am_ref, an_ref, k, i, nt)
                g_main = gm_ref[k]
                dcb_ref[k] += jnp.sum(g_main, axis=0, keepdims=True)
                da = jnp.zeros((TM, FF_SLOT), F32)
                for dc in (-1, 0, 1):
                    valid = _tap_valid(dc, i, TM, 0)
                    q = functools.reduce(lambda p, r: p + r, [g_ext[GRID_W - GRID_W * dr:GRID_W - GRID_W * dr + TM]
                                                              * _row_weight(cw_ref, k, dr, dc, i) for dr in (-1, 0, 1)])
                    da = da + (q if dc == 0 else pltpu.roll(jnp.where(valid, q, 0.0), dc % TM, 0))
                    g_shift = g_main if dc == 0 else pltpu.roll(jnp.where(valid, g_main, 0.0), dc % TM, 0)
                    for dr in (-1, 0, 1):
                        lo = GRID_W + GRID_W * dr
                        tap = 3 * (dr + 1) + dc + 1
                        dw = jnp.sum(g_shift * a_ext[lo:lo + TM], axis=0, keepdims=True)
                        dcw_ref[k, tap:tap + 1, :] += dw if dr == 0 else jnp.where(i == 0, 0.0, dw)
                dav_ref[k] = da.astype(BF16)

        _unless_ctx(skip_ctx, i == 0, (dav_ref,), work)

    whole = lambda rows: pl.BlockSpec((N_FFK, rows, FF_SLOT), lambda i: (0, 0, 0))
    return pl.pallas_call(
        body, name="conv_bwd", grid=(nt,),
        in_specs=[ANY] + _halo_specs(nt) + _halo_specs(nt) + [VMEM_WHOLE],
        out_specs=[pl.BlockSpec((N_FFK, TM, FF_SLOT), lambda i: (0, i, 0)), whole(9), whole(1)],
        out_shape=[jax.ShapeDtypeStruct(dav.shape, BF16), jax.ShapeDtypeStruct((N_FFK, 9, FF_SLOT), F32),
                   jax.ShapeDtypeStruct((N_FFK, 1, FF_SLOT), F32)],
        input_output_aliases={0: 0}, compiler_params=_cp(1))(dav, dac, dac, dac, av, av, av, cw)


def _norm_mod_bwd(x_ref, nw_ref, mod_ref, k_shift, dh, dx_in, dx_ref, dnw_ref, dmod_ref, is_ctx):
    _, vjp = jax.vjp(_norm_mod, x_ref[...], nw_ref[...], mod_ref[0, k_shift], mod_ref[0, k_shift + 1])
    dx, dnw, dshift, dscale = vjp(dh)
    dx_ref[...] = dx_in + dx
    dnw_ref[...] += dnw
    _stream_add(dmod_ref, 0, is_ctx, dshift)
    _stream_add(dmod_ref, 1, is_ctx, dscale)


def ffn_up_bwd_x(dx2, x, dav, mod, nw, wg, skip_ctx):
    t = x.shape[0]

    def body(dx2_ref, x_ref, dav_ref, mod_ref, nw_ref, w_ref, dx_ref, dnw_ref, dmod_ref):
        i = pl.program_id(0)

        @pl.when(i == 0)
        def _():
            dnw_ref[...] = jnp.zeros_like(dnw_ref)
            dmod_ref[...] = jnp.zeros_like(dmod_ref)

        def work():
            dh = mm_nt(dav_ref[0], w_ref[0])
            for j in range(1, N_DEV):
                dh = dh + mm_nt(dav_ref[j], w_ref[j])
            _norm_mod_bwd(x_ref, nw_ref, mod_ref, 3, dh, dx2_ref[...], dx_ref, dnw_ref, dmod_ref, i == 0)

        _unless_ctx(skip_ctx, i == 0, (dx_ref,), work)

    tile = pl.BlockSpec((TM, D), lambda i: (i, 0))
    return pl.pallas_call(
        body, name="ffn_up_bwd_x", grid=(t // TM,),
        in_specs=[tile, tile, pl.BlockSpec((N_DEV, TM, FF_SLOT), lambda i: (0, i, 0)), pl.BlockSpec((1, 6, 1, D), _stream_row(TM)),
                  pl.BlockSpec((1, D), lambda i: (0, 0)), VMEM_WHOLE],
        out_specs=[tile, pl.BlockSpec((1, D), lambda i: (0, 0)), pl.BlockSpec((2, 2, 1, D), lambda i: (0, 0, 0, 0))],
        out_shape=[jax.ShapeDtypeStruct((t, D), F32), jax.ShapeDtypeStruct((1, D), F32), jax.ShapeDtypeStruct((2, 2, 1, D), F32)],
        compiler_params=_cp(1))(dx2, x, dav, mod, nw, wg)


def weight_grad(at, dout, slot, name, after=None):
    rows, t = at.shape
    stacked = dout.ndim == 3
    n = dout.shape[0] if stacked else dout.shape[1] // slot

    def body(a_ref, d_ref, *rest):
        dw_ref = rest[-1]
        dw_ref[0] = jnp.dot(a_ref[...], d_ref[0] if stacked else d_ref[...], preferred_element_type=F32).astype(dw_ref.dtype)

    d_spec = pl.BlockSpec((1, t, slot), lambda j: (j, 0, 0)) if stacked else pl.BlockSpec((t, slot), lambda j: (0, j))
    extra = [] if after is None else [jnp.reshape(after, (1, 1))]
    return pl.pallas_call(
        body, name=name, grid=(n,), in_specs=[VMEM_WHOLE, d_spec] + [ANY] * len(extra),
        out_specs=pl.BlockSpec((1, rows, slot), lambda j: (j, 0, 0)),
        out_shape=jax.ShapeDtypeStruct((n, rows, slot), GRAD_WIRE), compiler_params=_cp(1))(at, dout, *extra)


def weight_grad_rows(at, dout, name):
    n, t, rows = at.shape
    cols = dout.shape[1]

    def body(a_ref, d_ref, dw_ref):
        dw_ref[0] = _dot(a_ref[0], d_ref[...], ((0,), (0,))).astype(dw_ref.dtype)

    return pl.pallas_call(
        body, name=name, grid=(n,), in_specs=[pl.BlockSpec((1, t, rows), lambda k: (k, 0, 0)), VMEM_WHOLE],
        out_specs=pl.BlockSpec((1, rows, cols), lambda k: (k, 0, 0)),
        out_shape=jax.ShapeDtypeStruct((n, rows, cols), GRAD_WIRE), compiler_params=_cp(1))(at, dout)


def mixer_bwd(dx, parts, o, pa, pb, y, mod, lnw, lnb, sw, sb, hnw, wa, wb, wo, skip_ctx):
    t = dx.shape[0]
    tm = TM
    n_ctx = CTX // tm

    def body(dx_ref, u_ref, v_ref, og_ref, ga_ref, gb_ref, o_ref, pa_ref, pb_ref, y_ref, mod_ref, lnw_ref, lnb_ref, sw_ref,
             sb_ref, hnw_ref, wa_ref, wb_ref, wo_ref, dp_ref, do_ref, dy_ref, dpa_ref, dpb_ref, dlnw_ref, dlnb_ref, dsw_ref,
             dsb_ref, dhnw_ref, dg_ref):
        i = pl.program_id(0)

        @pl.when(i == 0)
        def _():
            for r in (dlnw_ref, dlnb_ref, dsw_ref, dsb_ref, dhnw_ref, dg_ref):
                r[...] = jnp.zeros_like(r)

        def work():
            _, _, vjps, vjp_b = _mixer_tile(slice(0, tm), u_ref, v_ref, og_ref, o_ref, lnw_ref, lnb_ref, sw_ref, sb_ref, hnw_ref)
            pa, pb = pa_ref[...].astype(F32), pb_ref[...].astype(F32)
            sa, sbg = jax.nn.sigmoid(ga_ref[...].astype(F32)), jax.nn.sigmoid(gb_ref[...].astype(F32))
            dxv = dx_ref[...]
            _stream_add(dg_ref, 0, i < n_ctx, jnp.sum(dxv * y_ref[...].astype(F32), axis=0, keepdims=True))
            dy = (mod_ref[0, 2] * dxv).astype(BF16)
            dy_ref[...] = dy
            dmerged = mm_nt(dy, wo_ref[...])
            dpa, dpb = (sa * dmerged).astype(BF16), (sbg * dmerged).astype(BF16)
            dpa_ref[...], dpb_ref[...] = dpa, dpb
            first = 4 * D
            dp_ref[:, first + 3 * D:first + 4 * D] = (dmerged * pa * sa * (1.0 - sa)).astype(BF16)
            dp_ref[:, first + 4 * D:first + 5 * D] = (dmerged * pb * sbg * (1.0 - sbg)).astype(BF16)
            dya = mm_nt(dpa, wa_ref[...])
            dob, dog, dhnw = vjp_b(mm_nt(dpb, wb_ref[...]))
            dp_ref[:, first + 2 * D:first + 3 * D] = dog.astype(BF16)
            dhnw_ref[...] += dhnw
            for g in range(HEADS):
                do_ref[:, _hsl(g)] = dob[g]
            for c, vjp_a in enumerate(vjps):
                rows = slice(c * SGU_CH, (c + 1) * SGU_CH)
                dub, dvb, dlnw, dlnb, dsw, dsb = vjp_a(dya[rows])
                for g in range(HEADS):
                    dp_ref[rows, first + g * HD:first + (g + 1) * HD] = dub[g].astype(BF16)
                    dp_ref[rows, first + D + g * HD:first + D + (g + 1) * HD] = dvb[g].astype(BF16)
                    dlnw_ref[:, _hsl(g)] += dlnw[g]
                    dlnb_ref[:, _hsl(g)] += dlnb[g]
                    dsw_ref[g] += dsw[g]
                    dsb_ref[g] += dsb[g]

        _unless_ctx(skip_ctx, i < n_ctx, (dp_ref, do_ref, dy_ref, dpa_ref, dpb_ref), work)

    vec = lambda n: pl.BlockSpec((1, n), lambda i: (0, 0))
    tile = pl.BlockSpec((tm, D), lambda i: (i, 0))
    sds = jax.ShapeDtypeStruct
    return pl.pallas_call(
        body, name="mixer_bwd", grid=(t // tm,),
        in_specs=[tile] + _part_specs(tm, 0, 5)
        + [pl.BlockSpec((2, tm, D), lambda i: (0, i, 0)), tile, tile, tile, pl.BlockSpec((1, 6, 1, D), _stream_row(tm)),
           vec(D), vec(D), VMEM_WHOLE, VMEM_WHOLE, vec(HD), VMEM_WHOLE, VMEM_WHOLE, VMEM_WHOLE],
        out_specs=[pl.BlockSpec((tm, D_IN), lambda i: (i, 0)), tile, tile, tile, tile, vec(D), vec(D),
                   VMEM_WHOLE, VMEM_WHOLE, vec(HD), pl.BlockSpec((2, 1, 1, D), lambda i: (0, 0, 0, 0))],
        out_shape=[sds((t, D_IN), BF16), sds((t, D), F32), sds((t, D), BF16), sds((t, D), BF16), sds((t, D), BF16),
                   sds((1, D), F32), sds((1, D), F32), sds((HEADS, SGU_CH, SGU_CH), F32), sds((HEADS, SGU_CH, 1), F32),
                   sds((1, HD), F32), sds((2, 1, 1, D), F32)],
        compiler_params=_cp(1))(dx, parts, parts, parts, parts, parts, o, pa, pb, y, mod, lnw, lnb, sw, sb, hnw, wa, wb, wo)


def hgrn_bwd(d, parts, lb, mc, mtc, mrefc, ck, do, first=None, dparts=None):
    t = parts.shape[0]
    nb = t // SCAN_ROWS
    block = _scan_block(nb)
    rev = lambda s: block(d, nb - 1 - s)

    def body(q_ref, f_ref, i_ref, lb_ref, m_ref, mt_ref, mr_ref, ck_ref, do_ref, *rest):
        dst = rest[-1]
        dlb_ref = rest[-2]

        @pl.when(pl.program_id(0) == 0)
        def _():
            dst[...] = jnp.zeros_like(dst)
            dlb_ref[...] = jnp.zeros_like(dlb_ref)

        heads = range(HEADS)
        fn = functools.partial(_hgrn_chunk, m=m_ref[0], mt=mt_ref[0], mref=mr_ref[0])
        for c in reversed(range(SCAN_STEP)):
            first_row = c * CH if d == 0 else (SCAN_STEP - 1 - c) * CH
            rows = slice(first_row, first_row + CH)
            _, vjp = jax.vjp(fn, [ck_ref[0, c, h].astype(F32) for h in heads], [q_ref[rows, _hsl(h)] for h in heads],
                             [f_ref[rows, _hsl(h)] for h in heads], [i_ref[rows, _hsl(h)] for h in heads],
                             [lb_ref[0, :, _hsl(h)] for h in heads])
            dstl, dq, df, di, dlb = vjp(([do_ref[rows, _hsl(h)] for h in heads], [dst[h] for h in heads]))
            for h in heads:
                dst[h] = dstl[h]
                dlb_ref[0, :, _hsl(h)] += dlb[h]
                if d == 0:
                    dq_ref, df_ref, di_ref = rest[:3]
                    dq_ref[rows, _hsl(h)] = dq[h].astype(BF16)
                    df_ref[rows, _hsl(h)] = df[h].astype(BF16)
                    di_ref[rows, _hsl(h)] = di[h].astype(BF16)
                else:
                    dq0_ref, df0_ref, di0_ref, _, dp_ref = rest[:5]
                    col = lambda k: slice(k * D + h * HD, k * D + (h + 1) * HD)
                    dp_ref[rows, col(0)] = (dq0_ref[rows, _hsl(h)].astype(F32) + dq[h]).astype(BF16)
                    dp_ref[rows, col(1)] = df0_ref[rows, _hsl(h)]
                    dp_ref[rows, col(2)] = df[h].astype(BF16)
                    dp_ref[rows, col(3)] = (di0_ref[rows, _hsl(h)].astype(F32) + di[h]).astype(BF16)

    const = lambda s: (d, 0, 0)
    at = lambda k: pl.BlockSpec((SCAN_ROWS, D), lambda s: (rev(s), k))
    in_specs = [at(0), at(1 + d), at(3), pl.BlockSpec((1, 1, D), const), pl.BlockSpec((1, CH, CH), const),
                pl.BlockSpec((1, CH, CH), const), pl.BlockSpec((1, CH, 1), const),
                pl.BlockSpec((1, SCAN_STEP, HEADS, HD, HD), lambda s: (d, nb - 1 - s, 0, 0, 0)), at(0)]
    dlb_spec, dlb_shape = pl.BlockSpec((1, 1, D), lambda s: (0, 0, 0)), jax.ShapeDtypeStruct((1, 1, D), F32)
    common = dict(grid=(nb,), scratch_shapes=[pltpu.VMEM((HEADS, HD, HD), F32)], compiler_params=_cp(1))
    if d == 0:
        return pl.pallas_call(body, name="hgrn_bwd_fwd_dir", in_specs=in_specs, out_specs=[at(0)] * 3 + [dlb_spec],
                              out_shape=[jax.ShapeDtypeStruct((t, D), BF16)] * 3 + [dlb_shape], **common,
                              )(parts, parts, parts, lb, mc, mtc, mrefc, ck, do)
    return pl.pallas_call(body, name="hgrn_bwd_bwd_dir", in_specs=in_specs + [at(0)] * 3 + [ANY],
                          out_specs=[pl.BlockSpec((SCAN_ROWS, 4 * D), lambda s: (rev(s), 0)), dlb_spec],
                          out_shape=[jax.ShapeDtypeStruct(dparts.shape, BF16), dlb_shape], input_output_aliases={12: 0},
                          **common)(parts, parts, parts, lb, mc, mtc, mrefc, ck, do, *first, dparts)


def in_proj_bwd_x(dx1, x, dparts, mod, nw, wg, after=None, latent_only=False):
    t = x.shape[0]
    tm = TM
    n_ctx = CTX // tm

    def body(dx1_ref, x_ref, dp_ref, mod_ref, nw_ref, w_ref, *rest):
        dx_ref, dnw_ref, dmod_ref, wfull, sems = rest[-5:]
        i = pl.program_id(0)

        @pl.when(i == 0)
        def _():
            dnw_ref[...] = jnp.zeros_like(dnw_ref)
            dmod_ref[...] = jnp.zeros_like(dmod_ref)
            copies = [pltpu.make_async_copy(w_ref.at[j], wfull.at[:, j * IN_SLOT:(j + 1) * IN_SLOT], sems.at[j])
                      for j in range(N_DEV)]
            for c in copies:
                c.start()
            for c in copies:
                c.wait()

        chunk = 2 * IN_SLOT
        dh = mm_nt(dp_ref[:, 0:chunk], wfull[:, 0:chunk])
        for k in range(1, D_IN // chunk):
            dh = dh + mm_nt(dp_ref[:, k * chunk:(k + 1) * chunk], wfull[:, k * chunk:(k + 1) * chunk])
        _norm_mod_bwd(x_ref, nw_ref, mod_ref, 0, dh, dx1_ref[...], dx_ref, dnw_ref, dmod_ref, i < n_ctx)

    tile = pl.BlockSpec((tm, D), lambda i: (i, 0))
    extra = [] if after is None else [jnp.reshape(after, (1, 1))]
    return pl.pallas_call(
        body, name="in_proj_bwd_x", grid=(t // tm,),
        in_specs=[tile, tile, pl.BlockSpec((tm, D_IN), lambda i: (i, 0)), pl.BlockSpec((1, 6, 1, D), _stream_row(tm)),
                  pl.BlockSpec((1, D), lambda i: (0, 0)), ANY] + [ANY] * len(extra),
        out_specs=[pl.BlockSpec((tm, D), lambda i: (jnp.maximum(i - n_ctx, 0), 0)) if latent_only else tile,
                   pl.BlockSpec((1, D), lambda i: (0, 0)), pl.BlockSpec((2, 2, 1, D), lambda i: (0, 0, 0, 0))],
        out_shape=[jax.ShapeDtypeStruct((t - CTX if latent_only else t, D), F32), jax.ShapeDtypeStruct((1, D), F32),
                   jax.ShapeDtypeStruct((2, 2, 1, D), F32)],
        scratch_shapes=[pltpu.VMEM((D, D_IN), BF16), pltpu.SemaphoreType.DMA((N_DEV,))],
        compiler_params=_cp(1))(dx1, x, dparts, mod, nw, wg, *extra)


def _lb_fn(h0, h1):
    m = jnp.maximum(h0, h1)
    e0, e1 = jnp.exp(h0 - m), jnp.exp(h1 - m)
    return e1 / (e0 + e1)


def lower_bounds(hlb):
    def body(h_ref, out_ref):
        out_ref[...] = _lb_fn(h_ref[0:1, :], h_ref[1:2, :])
    return pl.pallas_call(body, name="lower_bounds", out_shape=jax.ShapeDtypeStruct((1, 2 * D), F32))(hlb)


def lower_bounds_bwd(hlb, dlb1):
    def body(h_ref, d_ref, out_ref):
        _, vjp = jax.vjp(_lb_fn, h_ref[0:1, :], h_ref[1:2, :])
        d0, d1 = vjp(d_ref[...])
        out_ref[0:1, :] = d0
        out_ref[1:2, :] = d1
    return pl.pallas_call(body, name="lower_bounds_bwd", out_shape=jax.ShapeDtypeStruct((2, 2 * D), F32))(hlb, dlb1)


def _ada_fn(c_all, cctx8, w, b):
    dot = lambda a, l: mm(_silu(a), w[l]) + b[l]
    return [dot(c_all, l) for l in range(2)], [dot(cctx8, l) for l in range(2)]


def ada_fwd(c_all, cctx8, w, b):
    cols = w.shape[-1]

    def body(c_ref, cc_ref, w_ref, b_ref, out_ref):
        ox, oc = _ada_fn(c_ref[...], cc_ref[...], [w_ref[0], w_ref[1]], [b_ref[0], b_ref[1]])
        for l in range(2):
            out_ref[l, 0] = ox[l]
            out_ref[l, 1] = oc[l]
    return pl.pallas_call(body, name="ada_fwd", out_shape=jax.ShapeDtypeStruct((2, 2, N_DEV, cols), F32),
                          compiler_params=_cp(0))(c_all, cctx8, w, b)


def ada_bwd(c_all, cctx8, w, b, dmx, dmc):
    cols = w.shape[-1]

    def body(c_ref, cc_ref, w_ref, b_ref, dmx_ref, dmc_ref, dw_ref, dc_ref):
        fn = lambda cc, w0, w1: _ada_fn(c_ref[...], cc, [w0, w1], [b_ref[0], b_ref[1]])
        _, vjp = jax.vjp(fn, cc_ref[...], w_ref[0], w_ref[1])
        dcc, dw0, dw1 = vjp(([dmx_ref[0], dmx_ref[1]], [dmc_ref[0], dmc_ref[1]]))
        dw_ref[0] = dw0
        dw_ref[1] = dw1
        dc_ref[...] = jnp.sum(dcc, axis=0, keepdims=True)
    return pl.pallas_call(body, name="ada_bwd", out_shape=[jax.ShapeDtypeStruct((2, D, cols), F32), jax.ShapeDtypeStruct((1, D), F32)],
                          compiler_params=_cp(0))(c_all, cctx8, w, b, dmx, dmc)


def adamw(w, m, v, gparts, name):
    r, c = w.shape
    p = gparts.shape[0]
    rt = r
    while rt % 16 == 0 and (p + 7) * rt * c * 4 * 2 > ELEMENTWISE_VMEM:
        rt //= 2

    def body(w_ref, m_ref, v_ref, g_ref, go_ref, d_ref, mo_ref, vo_ref):
        g = g_ref[0].astype(F32)
        for k in range(1, p):
            g = g + g_ref[k].astype(F32)
        m2 = ADAM_B1 * m_ref[...] + (1.0 - ADAM_B1) * g
        v2 = ADAM_B2 * v_ref[...] + (1.0 - ADAM_B2) * (g * g)
        m_hat = m2 / (1.0 - ADAM_B1 ** ADAM_STEP)
        v_hat = v2 / (1.0 - ADAM_B2 ** ADAM_STEP)
        go_ref[...] = g
        d_ref[...] = -ADAM_LR * (m_hat / (jnp.sqrt(v_hat) + ADAM_EPS) + ADAM_WD * w_ref[...])
        mo_ref[...] = m2
        vo_ref[...] = v2

    tile = pl.BlockSpec((rt, c), lambda i: (i, 0))
    return pl.pallas_call(
        body, name=name, grid=(r // rt,),
        in_specs=[tile, tile, tile, pl.BlockSpec((p, rt, c), lambda i: (0, i, 0))], out_specs=[tile] * 4,
        out_shape=[jax.ShapeDtypeStruct((r, c), F32)] * 4, compiler_params=_cp(1))(w, m, v, gparts)


def _me():
    x, y, c = lax.axis_index("x"), lax.axis_index("y"), lax.axis_index("c")
    return x, y, c, 4 * x + 2 * y + c


def _peer(x, y, c, p):
    fx, fy, fc = (p >> 2) & 1, (p >> 1) & 1, p & 1
    return (1 - x if fx else x, 1 - y if fy else y, 1 - c if fc else c)


def all_gather(arrs, name, after=None):
    n = len(arrs)
    extra = [] if after is None else list(after) if isinstance(after, (list, tuple)) else [after]

    def body(*refs):
        ins, outs = refs[:n], refs[n + len(extra):2 * n + len(extra)]
        send, recv, local = refs[2 * n + len(extra):]
        x, y, c, me = _me()
        copies = []
        for a in range(n):
            lc = pltpu.make_async_copy(ins[a], outs[a].at[me], local.at[a])
            lc.start()
            copies.append(lc)
            for p in range(1, N_DEV):
                cp = pltpu.make_async_remote_copy(src_ref=ins[a], dst_ref=outs[a].at[me], send_sem=send.at[a, p - 1],
                                                  recv_sem=recv.at[a, p - 1], device_id=_peer(x, y, c, p),
                                                  device_id_type=pl.DeviceIdType.MESH)
                cp.start()
                copies.append(cp)
        for cp in copies:
            cp.wait()

    return pl.pallas_call(
        body, name=name, in_specs=[ANY] * (n + len(extra)), out_specs=[ANY] * n,
        out_shape=[jax.ShapeDtypeStruct((N_DEV,) + a.shape, a.dtype) for a in arrs],
        scratch_shapes=[pltpu.SemaphoreType.DMA((n, N_DEV - 1)), pltpu.SemaphoreType.DMA((n, N_DEV - 1)),
                        pltpu.SemaphoreType.DMA((n,))])(*arrs, *extra)


HBM = pl.BlockSpec(memory_space=pltpu.HBM)
SEM = pl.BlockSpec(memory_space=pltpu.SEMAPHORE)


def _in_hbm(a):
    return pltpu.with_memory_space_constraint(a, pltpu.HBM)


ALL_PEERS = tuple(range(1, N_DEV))
SAME_CORE_AND_SIBLING = (1, 2, 4, 6)
OTHER_CHIPS = (2, 4, 6)


def _exchange_refs(srcs, lands, layer, scatter, a, x, y, c, p, forward=False):
    me = 4 * x + 2 * y + c
    px, py, pc = _peer(x, y, c, p) if p else (x, y, c)
    if forward and p:
        slot = lands[a].at[4 * px + 2 * py + pc]
        return slot, slot, _peer(x, y, c, 1)
    dst = lands[a].at[me] if layer is None else lands[a].at[me, layer]
    src = srcs[a].at[4 * px + 2 * py + pc] if scatter else dst
    return src, dst, (px, py, pc)


def exchange_start(srcs, lands, layer, scatter, name, after=None, peers=ALL_PEERS, forward=False):
    n, ns = len(lands), len(srcs)
    extra = [] if after is None else [after]

    def body(*refs):
        ins, lz = refs[:ns], refs[ns:ns + n]
        send, recv = refs[ns + n + len(extra)], refs[ns + n + len(extra) + 1]
        token = refs[-1]
        x, y, c, _ = _me()
        for a in range(n):
            for p in peers:
                src, dst, peer = _exchange_refs(ins, lz, layer, scatter, a, x, y, c, p, forward)
                k = a * (N_DEV - 1) + p - 1
                pltpu.make_async_remote_copy(src_ref=src, dst_ref=dst, send_sem=send.at[k], recv_sem=recv.at[k],
                                             device_id=peer, device_id_type=pl.DeviceIdType.MESH).start()
        token[...] = jnp.zeros_like(token)

    thru = [pltpu.HBM(a.shape, a.dtype) for a in list(srcs) + list(lands)]
    out = pl.pallas_call(
        body, name=name, in_specs=[HBM] * (ns + n) + [ANY] * len(extra),
        out_specs=[SEM, SEM] + [HBM] * (ns + n) + [pl.BlockSpec(memory_space=pltpu.VMEM)],
        out_shape=[pltpu.SemaphoreType.DMA((n * (N_DEV - 1),)), pltpu.SemaphoreType.DMA((n * (N_DEV - 1),))] + thru
        + [jax.ShapeDtypeStruct((8, 128), F32)],
        input_output_aliases={i: 2 + i for i in range(ns + n)},
        compiler_params=pltpu.CompilerParams(has_side_effects=pltpu.SideEffectType.DATAFLOW_SIDE_EFFECTING),
    )(*[_in_hbm(a) for a in list(srcs) + list(lands)], *extra)
    return out[0], out[1], out[2:2 + ns], out[2 + ns:2 + ns + n], out[-1]


def exchange_wait(send, recv, srcs, lands, layer, scatter, after, name, peers=ALL_PEERS):
    n, ns = len(lands), len(srcs)

    def body(*refs):
        ins, lz = refs[:ns], refs[ns:ns + n]
        send_ref, recv_ref = refs[ns + n], refs[ns + n + 1]
        x, y, c, _ = _me()
        for a in range(n):
            for p in peers:
                src, dst, peer = _exchange_refs(ins, lz, layer, scatter, a, x, y, c, 0)
                k = a * (N_DEV - 1) + p - 1
                cp = pltpu.make_async_remote_copy(src_ref=src, dst_ref=dst, send_sem=send_ref.at[k],
                                                  recv_sem=recv_ref.at[k], device_id=peer,
                                                  device_id_type=pl.DeviceIdType.MESH)
                cp.wait_send()
                cp.wait_recv()

    thru = [pltpu.HBM(a.shape, a.dtype) for a in list(srcs) + list(lands)]
    out = pl.pallas_call(
        body, name=name, in_specs=[HBM] * (ns + n) + [SEM, SEM, ANY], out_specs=[HBM] * (ns + n), out_shape=thru,
        input_output_aliases={i: i for i in range(ns + n)},
        compiler_params=pltpu.CompilerParams(has_side_effects=pltpu.SideEffectType.DATAFLOW_SIDE_EFFECTING),
    )(*srcs, *lands, send, recv, after)
    return out[ns:]


def place_own(src, land, me, layer, scatter, name, src_layer=None, after=None):
    create = isinstance(land, jax.ShapeDtypeStruct)
    r, c = src.shape[-2:]
    rt = r
    while rt % 32 == 0 and rt * c * 4 > COPY_BLOCK_BYTES:
        rt //= 2

    extra = [] if after is None else [after]

    def body(me_ref, src_ref, *rest):
        out_ref = rest[-1]
        out_ref[...] = src_ref[...].reshape(out_ref.shape).astype(out_ref.dtype)

    src_spec = (pl.BlockSpec((1, rt, c), lambda i, m: (m[0], i, 0)) if scatter else
                pl.BlockSpec((rt, c), lambda i, m: (i, 0)) if src_layer is None else
                pl.BlockSpec((1, rt, c), lambda i, m: (src_layer, i, 0)))
    out_spec = (pl.BlockSpec((1, rt, c), lambda i, m: (m[0], i, 0)) if layer is None
                else pl.BlockSpec((1, 1, rt, c), lambda i, m: (m[0], layer, i, 0)))
    grid_spec = pltpu.PrefetchScalarGridSpec(num_scalar_prefetch=1, grid=(r // rt,),
                                             in_specs=[src_spec] + ([] if create else [ANY]) + [ANY] * len(extra),
                                             out_specs=out_spec)
    return pl.pallas_call(body, name=name, grid_spec=grid_spec, out_shape=jax.ShapeDtypeStruct(land.shape, land.dtype),
                          input_output_aliases={} if create else {2: 0}, compiler_params=_cp(1),
                          )(*((me, src) if create else (me, src, land)), *extra)


def _scan_constants():
    r = lax.broadcasted_iota(jnp.int32, (CH, CH), 0)
    s = lax.broadcasted_iota(jnp.int32, (CH, CH), 1)
    lower = (s <= r).astype(F32)
    t = jnp.arange(CH)[:, None]
    mc = jnp.stack([lower, lower.T])
    mref = jnp.stack([(t <= CH // 2 - 1).astype(F32), (t >= CH // 2).astype(F32)])
    return mc, jnp.stack([lower.T, lower]), mref


def local_step(x, ctx, target, mod, lb, w, fetch=None, publish=None, small_ready=None, small_early=None):
    kept = {}

    def keep(l, part, grads):
        kept[(l, part)] = grads
        return 0.0

    fetch = fetch or (lambda l, part, after: w)
    publish = publish or keep
    n_layers = len(mod)
    mc, mtc, mrefc = _scan_constants()
    xs = jnp.concatenate([ctx, x], axis=0)
    saved, big = [], []
    for l in range(n_layers):
        wl = dict(fetch(l, "in", xs))
        parts, ht, iv, mix = in_proj_fwd(xs, mod[l], w["nw1"][l], wl["win"][l])
        o, ck = hgrn_fwd(parts, iv, lb[l], mc, mtc, mrefc)
        wl.update(fetch(l, "mix", o))
        last = l == n_layers - 1
        x1, pa, pb, ym, yat, ybt, mt = mixer_fwd(xs, mix, o, mod[l], w["lnw"][l], w["lnb"][l], w["sw"][l], w["sb"][l],
                                                 w["hnw"][l], wl["wa"][l], wl["wb"][l], wl["wo"][l], last)
        wl.update(fetch(l, "ffn", x1))
        av, h2t = ffn_up_fwd(x1, mod[l], w["nw2"][l], wl["wup"][l], last)
        x2, ac, y, z = ffn_down_fwd(x1, av, mod[l], w["cw"][l], w["cb"][l], wl["wd"][l], last)
        saved.append((xs, parts, mix, o, ck, x1, av, ac, y, z, ht, h2t, pa, pb, ym, yat, ybt, mt))
        big.append(wl)
        xs = x2
    loss, dx, dfw = loss_fwd_bwd(xs, target, w["fw"])
    g = {k: [None] * n_layers for k in ("nw1", "nw2", "lnw", "lnb", "sw", "sb", "hnw", "cw", "cb")}
    g["fw"] = dfw
    dmod, dlb = [None] * n_layers, [None] * n_layers
    tok = 0.0
    for l in reversed(range(n_layers)):
        x0, parts, mix, o, ck, x1, av, ac, y, z, ht, h2t, pa, pb, ym, yat, ybt, mt = saved[l]
        wl = big[l]
        last = l == n_layers - 1
        dav, dac, dout, dg2 = ffn_down_bwd(dx, ac, av, y, mod[l] + tok, wl["wd"][l], last)
        dwd = weight_grad_rows(z, dout, "ffn_down_bwd_w")
        dav, g["cw"][l], g["cb"][l] = conv_bwd(dav, dac, av, w["cw"][l], last)
        dx1, g["nw2"][l], dmod2 = ffn_up_bwd_x(dx, x1, dav, mod[l], w["nw2"][l], wl["wup"][l], last)
        dwup = weight_grad(h2t, dav, FF_SLOT, "ffn_up_bwd_w")
        tok = publish(l, "ffn", {"wd": dwd, "wup": dwup})
        (dparts, do, dy, dpa, dpb, g["lnw"][l], g["lnb"][l], g["sw"][l], g["sb"][l], g["hnw"][l],
         dg1) = mixer_bwd(dx1, mix, o, pa, pb, ym, mod[l] + tok, w["lnw"][l], w["lnb"][l], w["sw"][l], w["sb"][l],
                          w["hnw"][l], wl["wa"][l], wl["wb"][l], wl["wo"][l], last)
        tok = publish(l, "mix", {"wa": weight_grad(yat, dpa, D, "mixer_bwd_wa"), "wb": weight_grad(ybt, dpb, D, "mixer_bwd_wb"),
                                 "wo": weight_grad(mt, dy, D, "mixer_bwd_wo")})
        if l == 0 and small_early:
            dmod[0] = jnp.concatenate([jnp.zeros((2, 2, 1, D), F32), dg1, dmod2, dg2], axis=1)
            tok = tok + small_early(loss[0, 0], g, dmod, dlb)
        dq, df, di, dlb_f = hgrn_bwd(0, parts, lb[l] + tok, mc, mtc, mrefc, ck, do)
        dparts, dlb_b = hgrn_bwd(1, parts, lb[l], mc, mtc, mrefc, ck, do, (dq, df, di), dparts)
        dlb[l] = jnp.concatenate([dlb_f, dlb_b], axis=0)
        tok = publish(l, "in", {"win": weight_grad(ht, dparts, IN_SLOT, "in_proj_bwd_w")})
        dx, g["nw1"][l], dmod1 = in_proj_bwd_x(dx1, x0, dparts, mod[l], w["nw1"][l], wl["win"][l], after=tok,
                                               latent_only=l == 0)
        dmod[l] = jnp.concatenate([dmod1, dg1, dmod2, dg2], axis=1)
    done = small_ready(loss[0, 0], g, dmod, dlb) if small_ready else 0.0
    for (l, part), grads in kept.items():
        for k, v in grads.items():
            g.setdefault(k, [None] * n_layers)[l] = v
    return loss[0, 0], dx, g, dmod, dlb, done


ROW = 1024
REPLICATED = ("norm1_w", "sgu_ln_w", "sgu_ln_b", "sgu_w", "sgu_b", "hgrn_lower_bounds", "hgrn_norm_w", "norm2_w",
              "ffn_conv_b", "final_norm_w")
WEIGHT_ORDER = ("c_ctx", "ada_w", "ada_b", "norm1_w", "w_in", "sgu_ln_w", "sgu_ln_b", "sgu_w", "sgu_b", "hgrn_lower_bounds",
                "hgrn_norm_w", "w_branch_a", "w_branch_b", "w_out", "norm2_w", "ffn_w_up", "ffn_conv_w", "ffn_conv_b",
                "ffn_w_down", "final_norm_w")


def _rows_of(n):
    return -(-n // (8 * ROW)) * 8


def _pack(arrs, total_rows=None):
    parts = []
    for a in arrs:
        flat = a.reshape(-1).astype(F32)
        rows = _rows_of(flat.shape[0])
        parts.append(jnp.pad(flat, (0, rows * ROW - flat.shape[0])).reshape(rows, ROW))
    have = sum(p.shape[0] for p in parts)
    if total_rows is not None and total_rows > have:
        parts.append(jnp.zeros((total_rows - have, ROW), F32))
    return jnp.concatenate(parts, axis=0)


def _unpack(packed, shapes):
    lead = packed.shape[:-2]
    out, r0 = [], 0
    for s in shapes:
        n = math.prod(s)
        rows = _rows_of(n)
        out.append(packed[..., r0:r0 + rows, :].reshape(lead + (rows * ROW,))[..., :n].reshape(lead + tuple(s)))
        r0 += rows
    return out


def kernel(x, c, ctx, c_ctx, ada_w, ada_b, norm1_w, w_in, sgu_ln_w, sgu_ln_b, sgu_w, sgu_b, hgrn_lower_bounds, hgrn_norm_w, w_branch_a, w_branch_b, w_out, norm2_w, ffn_w_up, ffn_conv_w, ffn_conv_b, ffn_w_down, final_norm_w, loss_target, m_c_ctx, m_ada_w, m_ada_b, m_norm1_w, m_w_in, m_sgu_ln_w, m_sgu_ln_b, m_sgu_w, m_sgu_b, m_hgrn_lower_bounds, m_hgrn_norm_w, m_w_branch_a, m_w_branch_b, m_w_out, m_norm2_w, m_ffn_w_up, m_ffn_conv_w, m_ffn_conv_b, m_ffn_w_down, m_final_norm_w, v_c_ctx, v_ada_w, v_ada_b, v_norm1_w, v_w_in, v_sgu_ln_w, v_sgu_ln_b, v_sgu_w, v_sgu_b, v_hgrn_lower_bounds, v_hgrn_norm_w, v_w_branch_a, v_w_branch_b, v_w_out, v_norm2_w, v_ffn_w_up, v_ffn_conv_w, v_ffn_conv_b, v_ffn_w_down, v_final_norm_w):
    wts = dict(c_ctx=c_ctx, ada_w=ada_w, ada_b=ada_b, norm1_w=norm1_w, w_in=w_in, sgu_ln_w=sgu_ln_w, sgu_ln_b=sgu_ln_b,
               sgu_w=sgu_w, sgu_b=sgu_b, hgrn_lower_bounds=hgrn_lower_bounds, hgrn_norm_w=hgrn_norm_w, w_branch_a=w_branch_a,
               w_branch_b=w_branch_b, w_out=w_out, norm2_w=norm2_w, ffn_w_up=ffn_w_up, ffn_conv_w=ffn_conv_w,
               ffn_conv_b=ffn_conv_b, ffn_w_down=ffn_w_down, final_norm_w=final_norm_w)
    mom1 = dict(c_ctx=m_c_ctx, ada_w=m_ada_w, ada_b=m_ada_b, norm1_w=m_norm1_w, w_in=m_w_in, sgu_ln_w=m_sgu_ln_w,
                sgu_ln_b=m_sgu_ln_b, sgu_w=m_sgu_w, sgu_b=m_sgu_b, hgrn_lower_bounds=m_hgrn_lower_bounds,
                hgrn_norm_w=m_hgrn_norm_w, w_branch_a=m_w_branch_a, w_branch_b=m_w_branch_b, w_out=m_w_out, norm2_w=m_norm2_w,
                ffn_w_up=m_ffn_w_up, ffn_conv_w=m_ffn_conv_w, ffn_conv_b=m_ffn_conv_b, ffn_w_down=m_ffn_w_down,
                final_norm_w=m_final_norm_w)
    mom2 = dict(c_ctx=v_c_ctx, ada_w=v_ada_w, ada_b=v_ada_b, norm1_w=v_norm1_w, w_in=v_w_in, sgu_ln_w=v_sgu_ln_w,
                sgu_ln_b=v_sgu_ln_b, sgu_w=v_sgu_w, sgu_b=v_sgu_b, hgrn_lower_bounds=v_hgrn_lower_bounds,
                hgrn_norm_w=v_hgrn_norm_w, w_branch_a=v_w_branch_a, w_branch_b=v_w_branch_b, w_out=v_w_out, norm2_w=v_norm2_w,
                ffn_w_up=v_ffn_w_up, ffn_conv_w=v_ffn_conv_w, ffn_conv_b=v_ffn_conv_b, ffn_w_down=v_ffn_w_down,
                final_norm_w=v_final_norm_w)
    n_layers = w_in.shape[0]
    layers = range(n_layers)
    me = 4 * lax.axis_index("x") + 2 * lax.axis_index("y") + lax.axis_index("c")
    ada_cols = ada_w.shape[-1]

    big = ("w_in", "ffn_w_up", "w_branch_a", "w_branch_b", "w_out", "ffn_w_down")
    short = {"w_in": "win", "ffn_w_up": "wup", "w_branch_a": "wa", "w_branch_b": "wb", "w_out": "wo", "ffn_w_down": "wd"}
    me1 = me.reshape(1).astype(jnp.int32)
    mixer, ffn = ("w_branch_a", "w_branch_b", "w_out"), ("ffn_w_up", "ffn_w_down")
    groups = [[(k, l) for k in part] for l in layers for part in (("w_in",), mixer, ffn)]
    group_of = {(l, part): 3 * l + n for l in layers for n, part in enumerate(("in", "mix", "ffn"))}
    in_flight, started = [], 0.0

    def own_slots(n, after):
        return [place_own(wts[k], jax.ShapeDtypeStruct((N_DEV,) + wts[k].shape[1:], BF16), me1, None, False,
                          f"gather_own_{short[k]}_{l}", src_layer=l, after=after) for k, l in groups[n]]

    def start_group(n, lands, after):
        in_flight.append(exchange_start([], lands, None, False, f"gather_weights_start_{n}", after=after,
                                        peers=SAME_CORE_AND_SIBLING if n == 0 else ALL_PEERS))
        return in_flight[-1][-1]

    (c_all,) = all_gather([c], "gather_c")
    c_all = c_all.reshape(N_DEV, D)
    token = start_group(0, own_slots(0, c_all), c_all)
    later = [own_slots(n, token) for n in range(1, len(groups))]
    cctx8 = jnp.broadcast_to(c_ctx[None, :], (N_DEV, D))
    ada_b_cols = lax.dynamic_slice_in_dim(ada_b, me * ada_cols, ada_cols, axis=1)[:, None, :]
    mod_cols = ada_fwd(c_all, cctx8, ada_w, ada_b_cols)
    xs = jnp.concatenate([ctx[0], x[0]], axis=0)
    lb1 = lower_bounds(hgrn_lower_bounds)
    mod_all, conv_all = all_gather([mod_cols, ffn_conv_w.reshape(n_layers, 9, -1)], "gather_mod_conv",
                                   after=[token, xs, lb1] + [a for lands in later for a in lands])
    conv_full = [conv_all[:, l].transpose(1, 0, 2).reshape(9, N_FFK, FF_SLOT).transpose(1, 0, 2) for l in layers]
    for n in range(1, len(groups)):
        token = start_group(n, later[n - 1], mod_all if n == 1 else token)
    for started_group in in_flight:
        started = started + started_group[-1][0, 0]

    def as_used(k, a):
        return a if k in ("w_in", "ffn_w_up") else a.reshape(N_FFK, FF_SLOT, D) if k == "ffn_w_down" else a.reshape(D, D)

    arrived = {}

    def fetch(l, part, after):
        n = group_of[(l, part)]
        send, recv, _, lands, _ = in_flight[n]
        first = n == 0
        got = exchange_wait(send, recv, [], lands, None, False, after, f"gather_weights_wait_{n}",
                            peers=SAME_CORE_AND_SIBLING if first else ALL_PEERS)
        if first:
            send, recv, _, lands, _ = exchange_start([], got, None, False, "gather_weights_pass_on", peers=OTHER_CHIPS,
                                                     forward=True)
            got = exchange_wait(send, recv, [], lands, None, False, after, "gather_weights_passed_on", peers=OTHER_CHIPS)
        for (k, ll), a in zip(groups[n], got):
            arrived.setdefault(short[k], [None] * n_layers)[ll] = as_used(k, a)
        return arrived

    mod_x = lax.dynamic_index_in_dim(mod_all[:, :, 0], me, axis=2, keepdims=False)
    mod_c = mod_all[:, :, 1, 0]
    mod = [jnp.stack([mod_c[:, l].reshape(6, 1, D), mod_x[:, l].reshape(6, 1, D)]) for l in layers]
    mod[0] = mod[0] + started

    lb = [jnp.zeros((2, 1, D), F32), lb1.reshape(2, 1, D)]

    w = {
        "nw1": [norm1_w[l][None] for l in layers], "nw2": [norm2_w[l][None] for l in layers],
        "lnw": [sgu_ln_w[l][None] for l in layers], "lnb": [sgu_ln_b[l][None] for l in layers],
        "sw": [sgu_w[l] for l in layers], "sb": [sgu_b[l][:, :, None] for l in layers],
        "hnw": [hgrn_norm_w[l][None] for l in layers], "cw": conv_full,
        "cb": [ffn_conv_b[l].reshape(N_FFK, 1, FF_SLOT) for l in layers], "fw": final_norm_w[None],
    }
    long = {v: k for k, v in short.items()}
    landing, sent = {}, []

    def publish(l, part, grads):
        keys = [long[k] for k in grads]
        slots = [a.reshape((N_DEV, -1, a.shape[-1])) for a in grads.values()]
        zones = [place_own(s, landing.get(k, jax.ShapeDtypeStruct((N_DEV, n_layers) + s.shape[1:], s.dtype)), me1, l, True,
                           f"scatter_own_{short[k]}_{l}") for k, s in zip(keys, slots)]
        send, recv, srcs, zones, token = exchange_start(slots, zones, l, True, f"scatter_grads_start_{part}_{l}")
        landing.update(zip(keys, zones))
        sent.append((keys, l, part, send, recv, srcs, token))
        return token[0, 0]

    out = {}
    flat2 = lambda a: a.reshape(-1, a.shape[-1])

    def finish(part, after):
        done = []
        for keys, l, p, send, recv, srcs, _ in sent:
            if p == part:
                zones = exchange_wait(send, recv, srcs, [landing[k] for k in keys], l, True, after,
                                      f"scatter_grads_wait_{part}_{l}")
                landing.update(zip(keys, zones))
                done = keys
        for k in done:
            r = landing[k]
            res = adamw(flat2(wts[k]), flat2(mom1[k]), flat2(mom2[k]), r.reshape(N_DEV, -1, r.shape[-1]), "adamw_" + k)
            out[k] = tuple(a.reshape(wts[k].shape) for a in res)

    rep_rows = -(-sum(_rows_of(wts[k].size) for k in REPLICATED) // 64) * 64
    conv_rows = _rows_of(n_layers * 9 * D_FF)
    dmod_rows = _rows_of(n_layers * 6 * D)
    early = {}

    def small_early(loss_part, g, dmod, dlb):
        d_hlb = lower_bounds_bwd(hgrn_lower_bounds, dlb[1].reshape(1, 2 * D))
        st = lambda k: jnp.stack([jnp.zeros((1, D), F32) if a is None else a for a in g[k]])
        rep_grads = {"norm1_w": st("nw1"), "sgu_ln_w": st("lnw"), "sgu_ln_b": st("lnb"), "sgu_w": st("sw"), "sgu_b": st("sb"),
                     "hgrn_lower_bounds": d_hlb, "hgrn_norm_w": st("hnw"), "norm2_w": st("nw2"), "ffn_conv_b": st("cb"),
                     "final_norm_w": g["fw"]}
        d_conv = jnp.stack([g["cw"][l].transpose(1, 0, 2).reshape(9, D_FF) for l in layers])
        dmod_x = jnp.stack([dmod[l][1].reshape(6 * D) for l in layers])
        dmod_c = jnp.stack([dmod[l][0].reshape(6 * D) for l in layers])
        small = jnp.concatenate([_pack([rep_grads[k] for k in REPLICATED], rep_rows),
                                 _pack([d_conv, dmod_x, dmod_c, loss_part.reshape(1)])], axis=0)
        zone = place_own(small, jax.ShapeDtypeStruct((N_DEV,) + small.shape, F32), me1, None, False, "gather_small_own")
        early["send"], early["recv"], _, early["zones"], token = exchange_start([], [zone], None, False, "gather_small_start")
        return token[0, 0]

    def small_ready(loss_part, g, dmod, dlb):
        late = _pack([g["nw1"][0], dmod[0][1, 0:2], dmod[0][0, 0:2]])
        for part in ("ffn", "mix"):
            finish(part, late)
        (late_all,) = all_gather([late], "gather_small_late", after=[out[k][0] for k in big[1:]])
        (small_all,) = exchange_wait(early["send"], early["recv"], [], early["zones"], None, False, late_all,
                                     "gather_small_wait")
        at_x = rep_rows + conv_rows
        small_all = small_all.at[:, 0:1].set(late_all[:, 0:1])
        small_all = small_all.at[:, at_x:at_x + 2].set(late_all[:, 8:10])
        small_all = small_all.at[:, at_x + dmod_rows:at_x + dmod_rows + 2].set(late_all[:, 16:18])
        d_conv_shape, dmod_shape = (n_layers, 9, D_FF), (n_layers, 6 * D)
        conv_g, dmx_all, dmc_all, loss_all = _unpack(small_all[:, rep_rows:], [d_conv_shape, dmod_shape, dmod_shape, (1,)])
        out["loss"] = functools.reduce(lambda a, b: a + b, [loss_all[k, 0] for k in range(N_DEV)])

        rep = adamw(_pack([wts[k] for k in REPLICATED], rep_rows), _pack([mom1[k] for k in REPLICATED], rep_rows),
                    _pack([mom2[k] for k in REPLICATED], rep_rows), small_all, "adamw_replicated")
        rep = [_unpack(r, [wts[k].shape for k in REPLICATED]) for r in rep]
        for n, k in enumerate(REPLICATED):
            out[k] = tuple(r[n] for r in rep)

        conv_mine = lax.dynamic_index_in_dim(conv_g.reshape(N_DEV, n_layers, 9, N_DEV, -1), me, axis=3, keepdims=False)
        res = adamw(flat2(ffn_conv_w), flat2(m_ffn_conv_w), flat2(v_ffn_conv_w),
                    conv_mine.reshape(N_DEV, -1, conv_mine.shape[-1]), "adamw_conv_w")
        out["ffn_conv_w"] = tuple(r.reshape(ffn_conv_w.shape) for r in res)

        out["ada_b"] = tuple(adamw(ada_b, m_ada_b, v_ada_b, jnp.concatenate([dmx_all, dmc_all], axis=0), "adamw_ada_b"))

        cols_of = lambda a: lax.dynamic_slice_in_dim(a, me * ada_cols, ada_cols, axis=2).transpose(1, 0, 2)
        d_ada_w, d_cctx = ada_bwd(c_all, cctx8, ada_w, ada_b_cols, cols_of(dmx_all), cols_of(dmc_all))
        res = adamw(flat2(ada_w), flat2(m_ada_w), flat2(v_ada_w), flat2(d_ada_w)[None], "adamw_ada_w")
        out["ada_w"] = tuple(r.reshape(ada_w.shape) for r in res)
        (d_cctx_all,) = all_gather([d_cctx], "gather_c_ctx_grad")
        res = adamw(c_ctx[None], m_c_ctx[None], v_c_ctx[None], d_cctx_all, "adamw_c_ctx")
        out["c_ctx"] = tuple(r[0] for r in res)
        return d_cctx_all

    _, grad_x, _, _, _, small_done = local_step(x[0], ctx[0], loss_target[0], mod, lb, w, fetch, publish, small_ready,
                                                small_early)
    loss = out["loss"]

    finish("in", small_done)
    return (loss, grad_x[None]) + tuple(out[k][n] for n in range(4) for k in WEIGHT_ORDER)
```
